```python
import jax, jax.numpy as jnp
from jax import lax
import numpy as np

D_MODEL = 1024
BATCH = 16
SEQ = 2048
DEPTH = 1

MEM_LEN = 256
GM_GROUPS = 4
GM_CHUNK = 128
GM_WIDTH = 512
GM_GROUP_DIM = GM_WIDTH // GM_GROUPS
HG_HEADS = 4
HG_KDIM = 128
HG_VDIM = 128
HG_WIDTH = HG_HEADS * HG_KDIM
HG_VWIDTH = HG_HEADS * HG_VDIM
HG_CHUNK = 64
XA_HEADS = 4
XA_HEAD_DIM = 128
XA_WIDTH = XA_HEADS * XA_HEAD_DIM
N_BRANCH = 3
BRANCH_WIDTH = 512
D_FF = 2816
CONV_WIDTH = 3
EPS = 1e-6
IN_SPLITS = (GM_WIDTH, GM_WIDTH, HG_WIDTH, HG_WIDTH, HG_VWIDTH, HG_VWIDTH, XA_WIDTH, N_BRANCH * D_MODEL)
IN_WIDTH = GM_WIDTH * 2 + HG_WIDTH * 2 + HG_VWIDTH * 2 + XA_WIDTH + N_BRANCH * D_MODEL

kernel_name = 'hybrid_gmlp_hgrn2_memattn_convffn'


def rms_norm(x, g):
    xf = x.astype(jnp.float32)
    y = xf * lax.rsqrt(jnp.mean(xf * xf, axis=-1, keepdims=True) + EPS)
    return y.astype(x.dtype) * g


def layer_norm(x, g, b):
    xf = x.astype(jnp.float32)
    mu = jnp.mean(xf, axis=-1, keepdims=True)
    var = jnp.mean(jnp.square(xf - mu), axis=-1, keepdims=True)
    return ((xf - mu) * lax.rsqrt(var + EPS)).astype(x.dtype) * g + b


def spatial_gating(z_u, z_v, ln_g, ln_b, w_s, b_s):
    B, S, _ = z_u.shape
    v = layer_norm(z_v, ln_g, ln_b).reshape(B, S // GM_CHUNK, GM_CHUNK, GM_GROUPS, GM_GROUP_DIM)
    causal = jnp.tril(jnp.ones((GM_CHUNK, GM_CHUNK), dtype=bool))
    w = jnp.where(causal[None], w_s, 0)
    mixed = jnp.einsum('gts,bnsgc->bntgc', w, v) + b_s.T[None, None, :, :, None]
    return z_u * mixed.reshape(B, S, GM_WIDTH)


def hgrn2(q, f_raw, i, g, lb, norm_g):
    B, S, _ = q.shape
    N = S // HG_CHUNK
    f32 = jnp.float32
    lbf = lb.astype(f32).reshape(HG_HEADS, HG_KDIM)
    fg = lbf + (1.0 - lbf) * jax.nn.sigmoid(f_raw.astype(f32).reshape(B, S, HG_HEADS, HG_KDIM))
    k = 1.0 - fg
    logf = jnp.log(fg)
    qf = jax.nn.silu(q.astype(f32)).reshape(B, S, HG_HEADS, HG_KDIM)
    vf = i.astype(f32).reshape(B, S, HG_HEADS, HG_VDIM)

    def to_chunks(t):
        return t.reshape(B, N, HG_CHUNK, HG_HEADS, t.shape[-1]).transpose(1, 0, 3, 2, 4)

    causal = jnp.tril(jnp.ones((HG_CHUNK, HG_CHUNK), dtype=bool))

    def step(state, xs):
        qc, kc, vc, lfc = xs
        a = jnp.cumsum(lfc, axis=2)
        inter = jnp.einsum('bhtk,bhkv->bhtv', qc * jnp.exp(a), state)
        diff = a[:, :, :, None, :] - a[:, :, None, :, :]
        decay = jnp.exp(jnp.where(causal[:, :, None], diff, -jnp.inf))
        scores = jnp.einsum('bhtk,bhtsk,bhsk->bhts', qc, decay, kc)
        intra = jnp.einsum('bhts,bhsv->bhtv', scores, vc)
        a_last = a[:, :, -1:, :]
        new_state = (jnp.exp(a_last[:, :, 0, :])[..., None] * state
                     + jnp.einsum('bhsk,bhsv->bhkv', kc * jnp.exp(a_last - a), vc))
        return new_state, inter + intra

    s0 = jnp.zeros((B, HG_HEADS, HG_KDIM, HG_VDIM), f32)
    _, o = lax.scan(step, s0, (to_chunks(qf), to_chunks(k), to_chunks(vf), to_chunks(logf)))
    o = o.transpose(1, 0, 3, 2, 4).reshape(B, S, HG_HEADS, HG_VDIM)
    o = rms_norm(o, norm_g.astype(f32)) * jax.nn.silu(g.astype(f32).reshape(B, S, HG_HEADS, HG_VDIM))
    return o.reshape(B, S, HG_VWIDTH).astype(q.dtype)


def memory_attention(q, mem, mem_g, w_kv):
    B, S, _ = q.shape
    M = mem.shape[1]
    kv = rms_norm(mem, mem_g) @ w_kv
    k, v = jnp.split(kv, 2, axis=-1)
    k = k.reshape(B, M, XA_HEADS, XA_HEAD_DIM)
    v = v.reshape(B, M, XA_HEADS, XA_HEAD_DIM)
    qh = q.reshape(B, S, XA_HEADS, XA_HEAD_DIM)
    s = jnp.einsum('bshd,bmhd->bhsm', qh, k).astype(jnp.float32) * (XA_HEAD_DIM ** -0.5)
    p = jax.nn.softmax(s, axis=-1).astype(q.dtype)
    return jnp.einsum('bhsm,bmhd->bshd', p, v).reshape(B, S, XA_WIDTH)


def conv_ffn(h, w_up, conv_w, conv_b, w_down):
    S = h.shape[1]
    a, b = jnp.split(h @ w_up, 2, axis=-1)
    ap = jnp.pad(a, ((0, 0), (CONV_WIDTH - 1, 0), (0, 0)))
    ac = conv_b + conv_w[0] * ap[:, 0:S]
    for j in range(1, CONV_WIDTH):
        ac = ac + conv_w[j] * ap[:, j:j + S]
    return (jax.nn.silu(ac) * b) @ w_down


def _fwd_setup_inputs(seed: int = 0) -> dict:
    key = jax.random.key(seed)
    ks = jax.random.split(key, 24)
    f32 = jnp.float32
    nrm = lambda k, shape, scale: jax.random.normal(k, shape, f32) * scale
    return {
        'x': nrm(ks[0], (BATCH, SEQ, D_MODEL), 1.0),
        'mem': nrm(ks[1], (BATCH, MEM_LEN, D_MODEL), 1.0),
        'norm1_g': 1.0 + nrm(ks[2], (DEPTH, D_MODEL), 0.02),
        'w_in': nrm(ks[3], (DEPTH, D_MODEL, IN_WIDTH), D_MODEL ** -0.5),
        'ln_v_g': 1.0 + nrm(ks[4], (DEPTH, GM_WIDTH), 0.02),
        'ln_v_b': nrm(ks[5], (DEPTH, GM_WIDTH), 0.02),
        'w_spatial': nrm(ks[6], (DEPTH, GM_GROUPS, GM_CHUNK, GM_CHUNK), GM_CHUNK ** -0.5),
        'b_spatial': 1.0 + nrm(ks[7], (DEPTH, GM_GROUPS, GM_CHUNK), 0.02),
        'lb_logits': nrm(ks[8], (DEPTH + 1, HG_WIDTH), 0.5),
        'hgrn_norm_g': 1.0 + nrm(ks[9], (DEPTH, HG_VDIM), 0.02),
        'mem_norm_g': 1.0 + nrm(ks[10], (DEPTH, D_MODEL), 0.02),
        'w_mem_kv': nrm(ks[11], (DEPTH, D_MODEL, 2 * XA_WIDTH), D_MODEL ** -0.5),
        'w_branch': nrm(ks[12], (DEPTH, N_BRANCH, BRANCH_WIDTH, D_MODEL), BRANCH_WIDTH ** -0.5),
        'w_out': nrm(ks[13], (DEPTH, D_MODEL, D_MODEL), D_MODEL ** -0.5),
        'norm2_g': 1.0 + nrm(ks[14], (DEPTH, D_MODEL), 0.02),
        'w_up': nrm(ks[15], (DEPTH, D_MODEL, 2 * D_FF), D_MODEL ** -0.5),
        'conv_w': nrm(ks[16], (DEPTH, CONV_WIDTH, D_FF), CONV_WIDTH ** -0.5),
        'conv_b': nrm(ks[17], (DEPTH, D_FF), 0.02),
        'w_down': nrm(ks[18], (DEPTH, D_FF, D_MODEL), D_FF ** -0.5),
        'final_g': 1.0 + nrm(ks[19], (D_MODEL,), 0.02),
    }


def _fwd_reference(x, mem, norm1_g, w_in, ln_v_g, ln_v_b, w_spatial, b_spatial, lb_logits,
              hgrn_norm_g, mem_norm_g, w_mem_kv, w_branch, w_out, norm2_g, w_up,
              conv_w, conv_b, w_down, final_g):
    B, S, D = x.shape
    split_points = np.cumsum(IN_SPLITS)[:-1].tolist()
    lb_all = jnp.cumsum(jax.nn.softmax(lb_logits.astype(jnp.float32), axis=0), axis=0)
    for l in range(DEPTH):
        h = rms_norm(x, norm1_g[l])
        proj = h @ w_in[l]
        zu, zv, hq, hf, hi, hg, xq, gate_logits = jnp.split(proj, split_points, axis=-1)
        a_out = spatial_gating(jax.nn.gelu(zu), jax.nn.gelu(zv), ln_v_g[l], ln_v_b[l],
                               w_spatial[l], b_spatial[l])
        b_out = hgrn2(hq, hf, hi, hg, lb_all[l], hgrn_norm_g[l])
        c_out = memory_attention(xq, mem, mem_norm_g[l], w_mem_kv[l])
        branches = jnp.stack([a_out, b_out, c_out], axis=0)
        up = jnp.einsum('nbsc,ncd->nbsd', branches, w_branch[l])
        gates = jax.nn.sigmoid(gate_logits.reshape(B, S, N_BRANCH, D))
        merged = jnp.einsum('bsnd,nbsd->bsd', gates, up)
        x = x + merged @ w_out[l]
        x = x + conv_ffn(rms_norm(x, norm2_g[l]), w_up[l], conv_w[l], conv_b[l], w_down[l])
    return rms_norm(x, final_g)


import jax as _jax
import jax.numpy as _jnp

TWIN_FORMAT = 'train_step'
FWD_PARAMS = ['x', 'mem', 'norm1_g', 'w_in', 'ln_v_g', 'ln_v_b', 'w_spatial', 'b_spatial', 'lb_logits', 'hgrn_norm_g', 'mem_norm_g', 'w_mem_kv', 'w_branch', 'w_out', 'norm2_g', 'w_up', 'conv_w', 'conv_b', 'w_down', 'final_g']
TWIN_WEIGHTS = ['norm1_g', 'w_in', 'ln_v_g', 'ln_v_b', 'w_spatial', 'b_spatial', 'lb_logits', 'hgrn_norm_g', 'mem_norm_g', 'w_mem_kv', 'w_branch', 'w_out', 'norm2_g', 'w_up', 'conv_w', 'conv_b', 'w_down', 'final_g']
TWIN_DIFF_INPUT = 'x'
TWIN_INPUTS = ['x', 'mem', 'norm1_g', 'w_in', 'ln_v_g', 'ln_v_b', 'w_spatial', 'b_spatial', 'lb_logits', 'hgrn_norm_g', 'mem_norm_g', 'w_mem_kv', 'w_branch', 'w_out', 'norm2_g', 'w_up', 'conv_w', 'conv_b', 'w_down', 'final_g', 'loss_target', 'm_norm1_g', 'm_w_in', 'm_ln_v_g', 'm_ln_v_b', 'm_w_spatial', 'm_b_spatial', 'm_lb_logits', 'm_hgrn_norm_g', 'm_mem_norm_g', 'm_w_mem_kv', 'm_w_branch', 'm_w_out', 'm_norm2_g', 'm_w_up', 'm_conv_w', 'm_conv_b', 'm_w_down', 'm_final_g', 'v_norm1_g', 'v_w_in', 'v_ln_v_g', 'v_ln_v_b', 'v_w_spatial', 'v_b_spatial', 'v_lb_logits', 'v_hgrn_norm_g', 'v_mem_norm_g', 'v_w_mem_kv', 'v_w_branch', 'v_w_out', 'v_norm2_g', 'v_w_up', 'v_conv_w', 'v_conv_b', 'v_w_down', 'v_final_g']
TWIN_OUTPUTS = ['loss', 'grad_x', 'grad_norm1_g', 'grad_w_in', 'grad_ln_v_g', 'grad_ln_v_b', 'grad_w_spatial', 'grad_b_spatial', 'grad_lb_logits', 'grad_hgrn_norm_g', 'grad_mem_norm_g', 'grad_w_mem_kv', 'grad_w_branch', 'grad_w_out', 'grad_norm2_g', 'grad_w_up', 'grad_conv_w', 'grad_conv_b', 'grad_w_down', 'grad_final_g', 'delta_norm1_g', 'delta_w_in', 'delta_ln_v_g', 'delta_ln_v_b', 'delta_w_spatial', 'delta_b_spatial', 'delta_lb_logits', 'delta_hgrn_norm_g', 'delta_mem_norm_g', 'delta_w_mem_kv', 'delta_w_branch', 'delta_w_out', 'delta_norm2_g', 'delta_w_up', 'delta_conv_w', 'delta_conv_b', 'delta_w_down', 'delta_final_g', 'new_m_norm1_g', 'new_m_w_in', 'new_m_ln_v_g', 'new_m_ln_v_b', 'new_m_w_spatial', 'new_m_b_spatial', 'new_m_lb_logits', 'new_m_hgrn_norm_g', 'new_m_mem_norm_g', 'new_m_w_mem_kv', 'new_m_w_branch', 'new_m_w_out', 'new_m_norm2_g', 'new_m_w_up', 'new_m_conv_w', 'new_m_conv_b', 'new_m_w_down', 'new_m_final_g', 'new_v_norm1_g', 'new_v_w_in', 'new_v_ln_v_g', 'new_v_ln_v_b', 'new_v_w_spatial', 'new_v_b_spatial', 'new_v_lb_logits', 'new_v_hgrn_norm_g', 'new_v_mem_norm_g', 'new_v_w_mem_kv', 'new_v_w_branch', 'new_v_w_out', 'new_v_norm2_g', 'new_v_w_up', 'new_v_conv_w', 'new_v_conv_b', 'new_v_w_down', 'new_v_final_g']
TWIN_LEAF_KINDS = {'loss': 'loss', 'grad_x': 'grad_x', 'grad_norm1_g': 'grad_w', 'grad_w_in': 'grad_w', 'grad_ln_v_g': 'grad_w', 'grad_ln_v_b': 'grad_w', 'grad_w_spatial': 'grad_w', 'grad_b_spatial': 'grad_w', 'grad_lb_logits': 'grad_w', 'grad_hgrn_norm_g': 'grad_w', 'grad_mem_norm_g': 'grad_w', 'grad_w_mem_kv': 'grad_w', 'grad_w_branch': 'grad_w', 'grad_w_out': 'grad_w', 'grad_norm2_g': 'grad_w', 'grad_w_up': 'grad_w', 'grad_conv_w': 'grad_w', 'grad_conv_b': 'grad_w', 'grad_w_down': 'grad_w', 'grad_final_g': 'grad_w', 'delta_norm1_g': 'delta_w', 'delta_w_in': 'delta_w', 'delta_ln_v_g': 'delta_w', 'delta_ln_v_b': 'delta_w', 'delta_w_spatial': 'delta_w', 'delta_b_spatial': 'delta_w', 'delta_lb_logits': 'delta_w', 'delta_hgrn_norm_g': 'delta_w', 'delta_mem_norm_g': 'delta_w', 'delta_w_mem_kv': 'delta_w', 'delta_w_branch': 'delta_w', 'delta_w_out': 'delta_w', 'delta_norm2_g': 'delta_w', 'delta_w_up': 'delta_w', 'delta_conv_w': 'delta_w', 'delta_conv_b': 'delta_w', 'delta_w_down': 'delta_w', 'delta_final_g': 'delta_w', 'new_m_norm1_g': 'new_m', 'new_m_w_in': 'new_m', 'new_m_ln_v_g': 'new_m', 'new_m_ln_v_b': 'new_m', 'new_m_w_spatial': 'new_m', 'new_m_b_spatial': 'new_m', 'new_m_lb_logits': 'new_m', 'new_m_hgrn_norm_g': 'new_m', 'new_m_mem_norm_g': 'new_m', 'new_m_w_mem_kv': 'new_m', 'new_m_w_branch': 'new_m', 'new_m_w_out': 'new_m', 'new_m_norm2_g': 'new_m', 'new_m_w_up': 'new_m', 'new_m_conv_w': 'new_m', 'new_m_conv_b': 'new_m', 'new_m_w_down': 'new_m', 'new_m_final_g': 'new_m', 'new_v_norm1_g': 'new_v', 'new_v_w_in': 'new_v', 'new_v_ln_v_g': 'new_v', 'new_v_ln_v_b': 'new_v', 'new_v_w_spatial': 'new_v', 'new_v_b_spatial': 'new_v', 'new_v_lb_logits': 'new_v', 'new_v_hgrn_norm_g': 'new_v', 'new_v_mem_norm_g': 'new_v', 'new_v_w_mem_kv': 'new_v', 'new_v_w_branch': 'new_v', 'new_v_w_out': 'new_v', 'new_v_norm2_g': 'new_v', 'new_v_w_up': 'new_v', 'new_v_conv_w': 'new_v', 'new_v_conv_b': 'new_v', 'new_v_w_down': 'new_v', 'new_v_final_g': 'new_v'}


def _forward(args):
    return _fwd_reference(*[args[k] for k in FWD_PARAMS])


def _output_shape():
    out = _jax.eval_shape(lambda: _forward(_fwd_setup_inputs(0)))
    return out.shape, out.dtype

N_MICROBATCH = 1
ADAM_LR = 0.001
ADAM_B1 = 0.9
ADAM_B2 = 0.999
ADAM_EPS = 1e-08
ADAM_WD = 0.01
ADAM_STEP = 10
PER_EXAMPLE_BATCH_AXIS = {'x': 0, 'mem': 0, 'loss_target': 0}
SHARED_INPUTS = []
_WEIGHT_DTYPES = {'norm1_g': _jnp.float32, 'w_in': _jnp.float32, 'ln_v_g': _jnp.float32, 'ln_v_b': _jnp.float32, 'w_spatial': _jnp.float32, 'b_spatial': _jnp.float32, 'lb_logits': _jnp.float32, 'hgrn_norm_g': _jnp.float32, 'mem_norm_g': _jnp.float32, 'w_mem_kv': _jnp.float32, 'w_branch': _jnp.float32, 'w_out': _jnp.float32, 'norm2_g': _jnp.float32, 'w_up': _jnp.float32, 'conv_w': _jnp.float32, 'conv_b': _jnp.float32, 'w_down': _jnp.float32, 'final_g': _jnp.float32}
MOMENT_SCALE = {'norm1_g': 1.289560e-01, 'w_in': 5.033862e-02, 'ln_v_g': 6.275982e-02, 'ln_v_b': 6.618359e-02, 'w_spatial': 6.100067e-02, 'b_spatial': 8.679473e-02, 'lb_logits': 7.033376e-03, 'hgrn_norm_g': 1.584890e-01, 'mem_norm_g': 1.510075e-02, 'w_mem_kv': 1.402170e-02, 'w_branch': 5.547670e-02, 'w_out': 9.472240e-02, 'norm2_g': 1.207200e-01, 'w_up': 5.142491e-02, 'conv_w': 5.231362e-02, 'conv_b': 5.195323e-02, 'w_down': 8.411157e-02, 'final_g': 3.190134e+01}


def _to_microbatches(a, axis):
    t = _jnp.moveaxis(a, axis, 0)
    t = t.reshape((N_MICROBATCH, t.shape[0] // N_MICROBATCH) + t.shape[1:])
    return _jnp.moveaxis(t, 1, axis + 1)


def setup_inputs(seed: int = 0) -> dict:
    inp = _fwd_setup_inputs(seed)
    key = _jax.random.fold_in(_jax.random.key(seed), 7919)
    shape, _ = _output_shape()
    out = dict(inp)
    out["loss_target"] = _jax.random.normal(_jax.random.fold_in(key, 0), shape, _jnp.float32)
    for i, name in enumerate(TWIN_WEIGHTS):
        w = inp[name].astype(_jnp.float32)
        if MOMENT_SCALE is None:
            s = _jnp.sqrt(_jnp.mean(_jnp.square(w)) + 1e-30)
        else:
            s = MOMENT_SCALE[name]
        km, kv = _jax.random.split(_jax.random.fold_in(key, i + 1))
        out[name] = w
        out["m_" + name] = s * _jax.random.normal(km, w.shape, _jnp.float32)
        out["v_" + name] = (s * s) * _jax.random.uniform(kv, w.shape, _jnp.float32, 0.5, 1.5)
    if N_MICROBATCH > 1:
        for name, axis in PER_EXAMPLE_BATCH_AXIS.items():
            out[name] = _to_microbatches(out[name], axis)
    return {'x': out['x'], 'mem': out['mem'], 'norm1_g': out['norm1_g'], 'w_in': out['w_in'], 'ln_v_g': out['ln_v_g'], 'ln_v_b': out['ln_v_b'], 'w_spatial': out['w_spatial'], 'b_spatial': out['b_spatial'], 'lb_logits': out['lb_logits'], 'hgrn_norm_g': out['hgrn_norm_g'], 'mem_norm_g': out['mem_norm_g'], 'w_mem_kv': out['w_mem_kv'], 'w_branch': out['w_branch'], 'w_out': out['w_out'], 'norm2_g': out['norm2_g'], 'w_up': out['w_up'], 'conv_w': out['conv_w'], 'conv_b': out['conv_b'], 'w_down': out['w_down'], 'final_g': out['final_g'], 'loss_target': out['loss_target'], 'm_norm1_g': out['m_norm1_g'], 'm_w_in': out['m_w_in'], 'm_ln_v_g': out['m_ln_v_g'], 'm_ln_v_b': out['m_ln_v_b'], 'm_w_spatial': out['m_w_spatial'], 'm_b_spatial': out['m_b_spatial'], 'm_lb_logits': out['m_lb_logits'], 'm_hgrn_norm_g': out['m_hgrn_norm_g'], 'm_mem_norm_g': out['m_mem_norm_g'], 'm_w_mem_kv': out['m_w_mem_kv'], 'm_w_branch': out['m_w_branch'], 'm_w_out': out['m_w_out'], 'm_norm2_g': out['m_norm2_g'], 'm_w_up': out['m_w_up'], 'm_conv_w': out['m_conv_w'], 'm_conv_b': out['m_conv_b'], 'm_w_down': out['m_w_down'], 'm_final_g': out['m_final_g'], 'v_norm1_g': out['v_norm1_g'], 'v_w_in': out['v_w_in'], 'v_ln_v_g': out['v_ln_v_g'], 'v_ln_v_b': out['v_ln_v_b'], 'v_w_spatial': out['v_w_spatial'], 'v_b_spatial': out['v_b_spatial'], 'v_lb_logits': out['v_lb_logits'], 'v_hgrn_norm_g': out['v_hgrn_norm_g'], 'v_mem_norm_g': out['v_mem_norm_g'], 'v_w_mem_kv': out['v_w_mem_kv'], 'v_w_branch': out['v_w_branch'], 'v_w_out': out['v_w_out'], 'v_norm2_g': out['v_norm2_g'], 'v_w_up': out['v_w_up'], 'v_conv_w': out['v_conv_w'], 'v_conv_b': out['v_conv_b'], 'v_w_down': out['v_w_down'], 'v_final_g': out['v_final_g']}


def _loss(weights, diff, rest, loss_target):
    with _jax.named_scope("forward"):
        args = {**rest, TWIN_DIFF_INPUT: diff, **{k: w.astype(_WEIGHT_DTYPES[k]) for k, w in weights.items()}}
        y = _forward(args)
    with _jax.named_scope("loss_head"):
        err = _jnp.square(y.astype(_jnp.float32) - loss_target)
        return 0.5 * _jnp.sum(_jnp.mean(err, axis=-1)) if err.ndim else 0.5 * err


def _adamw(w, g, m, v):
    m = ADAM_B1 * m + (1.0 - ADAM_B1) * g
    v = ADAM_B2 * v + (1.0 - ADAM_B2) * _jnp.square(g)
    m_hat = m / (1.0 - ADAM_B1 ** ADAM_STEP)
    v_hat = v / (1.0 - ADAM_B2 ** ADAM_STEP)
    delta = -ADAM_LR * (m_hat / (_jnp.sqrt(v_hat) + ADAM_EPS) + ADAM_WD * w)
    return delta, m, v


def reference(x, mem, norm1_g, w_in, ln_v_g, ln_v_b, w_spatial, b_spatial, lb_logits, hgrn_norm_g, mem_norm_g, w_mem_kv, w_branch, w_out, norm2_g, w_up, conv_w, conv_b, w_down, final_g, loss_target, m_norm1_g, m_w_in, m_ln_v_g, m_ln_v_b, m_w_spatial, m_b_spatial, m_lb_logits, m_hgrn_norm_g, m_mem_norm_g, m_w_mem_kv, m_w_branch, m_w_out, m_norm2_g, m_w_up, m_conv_w, m_conv_b, m_w_down, m_final_g, v_norm1_g, v_w_in, v_ln_v_g, v_ln_v_b, v_w_spatial, v_b_spatial, v_lb_logits, v_hgrn_norm_g, v_mem_norm_g, v_w_mem_kv, v_w_branch, v_w_out, v_norm2_g, v_w_up, v_conv_w, v_conv_b, v_w_down, v_final_g):
    given = dict(x=x, mem=mem, norm1_g=norm1_g, w_in=w_in, ln_v_g=ln_v_g, ln_v_b=ln_v_b, w_spatial=w_spatial, b_spatial=b_spatial, lb_logits=lb_logits, hgrn_norm_g=hgrn_norm_g, mem_norm_g=mem_norm_g, w_mem_kv=w_mem_kv, w_branch=w_branch, w_out=w_out, norm2_g=norm2_g, w_up=w_up, conv_w=conv_w, conv_b=conv_b, w_down=w_down, final_g=final_g, loss_target=loss_target, m_norm1_g=m_norm1_g, m_w_in=m_w_in, m_ln_v_g=m_ln_v_g, m_ln_v_b=m_ln_v_b, m_w_spatial=m_w_spatial, m_b_spatial=m_b_spatial, m_lb_logits=m_lb_logits, m_hgrn_norm_g=m_hgrn_norm_g, m_mem_norm_g=m_mem_norm_g, m_w_mem_kv=m_w_mem_kv, m_w_branch=m_w_branch, m_w_out=m_w_out, m_norm2_g=m_norm2_g, m_w_up=m_w_up, m_conv_w=m_conv_w, m_conv_b=m_conv_b, m_w_down=m_w_down, m_final_g=m_final_g, v_norm1_g=v_norm1_g, v_w_in=v_w_in, v_ln_v_g=v_ln_v_g, v_ln_v_b=v_ln_v_b, v_w_spatial=v_w_spatial, v_b_spatial=v_b_spatial, v_lb_logits=v_lb_logits, v_hgrn_norm_g=v_hgrn_norm_g, v_mem_norm_g=v_mem_norm_g, v_w_mem_kv=v_w_mem_kv, v_w_branch=v_w_branch, v_w_out=v_w_out, v_norm2_g=v_norm2_g, v_w_up=v_w_up, v_conv_w=v_conv_w, v_conv_b=v_conv_b, v_w_down=v_w_down, v_final_g=v_final_g)
    weights = {n: given[n] for n in TWIN_WEIGHTS}
    shared = {n: given[n] for n in SHARED_INPUTS}
    per_example = {n: given[n] for n in ['x', 'mem']}
    grad_fn = _jax.value_and_grad(_loss, argnums=(0, 1))

    def one_microbatch(ex, loss_target):
        ex = dict(ex)
        diff = ex.pop(TWIN_DIFF_INPUT)
        return grad_fn(weights, diff, {**shared, **ex}, loss_target)

    if N_MICROBATCH == 1:
        loss, (grad_w, grad_x) = one_microbatch(per_example, given["loss_target"])
    else:
        def body(carry, xs):
            loss_sum, grad_sum = carry
            l_k, (gw_k, gx_k) = one_microbatch(xs[0], xs[1])
            with _jax.named_scope("update"):
                return (loss_sum + l_k, _jax.tree.map(_jnp.add, grad_sum, gw_k)), gx_k

        init = (_jnp.zeros((), _jnp.float32), _jax.tree.map(_jnp.zeros_like, weights))
        (loss, grad_w), grad_x = _jax.lax.scan(body, init, (per_example, given["loss_target"]))
    with _jax.named_scope("update"):
        delta_w, new_m, new_v = {}, {}, {}
        for n in TWIN_WEIGHTS:
            delta_w[n], new_m[n], new_v[n] = _adamw(weights[n], grad_w[n], given["m_" + n], given["v_" + n])
    return (loss, grad_x, *[grad_w[n] for n in TWIN_WEIGHTS], *[delta_w[n] for n in TWIN_WEIGHTS],
            *[new_m[n] for n in TWIN_WEIGHTS], *[new_v[n] for n in TWIN_WEIGHTS])
```

```python
import functools

import jax
import jax.numpy as jnp
from jax import lax
from jax.experimental import pallas as pl
from jax.experimental.pallas import tpu as pltpu

f32 = jnp.float32
bf16 = jnp.bfloat16

N_DEV = 8
D_MODEL = 1024
EPS = 1e-6
GM_CHUNK = 128
HG_CHUNK = 64
HEAD = 128
N_HEAD = 4
MEM_LEN = 256
D_FF = 2816
IN_WIDTH = 6656
C_ZU, C_HQ, C_HF, C_HI, C_HG, C_XQ, C_GL = 0, 1024, 1536, 2048, 2560, 3072, 3584
ADAM_LR, ADAM_B1, ADAM_B2, ADAM_EPS, ADAM_WD, ADAM_STEP = 0.001, 0.9, 0.999, 1e-08, 0.01, 10
VMEM_LIMIT = 56 * 1024 * 1024
MESH = pl.DeviceIdType.MESH


def _pick(n, cands):
    for c in cands:
        if n % c == 0:
            return c
    return n


def _call(body, name, grid, in_specs, out_specs, out_shape, scratch=(), sem=None, **cp):
    params = dict(vmem_limit_bytes=VMEM_LIMIT, **cp)
    if sem is not None:
        params["dimension_semantics"] = sem
    return pl.pallas_call(
        body, name=name, grid=grid, in_specs=in_specs, out_specs=out_specs, out_shape=out_shape,
        scratch_shapes=list(scratch), compiler_params=pltpu.CompilerParams(**params))


_DN = {"nn": (((1,), (0,)), ((), ())), "nt": (((1,), (1,)), ((), ())), "tn": (((0,), (0,)), ((), ()))}


def _raw_dot(a, b, mode):
    return lax.dot_general(a.astype(bf16), b.astype(bf16), _DN[mode], preferred_element_type=f32)


@jax.custom_vjp
def _dot_nn(a, b):
    return _raw_dot(a, b, "nn")


_dot_nn.defvjp(lambda a, b: (_raw_dot(a, b, "nn"), (a, b)),
               lambda r, g: (_raw_dot(g, r[1], "nt"), _raw_dot(r[0], g, "tn")))


@jax.custom_vjp
def _dot_nt(a, b):
    return _raw_dot(a, b, "nt")


_dot_nt.defvjp(lambda a, b: (_raw_dot(a, b, "nt"), (a, b)),
               lambda r, g: (_raw_dot(g, r[1], "nn"), _raw_dot(g, r[0], "tn")))


@jax.custom_vjp
def _dot_tn(a, b):
    return _raw_dot(a, b, "tn")


_dot_tn.defvjp(lambda a, b: (_raw_dot(a, b, "tn"), (a, b)),
               lambda r, g: (_raw_dot(r[1], g, "nt"), _raw_dot(r[0], g, "nn")))


def _tri(n, lower):
    r = lax.broadcasted_iota(jnp.int32, (n, n), 0)
    c = lax.broadcasted_iota(jnp.int32, (n, n), 1)
    return ((c <= r) if lower else (c >= r)).astype(f32)


def _tri_dot(x, lower):
    return jnp.dot(_tri(x.shape[0], lower), x, preferred_element_type=f32, precision=lax.Precision.HIGHEST)


@jax.custom_vjp
def _cumsum_rows(x):
    return _tri_dot(x, True)


_cumsum_rows.defvjp(lambda x: (_tri_dot(x, True), None), lambda _, g: (_tri_dot(g, False),))


def _egrad(fn, x, ct):
    return jax.vjp(fn, x)[1](ct)[0]


def _mm(a, b, mode, out_dtype, name, residual=None):
    if mode == "nn":
        (M, K), (_, N) = a.shape, b.shape
    elif mode == "nt":
        (M, K), (N, _) = a.shape, b.shape
    else:
        (K, M), (_, N) = a.shape, b.shape
    tm = _pick(M, (512, 256, 128))
    tn = _pick(N, (512, 256, 128))
    tk = _pick(K, (1024, 512, 256, 128)) if mode != "tn" else _pick(K, (512, 256, 128))
    nk = K // tk

    def body(*refs):
        if residual is None:
            a_ref, b_ref, o_ref, acc_ref = refs
        else:
            a_ref, b_ref, r_ref, o_ref, acc_ref = refs
        k = pl.program_id(2)

        @pl.when(k == 0)
        def _():
            acc_ref[...] = jnp.zeros_like(acc_ref)

        acc_ref[...] += _raw_dot(a_ref[...], b_ref[...], mode)

        @pl.when(k == nk - 1)
        def _():
            r = acc_ref[...]
            if residual is not None:
                r = r + r_ref[...]
            o_ref[...] = r.astype(out_dtype)

    a_spec = {"nn": pl.BlockSpec((tm, tk), lambda i, j, k: (i, k)),
              "nt": pl.BlockSpec((tm, tk), lambda i, j, k: (i, k)),
              "tn": pl.BlockSpec((tk, tm), lambda i, j, k: (k, i))}[mode]
    b_spec = {"nn": pl.BlockSpec((tk, tn), lambda i, j, k: (k, j)),
              "nt": pl.BlockSpec((tn, tk), lambda i, j, k: (j, k)),
              "tn": pl.BlockSpec((tk, tn), lambda i, j, k: (k, j))}[mode]
    o_spec = pl.BlockSpec((tm, tn), lambda i, j, k: (i, j))
    in_specs = [a_spec, b_spec] + ([o_spec] if residual is not None else [])
    args = (a, b) + ((residual,) if residual is not None else ())
    return _call(body, name, (M // tm, N // tn, nk), in_specs, o_spec, jax.ShapeDtypeStruct((M, N), out_dtype),
                 scratch=[pltpu.VMEM((tm, tn), f32)], sem=("parallel", "parallel", "arbitrary"))(*args)


def _rms_fwd(x, g, name):
    R, Dd = x.shape
    tr = _pick(R, (512, 256, 128))

    def body(x_ref, g_ref, o_ref):
        xf = x_ref[...]
        y = xf * lax.rsqrt(jnp.mean(xf * xf, axis=-1, keepdims=True) + EPS)
        o_ref[...] = (y * g_ref[...]).astype(bf16)

    return _call(body, name, (R // tr,), [pl.BlockSpec((tr, Dd), lambda i: (i, 0)), pl.BlockSpec((1, Dd), lambda i: (0, 0))],
                 pl.BlockSpec((tr, Dd), lambda i: (i, 0)), jax.ShapeDtypeStruct((R, Dd), bf16), sem=("parallel",))(x, g)


def _rms_bwd(x, g, dh, name, residual=None):
    R, Dd = x.shape
    tr = _pick(R, (512, 256, 128))

    def body(*refs):
        if residual is None:
            x_ref, g_ref, dh_ref, dx_ref, dg_ref = refs
        else:
            x_ref, g_ref, dh_ref, r_ref, dx_ref, dg_ref = refs
        xf = x_ref[...]
        rs = lax.rsqrt(jnp.mean(xf * xf, axis=-1, keepdims=True) + EPS)
        y = xf * rs
        dh_ = dh_ref[...].astype(f32)
        dy = dh_ * g_ref[...]
        dx = rs * (dy - y * jnp.mean(dy * y, axis=-1, keepdims=True))
        if residual is not None:
            dx = dx + r_ref[...]
        dx_ref[...] = dx

        @pl.when(pl.program_id(0) == 0)
        def _():
            dg_ref[...] = jnp.zeros_like(dg_ref)

        dg_ref[...] += jnp.sum(dh_ * y, axis=0, keepdims=True)

    row = pl.BlockSpec((tr, Dd), lambda i: (i, 0))
    vec = pl.BlockSpec((1, Dd), lambda i: (0, 0))
    in_specs = [row, vec, row] + ([row] if residual is not None else [])
    args = (x, g, dh) + ((residual,) if residual is not None else ())
    return _call(body, name, (R // tr,), in_specs, (row, vec),
                 (jax.ShapeDtypeStruct((R, Dd), f32), jax.ShapeDtypeStruct((1, Dd), f32)), sem=("arbitrary",))(*args)


def _final_loss(x2, g, target):
    R, Dd = x2.shape
    tr = _pick(R, (512, 256, 128))

    def body(x_ref, g_ref, t_ref, loss_ref, dx_ref, dg_ref):
        xf = x_ref[...]
        rs = lax.rsqrt(jnp.mean(xf * xf, axis=-1, keepdims=True) + EPS)
        y = xf * rs
        err = y * g_ref[...] - t_ref[...]
        dh_ = err * (1.0 / Dd)
        dy = dh_ * g_ref[...]
        dx_ref[...] = rs * (dy - y * jnp.mean(dy * y, axis=-1, keepdims=True))

        @pl.when(pl.program_id(0) == 0)
        def _():
            dg_ref[...] = jnp.zeros_like(dg_ref)
            loss_ref[...] = jnp.zeros_like(loss_ref)

        dg_ref[...] += jnp.sum(dh_ * y, axis=0, keepdims=True)
        part = jnp.sum(jnp.mean(err * err, axis=-1, keepdims=True), axis=0, keepdims=True)
        loss_ref[...] += 0.5 * part

    row = pl.BlockSpec((tr, Dd), lambda i: (i, 0))
    vec = pl.BlockSpec((1, Dd), lambda i: (0, 0))
    return _call(body, "final_loss", (R // tr,), [row, vec, row], (pl.BlockSpec((1, 128), lambda i: (0, 0)), row, vec),
                 (jax.ShapeDtypeStruct((1, 128), f32), jax.ShapeDtypeStruct((R, Dd), f32), jax.ShapeDtypeStruct((1, Dd), f32)),
                 sem=("arbitrary",))(x2, g, target)


def _gmlp_parts(zuv, ln_g, ln_b):
    zu, zv = zuv[:, :512], zuv[:, 512:]
    u = jax.nn.gelu(zu)
    v = jax.nn.gelu(zv)
    mu = jnp.mean(v, axis=-1, keepdims=True)
    rs = lax.rsqrt(jnp.mean(jnp.square(v - mu), axis=-1, keepdims=True) + EPS)
    xh = (v - mu) * rs
    return zu, zv, u, xh, rs, xh * ln_g + ln_b


def _gmlp_fwd(proj, ln_g, ln_b, w_s, b_st):
    T = proj.shape[0]

    def body(p_ref, g_ref, b_ref, w_ref, bs_ref, o_ref):
        _, _, u, _, _, vn = _gmlp_parts(p_ref[...], g_ref[...], b_ref[...])
        causal = _tri(GM_CHUNK, True) > 0
        for gi in range(N_HEAD):
            sl = slice(gi * HEAD, (gi + 1) * HEAD)
            w = jnp.where(causal, w_ref[gi], 0.0)
            mixed = _raw_dot(w, vn[:, sl], "nn") + bs_ref[:, gi:gi + 1]
            o_ref[:, sl] = (u[:, sl] * mixed).astype(bf16)

    vec = pl.BlockSpec((1, 512), lambda i: (0, 0))
    return _call(body, "gmlp_fwd", (T // GM_CHUNK,),
                 [pl.BlockSpec((GM_CHUNK, 1024), lambda i: (i, 0)), vec, vec,
                  pl.BlockSpec((N_HEAD, GM_CHUNK, GM_CHUNK), lambda i: (0, 0, 0)), pl.BlockSpec((GM_CHUNK, 128), lambda i: (0, 0))],
                 pl.BlockSpec((GM_CHUNK, 512), lambda i: (i, 0)), jax.ShapeDtypeStruct((T, 512), bf16), sem=("parallel",))(
        proj, ln_g, ln_b, w_s, b_st)


def _gmlp_bwd(proj, ln_g, ln_b, w_s, b_st, da):
    T = proj.shape[0]

    def body(p_ref, g_ref, b_ref, w_ref, bs_ref, da_ref, dp_ref, dg_ref, db_ref, dw_ref, dbs_ref):
        zu, zv, u, xh, rs, vn = _gmlp_parts(p_ref[...], g_ref[...], b_ref[...])
        causal = _tri(GM_CHUNK, True) > 0
        lane = lax.broadcasted_iota(jnp.int32, (GM_CHUNK, 128), 1)
        dout = da_ref[...].astype(f32)

        @pl.when(pl.program_id(0) == 0)
        def _():
            for r in (dg_ref, db_ref, dw_ref, dbs_ref):
                r[...] = jnp.zeros_like(r)

        du, dvn, dbs = [], [], jnp.zeros((GM_CHUNK, 128), f32)
        for gi in range(N_HEAD):
            sl = slice(gi * HEAD, (gi + 1) * HEAD)
            w = jnp.where(causal, w_ref[gi], 0.0)
            mixed = _raw_dot(w, vn[:, sl], "nn") + bs_ref[:, gi:gi + 1]
            du.append(dout[:, sl] * mixed)
            dm = dout[:, sl] * u[:, sl]
            dbs = dbs + jnp.where(lane == gi, jnp.sum(dm, axis=-1, keepdims=True), 0.0)
            dw_ref[gi] += jnp.where(causal, _raw_dot(dm, vn[:, sl], "nt"), 0.0)
            dvn.append(_raw_dot(w, dm, "tn"))
        dbs_ref[...] += dbs
        du = jnp.concatenate(du, axis=-1)
        dvn = jnp.concatenate(dvn, axis=-1)
        dg_ref[...] += jnp.sum(dvn * xh, axis=0, keepdims=True)
        db_ref[...] += jnp.sum(dvn, axis=0, keepdims=True)
        dxh = dvn * g_ref[...]
        dv = rs * (dxh - jnp.mean(dxh, axis=-1, keepdims=True) - xh * jnp.mean(dxh * xh, axis=-1, keepdims=True))
        dp_ref[:, :512] = _egrad(jax.nn.gelu, zu, du).astype(bf16)
        dp_ref[:, 512:] = _egrad(jax.nn.gelu, zv, dv).astype(bf16)

    vec = pl.BlockSpec((1, 512), lambda i: (0, 0))
    wsp = pl.BlockSpec((N_HEAD, GM_CHUNK, GM_CHUNK), lambda i: (0, 0, 0))
    bsp = pl.BlockSpec((GM_CHUNK, 128), lambda i: (0, 0))
    return _call(body, "gmlp_bwd", (T // GM_CHUNK,),
                 [pl.BlockSpec((GM_CHUNK, 1024), lambda i: (i, 0)), vec, vec, wsp, bsp, pl.BlockSpec((GM_CHUNK, 512), lambda i: (i, 0))],
                 (pl.BlockSpec((GM_CHUNK, 1024), lambda i: (i, 0)), vec, vec, wsp, bsp),
                 (jax.ShapeDtypeStruct((T, 1024), bf16), jax.ShapeDtypeStruct((1, 512), f32), jax.ShapeDtypeStruct((1, 512), f32),
                  jax.ShapeDtypeStruct((N_HEAD, GM_CHUNK, GM_CHUNK), f32), jax.ShapeDtypeStruct((GM_CHUNK, 128), f32)),
                 sem=("arbitrary",))(proj, ln_g, ln_b, w_s, b_st, da)


def _hgrn_chunk(st0, q_raw, f_raw, i_raw, g_raw, l0, l1, ng):
    C = HG_CHUNK
    lb = jax.nn.sigmoid(l0 - l1)
    fg = lb + (1.0 - lb) * jax.nn.sigmoid(f_raw)
    kk = 1.0 - fg
    qf = jax.nn.silu(q_raw)
    a = _cumsum_rows(jnp.log(fg))
    row = lax.broadcasted_iota(jnp.int32, (C, HEAD), 0)
    a_last = jnp.sum(jnp.where(row == C - 1, a, 0.0), axis=0, keepdims=True)
    inter = _dot_nt(qf * jnp.exp(a), st0)
    t_i = lax.broadcasted_iota(jnp.int32, (C, C, HEAD), 0)
    s_i = lax.broadcasted_iota(jnp.int32, (C, C, HEAD), 1)
    decay = jnp.exp(jnp.where(s_i <= t_i, a[:, None, :] - a[None, :, :], -jnp.inf))
    scores = jnp.sum(qf[:, None, :] * decay * kk[None, :, :], axis=-1)
    o = inter + _dot_nn(scores, i_raw)
    st1 = jnp.exp(a_last) * st0 + _dot_tn(i_raw, kk * jnp.exp(a_last - a))
    on = o * lax.rsqrt(jnp.mean(o * o, axis=-1, keepdims=True) + EPS) * ng
    return st1, on * jax.nn.silu(g_raw)


def _hgrn_specs(S, rev):
    N = S // HG_CHUNK

    def col(c0):
        if rev:
            return pl.BlockSpec((HG_CHUNK, HEAD), lambda h, b, n: (b * N + (N - 1 - n), c0 // HEAD + h))
        return pl.BlockSpec((HG_CHUNK, HEAD), lambda h, b, n: (b * N + n, c0 // HEAD + h))

    def st():
        if rev:
            return pl.BlockSpec((1, 1, 1, HEAD, HEAD), lambda h, b, n: (h, b, N - 1 - n, 0, 0))
        return pl.BlockSpec((1, 1, 1, HEAD, HEAD), lambda h, b, n: (h, b, n, 0, 0))

    return N, col, st


def _hgrn_fwd(proj, l0, l1, ng, Bl, S):
    N, col, st = _hgrn_specs(S, False)
    T = Bl * S

    def body(q_ref, f_ref, i_ref, g_ref, l0_ref, l1_ref, ng_ref, o_ref, st_ref, state):
        @pl.when(pl.program_id(2) == 0)
        def _():
            state[...] = jnp.zeros_like(state)

        st0 = state[...]
        st_ref[0, 0, 0] = st0
        st1, out = _hgrn_chunk(st0, q_ref[...], f_ref[...], i_ref[...], g_ref[...], l0_ref[...], l1_ref[...], ng_ref[...])
        state[...] = st1
        o_ref[...] = out.astype(bf16)

    lsp = pl.BlockSpec((1, HEAD), lambda h, b, n: (0, h))
    return _call(body, "hgrn_fwd", (N_HEAD, Bl, N),
                 [col(C_HQ), col(C_HF), col(C_HI), col(C_HG), lsp, lsp, pl.BlockSpec((1, HEAD), lambda h, b, n: (0, 0))],
                 (col(0), st()),
                 (jax.ShapeDtypeStruct((T, 512), bf16), jax.ShapeDtypeStruct((N_HEAD, Bl, N, HEAD, HEAD), f32)),
                 scratch=[pltpu.VMEM((HEAD, HEAD), f32)], sem=("arbitrary", "arbitrary", "arbitrary"))(
        proj, proj, proj, proj, l0, l1, ng)


def _hgrn_bwd(proj, l0, l1, ng, states, db, Bl, S):
    N, col, st = _hgrn_specs(S, True)
    T = Bl * S

    def body(q_ref, f_ref, i_ref, g_ref, l0_ref, l1_ref, ng_ref, st_ref, db_ref,
             dq_ref, df_ref, di_ref, dg_ref, dl0_ref, dl1_ref, dng_ref, dstate):
        @pl.when(pl.program_id(2) == 0)
        def _():
            dstate[...] = jnp.zeros_like(dstate)

        @pl.when((pl.program_id(1) == 0) & (pl.program_id(2) == 0))
        def _():
            for r in (dl0_ref, dl1_ref, dng_ref):
                r[...] = jnp.zeros_like(r)

        _, vjp = jax.vjp(_hgrn_chunk, st_ref[0, 0, 0], q_ref[...], f_ref[...], i_ref[...], g_ref[...],
                         l0_ref[...], l1_ref[...], ng_ref[...])
        dst0, dq, df, di, dg, dl0, dl1, dng = vjp((dstate[...], db_ref[...].astype(f32)))
        dstate[...] = dst0
        dq_ref[...] = dq.astype(bf16)
        df_ref[...] = df.astype(bf16)
        di_ref[...] = di.astype(bf16)
        dg_ref[...] = dg.astype(bf16)
        dl0_ref[...] += dl0
        dl1_ref[...] += dl1
        dng_ref[0] += dng

    lsp = pl.BlockSpec((1, HEAD), lambda h, b, n: (0, h))
    dsp = [pl.BlockSpec((HG_CHUNK, HEAD), lambda h, b, n: (b * N + (N - 1 - n), h))] * 4
    outs = _call(body, "hgrn_bwd", (N_HEAD, Bl, N),
                 [col(C_HQ), col(C_HF), col(C_HI), col(C_HG), lsp, lsp, pl.BlockSpec((1, HEAD), lambda h, b, n: (0, 0)), st(), col(0)],
                 (*dsp, lsp, lsp, pl.BlockSpec((1, 1, HEAD), lambda h, b, n: (h, 0, 0))),
                 (*[jax.ShapeDtypeStruct((T, 512), bf16)] * 4, jax.ShapeDtypeStruct((1, 512), f32), jax.ShapeDtypeStruct((1, 512), f32),
                  jax.ShapeDtypeStruct((N_HEAD, 1, HEAD), f32)),
                 scratch=[pltpu.VMEM((HEAD, HEAD), f32)], sem=("arbitrary", "arbitrary", "arbitrary"))(
        proj, proj, proj, proj, l0, l1, ng, states, db)
    return outs


def _attn_probs(q, k):
    s = _raw_dot(q, k, "nt") * (HEAD ** -0.5)
    e = jnp.exp(s - jnp.max(s, axis=-1, keepdims=True))
    return e / jnp.sum(e, axis=-1, keepdims=True)


def _attn_specs(S, tq):
    nq = S // tq
    q = pl.BlockSpec((tq, HEAD), lambda b, h, i: (b * nq + i, C_XQ // HEAD + h))
    k = pl.BlockSpec((1, MEM_LEN, HEAD), lambda b, h, i: (b, 0, h))
    v = pl.BlockSpec((1, MEM_LEN, HEAD), lambda b, h, i: (b, 0, N_HEAD + h))
    o = pl.BlockSpec((tq, HEAD), lambda b, h, i: (b * nq + i, h))
    return nq, q, k, v, o


def _attn_fwd(proj, kv, Bl, S):
    tq = _pick(S, (512, 256, 128))
    nq, qs, ks, vs, os_ = _attn_specs(S, tq)

    def body(q_ref, k_ref, v_ref, o_ref):
        p = _attn_probs(q_ref[...], k_ref[0])
        o_ref[...] = _raw_dot(p, v_ref[0], "nn").astype(bf16)

    return _call(body, "attn_fwd", (Bl, N_HEAD, nq), [qs, ks, vs], os_, jax.ShapeDtypeStruct((Bl * S, 512), bf16),
                 sem=("parallel", "parallel", "parallel"))(proj, kv, kv)


def _attn_bwd(proj, kv, dc, Bl, S):
    tq = _pick(S, (512, 256, 128))
    nq, qs, ks, vs, os_ = _attn_specs(S, tq)

    def body(q_ref, k_ref, v_ref, do_ref, dq_ref, dk_ref, dv_ref):
        @pl.when(pl.program_id(2) == 0)
        def _():
            dk_ref[...] = jnp.zeros_like(dk_ref)
            dv_ref[...] = jnp.zeros_like(dv_ref)

        q, k, v, do = q_ref[...], k_ref[0], v_ref[0], do_ref[...]
        p = _attn_probs(q, k)
        dv_ref[0] += _raw_dot(p, do, "tn")
        dp = _raw_dot(do, v, "nt")
        ds = p * (dp - jnp.sum(dp * p, axis=-1, keepdims=True)) * (HEAD ** -0.5)
        dq_ref[...] = _raw_dot(ds, k, "nn").astype(bf16)
        dk_ref[0] += _raw_dot(ds, q, "tn")

    acc = pl.BlockSpec((1, MEM_LEN, HEAD), lambda b, h, i: (b, 0, h))
    return _call(body, "attn_bwd", (Bl, N_HEAD, nq), [qs, ks, vs, os_], (os_, acc, acc),
                 (jax.ShapeDtypeStruct((Bl * S, 512), bf16), jax.ShapeDtypeStruct((Bl, MEM_LEN, 512), f32),
                  jax.ShapeDtypeStruct((Bl, MEM_LEN, 512), f32)),
                 sem=("arbitrary", "arbitrary", "arbitrary"))(proj, kv, kv, dc)


def _merge_specs(T, tm, tn):
    br = pl.BlockSpec((tm, 512), lambda i, j: (i, 0))
    w = pl.BlockSpec((3, 512, tn), lambda i, j: (0, 0, j))
    gl = [pl.BlockSpec((tm, tn), functools.partial(lambda i, j, n: (i, (C_GL + n * D_MODEL) // tn + j), n=n)) for n in range(3)]
    return br, w, gl


def _merge_fwd(branches, wb, proj):
    T = proj.shape[0]
    tm, tn = _pick(T, (512, 256, 128)), 512
    br, w, gl = _merge_specs(T, tm, tn)

    def body(a_ref, b_ref, c_ref, w_ref, g0, g1, g2, o_ref):
        acc = jnp.zeros((tm, tn), f32)
        for n, (x_ref, g_ref) in enumerate(((a_ref, g0), (b_ref, g1), (c_ref, g2))):
            acc = acc + jax.nn.sigmoid(g_ref[...]) * _raw_dot(x_ref[...], w_ref[n], "nn")
        o_ref[...] = acc.astype(bf16)

    return _call(body, "merge_fwd", (T // tm, D_MODEL // tn), [br, br, br, w, *gl], pl.BlockSpec((tm, tn), lambda i, j: (i, j)),
                 jax.ShapeDtypeStruct((T, D_MODEL), bf16), sem=("parallel", "parallel"))(*branches, wb, proj, proj, proj)


def _merge_bwd(branches, wb, proj, dmerged):
    T = proj.shape[0]
    tm, tn = _pick(T, (512, 256, 128)), 512
    br, w, gl = _merge_specs(T, tm, tn)

    def body(a_ref, b_ref, c_ref, w_ref, g0, g1, g2, dm_ref, dgl_ref, dup_ref):
        dm = dm_ref[...]
        for n, (x_ref, g_ref) in enumerate(((a_ref, g0), (b_ref, g1), (c_ref, g2))):
            up = _raw_dot(x_ref[...], w_ref[n], "nn")
            logits = g_ref[...]
            dgl_ref[n] = _egrad(jax.nn.sigmoid, logits, dm * up).astype(bf16)
            dup_ref[n] = (dm * jax.nn.sigmoid(logits)).astype(bf16)

    blk = pl.BlockSpec((3, tm, tn), lambda i, j: (0, i, j))
    return _call(body, "merge_bwd", (T // tm, D_MODEL // tn), [br, br, br, w, *gl, pl.BlockSpec((tm, tn), lambda i, j: (i, j))],
                 (blk, blk), (jax.ShapeDtypeStruct((3, T, D_MODEL), bf16), jax.ShapeDtypeStruct((3, T, D_MODEL), bf16)),
                 sem=("parallel", "parallel"))(*branches, wb, proj, proj, proj, dmerged)


CONV_TC = 256


def _shift_down(a, k):
    row = lax.broadcasted_iota(jnp.int32, a.shape, 0)
    return jnp.where(row >= k, pltpu.roll(a, k, 0), 0.0)


def _shift_up(a, k):
    n = a.shape[0]
    row = lax.broadcasted_iota(jnp.int32, a.shape, 0)
    return jnp.where(row < n - k, pltpu.roll(a, n - k, 0), 0.0)


def _conv_pre(a, cw, cb):
    return cb + cw[0:1] * _shift_down(a, 2) + cw[1:2] * _shift_down(a, 1) + cw[2:3] * a


def _conv_fwd(ab, cw, cb, Bl, S):
    nc = D_FF // CONV_TC

    def body(a_ref, b_ref, cw_ref, cb_ref, o_ref):
        ac = _conv_pre(a_ref[0], cw_ref[...], cb_ref[...])
        o_ref[0] = (jax.nn.silu(ac) * b_ref[0]).astype(bf16)

    return _call(body, "conv_fwd", (Bl, nc),
                 [pl.BlockSpec((1, S, CONV_TC), lambda b, c: (b, 0, c)), pl.BlockSpec((1, S, CONV_TC), lambda b, c: (b, 0, nc + c)),
                  pl.BlockSpec((3, CONV_TC), lambda b, c: (0, c)), pl.BlockSpec((1, CONV_TC), lambda b, c: (0, c))],
                 pl.BlockSpec((1, S, CONV_TC), lambda b, c: (b, 0, c)), jax.ShapeDtypeStruct((Bl, S, D_FF), bf16),
                 sem=("parallel", "parallel"))(ab, ab, cw, cb)


def _conv_bwd(ab, cw, cb, dact, Bl, S):
    nc = D_FF // CONV_TC

    def body(a_ref, b_ref, cw_ref, cb_ref, d_ref, da_ref, db_ref, dcw_ref, dcb_ref):
        @pl.when(pl.program_id(1) == 0)
        def _():
            dcw_ref[...] = jnp.zeros_like(dcw_ref)
            dcb_ref[...] = jnp.zeros_like(dcb_ref)

        a, cw = a_ref[0], cw_ref[...]
        ac = _conv_pre(a, cw, cb_ref[...])
        dact_ = d_ref[0].astype(f32)
        db_ref[0] = (dact_ * jax.nn.silu(ac)).astype(bf16)
        dac = _egrad(jax.nn.silu, ac, dact_ * b_ref[0])
        da_ref[0] = (cw[2:3] * dac + cw[1:2] * _shift_up(dac, 1) + cw[0:1] * _shift_up(dac, 2)).astype(bf16)
        dcw_ref[0:1, :] += jnp.sum(dac * _shift_down(a, 2), axis=0, keepdims=True)
        dcw_ref[1:2, :] += jnp.sum(dac * _shift_down(a, 1), axis=0, keepdims=True)
        dcw_ref[2:3, :] += jnp.sum(dac * a, axis=0, keepdims=True)
        dcb_ref[...] += jnp.sum(dac, axis=0, keepdims=True)

    seq = pl.BlockSpec((1, S, CONV_TC), lambda c, b: (b, 0, c))
    return _call(body, "conv_bwd", (nc, Bl),
                 [seq, pl.BlockSpec((1, S, CONV_TC), lambda c, b: (b, 0, nc + c)), pl.BlockSpec((3, CONV_TC), lambda c, b: (0, c)),
                  pl.BlockSpec((1, CONV_TC), lambda c, b: (0, c)), seq],
                 (seq, seq, pl.BlockSpec((3, CONV_TC), lambda c, b: (0, c)), pl.BlockSpec((1, CONV_TC), lambda c, b: (0, c))),
                 (jax.ShapeDtypeStruct((Bl, S, D_FF), bf16), jax.ShapeDtypeStruct((Bl, S, D_FF), bf16),
                  jax.ShapeDtypeStruct((3, D_FF), f32), jax.ShapeDtypeStruct((1, D_FF), f32)),
                 sem=("arbitrary", "arbitrary"))(ab, ab, cw, cb, dact)


def _local_step(x, mem, target, p, w):
    Bl, S, Dd = x.shape
    T = Bl * S
    x2d, t2d, mem2d = x.reshape(T, Dd), target.reshape(T, Dd), mem.reshape(Bl * MEM_LEN, Dd)
    b_st = jnp.pad(p["b_spatial"].T, ((0, 0), (0, 128 - N_HEAD)))
    l0, l1 = p["lb_logits"][0:1], p["lb_logits"][1:2]

    h = _rms_fwd(x2d, p["norm1_g"], "norm1_fwd")
    proj = _mm(h, w["w_in"], "nn", f32, "proj_fwd")
    a_out = _gmlp_fwd(proj, p["ln_v_g"], p["ln_v_b"], p["w_spatial"], b_st)
    b_out, states = _hgrn_fwd(proj, l0, l1, p["hgrn_norm_g"], Bl, S)
    memn = _rms_fwd(mem2d, p["mem_norm_g"], "memnorm_fwd")
    kv = _mm(memn, w["w_mem_kv"], "nn", f32, "kv_fwd").reshape(Bl, MEM_LEN, 2 * 512)
    c_out = _attn_fwd(proj, kv, Bl, S)
    branches = (a_out, b_out, c_out)
    merged = _merge_fwd(branches, w["w_branch"], proj)
    x1 = _mm(merged, w["w_out"], "nn", f32, "out_fwd", residual=x2d)
    h2 = _rms_fwd(x1, p["norm2_g"], "norm2_fwd")
    ab = _mm(h2, w["w_up"], "nn", f32, "up_fwd")
    act = _conv_fwd(ab.reshape(Bl, S, 2 * D_FF), p["conv_w"], p["conv_b"], Bl, S).reshape(T, D_FF)
    x2 = _mm(act, w["w_down"], "nn", f32, "down_fwd", residual=x1)
    loss_part, dx2, d_final_g = _final_loss(x2, p["final_g"], t2d)

    g_w_down = _mm(act, dx2, "tn", f32, "down_dw")
    dact = _mm(dx2, w["w_down"], "nt", bf16, "down_dx")
    da, db, g_conv_w, g_conv_b = _conv_bwd(ab.reshape(Bl, S, 2 * D_FF), p["conv_w"], p["conv_b"], dact.reshape(Bl, S, D_FF), Bl, S)
    dab = jnp.concatenate([da.reshape(T, D_FF), db.reshape(T, D_FF)], axis=-1)
    g_w_up = _mm(h2, dab, "tn", f32, "up_dw")
    dh2 = _mm(dab, w["w_up"], "nt", f32, "up_dx")
    dx1, g_norm2 = _rms_bwd(x1, p["norm2_g"], dh2, "norm2_bwd", residual=dx2)

    g_w_out = _mm(merged, dx1, "tn", f32, "out_dw")
    dmerged = _mm(dx1, w["w_out"], "nt", f32, "out_dx")
    dgl, dup = _merge_bwd(branches, w["w_branch"], proj, dmerged)
    g_w_branch = jnp.stack([_mm(branches[n], dup[n], "tn", f32, f"branch_dw{n}") for n in range(3)])
    dbr = [_mm(dup[n], w["w_branch"][n], "nt", bf16, f"branch_dx{n}") for n in range(3)]
    dzuv, g_ln_g, g_ln_b, g_w_sp, g_b_st = _gmlp_bwd(proj, p["ln_v_g"], p["ln_v_b"], p["w_spatial"], b_st, dbr[0])
    dq, df, di, dg, g_l0, g_l1, g_ng = _hgrn_bwd(proj, l0, l1, p["hgrn_norm_g"], states, dbr[1], Bl, S)
    dxq, dk, dv = _attn_bwd(proj, kv, dbr[2], Bl, S)
    dkv = jnp.concatenate([dk, dv], axis=-1).reshape(Bl * MEM_LEN, 2 * 512)
    g_w_kv = _mm(memn, dkv, "tn", f32, "kv_dw")
    dmemn = _mm(dkv, w["w_mem_kv"], "nt", f32, "kv_dx")
    _, g_mem_norm = _rms_bwd(mem2d, p["mem_norm_g"], dmemn, "memnorm_bwd")
    dproj = jnp.concatenate([dzuv, dq, df, di, dg, dxq, dgl[0], dgl[1], dgl[2]], axis=-1)
    g_w_in = _mm(h, dproj, "tn", f32, "proj_dw")
    dh = _mm(dproj, w["w_in"], "nt", f32, "proj_dx")
    dx, g_norm1 = _rms_bwd(x2d, p["norm1_g"], dh, "norm1_bwd", residual=dx1)

    gw = dict(w_in=g_w_in, w_mem_kv=g_w_kv, w_branch=g_w_branch, w_out=g_w_out, w_up=g_w_up, conv_w=g_conv_w, w_down=g_w_down)
    gs = dict(norm1_g=g_norm1, ln_v_g=g_ln_g, ln_v_b=g_ln_b, w_spatial=g_w_sp[None], b_spatial=g_b_st[:, :N_HEAD].T[None],
              lb_logits=jnp.concatenate([g_l0, g_l1], axis=0), hgrn_norm_g=jnp.sum(g_ng, axis=0), mem_norm_g=g_mem_norm,
              norm2_g=g_norm2, conv_b=g_conv_b, final_g=d_final_g.reshape(-1))
    return loss_part[0, 0], dx.reshape(Bl, S, Dd), gw, gs


def _coords():
    return lax.axis_index("x"), lax.axis_index("y"), lax.axis_index("c")


def _peers(x, y, c):
    rel = [(0, 0, 1), (0, 1, 0), (0, 1, 1), (1, 0, 0), (1, 0, 1), (1, 1, 0), (1, 1, 1)]
    return [(x ^ dx, y ^ dy, c ^ dc) for dx, dy, dc in rel]


def _slot(dev):
    return 4 * dev[0] + 2 * dev[1] + dev[2]


def _all_gather(blk, name):
    R, L = blk.shape

    def body(x_ref, o_ref, send_sems, recv_sems, local_sem):
        x, y, c = _coords()
        me = (x, y, c)
        mine = pltpu.make_async_copy(x_ref, o_ref.at[_slot(me)], local_sem)
        mine.start()
        sends = []
        for k, peer in enumerate(_peers(x, y, c)):
            cp = pltpu.make_async_remote_copy(src_ref=x_ref, dst_ref=o_ref.at[_slot(me)], send_sem=send_sems.at[k],
                                              recv_sem=recv_sems.at[k], device_id=peer, device_id_type=MESH)
            cp.start()
            sends.append(cp)
        for k, peer in enumerate(_peers(x, y, c)):
            pltpu.make_async_remote_copy(src_ref=x_ref, dst_ref=o_ref.at[_slot(peer)], send_sem=send_sems.at[k],
                                         recv_sem=recv_sems.at[k], device_id=peer, device_id_type=MESH).wait_recv()
        for cp in sends:
            cp.wait_send()
        mine.wait()

    return pl.pallas_call(
        body, name=name, out_shape=jax.ShapeDtypeStruct((N_DEV, R, L), blk.dtype),
        in_specs=[pl.BlockSpec(memory_space=pl.ANY)], out_specs=pl.BlockSpec(memory_space=pl.ANY),
        scratch_shapes=[pltpu.SemaphoreType.DMA((7,)), pltpu.SemaphoreType.DMA((7,)), pltpu.SemaphoreType.DMA],
        )(blk)


def _all_to_all(parts, name):
    _, R, L = parts.shape

    def body(x_ref, o_ref, send_sems, recv_sems, local_sem):
        x, y, c = _coords()
        me = (x, y, c)
        mine = pltpu.make_async_copy(x_ref.at[_slot(me)], o_ref.at[_slot(me)], local_sem)
        mine.start()
        sends = []
        for k, peer in enumerate(_peers(x, y, c)):
            cp = pltpu.make_async_remote_copy(src_ref=x_ref.at[_slot(peer)], dst_ref=o_ref.at[_slot(me)], send_sem=send_sems.at[k],
                                              recv_sem=recv_sems.at[k], device_id=peer, device_id_type=MESH)
            cp.start()
            sends.append(cp)
        for k, peer in enumerate(_peers(x, y, c)):
            pltpu.make_async_remote_copy(src_ref=x_ref.at[_slot(me)], dst_ref=o_ref.at[_slot(peer)], send_sem=send_sems.at[k],
                                         recv_sem=recv_sems.at[k], device_id=peer, device_id_type=MESH).wait_recv()
        for cp in sends:
            cp.wait_send()
        mine.wait()

    return pl.pallas_call(
        body, name=name, out_shape=jax.ShapeDtypeStruct((N_DEV, R, L), parts.dtype),
        in_specs=[pl.BlockSpec(memory_space=pl.ANY)], out_specs=pl.BlockSpec(memory_space=pl.ANY),
        scratch_shapes=[pltpu.SemaphoreType.DMA((7,)), pltpu.SemaphoreType.DMA((7,)), pltpu.SemaphoreType.DMA],
        )(parts)


def _sum_slots(parts, name):
    n, R, L = parts.shape
    tr = _pick(R, (1024, 512, 256, 128, 64, 32, 16, 8))

    def body(x_ref, o_ref):
        acc = x_ref[0].astype(f32)
        for i in range(1, n):
            acc = acc + x_ref[i].astype(f32)
        o_ref[...] = acc

    return _call(body, name, (R // tr,), [pl.BlockSpec((n, tr, L), lambda i: (0, i, 0))], pl.BlockSpec((tr, L), lambda i: (i, 0)),
                 jax.ShapeDtypeStruct((R, L), f32), sem=("parallel",))(parts)


def _adamw(w, g, m, v, name):
    R, L = w.shape
    tr = _pick(R, (256, 128, 64, 32, 16, 8))

    def body(w_ref, g_ref, m_ref, v_ref, d_ref, nm_ref, nv_ref):
        g_ = g_ref[...]
        m_ = ADAM_B1 * m_ref[...] + (1.0 - ADAM_B1) * g_
        v_ = ADAM_B2 * v_ref[...] + (1.0 - ADAM_B2) * jnp.square(g_)
        m_hat = m_ / (1.0 - ADAM_B1 ** ADAM_STEP)
        v_hat = v_ / (1.0 - ADAM_B2 ** ADAM_STEP)
        d_ref[...] = -ADAM_LR * (m_hat / (jnp.sqrt(v_hat) + ADAM_EPS) + ADAM_WD * w_ref[...])
        nm_ref[...] = m_
        nv_ref[...] = v_

    blk = pl.BlockSpec((tr, L), lambda i: (i, 0))
    sh = jax.ShapeDtypeStruct((R, L), f32)
    return _call(body, name, (R // tr,), [blk] * 4, (blk,) * 3, (sh,) * 3, sem=("parallel",))(w, g, m, v)


BIG = ("w_in", "w_mem_kv", "w_branch", "w_out", "w_up", "conv_w", "w_down")
SMALL = ("norm1_g", "ln_v_g", "ln_v_b", "w_spatial", "b_spatial", "lb_logits", "hgrn_norm_g", "mem_norm_g", "norm2_g",
         "conv_b", "final_g")
CONV_W_PAD = 2048


def _flat_rows(a, pad_to=None):
    a = a.reshape(-1)
    if pad_to is not None:
        a = jnp.pad(a, (0, pad_to - a.shape[0]))
    return a.reshape(-1, 128)


def _shard_of_full(name, full, j):
    if name in ("w_in", "w_up", "conv_w"):
        n = full.shape[-1] // N_DEV
        return full[..., j * n:(j + 1) * n]
    if name == "w_branch":
        return full[:, :, j * 128:(j + 1) * 128]
    n = full.shape[0] // N_DEV
    return full[j * n:(j + 1) * n]


PACK_ROWS = 1024


def _pack_shard(shards, names):
    rows = jnp.concatenate([_flat_rows(shards[n], CONV_W_PAD if n.startswith("conv_w") else None) for n in names], axis=0)
    return jnp.pad(rows, ((0, (-rows.shape[0]) % PACK_ROWS), (0, 0)))


def _unpack_shard(rows, shapes, names):
    out, r = {}, 0
    for n in names:
        size = 1
        for s in shapes[n]:
            size *= s
        nrows = (CONV_W_PAD if n.startswith("conv_w") else size) // 128
        out[n] = rows[r:r + nrows].reshape(-1)[:size].reshape(shapes[n])
        r += nrows
    return out


def _full_from_gathered(name, g):
    if name in ("w_in", "w_up", "conv_w", "conv_w_lo"):
        return jnp.moveaxis(g, 0, -2).reshape(g.shape[1:-1] + (N_DEV * g.shape[-1],))
    if name == "w_branch":
        return jnp.moveaxis(g, 0, -2).reshape(3, 512, N_DEV * 128)
    return g.reshape((N_DEV * g.shape[1],) + g.shape[2:])


def kernel(x, mem, norm1_g, w_in, ln_v_g, ln_v_b, w_spatial, b_spatial, lb_logits, hgrn_norm_g, mem_norm_g, w_mem_kv, w_branch, w_out, norm2_g, w_up, conv_w, conv_b, w_down, final_g, loss_target, m_norm1_g, m_w_in, m_ln_v_g, m_ln_v_b, m_w_spatial, m_b_spatial, m_lb_logits, m_hgrn_norm_g, m_mem_norm_g, m_w_mem_kv, m_w_branch, m_w_out, m_norm2_g, m_w_up, m_conv_w, m_conv_b, m_w_down, m_final_g, v_norm1_g, v_w_in, v_ln_v_g, v_ln_v_b, v_w_spatial, v_b_spatial, v_lb_logits, v_hgrn_norm_g, v_mem_norm_g, v_w_mem_kv, v_w_branch, v_w_out, v_norm2_g, v_w_up, v_conv_w, v_conv_b, v_w_down, v_final_g):
    given = dict(locals())
    order = ("norm1_g", "w_in", "ln_v_g", "ln_v_b", "w_spatial", "b_spatial", "lb_logits", "hgrn_norm_g", "mem_norm_g",
             "w_mem_kv", "w_branch", "w_out", "norm2_g", "w_up", "conv_w", "conv_b", "w_down", "final_g")

    shard = {n: given[n][0] for n in BIG}
    shard_shapes = {n: shard[n].shape for n in BIG}

    wire = {n: shard[n].astype(bf16) for n in BIG}
    wire["conv_w_lo"] = (shard["conv_w"] - wire["conv_w"].astype(f32)).astype(bf16)
    wire_names = BIG + ("conv_w_lo",)
    wire_shapes = dict(shard_shapes, conv_w_lo=shard_shapes["conv_w"])
    gathered = _all_gather(_pack_shard(wire, wire_names), "gather_weights")
    per_dev = [_unpack_shard(gathered[j], wire_shapes, wire_names) for j in range(N_DEV)]
    w_full = {n: _full_from_gathered(n, jnp.stack([per_dev[j][n] for j in range(N_DEV)])) for n in wire_names}

    p = {n: given[n] for n in ("norm1_g", "ln_v_g", "ln_v_b", "lb_logits", "hgrn_norm_g", "mem_norm_g", "norm2_g", "conv_b")}
    p["w_spatial"] = w_spatial[0]
    p["b_spatial"] = b_spatial[0]
    p["final_g"] = final_g.reshape(1, -1)
    p["conv_w"] = w_full.pop("conv_w").astype(f32) + w_full.pop("conv_w_lo").astype(f32)

    loss_part, grad_x, gw, gs = _local_step(x, mem, loss_target, p, w_full)
    loss = lax.psum(loss_part, ("x", "y", "c"))

    parts = jnp.stack([_pack_shard({n: _shard_of_full(n, gw[n], j).astype(bf16) for n in BIG}, BIG) for j in range(N_DEV)])
    g_rows = _sum_slots(_all_to_all(parts, "scatter_grads"), "sum_grads")
    g_shard = _unpack_shard(g_rows, shard_shapes, BIG)

    gs_rows = jnp.concatenate([_flat_rows(gs[n].astype(f32)) for n in SMALL], axis=0)
    pad = (-gs_rows.shape[0]) % 128
    gs_rows = jnp.pad(gs_rows, ((0, pad), (0, 0)))
    gs_sum = _sum_slots(_all_gather(gs_rows, "gather_small_grads"), "sum_small_grads")

    def small_rows(prefix):
        rows = jnp.concatenate([_flat_rows(given[prefix + n]) for n in SMALL], axis=0)
        return jnp.pad(rows, ((0, pad), (0, 0)))

    sd, sm, sv = _adamw(small_rows(""), gs_sum, small_rows("m_"), small_rows("v_"), "adamw_small")

    def unpack_small(rows):
        out, r = {}, 0
        for n in SMALL:
            nrows = given[n].size // 128
            out[n] = rows[r:r + nrows].reshape(given[n].shape)
            r += nrows
        return out

    grads, delta, new_m, new_v = unpack_small(gs_sum), unpack_small(sd), unpack_small(sm), unpack_small(sv)
    for n in BIG:
        shp = given[n].shape
        two_d = (-1, shp[-1])
        d_, m_, v_ = _adamw(given[n].reshape(two_d), g_shard[n].reshape(two_d), given["m_" + n].reshape(two_d),
                            given["v_" + n].reshape(two_d), "adamw_" + n)
        grads[n], delta[n], new_m[n], new_v[n] = g_shard[n].reshape(shp), d_.reshape(shp), m_.reshape(shp), v_.reshape(shp)

    return (loss, grad_x, *[grads[n] for n in order], *[delta[n] for n in order], *[new_m[n] for n in order],
            *[new_v[n] for n in order])
```

```python
import functools

import jax
import jax.numpy as jnp
from jax import lax
from jax.experimental import pallas as pl
from jax.experimental.pallas import tpu as pltpu

f32 = jnp.float32
bf16 = jnp.bfloat16

N_DEV = 8
D_MODEL = 1024
EPS = 1e-6
GM_CHUNK = 128
HG_CHUNK = 64
HEAD = 128
N_HEAD = 4
MEM_LEN = 256
D_FF = 2816
IN_WIDTH = 6656
C_ZU, C_HQ, C_HF, C_HI, C_HG, C_XQ, C_GL = 0, 1024, 1536, 2048, 2560, 3072, 3584
ADAM_LR, ADAM_B1, ADAM_B2, ADAM_EPS, ADAM_WD, ADAM_STEP = 0.001, 0.9, 0.999, 1e-08, 0.01, 10
VMEM_LIMIT = 56 * 1024 * 1024
MESH = pl.DeviceIdType.MESH
HIGHEST = lax.Precision.HIGHEST


def _pick(n, cands):
    for c in cands:
        if n % c == 0:
            return c
    return n


def _call(body, name, grid, in_specs, out_specs, out_shape, scratch=(), sem=None, **cp):
    params = dict(vmem_limit_bytes=VMEM_LIMIT, **cp)
    if sem is not None:
        params["dimension_semantics"] = sem
    return pl.pallas_call(
        body, name=name, grid=grid, in_specs=in_specs, out_specs=out_specs, out_shape=out_shape,
        scratch_shapes=list(scratch), compiler_params=pltpu.CompilerParams(**params))


_DN = {"nn": (((1,), (0,)), ((), ())), "nt": (((1,), (1,)), ((), ())), "tn": (((0,), (0,)), ((), ()))}


def _raw_dot(a, b, mode):
    return lax.dot_general(a.astype(bf16), b.astype(bf16), _DN[mode], preferred_element_type=f32)


@jax.custom_vjp
def _dot_nn(a, b):
    return _raw_dot(a, b, "nn")


_dot_nn.defvjp(lambda a, b: (_raw_dot(a, b, "nn"), (a, b)),
               lambda r, g: (_raw_dot(g, r[1], "nt"), _raw_dot(r[0], g, "tn")))


@jax.custom_vjp
def _dot_nt(a, b):
    return _raw_dot(a, b, "nt")


_dot_nt.defvjp(lambda a, b: (_raw_dot(a, b, "nt"), (a, b)),
               lambda r, g: (_raw_dot(g, r[1], "nn"), _raw_dot(g, r[0], "tn")))


@jax.custom_vjp
def _dot_tn(a, b):
    return _raw_dot(a, b, "tn")


_dot_tn.defvjp(lambda a, b: (_raw_dot(a, b, "tn"), (a, b)),
               lambda r, g: (_raw_dot(r[1], g, "nt"), _raw_dot(r[0], g, "nn")))


def _tri(n, lower):
    r = lax.broadcasted_iota(jnp.int32, (n, n), 0)
    c = lax.broadcasted_iota(jnp.int32, (n, n), 1)
    return ((c <= r) if lower else (c >= r)).astype(f32)


def _tri_dot(x, lower):
    return jnp.dot(_tri(x.shape[0], lower), x, preferred_element_type=f32, precision=HIGHEST)


@jax.custom_vjp
def _cumsum_rows(x):
    return _tri_dot(x, True)


_cumsum_rows.defvjp(lambda x: (_tri_dot(x, True), None), lambda _, g: (_tri_dot(g, False),))


def _egrad(fn, x, ct):
    return jax.vjp(fn, x)[1](ct)[0]


def _mm(a, b, mode, out_dtype, name, tm, tn, tk=None, residual=None):
    if mode == "nn":
        (M, K), (_, N) = a.shape, b.shape
    elif mode == "nt":
        (M, K), (N, _) = a.shape, b.shape
    else:
        (K, M), (_, N) = a.shape, b.shape
    tm, tn = min(tm, M), min(tn, N)
    tk = K if tk is None else min(tk, K)
    assert M % tm == 0 and N % tn == 0 and K % tk == 0, (name, M, N, K, tm, tn, tk)
    nk = K // tk

    def body(*refs):
        acc_ref = refs[-1] if nk > 1 else None
        refs = refs[:-1] if nk > 1 else refs
        if residual is None:
            a_ref, b_ref, o_ref = refs
        else:
            a_ref, b_ref, r_ref, o_ref = refs

        def finish(r):
            if residual is not None:
                r = r + r_ref[...]
            o_ref[...] = r.astype(out_dtype)

        part = _raw_dot(a_ref[...], b_ref[...], mode)
        if nk == 1:
            finish(part)
            return
        k = pl.program_id(2)

        @pl.when(k == 0)
        def _():
            acc_ref[...] = part

        @pl.when((k > 0) & (k < nk - 1))
        def _():
            acc_ref[...] += part

        @pl.when(k == nk - 1)
        def _():
            finish(acc_ref[...] + part)

    a_spec = {"nn": pl.BlockSpec((tm, tk), lambda i, j, k: (i, k)),
              "nt": pl.BlockSpec((tm, tk), lambda i, j, k: (i, k)),
              "tn": pl.BlockSpec((tk, tm), lambda i, j, k: (k, i))}[mode]
    b_spec = {"nn": pl.BlockSpec((tk, tn), lambda i, j, k: (k, j)),
              "nt": pl.BlockSpec((tn, tk), lambda i, j, k: (j, k)),
              "tn": pl.BlockSpec((tk, tn), lambda i, j, k: (k, j))}[mode]
    o_spec = pl.BlockSpec((tm, tn), lambda i, j, k: (i, j))
    in_specs = [a_spec, b_spec] + ([o_spec] if residual is not None else [])
    args = (a, b) + ((residual,) if residual is not None else ())
    return _call(body, name, (M // tm, N // tn, nk), in_specs, o_spec, jax.ShapeDtypeStruct((M, N), out_dtype),
                 scratch=[pltpu.VMEM((tm, tn), f32)] if nk > 1 else [], sem=("parallel", "parallel", "arbitrary"))(*args)


def _rms_fwd(x, g, name):
    R, Dd = x.shape
    tr = _pick(R, (512, 256, 128))

    def body(x_ref, g_ref, o_ref):
        xf = x_ref[...]
        y = xf * lax.rsqrt(jnp.mean(xf * xf, axis=-1, keepdims=True) + EPS)
        o_ref[...] = (y * g_ref[...]).astype(bf16)

    return _call(body, name, (R // tr,), [pl.BlockSpec((tr, Dd), lambda i: (i, 0)), pl.BlockSpec((1, Dd), lambda i: (0, 0))],
                 pl.BlockSpec((tr, Dd), lambda i: (i, 0)), jax.ShapeDtypeStruct((R, Dd), bf16), sem=("parallel",))(x, g)


def _rms_bwd(x, g, dh, name, residual=None):
    R, Dd = x.shape
    tr = _pick(R, (512, 256, 128))

    def body(*refs):
        if residual is None:
            x_ref, g_ref, dh_ref, dx_ref, dg_ref = refs
        else:
            x_ref, g_ref, dh_ref, r_ref, dx_ref, dg_ref = refs
        xf = x_ref[...]
        rs = lax.rsqrt(jnp.mean(xf * xf, axis=-1, keepdims=True) + EPS)
        y = xf * rs
        dh_ = dh_ref[...].astype(f32)
        dy = dh_ * g_ref[...]
        dx = rs * (dy - y * jnp.mean(dy * y, axis=-1, keepdims=True))
        if residual is not None:
            dx = dx + r_ref[...]
        dx_ref[...] = dx

        @pl.when(pl.program_id(0) == 0)
        def _():
            dg_ref[...] = jnp.zeros_like(dg_ref)

        dg_ref[...] += jnp.sum(dh_ * y, axis=0, keepdims=True)

    row = pl.BlockSpec((tr, Dd), lambda i: (i, 0))
    vec = pl.BlockSpec((1, Dd), lambda i: (0, 0))
    in_specs = [row, vec, row] + ([row] if residual is not None else [])
    args = (x, g, dh) + ((residual,) if residual is not None else ())
    return _call(body, name, (R // tr,), in_specs, (row, vec),
                 (jax.ShapeDtypeStruct((R, Dd), f32), jax.ShapeDtypeStruct((1, Dd), f32)), sem=("arbitrary",))(*args)


def _final_loss(x2, g, target):
    R, Dd = x2.shape
    tr = _pick(R, (512, 256, 128))

    def body(x_ref, g_ref, t_ref, loss_ref, dx_ref, dg_ref):
        xf = x_ref[...]
        rs = lax.rsqrt(jnp.mean(xf * xf, axis=-1, keepdims=True) + EPS)
        y = xf * rs
        err = y * g_ref[...] - t_ref[...]
        dh_ = err * (1.0 / Dd)
        dy = dh_ * g_ref[...]
        dx_ref[...] = rs * (dy - y * jnp.mean(dy * y, axis=-1, keepdims=True))

        @pl.when(pl.program_id(0) == 0)
        def _():
            dg_ref[...] = jnp.zeros_like(dg_ref)
            loss_ref[...] = jnp.zeros_like(loss_ref)

        dg_ref[...] += jnp.sum(dh_ * y, axis=0, keepdims=True)
        part = jnp.sum(jnp.mean(err * err, axis=-1, keepdims=True), axis=0, keepdims=True)
        loss_ref[...] += 0.5 * part

    row = pl.BlockSpec((tr, Dd), lambda i: (i, 0))
    vec = pl.BlockSpec((1, Dd), lambda i: (0, 0))
    return _call(body, "final_loss", (R // tr,), [row, vec, row], (pl.BlockSpec((1, 128), lambda i: (0, 0)), row, vec),
                 (jax.ShapeDtypeStruct((1, 128), f32), jax.ShapeDtypeStruct((R, Dd), f32), jax.ShapeDtypeStruct((1, Dd), f32)),
                 sem=("arbitrary",))(x2, g, target)


def _gmlp_parts(zuv, ln_g, ln_b):
    zu, zv = zuv[:, :512], zuv[:, 512:]
    u = jax.nn.gelu(zu)
    v = jax.nn.gelu(zv)
    mu = jnp.mean(v, axis=-1, keepdims=True)
    rs = lax.rsqrt(jnp.mean(jnp.square(v - mu), axis=-1, keepdims=True) + EPS)
    xh = (v - mu) * rs
    return zu, zv, u, xh, rs, xh * ln_g + ln_b


def _gmlp_fwd(proj, ln_g, ln_b, w_s, b_st):
    T = proj.shape[0]

    def body(p_ref, g_ref, b_ref, w_ref, bs_ref, o_ref):
        _, _, u, _, _, vn = _gmlp_parts(p_ref[...], g_ref[...], b_ref[...])
        causal = _tri(GM_CHUNK, True) > 0
        for gi in range(N_HEAD):
            sl = slice(gi * HEAD, (gi + 1) * HEAD)
            w = jnp.where(causal, w_ref[gi], 0.0)
            mixed = _raw_dot(w, vn[:, sl], "nn") + bs_ref[:, gi:gi + 1]
            o_ref[:, sl] = (u[:, sl] * mixed).astype(bf16)

    vec = pl.BlockSpec((1, 512), lambda i: (0, 0))
    return _call(body, "gmlp_fwd", (T // GM_CHUNK,),
                 [pl.BlockSpec((GM_CHUNK, 1024), lambda i: (i, 0)), vec, vec,
                  pl.BlockSpec((N_HEAD, GM_CHUNK, GM_CHUNK), lambda i: (0, 0, 0)), pl.BlockSpec((GM_CHUNK, 128), lambda i: (0, 0))],
                 pl.BlockSpec((GM_CHUNK, 512), lambda i: (i, 0)), jax.ShapeDtypeStruct((T, 512), bf16), sem=("parallel",))(
        proj, ln_g, ln_b, w_s, b_st)


def _gmlp_bwd(proj, ln_g, ln_b, w_s, b_st, da):
    T = proj.shape[0]

    def body(p_ref, g_ref, b_ref, w_ref, bs_ref, da_ref, dp_ref, dg_ref, db_ref, dw_ref, dbs_ref):
        zu, zv, u, xh, rs, vn = _gmlp_parts(p_ref[...], g_ref[...], b_ref[...])
        causal = _tri(GM_CHUNK, True) > 0
        sub = lax.broadcasted_iota(jnp.int32, (8, GM_CHUNK), 0)
        ones = jnp.ones((8, HEAD), f32)
        dout = da_ref[...].astype(f32)

        @pl.when(pl.program_id(0) == 0)
        def _():
            for r in (dg_ref, db_ref, dw_ref, dbs_ref):
                r[...] = jnp.zeros_like(r)

        du, dvn, dbs = [], [], jnp.zeros((8, GM_CHUNK), f32)
        for gi in range(N_HEAD):
            sl = slice(gi * HEAD, (gi + 1) * HEAD)
            w = jnp.where(causal, w_ref[gi], 0.0)
            mixed = _raw_dot(w, vn[:, sl], "nn") + bs_ref[:, gi:gi + 1]
            du.append(dout[:, sl] * mixed)
            dm = dout[:, sl] * u[:, sl]
            row_sums = lax.dot_general(ones, dm, _DN["nt"], precision=HIGHEST, preferred_element_type=f32)
            dbs = dbs + jnp.where(sub == gi, row_sums, 0.0)
            dw_ref[gi] += jnp.where(causal, _raw_dot(dm, vn[:, sl], "nt"), 0.0)
            dvn.append(_raw_dot(w, dm, "tn"))
        dbs_ref[...] += dbs
        du = jnp.concatenate(du, axis=-1)
        dvn = jnp.concatenate(dvn, axis=-1)
        dg_ref[...] += jnp.sum(dvn * xh, axis=0, keepdims=True)
        db_ref[...] += jnp.sum(dvn, axis=0, keepdims=True)
        dxh = dvn * g_ref[...]
        dv = rs * (dxh - jnp.mean(dxh, axis=-1, keepdims=True) - xh * jnp.mean(dxh * xh, axis=-1, keepdims=True))
        dp_ref[:, :512] = _egrad(jax.nn.gelu, zu, du).astype(bf16)
        dp_ref[:, 512:] = _egrad(jax.nn.gelu, zv, dv).astype(bf16)

    vec = pl.BlockSpec((1, 512), lambda i: (0, 0))
    wsp = pl.BlockSpec((N_HEAD, GM_CHUNK, GM_CHUNK), lambda i: (0, 0, 0))
    return _call(body, "gmlp_bwd", (T // GM_CHUNK,),
                 [pl.BlockSpec((GM_CHUNK, 1024), lambda i: (i, 0)), vec, vec, wsp, pl.BlockSpec((GM_CHUNK, 128), lambda i: (0, 0)),
                  pl.BlockSpec((GM_CHUNK, 512), lambda i: (i, 0))],
                 (pl.BlockSpec((GM_CHUNK, 1024), lambda i: (i, 0)), vec, vec, wsp, pl.BlockSpec((8, GM_CHUNK), lambda i: (0, 0))),
                 (jax.ShapeDtypeStruct((T, 1024), bf16), jax.ShapeDtypeStruct((1, 512), f32), jax.ShapeDtypeStruct((1, 512), f32),
                  jax.ShapeDtypeStruct((N_HEAD, GM_CHUNK, GM_CHUNK), f32), jax.ShapeDtypeStruct((8, GM_CHUNK), f32)),
                 sem=("arbitrary",))(proj, ln_g, ln_b, w_s, b_st, da)


def _hgrn_chunk(st0, q_raw, f_raw, i_raw, g_raw, l0, l1, ng):
    C = HG_CHUNK
    lb = jax.nn.sigmoid(l0 - l1)
    fg = lb + (1.0 - lb) * jax.nn.sigmoid(f_raw)
    kk = 1.0 - fg
    qf = jax.nn.silu(q_raw)
    a = _cumsum_rows(jnp.log(fg))
    row = lax.broadcasted_iota(jnp.int32, (C, HEAD), 0)
    a_last = jnp.sum(jnp.where(row == C - 1, a, 0.0), axis=0, keepdims=True)
    inter = _dot_nt(qf * jnp.exp(a), st0)
    t_i = lax.broadcasted_iota(jnp.int32, (C, C, HEAD), 0)
    s_i = lax.broadcasted_iota(jnp.int32, (C, C, HEAD), 1)
    decay = jnp.exp(jnp.where(s_i <= t_i, a[:, None, :] - a[None, :, :], -jnp.inf))
    scores = jnp.sum(qf[:, None, :] * decay * kk[None, :, :], axis=-1)
    o = inter + _dot_nn(scores, i_raw)
    st1 = jnp.exp(a_last) * st0 + _dot_tn(i_raw, kk * jnp.exp(a_last - a))
    on = o * lax.rsqrt(jnp.mean(o * o, axis=-1, keepdims=True) + EPS) * ng
    return st1, on * jax.nn.silu(g_raw)


def _hgrn_specs(S, rev):
    N = S // HG_CHUNK

    def col(c0):
        if rev:
            return pl.BlockSpec((HG_CHUNK, HEAD), lambda h, b, n: (b * N + (N - 1 - n), c0 // HEAD + h))
        return pl.BlockSpec((HG_CHUNK, HEAD), lambda h, b, n: (b * N + n, c0 // HEAD + h))

    def st():
        if rev:
            return pl.BlockSpec((1, 1, 1, HEAD, HEAD), lambda h, b, n: (h, b, N - 1 - n, 0, 0))
        return pl.BlockSpec((1, 1, 1, HEAD, HEAD), lambda h, b, n: (h, b, n, 0, 0))

    return N, col, st


def _hgrn_fwd(proj, lb_logits, ng, Bl, S):
    N, col, st = _hgrn_specs(S, False)
    T = Bl * S

    def body(q_ref, f_ref, i_ref, g_ref, l_ref, ng_ref, o_ref, st_ref, state):
        @pl.when(pl.program_id(2) == 0)
        def _():
            state[...] = jnp.zeros_like(state)

        st0 = state[...]
        st_ref[0, 0, 0] = st0
        st1, out = _hgrn_chunk(st0, q_ref[...], f_ref[...], i_ref[...], g_ref[...], l_ref[0:1, :], l_ref[1:2, :], ng_ref[...])
        state[...] = st1
        o_ref[...] = out.astype(bf16)

    lsp = pl.BlockSpec((2, HEAD), lambda h, b, n: (0, h))
    return _call(body, "hgrn_fwd", (N_HEAD, Bl, N),
                 [col(C_HQ), col(C_HF), col(C_HI), col(C_HG), lsp, pl.BlockSpec((1, HEAD), lambda h, b, n: (0, 0))],
                 (col(0), st()),
                 (jax.ShapeDtypeStruct((T, 512), bf16), jax.ShapeDtypeStruct((N_HEAD, Bl, N, HEAD, HEAD), f32)),
                 scratch=[pltpu.VMEM((HEAD, HEAD), f32)], sem=("arbitrary", "arbitrary", "arbitrary"))(
        proj, proj, proj, proj, lb_logits, ng)


def _hgrn_bwd(proj, lb_logits, ng, states, db, Bl, S):
    N, col, st = _hgrn_specs(S, True)
    T = Bl * S

    def body(q_ref, f_ref, i_ref, g_ref, l_ref, ng_ref, st_ref, db_ref,
             dq_ref, df_ref, di_ref, dg_ref, dl_ref, dng_ref, dstate):
        @pl.when(pl.program_id(2) == 0)
        def _():
            dstate[...] = jnp.zeros_like(dstate)

        @pl.when((pl.program_id(1) == 0) & (pl.program_id(2) == 0))
        def _():
            dl_ref[...] = jnp.zeros_like(dl_ref)
            dng_ref[...] = jnp.zeros_like(dng_ref)

        _, vjp = jax.vjp(_hgrn_chunk, st_ref[0, 0, 0], q_ref[...], f_ref[...], i_ref[...], g_ref[...],
                         l_ref[0:1, :], l_ref[1:2, :], ng_ref[...])
        dst0, dq, df, di, dg, dl0, dl1, dng = vjp((dstate[...], db_ref[...].astype(f32)))
        dstate[...] = dst0
        dq_ref[...] = dq.astype(bf16)
        df_ref[...] = df.astype(bf16)
        di_ref[...] = di.astype(bf16)
        dg_ref[...] = dg.astype(bf16)
        dl_ref[0:1, :] += dl0
        dl_ref[1:2, :] += dl1
        dng_ref[0] += dng

    lsp = pl.BlockSpec((2, HEAD), lambda h, b, n: (0, h))
    dsp = [pl.BlockSpec((HG_CHUNK, HEAD), lambda h, b, n: (b * N + (N - 1 - n), h))] * 4
    return _call(body, "hgrn_bwd", (N_HEAD, Bl, N),
                 [col(C_HQ), col(C_HF), col(C_HI), col(C_HG), lsp, pl.BlockSpec((1, HEAD), lambda h, b, n: (0, 0)), st(), col(0)],
                 (*dsp, lsp, pl.BlockSpec((1, 1, HEAD), lambda h, b, n: (h, 0, 0))),
                 (*[jax.ShapeDtypeStruct((T, 512), bf16)] * 4, jax.ShapeDtypeStruct((2, 512), f32),
                  jax.ShapeDtypeStruct((N_HEAD, 1, HEAD), f32)),
                 scratch=[pltpu.VMEM((HEAD, HEAD), f32)], sem=("arbitrary", "arbitrary", "arbitrary"))(
        proj, proj, proj, proj, lb_logits, ng, states, db)


def _attn_probs(q, k):
    s = _raw_dot(q, k, "nt") * (HEAD ** -0.5)
    e = jnp.exp(s - jnp.max(s, axis=-1, keepdims=True))
    return e / jnp.sum(e, axis=-1, keepdims=True)


def _attn_specs(S, tq):
    nq = S // tq
    q = pl.BlockSpec((tq, 512), lambda b, i: (b * nq + i, C_XQ // 512))
    kv = pl.BlockSpec((1, MEM_LEN, 1024), lambda b, i: (b, 0, 0))
    o = pl.BlockSpec((tq, 512), lambda b, i: (b * nq + i, 0))
    return nq, q, kv, o


def _attn_fwd(proj, kv, Bl, S):
    tq = _pick(S, (512, 256, 128))
    nq, qs, kvs, os_ = _attn_specs(S, tq)

    def body(q_ref, kv_ref, o_ref):
        for h in range(N_HEAD):
            sl = slice(h * HEAD, (h + 1) * HEAD)
            p = _attn_probs(q_ref[:, sl], kv_ref[0, :, sl])
            o_ref[:, sl] = _raw_dot(p, kv_ref[0, :, 512 + h * HEAD:512 + (h + 1) * HEAD], "nn").astype(bf16)

    return _call(body, "attn_fwd", (Bl, nq), [qs, kvs], os_, jax.ShapeDtypeStruct((Bl * S, 512), bf16),
                 sem=("parallel", "parallel"))(proj, kv)


def _attn_bwd(proj, kv, dc, Bl, S):
    tq = _pick(S, (512, 256, 128))
    nq, qs, kvs, os_ = _attn_specs(S, tq)

    def body(q_ref, kv_ref, do_ref, dq_ref, dkv_ref):
        @pl.when(pl.program_id(1) == 0)
        def _():
            dkv_ref[...] = jnp.zeros_like(dkv_ref)

        for h in range(N_HEAD):
            sl = slice(h * HEAD, (h + 1) * HEAD)
            vsl = slice(512 + h * HEAD, 512 + (h + 1) * HEAD)
            q, k, v, do = q_ref[:, sl], kv_ref[0, :, sl], kv_ref[0, :, vsl], do_ref[:, sl]
            p = _attn_probs(q, k)
            dkv_ref[0, :, vsl] += _raw_dot(p, do, "tn")
            dp = _raw_dot(do, v, "nt")
            ds = p * (dp - jnp.sum(dp * p, axis=-1, keepdims=True)) * (HEAD ** -0.5)
            dq_ref[:, sl] = _raw_dot(ds, k, "nn").astype(bf16)
            dkv_ref[0, :, sl] += _raw_dot(ds, q, "tn")

    return _call(body, "attn_bwd", (Bl, nq), [qs, kvs, os_], (os_, kvs),
                 (jax.ShapeDtypeStruct((Bl * S, 512), bf16), jax.ShapeDtypeStruct((Bl, MEM_LEN, 1024), f32)),
                 sem=("arbitrary", "arbitrary"))(proj, kv, dc)


def _merge_specs(tm, tn):
    br = pl.BlockSpec((tm, 512), lambda i, j: (i, 0))
    w = pl.BlockSpec((512, tn), lambda i, j: (0, j))
    gl = [pl.BlockSpec((tm, tn), functools.partial(lambda i, j, n: (i, (C_GL + n * D_MODEL) // tn + j), n=n)) for n in range(3)]
    return [br, br, br, w, w, w, *gl]


def _merge_fwd(branches, wb, proj):
    T = proj.shape[0]
    tm, tn = _pick(T, (1024, 512, 256, 128)), 512

    def body(a_ref, b_ref, c_ref, w0, w1, w2, g0, g1, g2, o_ref):
        acc = jnp.zeros((tm, tn), f32)
        for x_ref, w_ref, g_ref in ((a_ref, w0, g0), (b_ref, w1, g1), (c_ref, w2, g2)):
            acc = acc + jax.nn.sigmoid(g_ref[...]) * _raw_dot(x_ref[...], w_ref[...], "nn")
        o_ref[...] = acc.astype(bf16)

    return _call(body, "merge_fwd", (T // tm, D_MODEL // tn), _merge_specs(tm, tn), pl.BlockSpec((tm, tn), lambda i, j: (i, j)),
                 jax.ShapeDtypeStruct((T, D_MODEL), bf16), sem=("parallel", "parallel"))(*branches, *wb, proj, proj, proj)


def _merge_bwd(branches, wb, proj, dmerged):
    T = proj.shape[0]
    tm, tn = _pick(T, (1024, 512, 256, 128)), 512

    def body(a_ref, b_ref, c_ref, w0, w1, w2, g0, g1, g2, dm_ref, dgl_ref, d0, d1, d2):
        dm = dm_ref[...]
        for n, (x_ref, w_ref, g_ref, d_ref) in enumerate(((a_ref, w0, g0, d0), (b_ref, w1, g1, d1), (c_ref, w2, g2, d2))):
            up = _raw_dot(x_ref[...], w_ref[...], "nn")
            logits = g_ref[...]
            dgl_ref[n] = _egrad(jax.nn.sigmoid, logits, dm * up).astype(bf16)
            d_ref[...] = (dm * jax.nn.sigmoid(logits)).astype(bf16)

    blk = pl.BlockSpec((tm, tn), lambda i, j: (i, j))
    sh = jax.ShapeDtypeStruct((T, D_MODEL), bf16)
    outs = _call(body, "merge_bwd", (T // tm, D_MODEL // tn), [*_merge_specs(tm, tn), blk],
                 (pl.BlockSpec((3, tm, tn), lambda i, j: (0, i, j)), blk, blk, blk),
                 (jax.ShapeDtypeStruct((3, T, D_MODEL), bf16), sh, sh, sh),
                 sem=("parallel", "parallel"))(*branches, *wb, proj, proj, proj, dmerged)
    return outs[0], outs[1:]


CONV_TC = 256


def _shift_down(a, k):
    row = lax.broadcasted_iota(jnp.int32, a.shape, 0)
    return jnp.where(row >= k, pltpu.roll(a, k, 0), 0.0)


def _shift_up(a, k):
    n = a.shape[0]
    row = lax.broadcasted_iota(jnp.int32, a.shape, 0)
    return jnp.where(row < n - k, pltpu.roll(a, n - k, 0), 0.0)


def _conv_pre(a, cw, cb):
    return cb + cw[0:1] * _shift_down(a, 2) + cw[1:2] * _shift_down(a, 1) + cw[2:3] * a


def _conv_fwd(ab, cw, cb, Bl, S):
    nc = D_FF // CONV_TC

    def body(a_ref, b_ref, cw_ref, cb_ref, o_ref):
        ac = _conv_pre(a_ref[0], cw_ref[...], cb_ref[...])
        o_ref[0] = (jax.nn.silu(ac) * b_ref[0]).astype(bf16)

    return _call(body, "conv_fwd", (Bl, nc),
                 [pl.BlockSpec((1, S, CONV_TC), lambda b, c: (b, 0, c)), pl.BlockSpec((1, S, CONV_TC), lambda b, c: (b, 0, nc + c)),
                  pl.BlockSpec((3, CONV_TC), lambda b, c: (0, c)), pl.BlockSpec((1, CONV_TC), lambda b, c: (0, c))],
                 pl.BlockSpec((1, S, CONV_TC), lambda b, c: (b, 0, c)), jax.ShapeDtypeStruct((Bl, S, D_FF), bf16),
                 sem=("parallel", "parallel"))(ab, ab, cw, cb)


def _conv_bwd(ab, cw, cb, dact, Bl, S):
    nc = D_FF // CONV_TC

    def body(a_ref, b_ref, cw_ref, cb_ref, d_ref, da_ref, db_ref, dcw_ref, dcb_ref):
        @pl.when(pl.program_id(1) == 0)
        def _():
            dcw_ref[...] = jnp.zeros_like(dcw_ref)
            dcb_ref[...] = jnp.zeros_like(dcb_ref)

        a, cw = a_ref[0], cw_ref[...]
        ac = _conv_pre(a, cw, cb_ref[...])
        dact_ = d_ref[0].astype(f32)
        db_ref[0] = (dact_ * jax.nn.silu(ac)).astype(bf16)
        dac = _egrad(jax.nn.silu, ac, dact_ * b_ref[0])
        da_ref[0] = (cw[2:3] * dac + cw[1:2] * _shift_up(dac, 1) + cw[0:1] * _shift_up(dac, 2)).astype(bf16)
        dcw_ref[0:1, :] += jnp.sum(dac * _shift_down(a, 2), axis=0, keepdims=True)
        dcw_ref[1:2, :] += jnp.sum(dac * _shift_down(a, 1), axis=0, keepdims=True)
        dcw_ref[2:3, :] += jnp.sum(dac * a, axis=0, keepdims=True)
        dcb_ref[...] += jnp.sum(dac, axis=0, keepdims=True)

    seq = pl.BlockSpec((1, S, CONV_TC), lambda c, b: (b, 0, c))
    return _call(body, "conv_bwd", (nc, Bl),
                 [seq, pl.BlockSpec((1, S, CONV_TC), lambda c, b: (b, 0, nc + c)), pl.BlockSpec((3, CONV_TC), lambda c, b: (0, c)),
                  pl.BlockSpec((1, CONV_TC), lambda c, b: (0, c)), seq],
                 (seq, seq, pl.BlockSpec((3, CONV_TC), lambda c, b: (0, c)), pl.BlockSpec((1, CONV_TC), lambda c, b: (0, c))),
                 (jax.ShapeDtypeStruct((Bl, S, D_FF), bf16), jax.ShapeDtypeStruct((Bl, S, D_FF), bf16),
                  jax.ShapeDtypeStruct((3, D_FF), f32), jax.ShapeDtypeStruct((1, D_FF), f32)),
                 sem=("arbitrary", "arbitrary"))(ab, ab, cw, cb, dact)


def _local_step(x, mem, target, p, w):
    Bl, S, Dd = x.shape
    T = Bl * S
    x2d, t2d, mem2d = x.reshape(T, Dd), target.reshape(T, Dd), mem.reshape(Bl * MEM_LEN, Dd)
    b_st = jnp.pad(p["b_spatial"].T, ((0, 0), (0, 128 - N_HEAD)))
    lbl, wb = p["lb_logits"], w["w_branch"]

    h = _rms_fwd(x2d, p["norm1_g"], "norm1_fwd")
    proj = _mm(h, w["w_in"], "nn", f32, "proj_fwd", 1024, 1664)
    a_out = _gmlp_fwd(proj, p["ln_v_g"], p["ln_v_b"], p["w_spatial"], b_st)
    b_out, states = _hgrn_fwd(proj, lbl, p["hgrn_norm_g"], Bl, S)
    memn = _rms_fwd(mem2d, p["mem_norm_g"], "memnorm_fwd")
    kv = _mm(memn, w["w_mem_kv"], "nn", f32, "kv_fwd", 512, 1024).reshape(Bl, MEM_LEN, 2 * 512)
    c_out = _attn_fwd(proj, kv, Bl, S)
    branches = (a_out, b_out, c_out)
    merged = _merge_fwd(branches, wb, proj)
    x1 = _mm(merged, w["w_out"], "nn", f32, "out_fwd", 1024, 1024, residual=x2d)
    h2 = _rms_fwd(x1, p["norm2_g"], "norm2_fwd")
    ab = _mm(h2, w["w_up"], "nn", f32, "up_fwd", 1024, 1408)
    act = _conv_fwd(ab.reshape(Bl, S, 2 * D_FF), p["conv_w"], p["conv_b"], Bl, S).reshape(T, D_FF)
    x2 = _mm(act, w["w_down"], "nn", f32, "down_fwd", 512, 1024, residual=x1)
    loss_part, dx2, g_final = _final_loss(x2, p["final_g"], t2d)

    g_w_down = _mm(act, dx2, "tn", bf16, "down_dw", 1408, 1024, 1024)
    dact = _mm(dx2, w["w_down"], "nt", bf16, "down_dx", 1024, 1408)
    da, db, g_conv_w, g_conv_b = _conv_bwd(ab.reshape(Bl, S, 2 * D_FF), p["conv_w"], p["conv_b"], dact.reshape(Bl, S, D_FF), Bl, S)
    dab = jnp.concatenate([da.reshape(T, D_FF), db.reshape(T, D_FF)], axis=-1)
    g_w_up = _mm(h2, dab, "tn", bf16, "up_dw", 512, 1408, 1024)
    dh2 = _mm(dab, w["w_up"], "nt", f32, "up_dx", 1024, 1024, 1408)
    dx1, g_norm2 = _rms_bwd(x1, p["norm2_g"], dh2, "norm2_bwd", residual=dx2)

    g_w_out = _mm(merged, dx1, "tn", bf16, "out_dw", 1024, 1024, 1024)
    dmerged = _mm(dx1, w["w_out"], "nt", f32, "out_dx", 1024, 1024)
    dgl, dup = _merge_bwd(branches, wb, proj, dmerged)
    g_w_branch = [_mm(branches[n], dup[n], "tn", bf16, f"branch_dw{n}", 512, 1024, 1024) for n in range(3)]
    dbr = [_mm(dup[n], wb[n], "nt", bf16, f"branch_dx{n}", 1024, 512) for n in range(3)]
    dzuv, g_ln_g, g_ln_b, g_w_sp, g_b_sp = _gmlp_bwd(proj, p["ln_v_g"], p["ln_v_b"], p["w_spatial"], b_st, dbr[0])
    dq, df, di, dg, g_lbl, g_ng = _hgrn_bwd(proj, lbl, p["hgrn_norm_g"], states, dbr[1], Bl, S)
    dxq, dkv = _attn_bwd(proj, kv, dbr[2], Bl, S)
    dkv = dkv.reshape(Bl * MEM_LEN, 2 * 512)
    g_w_kv = _mm(memn, dkv, "tn", bf16, "kv_dw", 1024, 1024, 512)
    dmemn = _mm(dkv, w["w_mem_kv"], "nt", f32, "kv_dx", 512, 1024)
    _, g_mem_norm = _rms_bwd(mem2d, p["mem_norm_g"], dmemn, "memnorm_bwd")
    dproj = jnp.concatenate([dzuv, dq, df, di, dg, dxq, dgl[0], dgl[1], dgl[2]], axis=-1)
    g_w_in = _mm(h, dproj, "tn", bf16, "proj_dw", 512, 1664, 1024)
    dh = _mm(dproj, w["w_in"], "nt", f32, "proj_dx", 1024, 1024, 1664)
    dx, g_norm1 = _rms_bwd(x2d, p["norm1_g"], dh, "norm1_bwd", residual=dx1)

    gw = dict(w_in=g_w_in, w_mem_kv=g_w_kv, w_branch=g_w_branch, w_out=g_w_out, w_up=g_w_up, conv_w=g_conv_w, w_down=g_w_down)
    gs = dict(w_spatial=g_w_sp, norm1_g=g_norm1, mem_norm_g=g_mem_norm, norm2_g=g_norm2, final_g=g_final, lb_logits=g_lbl,
              ln_v_g=g_ln_g, ln_v_b=g_ln_b, b_spatial=g_b_sp, hgrn_norm_g=g_ng, conv_b=g_conv_b)
    return loss_part, dx.reshape(Bl, S, Dd), gw, gs


def _coords():
    return lax.axis_index("x"), lax.axis_index("y"), lax.axis_index("c")


def _slot(dev):
    return 4 * dev[0] + 2 * dev[1] + dev[2]


def _comm_call(body, name, arrays, out_shapes, n_sem):
    n = len(arrays)
    hbm = pl.BlockSpec(memory_space=pl.ANY)
    return pl.pallas_call(
        body, name=name, out_shape=out_shapes, in_specs=[hbm] * n, out_specs=[hbm] * n,
        scratch_shapes=[pltpu.SemaphoreType.DMA((n_sem, n)), pltpu.SemaphoreType.DMA((n_sem, n)), pltpu.SemaphoreType.DMA((n,))])(*arrays)


def _all_gather(blocks, name):
    n = len(blocks)

    def body(*refs):
        x_refs, o_refs, (send_sems, recv_sems, local_sems) = refs[:n], refs[n:2 * n], refs[2 * n:]
        x, y, c = _coords()
        me, sibling = (x, y, c), (x, y, 1 - c)
        chips = [(1 - x, y), (x, 1 - y), (1 - x, 1 - y)]

        def copy(a, k, block_dev, to, from_input=False):
            dst = o_refs[a].at[_slot(block_dev)]
            return pltpu.make_async_remote_copy(src_ref=x_refs[a] if from_input else dst, dst_ref=dst, send_sem=send_sems.at[k, a],
                                                recv_sem=recv_sems.at[k, a], device_id=to, device_id_type=MESH)

        mine = [pltpu.make_async_copy(x_refs[a], o_refs[a].at[_slot(me)], local_sems.at[a]) for a in range(n)]
        first = [copy(a, 0, me, sibling, True) for a in range(n)]
        first += [copy(a, 1 + j, me, (*chip, c), True) for j, chip in enumerate(chips) for a in range(n)]
        for cp in mine + first:
            cp.start()
        passed = []
        for j, chip in enumerate(chips):
            for a in range(n):
                copy(a, 1 + j, (*chip, c), me).wait_recv()
                fwd = copy(a, 4 + j, (*chip, c), sibling)
                fwd.start()
                passed.append(fwd)
        for a in range(n):
            copy(a, 0, sibling, me).wait_recv()
        for j, chip in enumerate(chips):
            for a in range(n):
                copy(a, 4 + j, (*chip, 1 - c), me).wait_recv()
        for cp in first + passed:
            cp.wait_send()
        for cp in mine:
            cp.wait()

    return _comm_call(body, name, blocks, [jax.ShapeDtypeStruct((N_DEV,) + b.shape, b.dtype) for b in blocks], 7)


def _all_to_all(parts, name):
    n = len(parts)
    rel = [(0, 0, 1), (0, 1, 0), (0, 1, 1), (1, 0, 0), (1, 0, 1), (1, 1, 0), (1, 1, 1)]

    def body(*refs):
        x_refs, o_refs, (send_sems, recv_sems, local_sems) = refs[:n], refs[n:2 * n], refs[2 * n:]
        x, y, c = _coords()
        me = (x, y, c)
        peers = [(x ^ dx, y ^ dy, c ^ dc) for dx, dy, dc in rel]

        def copy(a, k, peer):
            return pltpu.make_async_remote_copy(src_ref=x_refs[a].at[_slot(peer)], dst_ref=o_refs[a].at[_slot(me)], send_sem=send_sems.at[k, a],
                                                recv_sem=recv_sems.at[k, a], device_id=peer, device_id_type=MESH)

        def arrival(a, k, peer):
            return pltpu.make_async_remote_copy(src_ref=x_refs[a].at[_slot(me)], dst_ref=o_refs[a].at[_slot(peer)], send_sem=send_sems.at[k, a],
                                                recv_sem=recv_sems.at[k, a], device_id=peer, device_id_type=MESH)

        mine = [pltpu.make_async_copy(x_refs[a].at[_slot(me)], o_refs[a].at[_slot(me)], local_sems.at[a]) for a in range(n)]
        sends = [copy(a, k, peer) for k, peer in enumerate(peers) for a in range(n)]
        for cp in mine + sends:
            cp.start()
        for k, peer in enumerate(peers):
            for a in range(n):
                arrival(a, k, peer).wait_recv()
        for cp in sends:
            cp.wait_send()
        for cp in mine:
            cp.wait()

    return _comm_call(body, name, parts, [jax.ShapeDtypeStruct(p.shape, p.dtype) for p in parts], 7)


def _adam_math(w, g, m, v):
    m_ = ADAM_B1 * m + (1.0 - ADAM_B1) * g
    v_ = ADAM_B2 * v + (1.0 - ADAM_B2) * jnp.square(g)
    m_hat = m_ / (1.0 - ADAM_B1 ** ADAM_STEP)
    v_hat = v_ / (1.0 - ADAM_B2 ** ADAM_STEP)
    return -ADAM_LR * (m_hat / (jnp.sqrt(v_hat) + ADAM_EPS) + ADAM_WD * w), m_, v_


def _reduce_adamw(parts, w, m, v, name):
    _, R, L = parts.shape
    tr = _pick(R, (256, 128, 64, 32, 16, 8))

    def body(p_ref, w_ref, m_ref, v_ref, g_ref, d_ref, nm_ref, nv_ref):
        g = p_ref[0].astype(f32)
        for i in range(1, N_DEV):
            g = g + p_ref[i].astype(f32)
        g_ref[...] = g
        d_ref[...], nm_ref[...], nv_ref[...] = _adam_math(w_ref[...], g, m_ref[...], v_ref[...])

    blk = pl.BlockSpec((tr, L), lambda i: (i, 0))
    sh = jax.ShapeDtypeStruct((R, L), f32)
    return _call(body, name, (R // tr,), [pl.BlockSpec((N_DEV, tr, L), lambda i: (0, i, 0)), blk, blk, blk], (blk,) * 4, (sh,) * 4,
                 sem=("parallel",))(parts, w, m, v)


SMALL = (("w_spatial", (512, 128), 0), ("norm1_g", (1, 1024), 512), ("mem_norm_g", (1, 1024), 520), ("norm2_g", (1, 1024), 528),
         ("final_g", (1, 1024), 536), ("lb_logits", (2, 512), 544), ("ln_v_g", (1, 512), 552), ("ln_v_b", (1, 512), 556),
         ("b_spatial", (4, 128), 560), ("hgrn_norm_g", (1, 128), 564), ("conv_b", (1, 2816), 565))
SMALL_USED, SMALL_ROWS = 587, 640


def _segments(shape, base):
    r, n = shape
    per = n // 128
    return [(base + i * per + j, i, slice(j * 128, (j + 1) * 128)) for i in range(r) for j in range(per)]


def _pack_small(gs):
    names = [n for n, _, _ in SMALL]

    def body(*refs):
        src, o_ref = dict(zip(names, refs[:-1])), refs[-1]
        o_ref[SMALL_USED:SMALL_ROWS, :] = jnp.zeros((SMALL_ROWS - SMALL_USED, 128), f32)
        for name, shape, base in SMALL:
            ref = src[name]
            if name == "w_spatial":
                o_ref[base:base + 512, :] = ref[...].reshape(512, 128)
            elif name == "b_spatial":
                o_ref[base:base + 4, :] = ref[0:4, :]
            elif name == "hgrn_norm_g":
                o_ref[base:base + 1, :] = ref[0] + ref[1] + ref[2] + ref[3]
            else:
                for row, i, sl in _segments(shape, base):
                    o_ref[row:row + 1, :] = ref[i:i + 1, sl]

    return pl.pallas_call(body, name="pack_small", out_shape=jax.ShapeDtypeStruct((SMALL_ROWS, 128), f32))(*[gs[n] for n in names])


def _small_update(gathered, w, m, v):
    names = [n for n, _, _ in SMALL]
    k = len(names)

    def body(*refs):
        p_ref = refs[0]
        ins = [dict(zip(names, refs[1 + i * k:1 + (i + 1) * k])) for i in range(3)]
        outs = [dict(zip(names, refs[1 + (3 + i) * k:1 + (4 + i) * k])) for i in range(4)]
        gsum = refs[-1]
        g = p_ref[0]
        for i in range(1, N_DEV):
            g = g + p_ref[i]
        gsum[...] = g
        for name, shape, base in SMALL:
            if name == "w_spatial":
                where = [(slice(base, base + 512), (slice(None), slice(None)))]
            else:
                where = [(slice(row, row + 1), (slice(i, i + 1), sl)) for row, i, sl in _segments(shape, base)]
            for rows, at in where:
                g_ = gsum[rows, :]
                d_, m_, v_ = _adam_math(ins[0][name][at], g_, ins[1][name][at], ins[2][name][at])
                for o, val in zip(outs, (g_, d_, m_, v_)):
                    o[name][at] = val

    args = [gathered] + [d[n] for d in (w, m, v) for n in names]
    out_shapes = [jax.ShapeDtypeStruct(shape, f32) for _ in range(4) for _, shape, _ in SMALL]
    outs = pl.pallas_call(body, name="small_update", out_shape=out_shapes, scratch_shapes=[pltpu.VMEM((SMALL_ROWS, 128), f32)])(*args)
    return [dict(zip(names, outs[i * k:(i + 1) * k])) for i in range(4)]


def _cols_full(g):
    return jnp.moveaxis(g, 0, -2).reshape(g.shape[1:-1] + (N_DEV * g.shape[-1],))


def _cols_parts(full):
    n = full.shape[-1] // N_DEV
    return jnp.moveaxis(full.reshape(full.shape[:-1] + (N_DEV, n)), -2, 0)


def kernel(x, mem, norm1_g, w_in, ln_v_g, ln_v_b, w_spatial, b_spatial, lb_logits, hgrn_norm_g, mem_norm_g, w_mem_kv, w_branch, w_out, norm2_g, w_up, conv_w, conv_b, w_down, final_g, loss_target, m_norm1_g, m_w_in, m_ln_v_g, m_ln_v_b, m_w_spatial, m_b_spatial, m_lb_logits, m_hgrn_norm_g, m_mem_norm_g, m_w_mem_kv, m_w_branch, m_w_out, m_norm2_g, m_w_up, m_conv_w, m_conv_b, m_w_down, m_final_g, v_norm1_g, v_w_in, v_ln_v_g, v_ln_v_b, v_w_spatial, v_b_spatial, v_lb_logits, v_hgrn_norm_g, v_mem_norm_g, v_w_mem_kv, v_w_branch, v_w_out, v_norm2_g, v_w_up, v_conv_w, v_conv_b, v_w_down, v_final_g):
    given = dict(locals())
    order = ("norm1_g", "w_in", "ln_v_g", "ln_v_b", "w_spatial", "b_spatial", "lb_logits", "hgrn_norm_g", "mem_norm_g",
             "w_mem_kv", "w_branch", "w_out", "norm2_g", "w_up", "conv_w", "conv_b", "w_down", "final_g")
    big = ("w_in", "w_mem_kv", "w_branch", "w_out", "w_up", "conv_w", "w_down")

    wire = [given[n][0].astype(f32 if n == "conv_w" else bf16) for n in big]
    g = dict(zip(big, _all_gather(wire, "gather_weights")))
    wb_full = _cols_full(g["w_branch"])
    w = dict(w_in=_cols_full(g["w_in"]), w_mem_kv=g["w_mem_kv"].reshape(D_MODEL, 2 * 512), w_branch=[wb_full[n] for n in range(3)],
             w_out=g["w_out"].reshape(D_MODEL, D_MODEL), w_up=_cols_full(g["w_up"]), w_down=g["w_down"].reshape(D_FF, D_MODEL))

    small_2d = lambda prefix: {n: given[prefix + n].reshape(shape) for n, shape, _ in SMALL}
    p = small_2d("")
    p["w_spatial"] = w_spatial[0]
    p["conv_w"] = _cols_full(g["conv_w"])

    loss_part, grad_x, gw, gs = _local_step(x, mem, loss_target, p, w)
    loss = lax.psum(loss_part[0, 0], ("x", "y", "c"))

    shard2d = {n: (-1, given[n].shape[-1]) for n in big}
    parts = dict(w_in=_cols_parts(gw["w_in"]), w_mem_kv=gw["w_mem_kv"].reshape(N_DEV, -1, 2 * 512),
                 w_branch=_cols_parts(jnp.stack(gw["w_branch"])).reshape(N_DEV, -1, 128), w_out=gw["w_out"].reshape(N_DEV, -1, D_MODEL),
                 w_up=_cols_parts(gw["w_up"]), conv_w=_cols_parts(gw["conv_w"]), w_down=gw["w_down"].reshape(N_DEV, -1, D_MODEL))
    recv = dict(zip(big, _all_to_all([parts[n] for n in big], "scatter_grads")))
    grads, delta, new_m, new_v = {}, {}, {}, {}
    for n in big:
        res = _reduce_adamw(recv[n], *[given[pre + n].reshape(shard2d[n]) for pre in ("", "m_", "v_")], "adamw_" + n)
        grads[n], delta[n], new_m[n], new_v[n] = [r.reshape(given[n].shape) for r in res]

    gathered = _all_gather([_pack_small(gs)], "gather_small_grads")[0]
    for dst, res in zip((grads, delta, new_m, new_v), _small_update(gathered, small_2d(""), small_2d("m_"), small_2d("v_"))):
        for n, _, _ in SMALL:
            dst[n] = res[n].reshape(given[n].shape)

    return (loss, grad_x, *[grads[n] for n in order], *[delta[n] for n in order], *[new_m[n] for n in order],
            *[new_v[n] for n in order])
```

```python
import functools

import jax
import jax.numpy as jnp
from jax import lax
from jax.experimental import pallas as pl
from jax.experimental.pallas import tpu as pltpu

f32 = jnp.float32
bf16 = jnp.bfloat16

N_DEV = 8
D_MODEL = 1024
EPS = 1e-6
GM_CHUNK = 128
HG_CHUNK = 64
HEAD = 128
N_HEAD = 4
MEM_LEN = 256
D_FF = 2816
IN_WIDTH = 6656
C_ZU, C_HQ, C_HF, C_HI, C_HG, C_XQ, C_GL = 0, 1024, 1536, 2048, 2560, 3072, 3584
ADAM_LR, ADAM_B1, ADAM_B2, ADAM_EPS, ADAM_WD, ADAM_STEP = 0.001, 0.9, 0.999, 1e-08, 0.01, 10
VMEM_LIMIT = 56 * 1024 * 1024
MESH = pl.DeviceIdType.MESH
HIGHEST = lax.Precision.HIGHEST


def _pick(n, cands):
    for c in cands:
        if n % c == 0:
            return c
    return n


def _call(body, name, grid, in_specs, out_specs, out_shape, scratch=(), sem=None, **cp):
    params = dict(vmem_limit_bytes=VMEM_LIMIT, **cp)
    if sem is not None:
        params["dimension_semantics"] = sem
    return pl.pallas_call(
        body, name=name, grid=grid, in_specs=in_specs, out_specs=out_specs, out_shape=out_shape,
        scratch_shapes=list(scratch), compiler_params=pltpu.CompilerParams(**params))


_DN = {"nn": (((1,), (0,)), ((), ())), "nt": (((1,), (1,)), ((), ())), "tn": (((0,), (0,)), ((), ()))}


def _raw_dot(a, b, mode):
    return lax.dot_general(a.astype(bf16), b.astype(bf16), _DN[mode], preferred_element_type=f32)


@jax.custom_vjp
def _dot_nn(a, b):
    return _raw_dot(a, b, "nn")


_dot_nn.defvjp(lambda a, b: (_raw_dot(a, b, "nn"), (a, b)),
               lambda r, g: (_raw_dot(g, r[1], "nt"), _raw_dot(r[0], g, "tn")))


@jax.custom_vjp
def _dot_nt(a, b):
    return _raw_dot(a, b, "nt")


_dot_nt.defvjp(lambda a, b: (_raw_dot(a, b, "nt"), (a, b)),
               lambda r, g: (_raw_dot(g, r[1], "nn"), _raw_dot(g, r[0], "tn")))


@jax.custom_vjp
def _dot_tn(a, b):
    return _raw_dot(a, b, "tn")


_dot_tn.defvjp(lambda a, b: (_raw_dot(a, b, "tn"), (a, b)),
               lambda r, g: (_raw_dot(r[1], g, "nt"), _raw_dot(r[0], g, "nn")))


def _tri(n, lower):
    r = lax.broadcasted_iota(jnp.int32, (n, n), 0)
    c = lax.broadcasted_iota(jnp.int32, (n, n), 1)
    return ((c <= r) if lower else (c >= r)).astype(f32)


def _tri_dot(x, lower):
    return jnp.dot(_tri(x.shape[0], lower), x, preferred_element_type=f32, precision=HIGHEST)


@jax.custom_vjp
def _cumsum_rows(x):
    return _tri_dot(x, True)


_cumsum_rows.defvjp(lambda x: (_tri_dot(x, True), None), lambda _, g: (_tri_dot(g, False),))


def _egrad(fn, x, ct):
    return jax.vjp(fn, x)[1](ct)[0]


def _mm(a, b, mode, out_dtype, name, tm, tn, tk=None, residual=None):
    if mode == "nn":
        (M, K), (_, N) = a.shape, b.shape
    elif mode == "nt":
        (M, K), (N, _) = a.shape, b.shape
    else:
        (K, M), (_, N) = a.shape, b.shape
    tm, tn = min(tm, M), min(tn, N)
    tk = K if tk is None else min(tk, K)
    assert M % tm == 0 and N % tn == 0 and K % tk == 0, (name, M, N, K, tm, tn, tk)
    nk = K // tk

    def body(*refs):
        acc_ref = refs[-1] if nk > 1 else None
        refs = refs[:-1] if nk > 1 else refs
        if residual is None:
            a_ref, b_ref, o_ref = refs
        else:
            a_ref, b_ref, r_ref, o_ref = refs

        def finish(r):
            if residual is not None:
                r = r + r_ref[...]
            o_ref[...] = r.astype(out_dtype)

        part = _raw_dot(a_ref[...], b_ref[...], mode)
        if nk == 1:
            finish(part)
            return
        k = pl.program_id(2)

        @pl.when(k == 0)
        def _():
            acc_ref[...] = part

        @pl.when((k > 0) & (k < nk - 1))
        def _():
            acc_ref[...] += part

        @pl.when(k == nk - 1)
        def _():
            finish(acc_ref[...] + part)

    a_spec = {"nn": pl.BlockSpec((tm, tk), lambda i, j, k: (i, k)),
              "nt": pl.BlockSpec((tm, tk), lambda i, j, k: (i, k)),
              "tn": pl.BlockSpec((tk, tm), lambda i, j, k: (k, i))}[mode]
    b_spec = {"nn": pl.BlockSpec((tk, tn), lambda i, j, k: (k, j)),
              "nt": pl.BlockSpec((tn, tk), lambda i, j, k: (j, k)),
              "tn": pl.BlockSpec((tk, tn), lambda i, j, k: (k, j))}[mode]
    o_spec = pl.BlockSpec((tm, tn), lambda i, j, k: (i, j))
    in_specs = [a_spec, b_spec] + ([o_spec] if residual is not None else [])
    args = (a, b) + ((residual,) if residual is not None else ())
    return _call(body, name, (M // tm, N // tn, nk), in_specs, o_spec, jax.ShapeDtypeStruct((M, N), out_dtype),
                 scratch=[pltpu.VMEM((tm, tn), f32)] if nk > 1 else [], sem=("parallel", "parallel", "arbitrary"))(*args)


def _rms_fwd(x, g, name):
    R, Dd = x.shape
    tr = _pick(R, (512, 256, 128))

    def body(x_ref, g_ref, o_ref):
        xf = x_ref[...]
        y = xf * lax.rsqrt(jnp.mean(xf * xf, axis=-1, keepdims=True) + EPS)
        o_ref[...] = (y * g_ref[...]).astype(bf16)

    return _call(body, name, (R // tr,), [pl.BlockSpec((tr, Dd), lambda i: (i, 0)), pl.BlockSpec((1, Dd), lambda i: (0, 0))],
                 pl.BlockSpec((tr, Dd), lambda i: (i, 0)), jax.ShapeDtypeStruct((R, Dd), bf16), sem=("parallel",))(x, g)


def _rms_bwd(x, g, dh, name, residual=None):
    R, Dd = x.shape
    tr = _pick(R, (512, 256, 128))

    def body(*refs):
        if residual is None:
            x_ref, g_ref, dh_ref, dx_ref, dg_ref = refs
        else:
            x_ref, g_ref, dh_ref, r_ref, dx_ref, dg_ref = refs
        xf = x_ref[...]
        rs = lax.rsqrt(jnp.mean(xf * xf, axis=-1, keepdims=True) + EPS)
        y = xf * rs
        dh_ = dh_ref[...].astype(f32)
        dy = dh_ * g_ref[...]
        dx = rs * (dy - y * jnp.mean(dy * y, axis=-1, keepdims=True))
        if residual is not None:
            dx = dx + r_ref[...]
        dx_ref[...] = dx

        @pl.when(pl.program_id(0) == 0)
        def _():
            dg_ref[...] = jnp.zeros_like(dg_ref)

        dg_ref[...] += jnp.sum(dh_ * y, axis=0, keepdims=True)

    row = pl.BlockSpec((tr, Dd), lambda i: (i, 0))
    vec = pl.BlockSpec((1, Dd), lambda i: (0, 0))
    in_specs = [row, vec, row] + ([row] if residual is not None else [])
    args = (x, g, dh) + ((residual,) if residual is not None else ())
    return _call(body, name, (R // tr,), in_specs, (row, vec),
                 (jax.ShapeDtypeStruct((R, Dd), f32), jax.ShapeDtypeStruct((1, Dd), f32)), sem=("arbitrary",))(*args)


def _final_loss(x2, g, target):
    R, Dd = x2.shape
    tr = _pick(R, (512, 256, 128))

    def body(x_ref, g_ref, t_ref, loss_ref, dx_ref, dg_ref):
        xf = x_ref[...]
        rs = lax.rsqrt(jnp.mean(xf * xf, axis=-1, keepdims=True) + EPS)
        y = xf * rs
        err = y * g_ref[...] - t_ref[...]
        dh_ = err * (1.0 / Dd)
        dy = dh_ * g_ref[...]
        dx_ref[...] = rs * (dy - y * jnp.mean(dy * y, axis=-1, keepdims=True))

        @pl.when(pl.program_id(0) == 0)
        def _():
            dg_ref[...] = jnp.zeros_like(dg_ref)
            loss_ref[...] = jnp.zeros_like(loss_ref)

        dg_ref[...] += jnp.sum(dh_ * y, axis=0, keepdims=True)
        part = jnp.sum(jnp.mean(err * err, axis=-1, keepdims=True), axis=0, keepdims=True)
        loss_ref[...] += 0.5 * part

    row = pl.BlockSpec((tr, Dd), lambda i: (i, 0))
    vec = pl.BlockSpec((1, Dd), lambda i: (0, 0))
    return _call(body, "final_loss", (R // tr,), [row, vec, row], (pl.BlockSpec((1, 128), lambda i: (0, 0)), row, vec),
                 (jax.ShapeDtypeStruct((1, 128), f32), jax.ShapeDtypeStruct((R, Dd), f32), jax.ShapeDtypeStruct((1, Dd), f32)),
                 sem=("arbitrary",))(x2, g, target)


def _gmlp_parts(zuv, ln_g, ln_b):
    zu, zv = zuv[:, :512], zuv[:, 512:]
    u = jax.nn.gelu(zu)
    v = jax.nn.gelu(zv)
    mu = jnp.mean(v, axis=-1, keepdims=True)
    rs = lax.rsqrt(jnp.mean(jnp.square(v - mu), axis=-1, keepdims=True) + EPS)
    xh = (v - mu) * rs
    return zu, zv, u, xh, rs, xh * ln_g + ln_b


def _gmlp_fwd(proj, ln_g, ln_b, w_s, b_st):
    T = proj.shape[0]

    def body(p_ref, g_ref, b_ref, w_ref, bs_ref, o_ref):
        _, _, u, _, _, vn = _gmlp_parts(p_ref[...], g_ref[...], b_ref[...])
        causal = _tri(GM_CHUNK, True) > 0
        for gi in range(N_HEAD):
            sl = slice(gi * HEAD, (gi + 1) * HEAD)
            w = jnp.where(causal, w_ref[gi], 0.0)
            mixed = _raw_dot(w, vn[:, sl], "nn") + bs_ref[:, gi:gi + 1]
            o_ref[:, sl] = (u[:, sl] * mixed).astype(bf16)

    vec = pl.BlockSpec((1, 512), lambda i: (0, 0))
    return _call(body, "gmlp_fwd", (T // GM_CHUNK,),
                 [pl.BlockSpec((GM_CHUNK, 1024), lambda i: (i, 0)), vec, vec,
                  pl.BlockSpec((N_HEAD, GM_CHUNK, GM_CHUNK), lambda i: (0, 0, 0)), pl.BlockSpec((GM_CHUNK, 128), lambda i: (0, 0))],
                 pl.BlockSpec((GM_CHUNK, 512), lambda i: (i, 0)), jax.ShapeDtypeStruct((T, 512), bf16), sem=("parallel",))(
        proj, ln_g, ln_b, w_s, b_st)


def _gmlp_bwd(proj, ln_g, ln_b, w_s, b_st, da):
    T = proj.shape[0]

    def body(p_ref, g_ref, b_ref, w_ref, bs_ref, da_ref, dp_ref, dg_ref, db_ref, dw_ref, dbs_ref):
        zu, zv, u, xh, rs, vn = _gmlp_parts(p_ref[...], g_ref[...], b_ref[...])
        causal = _tri(GM_CHUNK, True) > 0
        sub = lax.broadcasted_iota(jnp.int32, (8, GM_CHUNK), 0)
        ones = jnp.ones((8, HEAD), f32)
        dout = da_ref[...].astype(f32)

        @pl.when(pl.program_id(0) == 0)
        def _():
            for r in (dg_ref, db_ref, dw_ref, dbs_ref):
                r[...] = jnp.zeros_like(r)

        du, dvn, dbs = [], [], jnp.zeros((8, GM_CHUNK), f32)
        for gi in range(N_HEAD):
            sl = slice(gi * HEAD, (gi + 1) * HEAD)
            w = jnp.where(causal, w_ref[gi], 0.0)
            mixed = _raw_dot(w, vn[:, sl], "nn") + bs_ref[:, gi:gi + 1]
            du.append(dout[:, sl] * mixed)
            dm = dout[:, sl] * u[:, sl]
            row_sums = lax.dot_general(ones, dm, _DN["nt"], precision=HIGHEST, preferred_element_type=f32)
            dbs = dbs + jnp.where(sub == gi, row_sums, 0.0)
            dw_ref[gi] += jnp.where(causal, _raw_dot(dm, vn[:, sl], "nt"), 0.0)
            dvn.append(_raw_dot(w, dm, "tn"))
        dbs_ref[...] += dbs
        du = jnp.concatenate(du, axis=-1)
        dvn = jnp.concatenate(dvn, axis=-1)
        dg_ref[...] += jnp.sum(dvn * xh, axis=0, keepdims=True)
        db_ref[...] += jnp.sum(dvn, axis=0, keepdims=True)
        dxh = dvn * g_ref[...]
        dv = rs * (dxh - jnp.mean(dxh, axis=-1, keepdims=True) - xh * jnp.mean(dxh * xh, axis=-1, keepdims=True))
        dp_ref[:, :512] = _egrad(jax.nn.gelu, zu, du).astype(bf16)
        dp_ref[:, 512:] = _egrad(jax.nn.gelu, zv, dv).astype(bf16)

    vec = pl.BlockSpec((1, 512), lambda i: (0, 0))
    wsp = pl.BlockSpec((N_HEAD, GM_CHUNK, GM_CHUNK), lambda i: (0, 0, 0))
    return _call(body, "gmlp_bwd", (T // GM_CHUNK,),
                 [pl.BlockSpec((GM_CHUNK, 1024), lambda i: (i, 0)), vec, vec, wsp, pl.BlockSpec((GM_CHUNK, 128), lambda i: (0, 0)),
                  pl.BlockSpec((GM_CHUNK, 512), lambda i: (i, 0))],
                 (pl.BlockSpec((GM_CHUNK, 1024), lambda i: (i, 0)), vec, vec, wsp, pl.BlockSpec((8, GM_CHUNK), lambda i: (0, 0))),
                 (jax.ShapeDtypeStruct((T, 1024), bf16), jax.ShapeDtypeStruct((1, 512), f32), jax.ShapeDtypeStruct((1, 512), f32),
                  jax.ShapeDtypeStruct((N_HEAD, GM_CHUNK, GM_CHUNK), f32), jax.ShapeDtypeStruct((8, GM_CHUNK), f32)),
                 sem=("arbitrary",))(proj, ln_g, ln_b, w_s, b_st, da)


def _hgrn_chunk(st0, q_raw, f_raw, i_raw, g_raw, l0, l1, ng):
    C = HG_CHUNK
    lb = jax.nn.sigmoid(l0 - l1)
    fg = lb + (1.0 - lb) * jax.nn.sigmoid(f_raw)
    kk = 1.0 - fg
    qf = jax.nn.silu(q_raw)
    a = _cumsum_rows(jnp.log(fg))
    row = lax.broadcasted_iota(jnp.int32, (C, HEAD), 0)
    a_last = jnp.sum(jnp.where(row == C - 1, a, 0.0), axis=0, keepdims=True)
    inter = _dot_nt(qf * jnp.exp(a), st0)
    t_i = lax.broadcasted_iota(jnp.int32, (C, C, HEAD), 0)
    s_i = lax.broadcasted_iota(jnp.int32, (C, C, HEAD), 1)
    decay = jnp.exp(jnp.where(s_i <= t_i, a[:, None, :] - a[None, :, :], -jnp.inf))
    scores = jnp.sum(qf[:, None, :] * decay * kk[None, :, :], axis=-1)
    o = inter + _dot_nn(scores, i_raw)
    st1 = jnp.exp(a_last) * st0 + _dot_tn(i_raw, kk * jnp.exp(a_last - a))
    on = o * lax.rsqrt(jnp.mean(o * o, axis=-1, keepdims=True) + EPS) * ng
    return st1, on * jax.nn.silu(g_raw)


def _hgrn_specs(S, rev):
    N = S // HG_CHUNK

    def col(c0):
        if rev:
            return pl.BlockSpec((HG_CHUNK, HEAD), lambda h, b, n: (b * N + (N - 1 - n), c0 // HEAD + h))
        return pl.BlockSpec((HG_CHUNK, HEAD), lambda h, b, n: (b * N + n, c0 // HEAD + h))

    def st():
        if rev:
            return pl.BlockSpec((1, 1, 1, HEAD, HEAD), lambda h, b, n: (h, b, N - 1 - n, 0, 0))
        return pl.BlockSpec((1, 1, 1, HEAD, HEAD), lambda h, b, n: (h, b, n, 0, 0))

    return N, col, st


def _hgrn_fwd(proj, lb_logits, ng, Bl, S):
    N, col, st = _hgrn_specs(S, False)
    T = Bl * S

    def body(q_ref, f_ref, i_ref, g_ref, l_ref, ng_ref, o_ref, st_ref, state):
        @pl.when(pl.program_id(2) == 0)
        def _():
            state[...] = jnp.zeros_like(state)

        st0 = state[...]
        st_ref[0, 0, 0] = st0
        st1, out = _hgrn_chunk(st0, q_ref[...], f_ref[...], i_ref[...], g_ref[...], l_ref[0:1, :], l_ref[1:2, :], ng_ref[...])
        state[...] = st1
        o_ref[...] = out.astype(bf16)

    lsp = pl.BlockSpec((2, HEAD), lambda h, b, n: (0, h))
    return _call(body, "hgrn_fwd", (N_HEAD, Bl, N),
                 [col(C_HQ), col(C_HF), col(C_HI), col(C_HG), lsp, pl.BlockSpec((1, HEAD), lambda h, b, n: (0, 0))],
                 (col(0), st()),
                 (jax.ShapeDtypeStruct((T, 512), bf16), jax.ShapeDtypeStruct((N_HEAD, Bl, N, HEAD, HEAD), f32)),
                 scratch=[pltpu.VMEM((HEAD, HEAD), f32)], sem=("arbitrary", "arbitrary", "arbitrary"))(
        proj, proj, proj, proj, lb_logits, ng)


def _hgrn_bwd(proj, lb_logits, ng, states, db, Bl, S):
    N, col, st = _hgrn_specs(S, True)
    T = Bl * S

    def body(q_ref, f_ref, i_ref, g_ref, l_ref, ng_ref, st_ref, db_ref,
             dq_ref, df_ref, di_ref, dg_ref, dl_ref, dng_ref, dstate):
        @pl.when(pl.program_id(2) == 0)
        def _():
            dstate[...] = jnp.zeros_like(dstate)

        @pl.when((pl.program_id(1) == 0) & (pl.program_id(2) == 0))
        def _():
            dl_ref[...] = jnp.zeros_like(dl_ref)
            dng_ref[...] = jnp.zeros_like(dng_ref)

        _, vjp = jax.vjp(_hgrn_chunk, st_ref[0, 0, 0], q_ref[...], f_ref[...], i_ref[...], g_ref[...],
                         l_ref[0:1, :], l_ref[1:2, :], ng_ref[...])
        dst0, dq, df, di, dg, dl0, dl1, dng = vjp((dstate[...], db_ref[...].astype(f32)))
        dstate[...] = dst0
        dq_ref[...] = dq.astype(bf16)
        df_ref[...] = df.astype(bf16)
        di_ref[...] = di.astype(bf16)
        dg_ref[...] = dg.astype(bf16)
        dl_ref[0:1, :] += dl0
        dl_ref[1:2, :] += dl1
        dng_ref[0] += dng

    lsp = pl.BlockSpec((2, HEAD), lambda h, b, n: (0, h))
    dsp = [pl.BlockSpec((HG_CHUNK, HEAD), lambda h, b, n: (b * N + (N - 1 - n), h))] * 4
    return _call(body, "hgrn_bwd", (N_HEAD, Bl, N),
                 [col(C_HQ), col(C_HF), col(C_HI), col(C_HG), lsp, pl.BlockSpec((1, HEAD), lambda h, b, n: (0, 0)), st(), col(0)],
                 (*dsp, lsp, pl.BlockSpec((1, 1, HEAD), lambda h, b, n: (h, 0, 0))),
                 (*[jax.ShapeDtypeStruct((T, 512), bf16)] * 4, jax.ShapeDtypeStruct((2, 512), f32),
                  jax.ShapeDtypeStruct((N_HEAD, 1, HEAD), f32)),
                 scratch=[pltpu.VMEM((HEAD, HEAD), f32)], sem=("arbitrary", "arbitrary", "arbitrary"))(
        proj, proj, proj, proj, lb_logits, ng, states, db)


def _attn_probs(q, k):
    s = _raw_dot(q, k, "nt") * (HEAD ** -0.5)
    e = jnp.exp(s - jnp.max(s, axis=-1, keepdims=True))
    return e / jnp.sum(e, axis=-1, keepdims=True)


def _attn_specs(S, tq):
    nq = S // tq
    q = pl.BlockSpec((tq, 512), lambda b, i: (b * nq + i, C_XQ // 512))
    kv = pl.BlockSpec((1, MEM_LEN, 1024), lambda b, i: (b, 0, 0))
    o = pl.BlockSpec((tq, 512), lambda b, i: (b * nq + i, 0))
    return nq, q, kv, o


def _attn_fwd(proj, kv, Bl, S):
    tq = _pick(S, (512, 256, 128))
    nq, qs, kvs, os_ = _attn_specs(S, tq)

    def body(q_ref, kv_ref, o_ref):
        for h in range(N_HEAD):
            sl = slice(h * HEAD, (h + 1) * HEAD)
            p = _attn_probs(q_ref[:, sl], kv_ref[0, :, sl])
            o_ref[:, sl] = _raw_dot(p, kv_ref[0, :, 512 + h * HEAD:512 + (h + 1) * HEAD], "nn").astype(bf16)

    return _call(body, "attn_fwd", (Bl, nq), [qs, kvs], os_, jax.ShapeDtypeStruct((Bl * S, 512), bf16),
                 sem=("parallel", "parallel"))(proj, kv)


def _attn_bwd(proj, kv, dc, Bl, S):
    tq = _pick(S, (512, 256, 128))
    nq, qs, kvs, os_ = _attn_specs(S, tq)

    def body(q_ref, kv_ref, do_ref, dq_ref, dkv_ref):
        @pl.when(pl.program_id(1) == 0)
        def _():
            dkv_ref[...] = jnp.zeros_like(dkv_ref)

        for h in range(N_HEAD):
            sl = slice(h * HEAD, (h + 1) * HEAD)
            vsl = slice(512 + h * HEAD, 512 + (h + 1) * HEAD)
            q, k, v, do = q_ref[:, sl], kv_ref[0, :, sl], kv_ref[0, :, vsl], do_ref[:, sl]
            p = _attn_probs(q, k)
            dkv_ref[0, :, vsl] += _raw_dot(p, do, "tn")
            dp = _raw_dot(do, v, "nt")
            ds = p * (dp - jnp.sum(dp * p, axis=-1, keepdims=True)) * (HEAD ** -0.5)
            dq_ref[:, sl] = _raw_dot(ds, k, "nn").astype(bf16)
            dkv_ref[0, :, sl] += _raw_dot(ds, q, "tn")

    return _call(body, "attn_bwd", (Bl, nq), [qs, kvs, os_], (os_, kvs),
                 (jax.ShapeDtypeStruct((Bl * S, 512), bf16), jax.ShapeDtypeStruct((Bl, MEM_LEN, 1024), f32)),
                 sem=("arbitrary", "arbitrary"))(proj, kv, dc)


def _merge_specs(tm, tn):
    br = pl.BlockSpec((tm, 512), lambda i, j: (i, 0))
    w = pl.BlockSpec((512, tn), lambda i, j: (0, j))
    gl = [pl.BlockSpec((tm, tn), functools.partial(lambda i, j, n: (i, (C_GL + n * D_MODEL) // tn + j), n=n)) for n in range(3)]
    return [br, br, br, w, w, w, *gl]


def _merge_fwd(branches, wb, proj):
    T = proj.shape[0]
    tm, tn = _pick(T, (1024, 512, 256, 128)), 512

    def body(a_ref, b_ref, c_ref, w0, w1, w2, g0, g1, g2, o_ref):
        acc = jnp.zeros((tm, tn), f32)
        for x_ref, w_ref, g_ref in ((a_ref, w0, g0), (b_ref, w1, g1), (c_ref, w2, g2)):
            acc = acc + jax.nn.sigmoid(g_ref[...]) * _raw_dot(x_ref[...], w_ref[...], "nn")
        o_ref[...] = acc.astype(bf16)

    return _call(body, "merge_fwd", (T // tm, D_MODEL // tn), _merge_specs(tm, tn), pl.BlockSpec((tm, tn), lambda i, j: (i, j)),
                 jax.ShapeDtypeStruct((T, D_MODEL), bf16), sem=("parallel", "parallel"))(*branches, *wb, proj, proj, proj)


def _merge_bwd(branches, wb, proj, dmerged):
    T = proj.shape[0]
    tm, tn = _pick(T, (1024, 512, 256, 128)), 512

    def body(a_ref, b_ref, c_ref, w0, w1, w2, g0, g1, g2, dm_ref, dgl_ref, d0, d1, d2):
        dm = dm_ref[...]
        for n, (x_ref, w_ref, g_ref, d_ref) in enumerate(((a_ref, w0, g0, d0), (b_ref, w1, g1, d1), (c_ref, w2, g2, d2))):
            up = _raw_dot(x_ref[...], w_ref[...], "nn")
            logits = g_ref[...]
            dgl_ref[n] = _egrad(jax.nn.sigmoid, logits, dm * up).astype(bf16)
            d_ref[...] = (dm * jax.nn.sigmoid(logits)).astype(bf16)

    blk = pl.BlockSpec((tm, tn), lambda i, j: (i, j))
    sh = jax.ShapeDtypeStruct((T, D_MODEL), bf16)
    outs = _call(body, "merge_bwd", (T // tm, D_MODEL // tn), [*_merge_specs(tm, tn), blk],
                 (pl.BlockSpec((3, tm, tn), lambda i, j: (0, i, j)), blk, blk, blk),
                 (jax.ShapeDtypeStruct((3, T, D_MODEL), bf16), sh, sh, sh),
                 sem=("parallel", "parallel"))(*branches, *wb, proj, proj, proj, dmerged)
    return outs[0], outs[1:]


CONV_TC = 256


def _shift_down(a, k):
    row = lax.broadcasted_iota(jnp.int32, a.shape, 0)
    return jnp.where(row >= k, pltpu.roll(a, k, 0), 0.0)


def _shift_up(a, k):
    n = a.shape[0]
    row = lax.broadcasted_iota(jnp.int32, a.shape, 0)
    return jnp.where(row < n - k, pltpu.roll(a, n - k, 0), 0.0)


def _conv_pre(a, cw, cb):
    return cb + cw[0:1] * _shift_down(a, 2) + cw[1:2] * _shift_down(a, 1) + cw[2:3] * a


def _conv_fwd(ab, cw, cb, Bl, S):
    nc = D_FF // CONV_TC

    def body(a_ref, b_ref, cw_ref, cb_ref, o_ref):
        ac = _conv_pre(a_ref[0], cw_ref[...], cb_ref[...])
        o_ref[0] = (jax.nn.silu(ac) * b_ref[0]).astype(bf16)

    return _call(body, "conv_fwd", (Bl, nc),
                 [pl.BlockSpec((1, S, CONV_TC), lambda b, c: (b, 0, c)), pl.BlockSpec((1, S, CONV_TC), lambda b, c: (b, 0, nc + c)),
                  pl.BlockSpec((3, CONV_TC), lambda b, c: (0, c)), pl.BlockSpec((1, CONV_TC), lambda b, c: (0, c))],
                 pl.BlockSpec((1, S, CONV_TC), lambda b, c: (b, 0, c)), jax.ShapeDtypeStruct((Bl, S, D_FF), bf16),
                 sem=("parallel", "parallel"))(ab, ab, cw, cb)


def _conv_bwd(ab, cw, cb, dact, Bl, S):
    nc = D_FF // CONV_TC

    def body(a_ref, b_ref, cw_ref, cb_ref, d_ref, da_ref, db_ref, dcw_ref, dcb_ref):
        @pl.when(pl.program_id(1) == 0)
        def _():
            dcw_ref[...] = jnp.zeros_like(dcw_ref)
            dcb_ref[...] = jnp.zeros_like(dcb_ref)

        a, cw = a_ref[0], cw_ref[...]
        ac = _conv_pre(a, cw, cb_ref[...])
        dact_ = d_ref[0].astype(f32)
        db_ref[0] = (dact_ * jax.nn.silu(ac)).astype(bf16)
        dac = _egrad(jax.nn.silu, ac, dact_ * b_ref[0])
        da_ref[0] = (cw[2:3] * dac + cw[1:2] * _shift_up(dac, 1) + cw[0:1] * _shift_up(dac, 2)).astype(bf16)
        dcw_ref[0:1, :] += jnp.sum(dac * _shift_down(a, 2), axis=0, keepdims=True)
        dcw_ref[1:2, :] += jnp.sum(dac * _shift_down(a, 1), axis=0, keepdims=True)
        dcw_ref[2:3, :] += jnp.sum(dac * a, axis=0, keepdims=True)
        dcb_ref[...] += jnp.sum(dac, axis=0, keepdims=True)

    seq = pl.BlockSpec((1, S, CONV_TC), lambda c, b: (b, 0, c))
    return _call(body, "conv_bwd", (nc, Bl),
                 [seq, pl.BlockSpec((1, S, CONV_TC), lambda c, b: (b, 0, nc + c)), pl.BlockSpec((3, CONV_TC), lambda c, b: (0, c)),
                  pl.BlockSpec((1, CONV_TC), lambda c, b: (0, c)), seq],
                 (seq, seq, pl.BlockSpec((3, CONV_TC), lambda c, b: (0, c)), pl.BlockSpec((1, CONV_TC), lambda c, b: (0, c))),
                 (jax.ShapeDtypeStruct((Bl, S, D_FF), bf16), jax.ShapeDtypeStruct((Bl, S, D_FF), bf16),
                  jax.ShapeDtypeStruct((3, D_FF), f32), jax.ShapeDtypeStruct((1, D_FF), f32)),
                 sem=("arbitrary", "arbitrary"))(ab, ab, cw, cb, dact)


def _local_step(x, mem, target, p, w_in, tok, late_b, late_c, send):
    Bl, S, Dd = x.shape
    T = Bl * S
    x2d, t2d, mem2d = x.reshape(T, Dd), target.reshape(T, Dd), mem.reshape(Bl * MEM_LEN, Dd)
    b_st = jnp.pad(p["b_spatial"].T, ((0, 0), (0, 128 - N_HEAD)))
    lbl = p["lb_logits"]

    h = _rms_fwd(x2d, p["norm1_g"] + tok[0, 0], "norm1_fwd")
    proj = _mm(h, w_in, "nn", f32, "proj_fwd", 1024, 1664)
    a_out = _gmlp_fwd(proj, p["ln_v_g"], p["ln_v_b"], p["w_spatial"], b_st)
    b_out, states = _hgrn_fwd(proj, lbl, p["hgrn_norm_g"], Bl, S)
    memn = _rms_fwd(mem2d, p["mem_norm_g"], "memnorm_fwd")
    w = late_b(b_out)
    wb = w["w_branch"]
    kv = _mm(memn, w["w_mem_kv"], "nn", f32, "kv_fwd", 512, 1024).reshape(Bl, MEM_LEN, 2 * 512)
    c_out = _attn_fwd(proj, kv, Bl, S)
    branches = (a_out, b_out, c_out)
    merged = _merge_fwd(branches, wb, proj)
    x1 = _mm(merged, w["w_out"], "nn", f32, "out_fwd", 1024, 1024, residual=x2d)
    h2 = _rms_fwd(x1, p["norm2_g"], "norm2_fwd")
    w.update(late_c(h2))
    ab = _mm(h2, w["w_up"], "nn", f32, "up_fwd", 1024, 1408)
    act = _conv_fwd(ab.reshape(Bl, S, 2 * D_FF), w["conv_w"], p["conv_b"], Bl, S).reshape(T, D_FF)
    x2 = _mm(act, w["w_down"], "nn", f32, "down_fwd", 512, 1024, residual=x1)
    loss_part, dx2, g_final = _final_loss(x2, p["final_g"], t2d)

    g_w_down = _mm(act, dx2, "tn", bf16, "down_dw", 1408, 1024, 1024)
    dact = _mm(dx2, w["w_down"], "nt", bf16, "down_dx", 1024, 1408)
    da, db, g_conv_w, g_conv_b = _conv_bwd(ab.reshape(Bl, S, 2 * D_FF), w["conv_w"], p["conv_b"], dact.reshape(Bl, S, D_FF), Bl, S)
    dab = jnp.concatenate([da.reshape(T, D_FF), db.reshape(T, D_FF)], axis=-1)
    g_w_up = _mm(h2, dab, "tn", bf16, "up_dw", 512, 1408, 1024)
    tok1 = send("c", dict(w_up=g_w_up, conv_w=g_conv_w, w_down=g_w_down))
    dh2 = _mm(dab, w["w_up"], "nt", f32, "up_dx", 1024, 1024, 1408)
    dx1, g_norm2 = _rms_bwd(x1, p["norm2_g"] + tok1[0, 0], dh2, "norm2_bwd", residual=dx2)

    g_w_out = _mm(merged, dx1, "tn", bf16, "out_dw", 1024, 1024, 1024)
    dmerged = _mm(dx1, w["w_out"], "nt", f32, "out_dx", 1024, 1024)
    dgl, dup = _merge_bwd(branches, wb, proj, dmerged)
    g_w_branch = [_mm(branches[n], dup[n], "tn", bf16, f"branch_dw{n}", 512, 1024, 1024) for n in range(3)]
    dbr = [_mm(dup[n], wb[n], "nt", bf16, f"branch_dx{n}", 1024, 512) for n in range(3)]
    dxq, dkv = _attn_bwd(proj, kv, dbr[2], Bl, S)
    dkv = dkv.reshape(Bl * MEM_LEN, 2 * 512)
    g_w_kv = _mm(memn, dkv, "tn", bf16, "kv_dw", 1024, 1024, 512)
    tok2 = send("b", dict(w_mem_kv=g_w_kv, w_branch=g_w_branch, w_out=g_w_out))
    dmemn = _mm(dkv, w["w_mem_kv"], "nt", f32, "kv_dx", 512, 1024)
    _, g_mem_norm = _rms_bwd(mem2d, p["mem_norm_g"] + tok2[0, 0], dmemn, "memnorm_bwd")
    dzuv, g_ln_g, g_ln_b, g_w_sp, g_b_sp = _gmlp_bwd(proj, p["ln_v_g"] + tok2[0, 0], p["ln_v_b"], p["w_spatial"], b_st, dbr[0])
    dq, df, di, dg, g_lbl, g_ng = _hgrn_bwd(proj, lbl, p["hgrn_norm_g"], states, dbr[1], Bl, S)
    dproj = jnp.concatenate([dzuv, dq, df, di, dg, dxq, dgl[0], dgl[1], dgl[2]], axis=-1)
    g_w_in = _mm(h, dproj, "tn", bf16, "proj_dw", 512, 1664, 1024)
    tok3 = send("a", dict(w_in=g_w_in))
    dh = _mm(dproj, w_in, "nt", f32, "proj_dx", 1024, 1024, 1664)
    dx, g_norm1 = _rms_bwd(x2d, p["norm1_g"] + tok3[0, 0], dh, "norm1_bwd", residual=dx1)

    gs = dict(w_spatial=g_w_sp, norm1_g=g_norm1, mem_norm_g=g_mem_norm, norm2_g=g_norm2, final_g=g_final, lb_logits=g_lbl,
              ln_v_g=g_ln_g, ln_v_b=g_ln_b, b_spatial=g_b_sp, hgrn_norm_g=g_ng, conv_b=g_conv_b)
    return loss_part, dx.reshape(Bl, S, Dd), gs


def _coords():
    return lax.axis_index("x"), lax.axis_index("y"), lax.axis_index("c")


def _slot(dev):
    return 4 * dev[0] + 2 * dev[1] + dev[2]


def _comm_call(body, name, arrays, out_shapes, n_sem):
    n = len(arrays)
    hbm = pl.BlockSpec(memory_space=pl.ANY)
    return pl.pallas_call(
        body, name=name, out_shape=out_shapes, in_specs=[hbm] * n, out_specs=[hbm] * n,
        scratch_shapes=[pltpu.SemaphoreType.DMA((n_sem, n)), pltpu.SemaphoreType.DMA((n_sem, n)), pltpu.SemaphoreType.DMA((n,))])(*arrays)


def _all_gather(blocks, name):
    n = len(blocks)

    def body(*refs):
        x_refs, o_refs, (send_sems, recv_sems, local_sems) = refs[:n], refs[n:2 * n], refs[2 * n:]
        x, y, c = _coords()
        me, sibling = (x, y, c), (x, y, 1 - c)
        chips = [(1 - x, y), (x, 1 - y), (1 - x, 1 - y)]

        def copy(a, k, block_dev, to, from_input=False):
            dst = o_refs[a].at[_slot(block_dev)]
            return pltpu.make_async_remote_copy(src_ref=x_refs[a] if from_input else dst, dst_ref=dst, send_sem=send_sems.at[k, a],
                                                recv_sem=recv_sems.at[k, a], device_id=to, device_id_type=MESH)

        mine = [pltpu.make_async_copy(x_refs[a], o_refs[a].at[_slot(me)], local_sems.at[a]) for a in range(n)]
        first = [copy(a, 0, me, sibling, True) for a in range(n)]
        first += [copy(a, 1 + j, me, (*chip, c), True) for j, chip in enumerate(chips) for a in range(n)]
        for cp in mine + first:
            cp.start()
        passed = []
        for j, chip in enumerate(chips):
            for a in range(n):
                copy(a, 1 + j, (*chip, c), me).wait_recv()
                fwd = copy(a, 4 + j, (*chip, c), sibling)
                fwd.start()
                passed.append(fwd)
        for a in range(n):
            copy(a, 0, sibling, me).wait_recv()
        for j, chip in enumerate(chips):
            for a in range(n):
                copy(a, 4 + j, (*chip, 1 - c), me).wait_recv()
        for cp in first + passed:
            cp.wait_send()
        for cp in mine:
            cp.wait()

    return _comm_call(body, name, blocks, [jax.ShapeDtypeStruct((N_DEV,) + b.shape, b.dtype) for b in blocks], 7)


def _all_to_all(parts, name):
    n = len(parts)
    rel = [(0, 0, 1), (0, 1, 0), (0, 1, 1), (1, 0, 0), (1, 0, 1), (1, 1, 0), (1, 1, 1)]

    def body(*refs):
        x_refs, o_refs, (send_sems, recv_sems, local_sems) = refs[:n], refs[n:2 * n], refs[2 * n:]
        x, y, c = _coords()
        me = (x, y, c)
        peers = [(x ^ dx, y ^ dy, c ^ dc) for dx, dy, dc in rel]

        def copy(a, k, peer):
            return pltpu.make_async_remote_copy(src_ref=x_refs[a].at[_slot(peer)], dst_ref=o_refs[a].at[_slot(me)], send_sem=send_sems.at[k, a],
                                                recv_sem=recv_sems.at[k, a], device_id=peer, device_id_type=MESH)

        def arrival(a, k, peer):
            return pltpu.make_async_remote_copy(src_ref=x_refs[a].at[_slot(me)], dst_ref=o_refs[a].at[_slot(peer)], send_sem=send_sems.at[k, a],
                                                recv_sem=recv_sems.at[k, a], device_id=peer, device_id_type=MESH)

        mine = [pltpu.make_async_copy(x_refs[a].at[_slot(me)], o_refs[a].at[_slot(me)], local_sems.at[a]) for a in range(n)]
        sends = [copy(a, k, peer) for k, peer in enumerate(peers) for a in range(n)]
        for cp in mine + sends:
            cp.start()
        for k, peer in enumerate(peers):
            for a in range(n):
                arrival(a, k, peer).wait_recv()
        for cp in sends:
            cp.wait_send()
        for cp in mine:
            cp.wait()

    return _comm_call(body, name, parts, [jax.ShapeDtypeStruct(p.shape, p.dtype) for p in parts], 7)


_HBM = pl.BlockSpec(memory_space=pltpu.HBM)
_SEM = pl.BlockSpec(memory_space=pltpu.SEMAPHORE)
_REL = [(0, 0, 1), (0, 1, 0), (0, 1, 1), (1, 0, 0), (1, 0, 1), (1, 1, 0), (1, 1, 1)]


def _split_copies(gather, src, land, send, recv):
    x, y, c = _coords()
    me = (x, y, c)
    out, arrive = [], []
    for a in range(len(src)):
        for k, (dx, dy, dc) in enumerate(_REL):
            peer = (x ^ dx, y ^ dy, c ^ dc)
            mine = src[a] if gather else src[a].at[_slot(peer)]
            out.append(pltpu.make_async_remote_copy(src_ref=mine, dst_ref=land[a].at[_slot(me)], send_sem=send[a].at[k],
                                                    recv_sem=recv[a].at[k], device_id=peer, device_id_type=MESH))
            arrive.append(pltpu.make_async_remote_copy(src_ref=mine, dst_ref=land[a].at[_slot(peer)], send_sem=send[a].at[k],
                                                       recv_sem=recv[a].at[k], device_id=peer, device_id_type=MESH))
    return me, out, arrive


def _exchange_start(arrays, gather, name, after=None):
    n = len(arrays)
    e = 0 if after is None else 1
    lands = [lax.empty(((N_DEV,) + a.shape) if gather else a.shape, a.dtype) for a in arrays]

    def body(*refs):
        src, land = refs[:n], refs[n:2 * n]
        refs = refs[2 * n + e:]
        send, recv, token, local_sems = refs[:n], refs[n:2 * n], refs[4 * n], refs[4 * n + 1]
        me, out, _ = _split_copies(gather, src, land, send, recv)
        local = [pltpu.make_async_copy(src[a] if gather else src[a].at[_slot(me)], land[a].at[_slot(me)], local_sems.at[a])
                 for a in range(n)]
        for cp in out + local:
            cp.start()
        for cp in local:
            cp.wait()
        token[...] = jnp.zeros_like(token)

    sems = [pltpu.SemaphoreType.DMA((7,)) for _ in range(2 * n)]
    outs = pl.pallas_call(
        body, name=name,
        out_shape=(*sems, *[pltpu.HBM(a.shape, a.dtype) for a in arrays], *[pltpu.HBM(l.shape, l.dtype) for l in lands],
                   jax.ShapeDtypeStruct((8, 128), f32)),
        in_specs=[_HBM] * (2 * n) + [pl.BlockSpec(memory_space=pl.ANY)] * e,
        out_specs=(*[_SEM] * (2 * n), *[_HBM] * (2 * n), pl.BlockSpec(memory_space=pltpu.VMEM)),
        input_output_aliases={i: 2 * n + i for i in range(2 * n)},
        scratch_shapes=[pltpu.SemaphoreType.DMA((n,))],
        compiler_params=pltpu.CompilerParams(has_side_effects=pltpu.SideEffectType.DATAFLOW_SIDE_EFFECTING))(
        *[pltpu.with_memory_space_constraint(a, pltpu.HBM) for a in arrays],
        *[pltpu.with_memory_space_constraint(l, pltpu.HBM) for l in lands], *([after] if e else []))
    return (gather, n, outs[:4 * n]), outs[4 * n]


def _exchange_wait(handle, which, after, name):
    gather, n_all, vals = handle
    send_v, recv_v, src_v, land_v = [[vals[g * n_all + i] for i in which] for g in range(4)]
    n = len(which)

    def body(*refs):
        src, land, send, recv = refs[:n], refs[n:2 * n], refs[2 * n:3 * n], refs[3 * n:4 * n]
        _, out, arrive = _split_copies(gather, src, land, send, recv)
        for cp in out:
            cp.wait_send()
        for cp in arrive:
            cp.wait_recv()

    outs = pl.pallas_call(
        body, name=name,
        out_shape=(*[pltpu.HBM(a.shape, a.dtype) for a in src_v], *[pltpu.HBM(l.shape, l.dtype) for l in land_v]),
        in_specs=[*[_HBM] * (2 * n), *[_SEM] * (2 * n), pl.BlockSpec(memory_space=pl.ANY)], out_specs=[_HBM] * (2 * n),
        input_output_aliases={i: i for i in range(2 * n)},
        compiler_params=pltpu.CompilerParams(has_side_effects=pltpu.SideEffectType.DATAFLOW_SIDE_EFFECTING))(
        *src_v, *land_v, *send_v, *recv_v, after)
    return outs[n:]


def _adam_math(w, g, m, v):
    m_ = ADAM_B1 * m + (1.0 - ADAM_B1) * g
    v_ = ADAM_B2 * v + (1.0 - ADAM_B2) * jnp.square(g)
    m_hat = m_ / (1.0 - ADAM_B1 ** ADAM_STEP)
    v_hat = v_ / (1.0 - ADAM_B2 ** ADAM_STEP)
    return -ADAM_LR * (m_hat / (jnp.sqrt(v_hat) + ADAM_EPS) + ADAM_WD * w), m_, v_


def _reduce_adamw(parts, w, m, v, name):
    _, R, L = parts.shape
    tr = _pick(R, (256, 128, 64, 32, 16, 8))

    def body(p_ref, w_ref, m_ref, v_ref, g_ref, d_ref, nm_ref, nv_ref):
        g = p_ref[0].astype(f32)
        for i in range(1, N_DEV):
            g = g + p_ref[i].astype(f32)
        g_ref[...] = g
        d_ref[...], nm_ref[...], nv_ref[...] = _adam_math(w_ref[...], g, m_ref[...], v_ref[...])

    blk = pl.BlockSpec((tr, L), lambda i: (i, 0))
    sh = jax.ShapeDtypeStruct((R, L), f32)
    return _call(body, name, (R // tr,), [pl.BlockSpec((N_DEV, tr, L), lambda i: (0, i, 0)), blk, blk, blk], (blk,) * 4, (sh,) * 4,
                 sem=("parallel",))(parts, w, m, v)


SMALL = (("w_spatial", (512, 128), 0), ("norm1_g", (1, 1024), 512), ("mem_norm_g", (1, 1024), 520), ("norm2_g", (1, 1024), 528),
         ("final_g", (1, 1024), 536), ("lb_logits", (2, 512), 544), ("ln_v_g", (1, 512), 552), ("ln_v_b", (1, 512), 556),
         ("b_spatial", (4, 128), 560), ("hgrn_norm_g", (1, 128), 564), ("conv_b", (1, 2816), 565))
SMALL_USED, SMALL_ROWS = 587, 640


def _segments(shape, base):
    r, n = shape
    per = n // 128
    return [(base + i * per + j, i, slice(j * 128, (j + 1) * 128)) for i in range(r) for j in range(per)]


def _pack_small(gs):
    names = [n for n, _, _ in SMALL]

    def body(*refs):
        src, o_ref = dict(zip(names, refs[:-1])), refs[-1]
        o_ref[SMALL_USED:SMALL_ROWS, :] = jnp.zeros((SMALL_ROWS - SMALL_USED, 128), f32)
        for name, shape, base in SMALL:
            ref = src[name]
            if name == "w_spatial":
                o_ref[base:base + 512, :] = ref[...].reshape(512, 128)
            elif name == "b_spatial":
                o_ref[base:base + 4, :] = ref[0:4, :]
            elif name == "hgrn_norm_g":
                o_ref[base:base + 1, :] = ref[0] + ref[1] + ref[2] + ref[3]
            else:
                for row, i, sl in _segments(shape, base):
                    o_ref[row:row + 1, :] = ref[i:i + 1, sl]

    return pl.pallas_call(body, name="pack_small", out_shape=jax.ShapeDtypeStruct((SMALL_ROWS, 128), f32))(*[gs[n] for n in names])


def _small_update(gathered, w, m, v):
    names = [n for n, _, _ in SMALL]
    k = len(names)

    def body(*refs):
        p_ref = refs[0]
        ins = [dict(zip(names, refs[1 + i * k:1 + (i + 1) * k])) for i in range(3)]
        outs = [dict(zip(names, refs[1 + (3 + i) * k:1 + (4 + i) * k])) for i in range(4)]
        gsum = refs[-1]
        g = p_ref[0]
        for i in range(1, N_DEV):
            g = g + p_ref[i]
        gsum[...] = g
        for name, shape, base in SMALL:
            if name == "w_spatial":
                where = [(slice(base, base + 512), (slice(None), slice(None)))]
            else:
                where = [(slice(row, row + 1), (slice(i, i + 1), sl)) for row, i, sl in _segments(shape, base)]
            for rows, at in where:
                g_ = gsum[rows, :]
                d_, m_, v_ = _adam_math(ins[0][name][at], g_, ins[1][name][at], ins[2][name][at])
                for o, val in zip(outs, (g_, d_, m_, v_)):
                    o[name][at] = val

    args = [gathered] + [d[n] for d in (w, m, v) for n in names]
    out_shapes = [jax.ShapeDtypeStruct(shape, f32) for _ in range(4) for _, shape, _ in SMALL]
    outs = pl.pallas_call(body, name="small_update", out_shape=out_shapes, scratch_shapes=[pltpu.VMEM((SMALL_ROWS, 128), f32)])(*args)
    return [dict(zip(names, outs[i * k:(i + 1) * k])) for i in range(4)]


def _cols_full(g):
    return jnp.moveaxis(g, 0, -2).reshape(g.shape[1:-1] + (N_DEV * g.shape[-1],))


def _cols_parts(full):
    n = full.shape[-1] // N_DEV
    return jnp.moveaxis(full.reshape(full.shape[:-1] + (N_DEV, n)), -2, 0)


def kernel(x, mem, norm1_g, w_in, ln_v_g, ln_v_b, w_spatial, b_spatial, lb_logits, hgrn_norm_g, mem_norm_g, w_mem_kv, w_branch, w_out, norm2_g, w_up, conv_w, conv_b, w_down, final_g, loss_target, m_norm1_g, m_w_in, m_ln_v_g, m_ln_v_b, m_w_spatial, m_b_spatial, m_lb_logits, m_hgrn_norm_g, m_mem_norm_g, m_w_mem_kv, m_w_branch, m_w_out, m_norm2_g, m_w_up, m_conv_w, m_conv_b, m_w_down, m_final_g, v_norm1_g, v_w_in, v_ln_v_g, v_ln_v_b, v_w_spatial, v_b_spatial, v_lb_logits, v_hgrn_norm_g, v_mem_norm_g, v_w_mem_kv, v_w_branch, v_w_out, v_norm2_g, v_w_up, v_conv_w, v_conv_b, v_w_down, v_final_g):
    given = dict(locals())
    order = ("norm1_g", "w_in", "ln_v_g", "ln_v_b", "w_spatial", "b_spatial", "lb_logits", "hgrn_norm_g", "mem_norm_g",
             "w_mem_kv", "w_branch", "w_out", "norm2_g", "w_up", "conv_w", "conv_b", "w_down", "final_g")
    groups = dict(a=("w_in",), b=("w_mem_kv", "w_branch", "w_out"), c=("w_up", "conv_w", "w_down"))

    wire = {n: given[n][0].astype(f32 if n == "conv_w" else bf16) for ns in groups.values() for n in ns}
    g_in = _all_gather([wire["w_in"]], "gather_w_in")[0]
    late = groups["b"] + groups["c"]
    gather, tok = _exchange_start([wire[n] for n in late], True, "gather_rest_start", after=g_in)

    def late_b(after):
        kv_, br_, out_ = _exchange_wait(gather, (0, 1, 2), after, "gather_b_wait")
        br_ = _cols_full(br_)
        return dict(w_mem_kv=kv_.reshape(D_MODEL, 2 * 512), w_branch=[br_[n] for n in range(3)], w_out=out_.reshape(D_MODEL, D_MODEL))

    def late_c(after):
        up_, cw_, down_ = _exchange_wait(gather, (3, 4, 5), after, "gather_c_wait")
        return dict(w_up=_cols_full(up_), conv_w=_cols_full(cw_), w_down=down_.reshape(D_FF, D_MODEL))

    to_parts = dict(w_in=_cols_parts, w_up=_cols_parts, conv_w=_cols_parts,
                    w_branch=lambda g_: _cols_parts(jnp.stack(g_)).reshape(N_DEV, -1, 128),
                    w_mem_kv=lambda g_: g_.reshape(N_DEV, -1, 2 * 512), w_out=lambda g_: g_.reshape(N_DEV, -1, D_MODEL),
                    w_down=lambda g_: g_.reshape(N_DEV, -1, D_MODEL))
    scatters = {}

    def send(tag, grads_):
        scatters[tag], tok_ = _exchange_start([to_parts[n](grads_[n]) for n in groups[tag]], False, f"scatter_{tag}_start")
        return tok_

    small_2d = lambda prefix: {n: given[prefix + n].reshape(shape) for n, shape, _ in SMALL}
    p = small_2d("")
    p["w_spatial"] = w_spatial[0]
    loss_part, grad_x, gs = _local_step(x, mem, loss_target, p, _cols_full(g_in), tok, late_b, late_c, send)
    loss = lax.psum(loss_part[0, 0], ("x", "y", "c"))

    grads, delta, new_m, new_v = {}, {}, {}, {}
    gathered = _all_gather([_pack_small(gs)], "gather_small_grads")[0]
    for dst, res in zip((grads, delta, new_m, new_v), _small_update(gathered, small_2d(""), small_2d("m_"), small_2d("v_"))):
        for n, _, _ in SMALL:
            dst[n] = res[n].reshape(given[n].shape)

    after = gathered
    for tag in ("c", "b", "a"):
        recv = _exchange_wait(scatters[tag], tuple(range(len(groups[tag]))), after, f"scatter_{tag}_wait")
        for n, parts in zip(groups[tag], recv):
            two_d = (-1, given[n].shape[-1])
            res = _reduce_adamw(parts, *[given[pre + n].reshape(two_d) for pre in ("", "m_", "v_")], "adamw_" + n)
            grads[n], delta[n], new_m[n], new_v[n] = [r.reshape(given[n].shape) for r in res]
            after = res[0]

    return (loss, grad_x, *[grads[n] for n in order], *[delta[n] for n in order], *[new_m[n] for n in order],
            *[new_v[n] for n in order])
```

```python
import functools

import jax
import jax.numpy as jnp
from jax import lax
from jax.experimental import pallas as pl
from jax.experimental.pallas import tpu as pltpu

f32 = jnp.float32
bf16 = jnp.bfloat16

N_DEV = 8
D_MODEL = 1024
EPS = 1e-6
GM_CHUNK = 128
HG_CHUNK = 64
HEAD = 128
N_HEAD = 4
MEM_LEN = 256
D_FF = 2816
IN_WIDTH = 6656
C_ZU, C_HQ, C_HF, C_HI, C_HG, C_XQ, C_GL = 0, 1024, 1536, 2048, 2560, 3072, 3584
ADAM_LR, ADAM_B1, ADAM_B2, ADAM_EPS, ADAM_WD, ADAM_STEP = 0.001, 0.9, 0.999, 1e-08, 0.01, 10
VMEM_LIMIT = 56 * 1024 * 1024
MESH = pl.DeviceIdType.MESH


def _pick(n, cands):
    for c in cands:
        if n % c == 0:
            return c
    return n


def _call(body, name, grid, in_specs, out_specs, out_shape, scratch=(), sem=None, **cp):
    params = dict(vmem_limit_bytes=VMEM_LIMIT, **cp)
    if sem is not None:
        params["dimension_semantics"] = sem
    return pl.pallas_call(
        body, name=name, grid=grid, in_specs=in_specs, out_specs=out_specs, out_shape=out_shape,
        scratch_shapes=list(scratch), compiler_params=pltpu.CompilerParams(**params))


_DN = {"nn": (((1,), (0,)), ((), ())), "nt": (((1,), (1,)), ((), ())), "tn": (((0,), (0,)), ((), ()))}


def _raw_dot(a, b, mode):
    return lax.dot_general(a.astype(bf16), b.astype(bf16), _DN[mode], preferred_element_type=f32)


@jax.custom_vjp
def _dot_nn(a, b):
    return _raw_dot(a, b, "nn")


_dot_nn.defvjp(lambda a, b: (_raw_dot(a, b, "nn"), (a, b)),
               lambda r, g: (_raw_dot(g, r[1], "nt"), _raw_dot(r[0], g, "tn")))


@jax.custom_vjp
def _dot_nt(a, b):
    return _raw_dot(a, b, "nt")


_dot_nt.defvjp(lambda a, b: (_raw_dot(a, b, "nt"), (a, b)),
               lambda r, g: (_raw_dot(g, r[1], "nn"), _raw_dot(g, r[0], "tn")))


@jax.custom_vjp
def _dot_tn(a, b):
    return _raw_dot(a, b, "tn")


_dot_tn.defvjp(lambda a, b: (_raw_dot(a, b, "tn"), (a, b)),
               lambda r, g: (_raw_dot(r[1], g, "nt"), _raw_dot(r[0], g, "nn")))


def _tri(n, lower):
    r = lax.broadcasted_iota(jnp.int32, (n, n), 0)
    c = lax.broadcasted_iota(jnp.int32, (n, n), 1)
    return ((c <= r) if lower else (c >= r)).astype(f32)


def _sel_dot(sel, x, mode, x_first=False):
    hi = x.astype(bf16)
    rest = x - hi.astype(f32)
    mid = rest.astype(bf16)
    lo = (rest - mid.astype(f32)).astype(bf16)
    sel = sel.astype(bf16)
    dot = lambda piece: lax.dot_general(*((piece, sel) if x_first else (sel, piece)), _DN[mode], preferred_element_type=f32)
    return dot(hi) + dot(mid) + dot(lo)


def _egrad(fn, x, ct):
    return jax.vjp(fn, x)[1](ct)[0]


def _mm(a, b, mode, out_dtype, name, tm, tn, tk=None, residual=None):
    if mode == "nn":
        (M, K), (_, N) = a.shape, b.shape
    elif mode == "nt":
        (M, K), (N, _) = a.shape, b.shape
    else:
        (K, M), (_, N) = a.shape, b.shape
    tm, tn = min(tm, M), min(tn, N)
    tk = K if tk is None else min(tk, K)
    assert M % tm == 0 and N % tn == 0 and K % tk == 0, (name, M, N, K, tm, tn, tk)
    nk = K // tk

    def body(*refs):
        acc_ref = refs[-1] if nk > 1 else None
        refs = refs[:-1] if nk > 1 else refs
        if residual is None:
            a_ref, b_ref, o_ref = refs
        else:
            a_ref, b_ref, r_ref, o_ref = refs

        def finish(r):
            if residual is not None:
                r = r + r_ref[...]
            o_ref[...] = r.astype(out_dtype)

        part = _raw_dot(a_ref[...], b_ref[...], mode)
        if nk == 1:
            finish(part)
            return
        k = pl.program_id(2)

        @pl.when(k == 0)
        def _():
            acc_ref[...] = part

        @pl.when((k > 0) & (k < nk - 1))
        def _():
            acc_ref[...] += part

        @pl.when(k == nk - 1)
        def _():
            finish(acc_ref[...] + part)

    a_spec = {"nn": pl.BlockSpec((tm, tk), lambda i, j, k: (i, k)),
              "nt": pl.BlockSpec((tm, tk), lambda i, j, k: (i, k)),
              "tn": pl.BlockSpec((tk, tm), lambda i, j, k: (k, i))}[mode]
    b_spec = {"nn": pl.BlockSpec((tk, tn), lambda i, j, k: (k, j)),
              "nt": pl.BlockSpec((tn, tk), lambda i, j, k: (j, k)),
              "tn": pl.BlockSpec((tk, tn), lambda i, j, k: (k, j))}[mode]
    o_spec = pl.BlockSpec((tm, tn), lambda i, j, k: (i, j))
    in_specs = [a_spec, b_spec] + ([o_spec] if residual is not None else [])
    args = (a, b) + ((residual,) if residual is not None else ())
    return _call(body, name, (M // tm, N // tn, nk), in_specs, o_spec, jax.ShapeDtypeStruct((M, N), out_dtype),
                 scratch=[pltpu.VMEM((tm, tn), f32)] if nk > 1 else [], sem=("parallel", "parallel", "arbitrary"))(*args)


def _rms_fwd(x, g, name):
    R, Dd = x.shape
    tr = _pick(R, (512, 256, 128))

    def body(x_ref, g_ref, o_ref):
        xf = x_ref[...]
        y = xf * lax.rsqrt(jnp.mean(xf * xf, axis=-1, keepdims=True) + EPS)
        o_ref[...] = (y * g_ref[...]).astype(bf16)

    return _call(body, name, (R // tr,), [pl.BlockSpec((tr, Dd), lambda i: (i, 0)), pl.BlockSpec((1, Dd), lambda i: (0, 0))],
                 pl.BlockSpec((tr, Dd), lambda i: (i, 0)), jax.ShapeDtypeStruct((R, Dd), bf16), sem=("parallel",))(x, g)


def _rms_bwd(x, g, dh, name, residual=None):
    R, Dd = x.shape
    tr = _pick(R, (512, 256, 128))

    def body(*refs):
        if residual is None:
            x_ref, g_ref, dh_ref, dx_ref, dg_ref = refs
        else:
            x_ref, g_ref, dh_ref, r_ref, dx_ref, dg_ref = refs
        xf = x_ref[...]
        rs = lax.rsqrt(jnp.mean(xf * xf, axis=-1, keepdims=True) + EPS)
        y = xf * rs
        dh_ = dh_ref[...].astype(f32)
        dy = dh_ * g_ref[...]
        dx = rs * (dy - y * jnp.mean(dy * y, axis=-1, keepdims=True))
        if residual is not None:
            dx = dx + r_ref[...]
        dx_ref[...] = dx

        @pl.when(pl.program_id(0) == 0)
        def _():
            dg_ref[...] = jnp.zeros_like(dg_ref)

        dg_ref[...] += jnp.sum(dh_ * y, axis=0, keepdims=True)

    row = pl.BlockSpec((tr, Dd), lambda i: (i, 0))
    vec = pl.BlockSpec((1, Dd), lambda i: (0, 0))
    in_specs = [row, vec, row] + ([row] if residual is not None else [])
    args = (x, g, dh) + ((residual,) if residual is not None else ())
    return _call(body, name, (R // tr,), in_specs, (row, vec),
                 (jax.ShapeDtypeStruct((R, Dd), f32), jax.ShapeDtypeStruct((1, Dd), f32)), sem=("arbitrary",))(*args)


def _final_loss(x2, g, target):
    R, Dd = x2.shape
    tr = _pick(R, (512, 256, 128))

    def body(x_ref, g_ref, t_ref, loss_ref, dx_ref, dg_ref):
        xf = x_ref[...]
        rs = lax.rsqrt(jnp.mean(xf * xf, axis=-1, keepdims=True) + EPS)
        y = xf * rs
        err = y * g_ref[...] - t_ref[...]
        dh_ = err * (1.0 / Dd)
        dy = dh_ * g_ref[...]
        dx_ref[...] = rs * (dy - y * jnp.mean(dy * y, axis=-1, keepdims=True))

        @pl.when(pl.program_id(0) == 0)
        def _():
            dg_ref[...] = jnp.zeros_like(dg_ref)
            loss_ref[...] = jnp.zeros_like(loss_ref)

        dg_ref[...] += jnp.sum(dh_ * y, axis=0, keepdims=True)
        part = jnp.sum(jnp.mean(err * err, axis=-1, keepdims=True), axis=0, keepdims=True)
        loss_ref[...] += 0.5 * part

    row = pl.BlockSpec((tr, Dd), lambda i: (i, 0))
    vec = pl.BlockSpec((1, Dd), lambda i: (0, 0))
    return _call(body, "final_loss", (R // tr,), [row, vec, row], (pl.BlockSpec((1, 128), lambda i: (0, 0)), row, vec),
                 (jax.ShapeDtypeStruct((1, 128), f32), jax.ShapeDtypeStruct((R, Dd), f32), jax.ShapeDtypeStruct((1, Dd), f32)),
                 sem=("arbitrary",))(x2, g, target)


def _gmlp_parts(zuv, ln_g, ln_b):
    zu, zv = zuv[:, :512], zuv[:, 512:]
    u = jax.nn.gelu(zu)
    v = jax.nn.gelu(zv)
    mu = jnp.mean(v, axis=-1, keepdims=True)
    rs = lax.rsqrt(jnp.mean(jnp.square(v - mu), axis=-1, keepdims=True) + EPS)
    xh = (v - mu) * rs
    return zu, zv, u, xh, rs, xh * ln_g + ln_b


def _gmlp_fwd(proj, ln_g, ln_b, w_s, b_st):
    T = proj.shape[0]

    def body(p_ref, g_ref, b_ref, w_ref, bs_ref, o_ref):
        _, _, u, _, _, vn = _gmlp_parts(p_ref[...], g_ref[...], b_ref[...])
        causal = _tri(GM_CHUNK, True) > 0
        for gi in range(N_HEAD):
            sl = slice(gi * HEAD, (gi + 1) * HEAD)
            w = jnp.where(causal, w_ref[gi], 0.0)
            mixed = _raw_dot(w, vn[:, sl], "nn") + bs_ref[:, gi:gi + 1]
            o_ref[:, sl] = (u[:, sl] * mixed).astype(bf16)

    vec = pl.BlockSpec((1, 512), lambda i: (0, 0))
    return _call(body, "gmlp_fwd", (T // GM_CHUNK,),
                 [pl.BlockSpec((GM_CHUNK, 1024), lambda i: (i, 0)), vec, vec,
                  pl.BlockSpec((N_HEAD, GM_CHUNK, GM_CHUNK), lambda i: (0, 0, 0)), pl.BlockSpec((GM_CHUNK, 128), lambda i: (0, 0))],
                 pl.BlockSpec((GM_CHUNK, 512), lambda i: (i, 0)), jax.ShapeDtypeStruct((T, 512), bf16), sem=("parallel",))(
        proj, ln_g, ln_b, w_s, b_st)


def _gmlp_bwd(proj, ln_g, ln_b, w_s, b_st, da):
    T = proj.shape[0]

    def body(p_ref, g_ref, b_ref, w_ref, bs_ref, da_ref, dp_ref, dg_ref, db_ref, dw_ref, dbs_ref):
        zu, zv, u, xh, rs, vn = _gmlp_parts(p_ref[...], g_ref[...], b_ref[...])
        causal = _tri(GM_CHUNK, True) > 0
        sub = lax.broadcasted_iota(jnp.int32, (8, GM_CHUNK), 0)
        ones = jnp.ones((8, HEAD), f32)
        dout = da_ref[...].astype(f32)

        @pl.when(pl.program_id(0) == 0)
        def _():
            for r in (dg_ref, db_ref, dw_ref, dbs_ref):
                r[...] = jnp.zeros_like(r)

        du, dvn, dbs = [], [], jnp.zeros((8, GM_CHUNK), f32)
        for gi in range(N_HEAD):
            sl = slice(gi * HEAD, (gi + 1) * HEAD)
            w = jnp.where(causal, w_ref[gi], 0.0)
            mixed = _raw_dot(w, vn[:, sl], "nn") + bs_ref[:, gi:gi + 1]
            du.append(dout[:, sl] * mixed)
            dm = dout[:, sl] * u[:, sl]
            row_sums = _sel_dot(ones, dm, "nt")
            dbs = dbs + jnp.where(sub == gi, row_sums, 0.0)
            dw_ref[gi] += jnp.where(causal, _raw_dot(dm, vn[:, sl], "nt"), 0.0)
            dvn.append(_raw_dot(w, dm, "tn"))
        dbs_ref[...] += dbs
        du = jnp.concatenate(du, axis=-1)
        dvn = jnp.concatenate(dvn, axis=-1)
        dg_ref[...] += jnp.sum(dvn * xh, axis=0, keepdims=True)
        db_ref[...] += jnp.sum(dvn, axis=0, keepdims=True)
        dxh = dvn * g_ref[...]
        dv = rs * (dxh - jnp.mean(dxh, axis=-1, keepdims=True) - xh * jnp.mean(dxh * xh, axis=-1, keepdims=True))
        dp_ref[:, :512] = _egrad(jax.nn.gelu, zu, du).astype(bf16)
        dp_ref[:, 512:] = _egrad(jax.nn.gelu, zv, dv).astype(bf16)

    vec = pl.BlockSpec((1, 512), lambda i: (0, 0))
    wsp = pl.BlockSpec((N_HEAD, GM_CHUNK, GM_CHUNK), lambda i: (0, 0, 0))
    return _call(body, "gmlp_bwd", (T // GM_CHUNK,),
                 [pl.BlockSpec((GM_CHUNK, 1024), lambda i: (i, 0)), vec, vec, wsp, pl.BlockSpec((GM_CHUNK, 128), lambda i: (0, 0)),
                  pl.BlockSpec((GM_CHUNK, 512), lambda i: (i, 0))],
                 (pl.BlockSpec((GM_CHUNK, 1024), lambda i: (i, 0)), vec, vec, wsp, pl.BlockSpec((8, GM_CHUNK), lambda i: (0, 0))),
                 (jax.ShapeDtypeStruct((T, 1024), bf16), jax.ShapeDtypeStruct((1, 512), f32), jax.ShapeDtypeStruct((1, 512), f32),
                  jax.ShapeDtypeStruct((N_HEAD, GM_CHUNK, GM_CHUNK), f32), jax.ShapeDtypeStruct((8, GM_CHUNK), f32)),
                 sem=("arbitrary",))(proj, ln_g, ln_b, w_s, b_st, da)


HG_SUB = 16
HG_NSUB = HG_CHUNK // HG_SUB


def _two_level_matrix():
    r = lax.broadcasted_iota(jnp.int32, (2 * HG_CHUNK, HG_CHUNK), 0)
    c = lax.broadcasted_iota(jnp.int32, (2 * HG_CHUNK, HG_CHUNK), 1)
    t = jnp.where(r < HG_CHUNK, r, r - HG_CHUNK)
    local = (r < HG_CHUNK) & (t // HG_SUB == c // HG_SUB) & (c <= t)
    before = (r >= HG_CHUNK) & (c < (t // HG_SUB) * HG_SUB)
    return (local | before).astype(f32)


def _two_level_sums(x):
    two = _sel_dot(_two_level_matrix(), x, "nn")
    return two[:HG_CHUNK], two[HG_CHUNK:]


@jax.custom_vjp
def _two_level_cumsum(x):
    return _two_level_sums(x)


_two_level_cumsum.defvjp(
    lambda x: (_two_level_sums(x), None),
    lambda _, g: (_sel_dot(_two_level_matrix(), jnp.concatenate(g, axis=0), "tn"),))


def _tile_matrix():
    s = lax.broadcasted_iota(jnp.int32, (HG_SUB, HG_CHUNK), 0)
    j = lax.broadcasted_iota(jnp.int32, (HG_SUB, HG_CHUNK), 1)
    return (j % HG_SUB == s).astype(f32)


@jax.custom_vjp
def _tile_lanes(x):
    return _sel_dot(_tile_matrix(), x, "nn", x_first=True)


_tile_lanes.defvjp(
    lambda x: (_sel_dot(_tile_matrix(), x, "nn", x_first=True), None),
    lambda _, g: (_sel_dot(_tile_matrix(), g, "nt", x_first=True),))


def _block_rows(x):
    k = x.shape[-1]
    return jnp.broadcast_to(x.reshape(HG_NSUB, 1, HG_SUB, k), (HG_NSUB, HG_SUB, HG_SUB, k)).reshape(HG_CHUNK, HG_SUB, k)


def _hgrn_chunk(st0, q_raw, f_raw, i_raw, g_raw, l0, l1, ng):
    C, SUB = HG_CHUNK, HG_SUB
    lb = jax.nn.sigmoid(l0 - l1)
    fg = lb + (1.0 - lb) * jax.nn.sigmoid(f_raw)
    kk = 1.0 - fg
    qf = jax.nn.silu(q_raw)
    al, base = _two_level_cumsum(jnp.log(fg))
    a = al + base
    row = lax.broadcasted_iota(jnp.int32, (C, HEAD), 0)
    a_last = jnp.sum(jnp.where(row == C - 1, a, 0.0), axis=0, keepdims=True)
    inter = _dot_nt(qf * jnp.exp(a), st0)
    qt = qf * jnp.exp(al)
    rb = lax.broadcasted_iota(jnp.int32, (C, C), 0) // SUB
    cb = lax.broadcasted_iota(jnp.int32, (C, C), 1) // SUB
    scores = jnp.zeros((C, C), f32)
    for i in range(1, HG_NSUB):
        base_i = jnp.sum(jnp.where(row == i * SUB, base, 0.0), axis=0, keepdims=True)
        kt = kk * jnp.exp(jnp.minimum(base_i - a, 0.0))
        scores = scores + jnp.where((rb == i) & (cb < i), _dot_nt(qt, kt), 0.0)
    t_i = lax.broadcasted_iota(jnp.int32, (C, SUB, HEAD), 0) % SUB
    s_i = lax.broadcasted_iota(jnp.int32, (C, SUB, HEAD), 1)
    decay = jnp.exp(jnp.where(s_i <= t_i, al[:, None, :] - _block_rows(al), -jnp.inf))
    diag = jnp.sum(qf[:, None, :] * decay * _block_rows(kk), axis=-1)
    scores = scores + jnp.where(rb == cb, _tile_lanes(diag), 0.0)
    o = inter + _dot_nn(scores, i_raw)
    st1 = jnp.exp(a_last) * st0 + _dot_tn(i_raw, kk * jnp.exp(a_last - a))
    on = o * lax.rsqrt(jnp.mean(o * o, axis=-1, keepdims=True) + EPS) * ng
    return st1, on * jax.nn.silu(g_raw)


def _hgrn_specs(S, Bl, rev):
    N = S // HG_CHUNK
    chunk = (lambda n: N - 1 - n) if rev else (lambda n: n)
    col = lambda c0: pl.BlockSpec((Bl, HG_CHUNK, 512), lambda n: (0, chunk(n), c0 // 512))
    st = pl.BlockSpec((Bl, N_HEAD, 1, HEAD, HEAD), lambda n: (0, 0, chunk(n), 0, 0))
    full = lambda *s: pl.BlockSpec(s, functools.partial(lambda n, nd: (0,) * nd, nd=len(s)))
    return N, col, st, full


def _hgrn_fwd(proj, lb_logits, ng, Bl, S):
    N, col, st, full = _hgrn_specs(S, Bl, False)

    def body(q_ref, f_ref, i_ref, g_ref, l_ref, ng_ref, o_ref, st_ref, state):
        @pl.when(pl.program_id(0) == 0)
        def _():
            state[...] = jnp.zeros_like(state)

        for b in range(Bl):
            for h in range(N_HEAD):
                sl = slice(h * HEAD, (h + 1) * HEAD)
                st0 = state[b, h]
                st_ref[b, h, 0] = st0
                st1, out = _hgrn_chunk(st0, q_ref[b, :, sl], f_ref[b, :, sl], i_ref[b, :, sl], g_ref[b, :, sl],
                                       l_ref[0:1, sl], l_ref[1:2, sl], ng_ref[...])
                state[b, h] = st1
                o_ref[b, :, sl] = out.astype(bf16)

    return _call(body, "hgrn_fwd", (N,), [col(C_HQ), col(C_HF), col(C_HI), col(C_HG), full(2, 512), full(1, HEAD)],
                 (col(0), st),
                 (jax.ShapeDtypeStruct((Bl, S, 512), bf16), jax.ShapeDtypeStruct((Bl, N_HEAD, N, HEAD, HEAD), f32)),
                 scratch=[pltpu.VMEM((Bl, N_HEAD, HEAD, HEAD), f32)], sem=("arbitrary",))(
        proj, proj, proj, proj, lb_logits, ng)


def _hgrn_bwd(proj, lb_logits, ng, states, db, Bl, S):
    N, col, st, full = _hgrn_specs(S, Bl, True)

    def body(q_ref, f_ref, i_ref, g_ref, l_ref, ng_ref, st_ref, db_ref,
             dq_ref, df_ref, di_ref, dg_ref, dl_ref, dng_ref, dstate):
        @pl.when(pl.program_id(0) == 0)
        def _():
            dstate[...] = jnp.zeros_like(dstate)
            dl_ref[...] = jnp.zeros_like(dl_ref)
            dng_ref[...] = jnp.zeros_like(dng_ref)

        for b in range(Bl):
            for h in range(N_HEAD):
                sl = slice(h * HEAD, (h + 1) * HEAD)
                _, vjp = jax.vjp(_hgrn_chunk, st_ref[b, h, 0], q_ref[b, :, sl], f_ref[b, :, sl], i_ref[b, :, sl], g_ref[b, :, sl],
                                 l_ref[0:1, sl], l_ref[1:2, sl], ng_ref[...])
                dst0, dq, df, di, dg, dl0, dl1, dng = vjp((dstate[b, h], db_ref[b, :, sl].astype(f32)))
                dstate[b, h] = dst0
                dq_ref[b, :, sl] = dq.astype(bf16)
                df_ref[b, :, sl] = df.astype(bf16)
                di_ref[b, :, sl] = di.astype(bf16)
                dg_ref[b, :, sl] = dg.astype(bf16)
                dl_ref[0:1, sl] += dl0
                dl_ref[1:2, sl] += dl1
                dng_ref[b, h] += dng

    return _call(body, "hgrn_bwd", (N,),
                 [col(C_HQ), col(C_HF), col(C_HI), col(C_HG), full(2, 512), full(1, HEAD), st, col(0)],
                 (*[col(0)] * 4, full(2, 512), full(Bl, N_HEAD, 1, HEAD)),
                 (*[jax.ShapeDtypeStruct((Bl, S, 512), bf16)] * 4, jax.ShapeDtypeStruct((2, 512), f32),
                  jax.ShapeDtypeStruct((Bl, N_HEAD, 1, HEAD), f32)),
                 scratch=[pltpu.VMEM((Bl, N_HEAD, HEAD, HEAD), f32)], sem=("arbitrary",))(
        proj, proj, proj, proj, lb_logits, ng, states, db)


def _attn_probs(q, k):
    s = _raw_dot(q, k, "nt") * (HEAD ** -0.5)
    e = jnp.exp(s - jnp.max(s, axis=-1, keepdims=True))
    return e / jnp.sum(e, axis=-1, keepdims=True)


def _attn_specs(S, tq):
    nq = S // tq
    q = pl.BlockSpec((tq, 512), lambda b, i: (b * nq + i, C_XQ // 512))
    kv = pl.BlockSpec((1, MEM_LEN, 1024), lambda b, i: (b, 0, 0))
    o = pl.BlockSpec((tq, 512), lambda b, i: (b * nq + i, 0))
    return nq, q, kv, o


def _attn_fwd(proj, kv, Bl, S):
    tq = _pick(S, (512, 256, 128))
    nq, qs, kvs, os_ = _attn_specs(S, tq)

    def body(q_ref, kv_ref, o_ref):
        for h in range(N_HEAD):
            sl = slice(h * HEAD, (h + 1) * HEAD)
            p = _attn_probs(q_ref[:, sl], kv_ref[0, :, sl])
            o_ref[:, sl] = _raw_dot(p, kv_ref[0, :, 512 + h * HEAD:512 + (h + 1) * HEAD], "nn").astype(bf16)

    return _call(body, "attn_fwd", (Bl, nq), [qs, kvs], os_, jax.ShapeDtypeStruct((Bl * S, 512), bf16),
                 sem=("parallel", "parallel"))(proj, kv)


def _attn_bwd(proj, kv, dc, Bl, S):
    tq = _pick(S, (512, 256, 128))
    nq, qs, kvs, os_ = _attn_specs(S, tq)

    def body(q_ref, kv_ref, do_ref, dq_ref, dkv_ref):
        @pl.when(pl.program_id(1) == 0)
        def _():
            dkv_ref[...] = jnp.zeros_like(dkv_ref)

        for h in range(N_HEAD):
            sl = slice(h * HEAD, (h + 1) * HEAD)
            vsl = slice(512 + h * HEAD, 512 + (h + 1) * HEAD)
            q, k, v, do = q_ref[:, sl], kv_ref[0, :, sl], kv_ref[0, :, vsl], do_ref[:, sl]
            p = _attn_probs(q, k)
            dkv_ref[0, :, vsl] += _raw_dot(p, do, "tn")
            dp = _raw_dot(do, v, "nt")
            ds = p * (dp - jnp.sum(dp * p, axis=-1, keepdims=True)) * (HEAD ** -0.5)
            dq_ref[:, sl] = _raw_dot(ds, k, "nn").astype(bf16)
            dkv_ref[0, :, sl] += _raw_dot(ds, q, "tn")

    return _call(body, "attn_bwd", (Bl, nq), [qs, kvs, os_], (os_, kvs),
                 (jax.ShapeDtypeStruct((Bl * S, 512), bf16), jax.ShapeDtypeStruct((Bl, MEM_LEN, 1024), f32)),
                 sem=("arbitrary", "arbitrary"))(proj, kv, dc)


def _merge_specs(tm, tn):
    br = pl.BlockSpec((tm, 512), lambda i, j: (i, 0))
    w = pl.BlockSpec((512, tn), lambda i, j: (0, j))
    gl = [pl.BlockSpec((tm, tn), functools.partial(lambda i, j, n: (i, (C_GL + n * D_MODEL) // tn + j), n=n)) for n in range(3)]
    return [br, br, br, w, w, w, *gl]


def _merge_fwd(branches, wb, proj):
    T = proj.shape[0]
    tm, tn = _pick(T, (1024, 512, 256, 128)), 512

    def body(a_ref, b_ref, c_ref, w0, w1, w2, g0, g1, g2, o_ref):
        acc = jnp.zeros((tm, tn), f32)
        for x_ref, w_ref, g_ref in ((a_ref, w0, g0), (b_ref, w1, g1), (c_ref, w2, g2)):
            acc = acc + jax.nn.sigmoid(g_ref[...]) * _raw_dot(x_ref[...], w_ref[...], "nn")
        o_ref[...] = acc.astype(bf16)

    return _call(body, "merge_fwd", (T // tm, D_MODEL // tn), _merge_specs(tm, tn), pl.BlockSpec((tm, tn), lambda i, j: (i, j)),
                 jax.ShapeDtypeStruct((T, D_MODEL), bf16), sem=("parallel", "parallel"))(*branches, *wb, proj, proj, proj)


def _merge_bwd(branches, wb, proj, dmerged):
    T = proj.shape[0]
    tm, tn = _pick(T, (1024, 512, 256, 128)), 512

    def body(a_ref, b_ref, c_ref, w0, w1, w2, g0, g1, g2, dm_ref, dgl_ref, d0, d1, d2):
        dm = dm_ref[...]
        for n, (x_ref, w_ref, g_ref, d_ref) in enumerate(((a_ref, w0, g0, d0), (b_ref, w1, g1, d1), (c_ref, w2, g2, d2))):
            up = _raw_dot(x_ref[...], w_ref[...], "nn")
            logits = g_ref[...]
            dgl_ref[n] = _egrad(jax.nn.sigmoid, logits, dm * up).astype(bf16)
            d_ref[...] = (dm * jax.nn.sigmoid(logits)).astype(bf16)

    blk = pl.BlockSpec((tm, tn), lambda i, j: (i, j))
    sh = jax.ShapeDtypeStruct((T, D_MODEL), bf16)
    outs = _call(body, "merge_bwd", (T // tm, D_MODEL // tn), [*_merge_specs(tm, tn), blk],
                 (pl.BlockSpec((3, tm, tn), lambda i, j: (0, i, j)), blk, blk, blk),
                 (jax.ShapeDtypeStruct((3, T, D_MODEL), bf16), sh, sh, sh),
                 sem=("parallel", "parallel"))(*branches, *wb, proj, proj, proj, dmerged)
    return outs[0], outs[1:]


CONV_TC = 256


def _shift_down(a, k):
    row = lax.broadcasted_iota(jnp.int32, a.shape, 0)
    return jnp.where(row >= k, pltpu.roll(a, k, 0), 0.0)


def _shift_up(a, k):
    n = a.shape[0]
    row = lax.broadcasted_iota(jnp.int32, a.shape, 0)
    return jnp.where(row < n - k, pltpu.roll(a, n - k, 0), 0.0)


def _conv_pre(a, cw, cb):
    return cb + cw[0:1] * _shift_down(a, 2) + cw[1:2] * _shift_down(a, 1) + cw[2:3] * a


def _conv_fwd(ab, cw, cb, Bl, S):
    nc = D_FF // CONV_TC

    def body(a_ref, b_ref, cw_ref, cb_ref, o_ref):
        ac = _conv_pre(a_ref[0], cw_ref[...], cb_ref[...])
        o_ref[0] = (jax.nn.silu(ac) * b_ref[0]).astype(bf16)

    return _call(body, "conv_fwd", (Bl, nc),
                 [pl.BlockSpec((1, S, CONV_TC), lambda b, c: (b, 0, c)), pl.BlockSpec((1, S, CONV_TC), lambda b, c: (b, 0, nc + c)),
                  pl.BlockSpec((3, CONV_TC), lambda b, c: (0, c)), pl.BlockSpec((1, CONV_TC), lambda b, c: (0, c))],
                 pl.BlockSpec((1, S, CONV_TC), lambda b, c: (b, 0, c)), jax.ShapeDtypeStruct((Bl, S, D_FF), bf16),
                 sem=("parallel", "parallel"))(ab, ab, cw, cb)


def _conv_bwd(ab, cw, cb, dact, Bl, S):
    nc = D_FF // CONV_TC

    def body(a_ref, b_ref, cw_ref, cb_ref, d_ref, da_ref, db_ref, dcw_ref, dcb_ref):
        @pl.when(pl.program_id(1) == 0)
        def _():
            dcw_ref[...] = jnp.zeros_like(dcw_ref)
            dcb_ref[...] = jnp.zeros_like(dcb_ref)

        a, cw = a_ref[0], cw_ref[...]
        ac = _conv_pre(a, cw, cb_ref[...])
        dact_ = d_ref[0].astype(f32)
        db_ref[0] = (dact_ * jax.nn.silu(ac)).astype(bf16)
        dac = _egrad(jax.nn.silu, ac, dact_ * b_ref[0])
        da_ref[0] = (cw[2:3] * dac + cw[1:2] * _shift_up(dac, 1) + cw[0:1] * _shift_up(dac, 2)).astype(bf16)
        dcw_ref[0:1, :] += jnp.sum(dac * _shift_down(a, 2), axis=0, keepdims=True)
        dcw_ref[1:2, :] += jnp.sum(dac * _shift_down(a, 1), axis=0, keepdims=True)
        dcw_ref[2:3, :] += jnp.sum(dac * a, axis=0, keepdims=True)
        dcb_ref[...] += jnp.sum(dac, axis=0, keepdims=True)

    seq = pl.BlockSpec((1, S, CONV_TC), lambda c, b: (b, 0, c))
    return _call(body, "conv_bwd", (nc, Bl),
                 [seq, pl.BlockSpec((1, S, CONV_TC), lambda c, b: (b, 0, nc + c)), pl.BlockSpec((3, CONV_TC), lambda c, b: (0, c)),
                  pl.BlockSpec((1, CONV_TC), lambda c, b: (0, c)), seq],
                 (seq, seq, pl.BlockSpec((3, CONV_TC), lambda c, b: (0, c)), pl.BlockSpec((1, CONV_TC), lambda c, b: (0, c))),
                 (jax.ShapeDtypeStruct((Bl, S, D_FF), bf16), jax.ShapeDtypeStruct((Bl, S, D_FF), bf16),
                  jax.ShapeDtypeStruct((3, D_FF), f32), jax.ShapeDtypeStruct((1, D_FF), f32)),
                 sem=("arbitrary", "arbitrary"))(ab, ab, cw, cb, dact)


def _local_step(x, mem, target, p, w_in, tok, late_b, late_c, send):
    Bl, S, Dd = x.shape
    T = Bl * S
    x2d, t2d, mem2d = x.reshape(T, Dd), target.reshape(T, Dd), mem.reshape(Bl * MEM_LEN, Dd)
    b_st = jnp.pad(p["b_spatial"].T, ((0, 0), (0, 128 - N_HEAD)))
    lbl = p["lb_logits"]

    h = _rms_fwd(x2d, p["norm1_g"] + tok[0, 0], "norm1_fwd")
    proj = _mm(h, w_in, "nn", f32, "proj_fwd", 1024, 1664)
    a_out = _gmlp_fwd(proj, p["ln_v_g"], p["ln_v_b"], p["w_spatial"], b_st)
    proj3 = proj.reshape(Bl, S, IN_WIDTH)
    b_out, states = _hgrn_fwd(proj3, lbl, p["hgrn_norm_g"], Bl, S)
    b_out = b_out.reshape(T, 512)
    memn = _rms_fwd(mem2d, p["mem_norm_g"], "memnorm_fwd")
    w = late_b(b_out)
    wb = w["w_branch"]
    kv = _mm(memn, w["w_mem_kv"], "nn", f32, "kv_fwd", 512, 1024).reshape(Bl, MEM_LEN, 2 * 512)
    c_out = _attn_fwd(proj, kv, Bl, S)
    branches = (a_out, b_out, c_out)
    merged = _merge_fwd(branches, wb, proj)
    x1 = _mm(merged, w["w_out"], "nn", f32, "out_fwd", 1024, 1024, residual=x2d)
    h2 = _rms_fwd(x1, p["norm2_g"], "norm2_fwd")
    w.update(late_c(h2))
    ab = _mm(h2, w["w_up"], "nn", f32, "up_fwd", 1024, 1408)
    act = _conv_fwd(ab.reshape(Bl, S, 2 * D_FF), w["conv_w"], p["conv_b"], Bl, S).reshape(T, D_FF)
    x2 = _mm(act, w["w_down"], "nn", f32, "down_fwd", 512, 1024, residual=x1)
    loss_part, dx2, g_final = _final_loss(x2, p["final_g"], t2d)

    g_w_down = _mm(act, dx2, "tn", bf16, "down_dw", 1408, 1024, 1024)
    dact = _mm(dx2, w["w_down"], "nt", bf16, "down_dx", 1024, 1408)
    da, db, g_conv_w, g_conv_b = _conv_bwd(ab.reshape(Bl, S, 2 * D_FF), w["conv_w"], p["conv_b"], dact.reshape(Bl, S, D_FF), Bl, S)
    dab = jnp.concatenate([da.reshape(T, D_FF), db.reshape(T, D_FF)], axis=-1)
    g_w_up = _mm(h2, dab, "tn", bf16, "up_dw", 512, 1408, 1024)
    tok1 = send("c", dict(w_up=g_w_up, conv_w=g_conv_w, w_down=g_w_down))
    dh2 = _mm(dab, w["w_up"], "nt", f32, "up_dx", 1024, 1024, 1408)
    dx1, g_norm2 = _rms_bwd(x1, p["norm2_g"] + tok1[0, 0], dh2, "norm2_bwd", residual=dx2)

    g_w_out = _mm(merged, dx1, "tn", bf16, "out_dw", 1024, 1024, 1024)
    dmerged = _mm(dx1, w["w_out"], "nt", f32, "out_dx", 1024, 1024)
    dgl, dup = _merge_bwd(branches, wb, proj, dmerged)
    g_w_branch = [_mm(branches[n], dup[n], "tn", bf16, f"branch_dw{n}", 512, 1024, 1024) for n in range(3)]
    dbr = [_mm(dup[n], wb[n], "nt", bf16, f"branch_dx{n}", 1024, 512) for n in range(3)]
    dxq, dkv = _attn_bwd(proj, kv, dbr[2], Bl, S)
    dkv = dkv.reshape(Bl * MEM_LEN, 2 * 512)
    g_w_kv = _mm(memn, dkv, "tn", bf16, "kv_dw", 1024, 1024, 512)
    tok2 = send("b", dict(w_mem_kv=g_w_kv, w_branch=g_w_branch, w_out=g_w_out))
    dmemn = _mm(dkv, w["w_mem_kv"], "nt", f32, "kv_dx", 512, 1024)
    _, g_mem_norm = _rms_bwd(mem2d, p["mem_norm_g"] + tok2[0, 0], dmemn, "memnorm_bwd")
    dzuv, g_ln_g, g_ln_b, g_w_sp, g_b_sp = _gmlp_bwd(proj, p["ln_v_g"] + tok2[0, 0], p["ln_v_b"], p["w_spatial"], b_st, dbr[0])
    *dqfig, g_lbl, g_ng = _hgrn_bwd(proj3, lbl, p["hgrn_norm_g"], states, dbr[1].reshape(Bl, S, 512), Bl, S)
    dq, df, di, dg = [d.reshape(T, 512) for d in dqfig]
    dproj = jnp.concatenate([dzuv, dq, df, di, dg, dxq, dgl[0], dgl[1], dgl[2]], axis=-1)
    g_w_in = _mm(h, dproj, "tn", bf16, "proj_dw", 512, 1664, 1024)
    tok3 = send("a", dict(w_in=g_w_in))
    dh = _mm(dproj, w_in, "nt", f32, "proj_dx", 1024, 1024, 1664)
    dx, g_norm1 = _rms_bwd(x2d, p["norm1_g"] + tok3[0, 0], dh, "norm1_bwd", residual=dx1)

    gs = dict(w_spatial=g_w_sp, norm1_g=g_norm1, mem_norm_g=g_mem_norm, norm2_g=g_norm2, final_g=g_final, lb_logits=g_lbl,
              ln_v_g=g_ln_g, ln_v_b=g_ln_b, b_spatial=g_b_sp, hgrn_norm_g=g_ng, conv_b=g_conv_b)
    return loss_part, dx.reshape(Bl, S, Dd), gs


def _coords():
    return lax.axis_index("x"), lax.axis_index("y"), lax.axis_index("c")


def _slot(dev):
    return 4 * dev[0] + 2 * dev[1] + dev[2]


def _comm_call(body, name, arrays, out_shapes, n_sem):
    n = len(arrays)
    hbm = pl.BlockSpec(memory_space=pl.ANY)
    return pl.pallas_call(
        body, name=name, out_shape=out_shapes, in_specs=[hbm] * n, out_specs=[hbm] * n,
        scratch_shapes=[pltpu.SemaphoreType.DMA((n_sem, n)), pltpu.SemaphoreType.DMA((n_sem, n)), pltpu.SemaphoreType.DMA((n,))])(*arrays)


def _all_gather(blocks, name):
    n = len(blocks)

    def body(*refs):
        x_refs, o_refs, (send_sems, recv_sems, local_sems) = refs[:n], refs[n:2 * n], refs[2 * n:]
        x, y, c = _coords()
        me, sibling = (x, y, c), (x, y, 1 - c)
        chips = [(1 - x, y), (x, 1 - y), (1 - x, 1 - y)]

        def copy(a, k, block_dev, to, from_input=False):
            dst = o_refs[a].at[_slot(block_dev)]
            return pltpu.make_async_remote_copy(src_ref=x_refs[a] if from_input else dst, dst_ref=dst, send_sem=send_sems.at[k, a],
                                                recv_sem=recv_sems.at[k, a], device_id=to, device_id_type=MESH)

        mine = [pltpu.make_async_copy(x_refs[a], o_refs[a].at[_slot(me)], local_sems.at[a]) for a in range(n)]
        first = [copy(a, 0, me, sibling, True) for a in range(n)]
        first += [copy(a, 1 + j, me, (*chip, c), True) for j, chip in enumerate(chips) for a in range(n)]
        for cp in mine + first:
            cp.start()
        passed = []
        for j, chip in enumerate(chips):
            for a in range(n):
                copy(a, 1 + j, (*chip, c), me).wait_recv()
                fwd = copy(a, 4 + j, (*chip, c), sibling)
                fwd.start()
                passed.append(fwd)
        for a in range(n):
            copy(a, 0, sibling, me).wait_recv()
        for j, chip in enumerate(chips):
            for a in range(n):
                copy(a, 4 + j, (*chip, 1 - c), me).wait_recv()
        for cp in first + passed:
            cp.wait_send()
        for cp in mine:
            cp.wait()

    return _comm_call(body, name, blocks, [jax.ShapeDtypeStruct((N_DEV,) + b.shape, b.dtype) for b in blocks], 7)


def _all_to_all(parts, name):
    n = len(parts)
    rel = [(0, 0, 1), (0, 1, 0), (0, 1, 1), (1, 0, 0), (1, 0, 1), (1, 1, 0), (1, 1, 1)]

    def body(*refs):
        x_refs, o_refs, (send_sems, recv_sems, local_sems) = refs[:n], refs[n:2 * n], refs[2 * n:]
        x, y, c = _coords()
        me = (x, y, c)
        peers = [(x ^ dx, y ^ dy, c ^ dc) for dx, dy, dc in rel]

        def copy(a, k, peer):
            return pltpu.make_async_remote_copy(src_ref=x_refs[a].at[_slot(peer)], dst_ref=o_refs[a].at[_slot(me)], send_sem=send_sems.at[k, a],
                                                recv_sem=recv_sems.at[k, a], device_id=peer, device_id_type=MESH)

        def arrival(a, k, peer):
            return pltpu.make_async_remote_copy(src_ref=x_refs[a].at[_slot(me)], dst_ref=o_refs[a].at[_slot(peer)], send_sem=send_sems.at[k, a],
                                                recv_sem=recv_sems.at[k, a], device_id=peer, device_id_type=MESH)

        mine = [pltpu.make_async_copy(x_refs[a].at[_slot(me)], o_refs[a].at[_slot(me)], local_sems.at[a]) for a in range(n)]
        sends = [copy(a, k, peer) for k, peer in enumerate(peers) for a in range(n)]
        for cp in mine + sends:
            cp.start()
        for k, peer in enumerate(peers):
            for a in range(n):
                arrival(a, k, peer).wait_recv()
        for cp in sends:
            cp.wait_send()
        for cp in mine:
            cp.wait()

    return _comm_call(body, name, parts, [jax.ShapeDtypeStruct(p.shape, p.dtype) for p in parts], 7)


_HBM = pl.BlockSpec(memory_space=pltpu.HBM)
_SEM = pl.BlockSpec(memory_space=pltpu.SEMAPHORE)
_REL = [(0, 0, 1), (0, 1, 0), (0, 1, 1), (1, 0, 0), (1, 0, 1), (1, 1, 0), (1, 1, 1)]


def _split_copies(gather, src, land, send, recv, arrivals):
    x, y, c = _coords()
    me = (x, y, c)
    copies = []
    for a in range(len(src)):
        for k, (dx, dy, dc) in enumerate(_REL):
            peer = (x ^ dx, y ^ dy, c ^ dc)
            mine = src[a] if gather else src[a].at[_slot(peer)]
            copies.append(pltpu.make_async_remote_copy(
                src_ref=mine, dst_ref=land[a].at[_slot(peer if arrivals else me)], send_sem=send[a].at[k], recv_sem=recv[a].at[k],
                device_id=peer, device_id_type=MESH))
    return me, copies


def _exchange_start(arrays, gather, name, after=None):
    n = len(arrays)
    e = 0 if after is None else 1
    lands = [lax.empty(((N_DEV,) + a.shape) if gather else a.shape, a.dtype) for a in arrays]

    def body(*refs):
        src, land = refs[:n], refs[n:2 * n]
        refs = refs[2 * n + e:]
        send, recv, token, local_sems = refs[:n], refs[n:2 * n], refs[4 * n], refs[4 * n + 1]
        me, out = _split_copies(gather, src, land, send, recv, False)
        local = [pltpu.make_async_copy(src[a] if gather else src[a].at[_slot(me)], land[a].at[_slot(me)], local_sems.at[a])
                 for a in range(n)]
        for cp in local:
            cp.start()
        for cp in local:
            cp.wait()
        for cp in out:
            cp.start()
        token[...] = jnp.zeros_like(token)

    sems = [pltpu.SemaphoreType.DMA((7,)) for _ in range(2 * n)]
    outs = pl.pallas_call(
        body, name=name,
        out_shape=(*sems, *[pltpu.HBM(a.shape, a.dtype) for a in arrays], *[pltpu.HBM(l.shape, l.dtype) for l in lands],
                   jax.ShapeDtypeStruct((8, 128), f32)),
        in_specs=[_HBM] * (2 * n) + [pl.BlockSpec(memory_space=pl.ANY)] * e,
        out_specs=(*[_SEM] * (2 * n), *[_HBM] * (2 * n), pl.BlockSpec(memory_space=pltpu.VMEM)),
        input_output_aliases={i: 2 * n + i for i in range(2 * n)},
        scratch_shapes=[pltpu.SemaphoreType.DMA((n,))],
        compiler_params=pltpu.CompilerParams(has_side_effects=pltpu.SideEffectType.DATAFLOW_SIDE_EFFECTING))(
        *[pltpu.with_memory_space_constraint(a, pltpu.HBM) for a in arrays],
        *[pltpu.with_memory_space_constraint(l, pltpu.HBM) for l in lands], *([after] if e else []))
    return (gather, n, outs[:4 * n]), outs[4 * n]


def _exchange_wait(handle, which, after, name):
    gather, n_all, vals = handle
    send_v, recv_v, src_v, land_v = [[vals[g * n_all + i] for i in which] for g in range(4)]
    n = len(which)

    def body(*refs):
        src, land, send, recv = refs[:n], refs[n:2 * n], refs[2 * n:3 * n], refs[3 * n:4 * n]
        for cp in _split_copies(gather, src, land, send, recv, False)[1]:
            cp.wait_send()
        for cp in _split_copies(gather, src, land, send, recv, True)[1]:
            cp.wait_recv()

    outs = pl.pallas_call(
        body, name=name,
        out_shape=(*[pltpu.HBM(a.shape, a.dtype) for a in src_v], *[pltpu.HBM(l.shape, l.dtype) for l in land_v]),
        in_specs=[*[_HBM] * (2 * n), *[_SEM] * (2 * n), pl.BlockSpec(memory_space=pl.ANY)], out_specs=[_HBM] * (2 * n),
        input_output_aliases={i: i for i in range(2 * n)},
        compiler_params=pltpu.CompilerParams(has_side_effects=pltpu.SideEffectType.DATAFLOW_SIDE_EFFECTING))(
        *src_v, *land_v, *send_v, *recv_v, after)
    return outs[n:]


def _adam_math(w, g, m, v):
    m_ = ADAM_B1 * m + (1.0 - ADAM_B1) * g
    v_ = ADAM_B2 * v + (1.0 - ADAM_B2) * jnp.square(g)
    m_hat = m_ / (1.0 - ADAM_B1 ** ADAM_STEP)
    v_hat = v_ / (1.0 - ADAM_B2 ** ADAM_STEP)
    return -ADAM_LR * (m_hat / (jnp.sqrt(v_hat) + ADAM_EPS) + ADAM_WD * w), m_, v_


def _reduce_adamw(parts, w, m, v, name):
    _, R, L = parts.shape
    tr = _pick(R, (256, 128, 64, 32, 16, 8))

    def body(p_ref, w_ref, m_ref, v_ref, g_ref, d_ref, nm_ref, nv_ref):
        g = p_ref[0].astype(f32)
        for i in range(1, N_DEV):
            g = g + p_ref[i].astype(f32)
        g_ref[...] = g
        d_ref[...], nm_ref[...], nv_ref[...] = _adam_math(w_ref[...], g, m_ref[...], v_ref[...])

    blk = pl.BlockSpec((tr, L), lambda i: (i, 0))
    sh = jax.ShapeDtypeStruct((R, L), f32)
    return _call(body, name, (R // tr,), [pl.BlockSpec((N_DEV, tr, L), lambda i: (0, i, 0)), blk, blk, blk], (blk,) * 4, (sh,) * 4,
                 sem=("parallel",))(parts, w, m, v)


SMALL = (("w_spatial", (512, 128), 0), ("norm1_g", (1, 1024), 512), ("mem_norm_g", (1, 1024), 520), ("norm2_g", (1, 1024), 528),
         ("final_g", (1, 1024), 536), ("lb_logits", (2, 512), 544), ("ln_v_g", (1, 512), 552), ("ln_v_b", (1, 512), 556),
         ("b_spatial", (4, 128), 560), ("hgrn_norm_g", (1, 128), 564), ("conv_b", (1, 2816), 565))
SMALL_USED, SMALL_ROWS = 587, 640


def _segments(shape, base):
    r, n = shape
    per = n // 128
    return [(base + i * per + j, i, slice(j * 128, (j + 1) * 128)) for i in range(r) for j in range(per)]


def _pack_small(gs):
    names = [n for n, _, _ in SMALL]

    def body(*refs):
        src, o_ref = dict(zip(names, refs[:-1])), refs[-1]
        o_ref[SMALL_USED:SMALL_ROWS, :] = jnp.zeros((SMALL_ROWS - SMALL_USED, 128), f32)
        for name, shape, base in SMALL:
            ref = src[name]
            if name == "w_spatial":
                o_ref[base:base + 512, :] = ref[...].reshape(512, 128)
            elif name == "b_spatial":
                o_ref[base:base + 4, :] = ref[0:4, :]
            elif name == "hgrn_norm_g":
                per_head = [ref[b, h] for b in range(ref.shape[0]) for h in range(N_HEAD)]
                o_ref[base:base + 1, :] = functools.reduce(lambda u, v_: u + v_, per_head)
            else:
                for row, i, sl in _segments(shape, base):
                    o_ref[row:row + 1, :] = ref[i:i + 1, sl]

    return pl.pallas_call(body, name="pack_small", out_shape=jax.ShapeDtypeStruct((SMALL_ROWS, 128), f32))(*[gs[n] for n in names])


def _small_update(gathered, w, m, v):
    names = [n for n, _, _ in SMALL]
    k = len(names)

    def body(*refs):
        p_ref = refs[0]
        ins = [dict(zip(names, refs[1 + i * k:1 + (i + 1) * k])) for i in range(3)]
        outs = [dict(zip(names, refs[1 + (3 + i) * k:1 + (4 + i) * k])) for i in range(4)]
        gsum = refs[-1]
        g = p_ref[0]
        for i in range(1, N_DEV):
            g = g + p_ref[i]
        gsum[...] = g
        for name, shape, base in SMALL:
            if name == "w_spatial":
                where = [(slice(base, base + 512), (slice(None), slice(None)))]
            else:
                where = [(slice(row, row + 1), (slice(i, i + 1), sl)) for row, i, sl in _segments(shape, base)]
            for rows, at in where:
                g_ = gsum[rows, :]
                d_, m_, v_ = _adam_math(ins[0][name][at], g_, ins[1][name][at], ins[2][name][at])
                for o, val in zip(outs, (g_, d_, m_, v_)):
                    o[name][at] = val

    args = [gathered] + [d[n] for d in (w, m, v) for n in names]
    out_shapes = [jax.ShapeDtypeStruct(shape, f32) for _ in range(4) for _, shape, _ in SMALL]
    outs = pl.pallas_call(body, name="small_update", out_shape=out_shapes, scratch_shapes=[pltpu.VMEM((SMALL_ROWS, 128), f32)])(*args)
    return [dict(zip(names, outs[i * k:(i + 1) * k])) for i in range(4)]


def _cols_full(g):
    return jnp.moveaxis(g, 0, -2).reshape(g.shape[1:-1] + (N_DEV * g.shape[-1],))


def _cols_parts(full):
    n = full.shape[-1] // N_DEV
    return jnp.moveaxis(full.reshape(full.shape[:-1] + (N_DEV, n)), -2, 0)


def kernel(x, mem, norm1_g, w_in, ln_v_g, ln_v_b, w_spatial, b_spatial, lb_logits, hgrn_norm_g, mem_norm_g, w_mem_kv, w_branch, w_out, norm2_g, w_up, conv_w, conv_b, w_down, final_g, loss_target, m_norm1_g, m_w_in, m_ln_v_g, m_ln_v_b, m_w_spatial, m_b_spatial, m_lb_logits, m_hgrn_norm_g, m_mem_norm_g, m_w_mem_kv, m_w_branch, m_w_out, m_norm2_g, m_w_up, m_conv_w, m_conv_b, m_w_down, m_final_g, v_norm1_g, v_w_in, v_ln_v_g, v_ln_v_b, v_w_spatial, v_b_spatial, v_lb_logits, v_hgrn_norm_g, v_mem_norm_g, v_w_mem_kv, v_w_branch, v_w_out, v_norm2_g, v_w_up, v_conv_w, v_conv_b, v_w_down, v_final_g):
    given = dict(locals())
    order = ("norm1_g", "w_in", "ln_v_g", "ln_v_b", "w_spatial", "b_spatial", "lb_logits", "hgrn_norm_g", "mem_norm_g",
             "w_mem_kv", "w_branch", "w_out", "norm2_g", "w_up", "conv_w", "conv_b", "w_down", "final_g")
    groups = dict(a=("w_in",), b=("w_mem_kv", "w_branch", "w_out"), c=("w_up", "conv_w", "w_down"))

    wire = {n: given[n][0].astype(f32 if n == "conv_w" else bf16) for ns in groups.values() for n in ns}
    g_in = _all_gather([wire["w_in"]], "gather_w_in")[0]
    late = groups["b"] + groups["c"]
    gather, tok = _exchange_start([wire[n] for n in late], True, "gather_rest_start", after=g_in)

    def late_b(after):
        kv_, br_, out_ = _exchange_wait(gather, (0, 1, 2), after, "gather_b_wait")
        br_ = _cols_full(br_)
        return dict(w_mem_kv=kv_.reshape(D_MODEL, 2 * 512), w_branch=[br_[n] for n in range(3)], w_out=out_.reshape(D_MODEL, D_MODEL))

    def late_c(after):
        up_, cw_, down_ = _exchange_wait(gather, (3, 4, 5), after, "gather_c_wait")
        return dict(w_up=_cols_full(up_), conv_w=_cols_full(cw_), w_down=down_.reshape(D_FF, D_MODEL))

    to_parts = dict(w_in=_cols_parts, w_up=_cols_parts, conv_w=_cols_parts,
                    w_branch=lambda g_: _cols_parts(jnp.stack(g_)).reshape(N_DEV, -1, 128),
                    w_mem_kv=lambda g_: g_.reshape(N_DEV, -1, 2 * 512), w_out=lambda g_: g_.reshape(N_DEV, -1, D_MODEL),
                    w_down=lambda g_: g_.reshape(N_DEV, -1, D_MODEL))
    scatters = {}

    def send(tag, grads_):
        scatters[tag], tok_ = _exchange_start([to_parts[n](grads_[n]) for n in groups[tag]], False, f"scatter_{tag}_start")
        return tok_

    small_2d = lambda prefix: {n: given[prefix + n].reshape(shape) for n, shape, _ in SMALL}
    p = small_2d("")
    p["w_spatial"] = w_spatial[0]
    loss_part, grad_x, gs = _local_step(x, mem, loss_target, p, _cols_full(g_in), tok, late_b, late_c, send)
    loss = lax.psum(loss_part[0, 0], ("x", "y", "c"))

    small_gather, _ = _exchange_start([_pack_small(gs)], True, "gather_small_start")

    grads, delta, new_m, new_v = {}, {}, {}, {}
    after = grad_x
    for tag in ("c", "b", "a"):
        recv = _exchange_wait(scatters[tag], tuple(range(len(groups[tag]))), after, f"scatter_{tag}_wait")
        for n, parts in zip(groups[tag], recv):
            two_d = (-1, given[n].shape[-1])
            res = _reduce_adamw(parts, *[given[pre + n].reshape(two_d) for pre in ("", "m_", "v_")], "adamw_" + n)
            grads[n], delta[n], new_m[n], new_v[n] = [r.reshape(given[n].shape) for r in res]
            after = res[0]

    gathered = _exchange_wait(small_gather, (0,), after, "gather_small_wait")[0]
    for dst, res in zip((grads, delta, new_m, new_v), _small_update(gathered, small_2d(""), small_2d("m_"), small_2d("v_"))):
        for n, _, _ in SMALL:
            dst[n] = res[n].reshape(given[n].shape)

    return (loss, grad_x, *[grads[n] for n in order], *[delta[n] for n in order], *[new_m[n] for n in order],
            *[new_v[n] for n in order])
```

```python
import functools

import jax
import jax.numpy as jnp
from jax import lax
from jax.experimental import pallas as pl
from jax.experimental.pallas import tpu as pltpu

f32 = jnp.float32
bf16 = jnp.bfloat16

N_DEV = 8
D_MODEL = 1024
EPS = 1e-6
GM_CHUNK = 128
HG_CHUNK = 64
HEAD = 128
N_HEAD = 4
MEM_LEN = 256
D_FF = 2816
IN_WIDTH = 6656
C_ZU, C_HQ, C_HF, C_HI, C_HG, C_XQ, C_GL = 0, 1024, 1536, 2048, 2560, 3072, 3584
ADAM_LR, ADAM_B1, ADAM_B2, ADAM_EPS, ADAM_WD, ADAM_STEP = 0.001, 0.9, 0.999, 1e-08, 0.01, 10
VMEM_LIMIT = 56 * 1024 * 1024
MESH = pl.DeviceIdType.MESH


def _pick(n, cands):
    for c in cands:
        if n % c == 0:
            return c
    return n


def _call(body, name, grid, in_specs, out_specs, out_shape, scratch=(), sem=None, **cp):
    params = dict(vmem_limit_bytes=VMEM_LIMIT, **cp)
    if sem is not None:
        params["dimension_semantics"] = sem
    return pl.pallas_call(
        body, name=name, grid=grid, in_specs=in_specs, out_specs=out_specs, out_shape=out_shape,
        scratch_shapes=list(scratch), compiler_params=pltpu.CompilerParams(**params))


_DN = {"nn": (((1,), (0,)), ((), ())), "nt": (((1,), (1,)), ((), ())), "tn": (((0,), (0,)), ((), ()))}


def _raw_dot(a, b, mode):
    return lax.dot_general(a.astype(bf16), b.astype(bf16), _DN[mode], preferred_element_type=f32)


@jax.custom_vjp
def _dot_nn(a, b):
    return _raw_dot(a, b, "nn")


_dot_nn.defvjp(lambda a, b: (_raw_dot(a, b, "nn"), (a, b)),
               lambda r, g: (_raw_dot(g, r[1], "nt"), _raw_dot(r[0], g, "tn")))


@jax.custom_vjp
def _dot_nt(a, b):
    return _raw_dot(a, b, "nt")


_dot_nt.defvjp(lambda a, b: (_raw_dot(a, b, "nt"), (a, b)),
               lambda r, g: (_raw_dot(g, r[1], "nn"), _raw_dot(g, r[0], "tn")))


@jax.custom_vjp
def _dot_tn(a, b):
    return _raw_dot(a, b, "tn")


_dot_tn.defvjp(lambda a, b: (_raw_dot(a, b, "tn"), (a, b)),
               lambda r, g: (_raw_dot(r[1], g, "nt"), _raw_dot(r[0], g, "nn")))


def _tri(n, lower):
    r = lax.broadcasted_iota(jnp.int32, (n, n), 0)
    c = lax.broadcasted_iota(jnp.int32, (n, n), 1)
    return ((c <= r) if lower else (c >= r)).astype(f32)


def _sel_dot(sel, x, mode, x_first=False):
    hi = x.astype(bf16)
    rest = x - hi.astype(f32)
    mid = rest.astype(bf16)
    lo = (rest - mid.astype(f32)).astype(bf16)
    sel = sel.astype(bf16)
    dot = lambda piece: lax.dot_general(*((piece, sel) if x_first else (sel, piece)), _DN[mode], preferred_element_type=f32)
    return dot(hi) + dot(mid) + dot(lo)


def _egrad(fn, x, ct):
    return jax.vjp(fn, x)[1](ct)[0]


def _mm(a, b, mode, out_dtype, name, tm, tn, tk=None, residual=None):
    if mode == "nn":
        (M, K), (_, N) = a.shape, b.shape
    elif mode == "nt":
        (M, K), (N, _) = a.shape, b.shape
    else:
        (K, M), (_, N) = a.shape, b.shape
    tm, tn = min(tm, M), min(tn, N)
    tk = K if tk is None else min(tk, K)
    assert M % tm == 0 and N % tn == 0 and K % tk == 0, (name, M, N, K, tm, tn, tk)
    nk = K // tk

    def body(*refs):
        acc_ref = refs[-1] if nk > 1 else None
        refs = refs[:-1] if nk > 1 else refs
        if residual is None:
            a_ref, b_ref, o_ref = refs
        else:
            a_ref, b_ref, r_ref, o_ref = refs

        def finish(r):
            if residual is not None:
                r = r + r_ref[...]
            o_ref[...] = r.astype(out_dtype)

        part = _raw_dot(a_ref[...], b_ref[...], mode)
        if nk == 1:
            finish(part)
            return
        k = pl.program_id(2)

        @pl.when(k == 0)
        def _():
            acc_ref[...] = part

        @pl.when((k > 0) & (k < nk - 1))
        def _():
            acc_ref[...] += part

        @pl.when(k == nk - 1)
        def _():
            finish(acc_ref[...] + part)

    a_spec = {"nn": pl.BlockSpec((tm, tk), lambda i, j, k: (i, k)),
              "nt": pl.BlockSpec((tm, tk), lambda i, j, k: (i, k)),
              "tn": pl.BlockSpec((tk, tm), lambda i, j, k: (k, i))}[mode]
    b_spec = {"nn": pl.BlockSpec((tk, tn), lambda i, j, k: (k, j)),
              "nt": pl.BlockSpec((tn, tk), lambda i, j, k: (j, k)),
              "tn": pl.BlockSpec((tk, tn), lambda i, j, k: (k, j))}[mode]
    o_spec = pl.BlockSpec((tm, tn), lambda i, j, k: (i, j))
    in_specs = [a_spec, b_spec] + ([o_spec] if residual is not None else [])
    args = (a, b) + ((residual,) if residual is not None else ())
    return _call(body, name, (M // tm, N // tn, nk), in_specs, o_spec, jax.ShapeDtypeStruct((M, N), out_dtype),
                 scratch=[pltpu.VMEM((tm, tn), f32)] if nk > 1 else [], sem=("parallel", "parallel", "arbitrary"))(*args)


def _rms_fwd(x, g, name):
    R, Dd = x.shape
    tr = _pick(R, (512, 256, 128))

    def body(x_ref, g_ref, o_ref):
        xf = x_ref[...]
        y = xf * lax.rsqrt(jnp.mean(xf * xf, axis=-1, keepdims=True) + EPS)
        o_ref[...] = (y * g_ref[...]).astype(bf16)

    return _call(body, name, (R // tr,), [pl.BlockSpec((tr, Dd), lambda i: (i, 0)), pl.BlockSpec((1, Dd), lambda i: (0, 0))],
                 pl.BlockSpec((tr, Dd), lambda i: (i, 0)), jax.ShapeDtypeStruct((R, Dd), bf16), sem=("parallel",))(x, g)


def _rms_bwd(x, g, dh, name, residual=None):
    R, Dd = x.shape
    tr = _pick(R, (512, 256, 128))

    def body(*refs):
        if residual is None:
            x_ref, g_ref, dh_ref, dx_ref, dg_ref = refs
        else:
            x_ref, g_ref, dh_ref, r_ref, dx_ref, dg_ref = refs
        xf = x_ref[...]
        rs = lax.rsqrt(jnp.mean(xf * xf, axis=-1, keepdims=True) + EPS)
        y = xf * rs
        dh_ = dh_ref[...].astype(f32)
        dy = dh_ * g_ref[...]
        dx = rs * (dy - y * jnp.mean(dy * y, axis=-1, keepdims=True))
        if residual is not None:
            dx = dx + r_ref[...]
        dx_ref[...] = dx

        @pl.when(pl.program_id(0) == 0)
        def _():
            dg_ref[...] = jnp.zeros_like(dg_ref)

        dg_ref[...] += jnp.sum(dh_ * y, axis=0, keepdims=True)

    row = pl.BlockSpec((tr, Dd), lambda i: (i, 0))
    vec = pl.BlockSpec((1, Dd), lambda i: (0, 0))
    in_specs = [row, vec, row] + ([row] if residual is not None else [])
    args = (x, g, dh) + ((residual,) if residual is not None else ())
    return _call(body, name, (R // tr,), in_specs, (row, vec),
                 (jax.ShapeDtypeStruct((R, Dd), f32), jax.ShapeDtypeStruct((1, Dd), f32)), sem=("arbitrary",))(*args)


def _final_loss(x2, g, target):
    R, Dd = x2.shape
    tr = _pick(R, (512, 256, 128))

    def body(x_ref, g_ref, t_ref, loss_ref, dx_ref, dg_ref):
        xf = x_ref[...]
        rs = lax.rsqrt(jnp.mean(xf * xf, axis=-1, keepdims=True) + EPS)
        y = xf * rs
        err = y * g_ref[...] - t_ref[...]
        dh_ = err * (1.0 / Dd)
        dy = dh_ * g_ref[...]
        dx_ref[...] = rs * (dy - y * jnp.mean(dy * y, axis=-1, keepdims=True))

        @pl.when(pl.program_id(0) == 0)
        def _():
            dg_ref[...] = jnp.zeros_like(dg_ref)
            loss_ref[...] = jnp.zeros_like(loss_ref)

        dg_ref[...] += jnp.sum(dh_ * y, axis=0, keepdims=True)
        part = jnp.sum(jnp.mean(err * err, axis=-1, keepdims=True), axis=0, keepdims=True)
        loss_ref[...] += 0.5 * part

    row = pl.BlockSpec((tr, Dd), lambda i: (i, 0))
    vec = pl.BlockSpec((1, Dd), lambda i: (0, 0))
    return _call(body, "final_loss", (R // tr,), [row, vec, row], (pl.BlockSpec((1, 128), lambda i: (0, 0)), row, vec),
                 (jax.ShapeDtypeStruct((1, 128), f32), jax.ShapeDtypeStruct((R, Dd), f32), jax.ShapeDtypeStruct((1, Dd), f32)),
                 sem=("arbitrary",))(x2, g, target)


def _gmlp_parts(zuv, ln_g, ln_b):
    zu, zv = zuv[:, :512], zuv[:, 512:]
    u = jax.nn.gelu(zu)
    v = jax.nn.gelu(zv)
    mu = jnp.mean(v, axis=-1, keepdims=True)
    rs = lax.rsqrt(jnp.mean(jnp.square(v - mu), axis=-1, keepdims=True) + EPS)
    xh = (v - mu) * rs
    return zu, zv, u, xh, rs, xh * ln_g + ln_b


def _gmlp_fwd(proj, ln_g, ln_b, w_s, b_st):
    T = proj.shape[0]

    def body(p_ref, g_ref, b_ref, w_ref, bs_ref, o_ref):
        _, _, u, _, _, vn = _gmlp_parts(p_ref[...], g_ref[...], b_ref[...])
        causal = _tri(GM_CHUNK, True) > 0
        for gi in range(N_HEAD):
            sl = slice(gi * HEAD, (gi + 1) * HEAD)
            w = jnp.where(causal, w_ref[gi], 0.0)
            mixed = _raw_dot(w, vn[:, sl], "nn") + bs_ref[:, gi:gi + 1]
            o_ref[:, sl] = (u[:, sl] * mixed).astype(bf16)

    vec = pl.BlockSpec((1, 512), lambda i: (0, 0))
    return _call(body, "gmlp_fwd", (T // GM_CHUNK,),
                 [pl.BlockSpec((GM_CHUNK, 1024), lambda i: (i, 0)), vec, vec,
                  pl.BlockSpec((N_HEAD, GM_CHUNK, GM_CHUNK), lambda i: (0, 0, 0)), pl.BlockSpec((GM_CHUNK, 128), lambda i: (0, 0))],
                 pl.BlockSpec((GM_CHUNK, 512), lambda i: (i, 0)), jax.ShapeDtypeStruct((T, 512), bf16), sem=("parallel",))(
        proj, ln_g, ln_b, w_s, b_st)


def _gmlp_bwd(proj, ln_g, ln_b, w_s, b_st, da):
    T = proj.shape[0]

    def body(p_ref, g_ref, b_ref, w_ref, bs_ref, da_ref, dp_ref, dg_ref, db_ref, dw_ref, dbs_ref):
        zu, zv, u, xh, rs, vn = _gmlp_parts(p_ref[...], g_ref[...], b_ref[...])
        causal = _tri(GM_CHUNK, True) > 0
        sub = lax.broadcasted_iota(jnp.int32, (8, GM_CHUNK), 0)
        ones = jnp.ones((8, HEAD), f32)
        dout = da_ref[...].astype(f32)

        @pl.when(pl.program_id(0) == 0)
        def _():
            for r in (dg_ref, db_ref, dw_ref, dbs_ref):
                r[...] = jnp.zeros_like(r)

        du, dvn, dbs = [], [], jnp.zeros((8, GM_CHUNK), f32)
        for gi in range(N_HEAD):
            sl = slice(gi * HEAD, (gi + 1) * HEAD)
            w = jnp.where(causal, w_ref[gi], 0.0)
            mixed = _raw_dot(w, vn[:, sl], "nn") + bs_ref[:, gi:gi + 1]
            du.append(dout[:, sl] * mixed)
            dm = dout[:, sl] * u[:, sl]
            row_sums = _sel_dot(ones, dm, "nt")
            dbs = dbs + jnp.where(sub == gi, row_sums, 0.0)
            dw_ref[gi] += jnp.where(causal, _raw_dot(dm, vn[:, sl], "nt"), 0.0)
            dvn.append(_raw_dot(w, dm, "tn"))
        dbs_ref[...] += dbs
        du = jnp.concatenate(du, axis=-1)
        dvn = jnp.concatenate(dvn, axis=-1)
        dg_ref[...] += jnp.sum(dvn * xh, axis=0, keepdims=True)
        db_ref[...] += jnp.sum(dvn, axis=0, keepdims=True)
        dxh = dvn * g_ref[...]
        dv = rs * (dxh - jnp.mean(dxh, axis=-1, keepdims=True) - xh * jnp.mean(dxh * xh, axis=-1, keepdims=True))
        dp_ref[:, :512] = _egrad(jax.nn.gelu, zu, du).astype(bf16)
        dp_ref[:, 512:] = _egrad(jax.nn.gelu, zv, dv).astype(bf16)

    vec = pl.BlockSpec((1, 512), lambda i: (0, 0))
    wsp = pl.BlockSpec((N_HEAD, GM_CHUNK, GM_CHUNK), lambda i: (0, 0, 0))
    return _call(body, "gmlp_bwd", (T // GM_CHUNK,),
                 [pl.BlockSpec((GM_CHUNK, 1024), lambda i: (i, 0)), vec, vec, wsp, pl.BlockSpec((GM_CHUNK, 128), lambda i: (0, 0)),
                  pl.BlockSpec((GM_CHUNK, 512), lambda i: (i, 0))],
                 (pl.BlockSpec((GM_CHUNK, 1024), lambda i: (i, 0)), vec, vec, wsp, pl.BlockSpec((8, GM_CHUNK), lambda i: (0, 0))),
                 (jax.ShapeDtypeStruct((T, 1024), bf16), jax.ShapeDtypeStruct((1, 512), f32), jax.ShapeDtypeStruct((1, 512), f32),
                  jax.ShapeDtypeStruct((N_HEAD, GM_CHUNK, GM_CHUNK), f32), jax.ShapeDtypeStruct((8, GM_CHUNK), f32)),
                 sem=("arbitrary",))(proj, ln_g, ln_b, w_s, b_st, da)


HG_SUB = 16
HG_NSUB = HG_CHUNK // HG_SUB


def _two_level_matrix():
    r = lax.broadcasted_iota(jnp.int32, (2 * HG_CHUNK, HG_CHUNK), 0)
    c = lax.broadcasted_iota(jnp.int32, (2 * HG_CHUNK, HG_CHUNK), 1)
    t = jnp.where(r < HG_CHUNK, r, r - HG_CHUNK)
    local = (r < HG_CHUNK) & (t // HG_SUB == c // HG_SUB) & (c <= t)
    before = (r >= HG_CHUNK) & (c < (t // HG_SUB) * HG_SUB)
    return (local | before).astype(f32)


def _two_level_sums(x):
    two = _sel_dot(_two_level_matrix(), x, "nn")
    return two[:HG_CHUNK], two[HG_CHUNK:]


@jax.custom_vjp
def _two_level_cumsum(x):
    return _two_level_sums(x)


_two_level_cumsum.defvjp(
    lambda x: (_two_level_sums(x), None),
    lambda _, g: (_sel_dot(_two_level_matrix(), jnp.concatenate(g, axis=0), "tn"),))


def _tile_matrix():
    s = lax.broadcasted_iota(jnp.int32, (HG_SUB, HG_CHUNK), 0)
    j = lax.broadcasted_iota(jnp.int32, (HG_SUB, HG_CHUNK), 1)
    return (j % HG_SUB == s).astype(f32)


@jax.custom_vjp
def _tile_lanes(x):
    return _sel_dot(_tile_matrix(), x, "nn", x_first=True)


_tile_lanes.defvjp(
    lambda x: (_sel_dot(_tile_matrix(), x, "nn", x_first=True), None),
    lambda _, g: (_sel_dot(_tile_matrix(), g, "nt", x_first=True),))


def _block_rows(x):
    k = x.shape[-1]
    return jnp.broadcast_to(x.reshape(HG_NSUB, 1, HG_SUB, k), (HG_NSUB, HG_SUB, HG_SUB, k)).reshape(HG_CHUNK, HG_SUB, k)


def _hgrn_chunk(st0, q_raw, f_raw, i_raw, g_raw, l0, l1, ng):
    C, SUB = HG_CHUNK, HG_SUB
    lb = jax.nn.sigmoid(l0 - l1)
    fg = lb + (1.0 - lb) * jax.nn.sigmoid(f_raw)
    kk = 1.0 - fg
    qf = jax.nn.silu(q_raw)
    al, base = _two_level_cumsum(jnp.log(fg))
    a = al + base
    row = lax.broadcasted_iota(jnp.int32, (C, HEAD), 0)
    a_last = jnp.sum(jnp.where(row == C - 1, a, 0.0), axis=0, keepdims=True)
    inter = _dot_nt(qf * jnp.exp(a), st0)
    qt = qf * jnp.exp(al)
    rb = lax.broadcasted_iota(jnp.int32, (C, C), 0) // SUB
    cb = lax.broadcasted_iota(jnp.int32, (C, C), 1) // SUB
    scores = jnp.zeros((C, C), f32)
    for i in range(1, HG_NSUB):
        base_i = jnp.sum(jnp.where(row == i * SUB, base, 0.0), axis=0, keepdims=True)
        kt = kk * jnp.exp(jnp.minimum(base_i - a, 0.0))
        scores = scores + jnp.where((rb == i) & (cb < i), _dot_nt(qt, kt), 0.0)
    t_i = lax.broadcasted_iota(jnp.int32, (C, SUB, HEAD), 0) % SUB
    s_i = lax.broadcasted_iota(jnp.int32, (C, SUB, HEAD), 1)
    decay = jnp.exp(jnp.where(s_i <= t_i, al[:, None, :] - _block_rows(al), -jnp.inf))
    diag = jnp.sum(qf[:, None, :] * decay * _block_rows(kk), axis=-1)
    scores = scores + jnp.where(rb == cb, _tile_lanes(diag), 0.0)
    o = inter + _dot_nn(scores, i_raw)
    st1 = jnp.exp(a_last) * st0 + _dot_tn(i_raw, kk * jnp.exp(a_last - a))
    on = o * lax.rsqrt(jnp.mean(o * o, axis=-1, keepdims=True) + EPS) * ng
    return st1, on * jax.nn.silu(g_raw)


def _hgrn_specs(S, Bl, rev):
    N = S // HG_CHUNK
    chunk = (lambda n: N - 1 - n) if rev else (lambda n: n)
    col = lambda c0: pl.BlockSpec((Bl, HG_CHUNK, 512), lambda n: (0, chunk(n), c0 // 512))
    st = pl.BlockSpec((Bl, N_HEAD, 1, HEAD, HEAD), lambda n: (0, 0, chunk(n), 0, 0))
    full = lambda *s: pl.BlockSpec(s, functools.partial(lambda n, nd: (0,) * nd, nd=len(s)))
    return N, col, st, full


def _hgrn_fwd(proj, lb_logits, ng, Bl, S):
    N, col, st, full = _hgrn_specs(S, Bl, False)

    def body(q_ref, f_ref, i_ref, g_ref, l_ref, ng_ref, o_ref, st_ref, state):
        @pl.when(pl.program_id(0) == 0)
        def _():
            state[...] = jnp.zeros_like(state)

        for b in range(Bl):
            for h in range(N_HEAD):
                sl = slice(h * HEAD, (h + 1) * HEAD)
                st0 = state[b, h]
                st_ref[b, h, 0] = st0
                st1, out = _hgrn_chunk(st0, q_ref[b, :, sl], f_ref[b, :, sl], i_ref[b, :, sl], g_ref[b, :, sl],
                                       l_ref[0:1, sl], l_ref[1:2, sl], ng_ref[...])
                state[b, h] = st1
                o_ref[b, :, sl] = out.astype(bf16)

    return _call(body, "hgrn_fwd", (N,), [col(C_HQ), col(C_HF), col(C_HI), col(C_HG), full(2, 512), full(1, HEAD)],
                 (col(0), st),
                 (jax.ShapeDtypeStruct((Bl, S, 512), bf16), jax.ShapeDtypeStruct((Bl, N_HEAD, N, HEAD, HEAD), f32)),
                 scratch=[pltpu.VMEM((Bl, N_HEAD, HEAD, HEAD), f32)], sem=("arbitrary",))(
        proj, proj, proj, proj, lb_logits, ng)


def _hgrn_bwd(proj, lb_logits, ng, states, db, Bl, S):
    N, col, st, full = _hgrn_specs(S, Bl, True)

    def body(q_ref, f_ref, i_ref, g_ref, l_ref, ng_ref, st_ref, db_ref,
             dq_ref, df_ref, di_ref, dg_ref, dl_ref, dng_ref, dstate):
        @pl.when(pl.program_id(0) == 0)
        def _():
            dstate[...] = jnp.zeros_like(dstate)
            dl_ref[...] = jnp.zeros_like(dl_ref)
            dng_ref[...] = jnp.zeros_like(dng_ref)

        for b in range(Bl):
            for h in range(N_HEAD):
                sl = slice(h * HEAD, (h + 1) * HEAD)
                _, vjp = jax.vjp(_hgrn_chunk, st_ref[b, h, 0], q_ref[b, :, sl], f_ref[b, :, sl], i_ref[b, :, sl], g_ref[b, :, sl],
                                 l_ref[0:1, sl], l_ref[1:2, sl], ng_ref[...])
                dst0, dq, df, di, dg, dl0, dl1, dng = vjp((dstate[b, h], db_ref[b, :, sl].astype(f32)))
                dstate[b, h] = dst0
                dq_ref[b, :, sl] = dq.astype(bf16)
                df_ref[b, :, sl] = df.astype(bf16)
                di_ref[b, :, sl] = di.astype(bf16)
                dg_ref[b, :, sl] = dg.astype(bf16)
                dl_ref[0:1, sl] += dl0
                dl_ref[1:2, sl] += dl1
                dng_ref[b, h] += dng

    return _call(body, "hgrn_bwd", (N,),
                 [col(C_HQ), col(C_HF), col(C_HI), col(C_HG), full(2, 512), full(1, HEAD), st, col(0)],
                 (*[col(0)] * 4, full(2, 512), full(Bl, N_HEAD, 1, HEAD)),
                 (*[jax.ShapeDtypeStruct((Bl, S, 512), bf16)] * 4, jax.ShapeDtypeStruct((2, 512), f32),
                  jax.ShapeDtypeStruct((Bl, N_HEAD, 1, HEAD), f32)),
                 scratch=[pltpu.VMEM((Bl, N_HEAD, HEAD, HEAD), f32)], sem=("arbitrary",))(
        proj, proj, proj, proj, lb_logits, ng, states, db)


def _attn_probs(q, k):
    s = _raw_dot(q, k, "nt") * (HEAD ** -0.5)
    e = jnp.exp(s - jnp.max(s, axis=-1, keepdims=True))
    return e / jnp.sum(e, axis=-1, keepdims=True)


def _attn_specs(S, tq):
    nq = S // tq
    q = pl.BlockSpec((tq, 512), lambda b, i: (b * nq + i, C_XQ // 512))
    kv = pl.BlockSpec((1, MEM_LEN, 1024), lambda b, i: (b, 0, 0))
    o = pl.BlockSpec((tq, 512), lambda b, i: (b * nq + i, 0))
    return nq, q, kv, o


def _attn_fwd(proj, kv, Bl, S):
    tq = _pick(S, (512, 256, 128))
    nq, qs, kvs, os_ = _attn_specs(S, tq)

    def body(q_ref, kv_ref, o_ref):
        for h in range(N_HEAD):
            sl = slice(h * HEAD, (h + 1) * HEAD)
            p = _attn_probs(q_ref[:, sl], kv_ref[0, :, sl])
            o_ref[:, sl] = _raw_dot(p, kv_ref[0, :, 512 + h * HEAD:512 + (h + 1) * HEAD], "nn").astype(bf16)

    return _call(body, "attn_fwd", (Bl, nq), [qs, kvs], os_, jax.ShapeDtypeStruct((Bl * S, 512), bf16),
                 sem=("parallel", "parallel"))(proj, kv)


def _attn_bwd(proj, kv, dc, Bl, S):
    tq = _pick(S, (512, 256, 128))
    nq, qs, kvs, os_ = _attn_specs(S, tq)

    def body(q_ref, kv_ref, do_ref, dq_ref, dkv_ref):
        @pl.when(pl.program_id(1) == 0)
        def _():
            dkv_ref[...] = jnp.zeros_like(dkv_ref)

        for h in range(N_HEAD):
            sl = slice(h * HEAD, (h + 1) * HEAD)
            vsl = slice(512 + h * HEAD, 512 + (h + 1) * HEAD)
            q, k, v, do = q_ref[:, sl], kv_ref[0, :, sl], kv_ref[0, :, vsl], do_ref[:, sl]
            p = _attn_probs(q, k)
            dkv_ref[0, :, vsl] += _raw_dot(p, do, "tn")
            dp = _raw_dot(do, v, "nt")
            ds = p * (dp - jnp.sum(dp * p, axis=-1, keepdims=True)) * (HEAD ** -0.5)
            dq_ref[:, sl] = _raw_dot(ds, k, "nn").astype(bf16)
            dkv_ref[0, :, sl] += _raw_dot(ds, q, "tn")

    return _call(body, "attn_bwd", (Bl, nq), [qs, kvs, os_], (os_, kvs),
                 (jax.ShapeDtypeStruct((Bl * S, 512), bf16), jax.ShapeDtypeStruct((Bl, MEM_LEN, 1024), f32)),
                 sem=("arbitrary", "arbitrary"))(proj, kv, dc)


def _merge_specs(tm, tn):
    br = pl.BlockSpec((tm, 512), lambda i, j: (i, 0))
    w = pl.BlockSpec((512, tn), lambda i, j: (0, j))
    gl = [pl.BlockSpec((tm, tn), functools.partial(lambda i, j, n: (i, (C_GL + n * D_MODEL) // tn + j), n=n)) for n in range(3)]
    return [br, br, br, w, w, w, *gl]


def _merge_fwd(branches, wb, proj):
    T = proj.shape[0]
    tm, tn = _pick(T, (1024, 512, 256, 128)), 512

    def body(a_ref, b_ref, c_ref, w0, w1, w2, g0, g1, g2, o_ref):
        acc = jnp.zeros((tm, tn), f32)
        for x_ref, w_ref, g_ref in ((a_ref, w0, g0), (b_ref, w1, g1), (c_ref, w2, g2)):
            acc = acc + jax.nn.sigmoid(g_ref[...]) * _raw_dot(x_ref[...], w_ref[...], "nn")
        o_ref[...] = acc.astype(bf16)

    return _call(body, "merge_fwd", (T // tm, D_MODEL // tn), _merge_specs(tm, tn), pl.BlockSpec((tm, tn), lambda i, j: (i, j)),
                 jax.ShapeDtypeStruct((T, D_MODEL), bf16), sem=("parallel", "parallel"))(*branches, *wb, proj, proj, proj)


def _merge_bwd(branches, wb, proj, dmerged):
    T = proj.shape[0]
    tm, tn = _pick(T, (1024, 512, 256, 128)), 512

    def body(a_ref, b_ref, c_ref, w0, w1, w2, g0, g1, g2, dm_ref, dgl_ref, d0, d1, d2):
        dm = dm_ref[...]
        for n, (x_ref, w_ref, g_ref, d_ref) in enumerate(((a_ref, w0, g0, d0), (b_ref, w1, g1, d1), (c_ref, w2, g2, d2))):
            up = _raw_dot(x_ref[...], w_ref[...], "nn")
            logits = g_ref[...]
            dgl_ref[n] = _egrad(jax.nn.sigmoid, logits, dm * up).astype(bf16)
            d_ref[...] = (dm * jax.nn.sigmoid(logits)).astype(bf16)

    blk = pl.BlockSpec((tm, tn), lambda i, j: (i, j))
    sh = jax.ShapeDtypeStruct((T, D_MODEL), bf16)
    outs = _call(body, "merge_bwd", (T // tm, D_MODEL // tn), [*_merge_specs(tm, tn), blk],
                 (pl.BlockSpec((3, tm, tn), lambda i, j: (0, i, j)), blk, blk, blk),
                 (jax.ShapeDtypeStruct((3, T, D_MODEL), bf16), sh, sh, sh),
                 sem=("parallel", "parallel"))(*branches, *wb, proj, proj, proj, dmerged)
    return outs[0], outs[1:]


CONV_TC = 256


def _shift_down(a, k):
    row = lax.broadcasted_iota(jnp.int32, a.shape, 0)
    return jnp.where(row >= k, pltpu.roll(a, k, 0), 0.0)


def _shift_up(a, k):
    n = a.shape[0]
    row = lax.broadcasted_iota(jnp.int32, a.shape, 0)
    return jnp.where(row < n - k, pltpu.roll(a, n - k, 0), 0.0)


def _conv_pre(a, cw, cb):
    return cb + cw[0:1] * _shift_down(a, 2) + cw[1:2] * _shift_down(a, 1) + cw[2:3] * a


def _conv_fwd(ab, cw, cb, Bl, S):
    nc = D_FF // CONV_TC

    def body(a_ref, b_ref, cw_ref, cb_ref, o_ref):
        ac = _conv_pre(a_ref[0], cw_ref[...], cb_ref[...])
        o_ref[0] = (jax.nn.silu(ac) * b_ref[0]).astype(bf16)

    return _call(body, "conv_fwd", (Bl, nc),
                 [pl.BlockSpec((1, S, CONV_TC), lambda b, c: (b, 0, c)), pl.BlockSpec((1, S, CONV_TC), lambda b, c: (b, 0, nc + c)),
                  pl.BlockSpec((3, CONV_TC), lambda b, c: (0, c)), pl.BlockSpec((1, CONV_TC), lambda b, c: (0, c))],
                 pl.BlockSpec((1, S, CONV_TC), lambda b, c: (b, 0, c)), jax.ShapeDtypeStruct((Bl, S, D_FF), bf16),
                 sem=("parallel", "parallel"))(ab, ab, cw, cb)


def _conv_bwd(ab, cw, cb, dact, Bl, S):
    nc = D_FF // CONV_TC

    def body(a_ref, b_ref, cw_ref, cb_ref, d_ref, da_ref, db_ref, dcw_ref, dcb_ref):
        @pl.when(pl.program_id(1) == 0)
        def _():
            dcw_ref[...] = jnp.zeros_like(dcw_ref)
            dcb_ref[...] = jnp.zeros_like(dcb_ref)

        a, cw = a_ref[0], cw_ref[...]
        ac = _conv_pre(a, cw, cb_ref[...])
        dact_ = d_ref[0].astype(f32)
        db_ref[0] = (dact_ * jax.nn.silu(ac)).astype(bf16)
        dac = _egrad(jax.nn.silu, ac, dact_ * b_ref[0])
        da_ref[0] = (cw[2:3] * dac + cw[1:2] * _shift_up(dac, 1) + cw[0:1] * _shift_up(dac, 2)).astype(bf16)
        dcw_ref[0:1, :] += jnp.sum(dac * _shift_down(a, 2), axis=0, keepdims=True)
        dcw_ref[1:2, :] += jnp.sum(dac * _shift_down(a, 1), axis=0, keepdims=True)
        dcw_ref[2:3, :] += jnp.sum(dac * a, axis=0, keepdims=True)
        dcb_ref[...] += jnp.sum(dac, axis=0, keepdims=True)

    seq = pl.BlockSpec((1, S, CONV_TC), lambda c, b: (b, 0, c))
    return _call(body, "conv_bwd", (nc, Bl),
                 [seq, pl.BlockSpec((1, S, CONV_TC), lambda c, b: (b, 0, nc + c)), pl.BlockSpec((3, CONV_TC), lambda c, b: (0, c)),
                  pl.BlockSpec((1, CONV_TC), lambda c, b: (0, c)), seq],
                 (seq, seq, pl.BlockSpec((3, CONV_TC), lambda c, b: (0, c)), pl.BlockSpec((1, CONV_TC), lambda c, b: (0, c))),
                 (jax.ShapeDtypeStruct((Bl, S, D_FF), bf16), jax.ShapeDtypeStruct((Bl, S, D_FF), bf16),
                  jax.ShapeDtypeStruct((3, D_FF), f32), jax.ShapeDtypeStruct((1, D_FF), f32)),
                 sem=("arbitrary", "arbitrary"))(ab, ab, cw, cb, dact)


def _local_step(x, mem, target, p, w_in, tok, late_b, late_c, send):
    Bl, S, Dd = x.shape
    T = Bl * S
    x2d, t2d, mem2d = x.reshape(T, Dd), target.reshape(T, Dd), mem.reshape(Bl * MEM_LEN, Dd)
    b_st = jnp.pad(p["b_spatial"].T, ((0, 0), (0, 128 - N_HEAD)))
    lbl = p["lb_logits"]

    h = _rms_fwd(x2d, p["norm1_g"] + tok[0, 0], "norm1_fwd")
    proj = _mm(h, w_in, "nn", f32, "proj_fwd", 1024, 1664)
    a_out = _gmlp_fwd(proj, p["ln_v_g"], p["ln_v_b"], p["w_spatial"], b_st)
    proj3 = proj.reshape(Bl, S, IN_WIDTH)
    b_out, states = _hgrn_fwd(proj3, lbl, p["hgrn_norm_g"], Bl, S)
    b_out = b_out.reshape(T, 512)
    memn = _rms_fwd(mem2d, p["mem_norm_g"], "memnorm_fwd")
    w = late_b(b_out)
    wb = w["w_branch"]
    kv = _mm(memn, w["w_mem_kv"], "nn", f32, "kv_fwd", 512, 1024).reshape(Bl, MEM_LEN, 2 * 512)
    c_out = _attn_fwd(proj, kv, Bl, S)
    branches = (a_out, b_out, c_out)
    merged = _merge_fwd(branches, wb, proj)
    x1 = _mm(merged, w["w_out"], "nn", f32, "out_fwd", 1024, 1024, residual=x2d)
    h2 = _rms_fwd(x1, p["norm2_g"], "norm2_fwd")
    w.update(late_c(h2))
    ab = _mm(h2, w["w_up"], "nn", f32, "up_fwd", 1024, 1408)
    act = _conv_fwd(ab.reshape(Bl, S, 2 * D_FF), w["conv_w"], p["conv_b"], Bl, S).reshape(T, D_FF)
    x2 = _mm(act, w["w_down"], "nn", f32, "down_fwd", 512, 1024, residual=x1)
    loss_part, dx2, g_final = _final_loss(x2, p["final_g"], t2d)

    g_w_down = _mm(act, dx2, "tn", bf16, "down_dw", 1408, 1024, 1024)
    dact = _mm(dx2, w["w_down"], "nt", bf16, "down_dx", 1024, 1408)
    da, db, g_conv_w, g_conv_b = _conv_bwd(ab.reshape(Bl, S, 2 * D_FF), w["conv_w"], p["conv_b"], dact.reshape(Bl, S, D_FF), Bl, S)
    dab = jnp.concatenate([da.reshape(T, D_FF), db.reshape(T, D_FF)], axis=-1)
    g_w_up = _mm(h2, dab, "tn", bf16, "up_dw", 512, 1408, 1024)
    tok1 = send("c", dict(w_up=g_w_up, conv_w=g_conv_w, w_down=g_w_down))
    dh2 = _mm(dab, w["w_up"], "nt", f32, "up_dx", 1024, 1024, 1408)
    dx1, g_norm2 = _rms_bwd(x1, p["norm2_g"] + tok1[0, 0], dh2, "norm2_bwd", residual=dx2)

    g_w_out = _mm(merged, dx1, "tn", bf16, "out_dw", 1024, 1024, 1024)
    dmerged = _mm(dx1, w["w_out"], "nt", f32, "out_dx", 1024, 1024)
    dgl, dup = _merge_bwd(branches, wb, proj, dmerged)
    g_w_branch = [_mm(branches[n], dup[n], "tn", bf16, f"branch_dw{n}", 512, 1024, 1024) for n in range(3)]
    dbr = [_mm(dup[n], wb[n], "nt", bf16, f"branch_dx{n}", 1024, 512) for n in range(3)]
    dxq, dkv = _attn_bwd(proj, kv, dbr[2], Bl, S)
    dkv = dkv.reshape(Bl * MEM_LEN, 2 * 512)
    g_w_kv = _mm(memn, dkv, "tn", bf16, "kv_dw", 1024, 1024, 512)
    tok2 = send("b", dict(w_mem_kv=g_w_kv, w_branch=g_w_branch, w_out=g_w_out))
    dmemn = _mm(dkv, w["w_mem_kv"], "nt", f32, "kv_dx", 512, 1024)
    _, g_mem_norm = _rms_bwd(mem2d, p["mem_norm_g"] + tok2[0, 0], dmemn, "memnorm_bwd")
    dzuv, g_ln_g, g_ln_b, g_w_sp, g_b_sp = _gmlp_bwd(proj, p["ln_v_g"] + tok2[0, 0], p["ln_v_b"], p["w_spatial"], b_st, dbr[0])
    *dqfig, g_lbl, g_ng = _hgrn_bwd(proj3, lbl, p["hgrn_norm_g"], states, dbr[1].reshape(Bl, S, 512), Bl, S)
    dq, df, di, dg = [d.reshape(T, 512) for d in dqfig]
    dproj = jnp.concatenate([dzuv, dq, df, di, dg, dxq, dgl[0], dgl[1], dgl[2]], axis=-1)
    g_w_in = _mm(h, dproj, "tn", bf16, "proj_dw", 512, 1664, 1024)
    tok3 = send("a", dict(w_in=g_w_in))
    dh = _mm(dproj, w_in, "nt", f32, "proj_dx", 1024, 1024, 1664)
    dx, g_norm1 = _rms_bwd(x2d, p["norm1_g"] + tok3[0, 0], dh, "norm1_bwd", residual=dx1)

    gs = dict(w_spatial=g_w_sp, norm1_g=g_norm1, mem_norm_g=g_mem_norm, norm2_g=g_norm2, final_g=g_final, lb_logits=g_lbl,
              ln_v_g=g_ln_g, ln_v_b=g_ln_b, b_spatial=g_b_sp, hgrn_norm_g=g_ng, conv_b=g_conv_b)
    return loss_part, dx.reshape(Bl, S, Dd), gs


def _coords():
    return lax.axis_index("x"), lax.axis_index("y"), lax.axis_index("c")


def _slot(dev):
    return 4 * dev[0] + 2 * dev[1] + dev[2]


def _comm_call(body, name, arrays, out_shapes, n_sem):
    n = len(arrays)
    hbm = pl.BlockSpec(memory_space=pl.ANY)
    return pl.pallas_call(
        body, name=name, out_shape=out_shapes, in_specs=[hbm] * n, out_specs=[hbm] * n,
        scratch_shapes=[pltpu.SemaphoreType.DMA((n_sem, n)), pltpu.SemaphoreType.DMA((n_sem, n)), pltpu.SemaphoreType.DMA((n,))])(*arrays)


def _all_gather(blocks, name):
    n = len(blocks)

    def body(*refs):
        x_refs, o_refs, (send_sems, recv_sems, local_sems) = refs[:n], refs[n:2 * n], refs[2 * n:]
        x, y, c = _coords()
        me, sibling = (x, y, c), (x, y, 1 - c)
        chips = [(1 - x, y), (x, 1 - y), (1 - x, 1 - y)]

        def copy(a, k, block_dev, to, from_input=False):
            dst = o_refs[a].at[_slot(block_dev)]
            return pltpu.make_async_remote_copy(src_ref=x_refs[a] if from_input else dst, dst_ref=dst, send_sem=send_sems.at[k, a],
                                                recv_sem=recv_sems.at[k, a], device_id=to, device_id_type=MESH)

        mine = [pltpu.make_async_copy(x_refs[a], o_refs[a].at[_slot(me)], local_sems.at[a]) for a in range(n)]
        first = [copy(a, 0, me, sibling, True) for a in range(n)]
        first += [copy(a, 1 + j, me, (*chip, c), True) for j, chip in enumerate(chips) for a in range(n)]
        for cp in mine + first:
            cp.start()
        passed = []
        for j, chip in enumerate(chips):
            for a in range(n):
                copy(a, 1 + j, (*chip, c), me).wait_recv()
                fwd = copy(a, 4 + j, (*chip, c), sibling)
                fwd.start()
                passed.append(fwd)
        for a in range(n):
            copy(a, 0, sibling, me).wait_recv()
        for j, chip in enumerate(chips):
            for a in range(n):
                copy(a, 4 + j, (*chip, 1 - c), me).wait_recv()
        for cp in first + passed:
            cp.wait_send()
        for cp in mine:
            cp.wait()

    return _comm_call(body, name, blocks, [jax.ShapeDtypeStruct((N_DEV,) + b.shape, b.dtype) for b in blocks], 7)


def _all_to_all(parts, name):
    n = len(parts)
    rel = [(0, 0, 1), (0, 1, 0), (0, 1, 1), (1, 0, 0), (1, 0, 1), (1, 1, 0), (1, 1, 1)]

    def body(*refs):
        x_refs, o_refs, (send_sems, recv_sems, local_sems) = refs[:n], refs[n:2 * n], refs[2 * n:]
        x, y, c = _coords()
        me = (x, y, c)
        peers = [(x ^ dx, y ^ dy, c ^ dc) for dx, dy, dc in rel]

        def copy(a, k, peer):
            return pltpu.make_async_remote_copy(src_ref=x_refs[a].at[_slot(peer)], dst_ref=o_refs[a].at[_slot(me)], send_sem=send_sems.at[k, a],
                                                recv_sem=recv_sems.at[k, a], device_id=peer, device_id_type=MESH)

        def arrival(a, k, peer):
            return pltpu.make_async_remote_copy(src_ref=x_refs[a].at[_slot(me)], dst_ref=o_refs[a].at[_slot(peer)], send_sem=send_sems.at[k, a],
                                                recv_sem=recv_sems.at[k, a], device_id=peer, device_id_type=MESH)

        mine = [pltpu.make_async_copy(x_refs[a].at[_slot(me)], o_refs[a].at[_slot(me)], local_sems.at[a]) for a in range(n)]
        sends = [copy(a, k, peer) for k, peer in enumerate(peers) for a in range(n)]
        for cp in mine + sends:
            cp.start()
        for k, peer in enumerate(peers):
            for a in range(n):
                arrival(a, k, peer).wait_recv()
        for cp in sends:
            cp.wait_send()
        for cp in mine:
            cp.wait()

    return _comm_call(body, name, parts, [jax.ShapeDtypeStruct(p.shape, p.dtype) for p in parts], 7)


_HBM = pl.BlockSpec(memory_space=pltpu.HBM)
_SEM = pl.BlockSpec(memory_space=pltpu.SEMAPHORE)
_REL = [(0, 0, 1), (0, 1, 0), (0, 1, 1), (1, 0, 0), (1, 0, 1), (1, 1, 0), (1, 1, 1)]


_LINK_ORDER = (3, 1, 5, 4, 2, 6, 0)
SEND_PIECES = 4


def _pieces(shape, dtype):
    rows = shape[0]
    unit = 1 if len(shape) > 2 else (16 if dtype == bf16 else 8)
    for n in (SEND_PIECES, 2):
        if rows % (n * unit) == 0:
            return [pl.ds(i * (rows // n), rows // n) for i in range(n)]
    return [pl.ds(0, rows)]


def _split_copies(gather, src, land, send, recv, pieces):
    x, y, c = _coords()
    me = (x, y, c)
    copies = []
    for a in range(len(src)):
        block = src[a].shape if gather else src[a].shape[1:]
        for rows in (_pieces(block, src[a].dtype) if pieces else [None]):
            for k in _LINK_ORDER:
                dx, dy, dc = _REL[k]
                peer = (x ^ dx, y ^ dy, c ^ dc)
                mine, there = (src[a] if gather else src[a].at[_slot(peer)]), land[a].at[_slot(me)]
                if rows is not None:
                    mine, there = mine.at[rows], there.at[rows]
                copies.append(pltpu.make_async_remote_copy(src_ref=mine, dst_ref=there, send_sem=send[a].at[k], recv_sem=recv[a].at[k],
                                                           device_id=peer, device_id_type=MESH))
    return me, copies


def _arrivals(gather, src, land, send, recv):
    x, y, c = _coords()
    out = []
    for a in range(len(src)):
        for k, (dx, dy, dc) in enumerate(_REL):
            peer = (x ^ dx, y ^ dy, c ^ dc)
            out.append(pltpu.make_async_remote_copy(src_ref=src[a] if gather else src[a].at[_slot(peer)], dst_ref=land[a].at[_slot(peer)],
                                                    send_sem=send[a].at[k], recv_sem=recv[a].at[k], device_id=peer, device_id_type=MESH))
    return out


def _exchange_start(arrays, gather, name, after=None):
    n = len(arrays)
    e = 0 if after is None else 1
    lands = [lax.empty(((N_DEV,) + a.shape) if gather else a.shape, a.dtype) for a in arrays]

    def body(*refs):
        src, land = refs[:n], refs[n:2 * n]
        refs = refs[2 * n + e:]
        send, recv, token, local_sems = refs[:n], refs[n:2 * n], refs[4 * n], refs[4 * n + 1]
        me, out = _split_copies(gather, src, land, send, recv, True)
        local = [pltpu.make_async_copy(src[a] if gather else src[a].at[_slot(me)], land[a].at[_slot(me)], local_sems.at[a])
                 for a in range(n)]
        for cp in local:
            cp.start()
        for cp in local:
            cp.wait()
        for cp in out:
            cp.start()
        token[...] = jnp.zeros_like(token)

    sems = [pltpu.SemaphoreType.DMA((7,)) for _ in range(2 * n)]
    outs = pl.pallas_call(
        body, name=name,
        out_shape=(*sems, *[pltpu.HBM(a.shape, a.dtype) for a in arrays], *[pltpu.HBM(l.shape, l.dtype) for l in lands],
                   jax.ShapeDtypeStruct((8, 128), f32)),
        in_specs=[_HBM] * (2 * n) + [pl.BlockSpec(memory_space=pl.ANY)] * e,
        out_specs=(*[_SEM] * (2 * n), *[_HBM] * (2 * n), pl.BlockSpec(memory_space=pltpu.VMEM)),
        input_output_aliases={i: 2 * n + i for i in range(2 * n)},
        scratch_shapes=[pltpu.SemaphoreType.DMA((n,))],
        compiler_params=pltpu.CompilerParams(has_side_effects=pltpu.SideEffectType.DATAFLOW_SIDE_EFFECTING))(
        *[pltpu.with_memory_space_constraint(a, pltpu.HBM) for a in arrays],
        *[pltpu.with_memory_space_constraint(l, pltpu.HBM) for l in lands], *([after] if e else []))
    return (gather, n, outs[:4 * n]), outs[4 * n]


def _exchange_wait(handle, which, after, name):
    gather, n_all, vals = handle
    send_v, recv_v, src_v, land_v = [[vals[g * n_all + i] for i in which] for g in range(4)]
    n = len(which)

    def body(*refs):
        src, land, send, recv = refs[:n], refs[n:2 * n], refs[2 * n:3 * n], refs[3 * n:4 * n]
        for cp in _split_copies(gather, src, land, send, recv, False)[1]:
            cp.wait_send()
        for cp in _arrivals(gather, src, land, send, recv):
            cp.wait_recv()

    outs = pl.pallas_call(
        body, name=name,
        out_shape=(*[pltpu.HBM(a.shape, a.dtype) for a in src_v], *[pltpu.HBM(l.shape, l.dtype) for l in land_v]),
        in_specs=[*[_HBM] * (2 * n), *[_SEM] * (2 * n), pl.BlockSpec(memory_space=pl.ANY)], out_specs=[_HBM] * (2 * n),
        input_output_aliases={i: i for i in range(2 * n)},
        compiler_params=pltpu.CompilerParams(has_side_effects=pltpu.SideEffectType.DATAFLOW_SIDE_EFFECTING))(
        *src_v, *land_v, *send_v, *recv_v, after)
    return outs[n:]


def _adam_math(w, g, m, v):
    m_ = ADAM_B1 * m + (1.0 - ADAM_B1) * g
    v_ = ADAM_B2 * v + (1.0 - ADAM_B2) * jnp.square(g)
    m_hat = m_ / (1.0 - ADAM_B1 ** ADAM_STEP)
    v_hat = v_ / (1.0 - ADAM_B2 ** ADAM_STEP)
    return -ADAM_LR * (m_hat / (jnp.sqrt(v_hat) + ADAM_EPS) + ADAM_WD * w), m_, v_


def _reduce_adamw(parts, w, m, v, name):
    _, R, L = parts.shape
    tr = _pick(R, (256, 128, 64, 32, 16, 8))

    def body(p_ref, w_ref, m_ref, v_ref, g_ref, d_ref, nm_ref, nv_ref):
        g = p_ref[0].astype(f32)
        for i in range(1, N_DEV):
            g = g + p_ref[i].astype(f32)
        g_ref[...] = g
        d_ref[...], nm_ref[...], nv_ref[...] = _adam_math(w_ref[...], g, m_ref[...], v_ref[...])

    blk = pl.BlockSpec((tr, L), lambda i: (i, 0))
    sh = jax.ShapeDtypeStruct((R, L), f32)
    return _call(body, name, (R // tr,), [pl.BlockSpec((N_DEV, tr, L), lambda i: (0, i, 0)), blk, blk, blk], (blk,) * 4, (sh,) * 4,
                 sem=("parallel",))(parts, w, m, v)


SMALL = (("w_spatial", (512, 128), 0), ("norm1_g", (1, 1024), 512), ("mem_norm_g", (1, 1024), 520), ("norm2_g", (1, 1024), 528),
         ("final_g", (1, 1024), 536), ("lb_logits", (2, 512), 544), ("ln_v_g", (1, 512), 552), ("ln_v_b", (1, 512), 556),
         ("b_spatial", (4, 128), 560), ("hgrn_norm_g", (1, 128), 564), ("conv_b", (1, 2816), 565))
SMALL_USED, SMALL_ROWS = 587, 640


def _segments(shape, base):
    r, n = shape
    per = n // 128
    return [(base + i * per + j, i, slice(j * 128, (j + 1) * 128)) for i in range(r) for j in range(per)]


def _pack_small(gs):
    names = [n for n, _, _ in SMALL]

    def body(*refs):
        src, o_ref = dict(zip(names, refs[:-1])), refs[-1]
        o_ref[SMALL_USED:SMALL_ROWS, :] = jnp.zeros((SMALL_ROWS - SMALL_USED, 128), f32)
        for name, shape, base in SMALL:
            ref = src[name]
            if name == "w_spatial":
                o_ref[base:base + 512, :] = ref[...].reshape(512, 128)
            elif name == "b_spatial":
                o_ref[base:base + 4, :] = ref[0:4, :]
            elif name == "hgrn_norm_g":
                per_head = [ref[b, h] for b in range(ref.shape[0]) for h in range(N_HEAD)]
                o_ref[base:base + 1, :] = functools.reduce(lambda u, v_: u + v_, per_head)
            else:
                for row, i, sl in _segments(shape, base):
                    o_ref[row:row + 1, :] = ref[i:i + 1, sl]

    return pl.pallas_call(body, name="pack_small", out_shape=jax.ShapeDtypeStruct((SMALL_ROWS, 128), f32))(*[gs[n] for n in names])


def _small_update(gathered, w, m, v):
    names = [n for n, _, _ in SMALL]
    k = len(names)

    def body(*refs):
        p_ref = refs[0]
        ins = [dict(zip(names, refs[1 + i * k:1 + (i + 1) * k])) for i in range(3)]
        outs = [dict(zip(names, refs[1 + (3 + i) * k:1 + (4 + i) * k])) for i in range(4)]
        gsum = refs[-1]
        g = p_ref[0]
        for i in range(1, N_DEV):
            g = g + p_ref[i]
        gsum[...] = g
        for name, shape, base in SMALL:
            if name == "w_spatial":
                where = [(slice(base, base + 512), (slice(None), slice(None)))]
            else:
                where = [(slice(row, row + 1), (slice(i, i + 1), sl)) for row, i, sl in _segments(shape, base)]
            for rows, at in where:
                g_ = gsum[rows, :]
                d_, m_, v_ = _adam_math(ins[0][name][at], g_, ins[1][name][at], ins[2][name][at])
                for o, val in zip(outs, (g_, d_, m_, v_)):
                    o[name][at] = val

    args = [gathered] + [d[n] for d in (w, m, v) for n in names]
    out_shapes = [jax.ShapeDtypeStruct(shape, f32) for _ in range(4) for _, shape, _ in SMALL]
    outs = pl.pallas_call(body, name="small_update", out_shape=out_shapes, scratch_shapes=[pltpu.VMEM((SMALL_ROWS, 128), f32)])(*args)
    return [dict(zip(names, outs[i * k:(i + 1) * k])) for i in range(4)]


def _cols_full(g):
    return jnp.moveaxis(g, 0, -2).reshape(g.shape[1:-1] + (N_DEV * g.shape[-1],))


def _cols_parts(full):
    n = full.shape[-1] // N_DEV
    return jnp.moveaxis(full.reshape(full.shape[:-1] + (N_DEV, n)), -2, 0)


def kernel(x, mem, norm1_g, w_in, ln_v_g, ln_v_b, w_spatial, b_spatial, lb_logits, hgrn_norm_g, mem_norm_g, w_mem_kv, w_branch, w_out, norm2_g, w_up, conv_w, conv_b, w_down, final_g, loss_target, m_norm1_g, m_w_in, m_ln_v_g, m_ln_v_b, m_w_spatial, m_b_spatial, m_lb_logits, m_hgrn_norm_g, m_mem_norm_g, m_w_mem_kv, m_w_branch, m_w_out, m_norm2_g, m_w_up, m_conv_w, m_conv_b, m_w_down, m_final_g, v_norm1_g, v_w_in, v_ln_v_g, v_ln_v_b, v_w_spatial, v_b_spatial, v_lb_logits, v_hgrn_norm_g, v_mem_norm_g, v_w_mem_kv, v_w_branch, v_w_out, v_norm2_g, v_w_up, v_conv_w, v_conv_b, v_w_down, v_final_g):
    given = dict(locals())
    order = ("norm1_g", "w_in", "ln_v_g", "ln_v_b", "w_spatial", "b_spatial", "lb_logits", "hgrn_norm_g", "mem_norm_g",
             "w_mem_kv", "w_branch", "w_out", "norm2_g", "w_up", "conv_w", "conv_b", "w_down", "final_g")
    groups = dict(a=("w_in",), b=("w_mem_kv", "w_branch", "w_out"), c=("w_up", "conv_w", "w_down"))

    wire = {n: given[n][0].astype(f32 if n == "conv_w" else bf16) for ns in groups.values() for n in ns}
    g_in = _all_gather([wire["w_in"]], "gather_w_in")[0]
    late = groups["b"] + groups["c"]
    gather, tok = _exchange_start([wire[n] for n in late], True, "gather_rest_start", after=g_in)

    def late_b(after):
        kv_, br_, out_ = _exchange_wait(gather, (0, 1, 2), after, "gather_b_wait")
        br_ = _cols_full(br_)
        return dict(w_mem_kv=kv_.reshape(D_MODEL, 2 * 512), w_branch=[br_[n] for n in range(3)], w_out=out_.reshape(D_MODEL, D_MODEL))

    def late_c(after):
        up_, cw_, down_ = _exchange_wait(gather, (3, 4, 5), after, "gather_c_wait")
        return dict(w_up=_cols_full(up_), conv_w=_cols_full(cw_), w_down=down_.reshape(D_FF, D_MODEL))

    to_parts = dict(w_in=_cols_parts, w_up=_cols_parts, conv_w=_cols_parts,
                    w_branch=lambda g_: _cols_parts(jnp.stack(g_)).reshape(N_DEV, -1, 128),
                    w_mem_kv=lambda g_: g_.reshape(N_DEV, -1, 2 * 512), w_out=lambda g_: g_.reshape(N_DEV, -1, D_MODEL),
                    w_down=lambda g_: g_.reshape(N_DEV, -1, D_MODEL))
    scatters = {}

    def send(tag, grads_):
        scatters[tag], tok_ = _exchange_start([to_parts[n](grads_[n]) for n in groups[tag]], False, f"scatter_{tag}_start")
        return tok_

    small_2d = lambda prefix: {n: given[prefix + n].reshape(shape) for n, shape, _ in SMALL}
    p = small_2d("")
    p["w_spatial"] = w_spatial[0]
    loss_part, grad_x, gs = _local_step(x, mem, loss_target, p, _cols_full(g_in), tok, late_b, late_c, send)
    loss = lax.psum(loss_part[0, 0], ("x", "y", "c"))

    small_gather, _ = _exchange_start([_pack_small(gs)], True, "gather_small_start")

    grads, delta, new_m, new_v = {}, {}, {}, {}
    after = grad_x
    for tag in ("c", "b", "a"):
        recv = _exchange_wait(scatters[tag], tuple(range(len(groups[tag]))), after, f"scatter_{tag}_wait")
        for n, parts in zip(groups[tag], recv):
            two_d = (-1, given[n].shape[-1])
            res = _reduce_adamw(parts, *[given[pre + n].reshape(two_d) for pre in ("", "m_", "v_")], "adamw_" + n)
            grads[n], delta[n], new_m[n], new_v[n] = [r.reshape(given[n].shape) for r in res]
            after = res[0]

    gathered = _exchange_wait(small_gather, (0,), after, "gather_small_wait")[0]
    for dst, res in zip((grads, delta, new_m, new_v), _small_update(gathered, small_2d(""), small_2d("m_"), small_2d("v_"))):
        for n, _, _ in SMALL:
            dst[n] = res[n].reshape(given[n].shape)

    return (loss, grad_x, *[grads[n] for n in order], *[delta[n] for n in order], *[new_m[n] for n in order],
            *[new_v[n] for n in order])
```

```python
import functools

import jax
import jax.numpy as jnp
from jax import lax
from jax.experimental import pallas as pl
from jax.experimental.pallas import tpu as pltpu
from jax.experimental.pallas import tpu_sc as plsc

f32 = jnp.float32
bf16 = jnp.bfloat16

N_DEV = 8
D_MODEL = 1024
EPS = 1e-6
GM_CHUNK = 128
HG_CHUNK = 64
HEAD = 128
N_HEAD = 4
MEM_LEN = 256
D_FF = 2816
IN_WIDTH = 6656
C_ZU, C_HQ, C_HF, C_HI, C_HG, C_XQ, C_GL = 0, 1024, 1536, 2048, 2560, 3072, 3584
ADAM_LR, ADAM_B1, ADAM_B2, ADAM_EPS, ADAM_WD, ADAM_STEP = 0.001, 0.9, 0.999, 1e-08, 0.01, 10
VMEM_LIMIT = 56 * 1024 * 1024
MESH = pl.DeviceIdType.MESH


def _pick(n, cands):
    for c in cands:
        if n % c == 0:
            return c
    return n


def _call(body, name, grid, in_specs, out_specs, out_shape, scratch=(), sem=None, **cp):
    params = dict(vmem_limit_bytes=VMEM_LIMIT, **cp)
    if sem is not None:
        params["dimension_semantics"] = sem
    return pl.pallas_call(
        body, name=name, grid=grid, in_specs=in_specs, out_specs=out_specs, out_shape=out_shape,
        scratch_shapes=list(scratch), compiler_params=pltpu.CompilerParams(**params))


_DN = {"nn": (((1,), (0,)), ((), ())), "nt": (((1,), (1,)), ((), ())), "tn": (((0,), (0,)), ((), ()))}


def _raw_dot(a, b, mode):
    return lax.dot_general(a.astype(bf16), b.astype(bf16), _DN[mode], preferred_element_type=f32)


@jax.custom_vjp
def _dot_nn(a, b):
    return _raw_dot(a, b, "nn")


_dot_nn.defvjp(lambda a, b: (_raw_dot(a, b, "nn"), (a, b)),
               lambda r, g: (_raw_dot(g, r[1], "nt"), _raw_dot(r[0], g, "tn")))


@jax.custom_vjp
def _dot_nt(a, b):
    return _raw_dot(a, b, "nt")


_dot_nt.defvjp(lambda a, b: (_raw_dot(a, b, "nt"), (a, b)),
               lambda r, g: (_raw_dot(g, r[1], "nn"), _raw_dot(g, r[0], "tn")))


@jax.custom_vjp
def _dot_tn(a, b):
    return _raw_dot(a, b, "tn")


_dot_tn.defvjp(lambda a, b: (_raw_dot(a, b, "tn"), (a, b)),
               lambda r, g: (_raw_dot(r[1], g, "nt"), _raw_dot(r[0], g, "nn")))


def _tri(n, lower):
    r = lax.broadcasted_iota(jnp.int32, (n, n), 0)
    c = lax.broadcasted_iota(jnp.int32, (n, n), 1)
    return ((c <= r) if lower else (c >= r)).astype(f32)


def _sel_dot(sel, x, mode, x_first=False):
    hi = x.astype(bf16)
    rest = x - hi.astype(f32)
    mid = rest.astype(bf16)
    lo = (rest - mid.astype(f32)).astype(bf16)
    sel = sel.astype(bf16)
    dot = lambda piece: lax.dot_general(*((piece, sel) if x_first else (sel, piece)), _DN[mode], preferred_element_type=f32)
    return dot(hi) + dot(mid) + dot(lo)


def _egrad(fn, x, ct):
    return jax.vjp(fn, x)[1](ct)[0]


def _mm(a, b, mode, out_dtype, name, tm, tn, tk=None, residual=None):
    if mode == "nn":
        (M, K), (_, N) = a.shape, b.shape
    elif mode == "nt":
        (M, K), (N, _) = a.shape, b.shape
    else:
        (K, M), (_, N) = a.shape, b.shape
    tm, tn = min(tm, M), min(tn, N)
    tk = K if tk is None else min(tk, K)
    assert M % tm == 0 and N % tn == 0 and K % tk == 0, (name, M, N, K, tm, tn, tk)
    nk = K // tk

    def body(*refs):
        acc_ref = refs[-1] if nk > 1 else None
        refs = refs[:-1] if nk > 1 else refs
        if residual is None:
            a_ref, b_ref, o_ref = refs
        else:
            a_ref, b_ref, r_ref, o_ref = refs

        def finish(r):
            if residual is not None:
                r = r + r_ref[...]
            o_ref[...] = r.astype(out_dtype)

        part = _raw_dot(a_ref[...], b_ref[...], mode)
        if nk == 1:
            finish(part)
            return
        k = pl.program_id(2)

        @pl.when(k == 0)
        def _():
            acc_ref[...] = part

        @pl.when((k > 0) & (k < nk - 1))
        def _():
            acc_ref[...] += part

        @pl.when(k == nk - 1)
        def _():
            finish(acc_ref[...] + part)

    a_spec = {"nn": pl.BlockSpec((tm, tk), lambda i, j, k: (i, k)),
              "nt": pl.BlockSpec((tm, tk), lambda i, j, k: (i, k)),
              "tn": pl.BlockSpec((tk, tm), lambda i, j, k: (k, i))}[mode]
    b_spec = {"nn": pl.BlockSpec((tk, tn), lambda i, j, k: (k, j)),
              "nt": pl.BlockSpec((tn, tk), lambda i, j, k: (j, k)),
              "tn": pl.BlockSpec((tk, tn), lambda i, j, k: (k, j))}[mode]
    o_spec = pl.BlockSpec((tm, tn), lambda i, j, k: (i, j))
    in_specs = [a_spec, b_spec] + ([o_spec] if residual is not None else [])
    args = (a, b) + ((residual,) if residual is not None else ())
    return _call(body, name, (M // tm, N // tn, nk), in_specs, o_spec, jax.ShapeDtypeStruct((M, N), out_dtype),
                 scratch=[pltpu.VMEM((tm, tn), f32)] if nk > 1 else [], sem=("parallel", "parallel", "arbitrary"))(*args)


def _rms_fwd(x, g, name):
    R, Dd = x.shape
    tr = _pick(R, (512, 256, 128))

    def body(x_ref, g_ref, o_ref):
        xf = x_ref[...]
        y = xf * lax.rsqrt(jnp.mean(xf * xf, axis=-1, keepdims=True) + EPS)
        o_ref[...] = (y * g_ref[...]).astype(bf16)

    return _call(body, name, (R // tr,), [pl.BlockSpec((tr, Dd), lambda i: (i, 0)), pl.BlockSpec((1, Dd), lambda i: (0, 0))],
                 pl.BlockSpec((tr, Dd), lambda i: (i, 0)), jax.ShapeDtypeStruct((R, Dd), bf16), sem=("parallel",))(x, g)


def _rms_bwd(x, g, dh, name, residual=None):
    R, Dd = x.shape
    tr = _pick(R, (512, 256, 128))

    def body(*refs):
        if residual is None:
            x_ref, g_ref, dh_ref, dx_ref, dg_ref = refs
        else:
            x_ref, g_ref, dh_ref, r_ref, dx_ref, dg_ref = refs
        xf = x_ref[...]
        rs = lax.rsqrt(jnp.mean(xf * xf, axis=-1, keepdims=True) + EPS)
        y = xf * rs
        dh_ = dh_ref[...].astype(f32)
        dy = dh_ * g_ref[...]
        dx = rs * (dy - y * jnp.mean(dy * y, axis=-1, keepdims=True))
        if residual is not None:
            dx = dx + r_ref[...]
        dx_ref[...] = dx

        @pl.when(pl.program_id(0) == 0)
        def _():
            dg_ref[...] = jnp.zeros_like(dg_ref)

        dg_ref[...] += jnp.sum(dh_ * y, axis=0, keepdims=True)

    row = pl.BlockSpec((tr, Dd), lambda i: (i, 0))
    vec = pl.BlockSpec((1, Dd), lambda i: (0, 0))
    in_specs = [row, vec, row] + ([row] if residual is not None else [])
    args = (x, g, dh) + ((residual,) if residual is not None else ())
    return _call(body, name, (R // tr,), in_specs, (row, vec),
                 (jax.ShapeDtypeStruct((R, Dd), f32), jax.ShapeDtypeStruct((1, Dd), f32)), sem=("arbitrary",))(*args)


def _final_loss(x2, g, target):
    R, Dd = x2.shape
    tr = _pick(R, (512, 256, 128))

    def body(x_ref, g_ref, t_ref, loss_ref, dx_ref, dg_ref):
        xf = x_ref[...]
        rs = lax.rsqrt(jnp.mean(xf * xf, axis=-1, keepdims=True) + EPS)
        y = xf * rs
        err = y * g_ref[...] - t_ref[...]
        dh_ = err * (1.0 / Dd)
        dy = dh_ * g_ref[...]
        dx_ref[...] = rs * (dy - y * jnp.mean(dy * y, axis=-1, keepdims=True))

        @pl.when(pl.program_id(0) == 0)
        def _():
            dg_ref[...] = jnp.zeros_like(dg_ref)
            loss_ref[...] = jnp.zeros_like(loss_ref)

        dg_ref[...] += jnp.sum(dh_ * y, axis=0, keepdims=True)
        part = jnp.sum(jnp.mean(err * err, axis=-1, keepdims=True), axis=0, keepdims=True)
        loss_ref[...] += 0.5 * part

    row = pl.BlockSpec((tr, Dd), lambda i: (i, 0))
    vec = pl.BlockSpec((1, Dd), lambda i: (0, 0))
    return _call(body, "final_loss", (R // tr,), [row, vec, row], (pl.BlockSpec((1, 128), lambda i: (0, 0)), row, vec),
                 (jax.ShapeDtypeStruct((1, 128), f32), jax.ShapeDtypeStruct((R, Dd), f32), jax.ShapeDtypeStruct((1, Dd), f32)),
                 sem=("arbitrary",))(x2, g, target)


def _gmlp_parts(zuv, ln_g, ln_b):
    zu, zv = zuv[:, :512], zuv[:, 512:]
    u = jax.nn.gelu(zu)
    v = jax.nn.gelu(zv)
    mu = jnp.mean(v, axis=-1, keepdims=True)
    rs = lax.rsqrt(jnp.mean(jnp.square(v - mu), axis=-1, keepdims=True) + EPS)
    xh = (v - mu) * rs
    return zu, zv, u, xh, rs, xh * ln_g + ln_b


def _gmlp_fwd(proj, ln_g, ln_b, w_s, b_st):
    T = proj.shape[0]

    def body(p_ref, g_ref, b_ref, w_ref, bs_ref, o_ref):
        _, _, u, _, _, vn = _gmlp_parts(p_ref[...], g_ref[...], b_ref[...])
        causal = _tri(GM_CHUNK, True) > 0
        for gi in range(N_HEAD):
            sl = slice(gi * HEAD, (gi + 1) * HEAD)
            w = jnp.where(causal, w_ref[gi], 0.0)
            mixed = _raw_dot(w, vn[:, sl], "nn") + bs_ref[:, gi:gi + 1]
            o_ref[:, sl] = (u[:, sl] * mixed).astype(bf16)

    vec = pl.BlockSpec((1, 512), lambda i: (0, 0))
    return _call(body, "gmlp_fwd", (T // GM_CHUNK,),
                 [pl.BlockSpec((GM_CHUNK, 1024), lambda i: (i, 0)), vec, vec,
                  pl.BlockSpec((N_HEAD, GM_CHUNK, GM_CHUNK), lambda i: (0, 0, 0)), pl.BlockSpec((GM_CHUNK, 128), lambda i: (0, 0))],
                 pl.BlockSpec((GM_CHUNK, 512), lambda i: (i, 0)), jax.ShapeDtypeStruct((T, 512), bf16), sem=("parallel",))(
        proj, ln_g, ln_b, w_s, b_st)


def _gmlp_bwd(proj, ln_g, ln_b, w_s, b_st, da):
    T = proj.shape[0]

    def body(p_ref, g_ref, b_ref, w_ref, bs_ref, da_ref, dp_ref, dg_ref, db_ref, dw_ref, dbs_ref):
        zu, zv, u, xh, rs, vn = _gmlp_parts(p_ref[...], g_ref[...], b_ref[...])
        causal = _tri(GM_CHUNK, True) > 0
        sub = lax.broadcasted_iota(jnp.int32, (8, GM_CHUNK), 0)
        ones = jnp.ones((8, HEAD), f32)
        dout = da_ref[...].astype(f32)

        @pl.when(pl.program_id(0) == 0)
        def _():
            for r in (dg_ref, db_ref, dw_ref, dbs_ref):
                r[...] = jnp.zeros_like(r)

        du, dvn, dbs = [], [], jnp.zeros((8, GM_CHUNK), f32)
        for gi in range(N_HEAD):
            sl = slice(gi * HEAD, (gi + 1) * HEAD)
            w = jnp.where(causal, w_ref[gi], 0.0)
            mixed = _raw_dot(w, vn[:, sl], "nn") + bs_ref[:, gi:gi + 1]
            du.append(dout[:, sl] * mixed)
            dm = dout[:, sl] * u[:, sl]
            row_sums = _sel_dot(ones, dm, "nt")
            dbs = dbs + jnp.where(sub == gi, row_sums, 0.0)
            dw_ref[gi] += jnp.where(causal, _raw_dot(dm, vn[:, sl], "nt"), 0.0)
            dvn.append(_raw_dot(w, dm, "tn"))
        dbs_ref[...] += dbs
        du = jnp.concatenate(du, axis=-1)
        dvn = jnp.concatenate(dvn, axis=-1)
        dg_ref[...] += jnp.sum(dvn * xh, axis=0, keepdims=True)
        db_ref[...] += jnp.sum(dvn, axis=0, keepdims=True)
        dxh = dvn * g_ref[...]
        dv = rs * (dxh - jnp.mean(dxh, axis=-1, keepdims=True) - xh * jnp.mean(dxh * xh, axis=-1, keepdims=True))
        dp_ref[:, :512] = _egrad(jax.nn.gelu, zu, du).astype(bf16)
        dp_ref[:, 512:] = _egrad(jax.nn.gelu, zv, dv).astype(bf16)

    vec = pl.BlockSpec((1, 512), lambda i: (0, 0))
    wsp = pl.BlockSpec((N_HEAD, GM_CHUNK, GM_CHUNK), lambda i: (0, 0, 0))
    return _call(body, "gmlp_bwd", (T // GM_CHUNK,),
                 [pl.BlockSpec((GM_CHUNK, 1024), lambda i: (i, 0)), vec, vec, wsp, pl.BlockSpec((GM_CHUNK, 128), lambda i: (0, 0)),
                  pl.BlockSpec((GM_CHUNK, 512), lambda i: (i, 0))],
                 (pl.BlockSpec((GM_CHUNK, 1024), lambda i: (i, 0)), vec, vec, wsp, pl.BlockSpec((8, GM_CHUNK), lambda i: (0, 0))),
                 (jax.ShapeDtypeStruct((T, 1024), bf16), jax.ShapeDtypeStruct((1, 512), f32), jax.ShapeDtypeStruct((1, 512), f32),
                  jax.ShapeDtypeStruct((N_HEAD, GM_CHUNK, GM_CHUNK), f32), jax.ShapeDtypeStruct((8, GM_CHUNK), f32)),
                 sem=("arbitrary",))(proj, ln_g, ln_b, w_s, b_st, da)


HG_SUB = 16
HG_NSUB = HG_CHUNK // HG_SUB


def _two_level_matrix():
    r = lax.broadcasted_iota(jnp.int32, (2 * HG_CHUNK, HG_CHUNK), 0)
    c = lax.broadcasted_iota(jnp.int32, (2 * HG_CHUNK, HG_CHUNK), 1)
    t = jnp.where(r < HG_CHUNK, r, r - HG_CHUNK)
    local = (r < HG_CHUNK) & (t // HG_SUB == c // HG_SUB) & (c <= t)
    before = (r >= HG_CHUNK) & (c < (t // HG_SUB) * HG_SUB)
    return (local | before).astype(f32)


def _two_level_sums(x):
    two = _sel_dot(_two_level_matrix(), x, "nn")
    return two[:HG_CHUNK], two[HG_CHUNK:]


@jax.custom_vjp
def _two_level_cumsum(x):
    return _two_level_sums(x)


_two_level_cumsum.defvjp(
    lambda x: (_two_level_sums(x), None),
    lambda _, g: (_sel_dot(_two_level_matrix(), jnp.concatenate(g, axis=0), "tn"),))


def _tile_matrix():
    s = lax.broadcasted_iota(jnp.int32, (HG_SUB, HG_CHUNK), 0)
    j = lax.broadcasted_iota(jnp.int32, (HG_SUB, HG_CHUNK), 1)
    return (j % HG_SUB == s).astype(f32)


@jax.custom_vjp
def _tile_lanes(x):
    return _sel_dot(_tile_matrix(), x, "nn", x_first=True)


_tile_lanes.defvjp(
    lambda x: (_sel_dot(_tile_matrix(), x, "nn", x_first=True), None),
    lambda _, g: (_sel_dot(_tile_matrix(), g, "nt", x_first=True),))


def _block_rows(x):
    k = x.shape[-1]
    return jnp.broadcast_to(x.reshape(HG_NSUB, 1, HG_SUB, k), (HG_NSUB, HG_SUB, HG_SUB, k)).reshape(HG_CHUNK, HG_SUB, k)


def _hgrn_chunk(st0, q_raw, f_raw, i_raw, g_raw, l0, l1, ng):
    C, SUB = HG_CHUNK, HG_SUB
    lb = jax.nn.sigmoid(l0 - l1)
    fg = lb + (1.0 - lb) * jax.nn.sigmoid(f_raw)
    kk = 1.0 - fg
    qf = jax.nn.silu(q_raw)
    al, base = _two_level_cumsum(jnp.log(fg))
    a = al + base
    row = lax.broadcasted_iota(jnp.int32, (C, HEAD), 0)
    a_last = jnp.sum(jnp.where(row == C - 1, a, 0.0), axis=0, keepdims=True)
    inter = _dot_nt(qf * jnp.exp(a), st0)
    qt = qf * jnp.exp(al)
    rb = lax.broadcasted_iota(jnp.int32, (C, C), 0) // SUB
    cb = lax.broadcasted_iota(jnp.int32, (C, C), 1) // SUB
    scores = jnp.zeros((C, C), f32)
    for i in range(1, HG_NSUB):
        base_i = jnp.sum(jnp.where(row == i * SUB, base, 0.0), axis=0, keepdims=True)
        kt = kk * jnp.exp(jnp.minimum(base_i - a, 0.0))
        scores = scores + jnp.where((rb == i) & (cb < i), _dot_nt(qt, kt), 0.0)
    t_i = lax.broadcasted_iota(jnp.int32, (C, SUB, HEAD), 0) % SUB
    s_i = lax.broadcasted_iota(jnp.int32, (C, SUB, HEAD), 1)
    decay = jnp.exp(jnp.where(s_i <= t_i, al[:, None, :] - _block_rows(al), -jnp.inf))
    diag = jnp.sum(qf[:, None, :] * decay * _block_rows(kk), axis=-1)
    scores = scores + jnp.where(rb == cb, _tile_lanes(diag), 0.0)
    o = inter + _dot_nn(scores, i_raw)
    st1 = jnp.exp(a_last) * st0 + _dot_tn(i_raw, kk * jnp.exp(a_last - a))
    on = o * lax.rsqrt(jnp.mean(o * o, axis=-1, keepdims=True) + EPS) * ng
    return st1, on * jax.nn.silu(g_raw)


def _hgrn_specs(S, Bl, rev):
    N = S // HG_CHUNK
    chunk = (lambda n: N - 1 - n) if rev else (lambda n: n)
    col = lambda c0: pl.BlockSpec((Bl, HG_CHUNK, 512), lambda n: (0, chunk(n), c0 // 512))
    st = pl.BlockSpec((Bl, N_HEAD, 1, HEAD, HEAD), lambda n: (0, 0, chunk(n), 0, 0))
    full = lambda *s: pl.BlockSpec(s, functools.partial(lambda n, nd: (0,) * nd, nd=len(s)))
    return N, col, st, full


def _hgrn_fwd(proj, lb_logits, ng, Bl, S):
    N, col, st, full = _hgrn_specs(S, Bl, False)

    def body(q_ref, f_ref, i_ref, g_ref, l_ref, ng_ref, o_ref, st_ref, state):
        @pl.when(pl.program_id(0) == 0)
        def _():
            state[...] = jnp.zeros_like(state)

        for b in range(Bl):
            for h in range(N_HEAD):
                sl = slice(h * HEAD, (h + 1) * HEAD)
                st0 = state[b, h]
                st_ref[b, h, 0] = st0
                st1, out = _hgrn_chunk(st0, q_ref[b, :, sl], f_ref[b, :, sl], i_ref[b, :, sl], g_ref[b, :, sl],
                                       l_ref[0:1, sl], l_ref[1:2, sl], ng_ref[...])
                state[b, h] = st1
                o_ref[b, :, sl] = out.astype(bf16)

    return _call(body, "hgrn_fwd", (N,), [col(C_HQ), col(C_HF), col(C_HI), col(C_HG), full(2, 512), full(1, HEAD)],
                 (col(0), st),
                 (jax.ShapeDtypeStruct((Bl, S, 512), bf16), jax.ShapeDtypeStruct((Bl, N_HEAD, N, HEAD, HEAD), f32)),
                 scratch=[pltpu.VMEM((Bl, N_HEAD, HEAD, HEAD), f32)], sem=("arbitrary",))(
        proj, proj, proj, proj, lb_logits, ng)


def _hgrn_bwd(proj, lb_logits, ng, states, db, Bl, S):
    N, col, st, full = _hgrn_specs(S, Bl, True)

    def body(q_ref, f_ref, i_ref, g_ref, l_ref, ng_ref, st_ref, db_ref,
             dq_ref, df_ref, di_ref, dg_ref, dl_ref, dng_ref, dstate):
        @pl.when(pl.program_id(0) == 0)
        def _():
            dstate[...] = jnp.zeros_like(dstate)
            dl_ref[...] = jnp.zeros_like(dl_ref)
            dng_ref[...] = jnp.zeros_like(dng_ref)

        for b in range(Bl):
            for h in range(N_HEAD):
                sl = slice(h * HEAD, (h + 1) * HEAD)
                _, vjp = jax.vjp(_hgrn_chunk, st_ref[b, h, 0], q_ref[b, :, sl], f_ref[b, :, sl], i_ref[b, :, sl], g_ref[b, :, sl],
                                 l_ref[0:1, sl], l_ref[1:2, sl], ng_ref[...])
                dst0, dq, df, di, dg, dl0, dl1, dng = vjp((dstate[b, h], db_ref[b, :, sl].astype(f32)))
                dstate[b, h] = dst0
                dq_ref[b, :, sl] = dq.astype(bf16)
                df_ref[b, :, sl] = df.astype(bf16)
                di_ref[b, :, sl] = di.astype(bf16)
                dg_ref[b, :, sl] = dg.astype(bf16)
                dl_ref[0:1, sl] += dl0
                dl_ref[1:2, sl] += dl1
                dng_ref[b, h] += dng

    return _call(body, "hgrn_bwd", (N,),
                 [col(C_HQ), col(C_HF), col(C_HI), col(C_HG), full(2, 512), full(1, HEAD), st, col(0)],
                 (*[col(0)] * 4, full(2, 512), full(Bl, N_HEAD, 1, HEAD)),
                 (*[jax.ShapeDtypeStruct((Bl, S, 512), bf16)] * 4, jax.ShapeDtypeStruct((2, 512), f32),
                  jax.ShapeDtypeStruct((Bl, N_HEAD, 1, HEAD), f32)),
                 scratch=[pltpu.VMEM((Bl, N_HEAD, HEAD, HEAD), f32)], sem=("arbitrary",))(
        proj, proj, proj, proj, lb_logits, ng, states, db)


def _attn_probs(q, k):
    s = _raw_dot(q, k, "nt") * (HEAD ** -0.5)
    e = jnp.exp(s - jnp.max(s, axis=-1, keepdims=True))
    return e / jnp.sum(e, axis=-1, keepdims=True)


def _attn_specs(S, tq):
    nq = S // tq
    q = pl.BlockSpec((tq, 512), lambda b, i: (b * nq + i, C_XQ // 512))
    kv = pl.BlockSpec((1, MEM_LEN, 1024), lambda b, i: (b, 0, 0))
    o = pl.BlockSpec((tq, 512), lambda b, i: (b * nq + i, 0))
    return nq, q, kv, o


def _attn_fwd(proj, kv, Bl, S):
    tq = _pick(S, (512, 256, 128))
    nq, qs, kvs, os_ = _attn_specs(S, tq)

    def body(q_ref, kv_ref, o_ref):
        for h in range(N_HEAD):
            sl = slice(h * HEAD, (h + 1) * HEAD)
            p = _attn_probs(q_ref[:, sl], kv_ref[0, :, sl])
            o_ref[:, sl] = _raw_dot(p, kv_ref[0, :, 512 + h * HEAD:512 + (h + 1) * HEAD], "nn").astype(bf16)

    return _call(body, "attn_fwd", (Bl, nq), [qs, kvs], os_, jax.ShapeDtypeStruct((Bl * S, 512), bf16),
                 sem=("parallel", "parallel"))(proj, kv)


def _attn_bwd(proj, kv, dc, Bl, S):
    tq = _pick(S, (512, 256, 128))
    nq, qs, kvs, os_ = _attn_specs(S, tq)

    def body(q_ref, kv_ref, do_ref, dq_ref, dkv_ref):
        @pl.when(pl.program_id(1) == 0)
        def _():
            dkv_ref[...] = jnp.zeros_like(dkv_ref)

        for h in range(N_HEAD):
            sl = slice(h * HEAD, (h + 1) * HEAD)
            vsl = slice(512 + h * HEAD, 512 + (h + 1) * HEAD)
            q, k, v, do = q_ref[:, sl], kv_ref[0, :, sl], kv_ref[0, :, vsl], do_ref[:, sl]
            p = _attn_probs(q, k)
            dkv_ref[0, :, vsl] += _raw_dot(p, do, "tn")
            dp = _raw_dot(do, v, "nt")
            ds = p * (dp - jnp.sum(dp * p, axis=-1, keepdims=True)) * (HEAD ** -0.5)
            dq_ref[:, sl] = _raw_dot(ds, k, "nn").astype(bf16)
            dkv_ref[0, :, sl] += _raw_dot(ds, q, "tn")

    return _call(body, "attn_bwd", (Bl, nq), [qs, kvs, os_], (os_, kvs),
                 (jax.ShapeDtypeStruct((Bl * S, 512), bf16), jax.ShapeDtypeStruct((Bl, MEM_LEN, 1024), f32)),
                 sem=("arbitrary", "arbitrary"))(proj, kv, dc)


def _merge_specs(tm, tn):
    br = pl.BlockSpec((tm, 512), lambda i, j: (i, 0))
    w = pl.BlockSpec((512, tn), lambda i, j: (0, j))
    gl = [pl.BlockSpec((tm, tn), functools.partial(lambda i, j, n: (i, (C_GL + n * D_MODEL) // tn + j), n=n)) for n in range(3)]
    return [br, br, br, w, w, w, *gl]


def _merge_fwd(branches, wb, proj):
    T = proj.shape[0]
    tm, tn = _pick(T, (1024, 512, 256, 128)), 512

    def body(a_ref, b_ref, c_ref, w0, w1, w2, g0, g1, g2, o_ref):
        acc = jnp.zeros((tm, tn), f32)
        for x_ref, w_ref, g_ref in ((a_ref, w0, g0), (b_ref, w1, g1), (c_ref, w2, g2)):
            acc = acc + jax.nn.sigmoid(g_ref[...]) * _raw_dot(x_ref[...], w_ref[...], "nn")
        o_ref[...] = acc.astype(bf16)

    return _call(body, "merge_fwd", (T // tm, D_MODEL // tn), _merge_specs(tm, tn), pl.BlockSpec((tm, tn), lambda i, j: (i, j)),
                 jax.ShapeDtypeStruct((T, D_MODEL), bf16), sem=("parallel", "parallel"))(*branches, *wb, proj, proj, proj)


def _merge_bwd(branches, wb, proj, dmerged):
    T = proj.shape[0]
    tm, tn = _pick(T, (1024, 512, 256, 128)), 512

    def body(a_ref, b_ref, c_ref, w0, w1, w2, g0, g1, g2, dm_ref, dgl_ref, d0, d1, d2):
        dm = dm_ref[...]
        for n, (x_ref, w_ref, g_ref, d_ref) in enumerate(((a_ref, w0, g0, d0), (b_ref, w1, g1, d1), (c_ref, w2, g2, d2))):
            up = _raw_dot(x_ref[...], w_ref[...], "nn")
            logits = g_ref[...]
            dgl_ref[n] = _egrad(jax.nn.sigmoid, logits, dm * up).astype(bf16)
            d_ref[...] = (dm * jax.nn.sigmoid(logits)).astype(bf16)

    blk = pl.BlockSpec((tm, tn), lambda i, j: (i, j))
    sh = jax.ShapeDtypeStruct((T, D_MODEL), bf16)
    outs = _call(body, "merge_bwd", (T // tm, D_MODEL // tn), [*_merge_specs(tm, tn), blk],
                 (pl.BlockSpec((3, tm, tn), lambda i, j: (0, i, j)), blk, blk, blk),
                 (jax.ShapeDtypeStruct((3, T, D_MODEL), bf16), sh, sh, sh),
                 sem=("parallel", "parallel"))(*branches, *wb, proj, proj, proj, dmerged)
    return outs[0], outs[1:]


CONV_TC = 256


def _shift_down(a, k):
    row = lax.broadcasted_iota(jnp.int32, a.shape, 0)
    return jnp.where(row >= k, pltpu.roll(a, k, 0), 0.0)


def _shift_up(a, k):
    n = a.shape[0]
    row = lax.broadcasted_iota(jnp.int32, a.shape, 0)
    return jnp.where(row < n - k, pltpu.roll(a, n - k, 0), 0.0)


def _conv_pre(a, cw, cb):
    return cb + cw[0:1] * _shift_down(a, 2) + cw[1:2] * _shift_down(a, 1) + cw[2:3] * a


def _conv_fwd(ab, cw, cb, Bl, S):
    nc = D_FF // CONV_TC

    def body(a_ref, b_ref, cw_ref, cb_ref, o_ref):
        ac = _conv_pre(a_ref[0], cw_ref[...], cb_ref[...])
        o_ref[0] = (jax.nn.silu(ac) * b_ref[0]).astype(bf16)

    return _call(body, "conv_fwd", (Bl, nc),
                 [pl.BlockSpec((1, S, CONV_TC), lambda b, c: (b, 0, c)), pl.BlockSpec((1, S, CONV_TC), lambda b, c: (b, 0, nc + c)),
                  pl.BlockSpec((3, CONV_TC), lambda b, c: (0, c)), pl.BlockSpec((1, CONV_TC), lambda b, c: (0, c))],
                 pl.BlockSpec((1, S, CONV_TC), lambda b, c: (b, 0, c)), jax.ShapeDtypeStruct((Bl, S, D_FF), bf16),
                 sem=("parallel", "parallel"))(ab, ab, cw, cb)


def _conv_bwd(ab, cw, cb, dact, Bl, S):
    nc = D_FF // CONV_TC

    def body(a_ref, b_ref, cw_ref, cb_ref, d_ref, da_ref, db_ref, dcw_ref, dcb_ref):
        @pl.when(pl.program_id(1) == 0)
        def _():
            dcw_ref[...] = jnp.zeros_like(dcw_ref)
            dcb_ref[...] = jnp.zeros_like(dcb_ref)

        a, cw = a_ref[0], cw_ref[...]
        ac = _conv_pre(a, cw, cb_ref[...])
        dact_ = d_ref[0].astype(f32)
        db_ref[0] = (dact_ * jax.nn.silu(ac)).astype(bf16)
        dac = _egrad(jax.nn.silu, ac, dact_ * b_ref[0])
        da_ref[0] = (cw[2:3] * dac + cw[1:2] * _shift_up(dac, 1) + cw[0:1] * _shift_up(dac, 2)).astype(bf16)
        dcw_ref[0:1, :] += jnp.sum(dac * _shift_down(a, 2), axis=0, keepdims=True)
        dcw_ref[1:2, :] += jnp.sum(dac * _shift_down(a, 1), axis=0, keepdims=True)
        dcw_ref[2:3, :] += jnp.sum(dac * a, axis=0, keepdims=True)
        dcb_ref[...] += jnp.sum(dac, axis=0, keepdims=True)

    seq = pl.BlockSpec((1, S, CONV_TC), lambda c, b: (b, 0, c))
    return _call(body, "conv_bwd", (nc, Bl),
                 [seq, pl.BlockSpec((1, S, CONV_TC), lambda c, b: (b, 0, nc + c)), pl.BlockSpec((3, CONV_TC), lambda c, b: (0, c)),
                  pl.BlockSpec((1, CONV_TC), lambda c, b: (0, c)), seq],
                 (seq, seq, pl.BlockSpec((3, CONV_TC), lambda c, b: (0, c)), pl.BlockSpec((1, CONV_TC), lambda c, b: (0, c))),
                 (jax.ShapeDtypeStruct((Bl, S, D_FF), bf16), jax.ShapeDtypeStruct((Bl, S, D_FF), bf16),
                  jax.ShapeDtypeStruct((3, D_FF), f32), jax.ShapeDtypeStruct((1, D_FF), f32)),
                 sem=("arbitrary", "arbitrary"))(ab, ab, cw, cb, dact)


def _local_step(x, mem, target, p, w_in, tok, late_b, late_c, send):
    Bl, S, Dd = x.shape
    T = Bl * S
    x2d, t2d, mem2d = x.reshape(T, Dd), target.reshape(T, Dd), mem.reshape(Bl * MEM_LEN, Dd)
    b_st = jnp.pad(p["b_spatial"].T, ((0, 0), (0, 128 - N_HEAD)))
    lbl = p["lb_logits"]

    h = _rms_fwd(x2d, p["norm1_g"] + tok[0, 0], "norm1_fwd")
    proj = _mm(h, w_in, "nn", f32, "proj_fwd", 1024, 1664)
    a_out = _gmlp_fwd(proj, p["ln_v_g"], p["ln_v_b"], p["w_spatial"], b_st)
    proj3 = proj.reshape(Bl, S, IN_WIDTH)
    b_out, states = _hgrn_fwd(proj3, lbl, p["hgrn_norm_g"], Bl, S)
    b_out = b_out.reshape(T, 512)
    memn = _rms_fwd(mem2d, p["mem_norm_g"], "memnorm_fwd")
    w = late_b(b_out)
    wb = w["w_branch"]
    kv = _mm(memn, w["w_mem_kv"], "nn", f32, "kv_fwd", 512, 1024).reshape(Bl, MEM_LEN, 2 * 512)
    c_out = _attn_fwd(proj, kv, Bl, S)
    branches = (a_out, b_out, c_out)
    merged = _merge_fwd(branches, wb, proj)
    x1 = _mm(merged, w["w_out"], "nn", f32, "out_fwd", 1024, 1024, residual=x2d)
    h2 = _rms_fwd(x1, p["norm2_g"], "norm2_fwd")
    w.update(late_c(h2))
    ab = _mm(h2, w["w_up"], "nn", f32, "up_fwd", 1024, 1408)
    act = _conv_fwd(ab.reshape(Bl, S, 2 * D_FF), w["conv_w"], p["conv_b"], Bl, S).reshape(T, D_FF)
    x2 = _mm(act, w["w_down"], "nn", f32, "down_fwd", 512, 1024, residual=x1)
    loss_part, dx2, g_final = _final_loss(x2, p["final_g"], t2d)

    g_w_down = _mm(act, dx2, "tn", bf16, "down_dw", 1408, 1024, 1024)
    dact = _mm(dx2, w["w_down"], "nt", bf16, "down_dx", 1024, 1408)
    da, db, g_conv_w, g_conv_b = _conv_bwd(ab.reshape(Bl, S, 2 * D_FF), w["conv_w"], p["conv_b"], dact.reshape(Bl, S, D_FF), Bl, S)
    dab = jnp.concatenate([da.reshape(T, D_FF), db.reshape(T, D_FF)], axis=-1)
    g_w_up = _mm(h2, dab, "tn", bf16, "up_dw", 512, 1408, 1024)
    tok1 = send("c", dict(w_up=g_w_up, conv_w=g_conv_w, w_down=g_w_down))
    dh2 = _mm(dab, w["w_up"], "nt", f32, "up_dx", 1024, 1024, 1408)
    dx1, g_norm2 = _rms_bwd(x1, p["norm2_g"] + tok1[0, 0], dh2, "norm2_bwd", residual=dx2)

    g_w_out = _mm(merged, dx1, "tn", bf16, "out_dw", 1024, 1024, 1024)
    dmerged = _mm(dx1, w["w_out"], "nt", f32, "out_dx", 1024, 1024)
    dgl, dup = _merge_bwd(branches, wb, proj, dmerged)
    g_w_branch = [_mm(branches[n], dup[n], "tn", bf16, f"branch_dw{n}", 512, 1024, 1024) for n in range(3)]
    dbr = [_mm(dup[n], wb[n], "nt", bf16, f"branch_dx{n}", 1024, 512) for n in range(3)]
    dxq, dkv = _attn_bwd(proj, kv, dbr[2], Bl, S)
    dkv = dkv.reshape(Bl * MEM_LEN, 2 * 512)
    g_w_kv = _mm(memn, dkv, "tn", bf16, "kv_dw", 1024, 1024, 512)
    tok2 = send("b", dict(w_mem_kv=g_w_kv, w_branch=g_w_branch, w_out=g_w_out))
    dmemn = _mm(dkv, w["w_mem_kv"], "nt", f32, "kv_dx", 512, 1024)
    _, g_mem_norm = _rms_bwd(mem2d, p["mem_norm_g"] + tok2[0, 0], dmemn, "memnorm_bwd")
    dzuv, g_ln_g, g_ln_b, g_w_sp, g_b_sp = _gmlp_bwd(proj, p["ln_v_g"] + tok2[0, 0], p["ln_v_b"], p["w_spatial"], b_st, dbr[0])
    *dqfig, g_lbl, g_ng = _hgrn_bwd(proj3, lbl, p["hgrn_norm_g"], states, dbr[1].reshape(Bl, S, 512), Bl, S)
    dq, df, di, dg = [d.reshape(T, 512) for d in dqfig]
    dproj = jnp.concatenate([dzuv, dq, df, di, dg, dxq, dgl[0], dgl[1], dgl[2]], axis=-1)
    g_w_in = _mm(h, dproj, "tn", bf16, "proj_dw", 512, 1664, 1024)
    tok3 = send("a", dict(w_in=g_w_in))
    dh = _mm(dproj, w_in, "nt", f32, "proj_dx", 1024, 1024, 1664)
    dx, g_norm1 = _rms_bwd(x2d, p["norm1_g"] + tok3[0, 0], dh, "norm1_bwd", residual=dx1)

    gs = dict(w_spatial=g_w_sp, norm1_g=g_norm1, mem_norm_g=g_mem_norm, norm2_g=g_norm2, final_g=g_final, lb_logits=g_lbl,
              ln_v_g=g_ln_g, ln_v_b=g_ln_b, b_spatial=g_b_sp, hgrn_norm_g=g_ng, conv_b=g_conv_b)
    return loss_part, dx.reshape(Bl, S, Dd), gs


def _coords():
    return lax.axis_index("x"), lax.axis_index("y"), lax.axis_index("c")


def _slot(dev):
    return 4 * dev[0] + 2 * dev[1] + dev[2]


def _comm_call(body, name, arrays, out_shapes, n_sem):
    n = len(arrays)
    hbm = pl.BlockSpec(memory_space=pl.ANY)
    return pl.pallas_call(
        body, name=name, out_shape=out_shapes, in_specs=[hbm] * n, out_specs=[hbm] * n,
        scratch_shapes=[pltpu.SemaphoreType.DMA((n_sem, n)), pltpu.SemaphoreType.DMA((n_sem, n)), pltpu.SemaphoreType.DMA((n,))])(*arrays)


def _all_gather(blocks, name):
    n = len(blocks)

    def body(*refs):
        x_refs, o_refs, (send_sems, recv_sems, local_sems) = refs[:n], refs[n:2 * n], refs[2 * n:]
        x, y, c = _coords()
        me, sibling = (x, y, c), (x, y, 1 - c)
        chips = [(1 - x, y), (x, 1 - y), (1 - x, 1 - y)]

        def copy(a, k, block_dev, to, from_input=False):
            dst = o_refs[a].at[_slot(block_dev)]
            return pltpu.make_async_remote_copy(src_ref=x_refs[a] if from_input else dst, dst_ref=dst, send_sem=send_sems.at[k, a],
                                                recv_sem=recv_sems.at[k, a], device_id=to, device_id_type=MESH)

        mine = [pltpu.make_async_copy(x_refs[a], o_refs[a].at[_slot(me)], local_sems.at[a]) for a in range(n)]
        first = [copy(a, 0, me, sibling, True) for a in range(n)]
        first += [copy(a, 1 + j, me, (*chip, c), True) for j, chip in enumerate(chips) for a in range(n)]
        for cp in mine + first:
            cp.start()
        passed = []
        for j, chip in enumerate(chips):
            for a in range(n):
                copy(a, 1 + j, (*chip, c), me).wait_recv()
                fwd = copy(a, 4 + j, (*chip, c), sibling)
                fwd.start()
                passed.append(fwd)
        for a in range(n):
            copy(a, 0, sibling, me).wait_recv()
        for j, chip in enumerate(chips):
            for a in range(n):
                copy(a, 4 + j, (*chip, 1 - c), me).wait_recv()
        for cp in first + passed:
            cp.wait_send()
        for cp in mine:
            cp.wait()

    return _comm_call(body, name, blocks, [jax.ShapeDtypeStruct((N_DEV,) + b.shape, b.dtype) for b in blocks], 7)


def _all_to_all(parts, name):
    n = len(parts)
    rel = [(0, 0, 1), (0, 1, 0), (0, 1, 1), (1, 0, 0), (1, 0, 1), (1, 1, 0), (1, 1, 1)]

    def body(*refs):
        x_refs, o_refs, (send_sems, recv_sems, local_sems) = refs[:n], refs[n:2 * n], refs[2 * n:]
        x, y, c = _coords()
        me = (x, y, c)
        peers = [(x ^ dx, y ^ dy, c ^ dc) for dx, dy, dc in rel]

        def copy(a, k, peer):
            return pltpu.make_async_remote_copy(src_ref=x_refs[a].at[_slot(peer)], dst_ref=o_refs[a].at[_slot(me)], send_sem=send_sems.at[k, a],
                                                recv_sem=recv_sems.at[k, a], device_id=peer, device_id_type=MESH)

        def arrival(a, k, peer):
            return pltpu.make_async_remote_copy(src_ref=x_refs[a].at[_slot(me)], dst_ref=o_refs[a].at[_slot(peer)], send_sem=send_sems.at[k, a],
                                                recv_sem=recv_sems.at[k, a], device_id=peer, device_id_type=MESH)

        mine = [pltpu.make_async_copy(x_refs[a].at[_slot(me)], o_refs[a].at[_slot(me)], local_sems.at[a]) for a in range(n)]
        sends = [copy(a, k, peer) for k, peer in enumerate(peers) for a in range(n)]
        for cp in mine + sends:
            cp.start()
        for k, peer in enumerate(peers):
            for a in range(n):
                arrival(a, k, peer).wait_recv()
        for cp in sends:
            cp.wait_send()
        for cp in mine:
            cp.wait()

    return _comm_call(body, name, parts, [jax.ShapeDtypeStruct(p.shape, p.dtype) for p in parts], 7)


_HBM = pl.BlockSpec(memory_space=pltpu.HBM)
_SEM = pl.BlockSpec(memory_space=pltpu.SEMAPHORE)
_REL = [(0, 0, 1), (0, 1, 0), (0, 1, 1), (1, 0, 0), (1, 0, 1), (1, 1, 0), (1, 1, 1)]


_LINK_ORDER = (3, 1, 5, 4, 2, 6, 0)
SEND_PIECES = 4


def _pieces(shape, dtype):
    rows = shape[0]
    unit = 1 if len(shape) > 2 else (16 if dtype == bf16 else 8)
    for n in (SEND_PIECES, 2):
        if rows % (n * unit) == 0:
            return [pl.ds(i * (rows // n), rows // n) for i in range(n)]
    return [pl.ds(0, rows)]


def _split_copies(gather, src, land, send, recv, pieces):
    x, y, c = _coords()
    me = (x, y, c)
    copies = []
    for a in range(len(src)):
        block = src[a].shape if gather else src[a].shape[1:]
        for rows in (_pieces(block, src[a].dtype) if pieces else [None]):
            for k in _LINK_ORDER:
                dx, dy, dc = _REL[k]
                peer = (x ^ dx, y ^ dy, c ^ dc)
                mine, there = (src[a] if gather else src[a].at[_slot(peer)]), land[a].at[_slot(me)]
                if rows is not None:
                    mine, there = mine.at[rows], there.at[rows]
                copies.append(pltpu.make_async_remote_copy(src_ref=mine, dst_ref=there, send_sem=send[a].at[k], recv_sem=recv[a].at[k],
                                                           device_id=peer, device_id_type=MESH))
    return me, copies


def _arrivals(gather, src, land, send, recv):
    x, y, c = _coords()
    out = []
    for a in range(len(src)):
        for k, (dx, dy, dc) in enumerate(_REL):
            peer = (x ^ dx, y ^ dy, c ^ dc)
            out.append(pltpu.make_async_remote_copy(src_ref=src[a] if gather else src[a].at[_slot(peer)], dst_ref=land[a].at[_slot(peer)],
                                                    send_sem=send[a].at[k], recv_sem=recv[a].at[k], device_id=peer, device_id_type=MESH))
    return out


def _exchange_start(arrays, gather, name, after=None):
    n = len(arrays)
    e = 0 if after is None else 1
    lands = [lax.empty(((N_DEV,) + a.shape) if gather else a.shape, a.dtype) for a in arrays]

    def body(*refs):
        src, land = refs[:n], refs[n:2 * n]
        refs = refs[2 * n + e:]
        send, recv, token, local_sems = refs[:n], refs[n:2 * n], refs[4 * n], refs[4 * n + 1]
        me, out = _split_copies(gather, src, land, send, recv, True)
        local = [pltpu.make_async_copy(src[a] if gather else src[a].at[_slot(me)], land[a].at[_slot(me)], local_sems.at[a])
                 for a in range(n)]
        for cp in local:
            cp.start()
        for cp in local:
            cp.wait()
        for cp in out:
            cp.start()
        token[...] = jnp.zeros_like(token)

    sems = [pltpu.SemaphoreType.DMA((7,)) for _ in range(2 * n)]
    outs = pl.pallas_call(
        body, name=name,
        out_shape=(*sems, *[pltpu.HBM(a.shape, a.dtype) for a in arrays], *[pltpu.HBM(l.shape, l.dtype) for l in lands],
                   jax.ShapeDtypeStruct((8, 128), f32)),
        in_specs=[_HBM] * (2 * n) + [pl.BlockSpec(memory_space=pl.ANY)] * e,
        out_specs=(*[_SEM] * (2 * n), *[_HBM] * (2 * n), pl.BlockSpec(memory_space=pltpu.VMEM)),
        input_output_aliases={i: 2 * n + i for i in range(2 * n)},
        scratch_shapes=[pltpu.SemaphoreType.DMA((n,))],
        compiler_params=pltpu.CompilerParams(has_side_effects=pltpu.SideEffectType.DATAFLOW_SIDE_EFFECTING))(
        *[pltpu.with_memory_space_constraint(a, pltpu.HBM) for a in arrays],
        *[pltpu.with_memory_space_constraint(l, pltpu.HBM) for l in lands], *([after] if e else []))
    return (gather, n, outs[:4 * n]), outs[4 * n]


def _exchange_wait(handle, which, after, name):
    gather, n_all, vals = handle
    send_v, recv_v, src_v, land_v = [[vals[g * n_all + i] for i in which] for g in range(4)]
    n = len(which)

    def body(*refs):
        src, land, send, recv = refs[:n], refs[n:2 * n], refs[2 * n:3 * n], refs[3 * n:4 * n]
        for cp in _split_copies(gather, src, land, send, recv, False)[1]:
            cp.wait_send()
        for cp in _arrivals(gather, src, land, send, recv):
            cp.wait_recv()

    outs = pl.pallas_call(
        body, name=name,
        out_shape=(*[pltpu.HBM(a.shape, a.dtype) for a in src_v], *[pltpu.HBM(l.shape, l.dtype) for l in land_v]),
        in_specs=[*[_HBM] * (2 * n), *[_SEM] * (2 * n), pl.BlockSpec(memory_space=pl.ANY)], out_specs=[_HBM] * (2 * n),
        input_output_aliases={i: i for i in range(2 * n)},
        compiler_params=pltpu.CompilerParams(has_side_effects=pltpu.SideEffectType.DATAFLOW_SIDE_EFFECTING))(
        *src_v, *land_v, *send_v, *recv_v, after)
    return outs[n:]


def _seq_exchange(arrays, gather, name, collective_id):
    n = len(arrays)
    hbm = pltpu.MemorySpace.HBM
    srcs = [jax.new_ref(a, memory_space=hbm) for a in arrays]
    lands = [jax.empty_ref(jax.ShapeDtypeStruct(((N_DEV,) + a.shape) if gather else a.shape, a.dtype), memory_space=hbm) for a in arrays]

    @pl.kernel(mesh=plsc.ScalarSubcoreMesh(axis_name="sequencer", num_cores=1), name=name,
               scratch_types=(pltpu.SemaphoreType.DMA((7, n)), pltpu.SemaphoreType.DMA((7, n)), pltpu.SemaphoreType.DMA((n,))),
               compiler_params=pltpu.CompilerParams(collective_id=collective_id))
    def launch(send, recv, local):
        x, y, c = _coords()
        me = (x, y, c)
        peers = [(x ^ dx, y ^ dy, c ^ dc) for dx, dy, dc in _REL]
        barrier = pltpu.get_barrier_semaphore()
        for peer in peers:
            pl.semaphore_signal(barrier, inc=1, device_id=peer, device_id_type=MESH)
        pl.semaphore_wait(barrier, len(peers))

        def copy(a, k, peer, arrival):
            return pltpu.make_async_remote_copy(
                src_ref=srcs[a] if gather else srcs[a].at[_slot(peer)], dst_ref=lands[a].at[_slot(peer if arrival else me)],
                send_sem=send.at[k, a], recv_sem=recv.at[k, a], device_id=peer, device_id_type=MESH)

        mine = [pltpu.make_async_copy(srcs[a] if gather else srcs[a].at[_slot(me)], lands[a].at[_slot(me)], local.at[a])
                for a in range(n)]
        out = [copy(a, k, peer, False) for a in range(n) for k, peer in enumerate(peers)]
        for cp in mine + out:
            cp.start()
        for a in range(n):
            for k, peer in enumerate(peers):
                copy(a, k, peer, True).wait_recv()
        for cp in out:
            cp.wait_send()
        for cp in mine:
            cp.wait()

    launch()
    return [land[...] for land in lands]


def _adam_math(w, g, m, v):
    m_ = ADAM_B1 * m + (1.0 - ADAM_B1) * g
    v_ = ADAM_B2 * v + (1.0 - ADAM_B2) * jnp.square(g)
    m_hat = m_ / (1.0 - ADAM_B1 ** ADAM_STEP)
    v_hat = v_ / (1.0 - ADAM_B2 ** ADAM_STEP)
    return -ADAM_LR * (m_hat / (jnp.sqrt(v_hat) + ADAM_EPS) + ADAM_WD * w), m_, v_


def _reduce_adamw(parts, w, m, v, name):
    _, R, L = parts.shape
    tr = _pick(R, (256, 128, 64, 32, 16, 8))

    def body(p_ref, w_ref, m_ref, v_ref, g_ref, d_ref, nm_ref, nv_ref):
        g = p_ref[0].astype(f32)
        for i in range(1, N_DEV):
            g = g + p_ref[i].astype(f32)
        g_ref[...] = g
        d_ref[...], nm_ref[...], nv_ref[...] = _adam_math(w_ref[...], g, m_ref[...], v_ref[...])

    blk = pl.BlockSpec((tr, L), lambda i: (i, 0))
    sh = jax.ShapeDtypeStruct((R, L), f32)
    return _call(body, name, (R // tr,), [pl.BlockSpec((N_DEV, tr, L), lambda i: (0, i, 0)), blk, blk, blk], (blk,) * 4, (sh,) * 4,
                 sem=("parallel",))(parts, w, m, v)


SMALL = (("w_spatial", (512, 128), 0), ("norm1_g", (1, 1024), 512), ("mem_norm_g", (1, 1024), 520), ("norm2_g", (1, 1024), 528),
         ("final_g", (1, 1024), 536), ("lb_logits", (2, 512), 544), ("ln_v_g", (1, 512), 552), ("ln_v_b", (1, 512), 556),
         ("b_spatial", (4, 128), 560), ("hgrn_norm_g", (1, 128), 564), ("conv_b", (1, 2816), 565))
SMALL_USED, SMALL_ROWS = 587, 640


def _segments(shape, base):
    r, n = shape
    per = n // 128
    return [(base + i * per + j, i, slice(j * 128, (j + 1) * 128)) for i in range(r) for j in range(per)]


def _pack_small(gs):
    names = [n for n, _, _ in SMALL]

    def body(*refs):
        src, o_ref = dict(zip(names, refs[:-1])), refs[-1]
        o_ref[SMALL_USED:SMALL_ROWS, :] = jnp.zeros((SMALL_ROWS - SMALL_USED, 128), f32)
        for name, shape, base in SMALL:
            ref = src[name]
            if name == "w_spatial":
                o_ref[base:base + 512, :] = ref[...].reshape(512, 128)
            elif name == "b_spatial":
                o_ref[base:base + 4, :] = ref[0:4, :]
            elif name == "hgrn_norm_g":
                per_head = [ref[b, h] for b in range(ref.shape[0]) for h in range(N_HEAD)]
                o_ref[base:base + 1, :] = functools.reduce(lambda u, v_: u + v_, per_head)
            else:
                for row, i, sl in _segments(shape, base):
                    o_ref[row:row + 1, :] = ref[i:i + 1, sl]

    return pl.pallas_call(body, name="pack_small", out_shape=jax.ShapeDtypeStruct((SMALL_ROWS, 128), f32))(*[gs[n] for n in names])


def _small_update(gathered, w, m, v):
    names = [n for n, _, _ in SMALL]
    k = len(names)

    def body(*refs):
        p_ref = refs[0]
        ins = [dict(zip(names, refs[1 + i * k:1 + (i + 1) * k])) for i in range(3)]
        outs = [dict(zip(names, refs[1 + (3 + i) * k:1 + (4 + i) * k])) for i in range(4)]
        gsum = refs[-1]
        g = p_ref[0]
        for i in range(1, N_DEV):
            g = g + p_ref[i]
        gsum[...] = g
        for name, shape, base in SMALL:
            if name == "w_spatial":
                where = [(slice(base, base + 512), (slice(None), slice(None)))]
            else:
                where = [(slice(row, row + 1), (slice(i, i + 1), sl)) for row, i, sl in _segments(shape, base)]
            for rows, at in where:
                g_ = gsum[rows, :]
                d_, m_, v_ = _adam_math(ins[0][name][at], g_, ins[1][name][at], ins[2][name][at])
                for o, val in zip(outs, (g_, d_, m_, v_)):
                    o[name][at] = val

    args = [gathered] + [d[n] for d in (w, m, v) for n in names]
    out_shapes = [jax.ShapeDtypeStruct(shape, f32) for _ in range(4) for _, shape, _ in SMALL]
    outs = pl.pallas_call(body, name="small_update", out_shape=out_shapes, scratch_shapes=[pltpu.VMEM((SMALL_ROWS, 128), f32)])(*args)
    return [dict(zip(names, outs[i * k:(i + 1) * k])) for i in range(4)]


def _cols_full(g):
    return jnp.moveaxis(g, 0, -2).reshape(g.shape[1:-1] + (N_DEV * g.shape[-1],))


def _cols_parts(full):
    n = full.shape[-1] // N_DEV
    return jnp.moveaxis(full.reshape(full.shape[:-1] + (N_DEV, n)), -2, 0)


def kernel(x, mem, norm1_g, w_in, ln_v_g, ln_v_b, w_spatial, b_spatial, lb_logits, hgrn_norm_g, mem_norm_g, w_mem_kv, w_branch, w_out, norm2_g, w_up, conv_w, conv_b, w_down, final_g, loss_target, m_norm1_g, m_w_in, m_ln_v_g, m_ln_v_b, m_w_spatial, m_b_spatial, m_lb_logits, m_hgrn_norm_g, m_mem_norm_g, m_w_mem_kv, m_w_branch, m_w_out, m_norm2_g, m_w_up, m_conv_w, m_conv_b, m_w_down, m_final_g, v_norm1_g, v_w_in, v_ln_v_g, v_ln_v_b, v_w_spatial, v_b_spatial, v_lb_logits, v_hgrn_norm_g, v_mem_norm_g, v_w_mem_kv, v_w_branch, v_w_out, v_norm2_g, v_w_up, v_conv_w, v_conv_b, v_w_down, v_final_g):
    given = dict(locals())
    order = ("norm1_g", "w_in", "ln_v_g", "ln_v_b", "w_spatial", "b_spatial", "lb_logits", "hgrn_norm_g", "mem_norm_g",
             "w_mem_kv", "w_branch", "w_out", "norm2_g", "w_up", "conv_w", "conv_b", "w_down", "final_g")
    groups = dict(a=("w_in",), b=("w_mem_kv", "w_branch", "w_out"), c=("w_up", "conv_w", "w_down"))

    wire = {n: given[n][0].astype(f32 if n == "conv_w" else bf16) for ns in groups.values() for n in ns}
    g_in = _all_gather([wire["w_in"]], "gather_w_in")[0]
    late = groups["b"] + groups["c"]
    g_in, rest_wire = lax.optimization_barrier((g_in, [wire[n] for n in late]))
    rest = _seq_exchange(rest_wire, True, "gather_rest", 1)
    tok = jnp.zeros((8, 128), f32)

    def late_b(after):
        kv_, br_, out_ = rest[0:3]
        br_ = _cols_full(br_)
        return dict(w_mem_kv=kv_.reshape(D_MODEL, 2 * 512), w_branch=[br_[n] for n in range(3)], w_out=out_.reshape(D_MODEL, D_MODEL))

    def late_c(after):
        up_, cw_, down_ = rest[3:6]
        return dict(w_up=_cols_full(up_), conv_w=_cols_full(cw_), w_down=down_.reshape(D_FF, D_MODEL))

    to_parts = dict(w_in=_cols_parts, w_up=_cols_parts, conv_w=_cols_parts,
                    w_branch=lambda g_: _cols_parts(jnp.stack(g_)).reshape(N_DEV, -1, 128),
                    w_mem_kv=lambda g_: g_.reshape(N_DEV, -1, 2 * 512), w_out=lambda g_: g_.reshape(N_DEV, -1, D_MODEL),
                    w_down=lambda g_: g_.reshape(N_DEV, -1, D_MODEL))
    scatters = {}

    def send(tag, grads_):
        scatters[tag] = _seq_exchange([to_parts[n](grads_[n]) for n in groups[tag]], False, f"scatter_{tag}", dict(c=2, b=3, a=4)[tag])
        return tok

    small_2d = lambda prefix: {n: given[prefix + n].reshape(shape) for n, shape, _ in SMALL}
    p = small_2d("")
    p["w_spatial"] = w_spatial[0]
    loss_part, grad_x, gs = _local_step(x, mem, loss_target, p, _cols_full(g_in), tok, late_b, late_c, send)
    loss = lax.psum(loss_part[0, 0], ("x", "y", "c"))

    gathered = _seq_exchange([_pack_small(gs)], True, "gather_small", 5)[0]

    grads, delta, new_m, new_v = {}, {}, {}, {}
    for tag in ("c", "b", "a"):
        for n, parts in zip(groups[tag], scatters[tag]):
            two_d = (-1, given[n].shape[-1])
            res = _reduce_adamw(parts, *[given[pre + n].reshape(two_d) for pre in ("", "m_", "v_")], "adamw_" + n)
            grads[n], delta[n], new_m[n], new_v[n] = [r.reshape(given[n].shape) for r in res]

    for dst, res in zip((grads, delta, new_m, new_v), _small_update(gathered, small_2d(""), small_2d("m_"), small_2d("v_"))):
        for n, _, _ in SMALL:
            dst[n] = res[n].reshape(given[n].shape)

    return (loss, grad_x, *[grads[n] for n in order], *[delta[n] for n in order], *[new_m[n] for n in order],
            *[new_v[n] for n in order])
```

```python
import functools

import jax
import jax.numpy as jnp
from jax import lax
from jax.experimental import pallas as pl
from jax.experimental.pallas import tpu as pltpu

f32 = jnp.float32
bf16 = jnp.bfloat16

N_DEV = 8
D_MODEL = 1024
EPS = 1e-6
GM_CHUNK = 128
HG_CHUNK = 64
HEAD = 128
N_HEAD = 4
MEM_LEN = 256
D_FF = 2816
IN_WIDTH = 6656
C_ZU, C_HQ, C_HF, C_HI, C_HG, C_XQ, C_GL = 0, 1024, 1536, 2048, 2560, 3072, 3584
ADAM_LR, ADAM_B1, ADAM_B2, ADAM_EPS, ADAM_WD, ADAM_STEP = 0.001, 0.9, 0.999, 1e-08, 0.01, 10
VMEM_LIMIT = 56 * 1024 * 1024
MESH = pl.DeviceIdType.MESH


def _pick(n, cands):
    for c in cands:
        if n % c == 0:
            return c
    return n


def _call(body, name, grid, in_specs, out_specs, out_shape, scratch=(), sem=None, **cp):
    params = dict(vmem_limit_bytes=VMEM_LIMIT, **cp)
    if sem is not None:
        params["dimension_semantics"] = sem
    return pl.pallas_call(
        body, name=name, grid=grid, in_specs=in_specs, out_specs=out_specs, out_shape=out_shape,
        scratch_shapes=list(scratch), compiler_params=pltpu.CompilerParams(**params))


_DN = {"nn": (((1,), (0,)), ((), ())), "nt": (((1,), (1,)), ((), ())), "tn": (((0,), (0,)), ((), ()))}


def _raw_dot(a, b, mode):
    return lax.dot_general(a.astype(bf16), b.astype(bf16), _DN[mode], preferred_element_type=f32)


@jax.custom_vjp
def _dot_nn(a, b):
    return _raw_dot(a, b, "nn")


_dot_nn.defvjp(lambda a, b: (_raw_dot(a, b, "nn"), (a, b)),
               lambda r, g: (_raw_dot(g, r[1], "nt"), _raw_dot(r[0], g, "tn")))


@jax.custom_vjp
def _dot_nt(a, b):
    return _raw_dot(a, b, "nt")


_dot_nt.defvjp(lambda a, b: (_raw_dot(a, b, "nt"), (a, b)),
               lambda r, g: (_raw_dot(g, r[1], "nn"), _raw_dot(g, r[0], "tn")))


@jax.custom_vjp
def _dot_tn(a, b):
    return _raw_dot(a, b, "tn")


_dot_tn.defvjp(lambda a, b: (_raw_dot(a, b, "tn"), (a, b)),
               lambda r, g: (_raw_dot(r[1], g, "nt"), _raw_dot(r[0], g, "nn")))


def _tri(n, lower):
    r = lax.broadcasted_iota(jnp.int32, (n, n), 0)
    c = lax.broadcasted_iota(jnp.int32, (n, n), 1)
    return ((c <= r) if lower else (c >= r)).astype(f32)


def _sel_dot(sel, x, mode, x_first=False):
    hi = x.astype(bf16)
    rest = x - hi.astype(f32)
    mid = rest.astype(bf16)
    lo = (rest - mid.astype(f32)).astype(bf16)
    sel = sel.astype(bf16)
    dot = lambda piece: lax.dot_general(*((piece, sel) if x_first else (sel, piece)), _DN[mode], preferred_element_type=f32)
    return dot(hi) + dot(mid) + dot(lo)


def _egrad(fn, x, ct):
    return jax.vjp(fn, x)[1](ct)[0]


def _mm(a, b, mode, out_dtype, name, tm, tn, tk=None, residual=None):
    if mode == "nn":
        (M, K), (_, N) = a.shape, b.shape
    elif mode == "nt":
        (M, K), (N, _) = a.shape, b.shape
    else:
        (K, M), (_, N) = a.shape, b.shape
    tm, tn = min(tm, M), min(tn, N)
    tk = K if tk is None else min(tk, K)
    assert M % tm == 0 and N % tn == 0 and K % tk == 0, (name, M, N, K, tm, tn, tk)
    nk = K // tk

    def body(*refs):
        acc_ref = refs[-1] if nk > 1 else None
        refs = refs[:-1] if nk > 1 else refs
        if residual is None:
            a_ref, b_ref, o_ref = refs
        else:
            a_ref, b_ref, r_ref, o_ref = refs

        def finish(r):
            if residual is not None:
                r = r + r_ref[...]
            o_ref[...] = r.astype(out_dtype)

        part = _raw_dot(a_ref[...], b_ref[...], mode)
        if nk == 1:
            finish(part)
            return
        k = pl.program_id(2)

        @pl.when(k == 0)
        def _():
            acc_ref[...] = part

        @pl.when((k > 0) & (k < nk - 1))
        def _():
            acc_ref[...] += part

        @pl.when(k == nk - 1)
        def _():
            finish(acc_ref[...] + part)

    a_spec = {"nn": pl.BlockSpec((tm, tk), lambda i, j, k: (i, k)),
              "nt": pl.BlockSpec((tm, tk), lambda i, j, k: (i, k)),
              "tn": pl.BlockSpec((tk, tm), lambda i, j, k: (k, i))}[mode]
    b_spec = {"nn": pl.BlockSpec((tk, tn), lambda i, j, k: (k, j)),
              "nt": pl.BlockSpec((tn, tk), lambda i, j, k: (j, k)),
              "tn": pl.BlockSpec((tk, tn), lambda i, j, k: (k, j))}[mode]
    o_spec = pl.BlockSpec((tm, tn), lambda i, j, k: (i, j))
    in_specs = [a_spec, b_spec] + ([o_spec] if residual is not None else [])
    args = (a, b) + ((residual,) if residual is not None else ())
    return _call(body, name, (M // tm, N // tn, nk), in_specs, o_spec, jax.ShapeDtypeStruct((M, N), out_dtype),
                 scratch=[pltpu.VMEM((tm, tn), f32)] if nk > 1 else [], sem=("parallel", "parallel", "arbitrary"))(*args)


def _rms_fwd(x, g, name):
    R, Dd = x.shape
    tr = _pick(R, (512, 256, 128))

    def body(x_ref, g_ref, o_ref):
        xf = x_ref[...]
        y = xf * lax.rsqrt(jnp.mean(xf * xf, axis=-1, keepdims=True) + EPS)
        o_ref[...] = (y * g_ref[...]).astype(bf16)

    return _call(body, name, (R // tr,), [pl.BlockSpec((tr, Dd), lambda i: (i, 0)), pl.BlockSpec((1, Dd), lambda i: (0, 0))],
                 pl.BlockSpec((tr, Dd), lambda i: (i, 0)), jax.ShapeDtypeStruct((R, Dd), bf16), sem=("parallel",))(x, g)


def _rms_bwd(x, g, dh, name, residual=None):
    R, Dd = x.shape
    tr = _pick(R, (512, 256, 128))

    def body(*refs):
        if residual is None:
            x_ref, g_ref, dh_ref, dx_ref, dg_ref = refs
        else:
            x_ref, g_ref, dh_ref, r_ref, dx_ref, dg_ref = refs
        xf = x_ref[...]
        rs = lax.rsqrt(jnp.mean(xf * xf, axis=-1, keepdims=True) + EPS)
        y = xf * rs
        dh_ = dh_ref[...].astype(f32)
        dy = dh_ * g_ref[...]
        dx = rs * (dy - y * jnp.mean(dy * y, axis=-1, keepdims=True))
        if residual is not None:
            dx = dx + r_ref[...]
        dx_ref[...] = dx

        @pl.when(pl.program_id(0) == 0)
        def _():
            dg_ref[...] = jnp.zeros_like(dg_ref)

        dg_ref[...] += jnp.sum(dh_ * y, axis=0, keepdims=True)

    row = pl.BlockSpec((tr, Dd), lambda i: (i, 0))
    vec = pl.BlockSpec((1, Dd), lambda i: (0, 0))
    in_specs = [row, vec, row] + ([row] if residual is not None else [])
    args = (x, g, dh) + ((residual,) if residual is not None else ())
    return _call(body, name, (R // tr,), in_specs, (row, vec),
                 (jax.ShapeDtypeStruct((R, Dd), f32), jax.ShapeDtypeStruct((1, Dd), f32)), sem=("arbitrary",))(*args)


def _final_loss(x2, g, target):
    R, Dd = x2.shape
    tr = _pick(R, (512, 256, 128))

    def body(x_ref, g_ref, t_ref, loss_ref, dx_ref, dg_ref):
        xf = x_ref[...]
        rs = lax.rsqrt(jnp.mean(xf * xf, axis=-1, keepdims=True) + EPS)
        y = xf * rs
        err = y * g_ref[...] - t_ref[...]
        dh_ = err * (1.0 / Dd)
        dy = dh_ * g_ref[...]
        dx_ref[...] = rs * (dy - y * jnp.mean(dy * y, axis=-1, keepdims=True))

        @pl.when(pl.program_id(0) == 0)
        def _():
            dg_ref[...] = jnp.zeros_like(dg_ref)
            loss_ref[...] = jnp.zeros_like(loss_ref)

        dg_ref[...] += jnp.sum(dh_ * y, axis=0, keepdims=True)
        part = jnp.sum(jnp.mean(err * err, axis=-1, keepdims=True), axis=0, keepdims=True)
        loss_ref[...] += 0.5 * part

    row = pl.BlockSpec((tr, Dd), lambda i: (i, 0))
    vec = pl.BlockSpec((1, Dd), lambda i: (0, 0))
    return _call(body, "final_loss", (R // tr,), [row, vec, row], (pl.BlockSpec((1, 128), lambda i: (0, 0)), row, vec),
                 (jax.ShapeDtypeStruct((1, 128), f32), jax.ShapeDtypeStruct((R, Dd), f32), jax.ShapeDtypeStruct((1, Dd), f32)),
                 sem=("arbitrary",))(x2, g, target)


def _gmlp_parts(zuv, ln_g, ln_b):
    zu, zv = zuv[:, :512], zuv[:, 512:]
    u = jax.nn.gelu(zu)
    v = jax.nn.gelu(zv)
    mu = jnp.mean(v, axis=-1, keepdims=True)
    rs = lax.rsqrt(jnp.mean(jnp.square(v - mu), axis=-1, keepdims=True) + EPS)
    xh = (v - mu) * rs
    return zu, zv, u, xh, rs, xh * ln_g + ln_b


def _gmlp_fwd(proj, ln_g, ln_b, w_s, b_st):
    T = proj.shape[0]

    def body(p_ref, g_ref, b_ref, w_ref, bs_ref, o_ref):
        _, _, u, _, _, vn = _gmlp_parts(p_ref[...].astype(f32), g_ref[...], b_ref[...])
        causal = _tri(GM_CHUNK, True) > 0
        for gi in range(N_HEAD):
            sl = slice(gi * HEAD, (gi + 1) * HEAD)
            w = jnp.where(causal, w_ref[gi], 0.0)
            mixed = _raw_dot(w, vn[:, sl], "nn") + bs_ref[:, gi:gi + 1]
            o_ref[:, sl] = (u[:, sl] * mixed).astype(bf16)

    vec = pl.BlockSpec((1, 512), lambda i: (0, 0))
    return _call(body, "gmlp_fwd", (T // GM_CHUNK,),
                 [pl.BlockSpec((GM_CHUNK, 1024), lambda i: (i, 0)), vec, vec,
                  pl.BlockSpec((N_HEAD, GM_CHUNK, GM_CHUNK), lambda i: (0, 0, 0)), pl.BlockSpec((GM_CHUNK, 128), lambda i: (0, 0))],
                 pl.BlockSpec((GM_CHUNK, 512), lambda i: (i, 0)), jax.ShapeDtypeStruct((T, 512), bf16), sem=("parallel",))(
        proj, ln_g, ln_b, w_s, b_st)


def _gmlp_bwd(proj, ln_g, ln_b, w_s, b_st, da):
    T = proj.shape[0]

    def body(p_ref, g_ref, b_ref, w_ref, bs_ref, da_ref, dp_ref, dg_ref, db_ref, dw_ref, dbs_ref):
        zu, zv, u, xh, rs, vn = _gmlp_parts(p_ref[...].astype(f32), g_ref[...], b_ref[...])
        causal = _tri(GM_CHUNK, True) > 0
        sub = lax.broadcasted_iota(jnp.int32, (8, GM_CHUNK), 0)
        ones = jnp.ones((8, HEAD), f32)
        dout = da_ref[...].astype(f32)

        @pl.when(pl.program_id(0) == 0)
        def _():
            for r in (dg_ref, db_ref, dw_ref, dbs_ref):
                r[...] = jnp.zeros_like(r)

        du, dvn, dbs = [], [], jnp.zeros((8, GM_CHUNK), f32)
        for gi in range(N_HEAD):
            sl = slice(gi * HEAD, (gi + 1) * HEAD)
            w = jnp.where(causal, w_ref[gi], 0.0)
            mixed = _raw_dot(w, vn[:, sl], "nn") + bs_ref[:, gi:gi + 1]
            du.append(dout[:, sl] * mixed)
            dm = dout[:, sl] * u[:, sl]
            row_sums = _sel_dot(ones, dm, "nt")
            dbs = dbs + jnp.where(sub == gi, row_sums, 0.0)
            dw_ref[gi] += jnp.where(causal, _raw_dot(dm, vn[:, sl], "nt"), 0.0)
            dvn.append(_raw_dot(w, dm, "tn"))
        dbs_ref[...] += dbs
        du = jnp.concatenate(du, axis=-1)
        dvn = jnp.concatenate(dvn, axis=-1)
        dg_ref[...] += jnp.sum(dvn * xh, axis=0, keepdims=True)
        db_ref[...] += jnp.sum(dvn, axis=0, keepdims=True)
        dxh = dvn * g_ref[...]
        dv = rs * (dxh - jnp.mean(dxh, axis=-1, keepdims=True) - xh * jnp.mean(dxh * xh, axis=-1, keepdims=True))
        dp_ref[:, :512] = _egrad(jax.nn.gelu, zu, du).astype(bf16)
        dp_ref[:, 512:] = _egrad(jax.nn.gelu, zv, dv).astype(bf16)

    vec = pl.BlockSpec((1, 512), lambda i: (0, 0))
    wsp = pl.BlockSpec((N_HEAD, GM_CHUNK, GM_CHUNK), lambda i: (0, 0, 0))
    return _call(body, "gmlp_bwd", (T // GM_CHUNK,),
                 [pl.BlockSpec((GM_CHUNK, 1024), lambda i: (i, 0)), vec, vec, wsp, pl.BlockSpec((GM_CHUNK, 128), lambda i: (0, 0)),
                  pl.BlockSpec((GM_CHUNK, 512), lambda i: (i, 0))],
                 (pl.BlockSpec((GM_CHUNK, 1024), lambda i: (i, 0)), vec, vec, wsp, pl.BlockSpec((8, GM_CHUNK), lambda i: (0, 0))),
                 (jax.ShapeDtypeStruct((T, 1024), bf16), jax.ShapeDtypeStruct((1, 512), f32), jax.ShapeDtypeStruct((1, 512), f32),
                  jax.ShapeDtypeStruct((N_HEAD, GM_CHUNK, GM_CHUNK), f32), jax.ShapeDtypeStruct((8, GM_CHUNK), f32)),
                 sem=("arbitrary",))(proj, ln_g, ln_b, w_s, b_st, da)


HG_SUB = 8
HG_NSUB = HG_CHUNK // HG_SUB


def _two_level_matrix():
    r = lax.broadcasted_iota(jnp.int32, (2 * HG_CHUNK, HG_CHUNK), 0)
    c = lax.broadcasted_iota(jnp.int32, (2 * HG_CHUNK, HG_CHUNK), 1)
    t = jnp.where(r < HG_CHUNK, r, r - HG_CHUNK)
    local = (r < HG_CHUNK) & (t // HG_SUB == c // HG_SUB) & (c <= t)
    before = (r >= HG_CHUNK) & (c < (t // HG_SUB) * HG_SUB)
    return (local | before).astype(f32)


def _two_level_sums(x):
    two = _sel_dot(_two_level_matrix(), x, "nn")
    return two[:HG_CHUNK], two[HG_CHUNK:]


@jax.custom_vjp
def _two_level_cumsum(x):
    return _two_level_sums(x)


_two_level_cumsum.defvjp(
    lambda x: (_two_level_sums(x), None),
    lambda _, g: (_sel_dot(_two_level_matrix(), jnp.concatenate(g, axis=0), "tn"),))


def _tile_matrix():
    s = lax.broadcasted_iota(jnp.int32, (HG_SUB, HG_CHUNK), 0)
    j = lax.broadcasted_iota(jnp.int32, (HG_SUB, HG_CHUNK), 1)
    return (j % HG_SUB == s).astype(f32)


@jax.custom_vjp
def _tile_lanes(x):
    return _sel_dot(_tile_matrix(), x, "nn", x_first=True)


_tile_lanes.defvjp(
    lambda x: (_sel_dot(_tile_matrix(), x, "nn", x_first=True), None),
    lambda _, g: (_sel_dot(_tile_matrix(), g, "nt", x_first=True),))


def _block_rows(x):
    k = x.shape[-1]
    return jnp.broadcast_to(x.reshape(HG_NSUB, 1, HG_SUB, k), (HG_NSUB, HG_SUB, HG_SUB, k)).reshape(HG_CHUNK, HG_SUB, k)


def _hgrn_chunk(st0, q_raw, f_raw, i_raw, g_raw, l0, l1, ng):
    C, SUB = HG_CHUNK, HG_SUB
    lb = jax.nn.sigmoid(l0 - l1)
    fg = lb + (1.0 - lb) * jax.nn.sigmoid(f_raw)
    kk = 1.0 - fg
    qf = jax.nn.silu(q_raw)
    al, base = _two_level_cumsum(jnp.log(fg))
    a = al + base
    row = lax.broadcasted_iota(jnp.int32, (C, HEAD), 0)
    a_last = jnp.sum(jnp.where(row == C - 1, a, 0.0), axis=0, keepdims=True)
    inter = _dot_nt(qf * jnp.exp(a), st0)
    qt = qf * jnp.exp(al)
    rb = lax.broadcasted_iota(jnp.int32, (C, C), 0) // SUB
    cb = lax.broadcasted_iota(jnp.int32, (C, C), 1) // SUB
    scores = jnp.zeros((C, C), f32)
    for i in range(1, HG_NSUB):
        base_i = jnp.sum(jnp.where(row == i * SUB, base, 0.0), axis=0, keepdims=True)
        kt = kk * jnp.exp(jnp.minimum(base_i - a, 0.0))
        scores = scores + jnp.where((rb == i) & (cb < i), _dot_nt(qt, kt), 0.0)
    t_i = lax.broadcasted_iota(jnp.int32, (C, SUB, HEAD), 0) % SUB
    s_i = lax.broadcasted_iota(jnp.int32, (C, SUB, HEAD), 1)
    decay = jnp.exp(jnp.where(s_i <= t_i, al[:, None, :] - _block_rows(al), -jnp.inf))
    diag = jnp.sum(qf[:, None, :] * decay * _block_rows(kk), axis=-1)
    scores = scores + jnp.where(rb == cb, _tile_lanes(diag), 0.0)
    o = inter + _dot_nn(scores, i_raw)
    st1 = jnp.exp(a_last) * st0 + _dot_tn(i_raw, kk * jnp.exp(a_last - a))
    on = o * lax.rsqrt(jnp.mean(o * o, axis=-1, keepdims=True) + EPS) * ng
    return st1, on * jax.nn.silu(g_raw)


def _hgrn_specs(S, Bl, rev):
    N = S // HG_CHUNK
    chunk = (lambda n: N - 1 - n) if rev else (lambda n: n)
    col = lambda c0: pl.BlockSpec((Bl, HG_CHUNK, 512), lambda n: (0, chunk(n), c0 // 512))
    st = pl.BlockSpec((Bl, N_HEAD, 1, HEAD, HEAD), lambda n: (0, 0, chunk(n), 0, 0))
    full = lambda *s: pl.BlockSpec(s, functools.partial(lambda n, nd: (0,) * nd, nd=len(s)))
    return N, col, st, full


def _hgrn_fwd(proj, lb_logits, ng, Bl, S):
    N, col, st, full = _hgrn_specs(S, Bl, False)

    def body(q_ref, f_ref, i_ref, g_ref, l_ref, ng_ref, o_ref, st_ref, state):
        @pl.when(pl.program_id(0) == 0)
        def _():
            state[...] = jnp.zeros_like(state)

        for b in range(Bl):
            for h in range(N_HEAD):
                sl = slice(h * HEAD, (h + 1) * HEAD)
                st0 = state[b, h]
                st_ref[b, h, 0] = st0
                st1, out = _hgrn_chunk(st0, *[r[b, :, sl].astype(f32) for r in (q_ref, f_ref, i_ref, g_ref)],
                                       l_ref[0:1, sl], l_ref[1:2, sl], ng_ref[...])
                state[b, h] = st1
                o_ref[b, :, sl] = out.astype(bf16)

    return _call(body, "hgrn_fwd", (N,), [col(C_HQ), col(C_HF), col(C_HI), col(C_HG), full(2, 512), full(1, HEAD)],
                 (col(0), st),
                 (jax.ShapeDtypeStruct((Bl, S, 512), bf16), jax.ShapeDtypeStruct((Bl, N_HEAD, N, HEAD, HEAD), f32)),
                 scratch=[pltpu.VMEM((Bl, N_HEAD, HEAD, HEAD), f32)], sem=("arbitrary",))(
        proj, proj, proj, proj, lb_logits, ng)


def _hgrn_bwd(proj, lb_logits, ng, states, db, Bl, S):
    N, col, st, full = _hgrn_specs(S, Bl, True)

    def body(q_ref, f_ref, i_ref, g_ref, l_ref, ng_ref, st_ref, db_ref,
             dq_ref, df_ref, di_ref, dg_ref, dl_ref, dng_ref, dstate):
        @pl.when(pl.program_id(0) == 0)
        def _():
            dstate[...] = jnp.zeros_like(dstate)
            dl_ref[...] = jnp.zeros_like(dl_ref)
            dng_ref[...] = jnp.zeros_like(dng_ref)

        for b in range(Bl):
            for h in range(N_HEAD):
                sl = slice(h * HEAD, (h + 1) * HEAD)
                _, vjp = jax.vjp(_hgrn_chunk, st_ref[b, h, 0], *[r[b, :, sl].astype(f32) for r in (q_ref, f_ref, i_ref, g_ref)],
                                 l_ref[0:1, sl], l_ref[1:2, sl], ng_ref[...])
                dst0, dq, df, di, dg, dl0, dl1, dng = vjp((dstate[b, h], db_ref[b, :, sl].astype(f32)))
                dstate[b, h] = dst0
                dq_ref[b, :, sl] = dq.astype(bf16)
                df_ref[b, :, sl] = df.astype(bf16)
                di_ref[b, :, sl] = di.astype(bf16)
                dg_ref[b, :, sl] = dg.astype(bf16)
                dl_ref[0:1, sl] += dl0
                dl_ref[1:2, sl] += dl1
                dng_ref[b, h] += dng

    return _call(body, "hgrn_bwd", (N,),
                 [col(C_HQ), col(C_HF), col(C_HI), col(C_HG), full(2, 512), full(1, HEAD), st, col(0)],
                 (*[col(0)] * 4, full(2, 512), full(Bl, N_HEAD, 1, HEAD)),
                 (*[jax.ShapeDtypeStruct((Bl, S, 512), bf16)] * 4, jax.ShapeDtypeStruct((2, 512), f32),
                  jax.ShapeDtypeStruct((Bl, N_HEAD, 1, HEAD), f32)),
                 scratch=[pltpu.VMEM((Bl, N_HEAD, HEAD, HEAD), f32)], sem=("arbitrary",))(
        proj, proj, proj, proj, lb_logits, ng, states, db)


def _attn_probs(q, k):
    s = _raw_dot(q, k, "nt") * (HEAD ** -0.5)
    e = jnp.exp(s - jnp.max(s, axis=-1, keepdims=True))
    return e / jnp.sum(e, axis=-1, keepdims=True)


def _attn_specs(S, tq):
    nq = S // tq
    q = pl.BlockSpec((tq, 512), lambda b, i: (b * nq + i, C_XQ // 512))
    kv = pl.BlockSpec((1, MEM_LEN, 1024), lambda b, i: (b, 0, 0))
    o = pl.BlockSpec((tq, 512), lambda b, i: (b * nq + i, 0))
    return nq, q, kv, o


def _attn_fwd(proj, kv, Bl, S):
    tq = _pick(S, (512, 256, 128))
    nq, qs, kvs, os_ = _attn_specs(S, tq)

    def body(q_ref, kv_ref, o_ref):
        for h in range(N_HEAD):
            sl = slice(h * HEAD, (h + 1) * HEAD)
            p = _attn_probs(q_ref[:, sl], kv_ref[0, :, sl])
            o_ref[:, sl] = _raw_dot(p, kv_ref[0, :, 512 + h * HEAD:512 + (h + 1) * HEAD], "nn").astype(bf16)

    return _call(body, "attn_fwd", (Bl, nq), [qs, kvs], os_, jax.ShapeDtypeStruct((Bl * S, 512), bf16),
                 sem=("parallel", "parallel"))(proj, kv)


def _attn_bwd(proj, kv, dc, Bl, S):
    tq = _pick(S, (512, 256, 128))
    nq, qs, kvs, os_ = _attn_specs(S, tq)

    def body(q_ref, kv_ref, do_ref, dq_ref, dkv_ref):
        @pl.when(pl.program_id(1) == 0)
        def _():
            dkv_ref[...] = jnp.zeros_like(dkv_ref)

        for h in range(N_HEAD):
            sl = slice(h * HEAD, (h + 1) * HEAD)
            vsl = slice(512 + h * HEAD, 512 + (h + 1) * HEAD)
            q, k, v, do = q_ref[:, sl], kv_ref[0, :, sl], kv_ref[0, :, vsl], do_ref[:, sl]
            p = _attn_probs(q, k)
            dkv_ref[0, :, vsl] += _raw_dot(p, do, "tn")
            dp = _raw_dot(do, v, "nt")
            ds = p * (dp - jnp.sum(dp * p, axis=-1, keepdims=True)) * (HEAD ** -0.5)
            dq_ref[:, sl] = _raw_dot(ds, k, "nn").astype(bf16)
            dkv_ref[0, :, sl] += _raw_dot(ds, q, "tn")

    return _call(body, "attn_bwd", (Bl, nq), [qs, kvs, os_], (os_, kvs),
                 (jax.ShapeDtypeStruct((Bl * S, 512), bf16), jax.ShapeDtypeStruct((Bl, MEM_LEN, 1024), f32)),
                 sem=("arbitrary", "arbitrary"))(proj, kv, dc)


def _merge_specs(tm, tn):
    br = pl.BlockSpec((tm, 512), lambda i, j: (i, 0))
    w = pl.BlockSpec((512, tn), lambda i, j: (0, j))
    gl = [pl.BlockSpec((tm, tn), functools.partial(lambda i, j, n: (i, (C_GL + n * D_MODEL) // tn + j), n=n)) for n in range(3)]
    return [br, br, br, w, w, w, *gl]


def _merge_fwd(branches, wb, proj):
    T = proj.shape[0]
    tm, tn = _pick(T, (1024, 512, 256, 128)), 512

    def body(a_ref, b_ref, c_ref, w0, w1, w2, g0, g1, g2, o_ref):
        acc = jnp.zeros((tm, tn), f32)
        for x_ref, w_ref, g_ref in ((a_ref, w0, g0), (b_ref, w1, g1), (c_ref, w2, g2)):
            acc = acc + jax.nn.sigmoid(g_ref[...].astype(f32)) * _raw_dot(x_ref[...], w_ref[...], "nn")
        o_ref[...] = acc.astype(bf16)

    return _call(body, "merge_fwd", (T // tm, D_MODEL // tn), _merge_specs(tm, tn), pl.BlockSpec((tm, tn), lambda i, j: (i, j)),
                 jax.ShapeDtypeStruct((T, D_MODEL), bf16), sem=("parallel", "parallel"))(*branches, *wb, proj, proj, proj)


def _merge_bwd(branches, wb, proj, dmerged):
    T = proj.shape[0]
    tm, tn = _pick(T, (1024, 512, 256, 128)), 512

    def body(a_ref, b_ref, c_ref, w0, w1, w2, g0, g1, g2, dm_ref, dgl_ref, d0, d1, d2):
        dm = dm_ref[...]
        for n, (x_ref, w_ref, g_ref, d_ref) in enumerate(((a_ref, w0, g0, d0), (b_ref, w1, g1, d1), (c_ref, w2, g2, d2))):
            up = _raw_dot(x_ref[...], w_ref[...], "nn")
            logits = g_ref[...].astype(f32)
            dgl_ref[n] = _egrad(jax.nn.sigmoid, logits, dm * up).astype(bf16)
            d_ref[...] = (dm * jax.nn.sigmoid(logits)).astype(bf16)

    blk = pl.BlockSpec((tm, tn), lambda i, j: (i, j))
    sh = jax.ShapeDtypeStruct((T, D_MODEL), bf16)
    outs = _call(body, "merge_bwd", (T // tm, D_MODEL // tn), [*_merge_specs(tm, tn), blk],
                 (pl.BlockSpec((3, tm, tn), lambda i, j: (0, i, j)), blk, blk, blk),
                 (jax.ShapeDtypeStruct((3, T, D_MODEL), bf16), sh, sh, sh),
                 sem=("parallel", "parallel"))(*branches, *wb, proj, proj, proj, dmerged)
    return outs[0], outs[1:]


CONV_TC = 256


def _shift_down(a, k):
    row = lax.broadcasted_iota(jnp.int32, a.shape, 0)
    return jnp.where(row >= k, pltpu.roll(a, k, 0), 0.0)


def _shift_up(a, k):
    n = a.shape[0]
    row = lax.broadcasted_iota(jnp.int32, a.shape, 0)
    return jnp.where(row < n - k, pltpu.roll(a, n - k, 0), 0.0)


def _conv_pre(a, cw, cb):
    return cb + cw[0:1] * _shift_down(a, 2) + cw[1:2] * _shift_down(a, 1) + cw[2:3] * a


def _conv_fwd(ab, cw, cb, Bl, S):
    nc = D_FF // CONV_TC

    def body(a_ref, b_ref, cw_ref, cb_ref, o_ref):
        ac = _conv_pre(a_ref[0].astype(f32), cw_ref[...], cb_ref[...])
        o_ref[0] = (jax.nn.silu(ac) * b_ref[0].astype(f32)).astype(bf16)

    return _call(body, "conv_fwd", (Bl, nc),
                 [pl.BlockSpec((1, S, CONV_TC), lambda b, c: (b, 0, c)), pl.BlockSpec((1, S, CONV_TC), lambda b, c: (b, 0, nc + c)),
                  pl.BlockSpec((3, CONV_TC), lambda b, c: (0, c)), pl.BlockSpec((1, CONV_TC), lambda b, c: (0, c))],
                 pl.BlockSpec((1, S, CONV_TC), lambda b, c: (b, 0, c)), jax.ShapeDtypeStruct((Bl, S, D_FF), bf16),
                 sem=("parallel", "parallel"))(ab, ab, cw, cb)


def _conv_bwd(ab, cw, cb, dact, Bl, S):
    nc = D_FF // CONV_TC

    def body(a_ref, b_ref, cw_ref, cb_ref, d_ref, da_ref, db_ref, dcw_ref, dcb_ref):
        @pl.when(pl.program_id(1) == 0)
        def _():
            dcw_ref[...] = jnp.zeros_like(dcw_ref)
            dcb_ref[...] = jnp.zeros_like(dcb_ref)

        a, cw = a_ref[0].astype(f32), cw_ref[...]
        ac = _conv_pre(a, cw, cb_ref[...])
        dact_ = d_ref[0].astype(f32)
        db_ref[0] = (dact_ * jax.nn.silu(ac)).astype(bf16)
        dac = _egrad(jax.nn.silu, ac, dact_ * b_ref[0].astype(f32))
        da_ref[0] = (cw[2:3] * dac + cw[1:2] * _shift_up(dac, 1) + cw[0:1] * _shift_up(dac, 2)).astype(bf16)
        dcw_ref[0:1, :] += jnp.sum(dac * _shift_down(a, 2), axis=0, keepdims=True)
        dcw_ref[1:2, :] += jnp.sum(dac * _shift_down(a, 1), axis=0, keepdims=True)
        dcw_ref[2:3, :] += jnp.sum(dac * a, axis=0, keepdims=True)
        dcb_ref[...] += jnp.sum(dac, axis=0, keepdims=True)

    seq = pl.BlockSpec((1, S, CONV_TC), lambda c, b: (b, 0, c))
    return _call(body, "conv_bwd", (nc, Bl),
                 [seq, pl.BlockSpec((1, S, CONV_TC), lambda c, b: (b, 0, nc + c)), pl.BlockSpec((3, CONV_TC), lambda c, b: (0, c)),
                  pl.BlockSpec((1, CONV_TC), lambda c, b: (0, c)), seq],
                 (seq, seq, pl.BlockSpec((3, CONV_TC), lambda c, b: (0, c)), pl.BlockSpec((1, CONV_TC), lambda c, b: (0, c))),
                 (jax.ShapeDtypeStruct((Bl, S, D_FF), bf16), jax.ShapeDtypeStruct((Bl, S, D_FF), bf16),
                  jax.ShapeDtypeStruct((3, D_FF), f32), jax.ShapeDtypeStruct((1, D_FF), f32)),
                 sem=("arbitrary", "arbitrary"))(ab, ab, cw, cb, dact)


def _local_step(x, mem, target, p, w_in, tok, late_b, late_c, send):
    Bl, S, Dd = x.shape
    T = Bl * S
    x2d, t2d, mem2d = x.reshape(T, Dd), target.reshape(T, Dd), mem.reshape(Bl * MEM_LEN, Dd)
    b_st = jnp.pad(p["b_spatial"].T, ((0, 0), (0, 128 - N_HEAD)))
    lbl = p["lb_logits"]

    h = _rms_fwd(x2d, p["norm1_g"] + tok[0, 0], "norm1_fwd")
    proj = _mm(h, w_in, "nn", bf16, "proj_fwd", 1024, 1664)
    a_out = _gmlp_fwd(proj, p["ln_v_g"], p["ln_v_b"], p["w_spatial"], b_st)
    proj3 = proj.reshape(Bl, S, IN_WIDTH)
    b_out, states = _hgrn_fwd(proj3, lbl, p["hgrn_norm_g"], Bl, S)
    b_out = b_out.reshape(T, 512)
    memn = _rms_fwd(mem2d, p["mem_norm_g"], "memnorm_fwd")
    w = late_b(b_out)
    wb = w["w_branch"]
    kv = _mm(memn, w["w_mem_kv"], "nn", f32, "kv_fwd", 512, 1024).reshape(Bl, MEM_LEN, 2 * 512)
    c_out = _attn_fwd(proj, kv, Bl, S)
    branches = (a_out, b_out, c_out)
    merged = _merge_fwd(branches, wb, proj)
    x1 = _mm(merged, w["w_out"], "nn", f32, "out_fwd", 1024, 1024, residual=x2d)
    h2 = _rms_fwd(x1, p["norm2_g"], "norm2_fwd")
    w.update(late_c(h2))
    ab = _mm(h2, w["w_up"], "nn", bf16, "up_fwd", 1024, 1408)
    act = _conv_fwd(ab.reshape(Bl, S, 2 * D_FF), w["conv_w"], p["conv_b"], Bl, S).reshape(T, D_FF)
    x2 = _mm(act, w["w_down"], "nn", f32, "down_fwd", 512, 1024, residual=x1)
    loss_part, dx2, g_final = _final_loss(x2, p["final_g"], t2d)

    g_w_down = _mm(act, dx2, "tn", bf16, "down_dw", 1408, 1024, 1024)
    dact = _mm(dx2, w["w_down"], "nt", bf16, "down_dx", 1024, 1408)
    da, db, g_conv_w, g_conv_b = _conv_bwd(ab.reshape(Bl, S, 2 * D_FF), w["conv_w"], p["conv_b"], dact.reshape(Bl, S, D_FF), Bl, S)
    dab = jnp.concatenate([da.reshape(T, D_FF), db.reshape(T, D_FF)], axis=-1)
    g_w_up = _mm(h2, dab, "tn", bf16, "up_dw", 512, 1408, 1024)
    tok1 = send("c", dict(w_up=g_w_up, conv_w=g_conv_w, w_down=g_w_down))
    dh2 = _mm(dab, w["w_up"], "nt", f32, "up_dx", 1024, 1024, 1408)
    dx1, g_norm2 = _rms_bwd(x1, p["norm2_g"] + tok1[0, 0], dh2, "norm2_bwd", residual=dx2)

    g_w_out = _mm(merged, dx1, "tn", bf16, "out_dw", 1024, 1024, 1024)
    dmerged = _mm(dx1, w["w_out"], "nt", f32, "out_dx", 1024, 1024)
    dgl, dup = _merge_bwd(branches, wb, proj, dmerged)
    g_w_branch = [_mm(branches[n], dup[n], "tn", bf16, f"branch_dw{n}", 512, 1024, 1024) for n in range(3)]
    dbr = [_mm(dup[n], wb[n], "nt", bf16, f"branch_dx{n}", 1024, 512) for n in range(3)]
    dxq, dkv = _attn_bwd(proj, kv, dbr[2], Bl, S)
    dkv = dkv.reshape(Bl * MEM_LEN, 2 * 512)
    g_w_kv = _mm(memn, dkv, "tn", bf16, "kv_dw", 1024, 1024, 512)
    tok2 = send("b", dict(w_mem_kv=g_w_kv, w_branch=g_w_branch, w_out=g_w_out))
    dmemn = _mm(dkv, w["w_mem_kv"], "nt", f32, "kv_dx", 512, 1024)
    _, g_mem_norm = _rms_bwd(mem2d, p["mem_norm_g"] + tok2[0, 0], dmemn, "memnorm_bwd")
    dzuv, g_ln_g, g_ln_b, g_w_sp, g_b_sp = _gmlp_bwd(proj, p["ln_v_g"] + tok2[0, 0], p["ln_v_b"], p["w_spatial"], b_st, dbr[0])
    *dqfig, g_lbl, g_ng = _hgrn_bwd(proj3, lbl, p["hgrn_norm_g"], states, dbr[1].reshape(Bl, S, 512), Bl, S)
    dq, df, di, dg = [d.reshape(T, 512) for d in dqfig]
    dproj = jnp.concatenate([dzuv, dq, df, di, dg, dxq, dgl[0], dgl[1], dgl[2]], axis=-1)
    g_w_in = _mm(h, dproj, "tn", bf16, "proj_dw", 512, 1664, 1024)
    tok3 = send("a", dict(w_in=g_w_in))
    dh = _mm(dproj, w_in, "nt", f32, "proj_dx", 1024, 1024, 1664)
    dx, g_norm1 = _rms_bwd(x2d, p["norm1_g"] + tok3[0, 0], dh, "norm1_bwd", residual=dx1)

    gs = dict(w_spatial=g_w_sp, norm1_g=g_norm1, mem_norm_g=g_mem_norm, norm2_g=g_norm2, final_g=g_final, lb_logits=g_lbl,
              ln_v_g=g_ln_g, ln_v_b=g_ln_b, b_spatial=g_b_sp, hgrn_norm_g=g_ng, conv_b=g_conv_b)
    return loss_part, dx.reshape(Bl, S, Dd), gs


def _coords():
    return lax.axis_index("x"), lax.axis_index("y"), lax.axis_index("c")


def _slot(dev):
    return 4 * dev[0] + 2 * dev[1] + dev[2]


def _comm_call(body, name, arrays, out_shapes, n_sem):
    n = len(arrays)
    hbm = pl.BlockSpec(memory_space=pl.ANY)
    return pl.pallas_call(
        body, name=name, out_shape=out_shapes, in_specs=[hbm] * n, out_specs=[hbm] * n,
        scratch_shapes=[pltpu.SemaphoreType.DMA((n_sem, n)), pltpu.SemaphoreType.DMA((n_sem, n)), pltpu.SemaphoreType.DMA((n,))])(*arrays)


def _all_gather(blocks, name):
    n = len(blocks)

    def body(*refs):
        x_refs, o_refs, (send_sems, recv_sems, local_sems) = refs[:n], refs[n:2 * n], refs[2 * n:]
        x, y, c = _coords()
        me, sibling = (x, y, c), (x, y, 1 - c)
        chips = [(1 - x, y), (x, 1 - y), (1 - x, 1 - y)]

        def copy(a, k, block_dev, to, from_input=False):
            dst = o_refs[a].at[_slot(block_dev)]
            return pltpu.make_async_remote_copy(src_ref=x_refs[a] if from_input else dst, dst_ref=dst, send_sem=send_sems.at[k, a],
                                                recv_sem=recv_sems.at[k, a], device_id=to, device_id_type=MESH)

        mine = [pltpu.make_async_copy(x_refs[a], o_refs[a].at[_slot(me)], local_sems.at[a]) for a in range(n)]
        first = [copy(a, 0, me, sibling, True) for a in range(n)]
        first += [copy(a, 1 + j, me, (*chip, c), True) for j, chip in enumerate(chips) for a in range(n)]
        for cp in mine + first:
            cp.start()
        passed = []
        for j, chip in enumerate(chips):
            for a in range(n):
                copy(a, 1 + j, (*chip, c), me).wait_recv()
                fwd = copy(a, 4 + j, (*chip, c), sibling)
                fwd.start()
                passed.append(fwd)
        for a in range(n):
            copy(a, 0, sibling, me).wait_recv()
        for j, chip in enumerate(chips):
            for a in range(n):
                copy(a, 4 + j, (*chip, 1 - c), me).wait_recv()
        for cp in first + passed:
            cp.wait_send()
        for cp in mine:
            cp.wait()

    return _comm_call(body, name, blocks, [jax.ShapeDtypeStruct((N_DEV,) + b.shape, b.dtype) for b in blocks], 7)


def _all_to_all(parts, name):
    n = len(parts)
    rel = [(0, 0, 1), (0, 1, 0), (0, 1, 1), (1, 0, 0), (1, 0, 1), (1, 1, 0), (1, 1, 1)]

    def body(*refs):
        x_refs, o_refs, (send_sems, recv_sems, local_sems) = refs[:n], refs[n:2 * n], refs[2 * n:]
        x, y, c = _coords()
        me = (x, y, c)
        peers = [(x ^ dx, y ^ dy, c ^ dc) for dx, dy, dc in rel]

        def copy(a, k, peer):
            return pltpu.make_async_remote_copy(src_ref=x_refs[a].at[_slot(peer)], dst_ref=o_refs[a].at[_slot(me)], send_sem=send_sems.at[k, a],
                                                recv_sem=recv_sems.at[k, a], device_id=peer, device_id_type=MESH)

        def arrival(a, k, peer):
            return pltpu.make_async_remote_copy(src_ref=x_refs[a].at[_slot(me)], dst_ref=o_refs[a].at[_slot(peer)], send_sem=send_sems.at[k, a],
                                                recv_sem=recv_sems.at[k, a], device_id=peer, device_id_type=MESH)

        mine = [pltpu.make_async_copy(x_refs[a].at[_slot(me)], o_refs[a].at[_slot(me)], local_sems.at[a]) for a in range(n)]
        sends = [copy(a, k, peer) for k, peer in enumerate(peers) for a in range(n)]
        for cp in mine + sends:
            cp.start()
        for k, peer in enumerate(peers):
            for a in range(n):
                arrival(a, k, peer).wait_recv()
        for cp in sends:
            cp.wait_send()
        for cp in mine:
            cp.wait()

    return _comm_call(body, name, parts, [jax.ShapeDtypeStruct(p.shape, p.dtype) for p in parts], 7)


_HBM = pl.BlockSpec(memory_space=pltpu.HBM)
_SEM = pl.BlockSpec(memory_space=pltpu.SEMAPHORE)
_REL = [(0, 0, 1), (0, 1, 0), (0, 1, 1), (1, 0, 0), (1, 0, 1), (1, 1, 0), (1, 1, 1)]


_LINK_ORDER = (3, 1, 5, 4, 2, 6, 0)
SEND_PIECES = 4


def _pieces(shape, dtype):
    rows = shape[0]
    unit = 1 if len(shape) > 2 else (16 if dtype == bf16 else 8)
    for n in (SEND_PIECES, 2):
        if rows % (n * unit) == 0:
            return [pl.ds(i * (rows // n), rows // n) for i in range(n)]
    return [pl.ds(0, rows)]


def _split_copies(gather, src, land, send, recv, pieces):
    x, y, c = _coords()
    me = (x, y, c)
    copies = []
    for a in range(len(src)):
        block = src[a].shape if gather else src[a].shape[1:]
        for rows in (_pieces(block, src[a].dtype) if pieces else [None]):
            for k in _LINK_ORDER:
                dx, dy, dc = _REL[k]
                peer = (x ^ dx, y ^ dy, c ^ dc)
                mine, there = (src[a] if gather else src[a].at[_slot(peer)]), land[a].at[_slot(me)]
                if rows is not None:
                    mine, there = mine.at[rows], there.at[rows]
                copies.append(pltpu.make_async_remote_copy(src_ref=mine, dst_ref=there, send_sem=send[a].at[k], recv_sem=recv[a].at[k],
                                                           device_id=peer, device_id_type=MESH))
    return me, copies


def _arrivals(gather, src, land, send, recv):
    x, y, c = _coords()
    out = []
    for a in range(len(src)):
        for k, (dx, dy, dc) in enumerate(_REL):
            peer = (x ^ dx, y ^ dy, c ^ dc)
            out.append(pltpu.make_async_remote_copy(src_ref=src[a] if gather else src[a].at[_slot(peer)], dst_ref=land[a].at[_slot(peer)],
                                                    send_sem=send[a].at[k], recv_sem=recv[a].at[k], device_id=peer, device_id_type=MESH))
    return out


def _exchange_start(arrays, gather, name, after=None):
    n = len(arrays)
    e = 0 if after is None else 1
    lands = [lax.empty(((N_DEV,) + a.shape) if gather else a.shape, a.dtype) for a in arrays]

    def body(*refs):
        src, land = refs[:n], refs[n:2 * n]
        refs = refs[2 * n + e:]
        send, recv, token, local_sems = refs[:n], refs[n:2 * n], refs[4 * n], refs[4 * n + 1]
        me, out = _split_copies(gather, src, land, send, recv, True)
        local = [pltpu.make_async_copy(src[a] if gather else src[a].at[_slot(me)], land[a].at[_slot(me)], local_sems.at[a])
                 for a in range(n)]
        for cp in local:
            cp.start()
        for cp in local:
            cp.wait()
        for cp in out:
            cp.start()
        token[...] = jnp.zeros_like(token)

    sems = [pltpu.SemaphoreType.DMA((7,)) for _ in range(2 * n)]
    outs = pl.pallas_call(
        body, name=name,
        out_shape=(*sems, *[pltpu.HBM(a.shape, a.dtype) for a in arrays], *[pltpu.HBM(l.shape, l.dtype) for l in lands],
                   jax.ShapeDtypeStruct((8, 128), f32)),
        in_specs=[_HBM] * (2 * n) + [pl.BlockSpec(memory_space=pl.ANY)] * e,
        out_specs=(*[_SEM] * (2 * n), *[_HBM] * (2 * n), pl.BlockSpec(memory_space=pltpu.VMEM)),
        input_output_aliases={i: 2 * n + i for i in range(2 * n)},
        scratch_shapes=[pltpu.SemaphoreType.DMA((n,))],
        compiler_params=pltpu.CompilerParams(has_side_effects=pltpu.SideEffectType.DATAFLOW_SIDE_EFFECTING))(
        *[pltpu.with_memory_space_constraint(a, pltpu.HBM) for a in arrays],
        *[pltpu.with_memory_space_constraint(l, pltpu.HBM) for l in lands], *([after] if e else []))
    return (gather, n, outs[:4 * n]), outs[4 * n]


def _exchange_wait(handle, which, after, name):
    gather, n_all, vals = handle
    send_v, recv_v, src_v, land_v = [[vals[g * n_all + i] for i in which] for g in range(4)]
    n = len(which)

    def body(*refs):
        src, land, send, recv = refs[:n], refs[n:2 * n], refs[2 * n:3 * n], refs[3 * n:4 * n]
        for cp in _split_copies(gather, src, land, send, recv, False)[1]:
            cp.wait_send()
        for cp in _arrivals(gather, src, land, send, recv):
            cp.wait_recv()

    outs = pl.pallas_call(
        body, name=name,
        out_shape=(*[pltpu.HBM(a.shape, a.dtype) for a in src_v], *[pltpu.HBM(l.shape, l.dtype) for l in land_v]),
        in_specs=[*[_HBM] * (2 * n), *[_SEM] * (2 * n), pl.BlockSpec(memory_space=pl.ANY)], out_specs=[_HBM] * (2 * n),
        input_output_aliases={i: i for i in range(2 * n)},
        compiler_params=pltpu.CompilerParams(has_side_effects=pltpu.SideEffectType.DATAFLOW_SIDE_EFFECTING))(
        *src_v, *land_v, *send_v, *recv_v, after)
    return outs[n:]


def _adam_math(w, g, m, v):
    m_ = ADAM_B1 * m + (1.0 - ADAM_B1) * g
    v_ = ADAM_B2 * v + (1.0 - ADAM_B2) * jnp.square(g)
    m_hat = m_ / (1.0 - ADAM_B1 ** ADAM_STEP)
    v_hat = v_ / (1.0 - ADAM_B2 ** ADAM_STEP)
    return -ADAM_LR * (m_hat / (jnp.sqrt(v_hat) + ADAM_EPS) + ADAM_WD * w), m_, v_


def _reduce_adamw(parts, w, m, v, name):
    _, R, L = parts.shape
    tr = _pick(R, (256, 128, 64, 32, 16, 8))

    def body(p_ref, w_ref, m_ref, v_ref, g_ref, d_ref, nm_ref, nv_ref):
        g = p_ref[0].astype(f32)
        for i in range(1, N_DEV):
            g = g + p_ref[i].astype(f32)
        g_ref[...] = g
        d_ref[...], nm_ref[...], nv_ref[...] = _adam_math(w_ref[...], g, m_ref[...], v_ref[...])

    blk = pl.BlockSpec((tr, L), lambda i: (i, 0))
    sh = jax.ShapeDtypeStruct((R, L), f32)
    return _call(body, name, (R // tr,), [pl.BlockSpec((N_DEV, tr, L), lambda i: (0, i, 0)), blk, blk, blk], (blk,) * 4, (sh,) * 4,
                 sem=("parallel",))(parts, w, m, v)


SMALL = (("w_spatial", (512, 128), 0), ("norm1_g", (1, 1024), 512), ("mem_norm_g", (1, 1024), 520), ("norm2_g", (1, 1024), 528),
         ("final_g", (1, 1024), 536), ("lb_logits", (2, 512), 544), ("ln_v_g", (1, 512), 552), ("ln_v_b", (1, 512), 556),
         ("b_spatial", (4, 128), 560), ("hgrn_norm_g", (1, 128), 564), ("conv_b", (1, 2816), 565))
SMALL_USED, SMALL_ROWS = 587, 640


def _segments(shape, base):
    r, n = shape
    per = n // 128
    return [(base + i * per + j, i, slice(j * 128, (j + 1) * 128)) for i in range(r) for j in range(per)]


def _pack_small(gs):
    names = [n for n, _, _ in SMALL]

    def body(*refs):
        src, o_ref = dict(zip(names, refs[:-1])), refs[-1]
        o_ref[SMALL_USED:SMALL_ROWS, :] = jnp.zeros((SMALL_ROWS - SMALL_USED, 128), f32)
        for name, shape, base in SMALL:
            ref = src[name]
            if name == "w_spatial":
                o_ref[base:base + 512, :] = ref[...].reshape(512, 128)
            elif name == "b_spatial":
                o_ref[base:base + 4, :] = ref[0:4, :]
            elif name == "hgrn_norm_g":
                per_head = [ref[b, h] for b in range(ref.shape[0]) for h in range(N_HEAD)]
                o_ref[base:base + 1, :] = functools.reduce(lambda u, v_: u + v_, per_head)
            else:
                for row, i, sl in _segments(shape, base):
                    o_ref[row:row + 1, :] = ref[i:i + 1, sl]

    return pl.pallas_call(body, name="pack_small", out_shape=jax.ShapeDtypeStruct((SMALL_ROWS, 128), f32))(*[gs[n] for n in names])


def _small_update(gathered, w, m, v):
    names = [n for n, _, _ in SMALL]
    k = len(names)

    def body(*refs):
        p_ref = refs[0]
        ins = [dict(zip(names, refs[1 + i * k:1 + (i + 1) * k])) for i in range(3)]
        outs = [dict(zip(names, refs[1 + (3 + i) * k:1 + (4 + i) * k])) for i in range(4)]
        gsum = refs[-1]
        g = p_ref[0]
        for i in range(1, N_DEV):
            g = g + p_ref[i]
        gsum[...] = g
        for name, shape, base in SMALL:
            if name == "w_spatial":
                where = [(slice(base, base + 512), (slice(None), slice(None)))]
            else:
                where = [(slice(row, row + 1), (slice(i, i + 1), sl)) for row, i, sl in _segments(shape, base)]
            for rows, at in where:
                g_ = gsum[rows, :]
                d_, m_, v_ = _adam_math(ins[0][name][at], g_, ins[1][name][at], ins[2][name][at])
                for o, val in zip(outs, (g_, d_, m_, v_)):
                    o[name][at] = val

    args = [gathered] + [d[n] for d in (w, m, v) for n in names]
    out_shapes = [jax.ShapeDtypeStruct(shape, f32) for _ in range(4) for _, shape, _ in SMALL]
    outs = pl.pallas_call(body, name="small_update", out_shape=out_shapes, scratch_shapes=[pltpu.VMEM((SMALL_ROWS, 128), f32)])(*args)
    return [dict(zip(names, outs[i * k:(i + 1) * k])) for i in range(4)]


def _cols_full(g):
    return jnp.moveaxis(g, 0, -2).reshape(g.shape[1:-1] + (N_DEV * g.shape[-1],))


def _cols_parts(full):
    n = full.shape[-1] // N_DEV
    return jnp.moveaxis(full.reshape(full.shape[:-1] + (N_DEV, n)), -2, 0)


def kernel(x, mem, norm1_g, w_in, ln_v_g, ln_v_b, w_spatial, b_spatial, lb_logits, hgrn_norm_g, mem_norm_g, w_mem_kv, w_branch, w_out, norm2_g, w_up, conv_w, conv_b, w_down, final_g, loss_target, m_norm1_g, m_w_in, m_ln_v_g, m_ln_v_b, m_w_spatial, m_b_spatial, m_lb_logits, m_hgrn_norm_g, m_mem_norm_g, m_w_mem_kv, m_w_branch, m_w_out, m_norm2_g, m_w_up, m_conv_w, m_conv_b, m_w_down, m_final_g, v_norm1_g, v_w_in, v_ln_v_g, v_ln_v_b, v_w_spatial, v_b_spatial, v_lb_logits, v_hgrn_norm_g, v_mem_norm_g, v_w_mem_kv, v_w_branch, v_w_out, v_norm2_g, v_w_up, v_conv_w, v_conv_b, v_w_down, v_final_g):
    given = dict(locals())
    order = ("norm1_g", "w_in", "ln_v_g", "ln_v_b", "w_spatial", "b_spatial", "lb_logits", "hgrn_norm_g", "mem_norm_g",
             "w_mem_kv", "w_branch", "w_out", "norm2_g", "w_up", "conv_w", "conv_b", "w_down", "final_g")
    groups = dict(a=("w_in",), b=("w_mem_kv", "w_branch", "w_out"), c=("w_up", "conv_w", "w_down"))

    wire = {n: given[n][0].astype(f32 if n == "conv_w" else bf16) for ns in groups.values() for n in ns}
    g_in = _all_gather([wire["w_in"]], "gather_w_in")[0]
    late = groups["b"] + groups["c"]
    gather, tok = _exchange_start([wire[n] for n in late], True, "gather_rest_start", after=g_in)

    def late_b(after):
        kv_, br_, out_ = _exchange_wait(gather, (0, 1, 2), after, "gather_b_wait")
        br_ = _cols_full(br_)
        return dict(w_mem_kv=kv_.reshape(D_MODEL, 2 * 512), w_branch=[br_[n] for n in range(3)], w_out=out_.reshape(D_MODEL, D_MODEL))

    def late_c(after):
        up_, cw_, down_ = _exchange_wait(gather, (3, 4, 5), after, "gather_c_wait")
        return dict(w_up=_cols_full(up_), conv_w=_cols_full(cw_), w_down=down_.reshape(D_FF, D_MODEL))

    to_parts = dict(w_in=_cols_parts, w_up=_cols_parts, conv_w=_cols_parts,
                    w_branch=lambda g_: _cols_parts(jnp.stack(g_)).reshape(N_DEV, -1, 128),
                    w_mem_kv=lambda g_: g_.reshape(N_DEV, -1, 2 * 512), w_out=lambda g_: g_.reshape(N_DEV, -1, D_MODEL),
                    w_down=lambda g_: g_.reshape(N_DEV, -1, D_MODEL))
    scatters = {}

    def send(tag, grads_):
        scatters[tag], tok_ = _exchange_start([to_parts[n](grads_[n]) for n in groups[tag]], False, f"scatter_{tag}_start")
        return tok_

    small_2d = lambda prefix: {n: given[prefix + n].reshape(shape) for n, shape, _ in SMALL}
    p = small_2d("")
    p["w_spatial"] = w_spatial[0]
    loss_part, grad_x, gs = _local_step(x, mem, loss_target, p, _cols_full(g_in), tok, late_b, late_c, send)
    loss = lax.psum(loss_part[0, 0], ("x", "y", "c"))

    small_gather, _ = _exchange_start([_pack_small(gs)], True, "gather_small_start")

    grads, delta, new_m, new_v = {}, {}, {}, {}
    after = grad_x
    for tag in ("c", "b", "a"):
        recv = _exchange_wait(scatters[tag], tuple(range(len(groups[tag]))), after, f"scatter_{tag}_wait")
        for n, parts in zip(groups[tag], recv):
            two_d = (-1, given[n].shape[-1])
            res = _reduce_adamw(parts, *[given[pre + n].reshape(two_d) for pre in ("", "m_", "v_")], "adamw_" + n)
            grads[n], delta[n], new_m[n], new_v[n] = [r.reshape(given[n].shape) for r in res]
            after = res[0]

    gathered = _exchange_wait(small_gather, (0,), after, "gather_small_wait")[0]
    for dst, res in zip((grads, delta, new_m, new_v), _small_update(gathered, small_2d(""), small_2d("m_"), small_2d("v_"))):
        for n, _, _ in SMALL:
            dst[n] = res[n].reshape(given[n].shape)

    return (loss, grad_x, *[grads[n] for n in order], *[delta[n] for n in order], *[new_m[n] for n in order],
            *[new_v[n] for n in order])
```

```python
import functools

import jax
import jax.numpy as jnp
from jax import lax
from jax.experimental import pallas as pl
from jax.experimental.pallas import tpu as pltpu
from jax.experimental.pallas import tpu_sc as plsc

f32 = jnp.float32
bf16 = jnp.bfloat16

N_DEV = 8
D_MODEL = 1024
EPS = 1e-6
GM_CHUNK = 128
HG_CHUNK = 64
HEAD = 128
N_HEAD = 4
MEM_LEN = 256
D_FF = 2816
IN_WIDTH = 6656
C_ZU, C_HQ, C_HF, C_HI, C_HG, C_XQ, C_GL = 0, 1024, 1536, 2048, 2560, 3072, 3584
ADAM_LR, ADAM_B1, ADAM_B2, ADAM_EPS, ADAM_WD, ADAM_STEP = 0.001, 0.9, 0.999, 1e-08, 0.01, 10
VMEM_LIMIT = 56 * 1024 * 1024
MESH = pl.DeviceIdType.MESH


def _pick(n, cands):
    for c in cands:
        if n % c == 0:
            return c
    return n


def _call(body, name, grid, in_specs, out_specs, out_shape, scratch=(), sem=None, **cp):
    params = dict(vmem_limit_bytes=VMEM_LIMIT, **cp)
    if sem is not None:
        params["dimension_semantics"] = sem
    return pl.pallas_call(
        body, name=name, grid=grid, in_specs=in_specs, out_specs=out_specs, out_shape=out_shape,
        scratch_shapes=list(scratch), compiler_params=pltpu.CompilerParams(**params))


_DN = {"nn": (((1,), (0,)), ((), ())), "nt": (((1,), (1,)), ((), ())), "tn": (((0,), (0,)), ((), ()))}


def _raw_dot(a, b, mode):
    return lax.dot_general(a.astype(bf16), b.astype(bf16), _DN[mode], preferred_element_type=f32)


@jax.custom_vjp
def _dot_nn(a, b):
    return _raw_dot(a, b, "nn")


_dot_nn.defvjp(lambda a, b: (_raw_dot(a, b, "nn"), (a, b)),
               lambda r, g: (_raw_dot(g, r[1], "nt"), _raw_dot(r[0], g, "tn")))


@jax.custom_vjp
def _dot_nt(a, b):
    return _raw_dot(a, b, "nt")


_dot_nt.defvjp(lambda a, b: (_raw_dot(a, b, "nt"), (a, b)),
               lambda r, g: (_raw_dot(g, r[1], "nn"), _raw_dot(g, r[0], "tn")))


@jax.custom_vjp
def _dot_tn(a, b):
    return _raw_dot(a, b, "tn")


_dot_tn.defvjp(lambda a, b: (_raw_dot(a, b, "tn"), (a, b)),
               lambda r, g: (_raw_dot(r[1], g, "nt"), _raw_dot(r[0], g, "nn")))


def _tri(n, lower):
    r = lax.broadcasted_iota(jnp.int32, (n, n), 0)
    c = lax.broadcasted_iota(jnp.int32, (n, n), 1)
    return ((c <= r) if lower else (c >= r)).astype(f32)


def _sel_dot(sel, x, mode, x_first=False):
    hi = x.astype(bf16)
    rest = x - hi.astype(f32)
    mid = rest.astype(bf16)
    lo = (rest - mid.astype(f32)).astype(bf16)
    sel = sel.astype(bf16)
    dot = lambda piece: lax.dot_general(*((piece, sel) if x_first else (sel, piece)), _DN[mode], preferred_element_type=f32)
    return dot(hi) + dot(mid) + dot(lo)


def _egrad(fn, x, ct):
    return jax.vjp(fn, x)[1](ct)[0]


def _mm(a, b, mode, out_dtype, name, tm, tn, tk=None, residual=None):
    if mode == "nn":
        (M, K), (_, N) = a.shape, b.shape
    elif mode == "nt":
        (M, K), (N, _) = a.shape, b.shape
    else:
        (K, M), (_, N) = a.shape, b.shape
    tm, tn = min(tm, M), min(tn, N)
    tk = K if tk is None else min(tk, K)
    assert M % tm == 0 and N % tn == 0 and K % tk == 0, (name, M, N, K, tm, tn, tk)
    nk = K // tk

    def body(*refs):
        acc_ref = refs[-1] if nk > 1 else None
        refs = refs[:-1] if nk > 1 else refs
        if residual is None:
            a_ref, b_ref, o_ref = refs
        else:
            a_ref, b_ref, r_ref, o_ref = refs

        def finish(r):
            if residual is not None:
                r = r + r_ref[...]
            o_ref[...] = r.astype(out_dtype)

        part = _raw_dot(a_ref[...], b_ref[...], mode)
        if nk == 1:
            finish(part)
            return
        k = pl.program_id(2)

        @pl.when(k == 0)
        def _():
            acc_ref[...] = part

        @pl.when((k > 0) & (k < nk - 1))
        def _():
            acc_ref[...] += part

        @pl.when(k == nk - 1)
        def _():
            finish(acc_ref[...] + part)

    a_spec = {"nn": pl.BlockSpec((tm, tk), lambda i, j, k: (i, k)),
              "nt": pl.BlockSpec((tm, tk), lambda i, j, k: (i, k)),
              "tn": pl.BlockSpec((tk, tm), lambda i, j, k: (k, i))}[mode]
    b_spec = {"nn": pl.BlockSpec((tk, tn), lambda i, j, k: (k, j)),
              "nt": pl.BlockSpec((tn, tk), lambda i, j, k: (j, k)),
              "tn": pl.BlockSpec((tk, tn), lambda i, j, k: (k, j))}[mode]
    o_spec = pl.BlockSpec((tm, tn), lambda i, j, k: (i, j))
    in_specs = [a_spec, b_spec] + ([o_spec] if residual is not None else [])
    args = (a, b) + ((residual,) if residual is not None else ())
    return _call(body, name, (M // tm, N // tn, nk), in_specs, o_spec, jax.ShapeDtypeStruct((M, N), out_dtype),
                 scratch=[pltpu.VMEM((tm, tn), f32)] if nk > 1 else [], sem=("parallel", "parallel", "arbitrary"))(*args)


def _rms_fwd(x, g, name):
    R, Dd = x.shape
    tr = _pick(R, (512, 256, 128))

    def body(x_ref, g_ref, o_ref):
        xf = x_ref[...]
        y = xf * lax.rsqrt(jnp.mean(xf * xf, axis=-1, keepdims=True) + EPS)
        o_ref[...] = (y * g_ref[...]).astype(bf16)

    return _call(body, name, (R // tr,), [pl.BlockSpec((tr, Dd), lambda i: (i, 0)), pl.BlockSpec((1, Dd), lambda i: (0, 0))],
                 pl.BlockSpec((tr, Dd), lambda i: (i, 0)), jax.ShapeDtypeStruct((R, Dd), bf16), sem=("parallel",))(x, g)


def _rms_bwd(x, g, dh, name, residual=None):
    R, Dd = x.shape
    tr = _pick(R, (512, 256, 128))

    def body(*refs):
        if residual is None:
            x_ref, g_ref, dh_ref, dx_ref, dg_ref = refs
        else:
            x_ref, g_ref, dh_ref, r_ref, dx_ref, dg_ref = refs
        xf = x_ref[...]
        rs = lax.rsqrt(jnp.mean(xf * xf, axis=-1, keepdims=True) + EPS)
        y = xf * rs
        dh_ = dh_ref[...].astype(f32)
        dy = dh_ * g_ref[...]
        dx = rs * (dy - y * jnp.mean(dy * y, axis=-1, keepdims=True))
        if residual is not None:
            dx = dx + r_ref[...]
        dx_ref[...] = dx

        @pl.when(pl.program_id(0) == 0)
        def _():
            dg_ref[...] = jnp.zeros_like(dg_ref)

        dg_ref[...] += jnp.sum(dh_ * y, axis=0, keepdims=True)

    row = pl.BlockSpec((tr, Dd), lambda i: (i, 0))
    vec = pl.BlockSpec((1, Dd), lambda i: (0, 0))
    in_specs = [row, vec, row] + ([row] if residual is not None else [])
    args = (x, g, dh) + ((residual,) if residual is not None else ())
    return _call(body, name, (R // tr,), in_specs, (row, vec),
                 (jax.ShapeDtypeStruct((R, Dd), f32), jax.ShapeDtypeStruct((1, Dd), f32)), sem=("arbitrary",))(*args)


def _final_loss(x2, g, target):
    R, Dd = x2.shape
    tr = _pick(R, (512, 256, 128))

    def body(x_ref, g_ref, t_ref, loss_ref, dx_ref, dg_ref):
        xf = x_ref[...]
        rs = lax.rsqrt(jnp.mean(xf * xf, axis=-1, keepdims=True) + EPS)
        y = xf * rs
        err = y * g_ref[...] - t_ref[...]
        dh_ = err * (1.0 / Dd)
        dy = dh_ * g_ref[...]
        dx_ref[...] = rs * (dy - y * jnp.mean(dy * y, axis=-1, keepdims=True))

        @pl.when(pl.program_id(0) == 0)
        def _():
            dg_ref[...] = jnp.zeros_like(dg_ref)
            loss_ref[...] = jnp.zeros_like(loss_ref)

        dg_ref[...] += jnp.sum(dh_ * y, axis=0, keepdims=True)
        part = jnp.sum(jnp.mean(err * err, axis=-1, keepdims=True), axis=0, keepdims=True)
        loss_ref[...] += 0.5 * part

    row = pl.BlockSpec((tr, Dd), lambda i: (i, 0))
    vec = pl.BlockSpec((1, Dd), lambda i: (0, 0))
    return _call(body, "final_loss", (R // tr,), [row, vec, row], (pl.BlockSpec((1, 128), lambda i: (0, 0)), row, vec),
                 (jax.ShapeDtypeStruct((1, 128), f32), jax.ShapeDtypeStruct((R, Dd), f32), jax.ShapeDtypeStruct((1, Dd), f32)),
                 sem=("arbitrary",))(x2, g, target)


def _gmlp_parts(zuv, ln_g, ln_b):
    zu, zv = zuv[:, :512], zuv[:, 512:]
    u = jax.nn.gelu(zu)
    v = jax.nn.gelu(zv)
    mu = jnp.mean(v, axis=-1, keepdims=True)
    rs = lax.rsqrt(jnp.mean(jnp.square(v - mu), axis=-1, keepdims=True) + EPS)
    xh = (v - mu) * rs
    return zu, zv, u, xh, rs, xh * ln_g + ln_b


def _gmlp_fwd(proj, ln_g, ln_b, w_s, b_st):
    T = proj.shape[0]

    def body(p_ref, g_ref, b_ref, w_ref, bs_ref, o_ref):
        _, _, u, _, _, vn = _gmlp_parts(p_ref[...].astype(f32), g_ref[...], b_ref[...])
        causal = _tri(GM_CHUNK, True) > 0
        for gi in range(N_HEAD):
            sl = slice(gi * HEAD, (gi + 1) * HEAD)
            w = jnp.where(causal, w_ref[gi], 0.0)
            mixed = _raw_dot(w, vn[:, sl], "nn") + bs_ref[:, gi:gi + 1]
            o_ref[:, sl] = (u[:, sl] * mixed).astype(bf16)

    vec = pl.BlockSpec((1, 512), lambda i: (0, 0))
    return _call(body, "gmlp_fwd", (T // GM_CHUNK,),
                 [pl.BlockSpec((GM_CHUNK, 1024), lambda i: (i, 0)), vec, vec,
                  pl.BlockSpec((N_HEAD, GM_CHUNK, GM_CHUNK), lambda i: (0, 0, 0)), pl.BlockSpec((GM_CHUNK, 128), lambda i: (0, 0))],
                 pl.BlockSpec((GM_CHUNK, 512), lambda i: (i, 0)), jax.ShapeDtypeStruct((T, 512), bf16), sem=("parallel",))(
        proj, ln_g, ln_b, w_s, b_st)


def _gmlp_bwd(proj, ln_g, ln_b, w_s, b_st, da):
    T = proj.shape[0]

    def body(p_ref, g_ref, b_ref, w_ref, bs_ref, da_ref, dp_ref, dg_ref, db_ref, dw_ref, dbs_ref):
        zu, zv, u, xh, rs, vn = _gmlp_parts(p_ref[...].astype(f32), g_ref[...], b_ref[...])
        causal = _tri(GM_CHUNK, True) > 0
        sub = lax.broadcasted_iota(jnp.int32, (8, GM_CHUNK), 0)
        ones = jnp.ones((8, HEAD), f32)
        dout = da_ref[...].astype(f32)

        @pl.when(pl.program_id(0) == 0)
        def _():
            for r in (dg_ref, db_ref, dw_ref, dbs_ref):
                r[...] = jnp.zeros_like(r)

        du, dvn, dbs = [], [], jnp.zeros((8, GM_CHUNK), f32)
        for gi in range(N_HEAD):
            sl = slice(gi * HEAD, (gi + 1) * HEAD)
            w = jnp.where(causal, w_ref[gi], 0.0)
            mixed = _raw_dot(w, vn[:, sl], "nn") + bs_ref[:, gi:gi + 1]
            du.append(dout[:, sl] * mixed)
            dm = dout[:, sl] * u[:, sl]
            row_sums = _sel_dot(ones, dm, "nt")
            dbs = dbs + jnp.where(sub == gi, row_sums, 0.0)
            dw_ref[gi] += jnp.where(causal, _raw_dot(dm, vn[:, sl], "nt"), 0.0)
            dvn.append(_raw_dot(w, dm, "tn"))
        dbs_ref[...] += dbs
        du = jnp.concatenate(du, axis=-1)
        dvn = jnp.concatenate(dvn, axis=-1)
        dg_ref[...] += jnp.sum(dvn * xh, axis=0, keepdims=True)
        db_ref[...] += jnp.sum(dvn, axis=0, keepdims=True)
        dxh = dvn * g_ref[...]
        dv = rs * (dxh - jnp.mean(dxh, axis=-1, keepdims=True) - xh * jnp.mean(dxh * xh, axis=-1, keepdims=True))
        dp_ref[:, :512] = _egrad(jax.nn.gelu, zu, du).astype(bf16)
        dp_ref[:, 512:] = _egrad(jax.nn.gelu, zv, dv).astype(bf16)

    vec = pl.BlockSpec((1, 512), lambda i: (0, 0))
    wsp = pl.BlockSpec((N_HEAD, GM_CHUNK, GM_CHUNK), lambda i: (0, 0, 0))
    return _call(body, "gmlp_bwd", (T // GM_CHUNK,),
                 [pl.BlockSpec((GM_CHUNK, 1024), lambda i: (i, 0)), vec, vec, wsp, pl.BlockSpec((GM_CHUNK, 128), lambda i: (0, 0)),
                  pl.BlockSpec((GM_CHUNK, 512), lambda i: (i, 0))],
                 (pl.BlockSpec((GM_CHUNK, 1024), lambda i: (i, 0)), vec, vec, wsp, pl.BlockSpec((8, GM_CHUNK), lambda i: (0, 0))),
                 (jax.ShapeDtypeStruct((T, 1024), bf16), jax.ShapeDtypeStruct((1, 512), f32), jax.ShapeDtypeStruct((1, 512), f32),
                  jax.ShapeDtypeStruct((N_HEAD, GM_CHUNK, GM_CHUNK), f32), jax.ShapeDtypeStruct((8, GM_CHUNK), f32)),
                 sem=("arbitrary",))(proj, ln_g, ln_b, w_s, b_st, da)


HG_SUB = 8
HG_NSUB = HG_CHUNK // HG_SUB


def _two_level_matrix():
    r = lax.broadcasted_iota(jnp.int32, (2 * HG_CHUNK, HG_CHUNK), 0)
    c = lax.broadcasted_iota(jnp.int32, (2 * HG_CHUNK, HG_CHUNK), 1)
    t = jnp.where(r < HG_CHUNK, r, r - HG_CHUNK)
    local = (r < HG_CHUNK) & (t // HG_SUB == c // HG_SUB) & (c <= t)
    before = (r >= HG_CHUNK) & (c < (t // HG_SUB) * HG_SUB)
    return (local | before).astype(f32)


def _two_level_sums(x):
    two = _sel_dot(_two_level_matrix(), x, "nn")
    return two[:HG_CHUNK], two[HG_CHUNK:]


@jax.custom_vjp
def _two_level_cumsum(x):
    return _two_level_sums(x)


_two_level_cumsum.defvjp(
    lambda x: (_two_level_sums(x), None),
    lambda _, g: (_sel_dot(_two_level_matrix(), jnp.concatenate(g, axis=0), "tn"),))


def _tile_matrix():
    s = lax.broadcasted_iota(jnp.int32, (HG_SUB, HG_CHUNK), 0)
    j = lax.broadcasted_iota(jnp.int32, (HG_SUB, HG_CHUNK), 1)
    return (j % HG_SUB == s).astype(f32)


@jax.custom_vjp
def _tile_lanes(x):
    return _sel_dot(_tile_matrix(), x, "nn", x_first=True)


_tile_lanes.defvjp(
    lambda x: (_sel_dot(_tile_matrix(), x, "nn", x_first=True), None),
    lambda _, g: (_sel_dot(_tile_matrix(), g, "nt", x_first=True),))


def _block_rows(x):
    k = x.shape[-1]
    return jnp.broadcast_to(x.reshape(HG_NSUB, 1, HG_SUB, k), (HG_NSUB, HG_SUB, HG_SUB, k)).reshape(HG_CHUNK, HG_SUB, k)


def _hgrn_chunk(st0, q_raw, f_raw, i_raw, g_raw, l0, l1, ng):
    C, SUB = HG_CHUNK, HG_SUB
    lb = jax.nn.sigmoid(l0 - l1)
    fg = lb + (1.0 - lb) * jax.nn.sigmoid(f_raw)
    kk = 1.0 - fg
    qf = jax.nn.silu(q_raw)
    al, base = _two_level_cumsum(jnp.log(fg))
    a = al + base
    row = lax.broadcasted_iota(jnp.int32, (C, HEAD), 0)
    a_last = jnp.sum(jnp.where(row == C - 1, a, 0.0), axis=0, keepdims=True)
    inter = _dot_nt(qf * jnp.exp(a), st0)
    qt = qf * jnp.exp(al)
    rb = lax.broadcasted_iota(jnp.int32, (C, C), 0) // SUB
    cb = lax.broadcasted_iota(jnp.int32, (C, C), 1) // SUB
    scores = jnp.zeros((C, C), f32)
    for i in range(1, HG_NSUB):
        base_i = jnp.sum(jnp.where(row == i * SUB, base, 0.0), axis=0, keepdims=True)
        kt = kk * jnp.exp(jnp.minimum(base_i - a, 0.0))
        scores = scores + jnp.where((rb == i) & (cb < i), _dot_nt(qt, kt), 0.0)
    t_i = lax.broadcasted_iota(jnp.int32, (C, SUB, HEAD), 0) % SUB
    s_i = lax.broadcasted_iota(jnp.int32, (C, SUB, HEAD), 1)
    decay = jnp.exp(jnp.where(s_i <= t_i, al[:, None, :] - _block_rows(al), -jnp.inf))
    diag = jnp.sum(qf[:, None, :] * decay * _block_rows(kk), axis=-1)
    scores = scores + jnp.where(rb == cb, _tile_lanes(diag), 0.0)
    o = inter + _dot_nn(scores, i_raw)
    st1 = jnp.exp(a_last) * st0 + _dot_tn(i_raw, kk * jnp.exp(a_last - a))
    on = o * lax.rsqrt(jnp.mean(o * o, axis=-1, keepdims=True) + EPS) * ng
    return st1, on * jax.nn.silu(g_raw)


def _hgrn_specs(S, Bl, rev):
    N = S // HG_CHUNK
    chunk = (lambda n: N - 1 - n) if rev else (lambda n: n)
    col = lambda c0: pl.BlockSpec((Bl, HG_CHUNK, 512), lambda n: (0, chunk(n), c0 // 512))
    st = pl.BlockSpec((Bl, N_HEAD, 1, HEAD, HEAD), lambda n: (0, 0, chunk(n), 0, 0))
    full = lambda *s: pl.BlockSpec(s, functools.partial(lambda n, nd: (0,) * nd, nd=len(s)))
    return N, col, st, full


def _hgrn_fwd(proj, lb_logits, ng, Bl, S):
    N, col, st, full = _hgrn_specs(S, Bl, False)

    def body(q_ref, f_ref, i_ref, g_ref, l_ref, ng_ref, o_ref, st_ref, state):
        @pl.when(pl.program_id(0) == 0)
        def _():
            state[...] = jnp.zeros_like(state)

        for b in range(Bl):
            for h in range(N_HEAD):
                sl = slice(h * HEAD, (h + 1) * HEAD)
                st0 = state[b, h]
                st_ref[b, h, 0] = st0
                st1, out = _hgrn_chunk(st0, *[r[b, :, sl].astype(f32) for r in (q_ref, f_ref, i_ref, g_ref)],
                                       l_ref[0:1, sl], l_ref[1:2, sl], ng_ref[...])
                state[b, h] = st1
                o_ref[b, :, sl] = out.astype(bf16)

    return _call(body, "hgrn_fwd", (N,), [col(C_HQ), col(C_HF), col(C_HI), col(C_HG), full(2, 512), full(1, HEAD)],
                 (col(0), st),
                 (jax.ShapeDtypeStruct((Bl, S, 512), bf16), jax.ShapeDtypeStruct((Bl, N_HEAD, N, HEAD, HEAD), f32)),
                 scratch=[pltpu.VMEM((Bl, N_HEAD, HEAD, HEAD), f32)], sem=("arbitrary",))(
        proj, proj, proj, proj, lb_logits, ng)


def _hgrn_bwd(proj, lb_logits, ng, states, db, Bl, S):
    N, col, st, full = _hgrn_specs(S, Bl, True)

    def body(q_ref, f_ref, i_ref, g_ref, l_ref, ng_ref, st_ref, db_ref,
             dq_ref, df_ref, di_ref, dg_ref, dl_ref, dng_ref, dstate):
        @pl.when(pl.program_id(0) == 0)
        def _():
            dstate[...] = jnp.zeros_like(dstate)
            dl_ref[...] = jnp.zeros_like(dl_ref)
            dng_ref[...] = jnp.zeros_like(dng_ref)

        for b in range(Bl):
            for h in range(N_HEAD):
                sl = slice(h * HEAD, (h + 1) * HEAD)
                _, vjp = jax.vjp(_hgrn_chunk, st_ref[b, h, 0], *[r[b, :, sl].astype(f32) for r in (q_ref, f_ref, i_ref, g_ref)],
                                 l_ref[0:1, sl], l_ref[1:2, sl], ng_ref[...])
                dst0, dq, df, di, dg, dl0, dl1, dng = vjp((dstate[b, h], db_ref[b, :, sl].astype(f32)))
                dstate[b, h] = dst0
                dq_ref[b, :, sl] = dq.astype(bf16)
                df_ref[b, :, sl] = df.astype(bf16)
                di_ref[b, :, sl] = di.astype(bf16)
                dg_ref[b, :, sl] = dg.astype(bf16)
                dl_ref[0:1, sl] += dl0
                dl_ref[1:2, sl] += dl1
                dng_ref[b, h] += dng

    return _call(body, "hgrn_bwd", (N,),
                 [col(C_HQ), col(C_HF), col(C_HI), col(C_HG), full(2, 512), full(1, HEAD), st, col(0)],
                 (*[col(0)] * 4, full(2, 512), full(Bl, N_HEAD, 1, HEAD)),
                 (*[jax.ShapeDtypeStruct((Bl, S, 512), bf16)] * 4, jax.ShapeDtypeStruct((2, 512), f32),
                  jax.ShapeDtypeStruct((Bl, N_HEAD, 1, HEAD), f32)),
                 scratch=[pltpu.VMEM((Bl, N_HEAD, HEAD, HEAD), f32)], sem=("arbitrary",))(
        proj, proj, proj, proj, lb_logits, ng, states, db)


def _attn_probs(q, k):
    s = _raw_dot(q, k, "nt") * (HEAD ** -0.5)
    e = jnp.exp(s - jnp.max(s, axis=-1, keepdims=True))
    return e / jnp.sum(e, axis=-1, keepdims=True)


def _attn_specs(S, tq):
    nq = S // tq
    q = pl.BlockSpec((tq, 512), lambda b, i: (b * nq + i, C_XQ // 512))
    kv = pl.BlockSpec((1, MEM_LEN, 1024), lambda b, i: (b, 0, 0))
    o = pl.BlockSpec((tq, 512), lambda b, i: (b * nq + i, 0))
    return nq, q, kv, o


def _attn_fwd(proj, kv, Bl, S):
    tq = _pick(S, (512, 256, 128))
    nq, qs, kvs, os_ = _attn_specs(S, tq)

    def body(q_ref, kv_ref, o_ref):
        for h in range(N_HEAD):
            sl = slice(h * HEAD, (h + 1) * HEAD)
            p = _attn_probs(q_ref[:, sl], kv_ref[0, :, sl])
            o_ref[:, sl] = _raw_dot(p, kv_ref[0, :, 512 + h * HEAD:512 + (h + 1) * HEAD], "nn").astype(bf16)

    return _call(body, "attn_fwd", (Bl, nq), [qs, kvs], os_, jax.ShapeDtypeStruct((Bl * S, 512), bf16),
                 sem=("parallel", "parallel"))(proj, kv)


def _attn_bwd(proj, kv, dc, Bl, S):
    tq = _pick(S, (512, 256, 128))
    nq, qs, kvs, os_ = _attn_specs(S, tq)

    def body(q_ref, kv_ref, do_ref, dq_ref, dkv_ref):
        @pl.when(pl.program_id(1) == 0)
        def _():
            dkv_ref[...] = jnp.zeros_like(dkv_ref)

        for h in range(N_HEAD):
            sl = slice(h * HEAD, (h + 1) * HEAD)
            vsl = slice(512 + h * HEAD, 512 + (h + 1) * HEAD)
            q, k, v, do = q_ref[:, sl], kv_ref[0, :, sl], kv_ref[0, :, vsl], do_ref[:, sl]
            p = _attn_probs(q, k)
            dkv_ref[0, :, vsl] += _raw_dot(p, do, "tn")
            dp = _raw_dot(do, v, "nt")
            ds = p * (dp - jnp.sum(dp * p, axis=-1, keepdims=True)) * (HEAD ** -0.5)
            dq_ref[:, sl] = _raw_dot(ds, k, "nn").astype(bf16)
            dkv_ref[0, :, sl] += _raw_dot(ds, q, "tn")

    return _call(body, "attn_bwd", (Bl, nq), [qs, kvs, os_], (os_, kvs),
                 (jax.ShapeDtypeStruct((Bl * S, 512), bf16), jax.ShapeDtypeStruct((Bl, MEM_LEN, 1024), f32)),
                 sem=("arbitrary", "arbitrary"))(proj, kv, dc)


def _merge_specs(tm, tn):
    br = pl.BlockSpec((tm, 512), lambda i, j: (i, 0))
    w = pl.BlockSpec((512, tn), lambda i, j: (0, j))
    gl = [pl.BlockSpec((tm, tn), functools.partial(lambda i, j, n: (i, (C_GL + n * D_MODEL) // tn + j), n=n)) for n in range(3)]
    return [br, br, br, w, w, w, *gl]


def _merge_fwd(branches, wb, proj):
    T = proj.shape[0]
    tm, tn = _pick(T, (1024, 512, 256, 128)), 512

    def body(a_ref, b_ref, c_ref, w0, w1, w2, g0, g1, g2, o_ref):
        acc = jnp.zeros((tm, tn), f32)
        for x_ref, w_ref, g_ref in ((a_ref, w0, g0), (b_ref, w1, g1), (c_ref, w2, g2)):
            acc = acc + jax.nn.sigmoid(g_ref[...].astype(f32)) * _raw_dot(x_ref[...], w_ref[...], "nn")
        o_ref[...] = acc.astype(bf16)

    return _call(body, "merge_fwd", (T // tm, D_MODEL // tn), _merge_specs(tm, tn), pl.BlockSpec((tm, tn), lambda i, j: (i, j)),
                 jax.ShapeDtypeStruct((T, D_MODEL), bf16), sem=("parallel", "parallel"))(*branches, *wb, proj, proj, proj)


def _merge_bwd(branches, wb, proj, dmerged):
    T = proj.shape[0]
    tm, tn = _pick(T, (1024, 512, 256, 128)), 512

    def body(a_ref, b_ref, c_ref, w0, w1, w2, g0, g1, g2, dm_ref, dgl_ref, d0, d1, d2):
        dm = dm_ref[...]
        for n, (x_ref, w_ref, g_ref, d_ref) in enumerate(((a_ref, w0, g0, d0), (b_ref, w1, g1, d1), (c_ref, w2, g2, d2))):
            up = _raw_dot(x_ref[...], w_ref[...], "nn")
            logits = g_ref[...].astype(f32)
            dgl_ref[n] = _egrad(jax.nn.sigmoid, logits, dm * up).astype(bf16)
            d_ref[...] = (dm * jax.nn.sigmoid(logits)).astype(bf16)

    blk = pl.BlockSpec((tm, tn), lambda i, j: (i, j))
    sh = jax.ShapeDtypeStruct((T, D_MODEL), bf16)
    outs = _call(body, "merge_bwd", (T // tm, D_MODEL // tn), [*_merge_specs(tm, tn), blk],
                 (pl.BlockSpec((3, tm, tn), lambda i, j: (0, i, j)), blk, blk, blk),
                 (jax.ShapeDtypeStruct((3, T, D_MODEL), bf16), sh, sh, sh),
                 sem=("parallel", "parallel"))(*branches, *wb, proj, proj, proj, dmerged)
    return outs[0], outs[1:]


CONV_TC = 256


def _shift_down(a, k):
    row = lax.broadcasted_iota(jnp.int32, a.shape, 0)
    return jnp.where(row >= k, pltpu.roll(a, k, 0), 0.0)


def _shift_up(a, k):
    n = a.shape[0]
    row = lax.broadcasted_iota(jnp.int32, a.shape, 0)
    return jnp.where(row < n - k, pltpu.roll(a, n - k, 0), 0.0)


def _conv_pre(a, cw, cb):
    return cb + cw[0:1] * _shift_down(a, 2) + cw[1:2] * _shift_down(a, 1) + cw[2:3] * a


def _conv_fwd(ab, cw, cb, Bl, S):
    nc = D_FF // CONV_TC

    def body(a_ref, b_ref, cw_ref, cb_ref, o_ref):
        ac = _conv_pre(a_ref[0].astype(f32), cw_ref[...], cb_ref[...])
        o_ref[0] = (jax.nn.silu(ac) * b_ref[0].astype(f32)).astype(bf16)

    return _call(body, "conv_fwd", (Bl, nc),
                 [pl.BlockSpec((1, S, CONV_TC), lambda b, c: (b, 0, c)), pl.BlockSpec((1, S, CONV_TC), lambda b, c: (b, 0, nc + c)),
                  pl.BlockSpec((3, CONV_TC), lambda b, c: (0, c)), pl.BlockSpec((1, CONV_TC), lambda b, c: (0, c))],
                 pl.BlockSpec((1, S, CONV_TC), lambda b, c: (b, 0, c)), jax.ShapeDtypeStruct((Bl, S, D_FF), bf16),
                 sem=("parallel", "parallel"))(ab, ab, cw, cb)


def _conv_bwd(ab, cw, cb, dact, Bl, S):
    nc = D_FF // CONV_TC

    def body(a_ref, b_ref, cw_ref, cb_ref, d_ref, da_ref, db_ref, dcw_ref, dcb_ref):
        @pl.when(pl.program_id(1) == 0)
        def _():
            dcw_ref[...] = jnp.zeros_like(dcw_ref)
            dcb_ref[...] = jnp.zeros_like(dcb_ref)

        a, cw = a_ref[0].astype(f32), cw_ref[...]
        ac = _conv_pre(a, cw, cb_ref[...])
        dact_ = d_ref[0].astype(f32)
        db_ref[0] = (dact_ * jax.nn.silu(ac)).astype(bf16)
        dac = _egrad(jax.nn.silu, ac, dact_ * b_ref[0].astype(f32))
        da_ref[0] = (cw[2:3] * dac + cw[1:2] * _shift_up(dac, 1) + cw[0:1] * _shift_up(dac, 2)).astype(bf16)
        dcw_ref[0:1, :] += jnp.sum(dac * _shift_down(a, 2), axis=0, keepdims=True)
        dcw_ref[1:2, :] += jnp.sum(dac * _shift_down(a, 1), axis=0, keepdims=True)
        dcw_ref[2:3, :] += jnp.sum(dac * a, axis=0, keepdims=True)
        dcb_ref[...] += jnp.sum(dac, axis=0, keepdims=True)

    seq = pl.BlockSpec((1, S, CONV_TC), lambda c, b: (b, 0, c))
    return _call(body, "conv_bwd", (nc, Bl),
                 [seq, pl.BlockSpec((1, S, CONV_TC), lambda c, b: (b, 0, nc + c)), pl.BlockSpec((3, CONV_TC), lambda c, b: (0, c)),
                  pl.BlockSpec((1, CONV_TC), lambda c, b: (0, c)), seq],
                 (seq, seq, pl.BlockSpec((3, CONV_TC), lambda c, b: (0, c)), pl.BlockSpec((1, CONV_TC), lambda c, b: (0, c))),
                 (jax.ShapeDtypeStruct((Bl, S, D_FF), bf16), jax.ShapeDtypeStruct((Bl, S, D_FF), bf16),
                  jax.ShapeDtypeStruct((3, D_FF), f32), jax.ShapeDtypeStruct((1, D_FF), f32)),
                 sem=("arbitrary", "arbitrary"))(ab, ab, cw, cb, dact)


def _local_step(x, mem, target, p, w_in, late_b, late_c, send):
    Bl, S, Dd = x.shape
    T = Bl * S
    x2d, t2d, mem2d = x.reshape(T, Dd), target.reshape(T, Dd), mem.reshape(Bl * MEM_LEN, Dd)
    b_st = jnp.pad(p["b_spatial"].T, ((0, 0), (0, 128 - N_HEAD)))
    lbl = p["lb_logits"]

    h = _rms_fwd(x2d, p["norm1_g"], "norm1_fwd")
    proj = _mm(h, w_in, "nn", bf16, "proj_fwd", 1024, 1664)
    a_out = _gmlp_fwd(proj, p["ln_v_g"], p["ln_v_b"], p["w_spatial"], b_st)
    proj3 = proj.reshape(Bl, S, IN_WIDTH)
    b_out, states = _hgrn_fwd(proj3, lbl, p["hgrn_norm_g"], Bl, S)
    b_out = b_out.reshape(T, 512)
    memn = _rms_fwd(mem2d, p["mem_norm_g"], "memnorm_fwd")
    w = late_b(b_out)
    wb = w["w_branch"]
    kv = _mm(memn, w["w_mem_kv"], "nn", f32, "kv_fwd", 512, 1024).reshape(Bl, MEM_LEN, 2 * 512)
    c_out = _attn_fwd(proj, kv, Bl, S)
    branches = (a_out, b_out, c_out)
    merged = _merge_fwd(branches, wb, proj)
    x1 = _mm(merged, w["w_out"], "nn", f32, "out_fwd", 1024, 1024, residual=x2d)
    h2 = _rms_fwd(x1, p["norm2_g"], "norm2_fwd")
    w.update(late_c(h2))
    ab = _mm(h2, w["w_up"], "nn", bf16, "up_fwd", 1024, 1408)
    act = _conv_fwd(ab.reshape(Bl, S, 2 * D_FF), w["conv_w"], p["conv_b"], Bl, S).reshape(T, D_FF)
    x2 = _mm(act, w["w_down"], "nn", f32, "down_fwd", 512, 1024, residual=x1)
    loss_part, dx2, g_final = _final_loss(x2, p["final_g"], t2d)

    g_w_down = _mm(act, dx2, "tn", bf16, "down_dw", 1408, 1024, 1024)
    dact = _mm(dx2, w["w_down"], "nt", bf16, "down_dx", 1024, 1408)
    da, db, g_conv_w, g_conv_b = _conv_bwd(ab.reshape(Bl, S, 2 * D_FF), w["conv_w"], p["conv_b"], dact.reshape(Bl, S, D_FF), Bl, S)
    dab = jnp.concatenate([da.reshape(T, D_FF), db.reshape(T, D_FF)], axis=-1)
    g_w_up = _mm(h2, dab, "tn", bf16, "up_dw", 512, 1408, 1024)
    dab = send("c", dict(w_up=g_w_up, conv_w=g_conv_w, w_down=g_w_down), dab)
    dh2 = _mm(dab, w["w_up"], "nt", f32, "up_dx", 1024, 1024, 1408)
    dx1, g_norm2 = _rms_bwd(x1, p["norm2_g"], dh2, "norm2_bwd", residual=dx2)

    g_w_out = _mm(merged, dx1, "tn", bf16, "out_dw", 1024, 1024, 1024)
    dmerged = _mm(dx1, w["w_out"], "nt", f32, "out_dx", 1024, 1024)
    dgl, dup = _merge_bwd(branches, wb, proj, dmerged)
    g_w_branch = [_mm(branches[n], dup[n], "tn", bf16, f"branch_dw{n}", 512, 1024, 1024) for n in range(3)]
    dbr = [_mm(dup[n], wb[n], "nt", bf16, f"branch_dx{n}", 1024, 512) for n in range(3)]
    dxq, dkv = _attn_bwd(proj, kv, dbr[2], Bl, S)
    dkv = dkv.reshape(Bl * MEM_LEN, 2 * 512)
    g_w_kv = _mm(memn, dkv, "tn", bf16, "kv_dw", 1024, 1024, 512)
    dbr0 = send("b", dict(w_mem_kv=g_w_kv, w_branch=g_w_branch, w_out=g_w_out), dbr[0])
    dmemn = _mm(dkv, w["w_mem_kv"], "nt", f32, "kv_dx", 512, 1024)
    _, g_mem_norm = _rms_bwd(mem2d, p["mem_norm_g"], dmemn, "memnorm_bwd")
    dzuv, g_ln_g, g_ln_b, g_w_sp, g_b_sp = _gmlp_bwd(proj, p["ln_v_g"], p["ln_v_b"], p["w_spatial"], b_st, dbr0)
    *dqfig, g_lbl, g_ng = _hgrn_bwd(proj3, lbl, p["hgrn_norm_g"], states, dbr[1].reshape(Bl, S, 512), Bl, S)
    dq, df, di, dg = [d.reshape(T, 512) for d in dqfig]
    dproj = jnp.concatenate([dzuv, dq, df, di, dg, dxq, dgl[0], dgl[1], dgl[2]], axis=-1)
    g_w_in = _mm(h, dproj, "tn", bf16, "proj_dw", 512, 1664, 1024)
    dproj = send("a", dict(w_in=g_w_in), dproj)
    dh = _mm(dproj, w_in, "nt", f32, "proj_dx", 1024, 1024, 1664)
    dx, g_norm1 = _rms_bwd(x2d, p["norm1_g"], dh, "norm1_bwd", residual=dx1)

    gs = dict(w_spatial=g_w_sp, norm1_g=g_norm1, mem_norm_g=g_mem_norm, norm2_g=g_norm2, final_g=g_final, lb_logits=g_lbl,
              ln_v_g=g_ln_g, ln_v_b=g_ln_b, b_spatial=g_b_sp, hgrn_norm_g=g_ng, conv_b=g_conv_b)
    return loss_part, dx.reshape(Bl, S, Dd), gs


def _coords():
    return lax.axis_index("x"), lax.axis_index("y"), lax.axis_index("c")


def _slot(dev):
    return 4 * dev[0] + 2 * dev[1] + dev[2]


def _comm_call(body, name, arrays, out_shapes, n_sem):
    n = len(arrays)
    hbm = pl.BlockSpec(memory_space=pl.ANY)
    return pl.pallas_call(
        body, name=name, out_shape=out_shapes, in_specs=[hbm] * n, out_specs=[hbm] * n,
        scratch_shapes=[pltpu.SemaphoreType.DMA((n_sem, n)), pltpu.SemaphoreType.DMA((n_sem, n)), pltpu.SemaphoreType.DMA((n,))])(*arrays)


def _all_gather(blocks, name):
    n = len(blocks)

    def body(*refs):
        x_refs, o_refs, (send_sems, recv_sems, local_sems) = refs[:n], refs[n:2 * n], refs[2 * n:]
        x, y, c = _coords()
        me, sibling = (x, y, c), (x, y, 1 - c)
        chips = [(1 - x, y), (x, 1 - y), (1 - x, 1 - y)]

        def copy(a, k, block_dev, to, from_input=False):
            dst = o_refs[a].at[_slot(block_dev)]
            return pltpu.make_async_remote_copy(src_ref=x_refs[a] if from_input else dst, dst_ref=dst, send_sem=send_sems.at[k, a],
                                                recv_sem=recv_sems.at[k, a], device_id=to, device_id_type=MESH)

        mine = [pltpu.make_async_copy(x_refs[a], o_refs[a].at[_slot(me)], local_sems.at[a]) for a in range(n)]
        first = [copy(a, 0, me, sibling, True) for a in range(n)]
        first += [copy(a, 1 + j, me, (*chip, c), True) for j, chip in enumerate(chips) for a in range(n)]
        for cp in mine + first:
            cp.start()
        passed = []
        for j, chip in enumerate(chips):
            for a in range(n):
                copy(a, 1 + j, (*chip, c), me).wait_recv()
                fwd = copy(a, 4 + j, (*chip, c), sibling)
                fwd.start()
                passed.append(fwd)
        for a in range(n):
            copy(a, 0, sibling, me).wait_recv()
        for j, chip in enumerate(chips):
            for a in range(n):
                copy(a, 4 + j, (*chip, 1 - c), me).wait_recv()
        for cp in first + passed:
            cp.wait_send()
        for cp in mine:
            cp.wait()

    return _comm_call(body, name, blocks, [jax.ShapeDtypeStruct((N_DEV,) + b.shape, b.dtype) for b in blocks], 7)


def _all_to_all(parts, name):
    n = len(parts)
    rel = [(0, 0, 1), (0, 1, 0), (0, 1, 1), (1, 0, 0), (1, 0, 1), (1, 1, 0), (1, 1, 1)]

    def body(*refs):
        x_refs, o_refs, (send_sems, recv_sems, local_sems) = refs[:n], refs[n:2 * n], refs[2 * n:]
        x, y, c = _coords()
        me = (x, y, c)
        peers = [(x ^ dx, y ^ dy, c ^ dc) for dx, dy, dc in rel]

        def copy(a, k, peer):
            return pltpu.make_async_remote_copy(src_ref=x_refs[a].at[_slot(peer)], dst_ref=o_refs[a].at[_slot(me)], send_sem=send_sems.at[k, a],
                                                recv_sem=recv_sems.at[k, a], device_id=peer, device_id_type=MESH)

        def arrival(a, k, peer):
            return pltpu.make_async_remote_copy(src_ref=x_refs[a].at[_slot(me)], dst_ref=o_refs[a].at[_slot(peer)], send_sem=send_sems.at[k, a],
                                                recv_sem=recv_sems.at[k, a], device_id=peer, device_id_type=MESH)

        mine = [pltpu.make_async_copy(x_refs[a].at[_slot(me)], o_refs[a].at[_slot(me)], local_sems.at[a]) for a in range(n)]
        sends = [copy(a, k, peer) for k, peer in enumerate(peers) for a in range(n)]
        for cp in mine + sends:
            cp.start()
        for k, peer in enumerate(peers):
            for a in range(n):
                arrival(a, k, peer).wait_recv()
        for cp in sends:
            cp.wait_send()
        for cp in mine:
            cp.wait()

    return _comm_call(body, name, parts, [jax.ShapeDtypeStruct(p.shape, p.dtype) for p in parts], 7)


_HBM = pl.BlockSpec(memory_space=pltpu.HBM)
_SEM = pl.BlockSpec(memory_space=pltpu.SEMAPHORE)
_REL = [(0, 0, 1), (0, 1, 0), (0, 1, 1), (1, 0, 0), (1, 0, 1), (1, 1, 0), (1, 1, 1)]


_LINK_ORDER = (3, 1, 5, 4, 2, 6, 0)
SEND_PIECES = 4


def _pieces(shape, dtype):
    rows = shape[0]
    unit = 1 if len(shape) > 2 else (16 if dtype == bf16 else 8)
    for n in (SEND_PIECES, 2):
        if rows % (n * unit) == 0:
            return [pl.ds(i * (rows // n), rows // n) for i in range(n)]
    return [pl.ds(0, rows)]


def _split_copies(gather, src, land, send, recv, pieces):
    x, y, c = _coords()
    me = (x, y, c)
    copies = []
    for a in range(len(src)):
        block = src[a].shape if gather else src[a].shape[1:]
        for rows in (_pieces(block, src[a].dtype) if pieces else [None]):
            for k in _LINK_ORDER:
                dx, dy, dc = _REL[k]
                peer = (x ^ dx, y ^ dy, c ^ dc)
                mine, there = (src[a] if gather else src[a].at[_slot(peer)]), land[a].at[_slot(me)]
                if rows is not None:
                    mine, there = mine.at[rows], there.at[rows]
                copies.append(pltpu.make_async_remote_copy(src_ref=mine, dst_ref=there, send_sem=send[a].at[k], recv_sem=recv[a].at[k],
                                                           device_id=peer, device_id_type=MESH))
    return me, copies


def _arrivals(gather, src, land, send, recv):
    x, y, c = _coords()
    out = []
    for a in range(len(src)):
        for k, (dx, dy, dc) in enumerate(_REL):
            peer = (x ^ dx, y ^ dy, c ^ dc)
            out.append(pltpu.make_async_remote_copy(src_ref=src[a] if gather else src[a].at[_slot(peer)], dst_ref=land[a].at[_slot(peer)],
                                                    send_sem=send[a].at[k], recv_sem=recv[a].at[k], device_id=peer, device_id_type=MESH))
    return out


def _exchange_start(arrays, gather, name, after=None):
    n = len(arrays)
    e = 0 if after is None else 1
    lands = [lax.empty(((N_DEV,) + a.shape) if gather else a.shape, a.dtype) for a in arrays]

    def body(*refs):
        src, land = refs[:n], refs[n:2 * n]
        refs = refs[2 * n + e:]
        send, recv, token, local_sems = refs[:n], refs[n:2 * n], refs[4 * n], refs[4 * n + 1]
        me, out = _split_copies(gather, src, land, send, recv, True)
        local = [pltpu.make_async_copy(src[a] if gather else src[a].at[_slot(me)], land[a].at[_slot(me)], local_sems.at[a])
                 for a in range(n)]
        for cp in local:
            cp.start()
        for cp in local:
            cp.wait()
        for cp in out:
            cp.start()
        token[...] = jnp.zeros_like(token)

    sems = [pltpu.SemaphoreType.DMA((7,)) for _ in range(2 * n)]
    outs = pl.pallas_call(
        body, name=name,
        out_shape=(*sems, *[pltpu.HBM(a.shape, a.dtype) for a in arrays], *[pltpu.HBM(l.shape, l.dtype) for l in lands],
                   jax.ShapeDtypeStruct((8, 128), f32)),
        in_specs=[_HBM] * (2 * n) + [pl.BlockSpec(memory_space=pl.ANY)] * e,
        out_specs=(*[_SEM] * (2 * n), *[_HBM] * (2 * n), pl.BlockSpec(memory_space=pltpu.VMEM)),
        input_output_aliases={i: 2 * n + i for i in range(2 * n)},
        scratch_shapes=[pltpu.SemaphoreType.DMA((n,))],
        compiler_params=pltpu.CompilerParams(has_side_effects=pltpu.SideEffectType.DATAFLOW_SIDE_EFFECTING))(
        *[pltpu.with_memory_space_constraint(a, pltpu.HBM) for a in arrays],
        *[pltpu.with_memory_space_constraint(l, pltpu.HBM) for l in lands], *([after] if e else []))
    return (gather, n, outs[:4 * n]), outs[4 * n]


def _exchange_wait(handle, which, after, name):
    gather, n_all, vals = handle
    send_v, recv_v, src_v, land_v = [[vals[g * n_all + i] for i in which] for g in range(4)]
    n = len(which)

    def body(*refs):
        src, land, send, recv = refs[:n], refs[n:2 * n], refs[2 * n:3 * n], refs[3 * n:4 * n]
        for cp in _split_copies(gather, src, land, send, recv, False)[1]:
            cp.wait_send()
        for cp in _arrivals(gather, src, land, send, recv):
            cp.wait_recv()

    outs = pl.pallas_call(
        body, name=name,
        out_shape=(*[pltpu.HBM(a.shape, a.dtype) for a in src_v], *[pltpu.HBM(l.shape, l.dtype) for l in land_v]),
        in_specs=[*[_HBM] * (2 * n), *[_SEM] * (2 * n), pl.BlockSpec(memory_space=pl.ANY)], out_specs=[_HBM] * (2 * n),
        input_output_aliases={i: i for i in range(2 * n)},
        compiler_params=pltpu.CompilerParams(has_side_effects=pltpu.SideEffectType.DATAFLOW_SIDE_EFFECTING))(
        *src_v, *land_v, *send_v, *recv_v, after)
    return outs[n:]


def _seq_exchange(arrays, gather, name, collective_id):
    n = len(arrays)
    hbm = pltpu.MemorySpace.HBM
    srcs = [jax.new_ref(a, memory_space=hbm) for a in arrays]
    lands = [jax.empty_ref(jax.ShapeDtypeStruct(((N_DEV,) + a.shape) if gather else a.shape, a.dtype), memory_space=hbm) for a in arrays]

    @pl.kernel(mesh=plsc.ScalarSubcoreMesh(axis_name="sequencer", num_cores=1), name=name,
               scratch_types=(pltpu.SemaphoreType.DMA((7, n)), pltpu.SemaphoreType.DMA((7, n)), pltpu.SemaphoreType.DMA((n,))),
               compiler_params=pltpu.CompilerParams(collective_id=collective_id))
    def launch(send, recv, local):
        x, y, c = _coords()
        me = (x, y, c)
        peers = [(x ^ dx, y ^ dy, c ^ dc) for dx, dy, dc in _REL]
        barrier = pltpu.get_barrier_semaphore()
        for peer in peers:
            pl.semaphore_signal(barrier, inc=1, device_id=peer, device_id_type=MESH)
        pl.semaphore_wait(barrier, len(peers))

        def copy(a, k, peer, arrival):
            return pltpu.make_async_remote_copy(
                src_ref=srcs[a] if gather else srcs[a].at[_slot(peer)], dst_ref=lands[a].at[_slot(peer if arrival else me)],
                send_sem=send.at[k, a], recv_sem=recv.at[k, a], device_id=peer, device_id_type=MESH)

        mine = [pltpu.make_async_copy(srcs[a] if gather else srcs[a].at[_slot(me)], lands[a].at[_slot(me)], local.at[a])
                for a in range(n)]
        out = [copy(a, k, peer, False) for a in range(n) for k, peer in enumerate(peers)]
        for cp in mine + out:
            cp.start()
        for a in range(n):
            for k, peer in enumerate(peers):
                copy(a, k, peer, True).wait_recv()
        for cp in out:
            cp.wait_send()
        for cp in mine:
            cp.wait()

    launch()
    return [land[...] for land in lands]


def _adam_math(w, g, m, v):
    m_ = ADAM_B1 * m + (1.0 - ADAM_B1) * g
    v_ = ADAM_B2 * v + (1.0 - ADAM_B2) * jnp.square(g)
    m_hat = m_ / (1.0 - ADAM_B1 ** ADAM_STEP)
    v_hat = v_ / (1.0 - ADAM_B2 ** ADAM_STEP)
    return -ADAM_LR * (m_hat / (jnp.sqrt(v_hat) + ADAM_EPS) + ADAM_WD * w), m_, v_


def _reduce_adamw(parts, w, m, v, name):
    _, R, L = parts.shape
    tr = _pick(R, (256, 128, 64, 32, 16, 8))

    def body(p_ref, w_ref, m_ref, v_ref, g_ref, d_ref, nm_ref, nv_ref):
        g = p_ref[0].astype(f32)
        for i in range(1, N_DEV):
            g = g + p_ref[i].astype(f32)
        g_ref[...] = g
        d_ref[...], nm_ref[...], nv_ref[...] = _adam_math(w_ref[...], g, m_ref[...], v_ref[...])

    blk = pl.BlockSpec((tr, L), lambda i: (i, 0))
    sh = jax.ShapeDtypeStruct((R, L), f32)
    return _call(body, name, (R // tr,), [pl.BlockSpec((N_DEV, tr, L), lambda i: (0, i, 0)), blk, blk, blk], (blk,) * 4, (sh,) * 4,
                 sem=("parallel",))(parts, w, m, v)


SMALL = (("w_spatial", (512, 128), 0), ("norm1_g", (1, 1024), 512), ("mem_norm_g", (1, 1024), 520), ("norm2_g", (1, 1024), 528),
         ("final_g", (1, 1024), 536), ("lb_logits", (2, 512), 544), ("ln_v_g", (1, 512), 552), ("ln_v_b", (1, 512), 556),
         ("b_spatial", (4, 128), 560), ("hgrn_norm_g", (1, 128), 564), ("conv_b", (1, 2816), 565))
SMALL_USED, SMALL_ROWS = 587, 640


def _segments(shape, base):
    r, n = shape
    per = n // 128
    return [(base + i * per + j, i, slice(j * 128, (j + 1) * 128)) for i in range(r) for j in range(per)]


def _pack_small(gs):
    names = [n for n, _, _ in SMALL]

    def body(*refs):
        src, o_ref = dict(zip(names, refs[:-1])), refs[-1]
        o_ref[SMALL_USED:SMALL_ROWS, :] = jnp.zeros((SMALL_ROWS - SMALL_USED, 128), f32)
        for name, shape, base in SMALL:
            ref = src[name]
            if name == "w_spatial":
                o_ref[base:base + 512, :] = ref[...].reshape(512, 128)
            elif name == "b_spatial":
                o_ref[base:base + 4, :] = ref[0:4, :]
            elif name == "hgrn_norm_g":
                per_head = [ref[b, h] for b in range(ref.shape[0]) for h in range(N_HEAD)]
                o_ref[base:base + 1, :] = functools.reduce(lambda u, v_: u + v_, per_head)
            else:
                for row, i, sl in _segments(shape, base):
                    o_ref[row:row + 1, :] = ref[i:i + 1, sl]

    return pl.pallas_call(body, name="pack_small", out_shape=jax.ShapeDtypeStruct((SMALL_ROWS, 128), f32))(*[gs[n] for n in names])


def _small_update(gathered, w, m, v):
    names = [n for n, _, _ in SMALL]
    k = len(names)

    def body(*refs):
        p_ref = refs[0]
        ins = [dict(zip(names, refs[1 + i * k:1 + (i + 1) * k])) for i in range(3)]
        outs = [dict(zip(names, refs[1 + (3 + i) * k:1 + (4 + i) * k])) for i in range(4)]
        gsum = refs[-1]
        g = p_ref[0]
        for i in range(1, N_DEV):
            g = g + p_ref[i]
        gsum[...] = g
        for name, shape, base in SMALL:
            if name == "w_spatial":
                where = [(slice(base, base + 512), (slice(None), slice(None)))]
            else:
                where = [(slice(row, row + 1), (slice(i, i + 1), sl)) for row, i, sl in _segments(shape, base)]
            for rows, at in where:
                g_ = gsum[rows, :]
                d_, m_, v_ = _adam_math(ins[0][name][at], g_, ins[1][name][at], ins[2][name][at])
                for o, val in zip(outs, (g_, d_, m_, v_)):
                    o[name][at] = val

    args = [gathered] + [d[n] for d in (w, m, v) for n in names]
    out_shapes = [jax.ShapeDtypeStruct(shape, f32) for _ in range(4) for _, shape, _ in SMALL]
    outs = pl.pallas_call(body, name="small_update", out_shape=out_shapes, scratch_shapes=[pltpu.VMEM((SMALL_ROWS, 128), f32)])(*args)
    return [dict(zip(names, outs[i * k:(i + 1) * k])) for i in range(4)]


def _cols_full(g):
    return jnp.moveaxis(g, 0, -2).reshape(g.shape[1:-1] + (N_DEV * g.shape[-1],))


def _cols_parts(full):
    n = full.shape[-1] // N_DEV
    return jnp.moveaxis(full.reshape(full.shape[:-1] + (N_DEV, n)), -2, 0)


def kernel(x, mem, norm1_g, w_in, ln_v_g, ln_v_b, w_spatial, b_spatial, lb_logits, hgrn_norm_g, mem_norm_g, w_mem_kv, w_branch, w_out, norm2_g, w_up, conv_w, conv_b, w_down, final_g, loss_target, m_norm1_g, m_w_in, m_ln_v_g, m_ln_v_b, m_w_spatial, m_b_spatial, m_lb_logits, m_hgrn_norm_g, m_mem_norm_g, m_w_mem_kv, m_w_branch, m_w_out, m_norm2_g, m_w_up, m_conv_w, m_conv_b, m_w_down, m_final_g, v_norm1_g, v_w_in, v_ln_v_g, v_ln_v_b, v_w_spatial, v_b_spatial, v_lb_logits, v_hgrn_norm_g, v_mem_norm_g, v_w_mem_kv, v_w_branch, v_w_out, v_norm2_g, v_w_up, v_conv_w, v_conv_b, v_w_down, v_final_g):
    given = dict(locals())
    order = ("norm1_g", "w_in", "ln_v_g", "ln_v_b", "w_spatial", "b_spatial", "lb_logits", "hgrn_norm_g", "mem_norm_g",
             "w_mem_kv", "w_branch", "w_out", "norm2_g", "w_up", "conv_w", "conv_b", "w_down", "final_g")
    groups = dict(a=("w_in",), b=("w_mem_kv", "w_branch", "w_out"), c=("w_up", "conv_w", "w_down"))

    wire = {n: given[n][0].astype(f32 if n == "conv_w" else bf16) for ns in groups.values() for n in ns}
    g_in = _all_gather([wire["w_in"]], "gather_w_in")[0]
    late = groups["b"] + groups["c"]
    w_in_full = _cols_full(g_in)
    w_in_full, rest_wire = lax.optimization_barrier((w_in_full, [wire[n] for n in late]))
    rest = _seq_exchange(rest_wire, True, "gather_rest", 1)

    def late_b(after):
        _, (kv_, br_, out_) = lax.optimization_barrier((after, tuple(rest[0:3])))
        br_ = _cols_full(br_)
        return dict(w_mem_kv=kv_.reshape(D_MODEL, 2 * 512), w_branch=[br_[n] for n in range(3)], w_out=out_.reshape(D_MODEL, D_MODEL))

    def late_c(after):
        _, (up_, cw_, down_) = lax.optimization_barrier((after, tuple(rest[3:6])))
        return dict(w_up=_cols_full(up_), conv_w=_cols_full(cw_), w_down=down_.reshape(D_FF, D_MODEL))

    to_parts = dict(w_in=_cols_parts, w_up=_cols_parts, conv_w=_cols_parts,
                    w_branch=lambda g_: _cols_parts(jnp.stack(g_)).reshape(N_DEV, -1, 128),
                    w_mem_kv=lambda g_: g_.reshape(N_DEV, -1, 2 * 512), w_out=lambda g_: g_.reshape(N_DEV, -1, D_MODEL),
                    w_down=lambda g_: g_.reshape(N_DEV, -1, D_MODEL))
    scatters = {}

    def send(tag, grads_, chain):
        scatters[tag], tok = _exchange_start([to_parts[n](grads_[n]) for n in groups[tag]], False, f"scatter_{tag}_start")
        return lax.optimization_barrier((chain, tok))[0]

    small_2d = lambda prefix: {n: given[prefix + n].reshape(shape) for n, shape, _ in SMALL}
    p = small_2d("")
    p["w_spatial"] = w_spatial[0]
    loss_part, grad_x, gs = _local_step(x, mem, loss_target, p, w_in_full, late_b, late_c, send)
    loss = lax.psum(loss_part[0, 0], ("x", "y", "c"))

    small_gather, _ = _exchange_start([_pack_small(gs)], True, "gather_small_start")

    grads, delta, new_m, new_v = {}, {}, {}, {}
    after = grad_x
    for tag in ("c", "b", "a"):
        recv = _exchange_wait(scatters[tag], tuple(range(len(groups[tag]))), after, f"scatter_{tag}_wait")
        for n, parts in zip(groups[tag], recv):
            two_d = (-1, given[n].shape[-1])
            res = _reduce_adamw(parts, *[given[pre + n].reshape(two_d) for pre in ("", "m_", "v_")], "adamw_" + n)
            grads[n], delta[n], new_m[n], new_v[n] = [r.reshape(given[n].shape) for r in res]
            after = res[0]

    gathered = _exchange_wait(small_gather, (0,), after, "gather_small_wait")[0]
    for dst, res in zip((grads, delta, new_m, new_v), _small_update(gathered, small_2d(""), small_2d("m_"), small_2d("v_"))):
        for n, _, _ in SMALL:
            dst[n] = res[n].reshape(given[n].shape)

    return (loss, grad_x, *[grads[n] for n in order], *[delta[n] for n in order], *[new_m[n] for n in order],
            *[new_v[n] for n in order])
```

```python
import functools

import jax
import jax.numpy as jnp
from jax import lax
from jax.experimental import pallas as pl
from jax.experimental.pallas import tpu as pltpu
from jax.experimental.pallas import tpu_sc as plsc

f32 = jnp.float32
bf16 = jnp.bfloat16

N_DEV = 8
D_MODEL = 1024
EPS = 1e-6
GM_CHUNK = 128
HG_CHUNK = 64
HEAD = 128
N_HEAD = 4
MEM_LEN = 256
D_FF = 2816
IN_WIDTH = 6656
C_ZU, C_HQ, C_HF, C_HI, C_HG, C_XQ, C_GL = 0, 1024, 1536, 2048, 2560, 3072, 3584
ADAM_LR, ADAM_B1, ADAM_B2, ADAM_EPS, ADAM_WD, ADAM_STEP = 0.001, 0.9, 0.999, 1e-08, 0.01, 10
VMEM_LIMIT = 56 * 1024 * 1024
MESH = pl.DeviceIdType.MESH


def _pick(n, cands):
    for c in cands:
        if n % c == 0:
            return c
    return n


def _call(body, name, grid, in_specs, out_specs, out_shape, scratch=(), sem=None, **cp):
    params = dict(vmem_limit_bytes=VMEM_LIMIT, **cp)
    if sem is not None:
        params["dimension_semantics"] = sem
    return pl.pallas_call(
        body, name=name, grid=grid, in_specs=in_specs, out_specs=out_specs, out_shape=out_shape,
        scratch_shapes=list(scratch), compiler_params=pltpu.CompilerParams(**params))


_DN = {"nn": (((1,), (0,)), ((), ())), "nt": (((1,), (1,)), ((), ())), "tn": (((0,), (0,)), ((), ()))}


def _raw_dot(a, b, mode):
    return lax.dot_general(a.astype(bf16), b.astype(bf16), _DN[mode], preferred_element_type=f32)


@jax.custom_vjp
def _dot_nn(a, b):
    return _raw_dot(a, b, "nn")


_dot_nn.defvjp(lambda a, b: (_raw_dot(a, b, "nn"), (a, b)),
               lambda r, g: (_raw_dot(g, r[1], "nt"), _raw_dot(r[0], g, "tn")))


@jax.custom_vjp
def _dot_nt(a, b):
    return _raw_dot(a, b, "nt")


_dot_nt.defvjp(lambda a, b: (_raw_dot(a, b, "nt"), (a, b)),
               lambda r, g: (_raw_dot(g, r[1], "nn"), _raw_dot(g, r[0], "tn")))


@jax.custom_vjp
def _dot_tn(a, b):
    return _raw_dot(a, b, "tn")


_dot_tn.defvjp(lambda a, b: (_raw_dot(a, b, "tn"), (a, b)),
               lambda r, g: (_raw_dot(r[1], g, "nt"), _raw_dot(r[0], g, "nn")))


def _tri(n, lower):
    r = lax.broadcasted_iota(jnp.int32, (n, n), 0)
    c = lax.broadcasted_iota(jnp.int32, (n, n), 1)
    return ((c <= r) if lower else (c >= r)).astype(f32)


def _sel_dot(sel, x, mode, x_first=False):
    hi = x.astype(bf16)
    rest = x - hi.astype(f32)
    mid = rest.astype(bf16)
    lo = (rest - mid.astype(f32)).astype(bf16)
    sel = sel.astype(bf16)
    dot = lambda piece: lax.dot_general(*((piece, sel) if x_first else (sel, piece)), _DN[mode], preferred_element_type=f32)
    return dot(hi) + dot(mid) + dot(lo)


def _egrad(fn, x, ct):
    return jax.vjp(fn, x)[1](ct)[0]


def _mm(a, b, mode, out_dtype, name, tm, tn, tk=None, residual=None):
    if mode == "nn":
        (M, K), (_, N) = a.shape, b.shape
    elif mode == "nt":
        (M, K), (N, _) = a.shape, b.shape
    else:
        (K, M), (_, N) = a.shape, b.shape
    tm, tn = min(tm, M), min(tn, N)
    tk = K if tk is None else min(tk, K)
    assert M % tm == 0 and N % tn == 0 and K % tk == 0, (name, M, N, K, tm, tn, tk)
    nk = K // tk

    def body(*refs):
        acc_ref = refs[-1] if nk > 1 else None
        refs = refs[:-1] if nk > 1 else refs
        if residual is None:
            a_ref, b_ref, o_ref = refs
        else:
            a_ref, b_ref, r_ref, o_ref = refs

        def finish(r):
            if residual is not None:
                r = r + r_ref[...]
            o_ref[...] = r.astype(out_dtype)

        part = _raw_dot(a_ref[...], b_ref[...], mode)
        if nk == 1:
            finish(part)
            return
        k = pl.program_id(2)

        @pl.when(k == 0)
        def _():
            acc_ref[...] = part

        @pl.when((k > 0) & (k < nk - 1))
        def _():
            acc_ref[...] += part

        @pl.when(k == nk - 1)
        def _():
            finish(acc_ref[...] + part)

    a_spec = {"nn": pl.BlockSpec((tm, tk), lambda i, j, k: (i, k)),
              "nt": pl.BlockSpec((tm, tk), lambda i, j, k: (i, k)),
              "tn": pl.BlockSpec((tk, tm), lambda i, j, k: (k, i))}[mode]
    b_spec = {"nn": pl.BlockSpec((tk, tn), lambda i, j, k: (k, j)),
              "nt": pl.BlockSpec((tn, tk), lambda i, j, k: (j, k)),
              "tn": pl.BlockSpec((tk, tn), lambda i, j, k: (k, j))}[mode]
    o_spec = pl.BlockSpec((tm, tn), lambda i, j, k: (i, j))
    in_specs = [a_spec, b_spec] + ([o_spec] if residual is not None else [])
    args = (a, b) + ((residual,) if residual is not None else ())
    return _call(body, name, (M // tm, N // tn, nk), in_specs, o_spec, jax.ShapeDtypeStruct((M, N), out_dtype),
                 scratch=[pltpu.VMEM((tm, tn), f32)] if nk > 1 else [], sem=("parallel", "parallel", "arbitrary"))(*args)


def _rms_fwd(x, g, name):
    R, Dd = x.shape
    tr = _pick(R, (512, 256, 128))

    def body(x_ref, g_ref, o_ref):
        xf = x_ref[...]
        y = xf * lax.rsqrt(jnp.mean(xf * xf, axis=-1, keepdims=True) + EPS)
        o_ref[...] = (y * g_ref[...]).astype(bf16)

    return _call(body, name, (R // tr,), [pl.BlockSpec((tr, Dd), lambda i: (i, 0)), pl.BlockSpec((1, Dd), lambda i: (0, 0))],
                 pl.BlockSpec((tr, Dd), lambda i: (i, 0)), jax.ShapeDtypeStruct((R, Dd), bf16), sem=("parallel",))(x, g)


def _rms_bwd(x, g, dh, name, residual=None):
    R, Dd = x.shape
    tr = _pick(R, (512, 256, 128))

    def body(*refs):
        if residual is None:
            x_ref, g_ref, dh_ref, dx_ref, dg_ref = refs
        else:
            x_ref, g_ref, dh_ref, r_ref, dx_ref, dg_ref = refs
        xf = x_ref[...]
        rs = lax.rsqrt(jnp.mean(xf * xf, axis=-1, keepdims=True) + EPS)
        y = xf * rs
        dh_ = dh_ref[...].astype(f32)
        dy = dh_ * g_ref[...]
        dx = rs * (dy - y * jnp.mean(dy * y, axis=-1, keepdims=True))
        if residual is not None:
            dx = dx + r_ref[...]
        dx_ref[...] = dx

        @pl.when(pl.program_id(0) == 0)
        def _():
            dg_ref[...] = jnp.zeros_like(dg_ref)

        dg_ref[...] += jnp.sum(dh_ * y, axis=0, keepdims=True)

    row = pl.BlockSpec((tr, Dd), lambda i: (i, 0))
    vec = pl.BlockSpec((1, Dd), lambda i: (0, 0))
    in_specs = [row, vec, row] + ([row] if residual is not None else [])
    args = (x, g, dh) + ((residual,) if residual is not None else ())
    return _call(body, name, (R // tr,), in_specs, (row, vec),
                 (jax.ShapeDtypeStruct((R, Dd), f32), jax.ShapeDtypeStruct((1, Dd), f32)), sem=("arbitrary",))(*args)


def _final_loss(x2, g, target):
    R, Dd = x2.shape
    tr = _pick(R, (512, 256, 128))

    def body(x_ref, g_ref, t_ref, loss_ref, dx_ref, dg_ref):
        xf = x_ref[...]
        rs = lax.rsqrt(jnp.mean(xf * xf, axis=-1, keepdims=True) + EPS)
        y = xf * rs
        err = y * g_ref[...] - t_ref[...]
        dh_ = err * (1.0 / Dd)
        dy = dh_ * g_ref[...]
        dx_ref[...] = rs * (dy - y * jnp.mean(dy * y, axis=-1, keepdims=True))

        @pl.when(pl.program_id(0) == 0)
        def _():
            dg_ref[...] = jnp.zeros_like(dg_ref)
            loss_ref[...] = jnp.zeros_like(loss_ref)

        dg_ref[...] += jnp.sum(dh_ * y, axis=0, keepdims=True)
        part = jnp.sum(jnp.mean(err * err, axis=-1, keepdims=True), axis=0, keepdims=True)
        loss_ref[...] += 0.5 * part

    row = pl.BlockSpec((tr, Dd), lambda i: (i, 0))
    vec = pl.BlockSpec((1, Dd), lambda i: (0, 0))
    return _call(body, "final_loss", (R // tr,), [row, vec, row], (pl.BlockSpec((1, 128), lambda i: (0, 0)), row, vec),
                 (jax.ShapeDtypeStruct((1, 128), f32), jax.ShapeDtypeStruct((R, Dd), f32), jax.ShapeDtypeStruct((1, Dd), f32)),
                 sem=("arbitrary",))(x2, g, target)


def _gmlp_parts(zuv, ln_g, ln_b):
    zu, zv = zuv[:, :512], zuv[:, 512:]
    u = jax.nn.gelu(zu)
    v = jax.nn.gelu(zv)
    mu = jnp.mean(v, axis=-1, keepdims=True)
    rs = lax.rsqrt(jnp.mean(jnp.square(v - mu), axis=-1, keepdims=True) + EPS)
    xh = (v - mu) * rs
    return zu, zv, u, xh, rs, xh * ln_g + ln_b


def _gmlp_fwd(proj, ln_g, ln_b, w_s, b_st):
    T = proj.shape[0]

    def body(p_ref, g_ref, b_ref, w_ref, bs_ref, o_ref):
        _, _, u, _, _, vn = _gmlp_parts(p_ref[...].astype(f32), g_ref[...], b_ref[...])
        causal = _tri(GM_CHUNK, True) > 0
        for gi in range(N_HEAD):
            sl = slice(gi * HEAD, (gi + 1) * HEAD)
            w = jnp.where(causal, w_ref[gi], 0.0)
            mixed = _raw_dot(w, vn[:, sl], "nn") + bs_ref[:, gi:gi + 1]
            o_ref[:, sl] = (u[:, sl] * mixed).astype(bf16)

    vec = pl.BlockSpec((1, 512), lambda i: (0, 0))
    return _call(body, "gmlp_fwd", (T // GM_CHUNK,),
                 [pl.BlockSpec((GM_CHUNK, 1024), lambda i: (i, 0)), vec, vec,
                  pl.BlockSpec((N_HEAD, GM_CHUNK, GM_CHUNK), lambda i: (0, 0, 0)), pl.BlockSpec((GM_CHUNK, 128), lambda i: (0, 0))],
                 pl.BlockSpec((GM_CHUNK, 512), lambda i: (i, 0)), jax.ShapeDtypeStruct((T, 512), bf16), sem=("parallel",))(
        proj, ln_g, ln_b, w_s, b_st)


def _gmlp_bwd(proj, ln_g, ln_b, w_s, b_st, da):
    T = proj.shape[0]

    def body(p_ref, g_ref, b_ref, w_ref, bs_ref, da_ref, dp_ref, dg_ref, db_ref, dw_ref, dbs_ref):
        zu, zv, u, xh, rs, vn = _gmlp_parts(p_ref[...].astype(f32), g_ref[...], b_ref[...])
        causal = _tri(GM_CHUNK, True) > 0
        sub = lax.broadcasted_iota(jnp.int32, (8, GM_CHUNK), 0)
        ones = jnp.ones((8, HEAD), f32)
        dout = da_ref[...].astype(f32)

        @pl.when(pl.program_id(0) == 0)
        def _():
            for r in (dg_ref, db_ref, dw_ref, dbs_ref):
                r[...] = jnp.zeros_like(r)

        du, dvn, dbs = [], [], jnp.zeros((8, GM_CHUNK), f32)
        for gi in range(N_HEAD):
            sl = slice(gi * HEAD, (gi + 1) * HEAD)
            w = jnp.where(causal, w_ref[gi], 0.0)
            mixed = _raw_dot(w, vn[:, sl], "nn") + bs_ref[:, gi:gi + 1]
            du.append(dout[:, sl] * mixed)
            dm = dout[:, sl] * u[:, sl]
            row_sums = _sel_dot(ones, dm, "nt")
            dbs = dbs + jnp.where(sub == gi, row_sums, 0.0)
            dw_ref[gi] += jnp.where(causal, _raw_dot(dm, vn[:, sl], "nt"), 0.0)
            dvn.append(_raw_dot(w, dm, "tn"))
        dbs_ref[...] += dbs
        du = jnp.concatenate(du, axis=-1)
        dvn = jnp.concatenate(dvn, axis=-1)
        dg_ref[...] += jnp.sum(dvn * xh, axis=0, keepdims=True)
        db_ref[...] += jnp.sum(dvn, axis=0, keepdims=True)
        dxh = dvn * g_ref[...]
        dv = rs * (dxh - jnp.mean(dxh, axis=-1, keepdims=True) - xh * jnp.mean(dxh * xh, axis=-1, keepdims=True))
        dp_ref[:, :512] = _egrad(jax.nn.gelu, zu, du).astype(bf16)
        dp_ref[:, 512:] = _egrad(jax.nn.gelu, zv, dv).astype(bf16)

    vec = pl.BlockSpec((1, 512), lambda i: (0, 0))
    wsp = pl.BlockSpec((N_HEAD, GM_CHUNK, GM_CHUNK), lambda i: (0, 0, 0))
    return _call(body, "gmlp_bwd", (T // GM_CHUNK,),
                 [pl.BlockSpec((GM_CHUNK, 1024), lambda i: (i, 0)), vec, vec, wsp, pl.BlockSpec((GM_CHUNK, 128), lambda i: (0, 0)),
                  pl.BlockSpec((GM_CHUNK, 512), lambda i: (i, 0))],
                 (pl.BlockSpec((GM_CHUNK, 1024), lambda i: (i, 0)), vec, vec, wsp, pl.BlockSpec((8, GM_CHUNK), lambda i: (0, 0))),
                 (jax.ShapeDtypeStruct((T, 1024), bf16), jax.ShapeDtypeStruct((1, 512), f32), jax.ShapeDtypeStruct((1, 512), f32),
                  jax.ShapeDtypeStruct((N_HEAD, GM_CHUNK, GM_CHUNK), f32), jax.ShapeDtypeStruct((8, GM_CHUNK), f32)),
                 sem=("arbitrary",))(proj, ln_g, ln_b, w_s, b_st, da)


HG_SUB = 8
HG_NSUB = HG_CHUNK // HG_SUB


def _two_level_matrix():
    r = lax.broadcasted_iota(jnp.int32, (2 * HG_CHUNK, HG_CHUNK), 0)
    c = lax.broadcasted_iota(jnp.int32, (2 * HG_CHUNK, HG_CHUNK), 1)
    t = jnp.where(r < HG_CHUNK, r, r - HG_CHUNK)
    local = (r < HG_CHUNK) & (t // HG_SUB == c // HG_SUB) & (c <= t)
    before = (r >= HG_CHUNK) & (c < (t // HG_SUB) * HG_SUB)
    return (local | before).astype(f32)


def _two_level_sums(x):
    two = _sel_dot(_two_level_matrix(), x, "nn")
    return two[:HG_CHUNK], two[HG_CHUNK:]


@jax.custom_vjp
def _two_level_cumsum(x):
    return _two_level_sums(x)


_two_level_cumsum.defvjp(
    lambda x: (_two_level_sums(x), None),
    lambda _, g: (_sel_dot(_two_level_matrix(), jnp.concatenate(g, axis=0), "tn"),))


def _tile_matrix():
    s = lax.broadcasted_iota(jnp.int32, (HG_SUB, HG_CHUNK), 0)
    j = lax.broadcasted_iota(jnp.int32, (HG_SUB, HG_CHUNK), 1)
    return (j % HG_SUB == s).astype(f32)


@jax.custom_vjp
def _tile_lanes(x):
    return _sel_dot(_tile_matrix(), x, "nn", x_first=True)


_tile_lanes.defvjp(
    lambda x: (_sel_dot(_tile_matrix(), x, "nn", x_first=True), None),
    lambda _, g: (_sel_dot(_tile_matrix(), g, "nt", x_first=True),))


def _block_rows(x):
    k = x.shape[-1]
    return jnp.broadcast_to(x.reshape(HG_NSUB, 1, HG_SUB, k), (HG_NSUB, HG_SUB, HG_SUB, k)).reshape(HG_CHUNK, HG_SUB, k)


def _hgrn_chunk(st0, q_raw, f_raw, i_raw, g_raw, l0, l1, ng):
    C, SUB = HG_CHUNK, HG_SUB
    lb = jax.nn.sigmoid(l0 - l1)
    fg = lb + (1.0 - lb) * jax.nn.sigmoid(f_raw)
    kk = 1.0 - fg
    qf = jax.nn.silu(q_raw)
    al, base = _two_level_cumsum(jnp.log(fg))
    a = al + base
    row = lax.broadcasted_iota(jnp.int32, (C, HEAD), 0)
    a_last = jnp.sum(jnp.where(row == C - 1, a, 0.0), axis=0, keepdims=True)
    inter = _dot_nt(qf * jnp.exp(a), st0)
    qt = qf * jnp.exp(al)
    rb = lax.broadcasted_iota(jnp.int32, (C, C), 0) // SUB
    cb = lax.broadcasted_iota(jnp.int32, (C, C), 1) // SUB
    scores = jnp.zeros((C, C), f32)
    for i in range(1, HG_NSUB):
        base_i = jnp.sum(jnp.where(row == i * SUB, base, 0.0), axis=0, keepdims=True)
        kt = kk * jnp.exp(jnp.minimum(base_i - a, 0.0))
        scores = scores + jnp.where((rb == i) & (cb < i), _dot_nt(qt, kt), 0.0)
    t_i = lax.broadcasted_iota(jnp.int32, (C, SUB, HEAD), 0) % SUB
    s_i = lax.broadcasted_iota(jnp.int32, (C, SUB, HEAD), 1)
    decay = jnp.exp(jnp.where(s_i <= t_i, al[:, None, :] - _block_rows(al), -jnp.inf))
    diag = jnp.sum(qf[:, None, :] * decay * _block_rows(kk), axis=-1)
    scores = scores + jnp.where(rb == cb, _tile_lanes(diag), 0.0)
    o = inter + _dot_nn(scores, i_raw)
    st1 = jnp.exp(a_last) * st0 + _dot_tn(i_raw, kk * jnp.exp(a_last - a))
    on = o * lax.rsqrt(jnp.mean(o * o, axis=-1, keepdims=True) + EPS) * ng
    return st1, on * jax.nn.silu(g_raw)


def _hgrn_specs(S, Bl, rev):
    N = S // HG_CHUNK
    chunk = (lambda n: N - 1 - n) if rev else (lambda n: n)
    col = lambda c0: pl.BlockSpec((Bl, HG_CHUNK, 512), lambda n: (0, chunk(n), c0 // 512))
    st = pl.BlockSpec((Bl, N_HEAD, 1, HEAD, HEAD), lambda n: (0, 0, chunk(n), 0, 0))
    full = lambda *s: pl.BlockSpec(s, functools.partial(lambda n, nd: (0,) * nd, nd=len(s)))
    return N, col, st, full


def _hgrn_fwd(proj, lb_logits, ng, Bl, S):
    N, col, st, full = _hgrn_specs(S, Bl, False)

    def body(q_ref, f_ref, i_ref, g_ref, l_ref, ng_ref, o_ref, st_ref, state):
        @pl.when(pl.program_id(0) == 0)
        def _():
            state[...] = jnp.zeros_like(state)

        for b in range(Bl):
            for h in range(N_HEAD):
                sl = slice(h * HEAD, (h + 1) * HEAD)
                st0 = state[b, h]
                st_ref[b, h, 0] = st0
                st1, out = _hgrn_chunk(st0, *[r[b, :, sl].astype(f32) for r in (q_ref, f_ref, i_ref, g_ref)],
                                       l_ref[0:1, sl], l_ref[1:2, sl], ng_ref[...])
                state[b, h] = st1
                o_ref[b, :, sl] = out.astype(bf16)

    return _call(body, "hgrn_fwd", (N,), [col(C_HQ), col(C_HF), col(C_HI), col(C_HG), full(2, 512), full(1, HEAD)],
                 (col(0), st),
                 (jax.ShapeDtypeStruct((Bl, S, 512), bf16), jax.ShapeDtypeStruct((Bl, N_HEAD, N, HEAD, HEAD), f32)),
                 scratch=[pltpu.VMEM((Bl, N_HEAD, HEAD, HEAD), f32)], sem=("arbitrary",))(
        proj, proj, proj, proj, lb_logits, ng)


def _hgrn_bwd(proj, lb_logits, ng, states, db, Bl, S):
    N, col, st, full = _hgrn_specs(S, Bl, True)

    def body(q_ref, f_ref, i_ref, g_ref, l_ref, ng_ref, st_ref, db_ref,
             dq_ref, df_ref, di_ref, dg_ref, dl_ref, dng_ref, dstate):
        @pl.when(pl.program_id(0) == 0)
        def _():
            dstate[...] = jnp.zeros_like(dstate)
            dl_ref[...] = jnp.zeros_like(dl_ref)
            dng_ref[...] = jnp.zeros_like(dng_ref)

        for b in range(Bl):
            for h in range(N_HEAD):
                sl = slice(h * HEAD, (h + 1) * HEAD)
                _, vjp = jax.vjp(_hgrn_chunk, st_ref[b, h, 0], *[r[b, :, sl].astype(f32) for r in (q_ref, f_ref, i_ref, g_ref)],
                                 l_ref[0:1, sl], l_ref[1:2, sl], ng_ref[...])
                dst0, dq, df, di, dg, dl0, dl1, dng = vjp((dstate[b, h], db_ref[b, :, sl].astype(f32)))
                dstate[b, h] = dst0
                dq_ref[b, :, sl] = dq.astype(bf16)
                df_ref[b, :, sl] = df.astype(bf16)
                di_ref[b, :, sl] = di.astype(bf16)
                dg_ref[b, :, sl] = dg.astype(bf16)
                dl_ref[0:1, sl] += dl0
                dl_ref[1:2, sl] += dl1
                dng_ref[b, h] += dng

    return _call(body, "hgrn_bwd", (N,),
                 [col(C_HQ), col(C_HF), col(C_HI), col(C_HG), full(2, 512), full(1, HEAD), st, col(0)],
                 (*[col(0)] * 4, full(2, 512), full(Bl, N_HEAD, 1, HEAD)),
                 (*[jax.ShapeDtypeStruct((Bl, S, 512), bf16)] * 4, jax.ShapeDtypeStruct((2, 512), f32),
                  jax.ShapeDtypeStruct((Bl, N_HEAD, 1, HEAD), f32)),
                 scratch=[pltpu.VMEM((Bl, N_HEAD, HEAD, HEAD), f32)], sem=("arbitrary",))(
        proj, proj, proj, proj, lb_logits, ng, states, db)


def _attn_probs(q, k):
    s = _raw_dot(q, k, "nt") * (HEAD ** -0.5)
    e = jnp.exp(s - jnp.max(s, axis=-1, keepdims=True))
    return e / jnp.sum(e, axis=-1, keepdims=True)


def _attn_specs(S, tq):
    nq = S // tq
    q = pl.BlockSpec((tq, 512), lambda b, i: (b * nq + i, C_XQ // 512))
    kv = pl.BlockSpec((1, MEM_LEN, 1024), lambda b, i: (b, 0, 0))
    o = pl.BlockSpec((tq, 512), lambda b, i: (b * nq + i, 0))
    return nq, q, kv, o


def _attn_fwd(proj, kv, Bl, S):
    tq = _pick(S, (512, 256, 128))
    nq, qs, kvs, os_ = _attn_specs(S, tq)

    def body(q_ref, kv_ref, o_ref):
        for h in range(N_HEAD):
            sl = slice(h * HEAD, (h + 1) * HEAD)
            p = _attn_probs(q_ref[:, sl], kv_ref[0, :, sl])
            o_ref[:, sl] = _raw_dot(p, kv_ref[0, :, 512 + h * HEAD:512 + (h + 1) * HEAD], "nn").astype(bf16)

    return _call(body, "attn_fwd", (Bl, nq), [qs, kvs], os_, jax.ShapeDtypeStruct((Bl * S, 512), bf16),
                 sem=("parallel", "parallel"))(proj, kv)


def _attn_bwd(proj, kv, dc, Bl, S):
    tq = _pick(S, (512, 256, 128))
    nq, qs, kvs, os_ = _attn_specs(S, tq)

    def body(q_ref, kv_ref, do_ref, dq_ref, dkv_ref):
        @pl.when(pl.program_id(1) == 0)
        def _():
            dkv_ref[...] = jnp.zeros_like(dkv_ref)

        for h in range(N_HEAD):
            sl = slice(h * HEAD, (h + 1) * HEAD)
            vsl = slice(512 + h * HEAD, 512 + (h + 1) * HEAD)
            q, k, v, do = q_ref[:, sl], kv_ref[0, :, sl], kv_ref[0, :, vsl], do_ref[:, sl]
            p = _attn_probs(q, k)
            dkv_ref[0, :, vsl] += _raw_dot(p, do, "tn")
            dp = _raw_dot(do, v, "nt")
            ds = p * (dp - jnp.sum(dp * p, axis=-1, keepdims=True)) * (HEAD ** -0.5)
            dq_ref[:, sl] = _raw_dot(ds, k, "nn").astype(bf16)
            dkv_ref[0, :, sl] += _raw_dot(ds, q, "tn")

    return _call(body, "attn_bwd", (Bl, nq), [qs, kvs, os_], (os_, kvs),
                 (jax.ShapeDtypeStruct((Bl * S, 512), bf16), jax.ShapeDtypeStruct((Bl, MEM_LEN, 1024), f32)),
                 sem=("arbitrary", "arbitrary"))(proj, kv, dc)


def _merge_specs(tm, tn):
    br = pl.BlockSpec((tm, 512), lambda i, j: (i, 0))
    w = pl.BlockSpec((512, tn), lambda i, j: (0, j))
    gl = [pl.BlockSpec((tm, tn), functools.partial(lambda i, j, n: (i, (C_GL + n * D_MODEL) // tn + j), n=n)) for n in range(3)]
    return [br, br, br, w, w, w, *gl]


def _merge_fwd(branches, wb, proj):
    T = proj.shape[0]
    tm, tn = _pick(T, (1024, 512, 256, 128)), 512

    def body(a_ref, b_ref, c_ref, w0, w1, w2, g0, g1, g2, o_ref):
        acc = jnp.zeros((tm, tn), f32)
        for x_ref, w_ref, g_ref in ((a_ref, w0, g0), (b_ref, w1, g1), (c_ref, w2, g2)):
            acc = acc + jax.nn.sigmoid(g_ref[...].astype(f32)) * _raw_dot(x_ref[...], w_ref[...], "nn")
        o_ref[...] = acc.astype(bf16)

    return _call(body, "merge_fwd", (T // tm, D_MODEL // tn), _merge_specs(tm, tn), pl.BlockSpec((tm, tn), lambda i, j: (i, j)),
                 jax.ShapeDtypeStruct((T, D_MODEL), bf16), sem=("parallel", "parallel"))(*branches, *wb, proj, proj, proj)


def _merge_bwd(branches, wb, proj, dmerged):
    T = proj.shape[0]
    tm, tn = _pick(T, (1024, 512, 256, 128)), 512

    def body(a_ref, b_ref, c_ref, w0, w1, w2, g0, g1, g2, dm_ref, dgl_ref, d0, d1, d2):
        dm = dm_ref[...]
        for n, (x_ref, w_ref, g_ref, d_ref) in enumerate(((a_ref, w0, g0, d0), (b_ref, w1, g1, d1), (c_ref, w2, g2, d2))):
            up = _raw_dot(x_ref[...], w_ref[...], "nn")
            logits = g_ref[...].astype(f32)
            dgl_ref[n] = _egrad(jax.nn.sigmoid, logits, dm * up).astype(bf16)
            d_ref[...] = (dm * jax.nn.sigmoid(logits)).astype(bf16)

    blk = pl.BlockSpec((tm, tn), lambda i, j: (i, j))
    sh = jax.ShapeDtypeStruct((T, D_MODEL), bf16)
    outs = _call(body, "merge_bwd", (T // tm, D_MODEL // tn), [*_merge_specs(tm, tn), blk],
                 (pl.BlockSpec((3, tm, tn), lambda i, j: (0, i, j)), blk, blk, blk),
                 (jax.ShapeDtypeStruct((3, T, D_MODEL), bf16), sh, sh, sh),
                 sem=("parallel", "parallel"))(*branches, *wb, proj, proj, proj, dmerged)
    return outs[0], outs[1:]


CONV_TC = 256


def _shift_down(a, k):
    row = lax.broadcasted_iota(jnp.int32, a.shape, 0)
    return jnp.where(row >= k, pltpu.roll(a, k, 0), 0.0)


def _shift_up(a, k):
    n = a.shape[0]
    row = lax.broadcasted_iota(jnp.int32, a.shape, 0)
    return jnp.where(row < n - k, pltpu.roll(a, n - k, 0), 0.0)


def _conv_pre(a, cw, cb):
    return cb + cw[0:1] * _shift_down(a, 2) + cw[1:2] * _shift_down(a, 1) + cw[2:3] * a


def _conv_fwd(ab, cw, cb, Bl, S):
    nc = D_FF // CONV_TC

    def body(a_ref, b_ref, cw_ref, cb_ref, o_ref):
        ac = _conv_pre(a_ref[0].astype(f32), cw_ref[...], cb_ref[...])
        o_ref[0] = (jax.nn.silu(ac) * b_ref[0].astype(f32)).astype(bf16)

    return _call(body, "conv_fwd", (Bl, nc),
                 [pl.BlockSpec((1, S, CONV_TC), lambda b, c: (b, 0, c)), pl.BlockSpec((1, S, CONV_TC), lambda b, c: (b, 0, nc + c)),
                  pl.BlockSpec((3, CONV_TC), lambda b, c: (0, c)), pl.BlockSpec((1, CONV_TC), lambda b, c: (0, c))],
                 pl.BlockSpec((1, S, CONV_TC), lambda b, c: (b, 0, c)), jax.ShapeDtypeStruct((Bl, S, D_FF), bf16),
                 sem=("parallel", "parallel"))(ab, ab, cw, cb)


def _conv_bwd(ab, cw, cb, dact, Bl, S):
    nc = D_FF // CONV_TC

    def body(a_ref, b_ref, cw_ref, cb_ref, d_ref, da_ref, db_ref, dcw_ref, dcb_ref):
        @pl.when(pl.program_id(1) == 0)
        def _():
            dcw_ref[...] = jnp.zeros_like(dcw_ref)
            dcb_ref[...] = jnp.zeros_like(dcb_ref)

        a, cw = a_ref[0].astype(f32), cw_ref[...]
        ac = _conv_pre(a, cw, cb_ref[...])
        dact_ = d_ref[0].astype(f32)
        db_ref[0] = (dact_ * jax.nn.silu(ac)).astype(bf16)
        dac = _egrad(jax.nn.silu, ac, dact_ * b_ref[0].astype(f32))
        da_ref[0] = (cw[2:3] * dac + cw[1:2] * _shift_up(dac, 1) + cw[0:1] * _shift_up(dac, 2)).astype(bf16)
        dcw_ref[0:1, :] += jnp.sum(dac * _shift_down(a, 2), axis=0, keepdims=True)
        dcw_ref[1:2, :] += jnp.sum(dac * _shift_down(a, 1), axis=0, keepdims=True)
        dcw_ref[2:3, :] += jnp.sum(dac * a, axis=0, keepdims=True)
        dcb_ref[...] += jnp.sum(dac, axis=0, keepdims=True)

    seq = pl.BlockSpec((1, S, CONV_TC), lambda c, b: (b, 0, c))
    return _call(body, "conv_bwd", (nc, Bl),
                 [seq, pl.BlockSpec((1, S, CONV_TC), lambda c, b: (b, 0, nc + c)), pl.BlockSpec((3, CONV_TC), lambda c, b: (0, c)),
                  pl.BlockSpec((1, CONV_TC), lambda c, b: (0, c)), seq],
                 (seq, seq, pl.BlockSpec((3, CONV_TC), lambda c, b: (0, c)), pl.BlockSpec((1, CONV_TC), lambda c, b: (0, c))),
                 (jax.ShapeDtypeStruct((Bl, S, D_FF), bf16), jax.ShapeDtypeStruct((Bl, S, D_FF), bf16),
                  jax.ShapeDtypeStruct((3, D_FF), f32), jax.ShapeDtypeStruct((1, D_FF), f32)),
                 sem=("arbitrary", "arbitrary"))(ab, ab, cw, cb, dact)


def _local_step(x, mem, target, p, w_in, late_b, late_c, send):
    Bl, S, Dd = x.shape
    T = Bl * S
    x2d, t2d, mem2d = x.reshape(T, Dd), target.reshape(T, Dd), mem.reshape(Bl * MEM_LEN, Dd)
    b_st = jnp.pad(p["b_spatial"].T, ((0, 0), (0, 128 - N_HEAD)))
    lbl = p["lb_logits"]

    h = _rms_fwd(x2d, p["norm1_g"], "norm1_fwd")
    proj = _mm(h, w_in, "nn", bf16, "proj_fwd", 1024, 1664)
    a_out = _gmlp_fwd(proj, p["ln_v_g"], p["ln_v_b"], p["w_spatial"], b_st)
    proj3 = proj.reshape(Bl, S, IN_WIDTH)
    b_out, states = _hgrn_fwd(proj3, lbl, p["hgrn_norm_g"], Bl, S)
    b_out = b_out.reshape(T, 512)
    memn = _rms_fwd(mem2d, p["mem_norm_g"], "memnorm_fwd")
    w = late_b(b_out)
    wb = w["w_branch"]
    kv = _mm(memn, w["w_mem_kv"], "nn", f32, "kv_fwd", 512, 1024).reshape(Bl, MEM_LEN, 2 * 512)
    c_out = _attn_fwd(proj, kv, Bl, S)
    branches = (a_out, b_out, c_out)
    merged = _merge_fwd(branches, wb, proj)
    x1 = _mm(merged, w["w_out"], "nn", f32, "out_fwd", 1024, 1024, residual=x2d)
    h2 = _rms_fwd(x1, p["norm2_g"], "norm2_fwd")
    w.update(late_c(h2))
    ab = _mm(h2, w["w_up"], "nn", bf16, "up_fwd", 1024, 1408)
    act = _conv_fwd(ab.reshape(Bl, S, 2 * D_FF), w["conv_w"], p["conv_b"], Bl, S).reshape(T, D_FF)
    x2 = _mm(act, w["w_down"], "nn", f32, "down_fwd", 512, 1024, residual=x1)
    loss_part, dx2, g_final = _final_loss(x2, p["final_g"], t2d)

    g_w_down = _mm(act, dx2, "tn", bf16, "down_dw", 1408, 1024, 1024)
    dact = _mm(dx2, w["w_down"], "nt", bf16, "down_dx", 1024, 1408)
    da, db, g_conv_w, g_conv_b = _conv_bwd(ab.reshape(Bl, S, 2 * D_FF), w["conv_w"], p["conv_b"], dact.reshape(Bl, S, D_FF), Bl, S)
    dab = jnp.concatenate([da.reshape(T, D_FF), db.reshape(T, D_FF)], axis=-1)
    g_w_up = _mm(h2, dab, "tn", bf16, "up_dw", 512, 1408, 1024)
    tok = send("c", dict(w_up=g_w_up, conv_w=g_conv_w, w_down=g_w_down))
    dh2 = _mm(dab, w["w_up"], "nt", f32, "up_dx", 1024, 1024, 1408)
    dx1, g_norm2 = _rms_bwd(x1, p["norm2_g"] + tok[0, 0], dh2, "norm2_bwd", residual=dx2)

    g_w_out = _mm(merged, dx1, "tn", bf16, "out_dw", 1024, 1024, 1024)
    dmerged = _mm(dx1, w["w_out"], "nt", f32, "out_dx", 1024, 1024)
    dgl, dup = _merge_bwd(branches, wb, proj, dmerged)
    g_w_branch = [_mm(branches[n], dup[n], "tn", bf16, f"branch_dw{n}", 512, 1024, 1024) for n in range(3)]
    dbr = [_mm(dup[n], wb[n], "nt", bf16, f"branch_dx{n}", 1024, 512) for n in range(3)]
    dxq, dkv = _attn_bwd(proj, kv, dbr[2], Bl, S)
    dkv = dkv.reshape(Bl * MEM_LEN, 2 * 512)
    g_w_kv = _mm(memn, dkv, "tn", bf16, "kv_dw", 1024, 1024, 512)
    tok = send("b", dict(w_mem_kv=g_w_kv, w_branch=g_w_branch, w_out=g_w_out))
    dmemn = _mm(dkv, w["w_mem_kv"], "nt", f32, "kv_dx", 512, 1024)
    _, g_mem_norm = _rms_bwd(mem2d, p["mem_norm_g"], dmemn, "memnorm_bwd")
    dzuv, g_ln_g, g_ln_b, g_w_sp, g_b_sp = _gmlp_bwd(proj, p["ln_v_g"] + tok[0, 0], p["ln_v_b"], p["w_spatial"], b_st, dbr[0])
    *dqfig, g_lbl, g_ng = _hgrn_bwd(proj3, lbl, p["hgrn_norm_g"], states, dbr[1].reshape(Bl, S, 512), Bl, S)
    dq, df, di, dg = [d.reshape(T, 512) for d in dqfig]
    dproj = jnp.concatenate([dzuv, dq, df, di, dg, dxq, dgl[0], dgl[1], dgl[2]], axis=-1)
    g_w_in = _mm(h, dproj, "tn", bf16, "proj_dw", 512, 1664, 1024)
    tok = send("a", dict(w_in=g_w_in))
    dh = _mm(dproj, w_in, "nt", f32, "proj_dx", 1024, 1024, 1664)
    dx, g_norm1 = _rms_bwd(x2d, p["norm1_g"] + tok[0, 0], dh, "norm1_bwd", residual=dx1)

    gs = dict(w_spatial=g_w_sp, norm1_g=g_norm1, mem_norm_g=g_mem_norm, norm2_g=g_norm2, final_g=g_final, lb_logits=g_lbl,
              ln_v_g=g_ln_g, ln_v_b=g_ln_b, b_spatial=g_b_sp, hgrn_norm_g=g_ng, conv_b=g_conv_b)
    return loss_part, dx.reshape(Bl, S, Dd), gs


def _coords():
    return lax.axis_index("x"), lax.axis_index("y"), lax.axis_index("c")


def _slot(dev):
    return 4 * dev[0] + 2 * dev[1] + dev[2]


def _comm_call(body, name, arrays, out_shapes, n_sem):
    n = len(arrays)
    hbm = pl.BlockSpec(memory_space=pl.ANY)
    return pl.pallas_call(
        body, name=name, out_shape=out_shapes, in_specs=[hbm] * n, out_specs=[hbm] * n,
        scratch_shapes=[pltpu.SemaphoreType.DMA((n_sem, n)), pltpu.SemaphoreType.DMA((n_sem, n)), pltpu.SemaphoreType.DMA((n,))])(*arrays)


def _all_gather(blocks, name):
    n = len(blocks)

    def body(*refs):
        x_refs, o_refs, (send_sems, recv_sems, local_sems) = refs[:n], refs[n:2 * n], refs[2 * n:]
        x, y, c = _coords()
        me, sibling = (x, y, c), (x, y, 1 - c)
        chips = [(1 - x, y), (x, 1 - y), (1 - x, 1 - y)]

        def copy(a, k, block_dev, to, from_input=False):
            dst = o_refs[a].at[_slot(block_dev)]
            return pltpu.make_async_remote_copy(src_ref=x_refs[a] if from_input else dst, dst_ref=dst, send_sem=send_sems.at[k, a],
                                                recv_sem=recv_sems.at[k, a], device_id=to, device_id_type=MESH)

        mine = [pltpu.make_async_copy(x_refs[a], o_refs[a].at[_slot(me)], local_sems.at[a]) for a in range(n)]
        first = [copy(a, 0, me, sibling, True) for a in range(n)]
        first += [copy(a, 1 + j, me, (*chip, c), True) for j, chip in enumerate(chips) for a in range(n)]
        for cp in mine + first:
            cp.start()
        passed = []
        for j, chip in enumerate(chips):
            for a in range(n):
                copy(a, 1 + j, (*chip, c), me).wait_recv()
                fwd = copy(a, 4 + j, (*chip, c), sibling)
                fwd.start()
                passed.append(fwd)
        for a in range(n):
            copy(a, 0, sibling, me).wait_recv()
        for j, chip in enumerate(chips):
            for a in range(n):
                copy(a, 4 + j, (*chip, 1 - c), me).wait_recv()
        for cp in first + passed:
            cp.wait_send()
        for cp in mine:
            cp.wait()

    return _comm_call(body, name, blocks, [jax.ShapeDtypeStruct((N_DEV,) + b.shape, b.dtype) for b in blocks], 7)


def _all_to_all(parts, name):
    n = len(parts)
    rel = [(0, 0, 1), (0, 1, 0), (0, 1, 1), (1, 0, 0), (1, 0, 1), (1, 1, 0), (1, 1, 1)]

    def body(*refs):
        x_refs, o_refs, (send_sems, recv_sems, local_sems) = refs[:n], refs[n:2 * n], refs[2 * n:]
        x, y, c = _coords()
        me = (x, y, c)
        peers = [(x ^ dx, y ^ dy, c ^ dc) for dx, dy, dc in rel]

        def copy(a, k, peer):
            return pltpu.make_async_remote_copy(src_ref=x_refs[a].at[_slot(peer)], dst_ref=o_refs[a].at[_slot(me)], send_sem=send_sems.at[k, a],
                                                recv_sem=recv_sems.at[k, a], device_id=peer, device_id_type=MESH)

        def arrival(a, k, peer):
            return pltpu.make_async_remote_copy(src_ref=x_refs[a].at[_slot(me)], dst_ref=o_refs[a].at[_slot(peer)], send_sem=send_sems.at[k, a],
                                                recv_sem=recv_sems.at[k, a], device_id=peer, device_id_type=MESH)

        mine = [pltpu.make_async_copy(x_refs[a].at[_slot(me)], o_refs[a].at[_slot(me)], local_sems.at[a]) for a in range(n)]
        sends = [copy(a, k, peer) for k, peer in enumerate(peers) for a in range(n)]
        for cp in mine + sends:
            cp.start()
        for k, peer in enumerate(peers):
            for a in range(n):
                arrival(a, k, peer).wait_recv()
        for cp in sends:
            cp.wait_send()
        for cp in mine:
            cp.wait()

    return _comm_call(body, name, parts, [jax.ShapeDtypeStruct(p.shape, p.dtype) for p in parts], 7)


_HBM = pl.BlockSpec(memory_space=pltpu.HBM)
_SEM = pl.BlockSpec(memory_space=pltpu.SEMAPHORE)
_REL = [(0, 0, 1), (0, 1, 0), (0, 1, 1), (1, 0, 0), (1, 0, 1), (1, 1, 0), (1, 1, 1)]


_LINK_ORDER = (3, 1, 5, 4, 2, 6, 0)
SEND_PIECES = 4


def _pieces(shape, dtype):
    rows = shape[0]
    unit = 1 if len(shape) > 2 else (16 if dtype == bf16 else 8)
    for n in (SEND_PIECES, 2):
        if rows % (n * unit) == 0:
            return [pl.ds(i * (rows // n), rows // n) for i in range(n)]
    return [pl.ds(0, rows)]


def _split_copies(gather, src, land, send, recv, pieces):
    x, y, c = _coords()
    me = (x, y, c)
    copies = []
    for a in range(len(src)):
        block = src[a].shape if gather else src[a].shape[1:]
        for rows in (_pieces(block, src[a].dtype) if pieces else [None]):
            for k in _LINK_ORDER:
                dx, dy, dc = _REL[k]
                peer = (x ^ dx, y ^ dy, c ^ dc)
                mine, there = (src[a] if gather else src[a].at[_slot(peer)]), land[a].at[_slot(me)]
                if rows is not None:
                    mine, there = mine.at[rows], there.at[rows]
                copies.append(pltpu.make_async_remote_copy(src_ref=mine, dst_ref=there, send_sem=send[a].at[k], recv_sem=recv[a].at[k],
                                                           device_id=peer, device_id_type=MESH))
    return me, copies


def _arrivals(gather, src, land, send, recv):
    x, y, c = _coords()
    out = []
    for a in range(len(src)):
        for k, (dx, dy, dc) in enumerate(_REL):
            peer = (x ^ dx, y ^ dy, c ^ dc)
            out.append(pltpu.make_async_remote_copy(src_ref=src[a] if gather else src[a].at[_slot(peer)], dst_ref=land[a].at[_slot(peer)],
                                                    send_sem=send[a].at[k], recv_sem=recv[a].at[k], device_id=peer, device_id_type=MESH))
    return out


def _exchange_start(arrays, gather, name, after=None):
    n = len(arrays)
    e = 0 if after is None else 1
    lands = [lax.empty(((N_DEV,) + a.shape) if gather else a.shape, a.dtype) for a in arrays]

    def body(*refs):
        src, land = refs[:n], refs[n:2 * n]
        refs = refs[2 * n + e:]
        send, recv, token, local_sems = refs[:n], refs[n:2 * n], refs[4 * n], refs[4 * n + 1]
        me, out = _split_copies(gather, src, land, send, recv, True)
        local = [pltpu.make_async_copy(src[a] if gather else src[a].at[_slot(me)], land[a].at[_slot(me)], local_sems.at[a])
                 for a in range(n)]
        for cp in local:
            cp.start()
        for cp in local:
            cp.wait()
        for cp in out:
            cp.start()
        token[...] = jnp.zeros_like(token)

    sems = [pltpu.SemaphoreType.DMA((7,)) for _ in range(2 * n)]
    outs = pl.pallas_call(
        body, name=name,
        out_shape=(*sems, *[pltpu.HBM(a.shape, a.dtype) for a in arrays], *[pltpu.HBM(l.shape, l.dtype) for l in lands],
                   jax.ShapeDtypeStruct((8, 128), f32)),
        in_specs=[_HBM] * (2 * n) + [pl.BlockSpec(memory_space=pl.ANY)] * e,
        out_specs=(*[_SEM] * (2 * n), *[_HBM] * (2 * n), pl.BlockSpec(memory_space=pltpu.VMEM)),
        input_output_aliases={i: 2 * n + i for i in range(2 * n)},
        scratch_shapes=[pltpu.SemaphoreType.DMA((n,))],
        compiler_params=pltpu.CompilerParams(has_side_effects=pltpu.SideEffectType.DATAFLOW_SIDE_EFFECTING))(
        *[pltpu.with_memory_space_constraint(a, pltpu.HBM) for a in arrays],
        *[pltpu.with_memory_space_constraint(l, pltpu.HBM) for l in lands], *([after] if e else []))
    return (gather, n, outs[:4 * n]), outs[4 * n]


def _exchange_wait(handle, which, after, name):
    gather, n_all, vals = handle
    send_v, recv_v, src_v, land_v = [[vals[g * n_all + i] for i in which] for g in range(4)]
    n = len(which)

    def body(*refs):
        src, land, send, recv = refs[:n], refs[n:2 * n], refs[2 * n:3 * n], refs[3 * n:4 * n]
        for cp in _split_copies(gather, src, land, send, recv, False)[1]:
            cp.wait_send()
        for cp in _arrivals(gather, src, land, send, recv):
            cp.wait_recv()

    outs = pl.pallas_call(
        body, name=name,
        out_shape=(*[pltpu.HBM(a.shape, a.dtype) for a in src_v], *[pltpu.HBM(l.shape, l.dtype) for l in land_v]),
        in_specs=[*[_HBM] * (2 * n), *[_SEM] * (2 * n), pl.BlockSpec(memory_space=pl.ANY)], out_specs=[_HBM] * (2 * n),
        input_output_aliases={i: i for i in range(2 * n)},
        compiler_params=pltpu.CompilerParams(has_side_effects=pltpu.SideEffectType.DATAFLOW_SIDE_EFFECTING))(
        *src_v, *land_v, *send_v, *recv_v, after)
    return outs[n:]


def _seq_exchange(arrays, gather, name, collective_id):
    n = len(arrays)
    hbm = pltpu.MemorySpace.HBM
    srcs = [jax.new_ref(a, memory_space=hbm) for a in arrays]
    lands = [jax.empty_ref(jax.ShapeDtypeStruct(((N_DEV,) + a.shape) if gather else a.shape, a.dtype), memory_space=hbm) for a in arrays]

    @pl.kernel(mesh=plsc.ScalarSubcoreMesh(axis_name="sequencer", num_cores=1), name=name,
               scratch_types=(pltpu.SemaphoreType.DMA((7, n)), pltpu.SemaphoreType.DMA((7, n)), pltpu.SemaphoreType.DMA((n,))),
               compiler_params=pltpu.CompilerParams(collective_id=collective_id))
    def launch(send, recv, local):
        x, y, c = _coords()
        me = (x, y, c)
        peers = [(x ^ dx, y ^ dy, c ^ dc) for dx, dy, dc in _REL]
        barrier = pltpu.get_barrier_semaphore()
        for peer in peers:
            pl.semaphore_signal(barrier, inc=1, device_id=peer, device_id_type=MESH)
        pl.semaphore_wait(barrier, len(peers))

        def copy(a, k, peer, arrival):
            return pltpu.make_async_remote_copy(
                src_ref=srcs[a] if gather else srcs[a].at[_slot(peer)], dst_ref=lands[a].at[_slot(peer if arrival else me)],
                send_sem=send.at[k, a], recv_sem=recv.at[k, a], device_id=peer, device_id_type=MESH)

        mine = [pltpu.make_async_copy(srcs[a] if gather else srcs[a].at[_slot(me)], lands[a].at[_slot(me)], local.at[a])
                for a in range(n)]
        out = [copy(a, k, peer, False) for a in range(n) for k, peer in enumerate(peers)]
        for cp in mine + out:
            cp.start()
        for a in range(n):
            for k, peer in enumerate(peers):
                copy(a, k, peer, True).wait_recv()
        for cp in out:
            cp.wait_send()
        for cp in mine:
            cp.wait()

    launch()
    return [land[...] for land in lands]


def _adam_math(w, g, m, v):
    m_ = ADAM_B1 * m + (1.0 - ADAM_B1) * g
    v_ = ADAM_B2 * v + (1.0 - ADAM_B2) * jnp.square(g)
    m_hat = m_ / (1.0 - ADAM_B1 ** ADAM_STEP)
    v_hat = v_ / (1.0 - ADAM_B2 ** ADAM_STEP)
    return -ADAM_LR * (m_hat / (jnp.sqrt(v_hat) + ADAM_EPS) + ADAM_WD * w), m_, v_


def _reduce_adamw(parts, w, m, v, name):
    _, R, L = parts.shape
    tr = _pick(R, (256, 128, 64, 32, 16, 8))

    def body(p_ref, w_ref, m_ref, v_ref, g_ref, d_ref, nm_ref, nv_ref):
        g = p_ref[0].astype(f32)
        for i in range(1, N_DEV):
            g = g + p_ref[i].astype(f32)
        g_ref[...] = g
        d_ref[...], nm_ref[...], nv_ref[...] = _adam_math(w_ref[...], g, m_ref[...], v_ref[...])

    blk = pl.BlockSpec((tr, L), lambda i: (i, 0))
    sh = jax.ShapeDtypeStruct((R, L), f32)
    return _call(body, name, (R // tr,), [pl.BlockSpec((N_DEV, tr, L), lambda i: (0, i, 0)), blk, blk, blk], (blk,) * 4, (sh,) * 4,
                 sem=("parallel",))(parts, w, m, v)


SMALL = (("w_spatial", (512, 128), 0), ("norm1_g", (1, 1024), 512), ("mem_norm_g", (1, 1024), 520), ("norm2_g", (1, 1024), 528),
         ("final_g", (1, 1024), 536), ("lb_logits", (2, 512), 544), ("ln_v_g", (1, 512), 552), ("ln_v_b", (1, 512), 556),
         ("b_spatial", (4, 128), 560), ("hgrn_norm_g", (1, 128), 564), ("conv_b", (1, 2816), 565))
SMALL_USED, SMALL_ROWS = 587, 640


def _segments(shape, base):
    r, n = shape
    per = n // 128
    return [(base + i * per + j, i, slice(j * 128, (j + 1) * 128)) for i in range(r) for j in range(per)]


def _pack_small(gs):
    names = [n for n, _, _ in SMALL]

    def body(*refs):
        src, o_ref = dict(zip(names, refs[:-1])), refs[-1]
        o_ref[SMALL_USED:SMALL_ROWS, :] = jnp.zeros((SMALL_ROWS - SMALL_USED, 128), f32)
        for name, shape, base in SMALL:
            ref = src[name]
            if name == "w_spatial":
                o_ref[base:base + 512, :] = ref[...].reshape(512, 128)
            elif name == "b_spatial":
                o_ref[base:base + 4, :] = ref[0:4, :]
            elif name == "hgrn_norm_g":
                per_head = [ref[b, h] for b in range(ref.shape[0]) for h in range(N_HEAD)]
                o_ref[base:base + 1, :] = functools.reduce(lambda u, v_: u + v_, per_head)
            else:
                for row, i, sl in _segments(shape, base):
                    o_ref[row:row + 1, :] = ref[i:i + 1, sl]

    return pl.pallas_call(body, name="pack_small", out_shape=jax.ShapeDtypeStruct((SMALL_ROWS, 128), f32))(*[gs[n] for n in names])


def _small_update(gathered, w, m, v):
    names = [n for n, _, _ in SMALL]
    k = len(names)

    def body(*refs):
        p_ref = refs[0]
        ins = [dict(zip(names, refs[1 + i * k:1 + (i + 1) * k])) for i in range(3)]
        outs = [dict(zip(names, refs[1 + (3 + i) * k:1 + (4 + i) * k])) for i in range(4)]
        gsum = refs[-1]
        g = p_ref[0]
        for i in range(1, N_DEV):
            g = g + p_ref[i]
        gsum[...] = g
        for name, shape, base in SMALL:
            if name == "w_spatial":
                where = [(slice(base, base + 512), (slice(None), slice(None)))]
            else:
                where = [(slice(row, row + 1), (slice(i, i + 1), sl)) for row, i, sl in _segments(shape, base)]
            for rows, at in where:
                g_ = gsum[rows, :]
                d_, m_, v_ = _adam_math(ins[0][name][at], g_, ins[1][name][at], ins[2][name][at])
                for o, val in zip(outs, (g_, d_, m_, v_)):
                    o[name][at] = val

    args = [gathered] + [d[n] for d in (w, m, v) for n in names]
    out_shapes = [jax.ShapeDtypeStruct(shape, f32) for _ in range(4) for _, shape, _ in SMALL]
    outs = pl.pallas_call(body, name="small_update", out_shape=out_shapes, scratch_shapes=[pltpu.VMEM((SMALL_ROWS, 128), f32)])(*args)
    return [dict(zip(names, outs[i * k:(i + 1) * k])) for i in range(4)]


def _cols_full(g):
    return jnp.moveaxis(g, 0, -2).reshape(g.shape[1:-1] + (N_DEV * g.shape[-1],))


def _cols_parts(full):
    n = full.shape[-1] // N_DEV
    return jnp.moveaxis(full.reshape(full.shape[:-1] + (N_DEV, n)), -2, 0)


def kernel(x, mem, norm1_g, w_in, ln_v_g, ln_v_b, w_spatial, b_spatial, lb_logits, hgrn_norm_g, mem_norm_g, w_mem_kv, w_branch, w_out, norm2_g, w_up, conv_w, conv_b, w_down, final_g, loss_target, m_norm1_g, m_w_in, m_ln_v_g, m_ln_v_b, m_w_spatial, m_b_spatial, m_lb_logits, m_hgrn_norm_g, m_mem_norm_g, m_w_mem_kv, m_w_branch, m_w_out, m_norm2_g, m_w_up, m_conv_w, m_conv_b, m_w_down, m_final_g, v_norm1_g, v_w_in, v_ln_v_g, v_ln_v_b, v_w_spatial, v_b_spatial, v_lb_logits, v_hgrn_norm_g, v_mem_norm_g, v_w_mem_kv, v_w_branch, v_w_out, v_norm2_g, v_w_up, v_conv_w, v_conv_b, v_w_down, v_final_g):
    given = dict(locals())
    order = ("norm1_g", "w_in", "ln_v_g", "ln_v_b", "w_spatial", "b_spatial", "lb_logits", "hgrn_norm_g", "mem_norm_g",
             "w_mem_kv", "w_branch", "w_out", "norm2_g", "w_up", "conv_w", "conv_b", "w_down", "final_g")
    groups = dict(a=("w_in",), b=("w_mem_kv", "w_branch", "w_out"), c=("w_up", "conv_w", "w_down"))

    wire = {n: given[n][0].astype(f32 if n == "conv_w" else bf16) for ns in groups.values() for n in ns}
    g_in = _all_gather([wire["w_in"]], "gather_w_in")[0]
    late = groups["b"] + groups["c"]
    w_in_full = _cols_full(g_in)
    w_in_full, rest_wire = lax.optimization_barrier((w_in_full, [wire[n] for n in late]))
    rest = _seq_exchange(rest_wire, True, "gather_rest", 1)

    def late_b(after):
        _, (kv_, br_, out_) = lax.optimization_barrier((after, tuple(rest[0:3])))
        br_ = _cols_full(br_)
        return dict(w_mem_kv=kv_.reshape(D_MODEL, 2 * 512), w_branch=[br_[n] for n in range(3)], w_out=out_.reshape(D_MODEL, D_MODEL))

    def late_c(after):
        _, (up_, cw_, down_) = lax.optimization_barrier((after, tuple(rest[3:6])))
        return dict(w_up=_cols_full(up_), conv_w=_cols_full(cw_), w_down=down_.reshape(D_FF, D_MODEL))

    to_parts = dict(w_in=_cols_parts, w_up=_cols_parts, conv_w=_cols_parts,
                    w_branch=lambda g_: _cols_parts(jnp.stack(g_)).reshape(N_DEV, -1, 128),
                    w_mem_kv=lambda g_: g_.reshape(N_DEV, -1, 2 * 512), w_out=lambda g_: g_.reshape(N_DEV, -1, D_MODEL),
                    w_down=lambda g_: g_.reshape(N_DEV, -1, D_MODEL))
    scatters = {}

    def send(tag, grads_):
        scatters[tag], tok = _exchange_start([to_parts[n](grads_[n]) for n in groups[tag]], False, f"scatter_{tag}_start")
        return tok

    small_2d = lambda prefix: {n: given[prefix + n].reshape(shape) for n, shape, _ in SMALL}
    p = small_2d("")
    p["w_spatial"] = w_spatial[0]
    loss_part, grad_x, gs = _local_step(x, mem, loss_target, p, w_in_full, late_b, late_c, send)
    loss = lax.psum(loss_part[0, 0], ("x", "y", "c"))

    small_gather, _ = _exchange_start([_pack_small(gs)], True, "gather_small_start")

    grads, delta, new_m, new_v = {}, {}, {}, {}
    after = grad_x
    for tag in ("c", "b", "a"):
        recv = _exchange_wait(scatters[tag], tuple(range(len(groups[tag]))), after, f"scatter_{tag}_wait")
        for n, parts in zip(groups[tag], recv):
            two_d = (-1, given[n].shape[-1])
            res = _reduce_adamw(parts, *[given[pre + n].reshape(two_d) for pre in ("", "m_", "v_")], "adamw_" + n)
            grads[n], delta[n], new_m[n], new_v[n] = [r.reshape(given[n].shape) for r in res]
            after = res[0]

    gathered = _exchange_wait(small_gather, (0,), after, "gather_small_wait")[0]
    for dst, res in zip((grads, delta, new_m, new_v), _small_update(gathered, small_2d(""), small_2d("m_"), small_2d("v_"))):
        for n, _, _ in SMALL:
            dst[n] = res[n].reshape(given[n].shape)

    return (loss, grad_x, *[grads[n] for n in order], *[delta[n] for n in order], *[new_m[n] for n in order],
            *[new_v[n] for n in order])
```

```python
import functools

import jax
import jax.numpy as jnp
from jax import lax
from jax.experimental import pallas as pl
from jax.experimental.pallas import tpu as pltpu
from jax.experimental.pallas import tpu_sc as plsc

f32 = jnp.float32
bf16 = jnp.bfloat16

N_DEV = 8
D_MODEL = 1024
EPS = 1e-6
GM_CHUNK = 128
HG_CHUNK = 64
HEAD = 128
N_HEAD = 4
MEM_LEN = 256
D_FF = 2816
IN_WIDTH = 6656
C_ZU, C_HQ, C_HF, C_HI, C_HG, C_XQ, C_GL = 0, 1024, 1536, 2048, 2560, 3072, 3584
ADAM_LR, ADAM_B1, ADAM_B2, ADAM_EPS, ADAM_WD, ADAM_STEP = 0.001, 0.9, 0.999, 1e-08, 0.01, 10
VMEM_LIMIT = 56 * 1024 * 1024
MESH = pl.DeviceIdType.MESH


def _pick(n, cands):
    for c in cands:
        if n % c == 0:
            return c
    return n


def _call(body, name, grid, in_specs, out_specs, out_shape, scratch=(), sem=None, **cp):
    params = dict(vmem_limit_bytes=VMEM_LIMIT, **cp)
    if sem is not None:
        params["dimension_semantics"] = sem
    return pl.pallas_call(
        body, name=name, grid=grid, in_specs=in_specs, out_specs=out_specs, out_shape=out_shape,
        scratch_shapes=list(scratch), compiler_params=pltpu.CompilerParams(**params))


_DN = {"nn": (((1,), (0,)), ((), ())), "nt": (((1,), (1,)), ((), ())), "tn": (((0,), (0,)), ((), ()))}


def _raw_dot(a, b, mode):
    return lax.dot_general(a.astype(bf16), b.astype(bf16), _DN[mode], preferred_element_type=f32)


@jax.custom_vjp
def _dot_nn(a, b):
    return _raw_dot(a, b, "nn")


_dot_nn.defvjp(lambda a, b: (_raw_dot(a, b, "nn"), (a, b)),
               lambda r, g: (_raw_dot(g, r[1], "nt"), _raw_dot(r[0], g, "tn")))


@jax.custom_vjp
def _dot_nt(a, b):
    return _raw_dot(a, b, "nt")


_dot_nt.defvjp(lambda a, b: (_raw_dot(a, b, "nt"), (a, b)),
               lambda r, g: (_raw_dot(g, r[1], "nn"), _raw_dot(g, r[0], "tn")))


@jax.custom_vjp
def _dot_tn(a, b):
    return _raw_dot(a, b, "tn")


_dot_tn.defvjp(lambda a, b: (_raw_dot(a, b, "tn"), (a, b)),
               lambda r, g: (_raw_dot(r[1], g, "nt"), _raw_dot(r[0], g, "nn")))


def _tri(n, lower):
    r = lax.broadcasted_iota(jnp.int32, (n, n), 0)
    c = lax.broadcasted_iota(jnp.int32, (n, n), 1)
    return ((c <= r) if lower else (c >= r)).astype(f32)


def _sel_dot(sel, x, mode, x_first=False):
    hi = x.astype(bf16)
    rest = x - hi.astype(f32)
    mid = rest.astype(bf16)
    lo = (rest - mid.astype(f32)).astype(bf16)
    sel = sel.astype(bf16)
    dot = lambda piece: lax.dot_general(*((piece, sel) if x_first else (sel, piece)), _DN[mode], preferred_element_type=f32)
    return dot(hi) + dot(mid) + dot(lo)


def _egrad(fn, x, ct):
    return jax.vjp(fn, x)[1](ct)[0]


def _mm(a, b, mode, out_dtype, name, tm, tn, tk=None, residual=None, shard=None):
    if mode == "nn":
        (M, K), (_, N) = a.shape, b.shape
    elif mode == "nt":
        (M, K), (N, _) = a.shape, b.shape
    else:
        (K, M), (_, N) = a.shape, b.shape
    tm, tn = min(tm, M), min(tn, N)
    tk = K if tk is None else min(tk, K)
    assert M % tm == 0 and N % tn == 0 and K % tk == 0, (name, M, N, K, tm, tn, tk)
    nk = K // tk

    def body(*refs):
        acc_ref = refs[-1] if nk > 1 else None
        refs = refs[:-1] if nk > 1 else refs
        if residual is None:
            a_ref, b_ref, o_ref = refs
        else:
            a_ref, b_ref, r_ref, o_ref = refs

        def finish(r):
            if residual is not None:
                r = r + r_ref[...]
            if shard is None:
                o_ref[...] = r.astype(out_dtype)
            else:
                for s in range(tn // shard):
                    o_ref[s] = r[:, s * shard:(s + 1) * shard].astype(out_dtype)

        part = _raw_dot(a_ref[...], b_ref[...], mode)
        if nk == 1:
            finish(part)
            return
        k = pl.program_id(2)

        @pl.when(k == 0)
        def _():
            acc_ref[...] = part

        @pl.when((k > 0) & (k < nk - 1))
        def _():
            acc_ref[...] += part

        @pl.when(k == nk - 1)
        def _():
            finish(acc_ref[...] + part)

    a_spec = {"nn": pl.BlockSpec((tm, tk), lambda i, j, k: (i, k)),
              "nt": pl.BlockSpec((tm, tk), lambda i, j, k: (i, k)),
              "tn": pl.BlockSpec((tk, tm), lambda i, j, k: (k, i))}[mode]
    b_spec = {"nn": pl.BlockSpec((tk, tn), lambda i, j, k: (k, j)),
              "nt": pl.BlockSpec((tn, tk), lambda i, j, k: (j, k)),
              "tn": pl.BlockSpec((tk, tn), lambda i, j, k: (k, j))}[mode]
    o_spec = pl.BlockSpec((tm, tn), lambda i, j, k: (i, j))
    in_specs = [a_spec, b_spec] + ([o_spec] if residual is not None else [])
    args = (a, b) + ((residual,) if residual is not None else ())
    out_shape = jax.ShapeDtypeStruct((M, N), out_dtype)
    if shard is not None:
        assert residual is None and tn % shard == 0
        o_spec = pl.BlockSpec((tn // shard, tm, shard), lambda i, j, k: (j, i, 0))
        out_shape = jax.ShapeDtypeStruct((N // shard, M, shard), out_dtype)
    return _call(body, name, (M // tm, N // tn, nk), in_specs, o_spec, out_shape,
                 scratch=[pltpu.VMEM((tm, tn), f32)] if nk > 1 else [], sem=("parallel", "parallel", "arbitrary"))(*args)


def _rms_fwd(x, g, name):
    R, Dd = x.shape
    tr = _pick(R, (512, 256, 128))

    def body(x_ref, g_ref, o_ref):
        xf = x_ref[...]
        y = xf * lax.rsqrt(jnp.mean(xf * xf, axis=-1, keepdims=True) + EPS)
        o_ref[...] = (y * g_ref[...]).astype(bf16)

    return _call(body, name, (R // tr,), [pl.BlockSpec((tr, Dd), lambda i: (i, 0)), pl.BlockSpec((1, Dd), lambda i: (0, 0))],
                 pl.BlockSpec((tr, Dd), lambda i: (i, 0)), jax.ShapeDtypeStruct((R, Dd), bf16), sem=("parallel",))(x, g)


def _rms_bwd(x, g, dh, name, residual=None):
    R, Dd = x.shape
    tr = _pick(R, (512, 256, 128))

    def body(*refs):
        if residual is None:
            x_ref, g_ref, dh_ref, dx_ref, dg_ref = refs
        else:
            x_ref, g_ref, dh_ref, r_ref, dx_ref, dg_ref = refs
        xf = x_ref[...]
        rs = lax.rsqrt(jnp.mean(xf * xf, axis=-1, keepdims=True) + EPS)
        y = xf * rs
        dh_ = dh_ref[...].astype(f32)
        dy = dh_ * g_ref[...]
        dx = rs * (dy - y * jnp.mean(dy * y, axis=-1, keepdims=True))
        if residual is not None:
            dx = dx + r_ref[...]
        dx_ref[...] = dx

        @pl.when(pl.program_id(0) == 0)
        def _():
            dg_ref[...] = jnp.zeros_like(dg_ref)

        dg_ref[...] += jnp.sum(dh_ * y, axis=0, keepdims=True)

    row = pl.BlockSpec((tr, Dd), lambda i: (i, 0))
    vec = pl.BlockSpec((1, Dd), lambda i: (0, 0))
    in_specs = [row, vec, row] + ([row] if residual is not None else [])
    args = (x, g, dh) + ((residual,) if residual is not None else ())
    return _call(body, name, (R // tr,), in_specs, (row, vec),
                 (jax.ShapeDtypeStruct((R, Dd), f32), jax.ShapeDtypeStruct((1, Dd), f32)), sem=("arbitrary",))(*args)


def _final_loss(x2, g, target):
    R, Dd = x2.shape
    tr = _pick(R, (512, 256, 128))

    def body(x_ref, g_ref, t_ref, loss_ref, dx_ref, dg_ref):
        xf = x_ref[...]
        rs = lax.rsqrt(jnp.mean(xf * xf, axis=-1, keepdims=True) + EPS)
        y = xf * rs
        err = y * g_ref[...] - t_ref[...]
        dh_ = err * (1.0 / Dd)
        dy = dh_ * g_ref[...]
        dx_ref[...] = rs * (dy - y * jnp.mean(dy * y, axis=-1, keepdims=True))

        @pl.when(pl.program_id(0) == 0)
        def _():
            dg_ref[...] = jnp.zeros_like(dg_ref)
            loss_ref[...] = jnp.zeros_like(loss_ref)

        dg_ref[...] += jnp.sum(dh_ * y, axis=0, keepdims=True)
        part = jnp.sum(jnp.mean(err * err, axis=-1, keepdims=True), axis=0, keepdims=True)
        loss_ref[...] += 0.5 * part

    row = pl.BlockSpec((tr, Dd), lambda i: (i, 0))
    vec = pl.BlockSpec((1, Dd), lambda i: (0, 0))
    return _call(body, "final_loss", (R // tr,), [row, vec, row], (pl.BlockSpec((1, 128), lambda i: (0, 0)), row, vec),
                 (jax.ShapeDtypeStruct((1, 128), f32), jax.ShapeDtypeStruct((R, Dd), f32), jax.ShapeDtypeStruct((1, Dd), f32)),
                 sem=("arbitrary",))(x2, g, target)


def _gmlp_parts(zuv, ln_g, ln_b):
    zu, zv = zuv[:, :512], zuv[:, 512:]
    u = jax.nn.gelu(zu)
    v = jax.nn.gelu(zv)
    mu = jnp.mean(v, axis=-1, keepdims=True)
    rs = lax.rsqrt(jnp.mean(jnp.square(v - mu), axis=-1, keepdims=True) + EPS)
    xh = (v - mu) * rs
    return zu, zv, u, xh, rs, xh * ln_g + ln_b


def _gmlp_fwd(proj, ln_g, ln_b, w_s, b_st):
    T = proj.shape[0]

    def body(p_ref, g_ref, b_ref, w_ref, bs_ref, o_ref):
        _, _, u, _, _, vn = _gmlp_parts(p_ref[...].astype(f32), g_ref[...], b_ref[...])
        causal = _tri(GM_CHUNK, True) > 0
        for gi in range(N_HEAD):
            sl = slice(gi * HEAD, (gi + 1) * HEAD)
            w = jnp.where(causal, w_ref[gi], 0.0)
            mixed = _raw_dot(w, vn[:, sl], "nn") + bs_ref[:, gi:gi + 1]
            o_ref[:, sl] = (u[:, sl] * mixed).astype(bf16)

    vec = pl.BlockSpec((1, 512), lambda i: (0, 0))
    return _call(body, "gmlp_fwd", (T // GM_CHUNK,),
                 [pl.BlockSpec((GM_CHUNK, 1024), lambda i: (i, 0)), vec, vec,
                  pl.BlockSpec((N_HEAD, GM_CHUNK, GM_CHUNK), lambda i: (0, 0, 0)), pl.BlockSpec((GM_CHUNK, 128), lambda i: (0, 0))],
                 pl.BlockSpec((GM_CHUNK, 512), lambda i: (i, 0)), jax.ShapeDtypeStruct((T, 512), bf16), sem=("parallel",))(
        proj, ln_g, ln_b, w_s, b_st)


def _gmlp_bwd(proj, ln_g, ln_b, w_s, b_st, da):
    T = proj.shape[0]

    def body(p_ref, g_ref, b_ref, w_ref, bs_ref, da_ref, dp_ref, dg_ref, db_ref, dw_ref, dbs_ref):
        zu, zv, u, xh, rs, vn = _gmlp_parts(p_ref[...].astype(f32), g_ref[...], b_ref[...])
        causal = _tri(GM_CHUNK, True) > 0
        sub = lax.broadcasted_iota(jnp.int32, (8, GM_CHUNK), 0)
        ones = jnp.ones((8, HEAD), f32)
        dout = da_ref[...].astype(f32)

        @pl.when(pl.program_id(0) == 0)
        def _():
            for r in (dg_ref, db_ref, dw_ref, dbs_ref):
                r[...] = jnp.zeros_like(r)

        du, dvn, dbs = [], [], jnp.zeros((8, GM_CHUNK), f32)
        for gi in range(N_HEAD):
            sl = slice(gi * HEAD, (gi + 1) * HEAD)
            w = jnp.where(causal, w_ref[gi], 0.0)
            mixed = _raw_dot(w, vn[:, sl], "nn") + bs_ref[:, gi:gi + 1]
            du.append(dout[:, sl] * mixed)
            dm = dout[:, sl] * u[:, sl]
            row_sums = _sel_dot(ones, dm, "nt")
            dbs = dbs + jnp.where(sub == gi, row_sums, 0.0)
            dw_ref[gi] += jnp.where(causal, _raw_dot(dm, vn[:, sl], "nt"), 0.0)
            dvn.append(_raw_dot(w, dm, "tn"))
        dbs_ref[...] += dbs
        du = jnp.concatenate(du, axis=-1)
        dvn = jnp.concatenate(dvn, axis=-1)
        dg_ref[...] += jnp.sum(dvn * xh, axis=0, keepdims=True)
        db_ref[...] += jnp.sum(dvn, axis=0, keepdims=True)
        dxh = dvn * g_ref[...]
        dv = rs * (dxh - jnp.mean(dxh, axis=-1, keepdims=True) - xh * jnp.mean(dxh * xh, axis=-1, keepdims=True))
        dp_ref[:, :512] = _egrad(jax.nn.gelu, zu, du).astype(bf16)
        dp_ref[:, 512:] = _egrad(jax.nn.gelu, zv, dv).astype(bf16)

    vec = pl.BlockSpec((1, 512), lambda i: (0, 0))
    wsp = pl.BlockSpec((N_HEAD, GM_CHUNK, GM_CHUNK), lambda i: (0, 0, 0))
    return _call(body, "gmlp_bwd", (T // GM_CHUNK,),
                 [pl.BlockSpec((GM_CHUNK, 1024), lambda i: (i, 0)), vec, vec, wsp, pl.BlockSpec((GM_CHUNK, 128), lambda i: (0, 0)),
                  pl.BlockSpec((GM_CHUNK, 512), lambda i: (i, 0))],
                 (pl.BlockSpec((GM_CHUNK, 1024), lambda i: (i, 0)), vec, vec, wsp, pl.BlockSpec((8, GM_CHUNK), lambda i: (0, 0))),
                 (jax.ShapeDtypeStruct((T, 1024), bf16), jax.ShapeDtypeStruct((1, 512), f32), jax.ShapeDtypeStruct((1, 512), f32),
                  jax.ShapeDtypeStruct((N_HEAD, GM_CHUNK, GM_CHUNK), f32), jax.ShapeDtypeStruct((8, GM_CHUNK), f32)),
                 sem=("arbitrary",))(proj, ln_g, ln_b, w_s, b_st, da)


HG_SUB = 8
HG_NSUB = HG_CHUNK // HG_SUB


def _two_level_matrix():
    r = lax.broadcasted_iota(jnp.int32, (2 * HG_CHUNK, HG_CHUNK), 0)
    c = lax.broadcasted_iota(jnp.int32, (2 * HG_CHUNK, HG_CHUNK), 1)
    t = jnp.where(r < HG_CHUNK, r, r - HG_CHUNK)
    local = (r < HG_CHUNK) & (t // HG_SUB == c // HG_SUB) & (c <= t)
    before = (r >= HG_CHUNK) & (c < (t // HG_SUB) * HG_SUB)
    return (local | before).astype(f32)


def _two_level_sums(x):
    two = _sel_dot(_two_level_matrix(), x, "nn")
    return two[:HG_CHUNK], two[HG_CHUNK:]


@jax.custom_vjp
def _two_level_cumsum(x):
    return _two_level_sums(x)


_two_level_cumsum.defvjp(
    lambda x: (_two_level_sums(x), None),
    lambda _, g: (_sel_dot(_two_level_matrix(), jnp.concatenate(g, axis=0), "tn"),))


def _tile_matrix():
    s = lax.broadcasted_iota(jnp.int32, (HG_SUB, HG_CHUNK), 0)
    j = lax.broadcasted_iota(jnp.int32, (HG_SUB, HG_CHUNK), 1)
    return (j % HG_SUB == s).astype(f32)


@jax.custom_vjp
def _tile_lanes(x):
    return _sel_dot(_tile_matrix(), x, "nn", x_first=True)


_tile_lanes.defvjp(
    lambda x: (_sel_dot(_tile_matrix(), x, "nn", x_first=True), None),
    lambda _, g: (_sel_dot(_tile_matrix(), g, "nt", x_first=True),))


def _block_rows(x):
    k = x.shape[-1]
    return jnp.broadcast_to(x.reshape(HG_NSUB, 1, HG_SUB, k), (HG_NSUB, HG_SUB, HG_SUB, k)).reshape(HG_CHUNK, HG_SUB, k)


def _hgrn_chunk(st0, q_raw, f_raw, i_raw, g_raw, l0, l1, ng):
    C, SUB = HG_CHUNK, HG_SUB
    lb = jax.nn.sigmoid(l0 - l1)
    fg = lb + (1.0 - lb) * jax.nn.sigmoid(f_raw)
    kk = 1.0 - fg
    qf = jax.nn.silu(q_raw)
    al, base = _two_level_cumsum(jnp.log(fg))
    a = al + base
    row = lax.broadcasted_iota(jnp.int32, (C, HEAD), 0)
    a_last = jnp.sum(jnp.where(row == C - 1, a, 0.0), axis=0, keepdims=True)
    inter = _dot_nt(qf * jnp.exp(a), st0)
    qt = qf * jnp.exp(al)
    rb = lax.broadcasted_iota(jnp.int32, (C, C), 0) // SUB
    cb = lax.broadcasted_iota(jnp.int32, (C, C), 1) // SUB
    scores = jnp.zeros((C, C), f32)
    for i in range(1, HG_NSUB):
        base_i = jnp.sum(jnp.where(row == i * SUB, base, 0.0), axis=0, keepdims=True)
        kt = kk * jnp.exp(jnp.minimum(base_i - a, 0.0))
        scores = scores + jnp.where((rb == i) & (cb < i), _dot_nt(qt, kt), 0.0)
    t_i = lax.broadcasted_iota(jnp.int32, (C, SUB, HEAD), 0) % SUB
    s_i = lax.broadcasted_iota(jnp.int32, (C, SUB, HEAD), 1)
    decay = jnp.exp(jnp.where(s_i <= t_i, al[:, None, :] - _block_rows(al), -jnp.inf))
    diag = jnp.sum(qf[:, None, :] * decay * _block_rows(kk), axis=-1)
    scores = scores + jnp.where(rb == cb, _tile_lanes(diag), 0.0)
    o = inter + _dot_nn(scores, i_raw)
    st1 = jnp.exp(a_last) * st0 + _dot_tn(i_raw, kk * jnp.exp(a_last - a))
    on = o * lax.rsqrt(jnp.mean(o * o, axis=-1, keepdims=True) + EPS) * ng
    return st1, on * jax.nn.silu(g_raw)


def _hgrn_specs(S, Bl, rev):
    N = S // HG_CHUNK
    chunk = (lambda n: N - 1 - n) if rev else (lambda n: n)
    col = lambda c0: pl.BlockSpec((Bl, HG_CHUNK, 512), lambda n: (0, chunk(n), c0 // 512))
    st = pl.BlockSpec((Bl, N_HEAD, 1, HEAD, HEAD), lambda n: (0, 0, chunk(n), 0, 0))
    full = lambda *s: pl.BlockSpec(s, functools.partial(lambda n, nd: (0,) * nd, nd=len(s)))
    return N, col, st, full


def _hgrn_fwd(proj, lb_logits, ng, Bl, S):
    N, col, st, full = _hgrn_specs(S, Bl, False)

    def body(q_ref, f_ref, i_ref, g_ref, l_ref, ng_ref, o_ref, st_ref, state):
        @pl.when(pl.program_id(0) == 0)
        def _():
            state[...] = jnp.zeros_like(state)

        for b in range(Bl):
            for h in range(N_HEAD):
                sl = slice(h * HEAD, (h + 1) * HEAD)
                st0 = state[b, h]
                st_ref[b, h, 0] = st0
                st1, out = _hgrn_chunk(st0, *[r[b, :, sl].astype(f32) for r in (q_ref, f_ref, i_ref, g_ref)],
                                       l_ref[0:1, sl], l_ref[1:2, sl], ng_ref[...])
                state[b, h] = st1
                o_ref[b, :, sl] = out.astype(bf16)

    return _call(body, "hgrn_fwd", (N,), [col(C_HQ), col(C_HF), col(C_HI), col(C_HG), full(2, 512), full(1, HEAD)],
                 (col(0), st),
                 (jax.ShapeDtypeStruct((Bl, S, 512), bf16), jax.ShapeDtypeStruct((Bl, N_HEAD, N, HEAD, HEAD), f32)),
                 scratch=[pltpu.VMEM((Bl, N_HEAD, HEAD, HEAD), f32)], sem=("arbitrary",))(
        proj, proj, proj, proj, lb_logits, ng)


def _hgrn_bwd(proj, lb_logits, ng, states, db, Bl, S):
    N, col, st, full = _hgrn_specs(S, Bl, True)

    def body(q_ref, f_ref, i_ref, g_ref, l_ref, ng_ref, st_ref, db_ref,
             dq_ref, df_ref, di_ref, dg_ref, dl_ref, dng_ref, dstate):
        @pl.when(pl.program_id(0) == 0)
        def _():
            dstate[...] = jnp.zeros_like(dstate)
            dl_ref[...] = jnp.zeros_like(dl_ref)
            dng_ref[...] = jnp.zeros_like(dng_ref)

        for b in range(Bl):
            for h in range(N_HEAD):
                sl = slice(h * HEAD, (h + 1) * HEAD)
                _, vjp = jax.vjp(_hgrn_chunk, st_ref[b, h, 0], *[r[b, :, sl].astype(f32) for r in (q_ref, f_ref, i_ref, g_ref)],
                                 l_ref[0:1, sl], l_ref[1:2, sl], ng_ref[...])
                dst0, dq, df, di, dg, dl0, dl1, dng = vjp((dstate[b, h], db_ref[b, :, sl].astype(f32)))
                dstate[b, h] = dst0
                dq_ref[b, :, sl] = dq.astype(bf16)
                df_ref[b, :, sl] = df.astype(bf16)
                di_ref[b, :, sl] = di.astype(bf16)
                dg_ref[b, :, sl] = dg.astype(bf16)
                dl_ref[0:1, sl] += dl0
                dl_ref[1:2, sl] += dl1
                dng_ref[b, h] += dng

    return _call(body, "hgrn_bwd", (N,),
                 [col(C_HQ), col(C_HF), col(C_HI), col(C_HG), full(2, 512), full(1, HEAD), st, col(0)],
                 (*[col(0)] * 4, full(2, 512), full(Bl, N_HEAD, 1, HEAD)),
                 (*[jax.ShapeDtypeStruct((Bl, S, 512), bf16)] * 4, jax.ShapeDtypeStruct((2, 512), f32),
                  jax.ShapeDtypeStruct((Bl, N_HEAD, 1, HEAD), f32)),
                 scratch=[pltpu.VMEM((Bl, N_HEAD, HEAD, HEAD), f32)], sem=("arbitrary",))(
        proj, proj, proj, proj, lb_logits, ng, states, db)


def _attn_probs(q, k):
    s = _raw_dot(q, k, "nt") * (HEAD ** -0.5)
    e = jnp.exp(s - jnp.max(s, axis=-1, keepdims=True))
    return e / jnp.sum(e, axis=-1, keepdims=True)


def _attn_specs(S, tq):
    nq = S // tq
    q = pl.BlockSpec((tq, 512), lambda b, i: (b * nq + i, C_XQ // 512))
    kv = pl.BlockSpec((1, MEM_LEN, 1024), lambda b, i: (b, 0, 0))
    o = pl.BlockSpec((tq, 512), lambda b, i: (b * nq + i, 0))
    return nq, q, kv, o


def _attn_fwd(proj, kv, Bl, S):
    tq = _pick(S, (512, 256, 128))
    nq, qs, kvs, os_ = _attn_specs(S, tq)

    def body(q_ref, kv_ref, o_ref):
        for h in range(N_HEAD):
            sl = slice(h * HEAD, (h + 1) * HEAD)
            p = _attn_probs(q_ref[:, sl], kv_ref[0, :, sl])
            o_ref[:, sl] = _raw_dot(p, kv_ref[0, :, 512 + h * HEAD:512 + (h + 1) * HEAD], "nn").astype(bf16)

    return _call(body, "attn_fwd", (Bl, nq), [qs, kvs], os_, jax.ShapeDtypeStruct((Bl * S, 512), bf16),
                 sem=("parallel", "parallel"))(proj, kv)


def _attn_bwd(proj, kv, dc, Bl, S):
    tq = _pick(S, (512, 256, 128))
    nq, qs, kvs, os_ = _attn_specs(S, tq)

    def body(q_ref, kv_ref, do_ref, dq_ref, dkv_ref):
        @pl.when(pl.program_id(1) == 0)
        def _():
            dkv_ref[...] = jnp.zeros_like(dkv_ref)

        for h in range(N_HEAD):
            sl = slice(h * HEAD, (h + 1) * HEAD)
            vsl = slice(512 + h * HEAD, 512 + (h + 1) * HEAD)
            q, k, v, do = q_ref[:, sl], kv_ref[0, :, sl], kv_ref[0, :, vsl], do_ref[:, sl]
            p = _attn_probs(q, k)
            dkv_ref[0, :, vsl] += _raw_dot(p, do, "tn")
            dp = _raw_dot(do, v, "nt")
            ds = p * (dp - jnp.sum(dp * p, axis=-1, keepdims=True)) * (HEAD ** -0.5)
            dq_ref[:, sl] = _raw_dot(ds, k, "nn").astype(bf16)
            dkv_ref[0, :, sl] += _raw_dot(ds, q, "tn")

    return _call(body, "attn_bwd", (Bl, nq), [qs, kvs, os_], (os_, kvs),
                 (jax.ShapeDtypeStruct((Bl * S, 512), bf16), jax.ShapeDtypeStruct((Bl, MEM_LEN, 1024), f32)),
                 sem=("arbitrary", "arbitrary"))(proj, kv, dc)


def _merge_specs(tm, tn):
    br = pl.BlockSpec((tm, 512), lambda i, j: (i, 0))
    w = pl.BlockSpec((512, tn), lambda i, j: (0, j))
    gl = [pl.BlockSpec((tm, tn), functools.partial(lambda i, j, n: (i, (C_GL + n * D_MODEL) // tn + j), n=n)) for n in range(3)]
    return [br, br, br, w, w, w, *gl]


def _merge_fwd(branches, wb, proj):
    T = proj.shape[0]
    tm, tn = _pick(T, (1024, 512, 256, 128)), 512

    def body(a_ref, b_ref, c_ref, w0, w1, w2, g0, g1, g2, o_ref):
        acc = jnp.zeros((tm, tn), f32)
        for x_ref, w_ref, g_ref in ((a_ref, w0, g0), (b_ref, w1, g1), (c_ref, w2, g2)):
            acc = acc + jax.nn.sigmoid(g_ref[...].astype(f32)) * _raw_dot(x_ref[...], w_ref[...], "nn")
        o_ref[...] = acc.astype(bf16)

    return _call(body, "merge_fwd", (T // tm, D_MODEL // tn), _merge_specs(tm, tn), pl.BlockSpec((tm, tn), lambda i, j: (i, j)),
                 jax.ShapeDtypeStruct((T, D_MODEL), bf16), sem=("parallel", "parallel"))(*branches, *wb, proj, proj, proj)


def _merge_bwd(branches, wb, proj, dmerged):
    T = proj.shape[0]
    tm, tn = _pick(T, (1024, 512, 256, 128)), 512

    def body(a_ref, b_ref, c_ref, w0, w1, w2, g0, g1, g2, dm_ref, dgl_ref, d0, d1, d2):
        dm = dm_ref[...]
        for n, (x_ref, w_ref, g_ref, d_ref) in enumerate(((a_ref, w0, g0, d0), (b_ref, w1, g1, d1), (c_ref, w2, g2, d2))):
            up = _raw_dot(x_ref[...], w_ref[...], "nn")
            logits = g_ref[...].astype(f32)
            dgl_ref[n] = _egrad(jax.nn.sigmoid, logits, dm * up).astype(bf16)
            d_ref[...] = (dm * jax.nn.sigmoid(logits)).astype(bf16)

    blk = pl.BlockSpec((tm, tn), lambda i, j: (i, j))
    sh = jax.ShapeDtypeStruct((T, D_MODEL), bf16)
    outs = _call(body, "merge_bwd", (T // tm, D_MODEL // tn), [*_merge_specs(tm, tn), blk],
                 (pl.BlockSpec((3, tm, tn), lambda i, j: (0, i, j)), blk, blk, blk),
                 (jax.ShapeDtypeStruct((3, T, D_MODEL), bf16), sh, sh, sh),
                 sem=("parallel", "parallel"))(*branches, *wb, proj, proj, proj, dmerged)
    return outs[0], outs[1:]


CONV_TC = 256


def _shift_down(a, k):
    row = lax.broadcasted_iota(jnp.int32, a.shape, 0)
    return jnp.where(row >= k, pltpu.roll(a, k, 0), 0.0)


def _shift_up(a, k):
    n = a.shape[0]
    row = lax.broadcasted_iota(jnp.int32, a.shape, 0)
    return jnp.where(row < n - k, pltpu.roll(a, n - k, 0), 0.0)


def _conv_pre(a, cw, cb):
    return cb + cw[0:1] * _shift_down(a, 2) + cw[1:2] * _shift_down(a, 1) + cw[2:3] * a


def _conv_fwd(ab, cw, cb, Bl, S):
    nc = D_FF // CONV_TC

    def body(a_ref, b_ref, cw_ref, cb_ref, o_ref):
        ac = _conv_pre(a_ref[0].astype(f32), cw_ref[...], cb_ref[...])
        o_ref[0] = (jax.nn.silu(ac) * b_ref[0].astype(f32)).astype(bf16)

    return _call(body, "conv_fwd", (Bl, nc),
                 [pl.BlockSpec((1, S, CONV_TC), lambda b, c: (b, 0, c)), pl.BlockSpec((1, S, CONV_TC), lambda b, c: (b, 0, nc + c)),
                  pl.BlockSpec((3, CONV_TC), lambda b, c: (0, c)), pl.BlockSpec((1, CONV_TC), lambda b, c: (0, c))],
                 pl.BlockSpec((1, S, CONV_TC), lambda b, c: (b, 0, c)), jax.ShapeDtypeStruct((Bl, S, D_FF), bf16),
                 sem=("parallel", "parallel"))(ab, ab, cw, cb)


def _conv_bwd(ab, cw, cb, dact, Bl, S):
    nc = D_FF // CONV_TC

    def body(a_ref, b_ref, cw_ref, cb_ref, d_ref, da_ref, db_ref, dcw_ref, dcb_ref):
        @pl.when(pl.program_id(1) == 0)
        def _():
            dcw_ref[...] = jnp.zeros_like(dcw_ref)
            dcb_ref[...] = jnp.zeros_like(dcb_ref)

        a, cw = a_ref[0].astype(f32), cw_ref[...]
        ac = _conv_pre(a, cw, cb_ref[...])
        dact_ = d_ref[0].astype(f32)
        db_ref[0] = (dact_ * jax.nn.silu(ac)).astype(bf16)
        dac = _egrad(jax.nn.silu, ac, dact_ * b_ref[0].astype(f32))
        da_ref[0] = (cw[2:3] * dac + cw[1:2] * _shift_up(dac, 1) + cw[0:1] * _shift_up(dac, 2)).astype(bf16)
        dcw_ref[0:1, :] += jnp.sum(dac * _shift_down(a, 2), axis=0, keepdims=True)
        dcw_ref[1:2, :] += jnp.sum(dac * _shift_down(a, 1), axis=0, keepdims=True)
        dcw_ref[2:3, :] += jnp.sum(dac * a, axis=0, keepdims=True)
        dcb_ref[...] += jnp.sum(dac, axis=0, keepdims=True)

    seq = pl.BlockSpec((1, S, CONV_TC), lambda c, b: (b, 0, c))
    return _call(body, "conv_bwd", (nc, Bl),
                 [seq, pl.BlockSpec((1, S, CONV_TC), lambda c, b: (b, 0, nc + c)), pl.BlockSpec((3, CONV_TC), lambda c, b: (0, c)),
                  pl.BlockSpec((1, CONV_TC), lambda c, b: (0, c)), seq],
                 (seq, seq, pl.BlockSpec((3, CONV_TC), lambda c, b: (0, c)), pl.BlockSpec((1, CONV_TC), lambda c, b: (0, c))),
                 (jax.ShapeDtypeStruct((Bl, S, D_FF), bf16), jax.ShapeDtypeStruct((Bl, S, D_FF), bf16),
                  jax.ShapeDtypeStruct((3, D_FF), f32), jax.ShapeDtypeStruct((1, D_FF), f32)),
                 sem=("arbitrary", "arbitrary"))(ab, ab, cw, cb, dact)


def _local_step(x, mem, target, p, w_in, late_b, late_c, send):
    Bl, S, Dd = x.shape
    T = Bl * S
    x2d, t2d, mem2d = x.reshape(T, Dd), target.reshape(T, Dd), mem.reshape(Bl * MEM_LEN, Dd)
    b_st = jnp.pad(p["b_spatial"].T, ((0, 0), (0, 128 - N_HEAD)))
    lbl = p["lb_logits"]

    h = _rms_fwd(x2d, p["norm1_g"], "norm1_fwd")
    proj = _mm(h, w_in, "nn", bf16, "proj_fwd", 1024, 1664)
    a_out = _gmlp_fwd(proj, p["ln_v_g"], p["ln_v_b"], p["w_spatial"], b_st)
    proj3 = proj.reshape(Bl, S, IN_WIDTH)
    b_out, states = _hgrn_fwd(proj3, lbl, p["hgrn_norm_g"], Bl, S)
    b_out = b_out.reshape(T, 512)
    memn = _rms_fwd(mem2d, p["mem_norm_g"], "memnorm_fwd")
    w = late_b(b_out)
    wb = w["w_branch"]
    kv = _mm(memn, w["w_mem_kv"], "nn", f32, "kv_fwd", 512, 1024).reshape(Bl, MEM_LEN, 2 * 512)
    c_out = _attn_fwd(proj, kv, Bl, S)
    branches = (a_out, b_out, c_out)
    merged = _merge_fwd(branches, wb, proj)
    x1 = _mm(merged, w["w_out"], "nn", f32, "out_fwd", 1024, 1024, residual=x2d)
    h2 = _rms_fwd(x1, p["norm2_g"], "norm2_fwd")
    w.update(late_c(h2))
    ab = _mm(h2, w["w_up"], "nn", bf16, "up_fwd", 1024, 1408)
    act = _conv_fwd(ab.reshape(Bl, S, 2 * D_FF), w["conv_w"], p["conv_b"], Bl, S).reshape(T, D_FF)
    x2 = _mm(act, w["w_down"], "nn", f32, "down_fwd", 512, 1024, residual=x1)
    loss_part, dx2, g_final = _final_loss(x2, p["final_g"], t2d)

    g_w_down = _mm(act, dx2, "tn", bf16, "down_dw", 1408, 1024, 1024)
    dact = _mm(dx2, w["w_down"], "nt", bf16, "down_dx", 1024, 1408)
    da, db, g_conv_w, g_conv_b = _conv_bwd(ab.reshape(Bl, S, 2 * D_FF), w["conv_w"], p["conv_b"], dact.reshape(Bl, S, D_FF), Bl, S)
    dab = jnp.concatenate([da.reshape(T, D_FF), db.reshape(T, D_FF)], axis=-1)
    g_w_up = _mm(h2, dab, "tn", bf16, "up_dw", 512, 1408, 1024, shard=2 * D_FF // N_DEV)
    tok = send("c", dict(w_up=g_w_up, conv_w=g_conv_w, w_down=g_w_down))
    dh2 = _mm(dab, w["w_up"], "nt", f32, "up_dx", 1024, 1024, 1408)
    dx1, g_norm2 = _rms_bwd(x1, p["norm2_g"] + tok[0, 0], dh2, "norm2_bwd", residual=dx2)

    g_w_out = _mm(merged, dx1, "tn", bf16, "out_dw", 1024, 1024, 1024)
    dmerged = _mm(dx1, w["w_out"], "nt", f32, "out_dx", 1024, 1024)
    dgl, dup = _merge_bwd(branches, wb, proj, dmerged)
    g_w_branch = [_mm(branches[n], dup[n], "tn", bf16, f"branch_dw{n}", 512, 1024, 1024) for n in range(3)]
    dbr = [_mm(dup[n], wb[n], "nt", bf16, f"branch_dx{n}", 1024, 512) for n in range(3)]
    dxq, dkv = _attn_bwd(proj, kv, dbr[2], Bl, S)
    dkv = dkv.reshape(Bl * MEM_LEN, 2 * 512)
    g_w_kv = _mm(memn, dkv, "tn", bf16, "kv_dw", 1024, 1024, 512)
    tok = send("b", dict(w_mem_kv=g_w_kv, w_branch=g_w_branch, w_out=g_w_out))
    dmemn = _mm(dkv, w["w_mem_kv"], "nt", f32, "kv_dx", 512, 1024)
    _, g_mem_norm = _rms_bwd(mem2d, p["mem_norm_g"], dmemn, "memnorm_bwd")
    dzuv, g_ln_g, g_ln_b, g_w_sp, g_b_sp = _gmlp_bwd(proj, p["ln_v_g"] + tok[0, 0], p["ln_v_b"], p["w_spatial"], b_st, dbr[0])
    *dqfig, g_lbl, g_ng = _hgrn_bwd(proj3, lbl, p["hgrn_norm_g"], states, dbr[1].reshape(Bl, S, 512), Bl, S)
    dq, df, di, dg = [d.reshape(T, 512) for d in dqfig]
    dproj = jnp.concatenate([dzuv, dq, df, di, dg, dxq, dgl[0], dgl[1], dgl[2]], axis=-1)
    g_w_in = _mm(h, dproj, "tn", bf16, "proj_dw", 512, 1664, 1024, shard=IN_WIDTH // N_DEV)
    tok = send("a", dict(w_in=g_w_in))
    dh = _mm(dproj, w_in, "nt", f32, "proj_dx", 1024, 1024, 1664)
    dx, g_norm1 = _rms_bwd(x2d, p["norm1_g"] + tok[0, 0], dh, "norm1_bwd", residual=dx1)

    gs = dict(w_spatial=g_w_sp, norm1_g=g_norm1, mem_norm_g=g_mem_norm, norm2_g=g_norm2, final_g=g_final, lb_logits=g_lbl,
              ln_v_g=g_ln_g, ln_v_b=g_ln_b, b_spatial=g_b_sp, hgrn_norm_g=g_ng, conv_b=g_conv_b)
    return loss_part, dx.reshape(Bl, S, Dd), gs


def _coords():
    return lax.axis_index("x"), lax.axis_index("y"), lax.axis_index("c")


def _slot(dev):
    return 4 * dev[0] + 2 * dev[1] + dev[2]


def _comm_call(body, name, arrays, out_shapes, n_sem):
    n = len(arrays)
    hbm = pl.BlockSpec(memory_space=pl.ANY)
    return pl.pallas_call(
        body, name=name, out_shape=out_shapes, in_specs=[hbm] * n, out_specs=[hbm] * n,
        scratch_shapes=[pltpu.SemaphoreType.DMA((n_sem, n)), pltpu.SemaphoreType.DMA((n_sem, n)), pltpu.SemaphoreType.DMA((n,))])(*arrays)


def _all_gather(blocks, name):
    n = len(blocks)

    def body(*refs):
        x_refs, o_refs, (send_sems, recv_sems, local_sems) = refs[:n], refs[n:2 * n], refs[2 * n:]
        x, y, c = _coords()
        me, sibling = (x, y, c), (x, y, 1 - c)
        chips = [(1 - x, y), (x, 1 - y), (1 - x, 1 - y)]

        def copy(a, k, block_dev, to, from_input=False):
            dst = o_refs[a].at[_slot(block_dev)]
            return pltpu.make_async_remote_copy(src_ref=x_refs[a] if from_input else dst, dst_ref=dst, send_sem=send_sems.at[k, a],
                                                recv_sem=recv_sems.at[k, a], device_id=to, device_id_type=MESH)

        mine = [pltpu.make_async_copy(x_refs[a], o_refs[a].at[_slot(me)], local_sems.at[a]) for a in range(n)]
        first = [copy(a, 0, me, sibling, True) for a in range(n)]
        first += [copy(a, 1 + j, me, (*chip, c), True) for j, chip in enumerate(chips) for a in range(n)]
        for cp in mine + first:
            cp.start()
        passed = []
        for j, chip in enumerate(chips):
            for a in range(n):
                copy(a, 1 + j, (*chip, c), me).wait_recv()
                fwd = copy(a, 4 + j, (*chip, c), sibling)
                fwd.start()
                passed.append(fwd)
        for a in range(n):
            copy(a, 0, sibling, me).wait_recv()
        for j, chip in enumerate(chips):
            for a in range(n):
                copy(a, 4 + j, (*chip, 1 - c), me).wait_recv()
        for cp in first + passed:
            cp.wait_send()
        for cp in mine:
            cp.wait()

    return _comm_call(body, name, blocks, [jax.ShapeDtypeStruct((N_DEV,) + b.shape, b.dtype) for b in blocks], 7)


def _all_to_all(parts, name):
    n = len(parts)
    rel = [(0, 0, 1), (0, 1, 0), (0, 1, 1), (1, 0, 0), (1, 0, 1), (1, 1, 0), (1, 1, 1)]

    def body(*refs):
        x_refs, o_refs, (send_sems, recv_sems, local_sems) = refs[:n], refs[n:2 * n], refs[2 * n:]
        x, y, c = _coords()
        me = (x, y, c)
        peers = [(x ^ dx, y ^ dy, c ^ dc) for dx, dy, dc in rel]

        def copy(a, k, peer):
            return pltpu.make_async_remote_copy(src_ref=x_refs[a].at[_slot(peer)], dst_ref=o_refs[a].at[_slot(me)], send_sem=send_sems.at[k, a],
                                                recv_sem=recv_sems.at[k, a], device_id=peer, device_id_type=MESH)

        def arrival(a, k, peer):
            return pltpu.make_async_remote_copy(src_ref=x_refs[a].at[_slot(me)], dst_ref=o_refs[a].at[_slot(peer)], send_sem=send_sems.at[k, a],
                                                recv_sem=recv_sems.at[k, a], device_id=peer, device_id_type=MESH)

        mine = [pltpu.make_async_copy(x_refs[a].at[_slot(me)], o_refs[a].at[_slot(me)], local_sems.at[a]) for a in range(n)]
        sends = [copy(a, k, peer) for k, peer in enumerate(peers) for a in range(n)]
        for cp in mine + sends:
            cp.start()
        for k, peer in enumerate(peers):
            for a in range(n):
                arrival(a, k, peer).wait_recv()
        for cp in sends:
            cp.wait_send()
        for cp in mine:
            cp.wait()

    return _comm_call(body, name, parts, [jax.ShapeDtypeStruct(p.shape, p.dtype) for p in parts], 7)


_HBM = pl.BlockSpec(memory_space=pltpu.HBM)
_SEM = pl.BlockSpec(memory_space=pltpu.SEMAPHORE)
_REL = [(0, 0, 1), (0, 1, 0), (0, 1, 1), (1, 0, 0), (1, 0, 1), (1, 1, 0), (1, 1, 1)]


_LINK_ORDER = (3, 1, 5, 4, 2, 6, 0)
SEND_PIECES = 4


def _pieces(shape, dtype):
    rows = shape[0]
    unit = 1 if len(shape) > 2 else (16 if dtype == bf16 else 8)
    for n in (SEND_PIECES, 2):
        if rows % (n * unit) == 0:
            return [pl.ds(i * (rows // n), rows // n) for i in range(n)]
    return [pl.ds(0, rows)]


def _split_copies(gather, src, land, send, recv, pieces):
    x, y, c = _coords()
    me = (x, y, c)
    copies = []
    for a in range(len(src)):
        block = src[a].shape if gather else src[a].shape[1:]
        for rows in (_pieces(block, src[a].dtype) if pieces else [None]):
            for k in _LINK_ORDER:
                dx, dy, dc = _REL[k]
                peer = (x ^ dx, y ^ dy, c ^ dc)
                mine, there = (src[a] if gather else src[a].at[_slot(peer)]), land[a].at[_slot(me)]
                if rows is not None:
                    mine, there = mine.at[rows], there.at[rows]
                copies.append(pltpu.make_async_remote_copy(src_ref=mine, dst_ref=there, send_sem=send[a].at[k], recv_sem=recv[a].at[k],
                                                           device_id=peer, device_id_type=MESH))
    return me, copies


def _arrivals(gather, src, land, send, recv):
    x, y, c = _coords()
    out = []
    for a in range(len(src)):
        for k, (dx, dy, dc) in enumerate(_REL):
            peer = (x ^ dx, y ^ dy, c ^ dc)
            out.append(pltpu.make_async_remote_copy(src_ref=src[a] if gather else src[a].at[_slot(peer)], dst_ref=land[a].at[_slot(peer)],
                                                    send_sem=send[a].at[k], recv_sem=recv[a].at[k], device_id=peer, device_id_type=MESH))
    return out


def _exchange_start(arrays, gather, name, after=None):
    n = len(arrays)
    e = 0 if after is None else 1
    lands = [lax.empty(((N_DEV,) + a.shape) if gather else a.shape, a.dtype) for a in arrays]

    def body(*refs):
        src, land = refs[:n], refs[n:2 * n]
        refs = refs[2 * n + e:]
        send, recv, token, local_sems = refs[:n], refs[n:2 * n], refs[4 * n], refs[4 * n + 1]
        me, out = _split_copies(gather, src, land, send, recv, True)
        local = [pltpu.make_async_copy(src[a] if gather else src[a].at[_slot(me)], land[a].at[_slot(me)], local_sems.at[a])
                 for a in range(n)]
        for cp in local:
            cp.start()
        for cp in local:
            cp.wait()
        for cp in out:
            cp.start()
        token[...] = jnp.zeros_like(token)

    sems = [pltpu.SemaphoreType.DMA((7,)) for _ in range(2 * n)]
    outs = pl.pallas_call(
        body, name=name,
        out_shape=(*sems, *[pltpu.HBM(a.shape, a.dtype) for a in arrays], *[pltpu.HBM(l.shape, l.dtype) for l in lands],
                   jax.ShapeDtypeStruct((8, 128), f32)),
        in_specs=[_HBM] * (2 * n) + [pl.BlockSpec(memory_space=pl.ANY)] * e,
        out_specs=(*[_SEM] * (2 * n), *[_HBM] * (2 * n), pl.BlockSpec(memory_space=pltpu.VMEM)),
        input_output_aliases={i: 2 * n + i for i in range(2 * n)},
        scratch_shapes=[pltpu.SemaphoreType.DMA((n,))],
        compiler_params=pltpu.CompilerParams(has_side_effects=pltpu.SideEffectType.DATAFLOW_SIDE_EFFECTING))(
        *[pltpu.with_memory_space_constraint(a, pltpu.HBM) for a in arrays],
        *[pltpu.with_memory_space_constraint(l, pltpu.HBM) for l in lands], *([after] if e else []))
    return (gather, n, outs[:4 * n]), outs[4 * n]


def _exchange_wait(handle, which, after, name):
    gather, n_all, vals = handle
    send_v, recv_v, src_v, land_v = [[vals[g * n_all + i] for i in which] for g in range(4)]
    n = len(which)

    def body(*refs):
        src, land, send, recv = refs[:n], refs[n:2 * n], refs[2 * n:3 * n], refs[3 * n:4 * n]
        for cp in _split_copies(gather, src, land, send, recv, False)[1]:
            cp.wait_send()
        for cp in _arrivals(gather, src, land, send, recv):
            cp.wait_recv()

    outs = pl.pallas_call(
        body, name=name,
        out_shape=(*[pltpu.HBM(a.shape, a.dtype) for a in src_v], *[pltpu.HBM(l.shape, l.dtype) for l in land_v]),
        in_specs=[*[_HBM] * (2 * n), *[_SEM] * (2 * n), pl.BlockSpec(memory_space=pl.ANY)], out_specs=[_HBM] * (2 * n),
        input_output_aliases={i: i for i in range(2 * n)},
        compiler_params=pltpu.CompilerParams(has_side_effects=pltpu.SideEffectType.DATAFLOW_SIDE_EFFECTING))(
        *src_v, *land_v, *send_v, *recv_v, after)
    return outs[n:]


def _seq_exchange(arrays, gather, name, collective_id):
    n = len(arrays)
    hbm = pltpu.MemorySpace.HBM
    srcs = [jax.new_ref(a, memory_space=hbm) for a in arrays]
    lands = [jax.empty_ref(jax.ShapeDtypeStruct(((N_DEV,) + a.shape) if gather else a.shape, a.dtype), memory_space=hbm) for a in arrays]

    @pl.kernel(mesh=plsc.ScalarSubcoreMesh(axis_name="sequencer", num_cores=1), name=name,
               scratch_types=(pltpu.SemaphoreType.DMA((7, n)), pltpu.SemaphoreType.DMA((7, n)), pltpu.SemaphoreType.DMA((n,))),
               compiler_params=pltpu.CompilerParams(collective_id=collective_id))
    def launch(send, recv, local):
        x, y, c = _coords()
        me = (x, y, c)
        peers = [(x ^ dx, y ^ dy, c ^ dc) for dx, dy, dc in _REL]
        barrier = pltpu.get_barrier_semaphore()
        for peer in peers:
            pl.semaphore_signal(barrier, inc=1, device_id=peer, device_id_type=MESH)
        pl.semaphore_wait(barrier, len(peers))

        def copy(a, k, peer, arrival):
            return pltpu.make_async_remote_copy(
                src_ref=srcs[a] if gather else srcs[a].at[_slot(peer)], dst_ref=lands[a].at[_slot(peer if arrival else me)],
                send_sem=send.at[k, a], recv_sem=recv.at[k, a], device_id=peer, device_id_type=MESH)

        mine = [pltpu.make_async_copy(srcs[a] if gather else srcs[a].at[_slot(me)], lands[a].at[_slot(me)], local.at[a])
                for a in range(n)]
        out = [copy(a, k, peer, False) for a in range(n) for k, peer in enumerate(peers)]
        for cp in mine + out:
            cp.start()
        for a in range(n):
            for k, peer in enumerate(peers):
                copy(a, k, peer, True).wait_recv()
        for cp in out:
            cp.wait_send()
        for cp in mine:
            cp.wait()

    launch()
    return [land[...] for land in lands]


def _adam_math(w, g, m, v):
    m_ = ADAM_B1 * m + (1.0 - ADAM_B1) * g
    v_ = ADAM_B2 * v + (1.0 - ADAM_B2) * jnp.square(g)
    m_hat = m_ / (1.0 - ADAM_B1 ** ADAM_STEP)
    v_hat = v_ / (1.0 - ADAM_B2 ** ADAM_STEP)
    return -ADAM_LR * (m_hat / (jnp.sqrt(v_hat) + ADAM_EPS) + ADAM_WD * w), m_, v_


def _reduce_adamw(parts, w, m, v, name):
    _, R, L = parts.shape
    tr = _pick(R, (256, 128, 64, 32, 16, 8))

    def body(p_ref, w_ref, m_ref, v_ref, g_ref, d_ref, nm_ref, nv_ref):
        g = p_ref[0].astype(f32)
        for i in range(1, N_DEV):
            g = g + p_ref[i].astype(f32)
        g_ref[...] = g
        d_ref[...], nm_ref[...], nv_ref[...] = _adam_math(w_ref[...], g, m_ref[...], v_ref[...])

    blk = pl.BlockSpec((tr, L), lambda i: (i, 0))
    sh = jax.ShapeDtypeStruct((R, L), f32)
    return _call(body, name, (R // tr,), [pl.BlockSpec((N_DEV, tr, L), lambda i: (0, i, 0)), blk, blk, blk], (blk,) * 4, (sh,) * 4,
                 sem=("parallel",))(parts, w, m, v)


SMALL = (("w_spatial", (512, 128), 0), ("norm1_g", (1, 1024), 512), ("mem_norm_g", (1, 1024), 520), ("norm2_g", (1, 1024), 528),
         ("final_g", (1, 1024), 536), ("lb_logits", (2, 512), 544), ("ln_v_g", (1, 512), 552), ("ln_v_b", (1, 512), 556),
         ("b_spatial", (4, 128), 560), ("hgrn_norm_g", (1, 128), 564), ("conv_b", (1, 2816), 565))
SMALL_USED, SMALL_ROWS = 587, 640


def _segments(shape, base):
    r, n = shape
    per = n // 128
    return [(base + i * per + j, i, slice(j * 128, (j + 1) * 128)) for i in range(r) for j in range(per)]


def _pack_small(gs):
    names = [n for n, _, _ in SMALL]

    def body(*refs):
        src, o_ref = dict(zip(names, refs[:-1])), refs[-1]
        o_ref[SMALL_USED:SMALL_ROWS, :] = jnp.zeros((SMALL_ROWS - SMALL_USED, 128), f32)
        for name, shape, base in SMALL:
            ref = src[name]
            if name == "w_spatial":
                o_ref[base:base + 512, :] = ref[...].reshape(512, 128)
            elif name == "b_spatial":
                o_ref[base:base + 4, :] = ref[0:4, :]
            elif name == "hgrn_norm_g":
                per_head = [ref[b, h] for b in range(ref.shape[0]) for h in range(N_HEAD)]
                o_ref[base:base + 1, :] = functools.reduce(lambda u, v_: u + v_, per_head)
            else:
                for row, i, sl in _segments(shape, base):
                    o_ref[row:row + 1, :] = ref[i:i + 1, sl]

    return pl.pallas_call(body, name="pack_small", out_shape=jax.ShapeDtypeStruct((SMALL_ROWS, 128), f32))(*[gs[n] for n in names])


def _small_update(gathered, w, m, v):
    names = [n for n, _, _ in SMALL]
    k = len(names)

    def body(*refs):
        p_ref = refs[0]
        ins = [dict(zip(names, refs[1 + i * k:1 + (i + 1) * k])) for i in range(3)]
        outs = [dict(zip(names, refs[1 + (3 + i) * k:1 + (4 + i) * k])) for i in range(4)]
        gsum = refs[-1]
        g = p_ref[0]
        for i in range(1, N_DEV):
            g = g + p_ref[i]
        gsum[...] = g
        for name, shape, base in SMALL:
            if name == "w_spatial":
                where = [(slice(base, base + 512), (slice(None), slice(None)))]
            else:
                where = [(slice(row, row + 1), (slice(i, i + 1), sl)) for row, i, sl in _segments(shape, base)]
            for rows, at in where:
                g_ = gsum[rows, :]
                d_, m_, v_ = _adam_math(ins[0][name][at], g_, ins[1][name][at], ins[2][name][at])
                for o, val in zip(outs, (g_, d_, m_, v_)):
                    o[name][at] = val

    args = [gathered] + [d[n] for d in (w, m, v) for n in names]
    out_shapes = [jax.ShapeDtypeStruct(shape, f32) for _ in range(4) for _, shape, _ in SMALL]
    outs = pl.pallas_call(body, name="small_update", out_shape=out_shapes, scratch_shapes=[pltpu.VMEM((SMALL_ROWS, 128), f32)])(*args)
    return [dict(zip(names, outs[i * k:(i + 1) * k])) for i in range(4)]


def _cols_full(g):
    return jnp.moveaxis(g, 0, -2).reshape(g.shape[1:-1] + (N_DEV * g.shape[-1],))


def _cols_parts(full):
    n = full.shape[-1] // N_DEV
    return jnp.moveaxis(full.reshape(full.shape[:-1] + (N_DEV, n)), -2, 0)


def kernel(x, mem, norm1_g, w_in, ln_v_g, ln_v_b, w_spatial, b_spatial, lb_logits, hgrn_norm_g, mem_norm_g, w_mem_kv, w_branch, w_out, norm2_g, w_up, conv_w, conv_b, w_down, final_g, loss_target, m_norm1_g, m_w_in, m_ln_v_g, m_ln_v_b, m_w_spatial, m_b_spatial, m_lb_logits, m_hgrn_norm_g, m_mem_norm_g, m_w_mem_kv, m_w_branch, m_w_out, m_norm2_g, m_w_up, m_conv_w, m_conv_b, m_w_down, m_final_g, v_norm1_g, v_w_in, v_ln_v_g, v_ln_v_b, v_w_spatial, v_b_spatial, v_lb_logits, v_hgrn_norm_g, v_mem_norm_g, v_w_mem_kv, v_w_branch, v_w_out, v_norm2_g, v_w_up, v_conv_w, v_conv_b, v_w_down, v_final_g):
    given = dict(locals())
    order = ("norm1_g", "w_in", "ln_v_g", "ln_v_b", "w_spatial", "b_spatial", "lb_logits", "hgrn_norm_g", "mem_norm_g",
             "w_mem_kv", "w_branch", "w_out", "norm2_g", "w_up", "conv_w", "conv_b", "w_down", "final_g")
    groups = dict(a=("w_in",), b=("w_mem_kv", "w_branch", "w_out"), c=("w_up", "conv_w", "w_down"))

    wire = {n: given[n][0].astype(f32 if n == "conv_w" else bf16) for ns in groups.values() for n in ns}
    g_in = _all_gather([wire["w_in"]], "gather_w_in")[0]
    late = groups["b"] + groups["c"]
    w_in_full = _cols_full(g_in)
    w_in_full, rest_wire = lax.optimization_barrier((w_in_full, [wire[n] for n in late]))
    rest = _seq_exchange(rest_wire, True, "gather_rest", 1)

    def late_b(after):
        _, (kv_, br_, out_) = lax.optimization_barrier((after, tuple(rest[0:3])))
        br_ = _cols_full(br_)
        return dict(w_mem_kv=kv_.reshape(D_MODEL, 2 * 512), w_branch=[br_[n] for n in range(3)], w_out=out_.reshape(D_MODEL, D_MODEL))

    def late_c(after):
        _, (up_, cw_, down_) = lax.optimization_barrier((after, tuple(rest[3:6])))
        return dict(w_up=_cols_full(up_), conv_w=_cols_full(cw_), w_down=down_.reshape(D_FF, D_MODEL))

    to_parts = dict(w_in=lambda g_: g_, w_up=lambda g_: g_, conv_w=_cols_parts,
                    w_branch=lambda g_: _cols_parts(jnp.stack(g_)).reshape(N_DEV, -1, 128),
                    w_mem_kv=lambda g_: g_.reshape(N_DEV, -1, 2 * 512), w_out=lambda g_: g_.reshape(N_DEV, -1, D_MODEL),
                    w_down=lambda g_: g_.reshape(N_DEV, -1, D_MODEL))
    scatters = {}

    def send(tag, grads_):
        parts = [to_parts[n](grads_[n]) for n in groups[tag]]
        if tag == "a":
            scatters[tag] = _seq_exchange(parts, False, "scatter_a", 2)
            return jnp.zeros((8, 128), f32)
        scatters[tag], tok = _exchange_start(parts, False, f"scatter_{tag}_start")
        return tok

    small_2d = lambda prefix: {n: given[prefix + n].reshape(shape) for n, shape, _ in SMALL}
    p = small_2d("")
    p["w_spatial"] = w_spatial[0]
    loss_part, grad_x, gs = _local_step(x, mem, loss_target, p, w_in_full, late_b, late_c, send)
    loss = lax.psum(loss_part[0, 0], ("x", "y", "c"))

    gathered = _seq_exchange([_pack_small(gs)], True, "gather_small", 3)[0]

    grads, delta, new_m, new_v = {}, {}, {}, {}
    after = grad_x
    for tag in ("c", "b", "a"):
        recv = scatters[tag]
        if tag != "a":
            recv = _exchange_wait(recv, tuple(range(len(groups[tag]))), after, f"scatter_{tag}_wait")
        for n, parts in zip(groups[tag], recv):
            two_d = (-1, given[n].shape[-1])
            res = _reduce_adamw(parts, *[given[pre + n].reshape(two_d) for pre in ("", "m_", "v_")], "adamw_" + n)
            grads[n], delta[n], new_m[n], new_v[n] = [r.reshape(given[n].shape) for r in res]
            after = res[0]

    for dst, res in zip((grads, delta, new_m, new_v), _small_update(gathered, small_2d(""), small_2d("m_"), small_2d("v_"))):
        for n, _, _ in SMALL:
            dst[n] = res[n].reshape(given[n].shape)

    return (loss, grad_x, *[grads[n] for n in order], *[delta[n] for n in order], *[new_m[n] for n in order],
            *[new_v[n] for n in order])
```

```python
import functools

import jax
import jax.numpy as jnp
from jax import lax
from jax.experimental import pallas as pl
from jax.experimental.pallas import tpu as pltpu
from jax.experimental.pallas import tpu_sc as plsc

f32 = jnp.float32
bf16 = jnp.bfloat16

N_DEV = 8
D_MODEL = 1024
EPS = 1e-6
GM_CHUNK = 128
HG_CHUNK = 64
HEAD = 128
N_HEAD = 4
MEM_LEN = 256
D_FF = 2816
IN_WIDTH = 6656
C_ZU, C_HQ, C_HF, C_HI, C_HG, C_XQ, C_GL = 0, 1024, 1536, 2048, 2560, 3072, 3584
ADAM_LR, ADAM_B1, ADAM_B2, ADAM_EPS, ADAM_WD, ADAM_STEP = 0.001, 0.9, 0.999, 1e-08, 0.01, 10
VMEM_LIMIT = 56 * 1024 * 1024
MESH = pl.DeviceIdType.MESH


def _pick(n, cands):
    for c in cands:
        if n % c == 0:
            return c
    return n


def _call(body, name, grid, in_specs, out_specs, out_shape, scratch=(), sem=None, **cp):
    params = dict(vmem_limit_bytes=VMEM_LIMIT, **cp)
    if sem is not None:
        params["dimension_semantics"] = sem
    return pl.pallas_call(
        body, name=name, grid=grid, in_specs=in_specs, out_specs=out_specs, out_shape=out_shape,
        scratch_shapes=list(scratch), compiler_params=pltpu.CompilerParams(**params))


_DN = {"nn": (((1,), (0,)), ((), ())), "nt": (((1,), (1,)), ((), ())), "tn": (((0,), (0,)), ((), ()))}


def _raw_dot(a, b, mode):
    return lax.dot_general(a.astype(bf16), b.astype(bf16), _DN[mode], preferred_element_type=f32)


@jax.custom_vjp
def _dot_nn(a, b):
    return _raw_dot(a, b, "nn")


_dot_nn.defvjp(lambda a, b: (_raw_dot(a, b, "nn"), (a, b)),
               lambda r, g: (_raw_dot(g, r[1], "nt"), _raw_dot(r[0], g, "tn")))


@jax.custom_vjp
def _dot_nt(a, b):
    return _raw_dot(a, b, "nt")


_dot_nt.defvjp(lambda a, b: (_raw_dot(a, b, "nt"), (a, b)),
               lambda r, g: (_raw_dot(g, r[1], "nn"), _raw_dot(g, r[0], "tn")))


@jax.custom_vjp
def _dot_tn(a, b):
    return _raw_dot(a, b, "tn")


_dot_tn.defvjp(lambda a, b: (_raw_dot(a, b, "tn"), (a, b)),
               lambda r, g: (_raw_dot(r[1], g, "nt"), _raw_dot(r[0], g, "nn")))


def _tri(n, lower):
    r = lax.broadcasted_iota(jnp.int32, (n, n), 0)
    c = lax.broadcasted_iota(jnp.int32, (n, n), 1)
    return ((c <= r) if lower else (c >= r)).astype(f32)


def _sel_dot(sel, x, mode, x_first=False):
    hi = x.astype(bf16)
    rest = x - hi.astype(f32)
    mid = rest.astype(bf16)
    lo = (rest - mid.astype(f32)).astype(bf16)
    sel = sel.astype(bf16)
    dot = lambda piece: lax.dot_general(*((piece, sel) if x_first else (sel, piece)), _DN[mode], preferred_element_type=f32)
    return dot(hi) + dot(mid) + dot(lo)


def _egrad(fn, x, ct):
    return jax.vjp(fn, x)[1](ct)[0]


def _mm(a, b, mode, out_dtype, name, tm, tn, tk=None, residual=None, shard=None):
    if mode == "nn":
        (M, K), (_, N) = a.shape, b.shape
    elif mode == "nt":
        (M, K), (N, _) = a.shape, b.shape
    else:
        (K, M), (_, N) = a.shape, b.shape
    tm, tn = min(tm, M), min(tn, N)
    tk = K if tk is None else min(tk, K)
    assert M % tm == 0 and N % tn == 0 and K % tk == 0, (name, M, N, K, tm, tn, tk)
    nk = K // tk

    def body(*refs):
        acc_ref = refs[-1] if nk > 1 else None
        refs = refs[:-1] if nk > 1 else refs
        if residual is None:
            a_ref, b_ref, o_ref = refs
        else:
            a_ref, b_ref, r_ref, o_ref = refs

        def finish(r):
            if residual is not None:
                r = r + r_ref[...]
            if shard is None:
                o_ref[...] = r.astype(out_dtype)
            else:
                for s in range(tn // shard):
                    o_ref[s] = r[:, s * shard:(s + 1) * shard].astype(out_dtype)

        part = _raw_dot(a_ref[...], b_ref[...], mode)
        if nk == 1:
            finish(part)
            return
        k = pl.program_id(2)

        @pl.when(k == 0)
        def _():
            acc_ref[...] = part

        @pl.when((k > 0) & (k < nk - 1))
        def _():
            acc_ref[...] += part

        @pl.when(k == nk - 1)
        def _():
            finish(acc_ref[...] + part)

    a_spec = {"nn": pl.BlockSpec((tm, tk), lambda i, j, k: (i, k)),
              "nt": pl.BlockSpec((tm, tk), lambda i, j, k: (i, k)),
              "tn": pl.BlockSpec((tk, tm), lambda i, j, k: (k, i))}[mode]
    b_spec = {"nn": pl.BlockSpec((tk, tn), lambda i, j, k: (k, j)),
              "nt": pl.BlockSpec((tn, tk), lambda i, j, k: (j, k)),
              "tn": pl.BlockSpec((tk, tn), lambda i, j, k: (k, j))}[mode]
    o_spec = pl.BlockSpec((tm, tn), lambda i, j, k: (i, j))
    in_specs = [a_spec, b_spec] + ([o_spec] if residual is not None else [])
    args = (a, b) + ((residual,) if residual is not None else ())
    out_shape = jax.ShapeDtypeStruct((M, N), out_dtype)
    if shard is not None:
        assert residual is None and tn % shard == 0
        o_spec = pl.BlockSpec((tn // shard, tm, shard), lambda i, j, k: (j, i, 0))
        out_shape = jax.ShapeDtypeStruct((N // shard, M, shard), out_dtype)
    return _call(body, name, (M // tm, N // tn, nk), in_specs, o_spec, out_shape,
                 scratch=[pltpu.VMEM((tm, tn), f32)] if nk > 1 else [], sem=("parallel", "parallel", "arbitrary"))(*args)


def _rms_fwd(x, g, name):
    R, Dd = x.shape
    tr = _pick(R, (512, 256, 128))

    def body(x_ref, g_ref, o_ref):
        xf = x_ref[...]
        y = xf * lax.rsqrt(jnp.mean(xf * xf, axis=-1, keepdims=True) + EPS)
        o_ref[...] = (y * g_ref[...]).astype(bf16)

    return _call(body, name, (R // tr,), [pl.BlockSpec((tr, Dd), lambda i: (i, 0)), pl.BlockSpec((1, Dd), lambda i: (0, 0))],
                 pl.BlockSpec((tr, Dd), lambda i: (i, 0)), jax.ShapeDtypeStruct((R, Dd), bf16), sem=("parallel",))(x, g)


def _rms_bwd(x, g, dh, name, residual=None):
    R, Dd = x.shape
    tr = _pick(R, (512, 256, 128))

    def body(*refs):
        if residual is None:
            x_ref, g_ref, dh_ref, dx_ref, dg_ref = refs
        else:
            x_ref, g_ref, dh_ref, r_ref, dx_ref, dg_ref = refs
        xf = x_ref[...]
        rs = lax.rsqrt(jnp.mean(xf * xf, axis=-1, keepdims=True) + EPS)
        y = xf * rs
        dh_ = dh_ref[...].astype(f32)
        dy = dh_ * g_ref[...]
        dx = rs * (dy - y * jnp.mean(dy * y, axis=-1, keepdims=True))
        if residual is not None:
            dx = dx + r_ref[...]
        dx_ref[...] = dx

        @pl.when(pl.program_id(0) == 0)
        def _():
            dg_ref[...] = jnp.zeros_like(dg_ref)

        dg_ref[...] += jnp.sum(dh_ * y, axis=0, keepdims=True)

    row = pl.BlockSpec((tr, Dd), lambda i: (i, 0))
    vec = pl.BlockSpec((1, Dd), lambda i: (0, 0))
    in_specs = [row, vec, row] + ([row] if residual is not None else [])
    args = (x, g, dh) + ((residual,) if residual is not None else ())
    return _call(body, name, (R // tr,), in_specs, (row, vec),
                 (jax.ShapeDtypeStruct((R, Dd), f32), jax.ShapeDtypeStruct((1, Dd), f32)), sem=("arbitrary",))(*args)


def _final_loss(x2, g, target):
    R, Dd = x2.shape
    tr = _pick(R, (512, 256, 128))

    def body(x_ref, g_ref, t_ref, loss_ref, dx_ref, dg_ref):
        xf = x_ref[...]
        rs = lax.rsqrt(jnp.mean(xf * xf, axis=-1, keepdims=True) + EPS)
        y = xf * rs
        err = y * g_ref[...] - t_ref[...]
        dh_ = err * (1.0 / Dd)
        dy = dh_ * g_ref[...]
        dx_ref[...] = rs * (dy - y * jnp.mean(dy * y, axis=-1, keepdims=True))

        @pl.when(pl.program_id(0) == 0)
        def _():
            dg_ref[...] = jnp.zeros_like(dg_ref)
            loss_ref[...] = jnp.zeros_like(loss_ref)

        dg_ref[...] += jnp.sum(dh_ * y, axis=0, keepdims=True)
        part = jnp.sum(jnp.mean(err * err, axis=-1, keepdims=True), axis=0, keepdims=True)
        loss_ref[...] += 0.5 * part

    row = pl.BlockSpec((tr, Dd), lambda i: (i, 0))
    vec = pl.BlockSpec((1, Dd), lambda i: (0, 0))
    return _call(body, "final_loss", (R // tr,), [row, vec, row], (pl.BlockSpec((1, 128), lambda i: (0, 0)), row, vec),
                 (jax.ShapeDtypeStruct((1, 128), f32), jax.ShapeDtypeStruct((R, Dd), f32), jax.ShapeDtypeStruct((1, Dd), f32)),
                 sem=("arbitrary",))(x2, g, target)


def _gmlp_parts(zuv, ln_g, ln_b):
    zu, zv = zuv[:, :512], zuv[:, 512:]
    u = jax.nn.gelu(zu)
    v = jax.nn.gelu(zv)
    mu = jnp.mean(v, axis=-1, keepdims=True)
    rs = lax.rsqrt(jnp.mean(jnp.square(v - mu), axis=-1, keepdims=True) + EPS)
    xh = (v - mu) * rs
    return zu, zv, u, xh, rs, xh * ln_g + ln_b


def _gmlp_fwd(proj, ln_g, ln_b, w_s, b_st):
    T = proj.shape[0]

    def body(p_ref, g_ref, b_ref, w_ref, bs_ref, o_ref):
        _, _, u, _, _, vn = _gmlp_parts(p_ref[...].astype(f32), g_ref[...], b_ref[...])
        causal = _tri(GM_CHUNK, True) > 0
        for gi in range(N_HEAD):
            sl = slice(gi * HEAD, (gi + 1) * HEAD)
            w = jnp.where(causal, w_ref[gi], 0.0)
            mixed = _raw_dot(w, vn[:, sl], "nn") + bs_ref[:, gi:gi + 1]
            o_ref[:, sl] = (u[:, sl] * mixed).astype(bf16)

    vec = pl.BlockSpec((1, 512), lambda i: (0, 0))
    return _call(body, "gmlp_fwd", (T // GM_CHUNK,),
                 [pl.BlockSpec((GM_CHUNK, 1024), lambda i: (i, 0)), vec, vec,
                  pl.BlockSpec((N_HEAD, GM_CHUNK, GM_CHUNK), lambda i: (0, 0, 0)), pl.BlockSpec((GM_CHUNK, 128), lambda i: (0, 0))],
                 pl.BlockSpec((GM_CHUNK, 512), lambda i: (i, 0)), jax.ShapeDtypeStruct((T, 512), bf16), sem=("parallel",))(
        proj, ln_g, ln_b, w_s, b_st)


def _gmlp_bwd(proj, ln_g, ln_b, w_s, b_st, da):
    T = proj.shape[0]

    def body(p_ref, g_ref, b_ref, w_ref, bs_ref, da_ref, dp_ref, dg_ref, db_ref, dw_ref, dbs_ref):
        zu, zv, u, xh, rs, vn = _gmlp_parts(p_ref[...].astype(f32), g_ref[...], b_ref[...])
        causal = _tri(GM_CHUNK, True) > 0
        sub = lax.broadcasted_iota(jnp.int32, (8, GM_CHUNK), 0)
        ones = jnp.ones((8, HEAD), f32)
        dout = da_ref[...].astype(f32)

        @pl.when(pl.program_id(0) == 0)
        def _():
            for r in (dg_ref, db_ref, dw_ref, dbs_ref):
                r[...] = jnp.zeros_like(r)

        du, dvn, dbs = [], [], jnp.zeros((8, GM_CHUNK), f32)
        for gi in range(N_HEAD):
            sl = slice(gi * HEAD, (gi + 1) * HEAD)
            w = jnp.where(causal, w_ref[gi], 0.0)
            mixed = _raw_dot(w, vn[:, sl], "nn") + bs_ref[:, gi:gi + 1]
            du.append(dout[:, sl] * mixed)
            dm = dout[:, sl] * u[:, sl]
            row_sums = _sel_dot(ones, dm, "nt")
            dbs = dbs + jnp.where(sub == gi, row_sums, 0.0)
            dw_ref[gi] += jnp.where(causal, _raw_dot(dm, vn[:, sl], "nt"), 0.0)
            dvn.append(_raw_dot(w, dm, "tn"))
        dbs_ref[...] += dbs
        du = jnp.concatenate(du, axis=-1)
        dvn = jnp.concatenate(dvn, axis=-1)
        dg_ref[...] += jnp.sum(dvn * xh, axis=0, keepdims=True)
        db_ref[...] += jnp.sum(dvn, axis=0, keepdims=True)
        dxh = dvn * g_ref[...]
        dv = rs * (dxh - jnp.mean(dxh, axis=-1, keepdims=True) - xh * jnp.mean(dxh * xh, axis=-1, keepdims=True))
        dp_ref[:, :512] = _egrad(jax.nn.gelu, zu, du).astype(bf16)
        dp_ref[:, 512:] = _egrad(jax.nn.gelu, zv, dv).astype(bf16)

    vec = pl.BlockSpec((1, 512), lambda i: (0, 0))
    wsp = pl.BlockSpec((N_HEAD, GM_CHUNK, GM_CHUNK), lambda i: (0, 0, 0))
    return _call(body, "gmlp_bwd", (T // GM_CHUNK,),
                 [pl.BlockSpec((GM_CHUNK, 1024), lambda i: (i, 0)), vec, vec, wsp, pl.BlockSpec((GM_CHUNK, 128), lambda i: (0, 0)),
                  pl.BlockSpec((GM_CHUNK, 512), lambda i: (i, 0))],
                 (pl.BlockSpec((GM_CHUNK, 1024), lambda i: (i, 0)), vec, vec, wsp, pl.BlockSpec((8, GM_CHUNK), lambda i: (0, 0))),
                 (jax.ShapeDtypeStruct((T, 1024), bf16), jax.ShapeDtypeStruct((1, 512), f32), jax.ShapeDtypeStruct((1, 512), f32),
                  jax.ShapeDtypeStruct((N_HEAD, GM_CHUNK, GM_CHUNK), f32), jax.ShapeDtypeStruct((8, GM_CHUNK), f32)),
                 sem=("arbitrary",))(proj, ln_g, ln_b, w_s, b_st, da)


HG_SUB = 8
HG_NSUB = HG_CHUNK // HG_SUB


def _two_level_matrix():
    r = lax.broadcasted_iota(jnp.int32, (2 * HG_CHUNK, HG_CHUNK), 0)
    c = lax.broadcasted_iota(jnp.int32, (2 * HG_CHUNK, HG_CHUNK), 1)
    t = jnp.where(r < HG_CHUNK, r, r - HG_CHUNK)
    local = (r < HG_CHUNK) & (t // HG_SUB == c // HG_SUB) & (c <= t)
    before = (r >= HG_CHUNK) & (c < (t // HG_SUB) * HG_SUB)
    return (local | before).astype(f32)


def _two_level_sums(x):
    two = _sel_dot(_two_level_matrix(), x, "nn")
    return two[:HG_CHUNK], two[HG_CHUNK:]


@jax.custom_vjp
def _two_level_cumsum(x):
    return _two_level_sums(x)


_two_level_cumsum.defvjp(
    lambda x: (_two_level_sums(x), None),
    lambda _, g: (_sel_dot(_two_level_matrix(), jnp.concatenate(g, axis=0), "tn"),))


def _tile_matrix():
    s = lax.broadcasted_iota(jnp.int32, (HG_SUB, HG_CHUNK), 0)
    j = lax.broadcasted_iota(jnp.int32, (HG_SUB, HG_CHUNK), 1)
    return (j % HG_SUB == s).astype(f32)


@jax.custom_vjp
def _tile_lanes(x):
    return _sel_dot(_tile_matrix(), x, "nn", x_first=True)


_tile_lanes.defvjp(
    lambda x: (_sel_dot(_tile_matrix(), x, "nn", x_first=True), None),
    lambda _, g: (_sel_dot(_tile_matrix(), g, "nt", x_first=True),))


def _block_rows(x):
    k = x.shape[-1]
    return jnp.broadcast_to(x.reshape(HG_NSUB, 1, HG_SUB, k), (HG_NSUB, HG_SUB, HG_SUB, k)).reshape(HG_CHUNK, HG_SUB, k)


def _hgrn_chunk(st0, q_raw, f_raw, i_raw, g_raw, l0, l1, ng):
    C, SUB = HG_CHUNK, HG_SUB
    lb = jax.nn.sigmoid(l0 - l1)
    fg = lb + (1.0 - lb) * jax.nn.sigmoid(f_raw)
    kk = 1.0 - fg
    qf = jax.nn.silu(q_raw)
    al, base = _two_level_cumsum(jnp.log(fg))
    a = al + base
    row = lax.broadcasted_iota(jnp.int32, (C, HEAD), 0)
    a_last = jnp.sum(jnp.where(row == C - 1, a, 0.0), axis=0, keepdims=True)
    inter = _dot_nt(qf * jnp.exp(a), st0)
    qt = qf * jnp.exp(al)
    rb = lax.broadcasted_iota(jnp.int32, (C, C), 0) // SUB
    cb = lax.broadcasted_iota(jnp.int32, (C, C), 1) // SUB
    scores = jnp.zeros((C, C), f32)
    for i in range(1, HG_NSUB):
        base_i = jnp.sum(jnp.where(row == i * SUB, base, 0.0), axis=0, keepdims=True)
        kt = kk * jnp.exp(jnp.minimum(base_i - a, 0.0))
        scores = scores + jnp.where((rb == i) & (cb < i), _dot_nt(qt, kt), 0.0)
    t_i = lax.broadcasted_iota(jnp.int32, (C, SUB, HEAD), 0) % SUB
    s_i = lax.broadcasted_iota(jnp.int32, (C, SUB, HEAD), 1)
    decay = jnp.exp(jnp.where(s_i <= t_i, al[:, None, :] - _block_rows(al), -jnp.inf))
    diag = jnp.sum(qf[:, None, :] * decay * _block_rows(kk), axis=-1)
    scores = scores + jnp.where(rb == cb, _tile_lanes(diag), 0.0)
    o = inter + _dot_nn(scores, i_raw)
    st1 = jnp.exp(a_last) * st0 + _dot_tn(i_raw, kk * jnp.exp(a_last - a))
    on = o * lax.rsqrt(jnp.mean(o * o, axis=-1, keepdims=True) + EPS) * ng
    return st1, on * jax.nn.silu(g_raw)


def _hgrn_specs(S, Bl, rev):
    N = S // HG_CHUNK
    chunk = (lambda n: N - 1 - n) if rev else (lambda n: n)
    col = lambda c0: pl.BlockSpec((Bl, HG_CHUNK, 512), lambda n: (0, chunk(n), c0 // 512))
    st = pl.BlockSpec((Bl, N_HEAD, 1, HEAD, HEAD), lambda n: (0, 0, chunk(n), 0, 0))
    full = lambda *s: pl.BlockSpec(s, functools.partial(lambda n, nd: (0,) * nd, nd=len(s)))
    return N, col, st, full


def _hgrn_fwd(proj, lb_logits, ng, Bl, S):
    N, col, st, full = _hgrn_specs(S, Bl, False)

    def body(q_ref, f_ref, i_ref, g_ref, l_ref, ng_ref, o_ref, st_ref, state):
        @pl.when(pl.program_id(0) == 0)
        def _():
            state[...] = jnp.zeros_like(state)

        for b in range(Bl):
            for h in range(N_HEAD):
                sl = slice(h * HEAD, (h + 1) * HEAD)
                st0 = state[b, h]
                st_ref[b, h, 0] = st0
                st1, out = _hgrn_chunk(st0, *[r[b, :, sl].astype(f32) for r in (q_ref, f_ref, i_ref, g_ref)],
                                       l_ref[0:1, sl], l_ref[1:2, sl], ng_ref[...])
                state[b, h] = st1
                o_ref[b, :, sl] = out.astype(bf16)

    return _call(body, "hgrn_fwd", (N,), [col(C_HQ), col(C_HF), col(C_HI), col(C_HG), full(2, 512), full(1, HEAD)],
                 (col(0), st),
                 (jax.ShapeDtypeStruct((Bl, S, 512), bf16), jax.ShapeDtypeStruct((Bl, N_HEAD, N, HEAD, HEAD), f32)),
                 scratch=[pltpu.VMEM((Bl, N_HEAD, HEAD, HEAD), f32)], sem=("arbitrary",))(
        proj, proj, proj, proj, lb_logits, ng)


def _hgrn_bwd(proj, lb_logits, ng, states, db, Bl, S):
    N, col, st, full = _hgrn_specs(S, Bl, True)

    def body(q_ref, f_ref, i_ref, g_ref, l_ref, ng_ref, st_ref, db_ref,
             dq_ref, df_ref, di_ref, dg_ref, dl_ref, dng_ref, dstate):
        @pl.when(pl.program_id(0) == 0)
        def _():
            dstate[...] = jnp.zeros_like(dstate)
            dl_ref[...] = jnp.zeros_like(dl_ref)
            dng_ref[...] = jnp.zeros_like(dng_ref)

        for b in range(Bl):
            for h in range(N_HEAD):
                sl = slice(h * HEAD, (h + 1) * HEAD)
                _, vjp = jax.vjp(_hgrn_chunk, st_ref[b, h, 0], *[r[b, :, sl].astype(f32) for r in (q_ref, f_ref, i_ref, g_ref)],
                                 l_ref[0:1, sl], l_ref[1:2, sl], ng_ref[...])
                dst0, dq, df, di, dg, dl0, dl1, dng = vjp((dstate[b, h], db_ref[b, :, sl].astype(f32)))
                dstate[b, h] = dst0
                dq_ref[b, :, sl] = dq.astype(bf16)
                df_ref[b, :, sl] = df.astype(bf16)
                di_ref[b, :, sl] = di.astype(bf16)
                dg_ref[b, :, sl] = dg.astype(bf16)
                dl_ref[0:1, sl] += dl0
                dl_ref[1:2, sl] += dl1
                dng_ref[b, h] += dng

    return _call(body, "hgrn_bwd", (N,),
                 [col(C_HQ), col(C_HF), col(C_HI), col(C_HG), full(2, 512), full(1, HEAD), st, col(0)],
                 (*[col(0)] * 4, full(2, 512), full(Bl, N_HEAD, 1, HEAD)),
                 (*[jax.ShapeDtypeStruct((Bl, S, 512), bf16)] * 4, jax.ShapeDtypeStruct((2, 512), f32),
                  jax.ShapeDtypeStruct((Bl, N_HEAD, 1, HEAD), f32)),
                 scratch=[pltpu.VMEM((Bl, N_HEAD, HEAD, HEAD), f32)], sem=("arbitrary",))(
        proj, proj, proj, proj, lb_logits, ng, states, db)


def _attn_probs(q, k):
    s = _raw_dot(q, k, "nt") * (HEAD ** -0.5)
    e = jnp.exp(s - jnp.max(s, axis=-1, keepdims=True))
    return e / jnp.sum(e, axis=-1, keepdims=True)


def _attn_specs(S, tq):
    nq = S // tq
    q = pl.BlockSpec((tq, 512), lambda b, i: (b * nq + i, C_XQ // 512))
    kv = pl.BlockSpec((1, MEM_LEN, 1024), lambda b, i: (b, 0, 0))
    o = pl.BlockSpec((tq, 512), lambda b, i: (b * nq + i, 0))
    return nq, q, kv, o


def _attn_fwd(proj, kv, Bl, S):
    tq = _pick(S, (512, 256, 128))
    nq, qs, kvs, os_ = _attn_specs(S, tq)

    def body(q_ref, kv_ref, o_ref):
        for h in range(N_HEAD):
            sl = slice(h * HEAD, (h + 1) * HEAD)
            p = _attn_probs(q_ref[:, sl], kv_ref[0, :, sl])
            o_ref[:, sl] = _raw_dot(p, kv_ref[0, :, 512 + h * HEAD:512 + (h + 1) * HEAD], "nn").astype(bf16)

    return _call(body, "attn_fwd", (Bl, nq), [qs, kvs], os_, jax.ShapeDtypeStruct((Bl * S, 512), bf16),
                 sem=("parallel", "parallel"))(proj, kv)


def _attn_bwd(proj, kv, dc, Bl, S):
    tq = _pick(S, (512, 256, 128))
    nq, qs, kvs, os_ = _attn_specs(S, tq)

    def body(q_ref, kv_ref, do_ref, dq_ref, dkv_ref):
        @pl.when(pl.program_id(1) == 0)
        def _():
            dkv_ref[...] = jnp.zeros_like(dkv_ref)

        for h in range(N_HEAD):
            sl = slice(h * HEAD, (h + 1) * HEAD)
            vsl = slice(512 + h * HEAD, 512 + (h + 1) * HEAD)
            q, k, v, do = q_ref[:, sl], kv_ref[0, :, sl], kv_ref[0, :, vsl], do_ref[:, sl]
            p = _attn_probs(q, k)
            dkv_ref[0, :, vsl] += _raw_dot(p, do, "tn")
            dp = _raw_dot(do, v, "nt")
            ds = p * (dp - jnp.sum(dp * p, axis=-1, keepdims=True)) * (HEAD ** -0.5)
            dq_ref[:, sl] = _raw_dot(ds, k, "nn").astype(bf16)
            dkv_ref[0, :, sl] += _raw_dot(ds, q, "tn")

    return _call(body, "attn_bwd", (Bl, nq), [qs, kvs, os_], (os_, kvs),
                 (jax.ShapeDtypeStruct((Bl * S, 512), bf16), jax.ShapeDtypeStruct((Bl, MEM_LEN, 1024), f32)),
                 sem=("arbitrary", "arbitrary"))(proj, kv, dc)


def _merge_specs(tm, tn):
    br = pl.BlockSpec((tm, 512), lambda i, j: (i, 0))
    w = pl.BlockSpec((512, tn), lambda i, j: (0, j))
    gl = [pl.BlockSpec((tm, tn), functools.partial(lambda i, j, n: (i, (C_GL + n * D_MODEL) // tn + j), n=n)) for n in range(3)]
    return [br, br, br, w, w, w, *gl]


def _merge_fwd(branches, wb, proj):
    T = proj.shape[0]
    tm, tn = _pick(T, (1024, 512, 256, 128)), 512

    def body(a_ref, b_ref, c_ref, w0, w1, w2, g0, g1, g2, o_ref):
        acc = jnp.zeros((tm, tn), f32)
        for x_ref, w_ref, g_ref in ((a_ref, w0, g0), (b_ref, w1, g1), (c_ref, w2, g2)):
            acc = acc + jax.nn.sigmoid(g_ref[...].astype(f32)) * _raw_dot(x_ref[...], w_ref[...], "nn")
        o_ref[...] = acc.astype(bf16)

    return _call(body, "merge_fwd", (T // tm, D_MODEL // tn), _merge_specs(tm, tn), pl.BlockSpec((tm, tn), lambda i, j: (i, j)),
                 jax.ShapeDtypeStruct((T, D_MODEL), bf16), sem=("parallel", "parallel"))(*branches, *wb, proj, proj, proj)


def _merge_bwd(branches, wb, proj, dmerged):
    T = proj.shape[0]
    tm, tn = _pick(T, (1024, 512, 256, 128)), 512

    def body(a_ref, b_ref, c_ref, w0, w1, w2, g0, g1, g2, dm_ref, dgl_ref, d0, d1, d2):
        dm = dm_ref[...]
        for n, (x_ref, w_ref, g_ref, d_ref) in enumerate(((a_ref, w0, g0, d0), (b_ref, w1, g1, d1), (c_ref, w2, g2, d2))):
            up = _raw_dot(x_ref[...], w_ref[...], "nn")
            logits = g_ref[...].astype(f32)
            dgl_ref[n] = _egrad(jax.nn.sigmoid, logits, dm * up).astype(bf16)
            d_ref[...] = (dm * jax.nn.sigmoid(logits)).astype(bf16)

    blk = pl.BlockSpec((tm, tn), lambda i, j: (i, j))
    sh = jax.ShapeDtypeStruct((T, D_MODEL), bf16)
    outs = _call(body, "merge_bwd", (T // tm, D_MODEL // tn), [*_merge_specs(tm, tn), blk],
                 (pl.BlockSpec((3, tm, tn), lambda i, j: (0, i, j)), blk, blk, blk),
                 (jax.ShapeDtypeStruct((3, T, D_MODEL), bf16), sh, sh, sh),
                 sem=("parallel", "parallel"))(*branches, *wb, proj, proj, proj, dmerged)
    return outs[0], outs[1:]


CONV_TC = 256


def _shift_down(a, k):
    row = lax.broadcasted_iota(jnp.int32, a.shape, 0)
    return jnp.where(row >= k, pltpu.roll(a, k, 0), 0.0)


def _shift_up(a, k):
    n = a.shape[0]
    row = lax.broadcasted_iota(jnp.int32, a.shape, 0)
    return jnp.where(row < n - k, pltpu.roll(a, n - k, 0), 0.0)


def _conv_pre(a, cw, cb):
    return cb + cw[0:1] * _shift_down(a, 2) + cw[1:2] * _shift_down(a, 1) + cw[2:3] * a


def _conv_fwd(ab, cw, cb, Bl, S):
    nc = D_FF // CONV_TC

    def body(a_ref, b_ref, cw_ref, cb_ref, o_ref):
        ac = _conv_pre(a_ref[0].astype(f32), cw_ref[...], cb_ref[...])
        o_ref[0] = (jax.nn.silu(ac) * b_ref[0].astype(f32)).astype(bf16)

    return _call(body, "conv_fwd", (Bl, nc),
                 [pl.BlockSpec((1, S, CONV_TC), lambda b, c: (b, 0, c)), pl.BlockSpec((1, S, CONV_TC), lambda b, c: (b, 0, nc + c)),
                  pl.BlockSpec((3, CONV_TC), lambda b, c: (0, c)), pl.BlockSpec((1, CONV_TC), lambda b, c: (0, c))],
                 pl.BlockSpec((1, S, CONV_TC), lambda b, c: (b, 0, c)), jax.ShapeDtypeStruct((Bl, S, D_FF), bf16),
                 sem=("parallel", "parallel"))(ab, ab, cw, cb)


def _conv_bwd(ab, cw, cb, dact, Bl, S):
    nc = D_FF // CONV_TC

    def body(a_ref, b_ref, cw_ref, cb_ref, d_ref, da_ref, db_ref, dcw_ref, dcb_ref):
        @pl.when(pl.program_id(1) == 0)
        def _():
            dcw_ref[...] = jnp.zeros_like(dcw_ref)
            dcb_ref[...] = jnp.zeros_like(dcb_ref)

        a, cw = a_ref[0].astype(f32), cw_ref[...]
        ac = _conv_pre(a, cw, cb_ref[...])
        dact_ = d_ref[0].astype(f32)
        db_ref[0] = (dact_ * jax.nn.silu(ac)).astype(bf16)
        dac = _egrad(jax.nn.silu, ac, dact_ * b_ref[0].astype(f32))
        da_ref[0] = (cw[2:3] * dac + cw[1:2] * _shift_up(dac, 1) + cw[0:1] * _shift_up(dac, 2)).astype(bf16)
        dcw_ref[0:1, :] += jnp.sum(dac * _shift_down(a, 2), axis=0, keepdims=True)
        dcw_ref[1:2, :] += jnp.sum(dac * _shift_down(a, 1), axis=0, keepdims=True)
        dcw_ref[2:3, :] += jnp.sum(dac * a, axis=0, keepdims=True)
        dcb_ref[...] += jnp.sum(dac, axis=0, keepdims=True)

    seq = pl.BlockSpec((1, S, CONV_TC), lambda c, b: (b, 0, c))
    return _call(body, "conv_bwd", (nc, Bl),
                 [seq, pl.BlockSpec((1, S, CONV_TC), lambda c, b: (b, 0, nc + c)), pl.BlockSpec((3, CONV_TC), lambda c, b: (0, c)),
                  pl.BlockSpec((1, CONV_TC), lambda c, b: (0, c)), seq],
                 (seq, seq, pl.BlockSpec((3, CONV_TC), lambda c, b: (0, c)), pl.BlockSpec((1, CONV_TC), lambda c, b: (0, c))),
                 (jax.ShapeDtypeStruct((Bl, S, D_FF), bf16), jax.ShapeDtypeStruct((Bl, S, D_FF), bf16),
                  jax.ShapeDtypeStruct((3, D_FF), f32), jax.ShapeDtypeStruct((1, D_FF), f32)),
                 sem=("arbitrary", "arbitrary"))(ab, ab, cw, cb, dact)


def _local_step(x, mem, target, p, w_in, late_b, late_c, send, settle):
    Bl, S, Dd = x.shape
    T = Bl * S
    x2d, t2d, mem2d = x.reshape(T, Dd), target.reshape(T, Dd), mem.reshape(Bl * MEM_LEN, Dd)
    b_st = jnp.pad(p["b_spatial"].T, ((0, 0), (0, 128 - N_HEAD)))
    lbl = p["lb_logits"]

    h = _rms_fwd(x2d, p["norm1_g"], "norm1_fwd")
    proj = _mm(h, w_in, "nn", bf16, "proj_fwd", 1024, 1664)
    a_out = _gmlp_fwd(proj, p["ln_v_g"], p["ln_v_b"], p["w_spatial"], b_st)
    proj3 = proj.reshape(Bl, S, IN_WIDTH)
    b_out, states = _hgrn_fwd(proj3, lbl, p["hgrn_norm_g"], Bl, S)
    b_out = b_out.reshape(T, 512)
    memn = _rms_fwd(mem2d, p["mem_norm_g"], "memnorm_fwd")
    w = late_b(b_out)
    wb = w["w_branch"]
    kv = _mm(memn, w["w_mem_kv"], "nn", f32, "kv_fwd", 512, 1024).reshape(Bl, MEM_LEN, 2 * 512)
    c_out = _attn_fwd(proj, kv, Bl, S)
    branches = (a_out, b_out, c_out)
    merged = _merge_fwd(branches, wb, proj)
    x1 = _mm(merged, w["w_out"], "nn", f32, "out_fwd", 1024, 1024, residual=x2d)
    h2 = _rms_fwd(x1, p["norm2_g"], "norm2_fwd")
    w.update(late_c(h2))
    ab = _mm(h2, w["w_up"], "nn", bf16, "up_fwd", 1024, 1408)
    act = _conv_fwd(ab.reshape(Bl, S, 2 * D_FF), w["conv_w"], p["conv_b"], Bl, S).reshape(T, D_FF)
    x2 = _mm(act, w["w_down"], "nn", f32, "down_fwd", 512, 1024, residual=x1)
    loss_part, dx2, g_final = _final_loss(x2, p["final_g"], t2d)

    g_w_down = _mm(act, dx2, "tn", bf16, "down_dw", 1408, 1024, 1024)
    dact = _mm(dx2, w["w_down"], "nt", bf16, "down_dx", 1024, 1408)
    da, db, g_conv_w, g_conv_b = _conv_bwd(ab.reshape(Bl, S, 2 * D_FF), w["conv_w"], p["conv_b"], dact.reshape(Bl, S, D_FF), Bl, S)
    dab = jnp.concatenate([da.reshape(T, D_FF), db.reshape(T, D_FF)], axis=-1)
    g_w_up = _mm(h2, dab, "tn", bf16, "up_dw", 512, 1408, 1024, shard=2 * D_FF // N_DEV)
    tok = send("c", dict(w_up=g_w_up, conv_w=g_conv_w, w_down=g_w_down))
    dh2 = _mm(dab, w["w_up"], "nt", f32, "up_dx", 1024, 1024, 1408)
    dx1, g_norm2 = _rms_bwd(x1, p["norm2_g"] + tok[0, 0], dh2, "norm2_bwd", residual=dx2)

    g_w_out = _mm(merged, dx1, "tn", bf16, "out_dw", 1024, 1024, 1024)
    dmerged = _mm(dx1, w["w_out"], "nt", f32, "out_dx", 1024, 1024)
    dgl, dup = _merge_bwd(branches, wb, proj, dmerged)
    g_w_branch = [_mm(branches[n], dup[n], "tn", bf16, f"branch_dw{n}", 512, 1024, 1024) for n in range(3)]
    dbr = [_mm(dup[n], wb[n], "nt", bf16, f"branch_dx{n}", 1024, 512) for n in range(3)]
    dxq, dkv = _attn_bwd(proj, kv, dbr[2], Bl, S)
    dkv = dkv.reshape(Bl * MEM_LEN, 2 * 512)
    g_w_kv = _mm(memn, dkv, "tn", bf16, "kv_dw", 1024, 1024, 512)
    tok = send("b", dict(w_mem_kv=g_w_kv, w_branch=g_w_branch, w_out=g_w_out))
    dmemn = _mm(dkv, w["w_mem_kv"], "nt", f32, "kv_dx", 512, 1024)
    _, g_mem_norm = _rms_bwd(mem2d, p["mem_norm_g"], dmemn, "memnorm_bwd")
    dzuv, g_ln_g, g_ln_b, g_w_sp, g_b_sp = _gmlp_bwd(proj, p["ln_v_g"] + tok[0, 0], p["ln_v_b"], p["w_spatial"], b_st, dbr[0])
    *dqfig, g_lbl, g_ng = _hgrn_bwd(proj3, lbl, p["hgrn_norm_g"], states, dbr[1].reshape(Bl, S, 512), Bl, S)
    dq, df, di, dg = [d.reshape(T, 512) for d in dqfig]
    dproj = jnp.concatenate([dzuv, dq, df, di, dg, settle(dxq), dgl[0], dgl[1], dgl[2]], axis=-1)
    g_w_in = _mm(h, dproj, "tn", bf16, "proj_dw", 512, 1664, 1024, shard=IN_WIDTH // N_DEV)
    tok = send("a", dict(w_in=g_w_in))
    dh = _mm(dproj, w_in, "nt", f32, "proj_dx", 1024, 1024, 1664)
    dx, g_norm1 = _rms_bwd(x2d, p["norm1_g"] + tok[0, 0], dh, "norm1_bwd", residual=dx1)

    gs = dict(w_spatial=g_w_sp, norm1_g=g_norm1, mem_norm_g=g_mem_norm, norm2_g=g_norm2, final_g=g_final, lb_logits=g_lbl,
              ln_v_g=g_ln_g, ln_v_b=g_ln_b, b_spatial=g_b_sp, hgrn_norm_g=g_ng, conv_b=g_conv_b)
    return loss_part, dx.reshape(Bl, S, Dd), gs


def _coords():
    return lax.axis_index("x"), lax.axis_index("y"), lax.axis_index("c")


def _slot(dev):
    return 4 * dev[0] + 2 * dev[1] + dev[2]


def _comm_call(body, name, arrays, out_shapes, n_sem):
    n = len(arrays)
    hbm = pl.BlockSpec(memory_space=pl.ANY)
    return pl.pallas_call(
        body, name=name, out_shape=out_shapes, in_specs=[hbm] * n, out_specs=[hbm] * n,
        scratch_shapes=[pltpu.SemaphoreType.DMA((n_sem, n)), pltpu.SemaphoreType.DMA((n_sem, n)), pltpu.SemaphoreType.DMA((n,))])(*arrays)


def _all_gather(blocks, name):
    n = len(blocks)

    def body(*refs):
        x_refs, o_refs, (send_sems, recv_sems, local_sems) = refs[:n], refs[n:2 * n], refs[2 * n:]
        x, y, c = _coords()
        me, sibling = (x, y, c), (x, y, 1 - c)
        chips = [(1 - x, y), (x, 1 - y), (1 - x, 1 - y)]

        def copy(a, k, block_dev, to, from_input=False):
            dst = o_refs[a].at[_slot(block_dev)]
            return pltpu.make_async_remote_copy(src_ref=x_refs[a] if from_input else dst, dst_ref=dst, send_sem=send_sems.at[k, a],
                                                recv_sem=recv_sems.at[k, a], device_id=to, device_id_type=MESH)

        mine = [pltpu.make_async_copy(x_refs[a], o_refs[a].at[_slot(me)], local_sems.at[a]) for a in range(n)]
        first = [copy(a, 0, me, sibling, True) for a in range(n)]
        first += [copy(a, 1 + j, me, (*chip, c), True) for j, chip in enumerate(chips) for a in range(n)]
        for cp in mine + first:
            cp.start()
        passed = []
        for j, chip in enumerate(chips):
            for a in range(n):
                copy(a, 1 + j, (*chip, c), me).wait_recv()
                fwd = copy(a, 4 + j, (*chip, c), sibling)
                fwd.start()
                passed.append(fwd)
        for a in range(n):
            copy(a, 0, sibling, me).wait_recv()
        for j, chip in enumerate(chips):
            for a in range(n):
                copy(a, 4 + j, (*chip, 1 - c), me).wait_recv()
        for cp in first + passed:
            cp.wait_send()
        for cp in mine:
            cp.wait()

    return _comm_call(body, name, blocks, [jax.ShapeDtypeStruct((N_DEV,) + b.shape, b.dtype) for b in blocks], 7)


def _all_to_all(parts, name):
    n = len(parts)
    rel = [(0, 0, 1), (0, 1, 0), (0, 1, 1), (1, 0, 0), (1, 0, 1), (1, 1, 0), (1, 1, 1)]

    def body(*refs):
        x_refs, o_refs, (send_sems, recv_sems, local_sems) = refs[:n], refs[n:2 * n], refs[2 * n:]
        x, y, c = _coords()
        me = (x, y, c)
        peers = [(x ^ dx, y ^ dy, c ^ dc) for dx, dy, dc in rel]

        def copy(a, k, peer):
            return pltpu.make_async_remote_copy(src_ref=x_refs[a].at[_slot(peer)], dst_ref=o_refs[a].at[_slot(me)], send_sem=send_sems.at[k, a],
                                                recv_sem=recv_sems.at[k, a], device_id=peer, device_id_type=MESH)

        def arrival(a, k, peer):
            return pltpu.make_async_remote_copy(src_ref=x_refs[a].at[_slot(me)], dst_ref=o_refs[a].at[_slot(peer)], send_sem=send_sems.at[k, a],
                                                recv_sem=recv_sems.at[k, a], device_id=peer, device_id_type=MESH)

        mine = [pltpu.make_async_copy(x_refs[a].at[_slot(me)], o_refs[a].at[_slot(me)], local_sems.at[a]) for a in range(n)]
        sends = [copy(a, k, peer) for k, peer in enumerate(peers) for a in range(n)]
        for cp in mine + sends:
            cp.start()
        for k, peer in enumerate(peers):
            for a in range(n):
                arrival(a, k, peer).wait_recv()
        for cp in sends:
            cp.wait_send()
        for cp in mine:
            cp.wait()

    return _comm_call(body, name, parts, [jax.ShapeDtypeStruct(p.shape, p.dtype) for p in parts], 7)


_HBM = pl.BlockSpec(memory_space=pltpu.HBM)
_SEM = pl.BlockSpec(memory_space=pltpu.SEMAPHORE)
_REL = [(0, 0, 1), (0, 1, 0), (0, 1, 1), (1, 0, 0), (1, 0, 1), (1, 1, 0), (1, 1, 1)]


_LINK_ORDER = (3, 1, 5, 4, 2, 6, 0)
SEND_PIECES = 4


def _pieces(shape, dtype):
    rows = shape[0]
    unit = 1 if len(shape) > 2 else (16 if dtype == bf16 else 8)
    for n in (SEND_PIECES, 2):
        if rows % (n * unit) == 0:
            return [pl.ds(i * (rows // n), rows // n) for i in range(n)]
    return [pl.ds(0, rows)]


def _split_copies(gather, src, land, send, recv, pieces):
    x, y, c = _coords()
    me = (x, y, c)
    copies = []
    for a in range(len(src)):
        block = src[a].shape if gather else src[a].shape[1:]
        for rows in (_pieces(block, src[a].dtype) if pieces else [None]):
            for k in _LINK_ORDER:
                dx, dy, dc = _REL[k]
                peer = (x ^ dx, y ^ dy, c ^ dc)
                mine, there = (src[a] if gather else src[a].at[_slot(peer)]), land[a].at[_slot(me)]
                if rows is not None:
                    mine, there = mine.at[rows], there.at[rows]
                copies.append(pltpu.make_async_remote_copy(src_ref=mine, dst_ref=there, send_sem=send[a].at[k], recv_sem=recv[a].at[k],
                                                           device_id=peer, device_id_type=MESH))
    return me, copies


def _arrivals(gather, src, land, send, recv):
    x, y, c = _coords()
    out = []
    for a in range(len(src)):
        for k, (dx, dy, dc) in enumerate(_REL):
            peer = (x ^ dx, y ^ dy, c ^ dc)
            out.append(pltpu.make_async_remote_copy(src_ref=src[a] if gather else src[a].at[_slot(peer)], dst_ref=land[a].at[_slot(peer)],
                                                    send_sem=send[a].at[k], recv_sem=recv[a].at[k], device_id=peer, device_id_type=MESH))
    return out


def _exchange_start(arrays, gather, name, after=None):
    n = len(arrays)
    e = 0 if after is None else 1
    lands = [lax.empty(((N_DEV,) + a.shape) if gather else a.shape, a.dtype) for a in arrays]

    def body(*refs):
        src, land = refs[:n], refs[n:2 * n]
        refs = refs[2 * n + e:]
        send, recv, token, local_sems = refs[:n], refs[n:2 * n], refs[4 * n], refs[4 * n + 1]
        me, out = _split_copies(gather, src, land, send, recv, True)
        local = [pltpu.make_async_copy(src[a] if gather else src[a].at[_slot(me)], land[a].at[_slot(me)], local_sems.at[a])
                 for a in range(n)]
        for cp in local:
            cp.start()
        for cp in local:
            cp.wait()
        for cp in out:
            cp.start()
        token[...] = jnp.zeros_like(token)

    sems = [pltpu.SemaphoreType.DMA((7,)) for _ in range(2 * n)]
    outs = pl.pallas_call(
        body, name=name,
        out_shape=(*sems, *[pltpu.HBM(a.shape, a.dtype) for a in arrays], *[pltpu.HBM(l.shape, l.dtype) for l in lands],
                   jax.ShapeDtypeStruct((8, 128), f32)),
        in_specs=[_HBM] * (2 * n) + [pl.BlockSpec(memory_space=pl.ANY)] * e,
        out_specs=(*[_SEM] * (2 * n), *[_HBM] * (2 * n), pl.BlockSpec(memory_space=pltpu.VMEM)),
        input_output_aliases={i: 2 * n + i for i in range(2 * n)},
        scratch_shapes=[pltpu.SemaphoreType.DMA((n,))],
        compiler_params=pltpu.CompilerParams(has_side_effects=pltpu.SideEffectType.DATAFLOW_SIDE_EFFECTING))(
        *[pltpu.with_memory_space_constraint(a, pltpu.HBM) for a in arrays],
        *[pltpu.with_memory_space_constraint(l, pltpu.HBM) for l in lands], *([after] if e else []))
    return (gather, n, outs[:4 * n]), outs[4 * n]


def _exchange_wait(handle, which, after, name):
    gather, n_all, vals = handle
    send_v, recv_v, src_v, land_v = [[vals[g * n_all + i] for i in which] for g in range(4)]
    n = len(which)

    def body(*refs):
        src, land, send, recv = refs[:n], refs[n:2 * n], refs[2 * n:3 * n], refs[3 * n:4 * n]
        for cp in _split_copies(gather, src, land, send, recv, False)[1]:
            cp.wait_send()
        for cp in _arrivals(gather, src, land, send, recv):
            cp.wait_recv()

    outs = pl.pallas_call(
        body, name=name,
        out_shape=(*[pltpu.HBM(a.shape, a.dtype) for a in src_v], *[pltpu.HBM(l.shape, l.dtype) for l in land_v]),
        in_specs=[*[_HBM] * (2 * n), *[_SEM] * (2 * n), pl.BlockSpec(memory_space=pl.ANY)], out_specs=[_HBM] * (2 * n),
        input_output_aliases={i: i for i in range(2 * n)},
        compiler_params=pltpu.CompilerParams(has_side_effects=pltpu.SideEffectType.DATAFLOW_SIDE_EFFECTING))(
        *src_v, *land_v, *send_v, *recv_v, after)
    return outs[n:]


def _seq_exchange(arrays, gather, name, collective_id):
    n = len(arrays)
    hbm = pltpu.MemorySpace.HBM
    srcs = [jax.new_ref(a, memory_space=hbm) for a in arrays]
    lands = [jax.empty_ref(jax.ShapeDtypeStruct(((N_DEV,) + a.shape) if gather else a.shape, a.dtype), memory_space=hbm) for a in arrays]

    @pl.kernel(mesh=plsc.ScalarSubcoreMesh(axis_name="sequencer", num_cores=1), name=name,
               scratch_types=(pltpu.SemaphoreType.DMA((7, n)), pltpu.SemaphoreType.DMA((7, n)), pltpu.SemaphoreType.DMA((n,))),
               compiler_params=pltpu.CompilerParams(collective_id=collective_id))
    def launch(send, recv, local):
        x, y, c = _coords()
        me = (x, y, c)
        peers = [(x ^ dx, y ^ dy, c ^ dc) for dx, dy, dc in _REL]
        barrier = pltpu.get_barrier_semaphore()
        for peer in peers:
            pl.semaphore_signal(barrier, inc=1, device_id=peer, device_id_type=MESH)
        pl.semaphore_wait(barrier, len(peers))

        def copy(a, k, peer, arrival):
            return pltpu.make_async_remote_copy(
                src_ref=srcs[a] if gather else srcs[a].at[_slot(peer)], dst_ref=lands[a].at[_slot(peer if arrival else me)],
                send_sem=send.at[k, a], recv_sem=recv.at[k, a], device_id=peer, device_id_type=MESH)

        mine = [pltpu.make_async_copy(srcs[a] if gather else srcs[a].at[_slot(me)], lands[a].at[_slot(me)], local.at[a])
                for a in range(n)]
        out = [copy(a, k, peer, False) for a in range(n) for k, peer in enumerate(peers)]
        for cp in mine + out:
            cp.start()
        for a in range(n):
            for k, peer in enumerate(peers):
                copy(a, k, peer, True).wait_recv()
        for cp in out:
            cp.wait_send()
        for cp in mine:
            cp.wait()

    launch()
    return [land[...] for land in lands]


def _adam_math(w, g, m, v):
    m_ = ADAM_B1 * m + (1.0 - ADAM_B1) * g
    v_ = ADAM_B2 * v + (1.0 - ADAM_B2) * jnp.square(g)
    m_hat = m_ / (1.0 - ADAM_B1 ** ADAM_STEP)
    v_hat = v_ / (1.0 - ADAM_B2 ** ADAM_STEP)
    return -ADAM_LR * (m_hat / (jnp.sqrt(v_hat) + ADAM_EPS) + ADAM_WD * w), m_, v_


def _reduce_adamw(parts, w, m, v, name):
    _, R, L = parts.shape
    tr = _pick(R, (256, 128, 64, 32, 16, 8))

    def body(p_ref, w_ref, m_ref, v_ref, g_ref, d_ref, nm_ref, nv_ref):
        g = p_ref[0].astype(f32)
        for i in range(1, N_DEV):
            g = g + p_ref[i].astype(f32)
        g_ref[...] = g
        d_ref[...], nm_ref[...], nv_ref[...] = _adam_math(w_ref[...], g, m_ref[...], v_ref[...])

    blk = pl.BlockSpec((tr, L), lambda i: (i, 0))
    sh = jax.ShapeDtypeStruct((R, L), f32)
    return _call(body, name, (R // tr,), [pl.BlockSpec((N_DEV, tr, L), lambda i: (0, i, 0)), blk, blk, blk], (blk,) * 4, (sh,) * 4,
                 sem=("parallel",))(parts, w, m, v)


SMALL = (("w_spatial", (512, 128), 0), ("norm1_g", (1, 1024), 512), ("mem_norm_g", (1, 1024), 520), ("norm2_g", (1, 1024), 528),
         ("final_g", (1, 1024), 536), ("lb_logits", (2, 512), 544), ("ln_v_g", (1, 512), 552), ("ln_v_b", (1, 512), 556),
         ("b_spatial", (4, 128), 560), ("hgrn_norm_g", (1, 128), 564), ("conv_b", (1, 2816), 565))
SMALL_USED, SMALL_ROWS = 587, 640


def _segments(shape, base):
    r, n = shape
    per = n // 128
    return [(base + i * per + j, i, slice(j * 128, (j + 1) * 128)) for i in range(r) for j in range(per)]


def _pack_small(gs):
    names = [n for n, _, _ in SMALL]

    def body(*refs):
        src, o_ref = dict(zip(names, refs[:-1])), refs[-1]
        o_ref[SMALL_USED:SMALL_ROWS, :] = jnp.zeros((SMALL_ROWS - SMALL_USED, 128), f32)
        for name, shape, base in SMALL:
            ref = src[name]
            if name == "w_spatial":
                o_ref[base:base + 512, :] = ref[...].reshape(512, 128)
            elif name == "b_spatial":
                o_ref[base:base + 4, :] = ref[0:4, :]
            elif name == "hgrn_norm_g":
                per_head = [ref[b, h] for b in range(ref.shape[0]) for h in range(N_HEAD)]
                o_ref[base:base + 1, :] = functools.reduce(lambda u, v_: u + v_, per_head)
            else:
                for row, i, sl in _segments(shape, base):
                    o_ref[row:row + 1, :] = ref[i:i + 1, sl]

    return pl.pallas_call(body, name="pack_small", out_shape=jax.ShapeDtypeStruct((SMALL_ROWS, 128), f32))(*[gs[n] for n in names])


def _small_update(gathered, w, m, v):
    names = [n for n, _, _ in SMALL]
    k = len(names)

    def body(*refs):
        p_ref = refs[0]
        ins = [dict(zip(names, refs[1 + i * k:1 + (i + 1) * k])) for i in range(3)]
        outs = [dict(zip(names, refs[1 + (3 + i) * k:1 + (4 + i) * k])) for i in range(4)]
        gsum = refs[-1]
        g = p_ref[0]
        for i in range(1, N_DEV):
            g = g + p_ref[i]
        gsum[...] = g
        for name, shape, base in SMALL:
            if name == "w_spatial":
                where = [(slice(base, base + 512), (slice(None), slice(None)))]
            else:
                where = [(slice(row, row + 1), (slice(i, i + 1), sl)) for row, i, sl in _segments(shape, base)]
            for rows, at in where:
                g_ = gsum[rows, :]
                d_, m_, v_ = _adam_math(ins[0][name][at], g_, ins[1][name][at], ins[2][name][at])
                for o, val in zip(outs, (g_, d_, m_, v_)):
                    o[name][at] = val

    args = [gathered] + [d[n] for d in (w, m, v) for n in names]
    out_shapes = [jax.ShapeDtypeStruct(shape, f32) for _ in range(4) for _, shape, _ in SMALL]
    outs = pl.pallas_call(body, name="small_update", out_shape=out_shapes, scratch_shapes=[pltpu.VMEM((SMALL_ROWS, 128), f32)])(*args)
    return [dict(zip(names, outs[i * k:(i + 1) * k])) for i in range(4)]


def _cols_full(g):
    return jnp.moveaxis(g, 0, -2).reshape(g.shape[1:-1] + (N_DEV * g.shape[-1],))


def _cols_parts(full):
    n = full.shape[-1] // N_DEV
    return jnp.moveaxis(full.reshape(full.shape[:-1] + (N_DEV, n)), -2, 0)


def kernel(x, mem, norm1_g, w_in, ln_v_g, ln_v_b, w_spatial, b_spatial, lb_logits, hgrn_norm_g, mem_norm_g, w_mem_kv, w_branch, w_out, norm2_g, w_up, conv_w, conv_b, w_down, final_g, loss_target, m_norm1_g, m_w_in, m_ln_v_g, m_ln_v_b, m_w_spatial, m_b_spatial, m_lb_logits, m_hgrn_norm_g, m_mem_norm_g, m_w_mem_kv, m_w_branch, m_w_out, m_norm2_g, m_w_up, m_conv_w, m_conv_b, m_w_down, m_final_g, v_norm1_g, v_w_in, v_ln_v_g, v_ln_v_b, v_w_spatial, v_b_spatial, v_lb_logits, v_hgrn_norm_g, v_mem_norm_g, v_w_mem_kv, v_w_branch, v_w_out, v_norm2_g, v_w_up, v_conv_w, v_conv_b, v_w_down, v_final_g):
    given = dict(locals())
    order = ("norm1_g", "w_in", "ln_v_g", "ln_v_b", "w_spatial", "b_spatial", "lb_logits", "hgrn_norm_g", "mem_norm_g",
             "w_mem_kv", "w_branch", "w_out", "norm2_g", "w_up", "conv_w", "conv_b", "w_down", "final_g")
    groups = dict(a=("w_in",), b=("w_mem_kv", "w_branch", "w_out"), c=("w_up", "conv_w", "w_down"))

    wire = {n: given[n][0].astype(f32 if n == "conv_w" else bf16) for ns in groups.values() for n in ns}
    g_in = _all_gather([wire["w_in"]], "gather_w_in")[0]
    late = groups["b"] + groups["c"]
    w_in_full = _cols_full(g_in)
    w_in_full, rest_wire = lax.optimization_barrier((w_in_full, [wire[n] for n in late]))
    rest = _seq_exchange(rest_wire, True, "gather_rest", 1)

    def late_b(after):
        _, (kv_, br_, out_) = lax.optimization_barrier((after, tuple(rest[0:3])))
        br_ = _cols_full(br_)
        return dict(w_mem_kv=kv_.reshape(D_MODEL, 2 * 512), w_branch=[br_[n] for n in range(3)], w_out=out_.reshape(D_MODEL, D_MODEL))

    def late_c(after):
        _, (up_, cw_, down_) = lax.optimization_barrier((after, tuple(rest[3:6])))
        return dict(w_up=_cols_full(up_), conv_w=_cols_full(cw_), w_down=down_.reshape(D_FF, D_MODEL))

    to_parts = dict(w_in=lambda g_: g_, w_up=lambda g_: g_, conv_w=_cols_parts,
                    w_branch=lambda g_: _cols_parts(jnp.stack(g_)).reshape(N_DEV, -1, 128),
                    w_mem_kv=lambda g_: g_.reshape(N_DEV, -1, 2 * 512), w_out=lambda g_: g_.reshape(N_DEV, -1, D_MODEL),
                    w_down=lambda g_: g_.reshape(N_DEV, -1, D_MODEL))
    scatters = {}

    def send(tag, grads_):
        parts = [to_parts[n](grads_[n]) for n in groups[tag]]
        scatters[tag] = _seq_exchange(parts, False, f"scatter_{tag}", dict(a=2, b=4, c=5)[tag])
        return jnp.zeros((8, 128), f32)

    small_2d = lambda prefix: {n: given[prefix + n].reshape(shape) for n, shape, _ in SMALL}
    p = small_2d("")
    p["w_spatial"] = w_spatial[0]
    updates = {}

    def update(tag):
        for n, parts in zip(groups[tag], scatters[tag]):
            two_d = (-1, given[n].shape[-1])
            updates[n] = _reduce_adamw(parts, *[given[pre + n].reshape(two_d) for pre in ("", "m_", "v_")], "adamw_" + n)

    def settle(chain):
        update("c")
        update("b")
        early = groups["c"] + groups["b"]
        chain, tied = lax.optimization_barrier((chain, [updates[n] for n in early]))
        updates.update(zip(early, tied))
        return chain

    loss_part, grad_x, gs = _local_step(x, mem, loss_target, p, w_in_full, late_b, late_c, send, settle)
    loss = lax.psum(loss_part[0, 0], ("x", "y", "c"))

    gathered = _seq_exchange([_pack_small(gs)], True, "gather_small", 3)[0]

    update("a")
    grads, delta, new_m, new_v = {}, {}, {}, {}
    for n, res in updates.items():
        grads[n], delta[n], new_m[n], new_v[n] = [r.reshape(given[n].shape) for r in res]

    for dst, res in zip((grads, delta, new_m, new_v), _small_update(gathered, small_2d(""), small_2d("m_"), small_2d("v_"))):
        for n, _, _ in SMALL:
            dst[n] = res[n].reshape(given[n].shape)

    return (loss, grad_x, *[grads[n] for n in order], *[delta[n] for n in order], *[new_m[n] for n in order],
            *[new_v[n] for n in order])
```

```python
import functools

import jax
import jax.numpy as jnp
from jax import lax
from jax.experimental import pallas as pl
from jax.experimental.pallas import tpu as pltpu
from jax.experimental.pallas import tpu_sc as plsc

f32 = jnp.float32
bf16 = jnp.bfloat16

N_DEV = 8
D_MODEL = 1024
EPS = 1e-6
GM_CHUNK = 128
HG_CHUNK = 64
HEAD = 128
N_HEAD = 4
MEM_LEN = 256
D_FF = 2816
IN_WIDTH = 6656
C_ZU, C_HQ, C_HF, C_HI, C_HG, C_XQ, C_GL = 0, 1024, 1536, 2048, 2560, 3072, 3584
ADAM_LR, ADAM_B1, ADAM_B2, ADAM_EPS, ADAM_WD, ADAM_STEP = 0.001, 0.9, 0.999, 1e-08, 0.01, 10
VMEM_LIMIT = 56 * 1024 * 1024
MESH = pl.DeviceIdType.MESH


def _pick(n, cands):
    for c in cands:
        if n % c == 0:
            return c
    return n


def _call(body, name, grid, in_specs, out_specs, out_shape, scratch=(), sem=None, **cp):
    params = dict(vmem_limit_bytes=VMEM_LIMIT, **cp)
    if sem is not None:
        params["dimension_semantics"] = sem
    return pl.pallas_call(
        body, name=name, grid=grid, in_specs=in_specs, out_specs=out_specs, out_shape=out_shape,
        scratch_shapes=list(scratch), compiler_params=pltpu.CompilerParams(**params))


_DN = {"nn": (((1,), (0,)), ((), ())), "nt": (((1,), (1,)), ((), ())), "tn": (((0,), (0,)), ((), ()))}


def _raw_dot(a, b, mode):
    return lax.dot_general(a.astype(bf16), b.astype(bf16), _DN[mode], preferred_element_type=f32)


@jax.custom_vjp
def _dot_nn(a, b):
    return _raw_dot(a, b, "nn")


_dot_nn.defvjp(lambda a, b: (_raw_dot(a, b, "nn"), (a, b)),
               lambda r, g: (_raw_dot(g, r[1], "nt"), _raw_dot(r[0], g, "tn")))


@jax.custom_vjp
def _dot_nt(a, b):
    return _raw_dot(a, b, "nt")


_dot_nt.defvjp(lambda a, b: (_raw_dot(a, b, "nt"), (a, b)),
               lambda r, g: (_raw_dot(g, r[1], "nn"), _raw_dot(g, r[0], "tn")))


@jax.custom_vjp
def _dot_tn(a, b):
    return _raw_dot(a, b, "tn")


_dot_tn.defvjp(lambda a, b: (_raw_dot(a, b, "tn"), (a, b)),
               lambda r, g: (_raw_dot(r[1], g, "nt"), _raw_dot(r[0], g, "nn")))


def _tri(n, lower):
    r = lax.broadcasted_iota(jnp.int32, (n, n), 0)
    c = lax.broadcasted_iota(jnp.int32, (n, n), 1)
    return ((c <= r) if lower else (c >= r)).astype(f32)


def _sel_dot(sel, x, mode, x_first=False):
    hi = x.astype(bf16)
    rest = x - hi.astype(f32)
    mid = rest.astype(bf16)
    lo = (rest - mid.astype(f32)).astype(bf16)
    sel = sel.astype(bf16)
    dot = lambda piece: lax.dot_general(*((piece, sel) if x_first else (sel, piece)), _DN[mode], preferred_element_type=f32)
    return dot(hi) + dot(mid) + dot(lo)


def _egrad(fn, x, ct):
    return jax.vjp(fn, x)[1](ct)[0]


def _mm(a, b, mode, out_dtype, name, tm, tn, tk=None, residual=None, shard=None):
    if mode == "nn":
        (M, K), (_, N) = a.shape, b.shape
    elif mode == "nt":
        (M, K), (N, _) = a.shape, b.shape
    else:
        (K, M), (_, N) = a.shape, b.shape
    tm, tn = min(tm, M), min(tn, N)
    tk = K if tk is None else min(tk, K)
    assert M % tm == 0 and N % tn == 0 and K % tk == 0, (name, M, N, K, tm, tn, tk)
    nk = K // tk

    def body(*refs):
        acc_ref = refs[-1] if nk > 1 else None
        refs = refs[:-1] if nk > 1 else refs
        if residual is None:
            a_ref, b_ref, o_ref = refs
        else:
            a_ref, b_ref, r_ref, o_ref = refs

        def finish(r):
            if residual is not None:
                r = r + r_ref[...]
            if shard is None:
                o_ref[...] = r.astype(out_dtype)
            else:
                for s in range(tn // shard):
                    o_ref[s] = r[:, s * shard:(s + 1) * shard].astype(out_dtype)

        part = _raw_dot(a_ref[...], b_ref[...], mode)
        if nk == 1:
            finish(part)
            return
        k = pl.program_id(2)

        @pl.when(k == 0)
        def _():
            acc_ref[...] = part

        @pl.when((k > 0) & (k < nk - 1))
        def _():
            acc_ref[...] += part

        @pl.when(k == nk - 1)
        def _():
            finish(acc_ref[...] + part)

    a_spec = {"nn": pl.BlockSpec((tm, tk), lambda i, j, k: (i, k)),
              "nt": pl.BlockSpec((tm, tk), lambda i, j, k: (i, k)),
              "tn": pl.BlockSpec((tk, tm), lambda i, j, k: (k, i))}[mode]
    b_spec = {"nn": pl.BlockSpec((tk, tn), lambda i, j, k: (k, j)),
              "nt": pl.BlockSpec((tn, tk), lambda i, j, k: (j, k)),
              "tn": pl.BlockSpec((tk, tn), lambda i, j, k: (k, j))}[mode]
    o_spec = pl.BlockSpec((tm, tn), lambda i, j, k: (i, j))
    in_specs = [a_spec, b_spec] + ([o_spec] if residual is not None else [])
    args = (a, b) + ((residual,) if residual is not None else ())
    out_shape = jax.ShapeDtypeStruct((M, N), out_dtype)
    if shard is not None:
        assert residual is None and tn % shard == 0
        o_spec = pl.BlockSpec((tn // shard, tm, shard), lambda i, j, k: (j, i, 0))
        out_shape = jax.ShapeDtypeStruct((N // shard, M, shard), out_dtype)
    return _call(body, name, (M // tm, N // tn, nk), in_specs, o_spec, out_shape,
                 scratch=[pltpu.VMEM((tm, tn), f32)] if nk > 1 else [], sem=("parallel", "parallel", "arbitrary"))(*args)


def _rms_fwd(x, g, name):
    R, Dd = x.shape
    tr = _pick(R, (512, 256, 128))

    def body(x_ref, g_ref, o_ref):
        xf = x_ref[...]
        y = xf * lax.rsqrt(jnp.mean(xf * xf, axis=-1, keepdims=True) + EPS)
        o_ref[...] = (y * g_ref[...]).astype(bf16)

    return _call(body, name, (R // tr,), [pl.BlockSpec((tr, Dd), lambda i: (i, 0)), pl.BlockSpec((1, Dd), lambda i: (0, 0))],
                 pl.BlockSpec((tr, Dd), lambda i: (i, 0)), jax.ShapeDtypeStruct((R, Dd), bf16), sem=("parallel",))(x, g)


def _rms_bwd(x, g, dh, name, residual=None):
    R, Dd = x.shape
    tr = _pick(R, (512, 256, 128))

    def body(*refs):
        if residual is None:
            x_ref, g_ref, dh_ref, dx_ref, dg_ref = refs
        else:
            x_ref, g_ref, dh_ref, r_ref, dx_ref, dg_ref = refs
        xf = x_ref[...]
        rs = lax.rsqrt(jnp.mean(xf * xf, axis=-1, keepdims=True) + EPS)
        y = xf * rs
        dh_ = dh_ref[...].astype(f32)
        dy = dh_ * g_ref[...]
        dx = rs * (dy - y * jnp.mean(dy * y, axis=-1, keepdims=True))
        if residual is not None:
            dx = dx + r_ref[...]
        dx_ref[...] = dx

        @pl.when(pl.program_id(0) == 0)
        def _():
            dg_ref[...] = jnp.zeros_like(dg_ref)

        dg_ref[...] += jnp.sum(dh_ * y, axis=0, keepdims=True)

    row = pl.BlockSpec((tr, Dd), lambda i: (i, 0))
    vec = pl.BlockSpec((1, Dd), lambda i: (0, 0))
    in_specs = [row, vec, row] + ([row] if residual is not None else [])
    args = (x, g, dh) + ((residual,) if residual is not None else ())
    return _call(body, name, (R // tr,), in_specs, (row, vec),
                 (jax.ShapeDtypeStruct((R, Dd), f32), jax.ShapeDtypeStruct((1, Dd), f32)), sem=("arbitrary",))(*args)


def _final_loss(x2, g, target):
    R, Dd = x2.shape
    tr = _pick(R, (512, 256, 128))

    def body(x_ref, g_ref, t_ref, loss_ref, dx_ref, dg_ref):
        xf = x_ref[...]
        rs = lax.rsqrt(jnp.mean(xf * xf, axis=-1, keepdims=True) + EPS)
        y = xf * rs
        err = y * g_ref[...] - t_ref[...]
        dh_ = err * (1.0 / Dd)
        dy = dh_ * g_ref[...]
        dx_ref[...] = rs * (dy - y * jnp.mean(dy * y, axis=-1, keepdims=True))

        @pl.when(pl.program_id(0) == 0)
        def _():
            dg_ref[...] = jnp.zeros_like(dg_ref)
            loss_ref[...] = jnp.zeros_like(loss_ref)

        dg_ref[...] += jnp.sum(dh_ * y, axis=0, keepdims=True)
        part = jnp.sum(jnp.mean(err * err, axis=-1, keepdims=True), axis=0, keepdims=True)
        loss_ref[...] += 0.5 * part

    row = pl.BlockSpec((tr, Dd), lambda i: (i, 0))
    vec = pl.BlockSpec((1, Dd), lambda i: (0, 0))
    return _call(body, "final_loss", (R // tr,), [row, vec, row], (pl.BlockSpec((1, 128), lambda i: (0, 0)), row, vec),
                 (jax.ShapeDtypeStruct((1, 128), f32), jax.ShapeDtypeStruct((R, Dd), f32), jax.ShapeDtypeStruct((1, Dd), f32)),
                 sem=("arbitrary",))(x2, g, target)


def _gmlp_parts(zuv, ln_g, ln_b):
    zu, zv = zuv[:, :512], zuv[:, 512:]
    u = jax.nn.gelu(zu)
    v = jax.nn.gelu(zv)
    mu = jnp.mean(v, axis=-1, keepdims=True)
    rs = lax.rsqrt(jnp.mean(jnp.square(v - mu), axis=-1, keepdims=True) + EPS)
    xh = (v - mu) * rs
    return zu, zv, u, xh, rs, xh * ln_g + ln_b


def _gmlp_fwd(proj, ln_g, ln_b, w_s, b_st):
    T = proj.shape[0]

    def body(p_ref, g_ref, b_ref, w_ref, bs_ref, o_ref):
        _, _, u, _, _, vn = _gmlp_parts(p_ref[...].astype(f32), g_ref[...], b_ref[...])
        causal = _tri(GM_CHUNK, True) > 0
        for gi in range(N_HEAD):
            sl = slice(gi * HEAD, (gi + 1) * HEAD)
            w = jnp.where(causal, w_ref[gi], 0.0)
            mixed = _raw_dot(w, vn[:, sl], "nn") + bs_ref[:, gi:gi + 1]
            o_ref[:, sl] = (u[:, sl] * mixed).astype(bf16)

    vec = pl.BlockSpec((1, 512), lambda i: (0, 0))
    return _call(body, "gmlp_fwd", (T // GM_CHUNK,),
                 [pl.BlockSpec((GM_CHUNK, 1024), lambda i: (i, 0)), vec, vec,
                  pl.BlockSpec((N_HEAD, GM_CHUNK, GM_CHUNK), lambda i: (0, 0, 0)), pl.BlockSpec((GM_CHUNK, 128), lambda i: (0, 0))],
                 pl.BlockSpec((GM_CHUNK, 512), lambda i: (i, 0)), jax.ShapeDtypeStruct((T, 512), bf16), sem=("parallel",))(
        proj, ln_g, ln_b, w_s, b_st)


def _gmlp_bwd(proj, ln_g, ln_b, w_s, b_st, da):
    T = proj.shape[0]

    def body(p_ref, g_ref, b_ref, w_ref, bs_ref, da_ref, dp_ref, dg_ref, db_ref, dw_ref, dbs_ref):
        zu, zv, u, xh, rs, vn = _gmlp_parts(p_ref[...].astype(f32), g_ref[...], b_ref[...])
        causal = _tri(GM_CHUNK, True) > 0
        sub = lax.broadcasted_iota(jnp.int32, (8, GM_CHUNK), 0)
        ones = jnp.ones((8, HEAD), f32)
        dout = da_ref[...].astype(f32)

        @pl.when(pl.program_id(0) == 0)
        def _():
            for r in (dg_ref, db_ref, dw_ref, dbs_ref):
                r[...] = jnp.zeros_like(r)

        du, dvn, dbs = [], [], jnp.zeros((8, GM_CHUNK), f32)
        for gi in range(N_HEAD):
            sl = slice(gi * HEAD, (gi + 1) * HEAD)
            w = jnp.where(causal, w_ref[gi], 0.0)
            mixed = _raw_dot(w, vn[:, sl], "nn") + bs_ref[:, gi:gi + 1]
            du.append(dout[:, sl] * mixed)
            dm = dout[:, sl] * u[:, sl]
            row_sums = _sel_dot(ones, dm, "nt")
            dbs = dbs + jnp.where(sub == gi, row_sums, 0.0)
            dw_ref[gi] += jnp.where(causal, _raw_dot(dm, vn[:, sl], "nt"), 0.0)
            dvn.append(_raw_dot(w, dm, "tn"))
        dbs_ref[...] += dbs
        du = jnp.concatenate(du, axis=-1)
        dvn = jnp.concatenate(dvn, axis=-1)
        dg_ref[...] += jnp.sum(dvn * xh, axis=0, keepdims=True)
        db_ref[...] += jnp.sum(dvn, axis=0, keepdims=True)
        dxh = dvn * g_ref[...]
        dv = rs * (dxh - jnp.mean(dxh, axis=-1, keepdims=True) - xh * jnp.mean(dxh * xh, axis=-1, keepdims=True))
        dp_ref[:, :512] = _egrad(jax.nn.gelu, zu, du).astype(bf16)
        dp_ref[:, 512:] = _egrad(jax.nn.gelu, zv, dv).astype(bf16)

    vec = pl.BlockSpec((1, 512), lambda i: (0, 0))
    wsp = pl.BlockSpec((N_HEAD, GM_CHUNK, GM_CHUNK), lambda i: (0, 0, 0))
    return _call(body, "gmlp_bwd", (T // GM_CHUNK,),
                 [pl.BlockSpec((GM_CHUNK, 1024), lambda i: (i, 0)), vec, vec, wsp, pl.BlockSpec((GM_CHUNK, 128), lambda i: (0, 0)),
                  pl.BlockSpec((GM_CHUNK, 512), lambda i: (i, 0))],
                 (pl.BlockSpec((GM_CHUNK, 1024), lambda i: (i, 0)), vec, vec, wsp, pl.BlockSpec((8, GM_CHUNK), lambda i: (0, 0))),
                 (jax.ShapeDtypeStruct((T, 1024), bf16), jax.ShapeDtypeStruct((1, 512), f32), jax.ShapeDtypeStruct((1, 512), f32),
                  jax.ShapeDtypeStruct((N_HEAD, GM_CHUNK, GM_CHUNK), f32), jax.ShapeDtypeStruct((8, GM_CHUNK), f32)),
                 sem=("arbitrary",))(proj, ln_g, ln_b, w_s, b_st, da)


HG_SUB = 8
HG_NSUB = HG_CHUNK // HG_SUB


def _two_level_matrix():
    r = lax.broadcasted_iota(jnp.int32, (2 * HG_CHUNK, HG_CHUNK), 0)
    c = lax.broadcasted_iota(jnp.int32, (2 * HG_CHUNK, HG_CHUNK), 1)
    t = jnp.where(r < HG_CHUNK, r, r - HG_CHUNK)
    local = (r < HG_CHUNK) & (t // HG_SUB == c // HG_SUB) & (c <= t)
    before = (r >= HG_CHUNK) & (c < (t // HG_SUB) * HG_SUB)
    return (local | before).astype(f32)


def _two_level_sums(x):
    two = _sel_dot(_two_level_matrix(), x, "nn")
    return two[:HG_CHUNK], two[HG_CHUNK:]


@jax.custom_vjp
def _two_level_cumsum(x):
    return _two_level_sums(x)


_two_level_cumsum.defvjp(
    lambda x: (_two_level_sums(x), None),
    lambda _, g: (_sel_dot(_two_level_matrix(), jnp.concatenate(g, axis=0), "tn"),))


def _tile_matrix():
    s = lax.broadcasted_iota(jnp.int32, (HG_SUB, HG_CHUNK), 0)
    j = lax.broadcasted_iota(jnp.int32, (HG_SUB, HG_CHUNK), 1)
    return (j % HG_SUB == s).astype(f32)


@jax.custom_vjp
def _tile_lanes(x):
    return _sel_dot(_tile_matrix(), x, "nn", x_first=True)


_tile_lanes.defvjp(
    lambda x: (_sel_dot(_tile_matrix(), x, "nn", x_first=True), None),
    lambda _, g: (_sel_dot(_tile_matrix(), g, "nt", x_first=True),))


def _block_rows(x):
    k = x.shape[-1]
    return jnp.broadcast_to(x.reshape(HG_NSUB, 1, HG_SUB, k), (HG_NSUB, HG_SUB, HG_SUB, k)).reshape(HG_CHUNK, HG_SUB, k)


def _hgrn_chunk(st0, q_raw, f_raw, i_raw, g_raw, l0, l1, ng):
    C, SUB = HG_CHUNK, HG_SUB
    lb = jax.nn.sigmoid(l0 - l1)
    fg = lb + (1.0 - lb) * jax.nn.sigmoid(f_raw)
    kk = 1.0 - fg
    qf = jax.nn.silu(q_raw)
    al, base = _two_level_cumsum(jnp.log(fg))
    a = al + base
    row = lax.broadcasted_iota(jnp.int32, (C, HEAD), 0)
    a_last = jnp.sum(jnp.where(row == C - 1, a, 0.0), axis=0, keepdims=True)
    inter = _dot_nt(qf * jnp.exp(a), st0)
    qt = qf * jnp.exp(al)
    rb = lax.broadcasted_iota(jnp.int32, (C, C), 0) // SUB
    cb = lax.broadcasted_iota(jnp.int32, (C, C), 1) // SUB
    scores = jnp.zeros((C, C), f32)
    for i in range(1, HG_NSUB):
        base_i = jnp.sum(jnp.where(row == i * SUB, base, 0.0), axis=0, keepdims=True)
        kt = kk * jnp.exp(jnp.minimum(base_i - a, 0.0))
        scores = scores + jnp.where((rb == i) & (cb < i), _dot_nt(qt, kt), 0.0)
    t_i = lax.broadcasted_iota(jnp.int32, (C, SUB, HEAD), 0) % SUB
    s_i = lax.broadcasted_iota(jnp.int32, (C, SUB, HEAD), 1)
    decay = jnp.exp(jnp.where(s_i <= t_i, al[:, None, :] - _block_rows(al), -jnp.inf))
    diag = jnp.sum(qf[:, None, :] * decay * _block_rows(kk), axis=-1)
    scores = scores + jnp.where(rb == cb, _tile_lanes(diag), 0.0)
    o = inter + _dot_nn(scores, i_raw)
    st1 = jnp.exp(a_last) * st0 + _dot_tn(i_raw, kk * jnp.exp(a_last - a))
    on = o * lax.rsqrt(jnp.mean(o * o, axis=-1, keepdims=True) + EPS) * ng
    return st1, on * jax.nn.silu(g_raw)


def _hgrn_specs(S, Bl, rev):
    N = S // HG_CHUNK
    chunk = (lambda n: N - 1 - n) if rev else (lambda n: n)
    col = lambda c0: pl.BlockSpec((Bl, HG_CHUNK, 512), lambda n: (0, chunk(n), c0 // 512))
    st = pl.BlockSpec((Bl, N_HEAD, 1, HEAD, HEAD), lambda n: (0, 0, chunk(n), 0, 0))
    full = lambda *s: pl.BlockSpec(s, functools.partial(lambda n, nd: (0,) * nd, nd=len(s)))
    return N, col, st, full


def _hgrn_fwd(proj, lb_logits, ng, Bl, S):
    N, col, st, full = _hgrn_specs(S, Bl, False)

    def body(q_ref, f_ref, i_ref, g_ref, l_ref, ng_ref, o_ref, st_ref, state):
        @pl.when(pl.program_id(0) == 0)
        def _():
            state[...] = jnp.zeros_like(state)

        for b in range(Bl):
            for h in range(N_HEAD):
                sl = slice(h * HEAD, (h + 1) * HEAD)
                st0 = state[b, h]
                st_ref[b, h, 0] = st0
                st1, out = _hgrn_chunk(st0, *[r[b, :, sl].astype(f32) for r in (q_ref, f_ref, i_ref, g_ref)],
                                       l_ref[0:1, sl], l_ref[1:2, sl], ng_ref[...])
                state[b, h] = st1
                o_ref[b, :, sl] = out.astype(bf16)

    return _call(body, "hgrn_fwd", (N,), [col(C_HQ), col(C_HF), col(C_HI), col(C_HG), full(2, 512), full(1, HEAD)],
                 (col(0), st),
                 (jax.ShapeDtypeStruct((Bl, S, 512), bf16), jax.ShapeDtypeStruct((Bl, N_HEAD, N, HEAD, HEAD), f32)),
                 scratch=[pltpu.VMEM((Bl, N_HEAD, HEAD, HEAD), f32)], sem=("arbitrary",))(
        proj, proj, proj, proj, lb_logits, ng)


def _hgrn_bwd(proj, lb_logits, ng, states, db, Bl, S):
    N, col, st, full = _hgrn_specs(S, Bl, True)

    def body(q_ref, f_ref, i_ref, g_ref, l_ref, ng_ref, st_ref, db_ref,
             dq_ref, df_ref, di_ref, dg_ref, dl_ref, dng_ref, dstate):
        @pl.when(pl.program_id(0) == 0)
        def _():
            dstate[...] = jnp.zeros_like(dstate)
            dl_ref[...] = jnp.zeros_like(dl_ref)
            dng_ref[...] = jnp.zeros_like(dng_ref)

        for b in range(Bl):
            for h in range(N_HEAD):
                sl = slice(h * HEAD, (h + 1) * HEAD)
                _, vjp = jax.vjp(_hgrn_chunk, st_ref[b, h, 0], *[r[b, :, sl].astype(f32) for r in (q_ref, f_ref, i_ref, g_ref)],
                                 l_ref[0:1, sl], l_ref[1:2, sl], ng_ref[...])
                dst0, dq, df, di, dg, dl0, dl1, dng = vjp((dstate[b, h], db_ref[b, :, sl].astype(f32)))
                dstate[b, h] = dst0
                dq_ref[b, :, sl] = dq.astype(bf16)
                df_ref[b, :, sl] = df.astype(bf16)
                di_ref[b, :, sl] = di.astype(bf16)
                dg_ref[b, :, sl] = dg.astype(bf16)
                dl_ref[0:1, sl] += dl0
                dl_ref[1:2, sl] += dl1
                dng_ref[b, h] += dng

    return _call(body, "hgrn_bwd", (N,),
                 [col(C_HQ), col(C_HF), col(C_HI), col(C_HG), full(2, 512), full(1, HEAD), st, col(0)],
                 (*[col(0)] * 4, full(2, 512), full(Bl, N_HEAD, 1, HEAD)),
                 (*[jax.ShapeDtypeStruct((Bl, S, 512), bf16)] * 4, jax.ShapeDtypeStruct((2, 512), f32),
                  jax.ShapeDtypeStruct((Bl, N_HEAD, 1, HEAD), f32)),
                 scratch=[pltpu.VMEM((Bl, N_HEAD, HEAD, HEAD), f32)], sem=("arbitrary",))(
        proj, proj, proj, proj, lb_logits, ng, states, db)


def _attn_probs(q, k):
    s = _raw_dot(q, k, "nt") * (HEAD ** -0.5)
    e = jnp.exp(s - jnp.max(s, axis=-1, keepdims=True))
    return e / jnp.sum(e, axis=-1, keepdims=True)


def _attn_specs(S, tq):
    nq = S // tq
    q = pl.BlockSpec((tq, 512), lambda b, i: (b * nq + i, C_XQ // 512))
    kv = pl.BlockSpec((1, MEM_LEN, 1024), lambda b, i: (b, 0, 0))
    o = pl.BlockSpec((tq, 512), lambda b, i: (b * nq + i, 0))
    return nq, q, kv, o


def _attn_fwd(proj, kv, Bl, S):
    tq = _pick(S, (512, 256, 128))
    nq, qs, kvs, os_ = _attn_specs(S, tq)

    def body(q_ref, kv_ref, o_ref):
        for h in range(N_HEAD):
            sl = slice(h * HEAD, (h + 1) * HEAD)
            p = _attn_probs(q_ref[:, sl], kv_ref[0, :, sl])
            o_ref[:, sl] = _raw_dot(p, kv_ref[0, :, 512 + h * HEAD:512 + (h + 1) * HEAD], "nn").astype(bf16)

    return _call(body, "attn_fwd", (Bl, nq), [qs, kvs], os_, jax.ShapeDtypeStruct((Bl * S, 512), bf16),
                 sem=("parallel", "parallel"))(proj, kv)


def _attn_bwd(proj, kv, dc, Bl, S):
    tq = _pick(S, (512, 256, 128))
    nq, qs, kvs, os_ = _attn_specs(S, tq)

    def body(q_ref, kv_ref, do_ref, dq_ref, dkv_ref):
        @pl.when(pl.program_id(1) == 0)
        def _():
            dkv_ref[...] = jnp.zeros_like(dkv_ref)

        for h in range(N_HEAD):
            sl = slice(h * HEAD, (h + 1) * HEAD)
            vsl = slice(512 + h * HEAD, 512 + (h + 1) * HEAD)
            q, k, v, do = q_ref[:, sl], kv_ref[0, :, sl], kv_ref[0, :, vsl], do_ref[:, sl]
            p = _attn_probs(q, k)
            dkv_ref[0, :, vsl] += _raw_dot(p, do, "tn")
            dp = _raw_dot(do, v, "nt")
            ds = p * (dp - jnp.sum(dp * p, axis=-1, keepdims=True)) * (HEAD ** -0.5)
            dq_ref[:, sl] = _raw_dot(ds, k, "nn").astype(bf16)
            dkv_ref[0, :, sl] += _raw_dot(ds, q, "tn")

    return _call(body, "attn_bwd", (Bl, nq), [qs, kvs, os_], (os_, kvs),
                 (jax.ShapeDtypeStruct((Bl * S, 512), bf16), jax.ShapeDtypeStruct((Bl, MEM_LEN, 1024), f32)),
                 sem=("arbitrary", "arbitrary"))(proj, kv, dc)


def _merge_specs(tm, tn):
    br = pl.BlockSpec((tm, 512), lambda i, j: (i, 0))
    w = pl.BlockSpec((512, tn), lambda i, j: (0, j))
    gl = [pl.BlockSpec((tm, tn), functools.partial(lambda i, j, n: (i, (C_GL + n * D_MODEL) // tn + j), n=n)) for n in range(3)]
    return [br, br, br, w, w, w, *gl]


def _merge_fwd(branches, wb, proj):
    T = proj.shape[0]
    tm, tn = _pick(T, (1024, 512, 256, 128)), 512

    def body(a_ref, b_ref, c_ref, w0, w1, w2, g0, g1, g2, o_ref):
        acc = jnp.zeros((tm, tn), f32)
        for x_ref, w_ref, g_ref in ((a_ref, w0, g0), (b_ref, w1, g1), (c_ref, w2, g2)):
            acc = acc + jax.nn.sigmoid(g_ref[...].astype(f32)) * _raw_dot(x_ref[...], w_ref[...], "nn")
        o_ref[...] = acc.astype(bf16)

    return _call(body, "merge_fwd", (T // tm, D_MODEL // tn), _merge_specs(tm, tn), pl.BlockSpec((tm, tn), lambda i, j: (i, j)),
                 jax.ShapeDtypeStruct((T, D_MODEL), bf16), sem=("parallel", "parallel"))(*branches, *wb, proj, proj, proj)


def _merge_bwd(branches, wb, proj, dmerged):
    T = proj.shape[0]
    tm, tn = _pick(T, (1024, 512, 256, 128)), 512

    def body(a_ref, b_ref, c_ref, w0, w1, w2, g0, g1, g2, dm_ref, dgl_ref, d0, d1, d2):
        dm = dm_ref[...]
        for n, (x_ref, w_ref, g_ref, d_ref) in enumerate(((a_ref, w0, g0, d0), (b_ref, w1, g1, d1), (c_ref, w2, g2, d2))):
            up = _raw_dot(x_ref[...], w_ref[...], "nn")
            logits = g_ref[...].astype(f32)
            dgl_ref[n] = _egrad(jax.nn.sigmoid, logits, dm * up).astype(bf16)
            d_ref[...] = (dm * jax.nn.sigmoid(logits)).astype(bf16)

    blk = pl.BlockSpec((tm, tn), lambda i, j: (i, j))
    sh = jax.ShapeDtypeStruct((T, D_MODEL), bf16)
    outs = _call(body, "merge_bwd", (T // tm, D_MODEL // tn), [*_merge_specs(tm, tn), blk],
                 (pl.BlockSpec((3, tm, tn), lambda i, j: (0, i, j)), blk, blk, blk),
                 (jax.ShapeDtypeStruct((3, T, D_MODEL), bf16), sh, sh, sh),
                 sem=("parallel", "parallel"))(*branches, *wb, proj, proj, proj, dmerged)
    return outs[0], outs[1:]


CONV_TC = 256


def _shift_down(a, k):
    row = lax.broadcasted_iota(jnp.int32, a.shape, 0)
    return jnp.where(row >= k, pltpu.roll(a, k, 0), 0.0)


def _shift_up(a, k):
    n = a.shape[0]
    row = lax.broadcasted_iota(jnp.int32, a.shape, 0)
    return jnp.where(row < n - k, pltpu.roll(a, n - k, 0), 0.0)


def _conv_pre(a, cw, cb):
    return cb + cw[0:1] * _shift_down(a, 2) + cw[1:2] * _shift_down(a, 1) + cw[2:3] * a


def _conv_fwd(ab, cw, cb, Bl, S):
    nc = D_FF // CONV_TC

    def body(a_ref, b_ref, cw_ref, cb_ref, o_ref):
        ac = _conv_pre(a_ref[0].astype(f32), cw_ref[...], cb_ref[...])
        o_ref[0] = (jax.nn.silu(ac) * b_ref[0].astype(f32)).astype(bf16)

    return _call(body, "conv_fwd", (Bl, nc),
                 [pl.BlockSpec((1, S, CONV_TC), lambda b, c: (b, 0, c)), pl.BlockSpec((1, S, CONV_TC), lambda b, c: (b, 0, nc + c)),
                  pl.BlockSpec((3, CONV_TC), lambda b, c: (0, c)), pl.BlockSpec((1, CONV_TC), lambda b, c: (0, c))],
                 pl.BlockSpec((1, S, CONV_TC), lambda b, c: (b, 0, c)), jax.ShapeDtypeStruct((Bl, S, D_FF), bf16),
                 sem=("parallel", "parallel"))(ab, ab, cw, cb)


def _conv_bwd(ab, cw, cb, dact, Bl, S):
    nc = D_FF // CONV_TC

    def body(a_ref, b_ref, cw_ref, cb_ref, d_ref, da_ref, db_ref, dcw_ref, dcb_ref):
        @pl.when(pl.program_id(1) == 0)
        def _():
            dcw_ref[...] = jnp.zeros_like(dcw_ref)
            dcb_ref[...] = jnp.zeros_like(dcb_ref)

        a, cw = a_ref[0].astype(f32), cw_ref[...]
        ac = _conv_pre(a, cw, cb_ref[...])
        dact_ = d_ref[0].astype(f32)
        db_ref[0] = (dact_ * jax.nn.silu(ac)).astype(bf16)
        dac = _egrad(jax.nn.silu, ac, dact_ * b_ref[0].astype(f32))
        da_ref[0] = (cw[2:3] * dac + cw[1:2] * _shift_up(dac, 1) + cw[0:1] * _shift_up(dac, 2)).astype(bf16)
        dcw_ref[0:1, :] += jnp.sum(dac * _shift_down(a, 2), axis=0, keepdims=True)
        dcw_ref[1:2, :] += jnp.sum(dac * _shift_down(a, 1), axis=0, keepdims=True)
        dcw_ref[2:3, :] += jnp.sum(dac * a, axis=0, keepdims=True)
        dcb_ref[...] += jnp.sum(dac, axis=0, keepdims=True)

    seq = pl.BlockSpec((1, S, CONV_TC), lambda c, b: (b, 0, c))
    return _call(body, "conv_bwd", (nc, Bl),
                 [seq, pl.BlockSpec((1, S, CONV_TC), lambda c, b: (b, 0, nc + c)), pl.BlockSpec((3, CONV_TC), lambda c, b: (0, c)),
                  pl.BlockSpec((1, CONV_TC), lambda c, b: (0, c)), seq],
                 (seq, seq, pl.BlockSpec((3, CONV_TC), lambda c, b: (0, c)), pl.BlockSpec((1, CONV_TC), lambda c, b: (0, c))),
                 (jax.ShapeDtypeStruct((Bl, S, D_FF), bf16), jax.ShapeDtypeStruct((Bl, S, D_FF), bf16),
                  jax.ShapeDtypeStruct((3, D_FF), f32), jax.ShapeDtypeStruct((1, D_FF), f32)),
                 sem=("arbitrary", "arbitrary"))(ab, ab, cw, cb, dact)


def _local_step(x, mem, target, p, w_in, late_b, late_c, send, settle):
    Bl, S, Dd = x.shape
    T = Bl * S
    x2d, t2d, mem2d = x.reshape(T, Dd), target.reshape(T, Dd), mem.reshape(Bl * MEM_LEN, Dd)
    b_st = jnp.pad(p["b_spatial"].T, ((0, 0), (0, 128 - N_HEAD)))
    lbl = p["lb_logits"]

    h = _rms_fwd(x2d, p["norm1_g"], "norm1_fwd")
    proj = _mm(h, w_in, "nn", bf16, "proj_fwd", 1024, 1664)
    a_out = _gmlp_fwd(proj, p["ln_v_g"], p["ln_v_b"], p["w_spatial"], b_st)
    proj3 = proj.reshape(Bl, S, IN_WIDTH)
    b_out, states = _hgrn_fwd(proj3, lbl, p["hgrn_norm_g"], Bl, S)
    b_out = b_out.reshape(T, 512)
    memn = _rms_fwd(mem2d, p["mem_norm_g"], "memnorm_fwd")
    w = late_b(b_out)
    wb = w["w_branch"]
    kv = _mm(memn, w["w_mem_kv"], "nn", f32, "kv_fwd", 512, 1024).reshape(Bl, MEM_LEN, 2 * 512)
    c_out = _attn_fwd(proj, kv, Bl, S)
    branches = (a_out, b_out, c_out)
    merged = _merge_fwd(branches, wb, proj)
    x1 = _mm(merged, w["w_out"], "nn", f32, "out_fwd", 1024, 1024, residual=x2d)
    h2 = _rms_fwd(x1, p["norm2_g"], "norm2_fwd")
    w.update(late_c(h2))
    ab = _mm(h2, w["w_up"], "nn", bf16, "up_fwd", 1024, 1408)
    act = _conv_fwd(ab.reshape(Bl, S, 2 * D_FF), w["conv_w"], p["conv_b"], Bl, S).reshape(T, D_FF)
    x2 = _mm(act, w["w_down"], "nn", f32, "down_fwd", 512, 1024, residual=x1)
    loss_part, dx2, g_final = _final_loss(x2, p["final_g"], t2d)

    g_w_down = _mm(act, dx2, "tn", bf16, "down_dw", 1408, 1024, 1024)
    dact = _mm(dx2, w["w_down"], "nt", bf16, "down_dx", 1024, 1408)
    da, db, g_conv_w, g_conv_b = _conv_bwd(ab.reshape(Bl, S, 2 * D_FF), w["conv_w"], p["conv_b"], dact.reshape(Bl, S, D_FF), Bl, S)
    dab = jnp.concatenate([da.reshape(T, D_FF), db.reshape(T, D_FF)], axis=-1)
    g_w_up = _mm(h2, dab, "tn", bf16, "up_dw", 512, 1408, 1024, shard=2 * D_FF // N_DEV)
    tok = send("c", dict(w_up=g_w_up, conv_w=g_conv_w, w_down=g_w_down))
    dh2 = _mm(dab, w["w_up"], "nt", f32, "up_dx", 1024, 1024, 1408)
    dx1, g_norm2 = _rms_bwd(x1, p["norm2_g"] + tok[0, 0], dh2, "norm2_bwd", residual=dx2)

    g_w_out = _mm(merged, dx1, "tn", bf16, "out_dw", 1024, 1024, 1024)
    dmerged = _mm(dx1, w["w_out"], "nt", f32, "out_dx", 1024, 1024)
    dgl, dup = _merge_bwd(branches, wb, proj, dmerged)
    g_w_branch = [_mm(branches[n], dup[n], "tn", bf16, f"branch_dw{n}", 512, 1024, 1024) for n in range(3)]
    dbr = [_mm(dup[n], wb[n], "nt", bf16, f"branch_dx{n}", 1024, 512) for n in range(3)]
    dxq, dkv = _attn_bwd(proj, kv, dbr[2], Bl, S)
    dkv = dkv.reshape(Bl * MEM_LEN, 2 * 512)
    g_w_kv = _mm(memn, dkv, "tn", bf16, "kv_dw", 1024, 1024, 512)
    tok = send("b", dict(w_mem_kv=g_w_kv, w_branch=g_w_branch, w_out=g_w_out))
    dmemn = _mm(dkv, w["w_mem_kv"], "nt", f32, "kv_dx", 512, 1024)
    _, g_mem_norm = _rms_bwd(mem2d, p["mem_norm_g"], dmemn, "memnorm_bwd")
    dzuv, g_ln_g, g_ln_b, g_w_sp, g_b_sp = _gmlp_bwd(proj, p["ln_v_g"] + tok[0, 0], p["ln_v_b"], p["w_spatial"], b_st, dbr[0])
    *dqfig, g_lbl, g_ng = _hgrn_bwd(proj3, lbl, p["hgrn_norm_g"], states, dbr[1].reshape(Bl, S, 512), Bl, S)
    dq, df, di, dg = [d.reshape(T, 512) for d in dqfig]
    dproj = jnp.concatenate([dzuv, dq, df, di, dg, settle(dxq), dgl[0], dgl[1], dgl[2]], axis=-1)
    g_w_in = _mm(h, dproj, "tn", bf16, "proj_dw", 512, 1664, 1024, shard=IN_WIDTH // N_DEV)
    tok = send("a", dict(w_in=g_w_in))
    dh = _mm(dproj, w_in, "nt", f32, "proj_dx", 1024, 1024, 1664)
    dx, g_norm1 = _rms_bwd(x2d, p["norm1_g"] + tok[0, 0], dh, "norm1_bwd", residual=dx1)

    gs = dict(w_spatial=g_w_sp, norm1_g=g_norm1, mem_norm_g=g_mem_norm, norm2_g=g_norm2, final_g=g_final, lb_logits=g_lbl,
              ln_v_g=g_ln_g, ln_v_b=g_ln_b, b_spatial=g_b_sp, hgrn_norm_g=g_ng, conv_b=g_conv_b)
    return loss_part, dx.reshape(Bl, S, Dd), gs


def _coords():
    return lax.axis_index("x"), lax.axis_index("y"), lax.axis_index("c")


def _slot(dev):
    return 4 * dev[0] + 2 * dev[1] + dev[2]


def _comm_call(body, name, arrays, out_shapes, n_sem):
    n = len(arrays)
    hbm = pl.BlockSpec(memory_space=pl.ANY)
    return pl.pallas_call(
        body, name=name, out_shape=out_shapes, in_specs=[hbm] * n, out_specs=[hbm] * n,
        scratch_shapes=[pltpu.SemaphoreType.DMA((n_sem, n)), pltpu.SemaphoreType.DMA((n_sem, n)), pltpu.SemaphoreType.DMA((n,))])(*arrays)


def _all_gather(blocks, name):
    n = len(blocks)

    def body(*refs):
        x_refs, o_refs, (send_sems, recv_sems, local_sems) = refs[:n], refs[n:2 * n], refs[2 * n:]
        x, y, c = _coords()
        me, sibling = (x, y, c), (x, y, 1 - c)
        chips = [(1 - x, y), (x, 1 - y), (1 - x, 1 - y)]

        def copy(a, k, block_dev, to, from_input=False):
            dst = o_refs[a].at[_slot(block_dev)]
            return pltpu.make_async_remote_copy(src_ref=x_refs[a] if from_input else dst, dst_ref=dst, send_sem=send_sems.at[k, a],
                                                recv_sem=recv_sems.at[k, a], device_id=to, device_id_type=MESH)

        mine = [pltpu.make_async_copy(x_refs[a], o_refs[a].at[_slot(me)], local_sems.at[a]) for a in range(n)]
        first = [copy(a, 0, me, sibling, True) for a in range(n)]
        first += [copy(a, 1 + j, me, (*chip, c), True) for j, chip in enumerate(chips) for a in range(n)]
        for cp in mine + first:
            cp.start()
        passed = []
        for j, chip in enumerate(chips):
            for a in range(n):
                copy(a, 1 + j, (*chip, c), me).wait_recv()
                fwd = copy(a, 4 + j, (*chip, c), sibling)
                fwd.start()
                passed.append(fwd)
        for a in range(n):
            copy(a, 0, sibling, me).wait_recv()
        for j, chip in enumerate(chips):
            for a in range(n):
                copy(a, 4 + j, (*chip, 1 - c), me).wait_recv()
        for cp in first + passed:
            cp.wait_send()
        for cp in mine:
            cp.wait()

    return _comm_call(body, name, blocks, [jax.ShapeDtypeStruct((N_DEV,) + b.shape, b.dtype) for b in blocks], 7)


def _all_to_all(parts, name):
    n = len(parts)
    rel = [(0, 0, 1), (0, 1, 0), (0, 1, 1), (1, 0, 0), (1, 0, 1), (1, 1, 0), (1, 1, 1)]

    def body(*refs):
        x_refs, o_refs, (send_sems, recv_sems, local_sems) = refs[:n], refs[n:2 * n], refs[2 * n:]
        x, y, c = _coords()
        me = (x, y, c)
        peers = [(x ^ dx, y ^ dy, c ^ dc) for dx, dy, dc in rel]

        def copy(a, k, peer):
            return pltpu.make_async_remote_copy(src_ref=x_refs[a].at[_slot(peer)], dst_ref=o_refs[a].at[_slot(me)], send_sem=send_sems.at[k, a],
                                                recv_sem=recv_sems.at[k, a], device_id=peer, device_id_type=MESH)

        def arrival(a, k, peer):
            return pltpu.make_async_remote_copy(src_ref=x_refs[a].at[_slot(me)], dst_ref=o_refs[a].at[_slot(peer)], send_sem=send_sems.at[k, a],
                                                recv_sem=recv_sems.at[k, a], device_id=peer, device_id_type=MESH)

        mine = [pltpu.make_async_copy(x_refs[a].at[_slot(me)], o_refs[a].at[_slot(me)], local_sems.at[a]) for a in range(n)]
        sends = [copy(a, k, peer) for k, peer in enumerate(peers) for a in range(n)]
        for cp in mine + sends:
            cp.start()
        for k, peer in enumerate(peers):
            for a in range(n):
                arrival(a, k, peer).wait_recv()
        for cp in sends:
            cp.wait_send()
        for cp in mine:
            cp.wait()

    return _comm_call(body, name, parts, [jax.ShapeDtypeStruct(p.shape, p.dtype) for p in parts], 7)


_HBM = pl.BlockSpec(memory_space=pltpu.HBM)
_SEM = pl.BlockSpec(memory_space=pltpu.SEMAPHORE)
_REL = [(0, 0, 1), (0, 1, 0), (0, 1, 1), (1, 0, 0), (1, 0, 1), (1, 1, 0), (1, 1, 1)]


_LINK_ORDER = (3, 1, 5, 4, 2, 6, 0)
SEND_PIECES = 4


def _pieces(shape, dtype):
    rows = shape[0]
    unit = 1 if len(shape) > 2 else (16 if dtype == bf16 else 8)
    for n in (SEND_PIECES, 2):
        if rows % (n * unit) == 0:
            return [pl.ds(i * (rows // n), rows // n) for i in range(n)]
    return [pl.ds(0, rows)]


def _split_copies(gather, src, land, send, recv, pieces):
    x, y, c = _coords()
    me = (x, y, c)
    copies = []
    for a in range(len(src)):
        block = src[a].shape if gather else src[a].shape[1:]
        for rows in (_pieces(block, src[a].dtype) if pieces else [None]):
            for k in _LINK_ORDER:
                dx, dy, dc = _REL[k]
                peer = (x ^ dx, y ^ dy, c ^ dc)
                mine, there = (src[a] if gather else src[a].at[_slot(peer)]), land[a].at[_slot(me)]
                if rows is not None:
                    mine, there = mine.at[rows], there.at[rows]
                copies.append(pltpu.make_async_remote_copy(src_ref=mine, dst_ref=there, send_sem=send[a].at[k], recv_sem=recv[a].at[k],
                                                           device_id=peer, device_id_type=MESH))
    return me, copies


def _arrivals(gather, src, land, send, recv):
    x, y, c = _coords()
    out = []
    for a in range(len(src)):
        for k, (dx, dy, dc) in enumerate(_REL):
            peer = (x ^ dx, y ^ dy, c ^ dc)
            out.append(pltpu.make_async_remote_copy(src_ref=src[a] if gather else src[a].at[_slot(peer)], dst_ref=land[a].at[_slot(peer)],
                                                    send_sem=send[a].at[k], recv_sem=recv[a].at[k], device_id=peer, device_id_type=MESH))
    return out


def _exchange_start(arrays, gather, name, after=None):
    n = len(arrays)
    e = 0 if after is None else 1
    lands = [lax.empty(((N_DEV,) + a.shape) if gather else a.shape, a.dtype) for a in arrays]

    def body(*refs):
        src, land = refs[:n], refs[n:2 * n]
        refs = refs[2 * n + e:]
        send, recv, token, local_sems = refs[:n], refs[n:2 * n], refs[4 * n], refs[4 * n + 1]
        me, out = _split_copies(gather, src, land, send, recv, True)
        local = [pltpu.make_async_copy(src[a] if gather else src[a].at[_slot(me)], land[a].at[_slot(me)], local_sems.at[a])
                 for a in range(n)]
        for cp in local:
            cp.start()
        for cp in local:
            cp.wait()
        for cp in out:
            cp.start()
        token[...] = jnp.zeros_like(token)

    sems = [pltpu.SemaphoreType.DMA((7,)) for _ in range(2 * n)]
    outs = pl.pallas_call(
        body, name=name,
        out_shape=(*sems, *[pltpu.HBM(a.shape, a.dtype) for a in arrays], *[pltpu.HBM(l.shape, l.dtype) for l in lands],
                   jax.ShapeDtypeStruct((8, 128), f32)),
        in_specs=[_HBM] * (2 * n) + [pl.BlockSpec(memory_space=pl.ANY)] * e,
        out_specs=(*[_SEM] * (2 * n), *[_HBM] * (2 * n), pl.BlockSpec(memory_space=pltpu.VMEM)),
        input_output_aliases={i: 2 * n + i for i in range(2 * n)},
        scratch_shapes=[pltpu.SemaphoreType.DMA((n,))],
        compiler_params=pltpu.CompilerParams(has_side_effects=pltpu.SideEffectType.DATAFLOW_SIDE_EFFECTING))(
        *[pltpu.with_memory_space_constraint(a, pltpu.HBM) for a in arrays],
        *[pltpu.with_memory_space_constraint(l, pltpu.HBM) for l in lands], *([after] if e else []))
    return (gather, n, outs[:4 * n]), outs[4 * n]


def _exchange_wait(handle, which, after, name):
    gather, n_all, vals = handle
    send_v, recv_v, src_v, land_v = [[vals[g * n_all + i] for i in which] for g in range(4)]
    n = len(which)

    def body(*refs):
        src, land, send, recv = refs[:n], refs[n:2 * n], refs[2 * n:3 * n], refs[3 * n:4 * n]
        for cp in _split_copies(gather, src, land, send, recv, False)[1]:
            cp.wait_send()
        for cp in _arrivals(gather, src, land, send, recv):
            cp.wait_recv()

    outs = pl.pallas_call(
        body, name=name,
        out_shape=(*[pltpu.HBM(a.shape, a.dtype) for a in src_v], *[pltpu.HBM(l.shape, l.dtype) for l in land_v]),
        in_specs=[*[_HBM] * (2 * n), *[_SEM] * (2 * n), pl.BlockSpec(memory_space=pl.ANY)], out_specs=[_HBM] * (2 * n),
        input_output_aliases={i: i for i in range(2 * n)},
        compiler_params=pltpu.CompilerParams(has_side_effects=pltpu.SideEffectType.DATAFLOW_SIDE_EFFECTING))(
        *src_v, *land_v, *send_v, *recv_v, after)
    return outs[n:]


def _seq_exchange(arrays, gather, name, collective_id):
    n = len(arrays)
    hbm = pltpu.MemorySpace.HBM
    srcs = [jax.new_ref(a, memory_space=hbm) for a in arrays]
    lands = [jax.empty_ref(jax.ShapeDtypeStruct(((N_DEV,) + a.shape) if gather else a.shape, a.dtype), memory_space=hbm) for a in arrays]

    @pl.kernel(mesh=plsc.ScalarSubcoreMesh(axis_name="sequencer", num_cores=1), name=name,
               scratch_types=(pltpu.SemaphoreType.DMA((7, n)), pltpu.SemaphoreType.DMA((7, n)), pltpu.SemaphoreType.DMA((n,))),
               compiler_params=pltpu.CompilerParams(collective_id=collective_id))
    def launch(send, recv, local):
        x, y, c = _coords()
        me = (x, y, c)
        peers = [(x ^ dx, y ^ dy, c ^ dc) for dx, dy, dc in _REL]
        barrier = pltpu.get_barrier_semaphore()
        for peer in peers:
            pl.semaphore_signal(barrier, inc=1, device_id=peer, device_id_type=MESH)
        pl.semaphore_wait(barrier, len(peers))

        def copy(a, k, peer, arrival):
            return pltpu.make_async_remote_copy(
                src_ref=srcs[a] if gather else srcs[a].at[_slot(peer)], dst_ref=lands[a].at[_slot(peer if arrival else me)],
                send_sem=send.at[k, a], recv_sem=recv.at[k, a], device_id=peer, device_id_type=MESH)

        mine = [pltpu.make_async_copy(srcs[a] if gather else srcs[a].at[_slot(me)], lands[a].at[_slot(me)], local.at[a])
                for a in range(n)]
        out = [copy(a, k, peer, False) for a in range(n) for k, peer in enumerate(peers)]
        for cp in mine + out:
            cp.start()
        for a in range(n):
            for k, peer in enumerate(peers):
                copy(a, k, peer, True).wait_recv()
        for cp in out:
            cp.wait_send()
        for cp in mine:
            cp.wait()

    launch()
    return [land[...] for land in lands]


def _adam_math(w, g, m, v):
    m_ = ADAM_B1 * m + (1.0 - ADAM_B1) * g
    v_ = ADAM_B2 * v + (1.0 - ADAM_B2) * jnp.square(g)
    m_hat = m_ / (1.0 - ADAM_B1 ** ADAM_STEP)
    v_hat = v_ / (1.0 - ADAM_B2 ** ADAM_STEP)
    return -ADAM_LR * (m_hat / (jnp.sqrt(v_hat) + ADAM_EPS) + ADAM_WD * w), m_, v_


def _reduce_adamw(parts, w, m, v, name):
    _, R, L = parts.shape
    tr = _pick(R, (256, 128, 64, 32, 16, 8))

    def body(p_ref, w_ref, m_ref, v_ref, g_ref, d_ref, nm_ref, nv_ref):
        g = p_ref[0].astype(f32)
        for i in range(1, N_DEV):
            g = g + p_ref[i].astype(f32)
        g_ref[...] = g
        d_ref[...], nm_ref[...], nv_ref[...] = _adam_math(w_ref[...], g, m_ref[...], v_ref[...])

    blk = pl.BlockSpec((tr, L), lambda i: (i, 0))
    sh = jax.ShapeDtypeStruct((R, L), f32)
    return _call(body, name, (R // tr,), [pl.BlockSpec((N_DEV, tr, L), lambda i: (0, i, 0)), blk, blk, blk], (blk,) * 4, (sh,) * 4,
                 sem=("parallel",))(parts, w, m, v)


SMALL = (("w_spatial", (512, 128), 0), ("norm1_g", (1, 1024), 512), ("mem_norm_g", (1, 1024), 520), ("norm2_g", (1, 1024), 528),
         ("final_g", (1, 1024), 536), ("lb_logits", (2, 512), 544), ("ln_v_g", (1, 512), 552), ("ln_v_b", (1, 512), 556),
         ("b_spatial", (4, 128), 560), ("hgrn_norm_g", (1, 128), 564), ("conv_b", (1, 2816), 565))
LOSS_ROW, SMALL_USED, SMALL_ROWS = 587, 588, 640


def _segments(shape, base):
    r, n = shape
    per = n // 128
    return [(base + i * per + j, i, slice(j * 128, (j + 1) * 128)) for i in range(r) for j in range(per)]


def _pack_small(gs, loss_part):
    names = [n for n, _, _ in SMALL]

    def body(*refs):
        src, loss_ref, o_ref = dict(zip(names, refs[:-2])), refs[-2], refs[-1]
        o_ref[SMALL_USED:SMALL_ROWS, :] = jnp.zeros((SMALL_ROWS - SMALL_USED, 128), f32)
        o_ref[LOSS_ROW:LOSS_ROW + 1, :] = loss_ref[...]
        for name, shape, base in SMALL:
            ref = src[name]
            if name == "w_spatial":
                o_ref[base:base + 512, :] = ref[...].reshape(512, 128)
            elif name == "b_spatial":
                o_ref[base:base + 4, :] = ref[0:4, :]
            elif name == "hgrn_norm_g":
                per_head = [ref[b, h] for b in range(ref.shape[0]) for h in range(N_HEAD)]
                o_ref[base:base + 1, :] = functools.reduce(lambda u, v_: u + v_, per_head)
            else:
                for row, i, sl in _segments(shape, base):
                    o_ref[row:row + 1, :] = ref[i:i + 1, sl]

    return pl.pallas_call(body, name="pack_small", out_shape=jax.ShapeDtypeStruct((SMALL_ROWS, 128), f32))(
        *[gs[n] for n in names], loss_part)


def _small_update(gathered, w, m, v):
    names = [n for n, _, _ in SMALL]
    k = len(names)

    def body(*refs):
        p_ref = refs[0]
        ins = [dict(zip(names, refs[1 + i * k:1 + (i + 1) * k])) for i in range(3)]
        outs = [dict(zip(names, refs[1 + (3 + i) * k:1 + (4 + i) * k])) for i in range(4)]
        loss_ref, gsum = refs[-2], refs[-1]
        g = p_ref[0]
        for i in range(1, N_DEV):
            g = g + p_ref[i]
        gsum[...] = g
        loss_ref[...] = gsum[LOSS_ROW:LOSS_ROW + 1, :]
        for name, shape, base in SMALL:
            if name == "w_spatial":
                where = [(slice(base, base + 512), (slice(None), slice(None)))]
            else:
                where = [(slice(row, row + 1), (slice(i, i + 1), sl)) for row, i, sl in _segments(shape, base)]
            for rows, at in where:
                g_ = gsum[rows, :]
                d_, m_, v_ = _adam_math(ins[0][name][at], g_, ins[1][name][at], ins[2][name][at])
                for o, val in zip(outs, (g_, d_, m_, v_)):
                    o[name][at] = val

    args = [gathered] + [d[n] for d in (w, m, v) for n in names]
    out_shapes = [jax.ShapeDtypeStruct(shape, f32) for _ in range(4) for _, shape, _ in SMALL] + [jax.ShapeDtypeStruct((1, 128), f32)]
    outs = pl.pallas_call(body, name="small_update", out_shape=out_shapes, scratch_shapes=[pltpu.VMEM((SMALL_ROWS, 128), f32)])(*args)
    return [dict(zip(names, outs[i * k:(i + 1) * k])) for i in range(4)], outs[-1]


def _cols_full(g):
    return jnp.moveaxis(g, 0, -2).reshape(g.shape[1:-1] + (N_DEV * g.shape[-1],))


def _cols_parts(full):
    n = full.shape[-1] // N_DEV
    return jnp.moveaxis(full.reshape(full.shape[:-1] + (N_DEV, n)), -2, 0)


def kernel(x, mem, norm1_g, w_in, ln_v_g, ln_v_b, w_spatial, b_spatial, lb_logits, hgrn_norm_g, mem_norm_g, w_mem_kv, w_branch, w_out, norm2_g, w_up, conv_w, conv_b, w_down, final_g, loss_target, m_norm1_g, m_w_in, m_ln_v_g, m_ln_v_b, m_w_spatial, m_b_spatial, m_lb_logits, m_hgrn_norm_g, m_mem_norm_g, m_w_mem_kv, m_w_branch, m_w_out, m_norm2_g, m_w_up, m_conv_w, m_conv_b, m_w_down, m_final_g, v_norm1_g, v_w_in, v_ln_v_g, v_ln_v_b, v_w_spatial, v_b_spatial, v_lb_logits, v_hgrn_norm_g, v_mem_norm_g, v_w_mem_kv, v_w_branch, v_w_out, v_norm2_g, v_w_up, v_conv_w, v_conv_b, v_w_down, v_final_g):
    given = dict(locals())
    order = ("norm1_g", "w_in", "ln_v_g", "ln_v_b", "w_spatial", "b_spatial", "lb_logits", "hgrn_norm_g", "mem_norm_g",
             "w_mem_kv", "w_branch", "w_out", "norm2_g", "w_up", "conv_w", "conv_b", "w_down", "final_g")
    groups = dict(a=("w_in",), b=("w_mem_kv", "w_branch", "w_out"), c=("w_up", "conv_w", "w_down"))

    wire = {n: given[n][0].astype(f32 if n == "conv_w" else bf16) for ns in groups.values() for n in ns}
    g_in = _all_gather([wire["w_in"]], "gather_w_in")[0]
    late = groups["b"] + groups["c"]
    w_in_full = _cols_full(g_in)
    w_in_full, rest_wire = lax.optimization_barrier((w_in_full, [wire[n] for n in late]))
    rest = _seq_exchange(rest_wire, True, "gather_rest", 1)

    def late_b(after):
        _, (kv_, br_, out_) = lax.optimization_barrier((after, tuple(rest[0:3])))
        br_ = _cols_full(br_)
        return dict(w_mem_kv=kv_.reshape(D_MODEL, 2 * 512), w_branch=[br_[n] for n in range(3)], w_out=out_.reshape(D_MODEL, D_MODEL))

    def late_c(after):
        _, (up_, cw_, down_) = lax.optimization_barrier((after, tuple(rest[3:6])))
        return dict(w_up=_cols_full(up_), conv_w=_cols_full(cw_), w_down=down_.reshape(D_FF, D_MODEL))

    to_parts = dict(w_in=lambda g_: g_, w_up=lambda g_: g_, conv_w=_cols_parts,
                    w_branch=lambda g_: _cols_parts(jnp.stack(g_)).reshape(N_DEV, -1, 128),
                    w_mem_kv=lambda g_: g_.reshape(N_DEV, -1, 2 * 512), w_out=lambda g_: g_.reshape(N_DEV, -1, D_MODEL),
                    w_down=lambda g_: g_.reshape(N_DEV, -1, D_MODEL))
    scatters = {}

    def send(tag, grads_):
        parts = [to_parts[n](grads_[n]) for n in groups[tag]]
        scatters[tag] = _seq_exchange(parts, False, f"scatter_{tag}", dict(a=2, b=4, c=5)[tag])
        return jnp.zeros((8, 128), f32)

    small_2d = lambda prefix: {n: given[prefix + n].reshape(shape) for n, shape, _ in SMALL}
    p = small_2d("")
    p["w_spatial"] = w_spatial[0]
    updates = {}

    def update(tag):
        for n, parts in zip(groups[tag], scatters[tag]):
            two_d = (-1, given[n].shape[-1])
            updates[n] = _reduce_adamw(parts, *[given[pre + n].reshape(two_d) for pre in ("", "m_", "v_")], "adamw_" + n)

    def settle(chain):
        update("c")
        update("b")
        early = groups["c"] + groups["b"]
        chain, tied = lax.optimization_barrier((chain, [updates[n] for n in early]))
        updates.update(zip(early, tied))
        return chain

    loss_part, grad_x, gs = _local_step(x, mem, loss_target, p, w_in_full, late_b, late_c, send, settle)

    gathered = _seq_exchange([_pack_small(gs, loss_part)], True, "gather_small", 3)[0]

    update("a")
    grads, delta, new_m, new_v = {}, {}, {}, {}
    for n, res in updates.items():
        grads[n], delta[n], new_m[n], new_v[n] = [r.reshape(given[n].shape) for r in res]

    small_results, loss_row = _small_update(gathered, small_2d(""), small_2d("m_"), small_2d("v_"))
    for dst, res in zip((grads, delta, new_m, new_v), small_results):
        for n, _, _ in SMALL:
            dst[n] = res[n].reshape(given[n].shape)
    loss = loss_row[0, 0]

    return (loss, grad_x, *[grads[n] for n in order], *[delta[n] for n in order], *[new_m[n] for n in order],
            *[new_v[n] for n in order])
```

```python
import functools

import jax
import jax.numpy as jnp
from jax import lax
from jax.experimental import pallas as pl
from jax.experimental.pallas import tpu as pltpu
from jax.experimental.pallas import tpu_sc as plsc

f32 = jnp.float32
bf16 = jnp.bfloat16

N_DEV = 8
D_MODEL = 1024
EPS = 1e-6
GM_CHUNK = 128
HG_CHUNK = 64
HEAD = 128
N_HEAD = 4
MEM_LEN = 256
D_FF = 2816
IN_WIDTH = 6656
C_ZU, C_HQ, C_HF, C_HI, C_HG, C_XQ, C_GL = 0, 1024, 1536, 2048, 2560, 3072, 3584
ADAM_LR, ADAM_B1, ADAM_B2, ADAM_EPS, ADAM_WD, ADAM_STEP = 0.001, 0.9, 0.999, 1e-08, 0.01, 10
VMEM_LIMIT = 56 * 1024 * 1024
MESH = pl.DeviceIdType.MESH


def _pick(n, cands):
    for c in cands:
        if n % c == 0:
            return c
    return n


def _call(body, name, grid, in_specs, out_specs, out_shape, scratch=(), sem=None, **cp):
    params = dict(vmem_limit_bytes=VMEM_LIMIT, **cp)
    if sem is not None:
        params["dimension_semantics"] = sem
    return pl.pallas_call(
        body, name=name, grid=grid, in_specs=in_specs, out_specs=out_specs, out_shape=out_shape,
        scratch_shapes=list(scratch), compiler_params=pltpu.CompilerParams(**params))


_DN = {"nn": (((1,), (0,)), ((), ())), "nt": (((1,), (1,)), ((), ())), "tn": (((0,), (0,)), ((), ()))}


def _raw_dot(a, b, mode):
    return lax.dot_general(a.astype(bf16), b.astype(bf16), _DN[mode], preferred_element_type=f32)


@jax.custom_vjp
def _dot_nn(a, b):
    return _raw_dot(a, b, "nn")


_dot_nn.defvjp(lambda a, b: (_raw_dot(a, b, "nn"), (a, b)),
               lambda r, g: (_raw_dot(g, r[1], "nt"), _raw_dot(r[0], g, "tn")))


@jax.custom_vjp
def _dot_nt(a, b):
    return _raw_dot(a, b, "nt")


_dot_nt.defvjp(lambda a, b: (_raw_dot(a, b, "nt"), (a, b)),
               lambda r, g: (_raw_dot(g, r[1], "nn"), _raw_dot(g, r[0], "tn")))


@jax.custom_vjp
def _dot_tn(a, b):
    return _raw_dot(a, b, "tn")


_dot_tn.defvjp(lambda a, b: (_raw_dot(a, b, "tn"), (a, b)),
               lambda r, g: (_raw_dot(r[1], g, "nt"), _raw_dot(r[0], g, "nn")))


def _tri(n, lower):
    r = lax.broadcasted_iota(jnp.int32, (n, n), 0)
    c = lax.broadcasted_iota(jnp.int32, (n, n), 1)
    return ((c <= r) if lower else (c >= r)).astype(f32)


def _sel_dot(sel, x, mode, x_first=False):
    hi = x.astype(bf16)
    rest = x - hi.astype(f32)
    mid = rest.astype(bf16)
    lo = (rest - mid.astype(f32)).astype(bf16)
    sel = sel.astype(bf16)
    dot = lambda piece: lax.dot_general(*((piece, sel) if x_first else (sel, piece)), _DN[mode], preferred_element_type=f32)
    return dot(hi) + dot(mid) + dot(lo)


def _egrad(fn, x, ct):
    return jax.vjp(fn, x)[1](ct)[0]


def _mm(a, b, mode, out_dtype, name, tm, tn, tk=None, residual=None, shard=None, n_outer=False):
    if mode == "nn":
        (M, K), (_, N) = a.shape, b.shape
    elif mode == "nt":
        (M, K), (N, _) = a.shape, b.shape
    else:
        (K, M), (_, N) = a.shape, b.shape
    tm, tn = min(tm, M), min(tn, N)
    tk = K if tk is None else min(tk, K)
    assert M % tm == 0 and N % tn == 0 and K % tk == 0, (name, M, N, K, tm, tn, tk)
    nk = K // tk

    def body(*refs):
        acc_ref = refs[-1] if nk > 1 else None
        refs = refs[:-1] if nk > 1 else refs
        if residual is None:
            a_ref, b_ref, o_ref = refs
        else:
            a_ref, b_ref, r_ref, o_ref = refs

        def finish(r):
            if residual is not None:
                r = r + r_ref[...]
            if shard is None:
                o_ref[...] = r.astype(out_dtype)
            else:
                for s in range(tn // shard):
                    o_ref[s] = r[:, s * shard:(s + 1) * shard].astype(out_dtype)

        part = _raw_dot(a_ref[...], b_ref[...], mode)
        if nk == 1:
            finish(part)
            return
        k = pl.program_id(2)

        @pl.when(k == 0)
        def _():
            acc_ref[...] = part

        @pl.when((k > 0) & (k < nk - 1))
        def _():
            acc_ref[...] += part

        @pl.when(k == nk - 1)
        def _():
            finish(acc_ref[...] + part)

    def at(index):
        return (lambda j, i, k: index(i, j, k)) if n_outer else index

    a_spec = {"nn": pl.BlockSpec((tm, tk), at(lambda i, j, k: (i, k))),
              "nt": pl.BlockSpec((tm, tk), at(lambda i, j, k: (i, k))),
              "tn": pl.BlockSpec((tk, tm), at(lambda i, j, k: (k, i)))}[mode]
    b_spec = {"nn": pl.BlockSpec((tk, tn), at(lambda i, j, k: (k, j))),
              "nt": pl.BlockSpec((tn, tk), at(lambda i, j, k: (j, k))),
              "tn": pl.BlockSpec((tk, tn), at(lambda i, j, k: (k, j)))}[mode]
    o_spec = pl.BlockSpec((tm, tn), at(lambda i, j, k: (i, j)))
    in_specs = [a_spec, b_spec] + ([o_spec] if residual is not None else [])
    args = (a, b) + ((residual,) if residual is not None else ())
    out_shape = jax.ShapeDtypeStruct((M, N), out_dtype)
    if shard is not None:
        assert residual is None and tn % shard == 0
        o_spec = pl.BlockSpec((tn // shard, tm, shard), at(lambda i, j, k: (j, i, 0)))
        out_shape = jax.ShapeDtypeStruct((N // shard, M, shard), out_dtype)
    grid = (N // tn, M // tm, nk) if n_outer else (M // tm, N // tn, nk)
    return _call(body, name, grid, in_specs, o_spec, out_shape,
                 scratch=[pltpu.VMEM((tm, tn), f32)] if nk > 1 else [], sem=("parallel", "parallel", "arbitrary"))(*args)


def _rms_fwd(x, g, name, transposed=False):
    R, Dd = x.shape
    tr = _pick(R, (512, 256, 128))

    def body(x_ref, g_ref, o_ref, *t_ref):
        xf = x_ref[...]
        y = xf * lax.rsqrt(jnp.mean(xf * xf, axis=-1, keepdims=True) + EPS) * g_ref[...]
        o_ref[...] = y.astype(bf16)
        if transposed:
            t_ref[0][...] = y.T.astype(bf16)

    row = pl.BlockSpec((tr, Dd), lambda i: (i, 0))
    out_specs, out_shape = row, jax.ShapeDtypeStruct((R, Dd), bf16)
    if transposed:
        out_specs, out_shape = (row, pl.BlockSpec((Dd, tr), lambda i: (0, i))), (out_shape, jax.ShapeDtypeStruct((Dd, R), bf16))
    return _call(body, name, (R // tr,), [row, pl.BlockSpec((1, Dd), lambda i: (0, 0))], out_specs, out_shape, sem=("parallel",))(x, g)


def _rms_bwd(x, g, dh, name, residual=None):
    R, Dd = x.shape
    tr = _pick(R, (512, 256, 128))

    def body(*refs):
        if residual is None:
            x_ref, g_ref, dh_ref, dx_ref, dg_ref = refs
        else:
            x_ref, g_ref, dh_ref, r_ref, dx_ref, dg_ref = refs
        xf = x_ref[...]
        rs = lax.rsqrt(jnp.mean(xf * xf, axis=-1, keepdims=True) + EPS)
        y = xf * rs
        dh_ = dh_ref[...].astype(f32)
        dy = dh_ * g_ref[...]
        dx = rs * (dy - y * jnp.mean(dy * y, axis=-1, keepdims=True))
        if residual is not None:
            dx = dx + r_ref[...]
        dx_ref[...] = dx

        @pl.when(pl.program_id(0) == 0)
        def _():
            dg_ref[...] = jnp.zeros_like(dg_ref)

        dg_ref[...] += jnp.sum(dh_ * y, axis=0, keepdims=True)

    row = pl.BlockSpec((tr, Dd), lambda i: (i, 0))
    vec = pl.BlockSpec((1, Dd), lambda i: (0, 0))
    in_specs = [row, vec, row] + ([row] if residual is not None else [])
    args = (x, g, dh) + ((residual,) if residual is not None else ())
    return _call(body, name, (R // tr,), in_specs, (row, vec),
                 (jax.ShapeDtypeStruct((R, Dd), f32), jax.ShapeDtypeStruct((1, Dd), f32)), sem=("arbitrary",))(*args)


def _final_loss(x2, g, target):
    R, Dd = x2.shape
    tr = _pick(R, (512, 256, 128))

    def body(x_ref, g_ref, t_ref, loss_ref, dx_ref, dg_ref):
        xf = x_ref[...]
        rs = lax.rsqrt(jnp.mean(xf * xf, axis=-1, keepdims=True) + EPS)
        y = xf * rs
        err = y * g_ref[...] - t_ref[...]
        dh_ = err * (1.0 / Dd)
        dy = dh_ * g_ref[...]
        dx_ref[...] = rs * (dy - y * jnp.mean(dy * y, axis=-1, keepdims=True))

        @pl.when(pl.program_id(0) == 0)
        def _():
            dg_ref[...] = jnp.zeros_like(dg_ref)
            loss_ref[...] = jnp.zeros_like(loss_ref)

        dg_ref[...] += jnp.sum(dh_ * y, axis=0, keepdims=True)
        part = jnp.sum(jnp.mean(err * err, axis=-1, keepdims=True), axis=0, keepdims=True)
        loss_ref[...] += 0.5 * part

    row = pl.BlockSpec((tr, Dd), lambda i: (i, 0))
    vec = pl.BlockSpec((1, Dd), lambda i: (0, 0))
    return _call(body, "final_loss", (R // tr,), [row, vec, row], (pl.BlockSpec((1, 128), lambda i: (0, 0)), row, vec),
                 (jax.ShapeDtypeStruct((1, 128), f32), jax.ShapeDtypeStruct((R, Dd), f32), jax.ShapeDtypeStruct((1, Dd), f32)),
                 sem=("arbitrary",))(x2, g, target)


def _gmlp_parts(zuv, ln_g, ln_b):
    zu, zv = zuv[:, :512], zuv[:, 512:]
    u = jax.nn.gelu(zu)
    v = jax.nn.gelu(zv)
    mu = jnp.mean(v, axis=-1, keepdims=True)
    rs = lax.rsqrt(jnp.mean(jnp.square(v - mu), axis=-1, keepdims=True) + EPS)
    xh = (v - mu) * rs
    return zu, zv, u, xh, rs, xh * ln_g + ln_b


def _gmlp_fwd(proj, ln_g, ln_b, w_s, b_st):
    T = proj.shape[0]

    def body(p_ref, g_ref, b_ref, w_ref, bs_ref, o_ref):
        _, _, u, _, _, vn = _gmlp_parts(p_ref[...].astype(f32), g_ref[...], b_ref[...])
        causal = _tri(GM_CHUNK, True) > 0
        for gi in range(N_HEAD):
            sl = slice(gi * HEAD, (gi + 1) * HEAD)
            w = jnp.where(causal, w_ref[gi], 0.0)
            mixed = _raw_dot(w, vn[:, sl], "nn") + bs_ref[:, gi:gi + 1]
            o_ref[:, sl] = (u[:, sl] * mixed).astype(bf16)

    vec = pl.BlockSpec((1, 512), lambda i: (0, 0))
    return _call(body, "gmlp_fwd", (T // GM_CHUNK,),
                 [pl.BlockSpec((GM_CHUNK, 1024), lambda i: (i, 0)), vec, vec,
                  pl.BlockSpec((N_HEAD, GM_CHUNK, GM_CHUNK), lambda i: (0, 0, 0)), pl.BlockSpec((GM_CHUNK, 128), lambda i: (0, 0))],
                 pl.BlockSpec((GM_CHUNK, 512), lambda i: (i, 0)), jax.ShapeDtypeStruct((T, 512), bf16), sem=("parallel",))(
        proj, ln_g, ln_b, w_s, b_st)


def _gmlp_bwd(proj, ln_g, ln_b, w_s, b_st, da):
    T = proj.shape[0]

    def body(p_ref, g_ref, b_ref, w_ref, bs_ref, da_ref, dp_ref, dg_ref, db_ref, dw_ref, dbs_ref):
        zu, zv, u, xh, rs, vn = _gmlp_parts(p_ref[...].astype(f32), g_ref[...], b_ref[...])
        causal = _tri(GM_CHUNK, True) > 0
        sub = lax.broadcasted_iota(jnp.int32, (8, GM_CHUNK), 0)
        ones = jnp.ones((8, HEAD), f32)
        dout = da_ref[...].astype(f32)

        @pl.when(pl.program_id(0) == 0)
        def _():
            for r in (dg_ref, db_ref, dw_ref, dbs_ref):
                r[...] = jnp.zeros_like(r)

        du, dvn, dbs = [], [], jnp.zeros((8, GM_CHUNK), f32)
        for gi in range(N_HEAD):
            sl = slice(gi * HEAD, (gi + 1) * HEAD)
            w = jnp.where(causal, w_ref[gi], 0.0)
            mixed = _raw_dot(w, vn[:, sl], "nn") + bs_ref[:, gi:gi + 1]
            du.append(dout[:, sl] * mixed)
            dm = dout[:, sl] * u[:, sl]
            row_sums = _sel_dot(ones, dm, "nt")
            dbs = dbs + jnp.where(sub == gi, row_sums, 0.0)
            dw_ref[gi] += jnp.where(causal, _raw_dot(dm, vn[:, sl], "nt"), 0.0)
            dvn.append(_raw_dot(w, dm, "tn"))
        dbs_ref[...] += dbs
        du = jnp.concatenate(du, axis=-1)
        dvn = jnp.concatenate(dvn, axis=-1)
        dg_ref[...] += jnp.sum(dvn * xh, axis=0, keepdims=True)
        db_ref[...] += jnp.sum(dvn, axis=0, keepdims=True)
        dxh = dvn * g_ref[...]
        dv = rs * (dxh - jnp.mean(dxh, axis=-1, keepdims=True) - xh * jnp.mean(dxh * xh, axis=-1, keepdims=True))
        dp_ref[:, :512] = _egrad(jax.nn.gelu, zu, du).astype(bf16)
        dp_ref[:, 512:] = _egrad(jax.nn.gelu, zv, dv).astype(bf16)

    vec = pl.BlockSpec((1, 512), lambda i: (0, 0))
    wsp = pl.BlockSpec((N_HEAD, GM_CHUNK, GM_CHUNK), lambda i: (0, 0, 0))
    return _call(body, "gmlp_bwd", (T // GM_CHUNK,),
                 [pl.BlockSpec((GM_CHUNK, 1024), lambda i: (i, 0)), vec, vec, wsp, pl.BlockSpec((GM_CHUNK, 128), lambda i: (0, 0)),
                  pl.BlockSpec((GM_CHUNK, 512), lambda i: (i, 0))],
                 (pl.BlockSpec((GM_CHUNK, 1024), lambda i: (i, 0)), vec, vec, wsp, pl.BlockSpec((8, GM_CHUNK), lambda i: (0, 0))),
                 (jax.ShapeDtypeStruct((T, 1024), bf16), jax.ShapeDtypeStruct((1, 512), f32), jax.ShapeDtypeStruct((1, 512), f32),
                  jax.ShapeDtypeStruct((N_HEAD, GM_CHUNK, GM_CHUNK), f32), jax.ShapeDtypeStruct((8, GM_CHUNK), f32)),
                 sem=("arbitrary",))(proj, ln_g, ln_b, w_s, b_st, da)


HG_SUB = 8
HG_NSUB = HG_CHUNK // HG_SUB


def _two_level_matrix():
    r = lax.broadcasted_iota(jnp.int32, (2 * HG_CHUNK, HG_CHUNK), 0)
    c = lax.broadcasted_iota(jnp.int32, (2 * HG_CHUNK, HG_CHUNK), 1)
    t = jnp.where(r < HG_CHUNK, r, r - HG_CHUNK)
    local = (r < HG_CHUNK) & (t // HG_SUB == c // HG_SUB) & (c <= t)
    before = (r >= HG_CHUNK) & (c < (t // HG_SUB) * HG_SUB)
    return (local | before).astype(f32)


def _two_level_sums(x):
    two = _sel_dot(_two_level_matrix(), x, "nn")
    return two[:HG_CHUNK], two[HG_CHUNK:]


@jax.custom_vjp
def _two_level_cumsum(x):
    return _two_level_sums(x)


_two_level_cumsum.defvjp(
    lambda x: (_two_level_sums(x), None),
    lambda _, g: (_sel_dot(_two_level_matrix(), jnp.concatenate(g, axis=0), "tn"),))


def _tile_matrix():
    s = lax.broadcasted_iota(jnp.int32, (HG_SUB, HG_CHUNK), 0)
    j = lax.broadcasted_iota(jnp.int32, (HG_SUB, HG_CHUNK), 1)
    return (j % HG_SUB == s).astype(f32)


@jax.custom_vjp
def _tile_lanes(x):
    return _sel_dot(_tile_matrix(), x, "nn", x_first=True)


_tile_lanes.defvjp(
    lambda x: (_sel_dot(_tile_matrix(), x, "nn", x_first=True), None),
    lambda _, g: (_sel_dot(_tile_matrix(), g, "nt", x_first=True),))


def _block_rows(x):
    k = x.shape[-1]
    return jnp.broadcast_to(x.reshape(HG_NSUB, 1, HG_SUB, k), (HG_NSUB, HG_SUB, HG_SUB, k)).reshape(HG_CHUNK, HG_SUB, k)


def _hgrn_chunk(st0, q_raw, f_raw, i_raw, g_raw, l0, l1, ng):
    C, SUB = HG_CHUNK, HG_SUB
    lb = jax.nn.sigmoid(l0 - l1)
    fg = lb + (1.0 - lb) * jax.nn.sigmoid(f_raw)
    kk = 1.0 - fg
    qf = jax.nn.silu(q_raw)
    al, base = _two_level_cumsum(jnp.log(fg))
    a = al + base
    row = lax.broadcasted_iota(jnp.int32, (C, HEAD), 0)
    a_last = jnp.sum(jnp.where(row == C - 1, a, 0.0), axis=0, keepdims=True)
    inter = _dot_nt(qf * jnp.exp(a), st0)
    qt = qf * jnp.exp(al)
    rb = lax.broadcasted_iota(jnp.int32, (C, C), 0) // SUB
    cb = lax.broadcasted_iota(jnp.int32, (C, C), 1) // SUB
    scores = jnp.zeros((C, C), f32)
    for i in range(1, HG_NSUB):
        base_i = jnp.sum(jnp.where(row == i * SUB, base, 0.0), axis=0, keepdims=True)
        kt = kk * jnp.exp(jnp.minimum(base_i - a, 0.0))
        scores = scores + jnp.where((rb == i) & (cb < i), _dot_nt(qt, kt), 0.0)
    t_i = lax.broadcasted_iota(jnp.int32, (C, SUB, HEAD), 0) % SUB
    s_i = lax.broadcasted_iota(jnp.int32, (C, SUB, HEAD), 1)
    decay = jnp.exp(jnp.where(s_i <= t_i, al[:, None, :] - _block_rows(al), -jnp.inf))
    diag = jnp.sum(qf[:, None, :] * decay * _block_rows(kk), axis=-1)
    scores = scores + jnp.where(rb == cb, _tile_lanes(diag), 0.0)
    o = inter + _dot_nn(scores, i_raw)
    st1 = jnp.exp(a_last) * st0 + _dot_tn(i_raw, kk * jnp.exp(a_last - a))
    on = o * lax.rsqrt(jnp.mean(o * o, axis=-1, keepdims=True) + EPS) * ng
    return st1, on * jax.nn.silu(g_raw)


def _hgrn_specs(S, Bl, rev):
    N = S // HG_CHUNK
    chunk = (lambda n: N - 1 - n) if rev else (lambda n: n)
    col = lambda c0: pl.BlockSpec((Bl, HG_CHUNK, 512), lambda n: (0, chunk(n), c0 // 512))
    st = pl.BlockSpec((Bl, N_HEAD, 1, HEAD, HEAD), lambda n: (0, 0, chunk(n), 0, 0))
    full = lambda *s: pl.BlockSpec(s, functools.partial(lambda n, nd: (0,) * nd, nd=len(s)))
    return N, col, st, full


def _hgrn_fwd(proj, lb_logits, ng, Bl, S):
    N, col, st, full = _hgrn_specs(S, Bl, False)

    def body(q_ref, f_ref, i_ref, g_ref, l_ref, ng_ref, o_ref, st_ref, state):
        @pl.when(pl.program_id(0) == 0)
        def _():
            state[...] = jnp.zeros_like(state)

        for b in range(Bl):
            for h in range(N_HEAD):
                sl = slice(h * HEAD, (h + 1) * HEAD)
                st0 = state[b, h]
                st_ref[b, h, 0] = st0
                st1, out = _hgrn_chunk(st0, *[r[b, :, sl].astype(f32) for r in (q_ref, f_ref, i_ref, g_ref)],
                                       l_ref[0:1, sl], l_ref[1:2, sl], ng_ref[...])
                state[b, h] = st1
                o_ref[b, :, sl] = out.astype(bf16)

    return _call(body, "hgrn_fwd", (N,), [col(C_HQ), col(C_HF), col(C_HI), col(C_HG), full(2, 512), full(1, HEAD)],
                 (col(0), st),
                 (jax.ShapeDtypeStruct((Bl, S, 512), bf16), jax.ShapeDtypeStruct((Bl, N_HEAD, N, HEAD, HEAD), f32)),
                 scratch=[pltpu.VMEM((Bl, N_HEAD, HEAD, HEAD), f32)], sem=("arbitrary",))(
        proj, proj, proj, proj, lb_logits, ng)


def _hgrn_bwd(proj, lb_logits, ng, states, db, Bl, S):
    N, col, st, full = _hgrn_specs(S, Bl, True)

    def body(q_ref, f_ref, i_ref, g_ref, l_ref, ng_ref, st_ref, db_ref,
             dq_ref, df_ref, di_ref, dg_ref, dl_ref, dng_ref, dstate):
        @pl.when(pl.program_id(0) == 0)
        def _():
            dstate[...] = jnp.zeros_like(dstate)
            dl_ref[...] = jnp.zeros_like(dl_ref)
            dng_ref[...] = jnp.zeros_like(dng_ref)

        for b in range(Bl):
            for h in range(N_HEAD):
                sl = slice(h * HEAD, (h + 1) * HEAD)
                _, vjp = jax.vjp(_hgrn_chunk, st_ref[b, h, 0], *[r[b, :, sl].astype(f32) for r in (q_ref, f_ref, i_ref, g_ref)],
                                 l_ref[0:1, sl], l_ref[1:2, sl], ng_ref[...])
                dst0, dq, df, di, dg, dl0, dl1, dng = vjp((dstate[b, h], db_ref[b, :, sl].astype(f32)))
                dstate[b, h] = dst0
                dq_ref[b, :, sl] = dq.astype(bf16)
                df_ref[b, :, sl] = df.astype(bf16)
                di_ref[b, :, sl] = di.astype(bf16)
                dg_ref[b, :, sl] = dg.astype(bf16)
                dl_ref[0:1, sl] += dl0
                dl_ref[1:2, sl] += dl1
                dng_ref[b, h] += dng

    return _call(body, "hgrn_bwd", (N,),
                 [col(C_HQ), col(C_HF), col(C_HI), col(C_HG), full(2, 512), full(1, HEAD), st, col(0)],
                 (*[col(0)] * 4, full(2, 512), full(Bl, N_HEAD, 1, HEAD)),
                 (*[jax.ShapeDtypeStruct((Bl, S, 512), bf16)] * 4, jax.ShapeDtypeStruct((2, 512), f32),
                  jax.ShapeDtypeStruct((Bl, N_HEAD, 1, HEAD), f32)),
                 scratch=[pltpu.VMEM((Bl, N_HEAD, HEAD, HEAD), f32)], sem=("arbitrary",))(
        proj, proj, proj, proj, lb_logits, ng, states, db)


def _attn_probs(q, k):
    s = _raw_dot(q, k, "nt") * (HEAD ** -0.5)
    e = jnp.exp(s - jnp.max(s, axis=-1, keepdims=True))
    return e / jnp.sum(e, axis=-1, keepdims=True)


def _attn_specs(S, tq):
    nq = S // tq
    q = pl.BlockSpec((tq, 512), lambda b, i: (b * nq + i, C_XQ // 512))
    kv = pl.BlockSpec((1, MEM_LEN, 1024), lambda b, i: (b, 0, 0))
    o = pl.BlockSpec((tq, 512), lambda b, i: (b * nq + i, 0))
    return nq, q, kv, o


def _attn_fwd(proj, kv, Bl, S):
    tq = _pick(S, (512, 256, 128))
    nq, qs, kvs, os_ = _attn_specs(S, tq)

    def body(q_ref, kv_ref, o_ref):
        for h in range(N_HEAD):
            sl = slice(h * HEAD, (h + 1) * HEAD)
            p = _attn_probs(q_ref[:, sl], kv_ref[0, :, sl])
            o_ref[:, sl] = _raw_dot(p, kv_ref[0, :, 512 + h * HEAD:512 + (h + 1) * HEAD], "nn").astype(bf16)

    return _call(body, "attn_fwd", (Bl, nq), [qs, kvs], os_, jax.ShapeDtypeStruct((Bl * S, 512), bf16),
                 sem=("parallel", "parallel"))(proj, kv)


def _attn_bwd(proj, kv, dc, Bl, S):
    tq = _pick(S, (512, 256, 128))
    nq, qs, kvs, os_ = _attn_specs(S, tq)

    def body(q_ref, kv_ref, do_ref, dq_ref, dkv_ref):
        @pl.when(pl.program_id(1) == 0)
        def _():
            dkv_ref[...] = jnp.zeros_like(dkv_ref)

        for h in range(N_HEAD):
            sl = slice(h * HEAD, (h + 1) * HEAD)
            vsl = slice(512 + h * HEAD, 512 + (h + 1) * HEAD)
            q, k, v, do = q_ref[:, sl], kv_ref[0, :, sl], kv_ref[0, :, vsl], do_ref[:, sl]
            p = _attn_probs(q, k)
            dkv_ref[0, :, vsl] += _raw_dot(p, do, "tn")
            dp = _raw_dot(do, v, "nt")
            ds = p * (dp - jnp.sum(dp * p, axis=-1, keepdims=True)) * (HEAD ** -0.5)
            dq_ref[:, sl] = _raw_dot(ds, k, "nn").astype(bf16)
            dkv_ref[0, :, sl] += _raw_dot(ds, q, "tn")

    return _call(body, "attn_bwd", (Bl, nq), [qs, kvs, os_], (os_, kvs),
                 (jax.ShapeDtypeStruct((Bl * S, 512), bf16), jax.ShapeDtypeStruct((Bl, MEM_LEN, 1024), f32)),
                 sem=("arbitrary", "arbitrary"))(proj, kv, dc)


def _merge_specs(tm, tn):
    br = pl.BlockSpec((tm, 512), lambda i, j: (i, 0))
    w = pl.BlockSpec((512, tn), lambda i, j: (0, j))
    gl = [pl.BlockSpec((tm, tn), functools.partial(lambda i, j, n: (i, (C_GL + n * D_MODEL) // tn + j), n=n)) for n in range(3)]
    return [br, br, br, w, w, w, *gl]


def _merge_fwd(branches, wb, proj):
    T = proj.shape[0]
    tm, tn = _pick(T, (1024, 512, 256, 128)), 512

    def body(a_ref, b_ref, c_ref, w0, w1, w2, g0, g1, g2, o_ref):
        acc = jnp.zeros((tm, tn), f32)
        for x_ref, w_ref, g_ref in ((a_ref, w0, g0), (b_ref, w1, g1), (c_ref, w2, g2)):
            acc = acc + jax.nn.sigmoid(g_ref[...].astype(f32)) * _raw_dot(x_ref[...], w_ref[...], "nn")
        o_ref[...] = acc.astype(bf16)

    return _call(body, "merge_fwd", (T // tm, D_MODEL // tn), _merge_specs(tm, tn), pl.BlockSpec((tm, tn), lambda i, j: (i, j)),
                 jax.ShapeDtypeStruct((T, D_MODEL), bf16), sem=("parallel", "parallel"))(*branches, *wb, proj, proj, proj)


def _merge_bwd(branches, wb, proj, dmerged):
    T = proj.shape[0]
    tm, tn = _pick(T, (1024, 512, 256, 128)), 512

    def body(a_ref, b_ref, c_ref, w0, w1, w2, g0, g1, g2, dm_ref, dgl_ref, d0, d1, d2):
        dm = dm_ref[...]
        for n, (x_ref, w_ref, g_ref, d_ref) in enumerate(((a_ref, w0, g0, d0), (b_ref, w1, g1, d1), (c_ref, w2, g2, d2))):
            up = _raw_dot(x_ref[...], w_ref[...], "nn")
            logits = g_ref[...].astype(f32)
            dgl_ref[n] = _egrad(jax.nn.sigmoid, logits, dm * up).astype(bf16)
            d_ref[...] = (dm * jax.nn.sigmoid(logits)).astype(bf16)

    blk = pl.BlockSpec((tm, tn), lambda i, j: (i, j))
    sh = jax.ShapeDtypeStruct((T, D_MODEL), bf16)
    outs = _call(body, "merge_bwd", (T // tm, D_MODEL // tn), [*_merge_specs(tm, tn), blk],
                 (pl.BlockSpec((3, tm, tn), lambda i, j: (0, i, j)), blk, blk, blk),
                 (jax.ShapeDtypeStruct((3, T, D_MODEL), bf16), sh, sh, sh),
                 sem=("parallel", "parallel"))(*branches, *wb, proj, proj, proj, dmerged)
    return outs[0], outs[1:]


CONV_TC = 256


def _shift_down(a, k):
    row = lax.broadcasted_iota(jnp.int32, a.shape, 0)
    return jnp.where(row >= k, pltpu.roll(a, k, 0), 0.0)


def _shift_up(a, k):
    n = a.shape[0]
    row = lax.broadcasted_iota(jnp.int32, a.shape, 0)
    return jnp.where(row < n - k, pltpu.roll(a, n - k, 0), 0.0)


def _conv_pre(a, cw, cb):
    return cb + cw[0:1] * _shift_down(a, 2) + cw[1:2] * _shift_down(a, 1) + cw[2:3] * a


def _conv_fwd(ab, cw, cb, Bl, S):
    nc = D_FF // CONV_TC

    def body(a_ref, b_ref, cw_ref, cb_ref, o_ref):
        ac = _conv_pre(a_ref[0].astype(f32), cw_ref[...], cb_ref[...])
        o_ref[0] = (jax.nn.silu(ac) * b_ref[0].astype(f32)).astype(bf16)

    return _call(body, "conv_fwd", (Bl, nc),
                 [pl.BlockSpec((1, S, CONV_TC), lambda b, c: (b, 0, c)), pl.BlockSpec((1, S, CONV_TC), lambda b, c: (b, 0, nc + c)),
                  pl.BlockSpec((3, CONV_TC), lambda b, c: (0, c)), pl.BlockSpec((1, CONV_TC), lambda b, c: (0, c))],
                 pl.BlockSpec((1, S, CONV_TC), lambda b, c: (b, 0, c)), jax.ShapeDtypeStruct((Bl, S, D_FF), bf16),
                 sem=("parallel", "parallel"))(ab, ab, cw, cb)


def _conv_bwd(ab, cw, cb, dact, Bl, S):
    nc = D_FF // CONV_TC

    def body(a_ref, b_ref, cw_ref, cb_ref, d_ref, da_ref, db_ref, dcw_ref, dcb_ref):
        @pl.when(pl.program_id(1) == 0)
        def _():
            dcw_ref[...] = jnp.zeros_like(dcw_ref)
            dcb_ref[...] = jnp.zeros_like(dcb_ref)

        a, cw = a_ref[0].astype(f32), cw_ref[...]
        ac = _conv_pre(a, cw, cb_ref[...])
        dact_ = d_ref[0].astype(f32)
        db_ref[0] = (dact_ * jax.nn.silu(ac)).astype(bf16)
        dac = _egrad(jax.nn.silu, ac, dact_ * b_ref[0].astype(f32))
        da_ref[0] = (cw[2:3] * dac + cw[1:2] * _shift_up(dac, 1) + cw[0:1] * _shift_up(dac, 2)).astype(bf16)
        dcw_ref[0:1, :] += jnp.sum(dac * _shift_down(a, 2), axis=0, keepdims=True)
        dcw_ref[1:2, :] += jnp.sum(dac * _shift_down(a, 1), axis=0, keepdims=True)
        dcw_ref[2:3, :] += jnp.sum(dac * a, axis=0, keepdims=True)
        dcb_ref[...] += jnp.sum(dac, axis=0, keepdims=True)

    seq = pl.BlockSpec((1, S, CONV_TC), lambda c, b: (b, 0, c))
    return _call(body, "conv_bwd", (nc, Bl),
                 [seq, pl.BlockSpec((1, S, CONV_TC), lambda c, b: (b, 0, nc + c)), pl.BlockSpec((3, CONV_TC), lambda c, b: (0, c)),
                  pl.BlockSpec((1, CONV_TC), lambda c, b: (0, c)), seq],
                 (seq, seq, pl.BlockSpec((3, CONV_TC), lambda c, b: (0, c)), pl.BlockSpec((1, CONV_TC), lambda c, b: (0, c))),
                 (jax.ShapeDtypeStruct((Bl, S, D_FF), bf16), jax.ShapeDtypeStruct((Bl, S, D_FF), bf16),
                  jax.ShapeDtypeStruct((3, D_FF), f32), jax.ShapeDtypeStruct((1, D_FF), f32)),
                 sem=("arbitrary", "arbitrary"))(ab, ab, cw, cb, dact)


def _local_step(x, mem, target, p, w_in, late_b, late_c, send, settle):
    Bl, S, Dd = x.shape
    T = Bl * S
    x2d, t2d, mem2d = x.reshape(T, Dd), target.reshape(T, Dd), mem.reshape(Bl * MEM_LEN, Dd)
    b_st = jnp.pad(p["b_spatial"].T, ((0, 0), (0, 128 - N_HEAD)))
    lbl = p["lb_logits"]

    h, h_t = _rms_fwd(x2d, p["norm1_g"], "norm1_fwd", transposed=True)
    proj = _mm(h, w_in, "nn", bf16, "proj_fwd", 1024, 1664)
    a_out = _gmlp_fwd(proj, p["ln_v_g"], p["ln_v_b"], p["w_spatial"], b_st)
    proj3 = proj.reshape(Bl, S, IN_WIDTH)
    b_out, states = _hgrn_fwd(proj3, lbl, p["hgrn_norm_g"], Bl, S)
    b_out = b_out.reshape(T, 512)
    memn = _rms_fwd(mem2d, p["mem_norm_g"], "memnorm_fwd")
    w = late_b(b_out)
    wb = w["w_branch"]
    kv = _mm(memn, w["w_mem_kv"], "nn", f32, "kv_fwd", 512, 1024).reshape(Bl, MEM_LEN, 2 * 512)
    c_out = _attn_fwd(proj, kv, Bl, S)
    branches = (a_out, b_out, c_out)
    merged = _merge_fwd(branches, wb, proj)
    x1 = _mm(merged, w["w_out"], "nn", f32, "out_fwd", 1024, 1024, residual=x2d)
    h2, h2_t = _rms_fwd(x1, p["norm2_g"], "norm2_fwd", transposed=True)
    w.update(late_c(h2))
    ab = _mm(h2, w["w_up"], "nn", bf16, "up_fwd", 1024, 1408)
    act = _conv_fwd(ab.reshape(Bl, S, 2 * D_FF), w["conv_w"], p["conv_b"], Bl, S).reshape(T, D_FF)
    x2 = _mm(act, w["w_down"], "nn", f32, "down_fwd", 512, 1024, residual=x1)
    loss_part, dx2, g_final = _final_loss(x2, p["final_g"], t2d)

    g_w_down = _mm(act, dx2, "tn", bf16, "down_dw", 1408, 1024, 1024)
    dact = _mm(dx2, w["w_down"], "nt", bf16, "down_dx", 1024, 1408)
    da, db, g_conv_w, g_conv_b = _conv_bwd(ab.reshape(Bl, S, 2 * D_FF), w["conv_w"], p["conv_b"], dact.reshape(Bl, S, D_FF), Bl, S)
    dab = jnp.concatenate([da.reshape(T, D_FF), db.reshape(T, D_FF)], axis=-1)
    g_w_up = _mm(h2_t, dab, "nn", bf16, "up_dw", 512, 1408, shard=2 * D_FF // N_DEV, n_outer=True)
    tok = send("c", dict(w_up=g_w_up, conv_w=g_conv_w, w_down=g_w_down))
    dh2 = _mm(dab, w["w_up"], "nt", f32, "up_dx", 512, 1024)
    dx1, g_norm2 = _rms_bwd(x1, p["norm2_g"] + tok[0, 0], dh2, "norm2_bwd", residual=dx2)

    g_w_out = _mm(merged, dx1, "tn", bf16, "out_dw", 1024, 1024, 1024)
    dmerged = _mm(dx1, w["w_out"], "nt", f32, "out_dx", 1024, 1024)
    dgl, dup = _merge_bwd(branches, wb, proj, dmerged)
    g_w_branch = [_mm(branches[n], dup[n], "tn", bf16, f"branch_dw{n}", 512, 1024, 1024) for n in range(3)]
    dbr = [_mm(dup[n], wb[n], "nt", bf16, f"branch_dx{n}", 1024, 512) for n in range(3)]
    dxq, dkv = _attn_bwd(proj, kv, dbr[2], Bl, S)
    dkv = dkv.reshape(Bl * MEM_LEN, 2 * 512)
    g_w_kv = _mm(memn, dkv, "tn", bf16, "kv_dw", 1024, 1024, 512)
    tok = send("b", dict(w_mem_kv=g_w_kv, w_branch=g_w_branch, w_out=g_w_out))
    dmemn = _mm(dkv, w["w_mem_kv"], "nt", f32, "kv_dx", 512, 1024)
    _, g_mem_norm = _rms_bwd(mem2d, p["mem_norm_g"], dmemn, "memnorm_bwd")
    dzuv, g_ln_g, g_ln_b, g_w_sp, g_b_sp = _gmlp_bwd(proj, p["ln_v_g"] + tok[0, 0], p["ln_v_b"], p["w_spatial"], b_st, dbr[0])
    *dqfig, g_lbl, g_ng = _hgrn_bwd(proj3, lbl, p["hgrn_norm_g"], states, dbr[1].reshape(Bl, S, 512), Bl, S)
    dq, df, di, dg = [d.reshape(T, 512) for d in dqfig]
    dproj = jnp.concatenate([dzuv, dq, df, di, dg, settle(dxq), dgl[0], dgl[1], dgl[2]], axis=-1)
    g_w_in = _mm(h_t, dproj, "nn", bf16, "proj_dw", 512, 1664, shard=IN_WIDTH // N_DEV, n_outer=True)
    tok = send("a", dict(w_in=g_w_in))
    dh = _mm(dproj, w_in, "nt", f32, "proj_dx", 512, 1024)
    dx, g_norm1 = _rms_bwd(x2d, p["norm1_g"] + tok[0, 0], dh, "norm1_bwd", residual=dx1)

    gs = dict(w_spatial=g_w_sp, norm1_g=g_norm1, mem_norm_g=g_mem_norm, norm2_g=g_norm2, final_g=g_final, lb_logits=g_lbl,
              ln_v_g=g_ln_g, ln_v_b=g_ln_b, b_spatial=g_b_sp, hgrn_norm_g=g_ng, conv_b=g_conv_b)
    return loss_part, dx.reshape(Bl, S, Dd), gs


def _coords():
    return lax.axis_index("x"), lax.axis_index("y"), lax.axis_index("c")


def _slot(dev):
    return 4 * dev[0] + 2 * dev[1] + dev[2]


def _comm_call(body, name, arrays, out_shapes, n_sem):
    n = len(arrays)
    hbm = pl.BlockSpec(memory_space=pl.ANY)
    return pl.pallas_call(
        body, name=name, out_shape=out_shapes, in_specs=[hbm] * n, out_specs=[hbm] * n,
        scratch_shapes=[pltpu.SemaphoreType.DMA((n_sem, n)), pltpu.SemaphoreType.DMA((n_sem, n)), pltpu.SemaphoreType.DMA((n,))])(*arrays)


def _all_gather(blocks, name):
    n = len(blocks)

    def body(*refs):
        x_refs, o_refs, (send_sems, recv_sems, local_sems) = refs[:n], refs[n:2 * n], refs[2 * n:]
        x, y, c = _coords()
        me, sibling = (x, y, c), (x, y, 1 - c)
        chips = [(1 - x, y), (x, 1 - y), (1 - x, 1 - y)]

        def copy(a, k, block_dev, to, from_input=False):
            dst = o_refs[a].at[_slot(block_dev)]
            return pltpu.make_async_remote_copy(src_ref=x_refs[a] if from_input else dst, dst_ref=dst, send_sem=send_sems.at[k, a],
                                                recv_sem=recv_sems.at[k, a], device_id=to, device_id_type=MESH)

        mine = [pltpu.make_async_copy(x_refs[a], o_refs[a].at[_slot(me)], local_sems.at[a]) for a in range(n)]
        first = [copy(a, 0, me, sibling, True) for a in range(n)]
        first += [copy(a, 1 + j, me, (*chip, c), True) for j, chip in enumerate(chips) for a in range(n)]
        for cp in mine + first:
            cp.start()
        passed = []
        for j, chip in enumerate(chips):
            for a in range(n):
                copy(a, 1 + j, (*chip, c), me).wait_recv()
                fwd = copy(a, 4 + j, (*chip, c), sibling)
                fwd.start()
                passed.append(fwd)
        for a in range(n):
            copy(a, 0, sibling, me).wait_recv()
        for j, chip in enumerate(chips):
            for a in range(n):
                copy(a, 4 + j, (*chip, 1 - c), me).wait_recv()
        for cp in first + passed:
            cp.wait_send()
        for cp in mine:
            cp.wait()

    return _comm_call(body, name, blocks, [jax.ShapeDtypeStruct((N_DEV,) + b.shape, b.dtype) for b in blocks], 7)


def _all_to_all(parts, name):
    n = len(parts)
    rel = [(0, 0, 1), (0, 1, 0), (0, 1, 1), (1, 0, 0), (1, 0, 1), (1, 1, 0), (1, 1, 1)]

    def body(*refs):
        x_refs, o_refs, (send_sems, recv_sems, local_sems) = refs[:n], refs[n:2 * n], refs[2 * n:]
        x, y, c = _coords()
        me = (x, y, c)
        peers = [(x ^ dx, y ^ dy, c ^ dc) for dx, dy, dc in rel]

        def copy(a, k, peer):
            return pltpu.make_async_remote_copy(src_ref=x_refs[a].at[_slot(peer)], dst_ref=o_refs[a].at[_slot(me)], send_sem=send_sems.at[k, a],
                                                recv_sem=recv_sems.at[k, a], device_id=peer, device_id_type=MESH)

        def arrival(a, k, peer):
            return pltpu.make_async_remote_copy(src_ref=x_refs[a].at[_slot(me)], dst_ref=o_refs[a].at[_slot(peer)], send_sem=send_sems.at[k, a],
                                                recv_sem=recv_sems.at[k, a], device_id=peer, device_id_type=MESH)

        mine = [pltpu.make_async_copy(x_refs[a].at[_slot(me)], o_refs[a].at[_slot(me)], local_sems.at[a]) for a in range(n)]
        sends = [copy(a, k, peer) for k, peer in enumerate(peers) for a in range(n)]
        for cp in mine + sends:
            cp.start()
        for k, peer in enumerate(peers):
            for a in range(n):
                arrival(a, k, peer).wait_recv()
        for cp in sends:
            cp.wait_send()
        for cp in mine:
            cp.wait()

    return _comm_call(body, name, parts, [jax.ShapeDtypeStruct(p.shape, p.dtype) for p in parts], 7)


_HBM = pl.BlockSpec(memory_space=pltpu.HBM)
_SEM = pl.BlockSpec(memory_space=pltpu.SEMAPHORE)
_REL = [(0, 0, 1), (0, 1, 0), (0, 1, 1), (1, 0, 0), (1, 0, 1), (1, 1, 0), (1, 1, 1)]


_LINK_ORDER = (3, 1, 5, 4, 2, 6, 0)
SEND_PIECES = 4


def _pieces(shape, dtype):
    rows = shape[0]
    unit = 1 if len(shape) > 2 else (16 if dtype == bf16 else 8)
    for n in (SEND_PIECES, 2):
        if rows % (n * unit) == 0:
            return [pl.ds(i * (rows // n), rows // n) for i in range(n)]
    return [pl.ds(0, rows)]


def _split_copies(gather, src, land, send, recv, pieces):
    x, y, c = _coords()
    me = (x, y, c)
    copies = []
    for a in range(len(src)):
        block = src[a].shape if gather else src[a].shape[1:]
        for rows in (_pieces(block, src[a].dtype) if pieces else [None]):
            for k in _LINK_ORDER:
                dx, dy, dc = _REL[k]
                peer = (x ^ dx, y ^ dy, c ^ dc)
                mine, there = (src[a] if gather else src[a].at[_slot(peer)]), land[a].at[_slot(me)]
                if rows is not None:
                    mine, there = mine.at[rows], there.at[rows]
                copies.append(pltpu.make_async_remote_copy(src_ref=mine, dst_ref=there, send_sem=send[a].at[k], recv_sem=recv[a].at[k],
                                                           device_id=peer, device_id_type=MESH))
    return me, copies


def _arrivals(gather, src, land, send, recv):
    x, y, c = _coords()
    out = []
    for a in range(len(src)):
        for k, (dx, dy, dc) in enumerate(_REL):
            peer = (x ^ dx, y ^ dy, c ^ dc)
            out.append(pltpu.make_async_remote_copy(src_ref=src[a] if gather else src[a].at[_slot(peer)], dst_ref=land[a].at[_slot(peer)],
                                                    send_sem=send[a].at[k], recv_sem=recv[a].at[k], device_id=peer, device_id_type=MESH))
    return out


def _exchange_start(arrays, gather, name, after=None):
    n = len(arrays)
    e = 0 if after is None else 1
    lands = [lax.empty(((N_DEV,) + a.shape) if gather else a.shape, a.dtype) for a in arrays]

    def body(*refs):
        src, land = refs[:n], refs[n:2 * n]
        refs = refs[2 * n + e:]
        send, recv, token, local_sems = refs[:n], refs[n:2 * n], refs[4 * n], refs[4 * n + 1]
        me, out = _split_copies(gather, src, land, send, recv, True)
        local = [pltpu.make_async_copy(src[a] if gather else src[a].at[_slot(me)], land[a].at[_slot(me)], local_sems.at[a])
                 for a in range(n)]
        for cp in local:
            cp.start()
        for cp in local:
            cp.wait()
        for cp in out:
            cp.start()
        token[...] = jnp.zeros_like(token)

    sems = [pltpu.SemaphoreType.DMA((7,)) for _ in range(2 * n)]
    outs = pl.pallas_call(
        body, name=name,
        out_shape=(*sems, *[pltpu.HBM(a.shape, a.dtype) for a in arrays], *[pltpu.HBM(l.shape, l.dtype) for l in lands],
                   jax.ShapeDtypeStruct((8, 128), f32)),
        in_specs=[_HBM] * (2 * n) + [pl.BlockSpec(memory_space=pl.ANY)] * e,
        out_specs=(*[_SEM] * (2 * n), *[_HBM] * (2 * n), pl.BlockSpec(memory_space=pltpu.VMEM)),
        input_output_aliases={i: 2 * n + i for i in range(2 * n)},
        scratch_shapes=[pltpu.SemaphoreType.DMA((n,))],
        compiler_params=pltpu.CompilerParams(has_side_effects=pltpu.SideEffectType.DATAFLOW_SIDE_EFFECTING))(
        *[pltpu.with_memory_space_constraint(a, pltpu.HBM) for a in arrays],
        *[pltpu.with_memory_space_constraint(l, pltpu.HBM) for l in lands], *([after] if e else []))
    return (gather, n, outs[:4 * n]), outs[4 * n]


def _exchange_wait(handle, which, after, name):
    gather, n_all, vals = handle
    send_v, recv_v, src_v, land_v = [[vals[g * n_all + i] for i in which] for g in range(4)]
    n = len(which)

    def body(*refs):
        src, land, send, recv = refs[:n], refs[n:2 * n], refs[2 * n:3 * n], refs[3 * n:4 * n]
        for cp in _split_copies(gather, src, land, send, recv, False)[1]:
            cp.wait_send()
        for cp in _arrivals(gather, src, land, send, recv):
            cp.wait_recv()

    outs = pl.pallas_call(
        body, name=name,
        out_shape=(*[pltpu.HBM(a.shape, a.dtype) for a in src_v], *[pltpu.HBM(l.shape, l.dtype) for l in land_v]),
        in_specs=[*[_HBM] * (2 * n), *[_SEM] * (2 * n), pl.BlockSpec(memory_space=pl.ANY)], out_specs=[_HBM] * (2 * n),
        input_output_aliases={i: i for i in range(2 * n)},
        compiler_params=pltpu.CompilerParams(has_side_effects=pltpu.SideEffectType.DATAFLOW_SIDE_EFFECTING))(
        *src_v, *land_v, *send_v, *recv_v, after)
    return outs[n:]


def _seq_exchange(arrays, gather, name, collective_id):
    n = len(arrays)
    hbm = pltpu.MemorySpace.HBM
    srcs = [jax.new_ref(a, memory_space=hbm) for a in arrays]
    lands = [jax.empty_ref(jax.ShapeDtypeStruct(((N_DEV,) + a.shape) if gather else a.shape, a.dtype), memory_space=hbm) for a in arrays]

    @pl.kernel(mesh=plsc.ScalarSubcoreMesh(axis_name="sequencer", num_cores=1), name=name,
               scratch_types=(pltpu.SemaphoreType.DMA((7, n)), pltpu.SemaphoreType.DMA((7, n)), pltpu.SemaphoreType.DMA((n,))),
               compiler_params=pltpu.CompilerParams(collective_id=collective_id))
    def launch(send, recv, local):
        x, y, c = _coords()
        me = (x, y, c)
        peers = [(x ^ dx, y ^ dy, c ^ dc) for dx, dy, dc in _REL]
        barrier = pltpu.get_barrier_semaphore()
        for peer in peers:
            pl.semaphore_signal(barrier, inc=1, device_id=peer, device_id_type=MESH)
        pl.semaphore_wait(barrier, len(peers))

        def copy(a, k, peer, arrival):
            return pltpu.make_async_remote_copy(
                src_ref=srcs[a] if gather else srcs[a].at[_slot(peer)], dst_ref=lands[a].at[_slot(peer if arrival else me)],
                send_sem=send.at[k, a], recv_sem=recv.at[k, a], device_id=peer, device_id_type=MESH)

        mine = [pltpu.make_async_copy(srcs[a] if gather else srcs[a].at[_slot(me)], lands[a].at[_slot(me)], local.at[a])
                for a in range(n)]
        out = [copy(a, k, peer, False) for a in range(n) for k, peer in enumerate(peers)]
        for cp in mine + out:
            cp.start()
        for a in range(n):
            for k, peer in enumerate(peers):
                copy(a, k, peer, True).wait_recv()
        for cp in out:
            cp.wait_send()
        for cp in mine:
            cp.wait()

    launch()
    return [land[...] for land in lands]


def _adam_math(w, g, m, v):
    m_ = ADAM_B1 * m + (1.0 - ADAM_B1) * g
    v_ = ADAM_B2 * v + (1.0 - ADAM_B2) * jnp.square(g)
    m_hat = m_ / (1.0 - ADAM_B1 ** ADAM_STEP)
    v_hat = v_ / (1.0 - ADAM_B2 ** ADAM_STEP)
    return -ADAM_LR * (m_hat / (jnp.sqrt(v_hat) + ADAM_EPS) + ADAM_WD * w), m_, v_


def _reduce_adamw(parts, w, m, v, name):
    _, R, L = parts.shape
    tr = _pick(R, (256, 128, 64, 32, 16, 8))

    def body(p_ref, w_ref, m_ref, v_ref, g_ref, d_ref, nm_ref, nv_ref):
        g = p_ref[0].astype(f32)
        for i in range(1, N_DEV):
            g = g + p_ref[i].astype(f32)
        g_ref[...] = g
        d_ref[...], nm_ref[...], nv_ref[...] = _adam_math(w_ref[...], g, m_ref[...], v_ref[...])

    blk = pl.BlockSpec((tr, L), lambda i: (i, 0))
    sh = jax.ShapeDtypeStruct((R, L), f32)
    return _call(body, name, (R // tr,), [pl.BlockSpec((N_DEV, tr, L), lambda i: (0, i, 0)), blk, blk, blk], (blk,) * 4, (sh,) * 4,
                 sem=("parallel",))(parts, w, m, v)


SMALL = (("w_spatial", (512, 128), 0), ("norm1_g", (1, 1024), 512), ("mem_norm_g", (1, 1024), 520), ("norm2_g", (1, 1024), 528),
         ("final_g", (1, 1024), 536), ("lb_logits", (2, 512), 544), ("ln_v_g", (1, 512), 552), ("ln_v_b", (1, 512), 556),
         ("b_spatial", (4, 128), 560), ("hgrn_norm_g", (1, 128), 564), ("conv_b", (1, 2816), 565))
LOSS_ROW, SMALL_USED, SMALL_ROWS = 587, 588, 640


def _segments(shape, base):
    r, n = shape
    per = n // 128
    return [(base + i * per + j, i, slice(j * 128, (j + 1) * 128)) for i in range(r) for j in range(per)]


def _pack_small(gs, loss_part):
    names = [n for n, _, _ in SMALL]

    def body(*refs):
        src, loss_ref, o_ref = dict(zip(names, refs[:-2])), refs[-2], refs[-1]
        o_ref[SMALL_USED:SMALL_ROWS, :] = jnp.zeros((SMALL_ROWS - SMALL_USED, 128), f32)
        o_ref[LOSS_ROW:LOSS_ROW + 1, :] = loss_ref[...]
        for name, shape, base in SMALL:
            ref = src[name]
            if name == "w_spatial":
                o_ref[base:base + 512, :] = ref[...].reshape(512, 128)
            elif name == "b_spatial":
                o_ref[base:base + 4, :] = ref[0:4, :]
            elif name == "hgrn_norm_g":
                per_head = [ref[b, h] for b in range(ref.shape[0]) for h in range(N_HEAD)]
                o_ref[base:base + 1, :] = functools.reduce(lambda u, v_: u + v_, per_head)
            else:
                for row, i, sl in _segments(shape, base):
                    o_ref[row:row + 1, :] = ref[i:i + 1, sl]

    return pl.pallas_call(body, name="pack_small", out_shape=jax.ShapeDtypeStruct((SMALL_ROWS, 128), f32))(
        *[gs[n] for n in names], loss_part)


def _small_update(gathered, w, m, v):
    names = [n for n, _, _ in SMALL]
    k = len(names)

    def body(*refs):
        p_ref = refs[0]
        ins = [dict(zip(names, refs[1 + i * k:1 + (i + 1) * k])) for i in range(3)]
        outs = [dict(zip(names, refs[1 + (3 + i) * k:1 + (4 + i) * k])) for i in range(4)]
        loss_ref, gsum = refs[-2], refs[-1]
        g = p_ref[0]
        for i in range(1, N_DEV):
            g = g + p_ref[i]
        gsum[...] = g
        loss_ref[...] = gsum[LOSS_ROW:LOSS_ROW + 1, :]
        for name, shape, base in SMALL:
            if name == "w_spatial":
                where = [(slice(base, base + 512), (slice(None), slice(None)))]
            else:
                where = [(slice(row, row + 1), (slice(i, i + 1), sl)) for row, i, sl in _segments(shape, base)]
            for rows, at in where:
                g_ = gsum[rows, :]
                d_, m_, v_ = _adam_math(ins[0][name][at], g_, ins[1][name][at], ins[2][name][at])
                for o, val in zip(outs, (g_, d_, m_, v_)):
                    o[name][at] = val

    args = [gathered] + [d[n] for d in (w, m, v) for n in names]
    out_shapes = [jax.ShapeDtypeStruct(shape, f32) for _ in range(4) for _, shape, _ in SMALL] + [jax.ShapeDtypeStruct((1, 128), f32)]
    outs = pl.pallas_call(body, name="small_update", out_shape=out_shapes, scratch_shapes=[pltpu.VMEM((SMALL_ROWS, 128), f32)])(*args)
    return [dict(zip(names, outs[i * k:(i + 1) * k])) for i in range(4)], outs[-1]


def _cols_full(g):
    return jnp.moveaxis(g, 0, -2).reshape(g.shape[1:-1] + (N_DEV * g.shape[-1],))


def _cols_parts(full):
    n = full.shape[-1] // N_DEV
    return jnp.moveaxis(full.reshape(full.shape[:-1] + (N_DEV, n)), -2, 0)


def kernel(x, mem, norm1_g, w_in, ln_v_g, ln_v_b, w_spatial, b_spatial, lb_logits, hgrn_norm_g, mem_norm_g, w_mem_kv, w_branch, w_out, norm2_g, w_up, conv_w, conv_b, w_down, final_g, loss_target, m_norm1_g, m_w_in, m_ln_v_g, m_ln_v_b, m_w_spatial, m_b_spatial, m_lb_logits, m_hgrn_norm_g, m_mem_norm_g, m_w_mem_kv, m_w_branch, m_w_out, m_norm2_g, m_w_up, m_conv_w, m_conv_b, m_w_down, m_final_g, v_norm1_g, v_w_in, v_ln_v_g, v_ln_v_b, v_w_spatial, v_b_spatial, v_lb_logits, v_hgrn_norm_g, v_mem_norm_g, v_w_mem_kv, v_w_branch, v_w_out, v_norm2_g, v_w_up, v_conv_w, v_conv_b, v_w_down, v_final_g):
    given = dict(locals())
    order = ("norm1_g", "w_in", "ln_v_g", "ln_v_b", "w_spatial", "b_spatial", "lb_logits", "hgrn_norm_g", "mem_norm_g",
             "w_mem_kv", "w_branch", "w_out", "norm2_g", "w_up", "conv_w", "conv_b", "w_down", "final_g")
    groups = dict(a=("w_in",), b=("w_mem_kv", "w_branch", "w_out"), c=("w_up", "conv_w", "w_down"))

    wire = {n: given[n][0].astype(f32 if n == "conv_w" else bf16) for ns in groups.values() for n in ns}
    g_in = _all_gather([wire["w_in"]], "gather_w_in")[0]
    late = groups["b"] + groups["c"]
    w_in_full = _cols_full(g_in)
    w_in_full, rest_wire = lax.optimization_barrier((w_in_full, [wire[n] for n in late]))
    rest = _seq_exchange(rest_wire, True, "gather_rest", 1)

    def late_b(after):
        _, (kv_, br_, out_) = lax.optimization_barrier((after, tuple(rest[0:3])))
        br_ = _cols_full(br_)
        return dict(w_mem_kv=kv_.reshape(D_MODEL, 2 * 512), w_branch=[br_[n] for n in range(3)], w_out=out_.reshape(D_MODEL, D_MODEL))

    def late_c(after):
        _, (up_, cw_, down_) = lax.optimization_barrier((after, tuple(rest[3:6])))
        return dict(w_up=_cols_full(up_), conv_w=_cols_full(cw_), w_down=down_.reshape(D_FF, D_MODEL))

    to_parts = dict(w_in=lambda g_: g_, w_up=lambda g_: g_, conv_w=_cols_parts,
                    w_branch=lambda g_: _cols_parts(jnp.stack(g_)).reshape(N_DEV, -1, 128),
                    w_mem_kv=lambda g_: g_.reshape(N_DEV, -1, 2 * 512), w_out=lambda g_: g_.reshape(N_DEV, -1, D_MODEL),
                    w_down=lambda g_: g_.reshape(N_DEV, -1, D_MODEL))
    scatters = {}

    def send(tag, grads_):
        parts = [to_parts[n](grads_[n]) for n in groups[tag]]
        scatters[tag] = _seq_exchange(parts, False, f"scatter_{tag}", dict(a=2, b=4, c=5)[tag])
        return jnp.zeros((8, 128), f32)

    small_2d = lambda prefix: {n: given[prefix + n].reshape(shape) for n, shape, _ in SMALL}
    p = small_2d("")
    p["w_spatial"] = w_spatial[0]
    updates = {}

    def update(tag):
        for n, parts in zip(groups[tag], scatters[tag]):
            two_d = (-1, given[n].shape[-1])
            updates[n] = _reduce_adamw(parts, *[given[pre + n].reshape(two_d) for pre in ("", "m_", "v_")], "adamw_" + n)

    def settle(chain):
        update("c")
        update("b")
        early = groups["c"] + groups["b"]
        chain, tied = lax.optimization_barrier((chain, [updates[n] for n in early]))
        updates.update(zip(early, tied))
        return chain

    loss_part, grad_x, gs = _local_step(x, mem, loss_target, p, w_in_full, late_b, late_c, send, settle)

    gathered = _seq_exchange([_pack_small(gs, loss_part)], True, "gather_small", 3)[0]

    update("a")
    grads, delta, new_m, new_v = {}, {}, {}, {}
    for n, res in updates.items():
        grads[n], delta[n], new_m[n], new_v[n] = [r.reshape(given[n].shape) for r in res]

    small_results, loss_row = _small_update(gathered, small_2d(""), small_2d("m_"), small_2d("v_"))
    for dst, res in zip((grads, delta, new_m, new_v), small_results):
        for n, _, _ in SMALL:
            dst[n] = res[n].reshape(given[n].shape)
    loss = loss_row[0, 0]

    return (loss, grad_x, *[grads[n] for n in order], *[delta[n] for n in order], *[new_m[n] for n in order],
            *[new_v[n] for n in order])
```

```python
import functools

import jax
import jax.numpy as jnp
from jax import lax
from jax.experimental import pallas as pl
from jax.experimental.pallas import tpu as pltpu
from jax.experimental.pallas import tpu_sc as plsc

f32 = jnp.float32
bf16 = jnp.bfloat16

N_DEV = 8
D_MODEL = 1024
EPS = 1e-6
GM_CHUNK = 128
HG_CHUNK = 64
HEAD = 128
N_HEAD = 4
MEM_LEN = 256
D_FF = 2816
IN_WIDTH = 6656
C_ZU, C_HQ, C_HF, C_HI, C_HG, C_XQ, C_GL = 0, 1024, 1536, 2048, 2560, 3072, 3584
ADAM_LR, ADAM_B1, ADAM_B2, ADAM_EPS, ADAM_WD, ADAM_STEP = 0.001, 0.9, 0.999, 1e-08, 0.01, 10
VMEM_LIMIT = 56 * 1024 * 1024
MESH = pl.DeviceIdType.MESH


def _pick(n, cands):
    for c in cands:
        if n % c == 0:
            return c
    return n


def _call(body, name, grid, in_specs, out_specs, out_shape, scratch=(), sem=None, **cp):
    params = dict(vmem_limit_bytes=VMEM_LIMIT, **cp)
    if sem is not None:
        params["dimension_semantics"] = sem
    return pl.pallas_call(
        body, name=name, grid=grid, in_specs=in_specs, out_specs=out_specs, out_shape=out_shape,
        scratch_shapes=list(scratch), compiler_params=pltpu.CompilerParams(**params))


_DN = {"nn": (((1,), (0,)), ((), ())), "nt": (((1,), (1,)), ((), ())), "tn": (((0,), (0,)), ((), ()))}


def _raw_dot(a, b, mode):
    return lax.dot_general(a.astype(bf16), b.astype(bf16), _DN[mode], preferred_element_type=f32)


@jax.custom_vjp
def _dot_nn(a, b):
    return _raw_dot(a, b, "nn")


_dot_nn.defvjp(lambda a, b: (_raw_dot(a, b, "nn"), (a, b)),
               lambda r, g: (_raw_dot(g, r[1], "nt"), _raw_dot(r[0], g, "tn")))


@jax.custom_vjp
def _dot_nt(a, b):
    return _raw_dot(a, b, "nt")


_dot_nt.defvjp(lambda a, b: (_raw_dot(a, b, "nt"), (a, b)),
               lambda r, g: (_raw_dot(g, r[1], "nn"), _raw_dot(g, r[0], "tn")))


@jax.custom_vjp
def _dot_tn(a, b):
    return _raw_dot(a, b, "tn")


_dot_tn.defvjp(lambda a, b: (_raw_dot(a, b, "tn"), (a, b)),
               lambda r, g: (_raw_dot(r[1], g, "nt"), _raw_dot(r[0], g, "nn")))


def _tri(n, lower):
    r = lax.broadcasted_iota(jnp.int32, (n, n), 0)
    c = lax.broadcasted_iota(jnp.int32, (n, n), 1)
    return ((c <= r) if lower else (c >= r)).astype(f32)


def _sel_dot(sel, x, mode, x_first=False):
    hi = x.astype(bf16)
    rest = x - hi.astype(f32)
    mid = rest.astype(bf16)
    lo = (rest - mid.astype(f32)).astype(bf16)
    sel = sel.astype(bf16)
    dot = lambda piece: lax.dot_general(*((piece, sel) if x_first else (sel, piece)), _DN[mode], preferred_element_type=f32)
    return dot(hi) + dot(mid) + dot(lo)


def _egrad(fn, x, ct):
    return jax.vjp(fn, x)[1](ct)[0]


def _mm(a, b, mode, out_dtype, name, tm, tn, tk=None, residual=None, shard=None, n_outer=False):
    if mode == "nn":
        (M, K), (_, N) = a.shape, b.shape
    elif mode == "nt":
        (M, K), (N, _) = a.shape, b.shape
    else:
        (K, M), (_, N) = a.shape, b.shape
    tm, tn = min(tm, M), min(tn, N)
    tk = K if tk is None else min(tk, K)
    assert M % tm == 0 and N % tn == 0 and K % tk == 0, (name, M, N, K, tm, tn, tk)
    nk = K // tk

    def body(*refs):
        acc_ref = refs[-1] if nk > 1 else None
        refs = refs[:-1] if nk > 1 else refs
        if residual is None:
            a_ref, b_ref, o_ref = refs
        else:
            a_ref, b_ref, r_ref, o_ref = refs

        def finish(r):
            if residual is not None:
                r = r + r_ref[...]
            if shard is None:
                o_ref[...] = r.astype(out_dtype)
            else:
                for s in range(tn // shard):
                    o_ref[s] = r[:, s * shard:(s + 1) * shard].astype(out_dtype)

        part = _raw_dot(a_ref[...], b_ref[...], mode)
        if nk == 1:
            finish(part)
            return
        k = pl.program_id(2)

        @pl.when(k == 0)
        def _():
            acc_ref[...] = part

        @pl.when((k > 0) & (k < nk - 1))
        def _():
            acc_ref[...] += part

        @pl.when(k == nk - 1)
        def _():
            finish(acc_ref[...] + part)

    def at(index):
        return (lambda j, i, k: index(i, j, k)) if n_outer else index

    a_spec = {"nn": pl.BlockSpec((tm, tk), at(lambda i, j, k: (i, k))),
              "nt": pl.BlockSpec((tm, tk), at(lambda i, j, k: (i, k))),
              "tn": pl.BlockSpec((tk, tm), at(lambda i, j, k: (k, i)))}[mode]
    b_spec = {"nn": pl.BlockSpec((tk, tn), at(lambda i, j, k: (k, j))),
              "nt": pl.BlockSpec((tn, tk), at(lambda i, j, k: (j, k))),
              "tn": pl.BlockSpec((tk, tn), at(lambda i, j, k: (k, j)))}[mode]
    o_spec = pl.BlockSpec((tm, tn), at(lambda i, j, k: (i, j)))
    in_specs = [a_spec, b_spec] + ([o_spec] if residual is not None else [])
    args = (a, b) + ((residual,) if residual is not None else ())
    out_shape = jax.ShapeDtypeStruct((M, N), out_dtype)
    if shard is not None:
        assert residual is None and tn % shard == 0
        o_spec = pl.BlockSpec((tn // shard, tm, shard), at(lambda i, j, k: (j, i, 0)))
        out_shape = jax.ShapeDtypeStruct((N // shard, M, shard), out_dtype)
    grid = (N // tn, M // tm, nk) if n_outer else (M // tm, N // tn, nk)
    return _call(body, name, grid, in_specs, o_spec, out_shape,
                 scratch=[pltpu.VMEM((tm, tn), f32)] if nk > 1 else [], sem=("parallel", "parallel", "arbitrary"))(*args)


def _rms_fwd(x, g, name, transposed=False):
    R, Dd = x.shape
    tr = _pick(R, (512, 256, 128))

    def body(x_ref, g_ref, o_ref, *t_ref):
        xf = x_ref[...]
        y = xf * lax.rsqrt(jnp.mean(xf * xf, axis=-1, keepdims=True) + EPS) * g_ref[...]
        o_ref[...] = y.astype(bf16)
        if transposed:
            t_ref[0][...] = y.T.astype(bf16)

    row = pl.BlockSpec((tr, Dd), lambda i: (i, 0))
    out_specs, out_shape = row, jax.ShapeDtypeStruct((R, Dd), bf16)
    if transposed:
        out_specs, out_shape = (row, pl.BlockSpec((Dd, tr), lambda i: (0, i))), (out_shape, jax.ShapeDtypeStruct((Dd, R), bf16))
    return _call(body, name, (R // tr,), [row, pl.BlockSpec((1, Dd), lambda i: (0, 0))], out_specs, out_shape, sem=("parallel",))(x, g)


def _rms_bwd(x, g, dh, name, residual=None):
    R, Dd = x.shape
    tr = _pick(R, (512, 256, 128))

    def body(*refs):
        if residual is None:
            x_ref, g_ref, dh_ref, dx_ref, dg_ref = refs
        else:
            x_ref, g_ref, dh_ref, r_ref, dx_ref, dg_ref = refs
        xf = x_ref[...]
        rs = lax.rsqrt(jnp.mean(xf * xf, axis=-1, keepdims=True) + EPS)
        y = xf * rs
        dh_ = dh_ref[...].astype(f32)
        dy = dh_ * g_ref[...]
        dx = rs * (dy - y * jnp.mean(dy * y, axis=-1, keepdims=True))
        if residual is not None:
            dx = dx + r_ref[...]
        dx_ref[...] = dx

        @pl.when(pl.program_id(0) == 0)
        def _():
            dg_ref[...] = jnp.zeros_like(dg_ref)

        dg_ref[...] += jnp.sum(dh_ * y, axis=0, keepdims=True)

    row = pl.BlockSpec((tr, Dd), lambda i: (i, 0))
    vec = pl.BlockSpec((1, Dd), lambda i: (0, 0))
    in_specs = [row, vec, row] + ([row] if residual is not None else [])
    args = (x, g, dh) + ((residual,) if residual is not None else ())
    return _call(body, name, (R // tr,), in_specs, (row, vec),
                 (jax.ShapeDtypeStruct((R, Dd), f32), jax.ShapeDtypeStruct((1, Dd), f32)), sem=("arbitrary",))(*args)


def _final_loss(x2, g, target):
    R, Dd = x2.shape
    tr = _pick(R, (512, 256, 128))

    def body(x_ref, g_ref, t_ref, loss_ref, dx_ref, dg_ref):
        xf = x_ref[...]
        rs = lax.rsqrt(jnp.mean(xf * xf, axis=-1, keepdims=True) + EPS)
        y = xf * rs
        err = y * g_ref[...] - t_ref[...]
        dh_ = err * (1.0 / Dd)
        dy = dh_ * g_ref[...]
        dx_ref[...] = rs * (dy - y * jnp.mean(dy * y, axis=-1, keepdims=True))

        @pl.when(pl.program_id(0) == 0)
        def _():
            dg_ref[...] = jnp.zeros_like(dg_ref)
            loss_ref[...] = jnp.zeros_like(loss_ref)

        dg_ref[...] += jnp.sum(dh_ * y, axis=0, keepdims=True)
        part = jnp.sum(jnp.mean(err * err, axis=-1, keepdims=True), axis=0, keepdims=True)
        loss_ref[...] += 0.5 * part

    row = pl.BlockSpec((tr, Dd), lambda i: (i, 0))
    vec = pl.BlockSpec((1, Dd), lambda i: (0, 0))
    return _call(body, "final_loss", (R // tr,), [row, vec, row], (pl.BlockSpec((1, 128), lambda i: (0, 0)), row, vec),
                 (jax.ShapeDtypeStruct((1, 128), f32), jax.ShapeDtypeStruct((R, Dd), f32), jax.ShapeDtypeStruct((1, Dd), f32)),
                 sem=("arbitrary",))(x2, g, target)


def _gmlp_parts(zuv, ln_g, ln_b):
    zu, zv = zuv[:, :512], zuv[:, 512:]
    u = jax.nn.gelu(zu)
    v = jax.nn.gelu(zv)
    mu = jnp.mean(v, axis=-1, keepdims=True)
    rs = lax.rsqrt(jnp.mean(jnp.square(v - mu), axis=-1, keepdims=True) + EPS)
    xh = (v - mu) * rs
    return zu, zv, u, xh, rs, xh * ln_g + ln_b


def _gmlp_fwd(proj, ln_g, ln_b, w_s, b_st):
    T = proj.shape[0]

    def body(p_ref, g_ref, b_ref, w_ref, bs_ref, o_ref):
        _, _, u, _, _, vn = _gmlp_parts(p_ref[...].astype(f32), g_ref[...], b_ref[...])
        causal = _tri(GM_CHUNK, True) > 0
        for gi in range(N_HEAD):
            sl = slice(gi * HEAD, (gi + 1) * HEAD)
            w = jnp.where(causal, w_ref[gi], 0.0)
            mixed = _raw_dot(w, vn[:, sl], "nn") + bs_ref[:, gi:gi + 1]
            o_ref[:, sl] = (u[:, sl] * mixed).astype(bf16)

    vec = pl.BlockSpec((1, 512), lambda i: (0, 0))
    return _call(body, "gmlp_fwd", (T // GM_CHUNK,),
                 [pl.BlockSpec((GM_CHUNK, 1024), lambda i: (i, 0)), vec, vec,
                  pl.BlockSpec((N_HEAD, GM_CHUNK, GM_CHUNK), lambda i: (0, 0, 0)), pl.BlockSpec((GM_CHUNK, 128), lambda i: (0, 0))],
                 pl.BlockSpec((GM_CHUNK, 512), lambda i: (i, 0)), jax.ShapeDtypeStruct((T, 512), bf16), sem=("parallel",))(
        proj, ln_g, ln_b, w_s, b_st)


def _gmlp_bwd(proj, ln_g, ln_b, w_s, b_st, da):
    T = proj.shape[0]

    def body(p_ref, g_ref, b_ref, w_ref, bs_ref, da_ref, dp_ref, dg_ref, db_ref, dw_ref, dbs_ref):
        zu, zv, u, xh, rs, vn = _gmlp_parts(p_ref[...].astype(f32), g_ref[...], b_ref[...])
        causal = _tri(GM_CHUNK, True) > 0
        sub = lax.broadcasted_iota(jnp.int32, (8, GM_CHUNK), 0)
        ones = jnp.ones((8, HEAD), f32)
        dout = da_ref[...].astype(f32)

        @pl.when(pl.program_id(0) == 0)
        def _():
            for r in (dg_ref, db_ref, dw_ref, dbs_ref):
                r[...] = jnp.zeros_like(r)

        du, dvn, dbs = [], [], jnp.zeros((8, GM_CHUNK), f32)
        for gi in range(N_HEAD):
            sl = slice(gi * HEAD, (gi + 1) * HEAD)
            w = jnp.where(causal, w_ref[gi], 0.0)
            mixed = _raw_dot(w, vn[:, sl], "nn") + bs_ref[:, gi:gi + 1]
            du.append(dout[:, sl] * mixed)
            dm = dout[:, sl] * u[:, sl]
            row_sums = _sel_dot(ones, dm, "nt")
            dbs = dbs + jnp.where(sub == gi, row_sums, 0.0)
            dw_ref[gi] += jnp.where(causal, _raw_dot(dm, vn[:, sl], "nt"), 0.0)
            dvn.append(_raw_dot(w, dm, "tn"))
        dbs_ref[...] += dbs
        du = jnp.concatenate(du, axis=-1)
        dvn = jnp.concatenate(dvn, axis=-1)
        dg_ref[...] += jnp.sum(dvn * xh, axis=0, keepdims=True)
        db_ref[...] += jnp.sum(dvn, axis=0, keepdims=True)
        dxh = dvn * g_ref[...]
        dv = rs * (dxh - jnp.mean(dxh, axis=-1, keepdims=True) - xh * jnp.mean(dxh * xh, axis=-1, keepdims=True))
        dp_ref[:, :512] = _egrad(jax.nn.gelu, zu, du).astype(bf16)
        dp_ref[:, 512:] = _egrad(jax.nn.gelu, zv, dv).astype(bf16)

    vec = pl.BlockSpec((1, 512), lambda i: (0, 0))
    wsp = pl.BlockSpec((N_HEAD, GM_CHUNK, GM_CHUNK), lambda i: (0, 0, 0))
    return _call(body, "gmlp_bwd", (T // GM_CHUNK,),
                 [pl.BlockSpec((GM_CHUNK, 1024), lambda i: (i, 0)), vec, vec, wsp, pl.BlockSpec((GM_CHUNK, 128), lambda i: (0, 0)),
                  pl.BlockSpec((GM_CHUNK, 512), lambda i: (i, 0))],
                 (pl.BlockSpec((GM_CHUNK, 1024), lambda i: (i, 0)), vec, vec, wsp, pl.BlockSpec((8, GM_CHUNK), lambda i: (0, 0))),
                 (jax.ShapeDtypeStruct((T, 1024), bf16), jax.ShapeDtypeStruct((1, 512), f32), jax.ShapeDtypeStruct((1, 512), f32),
                  jax.ShapeDtypeStruct((N_HEAD, GM_CHUNK, GM_CHUNK), f32), jax.ShapeDtypeStruct((8, GM_CHUNK), f32)),
                 sem=("arbitrary",))(proj, ln_g, ln_b, w_s, b_st, da)


HG_SUB = 8
HG_NSUB = HG_CHUNK // HG_SUB


def _two_level_matrix():
    r = lax.broadcasted_iota(jnp.int32, (2 * HG_CHUNK, HG_CHUNK), 0)
    c = lax.broadcasted_iota(jnp.int32, (2 * HG_CHUNK, HG_CHUNK), 1)
    t = jnp.where(r < HG_CHUNK, r, r - HG_CHUNK)
    local = (r < HG_CHUNK) & (t // HG_SUB == c // HG_SUB) & (c <= t)
    before = (r >= HG_CHUNK) & (c < (t // HG_SUB) * HG_SUB)
    return (local | before).astype(f32)


def _two_level_sums(x):
    two = _sel_dot(_two_level_matrix(), x, "nn")
    return two[:HG_CHUNK], two[HG_CHUNK:]


@jax.custom_vjp
def _two_level_cumsum(x):
    return _two_level_sums(x)


_two_level_cumsum.defvjp(
    lambda x: (_two_level_sums(x), None),
    lambda _, g: (_sel_dot(_two_level_matrix(), jnp.concatenate(g, axis=0), "tn"),))


def _tile_matrix():
    s = lax.broadcasted_iota(jnp.int32, (HG_SUB, HG_CHUNK), 0)
    j = lax.broadcasted_iota(jnp.int32, (HG_SUB, HG_CHUNK), 1)
    return (j % HG_SUB == s).astype(f32)


@jax.custom_vjp
def _tile_lanes(x):
    return _sel_dot(_tile_matrix(), x, "nn", x_first=True)


_tile_lanes.defvjp(
    lambda x: (_sel_dot(_tile_matrix(), x, "nn", x_first=True), None),
    lambda _, g: (_sel_dot(_tile_matrix(), g, "nt", x_first=True),))


def _block_rows(x):
    k = x.shape[-1]
    return jnp.broadcast_to(x.reshape(HG_NSUB, 1, HG_SUB, k), (HG_NSUB, HG_SUB, HG_SUB, k)).reshape(HG_CHUNK, HG_SUB, k)


def _hgrn_chunk(st0, q_raw, f_raw, i_raw, g_raw, l0, l1, ng):
    C, SUB = HG_CHUNK, HG_SUB
    lb = jax.nn.sigmoid(l0 - l1)
    fg = lb + (1.0 - lb) * jax.nn.sigmoid(f_raw)
    kk = 1.0 - fg
    qf = jax.nn.silu(q_raw)
    al, base = _two_level_cumsum(jnp.log(fg))
    a = al + base
    row = lax.broadcasted_iota(jnp.int32, (C, HEAD), 0)
    a_last = jnp.sum(jnp.where(row == C - 1, a, 0.0), axis=0, keepdims=True)
    inter = _dot_nt(qf * jnp.exp(a), st0)
    qt = qf * jnp.exp(al)
    rb = lax.broadcasted_iota(jnp.int32, (C, C), 0) // SUB
    cb = lax.broadcasted_iota(jnp.int32, (C, C), 1) // SUB
    scores = jnp.zeros((C, C), f32)
    for i in range(1, HG_NSUB):
        base_i = jnp.sum(jnp.where(row == i * SUB, base, 0.0), axis=0, keepdims=True)
        kt = kk * jnp.exp(jnp.minimum(base_i - a, 0.0))
        scores = scores + jnp.where((rb == i) & (cb < i), _dot_nt(qt, kt), 0.0)
    t_i = lax.broadcasted_iota(jnp.int32, (C, SUB, HEAD), 0) % SUB
    s_i = lax.broadcasted_iota(jnp.int32, (C, SUB, HEAD), 1)
    decay = jnp.exp(jnp.where(s_i <= t_i, al[:, None, :] - _block_rows(al), -jnp.inf))
    diag = jnp.sum(qf[:, None, :] * decay * _block_rows(kk), axis=-1)
    scores = scores + jnp.where(rb == cb, _tile_lanes(diag), 0.0)
    o = inter + _dot_nn(scores, i_raw)
    st1 = jnp.exp(a_last) * st0 + _dot_tn(i_raw, kk * jnp.exp(a_last - a))
    on = o * lax.rsqrt(jnp.mean(o * o, axis=-1, keepdims=True) + EPS) * ng
    return st1, on * jax.nn.silu(g_raw)


def _hgrn_specs(S, Bl, rev):
    N = S // HG_CHUNK
    chunk = (lambda n: N - 1 - n) if rev else (lambda n: n)
    col = lambda c0: pl.BlockSpec((Bl, HG_CHUNK, 512), lambda n: (0, chunk(n), c0 // 512))
    st = pl.BlockSpec((Bl, N_HEAD, 1, HEAD, HEAD), lambda n: (0, 0, chunk(n), 0, 0))
    full = lambda *s: pl.BlockSpec(s, functools.partial(lambda n, nd: (0,) * nd, nd=len(s)))
    return N, col, st, full


def _hgrn_fwd(proj, lb_logits, ng, Bl, S):
    N, col, st, full = _hgrn_specs(S, Bl, False)

    def body(q_ref, f_ref, i_ref, g_ref, l_ref, ng_ref, o_ref, st_ref, state):
        @pl.when(pl.program_id(0) == 0)
        def _():
            state[...] = jnp.zeros_like(state)

        for b in range(Bl):
            for h in range(N_HEAD):
                sl = slice(h * HEAD, (h + 1) * HEAD)
                st0 = state[b, h]
                st_ref[b, h, 0] = st0
                st1, out = _hgrn_chunk(st0, *[r[b, :, sl].astype(f32) for r in (q_ref, f_ref, i_ref, g_ref)],
                                       l_ref[0:1, sl], l_ref[1:2, sl], ng_ref[...])
                state[b, h] = st1
                o_ref[b, :, sl] = out.astype(bf16)

    return _call(body, "hgrn_fwd", (N,), [col(C_HQ), col(C_HF), col(C_HI), col(C_HG), full(2, 512), full(1, HEAD)],
                 (col(0), st),
                 (jax.ShapeDtypeStruct((Bl, S, 512), bf16), jax.ShapeDtypeStruct((Bl, N_HEAD, N, HEAD, HEAD), f32)),
                 scratch=[pltpu.VMEM((Bl, N_HEAD, HEAD, HEAD), f32)], sem=("arbitrary",))(
        proj, proj, proj, proj, lb_logits, ng)


def _hgrn_bwd(proj, lb_logits, ng, states, db, Bl, S):
    N, col, st, full = _hgrn_specs(S, Bl, True)

    def body(q_ref, f_ref, i_ref, g_ref, l_ref, ng_ref, st_ref, db_ref,
             dq_ref, df_ref, di_ref, dg_ref, dl_ref, dng_ref, dstate):
        @pl.when(pl.program_id(0) == 0)
        def _():
            dstate[...] = jnp.zeros_like(dstate)
            dl_ref[...] = jnp.zeros_like(dl_ref)
            dng_ref[...] = jnp.zeros_like(dng_ref)

        for b in range(Bl):
            for h in range(N_HEAD):
                sl = slice(h * HEAD, (h + 1) * HEAD)
                _, vjp = jax.vjp(_hgrn_chunk, st_ref[b, h, 0], *[r[b, :, sl].astype(f32) for r in (q_ref, f_ref, i_ref, g_ref)],
                                 l_ref[0:1, sl], l_ref[1:2, sl], ng_ref[...])
                dst0, dq, df, di, dg, dl0, dl1, dng = vjp((dstate[b, h], db_ref[b, :, sl].astype(f32)))
                dstate[b, h] = dst0
                dq_ref[b, :, sl] = dq.astype(bf16)
                df_ref[b, :, sl] = df.astype(bf16)
                di_ref[b, :, sl] = di.astype(bf16)
                dg_ref[b, :, sl] = dg.astype(bf16)
                dl_ref[0:1, sl] += dl0
                dl_ref[1:2, sl] += dl1
                dng_ref[b, h] += dng

    return _call(body, "hgrn_bwd", (N,),
                 [col(C_HQ), col(C_HF), col(C_HI), col(C_HG), full(2, 512), full(1, HEAD), st, col(0)],
                 (*[col(0)] * 4, full(2, 512), full(Bl, N_HEAD, 1, HEAD)),
                 (*[jax.ShapeDtypeStruct((Bl, S, 512), bf16)] * 4, jax.ShapeDtypeStruct((2, 512), f32),
                  jax.ShapeDtypeStruct((Bl, N_HEAD, 1, HEAD), f32)),
                 scratch=[pltpu.VMEM((Bl, N_HEAD, HEAD, HEAD), f32)], sem=("arbitrary",))(
        proj, proj, proj, proj, lb_logits, ng, states, db)


def _attn_probs(q, k):
    s = _raw_dot(q, k, "nt") * (HEAD ** -0.5)
    e = jnp.exp(s - jnp.max(s, axis=-1, keepdims=True))
    return e / jnp.sum(e, axis=-1, keepdims=True)


def _attn_specs(S, tq):
    nq = S // tq
    q = pl.BlockSpec((tq, 512), lambda b, i: (b * nq + i, C_XQ // 512))
    kv = pl.BlockSpec((1, MEM_LEN, 1024), lambda b, i: (b, 0, 0))
    o = pl.BlockSpec((tq, 512), lambda b, i: (b * nq + i, 0))
    return nq, q, kv, o


def _attn_fwd(proj, kv, Bl, S):
    tq = _pick(S, (512, 256, 128))
    nq, qs, kvs, os_ = _attn_specs(S, tq)

    def body(q_ref, kv_ref, o_ref):
        for h in range(N_HEAD):
            sl = slice(h * HEAD, (h + 1) * HEAD)
            p = _attn_probs(q_ref[:, sl], kv_ref[0, :, sl])
            o_ref[:, sl] = _raw_dot(p, kv_ref[0, :, 512 + h * HEAD:512 + (h + 1) * HEAD], "nn").astype(bf16)

    return _call(body, "attn_fwd", (Bl, nq), [qs, kvs], os_, jax.ShapeDtypeStruct((Bl * S, 512), bf16),
                 sem=("parallel", "parallel"))(proj, kv)


def _attn_bwd(proj, kv, dc, Bl, S):
    tq = _pick(S, (512, 256, 128))
    nq, qs, kvs, os_ = _attn_specs(S, tq)

    def body(q_ref, kv_ref, do_ref, dq_ref, dkv_ref):
        @pl.when(pl.program_id(1) == 0)
        def _():
            dkv_ref[...] = jnp.zeros_like(dkv_ref)

        for h in range(N_HEAD):
            sl = slice(h * HEAD, (h + 1) * HEAD)
            vsl = slice(512 + h * HEAD, 512 + (h + 1) * HEAD)
            q, k, v, do = q_ref[:, sl], kv_ref[0, :, sl], kv_ref[0, :, vsl], do_ref[:, sl]
            p = _attn_probs(q, k)
            dkv_ref[0, :, vsl] += _raw_dot(p, do, "tn")
            dp = _raw_dot(do, v, "nt")
            ds = p * (dp - jnp.sum(dp * p, axis=-1, keepdims=True)) * (HEAD ** -0.5)
            dq_ref[:, sl] = _raw_dot(ds, k, "nn").astype(bf16)
            dkv_ref[0, :, sl] += _raw_dot(ds, q, "tn")

    return _call(body, "attn_bwd", (Bl, nq), [qs, kvs, os_], (os_, kvs),
                 (jax.ShapeDtypeStruct((Bl * S, 512), bf16), jax.ShapeDtypeStruct((Bl, MEM_LEN, 1024), f32)),
                 sem=("arbitrary", "arbitrary"))(proj, kv, dc)


def _merge_specs(tm, tn):
    br = pl.BlockSpec((tm, 512), lambda i, j: (i, 0))
    w = pl.BlockSpec((512, tn), lambda i, j: (0, j))
    gl = [pl.BlockSpec((tm, tn), functools.partial(lambda i, j, n: (i, (C_GL + n * D_MODEL) // tn + j), n=n)) for n in range(3)]
    return [br, br, br, w, w, w, *gl]


def _merge_fwd(branches, wb, proj):
    T = proj.shape[0]
    tm, tn = _pick(T, (1024, 512, 256, 128)), 512

    def body(a_ref, b_ref, c_ref, w0, w1, w2, g0, g1, g2, o_ref):
        acc = jnp.zeros((tm, tn), f32)
        for x_ref, w_ref, g_ref in ((a_ref, w0, g0), (b_ref, w1, g1), (c_ref, w2, g2)):
            acc = acc + jax.nn.sigmoid(g_ref[...].astype(f32)) * _raw_dot(x_ref[...], w_ref[...], "nn")
        o_ref[...] = acc.astype(bf16)

    return _call(body, "merge_fwd", (T // tm, D_MODEL // tn), _merge_specs(tm, tn), pl.BlockSpec((tm, tn), lambda i, j: (i, j)),
                 jax.ShapeDtypeStruct((T, D_MODEL), bf16), sem=("parallel", "parallel"))(*branches, *wb, proj, proj, proj)


def _merge_bwd(branches, wb, proj, dmerged):
    T = proj.shape[0]
    tm, tn = _pick(T, (1024, 512, 256, 128)), 512

    def body(a_ref, b_ref, c_ref, w0, w1, w2, g0, g1, g2, dm_ref, dgl_ref, d0, d1, d2):
        dm = dm_ref[...]
        for n, (x_ref, w_ref, g_ref, d_ref) in enumerate(((a_ref, w0, g0, d0), (b_ref, w1, g1, d1), (c_ref, w2, g2, d2))):
            up = _raw_dot(x_ref[...], w_ref[...], "nn")
            logits = g_ref[...].astype(f32)
            dgl_ref[n] = _egrad(jax.nn.sigmoid, logits, dm * up).astype(bf16)
            d_ref[...] = (dm * jax.nn.sigmoid(logits)).astype(bf16)

    blk = pl.BlockSpec((tm, tn), lambda i, j: (i, j))
    sh = jax.ShapeDtypeStruct((T, D_MODEL), bf16)
    outs = _call(body, "merge_bwd", (T // tm, D_MODEL // tn), [*_merge_specs(tm, tn), blk],
                 (pl.BlockSpec((3, tm, tn), lambda i, j: (0, i, j)), blk, blk, blk),
                 (jax.ShapeDtypeStruct((3, T, D_MODEL), bf16), sh, sh, sh),
                 sem=("parallel", "parallel"))(*branches, *wb, proj, proj, proj, dmerged)
    return outs[0], outs[1:]


CONV_TC = 256


def _shift_down(a, k):
    row = lax.broadcasted_iota(jnp.int32, a.shape, 0)
    return jnp.where(row >= k, pltpu.roll(a, k, 0), 0.0)


def _shift_up(a, k):
    n = a.shape[0]
    row = lax.broadcasted_iota(jnp.int32, a.shape, 0)
    return jnp.where(row < n - k, pltpu.roll(a, n - k, 0), 0.0)


def _conv_pre(a, cw, cb):
    return cb + cw[0:1] * _shift_down(a, 2) + cw[1:2] * _shift_down(a, 1) + cw[2:3] * a


def _conv_fwd(ab, cw, cb, Bl, S):
    nc = D_FF // CONV_TC

    def body(a_ref, b_ref, cw_ref, cb_ref, o_ref):
        ac = _conv_pre(a_ref[0].astype(f32), cw_ref[...], cb_ref[...])
        o_ref[0] = (jax.nn.silu(ac) * b_ref[0].astype(f32)).astype(bf16)

    return _call(body, "conv_fwd", (Bl, nc),
                 [pl.BlockSpec((1, S, CONV_TC), lambda b, c: (b, 0, c)), pl.BlockSpec((1, S, CONV_TC), lambda b, c: (b, 0, nc + c)),
                  pl.BlockSpec((3, CONV_TC), lambda b, c: (0, c)), pl.BlockSpec((1, CONV_TC), lambda b, c: (0, c))],
                 pl.BlockSpec((1, S, CONV_TC), lambda b, c: (b, 0, c)), jax.ShapeDtypeStruct((Bl, S, D_FF), bf16),
                 sem=("parallel", "parallel"))(ab, ab, cw, cb)


def _conv_bwd(ab, cw, cb, dact, Bl, S):
    nc = D_FF // CONV_TC

    def body(a_ref, b_ref, cw_ref, cb_ref, d_ref, da_ref, db_ref, dcw_ref, dcb_ref):
        @pl.when(pl.program_id(1) == 0)
        def _():
            dcw_ref[...] = jnp.zeros_like(dcw_ref)
            dcb_ref[...] = jnp.zeros_like(dcb_ref)

        a, cw = a_ref[0].astype(f32), cw_ref[...]
        ac = _conv_pre(a, cw, cb_ref[...])
        dact_ = d_ref[0].astype(f32)
        db_ref[0] = (dact_ * jax.nn.silu(ac)).astype(bf16)
        dac = _egrad(jax.nn.silu, ac, dact_ * b_ref[0].astype(f32))
        da_ref[0] = (cw[2:3] * dac + cw[1:2] * _shift_up(dac, 1) + cw[0:1] * _shift_up(dac, 2)).astype(bf16)
        dcw_ref[0:1, :] += jnp.sum(dac * _shift_down(a, 2), axis=0, keepdims=True)
        dcw_ref[1:2, :] += jnp.sum(dac * _shift_down(a, 1), axis=0, keepdims=True)
        dcw_ref[2:3, :] += jnp.sum(dac * a, axis=0, keepdims=True)
        dcb_ref[...] += jnp.sum(dac, axis=0, keepdims=True)

    seq = pl.BlockSpec((1, S, CONV_TC), lambda c, b: (b, 0, c))
    return _call(body, "conv_bwd", (nc, Bl),
                 [seq, pl.BlockSpec((1, S, CONV_TC), lambda c, b: (b, 0, nc + c)), pl.BlockSpec((3, CONV_TC), lambda c, b: (0, c)),
                  pl.BlockSpec((1, CONV_TC), lambda c, b: (0, c)), seq],
                 (seq, seq, pl.BlockSpec((3, CONV_TC), lambda c, b: (0, c)), pl.BlockSpec((1, CONV_TC), lambda c, b: (0, c))),
                 (jax.ShapeDtypeStruct((Bl, S, D_FF), bf16), jax.ShapeDtypeStruct((Bl, S, D_FF), bf16),
                  jax.ShapeDtypeStruct((3, D_FF), f32), jax.ShapeDtypeStruct((1, D_FF), f32)),
                 sem=("arbitrary", "arbitrary"))(ab, ab, cw, cb, dact)


def _local_step(x, mem, target, p, w_in, late_b, late_c, send, settle):
    Bl, S, Dd = x.shape
    T = Bl * S
    x2d, t2d, mem2d = x.reshape(T, Dd), target.reshape(T, Dd), mem.reshape(Bl * MEM_LEN, Dd)
    b_st = jnp.pad(p["b_spatial"].T, ((0, 0), (0, 128 - N_HEAD)))
    lbl = p["lb_logits"]

    h, h_t = _rms_fwd(x2d, p["norm1_g"], "norm1_fwd", transposed=True)
    proj = _mm(h, w_in, "nn", bf16, "proj_fwd", 1024, 1664)
    a_out = _gmlp_fwd(proj, p["ln_v_g"], p["ln_v_b"], p["w_spatial"], b_st)
    proj3 = proj.reshape(Bl, S, IN_WIDTH)
    b_out, states = _hgrn_fwd(proj3, lbl, p["hgrn_norm_g"], Bl, S)
    b_out = b_out.reshape(T, 512)
    memn = _rms_fwd(mem2d, p["mem_norm_g"], "memnorm_fwd")
    w = late_b(b_out)
    wb = w["w_branch"]
    kv = _mm(memn, w["w_mem_kv"], "nn", f32, "kv_fwd", 512, 1024).reshape(Bl, MEM_LEN, 2 * 512)
    c_out = _attn_fwd(proj, kv, Bl, S)
    branches = (a_out, b_out, c_out)
    merged = _merge_fwd(branches, wb, proj)
    x1 = _mm(merged, w["w_out"], "nn", f32, "out_fwd", 1024, 1024, residual=x2d)
    h2, h2_t = _rms_fwd(x1, p["norm2_g"], "norm2_fwd", transposed=True)
    w.update(late_c(h2))
    ab = _mm(h2, w["w_up"], "nn", bf16, "up_fwd", 1024, 1408)
    act = _conv_fwd(ab.reshape(Bl, S, 2 * D_FF), w["conv_w"], p["conv_b"], Bl, S).reshape(T, D_FF)
    x2 = _mm(act, w["w_down"], "nn", f32, "down_fwd", 512, 1024, residual=x1)
    loss_part, dx2, g_final = _final_loss(x2, p["final_g"], t2d)

    g_w_down = _mm(act, dx2, "tn", bf16, "down_dw", 1408, 1024, 1024)
    dact = _mm(dx2, w["w_down"], "nt", bf16, "down_dx", 1024, 1408)
    da, db, g_conv_w, g_conv_b = _conv_bwd(ab.reshape(Bl, S, 2 * D_FF), w["conv_w"], p["conv_b"], dact.reshape(Bl, S, D_FF), Bl, S)
    dab = jnp.concatenate([da.reshape(T, D_FF), db.reshape(T, D_FF)], axis=-1)
    g_w_up = _mm(h2_t, dab, "nn", bf16, "up_dw", 512, 1408, shard=2 * D_FF // N_DEV, n_outer=True)
    tok = send("c", dict(w_up=g_w_up, conv_w=g_conv_w, w_down=g_w_down))
    dh2 = _mm(dab, w["w_up"], "nt", f32, "up_dx", 512, 1024)
    dx1, g_norm2 = _rms_bwd(x1, p["norm2_g"] + tok[0, 0], dh2, "norm2_bwd", residual=dx2)

    g_w_out = _mm(merged, dx1, "tn", bf16, "out_dw", 1024, 1024, 1024)
    dmerged = _mm(dx1, w["w_out"], "nt", f32, "out_dx", 1024, 1024)
    dgl, dup = _merge_bwd(branches, wb, proj, dmerged)
    g_w_branch = [_mm(branches[n], dup[n], "tn", bf16, f"branch_dw{n}", 512, 1024, 1024) for n in range(3)]
    dbr = [_mm(dup[n], wb[n], "nt", bf16, f"branch_dx{n}", 1024, 512) for n in range(3)]
    dxq, dkv = _attn_bwd(proj, kv, dbr[2], Bl, S)
    dkv = dkv.reshape(Bl * MEM_LEN, 2 * 512)
    g_w_kv = _mm(memn, dkv, "tn", bf16, "kv_dw", 1024, 1024, 512)
    tok = send("b", dict(w_mem_kv=g_w_kv, w_branch=g_w_branch, w_out=g_w_out))
    dmemn = _mm(dkv, w["w_mem_kv"], "nt", f32, "kv_dx", 512, 1024)
    _, g_mem_norm = _rms_bwd(mem2d, p["mem_norm_g"], dmemn, "memnorm_bwd")
    dzuv, g_ln_g, g_ln_b, g_w_sp, g_b_sp = _gmlp_bwd(proj, p["ln_v_g"] + tok[0, 0], p["ln_v_b"], p["w_spatial"], b_st, dbr[0])
    *dqfig, g_lbl, g_ng = _hgrn_bwd(proj3, lbl, p["hgrn_norm_g"], states, dbr[1].reshape(Bl, S, 512), Bl, S)
    dq, df, di, dg = [d.reshape(T, 512) for d in dqfig]
    dproj = jnp.concatenate([dzuv, dq, df, di, dg, dxq, dgl[0], dgl[1], dgl[2]], axis=-1)
    g_w_in = _mm(h_t, dproj, "nn", bf16, "proj_dw", 512, 1664, shard=IN_WIDTH // N_DEV, n_outer=True)
    tok = send("a", dict(w_in=g_w_in))
    dh = _mm(settle(dproj), w_in, "nt", f32, "proj_dx", 512, 1024)
    dx, g_norm1 = _rms_bwd(x2d, p["norm1_g"] + tok[0, 0], dh, "norm1_bwd", residual=dx1)

    gs = dict(w_spatial=g_w_sp, norm1_g=g_norm1, mem_norm_g=g_mem_norm, norm2_g=g_norm2, final_g=g_final, lb_logits=g_lbl,
              ln_v_g=g_ln_g, ln_v_b=g_ln_b, b_spatial=g_b_sp, hgrn_norm_g=g_ng, conv_b=g_conv_b)
    return loss_part, dx.reshape(Bl, S, Dd), gs


def _coords():
    return lax.axis_index("x"), lax.axis_index("y"), lax.axis_index("c")


def _slot(dev):
    return 4 * dev[0] + 2 * dev[1] + dev[2]


def _comm_call(body, name, arrays, out_shapes, n_sem):
    n = len(arrays)
    hbm = pl.BlockSpec(memory_space=pl.ANY)
    return pl.pallas_call(
        body, name=name, out_shape=out_shapes, in_specs=[hbm] * n, out_specs=[hbm] * n,
        scratch_shapes=[pltpu.SemaphoreType.DMA((n_sem, n)), pltpu.SemaphoreType.DMA((n_sem, n)), pltpu.SemaphoreType.DMA((n,))])(*arrays)


def _all_gather(blocks, name):
    n = len(blocks)

    def body(*refs):
        x_refs, o_refs, (send_sems, recv_sems, local_sems) = refs[:n], refs[n:2 * n], refs[2 * n:]
        x, y, c = _coords()
        me, sibling = (x, y, c), (x, y, 1 - c)
        chips = [(1 - x, y), (x, 1 - y), (1 - x, 1 - y)]

        def copy(a, k, block_dev, to, from_input=False):
            dst = o_refs[a].at[_slot(block_dev)]
            return pltpu.make_async_remote_copy(src_ref=x_refs[a] if from_input else dst, dst_ref=dst, send_sem=send_sems.at[k, a],
                                                recv_sem=recv_sems.at[k, a], device_id=to, device_id_type=MESH)

        mine = [pltpu.make_async_copy(x_refs[a], o_refs[a].at[_slot(me)], local_sems.at[a]) for a in range(n)]
        first = [copy(a, 0, me, sibling, True) for a in range(n)]
        first += [copy(a, 1 + j, me, (*chip, c), True) for j, chip in enumerate(chips) for a in range(n)]
        for cp in mine + first:
            cp.start()
        passed = []
        for j, chip in enumerate(chips):
            for a in range(n):
                copy(a, 1 + j, (*chip, c), me).wait_recv()
                fwd = copy(a, 4 + j, (*chip, c), sibling)
                fwd.start()
                passed.append(fwd)
        for a in range(n):
            copy(a, 0, sibling, me).wait_recv()
        for j, chip in enumerate(chips):
            for a in range(n):
                copy(a, 4 + j, (*chip, 1 - c), me).wait_recv()
        for cp in first + passed:
            cp.wait_send()
        for cp in mine:
            cp.wait()

    return _comm_call(body, name, blocks, [jax.ShapeDtypeStruct((N_DEV,) + b.shape, b.dtype) for b in blocks], 7)


def _all_to_all(parts, name):
    n = len(parts)
    rel = [(0, 0, 1), (0, 1, 0), (0, 1, 1), (1, 0, 0), (1, 0, 1), (1, 1, 0), (1, 1, 1)]

    def body(*refs):
        x_refs, o_refs, (send_sems, recv_sems, local_sems) = refs[:n], refs[n:2 * n], refs[2 * n:]
        x, y, c = _coords()
        me = (x, y, c)
        peers = [(x ^ dx, y ^ dy, c ^ dc) for dx, dy, dc in rel]

        def copy(a, k, peer):
            return pltpu.make_async_remote_copy(src_ref=x_refs[a].at[_slot(peer)], dst_ref=o_refs[a].at[_slot(me)], send_sem=send_sems.at[k, a],
                                                recv_sem=recv_sems.at[k, a], device_id=peer, device_id_type=MESH)

        def arrival(a, k, peer):
            return pltpu.make_async_remote_copy(src_ref=x_refs[a].at[_slot(me)], dst_ref=o_refs[a].at[_slot(peer)], send_sem=send_sems.at[k, a],
                                                recv_sem=recv_sems.at[k, a], device_id=peer, device_id_type=MESH)

        mine = [pltpu.make_async_copy(x_refs[a].at[_slot(me)], o_refs[a].at[_slot(me)], local_sems.at[a]) for a in range(n)]
        sends = [copy(a, k, peer) for k, peer in enumerate(peers) for a in range(n)]
        for cp in mine + sends:
            cp.start()
        for k, peer in enumerate(peers):
            for a in range(n):
                arrival(a, k, peer).wait_recv()
        for cp in sends:
            cp.wait_send()
        for cp in mine:
            cp.wait()

    return _comm_call(body, name, parts, [jax.ShapeDtypeStruct(p.shape, p.dtype) for p in parts], 7)


_HBM = pl.BlockSpec(memory_space=pltpu.HBM)
_SEM = pl.BlockSpec(memory_space=pltpu.SEMAPHORE)
_REL = [(0, 0, 1), (0, 1, 0), (0, 1, 1), (1, 0, 0), (1, 0, 1), (1, 1, 0), (1, 1, 1)]


_LINK_ORDER = (3, 1, 5, 4, 2, 6, 0)
SEND_PIECES = 4


def _pieces(shape, dtype):
    rows = shape[0]
    unit = 1 if len(shape) > 2 else (16 if dtype == bf16 else 8)
    for n in (SEND_PIECES, 2):
        if rows % (n * unit) == 0:
            return [pl.ds(i * (rows // n), rows // n) for i in range(n)]
    return [pl.ds(0, rows)]


def _split_copies(gather, src, land, send, recv, pieces):
    x, y, c = _coords()
    me = (x, y, c)
    copies = []
    for a in range(len(src)):
        block = src[a].shape if gather else src[a].shape[1:]
        for rows in (_pieces(block, src[a].dtype) if pieces else [None]):
            for k in _LINK_ORDER:
                dx, dy, dc = _REL[k]
                peer = (x ^ dx, y ^ dy, c ^ dc)
                mine, there = (src[a] if gather else src[a].at[_slot(peer)]), land[a].at[_slot(me)]
                if rows is not None:
                    mine, there = mine.at[rows], there.at[rows]
                copies.append(pltpu.make_async_remote_copy(src_ref=mine, dst_ref=there, send_sem=send[a].at[k], recv_sem=recv[a].at[k],
                                                           device_id=peer, device_id_type=MESH))
    return me, copies


def _arrivals(gather, src, land, send, recv):
    x, y, c = _coords()
    out = []
    for a in range(len(src)):
        for k, (dx, dy, dc) in enumerate(_REL):
            peer = (x ^ dx, y ^ dy, c ^ dc)
            out.append(pltpu.make_async_remote_copy(src_ref=src[a] if gather else src[a].at[_slot(peer)], dst_ref=land[a].at[_slot(peer)],
                                                    send_sem=send[a].at[k], recv_sem=recv[a].at[k], device_id=peer, device_id_type=MESH))
    return out


def _exchange_start(arrays, gather, name, after=None):
    n = len(arrays)
    e = 0 if after is None else 1
    lands = [lax.empty(((N_DEV,) + a.shape) if gather else a.shape, a.dtype) for a in arrays]

    def body(*refs):
        src, land = refs[:n], refs[n:2 * n]
        refs = refs[2 * n + e:]
        send, recv, token, local_sems = refs[:n], refs[n:2 * n], refs[4 * n], refs[4 * n + 1]
        me, out = _split_copies(gather, src, land, send, recv, True)
        local = [pltpu.make_async_copy(src[a] if gather else src[a].at[_slot(me)], land[a].at[_slot(me)], local_sems.at[a])
                 for a in range(n)]
        for cp in local:
            cp.start()
        for cp in local:
            cp.wait()
        for cp in out:
            cp.start()
        token[...] = jnp.zeros_like(token)

    sems = [pltpu.SemaphoreType.DMA((7,)) for _ in range(2 * n)]
    outs = pl.pallas_call(
        body, name=name,
        out_shape=(*sems, *[pltpu.HBM(a.shape, a.dtype) for a in arrays], *[pltpu.HBM(l.shape, l.dtype) for l in lands],
                   jax.ShapeDtypeStruct((8, 128), f32)),
        in_specs=[_HBM] * (2 * n) + [pl.BlockSpec(memory_space=pl.ANY)] * e,
        out_specs=(*[_SEM] * (2 * n), *[_HBM] * (2 * n), pl.BlockSpec(memory_space=pltpu.VMEM)),
        input_output_aliases={i: 2 * n + i for i in range(2 * n)},
        scratch_shapes=[pltpu.SemaphoreType.DMA((n,))],
        compiler_params=pltpu.CompilerParams(has_side_effects=pltpu.SideEffectType.DATAFLOW_SIDE_EFFECTING))(
        *[pltpu.with_memory_space_constraint(a, pltpu.HBM) for a in arrays],
        *[pltpu.with_memory_space_constraint(l, pltpu.HBM) for l in lands], *([after] if e else []))
    return (gather, n, outs[:4 * n]), outs[4 * n]


def _exchange_wait(handle, which, after, name):
    gather, n_all, vals = handle
    send_v, recv_v, src_v, land_v = [[vals[g * n_all + i] for i in which] for g in range(4)]
    n = len(which)

    def body(*refs):
        src, land, send, recv = refs[:n], refs[n:2 * n], refs[2 * n:3 * n], refs[3 * n:4 * n]
        for cp in _split_copies(gather, src, land, send, recv, False)[1]:
            cp.wait_send()
        for cp in _arrivals(gather, src, land, send, recv):
            cp.wait_recv()

    outs = pl.pallas_call(
        body, name=name,
        out_shape=(*[pltpu.HBM(a.shape, a.dtype) for a in src_v], *[pltpu.HBM(l.shape, l.dtype) for l in land_v]),
        in_specs=[*[_HBM] * (2 * n), *[_SEM] * (2 * n), pl.BlockSpec(memory_space=pl.ANY)], out_specs=[_HBM] * (2 * n),
        input_output_aliases={i: i for i in range(2 * n)},
        compiler_params=pltpu.CompilerParams(has_side_effects=pltpu.SideEffectType.DATAFLOW_SIDE_EFFECTING))(
        *src_v, *land_v, *send_v, *recv_v, after)
    return outs[n:]


def _seq_exchange(arrays, gather, name, collective_id):
    n = len(arrays)
    hbm = pltpu.MemorySpace.HBM
    srcs = [jax.new_ref(a, memory_space=hbm) for a in arrays]
    lands = [jax.empty_ref(jax.ShapeDtypeStruct(((N_DEV,) + a.shape) if gather else a.shape, a.dtype), memory_space=hbm) for a in arrays]

    @pl.kernel(mesh=plsc.ScalarSubcoreMesh(axis_name="sequencer", num_cores=1), name=name,
               scratch_types=(pltpu.SemaphoreType.DMA((7, n)), pltpu.SemaphoreType.DMA((7, n)), pltpu.SemaphoreType.DMA((n,))),
               compiler_params=pltpu.CompilerParams(collective_id=collective_id))
    def launch(send, recv, local):
        x, y, c = _coords()
        me = (x, y, c)
        peers = [(x ^ dx, y ^ dy, c ^ dc) for dx, dy, dc in _REL]
        barrier = pltpu.get_barrier_semaphore()
        for peer in peers:
            pl.semaphore_signal(barrier, inc=1, device_id=peer, device_id_type=MESH)
        pl.semaphore_wait(barrier, len(peers))

        def copy(a, k, peer, arrival):
            return pltpu.make_async_remote_copy(
                src_ref=srcs[a] if gather else srcs[a].at[_slot(peer)], dst_ref=lands[a].at[_slot(peer if arrival else me)],
                send_sem=send.at[k, a], recv_sem=recv.at[k, a], device_id=peer, device_id_type=MESH)

        mine = [pltpu.make_async_copy(srcs[a] if gather else srcs[a].at[_slot(me)], lands[a].at[_slot(me)], local.at[a])
                for a in range(n)]
        out = [copy(a, k, peer, False) for a in range(n) for k, peer in enumerate(peers)]
        for cp in mine + out:
            cp.start()
        for a in range(n):
            for k, peer in enumerate(peers):
                copy(a, k, peer, True).wait_recv()
        for cp in out:
            cp.wait_send()
        for cp in mine:
            cp.wait()

    launch()
    return [land[...] for land in lands]


def _adam_math(w, g, m, v):
    m_ = ADAM_B1 * m + (1.0 - ADAM_B1) * g
    v_ = ADAM_B2 * v + (1.0 - ADAM_B2) * jnp.square(g)
    m_hat = m_ / (1.0 - ADAM_B1 ** ADAM_STEP)
    v_hat = v_ / (1.0 - ADAM_B2 ** ADAM_STEP)
    return -ADAM_LR * (m_hat / (jnp.sqrt(v_hat) + ADAM_EPS) + ADAM_WD * w), m_, v_


def _reduce_adamw(parts, w, m, v, name):
    _, R, L = parts.shape
    tr = _pick(R, (256, 128, 64, 32, 16, 8))

    def body(p_ref, w_ref, m_ref, v_ref, g_ref, d_ref, nm_ref, nv_ref):
        g = p_ref[0].astype(f32)
        for i in range(1, N_DEV):
            g = g + p_ref[i].astype(f32)
        g_ref[...] = g
        d_ref[...], nm_ref[...], nv_ref[...] = _adam_math(w_ref[...], g, m_ref[...], v_ref[...])

    blk = pl.BlockSpec((tr, L), lambda i: (i, 0))
    sh = jax.ShapeDtypeStruct((R, L), f32)
    return _call(body, name, (R // tr,), [pl.BlockSpec((N_DEV, tr, L), lambda i: (0, i, 0)), blk, blk, blk], (blk,) * 4, (sh,) * 4,
                 sem=("parallel",))(parts, w, m, v)


SMALL = (("w_spatial", (512, 128), 0), ("norm1_g", (1, 1024), 512), ("mem_norm_g", (1, 1024), 520), ("norm2_g", (1, 1024), 528),
         ("final_g", (1, 1024), 536), ("lb_logits", (2, 512), 544), ("ln_v_g", (1, 512), 552), ("ln_v_b", (1, 512), 556),
         ("b_spatial", (4, 128), 560), ("hgrn_norm_g", (1, 128), 564), ("conv_b", (1, 2816), 565))
LOSS_ROW, SMALL_USED, SMALL_ROWS = 587, 588, 640


def _segments(shape, base):
    r, n = shape
    per = n // 128
    return [(base + i * per + j, i, slice(j * 128, (j + 1) * 128)) for i in range(r) for j in range(per)]


def _pack_small(gs, loss_part):
    names = [n for n, _, _ in SMALL]

    def body(*refs):
        src, loss_ref, o_ref = dict(zip(names, refs[:-2])), refs[-2], refs[-1]
        o_ref[SMALL_USED:SMALL_ROWS, :] = jnp.zeros((SMALL_ROWS - SMALL_USED, 128), f32)
        o_ref[LOSS_ROW:LOSS_ROW + 1, :] = loss_ref[...]
        for name, shape, base in SMALL:
            ref = src[name]
            if name == "w_spatial":
                o_ref[base:base + 512, :] = ref[...].reshape(512, 128)
            elif name == "b_spatial":
                o_ref[base:base + 4, :] = ref[0:4, :]
            elif name == "hgrn_norm_g":
                per_head = [ref[b, h] for b in range(ref.shape[0]) for h in range(N_HEAD)]
                o_ref[base:base + 1, :] = functools.reduce(lambda u, v_: u + v_, per_head)
            else:
                for row, i, sl in _segments(shape, base):
                    o_ref[row:row + 1, :] = ref[i:i + 1, sl]

    return pl.pallas_call(body, name="pack_small", out_shape=jax.ShapeDtypeStruct((SMALL_ROWS, 128), f32))(
        *[gs[n] for n in names], loss_part)


def _small_update(gathered, w, m, v):
    names = [n for n, _, _ in SMALL]
    k = len(names)

    def body(*refs):
        p_ref = refs[0]
        ins = [dict(zip(names, refs[1 + i * k:1 + (i + 1) * k])) for i in range(3)]
        outs = [dict(zip(names, refs[1 + (3 + i) * k:1 + (4 + i) * k])) for i in range(4)]
        loss_ref, gsum = refs[-2], refs[-1]
        g = p_ref[0]
        for i in range(1, N_DEV):
            g = g + p_ref[i]
        gsum[...] = g
        loss_ref[...] = gsum[LOSS_ROW:LOSS_ROW + 1, :]
        for name, shape, base in SMALL:
            if name == "w_spatial":
                where = [(slice(base, base + 512), (slice(None), slice(None)))]
            else:
                where = [(slice(row, row + 1), (slice(i, i + 1), sl)) for row, i, sl in _segments(shape, base)]
            for rows, at in where:
                g_ = gsum[rows, :]
                d_, m_, v_ = _adam_math(ins[0][name][at], g_, ins[1][name][at], ins[2][name][at])
                for o, val in zip(outs, (g_, d_, m_, v_)):
                    o[name][at] = val

    args = [gathered] + [d[n] for d in (w, m, v) for n in names]
    out_shapes = [jax.ShapeDtypeStruct(shape, f32) for _ in range(4) for _, shape, _ in SMALL] + [jax.ShapeDtypeStruct((1, 128), f32)]
    outs = pl.pallas_call(body, name="small_update", out_shape=out_shapes, scratch_shapes=[pltpu.VMEM((SMALL_ROWS, 128), f32)])(*args)
    return [dict(zip(names, outs[i * k:(i + 1) * k])) for i in range(4)], outs[-1]


def _cols_full(g):
    return jnp.moveaxis(g, 0, -2).reshape(g.shape[1:-1] + (N_DEV * g.shape[-1],))


def _cols_parts(full):
    n = full.shape[-1] // N_DEV
    return jnp.moveaxis(full.reshape(full.shape[:-1] + (N_DEV, n)), -2, 0)


def kernel(x, mem, norm1_g, w_in, ln_v_g, ln_v_b, w_spatial, b_spatial, lb_logits, hgrn_norm_g, mem_norm_g, w_mem_kv, w_branch, w_out, norm2_g, w_up, conv_w, conv_b, w_down, final_g, loss_target, m_norm1_g, m_w_in, m_ln_v_g, m_ln_v_b, m_w_spatial, m_b_spatial, m_lb_logits, m_hgrn_norm_g, m_mem_norm_g, m_w_mem_kv, m_w_branch, m_w_out, m_norm2_g, m_w_up, m_conv_w, m_conv_b, m_w_down, m_final_g, v_norm1_g, v_w_in, v_ln_v_g, v_ln_v_b, v_w_spatial, v_b_spatial, v_lb_logits, v_hgrn_norm_g, v_mem_norm_g, v_w_mem_kv, v_w_branch, v_w_out, v_norm2_g, v_w_up, v_conv_w, v_conv_b, v_w_down, v_final_g):
    given = dict(locals())
    order = ("norm1_g", "w_in", "ln_v_g", "ln_v_b", "w_spatial", "b_spatial", "lb_logits", "hgrn_norm_g", "mem_norm_g",
             "w_mem_kv", "w_branch", "w_out", "norm2_g", "w_up", "conv_w", "conv_b", "w_down", "final_g")
    groups = dict(a=("w_in",), b=("w_mem_kv", "w_branch", "w_out"), c=("w_up", "conv_w", "w_down"))

    wire = {n: given[n][0].astype(f32 if n == "conv_w" else bf16) for ns in groups.values() for n in ns}
    g_in = _all_gather([wire["w_in"]], "gather_w_in")[0]
    late = groups["b"] + groups["c"]
    w_in_full = _cols_full(g_in)
    w_in_full, rest_wire = lax.optimization_barrier((w_in_full, [wire[n] for n in late]))
    rest = _seq_exchange(rest_wire, True, "gather_rest", 1)

    def late_b(after):
        _, (kv_, br_, out_) = lax.optimization_barrier((after, tuple(rest[0:3])))
        br_ = _cols_full(br_)
        return dict(w_mem_kv=kv_.reshape(D_MODEL, 2 * 512), w_branch=[br_[n] for n in range(3)], w_out=out_.reshape(D_MODEL, D_MODEL))

    def late_c(after):
        _, (up_, cw_, down_) = lax.optimization_barrier((after, tuple(rest[3:6])))
        return dict(w_up=_cols_full(up_), conv_w=_cols_full(cw_), w_down=down_.reshape(D_FF, D_MODEL))

    to_parts = dict(w_in=lambda g_: g_, w_up=lambda g_: g_, conv_w=_cols_parts,
                    w_branch=lambda g_: _cols_parts(jnp.stack(g_)).reshape(N_DEV, -1, 128),
                    w_mem_kv=lambda g_: g_.reshape(N_DEV, -1, 2 * 512), w_out=lambda g_: g_.reshape(N_DEV, -1, D_MODEL),
                    w_down=lambda g_: g_.reshape(N_DEV, -1, D_MODEL))
    scatters = {}

    def send(tag, grads_):
        parts = [to_parts[n](grads_[n]) for n in groups[tag]]
        scatters[tag] = _seq_exchange(parts, False, f"scatter_{tag}", dict(a=2, b=4, c=5)[tag])
        return jnp.zeros((8, 128), f32)

    small_2d = lambda prefix: {n: given[prefix + n].reshape(shape) for n, shape, _ in SMALL}
    p = small_2d("")
    p["w_spatial"] = w_spatial[0]
    updates = {}

    def update(tag):
        for n, parts in zip(groups[tag], scatters[tag]):
            two_d = (-1, given[n].shape[-1])
            updates[n] = _reduce_adamw(parts, *[given[pre + n].reshape(two_d) for pre in ("", "m_", "v_")], "adamw_" + n)

    def settle(chain):
        update("c")
        update("b")
        early = groups["c"] + groups["b"]
        chain, tied = lax.optimization_barrier((chain, [updates[n] for n in early]))
        updates.update(zip(early, tied))
        return chain

    loss_part, grad_x, gs = _local_step(x, mem, loss_target, p, w_in_full, late_b, late_c, send, settle)

    gathered = _seq_exchange([_pack_small(gs, loss_part)], True, "gather_small", 3)[0]

    update("a")
    grads, delta, new_m, new_v = {}, {}, {}, {}
    for n, res in updates.items():
        grads[n], delta[n], new_m[n], new_v[n] = [r.reshape(given[n].shape) for r in res]

    small_results, loss_row = _small_update(gathered, small_2d(""), small_2d("m_"), small_2d("v_"))
    for dst, res in zip((grads, delta, new_m, new_v), small_results):
        for n, _, _ in SMALL:
            dst[n] = res[n].reshape(given[n].shape)
    loss = loss_row[0, 0]

    return (loss, grad_x, *[grads[n] for n in order], *[delta[n] for n in order], *[new_m[n] for n in order],
            *[new_v[n] for n in order])
```

```python
import functools

import jax
import jax.numpy as jnp
from jax import lax
from jax.experimental import pallas as pl
from jax.experimental.pallas import tpu as pltpu
from jax.experimental.pallas import tpu_sc as plsc

f32 = jnp.float32
bf16 = jnp.bfloat16

N_DEV = 8
D_MODEL = 1024
EPS = 1e-6
GM_CHUNK = 128
HG_CHUNK = 64
HEAD = 128
N_HEAD = 4
MEM_LEN = 256
D_FF = 2816
IN_WIDTH = 6656
C_ZU, C_HQ, C_HF, C_HI, C_HG, C_XQ, C_GL = 0, 1024, 1536, 2048, 2560, 3072, 3584
ADAM_LR, ADAM_B1, ADAM_B2, ADAM_EPS, ADAM_WD, ADAM_STEP = 0.001, 0.9, 0.999, 1e-08, 0.01, 10
VMEM_LIMIT = 56 * 1024 * 1024
MESH = pl.DeviceIdType.MESH


def _pick(n, cands):
    for c in cands:
        if n % c == 0:
            return c
    return n


def _call(body, name, grid, in_specs, out_specs, out_shape, scratch=(), sem=None, **cp):
    params = dict(vmem_limit_bytes=VMEM_LIMIT, **cp)
    if sem is not None:
        params["dimension_semantics"] = sem
    return pl.pallas_call(
        body, name=name, grid=grid, in_specs=in_specs, out_specs=out_specs, out_shape=out_shape,
        scratch_shapes=list(scratch), compiler_params=pltpu.CompilerParams(**params))


_DN = {"nn": (((1,), (0,)), ((), ())), "nt": (((1,), (1,)), ((), ())), "tn": (((0,), (0,)), ((), ()))}


def _raw_dot(a, b, mode):
    return lax.dot_general(a.astype(bf16), b.astype(bf16), _DN[mode], preferred_element_type=f32)


@jax.custom_vjp
def _dot_nn(a, b):
    return _raw_dot(a, b, "nn")


_dot_nn.defvjp(lambda a, b: (_raw_dot(a, b, "nn"), (a, b)),
               lambda r, g: (_raw_dot(g, r[1], "nt"), _raw_dot(r[0], g, "tn")))


@jax.custom_vjp
def _dot_nt(a, b):
    return _raw_dot(a, b, "nt")


_dot_nt.defvjp(lambda a, b: (_raw_dot(a, b, "nt"), (a, b)),
               lambda r, g: (_raw_dot(g, r[1], "nn"), _raw_dot(g, r[0], "tn")))


@jax.custom_vjp
def _dot_tn(a, b):
    return _raw_dot(a, b, "tn")


_dot_tn.defvjp(lambda a, b: (_raw_dot(a, b, "tn"), (a, b)),
               lambda r, g: (_raw_dot(r[1], g, "nt"), _raw_dot(r[0], g, "nn")))


def _tri(n, lower):
    r = lax.broadcasted_iota(jnp.int32, (n, n), 0)
    c = lax.broadcasted_iota(jnp.int32, (n, n), 1)
    return ((c <= r) if lower else (c >= r)).astype(f32)


def _sel_dot(sel, x, mode, x_first=False):
    hi = x.astype(bf16)
    rest = x - hi.astype(f32)
    mid = rest.astype(bf16)
    lo = (rest - mid.astype(f32)).astype(bf16)
    sel = sel.astype(bf16)
    dot = lambda piece: lax.dot_general(*((piece, sel) if x_first else (sel, piece)), _DN[mode], preferred_element_type=f32)
    return dot(hi) + dot(mid) + dot(lo)


def _egrad(fn, x, ct):
    return jax.vjp(fn, x)[1](ct)[0]


def _mm(a, b, mode, out_dtype, name, tm, tn, tk=None, residual=None, shard=None, n_outer=False):
    if mode == "nn":
        (M, K), (_, N) = a.shape, b.shape
    elif mode == "nt":
        (M, K), (N, _) = a.shape, b.shape
    else:
        (K, M), (_, N) = a.shape, b.shape
    tm, tn = min(tm, M), min(tn, N)
    tk = K if tk is None else min(tk, K)
    assert M % tm == 0 and N % tn == 0 and K % tk == 0, (name, M, N, K, tm, tn, tk)
    nk = K // tk

    def body(*refs):
        acc_ref = refs[-1] if nk > 1 else None
        refs = refs[:-1] if nk > 1 else refs
        if residual is None:
            a_ref, b_ref, o_ref = refs
        else:
            a_ref, b_ref, r_ref, o_ref = refs

        def finish(r):
            if residual is not None:
                r = r + r_ref[...]
            if shard is None:
                o_ref[...] = r.astype(out_dtype)
            else:
                for s in range(tn // shard):
                    o_ref[s] = r[:, s * shard:(s + 1) * shard].astype(out_dtype)

        part = _raw_dot(a_ref[...], b_ref[...], mode)
        if nk == 1:
            finish(part)
            return
        k = pl.program_id(2)

        @pl.when(k == 0)
        def _():
            acc_ref[...] = part

        @pl.when((k > 0) & (k < nk - 1))
        def _():
            acc_ref[...] += part

        @pl.when(k == nk - 1)
        def _():
            finish(acc_ref[...] + part)

    def at(index):
        return (lambda j, i, k: index(i, j, k)) if n_outer else index

    a_spec = {"nn": pl.BlockSpec((tm, tk), at(lambda i, j, k: (i, k))),
              "nt": pl.BlockSpec((tm, tk), at(lambda i, j, k: (i, k))),
              "tn": pl.BlockSpec((tk, tm), at(lambda i, j, k: (k, i)))}[mode]
    b_spec = {"nn": pl.BlockSpec((tk, tn), at(lambda i, j, k: (k, j))),
              "nt": pl.BlockSpec((tn, tk), at(lambda i, j, k: (j, k))),
              "tn": pl.BlockSpec((tk, tn), at(lambda i, j, k: (k, j)))}[mode]
    o_spec = pl.BlockSpec((tm, tn), at(lambda i, j, k: (i, j)))
    in_specs = [a_spec, b_spec] + ([o_spec] if residual is not None else [])
    args = (a, b) + ((residual,) if residual is not None else ())
    out_shape = jax.ShapeDtypeStruct((M, N), out_dtype)
    if shard is not None:
        assert residual is None and tn % shard == 0
        o_spec = pl.BlockSpec((tn // shard, tm, shard), at(lambda i, j, k: (j, i, 0)))
        out_shape = jax.ShapeDtypeStruct((N // shard, M, shard), out_dtype)
    grid = (N // tn, M // tm, nk) if n_outer else (M // tm, N // tn, nk)
    return _call(body, name, grid, in_specs, o_spec, out_shape,
                 scratch=[pltpu.VMEM((tm, tn), f32)] if nk > 1 else [], sem=("parallel", "parallel", "arbitrary"))(*args)


def _rms_fwd(x, g, name, transposed=False):
    R, Dd = x.shape
    tr = _pick(R, (512, 256, 128))

    def body(x_ref, g_ref, o_ref, *t_ref):
        xf = x_ref[...]
        y = xf * lax.rsqrt(jnp.mean(xf * xf, axis=-1, keepdims=True) + EPS) * g_ref[...]
        o_ref[...] = y.astype(bf16)
        if transposed:
            t_ref[0][...] = y.T.astype(bf16)

    row = pl.BlockSpec((tr, Dd), lambda i: (i, 0))
    out_specs, out_shape = row, jax.ShapeDtypeStruct((R, Dd), bf16)
    if transposed:
        out_specs, out_shape = (row, pl.BlockSpec((Dd, tr), lambda i: (0, i))), (out_shape, jax.ShapeDtypeStruct((Dd, R), bf16))
    return _call(body, name, (R // tr,), [row, pl.BlockSpec((1, Dd), lambda i: (0, 0))], out_specs, out_shape, sem=("parallel",))(x, g)


def _rms_bwd(x, g, dh, name, residual=None):
    R, Dd = x.shape
    tr = _pick(R, (512, 256, 128))

    def body(*refs):
        if residual is None:
            x_ref, g_ref, dh_ref, dx_ref, dg_ref = refs
        else:
            x_ref, g_ref, dh_ref, r_ref, dx_ref, dg_ref = refs
        xf = x_ref[...]
        rs = lax.rsqrt(jnp.mean(xf * xf, axis=-1, keepdims=True) + EPS)
        y = xf * rs
        dh_ = dh_ref[...].astype(f32)
        dy = dh_ * g_ref[...]
        dx = rs * (dy - y * jnp.mean(dy * y, axis=-1, keepdims=True))
        if residual is not None:
            dx = dx + r_ref[...]
        dx_ref[...] = dx

        @pl.when(pl.program_id(0) == 0)
        def _():
            dg_ref[...] = jnp.zeros_like(dg_ref)

        dg_ref[...] += jnp.sum(dh_ * y, axis=0, keepdims=True)

    row = pl.BlockSpec((tr, Dd), lambda i: (i, 0))
    vec = pl.BlockSpec((1, Dd), lambda i: (0, 0))
    in_specs = [row, vec, row] + ([row] if residual is not None else [])
    args = (x, g, dh) + ((residual,) if residual is not None else ())
    return _call(body, name, (R // tr,), in_specs, (row, vec),
                 (jax.ShapeDtypeStruct((R, Dd), f32), jax.ShapeDtypeStruct((1, Dd), f32)), sem=("arbitrary",))(*args)


def _final_loss(x2, g, target):
    R, Dd = x2.shape
    tr = _pick(R, (512, 256, 128))

    def body(x_ref, g_ref, t_ref, loss_ref, dx_ref, dxb_ref, dg_ref):
        xf = x_ref[...]
        rs = lax.rsqrt(jnp.mean(xf * xf, axis=-1, keepdims=True) + EPS)
        y = xf * rs
        err = y * g_ref[...] - t_ref[...]
        dh_ = err * (1.0 / Dd)
        dy = dh_ * g_ref[...]
        dx = rs * (dy - y * jnp.mean(dy * y, axis=-1, keepdims=True))
        dx_ref[...] = dx
        dxb_ref[...] = dx.astype(bf16)

        @pl.when(pl.program_id(0) == 0)
        def _():
            dg_ref[...] = jnp.zeros_like(dg_ref)
            loss_ref[...] = jnp.zeros_like(loss_ref)

        dg_ref[...] += jnp.sum(dh_ * y, axis=0, keepdims=True)
        part = jnp.sum(jnp.mean(err * err, axis=-1, keepdims=True), axis=0, keepdims=True)
        loss_ref[...] += 0.5 * part

    row = pl.BlockSpec((tr, Dd), lambda i: (i, 0))
    vec = pl.BlockSpec((1, Dd), lambda i: (0, 0))
    return _call(body, "final_loss", (R // tr,), [row, vec, row], (pl.BlockSpec((1, 128), lambda i: (0, 0)), row, row, vec),
                 (jax.ShapeDtypeStruct((1, 128), f32), jax.ShapeDtypeStruct((R, Dd), f32), jax.ShapeDtypeStruct((R, Dd), bf16),
                  jax.ShapeDtypeStruct((1, Dd), f32)), sem=("arbitrary",))(x2, g, target)


def _gmlp_parts(zuv, ln_g, ln_b):
    zu, zv = zuv[:, :512], zuv[:, 512:]
    u = jax.nn.gelu(zu)
    v = jax.nn.gelu(zv)
    mu = jnp.mean(v, axis=-1, keepdims=True)
    rs = lax.rsqrt(jnp.mean(jnp.square(v - mu), axis=-1, keepdims=True) + EPS)
    xh = (v - mu) * rs
    return zu, zv, u, xh, rs, xh * ln_g + ln_b


def _gmlp_fwd(proj, ln_g, ln_b, w_s, b_st):
    T = proj.shape[0]

    def body(p_ref, g_ref, b_ref, w_ref, bs_ref, o_ref):
        _, _, u, _, _, vn = _gmlp_parts(p_ref[...].astype(f32), g_ref[...], b_ref[...])
        causal = _tri(GM_CHUNK, True) > 0
        for gi in range(N_HEAD):
            sl = slice(gi * HEAD, (gi + 1) * HEAD)
            w = jnp.where(causal, w_ref[gi], 0.0)
            mixed = _raw_dot(w, vn[:, sl], "nn") + bs_ref[:, gi:gi + 1]
            o_ref[:, sl] = (u[:, sl] * mixed).astype(bf16)

    vec = pl.BlockSpec((1, 512), lambda i: (0, 0))
    return _call(body, "gmlp_fwd", (T // GM_CHUNK,),
                 [pl.BlockSpec((GM_CHUNK, 1024), lambda i: (i, 0)), vec, vec,
                  pl.BlockSpec((N_HEAD, GM_CHUNK, GM_CHUNK), lambda i: (0, 0, 0)), pl.BlockSpec((GM_CHUNK, 128), lambda i: (0, 0))],
                 pl.BlockSpec((GM_CHUNK, 512), lambda i: (i, 0)), jax.ShapeDtypeStruct((T, 512), bf16), sem=("parallel",))(
        proj, ln_g, ln_b, w_s, b_st)


def _gmlp_bwd(proj, ln_g, ln_b, w_s, b_st, da):
    T = proj.shape[0]

    def body(p_ref, g_ref, b_ref, w_ref, bs_ref, da_ref, dp_ref, dg_ref, db_ref, dw_ref, dbs_ref):
        zu, zv, u, xh, rs, vn = _gmlp_parts(p_ref[...].astype(f32), g_ref[...], b_ref[...])
        causal = _tri(GM_CHUNK, True) > 0
        sub = lax.broadcasted_iota(jnp.int32, (8, GM_CHUNK), 0)
        ones = jnp.ones((8, HEAD), f32)
        dout = da_ref[...].astype(f32)

        @pl.when(pl.program_id(0) == 0)
        def _():
            for r in (dg_ref, db_ref, dw_ref, dbs_ref):
                r[...] = jnp.zeros_like(r)

        du, dvn, dbs = [], [], jnp.zeros((8, GM_CHUNK), f32)
        for gi in range(N_HEAD):
            sl = slice(gi * HEAD, (gi + 1) * HEAD)
            w = jnp.where(causal, w_ref[gi], 0.0)
            mixed = _raw_dot(w, vn[:, sl], "nn") + bs_ref[:, gi:gi + 1]
            du.append(dout[:, sl] * mixed)
            dm = dout[:, sl] * u[:, sl]
            row_sums = _sel_dot(ones, dm, "nt")
            dbs = dbs + jnp.where(sub == gi, row_sums, 0.0)
            dw_ref[gi] += jnp.where(causal, _raw_dot(dm, vn[:, sl], "nt"), 0.0)
            dvn.append(_raw_dot(w, dm, "tn"))
        dbs_ref[...] += dbs
        du = jnp.concatenate(du, axis=-1)
        dvn = jnp.concatenate(dvn, axis=-1)
        dg_ref[...] += jnp.sum(dvn * xh, axis=0, keepdims=True)
        db_ref[...] += jnp.sum(dvn, axis=0, keepdims=True)
        dxh = dvn * g_ref[...]
        dv = rs * (dxh - jnp.mean(dxh, axis=-1, keepdims=True) - xh * jnp.mean(dxh * xh, axis=-1, keepdims=True))
        dp_ref[:, :512] = _egrad(jax.nn.gelu, zu, du).astype(bf16)
        dp_ref[:, 512:] = _egrad(jax.nn.gelu, zv, dv).astype(bf16)

    vec = pl.BlockSpec((1, 512), lambda i: (0, 0))
    wsp = pl.BlockSpec((N_HEAD, GM_CHUNK, GM_CHUNK), lambda i: (0, 0, 0))
    return _call(body, "gmlp_bwd", (T // GM_CHUNK,),
                 [pl.BlockSpec((GM_CHUNK, 1024), lambda i: (i, 0)), vec, vec, wsp, pl.BlockSpec((GM_CHUNK, 128), lambda i: (0, 0)),
                  pl.BlockSpec((GM_CHUNK, 512), lambda i: (i, 0))],
                 (pl.BlockSpec((GM_CHUNK, 1024), lambda i: (i, 0)), vec, vec, wsp, pl.BlockSpec((8, GM_CHUNK), lambda i: (0, 0))),
                 (jax.ShapeDtypeStruct((T, 1024), bf16), jax.ShapeDtypeStruct((1, 512), f32), jax.ShapeDtypeStruct((1, 512), f32),
                  jax.ShapeDtypeStruct((N_HEAD, GM_CHUNK, GM_CHUNK), f32), jax.ShapeDtypeStruct((8, GM_CHUNK), f32)),
                 sem=("arbitrary",))(proj, ln_g, ln_b, w_s, b_st, da)


HG_SUB = 8
HG_NSUB = HG_CHUNK // HG_SUB


def _two_level_matrix():
    r = lax.broadcasted_iota(jnp.int32, (2 * HG_CHUNK, HG_CHUNK), 0)
    c = lax.broadcasted_iota(jnp.int32, (2 * HG_CHUNK, HG_CHUNK), 1)
    t = jnp.where(r < HG_CHUNK, r, r - HG_CHUNK)
    local = (r < HG_CHUNK) & (t // HG_SUB == c // HG_SUB) & (c <= t)
    before = (r >= HG_CHUNK) & (c < (t // HG_SUB) * HG_SUB)
    return (local | before).astype(f32)


def _two_level_sums(x):
    two = _sel_dot(_two_level_matrix(), x, "nn")
    return two[:HG_CHUNK], two[HG_CHUNK:]


@jax.custom_vjp
def _two_level_cumsum(x):
    return _two_level_sums(x)


_two_level_cumsum.defvjp(
    lambda x: (_two_level_sums(x), None),
    lambda _, g: (_sel_dot(_two_level_matrix(), jnp.concatenate(g, axis=0), "tn"),))


def _tile_matrix():
    s = lax.broadcasted_iota(jnp.int32, (HG_SUB, HG_CHUNK), 0)
    j = lax.broadcasted_iota(jnp.int32, (HG_SUB, HG_CHUNK), 1)
    return (j % HG_SUB == s).astype(f32)


@jax.custom_vjp
def _tile_lanes(x):
    return _sel_dot(_tile_matrix(), x, "nn", x_first=True)


_tile_lanes.defvjp(
    lambda x: (_sel_dot(_tile_matrix(), x, "nn", x_first=True), None),
    lambda _, g: (_sel_dot(_tile_matrix(), g, "nt", x_first=True),))


def _block_rows(x):
    k = x.shape[-1]
    return jnp.broadcast_to(x.reshape(HG_NSUB, 1, HG_SUB, k), (HG_NSUB, HG_SUB, HG_SUB, k)).reshape(HG_CHUNK, HG_SUB, k)


def _hgrn_chunk(st0, q_raw, f_raw, i_raw, g_raw, l0, l1, ng):
    C, SUB = HG_CHUNK, HG_SUB
    lb = jax.nn.sigmoid(l0 - l1)
    fg = lb + (1.0 - lb) * jax.nn.sigmoid(f_raw)
    kk = 1.0 - fg
    qf = jax.nn.silu(q_raw)
    al, base = _two_level_cumsum(jnp.log(fg))
    a = al + base
    row = lax.broadcasted_iota(jnp.int32, (C, HEAD), 0)
    a_last = jnp.sum(jnp.where(row == C - 1, a, 0.0), axis=0, keepdims=True)
    inter = _dot_nt(qf * jnp.exp(a), st0)
    qt = qf * jnp.exp(al)
    rb = lax.broadcasted_iota(jnp.int32, (C, C), 0) // SUB
    cb = lax.broadcasted_iota(jnp.int32, (C, C), 1) // SUB
    scores = jnp.zeros((C, C), f32)
    for i in range(1, HG_NSUB):
        base_i = jnp.sum(jnp.where(row == i * SUB, base, 0.0), axis=0, keepdims=True)
        kt = kk * jnp.exp(jnp.minimum(base_i - a, 0.0))
        scores = scores + jnp.where((rb == i) & (cb < i), _dot_nt(qt, kt), 0.0)
    t_i = lax.broadcasted_iota(jnp.int32, (C, SUB, HEAD), 0) % SUB
    s_i = lax.broadcasted_iota(jnp.int32, (C, SUB, HEAD), 1)
    decay = jnp.exp(jnp.where(s_i <= t_i, al[:, None, :] - _block_rows(al), -jnp.inf))
    diag = jnp.sum(qf[:, None, :] * decay * _block_rows(kk), axis=-1)
    scores = scores + jnp.where(rb == cb, _tile_lanes(diag), 0.0)
    o = inter + _dot_nn(scores, i_raw)
    st1 = jnp.exp(a_last) * st0 + _dot_tn(i_raw, kk * jnp.exp(a_last - a))
    on = o * lax.rsqrt(jnp.mean(o * o, axis=-1, keepdims=True) + EPS) * ng
    return st1, on * jax.nn.silu(g_raw)


def _hgrn_specs(S, Bl, rev):
    N = S // HG_CHUNK
    chunk = (lambda n: N - 1 - n) if rev else (lambda n: n)
    col = lambda c0: pl.BlockSpec((Bl, HG_CHUNK, 512), lambda n: (0, chunk(n), c0 // 512))
    st = pl.BlockSpec((Bl, N_HEAD, 1, HEAD, HEAD), lambda n: (0, 0, chunk(n), 0, 0))
    full = lambda *s: pl.BlockSpec(s, functools.partial(lambda n, nd: (0,) * nd, nd=len(s)))
    return N, col, st, full


def _hgrn_fwd(proj, lb_logits, ng, Bl, S):
    N, col, st, full = _hgrn_specs(S, Bl, False)

    def body(q_ref, f_ref, i_ref, g_ref, l_ref, ng_ref, o_ref, st_ref, state):
        @pl.when(pl.program_id(0) == 0)
        def _():
            state[...] = jnp.zeros_like(state)

        for b in range(Bl):
            for h in range(N_HEAD):
                sl = slice(h * HEAD, (h + 1) * HEAD)
                st0 = state[b, h]
                st_ref[b, h, 0] = st0
                st1, out = _hgrn_chunk(st0, *[r[b, :, sl].astype(f32) for r in (q_ref, f_ref, i_ref, g_ref)],
                                       l_ref[0:1, sl], l_ref[1:2, sl], ng_ref[...])
                state[b, h] = st1
                o_ref[b, :, sl] = out.astype(bf16)

    return _call(body, "hgrn_fwd", (N,), [col(C_HQ), col(C_HF), col(C_HI), col(C_HG), full(2, 512), full(1, HEAD)],
                 (col(0), st),
                 (jax.ShapeDtypeStruct((Bl, S, 512), bf16), jax.ShapeDtypeStruct((Bl, N_HEAD, N, HEAD, HEAD), f32)),
                 scratch=[pltpu.VMEM((Bl, N_HEAD, HEAD, HEAD), f32)], sem=("arbitrary",))(
        proj, proj, proj, proj, lb_logits, ng)


def _hgrn_bwd(proj, lb_logits, ng, states, db, Bl, S):
    N, col, st, full = _hgrn_specs(S, Bl, True)

    def body(q_ref, f_ref, i_ref, g_ref, l_ref, ng_ref, st_ref, db_ref,
             dq_ref, df_ref, di_ref, dg_ref, dl_ref, dng_ref, dstate):
        @pl.when(pl.program_id(0) == 0)
        def _():
            dstate[...] = jnp.zeros_like(dstate)
            dl_ref[...] = jnp.zeros_like(dl_ref)
            dng_ref[...] = jnp.zeros_like(dng_ref)

        for b in range(Bl):
            for h in range(N_HEAD):
                sl = slice(h * HEAD, (h + 1) * HEAD)
                _, vjp = jax.vjp(_hgrn_chunk, st_ref[b, h, 0], *[r[b, :, sl].astype(f32) for r in (q_ref, f_ref, i_ref, g_ref)],
                                 l_ref[0:1, sl], l_ref[1:2, sl], ng_ref[...])
                dst0, dq, df, di, dg, dl0, dl1, dng = vjp((dstate[b, h], db_ref[b, :, sl].astype(f32)))
                dstate[b, h] = dst0
                dq_ref[b, :, sl] = dq.astype(bf16)
                df_ref[b, :, sl] = df.astype(bf16)
                di_ref[b, :, sl] = di.astype(bf16)
                dg_ref[b, :, sl] = dg.astype(bf16)
                dl_ref[0:1, sl] += dl0
                dl_ref[1:2, sl] += dl1
                dng_ref[b, h] += dng

    return _call(body, "hgrn_bwd", (N,),
                 [col(C_HQ), col(C_HF), col(C_HI), col(C_HG), full(2, 512), full(1, HEAD), st, col(0)],
                 (*[col(0)] * 4, full(2, 512), full(Bl, N_HEAD, 1, HEAD)),
                 (*[jax.ShapeDtypeStruct((Bl, S, 512), bf16)] * 4, jax.ShapeDtypeStruct((2, 512), f32),
                  jax.ShapeDtypeStruct((Bl, N_HEAD, 1, HEAD), f32)),
                 scratch=[pltpu.VMEM((Bl, N_HEAD, HEAD, HEAD), f32)], sem=("arbitrary",))(
        proj, proj, proj, proj, lb_logits, ng, states, db)


def _attn_probs(q, k):
    s = _raw_dot(q, k, "nt") * (HEAD ** -0.5)
    e = jnp.exp(s - jnp.max(s, axis=-1, keepdims=True))
    return e / jnp.sum(e, axis=-1, keepdims=True)


def _attn_specs(S, tq):
    nq = S // tq
    q = pl.BlockSpec((tq, 512), lambda b, i: (b * nq + i, C_XQ // 512))
    kv = pl.BlockSpec((1, MEM_LEN, 1024), lambda b, i: (b, 0, 0))
    o = pl.BlockSpec((tq, 512), lambda b, i: (b * nq + i, 0))
    return nq, q, kv, o


def _attn_fwd(proj, kv, Bl, S):
    tq = _pick(S, (512, 256, 128))
    nq, qs, kvs, os_ = _attn_specs(S, tq)

    def body(q_ref, kv_ref, o_ref):
        for h in range(N_HEAD):
            sl = slice(h * HEAD, (h + 1) * HEAD)
            p = _attn_probs(q_ref[:, sl], kv_ref[0, :, sl])
            o_ref[:, sl] = _raw_dot(p, kv_ref[0, :, 512 + h * HEAD:512 + (h + 1) * HEAD], "nn").astype(bf16)

    return _call(body, "attn_fwd", (Bl, nq), [qs, kvs], os_, jax.ShapeDtypeStruct((Bl * S, 512), bf16),
                 sem=("parallel", "parallel"))(proj, kv)


def _attn_bwd(proj, kv, dc, Bl, S):
    tq = _pick(S, (512, 256, 128))
    nq, qs, kvs, os_ = _attn_specs(S, tq)

    def body(q_ref, kv_ref, do_ref, dq_ref, dkv_ref):
        @pl.when(pl.program_id(1) == 0)
        def _():
            dkv_ref[...] = jnp.zeros_like(dkv_ref)

        for h in range(N_HEAD):
            sl = slice(h * HEAD, (h + 1) * HEAD)
            vsl = slice(512 + h * HEAD, 512 + (h + 1) * HEAD)
            q, k, v, do = q_ref[:, sl], kv_ref[0, :, sl], kv_ref[0, :, vsl], do_ref[:, sl]
            p = _attn_probs(q, k)
            dkv_ref[0, :, vsl] += _raw_dot(p, do, "tn")
            dp = _raw_dot(do, v, "nt")
            ds = p * (dp - jnp.sum(dp * p, axis=-1, keepdims=True)) * (HEAD ** -0.5)
            dq_ref[:, sl] = _raw_dot(ds, k, "nn").astype(bf16)
            dkv_ref[0, :, sl] += _raw_dot(ds, q, "tn")

    return _call(body, "attn_bwd", (Bl, nq), [qs, kvs, os_], (os_, kvs),
                 (jax.ShapeDtypeStruct((Bl * S, 512), bf16), jax.ShapeDtypeStruct((Bl, MEM_LEN, 1024), f32)),
                 sem=("arbitrary", "arbitrary"))(proj, kv, dc)


def _merge_specs(tm, tn):
    br = pl.BlockSpec((tm, 512), lambda i, j: (i, 0))
    w = pl.BlockSpec((512, tn), lambda i, j: (0, j))
    gl = [pl.BlockSpec((tm, tn), functools.partial(lambda i, j, n: (i, (C_GL + n * D_MODEL) // tn + j), n=n)) for n in range(3)]
    return [br, br, br, w, w, w, *gl]


def _merge_fwd(branches, wb, proj):
    T = proj.shape[0]
    tm, tn = _pick(T, (1024, 512, 256, 128)), 512

    def body(a_ref, b_ref, c_ref, w0, w1, w2, g0, g1, g2, o_ref):
        acc = jnp.zeros((tm, tn), f32)
        for x_ref, w_ref, g_ref in ((a_ref, w0, g0), (b_ref, w1, g1), (c_ref, w2, g2)):
            acc = acc + jax.nn.sigmoid(g_ref[...].astype(f32)) * _raw_dot(x_ref[...], w_ref[...], "nn")
        o_ref[...] = acc.astype(bf16)

    return _call(body, "merge_fwd", (T // tm, D_MODEL // tn), _merge_specs(tm, tn), pl.BlockSpec((tm, tn), lambda i, j: (i, j)),
                 jax.ShapeDtypeStruct((T, D_MODEL), bf16), sem=("parallel", "parallel"))(*branches, *wb, proj, proj, proj)


def _merge_bwd(branches, wb, proj, dmerged):
    T = proj.shape[0]
    tm, tn = _pick(T, (1024, 512, 256, 128)), 512

    def body(a_ref, b_ref, c_ref, w0, w1, w2, g0, g1, g2, dm_ref, dgl_ref, d0, d1, d2):
        dm = dm_ref[...]
        for n, (x_ref, w_ref, g_ref, d_ref) in enumerate(((a_ref, w0, g0, d0), (b_ref, w1, g1, d1), (c_ref, w2, g2, d2))):
            up = _raw_dot(x_ref[...], w_ref[...], "nn")
            logits = g_ref[...].astype(f32)
            dgl_ref[n] = _egrad(jax.nn.sigmoid, logits, dm * up).astype(bf16)
            d_ref[...] = (dm * jax.nn.sigmoid(logits)).astype(bf16)

    blk = pl.BlockSpec((tm, tn), lambda i, j: (i, j))
    sh = jax.ShapeDtypeStruct((T, D_MODEL), bf16)
    outs = _call(body, "merge_bwd", (T // tm, D_MODEL // tn), [*_merge_specs(tm, tn), blk],
                 (pl.BlockSpec((3, tm, tn), lambda i, j: (0, i, j)), blk, blk, blk),
                 (jax.ShapeDtypeStruct((3, T, D_MODEL), bf16), sh, sh, sh),
                 sem=("parallel", "parallel"))(*branches, *wb, proj, proj, proj, dmerged)
    return outs[0], outs[1:]


CONV_TC = 256


def _shift_down(a, k):
    r = pltpu.roll(a, k, 0)
    row = lax.broadcasted_iota(jnp.int32, (8, a.shape[1]), 0)
    return jnp.concatenate([jnp.where(row >= k, r[:8], 0.0), r[8:]], axis=0)


def _shift_up(a, k):
    n = a.shape[0]
    r = pltpu.roll(a, n - k, 0)
    row = lax.broadcasted_iota(jnp.int32, (8, a.shape[1]), 0)
    return jnp.concatenate([r[:n - 8], jnp.where(row < 8 - k, r[n - 8:], 0.0)], axis=0)


def _conv_pre(a, a1, a2, cw, cb):
    return cb + cw[0:1] * a2 + cw[1:2] * a1 + cw[2:3] * a


def _up_conv_fwd(h2, w_up, cw, cb):
    Bl, S, Dd = h2.shape
    nc = D_FF // CONV_TC

    def body(h_ref, wa_ref, wb_ref, cw_ref, cb_ref, a_ref, b_ref, o_ref):
        a16 = _raw_dot(h_ref[0], wa_ref[...], "nn").astype(bf16)
        b16 = _raw_dot(h_ref[0], wb_ref[...], "nn").astype(bf16)
        a_ref[0], b_ref[0] = a16, b16
        a = a16.astype(f32)
        ac = _conv_pre(a, _shift_down(a, 1), _shift_down(a, 2), cw_ref[...], cb_ref[...])
        o_ref[0] = (jax.nn.silu(ac) * b16.astype(f32)).astype(bf16)

    seq = pl.BlockSpec((1, S, CONV_TC), lambda b, c: (b, 0, c))
    sh = jax.ShapeDtypeStruct((Bl, S, D_FF), bf16)
    return _call(body, "up_conv_fwd", (Bl, nc),
                 [pl.BlockSpec((1, S, Dd), lambda b, c: (b, 0, 0)), pl.BlockSpec((Dd, CONV_TC), lambda b, c: (0, c)),
                  pl.BlockSpec((Dd, CONV_TC), lambda b, c: (0, nc + c)), pl.BlockSpec((3, CONV_TC), lambda b, c: (0, c)),
                  pl.BlockSpec((1, CONV_TC), lambda b, c: (0, c))],
                 (seq, seq, seq), (sh, sh, sh), sem=("parallel", "parallel"))(h2, w_up, w_up, cw, cb)


def _down_conv_bwd(dx2, w_down, a, b, cw, cb):
    Bl, S, Dd = dx2.shape
    nc = D_FF // CONV_TC

    def body(dx_ref, wd_ref, a_ref, b_ref, cw_ref, cb_ref, da_ref, db_ref, dcw_ref, dcb_ref):
        dact = _raw_dot(dx_ref[0], wd_ref[...], "nt").astype(bf16).astype(f32)
        a, cw = a_ref[0].astype(f32), cw_ref[...]
        a1, a2 = _shift_down(a, 1), _shift_down(a, 2)
        ac = _conv_pre(a, a1, a2, cw, cb_ref[...])
        sg = jax.nn.sigmoid(ac)
        gated = dact * sg
        db_ref[0] = (gated * ac).astype(bf16)
        dac = gated * b_ref[0].astype(f32) * (1.0 + ac * (1.0 - sg))
        da_ref[0] = (cw[2:3] * dac + cw[1:2] * _shift_up(dac, 1) + cw[0:1] * _shift_up(dac, 2)).astype(bf16)
        dcw_ref[0, 0:1, :] = jnp.sum(dac * a2, axis=0, keepdims=True)
        dcw_ref[0, 1:2, :] = jnp.sum(dac * a1, axis=0, keepdims=True)
        dcw_ref[0, 2:3, :] = jnp.sum(dac * a, axis=0, keepdims=True)
        dcb_ref[0] = jnp.sum(dac, axis=0, keepdims=True)

    seq = pl.BlockSpec((1, S, CONV_TC), lambda b_, c: (b_, 0, c))
    sh = jax.ShapeDtypeStruct((Bl, S, D_FF), bf16)
    return _call(body, "down_conv_bwd", (Bl, nc),
                 [pl.BlockSpec((1, S, Dd), lambda b_, c: (b_, 0, 0)), pl.BlockSpec((CONV_TC, Dd), lambda b_, c: (c, 0)), seq, seq,
                  pl.BlockSpec((3, CONV_TC), lambda b_, c: (0, c)), pl.BlockSpec((1, CONV_TC), lambda b_, c: (0, c))],
                 (seq, seq, pl.BlockSpec((1, 3, CONV_TC), lambda b_, c: (b_, 0, c)), pl.BlockSpec((1, 1, CONV_TC), lambda b_, c: (b_, 0, c))),
                 (sh, sh, jax.ShapeDtypeStruct((Bl, 3, D_FF), f32), jax.ShapeDtypeStruct((Bl, 1, D_FF), f32)),
                 sem=("parallel", "parallel"))(dx2, w_down, a, b, cw, cb)


def _local_step(x, mem, target, p, w_in, late_b, late_c, send, settle):
    Bl, S, Dd = x.shape
    T = Bl * S
    x2d, t2d, mem2d = x.reshape(T, Dd), target.reshape(T, Dd), mem.reshape(Bl * MEM_LEN, Dd)
    b_st = jnp.pad(p["b_spatial"].T, ((0, 0), (0, 128 - N_HEAD)))
    lbl = p["lb_logits"]

    h, h_t = _rms_fwd(x2d, p["norm1_g"], "norm1_fwd", transposed=True)
    proj = _mm(h, w_in, "nn", bf16, "proj_fwd", 1024, 1664)
    a_out = _gmlp_fwd(proj, p["ln_v_g"], p["ln_v_b"], p["w_spatial"], b_st)
    proj3 = proj.reshape(Bl, S, IN_WIDTH)
    b_out, states = _hgrn_fwd(proj3, lbl, p["hgrn_norm_g"], Bl, S)
    b_out = b_out.reshape(T, 512)
    memn = _rms_fwd(mem2d, p["mem_norm_g"], "memnorm_fwd")
    w = late_b(b_out)
    wb = w["w_branch"]
    kv = _mm(memn, w["w_mem_kv"], "nn", f32, "kv_fwd", 512, 1024).reshape(Bl, MEM_LEN, 2 * 512)
    c_out = _attn_fwd(proj, kv, Bl, S)
    branches = (a_out, b_out, c_out)
    merged = _merge_fwd(branches, wb, proj)
    x1 = _mm(merged, w["w_out"], "nn", f32, "out_fwd", 1024, 1024, residual=x2d)
    h2, h2_t = _rms_fwd(x1, p["norm2_g"], "norm2_fwd", transposed=True)
    w.update(late_c(h2))
    ffn_a, ffn_b, act = _up_conv_fwd(h2.reshape(Bl, S, Dd), w["w_up"], w["conv_w"], p["conv_b"])
    act = act.reshape(T, D_FF)
    x2 = _mm(act, w["w_down"], "nn", f32, "down_fwd", 512, 1024, residual=x1)
    loss_part, dx2, dx2_16, g_final = _final_loss(x2, p["final_g"], t2d)

    g_w_down = _mm(act, dx2_16, "tn", bf16, "down_dw", 1408, 1024, 1024)
    da, db, g_conv_w, g_conv_b = _down_conv_bwd(dx2_16.reshape(Bl, S, Dd), w["w_down"], ffn_a, ffn_b, w["conv_w"], p["conv_b"])
    da, db = da.reshape(T, D_FF), db.reshape(T, D_FF)
    shard = 2 * D_FF // N_DEV
    g_w_up = jnp.concatenate([_mm(h2_t, d, "nn", bf16, f"up_dw_{n}", 512, 1408, shard=shard, n_outer=True)
                              for n, d in (("a", da), ("b", db))], axis=0)
    tok = send("c", dict(w_up=g_w_up, conv_w=jnp.sum(g_conv_w, axis=0), w_down=g_w_down))
    dh2 = _mm(da, w["w_up_a"], "nt", f32, "up_dx_a", 512, 1024)
    dh2 = _mm(db, w["w_up_b"], "nt", f32, "up_dx_b", 512, 1024, residual=dh2)
    dx1, g_norm2 = _rms_bwd(x1, p["norm2_g"] + tok[0, 0], dh2, "norm2_bwd", residual=dx2)

    g_w_out = _mm(merged, dx1, "tn", bf16, "out_dw", 1024, 1024, 1024)
    dmerged = _mm(dx1, w["w_out"], "nt", f32, "out_dx", 1024, 1024)
    dgl, dup = _merge_bwd(branches, wb, proj, dmerged)
    g_w_branch = [_mm(branches[n], dup[n], "tn", bf16, f"branch_dw{n}", 512, 1024, 1024) for n in range(3)]
    dbr = [_mm(dup[n], wb[n], "nt", bf16, f"branch_dx{n}", 1024, 512) for n in range(3)]
    dxq, dkv = _attn_bwd(proj, kv, dbr[2], Bl, S)
    dkv = dkv.reshape(Bl * MEM_LEN, 2 * 512)
    g_w_kv = _mm(memn, dkv, "tn", bf16, "kv_dw", 1024, 1024, 512)
    tok = send("b", dict(w_mem_kv=g_w_kv, w_branch=g_w_branch, w_out=g_w_out))
    dmemn = _mm(dkv, w["w_mem_kv"], "nt", f32, "kv_dx", 512, 1024)
    _, g_mem_norm = _rms_bwd(mem2d, p["mem_norm_g"], dmemn, "memnorm_bwd")
    dzuv, g_ln_g, g_ln_b, g_w_sp, g_b_sp = _gmlp_bwd(proj, p["ln_v_g"] + tok[0, 0], p["ln_v_b"], p["w_spatial"], b_st, dbr[0])
    *dqfig, g_lbl, g_ng = _hgrn_bwd(proj3, lbl, p["hgrn_norm_g"], states, dbr[1].reshape(Bl, S, 512), Bl, S)
    dq, df, di, dg = [d.reshape(T, 512) for d in dqfig]
    dproj = jnp.concatenate([dzuv, dq, df, di, dg, dxq, dgl[0], dgl[1], dgl[2]], axis=-1)
    g_w_in = _mm(h_t, dproj, "nn", bf16, "proj_dw", 512, 1664, shard=IN_WIDTH // N_DEV, n_outer=True)
    tok = send("a", dict(w_in=g_w_in))
    dh = _mm(settle(dproj), w_in, "nt", f32, "proj_dx", 512, 1024)
    dx, g_norm1 = _rms_bwd(x2d, p["norm1_g"] + tok[0, 0], dh, "norm1_bwd", residual=dx1)

    gs = dict(w_spatial=g_w_sp, norm1_g=g_norm1, mem_norm_g=g_mem_norm, norm2_g=g_norm2, final_g=g_final, lb_logits=g_lbl,
              ln_v_g=g_ln_g, ln_v_b=g_ln_b, b_spatial=g_b_sp, hgrn_norm_g=g_ng, conv_b=g_conv_b)
    return loss_part, dx.reshape(Bl, S, Dd), gs


def _coords():
    return lax.axis_index("x"), lax.axis_index("y"), lax.axis_index("c")


def _slot(dev):
    return 4 * dev[0] + 2 * dev[1] + dev[2]


def _comm_call(body, name, arrays, out_shapes, n_sem):
    n = len(arrays)
    hbm = pl.BlockSpec(memory_space=pl.ANY)
    return pl.pallas_call(
        body, name=name, out_shape=out_shapes, in_specs=[hbm] * n, out_specs=[hbm] * n,
        scratch_shapes=[pltpu.SemaphoreType.DMA((n_sem, n)), pltpu.SemaphoreType.DMA((n_sem, n)), pltpu.SemaphoreType.DMA((n,))])(*arrays)


def _all_gather(blocks, name):
    n = len(blocks)

    def body(*refs):
        x_refs, o_refs, (send_sems, recv_sems, local_sems) = refs[:n], refs[n:2 * n], refs[2 * n:]
        x, y, c = _coords()
        me, sibling = (x, y, c), (x, y, 1 - c)
        chips = [(1 - x, y), (x, 1 - y), (1 - x, 1 - y)]

        def copy(a, k, block_dev, to, from_input=False):
            dst = o_refs[a].at[_slot(block_dev)]
            return pltpu.make_async_remote_copy(src_ref=x_refs[a] if from_input else dst, dst_ref=dst, send_sem=send_sems.at[k, a],
                                                recv_sem=recv_sems.at[k, a], device_id=to, device_id_type=MESH)

        mine = [pltpu.make_async_copy(x_refs[a], o_refs[a].at[_slot(me)], local_sems.at[a]) for a in range(n)]
        first = [copy(a, 0, me, sibling, True) for a in range(n)]
        first += [copy(a, 1 + j, me, (*chip, c), True) for j, chip in enumerate(chips) for a in range(n)]
        for cp in mine + first:
            cp.start()
        passed = []
        for j, chip in enumerate(chips):
            for a in range(n):
                copy(a, 1 + j, (*chip, c), me).wait_recv()
                fwd = copy(a, 4 + j, (*chip, c), sibling)
                fwd.start()
                passed.append(fwd)
        for a in range(n):
            copy(a, 0, sibling, me).wait_recv()
        for j, chip in enumerate(chips):
            for a in range(n):
                copy(a, 4 + j, (*chip, 1 - c), me).wait_recv()
        for cp in first + passed:
            cp.wait_send()
        for cp in mine:
            cp.wait()

    return _comm_call(body, name, blocks, [jax.ShapeDtypeStruct((N_DEV,) + b.shape, b.dtype) for b in blocks], 7)


def _all_to_all(parts, name):
    n = len(parts)
    rel = [(0, 0, 1), (0, 1, 0), (0, 1, 1), (1, 0, 0), (1, 0, 1), (1, 1, 0), (1, 1, 1)]

    def body(*refs):
        x_refs, o_refs, (send_sems, recv_sems, local_sems) = refs[:n], refs[n:2 * n], refs[2 * n:]
        x, y, c = _coords()
        me = (x, y, c)
        peers = [(x ^ dx, y ^ dy, c ^ dc) for dx, dy, dc in rel]

        def copy(a, k, peer):
            return pltpu.make_async_remote_copy(src_ref=x_refs[a].at[_slot(peer)], dst_ref=o_refs[a].at[_slot(me)], send_sem=send_sems.at[k, a],
                                                recv_sem=recv_sems.at[k, a], device_id=peer, device_id_type=MESH)

        def arrival(a, k, peer):
            return pltpu.make_async_remote_copy(src_ref=x_refs[a].at[_slot(me)], dst_ref=o_refs[a].at[_slot(peer)], send_sem=send_sems.at[k, a],
                                                recv_sem=recv_sems.at[k, a], device_id=peer, device_id_type=MESH)

        mine = [pltpu.make_async_copy(x_refs[a].at[_slot(me)], o_refs[a].at[_slot(me)], local_sems.at[a]) for a in range(n)]
        sends = [copy(a, k, peer) for k, peer in enumerate(peers) for a in range(n)]
        for cp in mine + sends:
            cp.start()
        for k, peer in enumerate(peers):
            for a in range(n):
                arrival(a, k, peer).wait_recv()
        for cp in sends:
            cp.wait_send()
        for cp in mine:
            cp.wait()

    return _comm_call(body, name, parts, [jax.ShapeDtypeStruct(p.shape, p.dtype) for p in parts], 7)


_HBM = pl.BlockSpec(memory_space=pltpu.HBM)
_SEM = pl.BlockSpec(memory_space=pltpu.SEMAPHORE)
_REL = [(0, 0, 1), (0, 1, 0), (0, 1, 1), (1, 0, 0), (1, 0, 1), (1, 1, 0), (1, 1, 1)]


_LINK_ORDER = (3, 1, 5, 4, 2, 6, 0)
SEND_PIECES = 4


def _pieces(shape, dtype):
    rows = shape[0]
    unit = 1 if len(shape) > 2 else (16 if dtype == bf16 else 8)
    for n in (SEND_PIECES, 2):
        if rows % (n * unit) == 0:
            return [pl.ds(i * (rows // n), rows // n) for i in range(n)]
    return [pl.ds(0, rows)]


def _split_copies(gather, src, land, send, recv, pieces):
    x, y, c = _coords()
    me = (x, y, c)
    copies = []
    for a in range(len(src)):
        block = src[a].shape if gather else src[a].shape[1:]
        for rows in (_pieces(block, src[a].dtype) if pieces else [None]):
            for k in _LINK_ORDER:
                dx, dy, dc = _REL[k]
                peer = (x ^ dx, y ^ dy, c ^ dc)
                mine, there = (src[a] if gather else src[a].at[_slot(peer)]), land[a].at[_slot(me)]
                if rows is not None:
                    mine, there = mine.at[rows], there.at[rows]
                copies.append(pltpu.make_async_remote_copy(src_ref=mine, dst_ref=there, send_sem=send[a].at[k], recv_sem=recv[a].at[k],
                                                           device_id=peer, device_id_type=MESH))
    return me, copies


def _arrivals(gather, src, land, send, recv):
    x, y, c = _coords()
    out = []
    for a in range(len(src)):
        for k, (dx, dy, dc) in enumerate(_REL):
            peer = (x ^ dx, y ^ dy, c ^ dc)
            out.append(pltpu.make_async_remote_copy(src_ref=src[a] if gather else src[a].at[_slot(peer)], dst_ref=land[a].at[_slot(peer)],
                                                    send_sem=send[a].at[k], recv_sem=recv[a].at[k], device_id=peer, device_id_type=MESH))
    return out


def _exchange_start(arrays, gather, name, after=None):
    n = len(arrays)
    e = 0 if after is None else 1
    lands = [lax.empty(((N_DEV,) + a.shape) if gather else a.shape, a.dtype) for a in arrays]

    def body(*refs):
        src, land = refs[:n], refs[n:2 * n]
        refs = refs[2 * n + e:]
        send, recv, token, local_sems = refs[:n], refs[n:2 * n], refs[4 * n], refs[4 * n + 1]
        me, out = _split_copies(gather, src, land, send, recv, True)
        local = [pltpu.make_async_copy(src[a] if gather else src[a].at[_slot(me)], land[a].at[_slot(me)], local_sems.at[a])
                 for a in range(n)]
        for cp in local:
            cp.start()
        for cp in local:
            cp.wait()
        for cp in out:
            cp.start()
        token[...] = jnp.zeros_like(token)

    sems = [pltpu.SemaphoreType.DMA((7,)) for _ in range(2 * n)]
    outs = pl.pallas_call(
        body, name=name,
        out_shape=(*sems, *[pltpu.HBM(a.shape, a.dtype) for a in arrays], *[pltpu.HBM(l.shape, l.dtype) for l in lands],
                   jax.ShapeDtypeStruct((8, 128), f32)),
        in_specs=[_HBM] * (2 * n) + [pl.BlockSpec(memory_space=pl.ANY)] * e,
        out_specs=(*[_SEM] * (2 * n), *[_HBM] * (2 * n), pl.BlockSpec(memory_space=pltpu.VMEM)),
        input_output_aliases={i: 2 * n + i for i in range(2 * n)},
        scratch_shapes=[pltpu.SemaphoreType.DMA((n,))],
        compiler_params=pltpu.CompilerParams(has_side_effects=pltpu.SideEffectType.DATAFLOW_SIDE_EFFECTING))(
        *[pltpu.with_memory_space_constraint(a, pltpu.HBM) for a in arrays],
        *[pltpu.with_memory_space_constraint(l, pltpu.HBM) for l in lands], *([after] if e else []))
    return (gather, n, outs[:4 * n]), outs[4 * n]


def _exchange_wait(handle, which, after, name):
    gather, n_all, vals = handle
    send_v, recv_v, src_v, land_v = [[vals[g * n_all + i] for i in which] for g in range(4)]
    n = len(which)

    def body(*refs):
        src, land, send, recv = refs[:n], refs[n:2 * n], refs[2 * n:3 * n], refs[3 * n:4 * n]
        for cp in _split_copies(gather, src, land, send, recv, False)[1]:
            cp.wait_send()
        for cp in _arrivals(gather, src, land, send, recv):
            cp.wait_recv()

    outs = pl.pallas_call(
        body, name=name,
        out_shape=(*[pltpu.HBM(a.shape, a.dtype) for a in src_v], *[pltpu.HBM(l.shape, l.dtype) for l in land_v]),
        in_specs=[*[_HBM] * (2 * n), *[_SEM] * (2 * n), pl.BlockSpec(memory_space=pl.ANY)], out_specs=[_HBM] * (2 * n),
        input_output_aliases={i: i for i in range(2 * n)},
        compiler_params=pltpu.CompilerParams(has_side_effects=pltpu.SideEffectType.DATAFLOW_SIDE_EFFECTING))(
        *src_v, *land_v, *send_v, *recv_v, after)
    return outs[n:]


def _seq_exchange(arrays, gather, name, collective_id):
    n = len(arrays)
    hbm = pltpu.MemorySpace.HBM
    srcs = [jax.new_ref(a, memory_space=hbm) for a in arrays]
    lands = [jax.empty_ref(jax.ShapeDtypeStruct(((N_DEV,) + a.shape) if gather else a.shape, a.dtype), memory_space=hbm) for a in arrays]

    @pl.kernel(mesh=plsc.ScalarSubcoreMesh(axis_name="sequencer", num_cores=1), name=name,
               scratch_types=(pltpu.SemaphoreType.DMA((7, n)), pltpu.SemaphoreType.DMA((7, n)), pltpu.SemaphoreType.DMA((n,))),
               compiler_params=pltpu.CompilerParams(collective_id=collective_id))
    def launch(send, recv, local):
        x, y, c = _coords()
        me = (x, y, c)
        peers = [(x ^ dx, y ^ dy, c ^ dc) for dx, dy, dc in _REL]
        barrier = pltpu.get_barrier_semaphore()
        for peer in peers:
            pl.semaphore_signal(barrier, inc=1, device_id=peer, device_id_type=MESH)
        pl.semaphore_wait(barrier, len(peers))

        def copy(a, k, peer, arrival):
            return pltpu.make_async_remote_copy(
                src_ref=srcs[a] if gather else srcs[a].at[_slot(peer)], dst_ref=lands[a].at[_slot(peer if arrival else me)],
                send_sem=send.at[k, a], recv_sem=recv.at[k, a], device_id=peer, device_id_type=MESH)

        mine = [pltpu.make_async_copy(srcs[a] if gather else srcs[a].at[_slot(me)], lands[a].at[_slot(me)], local.at[a])
                for a in range(n)]
        out = [copy(a, k, peer, False) for a in range(n) for k, peer in enumerate(peers)]
        for cp in mine + out:
            cp.start()
        for a in range(n):
            for k, peer in enumerate(peers):
                copy(a, k, peer, True).wait_recv()
        for cp in out:
            cp.wait_send()
        for cp in mine:
            cp.wait()

    launch()
    return [land[...] for land in lands]


def _adam_math(w, g, m, v):
    m_ = ADAM_B1 * m + (1.0 - ADAM_B1) * g
    v_ = ADAM_B2 * v + (1.0 - ADAM_B2) * jnp.square(g)
    m_hat = m_ / (1.0 - ADAM_B1 ** ADAM_STEP)
    v_hat = v_ / (1.0 - ADAM_B2 ** ADAM_STEP)
    return -ADAM_LR * (m_hat / (jnp.sqrt(v_hat) + ADAM_EPS) + ADAM_WD * w), m_, v_


def _reduce_adamw(parts, w, m, v, name):
    _, R, L = parts.shape
    tr = _pick(R, (256, 128, 64, 32, 16, 8))

    def body(p_ref, w_ref, m_ref, v_ref, g_ref, d_ref, nm_ref, nv_ref):
        g = p_ref[0].astype(f32)
        for i in range(1, N_DEV):
            g = g + p_ref[i].astype(f32)
        g_ref[...] = g
        d_ref[...], nm_ref[...], nv_ref[...] = _adam_math(w_ref[...], g, m_ref[...], v_ref[...])

    blk = pl.BlockSpec((tr, L), lambda i: (i, 0))
    sh = jax.ShapeDtypeStruct((R, L), f32)
    return _call(body, name, (R // tr,), [pl.BlockSpec((N_DEV, tr, L), lambda i: (0, i, 0)), blk, blk, blk], (blk,) * 4, (sh,) * 4,
                 sem=("parallel",))(parts, w, m, v)


SMALL = (("w_spatial", (512, 128), 0), ("norm1_g", (1, 1024), 512), ("mem_norm_g", (1, 1024), 520), ("norm2_g", (1, 1024), 528),
         ("final_g", (1, 1024), 536), ("lb_logits", (2, 512), 544), ("ln_v_g", (1, 512), 552), ("ln_v_b", (1, 512), 556),
         ("b_spatial", (4, 128), 560), ("hgrn_norm_g", (1, 128), 564), ("conv_b", (1, 2816), 565))
LOSS_ROW, SMALL_USED, SMALL_ROWS = 587, 588, 640


def _segments(shape, base):
    r, n = shape
    per = n // 128
    return [(base + i * per + j, i, slice(j * 128, (j + 1) * 128)) for i in range(r) for j in range(per)]


def _pack_small(gs, loss_part):
    names = [n for n, _, _ in SMALL]

    def body(*refs):
        src, loss_ref, o_ref = dict(zip(names, refs[:-2])), refs[-2], refs[-1]
        o_ref[SMALL_USED:SMALL_ROWS, :] = jnp.zeros((SMALL_ROWS - SMALL_USED, 128), f32)
        o_ref[LOSS_ROW:LOSS_ROW + 1, :] = loss_ref[...]
        for name, shape, base in SMALL:
            ref = src[name]
            if name == "w_spatial":
                o_ref[base:base + 512, :] = ref[...].reshape(512, 128)
            elif name == "b_spatial":
                o_ref[base:base + 4, :] = ref[0:4, :]
            elif name == "conv_b":
                per_example = functools.reduce(lambda u, v_: u + v_, [ref[b] for b in range(ref.shape[0])])
                for row, i, sl in _segments(shape, base):
                    o_ref[row:row + 1, :] = per_example[i:i + 1, sl]
            elif name == "hgrn_norm_g":
                per_head = [ref[b, h] for b in range(ref.shape[0]) for h in range(N_HEAD)]
                o_ref[base:base + 1, :] = functools.reduce(lambda u, v_: u + v_, per_head)
            else:
                for row, i, sl in _segments(shape, base):
                    o_ref[row:row + 1, :] = ref[i:i + 1, sl]

    return pl.pallas_call(body, name="pack_small", out_shape=jax.ShapeDtypeStruct((SMALL_ROWS, 128), f32))(
        *[gs[n] for n in names], loss_part)


def _small_update(gathered, w, m, v):
    names = [n for n, _, _ in SMALL]
    k = len(names)

    def body(*refs):
        p_ref = refs[0]
        ins = [dict(zip(names, refs[1 + i * k:1 + (i + 1) * k])) for i in range(3)]
        outs = [dict(zip(names, refs[1 + (3 + i) * k:1 + (4 + i) * k])) for i in range(4)]
        loss_ref, gsum = refs[-2], refs[-1]
        g = p_ref[0]
        for i in range(1, N_DEV):
            g = g + p_ref[i]
        gsum[...] = g
        loss_ref[...] = gsum[LOSS_ROW:LOSS_ROW + 1, :]
        for name, shape, base in SMALL:
            if name == "w_spatial":
                where = [(slice(base, base + 512), (slice(None), slice(None)))]
            else:
                where = [(slice(row, row + 1), (slice(i, i + 1), sl)) for row, i, sl in _segments(shape, base)]
            for rows, at in where:
                g_ = gsum[rows, :]
                d_, m_, v_ = _adam_math(ins[0][name][at], g_, ins[1][name][at], ins[2][name][at])
                for o, val in zip(outs, (g_, d_, m_, v_)):
                    o[name][at] = val

    args = [gathered] + [d[n] for d in (w, m, v) for n in names]
    out_shapes = [jax.ShapeDtypeStruct(shape, f32) for _ in range(4) for _, shape, _ in SMALL] + [jax.ShapeDtypeStruct((1, 128), f32)]
    outs = pl.pallas_call(body, name="small_update", out_shape=out_shapes, scratch_shapes=[pltpu.VMEM((SMALL_ROWS, 128), f32)])(*args)
    return [dict(zip(names, outs[i * k:(i + 1) * k])) for i in range(4)], outs[-1]


def _cols_full(g):
    return jnp.moveaxis(g, 0, -2).reshape(g.shape[1:-1] + (N_DEV * g.shape[-1],))


def _cols_parts(full):
    n = full.shape[-1] // N_DEV
    return jnp.moveaxis(full.reshape(full.shape[:-1] + (N_DEV, n)), -2, 0)


def kernel(x, mem, norm1_g, w_in, ln_v_g, ln_v_b, w_spatial, b_spatial, lb_logits, hgrn_norm_g, mem_norm_g, w_mem_kv, w_branch, w_out, norm2_g, w_up, conv_w, conv_b, w_down, final_g, loss_target, m_norm1_g, m_w_in, m_ln_v_g, m_ln_v_b, m_w_spatial, m_b_spatial, m_lb_logits, m_hgrn_norm_g, m_mem_norm_g, m_w_mem_kv, m_w_branch, m_w_out, m_norm2_g, m_w_up, m_conv_w, m_conv_b, m_w_down, m_final_g, v_norm1_g, v_w_in, v_ln_v_g, v_ln_v_b, v_w_spatial, v_b_spatial, v_lb_logits, v_hgrn_norm_g, v_mem_norm_g, v_w_mem_kv, v_w_branch, v_w_out, v_norm2_g, v_w_up, v_conv_w, v_conv_b, v_w_down, v_final_g):
    given = dict(locals())
    order = ("norm1_g", "w_in", "ln_v_g", "ln_v_b", "w_spatial", "b_spatial", "lb_logits", "hgrn_norm_g", "mem_norm_g",
             "w_mem_kv", "w_branch", "w_out", "norm2_g", "w_up", "conv_w", "conv_b", "w_down", "final_g")
    groups = dict(a=("w_in",), b=("w_mem_kv", "w_branch", "w_out"), c=("w_up", "conv_w", "w_down"))

    wire = {n: given[n][0].astype(f32 if n == "conv_w" else bf16) for ns in groups.values() for n in ns}
    g_in = _all_gather([wire["w_in"]], "gather_w_in")[0]
    late = groups["b"] + groups["c"]
    w_in_full = _cols_full(g_in)
    w_in_full, rest_wire = lax.optimization_barrier((w_in_full, [wire[n] for n in late]))
    rest = _seq_exchange(rest_wire, True, "gather_rest", 1)

    def late_b(after):
        _, (kv_, br_, out_) = lax.optimization_barrier((after, tuple(rest[0:3])))
        br_ = _cols_full(br_)
        return dict(w_mem_kv=kv_.reshape(D_MODEL, 2 * 512), w_branch=[br_[n] for n in range(3)], w_out=out_.reshape(D_MODEL, D_MODEL))

    def late_c(after):
        _, (up_, cw_, down_) = lax.optimization_barrier((after, tuple(rest[3:6])))
        up_ = _cols_full(up_)
        return dict(w_up=up_, w_up_a=up_[:, :D_FF], w_up_b=up_[:, D_FF:], conv_w=_cols_full(cw_), w_down=down_.reshape(D_FF, D_MODEL))

    to_parts = dict(w_in=lambda g_: g_, w_up=lambda g_: g_, conv_w=_cols_parts,
                    w_branch=lambda g_: _cols_parts(jnp.stack(g_)).reshape(N_DEV, -1, 128),
                    w_mem_kv=lambda g_: g_.reshape(N_DEV, -1, 2 * 512), w_out=lambda g_: g_.reshape(N_DEV, -1, D_MODEL),
                    w_down=lambda g_: g_.reshape(N_DEV, -1, D_MODEL))
    scatters = {}

    def send(tag, grads_):
        parts = [to_parts[n](grads_[n]) for n in groups[tag]]
        scatters[tag] = _seq_exchange(parts, False, f"scatter_{tag}", dict(a=2, b=4, c=5)[tag])
        return jnp.zeros((8, 128), f32)

    small_2d = lambda prefix: {n: given[prefix + n].reshape(shape) for n, shape, _ in SMALL}
    p = small_2d("")
    p["w_spatial"] = w_spatial[0]
    updates = {}

    def update(tag):
        for n, parts in zip(groups[tag], scatters[tag]):
            two_d = (-1, given[n].shape[-1])
            updates[n] = _reduce_adamw(parts, *[given[pre + n].reshape(two_d) for pre in ("", "m_", "v_")], "adamw_" + n)

    def settle(chain):
        update("c")
        update("b")
        early = groups["c"] + groups["b"]
        chain, tied = lax.optimization_barrier((chain, [updates[n] for n in early]))
        updates.update(zip(early, tied))
        return chain

    loss_part, grad_x, gs = _local_step(x, mem, loss_target, p, w_in_full, late_b, late_c, send, settle)

    gathered = _seq_exchange([_pack_small(gs, loss_part)], True, "gather_small", 3)[0]

    update("a")
    grads, delta, new_m, new_v = {}, {}, {}, {}
    for n, res in updates.items():
        grads[n], delta[n], new_m[n], new_v[n] = [r.reshape(given[n].shape) for r in res]

    small_results, loss_row = _small_update(gathered, small_2d(""), small_2d("m_"), small_2d("v_"))
    for dst, res in zip((grads, delta, new_m, new_v), small_results):
        for n, _, _ in SMALL:
            dst[n] = res[n].reshape(given[n].shape)
    loss = loss_row[0, 0]

    return (loss, grad_x, *[grads[n] for n in order], *[delta[n] for n in order], *[new_m[n] for n in order],
            *[new_v[n] for n in order])
```

```python
import functools

import jax
import jax.numpy as jnp
from jax import lax
from jax.experimental import pallas as pl
from jax.experimental.pallas import tpu as pltpu
from jax.experimental.pallas import tpu_sc as plsc

f32 = jnp.float32
bf16 = jnp.bfloat16

N_DEV = 8
D_MODEL = 1024
EPS = 1e-6
GM_CHUNK = 128
HG_CHUNK = 64
HEAD = 128
N_HEAD = 4
MEM_LEN = 256
D_FF = 2816
IN_WIDTH = 6656
C_ZU, C_HQ, C_HF, C_HI, C_HG, C_XQ, C_GL = 0, 1024, 1536, 2048, 2560, 3072, 3584
ADAM_LR, ADAM_B1, ADAM_B2, ADAM_EPS, ADAM_WD, ADAM_STEP = 0.001, 0.9, 0.999, 1e-08, 0.01, 10
VMEM_LIMIT = 56 * 1024 * 1024
MESH = pl.DeviceIdType.MESH


def _pick(n, cands):
    for c in cands:
        if n % c == 0:
            return c
    return n


def _call(body, name, grid, in_specs, out_specs, out_shape, scratch=(), sem=None, **cp):
    params = dict(vmem_limit_bytes=VMEM_LIMIT, **cp)
    if sem is not None:
        params["dimension_semantics"] = sem
    return pl.pallas_call(
        body, name=name, grid=grid, in_specs=in_specs, out_specs=out_specs, out_shape=out_shape,
        scratch_shapes=list(scratch), compiler_params=pltpu.CompilerParams(**params))


_DN = {"nn": (((1,), (0,)), ((), ())), "nt": (((1,), (1,)), ((), ())), "tn": (((0,), (0,)), ((), ()))}


def _raw_dot(a, b, mode):
    return lax.dot_general(a.astype(bf16), b.astype(bf16), _DN[mode], preferred_element_type=f32)


@jax.custom_vjp
def _dot_nn(a, b):
    return _raw_dot(a, b, "nn")


_dot_nn.defvjp(lambda a, b: (_raw_dot(a, b, "nn"), (a, b)),
               lambda r, g: (_raw_dot(g, r[1], "nt"), _raw_dot(r[0], g, "tn")))


@jax.custom_vjp
def _dot_nt(a, b):
    return _raw_dot(a, b, "nt")


_dot_nt.defvjp(lambda a, b: (_raw_dot(a, b, "nt"), (a, b)),
               lambda r, g: (_raw_dot(g, r[1], "nn"), _raw_dot(g, r[0], "tn")))


@jax.custom_vjp
def _dot_tn(a, b):
    return _raw_dot(a, b, "tn")


_dot_tn.defvjp(lambda a, b: (_raw_dot(a, b, "tn"), (a, b)),
               lambda r, g: (_raw_dot(r[1], g, "nt"), _raw_dot(r[0], g, "nn")))


def _tri(n, lower):
    r = lax.broadcasted_iota(jnp.int32, (n, n), 0)
    c = lax.broadcasted_iota(jnp.int32, (n, n), 1)
    return ((c <= r) if lower else (c >= r)).astype(f32)


def _sel_dot(sel, x, mode, x_first=False, pieces=3):
    sel = sel.astype(bf16)
    out, rest = None, x
    for p in range(pieces):
        piece = rest.astype(bf16)
        part = lax.dot_general(*((piece, sel) if x_first else (sel, piece)), _DN[mode], preferred_element_type=f32)
        out = part if out is None else out + part
        if p + 1 < pieces:
            rest = rest - piece.astype(f32)
    return out


def _egrad(fn, x, ct):
    return jax.vjp(fn, x)[1](ct)[0]


def _mm(a, b, mode, out_dtype, name, tm, tn, tk=None, residual=None, shard=None, n_outer=False):
    if mode == "nn":
        (M, K), (_, N) = a.shape, b.shape
    elif mode == "nt":
        (M, K), (N, _) = a.shape, b.shape
    else:
        (K, M), (_, N) = a.shape, b.shape
    tm, tn = min(tm, M), min(tn, N)
    tk = K if tk is None else min(tk, K)
    assert M % tm == 0 and N % tn == 0 and K % tk == 0, (name, M, N, K, tm, tn, tk)
    nk = K // tk

    def body(*refs):
        acc_ref = refs[-1] if nk > 1 else None
        refs = refs[:-1] if nk > 1 else refs
        if residual is None:
            a_ref, b_ref, o_ref = refs
        else:
            a_ref, b_ref, r_ref, o_ref = refs

        def finish(r):
            if residual is not None:
                r = r + r_ref[...]
            if shard is None:
                o_ref[...] = r.astype(out_dtype)
            else:
                for s in range(tn // shard):
                    o_ref[s] = r[:, s * shard:(s + 1) * shard].astype(out_dtype)

        part = _raw_dot(a_ref[...], b_ref[...], mode)
        if nk == 1:
            finish(part)
            return
        k = pl.program_id(2)

        @pl.when(k == 0)
        def _():
            acc_ref[...] = part

        @pl.when((k > 0) & (k < nk - 1))
        def _():
            acc_ref[...] += part

        @pl.when(k == nk - 1)
        def _():
            finish(acc_ref[...] + part)

    def at(index):
        return (lambda j, i, k: index(i, j, k)) if n_outer else index

    a_spec = {"nn": pl.BlockSpec((tm, tk), at(lambda i, j, k: (i, k))),
              "nt": pl.BlockSpec((tm, tk), at(lambda i, j, k: (i, k))),
              "tn": pl.BlockSpec((tk, tm), at(lambda i, j, k: (k, i)))}[mode]
    b_spec = {"nn": pl.BlockSpec((tk, tn), at(lambda i, j, k: (k, j))),
              "nt": pl.BlockSpec((tn, tk), at(lambda i, j, k: (j, k))),
              "tn": pl.BlockSpec((tk, tn), at(lambda i, j, k: (k, j)))}[mode]
    o_spec = pl.BlockSpec((tm, tn), at(lambda i, j, k: (i, j)))
    in_specs = [a_spec, b_spec] + ([o_spec] if residual is not None else [])
    args = (a, b) + ((residual,) if residual is not None else ())
    out_shape = jax.ShapeDtypeStruct((M, N), out_dtype)
    if shard is not None:
        assert residual is None and tn % shard == 0
        o_spec = pl.BlockSpec((tn // shard, tm, shard), at(lambda i, j, k: (j, i, 0)))
        out_shape = jax.ShapeDtypeStruct((N // shard, M, shard), out_dtype)
    grid = (N // tn, M // tm, nk) if n_outer else (M // tm, N // tn, nk)
    return _call(body, name, grid, in_specs, o_spec, out_shape,
                 scratch=[pltpu.VMEM((tm, tn), f32)] if nk > 1 else [], sem=("parallel", "parallel", "arbitrary"))(*args)


def _rms_fwd(x, g, name, transposed=False):
    R, Dd = x.shape
    tr = _pick(R, (512, 256, 128))

    def body(x_ref, g_ref, o_ref, *t_ref):
        xf = x_ref[...]
        y = xf * lax.rsqrt(jnp.mean(xf * xf, axis=-1, keepdims=True) + EPS) * g_ref[...]
        o_ref[...] = y.astype(bf16)
        if transposed:
            t_ref[0][...] = y.T.astype(bf16)

    row = pl.BlockSpec((tr, Dd), lambda i: (i, 0))
    out_specs, out_shape = row, jax.ShapeDtypeStruct((R, Dd), bf16)
    if transposed:
        out_specs, out_shape = (row, pl.BlockSpec((Dd, tr), lambda i: (0, i))), (out_shape, jax.ShapeDtypeStruct((Dd, R), bf16))
    return _call(body, name, (R // tr,), [row, pl.BlockSpec((1, Dd), lambda i: (0, 0))], out_specs, out_shape, sem=("parallel",))(x, g)


def _rms_bwd(x, g, dh, name, residual=None):
    R, Dd = x.shape
    tr = _pick(R, (512, 256, 128))

    def body(*refs):
        if residual is None:
            x_ref, g_ref, dh_ref, dx_ref, dg_ref = refs
        else:
            x_ref, g_ref, dh_ref, r_ref, dx_ref, dg_ref = refs
        xf = x_ref[...]
        rs = lax.rsqrt(jnp.mean(xf * xf, axis=-1, keepdims=True) + EPS)
        y = xf * rs
        dh_ = dh_ref[...].astype(f32)
        dy = dh_ * g_ref[...]
        dx = rs * (dy - y * jnp.mean(dy * y, axis=-1, keepdims=True))
        if residual is not None:
            dx = dx + r_ref[...]
        dx_ref[...] = dx

        @pl.when(pl.program_id(0) == 0)
        def _():
            dg_ref[...] = jnp.zeros_like(dg_ref)

        dg_ref[...] += jnp.sum(dh_ * y, axis=0, keepdims=True)

    row = pl.BlockSpec((tr, Dd), lambda i: (i, 0))
    vec = pl.BlockSpec((1, Dd), lambda i: (0, 0))
    in_specs = [row, vec, row] + ([row] if residual is not None else [])
    args = (x, g, dh) + ((residual,) if residual is not None else ())
    return _call(body, name, (R // tr,), in_specs, (row, vec),
                 (jax.ShapeDtypeStruct((R, Dd), f32), jax.ShapeDtypeStruct((1, Dd), f32)), sem=("arbitrary",))(*args)


def _final_loss(x2, g, target):
    R, Dd = x2.shape
    tr = _pick(R, (512, 256, 128))

    def body(x_ref, g_ref, t_ref, loss_ref, dx_ref, dxb_ref, dg_ref):
        xf = x_ref[...]
        rs = lax.rsqrt(jnp.mean(xf * xf, axis=-1, keepdims=True) + EPS)
        y = xf * rs
        err = y * g_ref[...] - t_ref[...]
        dh_ = err * (1.0 / Dd)
        dy = dh_ * g_ref[...]
        dx = rs * (dy - y * jnp.mean(dy * y, axis=-1, keepdims=True))
        dx_ref[...] = dx
        dxb_ref[...] = dx.astype(bf16)

        @pl.when(pl.program_id(0) == 0)
        def _():
            dg_ref[...] = jnp.zeros_like(dg_ref)
            loss_ref[...] = jnp.zeros_like(loss_ref)

        dg_ref[...] += jnp.sum(dh_ * y, axis=0, keepdims=True)
        part = jnp.sum(jnp.mean(err * err, axis=-1, keepdims=True), axis=0, keepdims=True)
        loss_ref[...] += 0.5 * part

    row = pl.BlockSpec((tr, Dd), lambda i: (i, 0))
    vec = pl.BlockSpec((1, Dd), lambda i: (0, 0))
    return _call(body, "final_loss", (R // tr,), [row, vec, row], (pl.BlockSpec((1, 128), lambda i: (0, 0)), row, row, vec),
                 (jax.ShapeDtypeStruct((1, 128), f32), jax.ShapeDtypeStruct((R, Dd), f32), jax.ShapeDtypeStruct((R, Dd), bf16),
                  jax.ShapeDtypeStruct((1, Dd), f32)), sem=("arbitrary",))(x2, g, target)


def _gmlp_parts(zuv, ln_g, ln_b):
    zu, zv = zuv[:, :512], zuv[:, 512:]
    u = jax.nn.gelu(zu)
    v = jax.nn.gelu(zv)
    mu = jnp.mean(v, axis=-1, keepdims=True)
    rs = lax.rsqrt(jnp.mean(jnp.square(v - mu), axis=-1, keepdims=True) + EPS)
    xh = (v - mu) * rs
    return zu, zv, u, xh, rs, xh * ln_g + ln_b


def _gmlp_fwd(proj, ln_g, ln_b, w_s, b_st):
    T = proj.shape[0]

    def body(p_ref, g_ref, b_ref, w_ref, bs_ref, o_ref):
        _, _, u, _, _, vn = _gmlp_parts(p_ref[...].astype(f32), g_ref[...], b_ref[...])
        causal = _tri(GM_CHUNK, True) > 0
        for gi in range(N_HEAD):
            sl = slice(gi * HEAD, (gi + 1) * HEAD)
            w = jnp.where(causal, w_ref[gi], 0.0)
            mixed = _raw_dot(w, vn[:, sl], "nn") + bs_ref[:, gi:gi + 1]
            o_ref[:, sl] = (u[:, sl] * mixed).astype(bf16)

    vec = pl.BlockSpec((1, 512), lambda i: (0, 0))
    return _call(body, "gmlp_fwd", (T // GM_CHUNK,),
                 [pl.BlockSpec((GM_CHUNK, 1024), lambda i: (i, 0)), vec, vec,
                  pl.BlockSpec((N_HEAD, GM_CHUNK, GM_CHUNK), lambda i: (0, 0, 0)), pl.BlockSpec((GM_CHUNK, 128), lambda i: (0, 0))],
                 pl.BlockSpec((GM_CHUNK, 512), lambda i: (i, 0)), jax.ShapeDtypeStruct((T, 512), bf16), sem=("parallel",))(
        proj, ln_g, ln_b, w_s, b_st)


def _gmlp_bwd(proj, ln_g, ln_b, w_s, b_st, da):
    T = proj.shape[0]

    def body(p_ref, g_ref, b_ref, w_ref, bs_ref, da_ref, dp_ref, dg_ref, db_ref, dw_ref, dbs_ref):
        zu, zv, u, xh, rs, vn = _gmlp_parts(p_ref[...].astype(f32), g_ref[...], b_ref[...])
        causal = _tri(GM_CHUNK, True) > 0
        sub = lax.broadcasted_iota(jnp.int32, (8, GM_CHUNK), 0)
        ones = jnp.ones((8, HEAD), f32)
        dout = da_ref[...].astype(f32)

        @pl.when(pl.program_id(0) == 0)
        def _():
            for r in (dg_ref, db_ref, dw_ref, dbs_ref):
                r[...] = jnp.zeros_like(r)

        du, dvn, dbs = [], [], jnp.zeros((8, GM_CHUNK), f32)
        for gi in range(N_HEAD):
            sl = slice(gi * HEAD, (gi + 1) * HEAD)
            w = jnp.where(causal, w_ref[gi], 0.0)
            mixed = _raw_dot(w, vn[:, sl], "nn") + bs_ref[:, gi:gi + 1]
            du.append(dout[:, sl] * mixed)
            dm = dout[:, sl] * u[:, sl]
            row_sums = _sel_dot(ones, dm, "nt")
            dbs = dbs + jnp.where(sub == gi, row_sums, 0.0)
            dw_ref[gi] += jnp.where(causal, _raw_dot(dm, vn[:, sl], "nt"), 0.0)
            dvn.append(_raw_dot(w, dm, "tn"))
        dbs_ref[...] += dbs
        du = jnp.concatenate(du, axis=-1)
        dvn = jnp.concatenate(dvn, axis=-1)
        dg_ref[...] += jnp.sum(dvn * xh, axis=0, keepdims=True)
        db_ref[...] += jnp.sum(dvn, axis=0, keepdims=True)
        dxh = dvn * g_ref[...]
        dv = rs * (dxh - jnp.mean(dxh, axis=-1, keepdims=True) - xh * jnp.mean(dxh * xh, axis=-1, keepdims=True))
        dp_ref[:, :512] = _egrad(jax.nn.gelu, zu, du).astype(bf16)
        dp_ref[:, 512:] = _egrad(jax.nn.gelu, zv, dv).astype(bf16)

    vec = pl.BlockSpec((1, 512), lambda i: (0, 0))
    wsp = pl.BlockSpec((N_HEAD, GM_CHUNK, GM_CHUNK), lambda i: (0, 0, 0))
    return _call(body, "gmlp_bwd", (T // GM_CHUNK,),
                 [pl.BlockSpec((GM_CHUNK, 1024), lambda i: (i, 0)), vec, vec, wsp, pl.BlockSpec((GM_CHUNK, 128), lambda i: (0, 0)),
                  pl.BlockSpec((GM_CHUNK, 512), lambda i: (i, 0))],
                 (pl.BlockSpec((GM_CHUNK, 1024), lambda i: (i, 0)), vec, vec, wsp, pl.BlockSpec((8, GM_CHUNK), lambda i: (0, 0))),
                 (jax.ShapeDtypeStruct((T, 1024), bf16), jax.ShapeDtypeStruct((1, 512), f32), jax.ShapeDtypeStruct((1, 512), f32),
                  jax.ShapeDtypeStruct((N_HEAD, GM_CHUNK, GM_CHUNK), f32), jax.ShapeDtypeStruct((8, GM_CHUNK), f32)),
                 sem=("arbitrary",))(proj, ln_g, ln_b, w_s, b_st, da)


HG_SUB = 8
HG_NSUB = HG_CHUNK // HG_SUB


def _two_level_matrix(transposed=False):
    shape = (HG_CHUNK, 2 * HG_CHUNK) if transposed else (2 * HG_CHUNK, HG_CHUNK)
    r = lax.broadcasted_iota(jnp.int32, shape, 1 if transposed else 0)
    c = lax.broadcasted_iota(jnp.int32, shape, 0 if transposed else 1)
    t = jnp.where(r < HG_CHUNK, r, r - HG_CHUNK)
    local = (r < HG_CHUNK) & (t // HG_SUB == c // HG_SUB) & (c <= t)
    before = (r >= HG_CHUNK) & (c < (t // HG_SUB) * HG_SUB)
    return (local | before).astype(f32)


def _two_level_sums(x):
    two = _sel_dot(_two_level_matrix(), x, "nn")
    return two[:HG_CHUNK], two[HG_CHUNK:]


@jax.custom_vjp
def _two_level_cumsum(x):
    return _two_level_sums(x)


_two_level_cumsum.defvjp(
    lambda x: (_two_level_sums(x), None),
    lambda _, g: (_sel_dot(_two_level_matrix(), jnp.concatenate(g, axis=0), "tn"),))


def _tile_matrix():
    s = lax.broadcasted_iota(jnp.int32, (HG_SUB, HG_CHUNK), 0)
    j = lax.broadcasted_iota(jnp.int32, (HG_SUB, HG_CHUNK), 1)
    return (j % HG_SUB == s).astype(f32)


@jax.custom_vjp
def _tile_lanes(x):
    return _sel_dot(_tile_matrix(), x, "nn", x_first=True, pieces=1)


_tile_lanes.defvjp(
    lambda x: (_sel_dot(_tile_matrix(), x, "nn", x_first=True, pieces=1), None),
    lambda _, g: (_sel_dot(_tile_matrix(), g, "nt", x_first=True, pieces=2),))


def _block_rows(x):
    k = x.shape[-1]
    return jnp.broadcast_to(x.reshape(HG_NSUB, 1, HG_SUB, k), (HG_NSUB, HG_SUB, HG_SUB, k)).reshape(HG_CHUNK, HG_SUB, k)


def _hgrn_chunk(st0, q_raw, f_raw, i_raw, g_raw, l0, l1, ng):
    C, SUB = HG_CHUNK, HG_SUB
    lb = jax.nn.sigmoid(l0 - l1)
    fg = lb + (1.0 - lb) * jax.nn.sigmoid(f_raw)
    kk = 1.0 - fg
    qf = jax.nn.silu(q_raw)
    al, base = _two_level_cumsum(jnp.log(fg))
    a = al + base
    row = lax.broadcasted_iota(jnp.int32, (C, HEAD), 0)
    a_last = jnp.sum(jnp.where(row == C - 1, a, 0.0), axis=0, keepdims=True)
    inter = _dot_nt(qf * jnp.exp(a), st0)
    qt = qf * jnp.exp(al)
    rb = lax.broadcasted_iota(jnp.int32, (C, C), 0) // SUB
    cb = lax.broadcasted_iota(jnp.int32, (C, C), 1) // SUB
    scores = jnp.zeros((C, C), f32)
    for i in range(1, HG_NSUB):
        base_i = jnp.sum(jnp.where(row == i * SUB, base, 0.0), axis=0, keepdims=True)
        kt = kk * jnp.exp(jnp.minimum(base_i - a, 0.0))
        scores = scores + jnp.where((rb == i) & (cb < i), _dot_nt(qt, kt), 0.0)
    t_i = lax.broadcasted_iota(jnp.int32, (C, SUB, HEAD), 0) % SUB
    s_i = lax.broadcasted_iota(jnp.int32, (C, SUB, HEAD), 1)
    decay = jnp.exp(jnp.where(s_i <= t_i, al[:, None, :] - _block_rows(al), -jnp.inf))
    diag = jnp.sum(qf[:, None, :] * decay * _block_rows(kk), axis=-1)
    scores = scores + jnp.where(rb == cb, _tile_lanes(diag), 0.0)
    o = inter + _dot_nn(scores, i_raw)
    st1 = jnp.exp(a_last) * st0 + _dot_tn(i_raw, kk * jnp.exp(a_last - a))
    on = o * lax.rsqrt(jnp.mean(o * o, axis=-1, keepdims=True) + EPS) * ng
    return st1, on * jax.nn.silu(g_raw)


def _hgrn_specs(S, Bl, rev):
    N = S // HG_CHUNK
    chunk = (lambda n: N - 1 - n) if rev else (lambda n: n)
    col = lambda c0: pl.BlockSpec((Bl, HG_CHUNK, 512), lambda n: (0, chunk(n), c0 // 512))
    st = pl.BlockSpec((Bl, N_HEAD, 1, HEAD, HEAD), lambda n: (0, 0, chunk(n), 0, 0))
    full = lambda *s: pl.BlockSpec(s, functools.partial(lambda n, nd: (0,) * nd, nd=len(s)))
    return N, col, st, full


def _hgrn_fwd(proj, lb_logits, ng, Bl, S):
    N, col, st, full = _hgrn_specs(S, Bl, False)

    def body(q_ref, f_ref, i_ref, g_ref, l_ref, ng_ref, o_ref, st_ref, state):
        @pl.when(pl.program_id(0) == 0)
        def _():
            state[...] = jnp.zeros_like(state)

        for b in range(Bl):
            for h in range(N_HEAD):
                sl = slice(h * HEAD, (h + 1) * HEAD)
                st0 = state[b, h]
                st_ref[b, h, 0] = st0
                st1, out = _hgrn_chunk(st0, *[r[b, :, sl].astype(f32) for r in (q_ref, f_ref, i_ref, g_ref)],
                                       l_ref[0:1, sl], l_ref[1:2, sl], ng_ref[...])
                state[b, h] = st1
                o_ref[b, :, sl] = out.astype(bf16)

    return _call(body, "hgrn_fwd", (N,), [col(C_HQ), col(C_HF), col(C_HI), col(C_HG), full(2, 512), full(1, HEAD)],
                 (col(0), st),
                 (jax.ShapeDtypeStruct((Bl, S, 512), bf16), jax.ShapeDtypeStruct((Bl, N_HEAD, N, HEAD, HEAD), f32)),
                 scratch=[pltpu.VMEM((Bl, N_HEAD, HEAD, HEAD), f32)], sem=("arbitrary",))(
        proj, proj, proj, proj, lb_logits, ng)


def _hgrn_bwd(proj, lb_logits, ng, states, db, Bl, S):
    N, col, st, full = _hgrn_specs(S, Bl, True)

    def body(q_ref, f_ref, i_ref, g_ref, l_ref, ng_ref, st_ref, db_ref,
             dq_ref, df_ref, di_ref, dg_ref, dl_ref, dng_ref, dstate):
        @pl.when(pl.program_id(0) == 0)
        def _():
            dstate[...] = jnp.zeros_like(dstate)
            dl_ref[...] = jnp.zeros_like(dl_ref)
            dng_ref[...] = jnp.zeros_like(dng_ref)

        for b in range(Bl):
            for h in range(N_HEAD):
                sl = slice(h * HEAD, (h + 1) * HEAD)
                _, vjp = jax.vjp(_hgrn_chunk, st_ref[b, h, 0], *[r[b, :, sl].astype(f32) for r in (q_ref, f_ref, i_ref, g_ref)],
                                 l_ref[0:1, sl], l_ref[1:2, sl], ng_ref[...])
                dst0, dq, df, di, dg, dl0, dl1, dng = vjp((dstate[b, h], db_ref[b, :, sl].astype(f32)))
                dstate[b, h] = dst0
                dq_ref[b, :, sl] = dq.astype(bf16)
                df_ref[b, :, sl] = df.astype(bf16)
                di_ref[b, :, sl] = di.astype(bf16)
                dg_ref[b, :, sl] = dg.astype(bf16)
                dl_ref[0:1, sl] += dl0
                dl_ref[1:2, sl] += dl1
                dng_ref[b, h] += dng

    return _call(body, "hgrn_bwd", (N,),
                 [col(C_HQ), col(C_HF), col(C_HI), col(C_HG), full(2, 512), full(1, HEAD), st, col(0)],
                 (*[col(0)] * 4, full(2, 512), full(Bl, N_HEAD, 1, HEAD)),
                 (*[jax.ShapeDtypeStruct((Bl, S, 512), bf16)] * 4, jax.ShapeDtypeStruct((2, 512), f32),
                  jax.ShapeDtypeStruct((Bl, N_HEAD, 1, HEAD), f32)),
                 scratch=[pltpu.VMEM((Bl, N_HEAD, HEAD, HEAD), f32)], sem=("arbitrary",))(
        proj, proj, proj, proj, lb_logits, ng, states, db)


def _attn_probs(q, k):
    s = _raw_dot(q, k, "nt") * (HEAD ** -0.5)
    e = jnp.exp(s - jnp.max(s, axis=-1, keepdims=True))
    return e / jnp.sum(e, axis=-1, keepdims=True)


def _attn_specs(S, tq):
    nq = S // tq
    q = pl.BlockSpec((tq, 512), lambda b, i: (b * nq + i, C_XQ // 512))
    kv = pl.BlockSpec((1, MEM_LEN, 1024), lambda b, i: (b, 0, 0))
    o = pl.BlockSpec((tq, 512), lambda b, i: (b * nq + i, 0))
    return nq, q, kv, o


def _attn_fwd(proj, kv, Bl, S):
    tq = _pick(S, (512, 256, 128))
    nq, qs, kvs, os_ = _attn_specs(S, tq)

    def body(q_ref, kv_ref, o_ref):
        for h in range(N_HEAD):
            sl = slice(h * HEAD, (h + 1) * HEAD)
            p = _attn_probs(q_ref[:, sl], kv_ref[0, :, sl])
            o_ref[:, sl] = _raw_dot(p, kv_ref[0, :, 512 + h * HEAD:512 + (h + 1) * HEAD], "nn").astype(bf16)

    return _call(body, "attn_fwd", (Bl, nq), [qs, kvs], os_, jax.ShapeDtypeStruct((Bl * S, 512), bf16),
                 sem=("parallel", "parallel"))(proj, kv)


def _attn_bwd(proj, kv, dc, Bl, S):
    tq = _pick(S, (512, 256, 128))
    nq, qs, kvs, os_ = _attn_specs(S, tq)

    def body(q_ref, kv_ref, do_ref, dq_ref, dkv_ref):
        @pl.when(pl.program_id(1) == 0)
        def _():
            dkv_ref[...] = jnp.zeros_like(dkv_ref)

        for h in range(N_HEAD):
            sl = slice(h * HEAD, (h + 1) * HEAD)
            vsl = slice(512 + h * HEAD, 512 + (h + 1) * HEAD)
            q, k, v, do = q_ref[:, sl], kv_ref[0, :, sl], kv_ref[0, :, vsl], do_ref[:, sl]
            p = _attn_probs(q, k)
            dkv_ref[0, :, vsl] += _raw_dot(p, do, "tn")
            dp = _raw_dot(do, v, "nt")
            ds = p * (dp - jnp.sum(dp * p, axis=-1, keepdims=True)) * (HEAD ** -0.5)
            dq_ref[:, sl] = _raw_dot(ds, k, "nn").astype(bf16)
            dkv_ref[0, :, sl] += _raw_dot(ds, q, "tn")

    return _call(body, "attn_bwd", (Bl, nq), [qs, kvs, os_], (os_, kvs),
                 (jax.ShapeDtypeStruct((Bl * S, 512), bf16), jax.ShapeDtypeStruct((Bl, MEM_LEN, 1024), f32)),
                 sem=("arbitrary", "arbitrary"))(proj, kv, dc)


def _merge_specs(tm, tn):
    br = pl.BlockSpec((tm, 512), lambda i, j: (i, 0))
    w = pl.BlockSpec((512, tn), lambda i, j: (0, j))
    gl = [pl.BlockSpec((tm, tn), functools.partial(lambda i, j, n: (i, (C_GL + n * D_MODEL) // tn + j), n=n)) for n in range(3)]
    return [br, br, br, w, w, w, *gl]


def _merge_fwd(branches, wb, proj):
    T = proj.shape[0]
    tm, tn = _pick(T, (1024, 512, 256, 128)), 512

    def body(a_ref, b_ref, c_ref, w0, w1, w2, g0, g1, g2, o_ref):
        acc = jnp.zeros((tm, tn), f32)
        for x_ref, w_ref, g_ref in ((a_ref, w0, g0), (b_ref, w1, g1), (c_ref, w2, g2)):
            acc = acc + jax.nn.sigmoid(g_ref[...].astype(f32)) * _raw_dot(x_ref[...], w_ref[...], "nn")
        o_ref[...] = acc.astype(bf16)

    return _call(body, "merge_fwd", (T // tm, D_MODEL // tn), _merge_specs(tm, tn), pl.BlockSpec((tm, tn), lambda i, j: (i, j)),
                 jax.ShapeDtypeStruct((T, D_MODEL), bf16), sem=("parallel", "parallel"))(*branches, *wb, proj, proj, proj)


def _merge_bwd(branches, wb, proj, dmerged):
    T = proj.shape[0]
    tm, tn = _pick(T, (1024, 512, 256, 128)), 512

    def body(a_ref, b_ref, c_ref, w0, w1, w2, g0, g1, g2, dm_ref, dgl_ref, d0, d1, d2):
        dm = dm_ref[...]
        for n, (x_ref, w_ref, g_ref, d_ref) in enumerate(((a_ref, w0, g0, d0), (b_ref, w1, g1, d1), (c_ref, w2, g2, d2))):
            up = _raw_dot(x_ref[...], w_ref[...], "nn")
            logits = g_ref[...].astype(f32)
            dgl_ref[n] = _egrad(jax.nn.sigmoid, logits, dm * up).astype(bf16)
            d_ref[...] = (dm * jax.nn.sigmoid(logits)).astype(bf16)

    blk = pl.BlockSpec((tm, tn), lambda i, j: (i, j))
    sh = jax.ShapeDtypeStruct((T, D_MODEL), bf16)
    outs = _call(body, "merge_bwd", (T // tm, D_MODEL // tn), [*_merge_specs(tm, tn), blk],
                 (pl.BlockSpec((3, tm, tn), lambda i, j: (0, i, j)), blk, blk, blk),
                 (jax.ShapeDtypeStruct((3, T, D_MODEL), bf16), sh, sh, sh),
                 sem=("parallel", "parallel"))(*branches, *wb, proj, proj, proj, dmerged)
    return outs[0], outs[1:]


CONV_TC = 256


def _shift_down(a, k):
    r = pltpu.roll(a, k, 0)
    row = lax.broadcasted_iota(jnp.int32, (8, a.shape[1]), 0)
    return jnp.concatenate([jnp.where(row >= k, r[:8], 0.0), r[8:]], axis=0)


def _shift_up(a, k):
    n = a.shape[0]
    r = pltpu.roll(a, n - k, 0)
    row = lax.broadcasted_iota(jnp.int32, (8, a.shape[1]), 0)
    return jnp.concatenate([r[:n - 8], jnp.where(row < 8 - k, r[n - 8:], 0.0)], axis=0)


def _conv_pre(a, a1, a2, cw, cb):
    return cb + cw[0:1] * a2 + cw[1:2] * a1 + cw[2:3] * a


def _up_conv_fwd(h2, w_up, cw, cb):
    Bl, S, Dd = h2.shape
    nc = D_FF // CONV_TC

    def body(h_ref, wa_ref, wb_ref, cw_ref, cb_ref, a_ref, b_ref, o_ref):
        a16 = _raw_dot(h_ref[0], wa_ref[...], "nn").astype(bf16)
        b16 = _raw_dot(h_ref[0], wb_ref[...], "nn").astype(bf16)
        a_ref[0], b_ref[0] = a16, b16
        a = a16.astype(f32)
        ac = _conv_pre(a, _shift_down(a, 1), _shift_down(a, 2), cw_ref[...], cb_ref[...])
        o_ref[0] = (jax.nn.silu(ac) * b16.astype(f32)).astype(bf16)

    seq = pl.BlockSpec((1, S, CONV_TC), lambda b, c: (b, 0, c))
    sh = jax.ShapeDtypeStruct((Bl, S, D_FF), bf16)
    return _call(body, "up_conv_fwd", (Bl, nc),
                 [pl.BlockSpec((1, S, Dd), lambda b, c: (b, 0, 0)), pl.BlockSpec((Dd, CONV_TC), lambda b, c: (0, c)),
                  pl.BlockSpec((Dd, CONV_TC), lambda b, c: (0, nc + c)), pl.BlockSpec((3, CONV_TC), lambda b, c: (0, c)),
                  pl.BlockSpec((1, CONV_TC), lambda b, c: (0, c))],
                 (seq, seq, seq), (sh, sh, sh), sem=("parallel", "parallel"))(h2, w_up, w_up, cw, cb)


def _down_conv_bwd(dx2, w_down, a, b, cw, cb):
    Bl, S, Dd = dx2.shape
    nc = D_FF // CONV_TC

    def body(dx_ref, wd_ref, a_ref, b_ref, cw_ref, cb_ref, da_ref, db_ref, dcw_ref, dcb_ref):
        dact = _raw_dot(dx_ref[0], wd_ref[...], "nt").astype(bf16).astype(f32)
        a, cw = a_ref[0].astype(f32), cw_ref[...]
        a1, a2 = _shift_down(a, 1), _shift_down(a, 2)
        ac = _conv_pre(a, a1, a2, cw, cb_ref[...])
        sg = jax.nn.sigmoid(ac)
        gated = dact * sg
        db_ref[0] = (gated * ac).astype(bf16)
        dac = gated * b_ref[0].astype(f32) * (1.0 + ac * (1.0 - sg))
        da_ref[0] = (cw[2:3] * dac + cw[1:2] * _shift_up(dac, 1) + cw[0:1] * _shift_up(dac, 2)).astype(bf16)
        dcw_ref[0, 0:1, :] = jnp.sum(dac * a2, axis=0, keepdims=True)
        dcw_ref[0, 1:2, :] = jnp.sum(dac * a1, axis=0, keepdims=True)
        dcw_ref[0, 2:3, :] = jnp.sum(dac * a, axis=0, keepdims=True)
        dcb_ref[0] = jnp.sum(dac, axis=0, keepdims=True)

    seq = pl.BlockSpec((1, S, CONV_TC), lambda b_, c: (b_, 0, c))
    sh = jax.ShapeDtypeStruct((Bl, S, D_FF), bf16)
    return _call(body, "down_conv_bwd", (Bl, nc),
                 [pl.BlockSpec((1, S, Dd), lambda b_, c: (b_, 0, 0)), pl.BlockSpec((CONV_TC, Dd), lambda b_, c: (c, 0)), seq, seq,
                  pl.BlockSpec((3, CONV_TC), lambda b_, c: (0, c)), pl.BlockSpec((1, CONV_TC), lambda b_, c: (0, c))],
                 (seq, seq, pl.BlockSpec((1, 3, CONV_TC), lambda b_, c: (b_, 0, c)), pl.BlockSpec((1, 1, CONV_TC), lambda b_, c: (b_, 0, c))),
                 (sh, sh, jax.ShapeDtypeStruct((Bl, 3, D_FF), f32), jax.ShapeDtypeStruct((Bl, 1, D_FF), f32)),
                 sem=("parallel", "parallel"))(dx2, w_down, a, b, cw, cb)


def _local_step(x, mem, target, p, w_in, late_b, late_c, send, settle):
    Bl, S, Dd = x.shape
    T = Bl * S
    x2d, t2d, mem2d = x.reshape(T, Dd), target.reshape(T, Dd), mem.reshape(Bl * MEM_LEN, Dd)
    b_st = jnp.pad(p["b_spatial"].T, ((0, 0), (0, 128 - N_HEAD)))
    lbl = p["lb_logits"]

    h, h_t = _rms_fwd(x2d, p["norm1_g"], "norm1_fwd", transposed=True)
    proj = _mm(h, w_in, "nn", bf16, "proj_fwd", 1024, 1664)
    a_out = _gmlp_fwd(proj, p["ln_v_g"], p["ln_v_b"], p["w_spatial"], b_st)
    proj3 = proj.reshape(Bl, S, IN_WIDTH)
    b_out, states = _hgrn_fwd(proj3, lbl, p["hgrn_norm_g"], Bl, S)
    b_out = b_out.reshape(T, 512)
    memn = _rms_fwd(mem2d, p["mem_norm_g"], "memnorm_fwd")
    w = late_b(b_out)
    wb = w["w_branch"]
    kv = _mm(memn, w["w_mem_kv"], "nn", f32, "kv_fwd", 512, 1024).reshape(Bl, MEM_LEN, 2 * 512)
    c_out = _attn_fwd(proj, kv, Bl, S)
    branches = (a_out, b_out, c_out)
    merged = _merge_fwd(branches, wb, proj)
    x1 = _mm(merged, w["w_out"], "nn", f32, "out_fwd", 1024, 1024, residual=x2d)
    h2, h2_t = _rms_fwd(x1, p["norm2_g"], "norm2_fwd", transposed=True)
    w.update(late_c(h2))
    ffn_a, ffn_b, act = _up_conv_fwd(h2.reshape(Bl, S, Dd), w["w_up"], w["conv_w"], p["conv_b"])
    act = act.reshape(T, D_FF)
    x2 = _mm(act, w["w_down"], "nn", f32, "down_fwd", 512, 1024, residual=x1)
    loss_part, dx2, dx2_16, g_final = _final_loss(x2, p["final_g"], t2d)

    g_w_down = _mm(act, dx2_16, "tn", bf16, "down_dw", 1408, 1024, 1024)
    da, db, g_conv_w, g_conv_b = _down_conv_bwd(dx2_16.reshape(Bl, S, Dd), w["w_down"], ffn_a, ffn_b, w["conv_w"], p["conv_b"])
    da, db = da.reshape(T, D_FF), db.reshape(T, D_FF)
    shard = 2 * D_FF // N_DEV
    g_w_up = jnp.concatenate([_mm(h2_t, d, "nn", bf16, f"up_dw_{n}", 512, 1408, shard=shard, n_outer=True)
                              for n, d in (("a", da), ("b", db))], axis=0)
    tok = send("c", dict(w_up=g_w_up, conv_w=jnp.sum(g_conv_w, axis=0), w_down=g_w_down))
    dh2 = _mm(da, w["w_up_a"], "nt", f32, "up_dx_a", 512, 1024)
    dh2 = _mm(db, w["w_up_b"], "nt", f32, "up_dx_b", 512, 1024, residual=dh2)
    dx1, g_norm2 = _rms_bwd(x1, p["norm2_g"] + tok[0, 0], dh2, "norm2_bwd", residual=dx2)

    g_w_out = _mm(merged, dx1, "tn", bf16, "out_dw", 1024, 1024, 1024)
    dmerged = _mm(dx1, w["w_out"], "nt", f32, "out_dx", 1024, 1024)
    dgl, dup = _merge_bwd(branches, wb, proj, dmerged)
    g_w_branch = [_mm(branches[n], dup[n], "tn", bf16, f"branch_dw{n}", 512, 1024, 1024) for n in range(3)]
    dbr = [_mm(dup[n], wb[n], "nt", bf16, f"branch_dx{n}", 1024, 512) for n in range(3)]
    dxq, dkv = _attn_bwd(proj, kv, dbr[2], Bl, S)
    dkv = dkv.reshape(Bl * MEM_LEN, 2 * 512)
    g_w_kv = _mm(memn, dkv, "tn", bf16, "kv_dw", 1024, 1024, 512)
    tok = send("b", dict(w_mem_kv=g_w_kv, w_branch=g_w_branch, w_out=g_w_out))
    dmemn = _mm(dkv, w["w_mem_kv"], "nt", f32, "kv_dx", 512, 1024)
    _, g_mem_norm = _rms_bwd(mem2d, p["mem_norm_g"], dmemn, "memnorm_bwd")
    dzuv, g_ln_g, g_ln_b, g_w_sp, g_b_sp = _gmlp_bwd(proj, p["ln_v_g"] + tok[0, 0], p["ln_v_b"], p["w_spatial"], b_st, dbr[0])
    *dqfig, g_lbl, g_ng = _hgrn_bwd(proj3, lbl, p["hgrn_norm_g"], states, dbr[1].reshape(Bl, S, 512), Bl, S)
    dq, df, di, dg = [d.reshape(T, 512) for d in dqfig]
    dproj = jnp.concatenate([dzuv, dq, df, di, dg, dxq, dgl[0], dgl[1], dgl[2]], axis=-1)
    g_w_in = _mm(h_t, dproj, "nn", bf16, "proj_dw", 512, 1664, shard=IN_WIDTH // N_DEV, n_outer=True)
    tok = send("a", dict(w_in=g_w_in))
    dh = _mm(settle(dproj), w_in, "nt", f32, "proj_dx", 512, 1024)
    dx, g_norm1 = _rms_bwd(x2d, p["norm1_g"] + tok[0, 0], dh, "norm1_bwd", residual=dx1)

    gs = dict(w_spatial=g_w_sp, norm1_g=g_norm1, mem_norm_g=g_mem_norm, norm2_g=g_norm2, final_g=g_final, lb_logits=g_lbl,
              ln_v_g=g_ln_g, ln_v_b=g_ln_b, b_spatial=g_b_sp, hgrn_norm_g=g_ng, conv_b=g_conv_b)
    return loss_part, dx.reshape(Bl, S, Dd), gs


def _coords():
    return lax.axis_index("x"), lax.axis_index("y"), lax.axis_index("c")


def _slot(dev):
    return 4 * dev[0] + 2 * dev[1] + dev[2]


def _comm_call(body, name, arrays, out_shapes, n_sem):
    n = len(arrays)
    hbm = pl.BlockSpec(memory_space=pl.ANY)
    return pl.pallas_call(
        body, name=name, out_shape=out_shapes, in_specs=[hbm] * n, out_specs=[hbm] * n,
        scratch_shapes=[pltpu.SemaphoreType.DMA((n_sem, n)), pltpu.SemaphoreType.DMA((n_sem, n)), pltpu.SemaphoreType.DMA((n,))])(*arrays)


def _all_gather(blocks, name):
    n = len(blocks)

    def body(*refs):
        x_refs, o_refs, (send_sems, recv_sems, local_sems) = refs[:n], refs[n:2 * n], refs[2 * n:]
        x, y, c = _coords()
        me, sibling = (x, y, c), (x, y, 1 - c)
        chips = [(1 - x, y), (x, 1 - y), (1 - x, 1 - y)]

        def copy(a, k, block_dev, to, from_input=False):
            dst = o_refs[a].at[_slot(block_dev)]
            return pltpu.make_async_remote_copy(src_ref=x_refs[a] if from_input else dst, dst_ref=dst, send_sem=send_sems.at[k, a],
                                                recv_sem=recv_sems.at[k, a], device_id=to, device_id_type=MESH)

        mine = [pltpu.make_async_copy(x_refs[a], o_refs[a].at[_slot(me)], local_sems.at[a]) for a in range(n)]
        first = [copy(a, 0, me, sibling, True) for a in range(n)]
        first += [copy(a, 1 + j, me, (*chip, c), True) for j, chip in enumerate(chips) for a in range(n)]
        for cp in mine + first:
            cp.start()
        passed = []
        for j, chip in enumerate(chips):
            for a in range(n):
                copy(a, 1 + j, (*chip, c), me).wait_recv()
                fwd = copy(a, 4 + j, (*chip, c), sibling)
                fwd.start()
                passed.append(fwd)
        for a in range(n):
            copy(a, 0, sibling, me).wait_recv()
        for j, chip in enumerate(chips):
            for a in range(n):
                copy(a, 4 + j, (*chip, 1 - c), me).wait_recv()
        for cp in first + passed:
            cp.wait_send()
        for cp in mine:
            cp.wait()

    return _comm_call(body, name, blocks, [jax.ShapeDtypeStruct((N_DEV,) + b.shape, b.dtype) for b in blocks], 7)


def _all_to_all(parts, name):
    n = len(parts)
    rel = [(0, 0, 1), (0, 1, 0), (0, 1, 1), (1, 0, 0), (1, 0, 1), (1, 1, 0), (1, 1, 1)]

    def body(*refs):
        x_refs, o_refs, (send_sems, recv_sems, local_sems) = refs[:n], refs[n:2 * n], refs[2 * n:]
        x, y, c = _coords()
        me = (x, y, c)
        peers = [(x ^ dx, y ^ dy, c ^ dc) for dx, dy, dc in rel]

        def copy(a, k, peer):
            return pltpu.make_async_remote_copy(src_ref=x_refs[a].at[_slot(peer)], dst_ref=o_refs[a].at[_slot(me)], send_sem=send_sems.at[k, a],
                                                recv_sem=recv_sems.at[k, a], device_id=peer, device_id_type=MESH)

        def arrival(a, k, peer):
            return pltpu.make_async_remote_copy(src_ref=x_refs[a].at[_slot(me)], dst_ref=o_refs[a].at[_slot(peer)], send_sem=send_sems.at[k, a],
                                                recv_sem=recv_sems.at[k, a], device_id=peer, device_id_type=MESH)

        mine = [pltpu.make_async_copy(x_refs[a].at[_slot(me)], o_refs[a].at[_slot(me)], local_sems.at[a]) for a in range(n)]
        sends = [copy(a, k, peer) for k, peer in enumerate(peers) for a in range(n)]
        for cp in mine + sends:
            cp.start()
        for k, peer in enumerate(peers):
            for a in range(n):
                arrival(a, k, peer).wait_recv()
        for cp in sends:
            cp.wait_send()
        for cp in mine:
            cp.wait()

    return _comm_call(body, name, parts, [jax.ShapeDtypeStruct(p.shape, p.dtype) for p in parts], 7)


_HBM = pl.BlockSpec(memory_space=pltpu.HBM)
_SEM = pl.BlockSpec(memory_space=pltpu.SEMAPHORE)
_REL = [(0, 0, 1), (0, 1, 0), (0, 1, 1), (1, 0, 0), (1, 0, 1), (1, 1, 0), (1, 1, 1)]


_LINK_ORDER = (3, 1, 5, 4, 2, 6, 0)
SEND_PIECES = 4


def _pieces(shape, dtype):
    rows = shape[0]
    unit = 1 if len(shape) > 2 else (16 if dtype == bf16 else 8)
    for n in (SEND_PIECES, 2):
        if rows % (n * unit) == 0:
            return [pl.ds(i * (rows // n), rows // n) for i in range(n)]
    return [pl.ds(0, rows)]


def _split_copies(gather, src, land, send, recv, pieces):
    x, y, c = _coords()
    me = (x, y, c)
    copies = []
    for a in range(len(src)):
        block = src[a].shape if gather else src[a].shape[1:]
        for rows in (_pieces(block, src[a].dtype) if pieces else [None]):
            for k in _LINK_ORDER:
                dx, dy, dc = _REL[k]
                peer = (x ^ dx, y ^ dy, c ^ dc)
                mine, there = (src[a] if gather else src[a].at[_slot(peer)]), land[a].at[_slot(me)]
                if rows is not None:
                    mine, there = mine.at[rows], there.at[rows]
                copies.append(pltpu.make_async_remote_copy(src_ref=mine, dst_ref=there, send_sem=send[a].at[k], recv_sem=recv[a].at[k],
                                                           device_id=peer, device_id_type=MESH))
    return me, copies


def _arrivals(gather, src, land, send, recv):
    x, y, c = _coords()
    out = []
    for a in range(len(src)):
        for k, (dx, dy, dc) in enumerate(_REL):
            peer = (x ^ dx, y ^ dy, c ^ dc)
            out.append(pltpu.make_async_remote_copy(src_ref=src[a] if gather else src[a].at[_slot(peer)], dst_ref=land[a].at[_slot(peer)],
                                                    send_sem=send[a].at[k], recv_sem=recv[a].at[k], device_id=peer, device_id_type=MESH))
    return out


def _exchange_start(arrays, gather, name, after=None):
    n = len(arrays)
    e = 0 if after is None else 1
    lands = [lax.empty(((N_DEV,) + a.shape) if gather else a.shape, a.dtype) for a in arrays]

    def body(*refs):
        src, land = refs[:n], refs[n:2 * n]
        refs = refs[2 * n + e:]
        send, recv, token, local_sems = refs[:n], refs[n:2 * n], refs[4 * n], refs[4 * n + 1]
        me, out = _split_copies(gather, src, land, send, recv, True)
        local = [pltpu.make_async_copy(src[a] if gather else src[a].at[_slot(me)], land[a].at[_slot(me)], local_sems.at[a])
                 for a in range(n)]
        for cp in local:
            cp.start()
        for cp in local:
            cp.wait()
        for cp in out:
            cp.start()
        token[...] = jnp.zeros_like(token)

    sems = [pltpu.SemaphoreType.DMA((7,)) for _ in range(2 * n)]
    outs = pl.pallas_call(
        body, name=name,
        out_shape=(*sems, *[pltpu.HBM(a.shape, a.dtype) for a in arrays], *[pltpu.HBM(l.shape, l.dtype) for l in lands],
                   jax.ShapeDtypeStruct((8, 128), f32)),
        in_specs=[_HBM] * (2 * n) + [pl.BlockSpec(memory_space=pl.ANY)] * e,
        out_specs=(*[_SEM] * (2 * n), *[_HBM] * (2 * n), pl.BlockSpec(memory_space=pltpu.VMEM)),
        input_output_aliases={i: 2 * n + i for i in range(2 * n)},
        scratch_shapes=[pltpu.SemaphoreType.DMA((n,))],
        compiler_params=pltpu.CompilerParams(has_side_effects=pltpu.SideEffectType.DATAFLOW_SIDE_EFFECTING))(
        *[pltpu.with_memory_space_constraint(a, pltpu.HBM) for a in arrays],
        *[pltpu.with_memory_space_constraint(l, pltpu.HBM) for l in lands], *([after] if e else []))
    return (gather, n, outs[:4 * n]), outs[4 * n]


def _exchange_wait(handle, which, after, name):
    gather, n_all, vals = handle
    send_v, recv_v, src_v, land_v = [[vals[g * n_all + i] for i in which] for g in range(4)]
    n = len(which)

    def body(*refs):
        src, land, send, recv = refs[:n], refs[n:2 * n], refs[2 * n:3 * n], refs[3 * n:4 * n]
        for cp in _split_copies(gather, src, land, send, recv, False)[1]:
            cp.wait_send()
        for cp in _arrivals(gather, src, land, send, recv):
            cp.wait_recv()

    outs = pl.pallas_call(
        body, name=name,
        out_shape=(*[pltpu.HBM(a.shape, a.dtype) for a in src_v], *[pltpu.HBM(l.shape, l.dtype) for l in land_v]),
        in_specs=[*[_HBM] * (2 * n), *[_SEM] * (2 * n), pl.BlockSpec(memory_space=pl.ANY)], out_specs=[_HBM] * (2 * n),
        input_output_aliases={i: i for i in range(2 * n)},
        compiler_params=pltpu.CompilerParams(has_side_effects=pltpu.SideEffectType.DATAFLOW_SIDE_EFFECTING))(
        *src_v, *land_v, *send_v, *recv_v, after)
    return outs[n:]


def _seq_exchange(arrays, gather, name, collective_id):
    n = len(arrays)
    hbm = pltpu.MemorySpace.HBM
    srcs = [jax.new_ref(a, memory_space=hbm) for a in arrays]
    lands = [jax.empty_ref(jax.ShapeDtypeStruct(((N_DEV,) + a.shape) if gather else a.shape, a.dtype), memory_space=hbm) for a in arrays]

    @pl.kernel(mesh=plsc.ScalarSubcoreMesh(axis_name="sequencer", num_cores=1), name=name,
               scratch_types=(pltpu.SemaphoreType.DMA((7, n)), pltpu.SemaphoreType.DMA((7, n)), pltpu.SemaphoreType.DMA((n,))),
               compiler_params=pltpu.CompilerParams(collective_id=collective_id))
    def launch(send, recv, local):
        x, y, c = _coords()
        me = (x, y, c)
        peers = [(x ^ dx, y ^ dy, c ^ dc) for dx, dy, dc in _REL]
        barrier = pltpu.get_barrier_semaphore()
        for peer in peers:
            pl.semaphore_signal(barrier, inc=1, device_id=peer, device_id_type=MESH)
        pl.semaphore_wait(barrier, len(peers))

        def copy(a, k, peer, arrival):
            return pltpu.make_async_remote_copy(
                src_ref=srcs[a] if gather else srcs[a].at[_slot(peer)], dst_ref=lands[a].at[_slot(peer if arrival else me)],
                send_sem=send.at[k, a], recv_sem=recv.at[k, a], device_id=peer, device_id_type=MESH)

        mine = [pltpu.make_async_copy(srcs[a] if gather else srcs[a].at[_slot(me)], lands[a].at[_slot(me)], local.at[a])
                for a in range(n)]
        out = [copy(a, k, peer, False) for a in range(n) for k, peer in enumerate(peers)]
        for cp in mine + out:
            cp.start()
        for a in range(n):
            for k, peer in enumerate(peers):
                copy(a, k, peer, True).wait_recv()
        for cp in out:
            cp.wait_send()
        for cp in mine:
            cp.wait()

    launch()
    return [land[...] for land in lands]


def _adam_math(w, g, m, v):
    m_ = ADAM_B1 * m + (1.0 - ADAM_B1) * g
    v_ = ADAM_B2 * v + (1.0 - ADAM_B2) * jnp.square(g)
    m_hat = m_ / (1.0 - ADAM_B1 ** ADAM_STEP)
    v_hat = v_ / (1.0 - ADAM_B2 ** ADAM_STEP)
    return -ADAM_LR * (m_hat / (jnp.sqrt(v_hat) + ADAM_EPS) + ADAM_WD * w), m_, v_


def _reduce_adamw(parts, w, m, v, name):
    _, R, L = parts.shape
    tr = _pick(R, (256, 128, 64, 32, 16, 8))

    def body(p_ref, w_ref, m_ref, v_ref, g_ref, d_ref, nm_ref, nv_ref):
        g = p_ref[0].astype(f32)
        for i in range(1, N_DEV):
            g = g + p_ref[i].astype(f32)
        g_ref[...] = g
        d_ref[...], nm_ref[...], nv_ref[...] = _adam_math(w_ref[...], g, m_ref[...], v_ref[...])

    blk = pl.BlockSpec((tr, L), lambda i: (i, 0))
    sh = jax.ShapeDtypeStruct((R, L), f32)
    return _call(body, name, (R // tr,), [pl.BlockSpec((N_DEV, tr, L), lambda i: (0, i, 0)), blk, blk, blk], (blk,) * 4, (sh,) * 4,
                 sem=("parallel",))(parts, w, m, v)


SMALL = (("w_spatial", (512, 128), 0), ("norm1_g", (1, 1024), 512), ("mem_norm_g", (1, 1024), 520), ("norm2_g", (1, 1024), 528),
         ("final_g", (1, 1024), 536), ("lb_logits", (2, 512), 544), ("ln_v_g", (1, 512), 552), ("ln_v_b", (1, 512), 556),
         ("b_spatial", (4, 128), 560), ("hgrn_norm_g", (1, 128), 564), ("conv_b", (1, 2816), 565))
LOSS_ROW, SMALL_USED, SMALL_ROWS = 587, 588, 640


def _segments(shape, base):
    r, n = shape
    per = n // 128
    return [(base + i * per + j, i, slice(j * 128, (j + 1) * 128)) for i in range(r) for j in range(per)]


def _pack_small(gs, loss_part):
    names = [n for n, _, _ in SMALL]

    def body(*refs):
        src, loss_ref, o_ref = dict(zip(names, refs[:-2])), refs[-2], refs[-1]
        o_ref[SMALL_USED:SMALL_ROWS, :] = jnp.zeros((SMALL_ROWS - SMALL_USED, 128), f32)
        o_ref[LOSS_ROW:LOSS_ROW + 1, :] = loss_ref[...]
        for name, shape, base in SMALL:
            ref = src[name]
            if name == "w_spatial":
                o_ref[base:base + 512, :] = ref[...].reshape(512, 128)
            elif name == "b_spatial":
                o_ref[base:base + 4, :] = ref[0:4, :]
            elif name == "conv_b":
                per_example = functools.reduce(lambda u, v_: u + v_, [ref[b] for b in range(ref.shape[0])])
                for row, i, sl in _segments(shape, base):
                    o_ref[row:row + 1, :] = per_example[i:i + 1, sl]
            elif name == "hgrn_norm_g":
                per_head = [ref[b, h] for b in range(ref.shape[0]) for h in range(N_HEAD)]
                o_ref[base:base + 1, :] = functools.reduce(lambda u, v_: u + v_, per_head)
            else:
                for row, i, sl in _segments(shape, base):
                    o_ref[row:row + 1, :] = ref[i:i + 1, sl]

    return pl.pallas_call(body, name="pack_small", out_shape=jax.ShapeDtypeStruct((SMALL_ROWS, 128), f32))(
        *[gs[n] for n in names], loss_part)


def _small_update(gathered, w, m, v):
    names = [n for n, _, _ in SMALL]
    k = len(names)

    def body(*refs):
        p_ref = refs[0]
        ins = [dict(zip(names, refs[1 + i * k:1 + (i + 1) * k])) for i in range(3)]
        outs = [dict(zip(names, refs[1 + (3 + i) * k:1 + (4 + i) * k])) for i in range(4)]
        loss_ref, gsum = refs[-2], refs[-1]
        g = p_ref[0]
        for i in range(1, N_DEV):
            g = g + p_ref[i]
        gsum[...] = g
        loss_ref[...] = gsum[LOSS_ROW:LOSS_ROW + 1, :]
        for name, shape, base in SMALL:
            if name == "w_spatial":
                where = [(slice(base, base + 512), (slice(None), slice(None)))]
            else:
                where = [(slice(row, row + 1), (slice(i, i + 1), sl)) for row, i, sl in _segments(shape, base)]
            for rows, at in where:
                g_ = gsum[rows, :]
                d_, m_, v_ = _adam_math(ins[0][name][at], g_, ins[1][name][at], ins[2][name][at])
                for o, val in zip(outs, (g_, d_, m_, v_)):
                    o[name][at] = val

    args = [gathered] + [d[n] for d in (w, m, v) for n in names]
    out_shapes = [jax.ShapeDtypeStruct(shape, f32) for _ in range(4) for _, shape, _ in SMALL] + [jax.ShapeDtypeStruct((1, 128), f32)]
    outs = pl.pallas_call(body, name="small_update", out_shape=out_shapes, scratch_shapes=[pltpu.VMEM((SMALL_ROWS, 128), f32)])(*args)
    return [dict(zip(names, outs[i * k:(i + 1) * k])) for i in range(4)], outs[-1]


def _cols_full(g):
    return jnp.moveaxis(g, 0, -2).reshape(g.shape[1:-1] + (N_DEV * g.shape[-1],))


def _cols_parts(full):
    n = full.shape[-1] // N_DEV
    return jnp.moveaxis(full.reshape(full.shape[:-1] + (N_DEV, n)), -2, 0)


def kernel(x, mem, norm1_g, w_in, ln_v_g, ln_v_b, w_spatial, b_spatial, lb_logits, hgrn_norm_g, mem_norm_g, w_mem_kv, w_branch, w_out, norm2_g, w_up, conv_w, conv_b, w_down, final_g, loss_target, m_norm1_g, m_w_in, m_ln_v_g, m_ln_v_b, m_w_spatial, m_b_spatial, m_lb_logits, m_hgrn_norm_g, m_mem_norm_g, m_w_mem_kv, m_w_branch, m_w_out, m_norm2_g, m_w_up, m_conv_w, m_conv_b, m_w_down, m_final_g, v_norm1_g, v_w_in, v_ln_v_g, v_ln_v_b, v_w_spatial, v_b_spatial, v_lb_logits, v_hgrn_norm_g, v_mem_norm_g, v_w_mem_kv, v_w_branch, v_w_out, v_norm2_g, v_w_up, v_conv_w, v_conv_b, v_w_down, v_final_g):
    given = dict(locals())
    order = ("norm1_g", "w_in", "ln_v_g", "ln_v_b", "w_spatial", "b_spatial", "lb_logits", "hgrn_norm_g", "mem_norm_g",
             "w_mem_kv", "w_branch", "w_out", "norm2_g", "w_up", "conv_w", "conv_b", "w_down", "final_g")
    groups = dict(a=("w_in",), b=("w_mem_kv", "w_branch", "w_out"), c=("w_up", "conv_w", "w_down"))

    wire = {n: given[n][0].astype(f32 if n == "conv_w" else bf16) for ns in groups.values() for n in ns}
    g_in = _all_gather([wire["w_in"]], "gather_w_in")[0]
    late = groups["b"] + groups["c"]
    w_in_full = _cols_full(g_in)
    w_in_full, rest_wire = lax.optimization_barrier((w_in_full, [wire[n] for n in late]))
    rest = _seq_exchange(rest_wire, True, "gather_rest", 1)

    def late_b(after):
        _, (kv_, br_, out_) = lax.optimization_barrier((after, tuple(rest[0:3])))
        br_ = _cols_full(br_)
        return dict(w_mem_kv=kv_.reshape(D_MODEL, 2 * 512), w_branch=[br_[n] for n in range(3)], w_out=out_.reshape(D_MODEL, D_MODEL))

    def late_c(after):
        _, (up_, cw_, down_) = lax.optimization_barrier((after, tuple(rest[3:6])))
        up_ = _cols_full(up_)
        return dict(w_up=up_, w_up_a=up_[:, :D_FF], w_up_b=up_[:, D_FF:], conv_w=_cols_full(cw_), w_down=down_.reshape(D_FF, D_MODEL))

    to_parts = dict(w_in=lambda g_: g_, w_up=lambda g_: g_, conv_w=_cols_parts,
                    w_branch=lambda g_: _cols_parts(jnp.stack(g_)).reshape(N_DEV, -1, 128),
                    w_mem_kv=lambda g_: g_.reshape(N_DEV, -1, 2 * 512), w_out=lambda g_: g_.reshape(N_DEV, -1, D_MODEL),
                    w_down=lambda g_: g_.reshape(N_DEV, -1, D_MODEL))
    scatters = {}

    def send(tag, grads_):
        parts = [to_parts[n](grads_[n]) for n in groups[tag]]
        scatters[tag] = _seq_exchange(parts, False, f"scatter_{tag}", dict(a=2, b=4, c=5)[tag])
        return jnp.zeros((8, 128), f32)

    small_2d = lambda prefix: {n: given[prefix + n].reshape(shape) for n, shape, _ in SMALL}
    p = small_2d("")
    p["w_spatial"] = w_spatial[0]
    updates = {}

    def update(tag):
        for n, parts in zip(groups[tag], scatters[tag]):
            two_d = (-1, given[n].shape[-1])
            updates[n] = _reduce_adamw(parts, *[given[pre + n].reshape(two_d) for pre in ("", "m_", "v_")], "adamw_" + n)

    def settle(chain):
        update("c")
        update("b")
        early = groups["c"] + groups["b"]
        chain, tied = lax.optimization_barrier((chain, [updates[n] for n in early]))
        updates.update(zip(early, tied))
        return chain

    loss_part, grad_x, gs = _local_step(x, mem, loss_target, p, w_in_full, late_b, late_c, send, settle)

    gathered = _seq_exchange([_pack_small(gs, loss_part)], True, "gather_small", 3)[0]

    update("a")
    grads, delta, new_m, new_v = {}, {}, {}, {}
    for n, res in updates.items():
        grads[n], delta[n], new_m[n], new_v[n] = [r.reshape(given[n].shape) for r in res]

    small_results, loss_row = _small_update(gathered, small_2d(""), small_2d("m_"), small_2d("v_"))
    for dst, res in zip((grads, delta, new_m, new_v), small_results):
        for n, _, _ in SMALL:
            dst[n] = res[n].reshape(given[n].shape)
    loss = loss_row[0, 0]

    return (loss, grad_x, *[grads[n] for n in order], *[delta[n] for n in order], *[new_m[n] for n in order],
            *[new_v[n] for n in order])
```

```python
import functools

import jax
import jax.numpy as jnp
from jax import lax
from jax.experimental import pallas as pl
from jax.experimental.pallas import tpu as pltpu
from jax.experimental.pallas import tpu_sc as plsc

f32 = jnp.float32
bf16 = jnp.bfloat16

N_DEV = 8
D_MODEL = 1024
EPS = 1e-6
GM_CHUNK = 128
HG_CHUNK = 64
HEAD = 128
N_HEAD = 4
MEM_LEN = 256
D_FF = 2816
IN_WIDTH = 6656
C_ZU, C_HQ, C_HF, C_HI, C_HG, C_XQ, C_GL = 0, 1024, 1536, 2048, 2560, 3072, 3584
ADAM_LR, ADAM_B1, ADAM_B2, ADAM_EPS, ADAM_WD, ADAM_STEP = 0.001, 0.9, 0.999, 1e-08, 0.01, 10
VMEM_LIMIT = 56 * 1024 * 1024
MESH = pl.DeviceIdType.MESH


def _pick(n, cands):
    for c in cands:
        if n % c == 0:
            return c
    return n


def _call(body, name, grid, in_specs, out_specs, out_shape, scratch=(), sem=None, **cp):
    params = dict(vmem_limit_bytes=VMEM_LIMIT, **cp)
    if sem is not None:
        params["dimension_semantics"] = sem
    return pl.pallas_call(
        body, name=name, grid=grid, in_specs=in_specs, out_specs=out_specs, out_shape=out_shape,
        scratch_shapes=list(scratch), compiler_params=pltpu.CompilerParams(**params))


_DN = {"nn": (((1,), (0,)), ((), ())), "nt": (((1,), (1,)), ((), ())), "tn": (((0,), (0,)), ((), ()))}


def _raw_dot(a, b, mode):
    return lax.dot_general(a.astype(bf16), b.astype(bf16), _DN[mode], preferred_element_type=f32)


@jax.custom_vjp
def _dot_nn(a, b):
    return _raw_dot(a, b, "nn")


_dot_nn.defvjp(lambda a, b: (_raw_dot(a, b, "nn"), (a, b)),
               lambda r, g: (_raw_dot(g, r[1], "nt"), _raw_dot(r[0], g, "tn")))


@jax.custom_vjp
def _dot_nt(a, b):
    return _raw_dot(a, b, "nt")


_dot_nt.defvjp(lambda a, b: (_raw_dot(a, b, "nt"), (a, b)),
               lambda r, g: (_raw_dot(g, r[1], "nn"), _raw_dot(g, r[0], "tn")))


@jax.custom_vjp
def _dot_tn(a, b):
    return _raw_dot(a, b, "tn")


_dot_tn.defvjp(lambda a, b: (_raw_dot(a, b, "tn"), (a, b)),
               lambda r, g: (_raw_dot(r[1], g, "nt"), _raw_dot(r[0], g, "nn")))


def _tri(n, lower):
    r = lax.broadcasted_iota(jnp.int32, (n, n), 0)
    c = lax.broadcasted_iota(jnp.int32, (n, n), 1)
    return ((c <= r) if lower else (c >= r)).astype(f32)


def _sel_dot(sel, x, mode, x_first=False, pieces=3):
    sel = sel.astype(bf16)
    out, rest = None, x
    for p in range(pieces):
        piece = rest.astype(bf16)
        part = lax.dot_general(*((piece, sel) if x_first else (sel, piece)), _DN[mode], preferred_element_type=f32)
        out = part if out is None else out + part
        if p + 1 < pieces:
            rest = rest - piece.astype(f32)
    return out


def _egrad(fn, x, ct):
    return jax.vjp(fn, x)[1](ct)[0]


def _mm(a, b, mode, out_dtype, name, tm, tn, tk=None, residual=None, shard=None, n_outer=False):
    if mode == "nn":
        (M, K), (_, N) = a.shape, b.shape
    elif mode == "nt":
        (M, K), (N, _) = a.shape, b.shape
    else:
        (K, M), (_, N) = a.shape, b.shape
    tm, tn = min(tm, M), min(tn, N)
    tk = K if tk is None else min(tk, K)
    assert M % tm == 0 and N % tn == 0 and K % tk == 0, (name, M, N, K, tm, tn, tk)
    nk = K // tk

    def body(*refs):
        acc_ref = refs[-1] if nk > 1 else None
        refs = refs[:-1] if nk > 1 else refs
        if residual is None:
            a_ref, b_ref, o_ref = refs
        else:
            a_ref, b_ref, r_ref, o_ref = refs

        def finish(r):
            if residual is not None:
                r = r + r_ref[...]
            if shard is None:
                o_ref[...] = r.astype(out_dtype)
            else:
                for s in range(tn // shard):
                    o_ref[s] = r[:, s * shard:(s + 1) * shard].astype(out_dtype)

        part = _raw_dot(a_ref[...], b_ref[...], mode)
        if nk == 1:
            finish(part)
            return
        k = pl.program_id(2)

        @pl.when(k == 0)
        def _():
            acc_ref[...] = part

        @pl.when((k > 0) & (k < nk - 1))
        def _():
            acc_ref[...] += part

        @pl.when(k == nk - 1)
        def _():
            finish(acc_ref[...] + part)

    def at(index):
        return (lambda j, i, k: index(i, j, k)) if n_outer else index

    a_spec = {"nn": pl.BlockSpec((tm, tk), at(lambda i, j, k: (i, k))),
              "nt": pl.BlockSpec((tm, tk), at(lambda i, j, k: (i, k))),
              "tn": pl.BlockSpec((tk, tm), at(lambda i, j, k: (k, i)))}[mode]
    b_spec = {"nn": pl.BlockSpec((tk, tn), at(lambda i, j, k: (k, j))),
              "nt": pl.BlockSpec((tn, tk), at(lambda i, j, k: (j, k))),
              "tn": pl.BlockSpec((tk, tn), at(lambda i, j, k: (k, j)))}[mode]
    o_spec = pl.BlockSpec((tm, tn), at(lambda i, j, k: (i, j)))
    in_specs = [a_spec, b_spec] + ([o_spec] if residual is not None else [])
    args = (a, b) + ((residual,) if residual is not None else ())
    out_shape = jax.ShapeDtypeStruct((M, N), out_dtype)
    if shard is not None:
        assert residual is None and tn % shard == 0
        o_spec = pl.BlockSpec((tn // shard, tm, shard), at(lambda i, j, k: (j, i, 0)))
        out_shape = jax.ShapeDtypeStruct((N // shard, M, shard), out_dtype)
    grid = (N // tn, M // tm, nk) if n_outer else (M // tm, N // tn, nk)
    return _call(body, name, grid, in_specs, o_spec, out_shape,
                 scratch=[pltpu.VMEM((tm, tn), f32)] if nk > 1 else [], sem=("parallel", "parallel", "arbitrary"))(*args)


def _rms_fwd(x, g, name, transposed=False):
    R, Dd = x.shape
    tr = _pick(R, (512, 256, 128))

    def body(x_ref, g_ref, o_ref, *t_ref):
        xf = x_ref[...]
        y = xf * lax.rsqrt(jnp.mean(xf * xf, axis=-1, keepdims=True) + EPS) * g_ref[...]
        o_ref[...] = y.astype(bf16)
        if transposed:
            t_ref[0][...] = y.T.astype(bf16)

    row = pl.BlockSpec((tr, Dd), lambda i: (i, 0))
    out_specs, out_shape = row, jax.ShapeDtypeStruct((R, Dd), bf16)
    if transposed:
        out_specs, out_shape = (row, pl.BlockSpec((Dd, tr), lambda i: (0, i))), (out_shape, jax.ShapeDtypeStruct((Dd, R), bf16))
    return _call(body, name, (R // tr,), [row, pl.BlockSpec((1, Dd), lambda i: (0, 0))], out_specs, out_shape, sem=("parallel",))(x, g)


def _rms_bwd(x, g, dh, name, residual=None):
    R, Dd = x.shape
    tr = _pick(R, (512, 256, 128))

    def body(*refs):
        if residual is None:
            x_ref, g_ref, dh_ref, dx_ref, dg_ref = refs
        else:
            x_ref, g_ref, dh_ref, r_ref, dx_ref, dg_ref = refs
        xf = x_ref[...]
        rs = lax.rsqrt(jnp.mean(xf * xf, axis=-1, keepdims=True) + EPS)
        y = xf * rs
        dh_ = dh_ref[...].astype(f32)
        dy = dh_ * g_ref[...]
        dx = rs * (dy - y * jnp.mean(dy * y, axis=-1, keepdims=True))
        if residual is not None:
            dx = dx + r_ref[...]
        dx_ref[...] = dx

        @pl.when(pl.program_id(0) == 0)
        def _():
            dg_ref[...] = jnp.zeros_like(dg_ref)

        dg_ref[...] += jnp.sum(dh_ * y, axis=0, keepdims=True)

    row = pl.BlockSpec((tr, Dd), lambda i: (i, 0))
    vec = pl.BlockSpec((1, Dd), lambda i: (0, 0))
    in_specs = [row, vec, row] + ([row] if residual is not None else [])
    args = (x, g, dh) + ((residual,) if residual is not None else ())
    return _call(body, name, (R // tr,), in_specs, (row, vec),
                 (jax.ShapeDtypeStruct((R, Dd), f32), jax.ShapeDtypeStruct((1, Dd), f32)), sem=("arbitrary",))(*args)


def _final_loss(x2, g, target):
    R, Dd = x2.shape
    tr = _pick(R, (512, 256, 128))

    def body(x_ref, g_ref, t_ref, loss_ref, dx_ref, dxb_ref, dg_ref):
        xf = x_ref[...]
        rs = lax.rsqrt(jnp.mean(xf * xf, axis=-1, keepdims=True) + EPS)
        y = xf * rs
        err = y * g_ref[...] - t_ref[...]
        dh_ = err * (1.0 / Dd)
        dy = dh_ * g_ref[...]
        dx = rs * (dy - y * jnp.mean(dy * y, axis=-1, keepdims=True))
        dx_ref[...] = dx
        dxb_ref[...] = dx.astype(bf16)

        @pl.when(pl.program_id(0) == 0)
        def _():
            dg_ref[...] = jnp.zeros_like(dg_ref)
            loss_ref[...] = jnp.zeros_like(loss_ref)

        dg_ref[...] += jnp.sum(dh_ * y, axis=0, keepdims=True)
        part = jnp.sum(jnp.mean(err * err, axis=-1, keepdims=True), axis=0, keepdims=True)
        loss_ref[...] += 0.5 * part

    row = pl.BlockSpec((tr, Dd), lambda i: (i, 0))
    vec = pl.BlockSpec((1, Dd), lambda i: (0, 0))
    return _call(body, "final_loss", (R // tr,), [row, vec, row], (pl.BlockSpec((1, 128), lambda i: (0, 0)), row, row, vec),
                 (jax.ShapeDtypeStruct((1, 128), f32), jax.ShapeDtypeStruct((R, Dd), f32), jax.ShapeDtypeStruct((R, Dd), bf16),
                  jax.ShapeDtypeStruct((1, Dd), f32)), sem=("arbitrary",))(x2, g, target)


def _gmlp_parts(zuv, ln_g, ln_b):
    zu, zv = zuv[:, :512], zuv[:, 512:]
    u = jax.nn.gelu(zu)
    v = jax.nn.gelu(zv)
    mu = jnp.mean(v, axis=-1, keepdims=True)
    rs = lax.rsqrt(jnp.mean(jnp.square(v - mu), axis=-1, keepdims=True) + EPS)
    xh = (v - mu) * rs
    return zu, zv, u, xh, rs, xh * ln_g + ln_b


GM_TILE_CHUNKS = 4


def _gmlp_tile(T):
    n = _pick(T // GM_CHUNK, (GM_TILE_CHUNKS, 2, 1))
    return n, n * GM_CHUNK


def _gmlp_fwd(proj, ln_g, ln_b, w_s, b_st):
    T = proj.shape[0]
    nch, rows = _gmlp_tile(T)

    def body(p_ref, g_ref, b_ref, w_ref, bs_ref, o_ref):
        _, _, u, _, _, vn = _gmlp_parts(p_ref[...].astype(f32), g_ref[...], b_ref[...])
        causal = _tri(GM_CHUNK, True) > 0
        for gi in range(N_HEAD):
            sl = slice(gi * HEAD, (gi + 1) * HEAD)
            w = jnp.where(causal, w_ref[gi], 0.0)
            for ch in range(nch):
                rs_ = slice(ch * GM_CHUNK, (ch + 1) * GM_CHUNK)
                mixed = _raw_dot(w, vn[rs_, sl], "nn") + bs_ref[:, gi:gi + 1]
                o_ref[rs_, sl] = (u[rs_, sl] * mixed).astype(bf16)

    vec = pl.BlockSpec((1, 512), lambda i: (0, 0))
    return _call(body, "gmlp_fwd", (T // rows,),
                 [pl.BlockSpec((rows, 1024), lambda i: (i, 0)), vec, vec,
                  pl.BlockSpec((N_HEAD, GM_CHUNK, GM_CHUNK), lambda i: (0, 0, 0)), pl.BlockSpec((GM_CHUNK, 128), lambda i: (0, 0))],
                 pl.BlockSpec((rows, 512), lambda i: (i, 0)), jax.ShapeDtypeStruct((T, 512), bf16), sem=("parallel",))(
        proj, ln_g, ln_b, w_s, b_st)


def _gmlp_bwd(proj, ln_g, ln_b, w_s, b_st, da):
    T = proj.shape[0]
    nch, rows = _gmlp_tile(T)

    def body(p_ref, g_ref, b_ref, w_ref, bs_ref, da_ref, dp_ref, dg_ref, db_ref, dw_ref, dbs_ref):
        zu, zv, u, xh, rs, vn = _gmlp_parts(p_ref[...].astype(f32), g_ref[...], b_ref[...])
        causal = _tri(GM_CHUNK, True) > 0
        sub = lax.broadcasted_iota(jnp.int32, (8, GM_CHUNK), 0)
        ones = jnp.ones((8, HEAD), f32)
        dout = da_ref[...].astype(f32)

        @pl.when(pl.program_id(0) == 0)
        def _():
            for r in (dg_ref, db_ref, dw_ref, dbs_ref):
                r[...] = jnp.zeros_like(r)

        du, dvn, dbs = [], [], jnp.zeros((8, GM_CHUNK), f32)
        for gi in range(N_HEAD):
            sl = slice(gi * HEAD, (gi + 1) * HEAD)
            w = jnp.where(causal, w_ref[gi], 0.0)
            du_g, dvn_g, dw_g = [], [], jnp.zeros((GM_CHUNK, GM_CHUNK), f32)
            for ch in range(nch):
                rs_ = slice(ch * GM_CHUNK, (ch + 1) * GM_CHUNK)
                mixed = _raw_dot(w, vn[rs_, sl], "nn") + bs_ref[:, gi:gi + 1]
                du_g.append(dout[rs_, sl] * mixed)
                dm = dout[rs_, sl] * u[rs_, sl]
                dbs = dbs + jnp.where(sub == gi, _sel_dot(ones, dm, "nt"), 0.0)
                dw_g = dw_g + _raw_dot(dm, vn[rs_, sl], "nt")
                dvn_g.append(_raw_dot(w, dm, "tn"))
            dw_ref[gi] += jnp.where(causal, dw_g, 0.0)
            du.append(jnp.concatenate(du_g, axis=0))
            dvn.append(jnp.concatenate(dvn_g, axis=0))
        dbs_ref[...] += dbs
        du = jnp.concatenate(du, axis=-1)
        dvn = jnp.concatenate(dvn, axis=-1)
        dg_ref[...] += jnp.sum(dvn * xh, axis=0, keepdims=True)
        db_ref[...] += jnp.sum(dvn, axis=0, keepdims=True)
        dxh = dvn * g_ref[...]
        dv = rs * (dxh - jnp.mean(dxh, axis=-1, keepdims=True) - xh * jnp.mean(dxh * xh, axis=-1, keepdims=True))
        dp_ref[:, :512] = _egrad(jax.nn.gelu, zu, du).astype(bf16)
        dp_ref[:, 512:] = _egrad(jax.nn.gelu, zv, dv).astype(bf16)

    vec = pl.BlockSpec((1, 512), lambda i: (0, 0))
    wsp = pl.BlockSpec((N_HEAD, GM_CHUNK, GM_CHUNK), lambda i: (0, 0, 0))
    return _call(body, "gmlp_bwd", (T // rows,),
                 [pl.BlockSpec((rows, 1024), lambda i: (i, 0)), vec, vec, wsp, pl.BlockSpec((GM_CHUNK, 128), lambda i: (0, 0)),
                  pl.BlockSpec((rows, 512), lambda i: (i, 0))],
                 (pl.BlockSpec((rows, 1024), lambda i: (i, 0)), vec, vec, wsp, pl.BlockSpec((8, GM_CHUNK), lambda i: (0, 0))),
                 (jax.ShapeDtypeStruct((T, 1024), bf16), jax.ShapeDtypeStruct((1, 512), f32), jax.ShapeDtypeStruct((1, 512), f32),
                  jax.ShapeDtypeStruct((N_HEAD, GM_CHUNK, GM_CHUNK), f32), jax.ShapeDtypeStruct((8, GM_CHUNK), f32)),
                 sem=("arbitrary",))(proj, ln_g, ln_b, w_s, b_st, da)


HG_SUB = 8
HG_NSUB = HG_CHUNK // HG_SUB


def _two_level_matrix(transposed=False):
    shape = (HG_CHUNK, 2 * HG_CHUNK) if transposed else (2 * HG_CHUNK, HG_CHUNK)
    r = lax.broadcasted_iota(jnp.int32, shape, 1 if transposed else 0)
    c = lax.broadcasted_iota(jnp.int32, shape, 0 if transposed else 1)
    t = jnp.where(r < HG_CHUNK, r, r - HG_CHUNK)
    local = (r < HG_CHUNK) & (t // HG_SUB == c // HG_SUB) & (c <= t)
    before = (r >= HG_CHUNK) & (c < (t // HG_SUB) * HG_SUB)
    return (local | before).astype(f32)


def _two_level_sums(x):
    two = _sel_dot(_two_level_matrix(), x, "nn")
    return two[:HG_CHUNK], two[HG_CHUNK:]


@jax.custom_vjp
def _two_level_cumsum(x):
    return _two_level_sums(x)


_two_level_cumsum.defvjp(
    lambda x: (_two_level_sums(x), None),
    lambda _, g: (_sel_dot(_two_level_matrix(), jnp.concatenate(g, axis=0), "tn"),))


def _tile_matrix():
    s = lax.broadcasted_iota(jnp.int32, (HG_SUB, HG_CHUNK), 0)
    j = lax.broadcasted_iota(jnp.int32, (HG_SUB, HG_CHUNK), 1)
    return (j % HG_SUB == s).astype(f32)


@jax.custom_vjp
def _tile_lanes(x):
    return _sel_dot(_tile_matrix(), x, "nn", x_first=True, pieces=1)


_tile_lanes.defvjp(
    lambda x: (_sel_dot(_tile_matrix(), x, "nn", x_first=True, pieces=1), None),
    lambda _, g: (_sel_dot(_tile_matrix(), g, "nt", x_first=True, pieces=2),))


def _block_rows(x):
    k = x.shape[-1]
    return jnp.broadcast_to(x.reshape(HG_NSUB, 1, HG_SUB, k), (HG_NSUB, HG_SUB, HG_SUB, k)).reshape(HG_CHUNK, HG_SUB, k)


def _hgrn_chunk(st0, q_raw, f_raw, i_raw, g_raw, l0, l1, ng):
    C, SUB = HG_CHUNK, HG_SUB
    lb = jax.nn.sigmoid(l0 - l1)
    fg = lb + (1.0 - lb) * jax.nn.sigmoid(f_raw)
    kk = 1.0 - fg
    qf = jax.nn.silu(q_raw)
    al, base = _two_level_cumsum(jnp.log(fg))
    a = al + base
    row = lax.broadcasted_iota(jnp.int32, (C, HEAD), 0)
    a_last = jnp.sum(jnp.where(row == C - 1, a, 0.0), axis=0, keepdims=True)
    inter = _dot_nt(qf * jnp.exp(a), st0)
    qt = qf * jnp.exp(al)
    rb = lax.broadcasted_iota(jnp.int32, (C, C), 0) // SUB
    cb = lax.broadcasted_iota(jnp.int32, (C, C), 1) // SUB
    scores = jnp.zeros((C, C), f32)
    for i in range(1, HG_NSUB):
        base_i = jnp.sum(jnp.where(row == i * SUB, base, 0.0), axis=0, keepdims=True)
        kt = kk * jnp.exp(jnp.minimum(base_i - a, 0.0))
        scores = scores + jnp.where((rb == i) & (cb < i), _dot_nt(qt, kt), 0.0)
    t_i = lax.broadcasted_iota(jnp.int32, (C, SUB, HEAD), 0) % SUB
    s_i = lax.broadcasted_iota(jnp.int32, (C, SUB, HEAD), 1)
    decay = jnp.exp(jnp.where(s_i <= t_i, al[:, None, :] - _block_rows(al), -jnp.inf))
    diag = jnp.sum(qf[:, None, :] * decay * _block_rows(kk), axis=-1)
    scores = scores + jnp.where(rb == cb, _tile_lanes(diag), 0.0)
    o = inter + _dot_nn(scores, i_raw)
    st1 = jnp.exp(a_last) * st0 + _dot_tn(i_raw, kk * jnp.exp(a_last - a))
    on = o * lax.rsqrt(jnp.mean(o * o, axis=-1, keepdims=True) + EPS) * ng
    return st1, on * jax.nn.silu(g_raw)


def _hgrn_specs(S, Bl, rev):
    N = S // HG_CHUNK
    chunk = (lambda n: N - 1 - n) if rev else (lambda n: n)
    col = lambda c0: pl.BlockSpec((Bl, HG_CHUNK, 512), lambda n: (0, chunk(n), c0 // 512))
    st = pl.BlockSpec((Bl, N_HEAD, 1, HEAD, HEAD), lambda n: (0, 0, chunk(n), 0, 0))
    full = lambda *s: pl.BlockSpec(s, functools.partial(lambda n, nd: (0,) * nd, nd=len(s)))
    return N, col, st, full


def _hgrn_fwd(proj, lb_logits, ng, Bl, S):
    N, col, st, full = _hgrn_specs(S, Bl, False)

    def body(q_ref, f_ref, i_ref, g_ref, l_ref, ng_ref, o_ref, st_ref, state):
        @pl.when(pl.program_id(0) == 0)
        def _():
            state[...] = jnp.zeros_like(state)

        for b in range(Bl):
            for h in range(N_HEAD):
                sl = slice(h * HEAD, (h + 1) * HEAD)
                st0 = state[b, h]
                st_ref[b, h, 0] = st0
                st1, out = _hgrn_chunk(st0, *[r[b, :, sl].astype(f32) for r in (q_ref, f_ref, i_ref, g_ref)],
                                       l_ref[0:1, sl], l_ref[1:2, sl], ng_ref[...])
                state[b, h] = st1
                o_ref[b, :, sl] = out.astype(bf16)

    return _call(body, "hgrn_fwd", (N,), [col(C_HQ), col(C_HF), col(C_HI), col(C_HG), full(2, 512), full(1, HEAD)],
                 (col(0), st),
                 (jax.ShapeDtypeStruct((Bl, S, 512), bf16), jax.ShapeDtypeStruct((Bl, N_HEAD, N, HEAD, HEAD), f32)),
                 scratch=[pltpu.VMEM((Bl, N_HEAD, HEAD, HEAD), f32)], sem=("arbitrary",))(
        proj, proj, proj, proj, lb_logits, ng)


def _hgrn_bwd(proj, lb_logits, ng, states, db, Bl, S):
    N, col, st, full = _hgrn_specs(S, Bl, True)

    def body(q_ref, f_ref, i_ref, g_ref, l_ref, ng_ref, st_ref, db_ref,
             dq_ref, df_ref, di_ref, dg_ref, dl_ref, dng_ref, dstate):
        @pl.when(pl.program_id(0) == 0)
        def _():
            dstate[...] = jnp.zeros_like(dstate)
            dl_ref[...] = jnp.zeros_like(dl_ref)
            dng_ref[...] = jnp.zeros_like(dng_ref)

        for b in range(Bl):
            for h in range(N_HEAD):
                sl = slice(h * HEAD, (h + 1) * HEAD)
                _, vjp = jax.vjp(_hgrn_chunk, st_ref[b, h, 0], *[r[b, :, sl].astype(f32) for r in (q_ref, f_ref, i_ref, g_ref)],
                                 l_ref[0:1, sl], l_ref[1:2, sl], ng_ref[...])
                dst0, dq, df, di, dg, dl0, dl1, dng = vjp((dstate[b, h], db_ref[b, :, sl].astype(f32)))
                dstate[b, h] = dst0
                dq_ref[b, :, sl] = dq.astype(bf16)
                df_ref[b, :, sl] = df.astype(bf16)
                di_ref[b, :, sl] = di.astype(bf16)
                dg_ref[b, :, sl] = dg.astype(bf16)
                dl_ref[0:1, sl] += dl0
                dl_ref[1:2, sl] += dl1
                dng_ref[b, h] += dng

    return _call(body, "hgrn_bwd", (N,),
                 [col(C_HQ), col(C_HF), col(C_HI), col(C_HG), full(2, 512), full(1, HEAD), st, col(0)],
                 (*[col(0)] * 4, full(2, 512), full(Bl, N_HEAD, 1, HEAD)),
                 (*[jax.ShapeDtypeStruct((Bl, S, 512), bf16)] * 4, jax.ShapeDtypeStruct((2, 512), f32),
                  jax.ShapeDtypeStruct((Bl, N_HEAD, 1, HEAD), f32)),
                 scratch=[pltpu.VMEM((Bl, N_HEAD, HEAD, HEAD), f32)], sem=("arbitrary",))(
        proj, proj, proj, proj, lb_logits, ng, states, db)


def _attn_probs(q, k):
    s = _raw_dot(q, k, "nt") * (HEAD ** -0.5)
    e = jnp.exp(s - jnp.max(s, axis=-1, keepdims=True))
    return e / jnp.sum(e, axis=-1, keepdims=True)


def _attn_specs(S, tq):
    nq = S // tq
    q = pl.BlockSpec((tq, 512), lambda b, i: (b * nq + i, C_XQ // 512))
    kv = pl.BlockSpec((1, MEM_LEN, 1024), lambda b, i: (b, 0, 0))
    o = pl.BlockSpec((tq, 512), lambda b, i: (b * nq + i, 0))
    return nq, q, kv, o


def _attn_fwd(proj, kv, Bl, S):
    tq = _pick(S, (512, 256, 128))
    nq, qs, kvs, os_ = _attn_specs(S, tq)

    def body(q_ref, kv_ref, o_ref):
        for h in range(N_HEAD):
            sl = slice(h * HEAD, (h + 1) * HEAD)
            p = _attn_probs(q_ref[:, sl], kv_ref[0, :, sl])
            o_ref[:, sl] = _raw_dot(p, kv_ref[0, :, 512 + h * HEAD:512 + (h + 1) * HEAD], "nn").astype(bf16)

    return _call(body, "attn_fwd", (Bl, nq), [qs, kvs], os_, jax.ShapeDtypeStruct((Bl * S, 512), bf16),
                 sem=("parallel", "parallel"))(proj, kv)


def _attn_bwd(proj, kv, dc, Bl, S):
    tq = _pick(S, (512, 256, 128))
    nq, qs, kvs, os_ = _attn_specs(S, tq)

    def body(q_ref, kv_ref, do_ref, dq_ref, dkv_ref):
        @pl.when(pl.program_id(1) == 0)
        def _():
            dkv_ref[...] = jnp.zeros_like(dkv_ref)

        for h in range(N_HEAD):
            sl = slice(h * HEAD, (h + 1) * HEAD)
            vsl = slice(512 + h * HEAD, 512 + (h + 1) * HEAD)
            q, k, v, do = q_ref[:, sl], kv_ref[0, :, sl], kv_ref[0, :, vsl], do_ref[:, sl]
            p = _attn_probs(q, k)
            dkv_ref[0, :, vsl] += _raw_dot(p, do, "tn")
            dp = _raw_dot(do, v, "nt")
            ds = p * (dp - jnp.sum(dp * p, axis=-1, keepdims=True)) * (HEAD ** -0.5)
            dq_ref[:, sl] = _raw_dot(ds, k, "nn").astype(bf16)
            dkv_ref[0, :, sl] += _raw_dot(ds, q, "tn")

    return _call(body, "attn_bwd", (Bl, nq), [qs, kvs, os_], (os_, kvs),
                 (jax.ShapeDtypeStruct((Bl * S, 512), bf16), jax.ShapeDtypeStruct((Bl, MEM_LEN, 1024), f32)),
                 sem=("arbitrary", "arbitrary"))(proj, kv, dc)


def _merge_specs(tm, tn):
    br = pl.BlockSpec((tm, 512), lambda i, j: (i, 0))
    w = pl.BlockSpec((512, tn), lambda i, j: (0, j))
    gl = [pl.BlockSpec((tm, tn), functools.partial(lambda i, j, n: (i, (C_GL + n * D_MODEL) // tn + j), n=n)) for n in range(3)]
    return [br, br, br, w, w, w, *gl]


def _merge_fwd(branches, wb, proj):
    T = proj.shape[0]
    tm, tn = _pick(T, (1024, 512, 256, 128)), 512

    def body(a_ref, b_ref, c_ref, w0, w1, w2, g0, g1, g2, o_ref):
        acc = jnp.zeros((tm, tn), f32)
        for x_ref, w_ref, g_ref in ((a_ref, w0, g0), (b_ref, w1, g1), (c_ref, w2, g2)):
            acc = acc + jax.nn.sigmoid(g_ref[...].astype(f32)) * _raw_dot(x_ref[...], w_ref[...], "nn")
        o_ref[...] = acc.astype(bf16)

    return _call(body, "merge_fwd", (T // tm, D_MODEL // tn), _merge_specs(tm, tn), pl.BlockSpec((tm, tn), lambda i, j: (i, j)),
                 jax.ShapeDtypeStruct((T, D_MODEL), bf16), sem=("parallel", "parallel"))(*branches, *wb, proj, proj, proj)


def _merge_bwd(branches, wb, proj, dmerged):
    T = proj.shape[0]
    tm, tn = _pick(T, (1024, 512, 256, 128)), 512

    def body(a_ref, b_ref, c_ref, w0, w1, w2, g0, g1, g2, dm_ref, dgl_ref, d0, d1, d2):
        dm = dm_ref[...]
        for n, (x_ref, w_ref, g_ref, d_ref) in enumerate(((a_ref, w0, g0, d0), (b_ref, w1, g1, d1), (c_ref, w2, g2, d2))):
            up = _raw_dot(x_ref[...], w_ref[...], "nn")
            logits = g_ref[...].astype(f32)
            dgl_ref[n] = _egrad(jax.nn.sigmoid, logits, dm * up).astype(bf16)
            d_ref[...] = (dm * jax.nn.sigmoid(logits)).astype(bf16)

    blk = pl.BlockSpec((tm, tn), lambda i, j: (i, j))
    sh = jax.ShapeDtypeStruct((T, D_MODEL), bf16)
    outs = _call(body, "merge_bwd", (T // tm, D_MODEL // tn), [*_merge_specs(tm, tn), blk],
                 (pl.BlockSpec((3, tm, tn), lambda i, j: (0, i, j)), blk, blk, blk),
                 (jax.ShapeDtypeStruct((3, T, D_MODEL), bf16), sh, sh, sh),
                 sem=("parallel", "parallel"))(*branches, *wb, proj, proj, proj, dmerged)
    return outs[0], outs[1:]


CONV_TC = 256


def _shift_down(a, k):
    r = pltpu.roll(a, k, 0)
    row = lax.broadcasted_iota(jnp.int32, (8, a.shape[1]), 0)
    return jnp.concatenate([jnp.where(row >= k, r[:8], 0.0), r[8:]], axis=0)


def _shift_up(a, k):
    n = a.shape[0]
    r = pltpu.roll(a, n - k, 0)
    row = lax.broadcasted_iota(jnp.int32, (8, a.shape[1]), 0)
    return jnp.concatenate([r[:n - 8], jnp.where(row < 8 - k, r[n - 8:], 0.0)], axis=0)


def _conv_pre(a, a1, a2, cw, cb):
    return cb + cw[0:1] * a2 + cw[1:2] * a1 + cw[2:3] * a


def _up_conv_fwd(h2, w_up, cw, cb):
    Bl, S, Dd = h2.shape
    nc = D_FF // CONV_TC

    def body(h_ref, wa_ref, wb_ref, cw_ref, cb_ref, a_ref, b_ref, o_ref):
        a16 = _raw_dot(h_ref[0], wa_ref[...], "nn").astype(bf16)
        b16 = _raw_dot(h_ref[0], wb_ref[...], "nn").astype(bf16)
        a_ref[0], b_ref[0] = a16, b16
        a = a16.astype(f32)
        ac = _conv_pre(a, _shift_down(a, 1), _shift_down(a, 2), cw_ref[...], cb_ref[...])
        o_ref[0] = (jax.nn.silu(ac) * b16.astype(f32)).astype(bf16)

    seq = pl.BlockSpec((1, S, CONV_TC), lambda b, c: (b, 0, c))
    sh = jax.ShapeDtypeStruct((Bl, S, D_FF), bf16)
    return _call(body, "up_conv_fwd", (Bl, nc),
                 [pl.BlockSpec((1, S, Dd), lambda b, c: (b, 0, 0)), pl.BlockSpec((Dd, CONV_TC), lambda b, c: (0, c)),
                  pl.BlockSpec((Dd, CONV_TC), lambda b, c: (0, nc + c)), pl.BlockSpec((3, CONV_TC), lambda b, c: (0, c)),
                  pl.BlockSpec((1, CONV_TC), lambda b, c: (0, c))],
                 (seq, seq, seq), (sh, sh, sh), sem=("parallel", "parallel"))(h2, w_up, w_up, cw, cb)


def _down_conv_bwd(dx2, w_down, a, b, cw, cb):
    Bl, S, Dd = dx2.shape
    nc = D_FF // CONV_TC

    def body(dx_ref, wd_ref, a_ref, b_ref, cw_ref, cb_ref, da_ref, db_ref, dcw_ref, dcb_ref):
        dact = _raw_dot(dx_ref[0], wd_ref[...], "nt").astype(bf16).astype(f32)
        a, cw = a_ref[0].astype(f32), cw_ref[...]
        a1, a2 = _shift_down(a, 1), _shift_down(a, 2)
        ac = _conv_pre(a, a1, a2, cw, cb_ref[...])
        sg = jax.nn.sigmoid(ac)
        gated = dact * sg
        db_ref[0] = (gated * ac).astype(bf16)
        dac = gated * b_ref[0].astype(f32) * (1.0 + ac * (1.0 - sg))
        da_ref[0] = (cw[2:3] * dac + cw[1:2] * _shift_up(dac, 1) + cw[0:1] * _shift_up(dac, 2)).astype(bf16)
        dcw_ref[0, 0:1, :] = jnp.sum(dac * a2, axis=0, keepdims=True)
        dcw_ref[0, 1:2, :] = jnp.sum(dac * a1, axis=0, keepdims=True)
        dcw_ref[0, 2:3, :] = jnp.sum(dac * a, axis=0, keepdims=True)
        dcb_ref[0] = jnp.sum(dac, axis=0, keepdims=True)

    seq = pl.BlockSpec((1, S, CONV_TC), lambda b_, c: (b_, 0, c))
    sh = jax.ShapeDtypeStruct((Bl, S, D_FF), bf16)
    return _call(body, "down_conv_bwd", (Bl, nc),
                 [pl.BlockSpec((1, S, Dd), lambda b_, c: (b_, 0, 0)), pl.BlockSpec((CONV_TC, Dd), lambda b_, c: (c, 0)), seq, seq,
                  pl.BlockSpec((3, CONV_TC), lambda b_, c: (0, c)), pl.BlockSpec((1, CONV_TC), lambda b_, c: (0, c))],
                 (seq, seq, pl.BlockSpec((1, 3, CONV_TC), lambda b_, c: (b_, 0, c)), pl.BlockSpec((1, 1, CONV_TC), lambda b_, c: (b_, 0, c))),
                 (sh, sh, jax.ShapeDtypeStruct((Bl, 3, D_FF), f32), jax.ShapeDtypeStruct((Bl, 1, D_FF), f32)),
                 sem=("parallel", "parallel"))(dx2, w_down, a, b, cw, cb)


def _local_step(x, mem, target, p, w_in, late_b, late_c, send, settle):
    Bl, S, Dd = x.shape
    T = Bl * S
    x2d, t2d, mem2d = x.reshape(T, Dd), target.reshape(T, Dd), mem.reshape(Bl * MEM_LEN, Dd)
    b_st = jnp.pad(p["b_spatial"].T, ((0, 0), (0, 128 - N_HEAD)))
    lbl = p["lb_logits"]

    h, h_t = _rms_fwd(x2d, p["norm1_g"], "norm1_fwd", transposed=True)
    proj = _mm(h, w_in, "nn", bf16, "proj_fwd", 1024, 1664)
    a_out = _gmlp_fwd(proj, p["ln_v_g"], p["ln_v_b"], p["w_spatial"], b_st)
    proj3 = proj.reshape(Bl, S, IN_WIDTH)
    b_out, states = _hgrn_fwd(proj3, lbl, p["hgrn_norm_g"], Bl, S)
    b_out = b_out.reshape(T, 512)
    memn = _rms_fwd(mem2d, p["mem_norm_g"], "memnorm_fwd")
    w = late_b(b_out)
    wb = w["w_branch"]
    kv = _mm(memn, w["w_mem_kv"], "nn", f32, "kv_fwd", 512, 1024).reshape(Bl, MEM_LEN, 2 * 512)
    c_out = _attn_fwd(proj, kv, Bl, S)
    branches = (a_out, b_out, c_out)
    merged = _merge_fwd(branches, wb, proj)
    x1 = _mm(merged, w["w_out"], "nn", f32, "out_fwd", 1024, 1024, residual=x2d)
    h2, h2_t = _rms_fwd(x1, p["norm2_g"], "norm2_fwd", transposed=True)
    w.update(late_c(h2))
    ffn_a, ffn_b, act = _up_conv_fwd(h2.reshape(Bl, S, Dd), w["w_up"], w["conv_w"], p["conv_b"])
    act = act.reshape(T, D_FF)
    x2 = _mm(act, w["w_down"], "nn", f32, "down_fwd", 512, 1024, residual=x1)
    loss_part, dx2, dx2_16, g_final = _final_loss(x2, p["final_g"], t2d)

    g_w_down = _mm(act, dx2_16, "tn", bf16, "down_dw", 1408, 1024, 1024)
    da, db, g_conv_w, g_conv_b = _down_conv_bwd(dx2_16.reshape(Bl, S, Dd), w["w_down"], ffn_a, ffn_b, w["conv_w"], p["conv_b"])
    da, db = da.reshape(T, D_FF), db.reshape(T, D_FF)
    shard = 2 * D_FF // N_DEV
    g_w_up = jnp.concatenate([_mm(h2_t, d, "nn", bf16, f"up_dw_{n}", 512, 1408, shard=shard, n_outer=True)
                              for n, d in (("a", da), ("b", db))], axis=0)
    tok = send("c", dict(w_up=g_w_up, conv_w=jnp.sum(g_conv_w, axis=0), w_down=g_w_down))
    dh2 = _mm(da, w["w_up_a"], "nt", f32, "up_dx_a", 512, 1024)
    dh2 = _mm(db, w["w_up_b"], "nt", f32, "up_dx_b", 512, 1024, residual=dh2)
    dx1, g_norm2 = _rms_bwd(x1, p["norm2_g"] + tok[0, 0], dh2, "norm2_bwd", residual=dx2)

    g_w_out = _mm(merged, dx1, "tn", bf16, "out_dw", 1024, 1024, 1024)
    dmerged = _mm(dx1, w["w_out"], "nt", f32, "out_dx", 1024, 1024)
    dgl, dup = _merge_bwd(branches, wb, proj, dmerged)
    g_w_branch = [_mm(branches[n], dup[n], "tn", bf16, f"branch_dw{n}", 512, 1024, 1024) for n in range(3)]
    dbr = [_mm(dup[n], wb[n], "nt", bf16, f"branch_dx{n}", 1024, 512) for n in range(3)]
    dxq, dkv = _attn_bwd(proj, kv, dbr[2], Bl, S)
    dkv = dkv.reshape(Bl * MEM_LEN, 2 * 512)
    g_w_kv = _mm(memn, dkv, "tn", bf16, "kv_dw", 1024, 1024, 512)
    tok = send("b", dict(w_mem_kv=g_w_kv, w_branch=g_w_branch, w_out=g_w_out))
    dmemn = _mm(dkv, w["w_mem_kv"], "nt", f32, "kv_dx", 512, 1024)
    _, g_mem_norm = _rms_bwd(mem2d, p["mem_norm_g"], dmemn, "memnorm_bwd")
    dzuv, g_ln_g, g_ln_b, g_w_sp, g_b_sp = _gmlp_bwd(proj, p["ln_v_g"] + tok[0, 0], p["ln_v_b"], p["w_spatial"], b_st, dbr[0])
    *dqfig, g_lbl, g_ng = _hgrn_bwd(proj3, lbl, p["hgrn_norm_g"], states, dbr[1].reshape(Bl, S, 512), Bl, S)
    dq, df, di, dg = [d.reshape(T, 512) for d in dqfig]
    dproj = jnp.concatenate([dzuv, dq, df, di, dg, dxq, dgl[0], dgl[1], dgl[2]], axis=-1)
    g_w_in = _mm(h_t, dproj, "nn", bf16, "proj_dw", 512, 1664, shard=IN_WIDTH // N_DEV, n_outer=True)
    tok = send("a", dict(w_in=g_w_in))
    dh = _mm(settle(dproj), w_in, "nt", f32, "proj_dx", 512, 1024)
    dx, g_norm1 = _rms_bwd(x2d, p["norm1_g"] + tok[0, 0], dh, "norm1_bwd", residual=dx1)

    gs = dict(w_spatial=g_w_sp, norm1_g=g_norm1, mem_norm_g=g_mem_norm, norm2_g=g_norm2, final_g=g_final, lb_logits=g_lbl,
              ln_v_g=g_ln_g, ln_v_b=g_ln_b, b_spatial=g_b_sp, hgrn_norm_g=g_ng, conv_b=g_conv_b)
    return loss_part, dx.reshape(Bl, S, Dd), gs


def _coords():
    return lax.axis_index("x"), lax.axis_index("y"), lax.axis_index("c")


def _slot(dev):
    return 4 * dev[0] + 2 * dev[1] + dev[2]


def _comm_call(body, name, arrays, out_shapes, n_sem):
    n = len(arrays)
    hbm = pl.BlockSpec(memory_space=pl.ANY)
    return pl.pallas_call(
        body, name=name, out_shape=out_shapes, in_specs=[hbm] * n, out_specs=[hbm] * n,
        scratch_shapes=[pltpu.SemaphoreType.DMA((n_sem, n)), pltpu.SemaphoreType.DMA((n_sem, n)), pltpu.SemaphoreType.DMA((n,))])(*arrays)


def _all_gather(blocks, name):
    n = len(blocks)

    def body(*refs):
        x_refs, o_refs, (send_sems, recv_sems, local_sems) = refs[:n], refs[n:2 * n], refs[2 * n:]
        x, y, c = _coords()
        me, sibling = (x, y, c), (x, y, 1 - c)
        chips = [(1 - x, y), (x, 1 - y), (1 - x, 1 - y)]

        def copy(a, k, block_dev, to, from_input=False):
            dst = o_refs[a].at[_slot(block_dev)]
            return pltpu.make_async_remote_copy(src_ref=x_refs[a] if from_input else dst, dst_ref=dst, send_sem=send_sems.at[k, a],
                                                recv_sem=recv_sems.at[k, a], device_id=to, device_id_type=MESH)

        mine = [pltpu.make_async_copy(x_refs[a], o_refs[a].at[_slot(me)], local_sems.at[a]) for a in range(n)]
        first = [copy(a, 0, me, sibling, True) for a in range(n)]
        first += [copy(a, 1 + j, me, (*chip, c), True) for j, chip in enumerate(chips) for a in range(n)]
        for cp in mine + first:
            cp.start()
        passed = []
        for j, chip in enumerate(chips):
            for a in range(n):
                copy(a, 1 + j, (*chip, c), me).wait_recv()
                fwd = copy(a, 4 + j, (*chip, c), sibling)
                fwd.start()
                passed.append(fwd)
        for a in range(n):
            copy(a, 0, sibling, me).wait_recv()
        for j, chip in enumerate(chips):
            for a in range(n):
                copy(a, 4 + j, (*chip, 1 - c), me).wait_recv()
        for cp in first + passed:
            cp.wait_send()
        for cp in mine:
            cp.wait()

    return _comm_call(body, name, blocks, [jax.ShapeDtypeStruct((N_DEV,) + b.shape, b.dtype) for b in blocks], 7)


def _all_to_all(parts, name):
    n = len(parts)
    rel = [(0, 0, 1), (0, 1, 0), (0, 1, 1), (1, 0, 0), (1, 0, 1), (1, 1, 0), (1, 1, 1)]

    def body(*refs):
        x_refs, o_refs, (send_sems, recv_sems, local_sems) = refs[:n], refs[n:2 * n], refs[2 * n:]
        x, y, c = _coords()
        me = (x, y, c)
        peers = [(x ^ dx, y ^ dy, c ^ dc) for dx, dy, dc in rel]

        def copy(a, k, peer):
            return pltpu.make_async_remote_copy(src_ref=x_refs[a].at[_slot(peer)], dst_ref=o_refs[a].at[_slot(me)], send_sem=send_sems.at[k, a],
                                                recv_sem=recv_sems.at[k, a], device_id=peer, device_id_type=MESH)

        def arrival(a, k, peer):
            return pltpu.make_async_remote_copy(src_ref=x_refs[a].at[_slot(me)], dst_ref=o_refs[a].at[_slot(peer)], send_sem=send_sems.at[k, a],
                                                recv_sem=recv_sems.at[k, a], device_id=peer, device_id_type=MESH)

        mine = [pltpu.make_async_copy(x_refs[a].at[_slot(me)], o_refs[a].at[_slot(me)], local_sems.at[a]) for a in range(n)]
        sends = [copy(a, k, peer) for k, peer in enumerate(peers) for a in range(n)]
        for cp in mine + sends:
            cp.start()
        for k, peer in enumerate(peers):
            for a in range(n):
                arrival(a, k, peer).wait_recv()
        for cp in sends:
            cp.wait_send()
        for cp in mine:
            cp.wait()

    return _comm_call(body, name, parts, [jax.ShapeDtypeStruct(p.shape, p.dtype) for p in parts], 7)


_HBM = pl.BlockSpec(memory_space=pltpu.HBM)
_SEM = pl.BlockSpec(memory_space=pltpu.SEMAPHORE)
_REL = [(0, 0, 1), (0, 1, 0), (0, 1, 1), (1, 0, 0), (1, 0, 1), (1, 1, 0), (1, 1, 1)]


_LINK_ORDER = (3, 1, 5, 4, 2, 6, 0)
SEND_PIECES = 4


def _pieces(shape, dtype):
    rows = shape[0]
    unit = 1 if len(shape) > 2 else (16 if dtype == bf16 else 8)
    for n in (SEND_PIECES, 2):
        if rows % (n * unit) == 0:
            return [pl.ds(i * (rows // n), rows // n) for i in range(n)]
    return [pl.ds(0, rows)]


def _split_copies(gather, src, land, send, recv, pieces):
    x, y, c = _coords()
    me = (x, y, c)
    copies = []
    for a in range(len(src)):
        block = src[a].shape if gather else src[a].shape[1:]
        for rows in (_pieces(block, src[a].dtype) if pieces else [None]):
            for k in _LINK_ORDER:
                dx, dy, dc = _REL[k]
                peer = (x ^ dx, y ^ dy, c ^ dc)
                mine, there = (src[a] if gather else src[a].at[_slot(peer)]), land[a].at[_slot(me)]
                if rows is not None:
                    mine, there = mine.at[rows], there.at[rows]
                copies.append(pltpu.make_async_remote_copy(src_ref=mine, dst_ref=there, send_sem=send[a].at[k], recv_sem=recv[a].at[k],
                                                           device_id=peer, device_id_type=MESH))
    return me, copies


def _arrivals(gather, src, land, send, recv):
    x, y, c = _coords()
    out = []
    for a in range(len(src)):
        for k, (dx, dy, dc) in enumerate(_REL):
            peer = (x ^ dx, y ^ dy, c ^ dc)
            out.append(pltpu.make_async_remote_copy(src_ref=src[a] if gather else src[a].at[_slot(peer)], dst_ref=land[a].at[_slot(peer)],
                                                    send_sem=send[a].at[k], recv_sem=recv[a].at[k], device_id=peer, device_id_type=MESH))
    return out


def _exchange_start(arrays, gather, name, after=None):
    n = len(arrays)
    e = 0 if after is None else 1
    lands = [lax.empty(((N_DEV,) + a.shape) if gather else a.shape, a.dtype) for a in arrays]

    def body(*refs):
        src, land = refs[:n], refs[n:2 * n]
        refs = refs[2 * n + e:]
        send, recv, token, local_sems = refs[:n], refs[n:2 * n], refs[4 * n], refs[4 * n + 1]
        me, out = _split_copies(gather, src, land, send, recv, True)
        local = [pltpu.make_async_copy(src[a] if gather else src[a].at[_slot(me)], land[a].at[_slot(me)], local_sems.at[a])
                 for a in range(n)]
        for cp in local:
            cp.start()
        for cp in local:
            cp.wait()
        for cp in out:
            cp.start()
        token[...] = jnp.zeros_like(token)

    sems = [pltpu.SemaphoreType.DMA((7,)) for _ in range(2 * n)]
    outs = pl.pallas_call(
        body, name=name,
        out_shape=(*sems, *[pltpu.HBM(a.shape, a.dtype) for a in arrays], *[pltpu.HBM(l.shape, l.dtype) for l in lands],
                   jax.ShapeDtypeStruct((8, 128), f32)),
        in_specs=[_HBM] * (2 * n) + [pl.BlockSpec(memory_space=pl.ANY)] * e,
        out_specs=(*[_SEM] * (2 * n), *[_HBM] * (2 * n), pl.BlockSpec(memory_space=pltpu.VMEM)),
        input_output_aliases={i: 2 * n + i for i in range(2 * n)},
        scratch_shapes=[pltpu.SemaphoreType.DMA((n,))],
        compiler_params=pltpu.CompilerParams(has_side_effects=pltpu.SideEffectType.DATAFLOW_SIDE_EFFECTING))(
        *[pltpu.with_memory_space_constraint(a, pltpu.HBM) for a in arrays],
        *[pltpu.with_memory_space_constraint(l, pltpu.HBM) for l in lands], *([after] if e else []))
    return (gather, n, outs[:4 * n]), outs[4 * n]


def _exchange_wait(handle, which, after, name):
    gather, n_all, vals = handle
    send_v, recv_v, src_v, land_v = [[vals[g * n_all + i] for i in which] for g in range(4)]
    n = len(which)

    def body(*refs):
        src, land, send, recv = refs[:n], refs[n:2 * n], refs[2 * n:3 * n], refs[3 * n:4 * n]
        for cp in _split_copies(gather, src, land, send, recv, False)[1]:
            cp.wait_send()
        for cp in _arrivals(gather, src, land, send, recv):
            cp.wait_recv()

    outs = pl.pallas_call(
        body, name=name,
        out_shape=(*[pltpu.HBM(a.shape, a.dtype) for a in src_v], *[pltpu.HBM(l.shape, l.dtype) for l in land_v]),
        in_specs=[*[_HBM] * (2 * n), *[_SEM] * (2 * n), pl.BlockSpec(memory_space=pl.ANY)], out_specs=[_HBM] * (2 * n),
        input_output_aliases={i: i for i in range(2 * n)},
        compiler_params=pltpu.CompilerParams(has_side_effects=pltpu.SideEffectType.DATAFLOW_SIDE_EFFECTING))(
        *src_v, *land_v, *send_v, *recv_v, after)
    return outs[n:]


def _seq_exchange(arrays, gather, name, collective_id):
    n = len(arrays)
    hbm = pltpu.MemorySpace.HBM
    srcs = [jax.new_ref(a, memory_space=hbm) for a in arrays]
    lands = [jax.empty_ref(jax.ShapeDtypeStruct(((N_DEV,) + a.shape) if gather else a.shape, a.dtype), memory_space=hbm) for a in arrays]

    @pl.kernel(mesh=plsc.ScalarSubcoreMesh(axis_name="sequencer", num_cores=1), name=name,
               scratch_types=(pltpu.SemaphoreType.DMA((7, n)), pltpu.SemaphoreType.DMA((7, n)), pltpu.SemaphoreType.DMA((n,))),
               compiler_params=pltpu.CompilerParams(collective_id=collective_id))
    def launch(send, recv, local):
        x, y, c = _coords()
        me = (x, y, c)
        peers = [(x ^ dx, y ^ dy, c ^ dc) for dx, dy, dc in _REL]
        barrier = pltpu.get_barrier_semaphore()
        for peer in peers:
            pl.semaphore_signal(barrier, inc=1, device_id=peer, device_id_type=MESH)
        pl.semaphore_wait(barrier, len(peers))

        def copy(a, k, peer, arrival):
            return pltpu.make_async_remote_copy(
                src_ref=srcs[a] if gather else srcs[a].at[_slot(peer)], dst_ref=lands[a].at[_slot(peer if arrival else me)],
                send_sem=send.at[k, a], recv_sem=recv.at[k, a], device_id=peer, device_id_type=MESH)

        mine = [pltpu.make_async_copy(srcs[a] if gather else srcs[a].at[_slot(me)], lands[a].at[_slot(me)], local.at[a])
                for a in range(n)]
        out = [copy(a, k, peer, False) for a in range(n) for k, peer in enumerate(peers)]
        for cp in mine + out:
            cp.start()
        for a in range(n):
            for k, peer in enumerate(peers):
                copy(a, k, peer, True).wait_recv()
        for cp in out:
            cp.wait_send()
        for cp in mine:
            cp.wait()

    launch()
    return [land[...] for land in lands]


def _adam_math(w, g, m, v):
    m_ = ADAM_B1 * m + (1.0 - ADAM_B1) * g
    v_ = ADAM_B2 * v + (1.0 - ADAM_B2) * jnp.square(g)
    m_hat = m_ / (1.0 - ADAM_B1 ** ADAM_STEP)
    v_hat = v_ / (1.0 - ADAM_B2 ** ADAM_STEP)
    return -ADAM_LR * (m_hat / (jnp.sqrt(v_hat) + ADAM_EPS) + ADAM_WD * w), m_, v_


def _reduce_adamw(parts, w, m, v, name):
    _, R, L = parts.shape
    tr = _pick(R, (256, 128, 64, 32, 16, 8))

    def body(p_ref, w_ref, m_ref, v_ref, g_ref, d_ref, nm_ref, nv_ref):
        g = p_ref[0].astype(f32)
        for i in range(1, N_DEV):
            g = g + p_ref[i].astype(f32)
        g_ref[...] = g
        d_ref[...], nm_ref[...], nv_ref[...] = _adam_math(w_ref[...], g, m_ref[...], v_ref[...])

    blk = pl.BlockSpec((tr, L), lambda i: (i, 0))
    sh = jax.ShapeDtypeStruct((R, L), f32)
    return _call(body, name, (R // tr,), [pl.BlockSpec((N_DEV, tr, L), lambda i: (0, i, 0)), blk, blk, blk], (blk,) * 4, (sh,) * 4,
                 sem=("parallel",))(parts, w, m, v)


SMALL = (("w_spatial", (512, 128), 0), ("norm1_g", (1, 1024), 512), ("mem_norm_g", (1, 1024), 520), ("norm2_g", (1, 1024), 528),
         ("final_g", (1, 1024), 536), ("lb_logits", (2, 512), 544), ("ln_v_g", (1, 512), 552), ("ln_v_b", (1, 512), 556),
         ("b_spatial", (4, 128), 560), ("hgrn_norm_g", (1, 128), 564), ("conv_b", (1, 2816), 565))
LOSS_ROW, SMALL_USED, SMALL_ROWS = 587, 588, 640


def _segments(shape, base):
    r, n = shape
    per = n // 128
    return [(base + i * per + j, i, slice(j * 128, (j + 1) * 128)) for i in range(r) for j in range(per)]


def _pack_small(gs, loss_part):
    names = [n for n, _, _ in SMALL]

    def body(*refs):
        src, loss_ref, o_ref = dict(zip(names, refs[:-2])), refs[-2], refs[-1]
        o_ref[SMALL_USED:SMALL_ROWS, :] = jnp.zeros((SMALL_ROWS - SMALL_USED, 128), f32)
        o_ref[LOSS_ROW:LOSS_ROW + 1, :] = loss_ref[...]
        for name, shape, base in SMALL:
            ref = src[name]
            if name == "w_spatial":
                o_ref[base:base + 512, :] = ref[...].reshape(512, 128)
            elif name == "b_spatial":
                o_ref[base:base + 4, :] = ref[0:4, :]
            elif name == "conv_b":
                per_example = functools.reduce(lambda u, v_: u + v_, [ref[b] for b in range(ref.shape[0])])
                for row, i, sl in _segments(shape, base):
                    o_ref[row:row + 1, :] = per_example[i:i + 1, sl]
            elif name == "hgrn_norm_g":
                per_head = [ref[b, h] for b in range(ref.shape[0]) for h in range(N_HEAD)]
                o_ref[base:base + 1, :] = functools.reduce(lambda u, v_: u + v_, per_head)
            else:
                for row, i, sl in _segments(shape, base):
                    o_ref[row:row + 1, :] = ref[i:i + 1, sl]

    return pl.pallas_call(body, name="pack_small", out_shape=jax.ShapeDtypeStruct((SMALL_ROWS, 128), f32))(
        *[gs[n] for n in names], loss_part)


def _small_update(gathered, w, m, v):
    names = [n for n, _, _ in SMALL]
    k = len(names)

    def body(*refs):
        p_ref = refs[0]
        ins = [dict(zip(names, refs[1 + i * k:1 + (i + 1) * k])) for i in range(3)]
        outs = [dict(zip(names, refs[1 + (3 + i) * k:1 + (4 + i) * k])) for i in range(4)]
        loss_ref, gsum = refs[-2], refs[-1]
        g = p_ref[0]
        for i in range(1, N_DEV):
            g = g + p_ref[i]
        gsum[...] = g
        loss_ref[...] = gsum[LOSS_ROW:LOSS_ROW + 1, :]
        for name, shape, base in SMALL:
            if name == "w_spatial":
                where = [(slice(base, base + 512), (slice(None), slice(None)))]
            else:
                where = [(slice(row, row + 1), (slice(i, i + 1), sl)) for row, i, sl in _segments(shape, base)]
            for rows, at in where:
                g_ = gsum[rows, :]
                d_, m_, v_ = _adam_math(ins[0][name][at], g_, ins[1][name][at], ins[2][name][at])
                for o, val in zip(outs, (g_, d_, m_, v_)):
                    o[name][at] = val

    args = [gathered] + [d[n] for d in (w, m, v) for n in names]
    out_shapes = [jax.ShapeDtypeStruct(shape, f32) for _ in range(4) for _, shape, _ in SMALL] + [jax.ShapeDtypeStruct((1, 128), f32)]
    outs = pl.pallas_call(body, name="small_update", out_shape=out_shapes, scratch_shapes=[pltpu.VMEM((SMALL_ROWS, 128), f32)])(*args)
    return [dict(zip(names, outs[i * k:(i + 1) * k])) for i in range(4)], outs[-1]


def _cols_full(g):
    return jnp.moveaxis(g, 0, -2).reshape(g.shape[1:-1] + (N_DEV * g.shape[-1],))


def _cols_parts(full):
    n = full.shape[-1] // N_DEV
    return jnp.moveaxis(full.reshape(full.shape[:-1] + (N_DEV, n)), -2, 0)


def kernel(x, mem, norm1_g, w_in, ln_v_g, ln_v_b, w_spatial, b_spatial, lb_logits, hgrn_norm_g, mem_norm_g, w_mem_kv, w_branch, w_out, norm2_g, w_up, conv_w, conv_b, w_down, final_g, loss_target, m_norm1_g, m_w_in, m_ln_v_g, m_ln_v_b, m_w_spatial, m_b_spatial, m_lb_logits, m_hgrn_norm_g, m_mem_norm_g, m_w_mem_kv, m_w_branch, m_w_out, m_norm2_g, m_w_up, m_conv_w, m_conv_b, m_w_down, m_final_g, v_norm1_g, v_w_in, v_ln_v_g, v_ln_v_b, v_w_spatial, v_b_spatial, v_lb_logits, v_hgrn_norm_g, v_mem_norm_g, v_w_mem_kv, v_w_branch, v_w_out, v_norm2_g, v_w_up, v_conv_w, v_conv_b, v_w_down, v_final_g):
    given = dict(locals())
    order = ("norm1_g", "w_in", "ln_v_g", "ln_v_b", "w_spatial", "b_spatial", "lb_logits", "hgrn_norm_g", "mem_norm_g",
             "w_mem_kv", "w_branch", "w_out", "norm2_g", "w_up", "conv_w", "conv_b", "w_down", "final_g")
    groups = dict(a=("w_in",), b=("w_mem_kv", "w_branch", "w_out"), c=("w_up", "conv_w", "w_down"))

    wire = {n: given[n][0].astype(f32 if n == "conv_w" else bf16) for ns in groups.values() for n in ns}
    g_in = _all_gather([wire["w_in"]], "gather_w_in")[0]
    late = groups["b"] + groups["c"]
    w_in_full = _cols_full(g_in)
    w_in_full, rest_wire = lax.optimization_barrier((w_in_full, [wire[n] for n in late]))
    rest = _seq_exchange(rest_wire, True, "gather_rest", 1)

    def late_b(after):
        _, (kv_, br_, out_) = lax.optimization_barrier((after, tuple(rest[0:3])))
        br_ = _cols_full(br_)
        return dict(w_mem_kv=kv_.reshape(D_MODEL, 2 * 512), w_branch=[br_[n] for n in range(3)], w_out=out_.reshape(D_MODEL, D_MODEL))

    def late_c(after):
        _, (up_, cw_, down_) = lax.optimization_barrier((after, tuple(rest[3:6])))
        up_ = _cols_full(up_)
        return dict(w_up=up_, w_up_a=up_[:, :D_FF], w_up_b=up_[:, D_FF:], conv_w=_cols_full(cw_), w_down=down_.reshape(D_FF, D_MODEL))

    to_parts = dict(w_in=lambda g_: g_, w_up=lambda g_: g_, conv_w=_cols_parts,
                    w_branch=lambda g_: _cols_parts(jnp.stack(g_)).reshape(N_DEV, -1, 128),
                    w_mem_kv=lambda g_: g_.reshape(N_DEV, -1, 2 * 512), w_out=lambda g_: g_.reshape(N_DEV, -1, D_MODEL),
                    w_down=lambda g_: g_.reshape(N_DEV, -1, D_MODEL))
    scatters = {}

    def send(tag, grads_):
        parts = [to_parts[n](grads_[n]) for n in groups[tag]]
        scatters[tag] = _seq_exchange(parts, False, f"scatter_{tag}", dict(a=2, b=4, c=5)[tag])
        return jnp.zeros((8, 128), f32)

    small_2d = lambda prefix: {n: given[prefix + n].reshape(shape) for n, shape, _ in SMALL}
    p = small_2d("")
    p["w_spatial"] = w_spatial[0]
    updates = {}

    def update(tag):
        for n, parts in zip(groups[tag], scatters[tag]):
            two_d = (-1, given[n].shape[-1])
            updates[n] = _reduce_adamw(parts, *[given[pre + n].reshape(two_d) for pre in ("", "m_", "v_")], "adamw_" + n)

    def settle(chain):
        update("c")
        update("b")
        early = groups["c"] + groups["b"]
        chain, tied = lax.optimization_barrier((chain, [updates[n] for n in early]))
        updates.update(zip(early, tied))
        return chain

    loss_part, grad_x, gs = _local_step(x, mem, loss_target, p, w_in_full, late_b, late_c, send, settle)

    gathered = _seq_exchange([_pack_small(gs, loss_part)], True, "gather_small", 3)[0]

    update("a")
    grads, delta, new_m, new_v = {}, {}, {}, {}
    for n, res in updates.items():
        grads[n], delta[n], new_m[n], new_v[n] = [r.reshape(given[n].shape) for r in res]

    small_results, loss_row = _small_update(gathered, small_2d(""), small_2d("m_"), small_2d("v_"))
    for dst, res in zip((grads, delta, new_m, new_v), small_results):
        for n, _, _ in SMALL:
            dst[n] = res[n].reshape(given[n].shape)
    loss = loss_row[0, 0]

    return (loss, grad_x, *[grads[n] for n in order], *[delta[n] for n in order], *[new_m[n] for n in order],
            *[new_v[n] for n in order])
```

```python
import functools

import jax
import jax.numpy as jnp
from jax import lax
from jax.experimental import pallas as pl
from jax.experimental.pallas import tpu as pltpu
from jax.experimental.pallas import tpu_sc as plsc

f32 = jnp.float32
bf16 = jnp.bfloat16

N_DEV = 8
D_MODEL = 1024
EPS = 1e-6
GM_CHUNK = 128
HG_CHUNK = 64
HEAD = 128
N_HEAD = 4
MEM_LEN = 256
D_FF = 2816
IN_WIDTH = 6656
C_ZU, C_HQ, C_HF, C_HI, C_HG, C_XQ, C_GL = 0, 1024, 1536, 2048, 2560, 3072, 3584
ADAM_LR, ADAM_B1, ADAM_B2, ADAM_EPS, ADAM_WD, ADAM_STEP = 0.001, 0.9, 0.999, 1e-08, 0.01, 10
VMEM_LIMIT = 56 * 1024 * 1024
MESH = pl.DeviceIdType.MESH


def _pick(n, cands):
    for c in cands:
        if n % c == 0:
            return c
    return n


def _call(body, name, grid, in_specs, out_specs, out_shape, scratch=(), sem=None, **cp):
    params = dict(vmem_limit_bytes=VMEM_LIMIT, **cp)
    if sem is not None:
        params["dimension_semantics"] = sem
    return pl.pallas_call(
        body, name=name, grid=grid, in_specs=in_specs, out_specs=out_specs, out_shape=out_shape,
        scratch_shapes=list(scratch), compiler_params=pltpu.CompilerParams(**params))


_DN = {"nn": (((1,), (0,)), ((), ())), "nt": (((1,), (1,)), ((), ())), "tn": (((0,), (0,)), ((), ()))}


def _raw_dot(a, b, mode):
    return lax.dot_general(a.astype(bf16), b.astype(bf16), _DN[mode], preferred_element_type=f32)


@jax.custom_vjp
def _dot_nn(a, b):
    return _raw_dot(a, b, "nn")


_dot_nn.defvjp(lambda a, b: (_raw_dot(a, b, "nn"), (a, b)),
               lambda r, g: (_raw_dot(g, r[1], "nt"), _raw_dot(r[0], g, "tn")))


@jax.custom_vjp
def _dot_nt(a, b):
    return _raw_dot(a, b, "nt")


_dot_nt.defvjp(lambda a, b: (_raw_dot(a, b, "nt"), (a, b)),
               lambda r, g: (_raw_dot(g, r[1], "nn"), _raw_dot(g, r[0], "tn")))


@jax.custom_vjp
def _dot_tn(a, b):
    return _raw_dot(a, b, "tn")


_dot_tn.defvjp(lambda a, b: (_raw_dot(a, b, "tn"), (a, b)),
               lambda r, g: (_raw_dot(r[1], g, "nt"), _raw_dot(r[0], g, "nn")))


def _tri(n, lower):
    r = lax.broadcasted_iota(jnp.int32, (n, n), 0)
    c = lax.broadcasted_iota(jnp.int32, (n, n), 1)
    return ((c <= r) if lower else (c >= r)).astype(f32)


def _sel_dot(sel, x, mode, x_first=False, pieces=3):
    sel = sel.astype(bf16)
    out, rest = None, x
    for p in range(pieces):
        piece = rest.astype(bf16)
        part = lax.dot_general(*((piece, sel) if x_first else (sel, piece)), _DN[mode], preferred_element_type=f32)
        out = part if out is None else out + part
        if p + 1 < pieces:
            rest = rest - piece.astype(f32)
    return out


def _egrad(fn, x, ct):
    return jax.vjp(fn, x)[1](ct)[0]


def _mm(a, b, mode, out_dtype, name, tm, tn, tk=None, residual=None, shard=None, n_outer=False):
    if mode == "nn":
        (M, K), (_, N) = a.shape, b.shape
    elif mode == "nt":
        (M, K), (N, _) = a.shape, b.shape
    else:
        (K, M), (_, N) = a.shape, b.shape
    tm, tn = min(tm, M), min(tn, N)
    tk = K if tk is None else min(tk, K)
    assert M % tm == 0 and N % tn == 0 and K % tk == 0, (name, M, N, K, tm, tn, tk)
    nk = K // tk

    def body(*refs):
        acc_ref = refs[-1] if nk > 1 else None
        refs = refs[:-1] if nk > 1 else refs
        if residual is None:
            a_ref, b_ref, o_ref = refs
        else:
            a_ref, b_ref, r_ref, o_ref = refs

        def finish(r):
            if residual is not None:
                r = r + r_ref[...]
            if shard is None:
                o_ref[...] = r.astype(out_dtype)
            else:
                for s in range(tn // shard):
                    o_ref[s] = r[:, s * shard:(s + 1) * shard].astype(out_dtype)

        part = _raw_dot(a_ref[...], b_ref[...], mode)
        if nk == 1:
            finish(part)
            return
        k = pl.program_id(2)

        @pl.when(k == 0)
        def _():
            acc_ref[...] = part

        @pl.when((k > 0) & (k < nk - 1))
        def _():
            acc_ref[...] += part

        @pl.when(k == nk - 1)
        def _():
            finish(acc_ref[...] + part)

    def at(index):
        return (lambda j, i, k: index(i, j, k)) if n_outer else index

    a_spec = {"nn": pl.BlockSpec((tm, tk), at(lambda i, j, k: (i, k))),
              "nt": pl.BlockSpec((tm, tk), at(lambda i, j, k: (i, k))),
              "tn": pl.BlockSpec((tk, tm), at(lambda i, j, k: (k, i)))}[mode]
    b_spec = {"nn": pl.BlockSpec((tk, tn), at(lambda i, j, k: (k, j))),
              "nt": pl.BlockSpec((tn, tk), at(lambda i, j, k: (j, k))),
              "tn": pl.BlockSpec((tk, tn), at(lambda i, j, k: (k, j)))}[mode]
    o_spec = pl.BlockSpec((tm, tn), at(lambda i, j, k: (i, j)))
    in_specs = [a_spec, b_spec] + ([o_spec] if residual is not None else [])
    args = (a, b) + ((residual,) if residual is not None else ())
    out_shape = jax.ShapeDtypeStruct((M, N), out_dtype)
    if shard is not None:
        assert residual is None and tn % shard == 0
        o_spec = pl.BlockSpec((tn // shard, tm, shard), at(lambda i, j, k: (j, i, 0)))
        out_shape = jax.ShapeDtypeStruct((N // shard, M, shard), out_dtype)
    grid = (N // tn, M // tm, nk) if n_outer else (M // tm, N // tn, nk)
    return _call(body, name, grid, in_specs, o_spec, out_shape,
                 scratch=[pltpu.VMEM((tm, tn), f32)] if nk > 1 else [], sem=("parallel", "parallel", "arbitrary"))(*args)


def _rms_fwd(x, g, name, transposed=False):
    R, Dd = x.shape
    tr = _pick(R, (512, 256, 128))

    def body(x_ref, g_ref, o_ref, *t_ref):
        xf = x_ref[...]
        y = xf * lax.rsqrt(jnp.mean(xf * xf, axis=-1, keepdims=True) + EPS) * g_ref[...]
        o_ref[...] = y.astype(bf16)
        if transposed:
            t_ref[0][...] = y.T.astype(bf16)

    row = pl.BlockSpec((tr, Dd), lambda i: (i, 0))
    out_specs, out_shape = row, jax.ShapeDtypeStruct((R, Dd), bf16)
    if transposed:
        out_specs, out_shape = (row, pl.BlockSpec((Dd, tr), lambda i: (0, i))), (out_shape, jax.ShapeDtypeStruct((Dd, R), bf16))
    return _call(body, name, (R // tr,), [row, pl.BlockSpec((1, Dd), lambda i: (0, 0))], out_specs, out_shape, sem=("parallel",))(x, g)


def _rms_bwd(x, g, dh, name, residual=None):
    R, Dd = x.shape
    tr = _pick(R, (512, 256, 128))

    def body(*refs):
        if residual is None:
            x_ref, g_ref, dh_ref, dx_ref, dg_ref = refs
        else:
            x_ref, g_ref, dh_ref, r_ref, dx_ref, dg_ref = refs
        xf = x_ref[...]
        rs = lax.rsqrt(jnp.mean(xf * xf, axis=-1, keepdims=True) + EPS)
        y = xf * rs
        dh_ = dh_ref[...].astype(f32)
        dy = dh_ * g_ref[...]
        dx = rs * (dy - y * jnp.mean(dy * y, axis=-1, keepdims=True))
        if residual is not None:
            dx = dx + r_ref[...]
        dx_ref[...] = dx

        @pl.when(pl.program_id(0) == 0)
        def _():
            dg_ref[...] = jnp.zeros_like(dg_ref)

        dg_ref[...] += jnp.sum(dh_ * y, axis=0, keepdims=True)

    row = pl.BlockSpec((tr, Dd), lambda i: (i, 0))
    vec = pl.BlockSpec((1, Dd), lambda i: (0, 0))
    in_specs = [row, vec, row] + ([row] if residual is not None else [])
    args = (x, g, dh) + ((residual,) if residual is not None else ())
    return _call(body, name, (R // tr,), in_specs, (row, vec),
                 (jax.ShapeDtypeStruct((R, Dd), f32), jax.ShapeDtypeStruct((1, Dd), f32)), sem=("arbitrary",))(*args)


def _final_loss(x2, g, target):
    R, Dd = x2.shape
    tr = _pick(R, (512, 256, 128))

    def body(x_ref, g_ref, t_ref, loss_ref, dx_ref, dxb_ref, dg_ref):
        xf = x_ref[...]
        rs = lax.rsqrt(jnp.mean(xf * xf, axis=-1, keepdims=True) + EPS)
        y = xf * rs
        err = y * g_ref[...] - t_ref[...]
        dh_ = err * (1.0 / Dd)
        dy = dh_ * g_ref[...]
        dx = rs * (dy - y * jnp.mean(dy * y, axis=-1, keepdims=True))
        dx_ref[...] = dx
        dxb_ref[...] = dx.astype(bf16)

        @pl.when(pl.program_id(0) == 0)
        def _():
            dg_ref[...] = jnp.zeros_like(dg_ref)
            loss_ref[...] = jnp.zeros_like(loss_ref)

        dg_ref[...] += jnp.sum(dh_ * y, axis=0, keepdims=True)
        part = jnp.sum(jnp.mean(err * err, axis=-1, keepdims=True), axis=0, keepdims=True)
        loss_ref[...] += 0.5 * part

    row = pl.BlockSpec((tr, Dd), lambda i: (i, 0))
    vec = pl.BlockSpec((1, Dd), lambda i: (0, 0))
    return _call(body, "final_loss", (R // tr,), [row, vec, row], (pl.BlockSpec((1, 128), lambda i: (0, 0)), row, row, vec),
                 (jax.ShapeDtypeStruct((1, 128), f32), jax.ShapeDtypeStruct((R, Dd), f32), jax.ShapeDtypeStruct((R, Dd), bf16),
                  jax.ShapeDtypeStruct((1, Dd), f32)), sem=("arbitrary",))(x2, g, target)


def _gmlp_parts(zuv, ln_g, ln_b):
    zu, zv = zuv[:, :512], zuv[:, 512:]
    u = jax.nn.gelu(zu)
    v = jax.nn.gelu(zv)
    mu = jnp.mean(v, axis=-1, keepdims=True)
    rs = lax.rsqrt(jnp.mean(jnp.square(v - mu), axis=-1, keepdims=True) + EPS)
    xh = (v - mu) * rs
    return zu, zv, u, xh, rs, xh * ln_g + ln_b


GM_TILE_CHUNKS = 4


def _gmlp_tile(T):
    n = _pick(T // GM_CHUNK, (GM_TILE_CHUNKS, 2, 1))
    return n, n * GM_CHUNK


def _gmlp_fwd(proj, ln_g, ln_b, w_s, b_st):
    T = proj.shape[0]
    nch, rows = _gmlp_tile(T)

    def body(p_ref, g_ref, b_ref, w_ref, bs_ref, o_ref):
        _, _, u, _, _, vn = _gmlp_parts(p_ref[...].astype(f32), g_ref[...], b_ref[...])
        causal = _tri(GM_CHUNK, True) > 0
        for gi in range(N_HEAD):
            sl = slice(gi * HEAD, (gi + 1) * HEAD)
            w = jnp.where(causal, w_ref[gi], 0.0)
            for ch in range(nch):
                rs_ = slice(ch * GM_CHUNK, (ch + 1) * GM_CHUNK)
                mixed = _raw_dot(w, vn[rs_, sl], "nn") + bs_ref[:, gi:gi + 1]
                o_ref[rs_, sl] = (u[rs_, sl] * mixed).astype(bf16)

    vec = pl.BlockSpec((1, 512), lambda i: (0, 0))
    return _call(body, "gmlp_fwd", (T // rows,),
                 [pl.BlockSpec((rows, 1024), lambda i: (i, 0)), vec, vec,
                  pl.BlockSpec((N_HEAD, GM_CHUNK, GM_CHUNK), lambda i: (0, 0, 0)), pl.BlockSpec((GM_CHUNK, 128), lambda i: (0, 0))],
                 pl.BlockSpec((rows, 512), lambda i: (i, 0)), jax.ShapeDtypeStruct((T, 512), bf16), sem=("parallel",))(
        proj, ln_g, ln_b, w_s, b_st)


def _gmlp_bwd(proj, ln_g, ln_b, w_s, b_st, da):
    T = proj.shape[0]
    nch, rows = _gmlp_tile(T)

    def body(p_ref, g_ref, b_ref, w_ref, bs_ref, da_ref, dp_ref, dg_ref, db_ref, dw_ref, dbs_ref):
        zu, zv, u, xh, rs, vn = _gmlp_parts(p_ref[...].astype(f32), g_ref[...], b_ref[...])
        causal = _tri(GM_CHUNK, True) > 0
        sub = lax.broadcasted_iota(jnp.int32, (8, GM_CHUNK), 0)
        ones = jnp.ones((8, HEAD), f32)
        dout = da_ref[...].astype(f32)

        @pl.when(pl.program_id(0) == 0)
        def _():
            for r in (dg_ref, db_ref, dw_ref, dbs_ref):
                r[...] = jnp.zeros_like(r)

        du, dvn, dbs = [], [], jnp.zeros((8, GM_CHUNK), f32)
        for gi in range(N_HEAD):
            sl = slice(gi * HEAD, (gi + 1) * HEAD)
            w = jnp.where(causal, w_ref[gi], 0.0)
            du_g, dvn_g, dw_g = [], [], jnp.zeros((GM_CHUNK, GM_CHUNK), f32)
            for ch in range(nch):
                rs_ = slice(ch * GM_CHUNK, (ch + 1) * GM_CHUNK)
                mixed = _raw_dot(w, vn[rs_, sl], "nn") + bs_ref[:, gi:gi + 1]
                du_g.append(dout[rs_, sl] * mixed)
                dm = dout[rs_, sl] * u[rs_, sl]
                dbs = dbs + jnp.where(sub == gi, _sel_dot(ones, dm, "nt"), 0.0)
                dw_g = dw_g + _raw_dot(dm, vn[rs_, sl], "nt")
                dvn_g.append(_raw_dot(w, dm, "tn"))
            dw_ref[gi] += jnp.where(causal, dw_g, 0.0)
            du.append(jnp.concatenate(du_g, axis=0))
            dvn.append(jnp.concatenate(dvn_g, axis=0))
        dbs_ref[...] += dbs
        du = jnp.concatenate(du, axis=-1)
        dvn = jnp.concatenate(dvn, axis=-1)
        dg_ref[...] += jnp.sum(dvn * xh, axis=0, keepdims=True)
        db_ref[...] += jnp.sum(dvn, axis=0, keepdims=True)
        dxh = dvn * g_ref[...]
        dv = rs * (dxh - jnp.mean(dxh, axis=-1, keepdims=True) - xh * jnp.mean(dxh * xh, axis=-1, keepdims=True))
        dp_ref[:, :512] = _egrad(jax.nn.gelu, zu, du).astype(bf16)
        dp_ref[:, 512:] = _egrad(jax.nn.gelu, zv, dv).astype(bf16)

    vec = pl.BlockSpec((1, 512), lambda i: (0, 0))
    wsp = pl.BlockSpec((N_HEAD, GM_CHUNK, GM_CHUNK), lambda i: (0, 0, 0))
    return _call(body, "gmlp_bwd", (T // rows,),
                 [pl.BlockSpec((rows, 1024), lambda i: (i, 0)), vec, vec, wsp, pl.BlockSpec((GM_CHUNK, 128), lambda i: (0, 0)),
                  pl.BlockSpec((rows, 512), lambda i: (i, 0))],
                 (pl.BlockSpec((rows, 1024), lambda i: (i, 0)), vec, vec, wsp, pl.BlockSpec((8, GM_CHUNK), lambda i: (0, 0))),
                 (jax.ShapeDtypeStruct((T, 1024), bf16), jax.ShapeDtypeStruct((1, 512), f32), jax.ShapeDtypeStruct((1, 512), f32),
                  jax.ShapeDtypeStruct((N_HEAD, GM_CHUNK, GM_CHUNK), f32), jax.ShapeDtypeStruct((8, GM_CHUNK), f32)),
                 sem=("arbitrary",))(proj, ln_g, ln_b, w_s, b_st, da)


HG_SUB = 8
HG_NSUB = HG_CHUNK // HG_SUB


def _two_level_matrix(transposed=False):
    shape = (HG_CHUNK, 2 * HG_CHUNK) if transposed else (2 * HG_CHUNK, HG_CHUNK)
    r = lax.broadcasted_iota(jnp.int32, shape, 1 if transposed else 0)
    c = lax.broadcasted_iota(jnp.int32, shape, 0 if transposed else 1)
    t = jnp.where(r < HG_CHUNK, r, r - HG_CHUNK)
    local = (r < HG_CHUNK) & (t // HG_SUB == c // HG_SUB) & (c <= t)
    before = (r >= HG_CHUNK) & (c < (t // HG_SUB) * HG_SUB)
    return (local | before).astype(f32)


def _two_level_sums(x):
    two = _sel_dot(_two_level_matrix(), x, "nn")
    return two[:HG_CHUNK], two[HG_CHUNK:]


@jax.custom_vjp
def _two_level_cumsum(x):
    return _two_level_sums(x)


_two_level_cumsum.defvjp(
    lambda x: (_two_level_sums(x), None),
    lambda _, g: (_sel_dot(_two_level_matrix(), jnp.concatenate(g, axis=0), "tn"),))


def _tile_matrix():
    s = lax.broadcasted_iota(jnp.int32, (HG_SUB, HG_CHUNK), 0)
    j = lax.broadcasted_iota(jnp.int32, (HG_SUB, HG_CHUNK), 1)
    return (j % HG_SUB == s).astype(f32)


@jax.custom_vjp
def _tile_lanes(x):
    return _sel_dot(_tile_matrix(), x, "nn", x_first=True, pieces=1)


_tile_lanes.defvjp(
    lambda x: (_sel_dot(_tile_matrix(), x, "nn", x_first=True, pieces=1), None),
    lambda _, g: (_sel_dot(_tile_matrix(), g, "nt", x_first=True, pieces=2),))


def _block_rows(x):
    k = x.shape[-1]
    return jnp.broadcast_to(x.reshape(HG_NSUB, 1, HG_SUB, k), (HG_NSUB, HG_SUB, HG_SUB, k)).reshape(HG_CHUNK, HG_SUB, k)


def _hgrn_chunk(st0, q_raw, f_raw, i_raw, g_raw, l0, l1, ng):
    C, SUB = HG_CHUNK, HG_SUB
    lb = jax.nn.sigmoid(l0 - l1)
    fg = lb + (1.0 - lb) * jax.nn.sigmoid(f_raw)
    kk = 1.0 - fg
    qf = jax.nn.silu(q_raw)
    al, base = _two_level_cumsum(jnp.log(fg))
    a = al + base
    row = lax.broadcasted_iota(jnp.int32, (C, HEAD), 0)
    a_last = jnp.sum(jnp.where(row == C - 1, a, 0.0), axis=0, keepdims=True)
    inter = _dot_nt(qf * jnp.exp(a), st0)
    qt = qf * jnp.exp(al)
    rb = lax.broadcasted_iota(jnp.int32, (C, C), 0) // SUB
    cb = lax.broadcasted_iota(jnp.int32, (C, C), 1) // SUB
    scores = jnp.zeros((C, C), f32)
    for i in range(1, HG_NSUB):
        base_i = jnp.sum(jnp.where(row == i * SUB, base, 0.0), axis=0, keepdims=True)
        kt = kk * jnp.exp(jnp.minimum(base_i - a, 0.0))
        scores = scores + jnp.where((rb == i) & (cb < i), _dot_nt(qt, kt), 0.0)
    t_i = lax.broadcasted_iota(jnp.int32, (C, SUB, HEAD), 0) % SUB
    s_i = lax.broadcasted_iota(jnp.int32, (C, SUB, HEAD), 1)
    decay = jnp.exp(jnp.where(s_i <= t_i, al[:, None, :] - _block_rows(al), -jnp.inf))
    diag = jnp.sum(qf[:, None, :] * decay * _block_rows(kk), axis=-1)
    scores = scores + jnp.where(rb == cb, _tile_lanes(diag), 0.0)
    o = inter + _dot_nn(scores, i_raw)
    st1 = jnp.exp(a_last) * st0 + _dot_tn(i_raw, kk * jnp.exp(a_last - a))
    on = o * lax.rsqrt(jnp.mean(o * o, axis=-1, keepdims=True) + EPS) * ng
    return st1, on * jax.nn.silu(g_raw)


def _hgrn_specs(S, Bl, rev):
    N = S // HG_CHUNK
    chunk = (lambda n: N - 1 - n) if rev else (lambda n: n)
    col = lambda c0: pl.BlockSpec((Bl, HG_CHUNK, 512), lambda n: (0, chunk(n), c0 // 512))
    st = pl.BlockSpec((Bl, N_HEAD, 1, HEAD, HEAD), lambda n: (0, 0, chunk(n), 0, 0))
    full = lambda *s: pl.BlockSpec(s, functools.partial(lambda n, nd: (0,) * nd, nd=len(s)))
    return N, col, st, full


def _hgrn_fwd(proj, lb_logits, ng, Bl, S):
    N, col, st, full = _hgrn_specs(S, Bl, False)

    def body(q_ref, f_ref, i_ref, g_ref, l_ref, ng_ref, o_ref, st_ref, state):
        @pl.when(pl.program_id(0) == 0)
        def _():
            state[...] = jnp.zeros_like(state)

        for b in range(Bl):
            for h in range(N_HEAD):
                sl = slice(h * HEAD, (h + 1) * HEAD)
                st0 = state[b, h]
                st_ref[b, h, 0] = st0
                st1, out = _hgrn_chunk(st0, *[r[b, :, sl].astype(f32) for r in (q_ref, f_ref, i_ref, g_ref)],
                                       l_ref[0:1, sl], l_ref[1:2, sl], ng_ref[...])
                state[b, h] = st1
                o_ref[b, :, sl] = out.astype(bf16)

    return _call(body, "hgrn_fwd", (N,), [col(C_HQ), col(C_HF), col(C_HI), col(C_HG), full(2, 512), full(1, HEAD)],
                 (col(0), st),
                 (jax.ShapeDtypeStruct((Bl, S, 512), bf16), jax.ShapeDtypeStruct((Bl, N_HEAD, N, HEAD, HEAD), f32)),
                 scratch=[pltpu.VMEM((Bl, N_HEAD, HEAD, HEAD), f32)], sem=("arbitrary",))(
        proj, proj, proj, proj, lb_logits, ng)


def _hgrn_bwd(proj, lb_logits, ng, states, db, Bl, S):
    N, col, st, full = _hgrn_specs(S, Bl, True)

    def body(q_ref, f_ref, i_ref, g_ref, l_ref, ng_ref, st_ref, db_ref,
             dq_ref, df_ref, di_ref, dg_ref, dl_ref, dng_ref, dstate):
        @pl.when(pl.program_id(0) == 0)
        def _():
            dstate[...] = jnp.zeros_like(dstate)
            dl_ref[...] = jnp.zeros_like(dl_ref)
            dng_ref[...] = jnp.zeros_like(dng_ref)

        for b in range(Bl):
            for h in range(N_HEAD):
                sl = slice(h * HEAD, (h + 1) * HEAD)
                _, vjp = jax.vjp(_hgrn_chunk, st_ref[b, h, 0], *[r[b, :, sl].astype(f32) for r in (q_ref, f_ref, i_ref, g_ref)],
                                 l_ref[0:1, sl], l_ref[1:2, sl], ng_ref[...])
                dst0, dq, df, di, dg, dl0, dl1, dng = vjp((dstate[b, h], db_ref[b, :, sl].astype(f32)))
                dstate[b, h] = dst0
                dq_ref[b, :, sl] = dq.astype(bf16)
                df_ref[b, :, sl] = df.astype(bf16)
                di_ref[b, :, sl] = di.astype(bf16)
                dg_ref[b, :, sl] = dg.astype(bf16)
                dl_ref[0:1, sl] += dl0
                dl_ref[1:2, sl] += dl1
                dng_ref[b, h] += dng

    return _call(body, "hgrn_bwd", (N,),
                 [col(C_HQ), col(C_HF), col(C_HI), col(C_HG), full(2, 512), full(1, HEAD), st, col(0)],
                 (*[col(0)] * 4, full(2, 512), full(Bl, N_HEAD, 1, HEAD)),
                 (*[jax.ShapeDtypeStruct((Bl, S, 512), bf16)] * 4, jax.ShapeDtypeStruct((2, 512), f32),
                  jax.ShapeDtypeStruct((Bl, N_HEAD, 1, HEAD), f32)),
                 scratch=[pltpu.VMEM((Bl, N_HEAD, HEAD, HEAD), f32)], sem=("arbitrary",))(
        proj, proj, proj, proj, lb_logits, ng, states, db)


def _attn_probs(q, k):
    s = _raw_dot(q, k, "nt") * (HEAD ** -0.5)
    e = jnp.exp(s - jnp.max(s, axis=-1, keepdims=True))
    return e / jnp.sum(e, axis=-1, keepdims=True)


def _attn_specs(S, tq):
    nq = S // tq
    q = pl.BlockSpec((tq, 512), lambda b, i: (b * nq + i, C_XQ // 512))
    kv = pl.BlockSpec((1, MEM_LEN, 1024), lambda b, i: (b, 0, 0))
    o = pl.BlockSpec((tq, 512), lambda b, i: (b * nq + i, 0))
    return nq, q, kv, o


def _attn_fwd(proj, kv, Bl, S):
    tq = _pick(S, (512, 256, 128))
    nq, qs, kvs, os_ = _attn_specs(S, tq)

    def body(q_ref, kv_ref, o_ref):
        for h in range(N_HEAD):
            sl = slice(h * HEAD, (h + 1) * HEAD)
            p = _attn_probs(q_ref[:, sl], kv_ref[0, :, sl])
            o_ref[:, sl] = _raw_dot(p, kv_ref[0, :, 512 + h * HEAD:512 + (h + 1) * HEAD], "nn").astype(bf16)

    return _call(body, "attn_fwd", (Bl, nq), [qs, kvs], os_, jax.ShapeDtypeStruct((Bl * S, 512), bf16),
                 sem=("parallel", "parallel"))(proj, kv)


def _attn_bwd(proj, kv, dc, Bl, S):
    tq = _pick(S, (512, 256, 128))
    nq, qs, kvs, os_ = _attn_specs(S, tq)

    def body(q_ref, kv_ref, do_ref, dq_ref, dkv_ref):
        @pl.when(pl.program_id(1) == 0)
        def _():
            dkv_ref[...] = jnp.zeros_like(dkv_ref)

        for h in range(N_HEAD):
            sl = slice(h * HEAD, (h + 1) * HEAD)
            vsl = slice(512 + h * HEAD, 512 + (h + 1) * HEAD)
            q, k, v, do = q_ref[:, sl], kv_ref[0, :, sl], kv_ref[0, :, vsl], do_ref[:, sl]
            p = _attn_probs(q, k)
            dkv_ref[0, :, vsl] += _raw_dot(p, do, "tn")
            dp = _raw_dot(do, v, "nt")
            ds = p * (dp - jnp.sum(dp * p, axis=-1, keepdims=True)) * (HEAD ** -0.5)
            dq_ref[:, sl] = _raw_dot(ds, k, "nn").astype(bf16)
            dkv_ref[0, :, sl] += _raw_dot(ds, q, "tn")

    return _call(body, "attn_bwd", (Bl, nq), [qs, kvs, os_], (os_, kvs),
                 (jax.ShapeDtypeStruct((Bl * S, 512), bf16), jax.ShapeDtypeStruct((Bl, MEM_LEN, 1024), f32)),
                 sem=("arbitrary", "arbitrary"))(proj, kv, dc)


def _merge_specs(tm, tn):
    br = pl.BlockSpec((tm, 512), lambda i, j: (i, 0))
    w = pl.BlockSpec((512, tn), lambda i, j: (0, j))
    gl = [pl.BlockSpec((tm, tn), functools.partial(lambda i, j, n: (i, (C_GL + n * D_MODEL) // tn + j), n=n)) for n in range(3)]
    return [br, br, br, w, w, w, *gl]


def _merge_fwd(branches, wb, proj):
    T = proj.shape[0]
    tm, tn = _pick(T, (1024, 512, 256, 128)), 512

    def body(a_ref, b_ref, c_ref, w0, w1, w2, g0, g1, g2, o_ref):
        acc = jnp.zeros((tm, tn), f32)
        for x_ref, w_ref, g_ref in ((a_ref, w0, g0), (b_ref, w1, g1), (c_ref, w2, g2)):
            acc = acc + jax.nn.sigmoid(g_ref[...].astype(f32)) * _raw_dot(x_ref[...], w_ref[...], "nn")
        o_ref[...] = acc.astype(bf16)

    return _call(body, "merge_fwd", (T // tm, D_MODEL // tn), _merge_specs(tm, tn), pl.BlockSpec((tm, tn), lambda i, j: (i, j)),
                 jax.ShapeDtypeStruct((T, D_MODEL), bf16), sem=("parallel", "parallel"))(*branches, *wb, proj, proj, proj)


def _merge_bwd(branches, wb, proj, dmerged):
    T = proj.shape[0]
    tm = _pick(T, (256, 128))
    half = D_MODEL // 2

    def body(a_ref, b_ref, c_ref, w0, w1, w2, g0a, g0b, g1a, g1b, g2a, g2b, dm_ref, dgl_ref, d0, d1, d2, gw_ref):
        @pl.when(pl.program_id(0) == 0)
        def _():
            gw_ref[...] = jnp.zeros_like(gw_ref)

        dm = dm_ref[...]
        for n, (x_ref, w_ref, ga, gb, d_ref) in enumerate(((a_ref, w0, g0a, g0b, d0), (b_ref, w1, g1a, g1b, d1), (c_ref, w2, g2a, g2b, d2))):
            x, w = x_ref[...], w_ref[...]
            up = _raw_dot(x, w, "nn")
            sg = jax.nn.sigmoid(jnp.concatenate([ga[...], gb[...]], axis=-1).astype(f32))
            dgl_ref[n] = (dm * up * sg * (1.0 - sg)).astype(bf16)
            dup = (dm * sg).astype(bf16)
            d_ref[...] = _raw_dot(dup, w, "nt").astype(bf16)
            gw_ref[n] += _raw_dot(x, dup, "tn")

    br = pl.BlockSpec((tm, 512), lambda i: (i, 0))
    w = pl.BlockSpec((512, D_MODEL), lambda i: (0, 0))
    gl = [pl.BlockSpec((tm, half), functools.partial(lambda i, c: (i, c), c=(C_GL + n * D_MODEL) // half + k)) for n in range(3) for k in range(2)]
    sh = jax.ShapeDtypeStruct((T, 512), bf16)
    outs = _call(body, "merge_bwd", (T // tm,), [br, br, br, w, w, w, *gl, pl.BlockSpec((tm, D_MODEL), lambda i: (i, 0))],
                 (pl.BlockSpec((3, tm, D_MODEL), lambda i: (0, i, 0)), br, br, br, pl.BlockSpec((3, 512, D_MODEL), lambda i: (0, 0, 0))),
                 (jax.ShapeDtypeStruct((3, T, D_MODEL), bf16), sh, sh, sh, jax.ShapeDtypeStruct((3, 512, D_MODEL), f32)),
                 sem=("arbitrary",))(*branches, *wb, *[proj] * 6, dmerged)
    return outs[0], outs[1:4], outs[4]


CONV_TC = 256


def _shift_down(a, k):
    r = pltpu.roll(a, k, 0)
    row = lax.broadcasted_iota(jnp.int32, (8, a.shape[1]), 0)
    return jnp.concatenate([jnp.where(row >= k, r[:8], 0.0), r[8:]], axis=0)


def _shift_up(a, k):
    n = a.shape[0]
    r = pltpu.roll(a, n - k, 0)
    row = lax.broadcasted_iota(jnp.int32, (8, a.shape[1]), 0)
    return jnp.concatenate([r[:n - 8], jnp.where(row < 8 - k, r[n - 8:], 0.0)], axis=0)


def _conv_pre(a, a1, a2, cw, cb):
    return cb + cw[0:1] * a2 + cw[1:2] * a1 + cw[2:3] * a


def _up_conv_fwd(h2, w_up, cw, cb):
    Bl, S, Dd = h2.shape
    nc = D_FF // CONV_TC

    def body(h_ref, wa_ref, wb_ref, cw_ref, cb_ref, a_ref, b_ref, o_ref):
        a16 = _raw_dot(h_ref[0], wa_ref[...], "nn").astype(bf16)
        b16 = _raw_dot(h_ref[0], wb_ref[...], "nn").astype(bf16)
        a_ref[0], b_ref[0] = a16, b16
        a = a16.astype(f32)
        ac = _conv_pre(a, _shift_down(a, 1), _shift_down(a, 2), cw_ref[...], cb_ref[...])
        o_ref[0] = (jax.nn.silu(ac) * b16.astype(f32)).astype(bf16)

    seq = pl.BlockSpec((1, S, CONV_TC), lambda b, c: (b, 0, c))
    sh = jax.ShapeDtypeStruct((Bl, S, D_FF), bf16)
    return _call(body, "up_conv_fwd", (Bl, nc),
                 [pl.BlockSpec((1, S, Dd), lambda b, c: (b, 0, 0)), pl.BlockSpec((Dd, CONV_TC), lambda b, c: (0, c)),
                  pl.BlockSpec((Dd, CONV_TC), lambda b, c: (0, nc + c)), pl.BlockSpec((3, CONV_TC), lambda b, c: (0, c)),
                  pl.BlockSpec((1, CONV_TC), lambda b, c: (0, c))],
                 (seq, seq, seq), (sh, sh, sh), sem=("parallel", "parallel"))(h2, w_up, w_up, cw, cb)


def _down_conv_bwd(dx2, w_down, a, b, cw, cb):
    Bl, S, Dd = dx2.shape
    nc = D_FF // CONV_TC

    def body(dx_ref, wd_ref, a_ref, b_ref, cw_ref, cb_ref, da_ref, db_ref, dcw_ref, dcb_ref):
        dact = _raw_dot(dx_ref[0], wd_ref[...], "nt").astype(bf16).astype(f32)
        a, cw = a_ref[0].astype(f32), cw_ref[...]
        a1, a2 = _shift_down(a, 1), _shift_down(a, 2)
        ac = _conv_pre(a, a1, a2, cw, cb_ref[...])
        sg = jax.nn.sigmoid(ac)
        gated = dact * sg
        db_ref[0] = (gated * ac).astype(bf16)
        dac = gated * b_ref[0].astype(f32) * (1.0 + ac * (1.0 - sg))
        da_ref[0] = (cw[2:3] * dac + cw[1:2] * _shift_up(dac, 1) + cw[0:1] * _shift_up(dac, 2)).astype(bf16)
        dcw_ref[0, 0:1, :] = jnp.sum(dac * a2, axis=0, keepdims=True)
        dcw_ref[0, 1:2, :] = jnp.sum(dac * a1, axis=0, keepdims=True)
        dcw_ref[0, 2:3, :] = jnp.sum(dac * a, axis=0, keepdims=True)
        dcb_ref[0] = jnp.sum(dac, axis=0, keepdims=True)

    seq = pl.BlockSpec((1, S, CONV_TC), lambda b_, c: (b_, 0, c))
    sh = jax.ShapeDtypeStruct((Bl, S, D_FF), bf16)
    return _call(body, "down_conv_bwd", (Bl, nc),
                 [pl.BlockSpec((1, S, Dd), lambda b_, c: (b_, 0, 0)), pl.BlockSpec((CONV_TC, Dd), lambda b_, c: (c, 0)), seq, seq,
                  pl.BlockSpec((3, CONV_TC), lambda b_, c: (0, c)), pl.BlockSpec((1, CONV_TC), lambda b_, c: (0, c))],
                 (seq, seq, pl.BlockSpec((1, 3, CONV_TC), lambda b_, c: (b_, 0, c)), pl.BlockSpec((1, 1, CONV_TC), lambda b_, c: (b_, 0, c))),
                 (sh, sh, jax.ShapeDtypeStruct((Bl, 3, D_FF), f32), jax.ShapeDtypeStruct((Bl, 1, D_FF), f32)),
                 sem=("parallel", "parallel"))(dx2, w_down, a, b, cw, cb)


def _local_step(x, mem, target, p, w_in, late_b, late_c, send, settle):
    Bl, S, Dd = x.shape
    T = Bl * S
    x2d, t2d, mem2d = x.reshape(T, Dd), target.reshape(T, Dd), mem.reshape(Bl * MEM_LEN, Dd)
    b_st = jnp.pad(p["b_spatial"].T, ((0, 0), (0, 128 - N_HEAD)))
    lbl = p["lb_logits"]

    h, h_t = _rms_fwd(x2d, p["norm1_g"], "norm1_fwd", transposed=True)
    proj = _mm(h, w_in, "nn", bf16, "proj_fwd", 1024, 1664)
    a_out = _gmlp_fwd(proj, p["ln_v_g"], p["ln_v_b"], p["w_spatial"], b_st)
    proj3 = proj.reshape(Bl, S, IN_WIDTH)
    b_out, states = _hgrn_fwd(proj3, lbl, p["hgrn_norm_g"], Bl, S)
    b_out = b_out.reshape(T, 512)
    memn = _rms_fwd(mem2d, p["mem_norm_g"], "memnorm_fwd")
    w = late_b(b_out)
    wb = w["w_branch"]
    kv = _mm(memn, w["w_mem_kv"], "nn", f32, "kv_fwd", 512, 1024).reshape(Bl, MEM_LEN, 2 * 512)
    c_out = _attn_fwd(proj, kv, Bl, S)
    branches = (a_out, b_out, c_out)
    merged = _merge_fwd(branches, wb, proj)
    x1 = _mm(merged, w["w_out"], "nn", f32, "out_fwd", 1024, 1024, residual=x2d)
    h2, h2_t = _rms_fwd(x1, p["norm2_g"], "norm2_fwd", transposed=True)
    w.update(late_c(h2))
    ffn_a, ffn_b, act = _up_conv_fwd(h2.reshape(Bl, S, Dd), w["w_up"], w["conv_w"], p["conv_b"])
    act = act.reshape(T, D_FF)
    x2 = _mm(act, w["w_down"], "nn", f32, "down_fwd", 512, 1024, residual=x1)
    loss_part, dx2, dx2_16, g_final = _final_loss(x2, p["final_g"], t2d)

    g_w_down = _mm(act, dx2_16, "tn", bf16, "down_dw", 1408, 1024, 1024)
    da, db, g_conv_w, g_conv_b = _down_conv_bwd(dx2_16.reshape(Bl, S, Dd), w["w_down"], ffn_a, ffn_b, w["conv_w"], p["conv_b"])
    da, db = da.reshape(T, D_FF), db.reshape(T, D_FF)
    shard = 2 * D_FF // N_DEV
    g_w_up = jnp.concatenate([_mm(h2_t, d, "nn", bf16, f"up_dw_{n}", 512, 1408, shard=shard, n_outer=True)
                              for n, d in (("a", da), ("b", db))], axis=0)
    tok = send("c", dict(w_up=g_w_up, conv_w=jnp.sum(g_conv_w, axis=0), w_down=g_w_down))
    dh2 = _mm(da, w["w_up_a"], "nt", f32, "up_dx_a", 512, 1024)
    dh2 = _mm(db, w["w_up_b"], "nt", f32, "up_dx_b", 512, 1024, residual=dh2)
    dx1, g_norm2 = _rms_bwd(x1, p["norm2_g"] + tok[0, 0], dh2, "norm2_bwd", residual=dx2)

    g_w_out = _mm(merged, dx1, "tn", bf16, "out_dw", 1024, 1024, 1024)
    dmerged = _mm(dx1, w["w_out"], "nt", f32, "out_dx", 1024, 1024)
    dgl, dbr, g_w_branch = _merge_bwd(branches, wb, proj, dmerged)
    dxq, dkv = _attn_bwd(proj, kv, dbr[2], Bl, S)
    dkv = dkv.reshape(Bl * MEM_LEN, 2 * 512)
    g_w_kv = _mm(memn, dkv, "tn", bf16, "kv_dw", 1024, 1024, 512)
    tok = send("b", dict(w_mem_kv=g_w_kv, w_branch=g_w_branch, w_out=g_w_out))
    dmemn = _mm(dkv, w["w_mem_kv"], "nt", f32, "kv_dx", 512, 1024)
    _, g_mem_norm = _rms_bwd(mem2d, p["mem_norm_g"], dmemn, "memnorm_bwd")
    dzuv, g_ln_g, g_ln_b, g_w_sp, g_b_sp = _gmlp_bwd(proj, p["ln_v_g"] + tok[0, 0], p["ln_v_b"], p["w_spatial"], b_st, dbr[0])
    *dqfig, g_lbl, g_ng = _hgrn_bwd(proj3, lbl, p["hgrn_norm_g"], states, dbr[1].reshape(Bl, S, 512), Bl, S)
    dq, df, di, dg = [d.reshape(T, 512) for d in dqfig]
    dproj = jnp.concatenate([dzuv, dq, df, di, dg, dxq, dgl[0], dgl[1], dgl[2]], axis=-1)
    g_w_in = _mm(h_t, dproj, "nn", bf16, "proj_dw", 512, 1664, shard=IN_WIDTH // N_DEV, n_outer=True)
    tok = send("a", dict(w_in=g_w_in))
    dh = _mm(settle(dproj), w_in, "nt", f32, "proj_dx", 512, 1024)
    dx, g_norm1 = _rms_bwd(x2d, p["norm1_g"] + tok[0, 0], dh, "norm1_bwd", residual=dx1)

    gs = dict(w_spatial=g_w_sp, norm1_g=g_norm1, mem_norm_g=g_mem_norm, norm2_g=g_norm2, final_g=g_final, lb_logits=g_lbl,
              ln_v_g=g_ln_g, ln_v_b=g_ln_b, b_spatial=g_b_sp, hgrn_norm_g=g_ng, conv_b=g_conv_b)
    return loss_part, dx.reshape(Bl, S, Dd), gs


def _coords():
    return lax.axis_index("x"), lax.axis_index("y"), lax.axis_index("c")


def _slot(dev):
    return 4 * dev[0] + 2 * dev[1] + dev[2]


def _comm_call(body, name, arrays, out_shapes, n_sem):
    n = len(arrays)
    hbm = pl.BlockSpec(memory_space=pl.ANY)
    return pl.pallas_call(
        body, name=name, out_shape=out_shapes, in_specs=[hbm] * n, out_specs=[hbm] * n,
        scratch_shapes=[pltpu.SemaphoreType.DMA((n_sem, n)), pltpu.SemaphoreType.DMA((n_sem, n)), pltpu.SemaphoreType.DMA((n,))])(*arrays)


def _all_gather(blocks, name):
    n = len(blocks)

    def body(*refs):
        x_refs, o_refs, (send_sems, recv_sems, local_sems) = refs[:n], refs[n:2 * n], refs[2 * n:]
        x, y, c = _coords()
        me, sibling = (x, y, c), (x, y, 1 - c)
        chips = [(1 - x, y), (x, 1 - y), (1 - x, 1 - y)]

        def copy(a, k, block_dev, to, from_input=False):
            dst = o_refs[a].at[_slot(block_dev)]
            return pltpu.make_async_remote_copy(src_ref=x_refs[a] if from_input else dst, dst_ref=dst, send_sem=send_sems.at[k, a],
                                                recv_sem=recv_sems.at[k, a], device_id=to, device_id_type=MESH)

        mine = [pltpu.make_async_copy(x_refs[a], o_refs[a].at[_slot(me)], local_sems.at[a]) for a in range(n)]
        first = [copy(a, 0, me, sibling, True) for a in range(n)]
        first += [copy(a, 1 + j, me, (*chip, c), True) for j, chip in enumerate(chips) for a in range(n)]
        for cp in mine + first:
            cp.start()
        passed = []
        for j, chip in enumerate(chips):
            for a in range(n):
                copy(a, 1 + j, (*chip, c), me).wait_recv()
                fwd = copy(a, 4 + j, (*chip, c), sibling)
                fwd.start()
                passed.append(fwd)
        for a in range(n):
            copy(a, 0, sibling, me).wait_recv()
        for j, chip in enumerate(chips):
            for a in range(n):
                copy(a, 4 + j, (*chip, 1 - c), me).wait_recv()
        for cp in first + passed:
            cp.wait_send()
        for cp in mine:
            cp.wait()

    return _comm_call(body, name, blocks, [jax.ShapeDtypeStruct((N_DEV,) + b.shape, b.dtype) for b in blocks], 7)


def _all_to_all(parts, name):
    n = len(parts)
    rel = [(0, 0, 1), (0, 1, 0), (0, 1, 1), (1, 0, 0), (1, 0, 1), (1, 1, 0), (1, 1, 1)]

    def body(*refs):
        x_refs, o_refs, (send_sems, recv_sems, local_sems) = refs[:n], refs[n:2 * n], refs[2 * n:]
        x, y, c = _coords()
        me = (x, y, c)
        peers = [(x ^ dx, y ^ dy, c ^ dc) for dx, dy, dc in rel]

        def copy(a, k, peer):
            return pltpu.make_async_remote_copy(src_ref=x_refs[a].at[_slot(peer)], dst_ref=o_refs[a].at[_slot(me)], send_sem=send_sems.at[k, a],
                                                recv_sem=recv_sems.at[k, a], device_id=peer, device_id_type=MESH)

        def arrival(a, k, peer):
            return pltpu.make_async_remote_copy(src_ref=x_refs[a].at[_slot(me)], dst_ref=o_refs[a].at[_slot(peer)], send_sem=send_sems.at[k, a],
                                                recv_sem=recv_sems.at[k, a], device_id=peer, device_id_type=MESH)

        mine = [pltpu.make_async_copy(x_refs[a].at[_slot(me)], o_refs[a].at[_slot(me)], local_sems.at[a]) for a in range(n)]
        sends = [copy(a, k, peer) for k, peer in enumerate(peers) for a in range(n)]
        for cp in mine + sends:
            cp.start()
        for k, peer in enumerate(peers):
            for a in range(n):
                arrival(a, k, peer).wait_recv()
        for cp in sends:
            cp.wait_send()
        for cp in mine:
            cp.wait()

    return _comm_call(body, name, parts, [jax.ShapeDtypeStruct(p.shape, p.dtype) for p in parts], 7)


_HBM = pl.BlockSpec(memory_space=pltpu.HBM)
_SEM = pl.BlockSpec(memory_space=pltpu.SEMAPHORE)
_REL = [(0, 0, 1), (0, 1, 0), (0, 1, 1), (1, 0, 0), (1, 0, 1), (1, 1, 0), (1, 1, 1)]


_LINK_ORDER = (3, 1, 5, 4, 2, 6, 0)
SEND_PIECES = 4


def _pieces(shape, dtype):
    rows = shape[0]
    unit = 1 if len(shape) > 2 else (16 if dtype == bf16 else 8)
    for n in (SEND_PIECES, 2):
        if rows % (n * unit) == 0:
            return [pl.ds(i * (rows // n), rows // n) for i in range(n)]
    return [pl.ds(0, rows)]


def _split_copies(gather, src, land, send, recv, pieces):
    x, y, c = _coords()
    me = (x, y, c)
    copies = []
    for a in range(len(src)):
        block = src[a].shape if gather else src[a].shape[1:]
        for rows in (_pieces(block, src[a].dtype) if pieces else [None]):
            for k in _LINK_ORDER:
                dx, dy, dc = _REL[k]
                peer = (x ^ dx, y ^ dy, c ^ dc)
                mine, there = (src[a] if gather else src[a].at[_slot(peer)]), land[a].at[_slot(me)]
                if rows is not None:
                    mine, there = mine.at[rows], there.at[rows]
                copies.append(pltpu.make_async_remote_copy(src_ref=mine, dst_ref=there, send_sem=send[a].at[k], recv_sem=recv[a].at[k],
                                                           device_id=peer, device_id_type=MESH))
    return me, copies


def _arrivals(gather, src, land, send, recv):
    x, y, c = _coords()
    out = []
    for a in range(len(src)):
        for k, (dx, dy, dc) in enumerate(_REL):
            peer = (x ^ dx, y ^ dy, c ^ dc)
            out.append(pltpu.make_async_remote_copy(src_ref=src[a] if gather else src[a].at[_slot(peer)], dst_ref=land[a].at[_slot(peer)],
                                                    send_sem=send[a].at[k], recv_sem=recv[a].at[k], device_id=peer, device_id_type=MESH))
    return out


def _exchange_start(arrays, gather, name, after=None):
    n = len(arrays)
    e = 0 if after is None else 1
    lands = [lax.empty(((N_DEV,) + a.shape) if gather else a.shape, a.dtype) for a in arrays]

    def body(*refs):
        src, land = refs[:n], refs[n:2 * n]
        refs = refs[2 * n + e:]
        send, recv, token, local_sems = refs[:n], refs[n:2 * n], refs[4 * n], refs[4 * n + 1]
        me, out = _split_copies(gather, src, land, send, recv, True)
        local = [pltpu.make_async_copy(src[a] if gather else src[a].at[_slot(me)], land[a].at[_slot(me)], local_sems.at[a])
                 for a in range(n)]
        for cp in local:
            cp.start()
        for cp in local:
            cp.wait()
        for cp in out:
            cp.start()
        token[...] = jnp.zeros_like(token)

    sems = [pltpu.SemaphoreType.DMA((7,)) for _ in range(2 * n)]
    outs = pl.pallas_call(
        body, name=name,
        out_shape=(*sems, *[pltpu.HBM(a.shape, a.dtype) for a in arrays], *[pltpu.HBM(l.shape, l.dtype) for l in lands],
                   jax.ShapeDtypeStruct((8, 128), f32)),
        in_specs=[_HBM] * (2 * n) + [pl.BlockSpec(memory_space=pl.ANY)] * e,
        out_specs=(*[_SEM] * (2 * n), *[_HBM] * (2 * n), pl.BlockSpec(memory_space=pltpu.VMEM)),
        input_output_aliases={i: 2 * n + i for i in range(2 * n)},
        scratch_shapes=[pltpu.SemaphoreType.DMA((n,))],
        compiler_params=pltpu.CompilerParams(has_side_effects=pltpu.SideEffectType.DATAFLOW_SIDE_EFFECTING))(
        *[pltpu.with_memory_space_constraint(a, pltpu.HBM) for a in arrays],
        *[pltpu.with_memory_space_constraint(l, pltpu.HBM) for l in lands], *([after] if e else []))
    return (gather, n, outs[:4 * n]), outs[4 * n]


def _exchange_wait(handle, which, after, name):
    gather, n_all, vals = handle
    send_v, recv_v, src_v, land_v = [[vals[g * n_all + i] for i in which] for g in range(4)]
    n = len(which)

    def body(*refs):
        src, land, send, recv = refs[:n], refs[n:2 * n], refs[2 * n:3 * n], refs[3 * n:4 * n]
        for cp in _split_copies(gather, src, land, send, recv, False)[1]:
            cp.wait_send()
        for cp in _arrivals(gather, src, land, send, recv):
            cp.wait_recv()

    outs = pl.pallas_call(
        body, name=name,
        out_shape=(*[pltpu.HBM(a.shape, a.dtype) for a in src_v], *[pltpu.HBM(l.shape, l.dtype) for l in land_v]),
        in_specs=[*[_HBM] * (2 * n), *[_SEM] * (2 * n), pl.BlockSpec(memory_space=pl.ANY)], out_specs=[_HBM] * (2 * n),
        input_output_aliases={i: i for i in range(2 * n)},
        compiler_params=pltpu.CompilerParams(has_side_effects=pltpu.SideEffectType.DATAFLOW_SIDE_EFFECTING))(
        *src_v, *land_v, *send_v, *recv_v, after)
    return outs[n:]


def _seq_exchange(arrays, gather, name, collective_id):
    n = len(arrays)
    hbm = pltpu.MemorySpace.HBM
    srcs = [jax.new_ref(a, memory_space=hbm) for a in arrays]
    lands = [jax.empty_ref(jax.ShapeDtypeStruct(((N_DEV,) + a.shape) if gather else a.shape, a.dtype), memory_space=hbm) for a in arrays]

    @pl.kernel(mesh=plsc.ScalarSubcoreMesh(axis_name="sequencer", num_cores=1), name=name,
               scratch_types=(pltpu.SemaphoreType.DMA((7, n)), pltpu.SemaphoreType.DMA((7, n)), pltpu.SemaphoreType.DMA((n,))),
               compiler_params=pltpu.CompilerParams(collective_id=collective_id))
    def launch(send, recv, local):
        x, y, c = _coords()
        me = (x, y, c)
        peers = [(x ^ dx, y ^ dy, c ^ dc) for dx, dy, dc in _REL]
        barrier = pltpu.get_barrier_semaphore()
        for peer in peers:
            pl.semaphore_signal(barrier, inc=1, device_id=peer, device_id_type=MESH)
        pl.semaphore_wait(barrier, len(peers))

        def copy(a, k, peer, arrival):
            return pltpu.make_async_remote_copy(
                src_ref=srcs[a] if gather else srcs[a].at[_slot(peer)], dst_ref=lands[a].at[_slot(peer if arrival else me)],
                send_sem=send.at[k, a], recv_sem=recv.at[k, a], device_id=peer, device_id_type=MESH)

        mine = [pltpu.make_async_copy(srcs[a] if gather else srcs[a].at[_slot(me)], lands[a].at[_slot(me)], local.at[a])
                for a in range(n)]
        out = [copy(a, k, peer, False) for a in range(n) for k, peer in enumerate(peers)]
        for cp in mine + out:
            cp.start()
        for a in range(n):
            for k, peer in enumerate(peers):
                copy(a, k, peer, True).wait_recv()
        for cp in out:
            cp.wait_send()
        for cp in mine:
            cp.wait()

    launch()
    return [land[...] for land in lands]


def _adam_math(w, g, m, v):
    m_ = ADAM_B1 * m + (1.0 - ADAM_B1) * g
    v_ = ADAM_B2 * v + (1.0 - ADAM_B2) * jnp.square(g)
    m_hat = m_ / (1.0 - ADAM_B1 ** ADAM_STEP)
    v_hat = v_ / (1.0 - ADAM_B2 ** ADAM_STEP)
    return -ADAM_LR * (m_hat / (jnp.sqrt(v_hat) + ADAM_EPS) + ADAM_WD * w), m_, v_


def _reduce_adamw(parts, w, m, v, name):
    _, R, L = parts.shape
    tr = _pick(R, (256, 128, 64, 32, 16, 8))

    def body(p_ref, w_ref, m_ref, v_ref, g_ref, d_ref, nm_ref, nv_ref):
        g = p_ref[0].astype(f32)
        for i in range(1, N_DEV):
            g = g + p_ref[i].astype(f32)
        g_ref[...] = g
        d_ref[...], nm_ref[...], nv_ref[...] = _adam_math(w_ref[...], g, m_ref[...], v_ref[...])

    blk = pl.BlockSpec((tr, L), lambda i: (i, 0))
    sh = jax.ShapeDtypeStruct((R, L), f32)
    return _call(body, name, (R // tr,), [pl.BlockSpec((N_DEV, tr, L), lambda i: (0, i, 0)), blk, blk, blk], (blk,) * 4, (sh,) * 4,
                 sem=("parallel",))(parts, w, m, v)


SMALL = (("w_spatial", (512, 128), 0), ("norm1_g", (1, 1024), 512), ("mem_norm_g", (1, 1024), 520), ("norm2_g", (1, 1024), 528),
         ("final_g", (1, 1024), 536), ("lb_logits", (2, 512), 544), ("ln_v_g", (1, 512), 552), ("ln_v_b", (1, 512), 556),
         ("b_spatial", (4, 128), 560), ("hgrn_norm_g", (1, 128), 564), ("conv_b", (1, 2816), 565))
LOSS_ROW, SMALL_USED, SMALL_ROWS = 587, 588, 640


def _segments(shape, base):
    r, n = shape
    per = n // 128
    return [(base + i * per + j, i, slice(j * 128, (j + 1) * 128)) for i in range(r) for j in range(per)]


def _pack_small(gs, loss_part):
    names = [n for n, _, _ in SMALL]

    def body(*refs):
        src, loss_ref, o_ref = dict(zip(names, refs[:-2])), refs[-2], refs[-1]
        o_ref[SMALL_USED:SMALL_ROWS, :] = jnp.zeros((SMALL_ROWS - SMALL_USED, 128), f32)
        o_ref[LOSS_ROW:LOSS_ROW + 1, :] = loss_ref[...]
        for name, shape, base in SMALL:
            ref = src[name]
            if name == "w_spatial":
                o_ref[base:base + 512, :] = ref[...].reshape(512, 128)
            elif name == "b_spatial":
                o_ref[base:base + 4, :] = ref[0:4, :]
            elif name == "conv_b":
                per_example = functools.reduce(lambda u, v_: u + v_, [ref[b] for b in range(ref.shape[0])])
                for row, i, sl in _segments(shape, base):
                    o_ref[row:row + 1, :] = per_example[i:i + 1, sl]
            elif name == "hgrn_norm_g":
                per_head = [ref[b, h] for b in range(ref.shape[0]) for h in range(N_HEAD)]
                o_ref[base:base + 1, :] = functools.reduce(lambda u, v_: u + v_, per_head)
            else:
                for row, i, sl in _segments(shape, base):
                    o_ref[row:row + 1, :] = ref[i:i + 1, sl]

    return pl.pallas_call(body, name="pack_small", out_shape=jax.ShapeDtypeStruct((SMALL_ROWS, 128), f32))(
        *[gs[n] for n in names], loss_part)


def _small_update(gathered, w, m, v):
    names = [n for n, _, _ in SMALL]
    k = len(names)

    def body(*refs):
        p_ref = refs[0]
        ins = [dict(zip(names, refs[1 + i * k:1 + (i + 1) * k])) for i in range(3)]
        outs = [dict(zip(names, refs[1 + (3 + i) * k:1 + (4 + i) * k])) for i in range(4)]
        loss_ref, gsum = refs[-2], refs[-1]
        g = p_ref[0]
        for i in range(1, N_DEV):
            g = g + p_ref[i]
        gsum[...] = g
        loss_ref[...] = gsum[LOSS_ROW:LOSS_ROW + 1, :]
        for name, shape, base in SMALL:
            if name == "w_spatial":
                where = [(slice(base, base + 512), (slice(None), slice(None)))]
            else:
                where = [(slice(row, row + 1), (slice(i, i + 1), sl)) for row, i, sl in _segments(shape, base)]
            for rows, at in where:
                g_ = gsum[rows, :]
                d_, m_, v_ = _adam_math(ins[0][name][at], g_, ins[1][name][at], ins[2][name][at])
                for o, val in zip(outs, (g_, d_, m_, v_)):
                    o[name][at] = val

    args = [gathered] + [d[n] for d in (w, m, v) for n in names]
    out_shapes = [jax.ShapeDtypeStruct(shape, f32) for _ in range(4) for _, shape, _ in SMALL] + [jax.ShapeDtypeStruct((1, 128), f32)]
    outs = pl.pallas_call(body, name="small_update", out_shape=out_shapes, scratch_shapes=[pltpu.VMEM((SMALL_ROWS, 128), f32)])(*args)
    return [dict(zip(names, outs[i * k:(i + 1) * k])) for i in range(4)], outs[-1]


def _cols_full(g):
    return jnp.moveaxis(g, 0, -2).reshape(g.shape[1:-1] + (N_DEV * g.shape[-1],))


def _cols_parts(full):
    n = full.shape[-1] // N_DEV
    return jnp.moveaxis(full.reshape(full.shape[:-1] + (N_DEV, n)), -2, 0)


def kernel(x, mem, norm1_g, w_in, ln_v_g, ln_v_b, w_spatial, b_spatial, lb_logits, hgrn_norm_g, mem_norm_g, w_mem_kv, w_branch, w_out, norm2_g, w_up, conv_w, conv_b, w_down, final_g, loss_target, m_norm1_g, m_w_in, m_ln_v_g, m_ln_v_b, m_w_spatial, m_b_spatial, m_lb_logits, m_hgrn_norm_g, m_mem_norm_g, m_w_mem_kv, m_w_branch, m_w_out, m_norm2_g, m_w_up, m_conv_w, m_conv_b, m_w_down, m_final_g, v_norm1_g, v_w_in, v_ln_v_g, v_ln_v_b, v_w_spatial, v_b_spatial, v_lb_logits, v_hgrn_norm_g, v_mem_norm_g, v_w_mem_kv, v_w_branch, v_w_out, v_norm2_g, v_w_up, v_conv_w, v_conv_b, v_w_down, v_final_g):
    given = dict(locals())
    order = ("norm1_g", "w_in", "ln_v_g", "ln_v_b", "w_spatial", "b_spatial", "lb_logits", "hgrn_norm_g", "mem_norm_g",
             "w_mem_kv", "w_branch", "w_out", "norm2_g", "w_up", "conv_w", "conv_b", "w_down", "final_g")
    groups = dict(a=("w_in",), b=("w_mem_kv", "w_branch", "w_out"), c=("w_up", "conv_w", "w_down"))

    wire = {n: given[n][0].astype(f32 if n == "conv_w" else bf16) for ns in groups.values() for n in ns}
    g_in = _all_gather([wire["w_in"]], "gather_w_in")[0]
    late = groups["b"] + groups["c"]
    w_in_full = _cols_full(g_in)
    w_in_full, rest_wire = lax.optimization_barrier((w_in_full, [wire[n] for n in late]))
    rest = _seq_exchange(rest_wire, True, "gather_rest", 1)

    def late_b(after):
        _, (kv_, br_, out_) = lax.optimization_barrier((after, tuple(rest[0:3])))
        br_ = _cols_full(br_)
        return dict(w_mem_kv=kv_.reshape(D_MODEL, 2 * 512), w_branch=[br_[n] for n in range(3)], w_out=out_.reshape(D_MODEL, D_MODEL))

    def late_c(after):
        _, (up_, cw_, down_) = lax.optimization_barrier((after, tuple(rest[3:6])))
        up_ = _cols_full(up_)
        return dict(w_up=up_, w_up_a=up_[:, :D_FF], w_up_b=up_[:, D_FF:], conv_w=_cols_full(cw_), w_down=down_.reshape(D_FF, D_MODEL))

    to_parts = dict(w_in=lambda g_: g_, w_up=lambda g_: g_, conv_w=_cols_parts,
                    w_branch=lambda g_: _cols_parts(g_.astype(bf16)).reshape(N_DEV, -1, 128),
                    w_mem_kv=lambda g_: g_.reshape(N_DEV, -1, 2 * 512), w_out=lambda g_: g_.reshape(N_DEV, -1, D_MODEL),
                    w_down=lambda g_: g_.reshape(N_DEV, -1, D_MODEL))
    scatters = {}

    def send(tag, grads_):
        parts = [to_parts[n](grads_[n]) for n in groups[tag]]
        scatters[tag] = _seq_exchange(parts, False, f"scatter_{tag}", dict(a=2, b=4, c=5)[tag])
        return jnp.zeros((8, 128), f32)

    small_2d = lambda prefix: {n: given[prefix + n].reshape(shape) for n, shape, _ in SMALL}
    p = small_2d("")
    p["w_spatial"] = w_spatial[0]
    updates = {}

    def update(tag):
        for n, parts in zip(groups[tag], scatters[tag]):
            two_d = (-1, given[n].shape[-1])
            updates[n] = _reduce_adamw(parts, *[given[pre + n].reshape(two_d) for pre in ("", "m_", "v_")], "adamw_" + n)

    def settle(chain):
        update("c")
        update("b")
        early = groups["c"] + groups["b"]
        chain, tied = lax.optimization_barrier((chain, [updates[n] for n in early]))
        updates.update(zip(early, tied))
        return chain

    loss_part, grad_x, gs = _local_step(x, mem, loss_target, p, w_in_full, late_b, late_c, send, settle)

    gathered = _seq_exchange([_pack_small(gs, loss_part)], True, "gather_small", 3)[0]

    update("a")
    grads, delta, new_m, new_v = {}, {}, {}, {}
    for n, res in updates.items():
        grads[n], delta[n], new_m[n], new_v[n] = [r.reshape(given[n].shape) for r in res]

    small_results, loss_row = _small_update(gathered, small_2d(""), small_2d("m_"), small_2d("v_"))
    for dst, res in zip((grads, delta, new_m, new_v), small_results):
        for n, _, _ in SMALL:
            dst[n] = res[n].reshape(given[n].shape)
    loss = loss_row[0, 0]

    return (loss, grad_x, *[grads[n] for n in order], *[delta[n] for n in order], *[new_m[n] for n in order],
            *[new_v[n] for n in order])
```

```python
import functools

import jax
import jax.numpy as jnp
from jax import lax
from jax.experimental import pallas as pl
from jax.experimental.pallas import tpu as pltpu
from jax.experimental.pallas import tpu_sc as plsc

f32 = jnp.float32
bf16 = jnp.bfloat16

N_DEV = 8
D_MODEL = 1024
EPS = 1e-6
GM_CHUNK = 128
HG_CHUNK = 64
HEAD = 128
N_HEAD = 4
MEM_LEN = 256
D_FF = 2816
IN_WIDTH = 6656
C_ZU, C_HQ, C_HF, C_HI, C_HG, C_XQ, C_GL = 0, 1024, 1536, 2048, 2560, 3072, 3584
ADAM_LR, ADAM_B1, ADAM_B2, ADAM_EPS, ADAM_WD, ADAM_STEP = 0.001, 0.9, 0.999, 1e-08, 0.01, 10
VMEM_LIMIT = 56 * 1024 * 1024
MESH = pl.DeviceIdType.MESH


def _pick(n, cands):
    for c in cands:
        if n % c == 0:
            return c
    return n


def _call(body, name, grid, in_specs, out_specs, out_shape, scratch=(), sem=None, **cp):
    params = dict(vmem_limit_bytes=VMEM_LIMIT, **cp)
    if sem is not None:
        params["dimension_semantics"] = sem
    return pl.pallas_call(
        body, name=name, grid=grid, in_specs=in_specs, out_specs=out_specs, out_shape=out_shape,
        scratch_shapes=list(scratch), compiler_params=pltpu.CompilerParams(**params))


_DN = {"nn": (((1,), (0,)), ((), ())), "nt": (((1,), (1,)), ((), ())), "tn": (((0,), (0,)), ((), ()))}


def _raw_dot(a, b, mode):
    return lax.dot_general(a.astype(bf16), b.astype(bf16), _DN[mode], preferred_element_type=f32)


@jax.custom_vjp
def _dot_nn(a, b):
    return _raw_dot(a, b, "nn")


_dot_nn.defvjp(lambda a, b: (_raw_dot(a, b, "nn"), (a, b)),
               lambda r, g: (_raw_dot(g, r[1], "nt"), _raw_dot(r[0], g, "tn")))


@jax.custom_vjp
def _dot_nt(a, b):
    return _raw_dot(a, b, "nt")


_dot_nt.defvjp(lambda a, b: (_raw_dot(a, b, "nt"), (a, b)),
               lambda r, g: (_raw_dot(g, r[1], "nn"), _raw_dot(g, r[0], "tn")))


@jax.custom_vjp
def _dot_tn(a, b):
    return _raw_dot(a, b, "tn")


_dot_tn.defvjp(lambda a, b: (_raw_dot(a, b, "tn"), (a, b)),
               lambda r, g: (_raw_dot(r[1], g, "nt"), _raw_dot(r[0], g, "nn")))


def _tri(n, lower):
    r = lax.broadcasted_iota(jnp.int32, (n, n), 0)
    c = lax.broadcasted_iota(jnp.int32, (n, n), 1)
    return ((c <= r) if lower else (c >= r)).astype(f32)


def _sel_dot(sel, x, mode, x_first=False, pieces=3):
    sel = sel.astype(bf16)
    out, rest = None, x
    for p in range(pieces):
        piece = rest.astype(bf16)
        part = lax.dot_general(*((piece, sel) if x_first else (sel, piece)), _DN[mode], preferred_element_type=f32)
        out = part if out is None else out + part
        if p + 1 < pieces:
            rest = rest - piece.astype(f32)
    return out


def _egrad(fn, x, ct):
    return jax.vjp(fn, x)[1](ct)[0]


def _mm(a, b, mode, out_dtype, name, tm, tn, tk=None, residual=None, shard=None, n_outer=False):
    if mode == "nn":
        (M, K), (_, N) = a.shape, b.shape
    elif mode == "nt":
        (M, K), (N, _) = a.shape, b.shape
    else:
        (K, M), (_, N) = a.shape, b.shape
    tm, tn = min(tm, M), min(tn, N)
    tk = K if tk is None else min(tk, K)
    assert M % tm == 0 and N % tn == 0 and K % tk == 0, (name, M, N, K, tm, tn, tk)
    nk = K // tk

    def body(*refs):
        acc_ref = refs[-1] if nk > 1 else None
        refs = refs[:-1] if nk > 1 else refs
        if residual is None:
            a_ref, b_ref, o_ref = refs
        else:
            a_ref, b_ref, r_ref, o_ref = refs

        def finish(r):
            if residual is not None:
                r = r + r_ref[...]
            if shard is None:
                o_ref[...] = r.astype(out_dtype)
            else:
                for s in range(tn // shard):
                    o_ref[s] = r[:, s * shard:(s + 1) * shard].astype(out_dtype)

        part = _raw_dot(a_ref[...], b_ref[...], mode)
        if nk == 1:
            finish(part)
            return
        k = pl.program_id(2)

        @pl.when(k == 0)
        def _():
            acc_ref[...] = part

        @pl.when((k > 0) & (k < nk - 1))
        def _():
            acc_ref[...] += part

        @pl.when(k == nk - 1)
        def _():
            finish(acc_ref[...] + part)

    def at(index):
        return (lambda j, i, k: index(i, j, k)) if n_outer else index

    a_spec = {"nn": pl.BlockSpec((tm, tk), at(lambda i, j, k: (i, k))),
              "nt": pl.BlockSpec((tm, tk), at(lambda i, j, k: (i, k))),
              "tn": pl.BlockSpec((tk, tm), at(lambda i, j, k: (k, i)))}[mode]
    b_spec = {"nn": pl.BlockSpec((tk, tn), at(lambda i, j, k: (k, j))),
              "nt": pl.BlockSpec((tn, tk), at(lambda i, j, k: (j, k))),
              "tn": pl.BlockSpec((tk, tn), at(lambda i, j, k: (k, j)))}[mode]
    o_spec = pl.BlockSpec((tm, tn), at(lambda i, j, k: (i, j)))
    in_specs = [a_spec, b_spec] + ([o_spec] if residual is not None else [])
    args = (a, b) + ((residual,) if residual is not None else ())
    out_shape = jax.ShapeDtypeStruct((M, N), out_dtype)
    if shard is not None:
        assert residual is None and tn % shard == 0
        o_spec = pl.BlockSpec((tn // shard, tm, shard), at(lambda i, j, k: (j, i, 0)))
        out_shape = jax.ShapeDtypeStruct((N // shard, M, shard), out_dtype)
    grid = (N // tn, M // tm, nk) if n_outer else (M // tm, N // tn, nk)
    return _call(body, name, grid, in_specs, o_spec, out_shape,
                 scratch=[pltpu.VMEM((tm, tn), f32)] if nk > 1 else [], sem=("parallel", "parallel", "arbitrary"))(*args)


def _rms_fwd(x, g, name, transposed=False):
    R, Dd = x.shape
    tr = _pick(R, (512, 256, 128))

    def body(x_ref, g_ref, o_ref, *t_ref):
        xf = x_ref[...]
        y = xf * lax.rsqrt(jnp.mean(xf * xf, axis=-1, keepdims=True) + EPS) * g_ref[...]
        o_ref[...] = y.astype(bf16)
        if transposed:
            t_ref[0][...] = y.T.astype(bf16)

    row = pl.BlockSpec((tr, Dd), lambda i: (i, 0))
    out_specs, out_shape = row, jax.ShapeDtypeStruct((R, Dd), bf16)
    if transposed:
        out_specs, out_shape = (row, pl.BlockSpec((Dd, tr), lambda i: (0, i))), (out_shape, jax.ShapeDtypeStruct((Dd, R), bf16))
    return _call(body, name, (R // tr,), [row, pl.BlockSpec((1, Dd), lambda i: (0, 0))], out_specs, out_shape, sem=("parallel",))(x, g)


def _rms_bwd(x, g, dh, name, residual=None):
    R, Dd = x.shape
    tr = _pick(R, (512, 256, 128))

    def body(*refs):
        if residual is None:
            x_ref, g_ref, dh_ref, dx_ref, dg_ref = refs
        else:
            x_ref, g_ref, dh_ref, r_ref, dx_ref, dg_ref = refs
        xf = x_ref[...]
        rs = lax.rsqrt(jnp.mean(xf * xf, axis=-1, keepdims=True) + EPS)
        y = xf * rs
        dh_ = dh_ref[...].astype(f32)
        dy = dh_ * g_ref[...]
        dx = rs * (dy - y * jnp.mean(dy * y, axis=-1, keepdims=True))
        if residual is not None:
            dx = dx + r_ref[...]
        dx_ref[...] = dx

        @pl.when(pl.program_id(0) == 0)
        def _():
            dg_ref[...] = jnp.zeros_like(dg_ref)

        dg_ref[...] += jnp.sum(dh_ * y, axis=0, keepdims=True)

    row = pl.BlockSpec((tr, Dd), lambda i: (i, 0))
    vec = pl.BlockSpec((1, Dd), lambda i: (0, 0))
    in_specs = [row, vec, row] + ([row] if residual is not None else [])
    args = (x, g, dh) + ((residual,) if residual is not None else ())
    return _call(body, name, (R // tr,), in_specs, (row, vec),
                 (jax.ShapeDtypeStruct((R, Dd), f32), jax.ShapeDtypeStruct((1, Dd), f32)), sem=("arbitrary",))(*args)


def _mm_rms_bwd(a, b, x, g, residual, name, tm, pre=None):
    M, _ = a.shape
    Dd = b.shape[0]
    tm = min(tm, M)

    def body(*refs):
        if pre is None:
            a_ref, b_ref, x_ref, g_ref, r_ref, dx_ref, dg_ref = refs
        else:
            a_ref, b_ref, p_ref, x_ref, g_ref, r_ref, dx_ref, dg_ref = refs
        dh_ = _raw_dot(a_ref[...], b_ref[...], "nt")
        if pre is not None:
            dh_ = dh_ + p_ref[...]
        xf = x_ref[...]
        rs = lax.rsqrt(jnp.mean(xf * xf, axis=-1, keepdims=True) + EPS)
        y = xf * rs
        dy = dh_ * g_ref[...]
        dx_ref[...] = rs * (dy - y * jnp.mean(dy * y, axis=-1, keepdims=True)) + r_ref[...]

        @pl.when(pl.program_id(0) == 0)
        def _():
            dg_ref[...] = jnp.zeros_like(dg_ref)

        dg_ref[...] += jnp.sum(dh_ * y, axis=0, keepdims=True)

    row = pl.BlockSpec((tm, Dd), lambda i: (i, 0))
    vec = pl.BlockSpec((1, Dd), lambda i: (0, 0))
    in_specs = [pl.BlockSpec((tm, a.shape[1]), lambda i: (i, 0)), pl.BlockSpec(b.shape, lambda i: (0, 0))]
    in_specs += ([row] if pre is not None else []) + [row, vec, row]
    args = (a, b) + ((pre,) if pre is not None else ()) + (x, g, residual)
    return _call(body, name, (M // tm,), in_specs, (row, vec),
                 (jax.ShapeDtypeStruct((M, Dd), f32), jax.ShapeDtypeStruct((1, Dd), f32)), sem=("arbitrary",))(*args)


def _down_final_loss(act, w_down, x1, g, target):
    R, Dd = x1.shape
    tr = _pick(R, (512, 256, 128))

    def body(a_ref, w_ref, x1_ref, g_ref, t_ref, loss_ref, dx_ref, dxb_ref, dg_ref):
        xf = _raw_dot(a_ref[...], w_ref[...], "nn") + x1_ref[...]
        rs = lax.rsqrt(jnp.mean(xf * xf, axis=-1, keepdims=True) + EPS)
        y = xf * rs
        err = y * g_ref[...] - t_ref[...]
        dh_ = err * (1.0 / Dd)
        dy = dh_ * g_ref[...]
        dx = rs * (dy - y * jnp.mean(dy * y, axis=-1, keepdims=True))
        dx_ref[...] = dx
        dxb_ref[...] = dx.astype(bf16)

        @pl.when(pl.program_id(0) == 0)
        def _():
            dg_ref[...] = jnp.zeros_like(dg_ref)
            loss_ref[...] = jnp.zeros_like(loss_ref)

        dg_ref[...] += jnp.sum(dh_ * y, axis=0, keepdims=True)
        part = jnp.sum(jnp.mean(err * err, axis=-1, keepdims=True), axis=0, keepdims=True)
        loss_ref[...] += 0.5 * part

    row = pl.BlockSpec((tr, Dd), lambda i: (i, 0))
    vec = pl.BlockSpec((1, Dd), lambda i: (0, 0))
    in_specs = [pl.BlockSpec((tr, act.shape[1]), lambda i: (i, 0)), pl.BlockSpec(w_down.shape, lambda i: (0, 0)), row, vec, row]
    return _call(body, "down_final_loss", (R // tr,), in_specs, (pl.BlockSpec((1, 128), lambda i: (0, 0)), row, row, vec),
                 (jax.ShapeDtypeStruct((1, 128), f32), jax.ShapeDtypeStruct((R, Dd), f32), jax.ShapeDtypeStruct((R, Dd), bf16),
                  jax.ShapeDtypeStruct((1, Dd), f32)), sem=("arbitrary",))(act, w_down, x1, g, target)


def _gmlp_parts(zuv, ln_g, ln_b):
    zu, zv = zuv[:, :512], zuv[:, 512:]
    u = jax.nn.gelu(zu)
    v = jax.nn.gelu(zv)
    mu = jnp.mean(v, axis=-1, keepdims=True)
    rs = lax.rsqrt(jnp.mean(jnp.square(v - mu), axis=-1, keepdims=True) + EPS)
    xh = (v - mu) * rs
    return zu, zv, u, xh, rs, xh * ln_g + ln_b


GM_TILE_CHUNKS = 4


def _gmlp_tile(T):
    n = _pick(T // GM_CHUNK, (GM_TILE_CHUNKS, 2, 1))
    return n, n * GM_CHUNK


def _gmlp_fwd(proj, ln_g, ln_b, w_s, b_st):
    T = proj.shape[0]
    nch, rows = _gmlp_tile(T)

    def body(p_ref, g_ref, b_ref, w_ref, bs_ref, o_ref):
        _, _, u, _, _, vn = _gmlp_parts(p_ref[...].astype(f32), g_ref[...], b_ref[...])
        causal = _tri(GM_CHUNK, True) > 0
        for gi in range(N_HEAD):
            sl = slice(gi * HEAD, (gi + 1) * HEAD)
            w = jnp.where(causal, w_ref[gi], 0.0)
            for ch in range(nch):
                rs_ = slice(ch * GM_CHUNK, (ch + 1) * GM_CHUNK)
                mixed = _raw_dot(w, vn[rs_, sl], "nn") + bs_ref[:, gi:gi + 1]
                o_ref[rs_, sl] = (u[rs_, sl] * mixed).astype(bf16)

    vec = pl.BlockSpec((1, 512), lambda i: (0, 0))
    return _call(body, "gmlp_fwd", (T // rows,),
                 [pl.BlockSpec((rows, 1024), lambda i: (i, 0)), vec, vec,
                  pl.BlockSpec((N_HEAD, GM_CHUNK, GM_CHUNK), lambda i: (0, 0, 0)), pl.BlockSpec((GM_CHUNK, 128), lambda i: (0, 0))],
                 pl.BlockSpec((rows, 512), lambda i: (i, 0)), jax.ShapeDtypeStruct((T, 512), bf16), sem=("parallel",))(
        proj, ln_g, ln_b, w_s, b_st)


def _gmlp_bwd(proj, ln_g, ln_b, w_s, b_st, da):
    T = proj.shape[0]
    nch, rows = _gmlp_tile(T)

    def body(p_ref, g_ref, b_ref, w_ref, bs_ref, da_ref, dp_ref, dg_ref, db_ref, dw_ref, dbs_ref):
        zu, zv, u, xh, rs, vn = _gmlp_parts(p_ref[...].astype(f32), g_ref[...], b_ref[...])
        causal = _tri(GM_CHUNK, True) > 0
        sub = lax.broadcasted_iota(jnp.int32, (8, GM_CHUNK), 0)
        ones = jnp.ones((8, HEAD), f32)
        dout = da_ref[...].astype(f32)

        @pl.when(pl.program_id(0) == 0)
        def _():
            for r in (dg_ref, db_ref, dw_ref, dbs_ref):
                r[...] = jnp.zeros_like(r)

        du, dvn, dbs = [], [], jnp.zeros((8, GM_CHUNK), f32)
        for gi in range(N_HEAD):
            sl = slice(gi * HEAD, (gi + 1) * HEAD)
            w = jnp.where(causal, w_ref[gi], 0.0)
            du_g, dvn_g, dw_g = [], [], jnp.zeros((GM_CHUNK, GM_CHUNK), f32)
            for ch in range(nch):
                rs_ = slice(ch * GM_CHUNK, (ch + 1) * GM_CHUNK)
                mixed = _raw_dot(w, vn[rs_, sl], "nn") + bs_ref[:, gi:gi + 1]
                du_g.append(dout[rs_, sl] * mixed)
                dm = dout[rs_, sl] * u[rs_, sl]
                dbs = dbs + jnp.where(sub == gi, _sel_dot(ones, dm, "nt"), 0.0)
                dw_g = dw_g + _raw_dot(dm, vn[rs_, sl], "nt")
                dvn_g.append(_raw_dot(w, dm, "tn"))
            dw_ref[gi] += jnp.where(causal, dw_g, 0.0)
            du.append(jnp.concatenate(du_g, axis=0))
            dvn.append(jnp.concatenate(dvn_g, axis=0))
        dbs_ref[...] += dbs
        du = jnp.concatenate(du, axis=-1)
        dvn = jnp.concatenate(dvn, axis=-1)
        dg_ref[...] += jnp.sum(dvn * xh, axis=0, keepdims=True)
        db_ref[...] += jnp.sum(dvn, axis=0, keepdims=True)
        dxh = dvn * g_ref[...]
        dv = rs * (dxh - jnp.mean(dxh, axis=-1, keepdims=True) - xh * jnp.mean(dxh * xh, axis=-1, keepdims=True))
        dp_ref[:, :512] = _egrad(jax.nn.gelu, zu, du).astype(bf16)
        dp_ref[:, 512:] = _egrad(jax.nn.gelu, zv, dv).astype(bf16)

    vec = pl.BlockSpec((1, 512), lambda i: (0, 0))
    wsp = pl.BlockSpec((N_HEAD, GM_CHUNK, GM_CHUNK), lambda i: (0, 0, 0))
    return _call(body, "gmlp_bwd", (T // rows,),
                 [pl.BlockSpec((rows, 1024), lambda i: (i, 0)), vec, vec, wsp, pl.BlockSpec((GM_CHUNK, 128), lambda i: (0, 0)),
                  pl.BlockSpec((rows, 512), lambda i: (i, 0))],
                 (pl.BlockSpec((rows, 1024), lambda i: (i, 0)), vec, vec, wsp, pl.BlockSpec((8, GM_CHUNK), lambda i: (0, 0))),
                 (jax.ShapeDtypeStruct((T, 1024), bf16), jax.ShapeDtypeStruct((1, 512), f32), jax.ShapeDtypeStruct((1, 512), f32),
                  jax.ShapeDtypeStruct((N_HEAD, GM_CHUNK, GM_CHUNK), f32), jax.ShapeDtypeStruct((8, GM_CHUNK), f32)),
                 sem=("arbitrary",))(proj, ln_g, ln_b, w_s, b_st, da)


HG_SUB = 8
HG_NSUB = HG_CHUNK // HG_SUB


def _two_level_matrix(transposed=False):
    shape = (HG_CHUNK, 2 * HG_CHUNK) if transposed else (2 * HG_CHUNK, HG_CHUNK)
    r = lax.broadcasted_iota(jnp.int32, shape, 1 if transposed else 0)
    c = lax.broadcasted_iota(jnp.int32, shape, 0 if transposed else 1)
    t = jnp.where(r < HG_CHUNK, r, r - HG_CHUNK)
    local = (r < HG_CHUNK) & (t // HG_SUB == c // HG_SUB) & (c <= t)
    before = (r >= HG_CHUNK) & (c < (t // HG_SUB) * HG_SUB)
    return (local | before).astype(f32)


def _two_level_sums(x):
    two = _sel_dot(_two_level_matrix(), x, "nn")
    return two[:HG_CHUNK], two[HG_CHUNK:]


@jax.custom_vjp
def _two_level_cumsum(x):
    return _two_level_sums(x)


_two_level_cumsum.defvjp(
    lambda x: (_two_level_sums(x), None),
    lambda _, g: (_sel_dot(_two_level_matrix(), jnp.concatenate(g, axis=0), "tn"),))


def _tile_matrix():
    s = lax.broadcasted_iota(jnp.int32, (HG_SUB, HG_CHUNK), 0)
    j = lax.broadcasted_iota(jnp.int32, (HG_SUB, HG_CHUNK), 1)
    return (j % HG_SUB == s).astype(f32)


@jax.custom_vjp
def _tile_lanes(x):
    return _sel_dot(_tile_matrix(), x, "nn", x_first=True, pieces=1)


_tile_lanes.defvjp(
    lambda x: (_sel_dot(_tile_matrix(), x, "nn", x_first=True, pieces=1), None),
    lambda _, g: (_sel_dot(_tile_matrix(), g, "nt", x_first=True, pieces=2),))


def _block_rows(x):
    k = x.shape[-1]
    return jnp.broadcast_to(x.reshape(HG_NSUB, 1, HG_SUB, k), (HG_NSUB, HG_SUB, HG_SUB, k)).reshape(HG_CHUNK, HG_SUB, k)


def _hgrn_chunk(st0, q_raw, f_raw, i_raw, g_raw, l0, l1, ng):
    C, SUB = HG_CHUNK, HG_SUB
    lb = jax.nn.sigmoid(l0 - l1)
    fg = lb + (1.0 - lb) * jax.nn.sigmoid(f_raw)
    kk = 1.0 - fg
    qf = jax.nn.silu(q_raw)
    al, base = _two_level_cumsum(jnp.log(fg))
    a = al + base
    row = lax.broadcasted_iota(jnp.int32, (C, HEAD), 0)
    a_last = jnp.sum(jnp.where(row == C - 1, a, 0.0), axis=0, keepdims=True)
    inter = _dot_nt(qf * jnp.exp(a), st0)
    qt = qf * jnp.exp(al)
    rb = lax.broadcasted_iota(jnp.int32, (C, C), 0) // SUB
    cb = lax.broadcasted_iota(jnp.int32, (C, C), 1) // SUB
    scores = jnp.zeros((C, C), f32)
    for i in range(1, HG_NSUB):
        base_i = jnp.sum(jnp.where(row == i * SUB, base, 0.0), axis=0, keepdims=True)
        kt = kk * jnp.exp(jnp.minimum(base_i - a, 0.0))
        scores = scores + jnp.where((rb == i) & (cb < i), _dot_nt(qt, kt), 0.0)
    t_i = lax.broadcasted_iota(jnp.int32, (C, SUB, HEAD), 0) % SUB
    s_i = lax.broadcasted_iota(jnp.int32, (C, SUB, HEAD), 1)
    decay = jnp.exp(jnp.where(s_i <= t_i, al[:, None, :] - _block_rows(al), -jnp.inf))
    diag = jnp.sum(qf[:, None, :] * decay * _block_rows(kk), axis=-1)
    scores = scores + jnp.where(rb == cb, _tile_lanes(diag), 0.0)
    o = inter + _dot_nn(scores, i_raw)
    st1 = jnp.exp(a_last) * st0 + _dot_tn(i_raw, kk * jnp.exp(a_last - a))
    on = o * lax.rsqrt(jnp.mean(o * o, axis=-1, keepdims=True) + EPS) * ng
    return st1, on * jax.nn.silu(g_raw)


def _hgrn_specs(S, Bl, rev):
    N = S // HG_CHUNK
    chunk = (lambda n: N - 1 - n) if rev else (lambda n: n)
    col = lambda c0: pl.BlockSpec((Bl, HG_CHUNK, 512), lambda n: (0, chunk(n), c0 // 512))
    st = pl.BlockSpec((Bl, N_HEAD, 1, HEAD, HEAD), lambda n: (0, 0, chunk(n), 0, 0))
    full = lambda *s: pl.BlockSpec(s, functools.partial(lambda n, nd: (0,) * nd, nd=len(s)))
    return N, col, st, full


def _hgrn_fwd(proj, lb_logits, ng, Bl, S):
    N, col, st, full = _hgrn_specs(S, Bl, False)

    def body(q_ref, f_ref, i_ref, g_ref, l_ref, ng_ref, o_ref, st_ref, state):
        @pl.when(pl.program_id(0) == 0)
        def _():
            state[...] = jnp.zeros_like(state)

        for b in range(Bl):
            for h in range(N_HEAD):
                sl = slice(h * HEAD, (h + 1) * HEAD)
                st0 = state[b, h]
                st_ref[b, h, 0] = st0
                st1, out = _hgrn_chunk(st0, *[r[b, :, sl].astype(f32) for r in (q_ref, f_ref, i_ref, g_ref)],
                                       l_ref[0:1, sl], l_ref[1:2, sl], ng_ref[...])
                state[b, h] = st1
                o_ref[b, :, sl] = out.astype(bf16)

    return _call(body, "hgrn_fwd", (N,), [col(C_HQ), col(C_HF), col(C_HI), col(C_HG), full(2, 512), full(1, HEAD)],
                 (col(0), st),
                 (jax.ShapeDtypeStruct((Bl, S, 512), bf16), jax.ShapeDtypeStruct((Bl, N_HEAD, N, HEAD, HEAD), f32)),
                 scratch=[pltpu.VMEM((Bl, N_HEAD, HEAD, HEAD), f32)], sem=("arbitrary",))(
        proj, proj, proj, proj, lb_logits, ng)


def _hgrn_bwd(proj, lb_logits, ng, states, db, Bl, S):
    N, col, st, full = _hgrn_specs(S, Bl, True)

    def body(q_ref, f_ref, i_ref, g_ref, l_ref, ng_ref, st_ref, db_ref,
             dq_ref, df_ref, di_ref, dg_ref, dl_ref, dng_ref, dstate):
        @pl.when(pl.program_id(0) == 0)
        def _():
            dstate[...] = jnp.zeros_like(dstate)
            dl_ref[...] = jnp.zeros_like(dl_ref)
            dng_ref[...] = jnp.zeros_like(dng_ref)

        for b in range(Bl):
            for h in range(N_HEAD):
                sl = slice(h * HEAD, (h + 1) * HEAD)
                _, vjp = jax.vjp(_hgrn_chunk, st_ref[b, h, 0], *[r[b, :, sl].astype(f32) for r in (q_ref, f_ref, i_ref, g_ref)],
                                 l_ref[0:1, sl], l_ref[1:2, sl], ng_ref[...])
                dst0, dq, df, di, dg, dl0, dl1, dng = vjp((dstate[b, h], db_ref[b, :, sl].astype(f32)))
                dstate[b, h] = dst0
                dq_ref[b, :, sl] = dq.astype(bf16)
                df_ref[b, :, sl] = df.astype(bf16)
                di_ref[b, :, sl] = di.astype(bf16)
                dg_ref[b, :, sl] = dg.astype(bf16)
                dl_ref[0:1, sl] += dl0
                dl_ref[1:2, sl] += dl1
                dng_ref[b, h] += dng

    return _call(body, "hgrn_bwd", (N,),
                 [col(C_HQ), col(C_HF), col(C_HI), col(C_HG), full(2, 512), full(1, HEAD), st, col(0)],
                 (*[col(0)] * 4, full(2, 512), full(Bl, N_HEAD, 1, HEAD)),
                 (*[jax.ShapeDtypeStruct((Bl, S, 512), bf16)] * 4, jax.ShapeDtypeStruct((2, 512), f32),
                  jax.ShapeDtypeStruct((Bl, N_HEAD, 1, HEAD), f32)),
                 scratch=[pltpu.VMEM((Bl, N_HEAD, HEAD, HEAD), f32)], sem=("arbitrary",))(
        proj, proj, proj, proj, lb_logits, ng, states, db)


def _attn_probs(q, k):
    s = _raw_dot(q, k, "nt") * (HEAD ** -0.5)
    e = jnp.exp(s - jnp.max(s, axis=-1, keepdims=True))
    return e / jnp.sum(e, axis=-1, keepdims=True)


def _attn_specs(S, tq):
    nq = S // tq
    q = pl.BlockSpec((tq, 512), lambda b, i: (b * nq + i, C_XQ // 512))
    kv = pl.BlockSpec((1, MEM_LEN, 1024), lambda b, i: (b, 0, 0))
    o = pl.BlockSpec((tq, 512), lambda b, i: (b * nq + i, 0))
    return nq, q, kv, o


def _attn_fwd(proj, kv, Bl, S):
    tq = _pick(S, (512, 256, 128))
    nq, qs, kvs, os_ = _attn_specs(S, tq)

    def body(q_ref, kv_ref, o_ref):
        for h in range(N_HEAD):
            sl = slice(h * HEAD, (h + 1) * HEAD)
            p = _attn_probs(q_ref[:, sl], kv_ref[0, :, sl])
            o_ref[:, sl] = _raw_dot(p, kv_ref[0, :, 512 + h * HEAD:512 + (h + 1) * HEAD], "nn").astype(bf16)

    return _call(body, "attn_fwd", (Bl, nq), [qs, kvs], os_, jax.ShapeDtypeStruct((Bl * S, 512), bf16),
                 sem=("parallel", "parallel"))(proj, kv)


def _attn_bwd(proj, kv, dc, Bl, S):
    tq = _pick(S, (512, 256, 128))
    nq, qs, kvs, os_ = _attn_specs(S, tq)

    def body(q_ref, kv_ref, do_ref, dq_ref, dkv_ref):
        @pl.when(pl.program_id(1) == 0)
        def _():
            dkv_ref[...] = jnp.zeros_like(dkv_ref)

        for h in range(N_HEAD):
            sl = slice(h * HEAD, (h + 1) * HEAD)
            vsl = slice(512 + h * HEAD, 512 + (h + 1) * HEAD)
            q, k, v, do = q_ref[:, sl], kv_ref[0, :, sl], kv_ref[0, :, vsl], do_ref[:, sl]
            p = _attn_probs(q, k)
            dkv_ref[0, :, vsl] += _raw_dot(p, do, "tn")
            dp = _raw_dot(do, v, "nt")
            ds = p * (dp - jnp.sum(dp * p, axis=-1, keepdims=True)) * (HEAD ** -0.5)
            dq_ref[:, sl] = _raw_dot(ds, k, "nn").astype(bf16)
            dkv_ref[0, :, sl] += _raw_dot(ds, q, "tn")

    return _call(body, "attn_bwd", (Bl, nq), [qs, kvs, os_], (os_, kvs),
                 (jax.ShapeDtypeStruct((Bl * S, 512), bf16), jax.ShapeDtypeStruct((Bl, MEM_LEN, 1024), f32)),
                 sem=("arbitrary", "arbitrary"))(proj, kv, dc)


def _gate_specs(tm):
    half = D_MODEL // 2
    return [pl.BlockSpec((tm, half), functools.partial(lambda i, c: (i, c), c=(C_GL + n * D_MODEL) // half + k))
            for n in range(3) for k in range(2)]


def _merge_out_norm_fwd(branches, wb, proj, w_out, x, g):
    T = proj.shape[0]
    tm = _pick(T, (512, 256, 128))

    def body(a_ref, b_ref, c_ref, w0, w1, w2, g0a, g0b, g1a, g1b, g2a, g2b, wo_ref, x_ref, g_ref, m_ref, x1_ref, h_ref, ht_ref):
        acc = jnp.zeros((tm, D_MODEL), f32)
        for x_n, w_ref, ga, gb in ((a_ref, w0, g0a, g0b), (b_ref, w1, g1a, g1b), (c_ref, w2, g2a, g2b)):
            gate = jax.nn.sigmoid(jnp.concatenate([ga[...], gb[...]], axis=-1).astype(f32))
            acc = acc + gate * _raw_dot(x_n[...], w_ref[...], "nn")
        merged = acc.astype(bf16)
        m_ref[...] = merged
        x1 = x_ref[...] + _raw_dot(merged, wo_ref[...], "nn")
        x1_ref[...] = x1
        y = x1 * lax.rsqrt(jnp.mean(x1 * x1, axis=-1, keepdims=True) + EPS) * g_ref[...]
        h_ref[...] = y.astype(bf16)
        ht_ref[...] = y.T.astype(bf16)

    br = pl.BlockSpec((tm, 512), lambda i: (i, 0))
    w = pl.BlockSpec((512, D_MODEL), lambda i: (0, 0))
    row = pl.BlockSpec((tm, D_MODEL), lambda i: (i, 0))
    return _call(body, "merge_out_norm_fwd", (T // tm,),
                 [br, br, br, w, w, w, *_gate_specs(tm), pl.BlockSpec((D_MODEL, D_MODEL), lambda i: (0, 0)), row,
                  pl.BlockSpec((1, D_MODEL), lambda i: (0, 0))],
                 (row, row, row, pl.BlockSpec((D_MODEL, tm), lambda i: (0, i))),
                 (jax.ShapeDtypeStruct((T, D_MODEL), bf16), jax.ShapeDtypeStruct((T, D_MODEL), f32),
                  jax.ShapeDtypeStruct((T, D_MODEL), bf16), jax.ShapeDtypeStruct((D_MODEL, T), bf16)),
                 sem=("parallel",))(*branches, *wb, *[proj] * 6, w_out, x, g)


def _merge_bwd(branches, wb, proj, dmerged):
    T = proj.shape[0]
    tm = _pick(T, (256, 128))

    def body(a_ref, b_ref, c_ref, w0, w1, w2, g0a, g0b, g1a, g1b, g2a, g2b, dm_ref, dgl_ref, d0, d1, d2, gw_ref):
        @pl.when(pl.program_id(0) == 0)
        def _():
            gw_ref[...] = jnp.zeros_like(gw_ref)

        dm = dm_ref[...]
        for n, (x_ref, w_ref, ga, gb, d_ref) in enumerate(((a_ref, w0, g0a, g0b, d0), (b_ref, w1, g1a, g1b, d1), (c_ref, w2, g2a, g2b, d2))):
            x, w = x_ref[...], w_ref[...]
            up = _raw_dot(x, w, "nn")
            sg = jax.nn.sigmoid(jnp.concatenate([ga[...], gb[...]], axis=-1).astype(f32))
            dgl_ref[n] = (dm * up * sg * (1.0 - sg)).astype(bf16)
            dup = (dm * sg).astype(bf16)
            d_ref[...] = _raw_dot(dup, w, "nt").astype(bf16)
            gw_ref[n] += _raw_dot(x, dup, "tn")

    br = pl.BlockSpec((tm, 512), lambda i: (i, 0))
    w = pl.BlockSpec((512, D_MODEL), lambda i: (0, 0))
    sh = jax.ShapeDtypeStruct((T, 512), bf16)
    outs = _call(body, "merge_bwd", (T // tm,), [br, br, br, w, w, w, *_gate_specs(tm), pl.BlockSpec((tm, D_MODEL), lambda i: (i, 0))],
                 (pl.BlockSpec((3, tm, D_MODEL), lambda i: (0, i, 0)), br, br, br, pl.BlockSpec((3, 512, D_MODEL), lambda i: (0, 0, 0))),
                 (jax.ShapeDtypeStruct((3, T, D_MODEL), bf16), sh, sh, sh, jax.ShapeDtypeStruct((3, 512, D_MODEL), f32)),
                 sem=("arbitrary",))(*branches, *wb, *[proj] * 6, dmerged)
    return outs[0], outs[1:4], outs[4]


CONV_TC = 256


def _shift_down(a, k):
    r = pltpu.roll(a, k, 0)
    row = lax.broadcasted_iota(jnp.int32, (8, a.shape[1]), 0)
    return jnp.concatenate([jnp.where(row >= k, r[:8], 0.0), r[8:]], axis=0)


def _shift_up(a, k):
    n = a.shape[0]
    r = pltpu.roll(a, n - k, 0)
    row = lax.broadcasted_iota(jnp.int32, (8, a.shape[1]), 0)
    return jnp.concatenate([r[:n - 8], jnp.where(row < 8 - k, r[n - 8:], 0.0)], axis=0)


def _conv_pre(a, a1, a2, cw, cb):
    return cb + cw[0:1] * a2 + cw[1:2] * a1 + cw[2:3] * a


def _up_conv_fwd(h2, w_up, cw, cb):
    Bl, S, Dd = h2.shape
    nc = D_FF // CONV_TC

    def body(h_ref, wa_ref, wb_ref, cw_ref, cb_ref, a_ref, b_ref, o_ref):
        a16 = _raw_dot(h_ref[0], wa_ref[...], "nn").astype(bf16)
        b16 = _raw_dot(h_ref[0], wb_ref[...], "nn").astype(bf16)
        a_ref[0], b_ref[0] = a16, b16
        a = a16.astype(f32)
        ac = _conv_pre(a, _shift_down(a, 1), _shift_down(a, 2), cw_ref[...], cb_ref[...])
        o_ref[0] = (jax.nn.silu(ac) * b16.astype(f32)).astype(bf16)

    seq = pl.BlockSpec((1, S, CONV_TC), lambda b, c: (b, 0, c))
    sh = jax.ShapeDtypeStruct((Bl, S, D_FF), bf16)
    return _call(body, "up_conv_fwd", (Bl, nc),
                 [pl.BlockSpec((1, S, Dd), lambda b, c: (b, 0, 0)), pl.BlockSpec((Dd, CONV_TC), lambda b, c: (0, c)),
                  pl.BlockSpec((Dd, CONV_TC), lambda b, c: (0, nc + c)), pl.BlockSpec((3, CONV_TC), lambda b, c: (0, c)),
                  pl.BlockSpec((1, CONV_TC), lambda b, c: (0, c))],
                 (seq, seq, seq), (sh, sh, sh), sem=("parallel", "parallel"))(h2, w_up, w_up, cw, cb)


def _down_conv_bwd(dx2, w_down, a, b, cw, cb):
    Bl, S, Dd = dx2.shape
    nc = D_FF // CONV_TC

    def body(dx_ref, wd_ref, a_ref, b_ref, cw_ref, cb_ref, da_ref, db_ref, dcw_ref, dcb_ref):
        dact = _raw_dot(dx_ref[0], wd_ref[...], "nt").astype(bf16).astype(f32)
        a, cw = a_ref[0].astype(f32), cw_ref[...]
        a1, a2 = _shift_down(a, 1), _shift_down(a, 2)
        ac = _conv_pre(a, a1, a2, cw, cb_ref[...])
        sg = jax.nn.sigmoid(ac)
        gated = dact * sg
        db_ref[0] = (gated * ac).astype(bf16)
        dac = gated * b_ref[0].astype(f32) * (1.0 + ac * (1.0 - sg))
        da_ref[0] = (cw[2:3] * dac + cw[1:2] * _shift_up(dac, 1) + cw[0:1] * _shift_up(dac, 2)).astype(bf16)
        dcw_ref[0, 0:1, :] = jnp.sum(dac * a2, axis=0, keepdims=True)
        dcw_ref[0, 1:2, :] = jnp.sum(dac * a1, axis=0, keepdims=True)
        dcw_ref[0, 2:3, :] = jnp.sum(dac * a, axis=0, keepdims=True)
        dcb_ref[0] = jnp.sum(dac, axis=0, keepdims=True)

    seq = pl.BlockSpec((1, S, CONV_TC), lambda b_, c: (b_, 0, c))
    sh = jax.ShapeDtypeStruct((Bl, S, D_FF), bf16)
    return _call(body, "down_conv_bwd", (Bl, nc),
                 [pl.BlockSpec((1, S, Dd), lambda b_, c: (b_, 0, 0)), pl.BlockSpec((CONV_TC, Dd), lambda b_, c: (c, 0)), seq, seq,
                  pl.BlockSpec((3, CONV_TC), lambda b_, c: (0, c)), pl.BlockSpec((1, CONV_TC), lambda b_, c: (0, c))],
                 (seq, seq, pl.BlockSpec((1, 3, CONV_TC), lambda b_, c: (b_, 0, c)), pl.BlockSpec((1, 1, CONV_TC), lambda b_, c: (b_, 0, c))),
                 (sh, sh, jax.ShapeDtypeStruct((Bl, 3, D_FF), f32), jax.ShapeDtypeStruct((Bl, 1, D_FF), f32)),
                 sem=("parallel", "parallel"))(dx2, w_down, a, b, cw, cb)


def _local_step(x, mem, target, p, w_in, late_b, late_c, send, settle):
    Bl, S, Dd = x.shape
    T = Bl * S
    x2d, t2d, mem2d = x.reshape(T, Dd), target.reshape(T, Dd), mem.reshape(Bl * MEM_LEN, Dd)
    b_st = jnp.pad(p["b_spatial"].T, ((0, 0), (0, 128 - N_HEAD)))
    lbl = p["lb_logits"]

    h, h_t = _rms_fwd(x2d, p["norm1_g"], "norm1_fwd", transposed=True)
    proj = _mm(h, w_in, "nn", bf16, "proj_fwd", 1024, 1664)
    a_out = _gmlp_fwd(proj, p["ln_v_g"], p["ln_v_b"], p["w_spatial"], b_st)
    proj3 = proj.reshape(Bl, S, IN_WIDTH)
    b_out, states = _hgrn_fwd(proj3, lbl, p["hgrn_norm_g"], Bl, S)
    b_out = b_out.reshape(T, 512)
    memn = _rms_fwd(mem2d, p["mem_norm_g"], "memnorm_fwd")
    w = late_b(b_out)
    wb = w["w_branch"]
    kv = _mm(memn, w["w_mem_kv"], "nn", f32, "kv_fwd", 512, 1024).reshape(Bl, MEM_LEN, 2 * 512)
    c_out = _attn_fwd(proj, kv, Bl, S)
    branches = (a_out, b_out, c_out)
    merged, x1, h2, h2_t = _merge_out_norm_fwd(branches, wb, proj, w["w_out"], x2d, p["norm2_g"])
    w.update(late_c(h2))
    ffn_a, ffn_b, act = _up_conv_fwd(h2.reshape(Bl, S, Dd), w["w_up"], w["conv_w"], p["conv_b"])
    act = act.reshape(T, D_FF)
    loss_part, dx2, dx2_16, g_final = _down_final_loss(act, w["w_down"], x1, p["final_g"], t2d)

    g_w_down = _mm(act, dx2_16, "tn", bf16, "down_dw", 1408, 1024, 1024)
    da, db, g_conv_w, g_conv_b = _down_conv_bwd(dx2_16.reshape(Bl, S, Dd), w["w_down"], ffn_a, ffn_b, w["conv_w"], p["conv_b"])
    da, db = da.reshape(T, D_FF), db.reshape(T, D_FF)
    shard = 2 * D_FF // N_DEV
    g_w_up = jnp.concatenate([_mm(h2_t, d, "nn", bf16, f"up_dw_{n}", 512, 1408, shard=shard, n_outer=True)
                              for n, d in (("a", da), ("b", db))], axis=0)
    tok = send("c", dict(w_up=g_w_up, conv_w=jnp.sum(g_conv_w, axis=0), w_down=g_w_down))
    dh2 = _mm(da, w["w_up_a"], "nt", f32, "up_dx_a", 512, 1024)
    dx1, g_norm2 = _mm_rms_bwd(db, w["w_up_b"], x1, p["norm2_g"] + tok[0, 0], dx2, "up_dx_b_norm2_bwd", 512, pre=dh2)

    g_w_out = _mm(merged, dx1, "tn", bf16, "out_dw", 1024, 1024, 1024)
    dmerged = _mm(dx1, w["w_out"], "nt", f32, "out_dx", 1024, 1024)
    dgl, dbr, g_w_branch = _merge_bwd(branches, wb, proj, dmerged)
    dxq, dkv = _attn_bwd(proj, kv, dbr[2], Bl, S)
    dkv = dkv.reshape(Bl * MEM_LEN, 2 * 512)
    g_w_kv = _mm(memn, dkv, "tn", bf16, "kv_dw", 1024, 1024, 512)
    tok = send("b", dict(w_mem_kv=g_w_kv, w_branch=g_w_branch, w_out=g_w_out))
    dmemn = _mm(dkv, w["w_mem_kv"], "nt", f32, "kv_dx", 512, 1024)
    _, g_mem_norm = _rms_bwd(mem2d, p["mem_norm_g"], dmemn, "memnorm_bwd")
    dzuv, g_ln_g, g_ln_b, g_w_sp, g_b_sp = _gmlp_bwd(proj, p["ln_v_g"] + tok[0, 0], p["ln_v_b"], p["w_spatial"], b_st, dbr[0])
    *dqfig, g_lbl, g_ng = _hgrn_bwd(proj3, lbl, p["hgrn_norm_g"], states, dbr[1].reshape(Bl, S, 512), Bl, S)
    dq, df, di, dg = [d.reshape(T, 512) for d in dqfig]
    dproj = jnp.concatenate([dzuv, dq, df, di, dg, dxq, dgl[0], dgl[1], dgl[2]], axis=-1)
    g_w_in = _mm(h_t, dproj, "nn", bf16, "proj_dw", 512, 1664, shard=IN_WIDTH // N_DEV, n_outer=True)
    tok = send("a", dict(w_in=g_w_in))
    dx, g_norm1 = _mm_rms_bwd(settle(dproj), w_in, x2d, p["norm1_g"] + tok[0, 0], dx1, "proj_dx_norm1_bwd", 256)

    gs = dict(w_spatial=g_w_sp, norm1_g=g_norm1, mem_norm_g=g_mem_norm, norm2_g=g_norm2, final_g=g_final, lb_logits=g_lbl,
              ln_v_g=g_ln_g, ln_v_b=g_ln_b, b_spatial=g_b_sp, hgrn_norm_g=g_ng, conv_b=g_conv_b)
    return loss_part, dx.reshape(Bl, S, Dd), gs


def _coords():
    return lax.axis_index("x"), lax.axis_index("y"), lax.axis_index("c")


def _slot(dev):
    return 4 * dev[0] + 2 * dev[1] + dev[2]


def _comm_call(body, name, arrays, out_shapes, n_sem):
    n = len(arrays)
    hbm = pl.BlockSpec(memory_space=pl.ANY)
    return pl.pallas_call(
        body, name=name, out_shape=out_shapes, in_specs=[hbm] * n, out_specs=[hbm] * n,
        scratch_shapes=[pltpu.SemaphoreType.DMA((n_sem, n)), pltpu.SemaphoreType.DMA((n_sem, n)), pltpu.SemaphoreType.DMA((n,))])(*arrays)


def _all_gather(blocks, name):
    n = len(blocks)

    def body(*refs):
        x_refs, o_refs, (send_sems, recv_sems, local_sems) = refs[:n], refs[n:2 * n], refs[2 * n:]
        x, y, c = _coords()
        me, sibling = (x, y, c), (x, y, 1 - c)
        chips = [(1 - x, y), (x, 1 - y), (1 - x, 1 - y)]

        def copy(a, k, block_dev, to, from_input=False):
            dst = o_refs[a].at[_slot(block_dev)]
            return pltpu.make_async_remote_copy(src_ref=x_refs[a] if from_input else dst, dst_ref=dst, send_sem=send_sems.at[k, a],
                                                recv_sem=recv_sems.at[k, a], device_id=to, device_id_type=MESH)

        mine = [pltpu.make_async_copy(x_refs[a], o_refs[a].at[_slot(me)], local_sems.at[a]) for a in range(n)]
        first = [copy(a, 0, me, sibling, True) for a in range(n)]
        first += [copy(a, 1 + j, me, (*chip, c), True) for j, chip in enumerate(chips) for a in range(n)]
        for cp in mine + first:
            cp.start()
        passed = []
        for j, chip in enumerate(chips):
            for a in range(n):
                copy(a, 1 + j, (*chip, c), me).wait_recv()
                fwd = copy(a, 4 + j, (*chip, c), sibling)
                fwd.start()
                passed.append(fwd)
        for a in range(n):
            copy(a, 0, sibling, me).wait_recv()
        for j, chip in enumerate(chips):
            for a in range(n):
                copy(a, 4 + j, (*chip, 1 - c), me).wait_recv()
        for cp in first + passed:
            cp.wait_send()
        for cp in mine:
            cp.wait()

    return _comm_call(body, name, blocks, [jax.ShapeDtypeStruct((N_DEV,) + b.shape, b.dtype) for b in blocks], 7)


def _all_to_all(parts, name):
    n = len(parts)
    rel = [(0, 0, 1), (0, 1, 0), (0, 1, 1), (1, 0, 0), (1, 0, 1), (1, 1, 0), (1, 1, 1)]

    def body(*refs):
        x_refs, o_refs, (send_sems, recv_sems, local_sems) = refs[:n], refs[n:2 * n], refs[2 * n:]
        x, y, c = _coords()
        me = (x, y, c)
        peers = [(x ^ dx, y ^ dy, c ^ dc) for dx, dy, dc in rel]

        def copy(a, k, peer):
            return pltpu.make_async_remote_copy(src_ref=x_refs[a].at[_slot(peer)], dst_ref=o_refs[a].at[_slot(me)], send_sem=send_sems.at[k, a],
                                                recv_sem=recv_sems.at[k, a], device_id=peer, device_id_type=MESH)

        def arrival(a, k, peer):
            return pltpu.make_async_remote_copy(src_ref=x_refs[a].at[_slot(me)], dst_ref=o_refs[a].at[_slot(peer)], send_sem=send_sems.at[k, a],
                                                recv_sem=recv_sems.at[k, a], device_id=peer, device_id_type=MESH)

        mine = [pltpu.make_async_copy(x_refs[a].at[_slot(me)], o_refs[a].at[_slot(me)], local_sems.at[a]) for a in range(n)]
        sends = [copy(a, k, peer) for k, peer in enumerate(peers) for a in range(n)]
        for cp in mine + sends:
            cp.start()
        for k, peer in enumerate(peers):
            for a in range(n):
                arrival(a, k, peer).wait_recv()
        for cp in sends:
            cp.wait_send()
        for cp in mine:
            cp.wait()

    return _comm_call(body, name, parts, [jax.ShapeDtypeStruct(p.shape, p.dtype) for p in parts], 7)


_HBM = pl.BlockSpec(memory_space=pltpu.HBM)
_SEM = pl.BlockSpec(memory_space=pltpu.SEMAPHORE)
_REL = [(0, 0, 1), (0, 1, 0), (0, 1, 1), (1, 0, 0), (1, 0, 1), (1, 1, 0), (1, 1, 1)]


_LINK_ORDER = (3, 1, 5, 4, 2, 6, 0)
SEND_PIECES = 4


def _pieces(shape, dtype):
    rows = shape[0]
    unit = 1 if len(shape) > 2 else (16 if dtype == bf16 else 8)
    for n in (SEND_PIECES, 2):
        if rows % (n * unit) == 0:
            return [pl.ds(i * (rows // n), rows // n) for i in range(n)]
    return [pl.ds(0, rows)]


def _split_copies(gather, src, land, send, recv, pieces):
    x, y, c = _coords()
    me = (x, y, c)
    copies = []
    for a in range(len(src)):
        block = src[a].shape if gather else src[a].shape[1:]
        for rows in (_pieces(block, src[a].dtype) if pieces else [None]):
            for k in _LINK_ORDER:
                dx, dy, dc = _REL[k]
                peer = (x ^ dx, y ^ dy, c ^ dc)
                mine, there = (src[a] if gather else src[a].at[_slot(peer)]), land[a].at[_slot(me)]
                if rows is not None:
                    mine, there = mine.at[rows], there.at[rows]
                copies.append(pltpu.make_async_remote_copy(src_ref=mine, dst_ref=there, send_sem=send[a].at[k], recv_sem=recv[a].at[k],
                                                           device_id=peer, device_id_type=MESH))
    return me, copies


def _arrivals(gather, src, land, send, recv):
    x, y, c = _coords()
    out = []
    for a in range(len(src)):
        for k, (dx, dy, dc) in enumerate(_REL):
            peer = (x ^ dx, y ^ dy, c ^ dc)
            out.append(pltpu.make_async_remote_copy(src_ref=src[a] if gather else src[a].at[_slot(peer)], dst_ref=land[a].at[_slot(peer)],
                                                    send_sem=send[a].at[k], recv_sem=recv[a].at[k], device_id=peer, device_id_type=MESH))
    return out


def _exchange_start(arrays, gather, name, after=None):
    n = len(arrays)
    e = 0 if after is None else 1
    lands = [lax.empty(((N_DEV,) + a.shape) if gather else a.shape, a.dtype) for a in arrays]

    def body(*refs):
        src, land = refs[:n], refs[n:2 * n]
        refs = refs[2 * n + e:]
        send, recv, token, local_sems = refs[:n], refs[n:2 * n], refs[4 * n], refs[4 * n + 1]
        me, out = _split_copies(gather, src, land, send, recv, True)
        local = [pltpu.make_async_copy(src[a] if gather else src[a].at[_slot(me)], land[a].at[_slot(me)], local_sems.at[a])
                 for a in range(n)]
        for cp in local:
            cp.start()
        for cp in local:
            cp.wait()
        for cp in out:
            cp.start()
        token[...] = jnp.zeros_like(token)

    sems = [pltpu.SemaphoreType.DMA((7,)) for _ in range(2 * n)]
    outs = pl.pallas_call(
        body, name=name,
        out_shape=(*sems, *[pltpu.HBM(a.shape, a.dtype) for a in arrays], *[pltpu.HBM(l.shape, l.dtype) for l in lands],
                   jax.ShapeDtypeStruct((8, 128), f32)),
        in_specs=[_HBM] * (2 * n) + [pl.BlockSpec(memory_space=pl.ANY)] * e,
        out_specs=(*[_SEM] * (2 * n), *[_HBM] * (2 * n), pl.BlockSpec(memory_space=pltpu.VMEM)),
        input_output_aliases={i: 2 * n + i for i in range(2 * n)},
        scratch_shapes=[pltpu.SemaphoreType.DMA((n,))],
        compiler_params=pltpu.CompilerParams(has_side_effects=pltpu.SideEffectType.DATAFLOW_SIDE_EFFECTING))(
        *[pltpu.with_memory_space_constraint(a, pltpu.HBM) for a in arrays],
        *[pltpu.with_memory_space_constraint(l, pltpu.HBM) for l in lands], *([after] if e else []))
    return (gather, n, outs[:4 * n]), outs[4 * n]


def _exchange_wait(handle, which, after, name):
    gather, n_all, vals = handle
    send_v, recv_v, src_v, land_v = [[vals[g * n_all + i] for i in which] for g in range(4)]
    n = len(which)

    def body(*refs):
        src, land, send, recv = refs[:n], refs[n:2 * n], refs[2 * n:3 * n], refs[3 * n:4 * n]
        for cp in _split_copies(gather, src, land, send, recv, False)[1]:
            cp.wait_send()
        for cp in _arrivals(gather, src, land, send, recv):
            cp.wait_recv()

    outs = pl.pallas_call(
        body, name=name,
        out_shape=(*[pltpu.HBM(a.shape, a.dtype) for a in src_v], *[pltpu.HBM(l.shape, l.dtype) for l in land_v]),
        in_specs=[*[_HBM] * (2 * n), *[_SEM] * (2 * n), pl.BlockSpec(memory_space=pl.ANY)], out_specs=[_HBM] * (2 * n),
        input_output_aliases={i: i for i in range(2 * n)},
        compiler_params=pltpu.CompilerParams(has_side_effects=pltpu.SideEffectType.DATAFLOW_SIDE_EFFECTING))(
        *src_v, *land_v, *send_v, *recv_v, after)
    return outs[n:]


def _seq_exchange(arrays, gather, name, collective_id):
    n = len(arrays)
    hbm = pltpu.MemorySpace.HBM
    srcs = [jax.new_ref(a, memory_space=hbm) for a in arrays]
    lands = [jax.empty_ref(jax.ShapeDtypeStruct(((N_DEV,) + a.shape) if gather else a.shape, a.dtype), memory_space=hbm) for a in arrays]

    @pl.kernel(mesh=plsc.ScalarSubcoreMesh(axis_name="sequencer", num_cores=1), name=name,
               scratch_types=(pltpu.SemaphoreType.DMA((7, n)), pltpu.SemaphoreType.DMA((7, n)), pltpu.SemaphoreType.DMA((n,))),
               compiler_params=pltpu.CompilerParams(collective_id=collective_id))
    def launch(send, recv, local):
        x, y, c = _coords()
        me = (x, y, c)
        peers = [(x ^ dx, y ^ dy, c ^ dc) for dx, dy, dc in _REL]
        barrier = pltpu.get_barrier_semaphore()
        for peer in peers:
            pl.semaphore_signal(barrier, inc=1, device_id=peer, device_id_type=MESH)
        pl.semaphore_wait(barrier, len(peers))

        def copy(a, k, peer, arrival):
            return pltpu.make_async_remote_copy(
                src_ref=srcs[a] if gather else srcs[a].at[_slot(peer)], dst_ref=lands[a].at[_slot(peer if arrival else me)],
                send_sem=send.at[k, a], recv_sem=recv.at[k, a], device_id=peer, device_id_type=MESH)

        mine = [pltpu.make_async_copy(srcs[a] if gather else srcs[a].at[_slot(me)], lands[a].at[_slot(me)], local.at[a])
                for a in range(n)]
        out = [copy(a, k, peer, False) for a in range(n) for k, peer in enumerate(peers)]
        for cp in mine + out:
            cp.start()
        for a in range(n):
            for k, peer in enumerate(peers):
                copy(a, k, peer, True).wait_recv()
        for cp in out:
            cp.wait_send()
        for cp in mine:
            cp.wait()

    launch()
    return [land[...] for land in lands]


def _adam_math(w, g, m, v):
    m_ = ADAM_B1 * m + (1.0 - ADAM_B1) * g
    v_ = ADAM_B2 * v + (1.0 - ADAM_B2) * jnp.square(g)
    m_hat = m_ / (1.0 - ADAM_B1 ** ADAM_STEP)
    v_hat = v_ / (1.0 - ADAM_B2 ** ADAM_STEP)
    return -ADAM_LR * (m_hat / (jnp.sqrt(v_hat) + ADAM_EPS) + ADAM_WD * w), m_, v_


def _reduce_adamw(parts, w, m, v, name):
    _, R, L = parts.shape
    tr = _pick(R, (256, 128, 64, 32, 16, 8))

    def body(p_ref, w_ref, m_ref, v_ref, g_ref, d_ref, nm_ref, nv_ref):
        g = p_ref[0].astype(f32)
        for i in range(1, N_DEV):
            g = g + p_ref[i].astype(f32)
        g_ref[...] = g
        d_ref[...], nm_ref[...], nv_ref[...] = _adam_math(w_ref[...], g, m_ref[...], v_ref[...])

    blk = pl.BlockSpec((tr, L), lambda i: (i, 0))
    sh = jax.ShapeDtypeStruct((R, L), f32)
    return _call(body, name, (R // tr,), [pl.BlockSpec((N_DEV, tr, L), lambda i: (0, i, 0)), blk, blk, blk], (blk,) * 4, (sh,) * 4,
                 sem=("parallel",))(parts, w, m, v)


SMALL = (("w_spatial", (512, 128), 0), ("norm1_g", (1, 1024), 512), ("mem_norm_g", (1, 1024), 520), ("norm2_g", (1, 1024), 528),
         ("final_g", (1, 1024), 536), ("lb_logits", (2, 512), 544), ("ln_v_g", (1, 512), 552), ("ln_v_b", (1, 512), 556),
         ("b_spatial", (4, 128), 560), ("hgrn_norm_g", (1, 128), 564), ("conv_b", (1, 2816), 565))
LOSS_ROW, SMALL_USED, SMALL_ROWS = 587, 588, 640


def _segments(shape, base):
    r, n = shape
    per = n // 128
    return [(base + i * per + j, i, slice(j * 128, (j + 1) * 128)) for i in range(r) for j in range(per)]


def _pack_small(gs, loss_part):
    names = [n for n, _, _ in SMALL]

    def body(*refs):
        src, loss_ref, o_ref = dict(zip(names, refs[:-2])), refs[-2], refs[-1]
        o_ref[SMALL_USED:SMALL_ROWS, :] = jnp.zeros((SMALL_ROWS - SMALL_USED, 128), f32)
        o_ref[LOSS_ROW:LOSS_ROW + 1, :] = loss_ref[...]
        for name, shape, base in SMALL:
            ref = src[name]
            if name == "w_spatial":
                o_ref[base:base + 512, :] = ref[...].reshape(512, 128)
            elif name == "b_spatial":
                o_ref[base:base + 4, :] = ref[0:4, :]
            elif name == "conv_b":
                per_example = functools.reduce(lambda u, v_: u + v_, [ref[b] for b in range(ref.shape[0])])
                for row, i, sl in _segments(shape, base):
                    o_ref[row:row + 1, :] = per_example[i:i + 1, sl]
            elif name == "hgrn_norm_g":
                per_head = [ref[b, h] for b in range(ref.shape[0]) for h in range(N_HEAD)]
                o_ref[base:base + 1, :] = functools.reduce(lambda u, v_: u + v_, per_head)
            else:
                for row, i, sl in _segments(shape, base):
                    o_ref[row:row + 1, :] = ref[i:i + 1, sl]

    return pl.pallas_call(body, name="pack_small", out_shape=jax.ShapeDtypeStruct((SMALL_ROWS, 128), f32))(
        *[gs[n] for n in names], loss_part)


def _small_update(gathered, w, m, v):
    names = [n for n, _, _ in SMALL]
    k = len(names)

    def body(*refs):
        p_ref = refs[0]
        ins = [dict(zip(names, refs[1 + i * k:1 + (i + 1) * k])) for i in range(3)]
        outs = [dict(zip(names, refs[1 + (3 + i) * k:1 + (4 + i) * k])) for i in range(4)]
        loss_ref, gsum = refs[-2], refs[-1]
        g = p_ref[0]
        for i in range(1, N_DEV):
            g = g + p_ref[i]
        gsum[...] = g
        loss_ref[...] = gsum[LOSS_ROW:LOSS_ROW + 1, :]
        for name, shape, base in SMALL:
            if name == "w_spatial":
                where = [(slice(base, base + 512), (slice(None), slice(None)))]
            else:
                where = [(slice(row, row + 1), (slice(i, i + 1), sl)) for row, i, sl in _segments(shape, base)]
            for rows, at in where:
                g_ = gsum[rows, :]
                d_, m_, v_ = _adam_math(ins[0][name][at], g_, ins[1][name][at], ins[2][name][at])
                for o, val in zip(outs, (g_, d_, m_, v_)):
                    o[name][at] = val

    args = [gathered] + [d[n] for d in (w, m, v) for n in names]
    out_shapes = [jax.ShapeDtypeStruct(shape, f32) for _ in range(4) for _, shape, _ in SMALL] + [jax.ShapeDtypeStruct((1, 128), f32)]
    outs = pl.pallas_call(body, name="small_update", out_shape=out_shapes, scratch_shapes=[pltpu.VMEM((SMALL_ROWS, 128), f32)])(*args)
    return [dict(zip(names, outs[i * k:(i + 1) * k])) for i in range(4)], outs[-1]


def _cols_full(g):
    return jnp.moveaxis(g, 0, -2).reshape(g.shape[1:-1] + (N_DEV * g.shape[-1],))


def _cols_parts(full):
    n = full.shape[-1] // N_DEV
    return jnp.moveaxis(full.reshape(full.shape[:-1] + (N_DEV, n)), -2, 0)


def kernel(x, mem, norm1_g, w_in, ln_v_g, ln_v_b, w_spatial, b_spatial, lb_logits, hgrn_norm_g, mem_norm_g, w_mem_kv, w_branch, w_out, norm2_g, w_up, conv_w, conv_b, w_down, final_g, loss_target, m_norm1_g, m_w_in, m_ln_v_g, m_ln_v_b, m_w_spatial, m_b_spatial, m_lb_logits, m_hgrn_norm_g, m_mem_norm_g, m_w_mem_kv, m_w_branch, m_w_out, m_norm2_g, m_w_up, m_conv_w, m_conv_b, m_w_down, m_final_g, v_norm1_g, v_w_in, v_ln_v_g, v_ln_v_b, v_w_spatial, v_b_spatial, v_lb_logits, v_hgrn_norm_g, v_mem_norm_g, v_w_mem_kv, v_w_branch, v_w_out, v_norm2_g, v_w_up, v_conv_w, v_conv_b, v_w_down, v_final_g):
    given = dict(locals())
    order = ("norm1_g", "w_in", "ln_v_g", "ln_v_b", "w_spatial", "b_spatial", "lb_logits", "hgrn_norm_g", "mem_norm_g",
             "w_mem_kv", "w_branch", "w_out", "norm2_g", "w_up", "conv_w", "conv_b", "w_down", "final_g")
    groups = dict(a=("w_in",), b=("w_mem_kv", "w_branch", "w_out"), c=("w_up", "conv_w", "w_down"))

    wire = {n: given[n][0].astype(f32 if n == "conv_w" else bf16) for ns in groups.values() for n in ns}
    g_in = _all_gather([wire["w_in"]], "gather_w_in")[0]
    late = groups["b"] + groups["c"]
    w_in_full = _cols_full(g_in)
    w_in_full, rest_wire = lax.optimization_barrier((w_in_full, [wire[n] for n in late]))
    rest = _seq_exchange(rest_wire, True, "gather_rest", 1)

    def late_b(after):
        _, (kv_, br_, out_) = lax.optimization_barrier((after, tuple(rest[0:3])))
        br_ = _cols_full(br_)
        return dict(w_mem_kv=kv_.reshape(D_MODEL, 2 * 512), w_branch=[br_[n] for n in range(3)], w_out=out_.reshape(D_MODEL, D_MODEL))

    def late_c(after):
        _, (up_, cw_, down_) = lax.optimization_barrier((after, tuple(rest[3:6])))
        up_ = _cols_full(up_)
        return dict(w_up=up_, w_up_a=up_[:, :D_FF], w_up_b=up_[:, D_FF:], conv_w=_cols_full(cw_), w_down=down_.reshape(D_FF, D_MODEL))

    to_parts = dict(w_in=lambda g_: g_, w_up=lambda g_: g_, conv_w=_cols_parts,
                    w_branch=lambda g_: _cols_parts(g_.astype(bf16)).reshape(N_DEV, -1, 128),
                    w_mem_kv=lambda g_: g_.reshape(N_DEV, -1, 2 * 512), w_out=lambda g_: g_.reshape(N_DEV, -1, D_MODEL),
                    w_down=lambda g_: g_.reshape(N_DEV, -1, D_MODEL))
    scatters = {}

    def send(tag, grads_):
        parts = [to_parts[n](grads_[n]) for n in groups[tag]]
        scatters[tag] = _seq_exchange(parts, False, f"scatter_{tag}", dict(a=2, b=4, c=5)[tag])
        return jnp.zeros((8, 128), f32)

    small_2d = lambda prefix: {n: given[prefix + n].reshape(shape) for n, shape, _ in SMALL}
    p = small_2d("")
    p["w_spatial"] = w_spatial[0]
    updates = {}

    def update(tag):
        for n, parts in zip(groups[tag], scatters[tag]):
            two_d = (-1, given[n].shape[-1])
            updates[n] = _reduce_adamw(parts, *[given[pre + n].reshape(two_d) for pre in ("", "m_", "v_")], "adamw_" + n)

    def settle(chain):
        update("c")
        update("b")
        early = groups["c"] + groups["b"]
        chain, tied = lax.optimization_barrier((chain, [updates[n] for n in early]))
        updates.update(zip(early, tied))
        return chain

    loss_part, grad_x, gs = _local_step(x, mem, loss_target, p, w_in_full, late_b, late_c, send, settle)

    gathered = _seq_exchange([_pack_small(gs, loss_part)], True, "gather_small", 3)[0]

    update("a")
    grads, delta, new_m, new_v = {}, {}, {}, {}
    for n, res in updates.items():
        grads[n], delta[n], new_m[n], new_v[n] = [r.reshape(given[n].shape) for r in res]

    small_results, loss_row = _small_update(gathered, small_2d(""), small_2d("m_"), small_2d("v_"))
    for dst, res in zip((grads, delta, new_m, new_v), small_results):
        for n, _, _ in SMALL:
            dst[n] = res[n].reshape(given[n].shape)
    loss = loss_row[0, 0]

    return (loss, grad_x, *[grads[n] for n in order], *[delta[n] for n in order], *[new_m[n] for n in order],
            *[new_v[n] for n in order])
```

```python
import functools

import jax
import jax.numpy as jnp
from jax import lax
from jax.experimental import pallas as pl
from jax.experimental.pallas import tpu as pltpu
from jax.experimental.pallas import tpu_sc as plsc

f32 = jnp.float32
bf16 = jnp.bfloat16

N_DEV = 8
D_MODEL = 1024
EPS = 1e-6
GM_CHUNK = 128
HG_CHUNK = 64
HEAD = 128
N_HEAD = 4
MEM_LEN = 256
D_FF = 2816
IN_WIDTH = 6656
C_ZU, C_HQ, C_HF, C_HI, C_HG, C_XQ, C_GL = 0, 1024, 1536, 2048, 2560, 3072, 3584
ADAM_LR, ADAM_B1, ADAM_B2, ADAM_EPS, ADAM_WD, ADAM_STEP = 0.001, 0.9, 0.999, 1e-08, 0.01, 10
VMEM_LIMIT = 56 * 1024 * 1024
MESH = pl.DeviceIdType.MESH


def _pick(n, cands):
    for c in cands:
        if n % c == 0:
            return c
    return n


def _call(body, name, grid, in_specs, out_specs, out_shape, scratch=(), sem=None, **cp):
    params = dict(vmem_limit_bytes=VMEM_LIMIT, **cp)
    if sem is not None:
        params["dimension_semantics"] = sem
    return pl.pallas_call(
        body, name=name, grid=grid, in_specs=in_specs, out_specs=out_specs, out_shape=out_shape,
        scratch_shapes=list(scratch), compiler_params=pltpu.CompilerParams(**params))


_DN = {"nn": (((1,), (0,)), ((), ())), "nt": (((1,), (1,)), ((), ())), "tn": (((0,), (0,)), ((), ()))}


def _raw_dot(a, b, mode):
    return lax.dot_general(a.astype(bf16), b.astype(bf16), _DN[mode], preferred_element_type=f32)


@jax.custom_vjp
def _dot_nn(a, b):
    return _raw_dot(a, b, "nn")


_dot_nn.defvjp(lambda a, b: (_raw_dot(a, b, "nn"), (a, b)),
               lambda r, g: (_raw_dot(g, r[1], "nt"), _raw_dot(r[0], g, "tn")))


@jax.custom_vjp
def _dot_nt(a, b):
    return _raw_dot(a, b, "nt")


_dot_nt.defvjp(lambda a, b: (_raw_dot(a, b, "nt"), (a, b)),
               lambda r, g: (_raw_dot(g, r[1], "nn"), _raw_dot(g, r[0], "tn")))


@jax.custom_vjp
def _dot_tn(a, b):
    return _raw_dot(a, b, "tn")


_dot_tn.defvjp(lambda a, b: (_raw_dot(a, b, "tn"), (a, b)),
               lambda r, g: (_raw_dot(r[1], g, "nt"), _raw_dot(r[0], g, "nn")))


def _tri(n, lower):
    r = lax.broadcasted_iota(jnp.int32, (n, n), 0)
    c = lax.broadcasted_iota(jnp.int32, (n, n), 1)
    return ((c <= r) if lower else (c >= r)).astype(f32)


def _sel_dot(sel, x, mode, x_first=False, pieces=3):
    sel = sel.astype(bf16)
    out, rest = None, x
    for p in range(pieces):
        piece = rest.astype(bf16)
        part = lax.dot_general(*((piece, sel) if x_first else (sel, piece)), _DN[mode], preferred_element_type=f32)
        out = part if out is None else out + part
        if p + 1 < pieces:
            rest = rest - piece.astype(f32)
    return out


def _egrad(fn, x, ct):
    return jax.vjp(fn, x)[1](ct)[0]


def _mm(a, b, mode, out_dtype, name, tm, tn, tk=None, residual=None, shard=None, n_outer=False):
    if mode == "nn":
        (M, K), (_, N) = a.shape, b.shape
    elif mode == "nt":
        (M, K), (N, _) = a.shape, b.shape
    else:
        (K, M), (_, N) = a.shape, b.shape
    tm, tn = min(tm, M), min(tn, N)
    tk = K if tk is None else min(tk, K)
    assert M % tm == 0 and N % tn == 0 and K % tk == 0, (name, M, N, K, tm, tn, tk)
    nk = K // tk

    def body(*refs):
        acc_ref = refs[-1] if nk > 1 else None
        refs = refs[:-1] if nk > 1 else refs
        if residual is None:
            a_ref, b_ref, o_ref = refs
        else:
            a_ref, b_ref, r_ref, o_ref = refs

        def finish(r):
            if residual is not None:
                r = r + r_ref[...]
            if shard is None:
                o_ref[...] = r.astype(out_dtype)
            else:
                for s in range(tn // shard):
                    o_ref[s] = r[:, s * shard:(s + 1) * shard].astype(out_dtype)

        part = _raw_dot(a_ref[...], b_ref[...], mode)
        if nk == 1:
            finish(part)
            return
        k = pl.program_id(2)

        @pl.when(k == 0)
        def _():
            acc_ref[...] = part

        @pl.when((k > 0) & (k < nk - 1))
        def _():
            acc_ref[...] += part

        @pl.when(k == nk - 1)
        def _():
            finish(acc_ref[...] + part)

    def at(index):
        return (lambda j, i, k: index(i, j, k)) if n_outer else index

    a_spec = {"nn": pl.BlockSpec((tm, tk), at(lambda i, j, k: (i, k))),
              "nt": pl.BlockSpec((tm, tk), at(lambda i, j, k: (i, k))),
              "tn": pl.BlockSpec((tk, tm), at(lambda i, j, k: (k, i)))}[mode]
    b_spec = {"nn": pl.BlockSpec((tk, tn), at(lambda i, j, k: (k, j))),
              "nt": pl.BlockSpec((tn, tk), at(lambda i, j, k: (j, k))),
              "tn": pl.BlockSpec((tk, tn), at(lambda i, j, k: (k, j)))}[mode]
    o_spec = pl.BlockSpec((tm, tn), at(lambda i, j, k: (i, j)))
    in_specs = [a_spec, b_spec] + ([o_spec] if residual is not None else [])
    args = (a, b) + ((residual,) if residual is not None else ())
    out_shape = jax.ShapeDtypeStruct((M, N), out_dtype)
    if shard is not None:
        assert residual is None and tn % shard == 0
        o_spec = pl.BlockSpec((tn // shard, tm, shard), at(lambda i, j, k: (j, i, 0)))
        out_shape = jax.ShapeDtypeStruct((N // shard, M, shard), out_dtype)
    grid = (N // tn, M // tm, nk) if n_outer else (M // tm, N // tn, nk)
    return _call(body, name, grid, in_specs, o_spec, out_shape,
                 scratch=[pltpu.VMEM((tm, tn), f32)] if nk > 1 else [], sem=("parallel", "parallel", "arbitrary"))(*args)


def _rms_fwd(x, g, name, transposed=False):
    R, Dd = x.shape
    tr = _pick(R, (512, 256, 128))

    def body(x_ref, g_ref, o_ref, *t_ref):
        xf = x_ref[...]
        y = xf * lax.rsqrt(jnp.mean(xf * xf, axis=-1, keepdims=True) + EPS) * g_ref[...]
        o_ref[...] = y.astype(bf16)
        if transposed:
            t_ref[0][...] = y.T.astype(bf16)

    row = pl.BlockSpec((tr, Dd), lambda i: (i, 0))
    out_specs, out_shape = row, jax.ShapeDtypeStruct((R, Dd), bf16)
    if transposed:
        out_specs, out_shape = (row, pl.BlockSpec((Dd, tr), lambda i: (0, i))), (out_shape, jax.ShapeDtypeStruct((Dd, R), bf16))
    return _call(body, name, (R // tr,), [row, pl.BlockSpec((1, Dd), lambda i: (0, 0))], out_specs, out_shape, sem=("parallel",))(x, g)


def _rms_bwd(x, g, dh, name, residual=None):
    R, Dd = x.shape
    tr = _pick(R, (512, 256, 128))

    def body(*refs):
        if residual is None:
            x_ref, g_ref, dh_ref, dx_ref, dg_ref = refs
        else:
            x_ref, g_ref, dh_ref, r_ref, dx_ref, dg_ref = refs
        xf = x_ref[...]
        rs = lax.rsqrt(jnp.mean(xf * xf, axis=-1, keepdims=True) + EPS)
        y = xf * rs
        dh_ = dh_ref[...].astype(f32)
        dy = dh_ * g_ref[...]
        dx = rs * (dy - y * jnp.mean(dy * y, axis=-1, keepdims=True))
        if residual is not None:
            dx = dx + r_ref[...]
        dx_ref[...] = dx

        @pl.when(pl.program_id(0) == 0)
        def _():
            dg_ref[...] = jnp.zeros_like(dg_ref)

        dg_ref[...] += jnp.sum(dh_ * y, axis=0, keepdims=True)

    row = pl.BlockSpec((tr, Dd), lambda i: (i, 0))
    vec = pl.BlockSpec((1, Dd), lambda i: (0, 0))
    in_specs = [row, vec, row] + ([row] if residual is not None else [])
    args = (x, g, dh) + ((residual,) if residual is not None else ())
    return _call(body, name, (R // tr,), in_specs, (row, vec),
                 (jax.ShapeDtypeStruct((R, Dd), f32), jax.ShapeDtypeStruct((1, Dd), f32)), sem=("arbitrary",))(*args)


def _mm_rms_bwd(pairs, x, g, residual, name, tm):
    M = x.shape[0]
    Dd = x.shape[1]
    tm = min(tm, M)
    n = len(pairs)

    def body(*refs):
        ab_refs, (x_ref, g_ref, r_ref, dx_ref, dg_ref) = refs[:2 * n], refs[2 * n:]
        dh_ = _raw_dot(ab_refs[0][...], ab_refs[1][...], "nt")
        for k in range(1, n):
            dh_ = dh_ + _raw_dot(ab_refs[2 * k][...], ab_refs[2 * k + 1][...], "nt")
        xf = x_ref[...]
        rs = lax.rsqrt(jnp.mean(xf * xf, axis=-1, keepdims=True) + EPS)
        y = xf * rs
        dy = dh_ * g_ref[...]
        dx_ref[...] = rs * (dy - y * jnp.mean(dy * y, axis=-1, keepdims=True)) + r_ref[...]

        @pl.when(pl.program_id(0) == 0)
        def _():
            dg_ref[...] = jnp.zeros_like(dg_ref)

        dg_ref[...] += jnp.sum(dh_ * y, axis=0, keepdims=True)

    row = pl.BlockSpec((tm, Dd), lambda i: (i, 0))
    vec = pl.BlockSpec((1, Dd), lambda i: (0, 0))
    in_specs, args = [], []
    for a, b in pairs:
        in_specs += [pl.BlockSpec((tm, a.shape[1]), lambda i: (i, 0)), pl.BlockSpec(b.shape, lambda i: (0, 0))]
        args += [a, b]
    in_specs += [row, vec, row]
    args += [x, g, residual]
    return _call(body, name, (M // tm,), in_specs, (row, vec),
                 (jax.ShapeDtypeStruct((M, Dd), f32), jax.ShapeDtypeStruct((1, Dd), f32)), sem=("arbitrary",))(*args)


def _down_final_loss(act, w_down, x1, g, target):
    R, Dd = x1.shape
    tr = _pick(R, (512, 256, 128))

    def body(a_ref, w_ref, x1_ref, g_ref, t_ref, loss_ref, dx_ref, dxb_ref, dg_ref):
        xf = _raw_dot(a_ref[...], w_ref[...], "nn") + x1_ref[...]
        rs = lax.rsqrt(jnp.mean(xf * xf, axis=-1, keepdims=True) + EPS)
        y = xf * rs
        err = y * g_ref[...] - t_ref[...]
        dh_ = err * (1.0 / Dd)
        dy = dh_ * g_ref[...]
        dx = rs * (dy - y * jnp.mean(dy * y, axis=-1, keepdims=True))
        dx_ref[...] = dx
        dxb_ref[...] = dx.astype(bf16)

        @pl.when(pl.program_id(0) == 0)
        def _():
            dg_ref[...] = jnp.zeros_like(dg_ref)
            loss_ref[...] = jnp.zeros_like(loss_ref)

        dg_ref[...] += jnp.sum(dh_ * y, axis=0, keepdims=True)
        part = jnp.sum(jnp.mean(err * err, axis=-1, keepdims=True), axis=0, keepdims=True)
        loss_ref[...] += 0.5 * part

    row = pl.BlockSpec((tr, Dd), lambda i: (i, 0))
    vec = pl.BlockSpec((1, Dd), lambda i: (0, 0))
    in_specs = [pl.BlockSpec((tr, act.shape[1]), lambda i: (i, 0)), pl.BlockSpec(w_down.shape, lambda i: (0, 0)), row, vec, row]
    return _call(body, "down_final_loss", (R // tr,), in_specs, (pl.BlockSpec((1, 128), lambda i: (0, 0)), row, row, vec),
                 (jax.ShapeDtypeStruct((1, 128), f32), jax.ShapeDtypeStruct((R, Dd), f32), jax.ShapeDtypeStruct((R, Dd), bf16),
                  jax.ShapeDtypeStruct((1, Dd), f32)), sem=("arbitrary",))(act, w_down, x1, g, target)


def _gmlp_parts(zuv, ln_g, ln_b):
    zu, zv = zuv[:, :512], zuv[:, 512:]
    u = jax.nn.gelu(zu)
    v = jax.nn.gelu(zv)
    mu = jnp.mean(v, axis=-1, keepdims=True)
    rs = lax.rsqrt(jnp.mean(jnp.square(v - mu), axis=-1, keepdims=True) + EPS)
    xh = (v - mu) * rs
    return zu, zv, u, xh, rs, xh * ln_g + ln_b


GM_TILE_CHUNKS = 4


def _gmlp_tile(T):
    n = _pick(T // GM_CHUNK, (GM_TILE_CHUNKS, 2, 1))
    return n, n * GM_CHUNK


def _gmlp_fwd(proj, ln_g, ln_b, w_s, b_st):
    T = proj.shape[0]
    nch, rows = _gmlp_tile(T)

    def body(p_ref, g_ref, b_ref, w_ref, bs_ref, o_ref):
        _, _, u, _, _, vn = _gmlp_parts(p_ref[...].astype(f32), g_ref[...], b_ref[...])
        causal = _tri(GM_CHUNK, True) > 0
        for gi in range(N_HEAD):
            sl = slice(gi * HEAD, (gi + 1) * HEAD)
            w = jnp.where(causal, w_ref[gi], 0.0)
            for ch in range(nch):
                rs_ = slice(ch * GM_CHUNK, (ch + 1) * GM_CHUNK)
                mixed = _raw_dot(w, vn[rs_, sl], "nn") + bs_ref[:, gi:gi + 1]
                o_ref[rs_, sl] = (u[rs_, sl] * mixed).astype(bf16)

    vec = pl.BlockSpec((1, 512), lambda i: (0, 0))
    return _call(body, "gmlp_fwd", (T // rows,),
                 [pl.BlockSpec((rows, 1024), lambda i: (i, 0)), vec, vec,
                  pl.BlockSpec((N_HEAD, GM_CHUNK, GM_CHUNK), lambda i: (0, 0, 0)), pl.BlockSpec((GM_CHUNK, 128), lambda i: (0, 0))],
                 pl.BlockSpec((rows, 512), lambda i: (i, 0)), jax.ShapeDtypeStruct((T, 512), bf16), sem=("parallel",))(
        proj, ln_g, ln_b, w_s, b_st)


def _gmlp_bwd(proj, ln_g, ln_b, w_s, b_st, da):
    T = proj.shape[0]
    nch, rows = _gmlp_tile(T)

    def body(p_ref, g_ref, b_ref, w_ref, bs_ref, da_ref, dp_ref, dg_ref, db_ref, dw_ref, dbs_ref):
        zu, zv, u, xh, rs, vn = _gmlp_parts(p_ref[...].astype(f32), g_ref[...], b_ref[...])
        causal = _tri(GM_CHUNK, True) > 0
        sub = lax.broadcasted_iota(jnp.int32, (8, GM_CHUNK), 0)
        ones = jnp.ones((8, HEAD), f32)
        dout = da_ref[...].astype(f32)

        @pl.when(pl.program_id(0) == 0)
        def _():
            for r in (dg_ref, db_ref, dw_ref, dbs_ref):
                r[...] = jnp.zeros_like(r)

        du, dvn, dbs = [], [], jnp.zeros((8, GM_CHUNK), f32)
        for gi in range(N_HEAD):
            sl = slice(gi * HEAD, (gi + 1) * HEAD)
            w = jnp.where(causal, w_ref[gi], 0.0)
            du_g, dvn_g, dw_g = [], [], jnp.zeros((GM_CHUNK, GM_CHUNK), f32)
            for ch in range(nch):
                rs_ = slice(ch * GM_CHUNK, (ch + 1) * GM_CHUNK)
                mixed = _raw_dot(w, vn[rs_, sl], "nn") + bs_ref[:, gi:gi + 1]
                du_g.append(dout[rs_, sl] * mixed)
                dm = dout[rs_, sl] * u[rs_, sl]
                dbs = dbs + jnp.where(sub == gi, _sel_dot(ones, dm, "nt"), 0.0)
                dw_g = dw_g + _raw_dot(dm, vn[rs_, sl], "nt")
                dvn_g.append(_raw_dot(w, dm, "tn"))
            dw_ref[gi] += jnp.where(causal, dw_g, 0.0)
            du.append(jnp.concatenate(du_g, axis=0))
            dvn.append(jnp.concatenate(dvn_g, axis=0))
        dbs_ref[...] += dbs
        du = jnp.concatenate(du, axis=-1)
        dvn = jnp.concatenate(dvn, axis=-1)
        dg_ref[...] += jnp.sum(dvn * xh, axis=0, keepdims=True)
        db_ref[...] += jnp.sum(dvn, axis=0, keepdims=True)
        dxh = dvn * g_ref[...]
        dv = rs * (dxh - jnp.mean(dxh, axis=-1, keepdims=True) - xh * jnp.mean(dxh * xh, axis=-1, keepdims=True))
        dp_ref[:, :512] = _egrad(jax.nn.gelu, zu, du).astype(bf16)
        dp_ref[:, 512:] = _egrad(jax.nn.gelu, zv, dv).astype(bf16)

    vec = pl.BlockSpec((1, 512), lambda i: (0, 0))
    wsp = pl.BlockSpec((N_HEAD, GM_CHUNK, GM_CHUNK), lambda i: (0, 0, 0))
    return _call(body, "gmlp_bwd", (T // rows,),
                 [pl.BlockSpec((rows, 1024), lambda i: (i, 0)), vec, vec, wsp, pl.BlockSpec((GM_CHUNK, 128), lambda i: (0, 0)),
                  pl.BlockSpec((rows, 512), lambda i: (i, 0))],
                 (pl.BlockSpec((rows, 1024), lambda i: (i, 0)), vec, vec, wsp, pl.BlockSpec((8, GM_CHUNK), lambda i: (0, 0))),
                 (jax.ShapeDtypeStruct((T, 1024), bf16), jax.ShapeDtypeStruct((1, 512), f32), jax.ShapeDtypeStruct((1, 512), f32),
                  jax.ShapeDtypeStruct((N_HEAD, GM_CHUNK, GM_CHUNK), f32), jax.ShapeDtypeStruct((8, GM_CHUNK), f32)),
                 sem=("arbitrary",))(proj, ln_g, ln_b, w_s, b_st, da)


HG_SUB = 8
HG_NSUB = HG_CHUNK // HG_SUB


def _two_level_matrix(transposed=False):
    shape = (HG_CHUNK, 2 * HG_CHUNK) if transposed else (2 * HG_CHUNK, HG_CHUNK)
    r = lax.broadcasted_iota(jnp.int32, shape, 1 if transposed else 0)
    c = lax.broadcasted_iota(jnp.int32, shape, 0 if transposed else 1)
    t = jnp.where(r < HG_CHUNK, r, r - HG_CHUNK)
    local = (r < HG_CHUNK) & (t // HG_SUB == c // HG_SUB) & (c <= t)
    before = (r >= HG_CHUNK) & (c < (t // HG_SUB) * HG_SUB)
    return (local | before).astype(f32)


def _two_level_sums(x):
    two = _sel_dot(_two_level_matrix(), x, "nn")
    return two[:HG_CHUNK], two[HG_CHUNK:]


@jax.custom_vjp
def _two_level_cumsum(x):
    return _two_level_sums(x)


_two_level_cumsum.defvjp(
    lambda x: (_two_level_sums(x), None),
    lambda _, g: (_sel_dot(_two_level_matrix(), jnp.concatenate(g, axis=0), "tn"),))


def _tile_matrix():
    s = lax.broadcasted_iota(jnp.int32, (HG_SUB, HG_CHUNK), 0)
    j = lax.broadcasted_iota(jnp.int32, (HG_SUB, HG_CHUNK), 1)
    return (j % HG_SUB == s).astype(f32)


@jax.custom_vjp
def _tile_lanes(x):
    return _sel_dot(_tile_matrix(), x, "nn", x_first=True, pieces=1)


_tile_lanes.defvjp(
    lambda x: (_sel_dot(_tile_matrix(), x, "nn", x_first=True, pieces=1), None),
    lambda _, g: (_sel_dot(_tile_matrix(), g, "nt", x_first=True, pieces=2),))


def _block_rows(x):
    k = x.shape[-1]
    return jnp.broadcast_to(x.reshape(HG_NSUB, 1, HG_SUB, k), (HG_NSUB, HG_SUB, HG_SUB, k)).reshape(HG_CHUNK, HG_SUB, k)


def _hgrn_chunk(st0, q_raw, f_raw, i_raw, g_raw, l0, l1, ng):
    C, SUB = HG_CHUNK, HG_SUB
    lb = jax.nn.sigmoid(l0 - l1)
    fg = lb + (1.0 - lb) * jax.nn.sigmoid(f_raw)
    kk = 1.0 - fg
    qf = jax.nn.silu(q_raw)
    al, base = _two_level_cumsum(jnp.log(fg))
    a = al + base
    row = lax.broadcasted_iota(jnp.int32, (C, HEAD), 0)
    a_last = jnp.sum(jnp.where(row == C - 1, a, 0.0), axis=0, keepdims=True)
    inter = _dot_nt(qf * jnp.exp(a), st0)
    qt = qf * jnp.exp(al)
    rb = lax.broadcasted_iota(jnp.int32, (C, C), 0) // SUB
    cb = lax.broadcasted_iota(jnp.int32, (C, C), 1) // SUB
    scores = jnp.zeros((C, C), f32)
    for i in range(1, HG_NSUB):
        base_i = jnp.sum(jnp.where(row == i * SUB, base, 0.0), axis=0, keepdims=True)
        kt = kk * jnp.exp(jnp.minimum(base_i - a, 0.0))
        scores = scores + jnp.where((rb == i) & (cb < i), _dot_nt(qt, kt), 0.0)
    t_i = lax.broadcasted_iota(jnp.int32, (C, SUB, HEAD), 0) % SUB
    s_i = lax.broadcasted_iota(jnp.int32, (C, SUB, HEAD), 1)
    decay = jnp.exp(jnp.where(s_i <= t_i, al[:, None, :] - _block_rows(al), -jnp.inf))
    diag = jnp.sum(qf[:, None, :] * decay * _block_rows(kk), axis=-1)
    scores = scores + jnp.where(rb == cb, _tile_lanes(diag), 0.0)
    o = inter + _dot_nn(scores, i_raw)
    st1 = jnp.exp(a_last) * st0 + _dot_tn(i_raw, kk * jnp.exp(a_last - a))
    on = o * lax.rsqrt(jnp.mean(o * o, axis=-1, keepdims=True) + EPS) * ng
    return st1, on * jax.nn.silu(g_raw)


def _hgrn_specs(S, Bl, rev):
    N = S // HG_CHUNK
    chunk = (lambda n: N - 1 - n) if rev else (lambda n: n)
    col = lambda c0: pl.BlockSpec((Bl, HG_CHUNK, 512), lambda n: (0, chunk(n), c0 // 512))
    st = pl.BlockSpec((Bl, N_HEAD, 1, HEAD, HEAD), lambda n: (0, 0, chunk(n), 0, 0))
    full = lambda *s: pl.BlockSpec(s, functools.partial(lambda n, nd: (0,) * nd, nd=len(s)))
    return N, col, st, full


def _hgrn_fwd(proj, lb_logits, ng, Bl, S):
    N, col, st, full = _hgrn_specs(S, Bl, False)

    def body(q_ref, f_ref, i_ref, g_ref, l_ref, ng_ref, o_ref, st_ref, state):
        @pl.when(pl.program_id(0) == 0)
        def _():
            state[...] = jnp.zeros_like(state)

        for b in range(Bl):
            for h in range(N_HEAD):
                sl = slice(h * HEAD, (h + 1) * HEAD)
                st0 = state[b, h]
                st_ref[b, h, 0] = st0
                st1, out = _hgrn_chunk(st0, *[r[b, :, sl].astype(f32) for r in (q_ref, f_ref, i_ref, g_ref)],
                                       l_ref[0:1, sl], l_ref[1:2, sl], ng_ref[...])
                state[b, h] = st1
                o_ref[b, :, sl] = out.astype(bf16)

    return _call(body, "hgrn_fwd", (N,), [col(C_HQ), col(C_HF), col(C_HI), col(C_HG), full(2, 512), full(1, HEAD)],
                 (col(0), st),
                 (jax.ShapeDtypeStruct((Bl, S, 512), bf16), jax.ShapeDtypeStruct((Bl, N_HEAD, N, HEAD, HEAD), f32)),
                 scratch=[pltpu.VMEM((Bl, N_HEAD, HEAD, HEAD), f32)], sem=("arbitrary",))(
        proj, proj, proj, proj, lb_logits, ng)


def _hgrn_bwd(proj, lb_logits, ng, states, db, Bl, S):
    N, col, st, full = _hgrn_specs(S, Bl, True)

    def body(q_ref, f_ref, i_ref, g_ref, l_ref, ng_ref, st_ref, db_ref,
             dq_ref, df_ref, di_ref, dg_ref, dl_ref, dng_ref, dstate):
        @pl.when(pl.program_id(0) == 0)
        def _():
            dstate[...] = jnp.zeros_like(dstate)
            dl_ref[...] = jnp.zeros_like(dl_ref)
            dng_ref[...] = jnp.zeros_like(dng_ref)

        for b in range(Bl):
            for h in range(N_HEAD):
                sl = slice(h * HEAD, (h + 1) * HEAD)
                _, vjp = jax.vjp(_hgrn_chunk, st_ref[b, h, 0], *[r[b, :, sl].astype(f32) for r in (q_ref, f_ref, i_ref, g_ref)],
                                 l_ref[0:1, sl], l_ref[1:2, sl], ng_ref[...])
                dst0, dq, df, di, dg, dl0, dl1, dng = vjp((dstate[b, h], db_ref[b, :, sl].astype(f32)))
                dstate[b, h] = dst0
                dq_ref[b, :, sl] = dq.astype(bf16)
                df_ref[b, :, sl] = df.astype(bf16)
                di_ref[b, :, sl] = di.astype(bf16)
                dg_ref[b, :, sl] = dg.astype(bf16)
                dl_ref[0:1, sl] += dl0
                dl_ref[1:2, sl] += dl1
                dng_ref[b, h] += dng

    return _call(body, "hgrn_bwd", (N,),
                 [col(C_HQ), col(C_HF), col(C_HI), col(C_HG), full(2, 512), full(1, HEAD), st, col(0)],
                 (*[col(0)] * 4, full(2, 512), full(Bl, N_HEAD, 1, HEAD)),
                 (*[jax.ShapeDtypeStruct((Bl, S, 512), bf16)] * 4, jax.ShapeDtypeStruct((2, 512), f32),
                  jax.ShapeDtypeStruct((Bl, N_HEAD, 1, HEAD), f32)),
                 scratch=[pltpu.VMEM((Bl, N_HEAD, HEAD, HEAD), f32)], sem=("arbitrary",))(
        proj, proj, proj, proj, lb_logits, ng, states, db)


def _attn_probs(q, k):
    s = _raw_dot(q, k, "nt") * (HEAD ** -0.5)
    e = jnp.exp(s - jnp.max(s, axis=-1, keepdims=True))
    return e / jnp.sum(e, axis=-1, keepdims=True)


def _attn_specs(S, tq):
    nq = S // tq
    q = pl.BlockSpec((tq, 512), lambda b, i: (b * nq + i, C_XQ // 512))
    kv = pl.BlockSpec((1, MEM_LEN, 1024), lambda b, i: (b, 0, 0))
    o = pl.BlockSpec((tq, 512), lambda b, i: (b * nq + i, 0))
    return nq, q, kv, o


def _attn_fwd(proj, kv, Bl, S):
    tq = _pick(S, (512, 256, 128))
    nq, qs, kvs, os_ = _attn_specs(S, tq)

    def body(q_ref, kv_ref, o_ref):
        for h in range(N_HEAD):
            sl = slice(h * HEAD, (h + 1) * HEAD)
            p = _attn_probs(q_ref[:, sl], kv_ref[0, :, sl])
            o_ref[:, sl] = _raw_dot(p, kv_ref[0, :, 512 + h * HEAD:512 + (h + 1) * HEAD], "nn").astype(bf16)

    return _call(body, "attn_fwd", (Bl, nq), [qs, kvs], os_, jax.ShapeDtypeStruct((Bl * S, 512), bf16),
                 sem=("parallel", "parallel"))(proj, kv)


def _attn_bwd(proj, kv, dc, Bl, S):
    tq = _pick(S, (512, 256, 128))
    nq, qs, kvs, os_ = _attn_specs(S, tq)

    def body(q_ref, kv_ref, do_ref, dq_ref, dkv_ref):
        @pl.when(pl.program_id(1) == 0)
        def _():
            dkv_ref[...] = jnp.zeros_like(dkv_ref)

        for h in range(N_HEAD):
            sl = slice(h * HEAD, (h + 1) * HEAD)
            vsl = slice(512 + h * HEAD, 512 + (h + 1) * HEAD)
            q, k, v, do = q_ref[:, sl], kv_ref[0, :, sl], kv_ref[0, :, vsl], do_ref[:, sl]
            p = _attn_probs(q, k)
            dkv_ref[0, :, vsl] += _raw_dot(p, do, "tn")
            dp = _raw_dot(do, v, "nt")
            ds = p * (dp - jnp.sum(dp * p, axis=-1, keepdims=True)) * (HEAD ** -0.5)
            dq_ref[:, sl] = _raw_dot(ds, k, "nn").astype(bf16)
            dkv_ref[0, :, sl] += _raw_dot(ds, q, "tn")

    return _call(body, "attn_bwd", (Bl, nq), [qs, kvs, os_], (os_, kvs),
                 (jax.ShapeDtypeStruct((Bl * S, 512), bf16), jax.ShapeDtypeStruct((Bl, MEM_LEN, 1024), f32)),
                 sem=("arbitrary", "arbitrary"))(proj, kv, dc)


def _gate_specs(tm):
    half = D_MODEL // 2
    return [pl.BlockSpec((tm, half), functools.partial(lambda i, c: (i, c), c=(C_GL + n * D_MODEL) // half + k))
            for n in range(3) for k in range(2)]


def _merge_out_norm_fwd(branches, wb, proj, w_out, x, g):
    T = proj.shape[0]
    tm = _pick(T, (512, 256, 128))

    def body(a_ref, b_ref, c_ref, w0, w1, w2, g0a, g0b, g1a, g1b, g2a, g2b, wo_ref, x_ref, g_ref, m_ref, x1_ref, h_ref, ht_ref):
        acc = jnp.zeros((tm, D_MODEL), f32)
        for x_n, w_ref, ga, gb in ((a_ref, w0, g0a, g0b), (b_ref, w1, g1a, g1b), (c_ref, w2, g2a, g2b)):
            gate = jax.nn.sigmoid(jnp.concatenate([ga[...], gb[...]], axis=-1).astype(f32))
            acc = acc + gate * _raw_dot(x_n[...], w_ref[...], "nn")
        merged = acc.astype(bf16)
        m_ref[...] = merged
        x1 = x_ref[...] + _raw_dot(merged, wo_ref[...], "nn")
        x1_ref[...] = x1
        y = x1 * lax.rsqrt(jnp.mean(x1 * x1, axis=-1, keepdims=True) + EPS) * g_ref[...]
        h_ref[...] = y.astype(bf16)
        ht_ref[...] = y.T.astype(bf16)

    br = pl.BlockSpec((tm, 512), lambda i: (i, 0))
    w = pl.BlockSpec((512, D_MODEL), lambda i: (0, 0))
    row = pl.BlockSpec((tm, D_MODEL), lambda i: (i, 0))
    return _call(body, "merge_out_norm_fwd", (T // tm,),
                 [br, br, br, w, w, w, *_gate_specs(tm), pl.BlockSpec((D_MODEL, D_MODEL), lambda i: (0, 0)), row,
                  pl.BlockSpec((1, D_MODEL), lambda i: (0, 0))],
                 (row, row, row, pl.BlockSpec((D_MODEL, tm), lambda i: (0, i))),
                 (jax.ShapeDtypeStruct((T, D_MODEL), bf16), jax.ShapeDtypeStruct((T, D_MODEL), f32),
                  jax.ShapeDtypeStruct((T, D_MODEL), bf16), jax.ShapeDtypeStruct((D_MODEL, T), bf16)),
                 sem=("parallel",))(*branches, *wb, *[proj] * 6, w_out, x, g)


def _merge_bwd(branches, wb, proj, dx1, w_out):
    T = proj.shape[0]
    tm = _pick(T, (256, 128))

    def body(a_ref, b_ref, c_ref, w0, w1, w2, g0a, g0b, g1a, g1b, g2a, g2b, dx_ref, wo_ref, dgl_ref, d0, d1, d2, gw_ref):
        @pl.when(pl.program_id(0) == 0)
        def _():
            gw_ref[...] = jnp.zeros_like(gw_ref)

        dm = _raw_dot(dx_ref[...], wo_ref[...], "nt")
        for n, (x_ref, w_ref, ga, gb, d_ref) in enumerate(((a_ref, w0, g0a, g0b, d0), (b_ref, w1, g1a, g1b, d1), (c_ref, w2, g2a, g2b, d2))):
            x, w = x_ref[...], w_ref[...]
            up = _raw_dot(x, w, "nn")
            sg = jax.nn.sigmoid(jnp.concatenate([ga[...], gb[...]], axis=-1).astype(f32))
            dgl_ref[n] = (dm * up * sg * (1.0 - sg)).astype(bf16)
            dup = (dm * sg).astype(bf16)
            d_ref[...] = _raw_dot(dup, w, "nt").astype(bf16)
            gw_ref[n] += _raw_dot(x, dup, "tn")

    br = pl.BlockSpec((tm, 512), lambda i: (i, 0))
    w = pl.BlockSpec((512, D_MODEL), lambda i: (0, 0))
    sh = jax.ShapeDtypeStruct((T, 512), bf16)
    outs = _call(body, "merge_bwd", (T // tm,),
                 [br, br, br, w, w, w, *_gate_specs(tm), pl.BlockSpec((tm, D_MODEL), lambda i: (i, 0)),
                  pl.BlockSpec((D_MODEL, D_MODEL), lambda i: (0, 0))],
                 (pl.BlockSpec((3, tm, D_MODEL), lambda i: (0, i, 0)), br, br, br, pl.BlockSpec((3, 512, D_MODEL), lambda i: (0, 0, 0))),
                 (jax.ShapeDtypeStruct((3, T, D_MODEL), bf16), sh, sh, sh, jax.ShapeDtypeStruct((3, 512, D_MODEL), f32)),
                 sem=("arbitrary",))(*branches, *wb, *[proj] * 6, dx1, w_out)
    return outs[0], outs[1:4], outs[4]


CONV_TC = 256


def _shift_down(a, k):
    r = pltpu.roll(a, k, 0)
    row = lax.broadcasted_iota(jnp.int32, (8, a.shape[1]), 0)
    return jnp.concatenate([jnp.where(row >= k, r[:8], 0.0), r[8:]], axis=0)


def _shift_up(a, k):
    n = a.shape[0]
    r = pltpu.roll(a, n - k, 0)
    row = lax.broadcasted_iota(jnp.int32, (8, a.shape[1]), 0)
    return jnp.concatenate([r[:n - 8], jnp.where(row < 8 - k, r[n - 8:], 0.0)], axis=0)


def _conv_pre(a, a1, a2, cw, cb):
    return cb + cw[0:1] * a2 + cw[1:2] * a1 + cw[2:3] * a


def _up_conv_fwd(h2, w_up, cw, cb):
    Bl, S, Dd = h2.shape
    nc = D_FF // CONV_TC

    def body(h_ref, wa_ref, wb_ref, cw_ref, cb_ref, a_ref, b_ref, o_ref):
        a16 = _raw_dot(h_ref[0], wa_ref[...], "nn").astype(bf16)
        b16 = _raw_dot(h_ref[0], wb_ref[...], "nn").astype(bf16)
        a_ref[0], b_ref[0] = a16, b16
        a = a16.astype(f32)
        ac = _conv_pre(a, _shift_down(a, 1), _shift_down(a, 2), cw_ref[...], cb_ref[...])
        o_ref[0] = (jax.nn.silu(ac) * b16.astype(f32)).astype(bf16)

    seq = pl.BlockSpec((1, S, CONV_TC), lambda b, c: (b, 0, c))
    sh = jax.ShapeDtypeStruct((Bl, S, D_FF), bf16)
    return _call(body, "up_conv_fwd", (Bl, nc),
                 [pl.BlockSpec((1, S, Dd), lambda b, c: (b, 0, 0)), pl.BlockSpec((Dd, CONV_TC), lambda b, c: (0, c)),
                  pl.BlockSpec((Dd, CONV_TC), lambda b, c: (0, nc + c)), pl.BlockSpec((3, CONV_TC), lambda b, c: (0, c)),
                  pl.BlockSpec((1, CONV_TC), lambda b, c: (0, c))],
                 (seq, seq, seq), (sh, sh, sh), sem=("parallel", "parallel"))(h2, w_up, w_up, cw, cb)


def _down_conv_bwd(dx2, w_down, a, b, cw, cb):
    Bl, S, Dd = dx2.shape
    nc = D_FF // CONV_TC

    def body(dx_ref, wd_ref, a_ref, b_ref, cw_ref, cb_ref, da_ref, db_ref, dcw_ref, dcb_ref):
        dact = _raw_dot(dx_ref[0], wd_ref[...], "nt").astype(bf16).astype(f32)
        a, cw = a_ref[0].astype(f32), cw_ref[...]
        a1, a2 = _shift_down(a, 1), _shift_down(a, 2)
        ac = _conv_pre(a, a1, a2, cw, cb_ref[...])
        sg = jax.nn.sigmoid(ac)
        gated = dact * sg
        db_ref[0] = (gated * ac).astype(bf16)
        dac = gated * b_ref[0].astype(f32) * (1.0 + ac * (1.0 - sg))
        da_ref[0] = (cw[2:3] * dac + cw[1:2] * _shift_up(dac, 1) + cw[0:1] * _shift_up(dac, 2)).astype(bf16)
        dcw_ref[0, 0:1, :] = jnp.sum(dac * a2, axis=0, keepdims=True)
        dcw_ref[0, 1:2, :] = jnp.sum(dac * a1, axis=0, keepdims=True)
        dcw_ref[0, 2:3, :] = jnp.sum(dac * a, axis=0, keepdims=True)
        dcb_ref[0] = jnp.sum(dac, axis=0, keepdims=True)

    seq = pl.BlockSpec((1, S, CONV_TC), lambda b_, c: (b_, 0, c))
    sh = jax.ShapeDtypeStruct((Bl, S, D_FF), bf16)
    return _call(body, "down_conv_bwd", (Bl, nc),
                 [pl.BlockSpec((1, S, Dd), lambda b_, c: (b_, 0, 0)), pl.BlockSpec((CONV_TC, Dd), lambda b_, c: (c, 0)), seq, seq,
                  pl.BlockSpec((3, CONV_TC), lambda b_, c: (0, c)), pl.BlockSpec((1, CONV_TC), lambda b_, c: (0, c))],
                 (seq, seq, pl.BlockSpec((1, 3, CONV_TC), lambda b_, c: (b_, 0, c)), pl.BlockSpec((1, 1, CONV_TC), lambda b_, c: (b_, 0, c))),
                 (sh, sh, jax.ShapeDtypeStruct((Bl, 3, D_FF), f32), jax.ShapeDtypeStruct((Bl, 1, D_FF), f32)),
                 sem=("parallel", "parallel"))(dx2, w_down, a, b, cw, cb)


def _local_step(x, mem, target, p, w_in, late_b, late_c, send, settle):
    Bl, S, Dd = x.shape
    T = Bl * S
    x2d, t2d, mem2d = x.reshape(T, Dd), target.reshape(T, Dd), mem.reshape(Bl * MEM_LEN, Dd)
    b_st = jnp.pad(p["b_spatial"].T, ((0, 0), (0, 128 - N_HEAD)))
    lbl = p["lb_logits"]

    h, h_t = _rms_fwd(x2d, p["norm1_g"], "norm1_fwd", transposed=True)
    proj = _mm(h, w_in, "nn", bf16, "proj_fwd", 1024, 1664)
    a_out = _gmlp_fwd(proj, p["ln_v_g"], p["ln_v_b"], p["w_spatial"], b_st)
    proj3 = proj.reshape(Bl, S, IN_WIDTH)
    b_out, states = _hgrn_fwd(proj3, lbl, p["hgrn_norm_g"], Bl, S)
    b_out = b_out.reshape(T, 512)
    memn = _rms_fwd(mem2d, p["mem_norm_g"], "memnorm_fwd")
    w = late_b(b_out)
    wb = w["w_branch"]
    kv = _mm(memn, w["w_mem_kv"], "nn", f32, "kv_fwd", 512, 1024).reshape(Bl, MEM_LEN, 2 * 512)
    c_out = _attn_fwd(proj, kv, Bl, S)
    branches = (a_out, b_out, c_out)
    merged, x1, h2, h2_t = _merge_out_norm_fwd(branches, wb, proj, w["w_out"], x2d, p["norm2_g"])
    w.update(late_c(h2))
    ffn_a, ffn_b, act = _up_conv_fwd(h2.reshape(Bl, S, Dd), w["w_up"], w["conv_w"], p["conv_b"])
    act = act.reshape(T, D_FF)
    loss_part, dx2, dx2_16, g_final = _down_final_loss(act, w["w_down"], x1, p["final_g"], t2d)

    g_w_down = _mm(act, dx2_16, "tn", bf16, "down_dw", 1408, 1024, 1024)
    da, db, g_conv_w, g_conv_b = _down_conv_bwd(dx2_16.reshape(Bl, S, Dd), w["w_down"], ffn_a, ffn_b, w["conv_w"], p["conv_b"])
    da, db = da.reshape(T, D_FF), db.reshape(T, D_FF)
    shard = 2 * D_FF // N_DEV
    g_w_up = jnp.concatenate([_mm(h2_t, d, "nn", bf16, f"up_dw_{n}", 512, 1408, shard=shard, n_outer=True)
                              for n, d in (("a", da), ("b", db))], axis=0)
    tok = send("c", dict(w_up=g_w_up, conv_w=jnp.sum(g_conv_w, axis=0), w_down=g_w_down))
    dx1, g_norm2 = _mm_rms_bwd([(da, w["w_up_a"]), (db, w["w_up_b"])], x1, p["norm2_g"] + tok[0, 0], dx2, "up_dx_norm2_bwd", 256)

    g_w_out = _mm(merged, dx1, "tn", bf16, "out_dw", 1024, 1024, 1024)
    dgl, dbr, g_w_branch = _merge_bwd(branches, wb, proj, dx1, w["w_out"])
    dxq, dkv = _attn_bwd(proj, kv, dbr[2], Bl, S)
    dkv = dkv.reshape(Bl * MEM_LEN, 2 * 512)
    g_w_kv = _mm(memn, dkv, "tn", bf16, "kv_dw", 1024, 1024, 512)
    tok = send("b", dict(w_mem_kv=g_w_kv, w_branch=g_w_branch, w_out=g_w_out))
    dmemn = _mm(dkv, w["w_mem_kv"], "nt", f32, "kv_dx", 512, 1024)
    _, g_mem_norm = _rms_bwd(mem2d, p["mem_norm_g"], dmemn, "memnorm_bwd")
    dzuv, g_ln_g, g_ln_b, g_w_sp, g_b_sp = _gmlp_bwd(proj, p["ln_v_g"] + tok[0, 0], p["ln_v_b"], p["w_spatial"], b_st, dbr[0])
    *dqfig, g_lbl, g_ng = _hgrn_bwd(proj3, lbl, p["hgrn_norm_g"], states, dbr[1].reshape(Bl, S, 512), Bl, S)
    dq, df, di, dg = [d.reshape(T, 512) for d in dqfig]
    dproj = jnp.concatenate([dzuv, dq, df, di, dg, dxq, dgl[0], dgl[1], dgl[2]], axis=-1)
    g_w_in = _mm(h_t, dproj, "nn", bf16, "proj_dw", 512, 1664, shard=IN_WIDTH // N_DEV, n_outer=True)
    tok = send("a", dict(w_in=g_w_in))
    dx, g_norm1 = _mm_rms_bwd([(settle(dproj), w_in)], x2d, p["norm1_g"] + tok[0, 0], dx1, "proj_dx_norm1_bwd", 256)

    gs = dict(w_spatial=g_w_sp, norm1_g=g_norm1, mem_norm_g=g_mem_norm, norm2_g=g_norm2, final_g=g_final, lb_logits=g_lbl,
              ln_v_g=g_ln_g, ln_v_b=g_ln_b, b_spatial=g_b_sp, hgrn_norm_g=g_ng, conv_b=g_conv_b)
    return loss_part, dx.reshape(Bl, S, Dd), gs


def _coords():
    return lax.axis_index("x"), lax.axis_index("y"), lax.axis_index("c")


def _slot(dev):
    return 4 * dev[0] + 2 * dev[1] + dev[2]


def _comm_call(body, name, arrays, out_shapes, n_sem):
    n = len(arrays)
    hbm = pl.BlockSpec(memory_space=pl.ANY)
    return pl.pallas_call(
        body, name=name, out_shape=out_shapes, in_specs=[hbm] * n, out_specs=[hbm] * n,
        scratch_shapes=[pltpu.SemaphoreType.DMA((n_sem, n)), pltpu.SemaphoreType.DMA((n_sem, n)), pltpu.SemaphoreType.DMA((n,))])(*arrays)


def _all_gather(blocks, name):
    n = len(blocks)

    def body(*refs):
        x_refs, o_refs, (send_sems, recv_sems, local_sems) = refs[:n], refs[n:2 * n], refs[2 * n:]
        x, y, c = _coords()
        me, sibling = (x, y, c), (x, y, 1 - c)
        chips = [(1 - x, y), (x, 1 - y), (1 - x, 1 - y)]

        def copy(a, k, block_dev, to, from_input=False):
            dst = o_refs[a].at[_slot(block_dev)]
            return pltpu.make_async_remote_copy(src_ref=x_refs[a] if from_input else dst, dst_ref=dst, send_sem=send_sems.at[k, a],
                                                recv_sem=recv_sems.at[k, a], device_id=to, device_id_type=MESH)

        mine = [pltpu.make_async_copy(x_refs[a], o_refs[a].at[_slot(me)], local_sems.at[a]) for a in range(n)]
        first = [copy(a, 0, me, sibling, True) for a in range(n)]
        first += [copy(a, 1 + j, me, (*chip, c), True) for j, chip in enumerate(chips) for a in range(n)]
        for cp in mine + first:
            cp.start()
        passed = []
        for j, chip in enumerate(chips):
            for a in range(n):
                copy(a, 1 + j, (*chip, c), me).wait_recv()
                fwd = copy(a, 4 + j, (*chip, c), sibling)
                fwd.start()
                passed.append(fwd)
        for a in range(n):
            copy(a, 0, sibling, me).wait_recv()
        for j, chip in enumerate(chips):
            for a in range(n):
                copy(a, 4 + j, (*chip, 1 - c), me).wait_recv()
        for cp in first + passed:
            cp.wait_send()
        for cp in mine:
            cp.wait()

    return _comm_call(body, name, blocks, [jax.ShapeDtypeStruct((N_DEV,) + b.shape, b.dtype) for b in blocks], 7)


def _all_to_all(parts, name):
    n = len(parts)
    rel = [(0, 0, 1), (0, 1, 0), (0, 1, 1), (1, 0, 0), (1, 0, 1), (1, 1, 0), (1, 1, 1)]

    def body(*refs):
        x_refs, o_refs, (send_sems, recv_sems, local_sems) = refs[:n], refs[n:2 * n], refs[2 * n:]
        x, y, c = _coords()
        me = (x, y, c)
        peers = [(x ^ dx, y ^ dy, c ^ dc) for dx, dy, dc in rel]

        def copy(a, k, peer):
            return pltpu.make_async_remote_copy(src_ref=x_refs[a].at[_slot(peer)], dst_ref=o_refs[a].at[_slot(me)], send_sem=send_sems.at[k, a],
                                                recv_sem=recv_sems.at[k, a], device_id=peer, device_id_type=MESH)

        def arrival(a, k, peer):
            return pltpu.make_async_remote_copy(src_ref=x_refs[a].at[_slot(me)], dst_ref=o_refs[a].at[_slot(peer)], send_sem=send_sems.at[k, a],
                                                recv_sem=recv_sems.at[k, a], device_id=peer, device_id_type=MESH)

        mine = [pltpu.make_async_copy(x_refs[a].at[_slot(me)], o_refs[a].at[_slot(me)], local_sems.at[a]) for a in range(n)]
        sends = [copy(a, k, peer) for k, peer in enumerate(peers) for a in range(n)]
        for cp in mine + sends:
            cp.start()
        for k, peer in enumerate(peers):
            for a in range(n):
                arrival(a, k, peer).wait_recv()
        for cp in sends:
            cp.wait_send()
        for cp in mine:
            cp.wait()

    return _comm_call(body, name, parts, [jax.ShapeDtypeStruct(p.shape, p.dtype) for p in parts], 7)


_HBM = pl.BlockSpec(memory_space=pltpu.HBM)
_SEM = pl.BlockSpec(memory_space=pltpu.SEMAPHORE)
_REL = [(0, 0, 1), (0, 1, 0), (0, 1, 1), (1, 0, 0), (1, 0, 1), (1, 1, 0), (1, 1, 1)]


_LINK_ORDER = (3, 1, 5, 4, 2, 6, 0)
SEND_PIECES = 4


def _pieces(shape, dtype):
    rows = shape[0]
    unit = 1 if len(shape) > 2 else (16 if dtype == bf16 else 8)
    for n in (SEND_PIECES, 2):
        if rows % (n * unit) == 0:
            return [pl.ds(i * (rows // n), rows // n) for i in range(n)]
    return [pl.ds(0, rows)]


def _split_copies(gather, src, land, send, recv, pieces):
    x, y, c = _coords()
    me = (x, y, c)
    copies = []
    for a in range(len(src)):
        block = src[a].shape if gather else src[a].shape[1:]
        for rows in (_pieces(block, src[a].dtype) if pieces else [None]):
            for k in _LINK_ORDER:
                dx, dy, dc = _REL[k]
                peer = (x ^ dx, y ^ dy, c ^ dc)
                mine, there = (src[a] if gather else src[a].at[_slot(peer)]), land[a].at[_slot(me)]
                if rows is not None:
                    mine, there = mine.at[rows], there.at[rows]
                copies.append(pltpu.make_async_remote_copy(src_ref=mine, dst_ref=there, send_sem=send[a].at[k], recv_sem=recv[a].at[k],
                                                           device_id=peer, device_id_type=MESH))
    return me, copies


def _arrivals(gather, src, land, send, recv):
    x, y, c = _coords()
    out = []
    for a in range(len(src)):
        for k, (dx, dy, dc) in enumerate(_REL):
            peer = (x ^ dx, y ^ dy, c ^ dc)
            out.append(pltpu.make_async_remote_copy(src_ref=src[a] if gather else src[a].at[_slot(peer)], dst_ref=land[a].at[_slot(peer)],
                                                    send_sem=send[a].at[k], recv_sem=recv[a].at[k], device_id=peer, device_id_type=MESH))
    return out


def _exchange_start(arrays, gather, name, after=None):
    n = len(arrays)
    e = 0 if after is None else 1
    lands = [lax.empty(((N_DEV,) + a.shape) if gather else a.shape, a.dtype) for a in arrays]

    def body(*refs):
        src, land = refs[:n], refs[n:2 * n]
        refs = refs[2 * n + e:]
        send, recv, token, local_sems = refs[:n], refs[n:2 * n], refs[4 * n], refs[4 * n + 1]
        me, out = _split_copies(gather, src, land, send, recv, True)
        local = [pltpu.make_async_copy(src[a] if gather else src[a].at[_slot(me)], land[a].at[_slot(me)], local_sems.at[a])
                 for a in range(n)]
        for cp in local:
            cp.start()
        for cp in local:
            cp.wait()
        for cp in out:
            cp.start()
        token[...] = jnp.zeros_like(token)

    sems = [pltpu.SemaphoreType.DMA((7,)) for _ in range(2 * n)]
    outs = pl.pallas_call(
        body, name=name,
        out_shape=(*sems, *[pltpu.HBM(a.shape, a.dtype) for a in arrays], *[pltpu.HBM(l.shape, l.dtype) for l in lands],
                   jax.ShapeDtypeStruct((8, 128), f32)),
        in_specs=[_HBM] * (2 * n) + [pl.BlockSpec(memory_space=pl.ANY)] * e,
        out_specs=(*[_SEM] * (2 * n), *[_HBM] * (2 * n), pl.BlockSpec(memory_space=pltpu.VMEM)),
        input_output_aliases={i: 2 * n + i for i in range(2 * n)},
        scratch_shapes=[pltpu.SemaphoreType.DMA((n,))],
        compiler_params=pltpu.CompilerParams(has_side_effects=pltpu.SideEffectType.DATAFLOW_SIDE_EFFECTING))(
        *[pltpu.with_memory_space_constraint(a, pltpu.HBM) for a in arrays],
        *[pltpu.with_memory_space_constraint(l, pltpu.HBM) for l in lands], *([after] if e else []))
    return (gather, n, outs[:4 * n]), outs[4 * n]


def _exchange_wait(handle, which, after, name):
    gather, n_all, vals = handle
    send_v, recv_v, src_v, land_v = [[vals[g * n_all + i] for i in which] for g in range(4)]
    n = len(which)

    def body(*refs):
        src, land, send, recv = refs[:n], refs[n:2 * n], refs[2 * n:3 * n], refs[3 * n:4 * n]
        for cp in _split_copies(gather, src, land, send, recv, False)[1]:
            cp.wait_send()
        for cp in _arrivals(gather, src, land, send, recv):
            cp.wait_recv()

    outs = pl.pallas_call(
        body, name=name,
        out_shape=(*[pltpu.HBM(a.shape, a.dtype) for a in src_v], *[pltpu.HBM(l.shape, l.dtype) for l in land_v]),
        in_specs=[*[_HBM] * (2 * n), *[_SEM] * (2 * n), pl.BlockSpec(memory_space=pl.ANY)], out_specs=[_HBM] * (2 * n),
        input_output_aliases={i: i for i in range(2 * n)},
        compiler_params=pltpu.CompilerParams(has_side_effects=pltpu.SideEffectType.DATAFLOW_SIDE_EFFECTING))(
        *src_v, *land_v, *send_v, *recv_v, after)
    return outs[n:]


def _seq_exchange(arrays, gather, name, collective_id):
    n = len(arrays)
    hbm = pltpu.MemorySpace.HBM
    srcs = [jax.new_ref(a, memory_space=hbm) for a in arrays]
    lands = [jax.empty_ref(jax.ShapeDtypeStruct(((N_DEV,) + a.shape) if gather else a.shape, a.dtype), memory_space=hbm) for a in arrays]

    @pl.kernel(mesh=plsc.ScalarSubcoreMesh(axis_name="sequencer", num_cores=1), name=name,
               scratch_types=(pltpu.SemaphoreType.DMA((7, n)), pltpu.SemaphoreType.DMA((7, n)), pltpu.SemaphoreType.DMA((n,))),
               compiler_params=pltpu.CompilerParams(collective_id=collective_id))
    def launch(send, recv, local):
        x, y, c = _coords()
        me = (x, y, c)
        peers = [(x ^ dx, y ^ dy, c ^ dc) for dx, dy, dc in _REL]
        barrier = pltpu.get_barrier_semaphore()
        for peer in peers:
            pl.semaphore_signal(barrier, inc=1, device_id=peer, device_id_type=MESH)
        pl.semaphore_wait(barrier, len(peers))

        def copy(a, k, peer, arrival):
            return pltpu.make_async_remote_copy(
                src_ref=srcs[a] if gather else srcs[a].at[_slot(peer)], dst_ref=lands[a].at[_slot(peer if arrival else me)],
                send_sem=send.at[k, a], recv_sem=recv.at[k, a], device_id=peer, device_id_type=MESH)

        mine = [pltpu.make_async_copy(srcs[a] if gather else srcs[a].at[_slot(me)], lands[a].at[_slot(me)], local.at[a])
                for a in range(n)]
        out = [copy(a, k, peer, False) for a in range(n) for k, peer in enumerate(peers)]
        for cp in mine + out:
            cp.start()
        for a in range(n):
            for k, peer in enumerate(peers):
                copy(a, k, peer, True).wait_recv()
        for cp in out:
            cp.wait_send()
        for cp in mine:
            cp.wait()

    launch()
    return [land[...] for land in lands]


def _adam_math(w, g, m, v):
    m_ = ADAM_B1 * m + (1.0 - ADAM_B1) * g
    v_ = ADAM_B2 * v + (1.0 - ADAM_B2) * jnp.square(g)
    m_hat = m_ / (1.0 - ADAM_B1 ** ADAM_STEP)
    v_hat = v_ / (1.0 - ADAM_B2 ** ADAM_STEP)
    return -ADAM_LR * (m_hat / (jnp.sqrt(v_hat) + ADAM_EPS) + ADAM_WD * w), m_, v_


def _reduce_adamw(parts, w, m, v, name):
    _, R, L = parts.shape
    tr = _pick(R, (256, 128, 64, 32, 16, 8))

    def body(p_ref, w_ref, m_ref, v_ref, g_ref, d_ref, nm_ref, nv_ref):
        g = p_ref[0].astype(f32)
        for i in range(1, N_DEV):
            g = g + p_ref[i].astype(f32)
        g_ref[...] = g
        d_ref[...], nm_ref[...], nv_ref[...] = _adam_math(w_ref[...], g, m_ref[...], v_ref[...])

    blk = pl.BlockSpec((tr, L), lambda i: (i, 0))
    sh = jax.ShapeDtypeStruct((R, L), f32)
    return _call(body, name, (R // tr,), [pl.BlockSpec((N_DEV, tr, L), lambda i: (0, i, 0)), blk, blk, blk], (blk,) * 4, (sh,) * 4,
                 sem=("parallel",))(parts, w, m, v)


SMALL = (("w_spatial", (512, 128), 0), ("norm1_g", (1, 1024), 512), ("mem_norm_g", (1, 1024), 520), ("norm2_g", (1, 1024), 528),
         ("final_g", (1, 1024), 536), ("lb_logits", (2, 512), 544), ("ln_v_g", (1, 512), 552), ("ln_v_b", (1, 512), 556),
         ("b_spatial", (4, 128), 560), ("hgrn_norm_g", (1, 128), 564), ("conv_b", (1, 2816), 565))
LOSS_ROW, SMALL_USED, SMALL_ROWS = 587, 588, 640


def _segments(shape, base):
    r, n = shape
    per = n // 128
    return [(base + i * per + j, i, slice(j * 128, (j + 1) * 128)) for i in range(r) for j in range(per)]


def _pack_small(gs, loss_part):
    names = [n for n, _, _ in SMALL]

    def body(*refs):
        src, loss_ref, o_ref = dict(zip(names, refs[:-2])), refs[-2], refs[-1]
        o_ref[SMALL_USED:SMALL_ROWS, :] = jnp.zeros((SMALL_ROWS - SMALL_USED, 128), f32)
        o_ref[LOSS_ROW:LOSS_ROW + 1, :] = loss_ref[...]
        for name, shape, base in SMALL:
            ref = src[name]
            if name == "w_spatial":
                o_ref[base:base + 512, :] = ref[...].reshape(512, 128)
            elif name == "b_spatial":
                o_ref[base:base + 4, :] = ref[0:4, :]
            elif name == "conv_b":
                per_example = functools.reduce(lambda u, v_: u + v_, [ref[b] for b in range(ref.shape[0])])
                for row, i, sl in _segments(shape, base):
                    o_ref[row:row + 1, :] = per_example[i:i + 1, sl]
            elif name == "hgrn_norm_g":
                per_head = [ref[b, h] for b in range(ref.shape[0]) for h in range(N_HEAD)]
                o_ref[base:base + 1, :] = functools.reduce(lambda u, v_: u + v_, per_head)
            else:
                for row, i, sl in _segments(shape, base):
                    o_ref[row:row + 1, :] = ref[i:i + 1, sl]

    return pl.pallas_call(body, name="pack_small", out_shape=jax.ShapeDtypeStruct((SMALL_ROWS, 128), f32))(
        *[gs[n] for n in names], loss_part)


def _small_update(gathered, w, m, v):
    names = [n for n, _, _ in SMALL]
    k = len(names)

    def body(*refs):
        p_ref = refs[0]
        ins = [dict(zip(names, refs[1 + i * k:1 + (i + 1) * k])) for i in range(3)]
        outs = [dict(zip(names, refs[1 + (3 + i) * k:1 + (4 + i) * k])) for i in range(4)]
        loss_ref, gsum = refs[-2], refs[-1]
        g = p_ref[0]
        for i in range(1, N_DEV):
            g = g + p_ref[i]
        gsum[...] = g
        loss_ref[...] = gsum[LOSS_ROW:LOSS_ROW + 1, :]
        for name, shape, base in SMALL:
            if name == "w_spatial":
                where = [(slice(base, base + 512), (slice(None), slice(None)))]
            else:
                where = [(slice(row, row + 1), (slice(i, i + 1), sl)) for row, i, sl in _segments(shape, base)]
            for rows, at in where:
                g_ = gsum[rows, :]
                d_, m_, v_ = _adam_math(ins[0][name][at], g_, ins[1][name][at], ins[2][name][at])
                for o, val in zip(outs, (g_, d_, m_, v_)):
                    o[name][at] = val

    args = [gathered] + [d[n] for d in (w, m, v) for n in names]
    out_shapes = [jax.ShapeDtypeStruct(shape, f32) for _ in range(4) for _, shape, _ in SMALL] + [jax.ShapeDtypeStruct((1, 128), f32)]
    outs = pl.pallas_call(body, name="small_update", out_shape=out_shapes, scratch_shapes=[pltpu.VMEM((SMALL_ROWS, 128), f32)])(*args)
    return [dict(zip(names, outs[i * k:(i + 1) * k])) for i in range(4)], outs[-1]


def _cols_full(g):
    return jnp.moveaxis(g, 0, -2).reshape(g.shape[1:-1] + (N_DEV * g.shape[-1],))


def _cols_parts(full):
    n = full.shape[-1] // N_DEV
    return jnp.moveaxis(full.reshape(full.shape[:-1] + (N_DEV, n)), -2, 0)


def kernel(x, mem, norm1_g, w_in, ln_v_g, ln_v_b, w_spatial, b_spatial, lb_logits, hgrn_norm_g, mem_norm_g, w_mem_kv, w_branch, w_out, norm2_g, w_up, conv_w, conv_b, w_down, final_g, loss_target, m_norm1_g, m_w_in, m_ln_v_g, m_ln_v_b, m_w_spatial, m_b_spatial, m_lb_logits, m_hgrn_norm_g, m_mem_norm_g, m_w_mem_kv, m_w_branch, m_w_out, m_norm2_g, m_w_up, m_conv_w, m_conv_b, m_w_down, m_final_g, v_norm1_g, v_w_in, v_ln_v_g, v_ln_v_b, v_w_spatial, v_b_spatial, v_lb_logits, v_hgrn_norm_g, v_mem_norm_g, v_w_mem_kv, v_w_branch, v_w_out, v_norm2_g, v_w_up, v_conv_w, v_conv_b, v_w_down, v_final_g):
    given = dict(locals())
    order = ("norm1_g", "w_in", "ln_v_g", "ln_v_b", "w_spatial", "b_spatial", "lb_logits", "hgrn_norm_g", "mem_norm_g",
             "w_mem_kv", "w_branch", "w_out", "norm2_g", "w_up", "conv_w", "conv_b", "w_down", "final_g")
    groups = dict(a=("w_in",), b=("w_mem_kv", "w_branch", "w_out"), c=("w_up", "conv_w", "w_down"))

    wire = {n: given[n][0].astype(f32 if n == "conv_w" else bf16) for ns in groups.values() for n in ns}
    g_in = _all_gather([wire["w_in"]], "gather_w_in")[0]
    w_in_full = _cols_full(g_in)
    w_in_full, wire_b, wire_c = lax.optimization_barrier((w_in_full, [wire[n] for n in groups["b"]], [wire[n] for n in groups["c"]]))
    rest_b = _seq_exchange(wire_b, True, "gather_b", 1)
    rest_c = _seq_exchange(wire_c, True, "gather_c", 6)

    def late_b(after):
        _, (kv_, br_, out_) = lax.optimization_barrier((after, tuple(rest_b)))
        br_ = _cols_full(br_)
        return dict(w_mem_kv=kv_.reshape(D_MODEL, 2 * 512), w_branch=[br_[n] for n in range(3)], w_out=out_.reshape(D_MODEL, D_MODEL))

    def late_c(after):
        _, (up_, cw_, down_) = lax.optimization_barrier((after, tuple(rest_c)))
        up_ = _cols_full(up_)
        return dict(w_up=up_, w_up_a=up_[:, :D_FF], w_up_b=up_[:, D_FF:], conv_w=_cols_full(cw_), w_down=down_.reshape(D_FF, D_MODEL))

    to_parts = dict(w_in=lambda g_: g_, w_up=lambda g_: g_, conv_w=_cols_parts,
                    w_branch=lambda g_: _cols_parts(g_.astype(bf16)).reshape(N_DEV, -1, 128),
                    w_mem_kv=lambda g_: g_.reshape(N_DEV, -1, 2 * 512), w_out=lambda g_: g_.reshape(N_DEV, -1, D_MODEL),
                    w_down=lambda g_: g_.reshape(N_DEV, -1, D_MODEL))
    scatters = {}

    def send(tag, grads_):
        parts = [to_parts[n](grads_[n]) for n in groups[tag]]
        scatters[tag] = _seq_exchange(parts, False, f"scatter_{tag}", dict(a=2, b=4, c=5)[tag])
        return jnp.zeros((8, 128), f32)

    small_2d = lambda prefix: {n: given[prefix + n].reshape(shape) for n, shape, _ in SMALL}
    p = small_2d("")
    p["w_spatial"] = w_spatial[0]
    updates = {}

    def update(tag):
        for n, parts in zip(groups[tag], scatters[tag]):
            two_d = (-1, given[n].shape[-1])
            updates[n] = _reduce_adamw(parts, *[given[pre + n].reshape(two_d) for pre in ("", "m_", "v_")], "adamw_" + n)

    def settle(chain):
        update("c")
        update("b")
        early = groups["c"] + groups["b"]
        chain, tied = lax.optimization_barrier((chain, [updates[n] for n in early]))
        updates.update(zip(early, tied))
        return chain

    loss_part, grad_x, gs = _local_step(x, mem, loss_target, p, w_in_full, late_b, late_c, send, settle)

    gathered = _seq_exchange([_pack_small(gs, loss_part)], True, "gather_small", 3)[0]

    update("a")
    grads, delta, new_m, new_v = {}, {}, {}, {}
    for n, res in updates.items():
        grads[n], delta[n], new_m[n], new_v[n] = [r.reshape(given[n].shape) for r in res]

    small_results, loss_row = _small_update(gathered, small_2d(""), small_2d("m_"), small_2d("v_"))
    for dst, res in zip((grads, delta, new_m, new_v), small_results):
        for n, _, _ in SMALL:
            dst[n] = res[n].reshape(given[n].shape)
    loss = loss_row[0, 0]

    return (loss, grad_x, *[grads[n] for n in order], *[delta[n] for n in order], *[new_m[n] for n in order],
            *[new_v[n] for n in order])
```

```python
import functools

import jax
import jax.numpy as jnp
from jax import lax
from jax.experimental import pallas as pl
from jax.experimental.pallas import tpu as pltpu
from jax.experimental.pallas import tpu_sc as plsc

f32 = jnp.float32
bf16 = jnp.bfloat16

N_DEV = 8
D_MODEL = 1024
EPS = 1e-6
GM_CHUNK = 128
HG_CHUNK = 64
HEAD = 128
N_HEAD = 4
MEM_LEN = 256
D_FF = 2816
IN_WIDTH = 6656
C_ZU, C_HQ, C_HF, C_HI, C_HG, C_XQ, C_GL = 0, 1024, 1536, 2048, 2560, 3072, 3584
ADAM_LR, ADAM_B1, ADAM_B2, ADAM_EPS, ADAM_WD, ADAM_STEP = 0.001, 0.9, 0.999, 1e-08, 0.01, 10
VMEM_LIMIT = 56 * 1024 * 1024
MESH = pl.DeviceIdType.MESH


def _pick(n, cands):
    for c in cands:
        if n % c == 0:
            return c
    return n


def _call(body, name, grid, in_specs, out_specs, out_shape, scratch=(), sem=None, **cp):
    params = dict(vmem_limit_bytes=VMEM_LIMIT, **cp)
    if sem is not None:
        params["dimension_semantics"] = sem
    return pl.pallas_call(
        body, name=name, grid=grid, in_specs=in_specs, out_specs=out_specs, out_shape=out_shape,
        scratch_shapes=list(scratch), compiler_params=pltpu.CompilerParams(**params))


_DN = {"nn": (((1,), (0,)), ((), ())), "nt": (((1,), (1,)), ((), ())), "tn": (((0,), (0,)), ((), ()))}


def _raw_dot(a, b, mode):
    return lax.dot_general(a.astype(bf16), b.astype(bf16), _DN[mode], preferred_element_type=f32)


@jax.custom_vjp
def _dot_nn(a, b):
    return _raw_dot(a, b, "nn")


_dot_nn.defvjp(lambda a, b: (_raw_dot(a, b, "nn"), (a, b)),
               lambda r, g: (_raw_dot(g, r[1], "nt"), _raw_dot(r[0], g, "tn")))


@jax.custom_vjp
def _dot_nt(a, b):
    return _raw_dot(a, b, "nt")


_dot_nt.defvjp(lambda a, b: (_raw_dot(a, b, "nt"), (a, b)),
               lambda r, g: (_raw_dot(g, r[1], "nn"), _raw_dot(g, r[0], "tn")))


@jax.custom_vjp
def _dot_tn(a, b):
    return _raw_dot(a, b, "tn")


_dot_tn.defvjp(lambda a, b: (_raw_dot(a, b, "tn"), (a, b)),
               lambda r, g: (_raw_dot(r[1], g, "nt"), _raw_dot(r[0], g, "nn")))


def _tri(n, lower):
    r = lax.broadcasted_iota(jnp.int32, (n, n), 0)
    c = lax.broadcasted_iota(jnp.int32, (n, n), 1)
    return ((c <= r) if lower else (c >= r)).astype(f32)


def _sel_dot(sel, x, mode, x_first=False, pieces=3):
    sel = sel.astype(bf16)
    out, rest = None, x
    for p in range(pieces):
        piece = rest.astype(bf16)
        part = lax.dot_general(*((piece, sel) if x_first else (sel, piece)), _DN[mode], preferred_element_type=f32)
        out = part if out is None else out + part
        if p + 1 < pieces:
            rest = rest - piece.astype(f32)
    return out


def _egrad(fn, x, ct):
    return jax.vjp(fn, x)[1](ct)[0]


def _mm(a, b, mode, out_dtype, name, tm, tn, tk=None, residual=None, shard=None, n_outer=False, into=None):
    if mode == "nn":
        (M, K), (_, N) = a.shape, b.shape
    elif mode == "nt":
        (M, K), (N, _) = a.shape, b.shape
    else:
        (K, M), (_, N) = a.shape, b.shape
    tm, tn = min(tm, M), min(tn, N)
    tk = K if tk is None else min(tk, K)
    assert M % tm == 0 and N % tn == 0 and K % tk == 0, (name, M, N, K, tm, tn, tk)
    nk = K // tk

    def body(*refs):
        acc_ref = refs[-1] if nk > 1 else None
        refs = refs[:-1] if nk > 1 else refs
        if residual is None:
            a_ref, b_ref, *_, o_ref = refs
        else:
            a_ref, b_ref, r_ref, o_ref = refs

        def finish(r):
            if residual is not None:
                r = r + r_ref[...]
            if shard is None:
                o_ref[...] = r.astype(out_dtype)
            else:
                for s in range(tn // shard):
                    o_ref[s] = r[:, s * shard:(s + 1) * shard].astype(out_dtype)

        part = _raw_dot(a_ref[...], b_ref[...], mode)
        if nk == 1:
            finish(part)
            return
        k = pl.program_id(2)

        @pl.when(k == 0)
        def _():
            acc_ref[...] = part

        @pl.when((k > 0) & (k < nk - 1))
        def _():
            acc_ref[...] += part

        @pl.when(k == nk - 1)
        def _():
            finish(acc_ref[...] + part)

    def at(index):
        return (lambda j, i, k: index(i, j, k)) if n_outer else index

    a_spec = {"nn": pl.BlockSpec((tm, tk), at(lambda i, j, k: (i, k))),
              "nt": pl.BlockSpec((tm, tk), at(lambda i, j, k: (i, k))),
              "tn": pl.BlockSpec((tk, tm), at(lambda i, j, k: (k, i)))}[mode]
    b_spec = {"nn": pl.BlockSpec((tk, tn), at(lambda i, j, k: (k, j))),
              "nt": pl.BlockSpec((tn, tk), at(lambda i, j, k: (j, k))),
              "tn": pl.BlockSpec((tk, tn), at(lambda i, j, k: (k, j)))}[mode]
    o_spec = pl.BlockSpec((tm, tn), at(lambda i, j, k: (i, j)))
    in_specs = [a_spec, b_spec] + ([o_spec] if residual is not None else [])
    args = (a, b) + ((residual,) if residual is not None else ())
    out_shape = jax.ShapeDtypeStruct((M, N), out_dtype)
    extra = {}
    if shard is not None:
        assert residual is None and tn % shard == 0
        per = tn // shard
        first = 0 if into is None else into[1] // per
        o_spec = pl.BlockSpec((per, tm, shard), at(lambda i, j, k: (j + first, i, 0)))
        out_shape = jax.ShapeDtypeStruct((N // shard, M, shard), out_dtype)
        if into is not None:
            out_shape = jax.ShapeDtypeStruct(into[0].shape, out_dtype)
            in_specs, args = in_specs + [pl.BlockSpec(memory_space=pl.ANY)], args + (into[0],)
            extra = dict(input_output_aliases={2: 0})
    grid = (N // tn, M // tm, nk) if n_outer else (M // tm, N // tn, nk)
    return pl.pallas_call(
        body, name=name, grid=grid, in_specs=in_specs, out_specs=o_spec, out_shape=out_shape,
        scratch_shapes=[pltpu.VMEM((tm, tn), f32)] if nk > 1 else [],
        compiler_params=pltpu.CompilerParams(vmem_limit_bytes=VMEM_LIMIT, dimension_semantics=("parallel", "parallel", "arbitrary")),
        **extra)(*args)


def _rms_fwd(x, g, name, transposed=False):
    R, Dd = x.shape
    tr = _pick(R, (512, 256, 128))

    def body(x_ref, g_ref, o_ref, *t_ref):
        xf = x_ref[...]
        y = xf * lax.rsqrt(jnp.mean(xf * xf, axis=-1, keepdims=True) + EPS) * g_ref[...]
        o_ref[...] = y.astype(bf16)
        if transposed:
            t_ref[0][...] = y.T.astype(bf16)

    row = pl.BlockSpec((tr, Dd), lambda i: (i, 0))
    out_specs, out_shape = row, jax.ShapeDtypeStruct((R, Dd), bf16)
    if transposed:
        out_specs, out_shape = (row, pl.BlockSpec((Dd, tr), lambda i: (0, i))), (out_shape, jax.ShapeDtypeStruct((Dd, R), bf16))
    return _call(body, name, (R // tr,), [row, pl.BlockSpec((1, Dd), lambda i: (0, 0))], out_specs, out_shape, sem=("parallel",))(x, g)


def _rms_bwd(x, g, dh, name, residual=None):
    R, Dd = x.shape
    tr = _pick(R, (512, 256, 128))

    def body(*refs):
        if residual is None:
            x_ref, g_ref, dh_ref, dx_ref, dg_ref = refs
        else:
            x_ref, g_ref, dh_ref, r_ref, dx_ref, dg_ref = refs
        xf = x_ref[...]
        rs = lax.rsqrt(jnp.mean(xf * xf, axis=-1, keepdims=True) + EPS)
        y = xf * rs
        dh_ = dh_ref[...].astype(f32)
        dy = dh_ * g_ref[...]
        dx = rs * (dy - y * jnp.mean(dy * y, axis=-1, keepdims=True))
        if residual is not None:
            dx = dx + r_ref[...]
        dx_ref[...] = dx

        @pl.when(pl.program_id(0) == 0)
        def _():
            dg_ref[...] = jnp.zeros_like(dg_ref)

        dg_ref[...] += jnp.sum(dh_ * y, axis=0, keepdims=True)

    row = pl.BlockSpec((tr, Dd), lambda i: (i, 0))
    vec = pl.BlockSpec((1, Dd), lambda i: (0, 0))
    in_specs = [row, vec, row] + ([row] if residual is not None else [])
    args = (x, g, dh) + ((residual,) if residual is not None else ())
    return _call(body, name, (R // tr,), in_specs, (row, vec),
                 (jax.ShapeDtypeStruct((R, Dd), f32), jax.ShapeDtypeStruct((1, Dd), f32)), sem=("arbitrary",))(*args)


def _mm_rms_bwd(pairs, x, g, residual, name, tm):
    M = x.shape[0]
    Dd = x.shape[1]
    tm = min(tm, M)
    n = len(pairs)

    def body(*refs):
        ab_refs, (x_ref, g_ref, r_ref, dx_ref, dg_ref) = refs[:2 * n], refs[2 * n:]
        dh_ = _raw_dot(ab_refs[0][...], ab_refs[1][...], "nt")
        for k in range(1, n):
            dh_ = dh_ + _raw_dot(ab_refs[2 * k][...], ab_refs[2 * k + 1][...], "nt")
        xf = x_ref[...]
        rs = lax.rsqrt(jnp.mean(xf * xf, axis=-1, keepdims=True) + EPS)
        y = xf * rs
        dy = dh_ * g_ref[...]
        dx_ref[...] = rs * (dy - y * jnp.mean(dy * y, axis=-1, keepdims=True)) + r_ref[...]

        @pl.when(pl.program_id(0) == 0)
        def _():
            dg_ref[...] = jnp.zeros_like(dg_ref)

        dg_ref[...] += jnp.sum(dh_ * y, axis=0, keepdims=True)

    row = pl.BlockSpec((tm, Dd), lambda i: (i, 0))
    vec = pl.BlockSpec((1, Dd), lambda i: (0, 0))
    in_specs, args = [], []
    for a, b, k in pairs:
        in_specs += [pl.BlockSpec((tm, a.shape[1]), lambda i: (i, 0)),
                     pl.BlockSpec((b.shape[0], a.shape[1]), functools.partial(lambda i, k_: (0, k_), k_=k))]
        args += [a, b]
    in_specs += [row, vec, row]
    args += [x, g, residual]
    return _call(body, name, (M // tm,), in_specs, (row, vec),
                 (jax.ShapeDtypeStruct((M, Dd), f32), jax.ShapeDtypeStruct((1, Dd), f32)), sem=("arbitrary",))(*args)


def _down_final_loss(act, w_down, x1, g, target):
    R, Dd = x1.shape
    tr = _pick(R, (512, 256, 128))

    def body(a_ref, w_ref, x1_ref, g_ref, t_ref, loss_ref, dx_ref, dxb_ref, dg_ref):
        xf = _raw_dot(a_ref[...], w_ref[...], "nn") + x1_ref[...]
        rs = lax.rsqrt(jnp.mean(xf * xf, axis=-1, keepdims=True) + EPS)
        y = xf * rs
        err = y * g_ref[...] - t_ref[...]
        dh_ = err * (1.0 / Dd)
        dy = dh_ * g_ref[...]
        dx = rs * (dy - y * jnp.mean(dy * y, axis=-1, keepdims=True))
        dx_ref[...] = dx
        dxb_ref[...] = dx.astype(bf16)

        @pl.when(pl.program_id(0) == 0)
        def _():
            dg_ref[...] = jnp.zeros_like(dg_ref)
            loss_ref[...] = jnp.zeros_like(loss_ref)

        dg_ref[...] += jnp.sum(dh_ * y, axis=0, keepdims=True)
        part = jnp.sum(jnp.mean(err * err, axis=-1, keepdims=True), axis=0, keepdims=True)
        loss_ref[...] += 0.5 * part

    row = pl.BlockSpec((tr, Dd), lambda i: (i, 0))
    vec = pl.BlockSpec((1, Dd), lambda i: (0, 0))
    in_specs = [pl.BlockSpec((tr, act.shape[1]), lambda i: (i, 0)), pl.BlockSpec(w_down.shape, lambda i: (0, 0)), row, vec, row]
    return _call(body, "down_final_loss", (R // tr,), in_specs, (pl.BlockSpec((1, 128), lambda i: (0, 0)), row, row, vec),
                 (jax.ShapeDtypeStruct((1, 128), f32), jax.ShapeDtypeStruct((R, Dd), f32), jax.ShapeDtypeStruct((R, Dd), bf16),
                  jax.ShapeDtypeStruct((1, Dd), f32)), sem=("arbitrary",))(act, w_down, x1, g, target)


def _gmlp_parts(zuv, ln_g, ln_b):
    zu, zv = zuv[:, :512], zuv[:, 512:]
    u = jax.nn.gelu(zu)
    v = jax.nn.gelu(zv)
    mu = jnp.mean(v, axis=-1, keepdims=True)
    rs = lax.rsqrt(jnp.mean(jnp.square(v - mu), axis=-1, keepdims=True) + EPS)
    xh = (v - mu) * rs
    return zu, zv, u, xh, rs, xh * ln_g + ln_b


GM_TILE_CHUNKS = 4


def _gmlp_tile(T):
    n = _pick(T // GM_CHUNK, (GM_TILE_CHUNKS, 2, 1))
    return n, n * GM_CHUNK


def _gmlp_fwd(proj, ln_g, ln_b, w_s, b_st):
    T = proj.shape[0]
    nch, rows = _gmlp_tile(T)

    def body(p_ref, g_ref, b_ref, w_ref, bs_ref, o_ref):
        _, _, u, _, _, vn = _gmlp_parts(p_ref[...].astype(f32), g_ref[...], b_ref[...])
        causal = _tri(GM_CHUNK, True) > 0
        for gi in range(N_HEAD):
            sl = slice(gi * HEAD, (gi + 1) * HEAD)
            w = jnp.where(causal, w_ref[gi], 0.0)
            for ch in range(nch):
                rs_ = slice(ch * GM_CHUNK, (ch + 1) * GM_CHUNK)
                mixed = _raw_dot(w, vn[rs_, sl], "nn") + bs_ref[:, gi:gi + 1]
                o_ref[rs_, sl] = (u[rs_, sl] * mixed).astype(bf16)

    vec = pl.BlockSpec((1, 512), lambda i: (0, 0))
    return _call(body, "gmlp_fwd", (T // rows,),
                 [pl.BlockSpec((rows, 1024), lambda i: (i, 0)), vec, vec,
                  pl.BlockSpec((N_HEAD, GM_CHUNK, GM_CHUNK), lambda i: (0, 0, 0)), pl.BlockSpec((GM_CHUNK, 128), lambda i: (0, 0))],
                 pl.BlockSpec((rows, 512), lambda i: (i, 0)), jax.ShapeDtypeStruct((T, 512), bf16), sem=("parallel",))(
        proj, ln_g, ln_b, w_s, b_st)


def _gmlp_bwd(proj, ln_g, ln_b, w_s, b_st, da):
    T = proj.shape[0]
    nch, rows = _gmlp_tile(T)

    def body(p_ref, g_ref, b_ref, w_ref, bs_ref, da_ref, dp_ref, dg_ref, db_ref, dw_ref, dbs_ref):
        zu, zv, u, xh, rs, vn = _gmlp_parts(p_ref[...].astype(f32), g_ref[...], b_ref[...])
        causal = _tri(GM_CHUNK, True) > 0
        sub = lax.broadcasted_iota(jnp.int32, (8, GM_CHUNK), 0)
        ones = jnp.ones((8, HEAD), f32)
        dout = da_ref[...].astype(f32)

        @pl.when(pl.program_id(0) == 0)
        def _():
            for r in (dg_ref, db_ref, dw_ref, dbs_ref):
                r[...] = jnp.zeros_like(r)

        du, dvn, dbs = [], [], jnp.zeros((8, GM_CHUNK), f32)
        for gi in range(N_HEAD):
            sl = slice(gi * HEAD, (gi + 1) * HEAD)
            w = jnp.where(causal, w_ref[gi], 0.0)
            du_g, dvn_g, dw_g = [], [], jnp.zeros((GM_CHUNK, GM_CHUNK), f32)
            for ch in range(nch):
                rs_ = slice(ch * GM_CHUNK, (ch + 1) * GM_CHUNK)
                mixed = _raw_dot(w, vn[rs_, sl], "nn") + bs_ref[:, gi:gi + 1]
                du_g.append(dout[rs_, sl] * mixed)
                dm = dout[rs_, sl] * u[rs_, sl]
                dbs = dbs + jnp.where(sub == gi, _sel_dot(ones, dm, "nt"), 0.0)
                dw_g = dw_g + _raw_dot(dm, vn[rs_, sl], "nt")
                dvn_g.append(_raw_dot(w, dm, "tn"))
            dw_ref[gi] += jnp.where(causal, dw_g, 0.0)
            du.append(jnp.concatenate(du_g, axis=0))
            dvn.append(jnp.concatenate(dvn_g, axis=0))
        dbs_ref[...] += dbs
        du = jnp.concatenate(du, axis=-1)
        dvn = jnp.concatenate(dvn, axis=-1)
        dg_ref[...] += jnp.sum(dvn * xh, axis=0, keepdims=True)
        db_ref[...] += jnp.sum(dvn, axis=0, keepdims=True)
        dxh = dvn * g_ref[...]
        dv = rs * (dxh - jnp.mean(dxh, axis=-1, keepdims=True) - xh * jnp.mean(dxh * xh, axis=-1, keepdims=True))
        dp_ref[:, :512] = _egrad(jax.nn.gelu, zu, du).astype(bf16)
        dp_ref[:, 512:] = _egrad(jax.nn.gelu, zv, dv).astype(bf16)

    vec = pl.BlockSpec((1, 512), lambda i: (0, 0))
    wsp = pl.BlockSpec((N_HEAD, GM_CHUNK, GM_CHUNK), lambda i: (0, 0, 0))
    return _call(body, "gmlp_bwd", (T // rows,),
                 [pl.BlockSpec((rows, 1024), lambda i: (i, 0)), vec, vec, wsp, pl.BlockSpec((GM_CHUNK, 128), lambda i: (0, 0)),
                  pl.BlockSpec((rows, 512), lambda i: (i, 0))],
                 (pl.BlockSpec((rows, 1024), lambda i: (i, 0)), vec, vec, wsp, pl.BlockSpec((8, GM_CHUNK), lambda i: (0, 0))),
                 (jax.ShapeDtypeStruct((T, 1024), bf16), jax.ShapeDtypeStruct((1, 512), f32), jax.ShapeDtypeStruct((1, 512), f32),
                  jax.ShapeDtypeStruct((N_HEAD, GM_CHUNK, GM_CHUNK), f32), jax.ShapeDtypeStruct((8, GM_CHUNK), f32)),
                 sem=("arbitrary",))(proj, ln_g, ln_b, w_s, b_st, da)


HG_SUB = 8
HG_NSUB = HG_CHUNK // HG_SUB


def _two_level_matrix(transposed=False):
    shape = (HG_CHUNK, 2 * HG_CHUNK) if transposed else (2 * HG_CHUNK, HG_CHUNK)
    r = lax.broadcasted_iota(jnp.int32, shape, 1 if transposed else 0)
    c = lax.broadcasted_iota(jnp.int32, shape, 0 if transposed else 1)
    t = jnp.where(r < HG_CHUNK, r, r - HG_CHUNK)
    local = (r < HG_CHUNK) & (t // HG_SUB == c // HG_SUB) & (c <= t)
    before = (r >= HG_CHUNK) & (c < (t // HG_SUB) * HG_SUB)
    return (local | before).astype(f32)


def _two_level_sums(x):
    two = _sel_dot(_two_level_matrix(), x, "nn")
    return two[:HG_CHUNK], two[HG_CHUNK:]


@jax.custom_vjp
def _two_level_cumsum(x):
    return _two_level_sums(x)


_two_level_cumsum.defvjp(
    lambda x: (_two_level_sums(x), None),
    lambda _, g: (_sel_dot(_two_level_matrix(), jnp.concatenate(g, axis=0), "tn"),))


def _tile_matrix():
    s = lax.broadcasted_iota(jnp.int32, (HG_SUB, HG_CHUNK), 0)
    j = lax.broadcasted_iota(jnp.int32, (HG_SUB, HG_CHUNK), 1)
    return (j % HG_SUB == s).astype(f32)


@jax.custom_vjp
def _tile_lanes(x):
    return _sel_dot(_tile_matrix(), x, "nn", x_first=True, pieces=1)


_tile_lanes.defvjp(
    lambda x: (_sel_dot(_tile_matrix(), x, "nn", x_first=True, pieces=1), None),
    lambda _, g: (_sel_dot(_tile_matrix(), g, "nt", x_first=True, pieces=2),))


def _block_rows(x):
    k = x.shape[-1]
    return jnp.broadcast_to(x.reshape(HG_NSUB, 1, HG_SUB, k), (HG_NSUB, HG_SUB, HG_SUB, k)).reshape(HG_CHUNK, HG_SUB, k)


def _hgrn_chunk(st0, q_raw, f_raw, i_raw, g_raw, l0, l1, ng):
    C, SUB = HG_CHUNK, HG_SUB
    lb = jax.nn.sigmoid(l0 - l1)
    fg = lb + (1.0 - lb) * jax.nn.sigmoid(f_raw)
    kk = 1.0 - fg
    qf = jax.nn.silu(q_raw)
    al, base = _two_level_cumsum(jnp.log(fg))
    a = al + base
    row = lax.broadcasted_iota(jnp.int32, (C, HEAD), 0)
    a_last = jnp.sum(jnp.where(row == C - 1, a, 0.0), axis=0, keepdims=True)
    inter = _dot_nt(qf * jnp.exp(a), st0)
    qt = qf * jnp.exp(al)
    rb = lax.broadcasted_iota(jnp.int32, (C, C), 0) // SUB
    cb = lax.broadcasted_iota(jnp.int32, (C, C), 1) // SUB
    scores = jnp.zeros((C, C), f32)
    for i in range(1, HG_NSUB):
        base_i = jnp.sum(jnp.where(row == i * SUB, base, 0.0), axis=0, keepdims=True)
        kt = kk * jnp.exp(jnp.minimum(base_i - a, 0.0))
        scores = scores + jnp.where((rb == i) & (cb < i), _dot_nt(qt, kt), 0.0)
    t_i = lax.broadcasted_iota(jnp.int32, (C, SUB, HEAD), 0) % SUB
    s_i = lax.broadcasted_iota(jnp.int32, (C, SUB, HEAD), 1)
    decay = jnp.exp(jnp.where(s_i <= t_i, al[:, None, :] - _block_rows(al), -jnp.inf))
    diag = jnp.sum(qf[:, None, :] * decay * _block_rows(kk), axis=-1)
    scores = scores + jnp.where(rb == cb, _tile_lanes(diag), 0.0)
    o = inter + _dot_nn(scores, i_raw)
    st1 = jnp.exp(a_last) * st0 + _dot_tn(i_raw, kk * jnp.exp(a_last - a))
    on = o * lax.rsqrt(jnp.mean(o * o, axis=-1, keepdims=True) + EPS) * ng
    return st1, on * jax.nn.silu(g_raw)


def _hgrn_specs(S, Bl, rev):
    N = S // HG_CHUNK
    chunk = (lambda n: N - 1 - n) if rev else (lambda n: n)
    col = lambda c0: pl.BlockSpec((Bl, HG_CHUNK, 512), lambda n: (0, chunk(n), c0 // 512))
    st = pl.BlockSpec((Bl, N_HEAD, 1, HEAD, HEAD), lambda n: (0, 0, chunk(n), 0, 0))
    full = lambda *s: pl.BlockSpec(s, functools.partial(lambda n, nd: (0,) * nd, nd=len(s)))
    return N, col, st, full


def _hgrn_fwd(proj, lb_logits, ng, Bl, S):
    N, col, st, full = _hgrn_specs(S, Bl, False)

    def body(q_ref, f_ref, i_ref, g_ref, l_ref, ng_ref, o_ref, st_ref, state):
        @pl.when(pl.program_id(0) == 0)
        def _():
            state[...] = jnp.zeros_like(state)

        for b in range(Bl):
            for h in range(N_HEAD):
                sl = slice(h * HEAD, (h + 1) * HEAD)
                st0 = state[b, h]
                st_ref[b, h, 0] = st0
                st1, out = _hgrn_chunk(st0, *[r[b, :, sl].astype(f32) for r in (q_ref, f_ref, i_ref, g_ref)],
                                       l_ref[0:1, sl], l_ref[1:2, sl], ng_ref[...])
                state[b, h] = st1
                o_ref[b, :, sl] = out.astype(bf16)

    return _call(body, "hgrn_fwd", (N,), [col(C_HQ), col(C_HF), col(C_HI), col(C_HG), full(2, 512), full(1, HEAD)],
                 (col(0), st),
                 (jax.ShapeDtypeStruct((Bl, S, 512), bf16), jax.ShapeDtypeStruct((Bl, N_HEAD, N, HEAD, HEAD), f32)),
                 scratch=[pltpu.VMEM((Bl, N_HEAD, HEAD, HEAD), f32)], sem=("arbitrary",))(
        proj, proj, proj, proj, lb_logits, ng)


def _hgrn_bwd(proj, lb_logits, ng, states, db, Bl, S):
    N, col, st, full = _hgrn_specs(S, Bl, True)

    def body(q_ref, f_ref, i_ref, g_ref, l_ref, ng_ref, st_ref, db_ref,
             dq_ref, df_ref, di_ref, dg_ref, dl_ref, dng_ref, dstate):
        @pl.when(pl.program_id(0) == 0)
        def _():
            dstate[...] = jnp.zeros_like(dstate)
            dl_ref[...] = jnp.zeros_like(dl_ref)
            dng_ref[...] = jnp.zeros_like(dng_ref)

        for b in range(Bl):
            for h in range(N_HEAD):
                sl = slice(h * HEAD, (h + 1) * HEAD)
                _, vjp = jax.vjp(_hgrn_chunk, st_ref[b, h, 0], *[r[b, :, sl].astype(f32) for r in (q_ref, f_ref, i_ref, g_ref)],
                                 l_ref[0:1, sl], l_ref[1:2, sl], ng_ref[...])
                dst0, dq, df, di, dg, dl0, dl1, dng = vjp((dstate[b, h], db_ref[b, :, sl].astype(f32)))
                dstate[b, h] = dst0
                dq_ref[b, :, sl] = dq.astype(bf16)
                df_ref[b, :, sl] = df.astype(bf16)
                di_ref[b, :, sl] = di.astype(bf16)
                dg_ref[b, :, sl] = dg.astype(bf16)
                dl_ref[0:1, sl] += dl0
                dl_ref[1:2, sl] += dl1
                dng_ref[b, h] += dng

    return _call(body, "hgrn_bwd", (N,),
                 [col(C_HQ), col(C_HF), col(C_HI), col(C_HG), full(2, 512), full(1, HEAD), st, col(0)],
                 (*[col(0)] * 4, full(2, 512), full(Bl, N_HEAD, 1, HEAD)),
                 (*[jax.ShapeDtypeStruct((Bl, S, 512), bf16)] * 4, jax.ShapeDtypeStruct((2, 512), f32),
                  jax.ShapeDtypeStruct((Bl, N_HEAD, 1, HEAD), f32)),
                 scratch=[pltpu.VMEM((Bl, N_HEAD, HEAD, HEAD), f32)], sem=("arbitrary",))(
        proj, proj, proj, proj, lb_logits, ng, states, db)


def _attn_probs(q, k):
    s = _raw_dot(q, k, "nt") * (HEAD ** -0.5)
    e = jnp.exp(s - jnp.max(s, axis=-1, keepdims=True))
    return e / jnp.sum(e, axis=-1, keepdims=True)


def _attn_specs(S, tq):
    nq = S // tq
    q = pl.BlockSpec((tq, 512), lambda b, i: (b * nq + i, C_XQ // 512))
    kv = pl.BlockSpec((1, MEM_LEN, 1024), lambda b, i: (b, 0, 0))
    o = pl.BlockSpec((tq, 512), lambda b, i: (b * nq + i, 0))
    return nq, q, kv, o


def _attn_fwd(proj, kv, Bl, S):
    tq = _pick(S, (512, 256, 128))
    nq, qs, kvs, os_ = _attn_specs(S, tq)

    def body(q_ref, kv_ref, o_ref):
        for h in range(N_HEAD):
            sl = slice(h * HEAD, (h + 1) * HEAD)
            p = _attn_probs(q_ref[:, sl], kv_ref[0, :, sl])
            o_ref[:, sl] = _raw_dot(p, kv_ref[0, :, 512 + h * HEAD:512 + (h + 1) * HEAD], "nn").astype(bf16)

    return _call(body, "attn_fwd", (Bl, nq), [qs, kvs], os_, jax.ShapeDtypeStruct((Bl * S, 512), bf16),
                 sem=("parallel", "parallel"))(proj, kv)


def _attn_bwd(proj, kv, dc, Bl, S):
    tq = _pick(S, (512, 256, 128))
    nq, qs, kvs, os_ = _attn_specs(S, tq)

    def body(q_ref, kv_ref, do_ref, dq_ref, dkv_ref):
        @pl.when(pl.program_id(1) == 0)
        def _():
            dkv_ref[...] = jnp.zeros_like(dkv_ref)

        for h in range(N_HEAD):
            sl = slice(h * HEAD, (h + 1) * HEAD)
            vsl = slice(512 + h * HEAD, 512 + (h + 1) * HEAD)
            q, k, v, do = q_ref[:, sl], kv_ref[0, :, sl], kv_ref[0, :, vsl], do_ref[:, sl]
            p = _attn_probs(q, k)
            dkv_ref[0, :, vsl] += _raw_dot(p, do, "tn")
            dp = _raw_dot(do, v, "nt")
            ds = p * (dp - jnp.sum(dp * p, axis=-1, keepdims=True)) * (HEAD ** -0.5)
            dq_ref[:, sl] = _raw_dot(ds, k, "nn").astype(bf16)
            dkv_ref[0, :, sl] += _raw_dot(ds, q, "tn")

    return _call(body, "attn_bwd", (Bl, nq), [qs, kvs, os_], (os_, kvs),
                 (jax.ShapeDtypeStruct((Bl * S, 512), bf16), jax.ShapeDtypeStruct((Bl, MEM_LEN, 1024), f32)),
                 sem=("arbitrary", "arbitrary"))(proj, kv, dc)


def _gate_specs(tm):
    half = D_MODEL // 2
    return [pl.BlockSpec((tm, half), functools.partial(lambda i, c: (i, c), c=(C_GL + n * D_MODEL) // half + k))
            for n in range(3) for k in range(2)]


def _merge_out_norm_fwd(branches, wb, proj, w_out, x, g):
    T = proj.shape[0]
    tm = _pick(T, (512, 256, 128))

    def body(a_ref, b_ref, c_ref, w0, w1, w2, g0a, g0b, g1a, g1b, g2a, g2b, wo_ref, x_ref, g_ref, m_ref, x1_ref, h_ref, ht_ref):
        acc = jnp.zeros((tm, D_MODEL), f32)
        for x_n, w_ref, ga, gb in ((a_ref, w0, g0a, g0b), (b_ref, w1, g1a, g1b), (c_ref, w2, g2a, g2b)):
            gate = jax.nn.sigmoid(jnp.concatenate([ga[...], gb[...]], axis=-1).astype(f32))
            acc = acc + gate * _raw_dot(x_n[...], w_ref[...], "nn")
        merged = acc.astype(bf16)
        m_ref[...] = merged
        x1 = x_ref[...] + _raw_dot(merged, wo_ref[...], "nn")
        x1_ref[...] = x1
        y = x1 * lax.rsqrt(jnp.mean(x1 * x1, axis=-1, keepdims=True) + EPS) * g_ref[...]
        h_ref[...] = y.astype(bf16)
        ht_ref[...] = y.T.astype(bf16)

    br = pl.BlockSpec((tm, 512), lambda i: (i, 0))
    w = pl.BlockSpec((512, D_MODEL), lambda i: (0, 0))
    row = pl.BlockSpec((tm, D_MODEL), lambda i: (i, 0))
    return _call(body, "merge_out_norm_fwd", (T // tm,),
                 [br, br, br, w, w, w, *_gate_specs(tm), pl.BlockSpec((D_MODEL, D_MODEL), lambda i: (0, 0)), row,
                  pl.BlockSpec((1, D_MODEL), lambda i: (0, 0))],
                 (row, row, row, pl.BlockSpec((D_MODEL, tm), lambda i: (0, i))),
                 (jax.ShapeDtypeStruct((T, D_MODEL), bf16), jax.ShapeDtypeStruct((T, D_MODEL), f32),
                  jax.ShapeDtypeStruct((T, D_MODEL), bf16), jax.ShapeDtypeStruct((D_MODEL, T), bf16)),
                 sem=("parallel",))(*branches, *wb, *[proj] * 6, w_out, x, g)


def _merge_bwd(branches, wb, proj, dx1, w_out):
    T = proj.shape[0]
    tm = _pick(T, (256, 128))

    def body(a_ref, b_ref, c_ref, w0, w1, w2, g0a, g0b, g1a, g1b, g2a, g2b, dx_ref, wo_ref, dgl_ref, d0, d1, d2, gw_ref):
        @pl.when(pl.program_id(0) == 0)
        def _():
            gw_ref[...] = jnp.zeros_like(gw_ref)

        dm = _raw_dot(dx_ref[...], wo_ref[...], "nt")
        for n, (x_ref, w_ref, ga, gb, d_ref) in enumerate(((a_ref, w0, g0a, g0b, d0), (b_ref, w1, g1a, g1b, d1), (c_ref, w2, g2a, g2b, d2))):
            x, w = x_ref[...], w_ref[...]
            up = _raw_dot(x, w, "nn")
            sg = jax.nn.sigmoid(jnp.concatenate([ga[...], gb[...]], axis=-1).astype(f32))
            dgl_ref[n] = (dm * up * sg * (1.0 - sg)).astype(bf16)
            dup = (dm * sg).astype(bf16)
            d_ref[...] = _raw_dot(dup, w, "nt").astype(bf16)
            gw_ref[n] += _raw_dot(x, dup, "tn")

    br = pl.BlockSpec((tm, 512), lambda i: (i, 0))
    w = pl.BlockSpec((512, D_MODEL), lambda i: (0, 0))
    sh = jax.ShapeDtypeStruct((T, 512), bf16)
    outs = _call(body, "merge_bwd", (T // tm,),
                 [br, br, br, w, w, w, *_gate_specs(tm), pl.BlockSpec((tm, D_MODEL), lambda i: (i, 0)),
                  pl.BlockSpec((D_MODEL, D_MODEL), lambda i: (0, 0))],
                 (pl.BlockSpec((3, tm, D_MODEL), lambda i: (0, i, 0)), br, br, br, pl.BlockSpec((3, 512, D_MODEL), lambda i: (0, 0, 0))),
                 (jax.ShapeDtypeStruct((3, T, D_MODEL), bf16), sh, sh, sh, jax.ShapeDtypeStruct((3, 512, D_MODEL), f32)),
                 sem=("arbitrary",))(*branches, *wb, *[proj] * 6, dx1, w_out)
    return outs[0], outs[1:4], outs[4]


CONV_TC = 256


def _shift_down(a, k):
    r = pltpu.roll(a, k, 0)
    row = lax.broadcasted_iota(jnp.int32, (8, a.shape[1]), 0)
    return jnp.concatenate([jnp.where(row >= k, r[:8], 0.0), r[8:]], axis=0)


def _shift_up(a, k):
    n = a.shape[0]
    r = pltpu.roll(a, n - k, 0)
    row = lax.broadcasted_iota(jnp.int32, (8, a.shape[1]), 0)
    return jnp.concatenate([r[:n - 8], jnp.where(row < 8 - k, r[n - 8:], 0.0)], axis=0)


def _conv_pre(a, a1, a2, cw, cb):
    return cb + cw[0:1] * a2 + cw[1:2] * a1 + cw[2:3] * a


def _up_conv_fwd(h2, w_up, cw, cb):
    Bl, S, Dd = h2.shape
    nc = D_FF // CONV_TC

    def body(h_ref, wa_ref, wb_ref, cw_ref, cb_ref, a_ref, b_ref, o_ref):
        a16 = _raw_dot(h_ref[0], wa_ref[...], "nn").astype(bf16)
        b16 = _raw_dot(h_ref[0], wb_ref[...], "nn").astype(bf16)
        a_ref[0], b_ref[0] = a16, b16
        a = a16.astype(f32)
        ac = _conv_pre(a, _shift_down(a, 1), _shift_down(a, 2), cw_ref[...], cb_ref[...])
        o_ref[0] = (jax.nn.silu(ac) * b16.astype(f32)).astype(bf16)

    seq = pl.BlockSpec((1, S, CONV_TC), lambda b, c: (b, 0, c))
    sh = jax.ShapeDtypeStruct((Bl, S, D_FF), bf16)
    return _call(body, "up_conv_fwd", (Bl, nc),
                 [pl.BlockSpec((1, S, Dd), lambda b, c: (b, 0, 0)), pl.BlockSpec((Dd, CONV_TC), lambda b, c: (0, c)),
                  pl.BlockSpec((Dd, CONV_TC), lambda b, c: (0, nc + c)), pl.BlockSpec((3, CONV_TC), lambda b, c: (0, c)),
                  pl.BlockSpec((1, CONV_TC), lambda b, c: (0, c))],
                 (seq, seq, seq), (sh, sh, sh), sem=("parallel", "parallel"))(h2, w_up, w_up, cw, cb)


def _down_conv_bwd(dx2, w_down, a, b, cw, cb):
    Bl, S, Dd = dx2.shape
    nc = D_FF // CONV_TC

    def body(dx_ref, wd_ref, a_ref, b_ref, cw_ref, cb_ref, da_ref, db_ref, dcw_ref, dcb_ref):
        dact = _raw_dot(dx_ref[0], wd_ref[...], "nt").astype(bf16).astype(f32)
        a, cw = a_ref[0].astype(f32), cw_ref[...]
        a1, a2 = _shift_down(a, 1), _shift_down(a, 2)
        ac = _conv_pre(a, a1, a2, cw, cb_ref[...])
        sg = jax.nn.sigmoid(ac)
        gated = dact * sg
        db_ref[0] = (gated * ac).astype(bf16)
        dac = gated * b_ref[0].astype(f32) * (1.0 + ac * (1.0 - sg))
        da_ref[0] = (cw[2:3] * dac + cw[1:2] * _shift_up(dac, 1) + cw[0:1] * _shift_up(dac, 2)).astype(bf16)
        dcw_ref[0, 0:1, :] = jnp.sum(dac * a2, axis=0, keepdims=True)
        dcw_ref[0, 1:2, :] = jnp.sum(dac * a1, axis=0, keepdims=True)
        dcw_ref[0, 2:3, :] = jnp.sum(dac * a, axis=0, keepdims=True)
        dcb_ref[0] = jnp.sum(dac, axis=0, keepdims=True)

    seq = pl.BlockSpec((1, S, CONV_TC), lambda b_, c: (b_, 0, c))
    sh = jax.ShapeDtypeStruct((Bl, S, D_FF), bf16)
    return _call(body, "down_conv_bwd", (Bl, nc),
                 [pl.BlockSpec((1, S, Dd), lambda b_, c: (b_, 0, 0)), pl.BlockSpec((CONV_TC, Dd), lambda b_, c: (c, 0)), seq, seq,
                  pl.BlockSpec((3, CONV_TC), lambda b_, c: (0, c)), pl.BlockSpec((1, CONV_TC), lambda b_, c: (0, c))],
                 (seq, seq, pl.BlockSpec((1, 3, CONV_TC), lambda b_, c: (b_, 0, c)), pl.BlockSpec((1, 1, CONV_TC), lambda b_, c: (b_, 0, c))),
                 (sh, sh, jax.ShapeDtypeStruct((Bl, 3, D_FF), f32), jax.ShapeDtypeStruct((Bl, 1, D_FF), f32)),
                 sem=("parallel", "parallel"))(dx2, w_down, a, b, cw, cb)


def _local_step(x, mem, target, p, w_in, late_b, late_c, send, settle):
    Bl, S, Dd = x.shape
    T = Bl * S
    x2d, t2d, mem2d = x.reshape(T, Dd), target.reshape(T, Dd), mem.reshape(Bl * MEM_LEN, Dd)
    b_st = jnp.pad(p["b_spatial"].T, ((0, 0), (0, 128 - N_HEAD)))
    lbl = p["lb_logits"]

    h, h_t = _rms_fwd(x2d, p["norm1_g"], "norm1_fwd", transposed=True)
    proj = _mm(h, w_in, "nn", bf16, "proj_fwd", 1024, 1664)
    a_out = _gmlp_fwd(proj, p["ln_v_g"], p["ln_v_b"], p["w_spatial"], b_st)
    proj3 = proj.reshape(Bl, S, IN_WIDTH)
    b_out, states = _hgrn_fwd(proj3, lbl, p["hgrn_norm_g"], Bl, S)
    b_out = b_out.reshape(T, 512)
    memn = _rms_fwd(mem2d, p["mem_norm_g"], "memnorm_fwd")
    w = late_b(b_out)
    wb = w["w_branch"]
    kv = _mm(memn, w["w_mem_kv"], "nn", f32, "kv_fwd", 512, 1024).reshape(Bl, MEM_LEN, 2 * 512)
    c_out = _attn_fwd(proj, kv, Bl, S)
    branches = (a_out, b_out, c_out)
    merged, x1, h2, h2_t = _merge_out_norm_fwd(branches, wb, proj, w["w_out"], x2d, p["norm2_g"])
    w.update(late_c(h2))
    ffn_a, ffn_b, act = _up_conv_fwd(h2.reshape(Bl, S, Dd), w["w_up"], w["conv_w"], p["conv_b"])
    act = act.reshape(T, D_FF)
    loss_part, dx2, dx2_16, g_final = _down_final_loss(act, w["w_down"], x1, p["final_g"], t2d)

    g_w_down = _mm(act, dx2_16, "tn", bf16, "down_dw", 1408, 1024, 1024)
    da, db, g_conv_w, g_conv_b = _down_conv_bwd(dx2_16.reshape(Bl, S, Dd), w["w_down"], ffn_a, ffn_b, w["conv_w"], p["conv_b"])
    da, db = da.reshape(T, D_FF), db.reshape(T, D_FF)
    shard = 2 * D_FF // N_DEV
    g_w_up = _mm(h2_t, da, "nn", bf16, "up_dw_a", 512, 1408, shard=shard, n_outer=True,
                 into=(lax.empty((N_DEV, D_MODEL, shard), bf16), 0))
    g_w_up = _mm(h2_t, db, "nn", bf16, "up_dw_b", 512, 1408, shard=shard, n_outer=True, into=(g_w_up, N_DEV // 2))
    tok = send("c", dict(w_up=g_w_up, conv_w=jnp.sum(g_conv_w, axis=0), w_down=g_w_down))
    dx1, g_norm2 = _mm_rms_bwd([(da, w["w_up"], 0), (db, w["w_up"], 1)], x1, p["norm2_g"] + tok[0, 0], dx2, "up_dx_norm2_bwd", 256)

    g_w_out = _mm(merged, dx1, "tn", bf16, "out_dw", 1024, 1024, 1024)
    dgl, dbr, g_w_branch = _merge_bwd(branches, wb, proj, dx1, w["w_out"])
    dxq, dkv = _attn_bwd(proj, kv, dbr[2], Bl, S)
    dkv = dkv.reshape(Bl * MEM_LEN, 2 * 512)
    g_w_kv = _mm(memn, dkv, "tn", bf16, "kv_dw", 1024, 1024, 512)
    tok = send("b", dict(w_mem_kv=g_w_kv, w_branch=g_w_branch, w_out=g_w_out))
    dmemn = _mm(dkv, w["w_mem_kv"], "nt", f32, "kv_dx", 512, 1024)
    _, g_mem_norm = _rms_bwd(mem2d, p["mem_norm_g"], dmemn, "memnorm_bwd")
    dzuv, g_ln_g, g_ln_b, g_w_sp, g_b_sp = _gmlp_bwd(proj, p["ln_v_g"] + tok[0, 0], p["ln_v_b"], p["w_spatial"], b_st, dbr[0])
    *dqfig, g_lbl, g_ng = _hgrn_bwd(proj3, lbl, p["hgrn_norm_g"], states, dbr[1].reshape(Bl, S, 512), Bl, S)
    dq, df, di, dg = [d.reshape(T, 512) for d in dqfig]
    dproj = jnp.concatenate([dzuv, dq, df, di, dg, dxq, dgl[0], dgl[1], dgl[2]], axis=-1)
    g_w_in = _mm(h_t, dproj, "nn", bf16, "proj_dw", 512, 1664, shard=IN_WIDTH // N_DEV, n_outer=True)
    tok = send("a", dict(w_in=g_w_in))
    dx, g_norm1 = _mm_rms_bwd([(settle(dproj), w_in, 0)], x2d, p["norm1_g"] + tok[0, 0], dx1, "proj_dx_norm1_bwd", 256)

    gs = dict(w_spatial=g_w_sp, norm1_g=g_norm1, mem_norm_g=g_mem_norm, norm2_g=g_norm2, final_g=g_final, lb_logits=g_lbl,
              ln_v_g=g_ln_g, ln_v_b=g_ln_b, b_spatial=g_b_sp, hgrn_norm_g=g_ng, conv_b=g_conv_b)
    return loss_part, dx.reshape(Bl, S, Dd), gs


def _coords():
    return lax.axis_index("x"), lax.axis_index("y"), lax.axis_index("c")


def _slot(dev):
    return 4 * dev[0] + 2 * dev[1] + dev[2]


def _comm_call(body, name, arrays, out_shapes, n_sem):
    n = len(arrays)
    hbm = pl.BlockSpec(memory_space=pl.ANY)
    return pl.pallas_call(
        body, name=name, out_shape=out_shapes, in_specs=[hbm] * n, out_specs=[hbm] * n,
        scratch_shapes=[pltpu.SemaphoreType.DMA((n_sem, n)), pltpu.SemaphoreType.DMA((n_sem, n)), pltpu.SemaphoreType.DMA((n,))])(*arrays)


def _all_gather(blocks, name):
    n = len(blocks)

    def body(*refs):
        x_refs, o_refs, (send_sems, recv_sems, local_sems) = refs[:n], refs[n:2 * n], refs[2 * n:]
        x, y, c = _coords()
        me, sibling = (x, y, c), (x, y, 1 - c)
        chips = [(1 - x, y), (x, 1 - y), (1 - x, 1 - y)]

        def copy(a, k, block_dev, to, from_input=False):
            dst = o_refs[a].at[_slot(block_dev)]
            return pltpu.make_async_remote_copy(src_ref=x_refs[a] if from_input else dst, dst_ref=dst, send_sem=send_sems.at[k, a],
                                                recv_sem=recv_sems.at[k, a], device_id=to, device_id_type=MESH)

        mine = [pltpu.make_async_copy(x_refs[a], o_refs[a].at[_slot(me)], local_sems.at[a]) for a in range(n)]
        first = [copy(a, 0, me, sibling, True) for a in range(n)]
        first += [copy(a, 1 + j, me, (*chip, c), True) for j, chip in enumerate(chips) for a in range(n)]
        for cp in mine + first:
            cp.start()
        passed = []
        for j, chip in enumerate(chips):
            for a in range(n):
                copy(a, 1 + j, (*chip, c), me).wait_recv()
                fwd = copy(a, 4 + j, (*chip, c), sibling)
                fwd.start()
                passed.append(fwd)
        for a in range(n):
            copy(a, 0, sibling, me).wait_recv()
        for j, chip in enumerate(chips):
            for a in range(n):
                copy(a, 4 + j, (*chip, 1 - c), me).wait_recv()
        for cp in first + passed:
            cp.wait_send()
        for cp in mine:
            cp.wait()

    return _comm_call(body, name, blocks, [jax.ShapeDtypeStruct((N_DEV,) + b.shape, b.dtype) for b in blocks], 7)


def _all_to_all(parts, name):
    n = len(parts)
    rel = [(0, 0, 1), (0, 1, 0), (0, 1, 1), (1, 0, 0), (1, 0, 1), (1, 1, 0), (1, 1, 1)]

    def body(*refs):
        x_refs, o_refs, (send_sems, recv_sems, local_sems) = refs[:n], refs[n:2 * n], refs[2 * n:]
        x, y, c = _coords()
        me = (x, y, c)
        peers = [(x ^ dx, y ^ dy, c ^ dc) for dx, dy, dc in rel]

        def copy(a, k, peer):
            return pltpu.make_async_remote_copy(src_ref=x_refs[a].at[_slot(peer)], dst_ref=o_refs[a].at[_slot(me)], send_sem=send_sems.at[k, a],
                                                recv_sem=recv_sems.at[k, a], device_id=peer, device_id_type=MESH)

        def arrival(a, k, peer):
            return pltpu.make_async_remote_copy(src_ref=x_refs[a].at[_slot(me)], dst_ref=o_refs[a].at[_slot(peer)], send_sem=send_sems.at[k, a],
                                                recv_sem=recv_sems.at[k, a], device_id=peer, device_id_type=MESH)

        mine = [pltpu.make_async_copy(x_refs[a].at[_slot(me)], o_refs[a].at[_slot(me)], local_sems.at[a]) for a in range(n)]
        sends = [copy(a, k, peer) for k, peer in enumerate(peers) for a in range(n)]
        for cp in mine + sends:
            cp.start()
        for k, peer in enumerate(peers):
            for a in range(n):
                arrival(a, k, peer).wait_recv()
        for cp in sends:
            cp.wait_send()
        for cp in mine:
            cp.wait()

    return _comm_call(body, name, parts, [jax.ShapeDtypeStruct(p.shape, p.dtype) for p in parts], 7)


_HBM = pl.BlockSpec(memory_space=pltpu.HBM)
_SEM = pl.BlockSpec(memory_space=pltpu.SEMAPHORE)
_REL = [(0, 0, 1), (0, 1, 0), (0, 1, 1), (1, 0, 0), (1, 0, 1), (1, 1, 0), (1, 1, 1)]


_LINK_ORDER = (3, 1, 5, 4, 2, 6, 0)
SEND_PIECES = 4


def _pieces(shape, dtype):
    rows = shape[0]
    unit = 1 if len(shape) > 2 else (16 if dtype == bf16 else 8)
    for n in (SEND_PIECES, 2):
        if rows % (n * unit) == 0:
            return [pl.ds(i * (rows // n), rows // n) for i in range(n)]
    return [pl.ds(0, rows)]


def _split_copies(gather, src, land, send, recv, pieces):
    x, y, c = _coords()
    me = (x, y, c)
    copies = []
    for a in range(len(src)):
        block = src[a].shape if gather else src[a].shape[1:]
        for rows in (_pieces(block, src[a].dtype) if pieces else [None]):
            for k in _LINK_ORDER:
                dx, dy, dc = _REL[k]
                peer = (x ^ dx, y ^ dy, c ^ dc)
                mine, there = (src[a] if gather else src[a].at[_slot(peer)]), land[a].at[_slot(me)]
                if rows is not None:
                    mine, there = mine.at[rows], there.at[rows]
                copies.append(pltpu.make_async_remote_copy(src_ref=mine, dst_ref=there, send_sem=send[a].at[k], recv_sem=recv[a].at[k],
                                                           device_id=peer, device_id_type=MESH))
    return me, copies


def _arrivals(gather, src, land, send, recv):
    x, y, c = _coords()
    out = []
    for a in range(len(src)):
        for k, (dx, dy, dc) in enumerate(_REL):
            peer = (x ^ dx, y ^ dy, c ^ dc)
            out.append(pltpu.make_async_remote_copy(src_ref=src[a] if gather else src[a].at[_slot(peer)], dst_ref=land[a].at[_slot(peer)],
                                                    send_sem=send[a].at[k], recv_sem=recv[a].at[k], device_id=peer, device_id_type=MESH))
    return out


def _exchange_start(arrays, gather, name, after=None):
    n = len(arrays)
    e = 0 if after is None else 1
    lands = [lax.empty(((N_DEV,) + a.shape) if gather else a.shape, a.dtype) for a in arrays]

    def body(*refs):
        src, land = refs[:n], refs[n:2 * n]
        refs = refs[2 * n + e:]
        send, recv, token, local_sems = refs[:n], refs[n:2 * n], refs[4 * n], refs[4 * n + 1]
        me, out = _split_copies(gather, src, land, send, recv, True)
        local = [pltpu.make_async_copy(src[a] if gather else src[a].at[_slot(me)], land[a].at[_slot(me)], local_sems.at[a])
                 for a in range(n)]
        for cp in local:
            cp.start()
        for cp in local:
            cp.wait()
        for cp in out:
            cp.start()
        token[...] = jnp.zeros_like(token)

    sems = [pltpu.SemaphoreType.DMA((7,)) for _ in range(2 * n)]
    outs = pl.pallas_call(
        body, name=name,
        out_shape=(*sems, *[pltpu.HBM(a.shape, a.dtype) for a in arrays], *[pltpu.HBM(l.shape, l.dtype) for l in lands],
                   jax.ShapeDtypeStruct((8, 128), f32)),
        in_specs=[_HBM] * (2 * n) + [pl.BlockSpec(memory_space=pl.ANY)] * e,
        out_specs=(*[_SEM] * (2 * n), *[_HBM] * (2 * n), pl.BlockSpec(memory_space=pltpu.VMEM)),
        input_output_aliases={i: 2 * n + i for i in range(2 * n)},
        scratch_shapes=[pltpu.SemaphoreType.DMA((n,))],
        compiler_params=pltpu.CompilerParams(has_side_effects=pltpu.SideEffectType.DATAFLOW_SIDE_EFFECTING))(
        *[pltpu.with_memory_space_constraint(a, pltpu.HBM) for a in arrays],
        *[pltpu.with_memory_space_constraint(l, pltpu.HBM) for l in lands], *([after] if e else []))
    return (gather, n, outs[:4 * n]), outs[4 * n]


def _exchange_wait(handle, which, after, name):
    gather, n_all, vals = handle
    send_v, recv_v, src_v, land_v = [[vals[g * n_all + i] for i in which] for g in range(4)]
    n = len(which)

    def body(*refs):
        src, land, send, recv = refs[:n], refs[n:2 * n], refs[2 * n:3 * n], refs[3 * n:4 * n]
        for cp in _split_copies(gather, src, land, send, recv, False)[1]:
            cp.wait_send()
        for cp in _arrivals(gather, src, land, send, recv):
            cp.wait_recv()

    outs = pl.pallas_call(
        body, name=name,
        out_shape=(*[pltpu.HBM(a.shape, a.dtype) for a in src_v], *[pltpu.HBM(l.shape, l.dtype) for l in land_v]),
        in_specs=[*[_HBM] * (2 * n), *[_SEM] * (2 * n), pl.BlockSpec(memory_space=pl.ANY)], out_specs=[_HBM] * (2 * n),
        input_output_aliases={i: i for i in range(2 * n)},
        compiler_params=pltpu.CompilerParams(has_side_effects=pltpu.SideEffectType.DATAFLOW_SIDE_EFFECTING))(
        *src_v, *land_v, *send_v, *recv_v, after)
    return outs[n:]


def _seq_exchange(arrays, gather, name, collective_id):
    n = len(arrays)
    hbm = pltpu.MemorySpace.HBM
    srcs = [jax.new_ref(a, memory_space=hbm) for a in arrays]
    lands = [jax.empty_ref(jax.ShapeDtypeStruct(((N_DEV,) + a.shape) if gather else a.shape, a.dtype), memory_space=hbm) for a in arrays]

    @pl.kernel(mesh=plsc.ScalarSubcoreMesh(axis_name="sequencer", num_cores=1), name=name,
               scratch_types=(pltpu.SemaphoreType.DMA((7, n)), pltpu.SemaphoreType.DMA((7, n)), pltpu.SemaphoreType.DMA((n,))),
               compiler_params=pltpu.CompilerParams(collective_id=collective_id))
    def launch(send, recv, local):
        x, y, c = _coords()
        me = (x, y, c)
        peers = [(x ^ dx, y ^ dy, c ^ dc) for dx, dy, dc in _REL]
        barrier = pltpu.get_barrier_semaphore()
        for peer in peers:
            pl.semaphore_signal(barrier, inc=1, device_id=peer, device_id_type=MESH)
        pl.semaphore_wait(barrier, len(peers))

        def copy(a, k, peer, arrival):
            return pltpu.make_async_remote_copy(
                src_ref=srcs[a] if gather else srcs[a].at[_slot(peer)], dst_ref=lands[a].at[_slot(peer if arrival else me)],
                send_sem=send.at[k, a], recv_sem=recv.at[k, a], device_id=peer, device_id_type=MESH)

        mine = [pltpu.make_async_copy(srcs[a] if gather else srcs[a].at[_slot(me)], lands[a].at[_slot(me)], local.at[a])
                for a in range(n)]
        out = [copy(a, k, peer, False) for a in range(n) for k, peer in enumerate(peers)]
        for cp in mine + out:
            cp.start()
        for a in range(n):
            for k, peer in enumerate(peers):
                copy(a, k, peer, True).wait_recv()
        for cp in out:
            cp.wait_send()
        for cp in mine:
            cp.wait()

    launch()
    return [land[...] for land in lands]


def _adam_math(w, g, m, v):
    m_ = ADAM_B1 * m + (1.0 - ADAM_B1) * g
    v_ = ADAM_B2 * v + (1.0 - ADAM_B2) * jnp.square(g)
    m_hat = m_ / (1.0 - ADAM_B1 ** ADAM_STEP)
    v_hat = v_ / (1.0 - ADAM_B2 ** ADAM_STEP)
    return -ADAM_LR * (m_hat / (jnp.sqrt(v_hat) + ADAM_EPS) + ADAM_WD * w), m_, v_


def _reduce_adamw(parts, w, m, v, name):
    _, R, L = parts.shape
    tr = _pick(R, (256, 128, 64, 32, 16, 8))

    def body(p_ref, w_ref, m_ref, v_ref, g_ref, d_ref, nm_ref, nv_ref):
        g = p_ref[0].astype(f32)
        for i in range(1, N_DEV):
            g = g + p_ref[i].astype(f32)
        g_ref[...] = g
        d_ref[...], nm_ref[...], nv_ref[...] = _adam_math(w_ref[...], g, m_ref[...], v_ref[...])

    blk = pl.BlockSpec((tr, L), lambda i: (i, 0))
    sh = jax.ShapeDtypeStruct((R, L), f32)
    return _call(body, name, (R // tr,), [pl.BlockSpec((N_DEV, tr, L), lambda i: (0, i, 0)), blk, blk, blk], (blk,) * 4, (sh,) * 4,
                 sem=("parallel",))(parts, w, m, v)


SMALL = (("w_spatial", (512, 128), 0), ("norm1_g", (1, 1024), 512), ("mem_norm_g", (1, 1024), 520), ("norm2_g", (1, 1024), 528),
         ("final_g", (1, 1024), 536), ("lb_logits", (2, 512), 544), ("ln_v_g", (1, 512), 552), ("ln_v_b", (1, 512), 556),
         ("b_spatial", (4, 128), 560), ("hgrn_norm_g", (1, 128), 564), ("conv_b", (1, 2816), 565))
LOSS_ROW, SMALL_USED, SMALL_ROWS = 587, 588, 640


def _segments(shape, base):
    r, n = shape
    per = n // 128
    return [(base + i * per + j, i, slice(j * 128, (j + 1) * 128)) for i in range(r) for j in range(per)]


def _pack_small(gs, loss_part):
    names = [n for n, _, _ in SMALL]

    def body(*refs):
        src, loss_ref, o_ref = dict(zip(names, refs[:-2])), refs[-2], refs[-1]
        o_ref[SMALL_USED:SMALL_ROWS, :] = jnp.zeros((SMALL_ROWS - SMALL_USED, 128), f32)
        o_ref[LOSS_ROW:LOSS_ROW + 1, :] = loss_ref[...]
        for name, shape, base in SMALL:
            ref = src[name]
            if name == "w_spatial":
                o_ref[base:base + 512, :] = ref[...].reshape(512, 128)
            elif name == "b_spatial":
                o_ref[base:base + 4, :] = ref[0:4, :]
            elif name == "conv_b":
                per_example = functools.reduce(lambda u, v_: u + v_, [ref[b] for b in range(ref.shape[0])])
                for row, i, sl in _segments(shape, base):
                    o_ref[row:row + 1, :] = per_example[i:i + 1, sl]
            elif name == "hgrn_norm_g":
                per_head = [ref[b, h] for b in range(ref.shape[0]) for h in range(N_HEAD)]
                o_ref[base:base + 1, :] = functools.reduce(lambda u, v_: u + v_, per_head)
            else:
                for row, i, sl in _segments(shape, base):
                    o_ref[row:row + 1, :] = ref[i:i + 1, sl]

    return pl.pallas_call(body, name="pack_small", out_shape=jax.ShapeDtypeStruct((SMALL_ROWS, 128), f32))(
        *[gs[n] for n in names], loss_part)


def _small_update(gathered, w, m, v):
    names = [n for n, _, _ in SMALL]
    k = len(names)

    def body(*refs):
        p_ref = refs[0]
        ins = [dict(zip(names, refs[1 + i * k:1 + (i + 1) * k])) for i in range(3)]
        outs = [dict(zip(names, refs[1 + (3 + i) * k:1 + (4 + i) * k])) for i in range(4)]
        loss_ref, gsum = refs[-2], refs[-1]
        g = p_ref[0]
        for i in range(1, N_DEV):
            g = g + p_ref[i]
        gsum[...] = g
        loss_ref[...] = gsum[LOSS_ROW:LOSS_ROW + 1, :]
        for name, shape, base in SMALL:
            if name == "w_spatial":
                where = [(slice(base, base + 512), (slice(None), slice(None)))]
            else:
                where = [(slice(row, row + 1), (slice(i, i + 1), sl)) for row, i, sl in _segments(shape, base)]
            for rows, at in where:
                g_ = gsum[rows, :]
                d_, m_, v_ = _adam_math(ins[0][name][at], g_, ins[1][name][at], ins[2][name][at])
                for o, val in zip(outs, (g_, d_, m_, v_)):
                    o[name][at] = val

    args = [gathered] + [d[n] for d in (w, m, v) for n in names]
    out_shapes = [jax.ShapeDtypeStruct(shape, f32) for _ in range(4) for _, shape, _ in SMALL] + [jax.ShapeDtypeStruct((1, 128), f32)]
    outs = pl.pallas_call(body, name="small_update", out_shape=out_shapes, scratch_shapes=[pltpu.VMEM((SMALL_ROWS, 128), f32)])(*args)
    return [dict(zip(names, outs[i * k:(i + 1) * k])) for i in range(4)], outs[-1]


def _cols_full(g):
    return jnp.moveaxis(g, 0, -2).reshape(g.shape[1:-1] + (N_DEV * g.shape[-1],))


def _join_cols(g, name):
    _, R, n = g.shape
    tr = _pick(R, (256, 128))

    def body(g_ref, o_ref):
        for j in range(N_DEV):
            o_ref[:, j * n:(j + 1) * n] = g_ref[j]

    return _call(body, name, (R // tr,), [pl.BlockSpec((N_DEV, tr, n), lambda i: (0, i, 0))],
                 pl.BlockSpec((tr, N_DEV * n), lambda i: (i, 0)), jax.ShapeDtypeStruct((R, N_DEV * n), g.dtype), sem=("parallel",))(g)


def _cols_parts(full):
    n = full.shape[-1] // N_DEV
    return jnp.moveaxis(full.reshape(full.shape[:-1] + (N_DEV, n)), -2, 0)


def kernel(x, mem, norm1_g, w_in, ln_v_g, ln_v_b, w_spatial, b_spatial, lb_logits, hgrn_norm_g, mem_norm_g, w_mem_kv, w_branch, w_out, norm2_g, w_up, conv_w, conv_b, w_down, final_g, loss_target, m_norm1_g, m_w_in, m_ln_v_g, m_ln_v_b, m_w_spatial, m_b_spatial, m_lb_logits, m_hgrn_norm_g, m_mem_norm_g, m_w_mem_kv, m_w_branch, m_w_out, m_norm2_g, m_w_up, m_conv_w, m_conv_b, m_w_down, m_final_g, v_norm1_g, v_w_in, v_ln_v_g, v_ln_v_b, v_w_spatial, v_b_spatial, v_lb_logits, v_hgrn_norm_g, v_mem_norm_g, v_w_mem_kv, v_w_branch, v_w_out, v_norm2_g, v_w_up, v_conv_w, v_conv_b, v_w_down, v_final_g):
    given = dict(locals())
    order = ("norm1_g", "w_in", "ln_v_g", "ln_v_b", "w_spatial", "b_spatial", "lb_logits", "hgrn_norm_g", "mem_norm_g",
             "w_mem_kv", "w_branch", "w_out", "norm2_g", "w_up", "conv_w", "conv_b", "w_down", "final_g")
    groups = dict(a=("w_in",), b=("w_mem_kv", "w_branch", "w_out"), c=("w_up", "conv_w", "w_down"))

    wire = {n: given[n][0].astype(f32 if n == "conv_w" else bf16) for ns in groups.values() for n in ns}
    g_in = _all_gather([wire["w_in"]], "gather_w_in")[0]
    w_in_full = _join_cols(g_in, "join_w_in")
    w_in_full, wire_b, wire_c = lax.optimization_barrier((w_in_full, [wire[n] for n in groups["b"]], [wire[n] for n in groups["c"]]))
    rest_b = _seq_exchange(wire_b, True, "gather_b", 1)
    rest_c = _seq_exchange(wire_c, True, "gather_c", 6)

    def late_b(after):
        _, (kv_, br_, out_) = lax.optimization_barrier((after, tuple(rest_b)))
        br_ = _cols_full(br_)
        return dict(w_mem_kv=kv_.reshape(D_MODEL, 2 * 512), w_branch=[br_[n] for n in range(3)], w_out=out_.reshape(D_MODEL, D_MODEL))

    def late_c(after):
        _, (up_, cw_, down_) = lax.optimization_barrier((after, tuple(rest_c)))
        up_ = _join_cols(up_, "join_w_up")
        return dict(w_up=up_, conv_w=_cols_full(cw_), w_down=down_.reshape(D_FF, D_MODEL))

    to_parts = dict(w_in=lambda g_: g_, w_up=lambda g_: g_, conv_w=_cols_parts,
                    w_branch=lambda g_: _cols_parts(g_.astype(bf16)).reshape(N_DEV, -1, 128),
                    w_mem_kv=lambda g_: g_.reshape(N_DEV, -1, 2 * 512), w_out=lambda g_: g_.reshape(N_DEV, -1, D_MODEL),
                    w_down=lambda g_: g_.reshape(N_DEV, -1, D_MODEL))
    scatters = {}

    def send(tag, grads_):
        parts = [to_parts[n](grads_[n]) for n in groups[tag]]
        scatters[tag] = _seq_exchange(parts, False, f"scatter_{tag}", dict(a=2, b=4, c=5)[tag])
        return jnp.zeros((8, 128), f32)

    small_2d = lambda prefix: {n: given[prefix + n].reshape(shape) for n, shape, _ in SMALL}
    p = small_2d("")
    p["w_spatial"] = w_spatial[0]
    updates = {}

    def update(tag):
        for n, parts in zip(groups[tag], scatters[tag]):
            two_d = (-1, given[n].shape[-1])
            updates[n] = _reduce_adamw(parts, *[given[pre + n].reshape(two_d) for pre in ("", "m_", "v_")], "adamw_" + n)

    def settle(chain):
        update("c")
        update("b")
        early = groups["c"] + groups["b"]
        chain, tied = lax.optimization_barrier((chain, [updates[n] for n in early]))
        updates.update(zip(early, tied))
        return chain

    loss_part, grad_x, gs = _local_step(x, mem, loss_target, p, w_in_full, late_b, late_c, send, settle)

    gathered = _seq_exchange([_pack_small(gs, loss_part)], True, "gather_small", 3)[0]

    update("a")
    grads, delta, new_m, new_v = {}, {}, {}, {}
    for n, res in updates.items():
        grads[n], delta[n], new_m[n], new_v[n] = [r.reshape(given[n].shape) for r in res]

    small_results, loss_row = _small_update(gathered, small_2d(""), small_2d("m_"), small_2d("v_"))
    for dst, res in zip((grads, delta, new_m, new_v), small_results):
        for n, _, _ in SMALL:
            dst[n] = res[n].reshape(given[n].shape)
    loss = loss_row[0, 0]

    return (loss, grad_x, *[grads[n] for n in order], *[delta[n] for n in order], *[new_m[n] for n in order],
            *[new_v[n] for n in order])
```

```python
import functools

import jax
import jax.numpy as jnp
from jax import lax
from jax.experimental import pallas as pl
from jax.experimental.pallas import tpu as pltpu
from jax.experimental.pallas import tpu_sc as plsc

f32 = jnp.float32
bf16 = jnp.bfloat16

N_DEV = 8
D_MODEL = 1024
EPS = 1e-6
GM_CHUNK = 128
HG_CHUNK = 64
HEAD = 128
N_HEAD = 4
MEM_LEN = 256
D_FF = 2816
IN_WIDTH = 6656
C_ZU, C_HQ, C_HF, C_HI, C_HG, C_XQ, C_GL = 0, 1024, 1536, 2048, 2560, 3072, 3584
ADAM_LR, ADAM_B1, ADAM_B2, ADAM_EPS, ADAM_WD, ADAM_STEP = 0.001, 0.9, 0.999, 1e-08, 0.01, 10
VMEM_LIMIT = 56 * 1024 * 1024
MESH = pl.DeviceIdType.MESH


def _pick(n, cands):
    for c in cands:
        if n % c == 0:
            return c
    return n


def _call(body, name, grid, in_specs, out_specs, out_shape, scratch=(), sem=None, **cp):
    params = dict(vmem_limit_bytes=VMEM_LIMIT, **cp)
    if sem is not None:
        params["dimension_semantics"] = sem
    return pl.pallas_call(
        body, name=name, grid=grid, in_specs=in_specs, out_specs=out_specs, out_shape=out_shape,
        scratch_shapes=list(scratch), compiler_params=pltpu.CompilerParams(**params))


_DN = {"nn": (((1,), (0,)), ((), ())), "nt": (((1,), (1,)), ((), ())), "tn": (((0,), (0,)), ((), ()))}


def _raw_dot(a, b, mode):
    return lax.dot_general(a.astype(bf16), b.astype(bf16), _DN[mode], preferred_element_type=f32)


@jax.custom_vjp
def _dot_nn(a, b):
    return _raw_dot(a, b, "nn")


_dot_nn.defvjp(lambda a, b: (_raw_dot(a, b, "nn"), (a, b)),
               lambda r, g: (_raw_dot(g, r[1], "nt"), _raw_dot(r[0], g, "tn")))


@jax.custom_vjp
def _dot_nt(a, b):
    return _raw_dot(a, b, "nt")


_dot_nt.defvjp(lambda a, b: (_raw_dot(a, b, "nt"), (a, b)),
               lambda r, g: (_raw_dot(g, r[1], "nn"), _raw_dot(g, r[0], "tn")))


@jax.custom_vjp
def _dot_tn(a, b):
    return _raw_dot(a, b, "tn")


_dot_tn.defvjp(lambda a, b: (_raw_dot(a, b, "tn"), (a, b)),
               lambda r, g: (_raw_dot(r[1], g, "nt"), _raw_dot(r[0], g, "nn")))


def _tri(n, lower):
    r = lax.broadcasted_iota(jnp.int32, (n, n), 0)
    c = lax.broadcasted_iota(jnp.int32, (n, n), 1)
    return ((c <= r) if lower else (c >= r)).astype(f32)


def _sel_dot(sel, x, mode, x_first=False, pieces=3):
    sel = sel.astype(bf16)
    out, rest = None, x
    for p in range(pieces):
        piece = rest.astype(bf16)
        part = lax.dot_general(*((piece, sel) if x_first else (sel, piece)), _DN[mode], preferred_element_type=f32)
        out = part if out is None else out + part
        if p + 1 < pieces:
            rest = rest - piece.astype(f32)
    return out


def _egrad(fn, x, ct):
    return jax.vjp(fn, x)[1](ct)[0]


def _mm(a, b, mode, out_dtype, name, tm, tn, tk=None, residual=None, shard=None, n_outer=False, into=None):
    if mode == "nn":
        (M, K), (_, N) = a.shape, b.shape
    elif mode == "nt":
        (M, K), (N, _) = a.shape, b.shape
    else:
        (K, M), (_, N) = a.shape, b.shape
    tm, tn = min(tm, M), min(tn, N)
    tk = K if tk is None else min(tk, K)
    assert M % tm == 0 and N % tn == 0 and K % tk == 0, (name, M, N, K, tm, tn, tk)
    nk = K // tk

    def body(*refs):
        acc_ref = refs[-1] if nk > 1 else None
        refs = refs[:-1] if nk > 1 else refs
        if residual is None:
            a_ref, b_ref, *_, o_ref = refs
        else:
            a_ref, b_ref, r_ref, o_ref = refs

        def finish(r):
            if residual is not None:
                r = r + r_ref[...]
            if shard is None:
                o_ref[...] = r.astype(out_dtype)
            else:
                for s in range(tn // shard):
                    o_ref[s] = r[:, s * shard:(s + 1) * shard].astype(out_dtype)

        part = _raw_dot(a_ref[...], b_ref[...], mode)
        if nk == 1:
            finish(part)
            return
        k = pl.program_id(2)

        @pl.when(k == 0)
        def _():
            acc_ref[...] = part

        @pl.when((k > 0) & (k < nk - 1))
        def _():
            acc_ref[...] += part

        @pl.when(k == nk - 1)
        def _():
            finish(acc_ref[...] + part)

    def at(index):
        return (lambda j, i, k: index(i, j, k)) if n_outer else index

    a_spec = {"nn": pl.BlockSpec((tm, tk), at(lambda i, j, k: (i, k))),
              "nt": pl.BlockSpec((tm, tk), at(lambda i, j, k: (i, k))),
              "tn": pl.BlockSpec((tk, tm), at(lambda i, j, k: (k, i)))}[mode]
    b_spec = {"nn": pl.BlockSpec((tk, tn), at(lambda i, j, k: (k, j))),
              "nt": pl.BlockSpec((tn, tk), at(lambda i, j, k: (j, k))),
              "tn": pl.BlockSpec((tk, tn), at(lambda i, j, k: (k, j)))}[mode]
    o_spec = pl.BlockSpec((tm, tn), at(lambda i, j, k: (i, j)))
    in_specs = [a_spec, b_spec] + ([o_spec] if residual is not None else [])
    args = (a, b) + ((residual,) if residual is not None else ())
    out_shape = jax.ShapeDtypeStruct((M, N), out_dtype)
    extra = {}
    if shard is not None:
        assert residual is None and tn % shard == 0
        per = tn // shard
        first = 0 if into is None else into[1] // per
        o_spec = pl.BlockSpec((per, tm, shard), at(lambda i, j, k: (j + first, i, 0)))
        out_shape = jax.ShapeDtypeStruct((N // shard, M, shard), out_dtype)
        if into is not None:
            out_shape = jax.ShapeDtypeStruct(into[0].shape, out_dtype)
            in_specs, args = in_specs + [pl.BlockSpec(memory_space=pl.ANY)], args + (into[0],)
            extra = dict(input_output_aliases={2: 0})
    grid = (N // tn, M // tm, nk) if n_outer else (M // tm, N // tn, nk)
    return pl.pallas_call(
        body, name=name, grid=grid, in_specs=in_specs, out_specs=o_spec, out_shape=out_shape,
        scratch_shapes=[pltpu.VMEM((tm, tn), f32)] if nk > 1 else [],
        compiler_params=pltpu.CompilerParams(vmem_limit_bytes=VMEM_LIMIT, dimension_semantics=("parallel", "parallel", "arbitrary")),
        **extra)(*args)


def _rms_fwd(x, g, name, transposed=False):
    R, Dd = x.shape
    tr = _pick(R, (512, 256, 128))

    def body(x_ref, g_ref, o_ref, *t_ref):
        xf = x_ref[...]
        y = xf * lax.rsqrt(jnp.mean(xf * xf, axis=-1, keepdims=True) + EPS) * g_ref[...]
        o_ref[...] = y.astype(bf16)
        if transposed:
            t_ref[0][...] = y.T.astype(bf16)

    row = pl.BlockSpec((tr, Dd), lambda i: (i, 0))
    out_specs, out_shape = row, jax.ShapeDtypeStruct((R, Dd), bf16)
    if transposed:
        out_specs, out_shape = (row, pl.BlockSpec((Dd, tr), lambda i: (0, i))), (out_shape, jax.ShapeDtypeStruct((Dd, R), bf16))
    return _call(body, name, (R // tr,), [row, pl.BlockSpec((1, Dd), lambda i: (0, 0))], out_specs, out_shape, sem=("parallel",))(x, g)


def _rms_bwd(x, g, dh, name, residual=None):
    R, Dd = x.shape
    tr = _pick(R, (512, 256, 128))

    def body(*refs):
        if residual is None:
            x_ref, g_ref, dh_ref, dx_ref, dg_ref = refs
        else:
            x_ref, g_ref, dh_ref, r_ref, dx_ref, dg_ref = refs
        xf = x_ref[...]
        rs = lax.rsqrt(jnp.mean(xf * xf, axis=-1, keepdims=True) + EPS)
        y = xf * rs
        dh_ = dh_ref[...].astype(f32)
        dy = dh_ * g_ref[...]
        dx = rs * (dy - y * jnp.mean(dy * y, axis=-1, keepdims=True))
        if residual is not None:
            dx = dx + r_ref[...]
        dx_ref[...] = dx

        @pl.when(pl.program_id(0) == 0)
        def _():
            dg_ref[...] = jnp.zeros_like(dg_ref)

        dg_ref[...] += jnp.sum(dh_ * y, axis=0, keepdims=True)

    row = pl.BlockSpec((tr, Dd), lambda i: (i, 0))
    vec = pl.BlockSpec((1, Dd), lambda i: (0, 0))
    in_specs = [row, vec, row] + ([row] if residual is not None else [])
    args = (x, g, dh) + ((residual,) if residual is not None else ())
    return _call(body, name, (R // tr,), in_specs, (row, vec),
                 (jax.ShapeDtypeStruct((R, Dd), f32), jax.ShapeDtypeStruct((1, Dd), f32)), sem=("arbitrary",))(*args)


def _mm_rms_bwd(pairs, x, g, residual, name, tm):
    M = x.shape[0]
    Dd = x.shape[1]
    tm = min(tm, M)
    n = len(pairs)

    def body(*refs):
        ab_refs, (x_ref, g_ref, r_ref, dx_ref, dg_ref) = refs[:2 * n], refs[2 * n:]
        dh_ = _raw_dot(ab_refs[0][...], ab_refs[1][...], "nt")
        for k in range(1, n):
            dh_ = dh_ + _raw_dot(ab_refs[2 * k][...], ab_refs[2 * k + 1][...], "nt")
        xf = x_ref[...]
        rs = lax.rsqrt(jnp.mean(xf * xf, axis=-1, keepdims=True) + EPS)
        y = xf * rs
        dy = dh_ * g_ref[...]
        dx_ref[...] = rs * (dy - y * jnp.mean(dy * y, axis=-1, keepdims=True)) + r_ref[...]

        @pl.when(pl.program_id(0) == 0)
        def _():
            dg_ref[...] = jnp.zeros_like(dg_ref)

        dg_ref[...] += jnp.sum(dh_ * y, axis=0, keepdims=True)

    row = pl.BlockSpec((tm, Dd), lambda i: (i, 0))
    vec = pl.BlockSpec((1, Dd), lambda i: (0, 0))
    in_specs, args = [], []
    for a, b, k in pairs:
        in_specs += [pl.BlockSpec((tm, a.shape[1]), lambda i: (i, 0)),
                     pl.BlockSpec((b.shape[0], a.shape[1]), functools.partial(lambda i, k_: (0, k_), k_=k))]
        args += [a, b]
    in_specs += [row, vec, row]
    args += [x, g, residual]
    return _call(body, name, (M // tm,), in_specs, (row, vec),
                 (jax.ShapeDtypeStruct((M, Dd), f32), jax.ShapeDtypeStruct((1, Dd), f32)), sem=("arbitrary",))(*args)


def _down_final_loss(act, w_down, x1, g, target):
    R, Dd = x1.shape
    tr = _pick(R, (512, 256, 128))

    def body(a_ref, w_ref, x1_ref, g_ref, t_ref, loss_ref, dx_ref, dxb_ref, dg_ref):
        xf = _raw_dot(a_ref[...], w_ref[...], "nn") + x1_ref[...]
        rs = lax.rsqrt(jnp.mean(xf * xf, axis=-1, keepdims=True) + EPS)
        y = xf * rs
        err = y * g_ref[...] - t_ref[...]
        dh_ = err * (1.0 / Dd)
        dy = dh_ * g_ref[...]
        dx = rs * (dy - y * jnp.mean(dy * y, axis=-1, keepdims=True))
        dx_ref[...] = dx
        dxb_ref[...] = dx.astype(bf16)

        @pl.when(pl.program_id(0) == 0)
        def _():
            dg_ref[...] = jnp.zeros_like(dg_ref)
            loss_ref[...] = jnp.zeros_like(loss_ref)

        dg_ref[...] += jnp.sum(dh_ * y, axis=0, keepdims=True)
        part = jnp.sum(jnp.mean(err * err, axis=-1, keepdims=True), axis=0, keepdims=True)
        loss_ref[...] += 0.5 * part

    row = pl.BlockSpec((tr, Dd), lambda i: (i, 0))
    vec = pl.BlockSpec((1, Dd), lambda i: (0, 0))
    in_specs = [pl.BlockSpec((tr, act.shape[1]), lambda i: (i, 0)), pl.BlockSpec(w_down.shape, lambda i: (0, 0)), row, vec, row]
    return _call(body, "down_final_loss", (R // tr,), in_specs, (pl.BlockSpec((1, 128), lambda i: (0, 0)), row, row, vec),
                 (jax.ShapeDtypeStruct((1, 128), f32), jax.ShapeDtypeStruct((R, Dd), f32), jax.ShapeDtypeStruct((R, Dd), bf16),
                  jax.ShapeDtypeStruct((1, Dd), f32)), sem=("arbitrary",))(act, w_down, x1, g, target)


def _gmlp_parts(zuv, ln_g, ln_b):
    zu, zv = zuv[:, :512], zuv[:, 512:]
    u = jax.nn.gelu(zu)
    v = jax.nn.gelu(zv)
    mu = jnp.mean(v, axis=-1, keepdims=True)
    rs = lax.rsqrt(jnp.mean(jnp.square(v - mu), axis=-1, keepdims=True) + EPS)
    xh = (v - mu) * rs
    return zu, zv, u, xh, rs, xh * ln_g + ln_b


GM_TILE_CHUNKS = 4


def _gmlp_tile(T):
    n = _pick(T // GM_CHUNK, (GM_TILE_CHUNKS, 2, 1))
    return n, n * GM_CHUNK


def _gmlp_fwd(proj, ln_g, ln_b, w_s, b_st):
    T = proj.shape[0]
    nch, rows = _gmlp_tile(T)

    def body(p_ref, g_ref, b_ref, w_ref, bs_ref, o_ref):
        _, _, u, _, _, vn = _gmlp_parts(p_ref[...].astype(f32), g_ref[...], b_ref[...])
        causal = _tri(GM_CHUNK, True) > 0
        for gi in range(N_HEAD):
            sl = slice(gi * HEAD, (gi + 1) * HEAD)
            w = jnp.where(causal, w_ref[gi], 0.0)
            for ch in range(nch):
                rs_ = slice(ch * GM_CHUNK, (ch + 1) * GM_CHUNK)
                mixed = _raw_dot(w, vn[rs_, sl], "nn") + bs_ref[:, gi:gi + 1]
                o_ref[rs_, sl] = (u[rs_, sl] * mixed).astype(bf16)

    vec = pl.BlockSpec((1, 512), lambda i: (0, 0))
    return _call(body, "gmlp_fwd", (T // rows,),
                 [pl.BlockSpec((rows, 1024), lambda i: (i, 0)), vec, vec,
                  pl.BlockSpec((N_HEAD, GM_CHUNK, GM_CHUNK), lambda i: (0, 0, 0)), pl.BlockSpec((GM_CHUNK, 128), lambda i: (0, 0))],
                 pl.BlockSpec((rows, 512), lambda i: (i, 0)), jax.ShapeDtypeStruct((T, 512), bf16), sem=("parallel",))(
        proj, ln_g, ln_b, w_s, b_st)


def _gmlp_bwd(proj, ln_g, ln_b, w_s, b_st, da):
    T = proj.shape[0]
    nch, rows = _gmlp_tile(T)

    def body(p_ref, g_ref, b_ref, w_ref, bs_ref, da_ref, dp_ref, dg_ref, db_ref, dw_ref, dbs_ref):
        zu, zv, u, xh, rs, vn = _gmlp_parts(p_ref[...].astype(f32), g_ref[...], b_ref[...])
        causal = _tri(GM_CHUNK, True) > 0
        sub = lax.broadcasted_iota(jnp.int32, (8, GM_CHUNK), 0)
        ones = jnp.ones((8, HEAD), f32)
        dout = da_ref[...].astype(f32)

        @pl.when(pl.program_id(0) == 0)
        def _():
            for r in (dg_ref, db_ref, dw_ref, dbs_ref):
                r[...] = jnp.zeros_like(r)

        du, dvn, dbs = [], [], jnp.zeros((8, GM_CHUNK), f32)
        for gi in range(N_HEAD):
            sl = slice(gi * HEAD, (gi + 1) * HEAD)
            w = jnp.where(causal, w_ref[gi], 0.0)
            du_g, dvn_g, dw_g = [], [], jnp.zeros((GM_CHUNK, GM_CHUNK), f32)
            for ch in range(nch):
                rs_ = slice(ch * GM_CHUNK, (ch + 1) * GM_CHUNK)
                mixed = _raw_dot(w, vn[rs_, sl], "nn") + bs_ref[:, gi:gi + 1]
                du_g.append(dout[rs_, sl] * mixed)
                dm = dout[rs_, sl] * u[rs_, sl]
                dbs = dbs + jnp.where(sub == gi, _sel_dot(ones, dm, "nt"), 0.0)
                dw_g = dw_g + _raw_dot(dm, vn[rs_, sl], "nt")
                dvn_g.append(_raw_dot(w, dm, "tn"))
            dw_ref[gi] += jnp.where(causal, dw_g, 0.0)
            du.append(jnp.concatenate(du_g, axis=0))
            dvn.append(jnp.concatenate(dvn_g, axis=0))
        dbs_ref[...] += dbs
        du = jnp.concatenate(du, axis=-1)
        dvn = jnp.concatenate(dvn, axis=-1)
        dg_ref[...] += jnp.sum(dvn * xh, axis=0, keepdims=True)
        db_ref[...] += jnp.sum(dvn, axis=0, keepdims=True)
        dxh = dvn * g_ref[...]
        dv = rs * (dxh - jnp.mean(dxh, axis=-1, keepdims=True) - xh * jnp.mean(dxh * xh, axis=-1, keepdims=True))
        dp_ref[:, :512] = _egrad(jax.nn.gelu, zu, du).astype(bf16)
        dp_ref[:, 512:] = _egrad(jax.nn.gelu, zv, dv).astype(bf16)

    vec = pl.BlockSpec((1, 512), lambda i: (0, 0))
    wsp = pl.BlockSpec((N_HEAD, GM_CHUNK, GM_CHUNK), lambda i: (0, 0, 0))
    return _call(body, "gmlp_bwd", (T // rows,),
                 [pl.BlockSpec((rows, 1024), lambda i: (i, 0)), vec, vec, wsp, pl.BlockSpec((GM_CHUNK, 128), lambda i: (0, 0)),
                  pl.BlockSpec((rows, 512), lambda i: (i, 0))],
                 (pl.BlockSpec((rows, 1024), lambda i: (i, 0)), vec, vec, wsp, pl.BlockSpec((8, GM_CHUNK), lambda i: (0, 0))),
                 (jax.ShapeDtypeStruct((T, 1024), bf16), jax.ShapeDtypeStruct((1, 512), f32), jax.ShapeDtypeStruct((1, 512), f32),
                  jax.ShapeDtypeStruct((N_HEAD, GM_CHUNK, GM_CHUNK), f32), jax.ShapeDtypeStruct((8, GM_CHUNK), f32)),
                 sem=("arbitrary",))(proj, ln_g, ln_b, w_s, b_st, da)


HG_SUB = 8
HG_NSUB = HG_CHUNK // HG_SUB


def _two_level_matrix(transposed=False):
    shape = (HG_CHUNK, 2 * HG_CHUNK) if transposed else (2 * HG_CHUNK, HG_CHUNK)
    r = lax.broadcasted_iota(jnp.int32, shape, 1 if transposed else 0)
    c = lax.broadcasted_iota(jnp.int32, shape, 0 if transposed else 1)
    t = jnp.where(r < HG_CHUNK, r, r - HG_CHUNK)
    local = (r < HG_CHUNK) & (t // HG_SUB == c // HG_SUB) & (c <= t)
    before = (r >= HG_CHUNK) & (c < (t // HG_SUB) * HG_SUB)
    return (local | before).astype(f32)


def _two_level_sums(x):
    two = _sel_dot(_two_level_matrix(), x, "nn")
    return two[:HG_CHUNK], two[HG_CHUNK:]


@jax.custom_vjp
def _two_level_cumsum(x):
    return _two_level_sums(x)


_two_level_cumsum.defvjp(
    lambda x: (_two_level_sums(x), None),
    lambda _, g: (_sel_dot(_two_level_matrix(), jnp.concatenate(g, axis=0), "tn"),))


def _tile_matrix():
    s = lax.broadcasted_iota(jnp.int32, (HG_SUB, HG_CHUNK), 0)
    j = lax.broadcasted_iota(jnp.int32, (HG_SUB, HG_CHUNK), 1)
    return (j % HG_SUB == s).astype(f32)


@jax.custom_vjp
def _tile_lanes(x):
    return _sel_dot(_tile_matrix(), x, "nn", x_first=True, pieces=1)


_tile_lanes.defvjp(
    lambda x: (_sel_dot(_tile_matrix(), x, "nn", x_first=True, pieces=1), None),
    lambda _, g: (_sel_dot(_tile_matrix(), g, "nt", x_first=True, pieces=2),))


def _block_rows(x):
    k = x.shape[-1]
    return jnp.broadcast_to(x.reshape(HG_NSUB, 1, HG_SUB, k), (HG_NSUB, HG_SUB, HG_SUB, k)).reshape(HG_CHUNK, HG_SUB, k)


def _hgrn_chunk(st0, q_raw, f_raw, i_raw, g_raw, l0, l1, ng):
    C, SUB = HG_CHUNK, HG_SUB
    lb = jax.nn.sigmoid(l0 - l1)
    fg = lb + (1.0 - lb) * jax.nn.sigmoid(f_raw)
    kk = 1.0 - fg
    qf = jax.nn.silu(q_raw)
    al, base = _two_level_cumsum(jnp.log(fg))
    a = al + base
    row = lax.broadcasted_iota(jnp.int32, (C, HEAD), 0)
    a_last = jnp.sum(jnp.where(row == C - 1, a, 0.0), axis=0, keepdims=True)
    inter = _dot_nt(qf * jnp.exp(a), st0)
    qt = qf * jnp.exp(al)
    rb = lax.broadcasted_iota(jnp.int32, (C, C), 0) // SUB
    cb = lax.broadcasted_iota(jnp.int32, (C, C), 1) // SUB
    scores = jnp.zeros((C, C), f32)
    for i in range(1, HG_NSUB):
        base_i = jnp.sum(jnp.where(row == i * SUB, base, 0.0), axis=0, keepdims=True)
        kt = kk * jnp.exp(jnp.minimum(base_i - a, 0.0))
        scores = scores + jnp.where((rb == i) & (cb < i), _dot_nt(qt, kt), 0.0)
    t_i = lax.broadcasted_iota(jnp.int32, (C, SUB, HEAD), 0) % SUB
    s_i = lax.broadcasted_iota(jnp.int32, (C, SUB, HEAD), 1)
    decay = jnp.exp(jnp.where(s_i <= t_i, al[:, None, :] - _block_rows(al), -jnp.inf))
    diag = jnp.sum(qf[:, None, :] * decay * _block_rows(kk), axis=-1)
    scores = scores + jnp.where(rb == cb, _tile_lanes(diag), 0.0)
    o = inter + _dot_nn(scores, i_raw)
    st1 = jnp.exp(a_last) * st0 + _dot_tn(i_raw, kk * jnp.exp(a_last - a))
    on = o * lax.rsqrt(jnp.mean(o * o, axis=-1, keepdims=True) + EPS) * ng
    return st1, on * jax.nn.silu(g_raw)


def _hgrn_specs(S, Bl, rev):
    N = S // HG_CHUNK
    chunk = (lambda n: N - 1 - n) if rev else (lambda n: n)
    col = lambda c0: pl.BlockSpec((Bl, HG_CHUNK, 512), lambda n: (0, chunk(n), c0 // 512))
    st = pl.BlockSpec((Bl, N_HEAD, 1, HEAD, HEAD), lambda n: (0, 0, chunk(n), 0, 0))
    full = lambda *s: pl.BlockSpec(s, functools.partial(lambda n, nd: (0,) * nd, nd=len(s)))
    return N, col, st, full


def _hgrn_fwd(proj, lb_logits, ng, Bl, S):
    N, col, st, full = _hgrn_specs(S, Bl, False)

    def body(q_ref, f_ref, i_ref, g_ref, l_ref, ng_ref, o_ref, st_ref, state):
        @pl.when(pl.program_id(0) == 0)
        def _():
            state[...] = jnp.zeros_like(state)

        for b in range(Bl):
            for h in range(N_HEAD):
                sl = slice(h * HEAD, (h + 1) * HEAD)
                st0 = state[b, h]
                st_ref[b, h, 0] = st0
                st1, out = _hgrn_chunk(st0, *[r[b, :, sl].astype(f32) for r in (q_ref, f_ref, i_ref, g_ref)],
                                       l_ref[0:1, sl], l_ref[1:2, sl], ng_ref[...])
                state[b, h] = st1
                o_ref[b, :, sl] = out.astype(bf16)

    return _call(body, "hgrn_fwd", (N,), [col(C_HQ), col(C_HF), col(C_HI), col(C_HG), full(2, 512), full(1, HEAD)],
                 (col(0), st),
                 (jax.ShapeDtypeStruct((Bl, S, 512), bf16), jax.ShapeDtypeStruct((Bl, N_HEAD, N, HEAD, HEAD), f32)),
                 scratch=[pltpu.VMEM((Bl, N_HEAD, HEAD, HEAD), f32)], sem=("arbitrary",))(
        proj, proj, proj, proj, lb_logits, ng)


def _hgrn_bwd(proj, lb_logits, ng, states, db, Bl, S):
    N, col, st, full = _hgrn_specs(S, Bl, True)

    def body(q_ref, f_ref, i_ref, g_ref, l_ref, ng_ref, st_ref, db_ref,
             dq_ref, df_ref, di_ref, dg_ref, dl_ref, dng_ref, dstate):
        @pl.when(pl.program_id(0) == 0)
        def _():
            dstate[...] = jnp.zeros_like(dstate)
            dl_ref[...] = jnp.zeros_like(dl_ref)
            dng_ref[...] = jnp.zeros_like(dng_ref)

        for b in range(Bl):
            for h in range(N_HEAD):
                sl = slice(h * HEAD, (h + 1) * HEAD)
                _, vjp = jax.vjp(_hgrn_chunk, st_ref[b, h, 0], *[r[b, :, sl].astype(f32) for r in (q_ref, f_ref, i_ref, g_ref)],
                                 l_ref[0:1, sl], l_ref[1:2, sl], ng_ref[...])
                dst0, dq, df, di, dg, dl0, dl1, dng = vjp((dstate[b, h], db_ref[b, :, sl].astype(f32)))
                dstate[b, h] = dst0
                dq_ref[b, :, sl] = dq.astype(bf16)
                df_ref[b, :, sl] = df.astype(bf16)
                di_ref[b, :, sl] = di.astype(bf16)
                dg_ref[b, :, sl] = dg.astype(bf16)
                dl_ref[0:1, sl] += dl0
                dl_ref[1:2, sl] += dl1
                dng_ref[b, h] += dng

    return _call(body, "hgrn_bwd", (N,),
                 [col(C_HQ), col(C_HF), col(C_HI), col(C_HG), full(2, 512), full(1, HEAD), st, col(0)],
                 (*[col(0)] * 4, full(2, 512), full(Bl, N_HEAD, 1, HEAD)),
                 (*[jax.ShapeDtypeStruct((Bl, S, 512), bf16)] * 4, jax.ShapeDtypeStruct((2, 512), f32),
                  jax.ShapeDtypeStruct((Bl, N_HEAD, 1, HEAD), f32)),
                 scratch=[pltpu.VMEM((Bl, N_HEAD, HEAD, HEAD), f32)], sem=("arbitrary",))(
        proj, proj, proj, proj, lb_logits, ng, states, db)


def _attn_probs(q, k):
    s = _raw_dot(q, k, "nt") * (HEAD ** -0.5)
    e = jnp.exp(s - jnp.max(s, axis=-1, keepdims=True))
    return e / jnp.sum(e, axis=-1, keepdims=True)


def _attn_specs(S, tq):
    nq = S // tq
    q = pl.BlockSpec((tq, 512), lambda b, i: (b * nq + i, C_XQ // 512))
    kv = pl.BlockSpec((1, MEM_LEN, 1024), lambda b, i: (b, 0, 0))
    o = pl.BlockSpec((tq, 512), lambda b, i: (b * nq + i, 0))
    return nq, q, kv, o


def _attn_fwd(proj, kv, Bl, S):
    tq = _pick(S, (512, 256, 128))
    nq, qs, kvs, os_ = _attn_specs(S, tq)

    def body(q_ref, kv_ref, o_ref):
        for h in range(N_HEAD):
            sl = slice(h * HEAD, (h + 1) * HEAD)
            p = _attn_probs(q_ref[:, sl], kv_ref[0, :, sl])
            o_ref[:, sl] = _raw_dot(p, kv_ref[0, :, 512 + h * HEAD:512 + (h + 1) * HEAD], "nn").astype(bf16)

    return _call(body, "attn_fwd", (Bl, nq), [qs, kvs], os_, jax.ShapeDtypeStruct((Bl * S, 512), bf16),
                 sem=("parallel", "parallel"))(proj, kv)


def _attn_bwd(proj, kv, dc, Bl, S):
    tq = _pick(S, (512, 256, 128))
    nq, qs, kvs, os_ = _attn_specs(S, tq)

    def body(q_ref, kv_ref, do_ref, dq_ref, dkv_ref):
        @pl.when(pl.program_id(1) == 0)
        def _():
            dkv_ref[...] = jnp.zeros_like(dkv_ref)

        for h in range(N_HEAD):
            sl = slice(h * HEAD, (h + 1) * HEAD)
            vsl = slice(512 + h * HEAD, 512 + (h + 1) * HEAD)
            q, k, v, do = q_ref[:, sl], kv_ref[0, :, sl], kv_ref[0, :, vsl], do_ref[:, sl]
            p = _attn_probs(q, k)
            dkv_ref[0, :, vsl] += _raw_dot(p, do, "tn")
            dp = _raw_dot(do, v, "nt")
            ds = p * (dp - jnp.sum(dp * p, axis=-1, keepdims=True)) * (HEAD ** -0.5)
            dq_ref[:, sl] = _raw_dot(ds, k, "nn").astype(bf16)
            dkv_ref[0, :, sl] += _raw_dot(ds, q, "tn")

    return _call(body, "attn_bwd", (Bl, nq), [qs, kvs, os_], (os_, kvs),
                 (jax.ShapeDtypeStruct((Bl * S, 512), bf16), jax.ShapeDtypeStruct((Bl, MEM_LEN, 1024), f32)),
                 sem=("arbitrary", "arbitrary"))(proj, kv, dc)


def _gate_specs(tm):
    half = D_MODEL // 2
    return [pl.BlockSpec((tm, half), functools.partial(lambda i, c: (i, c), c=(C_GL + n * D_MODEL) // half + k))
            for n in range(3) for k in range(2)]


def _merge_out_norm_fwd(branches, wb, proj, w_out, x, g):
    T = proj.shape[0]
    tm = _pick(T, (512, 256, 128))

    def body(a_ref, b_ref, c_ref, w0, w1, w2, g0a, g0b, g1a, g1b, g2a, g2b, wo_ref, x_ref, g_ref, m_ref, x1_ref, h_ref, ht_ref):
        acc = jnp.zeros((tm, D_MODEL), f32)
        for x_n, w_ref, ga, gb in ((a_ref, w0, g0a, g0b), (b_ref, w1, g1a, g1b), (c_ref, w2, g2a, g2b)):
            gate = jax.nn.sigmoid(jnp.concatenate([ga[...], gb[...]], axis=-1).astype(f32))
            acc = acc + gate * _raw_dot(x_n[...], w_ref[...], "nn")
        merged = acc.astype(bf16)
        m_ref[...] = merged
        x1 = x_ref[...] + _raw_dot(merged, wo_ref[...], "nn")
        x1_ref[...] = x1
        y = x1 * lax.rsqrt(jnp.mean(x1 * x1, axis=-1, keepdims=True) + EPS) * g_ref[...]
        h_ref[...] = y.astype(bf16)
        ht_ref[...] = y.T.astype(bf16)

    br = pl.BlockSpec((tm, 512), lambda i: (i, 0))
    w = pl.BlockSpec((512, D_MODEL), lambda i: (0, 0))
    row = pl.BlockSpec((tm, D_MODEL), lambda i: (i, 0))
    return _call(body, "merge_out_norm_fwd", (T // tm,),
                 [br, br, br, w, w, w, *_gate_specs(tm), pl.BlockSpec((D_MODEL, D_MODEL), lambda i: (0, 0)), row,
                  pl.BlockSpec((1, D_MODEL), lambda i: (0, 0))],
                 (row, row, row, pl.BlockSpec((D_MODEL, tm), lambda i: (0, i))),
                 (jax.ShapeDtypeStruct((T, D_MODEL), bf16), jax.ShapeDtypeStruct((T, D_MODEL), f32),
                  jax.ShapeDtypeStruct((T, D_MODEL), bf16), jax.ShapeDtypeStruct((D_MODEL, T), bf16)),
                 sem=("parallel",))(*branches, *wb, *[proj] * 6, w_out, x, g)


def _merge_bwd(branches, wb, proj, merged, dx1, w_out):
    T = proj.shape[0]
    tm = _pick(T, (256, 128))

    def body(a_ref, b_ref, c_ref, w0, w1, w2, g0a, g0b, g1a, g1b, g2a, g2b, m_ref, dx_ref, wo_ref, dgl_ref, d0, d1, d2, gw_ref, gwo_ref):
        @pl.when(pl.program_id(0) == 0)
        def _():
            gw_ref[...] = jnp.zeros_like(gw_ref)
            gwo_ref[...] = jnp.zeros_like(gwo_ref)

        dx = dx_ref[...].astype(bf16)
        gwo_ref[...] += _raw_dot(m_ref[...], dx, "tn")
        dm = _raw_dot(dx, wo_ref[...], "nt")
        for n, (x_ref, w_ref, ga, gb, d_ref) in enumerate(((a_ref, w0, g0a, g0b, d0), (b_ref, w1, g1a, g1b, d1), (c_ref, w2, g2a, g2b, d2))):
            x, w = x_ref[...], w_ref[...]
            up = _raw_dot(x, w, "nn")
            sg = jax.nn.sigmoid(jnp.concatenate([ga[...], gb[...]], axis=-1).astype(f32))
            dgl_ref[n] = (dm * up * sg * (1.0 - sg)).astype(bf16)
            dup = (dm * sg).astype(bf16)
            d_ref[...] = _raw_dot(dup, w, "nt").astype(bf16)
            gw_ref[n] += _raw_dot(x, dup, "tn")

    br = pl.BlockSpec((tm, 512), lambda i: (i, 0))
    w = pl.BlockSpec((512, D_MODEL), lambda i: (0, 0))
    sh = jax.ShapeDtypeStruct((T, 512), bf16)
    row = pl.BlockSpec((tm, D_MODEL), lambda i: (i, 0))
    square = pl.BlockSpec((D_MODEL, D_MODEL), lambda i: (0, 0))
    outs = _call(body, "merge_bwd", (T // tm,), [br, br, br, w, w, w, *_gate_specs(tm), row, row, square],
                 (pl.BlockSpec((3, tm, D_MODEL), lambda i: (0, i, 0)), br, br, br, pl.BlockSpec((3, 512, D_MODEL), lambda i: (0, 0, 0)), square),
                 (jax.ShapeDtypeStruct((3, T, D_MODEL), bf16), sh, sh, sh, jax.ShapeDtypeStruct((3, 512, D_MODEL), f32),
                  jax.ShapeDtypeStruct((D_MODEL, D_MODEL), f32)),
                 sem=("arbitrary",))(*branches, *wb, *[proj] * 6, merged, dx1, w_out)
    return outs[0], outs[1:4], outs[4], outs[5]


CONV_TC = 256


def _shift_down(a, k):
    r = pltpu.roll(a, k, 0)
    row = lax.broadcasted_iota(jnp.int32, (8, a.shape[1]), 0)
    return jnp.concatenate([jnp.where(row >= k, r[:8], 0.0), r[8:]], axis=0)


def _shift_up(a, k):
    n = a.shape[0]
    r = pltpu.roll(a, n - k, 0)
    row = lax.broadcasted_iota(jnp.int32, (8, a.shape[1]), 0)
    return jnp.concatenate([r[:n - 8], jnp.where(row < 8 - k, r[n - 8:], 0.0)], axis=0)


def _conv_pre(a, a1, a2, cw, cb):
    return cb + cw[0:1] * a2 + cw[1:2] * a1 + cw[2:3] * a


def _up_conv_fwd(h2, w_up, cw, cb):
    Bl, S, Dd = h2.shape
    nc = D_FF // CONV_TC

    def body(h_ref, wa_ref, wb_ref, cw_ref, cb_ref, a_ref, b_ref, o_ref):
        a16 = _raw_dot(h_ref[0], wa_ref[...], "nn").astype(bf16)
        b16 = _raw_dot(h_ref[0], wb_ref[...], "nn").astype(bf16)
        a_ref[0], b_ref[0] = a16, b16
        a = a16.astype(f32)
        ac = _conv_pre(a, _shift_down(a, 1), _shift_down(a, 2), cw_ref[...], cb_ref[...])
        o_ref[0] = (jax.nn.silu(ac) * b16.astype(f32)).astype(bf16)

    seq = pl.BlockSpec((1, S, CONV_TC), lambda b, c: (b, 0, c))
    sh = jax.ShapeDtypeStruct((Bl, S, D_FF), bf16)
    return _call(body, "up_conv_fwd", (Bl, nc),
                 [pl.BlockSpec((1, S, Dd), lambda b, c: (b, 0, 0)), pl.BlockSpec((Dd, CONV_TC), lambda b, c: (0, c)),
                  pl.BlockSpec((Dd, CONV_TC), lambda b, c: (0, nc + c)), pl.BlockSpec((3, CONV_TC), lambda b, c: (0, c)),
                  pl.BlockSpec((1, CONV_TC), lambda b, c: (0, c))],
                 (seq, seq, seq), (sh, sh, sh), sem=("parallel", "parallel"))(h2, w_up, w_up, cw, cb)


def _down_conv_bwd(dx2, w_down, a, b, cw, cb):
    Bl, S, Dd = dx2.shape
    nc = D_FF // CONV_TC

    def body(dx_ref, wd_ref, a_ref, b_ref, cw_ref, cb_ref, da_ref, db_ref, dcw_ref, dcb_ref):
        dact = _raw_dot(dx_ref[0], wd_ref[...], "nt").astype(bf16).astype(f32)
        a, cw = a_ref[0].astype(f32), cw_ref[...]
        a1, a2 = _shift_down(a, 1), _shift_down(a, 2)
        ac = _conv_pre(a, a1, a2, cw, cb_ref[...])
        sg = jax.nn.sigmoid(ac)
        gated = dact * sg
        db_ref[0] = (gated * ac).astype(bf16)
        dac = gated * b_ref[0].astype(f32) * (1.0 + ac * (1.0 - sg))
        da_ref[0] = (cw[2:3] * dac + cw[1:2] * _shift_up(dac, 1) + cw[0:1] * _shift_up(dac, 2)).astype(bf16)
        dcw_ref[0, 0:1, :] = jnp.sum(dac * a2, axis=0, keepdims=True)
        dcw_ref[0, 1:2, :] = jnp.sum(dac * a1, axis=0, keepdims=True)
        dcw_ref[0, 2:3, :] = jnp.sum(dac * a, axis=0, keepdims=True)
        dcb_ref[0] = jnp.sum(dac, axis=0, keepdims=True)

    seq = pl.BlockSpec((1, S, CONV_TC), lambda b_, c: (b_, 0, c))
    sh = jax.ShapeDtypeStruct((Bl, S, D_FF), bf16)
    return _call(body, "down_conv_bwd", (Bl, nc),
                 [pl.BlockSpec((1, S, Dd), lambda b_, c: (b_, 0, 0)), pl.BlockSpec((CONV_TC, Dd), lambda b_, c: (c, 0)), seq, seq,
                  pl.BlockSpec((3, CONV_TC), lambda b_, c: (0, c)), pl.BlockSpec((1, CONV_TC), lambda b_, c: (0, c))],
                 (seq, seq, pl.BlockSpec((1, 3, CONV_TC), lambda b_, c: (b_, 0, c)), pl.BlockSpec((1, 1, CONV_TC), lambda b_, c: (b_, 0, c))),
                 (sh, sh, jax.ShapeDtypeStruct((Bl, 3, D_FF), f32), jax.ShapeDtypeStruct((Bl, 1, D_FF), f32)),
                 sem=("parallel", "parallel"))(dx2, w_down, a, b, cw, cb)


def _local_step(x, mem, target, p, w_in, late_b, late_c, send, settle):
    Bl, S, Dd = x.shape
    T = Bl * S
    x2d, t2d, mem2d = x.reshape(T, Dd), target.reshape(T, Dd), mem.reshape(Bl * MEM_LEN, Dd)
    b_st = jnp.pad(p["b_spatial"].T, ((0, 0), (0, 128 - N_HEAD)))
    lbl = p["lb_logits"]

    h, h_t = _rms_fwd(x2d, p["norm1_g"], "norm1_fwd", transposed=True)
    proj = _mm(h, w_in, "nn", bf16, "proj_fwd", 1024, 1664)
    a_out = _gmlp_fwd(proj, p["ln_v_g"], p["ln_v_b"], p["w_spatial"], b_st)
    proj3 = proj.reshape(Bl, S, IN_WIDTH)
    b_out, states = _hgrn_fwd(proj3, lbl, p["hgrn_norm_g"], Bl, S)
    b_out = b_out.reshape(T, 512)
    memn = _rms_fwd(mem2d, p["mem_norm_g"], "memnorm_fwd")
    w = late_b(b_out)
    wb = w["w_branch"]
    kv = _mm(memn, w["w_mem_kv"], "nn", f32, "kv_fwd", 512, 1024).reshape(Bl, MEM_LEN, 2 * 512)
    c_out = _attn_fwd(proj, kv, Bl, S)
    branches = (a_out, b_out, c_out)
    merged, x1, h2, h2_t = _merge_out_norm_fwd(branches, wb, proj, w["w_out"], x2d, p["norm2_g"])
    w.update(late_c(h2))
    ffn_a, ffn_b, act = _up_conv_fwd(h2.reshape(Bl, S, Dd), w["w_up"], w["conv_w"], p["conv_b"])
    act = act.reshape(T, D_FF)
    loss_part, dx2, dx2_16, g_final = _down_final_loss(act, w["w_down"], x1, p["final_g"], t2d)

    g_w_down = _mm(act, dx2_16, "tn", bf16, "down_dw", 1408, 1024, 1024)
    da, db, g_conv_w, g_conv_b = _down_conv_bwd(dx2_16.reshape(Bl, S, Dd), w["w_down"], ffn_a, ffn_b, w["conv_w"], p["conv_b"])
    da, db = da.reshape(T, D_FF), db.reshape(T, D_FF)
    shard = 2 * D_FF // N_DEV
    g_w_up = _mm(h2_t, da, "nn", bf16, "up_dw_a", 512, 1408, shard=shard, n_outer=True,
                 into=(lax.empty((N_DEV, D_MODEL, shard), bf16), 0))
    g_w_up = _mm(h2_t, db, "nn", bf16, "up_dw_b", 512, 1408, shard=shard, n_outer=True, into=(g_w_up, N_DEV // 2))
    send("c", dict(w_up=g_w_up, conv_w=jnp.sum(g_conv_w, axis=0), w_down=g_w_down))
    dx1, g_norm2 = _mm_rms_bwd([(da, w["w_up"], 0), (db, w["w_up"], 1)], x1, p["norm2_g"], dx2, "up_dx_norm2_bwd", 256)

    dgl, dbr, g_w_branch, g_w_out = _merge_bwd(branches, wb, proj, merged, dx1, w["w_out"])
    dxq, dkv = _attn_bwd(proj, kv, dbr[2], Bl, S)
    dkv = dkv.reshape(Bl * MEM_LEN, 2 * 512)
    g_w_kv = _mm(memn, dkv, "tn", bf16, "kv_dw", 1024, 1024, 512)
    send("b", dict(w_mem_kv=g_w_kv, w_branch=g_w_branch, w_out=g_w_out))
    dmemn = _mm(dkv, w["w_mem_kv"], "nt", f32, "kv_dx", 512, 1024)
    _, g_mem_norm = _rms_bwd(mem2d, p["mem_norm_g"], dmemn, "memnorm_bwd")
    dzuv, g_ln_g, g_ln_b, g_w_sp, g_b_sp = _gmlp_bwd(proj, p["ln_v_g"], p["ln_v_b"], p["w_spatial"], b_st, dbr[0])
    *dqfig, g_lbl, g_ng = _hgrn_bwd(proj3, lbl, p["hgrn_norm_g"], states, dbr[1].reshape(Bl, S, 512), Bl, S)
    dq, df, di, dg = [d.reshape(T, 512) for d in dqfig]
    dproj = jnp.concatenate([dzuv, dq, df, di, dg, dxq, dgl[0], dgl[1], dgl[2]], axis=-1)
    g_w_in = _mm(h_t, dproj, "nn", bf16, "proj_dw", 512, 1664, shard=IN_WIDTH // N_DEV, n_outer=True)
    send("a", dict(w_in=g_w_in))
    dx, g_norm1 = _mm_rms_bwd([(settle(dproj), w_in, 0)], x2d, p["norm1_g"], dx1, "proj_dx_norm1_bwd", 256)

    gs = dict(w_spatial=g_w_sp, norm1_g=g_norm1, mem_norm_g=g_mem_norm, norm2_g=g_norm2, final_g=g_final, lb_logits=g_lbl,
              ln_v_g=g_ln_g, ln_v_b=g_ln_b, b_spatial=g_b_sp, hgrn_norm_g=g_ng, conv_b=g_conv_b)
    return loss_part, dx.reshape(Bl, S, Dd), gs


def _coords():
    return lax.axis_index("x"), lax.axis_index("y"), lax.axis_index("c")


def _slot(dev):
    return 4 * dev[0] + 2 * dev[1] + dev[2]


def _comm_call(body, name, arrays, out_shapes, n_sem):
    n = len(arrays)
    hbm = pl.BlockSpec(memory_space=pl.ANY)
    return pl.pallas_call(
        body, name=name, out_shape=out_shapes, in_specs=[hbm] * n, out_specs=[hbm] * n,
        scratch_shapes=[pltpu.SemaphoreType.DMA((n_sem, n)), pltpu.SemaphoreType.DMA((n_sem, n)), pltpu.SemaphoreType.DMA((n,))])(*arrays)


def _all_gather(blocks, name):
    n = len(blocks)

    def body(*refs):
        x_refs, o_refs, (send_sems, recv_sems, local_sems) = refs[:n], refs[n:2 * n], refs[2 * n:]
        x, y, c = _coords()
        me, sibling = (x, y, c), (x, y, 1 - c)
        chips = [(1 - x, y), (x, 1 - y), (1 - x, 1 - y)]

        def copy(a, k, block_dev, to, from_input=False):
            dst = o_refs[a].at[_slot(block_dev)]
            return pltpu.make_async_remote_copy(src_ref=x_refs[a] if from_input else dst, dst_ref=dst, send_sem=send_sems.at[k, a],
                                                recv_sem=recv_sems.at[k, a], device_id=to, device_id_type=MESH)

        mine = [pltpu.make_async_copy(x_refs[a], o_refs[a].at[_slot(me)], local_sems.at[a]) for a in range(n)]
        first = [copy(a, 0, me, sibling, True) for a in range(n)]
        first += [copy(a, 1 + j, me, (*chip, c), True) for j, chip in enumerate(chips) for a in range(n)]
        for cp in mine + first:
            cp.start()
        passed = []
        for j, chip in enumerate(chips):
            for a in range(n):
                copy(a, 1 + j, (*chip, c), me).wait_recv()
                fwd = copy(a, 4 + j, (*chip, c), sibling)
                fwd.start()
                passed.append(fwd)
        for a in range(n):
            copy(a, 0, sibling, me).wait_recv()
        for j, chip in enumerate(chips):
            for a in range(n):
                copy(a, 4 + j, (*chip, 1 - c), me).wait_recv()
        for cp in first + passed:
            cp.wait_send()
        for cp in mine:
            cp.wait()

    return _comm_call(body, name, blocks, [jax.ShapeDtypeStruct((N_DEV,) + b.shape, b.dtype) for b in blocks], 7)


_REL = [(0, 0, 1), (0, 1, 0), (0, 1, 1), (1, 0, 0), (1, 0, 1), (1, 1, 0), (1, 1, 1)]


def _seq_exchange(arrays, gather, name, collective_id):
    n = len(arrays)
    hbm = pltpu.MemorySpace.HBM
    srcs = [jax.new_ref(a, memory_space=hbm) for a in arrays]
    lands = [jax.empty_ref(jax.ShapeDtypeStruct(((N_DEV,) + a.shape) if gather else a.shape, a.dtype), memory_space=hbm) for a in arrays]

    @pl.kernel(mesh=plsc.ScalarSubcoreMesh(axis_name="sequencer", num_cores=1), name=name,
               scratch_types=(pltpu.SemaphoreType.DMA((7, n)), pltpu.SemaphoreType.DMA((7, n)), pltpu.SemaphoreType.DMA((n,))),
               compiler_params=pltpu.CompilerParams(collective_id=collective_id))
    def launch(send, recv, local):
        x, y, c = _coords()
        me = (x, y, c)
        peers = [(x ^ dx, y ^ dy, c ^ dc) for dx, dy, dc in _REL]
        barrier = pltpu.get_barrier_semaphore()
        for peer in peers:
            pl.semaphore_signal(barrier, inc=1, device_id=peer, device_id_type=MESH)
        pl.semaphore_wait(barrier, len(peers))

        def copy(a, k, peer, arrival):
            return pltpu.make_async_remote_copy(
                src_ref=srcs[a] if gather else srcs[a].at[_slot(peer)], dst_ref=lands[a].at[_slot(peer if arrival else me)],
                send_sem=send.at[k, a], recv_sem=recv.at[k, a], device_id=peer, device_id_type=MESH)

        mine = [pltpu.make_async_copy(srcs[a] if gather else srcs[a].at[_slot(me)], lands[a].at[_slot(me)], local.at[a])
                for a in range(n)]
        out = [copy(a, k, peer, False) for a in range(n) for k, peer in enumerate(peers)]
        for cp in mine + out:
            cp.start()
        for a in range(n):
            for k, peer in enumerate(peers):
                copy(a, k, peer, True).wait_recv()
        for cp in out:
            cp.wait_send()
        for cp in mine:
            cp.wait()

    launch()
    return [land[...] for land in lands]


def _adam_math(w, g, m, v):
    m_ = ADAM_B1 * m + (1.0 - ADAM_B1) * g
    v_ = ADAM_B2 * v + (1.0 - ADAM_B2) * jnp.square(g)
    m_hat = m_ / (1.0 - ADAM_B1 ** ADAM_STEP)
    v_hat = v_ / (1.0 - ADAM_B2 ** ADAM_STEP)
    return -ADAM_LR * (m_hat / (jnp.sqrt(v_hat) + ADAM_EPS) + ADAM_WD * w), m_, v_


def _reduce_adamw(parts, w, m, v, name):
    _, R, L = parts.shape
    tr = _pick(R, (256, 128, 64, 32, 16, 8))

    def body(p_ref, w_ref, m_ref, v_ref, g_ref, d_ref, nm_ref, nv_ref):
        g = p_ref[0].astype(f32)
        for i in range(1, N_DEV):
            g = g + p_ref[i].astype(f32)
        g_ref[...] = g
        d_ref[...], nm_ref[...], nv_ref[...] = _adam_math(w_ref[...], g, m_ref[...], v_ref[...])

    blk = pl.BlockSpec((tr, L), lambda i: (i, 0))
    sh = jax.ShapeDtypeStruct((R, L), f32)
    return _call(body, name, (R // tr,), [pl.BlockSpec((N_DEV, tr, L), lambda i: (0, i, 0)), blk, blk, blk], (blk,) * 4, (sh,) * 4,
                 sem=("parallel",))(parts, w, m, v)


SMALL = (("w_spatial", (512, 128), 0), ("norm1_g", (1, 1024), 512), ("mem_norm_g", (1, 1024), 520), ("norm2_g", (1, 1024), 528),
         ("final_g", (1, 1024), 536), ("lb_logits", (2, 512), 544), ("ln_v_g", (1, 512), 552), ("ln_v_b", (1, 512), 556),
         ("b_spatial", (4, 128), 560), ("hgrn_norm_g", (1, 128), 564), ("conv_b", (1, 2816), 565))
LOSS_ROW, SMALL_USED, SMALL_ROWS = 587, 588, 640


def _segments(shape, base):
    r, n = shape
    per = n // 128
    return [(base + i * per + j, i, slice(j * 128, (j + 1) * 128)) for i in range(r) for j in range(per)]


def _pack_small(gs, loss_part):
    names = [n for n, _, _ in SMALL]

    def body(*refs):
        src, loss_ref, o_ref = dict(zip(names, refs[:-2])), refs[-2], refs[-1]
        o_ref[SMALL_USED:SMALL_ROWS, :] = jnp.zeros((SMALL_ROWS - SMALL_USED, 128), f32)
        o_ref[LOSS_ROW:LOSS_ROW + 1, :] = loss_ref[...]
        for name, shape, base in SMALL:
            ref = src[name]
            if name == "w_spatial":
                o_ref[base:base + 512, :] = ref[...].reshape(512, 128)
            elif name == "b_spatial":
                o_ref[base:base + 4, :] = ref[0:4, :]
            elif name == "conv_b":
                per_example = functools.reduce(lambda u, v_: u + v_, [ref[b] for b in range(ref.shape[0])])
                for row, i, sl in _segments(shape, base):
                    o_ref[row:row + 1, :] = per_example[i:i + 1, sl]
            elif name == "hgrn_norm_g":
                per_head = [ref[b, h] for b in range(ref.shape[0]) for h in range(N_HEAD)]
                o_ref[base:base + 1, :] = functools.reduce(lambda u, v_: u + v_, per_head)
            else:
                for row, i, sl in _segments(shape, base):
                    o_ref[row:row + 1, :] = ref[i:i + 1, sl]

    return pl.pallas_call(body, name="pack_small", out_shape=jax.ShapeDtypeStruct((SMALL_ROWS, 128), f32))(
        *[gs[n] for n in names], loss_part)


def _small_update(gathered, w, m, v):
    names = [n for n, _, _ in SMALL]
    k = len(names)

    def body(*refs):
        p_ref = refs[0]
        ins = [dict(zip(names, refs[1 + i * k:1 + (i + 1) * k])) for i in range(3)]
        outs = [dict(zip(names, refs[1 + (3 + i) * k:1 + (4 + i) * k])) for i in range(4)]
        loss_ref, gsum = refs[-2], refs[-1]
        g = p_ref[0]
        for i in range(1, N_DEV):
            g = g + p_ref[i]
        gsum[...] = g
        loss_ref[...] = gsum[LOSS_ROW:LOSS_ROW + 1, :]
        for name, shape, base in SMALL:
            if name == "w_spatial":
                where = [(slice(base, base + 512), (slice(None), slice(None)))]
            else:
                where = [(slice(row, row + 1), (slice(i, i + 1), sl)) for row, i, sl in _segments(shape, base)]
            for rows, at in where:
                g_ = gsum[rows, :]
                d_, m_, v_ = _adam_math(ins[0][name][at], g_, ins[1][name][at], ins[2][name][at])
                for o, val in zip(outs, (g_, d_, m_, v_)):
                    o[name][at] = val

    args = [gathered] + [d[n] for d in (w, m, v) for n in names]
    out_shapes = [jax.ShapeDtypeStruct(shape, f32) for _ in range(4) for _, shape, _ in SMALL] + [jax.ShapeDtypeStruct((1, 128), f32)]
    outs = pl.pallas_call(body, name="small_update", out_shape=out_shapes, scratch_shapes=[pltpu.VMEM((SMALL_ROWS, 128), f32)])(*args)
    return [dict(zip(names, outs[i * k:(i + 1) * k])) for i in range(4)], outs[-1]


def _cols_full(g):
    return jnp.moveaxis(g, 0, -2).reshape(g.shape[1:-1] + (N_DEV * g.shape[-1],))


def _join_cols(g, name):
    _, R, n = g.shape
    tr = _pick(R, (256, 128))

    def body(g_ref, o_ref):
        for j in range(N_DEV):
            o_ref[:, j * n:(j + 1) * n] = g_ref[j]

    return _call(body, name, (R // tr,), [pl.BlockSpec((N_DEV, tr, n), lambda i: (0, i, 0))],
                 pl.BlockSpec((tr, N_DEV * n), lambda i: (i, 0)), jax.ShapeDtypeStruct((R, N_DEV * n), g.dtype), sem=("parallel",))(g)


def _cols_parts(full):
    n = full.shape[-1] // N_DEV
    return jnp.moveaxis(full.reshape(full.shape[:-1] + (N_DEV, n)), -2, 0)


def kernel(x, mem, norm1_g, w_in, ln_v_g, ln_v_b, w_spatial, b_spatial, lb_logits, hgrn_norm_g, mem_norm_g, w_mem_kv, w_branch, w_out, norm2_g, w_up, conv_w, conv_b, w_down, final_g, loss_target, m_norm1_g, m_w_in, m_ln_v_g, m_ln_v_b, m_w_spatial, m_b_spatial, m_lb_logits, m_hgrn_norm_g, m_mem_norm_g, m_w_mem_kv, m_w_branch, m_w_out, m_norm2_g, m_w_up, m_conv_w, m_conv_b, m_w_down, m_final_g, v_norm1_g, v_w_in, v_ln_v_g, v_ln_v_b, v_w_spatial, v_b_spatial, v_lb_logits, v_hgrn_norm_g, v_mem_norm_g, v_w_mem_kv, v_w_branch, v_w_out, v_norm2_g, v_w_up, v_conv_w, v_conv_b, v_w_down, v_final_g):
    given = dict(locals())
    order = ("norm1_g", "w_in", "ln_v_g", "ln_v_b", "w_spatial", "b_spatial", "lb_logits", "hgrn_norm_g", "mem_norm_g",
             "w_mem_kv", "w_branch", "w_out", "norm2_g", "w_up", "conv_w", "conv_b", "w_down", "final_g")
    groups = dict(a=("w_in",), b=("w_mem_kv", "w_branch", "w_out"), c=("w_up", "conv_w", "w_down"))

    wire = {n: given[n][0].astype(f32 if n == "conv_w" else bf16) for ns in groups.values() for n in ns}
    g_in = _all_gather([wire["w_in"]], "gather_w_in")[0]
    w_in_full = _join_cols(g_in, "join_w_in")
    w_in_full, wire_b, wire_c = lax.optimization_barrier((w_in_full, [wire[n] for n in groups["b"]], [wire[n] for n in groups["c"]]))
    rest_b = _seq_exchange(wire_b, True, "gather_b", 1)
    rest_c = _seq_exchange(wire_c, True, "gather_c", 6)

    def late_b(after):
        _, (kv_, br_, out_) = lax.optimization_barrier((after, tuple(rest_b)))
        br_ = _cols_full(br_)
        return dict(w_mem_kv=kv_.reshape(D_MODEL, 2 * 512), w_branch=[br_[n] for n in range(3)], w_out=out_.reshape(D_MODEL, D_MODEL))

    def late_c(after):
        _, (up_, cw_, down_) = lax.optimization_barrier((after, tuple(rest_c)))
        up_ = _join_cols(up_, "join_w_up")
        return dict(w_up=up_, conv_w=_cols_full(cw_), w_down=down_.reshape(D_FF, D_MODEL))

    to_parts = dict(w_in=lambda g_: g_, w_up=lambda g_: g_, conv_w=_cols_parts,
                    w_branch=lambda g_: _cols_parts(g_.astype(bf16)).reshape(N_DEV, -1, 128),
                    w_mem_kv=lambda g_: g_.reshape(N_DEV, -1, 2 * 512), w_out=lambda g_: g_.astype(bf16).reshape(N_DEV, -1, D_MODEL),
                    w_down=lambda g_: g_.reshape(N_DEV, -1, D_MODEL))
    scatters = {}

    def send(tag, grads_):
        parts = [to_parts[n](grads_[n]) for n in groups[tag]]
        scatters[tag] = _seq_exchange(parts, False, f"scatter_{tag}", dict(a=2, b=4, c=5)[tag])

    small_2d = lambda prefix: {n: given[prefix + n].reshape(shape) for n, shape, _ in SMALL}
    p = small_2d("")
    p["w_spatial"] = w_spatial[0]
    updates = {}

    def update(tag):
        for n, parts in zip(groups[tag], scatters[tag]):
            two_d = (-1, given[n].shape[-1])
            updates[n] = _reduce_adamw(parts, *[given[pre + n].reshape(two_d) for pre in ("", "m_", "v_")], "adamw_" + n)

    def settle(chain):
        update("c")
        update("b")
        early = groups["c"] + groups["b"]
        chain, tied = lax.optimization_barrier((chain, [updates[n] for n in early]))
        updates.update(zip(early, tied))
        return chain

    loss_part, grad_x, gs = _local_step(x, mem, loss_target, p, w_in_full, late_b, late_c, send, settle)

    gathered = _seq_exchange([_pack_small(gs, loss_part)], True, "gather_small", 3)[0]

    update("a")
    grads, delta, new_m, new_v = {}, {}, {}, {}
    for n, res in updates.items():
        grads[n], delta[n], new_m[n], new_v[n] = [r.reshape(given[n].shape) for r in res]

    small_results, loss_row = _small_update(gathered, small_2d(""), small_2d("m_"), small_2d("v_"))
    for dst, res in zip((grads, delta, new_m, new_v), small_results):
        for n, _, _ in SMALL:
            dst[n] = res[n].reshape(given[n].shape)
    loss = loss_row[0, 0]

    return (loss, grad_x, *[grads[n] for n in order], *[delta[n] for n in order], *[new_m[n] for n in order],
            *[new_v[n] for n in order])
```

```python
import functools

import jax
import jax.numpy as jnp
from jax import lax
from jax.experimental import pallas as pl
from jax.experimental.pallas import tpu as pltpu
from jax.experimental.pallas import tpu_sc as plsc

f32 = jnp.float32
bf16 = jnp.bfloat16

N_DEV = 8
D_MODEL = 1024
EPS = 1e-6
GM_CHUNK = 128
HG_CHUNK = 64
HEAD = 128
N_HEAD = 4
MEM_LEN = 256
D_FF = 2816
IN_WIDTH = 6656
C_ZU, C_HQ, C_HF, C_HI, C_HG, C_XQ, C_GL = 0, 1024, 1536, 2048, 2560, 3072, 3584
ADAM_LR, ADAM_B1, ADAM_B2, ADAM_EPS, ADAM_WD, ADAM_STEP = 0.001, 0.9, 0.999, 1e-08, 0.01, 10
VMEM_LIMIT = 56 * 1024 * 1024
MESH = pl.DeviceIdType.MESH


def _pick(n, cands):
    for c in cands:
        if n % c == 0:
            return c
    return n


def _call(body, name, grid, in_specs, out_specs, out_shape, scratch=(), sem=None, **cp):
    params = dict(vmem_limit_bytes=VMEM_LIMIT, **cp)
    if sem is not None:
        params["dimension_semantics"] = sem
    return pl.pallas_call(
        body, name=name, grid=grid, in_specs=in_specs, out_specs=out_specs, out_shape=out_shape,
        scratch_shapes=list(scratch), compiler_params=pltpu.CompilerParams(**params))


_DN = {"nn": (((1,), (0,)), ((), ())), "nt": (((1,), (1,)), ((), ())), "tn": (((0,), (0,)), ((), ()))}


def _raw_dot(a, b, mode):
    return lax.dot_general(a.astype(bf16), b.astype(bf16), _DN[mode], preferred_element_type=f32)


@jax.custom_vjp
def _dot_nn(a, b):
    return _raw_dot(a, b, "nn")


_dot_nn.defvjp(lambda a, b: (_raw_dot(a, b, "nn"), (a, b)),
               lambda r, g: (_raw_dot(g, r[1], "nt"), _raw_dot(r[0], g, "tn")))


@jax.custom_vjp
def _dot_nt(a, b):
    return _raw_dot(a, b, "nt")


_dot_nt.defvjp(lambda a, b: (_raw_dot(a, b, "nt"), (a, b)),
               lambda r, g: (_raw_dot(g, r[1], "nn"), _raw_dot(g, r[0], "tn")))


@jax.custom_vjp
def _dot_tn(a, b):
    return _raw_dot(a, b, "tn")


_dot_tn.defvjp(lambda a, b: (_raw_dot(a, b, "tn"), (a, b)),
               lambda r, g: (_raw_dot(r[1], g, "nt"), _raw_dot(r[0], g, "nn")))


def _tri(n, lower):
    r = lax.broadcasted_iota(jnp.int32, (n, n), 0)
    c = lax.broadcasted_iota(jnp.int32, (n, n), 1)
    return ((c <= r) if lower else (c >= r)).astype(f32)


def _sel_dot(sel, x, mode, x_first=False, pieces=3):
    sel = sel.astype(bf16)
    out, rest = None, x
    for p in range(pieces):
        piece = rest.astype(bf16)
        part = lax.dot_general(*((piece, sel) if x_first else (sel, piece)), _DN[mode], preferred_element_type=f32)
        out = part if out is None else out + part
        if p + 1 < pieces:
            rest = rest - piece.astype(f32)
    return out


def _egrad(fn, x, ct):
    return jax.vjp(fn, x)[1](ct)[0]


def _mm(a, b, mode, out_dtype, name, tm, tn, tk=None, residual=None, shard=None, n_outer=False, into=None):
    if mode == "nn":
        (M, K), (_, N) = a.shape, b.shape
    elif mode == "nt":
        (M, K), (N, _) = a.shape, b.shape
    else:
        (K, M), (_, N) = a.shape, b.shape
    tm, tn = min(tm, M), min(tn, N)
    tk = K if tk is None else min(tk, K)
    assert M % tm == 0 and N % tn == 0 and K % tk == 0, (name, M, N, K, tm, tn, tk)
    nk = K // tk

    def body(*refs):
        acc_ref = refs[-1] if nk > 1 else None
        refs = refs[:-1] if nk > 1 else refs
        if residual is None:
            a_ref, b_ref, *_, o_ref = refs
        else:
            a_ref, b_ref, r_ref, o_ref = refs

        def finish(r):
            if residual is not None:
                r = r + r_ref[...]
            if shard is None:
                o_ref[...] = r.astype(out_dtype)
            else:
                for s in range(tn // shard):
                    o_ref[s] = r[:, s * shard:(s + 1) * shard].astype(out_dtype)

        part = _raw_dot(a_ref[...], b_ref[...], mode)
        if nk == 1:
            finish(part)
            return
        k = pl.program_id(2)

        @pl.when(k == 0)
        def _():
            acc_ref[...] = part

        @pl.when((k > 0) & (k < nk - 1))
        def _():
            acc_ref[...] += part

        @pl.when(k == nk - 1)
        def _():
            finish(acc_ref[...] + part)

    def at(index):
        return (lambda j, i, k: index(i, j, k)) if n_outer else index

    a_spec = {"nn": pl.BlockSpec((tm, tk), at(lambda i, j, k: (i, k))),
              "nt": pl.BlockSpec((tm, tk), at(lambda i, j, k: (i, k))),
              "tn": pl.BlockSpec((tk, tm), at(lambda i, j, k: (k, i)))}[mode]
    b_spec = {"nn": pl.BlockSpec((tk, tn), at(lambda i, j, k: (k, j))),
              "nt": pl.BlockSpec((tn, tk), at(lambda i, j, k: (j, k))),
              "tn": pl.BlockSpec((tk, tn), at(lambda i, j, k: (k, j)))}[mode]
    o_spec = pl.BlockSpec((tm, tn), at(lambda i, j, k: (i, j)))
    in_specs = [a_spec, b_spec] + ([o_spec] if residual is not None else [])
    args = (a, b) + ((residual,) if residual is not None else ())
    out_shape = jax.ShapeDtypeStruct((M, N), out_dtype)
    extra = {}
    if shard is not None:
        assert residual is None and tn % shard == 0
        per = tn // shard
        first = 0 if into is None else into[1] // per
        o_spec = pl.BlockSpec((per, tm, shard), at(lambda i, j, k: (j + first, i, 0)))
        out_shape = jax.ShapeDtypeStruct((N // shard, M, shard), out_dtype)
        if into is not None:
            out_shape = jax.ShapeDtypeStruct(into[0].shape, out_dtype)
            in_specs, args = in_specs + [pl.BlockSpec(memory_space=pl.ANY)], args + (into[0],)
            extra = dict(input_output_aliases={2: 0})
    grid = (N // tn, M // tm, nk) if n_outer else (M // tm, N // tn, nk)
    return pl.pallas_call(
        body, name=name, grid=grid, in_specs=in_specs, out_specs=o_spec, out_shape=out_shape,
        scratch_shapes=[pltpu.VMEM((tm, tn), f32)] if nk > 1 else [],
        compiler_params=pltpu.CompilerParams(vmem_limit_bytes=VMEM_LIMIT, dimension_semantics=("parallel", "parallel", "arbitrary")),
        **extra)(*args)


def _rms_fwd(x, g, name, transposed=False):
    R, Dd = x.shape
    tr = _pick(R, (512, 256, 128))

    def body(x_ref, g_ref, o_ref, *t_ref):
        xf = x_ref[...]
        y = xf * lax.rsqrt(jnp.mean(xf * xf, axis=-1, keepdims=True) + EPS) * g_ref[...]
        o_ref[...] = y.astype(bf16)
        if transposed:
            t_ref[0][...] = y.T.astype(bf16)

    row = pl.BlockSpec((tr, Dd), lambda i: (i, 0))
    out_specs, out_shape = row, jax.ShapeDtypeStruct((R, Dd), bf16)
    if transposed:
        out_specs, out_shape = (row, pl.BlockSpec((Dd, tr), lambda i: (0, i))), (out_shape, jax.ShapeDtypeStruct((Dd, R), bf16))
    return _call(body, name, (R // tr,), [row, pl.BlockSpec((1, Dd), lambda i: (0, 0))], out_specs, out_shape, sem=("parallel",))(x, g)


def _rms_bwd(x, g, dh, name, residual=None):
    R, Dd = x.shape
    tr = _pick(R, (512, 256, 128))

    def body(*refs):
        if residual is None:
            x_ref, g_ref, dh_ref, dx_ref, dg_ref = refs
        else:
            x_ref, g_ref, dh_ref, r_ref, dx_ref, dg_ref = refs
        xf = x_ref[...]
        rs = lax.rsqrt(jnp.mean(xf * xf, axis=-1, keepdims=True) + EPS)
        y = xf * rs
        dh_ = dh_ref[...].astype(f32)
        dy = dh_ * g_ref[...]
        dx = rs * (dy - y * jnp.mean(dy * y, axis=-1, keepdims=True))
        if residual is not None:
            dx = dx + r_ref[...]
        dx_ref[...] = dx

        @pl.when(pl.program_id(0) == 0)
        def _():
            dg_ref[...] = jnp.zeros_like(dg_ref)

        dg_ref[...] += jnp.sum(dh_ * y, axis=0, keepdims=True)

    row = pl.BlockSpec((tr, Dd), lambda i: (i, 0))
    vec = pl.BlockSpec((1, Dd), lambda i: (0, 0))
    in_specs = [row, vec, row] + ([row] if residual is not None else [])
    args = (x, g, dh) + ((residual,) if residual is not None else ())
    return _call(body, name, (R // tr,), in_specs, (row, vec),
                 (jax.ShapeDtypeStruct((R, Dd), f32), jax.ShapeDtypeStruct((1, Dd), f32)), sem=("arbitrary",))(*args)


def _mm_rms_bwd(pairs, x, g, residual, name, tm):
    M = x.shape[0]
    Dd = x.shape[1]
    tm = min(tm, M)
    n = len(pairs)

    def body(*refs):
        ab_refs, (x_ref, g_ref, r_ref, dx_ref, dg_ref) = refs[:2 * n], refs[2 * n:]
        dh_ = _raw_dot(ab_refs[0][...], ab_refs[1][...], "nt")
        for k in range(1, n):
            dh_ = dh_ + _raw_dot(ab_refs[2 * k][...], ab_refs[2 * k + 1][...], "nt")
        xf = x_ref[...]
        rs = lax.rsqrt(jnp.mean(xf * xf, axis=-1, keepdims=True) + EPS)
        y = xf * rs
        dy = dh_ * g_ref[...]
        dx_ref[...] = rs * (dy - y * jnp.mean(dy * y, axis=-1, keepdims=True)) + r_ref[...]

        @pl.when(pl.program_id(0) == 0)
        def _():
            dg_ref[...] = jnp.zeros_like(dg_ref)

        dg_ref[...] += jnp.sum(dh_ * y, axis=0, keepdims=True)

    row = pl.BlockSpec((tm, Dd), lambda i: (i, 0))
    vec = pl.BlockSpec((1, Dd), lambda i: (0, 0))
    in_specs, args = [], []
    for a, b, k in pairs:
        in_specs += [pl.BlockSpec((tm, a.shape[1]), lambda i: (i, 0)),
                     pl.BlockSpec((b.shape[0], a.shape[1]), functools.partial(lambda i, k_: (0, k_), k_=k))]
        args += [a, b]
    in_specs += [row, vec, row]
    args += [x, g, residual]
    return _call(body, name, (M // tm,), in_specs, (row, vec),
                 (jax.ShapeDtypeStruct((M, Dd), f32), jax.ShapeDtypeStruct((1, Dd), f32)), sem=("arbitrary",))(*args)


def _down_final_loss(act, w_down, x1, g, target):
    R, Dd = x1.shape
    tr = _pick(R, (512, 256, 128))

    def body(a_ref, w_ref, x1_ref, g_ref, t_ref, loss_ref, dx_ref, dxb_ref, dg_ref):
        xf = _raw_dot(a_ref[...], w_ref[...], "nn") + x1_ref[...]
        rs = lax.rsqrt(jnp.mean(xf * xf, axis=-1, keepdims=True) + EPS)
        y = xf * rs
        err = y * g_ref[...] - t_ref[...]
        dh_ = err * (1.0 / Dd)
        dy = dh_ * g_ref[...]
        dx = rs * (dy - y * jnp.mean(dy * y, axis=-1, keepdims=True))
        dx_ref[...] = dx
        dxb_ref[...] = dx.astype(bf16)

        @pl.when(pl.program_id(0) == 0)
        def _():
            dg_ref[...] = jnp.zeros_like(dg_ref)
            loss_ref[...] = jnp.zeros_like(loss_ref)

        dg_ref[...] += jnp.sum(dh_ * y, axis=0, keepdims=True)
        part = jnp.sum(jnp.mean(err * err, axis=-1, keepdims=True), axis=0, keepdims=True)
        loss_ref[...] += 0.5 * part

    row = pl.BlockSpec((tr, Dd), lambda i: (i, 0))
    vec = pl.BlockSpec((1, Dd), lambda i: (0, 0))
    in_specs = [pl.BlockSpec((tr, act.shape[1]), lambda i: (i, 0)), pl.BlockSpec(w_down.shape, lambda i: (0, 0)), row, vec, row]
    return _call(body, "down_final_loss", (R // tr,), in_specs, (pl.BlockSpec((1, 128), lambda i: (0, 0)), row, row, vec),
                 (jax.ShapeDtypeStruct((1, 128), f32), jax.ShapeDtypeStruct((R, Dd), f32), jax.ShapeDtypeStruct((R, Dd), bf16),
                  jax.ShapeDtypeStruct((1, Dd), f32)), sem=("arbitrary",))(act, w_down, x1, g, target)


def _gmlp_parts(zuv, ln_g, ln_b):
    zu, zv = zuv[:, :512], zuv[:, 512:]
    u = jax.nn.gelu(zu)
    v = jax.nn.gelu(zv)
    mu = jnp.mean(v, axis=-1, keepdims=True)
    rs = lax.rsqrt(jnp.mean(jnp.square(v - mu), axis=-1, keepdims=True) + EPS)
    xh = (v - mu) * rs
    return zu, zv, u, xh, rs, xh * ln_g + ln_b


GM_TILE_CHUNKS = 4


def _gmlp_tile(T):
    n = _pick(T // GM_CHUNK, (GM_TILE_CHUNKS, 2, 1))
    return n, n * GM_CHUNK


def _gmlp_fwd(proj, ln_g, ln_b, w_s, b_st):
    T = proj.shape[0]
    nch, rows = _gmlp_tile(T)

    def body(p_ref, g_ref, b_ref, w_ref, bs_ref, o_ref):
        _, _, u, _, _, vn = _gmlp_parts(p_ref[...].astype(f32), g_ref[...], b_ref[...])
        causal = _tri(GM_CHUNK, True) > 0
        for gi in range(N_HEAD):
            sl = slice(gi * HEAD, (gi + 1) * HEAD)
            w = jnp.where(causal, w_ref[gi], 0.0)
            for ch in range(nch):
                rs_ = slice(ch * GM_CHUNK, (ch + 1) * GM_CHUNK)
                mixed = _raw_dot(w, vn[rs_, sl], "nn") + bs_ref[:, gi:gi + 1]
                o_ref[rs_, sl] = (u[rs_, sl] * mixed).astype(bf16)

    vec = pl.BlockSpec((1, 512), lambda i: (0, 0))
    return _call(body, "gmlp_fwd", (T // rows,),
                 [pl.BlockSpec((rows, 1024), lambda i: (i, 0)), vec, vec,
                  pl.BlockSpec((N_HEAD, GM_CHUNK, GM_CHUNK), lambda i: (0, 0, 0)), pl.BlockSpec((GM_CHUNK, 128), lambda i: (0, 0))],
                 pl.BlockSpec((rows, 512), lambda i: (i, 0)), jax.ShapeDtypeStruct((T, 512), bf16), sem=("parallel",))(
        proj, ln_g, ln_b, w_s, b_st)


def _gmlp_bwd(proj, ln_g, ln_b, w_s, b_st, da):
    T = proj.shape[0]
    nch, rows = _gmlp_tile(T)

    def body(p_ref, g_ref, b_ref, w_ref, bs_ref, da_ref, dp_ref, dg_ref, db_ref, dw_ref, dbs_ref):
        zu, zv, u, xh, rs, vn = _gmlp_parts(p_ref[...].astype(f32), g_ref[...], b_ref[...])
        causal = _tri(GM_CHUNK, True) > 0
        sub = lax.broadcasted_iota(jnp.int32, (8, GM_CHUNK), 0)
        ones = jnp.ones((8, HEAD), f32)
        dout = da_ref[...].astype(f32)

        @pl.when(pl.program_id(0) == 0)
        def _():
            for r in (dg_ref, db_ref, dw_ref, dbs_ref):
                r[...] = jnp.zeros_like(r)

        du, dvn, dbs = [], [], jnp.zeros((8, GM_CHUNK), f32)
        for gi in range(N_HEAD):
            sl = slice(gi * HEAD, (gi + 1) * HEAD)
            w = jnp.where(causal, w_ref[gi], 0.0)
            du_g, dvn_g, dw_g = [], [], jnp.zeros((GM_CHUNK, GM_CHUNK), f32)
            for ch in range(nch):
                rs_ = slice(ch * GM_CHUNK, (ch + 1) * GM_CHUNK)
                mixed = _raw_dot(w, vn[rs_, sl], "nn") + bs_ref[:, gi:gi + 1]
                du_g.append(dout[rs_, sl] * mixed)
                dm = dout[rs_, sl] * u[rs_, sl]
                dbs = dbs + jnp.where(sub == gi, _sel_dot(ones, dm, "nt"), 0.0)
                dw_g = dw_g + _raw_dot(dm, vn[rs_, sl], "nt")
                dvn_g.append(_raw_dot(w, dm, "tn"))
            dw_ref[gi] += jnp.where(causal, dw_g, 0.0)
            du.append(jnp.concatenate(du_g, axis=0))
            dvn.append(jnp.concatenate(dvn_g, axis=0))
        dbs_ref[...] += dbs
        du = jnp.concatenate(du, axis=-1)
        dvn = jnp.concatenate(dvn, axis=-1)
        dg_ref[...] += jnp.sum(dvn * xh, axis=0, keepdims=True)
        db_ref[...] += jnp.sum(dvn, axis=0, keepdims=True)
        dxh = dvn * g_ref[...]
        dv = rs * (dxh - jnp.mean(dxh, axis=-1, keepdims=True) - xh * jnp.mean(dxh * xh, axis=-1, keepdims=True))
        dp_ref[:, :512] = _egrad(jax.nn.gelu, zu, du).astype(bf16)
        dp_ref[:, 512:] = _egrad(jax.nn.gelu, zv, dv).astype(bf16)

    vec = pl.BlockSpec((1, 512), lambda i: (0, 0))
    wsp = pl.BlockSpec((N_HEAD, GM_CHUNK, GM_CHUNK), lambda i: (0, 0, 0))
    return _call(body, "gmlp_bwd", (T // rows,),
                 [pl.BlockSpec((rows, 1024), lambda i: (i, 0)), vec, vec, wsp, pl.BlockSpec((GM_CHUNK, 128), lambda i: (0, 0)),
                  pl.BlockSpec((rows, 512), lambda i: (i, 0))],
                 (pl.BlockSpec((rows, 1024), lambda i: (i, 0)), vec, vec, wsp, pl.BlockSpec((8, GM_CHUNK), lambda i: (0, 0))),
                 (jax.ShapeDtypeStruct((T, 1024), bf16), jax.ShapeDtypeStruct((1, 512), f32), jax.ShapeDtypeStruct((1, 512), f32),
                  jax.ShapeDtypeStruct((N_HEAD, GM_CHUNK, GM_CHUNK), f32), jax.ShapeDtypeStruct((8, GM_CHUNK), f32)),
                 sem=("arbitrary",))(proj, ln_g, ln_b, w_s, b_st, da)


HG_SUB = 8
HG_NSUB = HG_CHUNK // HG_SUB


def _two_level_matrix(transposed=False):
    shape = (HG_CHUNK, 2 * HG_CHUNK) if transposed else (2 * HG_CHUNK, HG_CHUNK)
    r = lax.broadcasted_iota(jnp.int32, shape, 1 if transposed else 0)
    c = lax.broadcasted_iota(jnp.int32, shape, 0 if transposed else 1)
    t = jnp.where(r < HG_CHUNK, r, r - HG_CHUNK)
    local = (r < HG_CHUNK) & (t // HG_SUB == c // HG_SUB) & (c <= t)
    before = (r >= HG_CHUNK) & (c < (t // HG_SUB) * HG_SUB)
    return (local | before).astype(f32)


def _two_level_sums(x):
    two = _sel_dot(_two_level_matrix(), x, "nn")
    return two[:HG_CHUNK], two[HG_CHUNK:]


@jax.custom_vjp
def _two_level_cumsum(x):
    return _two_level_sums(x)


_two_level_cumsum.defvjp(
    lambda x: (_two_level_sums(x), None),
    lambda _, g: (_sel_dot(_two_level_matrix(), jnp.concatenate(g, axis=0), "tn"),))


def _tile_matrix():
    s = lax.broadcasted_iota(jnp.int32, (HG_SUB, HG_CHUNK), 0)
    j = lax.broadcasted_iota(jnp.int32, (HG_SUB, HG_CHUNK), 1)
    return (j % HG_SUB == s).astype(f32)


@jax.custom_vjp
def _tile_lanes(x):
    return _sel_dot(_tile_matrix(), x, "nn", x_first=True, pieces=1)


_tile_lanes.defvjp(
    lambda x: (_sel_dot(_tile_matrix(), x, "nn", x_first=True, pieces=1), None),
    lambda _, g: (_sel_dot(_tile_matrix(), g, "nt", x_first=True, pieces=2),))


def _block_rows(x):
    k = x.shape[-1]
    return jnp.broadcast_to(x.reshape(HG_NSUB, 1, HG_SUB, k), (HG_NSUB, HG_SUB, HG_SUB, k)).reshape(HG_CHUNK, HG_SUB, k)


def _hgrn_chunk(st0, q_raw, f_raw, i_raw, g_raw, l0, l1, ng):
    C, SUB = HG_CHUNK, HG_SUB
    lb = jax.nn.sigmoid(l0 - l1)
    fg = lb + (1.0 - lb) * jax.nn.sigmoid(f_raw)
    kk = 1.0 - fg
    qf = jax.nn.silu(q_raw)
    al, base = _two_level_cumsum(jnp.log(fg))
    a = al + base
    row = lax.broadcasted_iota(jnp.int32, (C, HEAD), 0)
    a_last = jnp.sum(jnp.where(row == C - 1, a, 0.0), axis=0, keepdims=True)
    inter = _dot_nt(qf * jnp.exp(a), st0)
    qt = qf * jnp.exp(al)
    rb = lax.broadcasted_iota(jnp.int32, (C, C), 0) // SUB
    cb = lax.broadcasted_iota(jnp.int32, (C, C), 1) // SUB
    scores = jnp.zeros((C, C), f32)
    for i in range(1, HG_NSUB):
        base_i = jnp.sum(jnp.where(row == i * SUB, base, 0.0), axis=0, keepdims=True)
        kt = kk * jnp.exp(jnp.minimum(base_i - a, 0.0))
        scores = scores + jnp.where((rb == i) & (cb < i), _dot_nt(qt, kt), 0.0)
    t_i = lax.broadcasted_iota(jnp.int32, (C, SUB, HEAD), 0) % SUB
    s_i = lax.broadcasted_iota(jnp.int32, (C, SUB, HEAD), 1)
    decay = jnp.exp(jnp.where(s_i <= t_i, al[:, None, :] - _block_rows(al), -jnp.inf))
    diag = jnp.sum(qf[:, None, :] * decay * _block_rows(kk), axis=-1)
    scores = scores + jnp.where(rb == cb, _tile_lanes(diag), 0.0)
    o = inter + _dot_nn(scores, i_raw)
    st1 = jnp.exp(a_last) * st0 + _dot_tn(i_raw, kk * jnp.exp(a_last - a))
    on = o * lax.rsqrt(jnp.mean(o * o, axis=-1, keepdims=True) + EPS) * ng
    return st1, on * jax.nn.silu(g_raw)


def _hgrn_specs(S, Bl, rev):
    N = S // HG_CHUNK
    chunk = (lambda n: N - 1 - n) if rev else (lambda n: n)
    col = lambda c0: pl.BlockSpec((Bl, HG_CHUNK, 512), lambda n: (0, chunk(n), c0 // 512))
    st = pl.BlockSpec((Bl, N_HEAD, 1, HEAD, HEAD), lambda n: (0, 0, chunk(n), 0, 0))
    full = lambda *s: pl.BlockSpec(s, functools.partial(lambda n, nd: (0,) * nd, nd=len(s)))
    return N, col, st, full


def _hgrn_fwd(proj, lb_logits, ng, Bl, S):
    N, col, st, full = _hgrn_specs(S, Bl, False)

    def body(q_ref, f_ref, i_ref, g_ref, l_ref, ng_ref, o_ref, st_ref, state):
        @pl.when(pl.program_id(0) == 0)
        def _():
            state[...] = jnp.zeros_like(state)

        for b in range(Bl):
            for h in range(N_HEAD):
                sl = slice(h * HEAD, (h + 1) * HEAD)
                st0 = state[b, h]
                st_ref[b, h, 0] = st0
                st1, out = _hgrn_chunk(st0, *[r[b, :, sl].astype(f32) for r in (q_ref, f_ref, i_ref, g_ref)],
                                       l_ref[0:1, sl], l_ref[1:2, sl], ng_ref[...])
                state[b, h] = st1
                o_ref[b, :, sl] = out.astype(bf16)

    return _call(body, "hgrn_fwd", (N,), [col(C_HQ), col(C_HF), col(C_HI), col(C_HG), full(2, 512), full(1, HEAD)],
                 (col(0), st),
                 (jax.ShapeDtypeStruct((Bl, S, 512), bf16), jax.ShapeDtypeStruct((Bl, N_HEAD, N, HEAD, HEAD), f32)),
                 scratch=[pltpu.VMEM((Bl, N_HEAD, HEAD, HEAD), f32)], sem=("arbitrary",))(
        proj, proj, proj, proj, lb_logits, ng)


def _hgrn_bwd(proj, lb_logits, ng, states, db, dzuv, dxq, dgl, Bl, S):
    N, col, st, full = _hgrn_specs(S, Bl, True)
    rows = lambda width: pl.BlockSpec((Bl, HG_CHUNK, width), lambda n: (0, N - 1 - n, 0))

    def body(q_ref, f_ref, i_ref, g_ref, l_ref, ng_ref, st_ref, db_ref, dzuv_ref, dxq_ref, dgl_ref,
             dp_ref, dl_ref, dng_ref, dstate):
        @pl.when(pl.program_id(0) == 0)
        def _():
            dstate[...] = jnp.zeros_like(dstate)
            dl_ref[...] = jnp.zeros_like(dl_ref)
            dng_ref[...] = jnp.zeros_like(dng_ref)

        dp_ref[:, :, C_ZU:C_HQ] = dzuv_ref[...]
        dp_ref[:, :, C_XQ:C_GL] = dxq_ref[...]
        for n in range(3):
            dp_ref[:, :, C_GL + n * D_MODEL:C_GL + (n + 1) * D_MODEL] = dgl_ref[n]
        dq_ref, df_ref, di_ref, dg_ref = [dp_ref.at[:, :, c0:c0 + 512] for c0 in (C_HQ, C_HF, C_HI, C_HG)]
        for b in range(Bl):
            for h in range(N_HEAD):
                sl = slice(h * HEAD, (h + 1) * HEAD)
                _, vjp = jax.vjp(_hgrn_chunk, st_ref[b, h, 0], *[r[b, :, sl].astype(f32) for r in (q_ref, f_ref, i_ref, g_ref)],
                                 l_ref[0:1, sl], l_ref[1:2, sl], ng_ref[...])
                dst0, dq, df, di, dg, dl0, dl1, dng = vjp((dstate[b, h], db_ref[b, :, sl].astype(f32)))
                dstate[b, h] = dst0
                dq_ref[b, :, sl] = dq.astype(bf16)
                df_ref[b, :, sl] = df.astype(bf16)
                di_ref[b, :, sl] = di.astype(bf16)
                dg_ref[b, :, sl] = dg.astype(bf16)
                dl_ref[0:1, sl] += dl0
                dl_ref[1:2, sl] += dl1
                dng_ref[b, h] += dng

    return _call(body, "hgrn_bwd", (N,),
                 [col(C_HQ), col(C_HF), col(C_HI), col(C_HG), full(2, 512), full(1, HEAD), st, col(0), rows(C_HQ - C_ZU),
                  rows(C_GL - C_XQ), pl.BlockSpec((3, Bl, HG_CHUNK, D_MODEL), lambda n: (0, 0, N - 1 - n, 0))],
                 (rows(IN_WIDTH), full(2, 512), full(Bl, N_HEAD, 1, HEAD)),
                 (jax.ShapeDtypeStruct((Bl, S, IN_WIDTH), bf16), jax.ShapeDtypeStruct((2, 512), f32),
                  jax.ShapeDtypeStruct((Bl, N_HEAD, 1, HEAD), f32)),
                 scratch=[pltpu.VMEM((Bl, N_HEAD, HEAD, HEAD), f32)], sem=("arbitrary",))(
        proj, proj, proj, proj, lb_logits, ng, states, db, dzuv, dxq, dgl)


def _attn_probs(q, k):
    s = _raw_dot(q, k, "nt") * (HEAD ** -0.5)
    e = jnp.exp(s - jnp.max(s, axis=-1, keepdims=True))
    return e / jnp.sum(e, axis=-1, keepdims=True)


def _attn_specs(S, tq):
    nq = S // tq
    q = pl.BlockSpec((tq, 512), lambda b, i: (b * nq + i, C_XQ // 512))
    kv = pl.BlockSpec((1, MEM_LEN, 1024), lambda b, i: (b, 0, 0))
    o = pl.BlockSpec((tq, 512), lambda b, i: (b * nq + i, 0))
    return nq, q, kv, o


def _attn_fwd(proj, kv, Bl, S):
    tq = _pick(S, (512, 256, 128))
    nq, qs, kvs, os_ = _attn_specs(S, tq)

    def body(q_ref, kv_ref, o_ref):
        for h in range(N_HEAD):
            sl = slice(h * HEAD, (h + 1) * HEAD)
            p = _attn_probs(q_ref[:, sl], kv_ref[0, :, sl])
            o_ref[:, sl] = _raw_dot(p, kv_ref[0, :, 512 + h * HEAD:512 + (h + 1) * HEAD], "nn").astype(bf16)

    return _call(body, "attn_fwd", (Bl, nq), [qs, kvs], os_, jax.ShapeDtypeStruct((Bl * S, 512), bf16),
                 sem=("parallel", "parallel"))(proj, kv)


def _attn_bwd(proj, kv, dc, Bl, S):
    tq = _pick(S, (512, 256, 128))
    nq, qs, kvs, os_ = _attn_specs(S, tq)

    def body(q_ref, kv_ref, do_ref, dq_ref, dkv_ref):
        @pl.when(pl.program_id(1) == 0)
        def _():
            dkv_ref[...] = jnp.zeros_like(dkv_ref)

        for h in range(N_HEAD):
            sl = slice(h * HEAD, (h + 1) * HEAD)
            vsl = slice(512 + h * HEAD, 512 + (h + 1) * HEAD)
            q, k, v, do = q_ref[:, sl], kv_ref[0, :, sl], kv_ref[0, :, vsl], do_ref[:, sl]
            p = _attn_probs(q, k)
            dkv_ref[0, :, vsl] += _raw_dot(p, do, "tn")
            dp = _raw_dot(do, v, "nt")
            ds = p * (dp - jnp.sum(dp * p, axis=-1, keepdims=True)) * (HEAD ** -0.5)
            dq_ref[:, sl] = _raw_dot(ds, k, "nn").astype(bf16)
            dkv_ref[0, :, sl] += _raw_dot(ds, q, "tn")

    return _call(body, "attn_bwd", (Bl, nq), [qs, kvs, os_], (os_, kvs),
                 (jax.ShapeDtypeStruct((Bl * S, 512), bf16), jax.ShapeDtypeStruct((Bl, MEM_LEN, 1024), f32)),
                 sem=("arbitrary", "arbitrary"))(proj, kv, dc)


def _gate_specs(tm):
    half = D_MODEL // 2
    return [pl.BlockSpec((tm, half), functools.partial(lambda i, c: (i, c), c=(C_GL + n * D_MODEL) // half + k))
            for n in range(3) for k in range(2)]


def _merge_out_norm_fwd(branches, wb, proj, w_out, x, g):
    T = proj.shape[0]
    tm = _pick(T, (512, 256, 128))

    def body(a_ref, b_ref, c_ref, w0, w1, w2, g0a, g0b, g1a, g1b, g2a, g2b, wo_ref, x_ref, g_ref, m_ref, x1_ref, h_ref, ht_ref):
        acc = jnp.zeros((tm, D_MODEL), f32)
        for x_n, w_ref, ga, gb in ((a_ref, w0, g0a, g0b), (b_ref, w1, g1a, g1b), (c_ref, w2, g2a, g2b)):
            gate = jax.nn.sigmoid(jnp.concatenate([ga[...], gb[...]], axis=-1).astype(f32))
            acc = acc + gate * _raw_dot(x_n[...], w_ref[...], "nn")
        merged = acc.astype(bf16)
        m_ref[...] = merged
        x1 = x_ref[...] + _raw_dot(merged, wo_ref[...], "nn")
        x1_ref[...] = x1
        y = x1 * lax.rsqrt(jnp.mean(x1 * x1, axis=-1, keepdims=True) + EPS) * g_ref[...]
        h_ref[...] = y.astype(bf16)
        ht_ref[...] = y.T.astype(bf16)

    br = pl.BlockSpec((tm, 512), lambda i: (i, 0))
    w = pl.BlockSpec((512, D_MODEL), lambda i: (0, 0))
    row = pl.BlockSpec((tm, D_MODEL), lambda i: (i, 0))
    return _call(body, "merge_out_norm_fwd", (T // tm,),
                 [br, br, br, w, w, w, *_gate_specs(tm), pl.BlockSpec((D_MODEL, D_MODEL), lambda i: (0, 0)), row,
                  pl.BlockSpec((1, D_MODEL), lambda i: (0, 0))],
                 (row, row, row, pl.BlockSpec((D_MODEL, tm), lambda i: (0, i))),
                 (jax.ShapeDtypeStruct((T, D_MODEL), bf16), jax.ShapeDtypeStruct((T, D_MODEL), f32),
                  jax.ShapeDtypeStruct((T, D_MODEL), bf16), jax.ShapeDtypeStruct((D_MODEL, T), bf16)),
                 sem=("parallel",))(*branches, *wb, *[proj] * 6, w_out, x, g)


def _merge_bwd(branches, wb, proj, merged, dx1, w_out):
    T = proj.shape[0]
    tm = _pick(T, (256, 128))

    def body(a_ref, b_ref, c_ref, w0, w1, w2, g0a, g0b, g1a, g1b, g2a, g2b, m_ref, dx_ref, wo_ref, dgl_ref, d0, d1, d2, gw_ref, gwo_ref):
        @pl.when(pl.program_id(0) == 0)
        def _():
            gw_ref[...] = jnp.zeros_like(gw_ref)
            gwo_ref[...] = jnp.zeros_like(gwo_ref)

        dx = dx_ref[...].astype(bf16)
        gwo_ref[...] += _raw_dot(m_ref[...], dx, "tn")
        dm = _raw_dot(dx, wo_ref[...], "nt")
        for n, (x_ref, w_ref, ga, gb, d_ref) in enumerate(((a_ref, w0, g0a, g0b, d0), (b_ref, w1, g1a, g1b, d1), (c_ref, w2, g2a, g2b, d2))):
            x, w = x_ref[...], w_ref[...]
            up = _raw_dot(x, w, "nn")
            sg = jax.nn.sigmoid(jnp.concatenate([ga[...], gb[...]], axis=-1).astype(f32))
            dgl_ref[n] = (dm * up * sg * (1.0 - sg)).astype(bf16)
            dup = (dm * sg).astype(bf16)
            d_ref[...] = _raw_dot(dup, w, "nt").astype(bf16)
            gw_ref[n] += _raw_dot(x, dup, "tn")

    br = pl.BlockSpec((tm, 512), lambda i: (i, 0))
    w = pl.BlockSpec((512, D_MODEL), lambda i: (0, 0))
    sh = jax.ShapeDtypeStruct((T, 512), bf16)
    row = pl.BlockSpec((tm, D_MODEL), lambda i: (i, 0))
    square = pl.BlockSpec((D_MODEL, D_MODEL), lambda i: (0, 0))
    outs = _call(body, "merge_bwd", (T // tm,), [br, br, br, w, w, w, *_gate_specs(tm), row, row, square],
                 (pl.BlockSpec((3, tm, D_MODEL), lambda i: (0, i, 0)), br, br, br, pl.BlockSpec((3, 512, D_MODEL), lambda i: (0, 0, 0)), square),
                 (jax.ShapeDtypeStruct((3, T, D_MODEL), bf16), sh, sh, sh, jax.ShapeDtypeStruct((3, 512, D_MODEL), f32),
                  jax.ShapeDtypeStruct((D_MODEL, D_MODEL), f32)),
                 sem=("arbitrary",))(*branches, *wb, *[proj] * 6, merged, dx1, w_out)
    return outs[0], outs[1:4], outs[4], outs[5]


CONV_TC = 256


def _shift_down(a, k):
    r = pltpu.roll(a, k, 0)
    row = lax.broadcasted_iota(jnp.int32, (8, a.shape[1]), 0)
    return jnp.concatenate([jnp.where(row >= k, r[:8], 0.0), r[8:]], axis=0)


def _shift_up(a, k):
    n = a.shape[0]
    r = pltpu.roll(a, n - k, 0)
    row = lax.broadcasted_iota(jnp.int32, (8, a.shape[1]), 0)
    return jnp.concatenate([r[:n - 8], jnp.where(row < 8 - k, r[n - 8:], 0.0)], axis=0)


def _conv_pre(a, a1, a2, cw, cb):
    return cb + cw[0:1] * a2 + cw[1:2] * a1 + cw[2:3] * a


def _up_conv_fwd(h2, w_up, cw, cb):
    Bl, S, Dd = h2.shape
    nc = D_FF // CONV_TC

    def body(h_ref, wa_ref, wb_ref, cw_ref, cb_ref, a_ref, b_ref, o_ref):
        a16 = _raw_dot(h_ref[0], wa_ref[...], "nn").astype(bf16)
        b16 = _raw_dot(h_ref[0], wb_ref[...], "nn").astype(bf16)
        a_ref[0], b_ref[0] = a16, b16
        a = a16.astype(f32)
        ac = _conv_pre(a, _shift_down(a, 1), _shift_down(a, 2), cw_ref[...], cb_ref[...])
        o_ref[0] = (jax.nn.silu(ac) * b16.astype(f32)).astype(bf16)

    seq = pl.BlockSpec((1, S, CONV_TC), lambda b, c: (b, 0, c))
    sh = jax.ShapeDtypeStruct((Bl, S, D_FF), bf16)
    return _call(body, "up_conv_fwd", (Bl, nc),
                 [pl.BlockSpec((1, S, Dd), lambda b, c: (b, 0, 0)), pl.BlockSpec((Dd, CONV_TC), lambda b, c: (0, c)),
                  pl.BlockSpec((Dd, CONV_TC), lambda b, c: (0, nc + c)), pl.BlockSpec((3, CONV_TC), lambda b, c: (0, c)),
                  pl.BlockSpec((1, CONV_TC), lambda b, c: (0, c))],
                 (seq, seq, seq), (sh, sh, sh), sem=("parallel", "parallel"))(h2, w_up, w_up, cw, cb)


def _down_conv_bwd(dx2, w_down, a, b, cw, cb):
    Bl, S, Dd = dx2.shape
    nc = D_FF // CONV_TC

    def body(dx_ref, wd_ref, a_ref, b_ref, cw_ref, cb_ref, da_ref, db_ref, dcw_ref, dcb_ref):
        dact = _raw_dot(dx_ref[0], wd_ref[...], "nt").astype(bf16).astype(f32)
        a, cw = a_ref[0].astype(f32), cw_ref[...]
        a1, a2 = _shift_down(a, 1), _shift_down(a, 2)
        ac = _conv_pre(a, a1, a2, cw, cb_ref[...])
        sg = jax.nn.sigmoid(ac)
        gated = dact * sg
        db_ref[0] = (gated * ac).astype(bf16)
        dac = gated * b_ref[0].astype(f32) * (1.0 + ac * (1.0 - sg))
        da_ref[0] = (cw[2:3] * dac + cw[1:2] * _shift_up(dac, 1) + cw[0:1] * _shift_up(dac, 2)).astype(bf16)
        dcw_ref[0, 0:1, :] = jnp.sum(dac * a2, axis=0, keepdims=True)
        dcw_ref[0, 1:2, :] = jnp.sum(dac * a1, axis=0, keepdims=True)
        dcw_ref[0, 2:3, :] = jnp.sum(dac * a, axis=0, keepdims=True)
        dcb_ref[0] = jnp.sum(dac, axis=0, keepdims=True)

    seq = pl.BlockSpec((1, S, CONV_TC), lambda b_, c: (b_, 0, c))
    sh = jax.ShapeDtypeStruct((Bl, S, D_FF), bf16)
    return _call(body, "down_conv_bwd", (Bl, nc),
                 [pl.BlockSpec((1, S, Dd), lambda b_, c: (b_, 0, 0)), pl.BlockSpec((CONV_TC, Dd), lambda b_, c: (c, 0)), seq, seq,
                  pl.BlockSpec((3, CONV_TC), lambda b_, c: (0, c)), pl.BlockSpec((1, CONV_TC), lambda b_, c: (0, c))],
                 (seq, seq, pl.BlockSpec((1, 3, CONV_TC), lambda b_, c: (b_, 0, c)), pl.BlockSpec((1, 1, CONV_TC), lambda b_, c: (b_, 0, c))),
                 (sh, sh, jax.ShapeDtypeStruct((Bl, 3, D_FF), f32), jax.ShapeDtypeStruct((Bl, 1, D_FF), f32)),
                 sem=("parallel", "parallel"))(dx2, w_down, a, b, cw, cb)


def _local_step(x, mem, target, p, w_in, late_b, late_c, send, settle):
    Bl, S, Dd = x.shape
    T = Bl * S
    x2d, t2d, mem2d = x.reshape(T, Dd), target.reshape(T, Dd), mem.reshape(Bl * MEM_LEN, Dd)
    b_st = jnp.pad(p["b_spatial"].T, ((0, 0), (0, 128 - N_HEAD)))
    lbl = p["lb_logits"]

    h, h_t = _rms_fwd(x2d, p["norm1_g"], "norm1_fwd", transposed=True)
    proj = _mm(h, w_in, "nn", bf16, "proj_fwd", 1024, 1664)
    a_out = _gmlp_fwd(proj, p["ln_v_g"], p["ln_v_b"], p["w_spatial"], b_st)
    proj3 = proj.reshape(Bl, S, IN_WIDTH)
    b_out, states = _hgrn_fwd(proj3, lbl, p["hgrn_norm_g"], Bl, S)
    b_out = b_out.reshape(T, 512)
    memn = _rms_fwd(mem2d, p["mem_norm_g"], "memnorm_fwd")
    w = late_b(b_out)
    wb = w["w_branch"]
    kv = _mm(memn, w["w_mem_kv"], "nn", f32, "kv_fwd", 512, 1024).reshape(Bl, MEM_LEN, 2 * 512)
    c_out = _attn_fwd(proj, kv, Bl, S)
    branches = (a_out, b_out, c_out)
    merged, x1, h2, h2_t = _merge_out_norm_fwd(branches, wb, proj, w["w_out"], x2d, p["norm2_g"])
    w.update(late_c(h2))
    ffn_a, ffn_b, act = _up_conv_fwd(h2.reshape(Bl, S, Dd), w["w_up"], w["conv_w"], p["conv_b"])
    act = act.reshape(T, D_FF)
    loss_part, dx2, dx2_16, g_final = _down_final_loss(act, w["w_down"], x1, p["final_g"], t2d)

    g_w_down = _mm(act, dx2_16, "tn", bf16, "down_dw", 1408, 1024, 1024)
    da, db, g_conv_w, g_conv_b = _down_conv_bwd(dx2_16.reshape(Bl, S, Dd), w["w_down"], ffn_a, ffn_b, w["conv_w"], p["conv_b"])
    da, db = da.reshape(T, D_FF), db.reshape(T, D_FF)
    shard = 2 * D_FF // N_DEV
    g_w_up = _mm(h2_t, da, "nn", bf16, "up_dw_a", 512, 1408, shard=shard, n_outer=True,
                 into=(lax.empty((N_DEV, D_MODEL, shard), bf16), 0))
    g_w_up = _mm(h2_t, db, "nn", bf16, "up_dw_b", 512, 1408, shard=shard, n_outer=True, into=(g_w_up, N_DEV // 2))
    send("c", dict(w_up=g_w_up, conv_w=jnp.sum(g_conv_w, axis=0), w_down=g_w_down))
    dx1, g_norm2 = _mm_rms_bwd([(da, w["w_up"], 0), (db, w["w_up"], 1)], x1, p["norm2_g"], dx2, "up_dx_norm2_bwd", 256)

    dgl, dbr, g_w_branch, g_w_out = _merge_bwd(branches, wb, proj, merged, dx1, w["w_out"])
    dxq, dkv = _attn_bwd(proj, kv, dbr[2], Bl, S)
    dkv = dkv.reshape(Bl * MEM_LEN, 2 * 512)
    g_w_kv = _mm(memn, dkv, "tn", bf16, "kv_dw", 1024, 1024, 512)
    send("b", dict(w_mem_kv=g_w_kv, w_branch=g_w_branch, w_out=g_w_out))
    dmemn = _mm(dkv, w["w_mem_kv"], "nt", f32, "kv_dx", 512, 1024)
    _, g_mem_norm = _rms_bwd(mem2d, p["mem_norm_g"], dmemn, "memnorm_bwd")
    dzuv, g_ln_g, g_ln_b, g_w_sp, g_b_sp = _gmlp_bwd(proj, p["ln_v_g"], p["ln_v_b"], p["w_spatial"], b_st, dbr[0])
    dproj, g_lbl, g_ng = _hgrn_bwd(proj3, lbl, p["hgrn_norm_g"], states, dbr[1].reshape(Bl, S, 512), dzuv.reshape(Bl, S, -1),
                                   dxq.reshape(Bl, S, -1), dgl.reshape(3, Bl, S, Dd), Bl, S)
    dproj = dproj.reshape(T, IN_WIDTH)
    g_w_in = _mm(h_t, dproj, "nn", bf16, "proj_dw", 512, 1664, shard=IN_WIDTH // N_DEV, n_outer=True)
    send("a", dict(w_in=g_w_in))
    dx, g_norm1 = _mm_rms_bwd([(settle(dproj), w_in, 0)], x2d, p["norm1_g"], dx1, "proj_dx_norm1_bwd", 256)

    gs = dict(w_spatial=g_w_sp, norm1_g=g_norm1, mem_norm_g=g_mem_norm, norm2_g=g_norm2, final_g=g_final, lb_logits=g_lbl,
              ln_v_g=g_ln_g, ln_v_b=g_ln_b, b_spatial=g_b_sp, hgrn_norm_g=g_ng, conv_b=g_conv_b)
    return loss_part, dx.reshape(Bl, S, Dd), gs


def _coords():
    return lax.axis_index("x"), lax.axis_index("y"), lax.axis_index("c")


def _slot(dev):
    return 4 * dev[0] + 2 * dev[1] + dev[2]


def _comm_call(body, name, arrays, out_shapes, n_sem):
    n = len(arrays)
    hbm = pl.BlockSpec(memory_space=pl.ANY)
    return pl.pallas_call(
        body, name=name, out_shape=out_shapes, in_specs=[hbm] * n, out_specs=[hbm] * n,
        scratch_shapes=[pltpu.SemaphoreType.DMA((n_sem, n)), pltpu.SemaphoreType.DMA((n_sem, n)), pltpu.SemaphoreType.DMA((n,))])(*arrays)


def _all_gather(blocks, name):
    n = len(blocks)

    def body(*refs):
        x_refs, o_refs, (send_sems, recv_sems, local_sems) = refs[:n], refs[n:2 * n], refs[2 * n:]
        x, y, c = _coords()
        me, sibling = (x, y, c), (x, y, 1 - c)
        chips = [(1 - x, y), (x, 1 - y), (1 - x, 1 - y)]

        def copy(a, k, block_dev, to, from_input=False):
            dst = o_refs[a].at[_slot(block_dev)]
            return pltpu.make_async_remote_copy(src_ref=x_refs[a] if from_input else dst, dst_ref=dst, send_sem=send_sems.at[k, a],
                                                recv_sem=recv_sems.at[k, a], device_id=to, device_id_type=MESH)

        mine = [pltpu.make_async_copy(x_refs[a], o_refs[a].at[_slot(me)], local_sems.at[a]) for a in range(n)]
        first = [copy(a, 0, me, sibling, True) for a in range(n)]
        first += [copy(a, 1 + j, me, (*chip, c), True) for j, chip in enumerate(chips) for a in range(n)]
        for cp in mine + first:
            cp.start()
        passed = []
        for j, chip in enumerate(chips):
            for a in range(n):
                copy(a, 1 + j, (*chip, c), me).wait_recv()
                fwd = copy(a, 4 + j, (*chip, c), sibling)
                fwd.start()
                passed.append(fwd)
        for a in range(n):
            copy(a, 0, sibling, me).wait_recv()
        for j, chip in enumerate(chips):
            for a in range(n):
                copy(a, 4 + j, (*chip, 1 - c), me).wait_recv()
        for cp in first + passed:
            cp.wait_send()
        for cp in mine:
            cp.wait()

    return _comm_call(body, name, blocks, [jax.ShapeDtypeStruct((N_DEV,) + b.shape, b.dtype) for b in blocks], 7)


_REL = [(0, 0, 1), (0, 1, 0), (0, 1, 1), (1, 0, 0), (1, 0, 1), (1, 1, 0), (1, 1, 1)]


def _seq_exchange(arrays, gather, name, collective_id):
    n = len(arrays)
    hbm = pltpu.MemorySpace.HBM
    srcs = [jax.new_ref(a, memory_space=hbm) for a in arrays]
    lands = [jax.empty_ref(jax.ShapeDtypeStruct(((N_DEV,) + a.shape) if gather else a.shape, a.dtype), memory_space=hbm) for a in arrays]

    @pl.kernel(mesh=plsc.ScalarSubcoreMesh(axis_name="sequencer", num_cores=1), name=name,
               scratch_types=(pltpu.SemaphoreType.DMA((7, n)), pltpu.SemaphoreType.DMA((7, n)), pltpu.SemaphoreType.DMA((n,))),
               compiler_params=pltpu.CompilerParams(collective_id=collective_id))
    def launch(send, recv, local):
        x, y, c = _coords()
        me = (x, y, c)
        peers = [(x ^ dx, y ^ dy, c ^ dc) for dx, dy, dc in _REL]
        barrier = pltpu.get_barrier_semaphore()
        for peer in peers:
            pl.semaphore_signal(barrier, inc=1, device_id=peer, device_id_type=MESH)
        pl.semaphore_wait(barrier, len(peers))

        def copy(a, k, peer, arrival):
            return pltpu.make_async_remote_copy(
                src_ref=srcs[a] if gather else srcs[a].at[_slot(peer)], dst_ref=lands[a].at[_slot(peer if arrival else me)],
                send_sem=send.at[k, a], recv_sem=recv.at[k, a], device_id=peer, device_id_type=MESH)

        mine = [pltpu.make_async_copy(srcs[a] if gather else srcs[a].at[_slot(me)], lands[a].at[_slot(me)], local.at[a])
                for a in range(n)]
        out = [copy(a, k, peer, False) for a in range(n) for k, peer in enumerate(peers)]
        for cp in mine + out:
            cp.start()
        for a in range(n):
            for k, peer in enumerate(peers):
                copy(a, k, peer, True).wait_recv()
        for cp in out:
            cp.wait_send()
        for cp in mine:
            cp.wait()

    launch()
    return [land[...] for land in lands]


def _adam_math(w, g, m, v):
    m_ = ADAM_B1 * m + (1.0 - ADAM_B1) * g
    v_ = ADAM_B2 * v + (1.0 - ADAM_B2) * jnp.square(g)
    m_hat = m_ / (1.0 - ADAM_B1 ** ADAM_STEP)
    v_hat = v_ / (1.0 - ADAM_B2 ** ADAM_STEP)
    return -ADAM_LR * (m_hat / (jnp.sqrt(v_hat) + ADAM_EPS) + ADAM_WD * w), m_, v_


def _reduce_adamw(parts, w, m, v, name):
    _, R, L = parts.shape
    tr = _pick(R, (256, 128, 64, 32, 16, 8))

    def body(p_ref, w_ref, m_ref, v_ref, g_ref, d_ref, nm_ref, nv_ref):
        g = p_ref[0].astype(f32)
        for i in range(1, N_DEV):
            g = g + p_ref[i].astype(f32)
        g_ref[...] = g
        d_ref[...], nm_ref[...], nv_ref[...] = _adam_math(w_ref[...], g, m_ref[...], v_ref[...])

    blk = pl.BlockSpec((tr, L), lambda i: (i, 0))
    sh = jax.ShapeDtypeStruct((R, L), f32)
    return _call(body, name, (R // tr,), [pl.BlockSpec((N_DEV, tr, L), lambda i: (0, i, 0)), blk, blk, blk], (blk,) * 4, (sh,) * 4,
                 sem=("parallel",))(parts, w, m, v)


SMALL = (("w_spatial", (512, 128), 0), ("norm1_g", (1, 1024), 512), ("mem_norm_g", (1, 1024), 520), ("norm2_g", (1, 1024), 528),
         ("final_g", (1, 1024), 536), ("lb_logits", (2, 512), 544), ("ln_v_g", (1, 512), 552), ("ln_v_b", (1, 512), 556),
         ("b_spatial", (4, 128), 560), ("hgrn_norm_g", (1, 128), 564), ("conv_b", (1, 2816), 565))
LOSS_ROW, SMALL_USED, SMALL_ROWS = 587, 588, 640


def _segments(shape, base):
    r, n = shape
    per = n // 128
    return [(base + i * per + j, i, slice(j * 128, (j + 1) * 128)) for i in range(r) for j in range(per)]


def _pack_small(gs, loss_part):
    names = [n for n, _, _ in SMALL]

    def body(*refs):
        src, loss_ref, o_ref = dict(zip(names, refs[:-2])), refs[-2], refs[-1]
        o_ref[SMALL_USED:SMALL_ROWS, :] = jnp.zeros((SMALL_ROWS - SMALL_USED, 128), f32)
        o_ref[LOSS_ROW:LOSS_ROW + 1, :] = loss_ref[...]
        for name, shape, base in SMALL:
            ref = src[name]
            if name == "w_spatial":
                o_ref[base:base + 512, :] = ref[...].reshape(512, 128)
            elif name == "b_spatial":
                o_ref[base:base + 4, :] = ref[0:4, :]
            elif name == "conv_b":
                per_example = functools.reduce(lambda u, v_: u + v_, [ref[b] for b in range(ref.shape[0])])
                for row, i, sl in _segments(shape, base):
                    o_ref[row:row + 1, :] = per_example[i:i + 1, sl]
            elif name == "hgrn_norm_g":
                per_head = [ref[b, h] for b in range(ref.shape[0]) for h in range(N_HEAD)]
                o_ref[base:base + 1, :] = functools.reduce(lambda u, v_: u + v_, per_head)
            else:
                for row, i, sl in _segments(shape, base):
                    o_ref[row:row + 1, :] = ref[i:i + 1, sl]

    return pl.pallas_call(body, name="pack_small", out_shape=jax.ShapeDtypeStruct((SMALL_ROWS, 128), f32))(
        *[gs[n] for n in names], loss_part)


def _small_update(gathered, w, m, v):
    names = [n for n, _, _ in SMALL]
    k = len(names)

    def body(*refs):
        p_ref = refs[0]
        ins = [dict(zip(names, refs[1 + i * k:1 + (i + 1) * k])) for i in range(3)]
        outs = [dict(zip(names, refs[1 + (3 + i) * k:1 + (4 + i) * k])) for i in range(4)]
        loss_ref, gsum = refs[-2], refs[-1]
        g = p_ref[0]
        for i in range(1, N_DEV):
            g = g + p_ref[i]
        gsum[...] = g
        loss_ref[...] = gsum[LOSS_ROW:LOSS_ROW + 1, :]
        for name, shape, base in SMALL:
            if name == "w_spatial":
                where = [(slice(base, base + 512), (slice(None), slice(None)))]
            else:
                where = [(slice(row, row + 1), (slice(i, i + 1), sl)) for row, i, sl in _segments(shape, base)]
            for rows, at in where:
                g_ = gsum[rows, :]
                d_, m_, v_ = _adam_math(ins[0][name][at], g_, ins[1][name][at], ins[2][name][at])
                for o, val in zip(outs, (g_, d_, m_, v_)):
                    o[name][at] = val

    args = [gathered] + [d[n] for d in (w, m, v) for n in names]
    out_shapes = [jax.ShapeDtypeStruct(shape, f32) for _ in range(4) for _, shape, _ in SMALL] + [jax.ShapeDtypeStruct((1, 128), f32)]
    outs = pl.pallas_call(body, name="small_update", out_shape=out_shapes, scratch_shapes=[pltpu.VMEM((SMALL_ROWS, 128), f32)])(*args)
    return [dict(zip(names, outs[i * k:(i + 1) * k])) for i in range(4)], outs[-1]


def _cols_full(g):
    return jnp.moveaxis(g, 0, -2).reshape(g.shape[1:-1] + (N_DEV * g.shape[-1],))


def _join_cols(g, name):
    _, R, n = g.shape
    tr = _pick(R, (256, 128))

    def body(g_ref, o_ref):
        for j in range(N_DEV):
            o_ref[:, j * n:(j + 1) * n] = g_ref[j]

    return _call(body, name, (R // tr,), [pl.BlockSpec((N_DEV, tr, n), lambda i: (0, i, 0))],
                 pl.BlockSpec((tr, N_DEV * n), lambda i: (i, 0)), jax.ShapeDtypeStruct((R, N_DEV * n), g.dtype), sem=("parallel",))(g)


def _cols_parts(full):
    n = full.shape[-1] // N_DEV
    return jnp.moveaxis(full.reshape(full.shape[:-1] + (N_DEV, n)), -2, 0)


def kernel(x, mem, norm1_g, w_in, ln_v_g, ln_v_b, w_spatial, b_spatial, lb_logits, hgrn_norm_g, mem_norm_g, w_mem_kv, w_branch, w_out, norm2_g, w_up, conv_w, conv_b, w_down, final_g, loss_target, m_norm1_g, m_w_in, m_ln_v_g, m_ln_v_b, m_w_spatial, m_b_spatial, m_lb_logits, m_hgrn_norm_g, m_mem_norm_g, m_w_mem_kv, m_w_branch, m_w_out, m_norm2_g, m_w_up, m_conv_w, m_conv_b, m_w_down, m_final_g, v_norm1_g, v_w_in, v_ln_v_g, v_ln_v_b, v_w_spatial, v_b_spatial, v_lb_logits, v_hgrn_norm_g, v_mem_norm_g, v_w_mem_kv, v_w_branch, v_w_out, v_norm2_g, v_w_up, v_conv_w, v_conv_b, v_w_down, v_final_g):
    given = dict(locals())
    order = ("norm1_g", "w_in", "ln_v_g", "ln_v_b", "w_spatial", "b_spatial", "lb_logits", "hgrn_norm_g", "mem_norm_g",
             "w_mem_kv", "w_branch", "w_out", "norm2_g", "w_up", "conv_w", "conv_b", "w_down", "final_g")
    groups = dict(a=("w_in",), b=("w_mem_kv", "w_branch", "w_out"), c=("w_up", "conv_w", "w_down"))

    wire = {n: given[n][0].astype(f32 if n == "conv_w" else bf16) for ns in groups.values() for n in ns}
    g_in = _all_gather([wire["w_in"]], "gather_w_in")[0]
    w_in_full = _join_cols(g_in, "join_w_in")
    w_in_full, wire_b, wire_c = lax.optimization_barrier((w_in_full, [wire[n] for n in groups["b"]], [wire[n] for n in groups["c"]]))
    rest_b = _seq_exchange(wire_b, True, "gather_b", 1)
    rest_c = _seq_exchange(wire_c, True, "gather_c", 6)

    def late_b(after):
        _, (kv_, br_, out_) = lax.optimization_barrier((after, tuple(rest_b)))
        br_ = _cols_full(br_)
        return dict(w_mem_kv=kv_.reshape(D_MODEL, 2 * 512), w_branch=[br_[n] for n in range(3)], w_out=out_.reshape(D_MODEL, D_MODEL))

    def late_c(after):
        _, (up_, cw_, down_) = lax.optimization_barrier((after, tuple(rest_c)))
        up_ = _join_cols(up_, "join_w_up")
        return dict(w_up=up_, conv_w=_cols_full(cw_), w_down=down_.reshape(D_FF, D_MODEL))

    to_parts = dict(w_in=lambda g_: g_, w_up=lambda g_: g_, conv_w=_cols_parts,
                    w_branch=lambda g_: _cols_parts(g_.astype(bf16)).reshape(N_DEV, -1, 128),
                    w_mem_kv=lambda g_: g_.reshape(N_DEV, -1, 2 * 512), w_out=lambda g_: g_.astype(bf16).reshape(N_DEV, -1, D_MODEL),
                    w_down=lambda g_: g_.reshape(N_DEV, -1, D_MODEL))
    scatters = {}

    def send(tag, grads_):
        parts = [to_parts[n](grads_[n]) for n in groups[tag]]
        scatters[tag] = _seq_exchange(parts, False, f"scatter_{tag}", dict(a=2, b=4, c=5)[tag])

    small_2d = lambda prefix: {n: given[prefix + n].reshape(shape) for n, shape, _ in SMALL}
    p = small_2d("")
    p["w_spatial"] = w_spatial[0]
    updates = {}

    def update(tag):
        for n, parts in zip(groups[tag], scatters[tag]):
            two_d = (-1, given[n].shape[-1])
            updates[n] = _reduce_adamw(parts, *[given[pre + n].reshape(two_d) for pre in ("", "m_", "v_")], "adamw_" + n)

    def settle(chain):
        update("c")
        update("b")
        early = groups["c"] + groups["b"]
        chain, tied = lax.optimization_barrier((chain, [updates[n] for n in early]))
        updates.update(zip(early, tied))
        return chain

    loss_part, grad_x, gs = _local_step(x, mem, loss_target, p, w_in_full, late_b, late_c, send, settle)

    gathered = _seq_exchange([_pack_small(gs, loss_part)], True, "gather_small", 3)[0]

    update("a")
    grads, delta, new_m, new_v = {}, {}, {}, {}
    for n, res in updates.items():
        grads[n], delta[n], new_m[n], new_v[n] = [r.reshape(given[n].shape) for r in res]

    small_results, loss_row = _small_update(gathered, small_2d(""), small_2d("m_"), small_2d("v_"))
    for dst, res in zip((grads, delta, new_m, new_v), small_results):
        for n, _, _ in SMALL:
            dst[n] = res[n].reshape(given[n].shape)
    loss = loss_row[0, 0]

    return (loss, grad_x, *[grads[n] for n in order], *[delta[n] for n in order], *[new_m[n] for n in order],
            *[new_v[n] for n in order])
```

```python
import functools

import jax
import jax.numpy as jnp
from jax import lax
from jax.experimental import pallas as pl
from jax.experimental.pallas import tpu as pltpu
from jax.experimental.pallas import tpu_sc as plsc

f32 = jnp.float32
bf16 = jnp.bfloat16

N_DEV = 8
D_MODEL = 1024
EPS = 1e-6
GM_CHUNK = 128
HG_CHUNK = 64
HEAD = 128
N_HEAD = 4
MEM_LEN = 256
D_FF = 2816
IN_WIDTH = 6656
C_ZU, C_HQ, C_HF, C_HI, C_HG, C_XQ, C_GL = 0, 1024, 1536, 2048, 2560, 3072, 3584
ADAM_LR, ADAM_B1, ADAM_B2, ADAM_EPS, ADAM_WD, ADAM_STEP = 0.001, 0.9, 0.999, 1e-08, 0.01, 10
VMEM_LIMIT = 56 * 1024 * 1024
MESH = pl.DeviceIdType.MESH


def _pick(n, cands):
    for c in cands:
        if n % c == 0:
            return c
    return n


def _call(body, name, grid, in_specs, out_specs, out_shape, scratch=(), sem=None, **cp):
    params = dict(vmem_limit_bytes=VMEM_LIMIT, **cp)
    if sem is not None:
        params["dimension_semantics"] = sem
    return pl.pallas_call(
        body, name=name, grid=grid, in_specs=in_specs, out_specs=out_specs, out_shape=out_shape,
        scratch_shapes=list(scratch), compiler_params=pltpu.CompilerParams(**params))


_DN = {"nn": (((1,), (0,)), ((), ())), "nt": (((1,), (1,)), ((), ())), "tn": (((0,), (0,)), ((), ()))}


def _raw_dot(a, b, mode):
    return lax.dot_general(a.astype(bf16), b.astype(bf16), _DN[mode], preferred_element_type=f32)


@jax.custom_vjp
def _dot_nn(a, b):
    return _raw_dot(a, b, "nn")


_dot_nn.defvjp(lambda a, b: (_raw_dot(a, b, "nn"), (a, b)),
               lambda r, g: (_raw_dot(g, r[1], "nt"), _raw_dot(r[0], g, "tn")))


@jax.custom_vjp
def _dot_nt(a, b):
    return _raw_dot(a, b, "nt")


_dot_nt.defvjp(lambda a, b: (_raw_dot(a, b, "nt"), (a, b)),
               lambda r, g: (_raw_dot(g, r[1], "nn"), _raw_dot(g, r[0], "tn")))


@jax.custom_vjp
def _dot_tn(a, b):
    return _raw_dot(a, b, "tn")


_dot_tn.defvjp(lambda a, b: (_raw_dot(a, b, "tn"), (a, b)),
               lambda r, g: (_raw_dot(r[1], g, "nt"), _raw_dot(r[0], g, "nn")))


def _tri(n, lower):
    r = lax.broadcasted_iota(jnp.int32, (n, n), 0)
    c = lax.broadcasted_iota(jnp.int32, (n, n), 1)
    return ((c <= r) if lower else (c >= r)).astype(f32)


def _sel_dot(sel, x, mode, x_first=False, pieces=3):
    sel = sel.astype(bf16)
    out, rest = None, x
    for p in range(pieces):
        piece = rest.astype(bf16)
        part = lax.dot_general(*((piece, sel) if x_first else (sel, piece)), _DN[mode], preferred_element_type=f32)
        out = part if out is None else out + part
        if p + 1 < pieces:
            rest = rest - piece.astype(f32)
    return out


def _egrad(fn, x, ct):
    return jax.vjp(fn, x)[1](ct)[0]


def _mm(a, b, mode, out_dtype, name, tm, tn, tk=None, residual=None, into=None):
    if mode == "nn":
        (M, K), (_, N) = a.shape, b.shape
    elif mode == "nt":
        (M, K), (N, _) = a.shape, b.shape
    else:
        (K, M), (_, N) = a.shape, b.shape
    tm, tn = min(tm, M), min(tn, N)
    tk = K if tk is None else min(tk, K)
    assert M % tm == 0 and N % tn == 0 and K % tk == 0, (name, M, N, K, tm, tn, tk)
    nk = K // tk

    def body(*refs):
        acc_ref = refs[-1] if nk > 1 else None
        refs = refs[:-1] if nk > 1 else refs
        if residual is None:
            a_ref, b_ref, *_, o_ref = refs
        else:
            a_ref, b_ref, r_ref, o_ref = refs

        def finish(r):
            if residual is not None:
                r = r + r_ref[...]
            o_ref[...] = r.astype(out_dtype)

        part = _raw_dot(a_ref[...], b_ref[...], mode)
        if nk == 1:
            finish(part)
            return
        k = pl.program_id(2)

        @pl.when(k == 0)
        def _():
            acc_ref[...] = part

        @pl.when((k > 0) & (k < nk - 1))
        def _():
            acc_ref[...] += part

        @pl.when(k == nk - 1)
        def _():
            finish(acc_ref[...] + part)

    a_spec = {"nn": pl.BlockSpec((tm, tk), lambda i, j, k: (i, k)),
              "nt": pl.BlockSpec((tm, tk), lambda i, j, k: (i, k)),
              "tn": pl.BlockSpec((tk, tm), lambda i, j, k: (k, i))}[mode]
    b_spec = {"nn": pl.BlockSpec((tk, tn), lambda i, j, k: (k, j)),
              "nt": pl.BlockSpec((tn, tk), lambda i, j, k: (j, k)),
              "tn": pl.BlockSpec((tk, tn), lambda i, j, k: (k, j))}[mode]
    o_spec = pl.BlockSpec((tm, tn), lambda i, j, k: (i, j))
    in_specs = [a_spec, b_spec] + ([o_spec] if residual is not None else [])
    args = (a, b) + ((residual,) if residual is not None else ())
    out_shape = jax.ShapeDtypeStruct((M, N), out_dtype)
    extra = {}
    if into is not None:
        assert residual is None and into[1] % tm == 0
        o_spec = pl.BlockSpec((tm, tn), lambda i, j, k: (i + into[1] // tm, j))
        out_shape = jax.ShapeDtypeStruct(into[0].shape, out_dtype)
        in_specs, args = in_specs + [pl.BlockSpec(memory_space=pl.ANY)], args + (into[0],)
        extra = dict(input_output_aliases={2: 0})
    return pl.pallas_call(
        body, name=name, grid=(M // tm, N // tn, nk), in_specs=in_specs, out_specs=o_spec, out_shape=out_shape,
        scratch_shapes=[pltpu.VMEM((tm, tn), f32)] if nk > 1 else [],
        compiler_params=pltpu.CompilerParams(vmem_limit_bytes=VMEM_LIMIT, dimension_semantics=("parallel", "parallel", "arbitrary")),
        **extra)(*args)


def _rms_fwd(x, g, name):
    R, Dd = x.shape
    tr = _pick(R, (512, 256, 128))

    def body(x_ref, g_ref, o_ref):
        xf = x_ref[...]
        o_ref[...] = (xf * lax.rsqrt(jnp.mean(xf * xf, axis=-1, keepdims=True) + EPS) * g_ref[...]).astype(bf16)

    row = pl.BlockSpec((tr, Dd), lambda i: (i, 0))
    return _call(body, name, (R // tr,), [row, pl.BlockSpec((1, Dd), lambda i: (0, 0))], row, jax.ShapeDtypeStruct((R, Dd), bf16),
                 sem=("parallel",))(x, g)


def _rms_bwd(x, g, dh, name, residual=None):
    R, Dd = x.shape
    tr = _pick(R, (512, 256, 128))

    def body(*refs):
        if residual is None:
            x_ref, g_ref, dh_ref, dx_ref, dg_ref = refs
        else:
            x_ref, g_ref, dh_ref, r_ref, dx_ref, dg_ref = refs
        xf = x_ref[...]
        rs = lax.rsqrt(jnp.mean(xf * xf, axis=-1, keepdims=True) + EPS)
        y = xf * rs
        dh_ = dh_ref[...].astype(f32)
        dy = dh_ * g_ref[...]
        dx = rs * (dy - y * jnp.mean(dy * y, axis=-1, keepdims=True))
        if residual is not None:
            dx = dx + r_ref[...]
        dx_ref[...] = dx

        @pl.when(pl.program_id(0) == 0)
        def _():
            dg_ref[...] = jnp.zeros_like(dg_ref)

        dg_ref[...] += jnp.sum(dh_ * y, axis=0, keepdims=True)

    row = pl.BlockSpec((tr, Dd), lambda i: (i, 0))
    vec = pl.BlockSpec((1, Dd), lambda i: (0, 0))
    in_specs = [row, vec, row] + ([row] if residual is not None else [])
    args = (x, g, dh) + ((residual,) if residual is not None else ())
    return _call(body, name, (R // tr,), in_specs, (row, vec),
                 (jax.ShapeDtypeStruct((R, Dd), f32), jax.ShapeDtypeStruct((1, Dd), f32)), sem=("arbitrary",))(*args)


def _mm_rms_bwd(pairs, x, g, residual, name, tm):
    M = x.shape[0]
    Dd = x.shape[1]
    tm = min(tm, M)
    n = len(pairs)

    def body(*refs):
        ab_refs, (x_ref, g_ref, r_ref, dx_ref, dg_ref) = refs[:2 * n], refs[2 * n:]
        dh_ = _raw_dot(ab_refs[0][...], ab_refs[1][...], "nn")
        for k in range(1, n):
            dh_ = dh_ + _raw_dot(ab_refs[2 * k][...], ab_refs[2 * k + 1][...], "nn")
        xf = x_ref[...]
        rs = lax.rsqrt(jnp.mean(xf * xf, axis=-1, keepdims=True) + EPS)
        y = xf * rs
        dy = dh_ * g_ref[...]
        dx_ref[...] = rs * (dy - y * jnp.mean(dy * y, axis=-1, keepdims=True)) + r_ref[...]

        @pl.when(pl.program_id(0) == 0)
        def _():
            dg_ref[...] = jnp.zeros_like(dg_ref)

        dg_ref[...] += jnp.sum(dh_ * y, axis=0, keepdims=True)

    row = pl.BlockSpec((tm, Dd), lambda i: (i, 0))
    vec = pl.BlockSpec((1, Dd), lambda i: (0, 0))
    in_specs, args = [], []
    for a, b, k in pairs:
        in_specs += [pl.BlockSpec((tm, a.shape[1]), lambda i: (i, 0)),
                     pl.BlockSpec((a.shape[1], b.shape[1]), functools.partial(lambda i, k_: (k_, 0), k_=k))]
        args += [a, b]
    in_specs += [row, vec, row]
    args += [x, g, residual]
    return _call(body, name, (M // tm,), in_specs, (row, vec),
                 (jax.ShapeDtypeStruct((M, Dd), f32), jax.ShapeDtypeStruct((1, Dd), f32)), sem=("arbitrary",))(*args)


def _down_final_loss(act, w_down, x1, g, target):
    R, Dd = x1.shape
    tr = _pick(R, (512, 256, 128))

    def body(a_ref, w_ref, x1_ref, g_ref, t_ref, loss_ref, dx_ref, dxb_ref, dg_ref):
        xf = _raw_dot(a_ref[...], w_ref[...], "nn") + x1_ref[...]
        rs = lax.rsqrt(jnp.mean(xf * xf, axis=-1, keepdims=True) + EPS)
        y = xf * rs
        err = y * g_ref[...] - t_ref[...]
        dh_ = err * (1.0 / Dd)
        dy = dh_ * g_ref[...]
        dx = rs * (dy - y * jnp.mean(dy * y, axis=-1, keepdims=True))
        dx_ref[...] = dx
        dxb_ref[...] = dx.astype(bf16)

        @pl.when(pl.program_id(0) == 0)
        def _():
            dg_ref[...] = jnp.zeros_like(dg_ref)
            loss_ref[...] = jnp.zeros_like(loss_ref)

        dg_ref[...] += jnp.sum(dh_ * y, axis=0, keepdims=True)
        part = jnp.sum(jnp.mean(err * err, axis=-1, keepdims=True), axis=0, keepdims=True)
        loss_ref[...] += 0.5 * part

    row = pl.BlockSpec((tr, Dd), lambda i: (i, 0))
    vec = pl.BlockSpec((1, Dd), lambda i: (0, 0))
    in_specs = [pl.BlockSpec((tr, act.shape[1]), lambda i: (i, 0)), pl.BlockSpec(w_down.shape, lambda i: (0, 0)), row, vec, row]
    return _call(body, "down_final_loss", (R // tr,), in_specs, (pl.BlockSpec((1, 128), lambda i: (0, 0)), row, row, vec),
                 (jax.ShapeDtypeStruct((1, 128), f32), jax.ShapeDtypeStruct((R, Dd), f32), jax.ShapeDtypeStruct((R, Dd), bf16),
                  jax.ShapeDtypeStruct((1, Dd), f32)), sem=("arbitrary",))(act, w_down, x1, g, target)


def _gmlp_parts(zuv, ln_g, ln_b):
    zu, zv = zuv[:, :512], zuv[:, 512:]
    u = jax.nn.gelu(zu)
    v = jax.nn.gelu(zv)
    mu = jnp.mean(v, axis=-1, keepdims=True)
    rs = lax.rsqrt(jnp.mean(jnp.square(v - mu), axis=-1, keepdims=True) + EPS)
    xh = (v - mu) * rs
    return zu, zv, u, xh, rs, xh * ln_g + ln_b


GM_TILE_CHUNKS = 4


def _gmlp_tile(T):
    n = _pick(T // GM_CHUNK, (GM_TILE_CHUNKS, 2, 1))
    return n, n * GM_CHUNK


def _gmlp_fwd(proj, ln_g, ln_b, w_s, b_st):
    T = proj.shape[0]
    nch, rows = _gmlp_tile(T)

    def body(p_ref, g_ref, b_ref, w_ref, bs_ref, o_ref):
        _, _, u, _, _, vn = _gmlp_parts(p_ref[...].astype(f32), g_ref[...], b_ref[...])
        causal = _tri(GM_CHUNK, True) > 0
        for gi in range(N_HEAD):
            sl = slice(gi * HEAD, (gi + 1) * HEAD)
            w = jnp.where(causal, w_ref[gi], 0.0)
            for ch in range(nch):
                rs_ = slice(ch * GM_CHUNK, (ch + 1) * GM_CHUNK)
                mixed = _raw_dot(w, vn[rs_, sl], "nn") + bs_ref[:, gi:gi + 1]
                o_ref[rs_, sl] = (u[rs_, sl] * mixed).astype(bf16)

    vec = pl.BlockSpec((1, 512), lambda i: (0, 0))
    return _call(body, "gmlp_fwd", (T // rows,),
                 [pl.BlockSpec((rows, 1024), lambda i: (i, 0)), vec, vec,
                  pl.BlockSpec((N_HEAD, GM_CHUNK, GM_CHUNK), lambda i: (0, 0, 0)), pl.BlockSpec((GM_CHUNK, 128), lambda i: (0, 0))],
                 pl.BlockSpec((rows, 512), lambda i: (i, 0)), jax.ShapeDtypeStruct((T, 512), bf16), sem=("parallel",))(
        proj, ln_g, ln_b, w_s, b_st)


def _gmlp_bwd(proj, ln_g, ln_b, w_s, b_st, da):
    T = proj.shape[0]
    nch, rows = _gmlp_tile(T)

    def body(p_ref, g_ref, b_ref, w_ref, bs_ref, da_ref, dp_ref, dg_ref, db_ref, dw_ref, dbs_ref):
        zu, zv, u, xh, rs, vn = _gmlp_parts(p_ref[...].astype(f32), g_ref[...], b_ref[...])
        causal = _tri(GM_CHUNK, True) > 0
        sub = lax.broadcasted_iota(jnp.int32, (8, GM_CHUNK), 0)
        ones = jnp.ones((8, HEAD), f32)
        dout = da_ref[...].astype(f32)

        @pl.when(pl.program_id(0) == 0)
        def _():
            for r in (dg_ref, db_ref, dw_ref, dbs_ref):
                r[...] = jnp.zeros_like(r)

        du, dvn, dbs = [], [], jnp.zeros((8, GM_CHUNK), f32)
        for gi in range(N_HEAD):
            sl = slice(gi * HEAD, (gi + 1) * HEAD)
            w = jnp.where(causal, w_ref[gi], 0.0)
            du_g, dvn_g, dw_g = [], [], jnp.zeros((GM_CHUNK, GM_CHUNK), f32)
            for ch in range(nch):
                rs_ = slice(ch * GM_CHUNK, (ch + 1) * GM_CHUNK)
                mixed = _raw_dot(w, vn[rs_, sl], "nn") + bs_ref[:, gi:gi + 1]
                du_g.append(dout[rs_, sl] * mixed)
                dm = dout[rs_, sl] * u[rs_, sl]
                dbs = dbs + jnp.where(sub == gi, _sel_dot(ones, dm, "nt"), 0.0)
                dw_g = dw_g + _raw_dot(dm, vn[rs_, sl], "nt")
                dvn_g.append(_raw_dot(w, dm, "tn"))
            dw_ref[gi] += jnp.where(causal, dw_g, 0.0)
            du.append(jnp.concatenate(du_g, axis=0))
            dvn.append(jnp.concatenate(dvn_g, axis=0))
        dbs_ref[...] += dbs
        du = jnp.concatenate(du, axis=-1)
        dvn = jnp.concatenate(dvn, axis=-1)
        dg_ref[...] += jnp.sum(dvn * xh, axis=0, keepdims=True)
        db_ref[...] += jnp.sum(dvn, axis=0, keepdims=True)
        dxh = dvn * g_ref[...]
        dv = rs * (dxh - jnp.mean(dxh, axis=-1, keepdims=True) - xh * jnp.mean(dxh * xh, axis=-1, keepdims=True))
        dp_ref[:, :512] = _egrad(jax.nn.gelu, zu, du).astype(bf16)
        dp_ref[:, 512:] = _egrad(jax.nn.gelu, zv, dv).astype(bf16)

    vec = pl.BlockSpec((1, 512), lambda i: (0, 0))
    wsp = pl.BlockSpec((N_HEAD, GM_CHUNK, GM_CHUNK), lambda i: (0, 0, 0))
    return _call(body, "gmlp_bwd", (T // rows,),
                 [pl.BlockSpec((rows, 1024), lambda i: (i, 0)), vec, vec, wsp, pl.BlockSpec((GM_CHUNK, 128), lambda i: (0, 0)),
                  pl.BlockSpec((rows, 512), lambda i: (i, 0))],
                 (pl.BlockSpec((rows, 1024), lambda i: (i, 0)), vec, vec, wsp, pl.BlockSpec((8, GM_CHUNK), lambda i: (0, 0))),
                 (jax.ShapeDtypeStruct((T, 1024), bf16), jax.ShapeDtypeStruct((1, 512), f32), jax.ShapeDtypeStruct((1, 512), f32),
                  jax.ShapeDtypeStruct((N_HEAD, GM_CHUNK, GM_CHUNK), f32), jax.ShapeDtypeStruct((8, GM_CHUNK), f32)),
                 sem=("arbitrary",))(proj, ln_g, ln_b, w_s, b_st, da)


HG_SUB = 8
HG_NSUB = HG_CHUNK // HG_SUB


def _two_level_matrix(transposed=False):
    shape = (HG_CHUNK, 2 * HG_CHUNK) if transposed else (2 * HG_CHUNK, HG_CHUNK)
    r = lax.broadcasted_iota(jnp.int32, shape, 1 if transposed else 0)
    c = lax.broadcasted_iota(jnp.int32, shape, 0 if transposed else 1)
    t = jnp.where(r < HG_CHUNK, r, r - HG_CHUNK)
    local = (r < HG_CHUNK) & (t // HG_SUB == c // HG_SUB) & (c <= t)
    before = (r >= HG_CHUNK) & (c < (t // HG_SUB) * HG_SUB)
    return (local | before).astype(f32)


def _two_level_sums(x):
    two = _sel_dot(_two_level_matrix(), x, "nn")
    return two[:HG_CHUNK], two[HG_CHUNK:]


@jax.custom_vjp
def _two_level_cumsum(x):
    return _two_level_sums(x)


_two_level_cumsum.defvjp(
    lambda x: (_two_level_sums(x), None),
    lambda _, g: (_sel_dot(_two_level_matrix(), jnp.concatenate(g, axis=0), "tn"),))


def _tile_matrix():
    s = lax.broadcasted_iota(jnp.int32, (HG_SUB, HG_CHUNK), 0)
    j = lax.broadcasted_iota(jnp.int32, (HG_SUB, HG_CHUNK), 1)
    return (j % HG_SUB == s).astype(f32)


@jax.custom_vjp
def _tile_lanes(x):
    return _sel_dot(_tile_matrix(), x, "nn", x_first=True, pieces=1)


_tile_lanes.defvjp(
    lambda x: (_sel_dot(_tile_matrix(), x, "nn", x_first=True, pieces=1), None),
    lambda _, g: (_sel_dot(_tile_matrix(), g, "nt", x_first=True, pieces=2),))


def _block_rows(x):
    k = x.shape[-1]
    return jnp.broadcast_to(x.reshape(HG_NSUB, 1, HG_SUB, k), (HG_NSUB, HG_SUB, HG_SUB, k)).reshape(HG_CHUNK, HG_SUB, k)


def _hgrn_chunk(st0, q_raw, f_raw, i_raw, g_raw, l0, l1, ng):
    C, SUB = HG_CHUNK, HG_SUB
    lb = jax.nn.sigmoid(l0 - l1)
    fg = lb + (1.0 - lb) * jax.nn.sigmoid(f_raw)
    kk = 1.0 - fg
    qf = jax.nn.silu(q_raw)
    al, base = _two_level_cumsum(jnp.log(fg))
    a = al + base
    row = lax.broadcasted_iota(jnp.int32, (C, HEAD), 0)
    a_last = jnp.sum(jnp.where(row == C - 1, a, 0.0), axis=0, keepdims=True)
    inter = _dot_nt(qf * jnp.exp(a), st0)
    qt = qf * jnp.exp(al)
    rb = lax.broadcasted_iota(jnp.int32, (C, C), 0) // SUB
    cb = lax.broadcasted_iota(jnp.int32, (C, C), 1) // SUB
    scores = jnp.zeros((C, C), f32)
    for i in range(1, HG_NSUB):
        base_i = jnp.sum(jnp.where(row == i * SUB, base, 0.0), axis=0, keepdims=True)
        kt = kk * jnp.exp(jnp.minimum(base_i - a, 0.0))
        scores = scores + jnp.where((rb == i) & (cb < i), _dot_nt(qt, kt), 0.0)
    t_i = lax.broadcasted_iota(jnp.int32, (C, SUB, HEAD), 0) % SUB
    s_i = lax.broadcasted_iota(jnp.int32, (C, SUB, HEAD), 1)
    decay = jnp.exp(jnp.where(s_i <= t_i, al[:, None, :] - _block_rows(al), -jnp.inf))
    diag = jnp.sum(qf[:, None, :] * decay * _block_rows(kk), axis=-1)
    scores = scores + jnp.where(rb == cb, _tile_lanes(diag), 0.0)
    o = inter + _dot_nn(scores, i_raw)
    st1 = jnp.exp(a_last) * st0 + _dot_tn(i_raw, kk * jnp.exp(a_last - a))
    on = o * lax.rsqrt(jnp.mean(o * o, axis=-1, keepdims=True) + EPS) * ng
    return st1, on * jax.nn.silu(g_raw)


def _hgrn_specs(S, Bl, rev):
    N = S // HG_CHUNK
    chunk = (lambda n: N - 1 - n) if rev else (lambda n: n)
    col = lambda c0: pl.BlockSpec((Bl, HG_CHUNK, 512), lambda n: (0, chunk(n), c0 // 512))
    st = pl.BlockSpec((Bl, N_HEAD, 1, HEAD, HEAD), lambda n: (0, 0, chunk(n), 0, 0))
    full = lambda *s: pl.BlockSpec(s, functools.partial(lambda n, nd: (0,) * nd, nd=len(s)))
    return N, col, st, full


def _hgrn_fwd(proj, lb_logits, ng, Bl, S):
    N, col, st, full = _hgrn_specs(S, Bl, False)

    def body(q_ref, f_ref, i_ref, g_ref, l_ref, ng_ref, o_ref, st_ref, state):
        @pl.when(pl.program_id(0) == 0)
        def _():
            state[...] = jnp.zeros_like(state)

        for b in range(Bl):
            for h in range(N_HEAD):
                sl = slice(h * HEAD, (h + 1) * HEAD)
                st0 = state[b, h]
                st_ref[b, h, 0] = st0
                st1, out = _hgrn_chunk(st0, *[r[b, :, sl].astype(f32) for r in (q_ref, f_ref, i_ref, g_ref)],
                                       l_ref[0:1, sl], l_ref[1:2, sl], ng_ref[...])
                state[b, h] = st1
                o_ref[b, :, sl] = out.astype(bf16)

    return _call(body, "hgrn_fwd", (N,), [col(C_HQ), col(C_HF), col(C_HI), col(C_HG), full(2, 512), full(1, HEAD)],
                 (col(0), st),
                 (jax.ShapeDtypeStruct((Bl, S, 512), bf16), jax.ShapeDtypeStruct((Bl, N_HEAD, N, HEAD, HEAD), f32)),
                 scratch=[pltpu.VMEM((Bl, N_HEAD, HEAD, HEAD), f32)], sem=("arbitrary",))(
        proj, proj, proj, proj, lb_logits, ng)


def _hgrn_bwd(proj, lb_logits, ng, states, db, dzuv, dxq, dgl, Bl, S):
    N, col, st, full = _hgrn_specs(S, Bl, True)
    rows = lambda width: pl.BlockSpec((Bl, HG_CHUNK, width), lambda n: (0, N - 1 - n, 0))

    def body(q_ref, f_ref, i_ref, g_ref, l_ref, ng_ref, st_ref, db_ref, dzuv_ref, dxq_ref, dgl_ref,
             dp_ref, dl_ref, dng_ref, dstate):
        @pl.when(pl.program_id(0) == 0)
        def _():
            dstate[...] = jnp.zeros_like(dstate)
            dl_ref[...] = jnp.zeros_like(dl_ref)
            dng_ref[...] = jnp.zeros_like(dng_ref)

        dp_ref[:, :, C_ZU:C_HQ] = dzuv_ref[...]
        dp_ref[:, :, C_XQ:C_GL] = dxq_ref[...]
        for n in range(3):
            dp_ref[:, :, C_GL + n * D_MODEL:C_GL + (n + 1) * D_MODEL] = dgl_ref[n]
        dq_ref, df_ref, di_ref, dg_ref = [dp_ref.at[:, :, c0:c0 + 512] for c0 in (C_HQ, C_HF, C_HI, C_HG)]
        for b in range(Bl):
            for h in range(N_HEAD):
                sl = slice(h * HEAD, (h + 1) * HEAD)
                _, vjp = jax.vjp(_hgrn_chunk, st_ref[b, h, 0], *[r[b, :, sl].astype(f32) for r in (q_ref, f_ref, i_ref, g_ref)],
                                 l_ref[0:1, sl], l_ref[1:2, sl], ng_ref[...])
                dst0, dq, df, di, dg, dl0, dl1, dng = vjp((dstate[b, h], db_ref[b, :, sl].astype(f32)))
                dstate[b, h] = dst0
                dq_ref[b, :, sl] = dq.astype(bf16)
                df_ref[b, :, sl] = df.astype(bf16)
                di_ref[b, :, sl] = di.astype(bf16)
                dg_ref[b, :, sl] = dg.astype(bf16)
                dl_ref[0:1, sl] += dl0
                dl_ref[1:2, sl] += dl1
                dng_ref[b, h] += dng

    return _call(body, "hgrn_bwd", (N,),
                 [col(C_HQ), col(C_HF), col(C_HI), col(C_HG), full(2, 512), full(1, HEAD), st, col(0), rows(C_HQ - C_ZU),
                  rows(C_GL - C_XQ), pl.BlockSpec((3, Bl, HG_CHUNK, D_MODEL), lambda n: (0, 0, N - 1 - n, 0))],
                 (rows(IN_WIDTH), full(2, 512), full(Bl, N_HEAD, 1, HEAD)),
                 (jax.ShapeDtypeStruct((Bl, S, IN_WIDTH), bf16), jax.ShapeDtypeStruct((2, 512), f32),
                  jax.ShapeDtypeStruct((Bl, N_HEAD, 1, HEAD), f32)),
                 scratch=[pltpu.VMEM((Bl, N_HEAD, HEAD, HEAD), f32)], sem=("arbitrary",))(
        proj, proj, proj, proj, lb_logits, ng, states, db, dzuv, dxq, dgl)


def _attn_probs(q, k):
    s = _raw_dot(q, k, "nt") * (HEAD ** -0.5)
    e = jnp.exp(s - jnp.max(s, axis=-1, keepdims=True))
    return e / jnp.sum(e, axis=-1, keepdims=True)


def _attn_specs(S, tq):
    nq = S // tq
    q = pl.BlockSpec((tq, 512), lambda b, i: (b * nq + i, C_XQ // 512))
    kv = pl.BlockSpec((1, MEM_LEN, 1024), lambda b, i: (b, 0, 0))
    o = pl.BlockSpec((tq, 512), lambda b, i: (b * nq + i, 0))
    return nq, q, kv, o


def _attn_fwd(proj, kv, Bl, S):
    tq = _pick(S, (512, 256, 128))
    nq, qs, kvs, os_ = _attn_specs(S, tq)

    def body(q_ref, kv_ref, o_ref):
        for h in range(N_HEAD):
            sl = slice(h * HEAD, (h + 1) * HEAD)
            p = _attn_probs(q_ref[:, sl], kv_ref[0, :, sl])
            o_ref[:, sl] = _raw_dot(p, kv_ref[0, :, 512 + h * HEAD:512 + (h + 1) * HEAD], "nn").astype(bf16)

    return _call(body, "attn_fwd", (Bl, nq), [qs, kvs], os_, jax.ShapeDtypeStruct((Bl * S, 512), bf16),
                 sem=("parallel", "parallel"))(proj, kv)


def _attn_bwd(proj, kv, dc, Bl, S):
    tq = _pick(S, (512, 256, 128))
    nq, qs, kvs, os_ = _attn_specs(S, tq)

    def body(q_ref, kv_ref, do_ref, dq_ref, dkv_ref):
        @pl.when(pl.program_id(1) == 0)
        def _():
            dkv_ref[...] = jnp.zeros_like(dkv_ref)

        for h in range(N_HEAD):
            sl = slice(h * HEAD, (h + 1) * HEAD)
            vsl = slice(512 + h * HEAD, 512 + (h + 1) * HEAD)
            q, k, v, do = q_ref[:, sl], kv_ref[0, :, sl], kv_ref[0, :, vsl], do_ref[:, sl]
            p = _attn_probs(q, k)
            dkv_ref[0, :, vsl] += _raw_dot(p, do, "tn")
            dp = _raw_dot(do, v, "nt")
            ds = p * (dp - jnp.sum(dp * p, axis=-1, keepdims=True)) * (HEAD ** -0.5)
            dq_ref[:, sl] = _raw_dot(ds, k, "nn").astype(bf16)
            dkv_ref[0, :, sl] += _raw_dot(ds, q, "tn")

    return _call(body, "attn_bwd", (Bl, nq), [qs, kvs, os_], (os_, kvs),
                 (jax.ShapeDtypeStruct((Bl * S, 512), bf16), jax.ShapeDtypeStruct((Bl, MEM_LEN, 1024), f32)),
                 sem=("arbitrary", "arbitrary"))(proj, kv, dc)


def _gate_specs(tm):
    half = D_MODEL // 2
    return [pl.BlockSpec((tm, half), functools.partial(lambda i, c: (i, c), c=(C_GL + n * D_MODEL) // half + k))
            for n in range(3) for k in range(2)]


def _merge_out_norm_fwd(branches, wb, proj, w_out, x, g):
    T = proj.shape[0]
    tm = _pick(T, (512, 256, 128))

    def body(a_ref, b_ref, c_ref, w0, w1, w2, g0a, g0b, g1a, g1b, g2a, g2b, wo_ref, x_ref, g_ref, m_ref, x1_ref, h_ref):
        acc = jnp.zeros((tm, D_MODEL), f32)
        for x_n, w_ref, ga, gb in ((a_ref, w0, g0a, g0b), (b_ref, w1, g1a, g1b), (c_ref, w2, g2a, g2b)):
            gate = jax.nn.sigmoid(jnp.concatenate([ga[...], gb[...]], axis=-1).astype(f32))
            acc = acc + gate * _raw_dot(x_n[...], w_ref[...], "nn")
        merged = acc.astype(bf16)
        m_ref[...] = merged
        x1 = x_ref[...] + _raw_dot(merged, wo_ref[...], "nn")
        x1_ref[...] = x1
        y = x1 * lax.rsqrt(jnp.mean(x1 * x1, axis=-1, keepdims=True) + EPS) * g_ref[...]
        h_ref[...] = y.astype(bf16)

    br = pl.BlockSpec((tm, 512), lambda i: (i, 0))
    w = pl.BlockSpec((512, D_MODEL), lambda i: (0, 0))
    row = pl.BlockSpec((tm, D_MODEL), lambda i: (i, 0))
    return _call(body, "merge_out_norm_fwd", (T // tm,),
                 [br, br, br, w, w, w, *_gate_specs(tm), pl.BlockSpec((D_MODEL, D_MODEL), lambda i: (0, 0)), row,
                  pl.BlockSpec((1, D_MODEL), lambda i: (0, 0))],
                 (row, row, row),
                 (jax.ShapeDtypeStruct((T, D_MODEL), bf16), jax.ShapeDtypeStruct((T, D_MODEL), f32),
                  jax.ShapeDtypeStruct((T, D_MODEL), bf16)),
                 sem=("parallel",))(*branches, *wb, *[proj] * 6, w_out, x, g)


def _merge_bwd(branches, wb, proj, merged, dx1, w_out):
    T = proj.shape[0]
    tm = _pick(T, (256, 128))

    def body(a_ref, b_ref, c_ref, w0, w1, w2, g0a, g0b, g1a, g1b, g2a, g2b, m_ref, dx_ref, wo_ref, dgl_ref, d0, d1, d2, gw_ref, gwo_ref):
        @pl.when(pl.program_id(0) == 0)
        def _():
            gw_ref[...] = jnp.zeros_like(gw_ref)
            gwo_ref[...] = jnp.zeros_like(gwo_ref)

        dx = dx_ref[...].astype(bf16)
        gwo_ref[...] += _raw_dot(m_ref[...], dx, "tn")
        dm = _raw_dot(dx, wo_ref[...], "nt")
        for n, (x_ref, w_ref, ga, gb, d_ref) in enumerate(((a_ref, w0, g0a, g0b, d0), (b_ref, w1, g1a, g1b, d1), (c_ref, w2, g2a, g2b, d2))):
            x, w = x_ref[...], w_ref[...]
            up = _raw_dot(x, w, "nn")
            sg = jax.nn.sigmoid(jnp.concatenate([ga[...], gb[...]], axis=-1).astype(f32))
            dgl_ref[n] = (dm * up * sg * (1.0 - sg)).astype(bf16)
            dup = (dm * sg).astype(bf16)
            d_ref[...] = _raw_dot(dup, w, "nt").astype(bf16)
            gw_ref[n] += _raw_dot(x, dup, "tn")

    br = pl.BlockSpec((tm, 512), lambda i: (i, 0))
    w = pl.BlockSpec((512, D_MODEL), lambda i: (0, 0))
    sh = jax.ShapeDtypeStruct((T, 512), bf16)
    row = pl.BlockSpec((tm, D_MODEL), lambda i: (i, 0))
    square = pl.BlockSpec((D_MODEL, D_MODEL), lambda i: (0, 0))
    outs = _call(body, "merge_bwd", (T // tm,), [br, br, br, w, w, w, *_gate_specs(tm), row, row, square],
                 (pl.BlockSpec((3, tm, D_MODEL), lambda i: (0, i, 0)), br, br, br, pl.BlockSpec((3, 512, D_MODEL), lambda i: (0, 0, 0)), square),
                 (jax.ShapeDtypeStruct((3, T, D_MODEL), bf16), sh, sh, sh, jax.ShapeDtypeStruct((3, 512, D_MODEL), f32),
                  jax.ShapeDtypeStruct((D_MODEL, D_MODEL), f32)),
                 sem=("arbitrary",))(*branches, *wb, *[proj] * 6, merged, dx1, w_out)
    return outs[0], outs[1:4], outs[4], outs[5]


CONV_TC = 256


def _shift_down(a, k):
    r = pltpu.roll(a, k, 0)
    row = lax.broadcasted_iota(jnp.int32, (8, a.shape[1]), 0)
    return jnp.concatenate([jnp.where(row >= k, r[:8], 0.0), r[8:]], axis=0)


def _shift_up(a, k):
    n = a.shape[0]
    r = pltpu.roll(a, n - k, 0)
    row = lax.broadcasted_iota(jnp.int32, (8, a.shape[1]), 0)
    return jnp.concatenate([r[:n - 8], jnp.where(row < 8 - k, r[n - 8:], 0.0)], axis=0)


def _conv_pre(a, a1, a2, cw, cb):
    return cb + cw[0:1] * a2 + cw[1:2] * a1 + cw[2:3] * a


def _up_conv_fwd(h2, w_up_t, cw, cb):
    Bl, S, Dd = h2.shape
    nc = D_FF // CONV_TC

    def body(h_ref, wa_ref, wb_ref, cw_ref, cb_ref, a_ref, b_ref, o_ref):
        a16 = _raw_dot(h_ref[0], wa_ref[...], "nt").astype(bf16)
        b16 = _raw_dot(h_ref[0], wb_ref[...], "nt").astype(bf16)
        a_ref[0], b_ref[0] = a16, b16
        a = a16.astype(f32)
        ac = _conv_pre(a, _shift_down(a, 1), _shift_down(a, 2), cw_ref[...], cb_ref[...])
        o_ref[0] = (jax.nn.silu(ac) * b16.astype(f32)).astype(bf16)

    seq = pl.BlockSpec((1, S, CONV_TC), lambda b, c: (b, 0, c))
    sh = jax.ShapeDtypeStruct((Bl, S, D_FF), bf16)
    return _call(body, "up_conv_fwd", (Bl, nc),
                 [pl.BlockSpec((1, S, Dd), lambda b, c: (b, 0, 0)), pl.BlockSpec((CONV_TC, Dd), lambda b, c: (c, 0)),
                  pl.BlockSpec((CONV_TC, Dd), lambda b, c: (nc + c, 0)), pl.BlockSpec((3, CONV_TC), lambda b, c: (0, c)),
                  pl.BlockSpec((1, CONV_TC), lambda b, c: (0, c))],
                 (seq, seq, seq), (sh, sh, sh), sem=("parallel", "parallel"))(h2, w_up_t, w_up_t, cw, cb)


def _down_conv_bwd(dx2, w_down, a, b, cw, cb):
    Bl, S, Dd = dx2.shape
    nc = D_FF // CONV_TC

    def body(dx_ref, wd_ref, a_ref, b_ref, cw_ref, cb_ref, da_ref, db_ref, dcw_ref, dcb_ref):
        dact = _raw_dot(dx_ref[0], wd_ref[...], "nt").astype(bf16).astype(f32)
        a, cw = a_ref[0].astype(f32), cw_ref[...]
        a1, a2 = _shift_down(a, 1), _shift_down(a, 2)
        ac = _conv_pre(a, a1, a2, cw, cb_ref[...])
        sg = jax.nn.sigmoid(ac)
        gated = dact * sg
        db_ref[0] = (gated * ac).astype(bf16)
        dac = gated * b_ref[0].astype(f32) * (1.0 + ac * (1.0 - sg))
        da_ref[0] = (cw[2:3] * dac + cw[1:2] * _shift_up(dac, 1) + cw[0:1] * _shift_up(dac, 2)).astype(bf16)
        dcw_ref[0, 0:1, :] = jnp.sum(dac * a2, axis=0, keepdims=True)
        dcw_ref[0, 1:2, :] = jnp.sum(dac * a1, axis=0, keepdims=True)
        dcw_ref[0, 2:3, :] = jnp.sum(dac * a, axis=0, keepdims=True)
        dcb_ref[0] = jnp.sum(dac, axis=0, keepdims=True)

    seq = pl.BlockSpec((1, S, CONV_TC), lambda b_, c: (b_, 0, c))
    sh = jax.ShapeDtypeStruct((Bl, S, D_FF), bf16)
    return _call(body, "down_conv_bwd", (Bl, nc),
                 [pl.BlockSpec((1, S, Dd), lambda b_, c: (b_, 0, 0)), pl.BlockSpec((CONV_TC, Dd), lambda b_, c: (c, 0)), seq, seq,
                  pl.BlockSpec((3, CONV_TC), lambda b_, c: (0, c)), pl.BlockSpec((1, CONV_TC), lambda b_, c: (0, c))],
                 (seq, seq, pl.BlockSpec((1, 3, CONV_TC), lambda b_, c: (b_, 0, c)), pl.BlockSpec((1, 1, CONV_TC), lambda b_, c: (b_, 0, c))),
                 (sh, sh, jax.ShapeDtypeStruct((Bl, 3, D_FF), f32), jax.ShapeDtypeStruct((Bl, 1, D_FF), f32)),
                 sem=("parallel", "parallel"))(dx2, w_down, a, b, cw, cb)


def _local_step(x, mem, target, p, w_in_t, late_b, late_c, send, settle):
    Bl, S, Dd = x.shape
    T = Bl * S
    x2d, t2d, mem2d = x.reshape(T, Dd), target.reshape(T, Dd), mem.reshape(Bl * MEM_LEN, Dd)
    b_st = jnp.pad(p["b_spatial"].T, ((0, 0), (0, 128 - N_HEAD)))
    lbl = p["lb_logits"]

    h = _rms_fwd(x2d, p["norm1_g"], "norm1_fwd")
    proj = _mm(h, w_in_t, "nt", bf16, "proj_fwd", 1024, 1664)
    a_out = _gmlp_fwd(proj, p["ln_v_g"], p["ln_v_b"], p["w_spatial"], b_st)
    proj3 = proj.reshape(Bl, S, IN_WIDTH)
    b_out, states = _hgrn_fwd(proj3, lbl, p["hgrn_norm_g"], Bl, S)
    b_out = b_out.reshape(T, 512)
    memn = _rms_fwd(mem2d, p["mem_norm_g"], "memnorm_fwd")
    w = late_b(b_out)
    wb = w["w_branch"]
    kv = _mm(memn, w["w_mem_kv"], "nn", f32, "kv_fwd", 512, 1024).reshape(Bl, MEM_LEN, 2 * 512)
    c_out = _attn_fwd(proj, kv, Bl, S)
    branches = (a_out, b_out, c_out)
    merged, x1, h2 = _merge_out_norm_fwd(branches, wb, proj, w["w_out"], x2d, p["norm2_g"])
    w.update(late_c(h2))
    ffn_a, ffn_b, act = _up_conv_fwd(h2.reshape(Bl, S, Dd), w["w_up_t"], w["conv_w"], p["conv_b"])
    act = act.reshape(T, D_FF)
    loss_part, dx2, dx2_16, g_final = _down_final_loss(act, w["w_down"], x1, p["final_g"], t2d)

    g_w_down = _mm(act, dx2_16, "tn", bf16, "down_dw", 1408, 1024, 1024)
    da, db, g_conv_w, g_conv_b = _down_conv_bwd(dx2_16.reshape(Bl, S, Dd), w["w_down"], ffn_a, ffn_b, w["conv_w"], p["conv_b"])
    da, db = da.reshape(T, D_FF), db.reshape(T, D_FF)
    g_w_up_t = _mm(da, h2, "tn", bf16, "up_dw_a", 1408, 1024, 1024, into=(lax.empty((2 * D_FF, D_MODEL), bf16), 0))
    g_w_up_t = _mm(db, h2, "tn", bf16, "up_dw_b", 1408, 1024, 1024, into=(g_w_up_t, D_FF))
    send("c", dict(w_up=g_w_up_t, conv_w=jnp.sum(g_conv_w, axis=0), w_down=g_w_down))
    dx1, g_norm2 = _mm_rms_bwd([(da, w["w_up_t"], 0), (db, w["w_up_t"], 1)], x1, p["norm2_g"], dx2, "up_dx_norm2_bwd", 256)

    dgl, dbr, g_w_branch, g_w_out = _merge_bwd(branches, wb, proj, merged, dx1, w["w_out"])
    dxq, dkv = _attn_bwd(proj, kv, dbr[2], Bl, S)
    dkv = dkv.reshape(Bl * MEM_LEN, 2 * 512)
    g_w_kv = _mm(memn, dkv, "tn", bf16, "kv_dw", 1024, 1024, 512)
    send("b", dict(w_mem_kv=g_w_kv, w_branch=g_w_branch, w_out=g_w_out))
    dmemn = _mm(dkv, w["w_mem_kv"], "nt", f32, "kv_dx", 512, 1024)
    _, g_mem_norm = _rms_bwd(mem2d, p["mem_norm_g"], dmemn, "memnorm_bwd")
    dzuv, g_ln_g, g_ln_b, g_w_sp, g_b_sp = _gmlp_bwd(proj, p["ln_v_g"], p["ln_v_b"], p["w_spatial"], b_st, dbr[0])
    dproj, g_lbl, g_ng = _hgrn_bwd(proj3, lbl, p["hgrn_norm_g"], states, dbr[1].reshape(Bl, S, 512), dzuv.reshape(Bl, S, -1),
                                   dxq.reshape(Bl, S, -1), dgl.reshape(3, Bl, S, Dd), Bl, S)
    dproj = dproj.reshape(T, IN_WIDTH)
    g_w_in_t = _mm(dproj, h, "tn", bf16, "proj_dw", 512, 1024)
    send("a", dict(w_in=g_w_in_t))
    dx, g_norm1 = _mm_rms_bwd([(settle(dproj), w_in_t, 0)], x2d, p["norm1_g"], dx1, "proj_dx_norm1_bwd", 256)

    gs = dict(w_spatial=g_w_sp, norm1_g=g_norm1, mem_norm_g=g_mem_norm, norm2_g=g_norm2, final_g=g_final, lb_logits=g_lbl,
              ln_v_g=g_ln_g, ln_v_b=g_ln_b, b_spatial=g_b_sp, hgrn_norm_g=g_ng, conv_b=g_conv_b)
    return loss_part, dx.reshape(Bl, S, Dd), gs


def _coords():
    return lax.axis_index("x"), lax.axis_index("y"), lax.axis_index("c")


def _slot(dev):
    return 4 * dev[0] + 2 * dev[1] + dev[2]


def _comm_call(body, name, arrays, out_shapes, n_sem):
    n = len(arrays)
    hbm = pl.BlockSpec(memory_space=pl.ANY)
    return pl.pallas_call(
        body, name=name, out_shape=out_shapes, in_specs=[hbm] * n, out_specs=[hbm] * n,
        scratch_shapes=[pltpu.SemaphoreType.DMA((n_sem, n)), pltpu.SemaphoreType.DMA((n_sem, n)), pltpu.SemaphoreType.DMA((n,))])(*arrays)


def _all_gather(blocks, name):
    n = len(blocks)

    def body(*refs):
        x_refs, o_refs, (send_sems, recv_sems, local_sems) = refs[:n], refs[n:2 * n], refs[2 * n:]
        x, y, c = _coords()
        me, sibling = (x, y, c), (x, y, 1 - c)
        chips = [(1 - x, y), (x, 1 - y), (1 - x, 1 - y)]

        def copy(a, k, block_dev, to, from_input=False):
            dst = o_refs[a].at[_slot(block_dev)]
            return pltpu.make_async_remote_copy(src_ref=x_refs[a] if from_input else dst, dst_ref=dst, send_sem=send_sems.at[k, a],
                                                recv_sem=recv_sems.at[k, a], device_id=to, device_id_type=MESH)

        mine = [pltpu.make_async_copy(x_refs[a], o_refs[a].at[_slot(me)], local_sems.at[a]) for a in range(n)]
        first = [copy(a, 0, me, sibling, True) for a in range(n)]
        first += [copy(a, 1 + j, me, (*chip, c), True) for j, chip in enumerate(chips) for a in range(n)]
        for cp in mine + first:
            cp.start()
        passed = []
        for j, chip in enumerate(chips):
            for a in range(n):
                copy(a, 1 + j, (*chip, c), me).wait_recv()
                fwd = copy(a, 4 + j, (*chip, c), sibling)
                fwd.start()
                passed.append(fwd)
        for a in range(n):
            copy(a, 0, sibling, me).wait_recv()
        for j, chip in enumerate(chips):
            for a in range(n):
                copy(a, 4 + j, (*chip, 1 - c), me).wait_recv()
        for cp in first + passed:
            cp.wait_send()
        for cp in mine:
            cp.wait()

    return _comm_call(body, name, blocks, [jax.ShapeDtypeStruct((N_DEV,) + b.shape, b.dtype) for b in blocks], 7)


_REL = [(0, 0, 1), (0, 1, 0), (0, 1, 1), (1, 0, 0), (1, 0, 1), (1, 1, 0), (1, 1, 1)]


def _seq_exchange(arrays, gather, name, collective_id):
    n = len(arrays)
    hbm = pltpu.MemorySpace.HBM
    srcs = [jax.new_ref(a, memory_space=hbm) for a in arrays]
    lands = [jax.empty_ref(jax.ShapeDtypeStruct(((N_DEV,) + a.shape) if gather else a.shape, a.dtype), memory_space=hbm) for a in arrays]

    @pl.kernel(mesh=plsc.ScalarSubcoreMesh(axis_name="sequencer", num_cores=1), name=name,
               scratch_types=(pltpu.SemaphoreType.DMA((7, n)), pltpu.SemaphoreType.DMA((7, n)), pltpu.SemaphoreType.DMA((n,))),
               compiler_params=pltpu.CompilerParams(collective_id=collective_id))
    def launch(send, recv, local):
        x, y, c = _coords()
        me = (x, y, c)
        peers = [(x ^ dx, y ^ dy, c ^ dc) for dx, dy, dc in _REL]
        barrier = pltpu.get_barrier_semaphore()
        for peer in peers:
            pl.semaphore_signal(barrier, inc=1, device_id=peer, device_id_type=MESH)
        pl.semaphore_wait(barrier, len(peers))

        def copy(a, k, peer, arrival):
            return pltpu.make_async_remote_copy(
                src_ref=srcs[a] if gather else srcs[a].at[_slot(peer)], dst_ref=lands[a].at[_slot(peer if arrival else me)],
                send_sem=send.at[k, a], recv_sem=recv.at[k, a], device_id=peer, device_id_type=MESH)

        mine = [pltpu.make_async_copy(srcs[a] if gather else srcs[a].at[_slot(me)], lands[a].at[_slot(me)], local.at[a])
                for a in range(n)]
        out = [copy(a, k, peer, False) for a in range(n) for k, peer in enumerate(peers)]
        for cp in mine + out:
            cp.start()
        for a in range(n):
            for k, peer in enumerate(peers):
                copy(a, k, peer, True).wait_recv()
        for cp in out:
            cp.wait_send()
        for cp in mine:
            cp.wait()

    launch()
    return [land[...] for land in lands]


def _adam_math(w, g, m, v):
    m_ = ADAM_B1 * m + (1.0 - ADAM_B1) * g
    v_ = ADAM_B2 * v + (1.0 - ADAM_B2) * jnp.square(g)
    m_hat = m_ / (1.0 - ADAM_B1 ** ADAM_STEP)
    v_hat = v_ / (1.0 - ADAM_B2 ** ADAM_STEP)
    return -ADAM_LR * (m_hat / (jnp.sqrt(v_hat) + ADAM_EPS) + ADAM_WD * w), m_, v_


def _reduce_adamw(parts, w, m, v, name):
    _, R, L = parts.shape
    tr = _pick(R, (256, 208, 176, 128, 64, 32, 16, 8))

    def body(p_ref, w_ref, m_ref, v_ref, g_ref, d_ref, nm_ref, nv_ref):
        g = p_ref[0].astype(f32)
        for i in range(1, N_DEV):
            g = g + p_ref[i].astype(f32)
        g_ref[...] = g
        d_ref[...], nm_ref[...], nv_ref[...] = _adam_math(w_ref[...], g, m_ref[...], v_ref[...])

    blk = pl.BlockSpec((tr, L), lambda i: (i, 0))
    sh = jax.ShapeDtypeStruct((R, L), f32)
    return _call(body, name, (R // tr,), [pl.BlockSpec((N_DEV, tr, L), lambda i: (0, i, 0)), blk, blk, blk], (blk,) * 4, (sh,) * 4,
                 sem=("parallel",))(parts, w, m, v)


SMALL = (("w_spatial", (512, 128), 0), ("norm1_g", (1, 1024), 512), ("mem_norm_g", (1, 1024), 520), ("norm2_g", (1, 1024), 528),
         ("final_g", (1, 1024), 536), ("lb_logits", (2, 512), 544), ("ln_v_g", (1, 512), 552), ("ln_v_b", (1, 512), 556),
         ("b_spatial", (4, 128), 560), ("hgrn_norm_g", (1, 128), 564), ("conv_b", (1, 2816), 565))
LOSS_ROW, SMALL_USED, SMALL_ROWS = 587, 588, 640


def _segments(shape, base):
    r, n = shape
    per = n // 128
    return [(base + i * per + j, i, slice(j * 128, (j + 1) * 128)) for i in range(r) for j in range(per)]


def _pack_small(gs, loss_part):
    names = [n for n, _, _ in SMALL]

    def body(*refs):
        src, loss_ref, o_ref = dict(zip(names, refs[:-2])), refs[-2], refs[-1]
        o_ref[SMALL_USED:SMALL_ROWS, :] = jnp.zeros((SMALL_ROWS - SMALL_USED, 128), f32)
        o_ref[LOSS_ROW:LOSS_ROW + 1, :] = loss_ref[...]
        for name, shape, base in SMALL:
            ref = src[name]
            if name == "w_spatial":
                o_ref[base:base + 512, :] = ref[...].reshape(512, 128)
            elif name == "b_spatial":
                o_ref[base:base + 4, :] = ref[0:4, :]
            elif name == "conv_b":
                per_example = functools.reduce(lambda u, v_: u + v_, [ref[b] for b in range(ref.shape[0])])
                for row, i, sl in _segments(shape, base):
                    o_ref[row:row + 1, :] = per_example[i:i + 1, sl]
            elif name == "hgrn_norm_g":
                per_head = [ref[b, h] for b in range(ref.shape[0]) for h in range(N_HEAD)]
                o_ref[base:base + 1, :] = functools.reduce(lambda u, v_: u + v_, per_head)
            else:
                for row, i, sl in _segments(shape, base):
                    o_ref[row:row + 1, :] = ref[i:i + 1, sl]

    return pl.pallas_call(body, name="pack_small", out_shape=jax.ShapeDtypeStruct((SMALL_ROWS, 128), f32))(
        *[gs[n] for n in names], loss_part)


def _small_update(gathered, w, m, v):
    names = [n for n, _, _ in SMALL]
    k = len(names)

    def body(*refs):
        p_ref = refs[0]
        ins = [dict(zip(names, refs[1 + i * k:1 + (i + 1) * k])) for i in range(3)]
        outs = [dict(zip(names, refs[1 + (3 + i) * k:1 + (4 + i) * k])) for i in range(4)]
        loss_ref, gsum = refs[-2], refs[-1]
        g = p_ref[0]
        for i in range(1, N_DEV):
            g = g + p_ref[i]
        gsum[...] = g
        loss_ref[...] = gsum[LOSS_ROW:LOSS_ROW + 1, :]
        for name, shape, base in SMALL:
            if name == "w_spatial":
                where = [(slice(base, base + 512), (slice(None), slice(None)))]
            else:
                where = [(slice(row, row + 1), (slice(i, i + 1), sl)) for row, i, sl in _segments(shape, base)]
            for rows, at in where:
                g_ = gsum[rows, :]
                d_, m_, v_ = _adam_math(ins[0][name][at], g_, ins[1][name][at], ins[2][name][at])
                for o, val in zip(outs, (g_, d_, m_, v_)):
                    o[name][at] = val

    args = [gathered] + [d[n] for d in (w, m, v) for n in names]
    out_shapes = [jax.ShapeDtypeStruct(shape, f32) for _ in range(4) for _, shape, _ in SMALL] + [jax.ShapeDtypeStruct((1, 128), f32)]
    outs = pl.pallas_call(body, name="small_update", out_shape=out_shapes, scratch_shapes=[pltpu.VMEM((SMALL_ROWS, 128), f32)])(*args)
    return [dict(zip(names, outs[i * k:(i + 1) * k])) for i in range(4)], outs[-1]


def _cols_full(g):
    return jnp.moveaxis(g, 0, -2).reshape(g.shape[1:-1] + (N_DEV * g.shape[-1],))


def _cols_parts(full):
    n = full.shape[-1] // N_DEV
    return jnp.moveaxis(full.reshape(full.shape[:-1] + (N_DEV, n)), -2, 0)


def kernel(x, mem, norm1_g, w_in, ln_v_g, ln_v_b, w_spatial, b_spatial, lb_logits, hgrn_norm_g, mem_norm_g, w_mem_kv, w_branch, w_out, norm2_g, w_up, conv_w, conv_b, w_down, final_g, loss_target, m_norm1_g, m_w_in, m_ln_v_g, m_ln_v_b, m_w_spatial, m_b_spatial, m_lb_logits, m_hgrn_norm_g, m_mem_norm_g, m_w_mem_kv, m_w_branch, m_w_out, m_norm2_g, m_w_up, m_conv_w, m_conv_b, m_w_down, m_final_g, v_norm1_g, v_w_in, v_ln_v_g, v_ln_v_b, v_w_spatial, v_b_spatial, v_lb_logits, v_hgrn_norm_g, v_mem_norm_g, v_w_mem_kv, v_w_branch, v_w_out, v_norm2_g, v_w_up, v_conv_w, v_conv_b, v_w_down, v_final_g):
    given = dict(locals())
    order = ("norm1_g", "w_in", "ln_v_g", "ln_v_b", "w_spatial", "b_spatial", "lb_logits", "hgrn_norm_g", "mem_norm_g",
             "w_mem_kv", "w_branch", "w_out", "norm2_g", "w_up", "conv_w", "conv_b", "w_down", "final_g")
    groups = dict(a=("w_in",), b=("w_mem_kv", "w_branch", "w_out"), c=("w_up", "conv_w", "w_down"))

    by_rows = ("w_in", "w_up")
    shard_of = lambda n, prefix="": jnp.swapaxes(given[prefix + n][0], 0, 1) if n in by_rows else given[prefix + n][0]
    wire = {n: shard_of(n).astype(f32 if n == "conv_w" else bf16) for ns in groups.values() for n in ns}
    w_in_full = _all_gather([wire["w_in"]], "gather_w_in")[0].reshape(IN_WIDTH, D_MODEL)
    w_in_full, wire_b, wire_c = lax.optimization_barrier((w_in_full, [wire[n] for n in groups["b"]], [wire[n] for n in groups["c"]]))
    rest_b = _seq_exchange(wire_b, True, "gather_b", 1)
    rest_c = _seq_exchange(wire_c, True, "gather_c", 6)

    def late_b(after):
        _, (kv_, br_, out_) = lax.optimization_barrier((after, tuple(rest_b)))
        br_ = _cols_full(br_)
        return dict(w_mem_kv=kv_.reshape(D_MODEL, 2 * 512), w_branch=[br_[n] for n in range(3)], w_out=out_.reshape(D_MODEL, D_MODEL))

    def late_c(after):
        _, (up_, cw_, down_) = lax.optimization_barrier((after, tuple(rest_c)))
        return dict(w_up_t=up_.reshape(2 * D_FF, D_MODEL), conv_w=_cols_full(cw_), w_down=down_.reshape(D_FF, D_MODEL))

    to_parts = dict(w_in=lambda g_: g_.reshape(N_DEV, -1, D_MODEL), w_up=lambda g_: g_.reshape(N_DEV, -1, D_MODEL), conv_w=_cols_parts,
                    w_branch=lambda g_: _cols_parts(g_.astype(bf16)).reshape(N_DEV, -1, 128),
                    w_mem_kv=lambda g_: g_.reshape(N_DEV, -1, 2 * 512), w_out=lambda g_: g_.astype(bf16).reshape(N_DEV, -1, D_MODEL),
                    w_down=lambda g_: g_.reshape(N_DEV, -1, D_MODEL))
    scatters = {}

    def send(tag, grads_):
        parts = [to_parts[n](grads_[n]) for n in groups[tag]]
        scatters[tag] = _seq_exchange(parts, False, f"scatter_{tag}", dict(a=2, b=4, c=5)[tag])

    small_2d = lambda prefix: {n: given[prefix + n].reshape(shape) for n, shape, _ in SMALL}
    p = small_2d("")
    p["w_spatial"] = w_spatial[0]
    updates = {}

    def update(tag):
        for n, parts in zip(groups[tag], scatters[tag]):
            state = [shard_of(n, pre) for pre in ("", "m_", "v_")]
            res = _reduce_adamw(parts, *[a.reshape(-1, a.shape[-1]) for a in state], "adamw_" + n)
            updates[n] = [jnp.swapaxes(r, 0, 1) for r in res] if n in by_rows else res

    def settle(chain):
        update("c")
        update("b")
        early = groups["c"] + groups["b"]
        chain, tied = lax.optimization_barrier((chain, [updates[n] for n in early]))
        updates.update(zip(early, tied))
        return chain

    loss_part, grad_x, gs = _local_step(x, mem, loss_target, p, w_in_full, late_b, late_c, send, settle)

    gathered = _seq_exchange([_pack_small(gs, loss_part)], True, "gather_small", 3)[0]

    update("a")
    grads, delta, new_m, new_v = {}, {}, {}, {}
    for n, res in updates.items():
        grads[n], delta[n], new_m[n], new_v[n] = [r.reshape(given[n].shape) for r in res]

    small_results, loss_row = _small_update(gathered, small_2d(""), small_2d("m_"), small_2d("v_"))
    for dst, res in zip((grads, delta, new_m, new_v), small_results):
        for n, _, _ in SMALL:
            dst[n] = res[n].reshape(given[n].shape)
    loss = loss_row[0, 0]

    return (loss, grad_x, *[grads[n] for n in order], *[delta[n] for n in order], *[new_m[n] for n in order],
            *[new_v[n] for n in order])
```

```python
import functools

import jax
import jax.numpy as jnp
from jax import lax
from jax.experimental import pallas as pl
from jax.experimental.pallas import tpu as pltpu
from jax.experimental.pallas import tpu_sc as plsc

f32 = jnp.float32
bf16 = jnp.bfloat16

N_DEV = 8
D_MODEL = 1024
EPS = 1e-6
GM_CHUNK = 128
HG_CHUNK = 64
HEAD = 128
N_HEAD = 4
MEM_LEN = 256
D_FF = 2816
IN_WIDTH = 6656
C_ZU, C_HQ, C_HF, C_HI, C_HG, C_XQ, C_GL = 0, 1024, 1536, 2048, 2560, 3072, 3584
ADAM_LR, ADAM_B1, ADAM_B2, ADAM_EPS, ADAM_WD, ADAM_STEP = 0.001, 0.9, 0.999, 1e-08, 0.01, 10
VMEM_LIMIT = 56 * 1024 * 1024
MESH = pl.DeviceIdType.MESH


def _pick(n, cands):
    for c in cands:
        if n % c == 0:
            return c
    return n


def _call(body, name, grid, in_specs, out_specs, out_shape, scratch=(), sem=None, **cp):
    params = dict(vmem_limit_bytes=VMEM_LIMIT, **cp)
    if sem is not None:
        params["dimension_semantics"] = sem
    return pl.pallas_call(
        body, name=name, grid=grid, in_specs=in_specs, out_specs=out_specs, out_shape=out_shape,
        scratch_shapes=list(scratch), compiler_params=pltpu.CompilerParams(**params))


_DN = {"nn": (((1,), (0,)), ((), ())), "nt": (((1,), (1,)), ((), ())), "tn": (((0,), (0,)), ((), ()))}


def _raw_dot(a, b, mode):
    return lax.dot_general(a.astype(bf16), b.astype(bf16), _DN[mode], preferred_element_type=f32)


@jax.custom_vjp
def _dot_nn(a, b):
    return _raw_dot(a, b, "nn")


_dot_nn.defvjp(lambda a, b: (_raw_dot(a, b, "nn"), (a, b)),
               lambda r, g: (_raw_dot(g, r[1], "nt"), _raw_dot(r[0], g, "tn")))


@jax.custom_vjp
def _dot_nt(a, b):
    return _raw_dot(a, b, "nt")


_dot_nt.defvjp(lambda a, b: (_raw_dot(a, b, "nt"), (a, b)),
               lambda r, g: (_raw_dot(g, r[1], "nn"), _raw_dot(g, r[0], "tn")))


@jax.custom_vjp
def _dot_tn(a, b):
    return _raw_dot(a, b, "tn")


_dot_tn.defvjp(lambda a, b: (_raw_dot(a, b, "tn"), (a, b)),
               lambda r, g: (_raw_dot(r[1], g, "nt"), _raw_dot(r[0], g, "nn")))


def _tri(n, lower):
    r = lax.broadcasted_iota(jnp.int32, (n, n), 0)
    c = lax.broadcasted_iota(jnp.int32, (n, n), 1)
    return ((c <= r) if lower else (c >= r)).astype(f32)


def _sel_dot(sel, x, mode, x_first=False, pieces=3):
    sel = sel.astype(bf16)
    out, rest = None, x
    for p in range(pieces):
        piece = rest.astype(bf16)
        part = lax.dot_general(*((piece, sel) if x_first else (sel, piece)), _DN[mode], preferred_element_type=f32)
        out = part if out is None else out + part
        if p + 1 < pieces:
            rest = rest - piece.astype(f32)
    return out


def _egrad(fn, x, ct):
    return jax.vjp(fn, x)[1](ct)[0]


def _mm(a, b, mode, out_dtype, name, tm, tn, tk=None, residual=None, into=None, n_tiles=None):
    if mode == "nn":
        (M, K), (_, N) = a.shape, b.shape
    elif mode == "nt":
        (M, K), (N, _) = a.shape, b.shape
    else:
        (K, M), (_, N) = a.shape, b.shape
    j0 = 0
    if n_tiles is not None:
        assert mode != "nt" and N % tn == 0 and residual is None
        j0, N = n_tiles[0], n_tiles[1] * tn
    tm, tn = min(tm, M), min(tn, N)
    tk = K if tk is None else min(tk, K)
    assert M % tm == 0 and N % tn == 0 and K % tk == 0, (name, M, N, K, tm, tn, tk)
    nk = K // tk

    def body(*refs):
        acc_ref = refs[-1] if nk > 1 else None
        refs = refs[:-1] if nk > 1 else refs
        if residual is None:
            a_ref, b_ref, *_, o_ref = refs
        else:
            a_ref, b_ref, r_ref, o_ref = refs

        def finish(r):
            if residual is not None:
                r = r + r_ref[...]
            o_ref[...] = r.astype(out_dtype)

        part = _raw_dot(a_ref[...], b_ref[...], mode)
        if nk == 1:
            finish(part)
            return
        k = pl.program_id(2)

        @pl.when(k == 0)
        def _():
            acc_ref[...] = part

        @pl.when((k > 0) & (k < nk - 1))
        def _():
            acc_ref[...] += part

        @pl.when(k == nk - 1)
        def _():
            finish(acc_ref[...] + part)

    a_spec = {"nn": pl.BlockSpec((tm, tk), lambda i, j, k: (i, k)),
              "nt": pl.BlockSpec((tm, tk), lambda i, j, k: (i, k)),
              "tn": pl.BlockSpec((tk, tm), lambda i, j, k: (k, i))}[mode]
    b_spec = {"nn": pl.BlockSpec((tk, tn), lambda i, j, k: (k, j + j0)),
              "nt": pl.BlockSpec((tn, tk), lambda i, j, k: (j, k)),
              "tn": pl.BlockSpec((tk, tn), lambda i, j, k: (k, j + j0))}[mode]
    o_spec = pl.BlockSpec((tm, tn), lambda i, j, k: (i, j))
    in_specs = [a_spec, b_spec] + ([o_spec] if residual is not None else [])
    args = (a, b) + ((residual,) if residual is not None else ())
    out_shape = jax.ShapeDtypeStruct((M, N), out_dtype)
    extra = {}
    if into is not None:
        assert residual is None and into[1] % tm == 0
        o_spec = pl.BlockSpec((tm, tn), lambda i, j, k: (i + into[1] // tm, j))
        out_shape = jax.ShapeDtypeStruct(into[0].shape, out_dtype)
        in_specs, args = in_specs + [pl.BlockSpec(memory_space=pl.ANY)], args + (into[0],)
        extra = dict(input_output_aliases={2: 0})
    return pl.pallas_call(
        body, name=name, grid=(M // tm, N // tn, nk), in_specs=in_specs, out_specs=o_spec, out_shape=out_shape,
        scratch_shapes=[pltpu.VMEM((tm, tn), f32)] if nk > 1 else [],
        compiler_params=pltpu.CompilerParams(vmem_limit_bytes=VMEM_LIMIT, dimension_semantics=("parallel", "parallel", "arbitrary")),
        **extra)(*args)


def _rms_fwd(x, g, name):
    R, Dd = x.shape
    tr = _pick(R, (512, 256, 128))

    def body(x_ref, g_ref, o_ref):
        xf = x_ref[...]
        o_ref[...] = (xf * lax.rsqrt(jnp.mean(xf * xf, axis=-1, keepdims=True) + EPS) * g_ref[...]).astype(bf16)

    row = pl.BlockSpec((tr, Dd), lambda i: (i, 0))
    return _call(body, name, (R // tr,), [row, pl.BlockSpec((1, Dd), lambda i: (0, 0))], row, jax.ShapeDtypeStruct((R, Dd), bf16),
                 sem=("parallel",))(x, g)


def _rms_bwd(x, g, dh, name, residual=None):
    R, Dd = x.shape
    tr = _pick(R, (512, 256, 128))

    def body(*refs):
        if residual is None:
            x_ref, g_ref, dh_ref, dx_ref, dg_ref = refs
        else:
            x_ref, g_ref, dh_ref, r_ref, dx_ref, dg_ref = refs
        xf = x_ref[...]
        rs = lax.rsqrt(jnp.mean(xf * xf, axis=-1, keepdims=True) + EPS)
        y = xf * rs
        dh_ = dh_ref[...].astype(f32)
        dy = dh_ * g_ref[...]
        dx = rs * (dy - y * jnp.mean(dy * y, axis=-1, keepdims=True))
        if residual is not None:
            dx = dx + r_ref[...]
        dx_ref[...] = dx

        @pl.when(pl.program_id(0) == 0)
        def _():
            dg_ref[...] = jnp.zeros_like(dg_ref)

        dg_ref[...] += jnp.sum(dh_ * y, axis=0, keepdims=True)

    row = pl.BlockSpec((tr, Dd), lambda i: (i, 0))
    vec = pl.BlockSpec((1, Dd), lambda i: (0, 0))
    in_specs = [row, vec, row] + ([row] if residual is not None else [])
    args = (x, g, dh) + ((residual,) if residual is not None else ())
    return _call(body, name, (R // tr,), in_specs, (row, vec),
                 (jax.ShapeDtypeStruct((R, Dd), f32), jax.ShapeDtypeStruct((1, Dd), f32)), sem=("arbitrary",))(*args)


def _mm_rms_bwd(pairs, x, g, residual, name, tm):
    M = x.shape[0]
    Dd = x.shape[1]
    tm = min(tm, M)
    n = len(pairs)

    def body(*refs):
        ab_refs, (x_ref, g_ref, r_ref, dx_ref, dg_ref) = refs[:2 * n], refs[2 * n:]
        dh_ = _raw_dot(ab_refs[0][...], ab_refs[1][...], "nn")
        for k in range(1, n):
            dh_ = dh_ + _raw_dot(ab_refs[2 * k][...], ab_refs[2 * k + 1][...], "nn")
        xf = x_ref[...]
        rs = lax.rsqrt(jnp.mean(xf * xf, axis=-1, keepdims=True) + EPS)
        y = xf * rs
        dy = dh_ * g_ref[...]
        dx_ref[...] = rs * (dy - y * jnp.mean(dy * y, axis=-1, keepdims=True)) + r_ref[...]

        @pl.when(pl.program_id(0) == 0)
        def _():
            dg_ref[...] = jnp.zeros_like(dg_ref)

        dg_ref[...] += jnp.sum(dh_ * y, axis=0, keepdims=True)

    row = pl.BlockSpec((tm, Dd), lambda i: (i, 0))
    vec = pl.BlockSpec((1, Dd), lambda i: (0, 0))
    in_specs, args = [], []
    for a, b, k in pairs:
        in_specs += [pl.BlockSpec((tm, a.shape[1]), lambda i: (i, 0)),
                     pl.BlockSpec((a.shape[1], b.shape[1]), functools.partial(lambda i, k_: (k_, 0), k_=k))]
        args += [a, b]
    in_specs += [row, vec, row]
    args += [x, g, residual]
    return _call(body, name, (M // tm,), in_specs, (row, vec),
                 (jax.ShapeDtypeStruct((M, Dd), f32), jax.ShapeDtypeStruct((1, Dd), f32)), sem=("arbitrary",))(*args)


def _down_final_loss(act, w_down, x1, g, target):
    R, Dd = x1.shape
    tr = _pick(R, (512, 256, 128))

    def body(a_ref, w_ref, x1_ref, g_ref, t_ref, loss_ref, dx_ref, dxb_ref, dg_ref):
        xf = _raw_dot(a_ref[...], w_ref[...], "nn") + x1_ref[...]
        rs = lax.rsqrt(jnp.mean(xf * xf, axis=-1, keepdims=True) + EPS)
        y = xf * rs
        err = y * g_ref[...] - t_ref[...]
        dh_ = err * (1.0 / Dd)
        dy = dh_ * g_ref[...]
        dx = rs * (dy - y * jnp.mean(dy * y, axis=-1, keepdims=True))
        dx_ref[...] = dx
        dxb_ref[...] = dx.astype(bf16)

        @pl.when(pl.program_id(0) == 0)
        def _():
            dg_ref[...] = jnp.zeros_like(dg_ref)
            loss_ref[...] = jnp.zeros_like(loss_ref)

        dg_ref[...] += jnp.sum(dh_ * y, axis=0, keepdims=True)
        part = jnp.sum(jnp.mean(err * err, axis=-1, keepdims=True), axis=0, keepdims=True)
        loss_ref[...] += 0.5 * part

    row = pl.BlockSpec((tr, Dd), lambda i: (i, 0))
    vec = pl.BlockSpec((1, Dd), lambda i: (0, 0))
    in_specs = [pl.BlockSpec((tr, act.shape[1]), lambda i: (i, 0)), pl.BlockSpec(w_down.shape, lambda i: (0, 0)), row, vec, row]
    return _call(body, "down_final_loss", (R // tr,), in_specs, (pl.BlockSpec((1, 128), lambda i: (0, 0)), row, row, vec),
                 (jax.ShapeDtypeStruct((1, 128), f32), jax.ShapeDtypeStruct((R, Dd), f32), jax.ShapeDtypeStruct((R, Dd), bf16),
                  jax.ShapeDtypeStruct((1, Dd), f32)), sem=("arbitrary",))(act, w_down, x1, g, target)


def _gmlp_parts(zuv, ln_g, ln_b):
    zu, zv = zuv[:, :512], zuv[:, 512:]
    u = jax.nn.gelu(zu)
    v = jax.nn.gelu(zv)
    mu = jnp.mean(v, axis=-1, keepdims=True)
    rs = lax.rsqrt(jnp.mean(jnp.square(v - mu), axis=-1, keepdims=True) + EPS)
    xh = (v - mu) * rs
    return zu, zv, u, xh, rs, xh * ln_g + ln_b


GM_TILE_CHUNKS = 4


def _gmlp_tile(T):
    n = _pick(T // GM_CHUNK, (GM_TILE_CHUNKS, 2, 1))
    return n, n * GM_CHUNK


def _gmlp_fwd(proj, ln_g, ln_b, w_s, b_st):
    T = proj.shape[0]
    nch, rows = _gmlp_tile(T)

    def body(p_ref, g_ref, b_ref, w_ref, bs_ref, o_ref):
        _, _, u, _, _, vn = _gmlp_parts(p_ref[...].astype(f32), g_ref[...], b_ref[...])
        causal = _tri(GM_CHUNK, True) > 0
        for gi in range(N_HEAD):
            sl = slice(gi * HEAD, (gi + 1) * HEAD)
            w = jnp.where(causal, w_ref[gi], 0.0)
            for ch in range(nch):
                rs_ = slice(ch * GM_CHUNK, (ch + 1) * GM_CHUNK)
                mixed = _raw_dot(w, vn[rs_, sl], "nn") + bs_ref[:, gi:gi + 1]
                o_ref[rs_, sl] = (u[rs_, sl] * mixed).astype(bf16)

    vec = pl.BlockSpec((1, 512), lambda i: (0, 0))
    return _call(body, "gmlp_fwd", (T // rows,),
                 [pl.BlockSpec((rows, 1024), lambda i: (i, 0)), vec, vec,
                  pl.BlockSpec((N_HEAD, GM_CHUNK, GM_CHUNK), lambda i: (0, 0, 0)), pl.BlockSpec((GM_CHUNK, 128), lambda i: (0, 0))],
                 pl.BlockSpec((rows, 512), lambda i: (i, 0)), jax.ShapeDtypeStruct((T, 512), bf16), sem=("parallel",))(
        proj, ln_g, ln_b, w_s, b_st)


def _gmlp_bwd(proj, ln_g, ln_b, w_s, b_st, da):
    T = proj.shape[0]
    nch, rows = _gmlp_tile(T)

    def body(p_ref, g_ref, b_ref, w_ref, bs_ref, da_ref, dp_ref, dg_ref, db_ref, dw_ref, dbs_ref):
        zu, zv, u, xh, rs, vn = _gmlp_parts(p_ref[...].astype(f32), g_ref[...], b_ref[...])
        causal = _tri(GM_CHUNK, True) > 0
        sub = lax.broadcasted_iota(jnp.int32, (8, GM_CHUNK), 0)
        ones = jnp.ones((8, HEAD), f32)
        dout = da_ref[...].astype(f32)

        @pl.when(pl.program_id(0) == 0)
        def _():
            for r in (dg_ref, db_ref, dw_ref, dbs_ref):
                r[...] = jnp.zeros_like(r)

        du, dvn, dbs = [], [], jnp.zeros((8, GM_CHUNK), f32)
        for gi in range(N_HEAD):
            sl = slice(gi * HEAD, (gi + 1) * HEAD)
            w = jnp.where(causal, w_ref[gi], 0.0)
            du_g, dvn_g, dw_g = [], [], jnp.zeros((GM_CHUNK, GM_CHUNK), f32)
            for ch in range(nch):
                rs_ = slice(ch * GM_CHUNK, (ch + 1) * GM_CHUNK)
                mixed = _raw_dot(w, vn[rs_, sl], "nn") + bs_ref[:, gi:gi + 1]
                du_g.append(dout[rs_, sl] * mixed)
                dm = dout[rs_, sl] * u[rs_, sl]
                dbs = dbs + jnp.where(sub == gi, _sel_dot(ones, dm, "nt"), 0.0)
                dw_g = dw_g + _raw_dot(dm, vn[rs_, sl], "nt")
                dvn_g.append(_raw_dot(w, dm, "tn"))
            dw_ref[gi] += jnp.where(causal, dw_g, 0.0)
            du.append(jnp.concatenate(du_g, axis=0))
            dvn.append(jnp.concatenate(dvn_g, axis=0))
        dbs_ref[...] += dbs
        du = jnp.concatenate(du, axis=-1)
        dvn = jnp.concatenate(dvn, axis=-1)
        dg_ref[...] += jnp.sum(dvn * xh, axis=0, keepdims=True)
        db_ref[...] += jnp.sum(dvn, axis=0, keepdims=True)
        dxh = dvn * g_ref[...]
        dv = rs * (dxh - jnp.mean(dxh, axis=-1, keepdims=True) - xh * jnp.mean(dxh * xh, axis=-1, keepdims=True))
        dp_ref[:, :512] = _egrad(jax.nn.gelu, zu, du).astype(bf16)
        dp_ref[:, 512:] = _egrad(jax.nn.gelu, zv, dv).astype(bf16)

    vec = pl.BlockSpec((1, 512), lambda i: (0, 0))
    wsp = pl.BlockSpec((N_HEAD, GM_CHUNK, GM_CHUNK), lambda i: (0, 0, 0))
    return _call(body, "gmlp_bwd", (T // rows,),
                 [pl.BlockSpec((rows, 1024), lambda i: (i, 0)), vec, vec, wsp, pl.BlockSpec((GM_CHUNK, 128), lambda i: (0, 0)),
                  pl.BlockSpec((rows, 512), lambda i: (i, 0))],
                 (pl.BlockSpec((rows, 1024), lambda i: (i, 0)), vec, vec, wsp, pl.BlockSpec((8, GM_CHUNK), lambda i: (0, 0))),
                 (jax.ShapeDtypeStruct((T, 1024), bf16), jax.ShapeDtypeStruct((1, 512), f32), jax.ShapeDtypeStruct((1, 512), f32),
                  jax.ShapeDtypeStruct((N_HEAD, GM_CHUNK, GM_CHUNK), f32), jax.ShapeDtypeStruct((8, GM_CHUNK), f32)),
                 sem=("arbitrary",))(proj, ln_g, ln_b, w_s, b_st, da)


HG_SUB = 8
HG_NSUB = HG_CHUNK // HG_SUB


def _two_level_matrix(transposed=False):
    shape = (HG_CHUNK, 2 * HG_CHUNK) if transposed else (2 * HG_CHUNK, HG_CHUNK)
    r = lax.broadcasted_iota(jnp.int32, shape, 1 if transposed else 0)
    c = lax.broadcasted_iota(jnp.int32, shape, 0 if transposed else 1)
    t = jnp.where(r < HG_CHUNK, r, r - HG_CHUNK)
    local = (r < HG_CHUNK) & (t // HG_SUB == c // HG_SUB) & (c <= t)
    before = (r >= HG_CHUNK) & (c < (t // HG_SUB) * HG_SUB)
    return (local | before).astype(f32)


def _two_level_sums(x):
    two = _sel_dot(_two_level_matrix(), x, "nn")
    return two[:HG_CHUNK], two[HG_CHUNK:]


@jax.custom_vjp
def _two_level_cumsum(x):
    return _two_level_sums(x)


_two_level_cumsum.defvjp(
    lambda x: (_two_level_sums(x), None),
    lambda _, g: (_sel_dot(_two_level_matrix(), jnp.concatenate(g, axis=0), "tn"),))


def _tile_matrix():
    s = lax.broadcasted_iota(jnp.int32, (HG_SUB, HG_CHUNK), 0)
    j = lax.broadcasted_iota(jnp.int32, (HG_SUB, HG_CHUNK), 1)
    return (j % HG_SUB == s).astype(f32)


@jax.custom_vjp
def _tile_lanes(x):
    return _sel_dot(_tile_matrix(), x, "nn", x_first=True, pieces=1)


_tile_lanes.defvjp(
    lambda x: (_sel_dot(_tile_matrix(), x, "nn", x_first=True, pieces=1), None),
    lambda _, g: (_sel_dot(_tile_matrix(), g, "nt", x_first=True, pieces=2),))


def _block_rows(x):
    k = x.shape[-1]
    return jnp.broadcast_to(x.reshape(HG_NSUB, 1, HG_SUB, k), (HG_NSUB, HG_SUB, HG_SUB, k)).reshape(HG_CHUNK, HG_SUB, k)


def _hgrn_chunk(st0, q_raw, f_raw, i_raw, g_raw, l0, l1, ng):
    C, SUB = HG_CHUNK, HG_SUB
    lb = jax.nn.sigmoid(l0 - l1)
    fg = lb + (1.0 - lb) * jax.nn.sigmoid(f_raw)
    kk = 1.0 - fg
    qf = jax.nn.silu(q_raw)
    al, base = _two_level_cumsum(jnp.log(fg))
    a = al + base
    row = lax.broadcasted_iota(jnp.int32, (C, HEAD), 0)
    a_last = jnp.sum(jnp.where(row == C - 1, a, 0.0), axis=0, keepdims=True)
    inter = _dot_nt(qf * jnp.exp(a), st0)
    qt = qf * jnp.exp(al)
    rb = lax.broadcasted_iota(jnp.int32, (C, C), 0) // SUB
    cb = lax.broadcasted_iota(jnp.int32, (C, C), 1) // SUB
    scores = jnp.zeros((C, C), f32)
    for i in range(1, HG_NSUB):
        base_i = jnp.sum(jnp.where(row == i * SUB, base, 0.0), axis=0, keepdims=True)
        kt = kk * jnp.exp(jnp.minimum(base_i - a, 0.0))
        scores = scores + jnp.where((rb == i) & (cb < i), _dot_nt(qt, kt), 0.0)
    t_i = lax.broadcasted_iota(jnp.int32, (C, SUB, HEAD), 0) % SUB
    s_i = lax.broadcasted_iota(jnp.int32, (C, SUB, HEAD), 1)
    decay = jnp.exp(jnp.where(s_i <= t_i, al[:, None, :] - _block_rows(al), -jnp.inf))
    diag = jnp.sum(qf[:, None, :] * decay * _block_rows(kk), axis=-1)
    scores = scores + jnp.where(rb == cb, _tile_lanes(diag), 0.0)
    o = inter + _dot_nn(scores, i_raw)
    st1 = jnp.exp(a_last) * st0 + _dot_tn(i_raw, kk * jnp.exp(a_last - a))
    on = o * lax.rsqrt(jnp.mean(o * o, axis=-1, keepdims=True) + EPS) * ng
    return st1, on * jax.nn.silu(g_raw)


def _hgrn_specs(S, Bl, rev):
    N = S // HG_CHUNK
    chunk = (lambda n: N - 1 - n) if rev else (lambda n: n)
    col = lambda c0: pl.BlockSpec((Bl, HG_CHUNK, 512), lambda n: (0, chunk(n), c0 // 512))
    st = pl.BlockSpec((Bl, N_HEAD, 1, HEAD, HEAD), lambda n: (0, 0, chunk(n), 0, 0))
    full = lambda *s: pl.BlockSpec(s, functools.partial(lambda n, nd: (0,) * nd, nd=len(s)))
    return N, col, st, full


def _hgrn_fwd(proj, lb_logits, ng, Bl, S):
    N, col, st, full = _hgrn_specs(S, Bl, False)

    def body(q_ref, f_ref, i_ref, g_ref, l_ref, ng_ref, o_ref, st_ref, state):
        @pl.when(pl.program_id(0) == 0)
        def _():
            state[...] = jnp.zeros_like(state)

        for b in range(Bl):
            for h in range(N_HEAD):
                sl = slice(h * HEAD, (h + 1) * HEAD)
                st0 = state[b, h]
                st_ref[b, h, 0] = st0
                st1, out = _hgrn_chunk(st0, *[r[b, :, sl].astype(f32) for r in (q_ref, f_ref, i_ref, g_ref)],
                                       l_ref[0:1, sl], l_ref[1:2, sl], ng_ref[...])
                state[b, h] = st1
                o_ref[b, :, sl] = out.astype(bf16)

    return _call(body, "hgrn_fwd", (N,), [col(C_HQ), col(C_HF), col(C_HI), col(C_HG), full(2, 512), full(1, HEAD)],
                 (col(0), st),
                 (jax.ShapeDtypeStruct((Bl, S, 512), bf16), jax.ShapeDtypeStruct((Bl, N_HEAD, N, HEAD, HEAD), f32)),
                 scratch=[pltpu.VMEM((Bl, N_HEAD, HEAD, HEAD), f32)], sem=("arbitrary",))(
        proj, proj, proj, proj, lb_logits, ng)


def _hgrn_bwd(proj, lb_logits, ng, states, db, dzuv, dxq, dgl, Bl, S):
    N, col, st, full = _hgrn_specs(S, Bl, True)
    rows = lambda width: pl.BlockSpec((Bl, HG_CHUNK, width), lambda n: (0, N - 1 - n, 0))

    def body(q_ref, f_ref, i_ref, g_ref, l_ref, ng_ref, st_ref, db_ref, dzuv_ref, dxq_ref, dgl_ref,
             dp_ref, dl_ref, dng_ref, dstate):
        @pl.when(pl.program_id(0) == 0)
        def _():
            dstate[...] = jnp.zeros_like(dstate)
            dl_ref[...] = jnp.zeros_like(dl_ref)
            dng_ref[...] = jnp.zeros_like(dng_ref)

        dp_ref[:, :, C_ZU:C_HQ] = dzuv_ref[...]
        dp_ref[:, :, C_XQ:C_GL] = dxq_ref[...]
        for n in range(3):
            dp_ref[:, :, C_GL + n * D_MODEL:C_GL + (n + 1) * D_MODEL] = dgl_ref[n]
        dq_ref, df_ref, di_ref, dg_ref = [dp_ref.at[:, :, c0:c0 + 512] for c0 in (C_HQ, C_HF, C_HI, C_HG)]
        for b in range(Bl):
            for h in range(N_HEAD):
                sl = slice(h * HEAD, (h + 1) * HEAD)
                _, vjp = jax.vjp(_hgrn_chunk, st_ref[b, h, 0], *[r[b, :, sl].astype(f32) for r in (q_ref, f_ref, i_ref, g_ref)],
                                 l_ref[0:1, sl], l_ref[1:2, sl], ng_ref[...])
                dst0, dq, df, di, dg, dl0, dl1, dng = vjp((dstate[b, h], db_ref[b, :, sl].astype(f32)))
                dstate[b, h] = dst0
                dq_ref[b, :, sl] = dq.astype(bf16)
                df_ref[b, :, sl] = df.astype(bf16)
                di_ref[b, :, sl] = di.astype(bf16)
                dg_ref[b, :, sl] = dg.astype(bf16)
                dl_ref[0:1, sl] += dl0
                dl_ref[1:2, sl] += dl1
                dng_ref[b, h] += dng

    return _call(body, "hgrn_bwd", (N,),
                 [col(C_HQ), col(C_HF), col(C_HI), col(C_HG), full(2, 512), full(1, HEAD), st, col(0), rows(C_HQ - C_ZU),
                  rows(C_GL - C_XQ), pl.BlockSpec((3, Bl, HG_CHUNK, D_MODEL), lambda n: (0, 0, N - 1 - n, 0))],
                 (rows(IN_WIDTH), full(2, 512), full(Bl, N_HEAD, 1, HEAD)),
                 (jax.ShapeDtypeStruct((Bl, S, IN_WIDTH), bf16), jax.ShapeDtypeStruct((2, 512), f32),
                  jax.ShapeDtypeStruct((Bl, N_HEAD, 1, HEAD), f32)),
                 scratch=[pltpu.VMEM((Bl, N_HEAD, HEAD, HEAD), f32)], sem=("arbitrary",))(
        proj, proj, proj, proj, lb_logits, ng, states, db, dzuv, dxq, dgl)


def _attn_probs(q, k):
    s = _raw_dot(q, k, "nt") * (HEAD ** -0.5)
    e = jnp.exp(s - jnp.max(s, axis=-1, keepdims=True))
    return e / jnp.sum(e, axis=-1, keepdims=True)


def _attn_specs(S, tq):
    nq = S // tq
    q = pl.BlockSpec((tq, 512), lambda b, i: (b * nq + i, C_XQ // 512))
    kv = pl.BlockSpec((1, MEM_LEN, 1024), lambda b, i: (b, 0, 0))
    o = pl.BlockSpec((tq, 512), lambda b, i: (b * nq + i, 0))
    return nq, q, kv, o


def _attn_fwd(proj, kv, Bl, S):
    tq = _pick(S, (512, 256, 128))
    nq, qs, kvs, os_ = _attn_specs(S, tq)

    def body(q_ref, kv_ref, o_ref):
        for h in range(N_HEAD):
            sl = slice(h * HEAD, (h + 1) * HEAD)
            p = _attn_probs(q_ref[:, sl], kv_ref[0, :, sl])
            o_ref[:, sl] = _raw_dot(p, kv_ref[0, :, 512 + h * HEAD:512 + (h + 1) * HEAD], "nn").astype(bf16)

    return _call(body, "attn_fwd", (Bl, nq), [qs, kvs], os_, jax.ShapeDtypeStruct((Bl * S, 512), bf16),
                 sem=("parallel", "parallel"))(proj, kv)


def _attn_bwd(proj, kv, dc, Bl, S):
    tq = _pick(S, (512, 256, 128))
    nq, qs, kvs, os_ = _attn_specs(S, tq)

    def body(q_ref, kv_ref, do_ref, dq_ref, dkv_ref):
        @pl.when(pl.program_id(1) == 0)
        def _():
            dkv_ref[...] = jnp.zeros_like(dkv_ref)

        for h in range(N_HEAD):
            sl = slice(h * HEAD, (h + 1) * HEAD)
            vsl = slice(512 + h * HEAD, 512 + (h + 1) * HEAD)
            q, k, v, do = q_ref[:, sl], kv_ref[0, :, sl], kv_ref[0, :, vsl], do_ref[:, sl]
            p = _attn_probs(q, k)
            dkv_ref[0, :, vsl] += _raw_dot(p, do, "tn")
            dp = _raw_dot(do, v, "nt")
            ds = p * (dp - jnp.sum(dp * p, axis=-1, keepdims=True)) * (HEAD ** -0.5)
            dq_ref[:, sl] = _raw_dot(ds, k, "nn").astype(bf16)
            dkv_ref[0, :, sl] += _raw_dot(ds, q, "tn")

    return _call(body, "attn_bwd", (Bl, nq), [qs, kvs, os_], (os_, kvs),
                 (jax.ShapeDtypeStruct((Bl * S, 512), bf16), jax.ShapeDtypeStruct((Bl, MEM_LEN, 1024), f32)),
                 sem=("arbitrary", "arbitrary"))(proj, kv, dc)


def _gate_specs(tm):
    half = D_MODEL // 2
    return [pl.BlockSpec((tm, half), functools.partial(lambda i, c: (i, c), c=(C_GL + n * D_MODEL) // half + k))
            for n in range(3) for k in range(2)]


def _merge_out_norm_fwd(branches, wb, proj, w_out, x, g):
    T = proj.shape[0]
    tm = _pick(T, (512, 256, 128))

    def body(a_ref, b_ref, c_ref, w0, w1, w2, g0a, g0b, g1a, g1b, g2a, g2b, wo_ref, x_ref, g_ref, m_ref, x1_ref, h_ref):
        acc = jnp.zeros((tm, D_MODEL), f32)
        for x_n, w_ref, ga, gb in ((a_ref, w0, g0a, g0b), (b_ref, w1, g1a, g1b), (c_ref, w2, g2a, g2b)):
            gate = jax.nn.sigmoid(jnp.concatenate([ga[...], gb[...]], axis=-1).astype(f32))
            acc = acc + gate * _raw_dot(x_n[...], w_ref[...], "nn")
        merged = acc.astype(bf16)
        m_ref[...] = merged
        x1 = x_ref[...] + _raw_dot(merged, wo_ref[...], "nn")
        x1_ref[...] = x1
        y = x1 * lax.rsqrt(jnp.mean(x1 * x1, axis=-1, keepdims=True) + EPS) * g_ref[...]
        h_ref[...] = y.astype(bf16)

    br = pl.BlockSpec((tm, 512), lambda i: (i, 0))
    w = pl.BlockSpec((512, D_MODEL), lambda i: (0, 0))
    row = pl.BlockSpec((tm, D_MODEL), lambda i: (i, 0))
    return _call(body, "merge_out_norm_fwd", (T // tm,),
                 [br, br, br, w, w, w, *_gate_specs(tm), pl.BlockSpec((D_MODEL, D_MODEL), lambda i: (0, 0)), row,
                  pl.BlockSpec((1, D_MODEL), lambda i: (0, 0))],
                 (row, row, row),
                 (jax.ShapeDtypeStruct((T, D_MODEL), bf16), jax.ShapeDtypeStruct((T, D_MODEL), f32),
                  jax.ShapeDtypeStruct((T, D_MODEL), bf16)),
                 sem=("parallel",))(*branches, *wb, *[proj] * 6, w_out, x, g)


def _merge_bwd(branches, wb, proj, merged, dx1, w_out):
    T = proj.shape[0]
    tm = _pick(T, (256, 128))

    def body(a_ref, b_ref, c_ref, w0, w1, w2, g0a, g0b, g1a, g1b, g2a, g2b, m_ref, dx_ref, wo_ref, dgl_ref, d0, d1, d2, gw_ref, gwo_ref):
        @pl.when(pl.program_id(0) == 0)
        def _():
            gw_ref[...] = jnp.zeros_like(gw_ref)
            gwo_ref[...] = jnp.zeros_like(gwo_ref)

        dx = dx_ref[...].astype(bf16)
        gwo_ref[...] += _raw_dot(m_ref[...], dx, "tn")
        dm = _raw_dot(dx, wo_ref[...], "nt")
        for n, (x_ref, w_ref, ga, gb, d_ref) in enumerate(((a_ref, w0, g0a, g0b, d0), (b_ref, w1, g1a, g1b, d1), (c_ref, w2, g2a, g2b, d2))):
            x, w = x_ref[...], w_ref[...]
            up = _raw_dot(x, w, "nn")
            sg = jax.nn.sigmoid(jnp.concatenate([ga[...], gb[...]], axis=-1).astype(f32))
            dgl_ref[n] = (dm * up * sg * (1.0 - sg)).astype(bf16)
            dup = (dm * sg).astype(bf16)
            d_ref[...] = _raw_dot(dup, w, "nt").astype(bf16)
            gw_ref[n] += _raw_dot(x, dup, "tn")

    br = pl.BlockSpec((tm, 512), lambda i: (i, 0))
    w = pl.BlockSpec((512, D_MODEL), lambda i: (0, 0))
    sh = jax.ShapeDtypeStruct((T, 512), bf16)
    row = pl.BlockSpec((tm, D_MODEL), lambda i: (i, 0))
    square = pl.BlockSpec((D_MODEL, D_MODEL), lambda i: (0, 0))
    outs = _call(body, "merge_bwd", (T // tm,), [br, br, br, w, w, w, *_gate_specs(tm), row, row, square],
                 (pl.BlockSpec((3, tm, D_MODEL), lambda i: (0, i, 0)), br, br, br, pl.BlockSpec((3, 512, D_MODEL), lambda i: (0, 0, 0)), square),
                 (jax.ShapeDtypeStruct((3, T, D_MODEL), bf16), sh, sh, sh, jax.ShapeDtypeStruct((3, 512, D_MODEL), f32),
                  jax.ShapeDtypeStruct((D_MODEL, D_MODEL), f32)),
                 sem=("arbitrary",))(*branches, *wb, *[proj] * 6, merged, dx1, w_out)
    return outs[0], outs[1:4], outs[4], outs[5]


CONV_TC = 256


def _shift_down(a, k):
    r = pltpu.roll(a, k, 0)
    row = lax.broadcasted_iota(jnp.int32, (8, a.shape[1]), 0)
    return jnp.concatenate([jnp.where(row >= k, r[:8], 0.0), r[8:]], axis=0)


def _shift_up(a, k):
    n = a.shape[0]
    r = pltpu.roll(a, n - k, 0)
    row = lax.broadcasted_iota(jnp.int32, (8, a.shape[1]), 0)
    return jnp.concatenate([r[:n - 8], jnp.where(row < 8 - k, r[n - 8:], 0.0)], axis=0)


def _conv_pre(a, a1, a2, cw, cb):
    return cb + cw[0:1] * a2 + cw[1:2] * a1 + cw[2:3] * a


def _up_conv_fwd(h2, w_up_t, cw, cb):
    Bl, S, Dd = h2.shape
    nc = D_FF // CONV_TC

    def body(h_ref, wa_ref, wb_ref, cw_ref, cb_ref, a_ref, b_ref, o_ref):
        a16 = _raw_dot(h_ref[0], wa_ref[...], "nt").astype(bf16)
        b16 = _raw_dot(h_ref[0], wb_ref[...], "nt").astype(bf16)
        a_ref[0], b_ref[0] = a16, b16
        a = a16.astype(f32)
        ac = _conv_pre(a, _shift_down(a, 1), _shift_down(a, 2), cw_ref[...], cb_ref[...])
        o_ref[0] = (jax.nn.silu(ac) * b16.astype(f32)).astype(bf16)

    seq = pl.BlockSpec((1, S, CONV_TC), lambda b, c: (b, 0, c))
    sh = jax.ShapeDtypeStruct((Bl, S, D_FF), bf16)
    return _call(body, "up_conv_fwd", (Bl, nc),
                 [pl.BlockSpec((1, S, Dd), lambda b, c: (b, 0, 0)), pl.BlockSpec((CONV_TC, Dd), lambda b, c: (c, 0)),
                  pl.BlockSpec((CONV_TC, Dd), lambda b, c: (nc + c, 0)), pl.BlockSpec((3, CONV_TC), lambda b, c: (0, c)),
                  pl.BlockSpec((1, CONV_TC), lambda b, c: (0, c))],
                 (seq, seq, seq), (sh, sh, sh), sem=("parallel", "parallel"))(h2, w_up_t, w_up_t, cw, cb)


def _down_conv_bwd(dx2, w_down, a, b, cw, cb):
    Bl, S, Dd = dx2.shape
    nc = D_FF // CONV_TC

    def body(dx_ref, wd_ref, a_ref, b_ref, cw_ref, cb_ref, da_ref, db_ref, dcw_ref, dcb_ref):
        dact = _raw_dot(dx_ref[0], wd_ref[...], "nt").astype(bf16).astype(f32)
        a, cw = a_ref[0].astype(f32), cw_ref[...]
        a1, a2 = _shift_down(a, 1), _shift_down(a, 2)
        ac = _conv_pre(a, a1, a2, cw, cb_ref[...])
        sg = jax.nn.sigmoid(ac)
        gated = dact * sg
        db_ref[0] = (gated * ac).astype(bf16)
        dac = gated * b_ref[0].astype(f32) * (1.0 + ac * (1.0 - sg))
        da_ref[0] = (cw[2:3] * dac + cw[1:2] * _shift_up(dac, 1) + cw[0:1] * _shift_up(dac, 2)).astype(bf16)
        dcw_ref[0, 0:1, :] = jnp.sum(dac * a2, axis=0, keepdims=True)
        dcw_ref[0, 1:2, :] = jnp.sum(dac * a1, axis=0, keepdims=True)
        dcw_ref[0, 2:3, :] = jnp.sum(dac * a, axis=0, keepdims=True)
        dcb_ref[0] = jnp.sum(dac, axis=0, keepdims=True)

    seq = pl.BlockSpec((1, S, CONV_TC), lambda b_, c: (b_, 0, c))
    sh = jax.ShapeDtypeStruct((Bl, S, D_FF), bf16)
    return _call(body, "down_conv_bwd", (Bl, nc),
                 [pl.BlockSpec((1, S, Dd), lambda b_, c: (b_, 0, 0)), pl.BlockSpec((CONV_TC, Dd), lambda b_, c: (c, 0)), seq, seq,
                  pl.BlockSpec((3, CONV_TC), lambda b_, c: (0, c)), pl.BlockSpec((1, CONV_TC), lambda b_, c: (0, c))],
                 (seq, seq, pl.BlockSpec((1, 3, CONV_TC), lambda b_, c: (b_, 0, c)), pl.BlockSpec((1, 1, CONV_TC), lambda b_, c: (b_, 0, c))),
                 (sh, sh, jax.ShapeDtypeStruct((Bl, 3, D_FF), f32), jax.ShapeDtypeStruct((Bl, 1, D_FF), f32)),
                 sem=("parallel", "parallel"))(dx2, w_down, a, b, cw, cb)


def _local_step(x, mem, target, p, w_in_t, late_b, late_c, send, settle):
    Bl, S, Dd = x.shape
    T = Bl * S
    x2d, t2d, mem2d = x.reshape(T, Dd), target.reshape(T, Dd), mem.reshape(Bl * MEM_LEN, Dd)
    b_st = jnp.pad(p["b_spatial"].T, ((0, 0), (0, 128 - N_HEAD)))
    lbl = p["lb_logits"]

    h = _rms_fwd(x2d, p["norm1_g"], "norm1_fwd")
    proj = _mm(h, w_in_t, "nt", bf16, "proj_fwd", 1024, 1664)
    a_out = _gmlp_fwd(proj, p["ln_v_g"], p["ln_v_b"], p["w_spatial"], b_st)
    proj3 = proj.reshape(Bl, S, IN_WIDTH)
    b_out, states = _hgrn_fwd(proj3, lbl, p["hgrn_norm_g"], Bl, S)
    b_out = b_out.reshape(T, 512)
    memn = _rms_fwd(mem2d, p["mem_norm_g"], "memnorm_fwd")
    w = late_b(b_out)
    wb = w["w_branch"]
    kv = _mm(memn, w["w_mem_kv"], "nn", f32, "kv_fwd", 512, 1024).reshape(Bl, MEM_LEN, 2 * 512)
    c_out = _attn_fwd(proj, kv, Bl, S)
    branches = (a_out, b_out, c_out)
    merged, x1, h2 = _merge_out_norm_fwd(branches, wb, proj, w["w_out"], x2d, p["norm2_g"])
    w.update(late_c(h2))
    ffn_a, ffn_b, act = _up_conv_fwd(h2.reshape(Bl, S, Dd), w["w_up_t"], w["conv_w"], p["conv_b"])
    act = act.reshape(T, D_FF)
    loss_part, dx2, dx2_16, g_final = _down_final_loss(act, w["w_down"], x1, p["final_g"], t2d)

    g_w_down = _mm(act, dx2_16, "tn", bf16, "down_dw", 1408, 1024, 1024)
    da, db, g_conv_w, g_conv_b = _down_conv_bwd(dx2_16.reshape(Bl, S, Dd), w["w_down"], ffn_a, ffn_b, w["conv_w"], p["conv_b"])
    da, db = da.reshape(T, D_FF), db.reshape(T, D_FF)
    g_w_up_t = _mm(da, h2, "tn", bf16, "up_dw_a", 1408, 1024, 1024, into=(lax.empty((2 * D_FF, D_MODEL), bf16), 0))
    g_w_up_t = _mm(db, h2, "tn", bf16, "up_dw_b", 1408, 1024, 1024, into=(g_w_up_t, D_FF))
    send("c", dict(w_up=g_w_up_t, conv_w=jnp.sum(g_conv_w, axis=0), w_down=g_w_down))
    dx1, g_norm2 = _mm_rms_bwd([(da, w["w_up_t"], 0), (db, w["w_up_t"], 1)], x1, p["norm2_g"], dx2, "up_dx_norm2_bwd", 256)

    dgl, dbr, g_w_branch, g_w_out = _merge_bwd(branches, wb, proj, merged, dx1, w["w_out"])
    dxq, dkv = _attn_bwd(proj, kv, dbr[2], Bl, S)
    dkv = dkv.reshape(Bl * MEM_LEN, 2 * 512)
    g_w_kv = _mm(memn, dkv, "tn", bf16, "kv_dw", 1024, 1024, 512)
    send("b", dict(w_mem_kv=g_w_kv, w_branch=g_w_branch, w_out=g_w_out))
    dzuv, g_ln_g, g_ln_b, g_w_sp, g_b_sp = _gmlp_bwd(proj, p["ln_v_g"], p["ln_v_b"], p["w_spatial"], b_st, dbr[0])
    dproj, g_lbl, g_ng = _hgrn_bwd(proj3, lbl, p["hgrn_norm_g"], states, dbr[1].reshape(Bl, S, 512), dzuv.reshape(Bl, S, -1),
                                   dxq.reshape(Bl, S, -1), dgl.reshape(3, Bl, S, Dd), Bl, S)
    dproj = dproj.reshape(T, IN_WIDTH)
    half = Dd // 2
    send("a0", dict(w_in_half=_mm(dproj, h, "tn", bf16, "proj_dw_0", 512, half, n_tiles=(0, 1))))
    g_half = _mm(dproj, h, "tn", bf16, "proj_dw_1", 512, half, n_tiles=(1, 1))
    dkv, g_half = lax.optimization_barrier((dkv, g_half))
    send("a1", dict(w_in_half=g_half))
    dmemn = _mm(dkv, w["w_mem_kv"], "nt", f32, "kv_dx", 512, 1024)
    _, g_mem_norm = _rms_bwd(mem2d, p["mem_norm_g"], dmemn, "memnorm_bwd")
    dx, g_norm1 = _mm_rms_bwd([(settle(dproj), w_in_t, 0)], x2d, p["norm1_g"], dx1, "proj_dx_norm1_bwd", 256)

    gs = dict(w_spatial=g_w_sp, norm1_g=g_norm1, mem_norm_g=g_mem_norm, norm2_g=g_norm2, final_g=g_final, lb_logits=g_lbl,
              ln_v_g=g_ln_g, ln_v_b=g_ln_b, b_spatial=g_b_sp, hgrn_norm_g=g_ng, conv_b=g_conv_b)
    return loss_part, dx.reshape(Bl, S, Dd), gs


def _coords():
    return lax.axis_index("x"), lax.axis_index("y"), lax.axis_index("c")


def _slot(dev):
    return 4 * dev[0] + 2 * dev[1] + dev[2]


def _comm_call(body, name, arrays, out_shapes, n_sem):
    n = len(arrays)
    hbm = pl.BlockSpec(memory_space=pl.ANY)
    return pl.pallas_call(
        body, name=name, out_shape=out_shapes, in_specs=[hbm] * n, out_specs=[hbm] * n,
        scratch_shapes=[pltpu.SemaphoreType.DMA((n_sem, n)), pltpu.SemaphoreType.DMA((n_sem, n)), pltpu.SemaphoreType.DMA((n,))])(*arrays)


def _all_gather(blocks, name):
    n = len(blocks)

    def body(*refs):
        x_refs, o_refs, (send_sems, recv_sems, local_sems) = refs[:n], refs[n:2 * n], refs[2 * n:]
        x, y, c = _coords()
        me, sibling = (x, y, c), (x, y, 1 - c)
        chips = [(1 - x, y), (x, 1 - y), (1 - x, 1 - y)]

        def copy(a, k, block_dev, to, from_input=False):
            dst = o_refs[a].at[_slot(block_dev)]
            return pltpu.make_async_remote_copy(src_ref=x_refs[a] if from_input else dst, dst_ref=dst, send_sem=send_sems.at[k, a],
                                                recv_sem=recv_sems.at[k, a], device_id=to, device_id_type=MESH)

        mine = [pltpu.make_async_copy(x_refs[a], o_refs[a].at[_slot(me)], local_sems.at[a]) for a in range(n)]
        first = [copy(a, 0, me, sibling, True) for a in range(n)]
        first += [copy(a, 1 + j, me, (*chip, c), True) for j, chip in enumerate(chips) for a in range(n)]
        for cp in mine + first:
            cp.start()
        passed = []
        for j, chip in enumerate(chips):
            for a in range(n):
                copy(a, 1 + j, (*chip, c), me).wait_recv()
                fwd = copy(a, 4 + j, (*chip, c), sibling)
                fwd.start()
                passed.append(fwd)
        for a in range(n):
            copy(a, 0, sibling, me).wait_recv()
        for j, chip in enumerate(chips):
            for a in range(n):
                copy(a, 4 + j, (*chip, 1 - c), me).wait_recv()
        for cp in first + passed:
            cp.wait_send()
        for cp in mine:
            cp.wait()

    return _comm_call(body, name, blocks, [jax.ShapeDtypeStruct((N_DEV,) + b.shape, b.dtype) for b in blocks], 7)


_REL = [(0, 0, 1), (0, 1, 0), (0, 1, 1), (1, 0, 0), (1, 0, 1), (1, 1, 0), (1, 1, 1)]


def _seq_exchange(arrays, gather, name, collective_id):
    n = len(arrays)
    hbm = pltpu.MemorySpace.HBM
    srcs = [jax.new_ref(a, memory_space=hbm) for a in arrays]
    lands = [jax.empty_ref(jax.ShapeDtypeStruct(((N_DEV,) + a.shape) if gather else a.shape, a.dtype), memory_space=hbm) for a in arrays]

    @pl.kernel(mesh=plsc.ScalarSubcoreMesh(axis_name="sequencer", num_cores=1), name=name,
               scratch_types=(pltpu.SemaphoreType.DMA((7, n)), pltpu.SemaphoreType.DMA((7, n)), pltpu.SemaphoreType.DMA((n,))),
               compiler_params=pltpu.CompilerParams(collective_id=collective_id))
    def launch(send, recv, local):
        x, y, c = _coords()
        me = (x, y, c)
        peers = [(x ^ dx, y ^ dy, c ^ dc) for dx, dy, dc in _REL]
        barrier = pltpu.get_barrier_semaphore()
        for peer in peers:
            pl.semaphore_signal(barrier, inc=1, device_id=peer, device_id_type=MESH)
        pl.semaphore_wait(barrier, len(peers))

        def copy(a, k, peer, arrival):
            return pltpu.make_async_remote_copy(
                src_ref=srcs[a] if gather else srcs[a].at[_slot(peer)], dst_ref=lands[a].at[_slot(peer if arrival else me)],
                send_sem=send.at[k, a], recv_sem=recv.at[k, a], device_id=peer, device_id_type=MESH)

        mine = [pltpu.make_async_copy(srcs[a] if gather else srcs[a].at[_slot(me)], lands[a].at[_slot(me)], local.at[a])
                for a in range(n)]
        out = [copy(a, k, peer, False) for a in range(n) for k, peer in enumerate(peers)]
        for cp in mine + out:
            cp.start()
        for a in range(n):
            for k, peer in enumerate(peers):
                copy(a, k, peer, True).wait_recv()
        for cp in out:
            cp.wait_send()
        for cp in mine:
            cp.wait()

    launch()
    return [land[...] for land in lands]


def _adam_math(w, g, m, v):
    m_ = ADAM_B1 * m + (1.0 - ADAM_B1) * g
    v_ = ADAM_B2 * v + (1.0 - ADAM_B2) * jnp.square(g)
    m_hat = m_ / (1.0 - ADAM_B1 ** ADAM_STEP)
    v_hat = v_ / (1.0 - ADAM_B2 ** ADAM_STEP)
    return -ADAM_LR * (m_hat / (jnp.sqrt(v_hat) + ADAM_EPS) + ADAM_WD * w), m_, v_


def _reduce_adamw(parts, w, m, v, name):
    R, L = w.shape
    tr = _pick(R, (256, 208, 176, 128, 64, 32, 16, 8))
    n = len(parts)

    def body(*refs):
        w_ref, m_ref, v_ref, g_ref, d_ref, nm_ref, nv_ref = refs[n:]
        pieces = []
        for p_ref in refs[:n]:
            g = p_ref[0].astype(f32)
            for i in range(1, N_DEV):
                g = g + p_ref[i].astype(f32)
            pieces.append(g)
        g = pieces[0] if n == 1 else jnp.concatenate(pieces, axis=-1)
        g_ref[...] = g
        d_ref[...], nm_ref[...], nv_ref[...] = _adam_math(w_ref[...], g, m_ref[...], v_ref[...])

    blk = pl.BlockSpec((tr, L), lambda i: (i, 0))
    sh = jax.ShapeDtypeStruct((R, L), f32)
    return _call(body, name, (R // tr,), [pl.BlockSpec((N_DEV, tr, q.shape[2]), lambda i: (0, i, 0)) for q in parts] + [blk, blk, blk],
                 (blk,) * 4, (sh,) * 4, sem=("parallel",))(*parts, w, m, v)


SMALL = (("w_spatial", (512, 128), 0), ("norm1_g", (1, 1024), 512), ("mem_norm_g", (1, 1024), 520), ("norm2_g", (1, 1024), 528),
         ("final_g", (1, 1024), 536), ("lb_logits", (2, 512), 544), ("ln_v_g", (1, 512), 552), ("ln_v_b", (1, 512), 556),
         ("b_spatial", (4, 128), 560), ("hgrn_norm_g", (1, 128), 564), ("conv_b", (1, 2816), 565))
LOSS_ROW, SMALL_USED, SMALL_ROWS = 587, 588, 640


def _segments(shape, base):
    r, n = shape
    per = n // 128
    return [(base + i * per + j, i, slice(j * 128, (j + 1) * 128)) for i in range(r) for j in range(per)]


def _pack_small(gs, loss_part):
    names = [n for n, _, _ in SMALL]

    def body(*refs):
        src, loss_ref, o_ref = dict(zip(names, refs[:-2])), refs[-2], refs[-1]
        o_ref[SMALL_USED:SMALL_ROWS, :] = jnp.zeros((SMALL_ROWS - SMALL_USED, 128), f32)
        o_ref[LOSS_ROW:LOSS_ROW + 1, :] = loss_ref[...]
        for name, shape, base in SMALL:
            ref = src[name]
            if name == "w_spatial":
                o_ref[base:base + 512, :] = ref[...].reshape(512, 128)
            elif name == "b_spatial":
                o_ref[base:base + 4, :] = ref[0:4, :]
            elif name == "conv_b":
                per_example = functools.reduce(lambda u, v_: u + v_, [ref[b] for b in range(ref.shape[0])])
                for row, i, sl in _segments(shape, base):
                    o_ref[row:row + 1, :] = per_example[i:i + 1, sl]
            elif name == "hgrn_norm_g":
                per_head = [ref[b, h] for b in range(ref.shape[0]) for h in range(N_HEAD)]
                o_ref[base:base + 1, :] = functools.reduce(lambda u, v_: u + v_, per_head)
            else:
                for row, i, sl in _segments(shape, base):
                    o_ref[row:row + 1, :] = ref[i:i + 1, sl]

    return pl.pallas_call(body, name="pack_small", out_shape=jax.ShapeDtypeStruct((SMALL_ROWS, 128), f32))(
        *[gs[n] for n in names], loss_part)


def _small_update(gathered, w, m, v):
    names = [n for n, _, _ in SMALL]
    k = len(names)

    def body(*refs):
        p_ref = refs[0]
        ins = [dict(zip(names, refs[1 + i * k:1 + (i + 1) * k])) for i in range(3)]
        outs = [dict(zip(names, refs[1 + (3 + i) * k:1 + (4 + i) * k])) for i in range(4)]
        loss_ref, gsum = refs[-2], refs[-1]
        g = p_ref[0]
        for i in range(1, N_DEV):
            g = g + p_ref[i]
        gsum[...] = g
        loss_ref[...] = gsum[LOSS_ROW:LOSS_ROW + 1, :]
        for name, shape, base in SMALL:
            if name == "w_spatial":
                where = [(slice(base, base + 512), (slice(None), slice(None)))]
            else:
                where = [(slice(row, row + 1), (slice(i, i + 1), sl)) for row, i, sl in _segments(shape, base)]
            for rows, at in where:
                g_ = gsum[rows, :]
                d_, m_, v_ = _adam_math(ins[0][name][at], g_, ins[1][name][at], ins[2][name][at])
                for o, val in zip(outs, (g_, d_, m_, v_)):
                    o[name][at] = val

    args = [gathered] + [d[n] for d in (w, m, v) for n in names]
    out_shapes = [jax.ShapeDtypeStruct(shape, f32) for _ in range(4) for _, shape, _ in SMALL] + [jax.ShapeDtypeStruct((1, 128), f32)]
    outs = pl.pallas_call(body, name="small_update", out_shape=out_shapes, scratch_shapes=[pltpu.VMEM((SMALL_ROWS, 128), f32)])(*args)
    return [dict(zip(names, outs[i * k:(i + 1) * k])) for i in range(4)], outs[-1]


def _cols_full(g):
    return jnp.moveaxis(g, 0, -2).reshape(g.shape[1:-1] + (N_DEV * g.shape[-1],))


def _cols_parts(full):
    n = full.shape[-1] // N_DEV
    return jnp.moveaxis(full.reshape(full.shape[:-1] + (N_DEV, n)), -2, 0)


def kernel(x, mem, norm1_g, w_in, ln_v_g, ln_v_b, w_spatial, b_spatial, lb_logits, hgrn_norm_g, mem_norm_g, w_mem_kv, w_branch, w_out, norm2_g, w_up, conv_w, conv_b, w_down, final_g, loss_target, m_norm1_g, m_w_in, m_ln_v_g, m_ln_v_b, m_w_spatial, m_b_spatial, m_lb_logits, m_hgrn_norm_g, m_mem_norm_g, m_w_mem_kv, m_w_branch, m_w_out, m_norm2_g, m_w_up, m_conv_w, m_conv_b, m_w_down, m_final_g, v_norm1_g, v_w_in, v_ln_v_g, v_ln_v_b, v_w_spatial, v_b_spatial, v_lb_logits, v_hgrn_norm_g, v_mem_norm_g, v_w_mem_kv, v_w_branch, v_w_out, v_norm2_g, v_w_up, v_conv_w, v_conv_b, v_w_down, v_final_g):
    given = dict(locals())
    order = ("norm1_g", "w_in", "ln_v_g", "ln_v_b", "w_spatial", "b_spatial", "lb_logits", "hgrn_norm_g", "mem_norm_g",
             "w_mem_kv", "w_branch", "w_out", "norm2_g", "w_up", "conv_w", "conv_b", "w_down", "final_g")
    groups = dict(a=("w_in",), b=("w_mem_kv", "w_branch", "w_out"), c=("w_up", "conv_w", "w_down"))

    by_rows = ("w_in", "w_up")
    shard_of = lambda n, prefix="": jnp.swapaxes(given[prefix + n][0], 0, 1) if n in by_rows else given[prefix + n][0]
    wire = {n: shard_of(n).astype(f32 if n == "conv_w" else bf16) for ns in groups.values() for n in ns}
    w_in_full = _all_gather([wire["w_in"]], "gather_w_in")[0].reshape(IN_WIDTH, D_MODEL)
    w_in_full, wire_b, wire_c = lax.optimization_barrier((w_in_full, [wire[n] for n in groups["b"]], [wire[n] for n in groups["c"]]))
    rest_b = _seq_exchange(wire_b, True, "gather_b", 1)
    rest_c = _seq_exchange(wire_c, True, "gather_c", 6)

    def late_b(after):
        _, (kv_, br_, out_) = lax.optimization_barrier((after, tuple(rest_b)))
        br_ = _cols_full(br_)
        return dict(w_mem_kv=kv_.reshape(D_MODEL, 2 * 512), w_branch=[br_[n] for n in range(3)], w_out=out_.reshape(D_MODEL, D_MODEL))

    def late_c(after):
        _, (up_, cw_, down_) = lax.optimization_barrier((after, tuple(rest_c)))
        return dict(w_up_t=up_.reshape(2 * D_FF, D_MODEL), conv_w=_cols_full(cw_), w_down=down_.reshape(D_FF, D_MODEL))

    to_parts = dict(w_in_half=lambda g_: g_.reshape(N_DEV, -1, D_MODEL // 2), w_up=lambda g_: g_.reshape(N_DEV, -1, D_MODEL), conv_w=_cols_parts,
                    w_branch=lambda g_: _cols_parts(g_.astype(bf16)).reshape(N_DEV, -1, 128),
                    w_mem_kv=lambda g_: g_.reshape(N_DEV, -1, 2 * 512), w_out=lambda g_: g_.astype(bf16).reshape(N_DEV, -1, D_MODEL),
                    w_down=lambda g_: g_.reshape(N_DEV, -1, D_MODEL))
    scatters = {}


    def send(tag, grads_):
        parts = [to_parts[n](g_) for n, g_ in grads_.items()]
        scatters[tag] = _seq_exchange(parts, False, f"scatter_{tag}", dict(a0=2, a1=7, b=4, c=5)[tag])

    small_2d = lambda prefix: {n: given[prefix + n].reshape(shape) for n, shape, _ in SMALL}
    p = small_2d("")
    p["w_spatial"] = w_spatial[0]
    updates = {}

    def update(tag):
        arrived = [scatters["a0"] + scatters["a1"]] if tag == "a" else [[parts] for parts in scatters[tag]]
        for n, parts in zip(groups[tag], arrived):
            state = [shard_of(n, pre) for pre in ("", "m_", "v_")]
            res = _reduce_adamw(parts, *[a.reshape(-1, a.shape[-1]) for a in state], "adamw_" + n)
            updates[n] = [jnp.swapaxes(r, 0, 1) for r in res] if n in by_rows else res

    def settle(chain):
        update("c")
        update("b")
        early = groups["c"] + groups["b"]
        chain, tied = lax.optimization_barrier((chain, [updates[n] for n in early]))
        updates.update(zip(early, tied))
        return chain

    loss_part, grad_x, gs = _local_step(x, mem, loss_target, p, w_in_full, late_b, late_c, send, settle)

    gathered = _seq_exchange([_pack_small(gs, loss_part)], True, "gather_small", 3)[0]

    update("a")
    grads, delta, new_m, new_v = {}, {}, {}, {}
    for n, res in updates.items():
        grads[n], delta[n], new_m[n], new_v[n] = [r.reshape(given[n].shape) for r in res]

    small_results, loss_row = _small_update(gathered, small_2d(""), small_2d("m_"), small_2d("v_"))
    for dst, res in zip((grads, delta, new_m, new_v), small_results):
        for n, _, _ in SMALL:
            dst[n] = res[n].reshape(given[n].shape)
    loss = loss_row[0, 0]

    return (loss, grad_x, *[grads[n] for n in order], *[delta[n] for n in order], *[new_m[n] for n in order],
            *[new_v[n] for n in order])
```

```python
import functools

import jax
import jax.numpy as jnp
from jax import lax
from jax.experimental import pallas as pl
from jax.experimental.pallas import tpu as pltpu
from jax.experimental.pallas import tpu_sc as plsc

f32 = jnp.float32
bf16 = jnp.bfloat16

N_DEV = 8
D_MODEL = 1024
EPS = 1e-6
GM_CHUNK = 128
HG_CHUNK = 64
HEAD = 128
N_HEAD = 4
MEM_LEN = 256
D_FF = 2816
IN_WIDTH = 6656
C_ZU, C_HQ, C_HF, C_HI, C_HG, C_XQ, C_GL = 0, 1024, 1536, 2048, 2560, 3072, 3584
ADAM_LR, ADAM_B1, ADAM_B2, ADAM_EPS, ADAM_WD, ADAM_STEP = 0.001, 0.9, 0.999, 1e-08, 0.01, 10
VMEM_LIMIT = 56 * 1024 * 1024
MESH = pl.DeviceIdType.MESH


def _pick(n, cands):
    for c in cands:
        if n % c == 0:
            return c
    return n


def _call(body, name, grid, in_specs, out_specs, out_shape, scratch=(), sem=None, **cp):
    params = dict(vmem_limit_bytes=VMEM_LIMIT, **cp)
    if sem is not None:
        params["dimension_semantics"] = sem
    return pl.pallas_call(
        body, name=name, grid=grid, in_specs=in_specs, out_specs=out_specs, out_shape=out_shape,
        scratch_shapes=list(scratch), compiler_params=pltpu.CompilerParams(**params))


_DN = {"nn": (((1,), (0,)), ((), ())), "nt": (((1,), (1,)), ((), ())), "tn": (((0,), (0,)), ((), ()))}


def _raw_dot(a, b, mode):
    return lax.dot_general(a.astype(bf16), b.astype(bf16), _DN[mode], preferred_element_type=f32)


@jax.custom_vjp
def _dot_nn(a, b):
    return _raw_dot(a, b, "nn")


_dot_nn.defvjp(lambda a, b: (_raw_dot(a, b, "nn"), (a, b)),
               lambda r, g: (_raw_dot(g, r[1], "nt"), _raw_dot(r[0], g, "tn")))


@jax.custom_vjp
def _dot_nt(a, b):
    return _raw_dot(a, b, "nt")


_dot_nt.defvjp(lambda a, b: (_raw_dot(a, b, "nt"), (a, b)),
               lambda r, g: (_raw_dot(g, r[1], "nn"), _raw_dot(g, r[0], "tn")))


@jax.custom_vjp
def _dot_tn(a, b):
    return _raw_dot(a, b, "tn")


_dot_tn.defvjp(lambda a, b: (_raw_dot(a, b, "tn"), (a, b)),
               lambda r, g: (_raw_dot(r[1], g, "nt"), _raw_dot(r[0], g, "nn")))


def _tri(n, lower):
    r = lax.broadcasted_iota(jnp.int32, (n, n), 0)
    c = lax.broadcasted_iota(jnp.int32, (n, n), 1)
    return ((c <= r) if lower else (c >= r)).astype(f32)


def _sel_dot(sel, x, mode, x_first=False, pieces=3):
    sel = sel.astype(bf16)
    out, rest = None, x
    for p in range(pieces):
        piece = rest.astype(bf16)
        part = lax.dot_general(*((piece, sel) if x_first else (sel, piece)), _DN[mode], preferred_element_type=f32)
        out = part if out is None else out + part
        if p + 1 < pieces:
            rest = rest - piece.astype(f32)
    return out


def _egrad(fn, x, ct):
    return jax.vjp(fn, x)[1](ct)[0]


def _mm(a, b, mode, out_dtype, name, tm, tn, tk=None, residual=None, into=None, n_tiles=None):
    if mode == "nn":
        (M, K), (_, N) = a.shape, b.shape
    elif mode == "nt":
        (M, K), (N, _) = a.shape, b.shape
    else:
        (K, M), (_, N) = a.shape, b.shape
    j0 = 0
    if n_tiles is not None:
        assert mode != "nt" and N % tn == 0 and residual is None
        j0, N = n_tiles[0], n_tiles[1] * tn
    tm, tn = min(tm, M), min(tn, N)
    tk = K if tk is None else min(tk, K)
    assert M % tm == 0 and N % tn == 0 and K % tk == 0, (name, M, N, K, tm, tn, tk)
    nk = K // tk

    def body(*refs):
        acc_ref = refs[-1] if nk > 1 else None
        refs = refs[:-1] if nk > 1 else refs
        if residual is None:
            a_ref, b_ref, *_, o_ref = refs
        else:
            a_ref, b_ref, r_ref, o_ref = refs

        def finish(r):
            if residual is not None:
                r = r + r_ref[...]
            o_ref[...] = r.astype(out_dtype)

        part = _raw_dot(a_ref[...], b_ref[...], mode)
        if nk == 1:
            finish(part)
            return
        k = pl.program_id(2)

        @pl.when(k == 0)
        def _():
            acc_ref[...] = part

        @pl.when((k > 0) & (k < nk - 1))
        def _():
            acc_ref[...] += part

        @pl.when(k == nk - 1)
        def _():
            finish(acc_ref[...] + part)

    a_spec = {"nn": pl.BlockSpec((tm, tk), lambda i, j, k: (i, k)),
              "nt": pl.BlockSpec((tm, tk), lambda i, j, k: (i, k)),
              "tn": pl.BlockSpec((tk, tm), lambda i, j, k: (k, i))}[mode]
    b_spec = {"nn": pl.BlockSpec((tk, tn), lambda i, j, k: (k, j + j0)),
              "nt": pl.BlockSpec((tn, tk), lambda i, j, k: (j, k)),
              "tn": pl.BlockSpec((tk, tn), lambda i, j, k: (k, j + j0))}[mode]
    o_spec = pl.BlockSpec((tm, tn), lambda i, j, k: (i, j))
    in_specs = [a_spec, b_spec] + ([o_spec] if residual is not None else [])
    args = (a, b) + ((residual,) if residual is not None else ())
    out_shape = jax.ShapeDtypeStruct((M, N), out_dtype)
    extra = {}
    if into is not None:
        assert residual is None and into[1] % tm == 0
        o_spec = pl.BlockSpec((tm, tn), lambda i, j, k: (i + into[1] // tm, j))
        out_shape = jax.ShapeDtypeStruct(into[0].shape, out_dtype)
        in_specs, args = in_specs + [pl.BlockSpec(memory_space=pl.ANY)], args + (into[0],)
        extra = dict(input_output_aliases={2: 0})
    return pl.pallas_call(
        body, name=name, grid=(M // tm, N // tn, nk), in_specs=in_specs, out_specs=o_spec, out_shape=out_shape,
        scratch_shapes=[pltpu.VMEM((tm, tn), f32)] if nk > 1 else [],
        compiler_params=pltpu.CompilerParams(vmem_limit_bytes=VMEM_LIMIT, dimension_semantics=("parallel", "parallel", "arbitrary")),
        **extra)(*args)


def _rms_fwd(x, g, name):
    R, Dd = x.shape
    tr = _pick(R, (512, 256, 128))

    def body(x_ref, g_ref, o_ref):
        xf = x_ref[...]
        o_ref[...] = (xf * lax.rsqrt(jnp.mean(xf * xf, axis=-1, keepdims=True) + EPS) * g_ref[...]).astype(bf16)

    row = pl.BlockSpec((tr, Dd), lambda i: (i, 0))
    return _call(body, name, (R // tr,), [row, pl.BlockSpec((1, Dd), lambda i: (0, 0))], row, jax.ShapeDtypeStruct((R, Dd), bf16),
                 sem=("parallel",))(x, g)


def _rms_bwd(x, g, dh, name, residual=None):
    R, Dd = x.shape
    tr = _pick(R, (512, 256, 128))

    def body(*refs):
        if residual is None:
            x_ref, g_ref, dh_ref, dx_ref, dg_ref = refs
        else:
            x_ref, g_ref, dh_ref, r_ref, dx_ref, dg_ref = refs
        xf = x_ref[...]
        rs = lax.rsqrt(jnp.mean(xf * xf, axis=-1, keepdims=True) + EPS)
        y = xf * rs
        dh_ = dh_ref[...].astype(f32)
        dy = dh_ * g_ref[...]
        dx = rs * (dy - y * jnp.mean(dy * y, axis=-1, keepdims=True))
        if residual is not None:
            dx = dx + r_ref[...]
        dx_ref[...] = dx

        @pl.when(pl.program_id(0) == 0)
        def _():
            dg_ref[...] = jnp.zeros_like(dg_ref)

        dg_ref[...] += jnp.sum(dh_ * y, axis=0, keepdims=True)

    row = pl.BlockSpec((tr, Dd), lambda i: (i, 0))
    vec = pl.BlockSpec((1, Dd), lambda i: (0, 0))
    in_specs = [row, vec, row] + ([row] if residual is not None else [])
    args = (x, g, dh) + ((residual,) if residual is not None else ())
    return _call(body, name, (R // tr,), in_specs, (row, vec),
                 (jax.ShapeDtypeStruct((R, Dd), f32), jax.ShapeDtypeStruct((1, Dd), f32)), sem=("arbitrary",))(*args)


def _mm_rms_bwd(pairs, x, g, residual, name, tm):
    M = x.shape[0]
    Dd = x.shape[1]
    tm = min(tm, M)
    n = len(pairs)

    def body(*refs):
        ab_refs, (x_ref, g_ref, r_ref, dx_ref, dg_ref) = refs[:2 * n], refs[2 * n:]
        dh_ = _raw_dot(ab_refs[0][...], ab_refs[1][...], "nn")
        for k in range(1, n):
            dh_ = dh_ + _raw_dot(ab_refs[2 * k][...], ab_refs[2 * k + 1][...], "nn")
        xf = x_ref[...]
        rs = lax.rsqrt(jnp.mean(xf * xf, axis=-1, keepdims=True) + EPS)
        y = xf * rs
        dy = dh_ * g_ref[...]
        dx_ref[...] = rs * (dy - y * jnp.mean(dy * y, axis=-1, keepdims=True)) + r_ref[...]

        @pl.when(pl.program_id(0) == 0)
        def _():
            dg_ref[...] = jnp.zeros_like(dg_ref)

        dg_ref[...] += jnp.sum(dh_ * y, axis=0, keepdims=True)

    row = pl.BlockSpec((tm, Dd), lambda i: (i, 0))
    vec = pl.BlockSpec((1, Dd), lambda i: (0, 0))
    in_specs, args = [], []
    for a, b, k in pairs:
        in_specs += [pl.BlockSpec((tm, a.shape[1]), lambda i: (i, 0)),
                     pl.BlockSpec((a.shape[1], b.shape[1]), functools.partial(lambda i, k_: (k_, 0), k_=k))]
        args += [a, b]
    in_specs += [row, vec, row]
    args += [x, g, residual]
    return _call(body, name, (M // tm,), in_specs, (row, vec),
                 (jax.ShapeDtypeStruct((M, Dd), f32), jax.ShapeDtypeStruct((1, Dd), f32)), sem=("arbitrary",))(*args)


def _down_final_loss(act, w_down, x1, g, target):
    R, Dd = x1.shape
    tr = _pick(R, (512, 256, 128))

    def body(a_ref, w_ref, x1_ref, g_ref, t_ref, loss_ref, dx_ref, dxb_ref, dg_ref):
        xf = _raw_dot(a_ref[...], w_ref[...], "nn") + x1_ref[...]
        rs = lax.rsqrt(jnp.mean(xf * xf, axis=-1, keepdims=True) + EPS)
        y = xf * rs
        err = y * g_ref[...] - t_ref[...]
        dh_ = err * (1.0 / Dd)
        dy = dh_ * g_ref[...]
        dx = rs * (dy - y * jnp.mean(dy * y, axis=-1, keepdims=True))
        dx_ref[...] = dx
        dxb_ref[...] = dx.astype(bf16)

        @pl.when(pl.program_id(0) == 0)
        def _():
            dg_ref[...] = jnp.zeros_like(dg_ref)
            loss_ref[...] = jnp.zeros_like(loss_ref)

        dg_ref[...] += jnp.sum(dh_ * y, axis=0, keepdims=True)
        part = jnp.sum(jnp.mean(err * err, axis=-1, keepdims=True), axis=0, keepdims=True)
        loss_ref[...] += 0.5 * part

    row = pl.BlockSpec((tr, Dd), lambda i: (i, 0))
    vec = pl.BlockSpec((1, Dd), lambda i: (0, 0))
    in_specs = [pl.BlockSpec((tr, act.shape[1]), lambda i: (i, 0)), pl.BlockSpec(w_down.shape, lambda i: (0, 0)), row, vec, row]
    return _call(body, "down_final_loss", (R // tr,), in_specs, (pl.BlockSpec((1, 128), lambda i: (0, 0)), row, row, vec),
                 (jax.ShapeDtypeStruct((1, 128), f32), jax.ShapeDtypeStruct((R, Dd), f32), jax.ShapeDtypeStruct((R, Dd), bf16),
                  jax.ShapeDtypeStruct((1, Dd), f32)), sem=("arbitrary",))(act, w_down, x1, g, target)


def _gmlp_parts(zuv, ln_g, ln_b):
    zu, zv = zuv[:, :512], zuv[:, 512:]
    u = jax.nn.gelu(zu)
    v = jax.nn.gelu(zv)
    mu = jnp.mean(v, axis=-1, keepdims=True)
    rs = lax.rsqrt(jnp.mean(jnp.square(v - mu), axis=-1, keepdims=True) + EPS)
    xh = (v - mu) * rs
    return zu, zv, u, xh, rs, xh * ln_g + ln_b


GM_TILE_CHUNKS = 4


def _gmlp_tile(T):
    n = _pick(T // GM_CHUNK, (GM_TILE_CHUNKS, 2, 1))
    return n, n * GM_CHUNK


def _gmlp_fwd(proj, ln_g, ln_b, w_s, b_st):
    T = proj.shape[0]
    nch, rows = _gmlp_tile(T)

    def body(p_ref, g_ref, b_ref, w_ref, bs_ref, o_ref):
        _, _, u, _, _, vn = _gmlp_parts(p_ref[...].astype(f32), g_ref[...], b_ref[...])
        causal = _tri(GM_CHUNK, True) > 0
        for gi in range(N_HEAD):
            sl = slice(gi * HEAD, (gi + 1) * HEAD)
            w = jnp.where(causal, w_ref[gi], 0.0)
            for ch in range(nch):
                rs_ = slice(ch * GM_CHUNK, (ch + 1) * GM_CHUNK)
                mixed = _raw_dot(w, vn[rs_, sl], "nn") + bs_ref[:, gi:gi + 1]
                o_ref[rs_, sl] = (u[rs_, sl] * mixed).astype(bf16)

    vec = pl.BlockSpec((1, 512), lambda i: (0, 0))
    return _call(body, "gmlp_fwd", (T // rows,),
                 [pl.BlockSpec((rows, 1024), lambda i: (i, 0)), vec, vec,
                  pl.BlockSpec((N_HEAD, GM_CHUNK, GM_CHUNK), lambda i: (0, 0, 0)), pl.BlockSpec((GM_CHUNK, 128), lambda i: (0, 0))],
                 pl.BlockSpec((rows, 512), lambda i: (i, 0)), jax.ShapeDtypeStruct((T, 512), bf16), sem=("parallel",))(
        proj, ln_g, ln_b, w_s, b_st)


def _gmlp_bwd(proj, ln_g, ln_b, w_s, b_st, da):
    T = proj.shape[0]
    nch, rows = _gmlp_tile(T)

    def body(p_ref, g_ref, b_ref, w_ref, bs_ref, da_ref, dp_ref, dg_ref, db_ref, dw_ref, dbs_ref):
        zu, zv, u, xh, rs, vn = _gmlp_parts(p_ref[...].astype(f32), g_ref[...], b_ref[...])
        causal = _tri(GM_CHUNK, True) > 0
        sub = lax.broadcasted_iota(jnp.int32, (8, GM_CHUNK), 0)
        ones = jnp.ones((8, HEAD), f32)
        dout = da_ref[...].astype(f32)

        @pl.when(pl.program_id(0) == 0)
        def _():
            for r in (dg_ref, db_ref, dw_ref, dbs_ref):
                r[...] = jnp.zeros_like(r)

        du, dvn, dbs = [], [], jnp.zeros((8, GM_CHUNK), f32)
        for gi in range(N_HEAD):
            sl = slice(gi * HEAD, (gi + 1) * HEAD)
            w = jnp.where(causal, w_ref[gi], 0.0)
            du_g, dvn_g, dw_g = [], [], jnp.zeros((GM_CHUNK, GM_CHUNK), f32)
            for ch in range(nch):
                rs_ = slice(ch * GM_CHUNK, (ch + 1) * GM_CHUNK)
                mixed = _raw_dot(w, vn[rs_, sl], "nn") + bs_ref[:, gi:gi + 1]
                du_g.append(dout[rs_, sl] * mixed)
                dm = dout[rs_, sl] * u[rs_, sl]
                dbs = dbs + jnp.where(sub == gi, _sel_dot(ones, dm, "nt"), 0.0)
                dw_g = dw_g + _raw_dot(dm, vn[rs_, sl], "nt")
                dvn_g.append(_raw_dot(w, dm, "tn"))
            dw_ref[gi] += jnp.where(causal, dw_g, 0.0)
            du.append(jnp.concatenate(du_g, axis=0))
            dvn.append(jnp.concatenate(dvn_g, axis=0))
        dbs_ref[...] += dbs
        du = jnp.concatenate(du, axis=-1)
        dvn = jnp.concatenate(dvn, axis=-1)
        dg_ref[...] += jnp.sum(dvn * xh, axis=0, keepdims=True)
        db_ref[...] += jnp.sum(dvn, axis=0, keepdims=True)
        dxh = dvn * g_ref[...]
        dv = rs * (dxh - jnp.mean(dxh, axis=-1, keepdims=True) - xh * jnp.mean(dxh * xh, axis=-1, keepdims=True))
        dp_ref[:, :512] = _egrad(jax.nn.gelu, zu, du).astype(bf16)
        dp_ref[:, 512:] = _egrad(jax.nn.gelu, zv, dv).astype(bf16)

    vec = pl.BlockSpec((1, 512), lambda i: (0, 0))
    wsp = pl.BlockSpec((N_HEAD, GM_CHUNK, GM_CHUNK), lambda i: (0, 0, 0))
    return _call(body, "gmlp_bwd", (T // rows,),
                 [pl.BlockSpec((rows, 1024), lambda i: (i, 0)), vec, vec, wsp, pl.BlockSpec((GM_CHUNK, 128), lambda i: (0, 0)),
                  pl.BlockSpec((rows, 512), lambda i: (i, 0))],
                 (pl.BlockSpec((rows, 1024), lambda i: (i, 0)), vec, vec, wsp, pl.BlockSpec((8, GM_CHUNK), lambda i: (0, 0))),
                 (jax.ShapeDtypeStruct((T, 1024), bf16), jax.ShapeDtypeStruct((1, 512), f32), jax.ShapeDtypeStruct((1, 512), f32),
                  jax.ShapeDtypeStruct((N_HEAD, GM_CHUNK, GM_CHUNK), f32), jax.ShapeDtypeStruct((8, GM_CHUNK), f32)),
                 sem=("arbitrary",))(proj, ln_g, ln_b, w_s, b_st, da)


HG_SUB = 8
HG_NSUB = HG_CHUNK // HG_SUB


def _two_level_matrix(transposed=False):
    shape = (HG_CHUNK, 2 * HG_CHUNK) if transposed else (2 * HG_CHUNK, HG_CHUNK)
    r = lax.broadcasted_iota(jnp.int32, shape, 1 if transposed else 0)
    c = lax.broadcasted_iota(jnp.int32, shape, 0 if transposed else 1)
    t = jnp.where(r < HG_CHUNK, r, r - HG_CHUNK)
    local = (r < HG_CHUNK) & (t // HG_SUB == c // HG_SUB) & (c <= t)
    before = (r >= HG_CHUNK) & (c < (t // HG_SUB) * HG_SUB)
    return (local | before).astype(f32)


def _two_level_sums(x):
    two = _sel_dot(_two_level_matrix(), x, "nn")
    return two[:HG_CHUNK], two[HG_CHUNK:]


@jax.custom_vjp
def _two_level_cumsum(x):
    return _two_level_sums(x)


_two_level_cumsum.defvjp(
    lambda x: (_two_level_sums(x), None),
    lambda _, g: (_sel_dot(_two_level_matrix(), jnp.concatenate(g, axis=0), "tn"),))


def _tile_matrix():
    s = lax.broadcasted_iota(jnp.int32, (HG_SUB, HG_CHUNK), 0)
    j = lax.broadcasted_iota(jnp.int32, (HG_SUB, HG_CHUNK), 1)
    return (j % HG_SUB == s).astype(f32)


@jax.custom_vjp
def _tile_lanes(x):
    return _sel_dot(_tile_matrix(), x, "nn", x_first=True, pieces=1)


_tile_lanes.defvjp(
    lambda x: (_sel_dot(_tile_matrix(), x, "nn", x_first=True, pieces=1), None),
    lambda _, g: (_sel_dot(_tile_matrix(), g, "nt", x_first=True, pieces=2),))


def _block_rows(x):
    k = x.shape[-1]
    return jnp.broadcast_to(x.reshape(HG_NSUB, 1, HG_SUB, k), (HG_NSUB, HG_SUB, HG_SUB, k)).reshape(HG_CHUNK, HG_SUB, k)


def _hgrn_chunk(st0, q_raw, f_raw, i_raw, g_raw, l0, l1, ng):
    C, SUB = HG_CHUNK, HG_SUB
    lb = jax.nn.sigmoid(l0 - l1)
    fg = lb + (1.0 - lb) * jax.nn.sigmoid(f_raw)
    kk = 1.0 - fg
    qf = jax.nn.silu(q_raw)
    al, base = _two_level_cumsum(jnp.log(fg))
    a = al + base
    row = lax.broadcasted_iota(jnp.int32, (C, HEAD), 0)
    a_last = jnp.sum(jnp.where(row == C - 1, a, 0.0), axis=0, keepdims=True)
    inter = _dot_nt(qf * jnp.exp(a), st0)
    qt = qf * jnp.exp(al)
    rb = lax.broadcasted_iota(jnp.int32, (C, C), 0) // SUB
    cb = lax.broadcasted_iota(jnp.int32, (C, C), 1) // SUB
    scores = jnp.zeros((C, C), f32)
    for i in range(1, HG_NSUB):
        base_i = jnp.sum(jnp.where(row == i * SUB, base, 0.0), axis=0, keepdims=True)
        kt = kk * jnp.exp(jnp.minimum(base_i - a, 0.0))
        scores = scores + jnp.where((rb == i) & (cb < i), _dot_nt(qt, kt), 0.0)
    t_i = lax.broadcasted_iota(jnp.int32, (C, SUB, HEAD), 0) % SUB
    s_i = lax.broadcasted_iota(jnp.int32, (C, SUB, HEAD), 1)
    decay = jnp.exp(jnp.where(s_i <= t_i, al[:, None, :] - _block_rows(al), -jnp.inf))
    diag = jnp.sum(qf[:, None, :] * decay * _block_rows(kk), axis=-1)
    scores = scores + jnp.where(rb == cb, _tile_lanes(diag), 0.0)
    o = inter + _dot_nn(scores, i_raw)
    st1 = jnp.exp(a_last) * st0 + _dot_tn(i_raw, kk * jnp.exp(a_last - a))
    on = o * lax.rsqrt(jnp.mean(o * o, axis=-1, keepdims=True) + EPS) * ng
    return st1, on * jax.nn.silu(g_raw)


def _hgrn_specs(S, Bl, rev):
    N = S // HG_CHUNK
    chunk = (lambda n: N - 1 - n) if rev else (lambda n: n)
    col = lambda c0: pl.BlockSpec((Bl, HG_CHUNK, 512), lambda n: (0, chunk(n), c0 // 512))
    st = pl.BlockSpec((Bl, N_HEAD, 1, HEAD, HEAD), lambda n: (0, 0, chunk(n), 0, 0))
    full = lambda *s: pl.BlockSpec(s, functools.partial(lambda n, nd: (0,) * nd, nd=len(s)))
    return N, col, st, full


def _hgrn_fwd(proj, lb_logits, ng, Bl, S):
    N, col, st, full = _hgrn_specs(S, Bl, False)

    def body(q_ref, f_ref, i_ref, g_ref, l_ref, ng_ref, o_ref, st_ref, state):
        @pl.when(pl.program_id(0) == 0)
        def _():
            state[...] = jnp.zeros_like(state)

        for b in range(Bl):
            for h in range(N_HEAD):
                sl = slice(h * HEAD, (h + 1) * HEAD)
                st0 = state[b, h]
                st_ref[b, h, 0] = st0
                st1, out = _hgrn_chunk(st0, *[r[b, :, sl].astype(f32) for r in (q_ref, f_ref, i_ref, g_ref)],
                                       l_ref[0:1, sl], l_ref[1:2, sl], ng_ref[...])
                state[b, h] = st1
                o_ref[b, :, sl] = out.astype(bf16)

    return _call(body, "hgrn_fwd", (N,), [col(C_HQ), col(C_HF), col(C_HI), col(C_HG), full(2, 512), full(1, HEAD)],
                 (col(0), st),
                 (jax.ShapeDtypeStruct((Bl, S, 512), bf16), jax.ShapeDtypeStruct((Bl, N_HEAD, N, HEAD, HEAD), f32)),
                 scratch=[pltpu.VMEM((Bl, N_HEAD, HEAD, HEAD), f32)], sem=("arbitrary",))(
        proj, proj, proj, proj, lb_logits, ng)


def _hgrn_bwd(proj, lb_logits, ng, states, db, dzuv, dxq, dgl, Bl, S):
    N, col, st, full = _hgrn_specs(S, Bl, True)
    rows = lambda width: pl.BlockSpec((Bl, HG_CHUNK, width), lambda n: (0, N - 1 - n, 0))

    def body(q_ref, f_ref, i_ref, g_ref, l_ref, ng_ref, st_ref, db_ref, dzuv_ref, dxq_ref, dgl_ref,
             dp_ref, dl_ref, dng_ref, dstate):
        @pl.when(pl.program_id(0) == 0)
        def _():
            dstate[...] = jnp.zeros_like(dstate)
            dl_ref[...] = jnp.zeros_like(dl_ref)
            dng_ref[...] = jnp.zeros_like(dng_ref)

        dp_ref[:, :, C_ZU:C_HQ] = dzuv_ref[...]
        dp_ref[:, :, C_XQ:C_GL] = dxq_ref[...]
        for n in range(3):
            dp_ref[:, :, C_GL + n * D_MODEL:C_GL + (n + 1) * D_MODEL] = dgl_ref[n]
        dq_ref, df_ref, di_ref, dg_ref = [dp_ref.at[:, :, c0:c0 + 512] for c0 in (C_HQ, C_HF, C_HI, C_HG)]
        for b in range(Bl):
            for h in range(N_HEAD):
                sl = slice(h * HEAD, (h + 1) * HEAD)
                _, vjp = jax.vjp(_hgrn_chunk, st_ref[b, h, 0], *[r[b, :, sl].astype(f32) for r in (q_ref, f_ref, i_ref, g_ref)],
                                 l_ref[0:1, sl], l_ref[1:2, sl], ng_ref[...])
                dst0, dq, df, di, dg, dl0, dl1, dng = vjp((dstate[b, h], db_ref[b, :, sl].astype(f32)))
                dstate[b, h] = dst0
                dq_ref[b, :, sl] = dq.astype(bf16)
                df_ref[b, :, sl] = df.astype(bf16)
                di_ref[b, :, sl] = di.astype(bf16)
                dg_ref[b, :, sl] = dg.astype(bf16)
                dl_ref[0:1, sl] += dl0
                dl_ref[1:2, sl] += dl1
                dng_ref[b, h] += dng

    return _call(body, "hgrn_bwd", (N,),
                 [col(C_HQ), col(C_HF), col(C_HI), col(C_HG), full(2, 512), full(1, HEAD), st, col(0), rows(C_HQ - C_ZU),
                  rows(C_GL - C_XQ), pl.BlockSpec((3, Bl, HG_CHUNK, D_MODEL), lambda n: (0, 0, N - 1 - n, 0))],
                 (rows(IN_WIDTH), full(2, 512), full(Bl, N_HEAD, 1, HEAD)),
                 (jax.ShapeDtypeStruct((Bl, S, IN_WIDTH), bf16), jax.ShapeDtypeStruct((2, 512), f32),
                  jax.ShapeDtypeStruct((Bl, N_HEAD, 1, HEAD), f32)),
                 scratch=[pltpu.VMEM((Bl, N_HEAD, HEAD, HEAD), f32)], sem=("arbitrary",))(
        proj, proj, proj, proj, lb_logits, ng, states, db, dzuv, dxq, dgl)


def _attn_probs(q, k):
    s = _raw_dot(q, k, "nt") * (HEAD ** -0.5)
    e = jnp.exp(s - jnp.max(s, axis=-1, keepdims=True))
    return e / jnp.sum(e, axis=-1, keepdims=True)


def _attn_specs(S, tq):
    nq = S // tq
    q = pl.BlockSpec((tq, 512), lambda b, i: (b * nq + i, C_XQ // 512))
    kv = pl.BlockSpec((1, MEM_LEN, 1024), lambda b, i: (b, 0, 0))
    o = pl.BlockSpec((tq, 512), lambda b, i: (b * nq + i, 0))
    return nq, q, kv, o


def _attn_fwd(proj, kv, Bl, S):
    tq = _pick(S, (512, 256, 128))
    nq, qs, kvs, os_ = _attn_specs(S, tq)

    def body(q_ref, kv_ref, o_ref):
        for h in range(N_HEAD):
            sl = slice(h * HEAD, (h + 1) * HEAD)
            p = _attn_probs(q_ref[:, sl], kv_ref[0, :, sl])
            o_ref[:, sl] = _raw_dot(p, kv_ref[0, :, 512 + h * HEAD:512 + (h + 1) * HEAD], "nn").astype(bf16)

    return _call(body, "attn_fwd", (Bl, nq), [qs, kvs], os_, jax.ShapeDtypeStruct((Bl * S, 512), bf16),
                 sem=("parallel", "parallel"))(proj, kv)


def _attn_bwd(proj, kv, dc, Bl, S):
    tq = _pick(S, (512, 256, 128))
    nq, qs, kvs, os_ = _attn_specs(S, tq)

    def body(q_ref, kv_ref, do_ref, dq_ref, dkv_ref):
        @pl.when(pl.program_id(1) == 0)
        def _():
            dkv_ref[...] = jnp.zeros_like(dkv_ref)

        for h in range(N_HEAD):
            sl = slice(h * HEAD, (h + 1) * HEAD)
            vsl = slice(512 + h * HEAD, 512 + (h + 1) * HEAD)
            q, k, v, do = q_ref[:, sl], kv_ref[0, :, sl], kv_ref[0, :, vsl], do_ref[:, sl]
            p = _attn_probs(q, k)
            dkv_ref[0, :, vsl] += _raw_dot(p, do, "tn")
            dp = _raw_dot(do, v, "nt")
            ds = p * (dp - jnp.sum(dp * p, axis=-1, keepdims=True)) * (HEAD ** -0.5)
            dq_ref[:, sl] = _raw_dot(ds, k, "nn").astype(bf16)
            dkv_ref[0, :, sl] += _raw_dot(ds, q, "tn")

    return _call(body, "attn_bwd", (Bl, nq), [qs, kvs, os_], (os_, kvs),
                 (jax.ShapeDtypeStruct((Bl * S, 512), bf16), jax.ShapeDtypeStruct((Bl, MEM_LEN, 1024), f32)),
                 sem=("arbitrary", "arbitrary"))(proj, kv, dc)


def _gate_specs(tm):
    half = D_MODEL // 2
    return [pl.BlockSpec((tm, half), functools.partial(lambda i, c: (i, c), c=(C_GL + n * D_MODEL) // half + k))
            for n in range(3) for k in range(2)]


def _merge_out_norm_fwd(branches, wb, proj, w_out, x, g):
    T = proj.shape[0]
    tm = _pick(T, (512, 256, 128))

    def body(a_ref, b_ref, c_ref, w0, w1, w2, g0a, g0b, g1a, g1b, g2a, g2b, wo_ref, x_ref, g_ref, m_ref, x1_ref, h_ref):
        acc = jnp.zeros((tm, D_MODEL), f32)
        for x_n, w_ref, ga, gb in ((a_ref, w0, g0a, g0b), (b_ref, w1, g1a, g1b), (c_ref, w2, g2a, g2b)):
            gate = jax.nn.sigmoid(jnp.concatenate([ga[...], gb[...]], axis=-1).astype(f32))
            acc = acc + gate * _raw_dot(x_n[...], w_ref[...], "nn")
        merged = acc.astype(bf16)
        m_ref[...] = merged
        x1 = x_ref[...] + _raw_dot(merged, wo_ref[...], "nn")
        x1_ref[...] = x1
        y = x1 * lax.rsqrt(jnp.mean(x1 * x1, axis=-1, keepdims=True) + EPS) * g_ref[...]
        h_ref[...] = y.astype(bf16)

    br = pl.BlockSpec((tm, 512), lambda i: (i, 0))
    w = pl.BlockSpec((512, D_MODEL), lambda i: (0, 0))
    row = pl.BlockSpec((tm, D_MODEL), lambda i: (i, 0))
    return _call(body, "merge_out_norm_fwd", (T // tm,),
                 [br, br, br, w, w, w, *_gate_specs(tm), pl.BlockSpec((D_MODEL, D_MODEL), lambda i: (0, 0)), row,
                  pl.BlockSpec((1, D_MODEL), lambda i: (0, 0))],
                 (row, row, row),
                 (jax.ShapeDtypeStruct((T, D_MODEL), bf16), jax.ShapeDtypeStruct((T, D_MODEL), f32),
                  jax.ShapeDtypeStruct((T, D_MODEL), bf16)),
                 sem=("parallel",))(*branches, *wb, *[proj] * 6, w_out, x, g)


def _merge_bwd(branches, wb, proj, merged, dx1, w_out):
    T = proj.shape[0]
    tm = _pick(T, (256, 128))

    def body(a_ref, b_ref, c_ref, w0, w1, w2, g0a, g0b, g1a, g1b, g2a, g2b, m_ref, dx_ref, wo_ref, dgl_ref, d0, d1, d2, gw_ref, gwo_ref):
        @pl.when(pl.program_id(0) == 0)
        def _():
            gw_ref[...] = jnp.zeros_like(gw_ref)
            gwo_ref[...] = jnp.zeros_like(gwo_ref)

        dx = dx_ref[...].astype(bf16)
        gwo_ref[...] += _raw_dot(m_ref[...], dx, "tn")
        dm = _raw_dot(dx, wo_ref[...], "nt")
        for n, (x_ref, w_ref, ga, gb, d_ref) in enumerate(((a_ref, w0, g0a, g0b, d0), (b_ref, w1, g1a, g1b, d1), (c_ref, w2, g2a, g2b, d2))):
            x, w = x_ref[...], w_ref[...]
            up = _raw_dot(x, w, "nn")
            sg = jax.nn.sigmoid(jnp.concatenate([ga[...], gb[...]], axis=-1).astype(f32))
            dgl_ref[n] = (dm * up * sg * (1.0 - sg)).astype(bf16)
            dup = (dm * sg).astype(bf16)
            d_ref[...] = _raw_dot(dup, w, "nt").astype(bf16)
            gw_ref[n] += _raw_dot(x, dup, "tn")

    br = pl.BlockSpec((tm, 512), lambda i: (i, 0))
    w = pl.BlockSpec((512, D_MODEL), lambda i: (0, 0))
    sh = jax.ShapeDtypeStruct((T, 512), bf16)
    row = pl.BlockSpec((tm, D_MODEL), lambda i: (i, 0))
    square = pl.BlockSpec((D_MODEL, D_MODEL), lambda i: (0, 0))
    outs = _call(body, "merge_bwd", (T // tm,), [br, br, br, w, w, w, *_gate_specs(tm), row, row, square],
                 (pl.BlockSpec((3, tm, D_MODEL), lambda i: (0, i, 0)), br, br, br, pl.BlockSpec((3, 512, D_MODEL), lambda i: (0, 0, 0)), square),
                 (jax.ShapeDtypeStruct((3, T, D_MODEL), bf16), sh, sh, sh, jax.ShapeDtypeStruct((3, 512, D_MODEL), f32),
                  jax.ShapeDtypeStruct((D_MODEL, D_MODEL), f32)),
                 sem=("arbitrary",))(*branches, *wb, *[proj] * 6, merged, dx1, w_out)
    return outs[0], outs[1:4], outs[4], outs[5]


CONV_TC = 256


def _shift_down(a, k):
    r = pltpu.roll(a, k, 0)
    row = lax.broadcasted_iota(jnp.int32, (8, a.shape[1]), 0)
    return jnp.concatenate([jnp.where(row >= k, r[:8], 0.0), r[8:]], axis=0)


def _shift_up(a, k):
    n = a.shape[0]
    r = pltpu.roll(a, n - k, 0)
    row = lax.broadcasted_iota(jnp.int32, (8, a.shape[1]), 0)
    return jnp.concatenate([r[:n - 8], jnp.where(row < 8 - k, r[n - 8:], 0.0)], axis=0)


def _conv_pre(a, a1, a2, cw, cb):
    return cb + cw[0:1] * a2 + cw[1:2] * a1 + cw[2:3] * a


def _up_conv_fwd(h2, w_up_t, cw, cb):
    Bl, S, Dd = h2.shape
    nc = D_FF // CONV_TC

    def body(h_ref, wa_ref, wb_ref, cw_ref, cb_ref, a_ref, b_ref, o_ref):
        a16 = _raw_dot(h_ref[0], wa_ref[...], "nt").astype(bf16)
        b16 = _raw_dot(h_ref[0], wb_ref[...], "nt").astype(bf16)
        a_ref[0], b_ref[0] = a16, b16
        a = a16.astype(f32)
        ac = _conv_pre(a, _shift_down(a, 1), _shift_down(a, 2), cw_ref[...], cb_ref[...])
        o_ref[0] = (jax.nn.silu(ac) * b16.astype(f32)).astype(bf16)

    seq = pl.BlockSpec((1, S, CONV_TC), lambda b, c: (b, 0, c))
    sh = jax.ShapeDtypeStruct((Bl, S, D_FF), bf16)
    return _call(body, "up_conv_fwd", (Bl, nc),
                 [pl.BlockSpec((1, S, Dd), lambda b, c: (b, 0, 0)), pl.BlockSpec((CONV_TC, Dd), lambda b, c: (c, 0)),
                  pl.BlockSpec((CONV_TC, Dd), lambda b, c: (nc + c, 0)), pl.BlockSpec((3, CONV_TC), lambda b, c: (0, c)),
                  pl.BlockSpec((1, CONV_TC), lambda b, c: (0, c))],
                 (seq, seq, seq), (sh, sh, sh), sem=("parallel", "parallel"))(h2, w_up_t, w_up_t, cw, cb)


def _down_conv_bwd(dx2, w_down, a, b, cw, cb):
    Bl, S, Dd = dx2.shape
    nc = D_FF // CONV_TC

    def body(dx_ref, wd_ref, a_ref, b_ref, cw_ref, cb_ref, da_ref, db_ref, dcw_ref, dcb_ref):
        dact = _raw_dot(dx_ref[0], wd_ref[...], "nt").astype(bf16).astype(f32)
        a, cw = a_ref[0].astype(f32), cw_ref[...]
        a1, a2 = _shift_down(a, 1), _shift_down(a, 2)
        ac = _conv_pre(a, a1, a2, cw, cb_ref[...])
        sg = jax.nn.sigmoid(ac)
        gated = dact * sg
        db_ref[0] = (gated * ac).astype(bf16)
        dac = gated * b_ref[0].astype(f32) * (1.0 + ac * (1.0 - sg))
        da_ref[0] = (cw[2:3] * dac + cw[1:2] * _shift_up(dac, 1) + cw[0:1] * _shift_up(dac, 2)).astype(bf16)
        dcw_ref[0, 0:1, :] = jnp.sum(dac * a2, axis=0, keepdims=True)
        dcw_ref[0, 1:2, :] = jnp.sum(dac * a1, axis=0, keepdims=True)
        dcw_ref[0, 2:3, :] = jnp.sum(dac * a, axis=0, keepdims=True)
        dcb_ref[0] = jnp.sum(dac, axis=0, keepdims=True)

    seq = pl.BlockSpec((1, S, CONV_TC), lambda b_, c: (b_, 0, c))
    sh = jax.ShapeDtypeStruct((Bl, S, D_FF), bf16)
    return _call(body, "down_conv_bwd", (Bl, nc),
                 [pl.BlockSpec((1, S, Dd), lambda b_, c: (b_, 0, 0)), pl.BlockSpec((CONV_TC, Dd), lambda b_, c: (c, 0)), seq, seq,
                  pl.BlockSpec((3, CONV_TC), lambda b_, c: (0, c)), pl.BlockSpec((1, CONV_TC), lambda b_, c: (0, c))],
                 (seq, seq, pl.BlockSpec((1, 3, CONV_TC), lambda b_, c: (b_, 0, c)), pl.BlockSpec((1, 1, CONV_TC), lambda b_, c: (b_, 0, c))),
                 (sh, sh, jax.ShapeDtypeStruct((Bl, 3, D_FF), f32), jax.ShapeDtypeStruct((Bl, 1, D_FF), f32)),
                 sem=("parallel", "parallel"))(dx2, w_down, a, b, cw, cb)


def _local_step(x, mem, target, p, w_in_t, late_b, late_c, send):
    Bl, S, Dd = x.shape
    T = Bl * S
    x2d, t2d, mem2d = x.reshape(T, Dd), target.reshape(T, Dd), mem.reshape(Bl * MEM_LEN, Dd)
    b_st = jnp.pad(p["b_spatial"].T, ((0, 0), (0, 128 - N_HEAD)))
    lbl = p["lb_logits"]

    h = _rms_fwd(x2d, p["norm1_g"], "norm1_fwd")
    proj = _mm(h, w_in_t, "nt", bf16, "proj_fwd", 1024, 1664)
    a_out = _gmlp_fwd(proj, p["ln_v_g"], p["ln_v_b"], p["w_spatial"], b_st)
    proj3 = proj.reshape(Bl, S, IN_WIDTH)
    b_out, states = _hgrn_fwd(proj3, lbl, p["hgrn_norm_g"], Bl, S)
    b_out = b_out.reshape(T, 512)
    memn = _rms_fwd(mem2d, p["mem_norm_g"], "memnorm_fwd")
    w = late_b(b_out)
    wb = w["w_branch"]
    kv = _mm(memn, w["w_mem_kv"], "nn", f32, "kv_fwd", 512, 1024).reshape(Bl, MEM_LEN, 2 * 512)
    c_out = _attn_fwd(proj, kv, Bl, S)
    branches = (a_out, b_out, c_out)
    merged, x1, h2 = _merge_out_norm_fwd(branches, wb, proj, w["w_out"], x2d, p["norm2_g"])
    w.update(late_c(h2))
    ffn_a, ffn_b, act = _up_conv_fwd(h2.reshape(Bl, S, Dd), w["w_up_t"], w["conv_w"], p["conv_b"])
    act = act.reshape(T, D_FF)
    loss_part, dx2, dx2_16, g_final = _down_final_loss(act, w["w_down"], x1, p["final_g"], t2d)

    g_w_down = _mm(act, dx2_16, "tn", bf16, "down_dw", 1408, 1024, 1024)
    da, db, g_conv_w, g_conv_b = _down_conv_bwd(dx2_16.reshape(Bl, S, Dd), w["w_down"], ffn_a, ffn_b, w["conv_w"], p["conv_b"])
    da, db = da.reshape(T, D_FF), db.reshape(T, D_FF)
    g_w_up_t = _mm(da, h2, "tn", bf16, "up_dw_a", 1408, 1024, 1024, into=(lax.empty((2 * D_FF, D_MODEL), bf16), 0))
    g_w_up_t = _mm(db, h2, "tn", bf16, "up_dw_b", 1408, 1024, 1024, into=(g_w_up_t, D_FF))
    send("c", dict(w_up=g_w_up_t, conv_w=jnp.sum(g_conv_w, axis=0), w_down=g_w_down))
    dx1, g_norm2 = _mm_rms_bwd([(da, w["w_up_t"], 0), (db, w["w_up_t"], 1)], x1, p["norm2_g"], dx2, "up_dx_norm2_bwd", 256)

    dgl, dbr, g_w_branch, g_w_out = _merge_bwd(branches, wb, proj, merged, dx1, w["w_out"])
    dxq, dkv = _attn_bwd(proj, kv, dbr[2], Bl, S)
    dkv = dkv.reshape(Bl * MEM_LEN, 2 * 512)
    g_w_kv = _mm(memn, dkv, "tn", bf16, "kv_dw", 1024, 1024, 512)
    send("b", dict(w_mem_kv=g_w_kv, w_branch=g_w_branch, w_out=g_w_out))
    dzuv, g_ln_g, g_ln_b, g_w_sp, g_b_sp = _gmlp_bwd(proj, p["ln_v_g"], p["ln_v_b"], p["w_spatial"], b_st, dbr[0])
    dproj, g_lbl, g_ng = _hgrn_bwd(proj3, lbl, p["hgrn_norm_g"], states, dbr[1].reshape(Bl, S, 512), dzuv.reshape(Bl, S, -1),
                                   dxq.reshape(Bl, S, -1), dgl.reshape(3, Bl, S, Dd), Bl, S)
    dproj = dproj.reshape(T, IN_WIDTH)
    half = Dd // 2
    send("a0", dict(w_in_half=_mm(dproj, h, "tn", bf16, "proj_dw_0", 512, half, n_tiles=(0, 1))))
    g_half = _mm(dproj, h, "tn", bf16, "proj_dw_1", 512, half, n_tiles=(1, 1))
    dkv, g_half = lax.optimization_barrier((dkv, g_half))
    send("a1", dict(w_in_half=g_half))
    dmemn = _mm(dkv, w["w_mem_kv"], "nt", f32, "kv_dx", 512, 1024)
    _, g_mem_norm = _rms_bwd(mem2d, p["mem_norm_g"], dmemn, "memnorm_bwd")
    dx, g_norm1 = _mm_rms_bwd([(dproj, w_in_t, 0)], x2d, p["norm1_g"], dx1, "proj_dx_norm1_bwd", 256)

    gs = dict(w_spatial=g_w_sp, norm1_g=g_norm1, mem_norm_g=g_mem_norm, norm2_g=g_norm2, final_g=g_final, lb_logits=g_lbl,
              ln_v_g=g_ln_g, ln_v_b=g_ln_b, b_spatial=g_b_sp, hgrn_norm_g=g_ng, conv_b=g_conv_b)
    return loss_part, dx.reshape(Bl, S, Dd), gs


def _coords():
    return lax.axis_index("x"), lax.axis_index("y"), lax.axis_index("c")


def _slot(dev):
    return 4 * dev[0] + 2 * dev[1] + dev[2]


def _comm_call(body, name, arrays, out_shapes, n_sem):
    n = len(arrays)
    hbm = pl.BlockSpec(memory_space=pl.ANY)
    return pl.pallas_call(
        body, name=name, out_shape=out_shapes, in_specs=[hbm] * n, out_specs=[hbm] * n,
        scratch_shapes=[pltpu.SemaphoreType.DMA((n_sem, n)), pltpu.SemaphoreType.DMA((n_sem, n)), pltpu.SemaphoreType.DMA((n,))])(*arrays)


def _all_gather(blocks, name):
    n = len(blocks)

    def body(*refs):
        x_refs, o_refs, (send_sems, recv_sems, local_sems) = refs[:n], refs[n:2 * n], refs[2 * n:]
        x, y, c = _coords()
        me, sibling = (x, y, c), (x, y, 1 - c)
        chips = [(1 - x, y), (x, 1 - y), (1 - x, 1 - y)]

        def copy(a, k, block_dev, to, from_input=False):
            dst = o_refs[a].at[_slot(block_dev)]
            return pltpu.make_async_remote_copy(src_ref=x_refs[a] if from_input else dst, dst_ref=dst, send_sem=send_sems.at[k, a],
                                                recv_sem=recv_sems.at[k, a], device_id=to, device_id_type=MESH)

        mine = [pltpu.make_async_copy(x_refs[a], o_refs[a].at[_slot(me)], local_sems.at[a]) for a in range(n)]
        first = [copy(a, 0, me, sibling, True) for a in range(n)]
        first += [copy(a, 1 + j, me, (*chip, c), True) for j, chip in enumerate(chips) for a in range(n)]
        for cp in mine + first:
            cp.start()
        passed = []
        for j, chip in enumerate(chips):
            for a in range(n):
                copy(a, 1 + j, (*chip, c), me).wait_recv()
                fwd = copy(a, 4 + j, (*chip, c), sibling)
                fwd.start()
                passed.append(fwd)
        for a in range(n):
            copy(a, 0, sibling, me).wait_recv()
        for j, chip in enumerate(chips):
            for a in range(n):
                copy(a, 4 + j, (*chip, 1 - c), me).wait_recv()
        for cp in first + passed:
            cp.wait_send()
        for cp in mine:
            cp.wait()

    return _comm_call(body, name, blocks, [jax.ShapeDtypeStruct((N_DEV,) + b.shape, b.dtype) for b in blocks], 7)


_REL = [(0, 0, 1), (0, 1, 0), (0, 1, 1), (1, 0, 0), (1, 0, 1), (1, 1, 0), (1, 1, 1)]


def _seq_exchange(arrays, gather, name, collective_id):
    n = len(arrays)
    hbm = pltpu.MemorySpace.HBM
    srcs = [jax.new_ref(a, memory_space=hbm) for a in arrays]
    lands = [jax.empty_ref(jax.ShapeDtypeStruct(((N_DEV,) + a.shape) if gather else a.shape, a.dtype), memory_space=hbm) for a in arrays]

    @pl.kernel(mesh=plsc.ScalarSubcoreMesh(axis_name="sequencer", num_cores=1), name=name,
               scratch_types=(pltpu.SemaphoreType.DMA((7, n)), pltpu.SemaphoreType.DMA((7, n)), pltpu.SemaphoreType.DMA((n,))),
               compiler_params=pltpu.CompilerParams(collective_id=collective_id))
    def launch(send, recv, local):
        x, y, c = _coords()
        me = (x, y, c)
        peers = [(x ^ dx, y ^ dy, c ^ dc) for dx, dy, dc in _REL]
        barrier = pltpu.get_barrier_semaphore()
        for peer in peers:
            pl.semaphore_signal(barrier, inc=1, device_id=peer, device_id_type=MESH)
        pl.semaphore_wait(barrier, len(peers))

        def copy(a, k, peer, arrival):
            return pltpu.make_async_remote_copy(
                src_ref=srcs[a] if gather else srcs[a].at[_slot(peer)], dst_ref=lands[a].at[_slot(peer if arrival else me)],
                send_sem=send.at[k, a], recv_sem=recv.at[k, a], device_id=peer, device_id_type=MESH)

        mine = [pltpu.make_async_copy(srcs[a] if gather else srcs[a].at[_slot(me)], lands[a].at[_slot(me)], local.at[a])
                for a in range(n)]
        out = [copy(a, k, peer, False) for a in range(n) for k, peer in enumerate(peers)]
        for cp in mine + out:
            cp.start()
        for a in range(n):
            for k, peer in enumerate(peers):
                copy(a, k, peer, True).wait_recv()
        for cp in out:
            cp.wait_send()
        for cp in mine:
            cp.wait()

    launch()
    return [land[...] for land in lands]


def _adam_math(w, g, m, v):
    m_ = ADAM_B1 * m + (1.0 - ADAM_B1) * g
    v_ = ADAM_B2 * v + (1.0 - ADAM_B2) * jnp.square(g)
    m_hat = m_ / (1.0 - ADAM_B1 ** ADAM_STEP)
    v_hat = v_ / (1.0 - ADAM_B2 ** ADAM_STEP)
    return -ADAM_LR * (m_hat / (jnp.sqrt(v_hat) + ADAM_EPS) + ADAM_WD * w), m_, v_


def _reduce_adamw(parts, w, m, v, name):
    R, L = w.shape
    tr = _pick(R, (256, 208, 176, 128, 64, 32, 16, 8))
    n = len(parts)

    def body(*refs):
        w_ref, m_ref, v_ref, g_ref, d_ref, nm_ref, nv_ref = refs[n:]
        pieces = []
        for p_ref in refs[:n]:
            g = p_ref[0].astype(f32)
            for i in range(1, N_DEV):
                g = g + p_ref[i].astype(f32)
            pieces.append(g)
        g = pieces[0] if n == 1 else jnp.concatenate(pieces, axis=-1)
        g_ref[...] = g
        d_ref[...], nm_ref[...], nv_ref[...] = _adam_math(w_ref[...], g, m_ref[...], v_ref[...])

    blk = pl.BlockSpec((tr, L), lambda i: (i, 0))
    sh = jax.ShapeDtypeStruct((R, L), f32)
    return _call(body, name, (R // tr,), [pl.BlockSpec((N_DEV, tr, q.shape[2]), lambda i: (0, i, 0)) for q in parts] + [blk, blk, blk],
                 (blk,) * 4, (sh,) * 4, sem=("parallel",))(*parts, w, m, v)


SMALL = (("w_spatial", (512, 128), 0), ("norm1_g", (1, 1024), 512), ("mem_norm_g", (1, 1024), 520), ("norm2_g", (1, 1024), 528),
         ("final_g", (1, 1024), 536), ("lb_logits", (2, 512), 544), ("ln_v_g", (1, 512), 552), ("ln_v_b", (1, 512), 556),
         ("b_spatial", (4, 128), 560), ("hgrn_norm_g", (1, 128), 564), ("conv_b", (1, 2816), 565))
LOSS_ROW, SMALL_USED, SMALL_ROWS = 587, 588, 640


def _segments(shape, base):
    r, n = shape
    per = n // 128
    return [(base + i * per + j, i, slice(j * 128, (j + 1) * 128)) for i in range(r) for j in range(per)]


def _pack_small(gs, loss_part):
    names = [n for n, _, _ in SMALL]

    def body(*refs):
        src, loss_ref, o_ref = dict(zip(names, refs[:-2])), refs[-2], refs[-1]
        o_ref[SMALL_USED:SMALL_ROWS, :] = jnp.zeros((SMALL_ROWS - SMALL_USED, 128), f32)
        o_ref[LOSS_ROW:LOSS_ROW + 1, :] = loss_ref[...]
        for name, shape, base in SMALL:
            ref = src[name]
            if name == "w_spatial":
                o_ref[base:base + 512, :] = ref[...].reshape(512, 128)
            elif name == "b_spatial":
                o_ref[base:base + 4, :] = ref[0:4, :]
            elif name == "conv_b":
                per_example = functools.reduce(lambda u, v_: u + v_, [ref[b] for b in range(ref.shape[0])])
                for row, i, sl in _segments(shape, base):
                    o_ref[row:row + 1, :] = per_example[i:i + 1, sl]
            elif name == "hgrn_norm_g":
                per_head = [ref[b, h] for b in range(ref.shape[0]) for h in range(N_HEAD)]
                o_ref[base:base + 1, :] = functools.reduce(lambda u, v_: u + v_, per_head)
            else:
                for row, i, sl in _segments(shape, base):
                    o_ref[row:row + 1, :] = ref[i:i + 1, sl]

    return pl.pallas_call(body, name="pack_small", out_shape=jax.ShapeDtypeStruct((SMALL_ROWS, 128), f32))(
        *[gs[n] for n in names], loss_part)


def _small_update(gathered, w, m, v):
    names = [n for n, _, _ in SMALL]
    k = len(names)

    def body(*refs):
        p_ref = refs[0]
        ins = [dict(zip(names, refs[1 + i * k:1 + (i + 1) * k])) for i in range(3)]
        outs = [dict(zip(names, refs[1 + (3 + i) * k:1 + (4 + i) * k])) for i in range(4)]
        loss_ref, gsum = refs[-2], refs[-1]
        g = p_ref[0]
        for i in range(1, N_DEV):
            g = g + p_ref[i]
        gsum[...] = g
        loss_ref[...] = gsum[LOSS_ROW:LOSS_ROW + 1, :]
        for name, shape, base in SMALL:
            if name == "w_spatial":
                where = [(slice(base, base + 512), (slice(None), slice(None)))]
            else:
                where = [(slice(row, row + 1), (slice(i, i + 1), sl)) for row, i, sl in _segments(shape, base)]
            for rows, at in where:
                g_ = gsum[rows, :]
                d_, m_, v_ = _adam_math(ins[0][name][at], g_, ins[1][name][at], ins[2][name][at])
                for o, val in zip(outs, (g_, d_, m_, v_)):
                    o[name][at] = val

    args = [gathered] + [d[n] for d in (w, m, v) for n in names]
    out_shapes = [jax.ShapeDtypeStruct(shape, f32) for _ in range(4) for _, shape, _ in SMALL] + [jax.ShapeDtypeStruct((1, 128), f32)]
    outs = pl.pallas_call(body, name="small_update", out_shape=out_shapes, scratch_shapes=[pltpu.VMEM((SMALL_ROWS, 128), f32)])(*args)
    return [dict(zip(names, outs[i * k:(i + 1) * k])) for i in range(4)], outs[-1]


def _cols_full(g):
    return jnp.moveaxis(g, 0, -2).reshape(g.shape[1:-1] + (N_DEV * g.shape[-1],))


def _cols_parts(full):
    n = full.shape[-1] // N_DEV
    return jnp.moveaxis(full.reshape(full.shape[:-1] + (N_DEV, n)), -2, 0)


def kernel(x, mem, norm1_g, w_in, ln_v_g, ln_v_b, w_spatial, b_spatial, lb_logits, hgrn_norm_g, mem_norm_g, w_mem_kv, w_branch, w_out, norm2_g, w_up, conv_w, conv_b, w_down, final_g, loss_target, m_norm1_g, m_w_in, m_ln_v_g, m_ln_v_b, m_w_spatial, m_b_spatial, m_lb_logits, m_hgrn_norm_g, m_mem_norm_g, m_w_mem_kv, m_w_branch, m_w_out, m_norm2_g, m_w_up, m_conv_w, m_conv_b, m_w_down, m_final_g, v_norm1_g, v_w_in, v_ln_v_g, v_ln_v_b, v_w_spatial, v_b_spatial, v_lb_logits, v_hgrn_norm_g, v_mem_norm_g, v_w_mem_kv, v_w_branch, v_w_out, v_norm2_g, v_w_up, v_conv_w, v_conv_b, v_w_down, v_final_g):
    given = dict(locals())
    order = ("norm1_g", "w_in", "ln_v_g", "ln_v_b", "w_spatial", "b_spatial", "lb_logits", "hgrn_norm_g", "mem_norm_g",
             "w_mem_kv", "w_branch", "w_out", "norm2_g", "w_up", "conv_w", "conv_b", "w_down", "final_g")
    groups = dict(a=("w_in",), b=("w_mem_kv", "w_branch", "w_out"), c=("w_up", "conv_w", "w_down"))

    by_rows = ("w_in", "w_up")
    shard_of = lambda n, prefix="": jnp.swapaxes(given[prefix + n][0], 0, 1) if n in by_rows else given[prefix + n][0]
    wire = {n: shard_of(n).astype(f32 if n == "conv_w" else bf16) for ns in groups.values() for n in ns}
    w_in_full = _all_gather([wire["w_in"]], "gather_w_in")[0].reshape(IN_WIDTH, D_MODEL)
    w_in_full, wire_b, wire_c = lax.optimization_barrier((w_in_full, [wire[n] for n in groups["b"]], [wire[n] for n in groups["c"]]))
    rest_b = _seq_exchange(wire_b, True, "gather_b", 1)
    rest_c = _seq_exchange(wire_c, True, "gather_c", 6)

    def late_b(after):
        _, (kv_, br_, out_) = lax.optimization_barrier((after, tuple(rest_b)))
        br_ = _cols_full(br_)
        return dict(w_mem_kv=kv_.reshape(D_MODEL, 2 * 512), w_branch=[br_[n] for n in range(3)], w_out=out_.reshape(D_MODEL, D_MODEL))

    def late_c(after):
        _, (up_, cw_, down_) = lax.optimization_barrier((after, tuple(rest_c)))
        return dict(w_up_t=up_.reshape(2 * D_FF, D_MODEL), conv_w=_cols_full(cw_), w_down=down_.reshape(D_FF, D_MODEL))

    to_parts = dict(w_in_half=lambda g_: g_.reshape(N_DEV, -1, D_MODEL // 2), w_up=lambda g_: g_.reshape(N_DEV, -1, D_MODEL), conv_w=_cols_parts,
                    w_branch=lambda g_: _cols_parts(g_.astype(bf16)).reshape(N_DEV, -1, 128),
                    w_mem_kv=lambda g_: g_.reshape(N_DEV, -1, 2 * 512), w_out=lambda g_: g_.astype(bf16).reshape(N_DEV, -1, D_MODEL),
                    w_down=lambda g_: g_.reshape(N_DEV, -1, D_MODEL))
    scatters = {}


    def send(tag, grads_):
        parts = [to_parts[n](g_) for n, g_ in grads_.items()]
        scatters[tag] = _seq_exchange(parts, False, f"scatter_{tag}", dict(a0=2, a1=7, b=4, c=5)[tag])

    small_2d = lambda prefix: {n: given[prefix + n].reshape(shape) for n, shape, _ in SMALL}
    p = small_2d("")
    p["w_spatial"] = w_spatial[0]
    updates = {}

    def update(tag):
        arrived = [scatters["a0"] + scatters["a1"]] if tag == "a" else [[parts] for parts in scatters[tag]]
        for n, parts in zip(groups[tag], arrived):
            state = [shard_of(n, pre) for pre in ("", "m_", "v_")]
            res = _reduce_adamw(parts, *[a.reshape(-1, a.shape[-1]) for a in state], "adamw_" + n)
            updates[n] = [jnp.swapaxes(r, 0, 1) for r in res] if n in by_rows else res

    loss_part, grad_x, gs = _local_step(x, mem, loss_target, p, w_in_full, late_b, late_c, send)

    packed, (scatters["c"], scatters["b"]) = lax.optimization_barrier((_pack_small(gs, loss_part), (scatters["c"], scatters["b"])))
    gathered = _seq_exchange([packed], True, "gather_small", 3)[0]

    update("c")
    update("b")
    update("a")
    grads, delta, new_m, new_v = {}, {}, {}, {}
    for n, res in updates.items():
        grads[n], delta[n], new_m[n], new_v[n] = [r.reshape(given[n].shape) for r in res]

    small_results, loss_row = _small_update(gathered, small_2d(""), small_2d("m_"), small_2d("v_"))
    for dst, res in zip((grads, delta, new_m, new_v), small_results):
        for n, _, _ in SMALL:
            dst[n] = res[n].reshape(given[n].shape)
    loss = loss_row[0, 0]

    return (loss, grad_x, *[grads[n] for n in order], *[delta[n] for n in order], *[new_m[n] for n in order],
            *[new_v[n] for n in order])
```

```python
import functools

import jax
import jax.numpy as jnp
from jax import lax
from jax.experimental import pallas as pl
from jax.experimental.pallas import tpu as pltpu
from jax.experimental.pallas import tpu_sc as plsc

f32 = jnp.float32
bf16 = jnp.bfloat16

N_DEV = 8
D_MODEL = 1024
EPS = 1e-6
GM_CHUNK = 128
HG_CHUNK = 64
HEAD = 128
N_HEAD = 4
MEM_LEN = 256
D_FF = 2816
IN_WIDTH = 6656
C_ZU, C_HQ, C_HF, C_HI, C_HG, C_XQ, C_GL = 0, 1024, 1536, 2048, 2560, 3072, 3584
ADAM_LR, ADAM_B1, ADAM_B2, ADAM_EPS, ADAM_WD, ADAM_STEP = 0.001, 0.9, 0.999, 1e-08, 0.01, 10
VMEM_LIMIT = 56 * 1024 * 1024
MESH = pl.DeviceIdType.MESH


def _pick(n, cands):
    for c in cands:
        if n % c == 0:
            return c
    return n


def _call(body, name, grid, in_specs, out_specs, out_shape, scratch=(), sem=None, **cp):
    params = dict(vmem_limit_bytes=VMEM_LIMIT, **cp)
    if sem is not None:
        params["dimension_semantics"] = sem
    return pl.pallas_call(
        body, name=name, grid=grid, in_specs=in_specs, out_specs=out_specs, out_shape=out_shape,
        scratch_shapes=list(scratch), compiler_params=pltpu.CompilerParams(**params))


_DN = {"nn": (((1,), (0,)), ((), ())), "nt": (((1,), (1,)), ((), ())), "tn": (((0,), (0,)), ((), ()))}


def _raw_dot(a, b, mode):
    return lax.dot_general(a.astype(bf16), b.astype(bf16), _DN[mode], preferred_element_type=f32)


@jax.custom_vjp
def _dot_nn(a, b):
    return _raw_dot(a, b, "nn")


_dot_nn.defvjp(lambda a, b: (_raw_dot(a, b, "nn"), (a, b)),
               lambda r, g: (_raw_dot(g, r[1], "nt"), _raw_dot(r[0], g, "tn")))


@jax.custom_vjp
def _dot_nt(a, b):
    return _raw_dot(a, b, "nt")


_dot_nt.defvjp(lambda a, b: (_raw_dot(a, b, "nt"), (a, b)),
               lambda r, g: (_raw_dot(g, r[1], "nn"), _raw_dot(g, r[0], "tn")))


@jax.custom_vjp
def _dot_tn(a, b):
    return _raw_dot(a, b, "tn")


_dot_tn.defvjp(lambda a, b: (_raw_dot(a, b, "tn"), (a, b)),
               lambda r, g: (_raw_dot(r[1], g, "nt"), _raw_dot(r[0], g, "nn")))


def _tri(n, lower):
    r = lax.broadcasted_iota(jnp.int32, (n, n), 0)
    c = lax.broadcasted_iota(jnp.int32, (n, n), 1)
    return ((c <= r) if lower else (c >= r)).astype(f32)


def _sel_dot(sel, x, mode, x_first=False, pieces=3):
    sel = sel.astype(bf16)
    out, rest = None, x
    for p in range(pieces):
        piece = rest.astype(bf16)
        part = lax.dot_general(*((piece, sel) if x_first else (sel, piece)), _DN[mode], preferred_element_type=f32)
        out = part if out is None else out + part
        if p + 1 < pieces:
            rest = rest - piece.astype(f32)
    return out


def _egrad(fn, x, ct):
    return jax.vjp(fn, x)[1](ct)[0]


def _mm(a, b, mode, out_dtype, name, tm, tn, tk=None, residual=None, into=None, n_tiles=None):
    if mode == "nn":
        (M, K), (_, N) = a.shape, b.shape
    elif mode == "nt":
        (M, K), (N, _) = a.shape, b.shape
    else:
        (K, M), (_, N) = a.shape, b.shape
    j0 = 0
    if n_tiles is not None:
        assert mode != "nt" and N % tn == 0 and residual is None
        j0, N = n_tiles[0], n_tiles[1] * tn
    tm, tn = min(tm, M), min(tn, N)
    tk = K if tk is None else min(tk, K)
    assert M % tm == 0 and N % tn == 0 and K % tk == 0, (name, M, N, K, tm, tn, tk)
    nk = K // tk

    def body(*refs):
        acc_ref = refs[-1] if nk > 1 else None
        refs = refs[:-1] if nk > 1 else refs
        if residual is None:
            a_ref, b_ref, *_, o_ref = refs
        else:
            a_ref, b_ref, r_ref, o_ref = refs

        def finish(r):
            if residual is not None:
                r = r + r_ref[...]
            o_ref[...] = r.astype(out_dtype)

        part = _raw_dot(a_ref[...], b_ref[...], mode)
        if nk == 1:
            finish(part)
            return
        k = pl.program_id(2)

        @pl.when(k == 0)
        def _():
            acc_ref[...] = part

        @pl.when((k > 0) & (k < nk - 1))
        def _():
            acc_ref[...] += part

        @pl.when(k == nk - 1)
        def _():
            finish(acc_ref[...] + part)

    a_spec = {"nn": pl.BlockSpec((tm, tk), lambda i, j, k: (i, k)),
              "nt": pl.BlockSpec((tm, tk), lambda i, j, k: (i, k)),
              "tn": pl.BlockSpec((tk, tm), lambda i, j, k: (k, i))}[mode]
    b_spec = {"nn": pl.BlockSpec((tk, tn), lambda i, j, k: (k, j + j0)),
              "nt": pl.BlockSpec((tn, tk), lambda i, j, k: (j, k)),
              "tn": pl.BlockSpec((tk, tn), lambda i, j, k: (k, j + j0))}[mode]
    o_spec = pl.BlockSpec((tm, tn), lambda i, j, k: (i, j))
    in_specs = [a_spec, b_spec] + ([o_spec] if residual is not None else [])
    args = (a, b) + ((residual,) if residual is not None else ())
    out_shape = jax.ShapeDtypeStruct((M, N), out_dtype)
    extra = {}
    if into is not None:
        assert residual is None and into[1] % tm == 0
        o_spec = pl.BlockSpec((tm, tn), lambda i, j, k: (i + into[1] // tm, j))
        out_shape = jax.ShapeDtypeStruct(into[0].shape, out_dtype)
        in_specs, args = in_specs + [pl.BlockSpec(memory_space=pl.ANY)], args + (into[0],)
        extra = dict(input_output_aliases={2: 0})
    return pl.pallas_call(
        body, name=name, grid=(M // tm, N // tn, nk), in_specs=in_specs, out_specs=o_spec, out_shape=out_shape,
        scratch_shapes=[pltpu.VMEM((tm, tn), f32)] if nk > 1 else [],
        compiler_params=pltpu.CompilerParams(vmem_limit_bytes=VMEM_LIMIT, dimension_semantics=("parallel", "parallel", "arbitrary")),
        **extra)(*args)


def _rms_fwd(x, g, name):
    R, Dd = x.shape
    tr = _pick(R, (512, 256, 128))

    def body(x_ref, g_ref, o_ref):
        xf = x_ref[...]
        o_ref[...] = (xf * lax.rsqrt(jnp.mean(xf * xf, axis=-1, keepdims=True) + EPS) * g_ref[...]).astype(bf16)

    row = pl.BlockSpec((tr, Dd), lambda i: (i, 0))
    return _call(body, name, (R // tr,), [row, pl.BlockSpec((1, Dd), lambda i: (0, 0))], row, jax.ShapeDtypeStruct((R, Dd), bf16),
                 sem=("parallel",))(x, g)


def _norm_proj_fwd(x, g, w_t, tm, tn):
    R, Dd = x.shape
    N = w_t.shape[0]
    tm = min(tm, R)
    assert R % tm == 0 and N % tn == 0

    def body(x_ref, g_ref, w_ref, p_ref, h_ref):
        @pl.when(pl.program_id(1) == 0)
        def _():
            xf = x_ref[...]
            h_ref[...] = (xf * lax.rsqrt(jnp.mean(xf * xf, axis=-1, keepdims=True) + EPS) * g_ref[...]).astype(bf16)

        p_ref[...] = _raw_dot(h_ref[...], w_ref[...], "nt").astype(bf16)

    row = pl.BlockSpec((tm, Dd), lambda i, j: (i, 0))
    return _call(body, "norm1_proj_fwd", (R // tm, N // tn),
                 [row, pl.BlockSpec((1, Dd), lambda i, j: (0, 0)), pl.BlockSpec((tn, Dd), lambda i, j: (j, 0))],
                 (pl.BlockSpec((tm, tn), lambda i, j: (i, j)), row),
                 (jax.ShapeDtypeStruct((R, N), bf16), jax.ShapeDtypeStruct((R, Dd), bf16)), sem=("parallel", "arbitrary"))(x, g, w_t)


def _rms_bwd(x, g, dh, name, residual=None):
    R, Dd = x.shape
    tr = _pick(R, (512, 256, 128))

    def body(*refs):
        if residual is None:
            x_ref, g_ref, dh_ref, dx_ref, dg_ref = refs
        else:
            x_ref, g_ref, dh_ref, r_ref, dx_ref, dg_ref = refs
        xf = x_ref[...]
        rs = lax.rsqrt(jnp.mean(xf * xf, axis=-1, keepdims=True) + EPS)
        y = xf * rs
        dh_ = dh_ref[...].astype(f32)
        dy = dh_ * g_ref[...]
        dx = rs * (dy - y * jnp.mean(dy * y, axis=-1, keepdims=True))
        if residual is not None:
            dx = dx + r_ref[...]
        dx_ref[...] = dx

        @pl.when(pl.program_id(0) == 0)
        def _():
            dg_ref[...] = jnp.zeros_like(dg_ref)

        dg_ref[...] += jnp.sum(dh_ * y, axis=0, keepdims=True)

    row = pl.BlockSpec((tr, Dd), lambda i: (i, 0))
    vec = pl.BlockSpec((1, Dd), lambda i: (0, 0))
    in_specs = [row, vec, row] + ([row] if residual is not None else [])
    args = (x, g, dh) + ((residual,) if residual is not None else ())
    return _call(body, name, (R // tr,), in_specs, (row, vec),
                 (jax.ShapeDtypeStruct((R, Dd), f32), jax.ShapeDtypeStruct((1, Dd), f32)), sem=("arbitrary",))(*args)


def _mm_rms_bwd(pairs, x, g, residual, name, tm):
    M = x.shape[0]
    Dd = x.shape[1]
    tm = min(tm, M)
    n = len(pairs)

    def body(*refs):
        ab_refs, (x_ref, g_ref, r_ref, dx_ref, dg_ref) = refs[:2 * n], refs[2 * n:]
        dh_ = _raw_dot(ab_refs[0][...], ab_refs[1][...], "nn")
        for k in range(1, n):
            dh_ = dh_ + _raw_dot(ab_refs[2 * k][...], ab_refs[2 * k + 1][...], "nn")
        xf = x_ref[...]
        rs = lax.rsqrt(jnp.mean(xf * xf, axis=-1, keepdims=True) + EPS)
        y = xf * rs
        dy = dh_ * g_ref[...]
        dx_ref[...] = rs * (dy - y * jnp.mean(dy * y, axis=-1, keepdims=True)) + r_ref[...]

        @pl.when(pl.program_id(0) == 0)
        def _():
            dg_ref[...] = jnp.zeros_like(dg_ref)

        dg_ref[...] += jnp.sum(dh_ * y, axis=0, keepdims=True)

    row = pl.BlockSpec((tm, Dd), lambda i: (i, 0))
    vec = pl.BlockSpec((1, Dd), lambda i: (0, 0))
    in_specs, args = [], []
    for a, b, k in pairs:
        in_specs += [pl.BlockSpec((tm, a.shape[1]), lambda i: (i, 0)),
                     pl.BlockSpec((a.shape[1], b.shape[1]), functools.partial(lambda i, k_: (k_, 0), k_=k))]
        args += [a, b]
    in_specs += [row, vec, row]
    args += [x, g, residual]
    return _call(body, name, (M // tm,), in_specs, (row, vec),
                 (jax.ShapeDtypeStruct((M, Dd), f32), jax.ShapeDtypeStruct((1, Dd), f32)), sem=("arbitrary",))(*args)


def _down_final_loss(act, w_down, x1, g, target):
    R, Dd = x1.shape
    tr = _pick(R, (512, 256, 128))

    def body(a_ref, w_ref, x1_ref, g_ref, t_ref, loss_ref, dx_ref, dxb_ref, dg_ref):
        xf = _raw_dot(a_ref[...], w_ref[...], "nn") + x1_ref[...]
        rs = lax.rsqrt(jnp.mean(xf * xf, axis=-1, keepdims=True) + EPS)
        y = xf * rs
        err = y * g_ref[...] - t_ref[...]
        dh_ = err * (1.0 / Dd)
        dy = dh_ * g_ref[...]
        dx = rs * (dy - y * jnp.mean(dy * y, axis=-1, keepdims=True))
        dx_ref[...] = dx
        dxb_ref[...] = dx.astype(bf16)

        @pl.when(pl.program_id(0) == 0)
        def _():
            dg_ref[...] = jnp.zeros_like(dg_ref)
            loss_ref[...] = jnp.zeros_like(loss_ref)

        dg_ref[...] += jnp.sum(dh_ * y, axis=0, keepdims=True)
        part = jnp.sum(jnp.mean(err * err, axis=-1, keepdims=True), axis=0, keepdims=True)
        loss_ref[...] += 0.5 * part

    row = pl.BlockSpec((tr, Dd), lambda i: (i, 0))
    vec = pl.BlockSpec((1, Dd), lambda i: (0, 0))
    in_specs = [pl.BlockSpec((tr, act.shape[1]), lambda i: (i, 0)), pl.BlockSpec(w_down.shape, lambda i: (0, 0)), row, vec, row]
    return _call(body, "down_final_loss", (R // tr,), in_specs, (pl.BlockSpec((1, 128), lambda i: (0, 0)), row, row, vec),
                 (jax.ShapeDtypeStruct((1, 128), f32), jax.ShapeDtypeStruct((R, Dd), f32), jax.ShapeDtypeStruct((R, Dd), bf16),
                  jax.ShapeDtypeStruct((1, Dd), f32)), sem=("arbitrary",))(act, w_down, x1, g, target)


def _gmlp_parts(zuv, ln_g, ln_b):
    zu, zv = zuv[:, :512], zuv[:, 512:]
    u = jax.nn.gelu(zu)
    v = jax.nn.gelu(zv)
    mu = jnp.mean(v, axis=-1, keepdims=True)
    rs = lax.rsqrt(jnp.mean(jnp.square(v - mu), axis=-1, keepdims=True) + EPS)
    xh = (v - mu) * rs
    return zu, zv, u, xh, rs, xh * ln_g + ln_b


GM_TILE_CHUNKS = 4


def _gmlp_tile(T):
    n = _pick(T // GM_CHUNK, (GM_TILE_CHUNKS, 2, 1))
    return n, n * GM_CHUNK


def _gmlp_fwd(proj, ln_g, ln_b, w_s, b_st):
    T = proj.shape[0]
    nch, rows = _gmlp_tile(T)

    def body(p_ref, g_ref, b_ref, w_ref, bs_ref, o_ref):
        _, _, u, _, _, vn = _gmlp_parts(p_ref[...].astype(f32), g_ref[...], b_ref[...])
        causal = _tri(GM_CHUNK, True) > 0
        for gi in range(N_HEAD):
            sl = slice(gi * HEAD, (gi + 1) * HEAD)
            w = jnp.where(causal, w_ref[gi], 0.0)
            for ch in range(nch):
                rs_ = slice(ch * GM_CHUNK, (ch + 1) * GM_CHUNK)
                mixed = _raw_dot(w, vn[rs_, sl], "nn") + bs_ref[:, gi:gi + 1]
                o_ref[rs_, sl] = (u[rs_, sl] * mixed).astype(bf16)

    vec = pl.BlockSpec((1, 512), lambda i: (0, 0))
    return _call(body, "gmlp_fwd", (T // rows,),
                 [pl.BlockSpec((rows, 1024), lambda i: (i, 0)), vec, vec,
                  pl.BlockSpec((N_HEAD, GM_CHUNK, GM_CHUNK), lambda i: (0, 0, 0)), pl.BlockSpec((GM_CHUNK, 128), lambda i: (0, 0))],
                 pl.BlockSpec((rows, 512), lambda i: (i, 0)), jax.ShapeDtypeStruct((T, 512), bf16), sem=("parallel",))(
        proj, ln_g, ln_b, w_s, b_st)


def _gmlp_bwd(proj, ln_g, ln_b, w_s, b_st, da):
    T = proj.shape[0]
    nch, rows = _gmlp_tile(T)

    def body(p_ref, g_ref, b_ref, w_ref, bs_ref, da_ref, dp_ref, dg_ref, db_ref, dw_ref, dbs_ref):
        zu, zv, u, xh, rs, vn = _gmlp_parts(p_ref[...].astype(f32), g_ref[...], b_ref[...])
        causal = _tri(GM_CHUNK, True) > 0
        sub = lax.broadcasted_iota(jnp.int32, (8, GM_CHUNK), 0)
        ones = jnp.ones((8, HEAD), f32)
        dout = da_ref[...].astype(f32)

        @pl.when(pl.program_id(0) == 0)
        def _():
            for r in (dg_ref, db_ref, dw_ref, dbs_ref):
                r[...] = jnp.zeros_like(r)

        du, dvn, dbs = [], [], jnp.zeros((8, GM_CHUNK), f32)
        for gi in range(N_HEAD):
            sl = slice(gi * HEAD, (gi + 1) * HEAD)
            w = jnp.where(causal, w_ref[gi], 0.0)
            du_g, dvn_g, dw_g = [], [], jnp.zeros((GM_CHUNK, GM_CHUNK), f32)
            for ch in range(nch):
                rs_ = slice(ch * GM_CHUNK, (ch + 1) * GM_CHUNK)
                mixed = _raw_dot(w, vn[rs_, sl], "nn") + bs_ref[:, gi:gi + 1]
                du_g.append(dout[rs_, sl] * mixed)
                dm = dout[rs_, sl] * u[rs_, sl]
                dbs = dbs + jnp.where(sub == gi, _sel_dot(ones, dm, "nt"), 0.0)
                dw_g = dw_g + _raw_dot(dm, vn[rs_, sl], "nt")
                dvn_g.append(_raw_dot(w, dm, "tn"))
            dw_ref[gi] += jnp.where(causal, dw_g, 0.0)
            du.append(jnp.concatenate(du_g, axis=0))
            dvn.append(jnp.concatenate(dvn_g, axis=0))
        dbs_ref[...] += dbs
        du = jnp.concatenate(du, axis=-1)
        dvn = jnp.concatenate(dvn, axis=-1)
        dg_ref[...] += jnp.sum(dvn * xh, axis=0, keepdims=True)
        db_ref[...] += jnp.sum(dvn, axis=0, keepdims=True)
        dxh = dvn * g_ref[...]
        dv = rs * (dxh - jnp.mean(dxh, axis=-1, keepdims=True) - xh * jnp.mean(dxh * xh, axis=-1, keepdims=True))
        dp_ref[:, :512] = _egrad(jax.nn.gelu, zu, du).astype(bf16)
        dp_ref[:, 512:] = _egrad(jax.nn.gelu, zv, dv).astype(bf16)

    vec = pl.BlockSpec((1, 512), lambda i: (0, 0))
    wsp = pl.BlockSpec((N_HEAD, GM_CHUNK, GM_CHUNK), lambda i: (0, 0, 0))
    return _call(body, "gmlp_bwd", (T // rows,),
                 [pl.BlockSpec((rows, 1024), lambda i: (i, 0)), vec, vec, wsp, pl.BlockSpec((GM_CHUNK, 128), lambda i: (0, 0)),
                  pl.BlockSpec((rows, 512), lambda i: (i, 0))],
                 (pl.BlockSpec((rows, 1024), lambda i: (i, 0)), vec, vec, wsp, pl.BlockSpec((8, GM_CHUNK), lambda i: (0, 0))),
                 (jax.ShapeDtypeStruct((T, 1024), bf16), jax.ShapeDtypeStruct((1, 512), f32), jax.ShapeDtypeStruct((1, 512), f32),
                  jax.ShapeDtypeStruct((N_HEAD, GM_CHUNK, GM_CHUNK), f32), jax.ShapeDtypeStruct((8, GM_CHUNK), f32)),
                 sem=("arbitrary",))(proj, ln_g, ln_b, w_s, b_st, da)


HG_SUB = 8
HG_NSUB = HG_CHUNK // HG_SUB


def _two_level_matrix(transposed=False):
    shape = (HG_CHUNK, 2 * HG_CHUNK) if transposed else (2 * HG_CHUNK, HG_CHUNK)
    r = lax.broadcasted_iota(jnp.int32, shape, 1 if transposed else 0)
    c = lax.broadcasted_iota(jnp.int32, shape, 0 if transposed else 1)
    t = jnp.where(r < HG_CHUNK, r, r - HG_CHUNK)
    local = (r < HG_CHUNK) & (t // HG_SUB == c // HG_SUB) & (c <= t)
    before = (r >= HG_CHUNK) & (c < (t // HG_SUB) * HG_SUB)
    return (local | before).astype(f32)


def _two_level_sums(x):
    two = _sel_dot(_two_level_matrix(), x, "nn")
    return two[:HG_CHUNK], two[HG_CHUNK:]


@jax.custom_vjp
def _two_level_cumsum(x):
    return _two_level_sums(x)


_two_level_cumsum.defvjp(
    lambda x: (_two_level_sums(x), None),
    lambda _, g: (_sel_dot(_two_level_matrix(), jnp.concatenate(g, axis=0), "tn"),))


def _tile_matrix():
    s = lax.broadcasted_iota(jnp.int32, (HG_SUB, HG_CHUNK), 0)
    j = lax.broadcasted_iota(jnp.int32, (HG_SUB, HG_CHUNK), 1)
    return (j % HG_SUB == s).astype(f32)


@jax.custom_vjp
def _tile_lanes(x):
    return _sel_dot(_tile_matrix(), x, "nn", x_first=True, pieces=1)


_tile_lanes.defvjp(
    lambda x: (_sel_dot(_tile_matrix(), x, "nn", x_first=True, pieces=1), None),
    lambda _, g: (_sel_dot(_tile_matrix(), g, "nt", x_first=True, pieces=2),))


def _block_rows(x):
    k = x.shape[-1]
    return jnp.broadcast_to(x.reshape(HG_NSUB, 1, HG_SUB, k), (HG_NSUB, HG_SUB, HG_SUB, k)).reshape(HG_CHUNK, HG_SUB, k)


def _hgrn_chunk(st0, q_raw, f_raw, i_raw, g_raw, l0, l1, ng):
    C, SUB = HG_CHUNK, HG_SUB
    lb = jax.nn.sigmoid(l0 - l1)
    fg = lb + (1.0 - lb) * jax.nn.sigmoid(f_raw)
    kk = 1.0 - fg
    qf = jax.nn.silu(q_raw)
    al, base = _two_level_cumsum(jnp.log(fg))
    a = al + base
    row = lax.broadcasted_iota(jnp.int32, (C, HEAD), 0)
    a_last = jnp.sum(jnp.where(row == C - 1, a, 0.0), axis=0, keepdims=True)
    inter = _dot_nt(qf * jnp.exp(a), st0)
    qt = qf * jnp.exp(al)
    rb = lax.broadcasted_iota(jnp.int32, (C, C), 0) // SUB
    cb = lax.broadcasted_iota(jnp.int32, (C, C), 1) // SUB
    scores = jnp.zeros((C, C), f32)
    for i in range(1, HG_NSUB):
        base_i = jnp.sum(jnp.where(row == i * SUB, base, 0.0), axis=0, keepdims=True)
        kt = kk * jnp.exp(jnp.minimum(base_i - a, 0.0))
        scores = scores + jnp.where((rb == i) & (cb < i), _dot_nt(qt, kt), 0.0)
    t_i = lax.broadcasted_iota(jnp.int32, (C, SUB, HEAD), 0) % SUB
    s_i = lax.broadcasted_iota(jnp.int32, (C, SUB, HEAD), 1)
    decay = jnp.exp(jnp.where(s_i <= t_i, al[:, None, :] - _block_rows(al), -jnp.inf))
    diag = jnp.sum(qf[:, None, :] * decay * _block_rows(kk), axis=-1)
    scores = scores + jnp.where(rb == cb, _tile_lanes(diag), 0.0)
    o = inter + _dot_nn(scores, i_raw)
    st1 = jnp.exp(a_last) * st0 + _dot_tn(i_raw, kk * jnp.exp(a_last - a))
    on = o * lax.rsqrt(jnp.mean(o * o, axis=-1, keepdims=True) + EPS) * ng
    return st1, on * jax.nn.silu(g_raw)


def _hgrn_specs(S, Bl, rev):
    N = S // HG_CHUNK
    chunk = (lambda n: N - 1 - n) if rev else (lambda n: n)
    col = lambda c0: pl.BlockSpec((Bl, HG_CHUNK, 512), lambda n: (0, chunk(n), c0 // 512))
    st = pl.BlockSpec((Bl, N_HEAD, 1, HEAD, HEAD), lambda n: (0, 0, chunk(n), 0, 0))
    full = lambda *s: pl.BlockSpec(s, functools.partial(lambda n, nd: (0,) * nd, nd=len(s)))
    return N, col, st, full


def _hgrn_fwd(proj, lb_logits, ng, Bl, S):
    N, col, st, full = _hgrn_specs(S, Bl, False)

    def body(q_ref, f_ref, i_ref, g_ref, l_ref, ng_ref, o_ref, st_ref, state):
        @pl.when(pl.program_id(0) == 0)
        def _():
            state[...] = jnp.zeros_like(state)

        for b in range(Bl):
            for h in range(N_HEAD):
                sl = slice(h * HEAD, (h + 1) * HEAD)
                st0 = state[b, h]
                st_ref[b, h, 0] = st0
                st1, out = _hgrn_chunk(st0, *[r[b, :, sl].astype(f32) for r in (q_ref, f_ref, i_ref, g_ref)],
                                       l_ref[0:1, sl], l_ref[1:2, sl], ng_ref[...])
                state[b, h] = st1
                o_ref[b, :, sl] = out.astype(bf16)

    return _call(body, "hgrn_fwd", (N,), [col(C_HQ), col(C_HF), col(C_HI), col(C_HG), full(2, 512), full(1, HEAD)],
                 (col(0), st),
                 (jax.ShapeDtypeStruct((Bl, S, 512), bf16), jax.ShapeDtypeStruct((Bl, N_HEAD, N, HEAD, HEAD), f32)),
                 scratch=[pltpu.VMEM((Bl, N_HEAD, HEAD, HEAD), f32)], sem=("arbitrary",))(
        proj, proj, proj, proj, lb_logits, ng)


def _hgrn_bwd(proj, lb_logits, ng, states, db, dzuv, dxq, dgl, Bl, S):
    N, col, st, full = _hgrn_specs(S, Bl, True)
    rows = lambda width: pl.BlockSpec((Bl, HG_CHUNK, width), lambda n: (0, N - 1 - n, 0))

    def body(q_ref, f_ref, i_ref, g_ref, l_ref, ng_ref, st_ref, db_ref, dzuv_ref, dxq_ref, dgl_ref,
             dp_ref, dl_ref, dng_ref, dstate):
        @pl.when(pl.program_id(0) == 0)
        def _():
            dstate[...] = jnp.zeros_like(dstate)
            dl_ref[...] = jnp.zeros_like(dl_ref)
            dng_ref[...] = jnp.zeros_like(dng_ref)

        dp_ref[:, :, C_ZU:C_HQ] = dzuv_ref[...]
        dp_ref[:, :, C_XQ:C_GL] = dxq_ref[...]
        for n in range(3):
            dp_ref[:, :, C_GL + n * D_MODEL:C_GL + (n + 1) * D_MODEL] = dgl_ref[n]
        dq_ref, df_ref, di_ref, dg_ref = [dp_ref.at[:, :, c0:c0 + 512] for c0 in (C_HQ, C_HF, C_HI, C_HG)]
        for b in range(Bl):
            for h in range(N_HEAD):
                sl = slice(h * HEAD, (h + 1) * HEAD)
                _, vjp = jax.vjp(_hgrn_chunk, st_ref[b, h, 0], *[r[b, :, sl].astype(f32) for r in (q_ref, f_ref, i_ref, g_ref)],
                                 l_ref[0:1, sl], l_ref[1:2, sl], ng_ref[...])
                dst0, dq, df, di, dg, dl0, dl1, dng = vjp((dstate[b, h], db_ref[b, :, sl].astype(f32)))
                dstate[b, h] = dst0
                dq_ref[b, :, sl] = dq.astype(bf16)
                df_ref[b, :, sl] = df.astype(bf16)
                di_ref[b, :, sl] = di.astype(bf16)
                dg_ref[b, :, sl] = dg.astype(bf16)
                dl_ref[0:1, sl] += dl0
                dl_ref[1:2, sl] += dl1
                dng_ref[b, h] += dng

    return _call(body, "hgrn_bwd", (N,),
                 [col(C_HQ), col(C_HF), col(C_HI), col(C_HG), full(2, 512), full(1, HEAD), st, col(0), rows(C_HQ - C_ZU),
                  rows(C_GL - C_XQ), pl.BlockSpec((3, Bl, HG_CHUNK, D_MODEL), lambda n: (0, 0, N - 1 - n, 0))],
                 (rows(IN_WIDTH), full(2, 512), full(Bl, N_HEAD, 1, HEAD)),
                 (jax.ShapeDtypeStruct((Bl, S, IN_WIDTH), bf16), jax.ShapeDtypeStruct((2, 512), f32),
                  jax.ShapeDtypeStruct((Bl, N_HEAD, 1, HEAD), f32)),
                 scratch=[pltpu.VMEM((Bl, N_HEAD, HEAD, HEAD), f32)], sem=("arbitrary",))(
        proj, proj, proj, proj, lb_logits, ng, states, db, dzuv, dxq, dgl)


def _attn_probs(q, k):
    s = _raw_dot(q, k, "nt") * (HEAD ** -0.5)
    e = jnp.exp(s - jnp.max(s, axis=-1, keepdims=True))
    return e / jnp.sum(e, axis=-1, keepdims=True)


def _attn_specs(S, tq):
    nq = S // tq
    q = pl.BlockSpec((tq, 512), lambda b, i: (b * nq + i, C_XQ // 512))
    kv = pl.BlockSpec((1, MEM_LEN, 1024), lambda b, i: (b, 0, 0))
    o = pl.BlockSpec((tq, 512), lambda b, i: (b * nq + i, 0))
    return nq, q, kv, o


def _attn_fwd(proj, kv, Bl, S):
    tq = _pick(S, (512, 256, 128))
    nq, qs, kvs, os_ = _attn_specs(S, tq)

    def body(q_ref, kv_ref, o_ref):
        for h in range(N_HEAD):
            sl = slice(h * HEAD, (h + 1) * HEAD)
            p = _attn_probs(q_ref[:, sl], kv_ref[0, :, sl])
            o_ref[:, sl] = _raw_dot(p, kv_ref[0, :, 512 + h * HEAD:512 + (h + 1) * HEAD], "nn").astype(bf16)

    return _call(body, "attn_fwd", (Bl, nq), [qs, kvs], os_, jax.ShapeDtypeStruct((Bl * S, 512), bf16),
                 sem=("parallel", "parallel"))(proj, kv)


def _attn_bwd(proj, kv, dc, Bl, S):
    tq = _pick(S, (512, 256, 128))
    nq, qs, kvs, os_ = _attn_specs(S, tq)

    def body(q_ref, kv_ref, do_ref, dq_ref, dkv_ref):
        @pl.when(pl.program_id(1) == 0)
        def _():
            dkv_ref[...] = jnp.zeros_like(dkv_ref)

        for h in range(N_HEAD):
            sl = slice(h * HEAD, (h + 1) * HEAD)
            vsl = slice(512 + h * HEAD, 512 + (h + 1) * HEAD)
            q, k, v, do = q_ref[:, sl], kv_ref[0, :, sl], kv_ref[0, :, vsl], do_ref[:, sl]
            p = _attn_probs(q, k)
            dkv_ref[0, :, vsl] += _raw_dot(p, do, "tn")
            dp = _raw_dot(do, v, "nt")
            ds = p * (dp - jnp.sum(dp * p, axis=-1, keepdims=True)) * (HEAD ** -0.5)
            dq_ref[:, sl] = _raw_dot(ds, k, "nn").astype(bf16)
            dkv_ref[0, :, sl] += _raw_dot(ds, q, "tn")

    return _call(body, "attn_bwd", (Bl, nq), [qs, kvs, os_], (os_, kvs),
                 (jax.ShapeDtypeStruct((Bl * S, 512), bf16), jax.ShapeDtypeStruct((Bl, MEM_LEN, 1024), f32)),
                 sem=("arbitrary", "arbitrary"))(proj, kv, dc)


def _gate_specs(tm):
    half = D_MODEL // 2
    return [pl.BlockSpec((tm, half), functools.partial(lambda i, c: (i, c), c=(C_GL + n * D_MODEL) // half + k))
            for n in range(3) for k in range(2)]


def _merge_out_norm_fwd(branches, wb, proj, w_out, x, g):
    T = proj.shape[0]
    tm = _pick(T, (512, 256, 128))

    def body(a_ref, b_ref, c_ref, w0, w1, w2, g0a, g0b, g1a, g1b, g2a, g2b, wo_ref, x_ref, g_ref, m_ref, x1_ref, h_ref):
        acc = jnp.zeros((tm, D_MODEL), f32)
        for x_n, w_ref, ga, gb in ((a_ref, w0, g0a, g0b), (b_ref, w1, g1a, g1b), (c_ref, w2, g2a, g2b)):
            gate = jax.nn.sigmoid(jnp.concatenate([ga[...], gb[...]], axis=-1).astype(f32))
            acc = acc + gate * _raw_dot(x_n[...], w_ref[...], "nn")
        merged = acc.astype(bf16)
        m_ref[...] = merged
        x1 = x_ref[...] + _raw_dot(merged, wo_ref[...], "nn")
        x1_ref[...] = x1
        y = x1 * lax.rsqrt(jnp.mean(x1 * x1, axis=-1, keepdims=True) + EPS) * g_ref[...]
        h_ref[...] = y.astype(bf16)

    br = pl.BlockSpec((tm, 512), lambda i: (i, 0))
    w = pl.BlockSpec((512, D_MODEL), lambda i: (0, 0))
    row = pl.BlockSpec((tm, D_MODEL), lambda i: (i, 0))
    return _call(body, "merge_out_norm_fwd", (T // tm,),
                 [br, br, br, w, w, w, *_gate_specs(tm), pl.BlockSpec((D_MODEL, D_MODEL), lambda i: (0, 0)), row,
                  pl.BlockSpec((1, D_MODEL), lambda i: (0, 0))],
                 (row, row, row),
                 (jax.ShapeDtypeStruct((T, D_MODEL), bf16), jax.ShapeDtypeStruct((T, D_MODEL), f32),
                  jax.ShapeDtypeStruct((T, D_MODEL), bf16)),
                 sem=("parallel",))(*branches, *wb, *[proj] * 6, w_out, x, g)


def _merge_bwd(branches, wb, proj, merged, dx1, w_out):
    T = proj.shape[0]
    tm = _pick(T, (256, 128))

    def body(a_ref, b_ref, c_ref, w0, w1, w2, g0a, g0b, g1a, g1b, g2a, g2b, m_ref, dx_ref, wo_ref, dgl_ref, d0, d1, d2, gw_ref, gwo_ref):
        @pl.when(pl.program_id(0) == 0)
        def _():
            gw_ref[...] = jnp.zeros_like(gw_ref)
            gwo_ref[...] = jnp.zeros_like(gwo_ref)

        dx = dx_ref[...].astype(bf16)
        gwo_ref[...] += _raw_dot(m_ref[...], dx, "tn")
        dm = _raw_dot(dx, wo_ref[...], "nt")
        for n, (x_ref, w_ref, ga, gb, d_ref) in enumerate(((a_ref, w0, g0a, g0b, d0), (b_ref, w1, g1a, g1b, d1), (c_ref, w2, g2a, g2b, d2))):
            x, w = x_ref[...], w_ref[...]
            up = _raw_dot(x, w, "nn")
            sg = jax.nn.sigmoid(jnp.concatenate([ga[...], gb[...]], axis=-1).astype(f32))
            dgl_ref[n] = (dm * up * sg * (1.0 - sg)).astype(bf16)
            dup = (dm * sg).astype(bf16)
            d_ref[...] = _raw_dot(dup, w, "nt").astype(bf16)
            gw_ref[n] += _raw_dot(x, dup, "tn")

    br = pl.BlockSpec((tm, 512), lambda i: (i, 0))
    w = pl.BlockSpec((512, D_MODEL), lambda i: (0, 0))
    sh = jax.ShapeDtypeStruct((T, 512), bf16)
    row = pl.BlockSpec((tm, D_MODEL), lambda i: (i, 0))
    square = pl.BlockSpec((D_MODEL, D_MODEL), lambda i: (0, 0))
    outs = _call(body, "merge_bwd", (T // tm,), [br, br, br, w, w, w, *_gate_specs(tm), row, row, square],
                 (pl.BlockSpec((3, tm, D_MODEL), lambda i: (0, i, 0)), br, br, br, pl.BlockSpec((3, 512, D_MODEL), lambda i: (0, 0, 0)), square),
                 (jax.ShapeDtypeStruct((3, T, D_MODEL), bf16), sh, sh, sh, jax.ShapeDtypeStruct((3, 512, D_MODEL), f32),
                  jax.ShapeDtypeStruct((D_MODEL, D_MODEL), f32)),
                 sem=("arbitrary",))(*branches, *wb, *[proj] * 6, merged, dx1, w_out)
    return outs[0], outs[1:4], outs[4], outs[5]


CONV_TC = 256


def _shift_down(a, k):
    r = pltpu.roll(a, k, 0)
    row = lax.broadcasted_iota(jnp.int32, (8, a.shape[1]), 0)
    return jnp.concatenate([jnp.where(row >= k, r[:8], 0.0), r[8:]], axis=0)


def _shift_up(a, k):
    n = a.shape[0]
    r = pltpu.roll(a, n - k, 0)
    row = lax.broadcasted_iota(jnp.int32, (8, a.shape[1]), 0)
    return jnp.concatenate([r[:n - 8], jnp.where(row < 8 - k, r[n - 8:], 0.0)], axis=0)


def _conv_pre(a, a1, a2, cw, cb):
    return cb + cw[0:1] * a2 + cw[1:2] * a1 + cw[2:3] * a


def _up_conv_fwd(h2, w_up_t, cw, cb):
    Bl, S, Dd = h2.shape
    nc = D_FF // CONV_TC

    def body(h_ref, wa_ref, wb_ref, cw_ref, cb_ref, a_ref, b_ref, o_ref):
        a16 = _raw_dot(h_ref[0], wa_ref[...], "nt").astype(bf16)
        b16 = _raw_dot(h_ref[0], wb_ref[...], "nt").astype(bf16)
        a_ref[0], b_ref[0] = a16, b16
        a = a16.astype(f32)
        ac = _conv_pre(a, _shift_down(a, 1), _shift_down(a, 2), cw_ref[...], cb_ref[...])
        o_ref[0] = (jax.nn.silu(ac) * b16.astype(f32)).astype(bf16)

    seq = pl.BlockSpec((1, S, CONV_TC), lambda b, c: (b, 0, c))
    sh = jax.ShapeDtypeStruct((Bl, S, D_FF), bf16)
    return _call(body, "up_conv_fwd", (Bl, nc),
                 [pl.BlockSpec((1, S, Dd), lambda b, c: (b, 0, 0)), pl.BlockSpec((CONV_TC, Dd), lambda b, c: (c, 0)),
                  pl.BlockSpec((CONV_TC, Dd), lambda b, c: (nc + c, 0)), pl.BlockSpec((3, CONV_TC), lambda b, c: (0, c)),
                  pl.BlockSpec((1, CONV_TC), lambda b, c: (0, c))],
                 (seq, seq, seq), (sh, sh, sh), sem=("parallel", "parallel"))(h2, w_up_t, w_up_t, cw, cb)


def _down_conv_bwd(dx2, w_down, a, b, cw, cb):
    Bl, S, Dd = dx2.shape
    nc = D_FF // CONV_TC

    def body(dx_ref, wd_ref, a_ref, b_ref, cw_ref, cb_ref, da_ref, db_ref, dcw_ref, dcb_ref):
        dact = _raw_dot(dx_ref[0], wd_ref[...], "nt").astype(bf16).astype(f32)
        a, cw = a_ref[0].astype(f32), cw_ref[...]
        a1, a2 = _shift_down(a, 1), _shift_down(a, 2)
        ac = _conv_pre(a, a1, a2, cw, cb_ref[...])
        sg = jax.nn.sigmoid(ac)
        gated = dact * sg
        db_ref[0] = (gated * ac).astype(bf16)
        dac = gated * b_ref[0].astype(f32) * (1.0 + ac * (1.0 - sg))
        da_ref[0] = (cw[2:3] * dac + cw[1:2] * _shift_up(dac, 1) + cw[0:1] * _shift_up(dac, 2)).astype(bf16)
        dcw_ref[0, 0:1, :] = jnp.sum(dac * a2, axis=0, keepdims=True)
        dcw_ref[0, 1:2, :] = jnp.sum(dac * a1, axis=0, keepdims=True)
        dcw_ref[0, 2:3, :] = jnp.sum(dac * a, axis=0, keepdims=True)
        dcb_ref[0] = jnp.sum(dac, axis=0, keepdims=True)

    seq = pl.BlockSpec((1, S, CONV_TC), lambda b_, c: (b_, 0, c))
    sh = jax.ShapeDtypeStruct((Bl, S, D_FF), bf16)
    return _call(body, "down_conv_bwd", (Bl, nc),
                 [pl.BlockSpec((1, S, Dd), lambda b_, c: (b_, 0, 0)), pl.BlockSpec((CONV_TC, Dd), lambda b_, c: (c, 0)), seq, seq,
                  pl.BlockSpec((3, CONV_TC), lambda b_, c: (0, c)), pl.BlockSpec((1, CONV_TC), lambda b_, c: (0, c))],
                 (seq, seq, pl.BlockSpec((1, 3, CONV_TC), lambda b_, c: (b_, 0, c)), pl.BlockSpec((1, 1, CONV_TC), lambda b_, c: (b_, 0, c))),
                 (sh, sh, jax.ShapeDtypeStruct((Bl, 3, D_FF), f32), jax.ShapeDtypeStruct((Bl, 1, D_FF), f32)),
                 sem=("parallel", "parallel"))(dx2, w_down, a, b, cw, cb)


def _local_step(x, mem, target, p, w_in_t, late_b, late_c, send):
    Bl, S, Dd = x.shape
    T = Bl * S
    x2d, t2d, mem2d = x.reshape(T, Dd), target.reshape(T, Dd), mem.reshape(Bl * MEM_LEN, Dd)
    b_st = jnp.pad(p["b_spatial"].T, ((0, 0), (0, 128 - N_HEAD)))
    lbl = p["lb_logits"]

    proj, h = _norm_proj_fwd(x2d, p["norm1_g"], w_in_t, 1024, 1664)
    a_out = _gmlp_fwd(proj, p["ln_v_g"], p["ln_v_b"], p["w_spatial"], b_st)
    proj3 = proj.reshape(Bl, S, IN_WIDTH)
    b_out, states = _hgrn_fwd(proj3, lbl, p["hgrn_norm_g"], Bl, S)
    b_out = b_out.reshape(T, 512)
    memn = _rms_fwd(mem2d, p["mem_norm_g"], "memnorm_fwd")
    w = late_b(b_out)
    wb = w["w_branch"]
    kv = _mm(memn, w["w_mem_kv"], "nn", f32, "kv_fwd", 512, 1024).reshape(Bl, MEM_LEN, 2 * 512)
    c_out = _attn_fwd(proj, kv, Bl, S)
    branches = (a_out, b_out, c_out)
    merged, x1, h2 = _merge_out_norm_fwd(branches, wb, proj, w["w_out"], x2d, p["norm2_g"])
    w.update(late_c(h2))
    ffn_a, ffn_b, act = _up_conv_fwd(h2.reshape(Bl, S, Dd), w["w_up_t"], w["conv_w"], p["conv_b"])
    act = act.reshape(T, D_FF)
    loss_part, dx2, dx2_16, g_final = _down_final_loss(act, w["w_down"], x1, p["final_g"], t2d)

    g_w_down = _mm(act, dx2_16, "tn", bf16, "down_dw", 1408, 1024, 1024)
    da, db, g_conv_w, g_conv_b = _down_conv_bwd(dx2_16.reshape(Bl, S, Dd), w["w_down"], ffn_a, ffn_b, w["conv_w"], p["conv_b"])
    da, db = da.reshape(T, D_FF), db.reshape(T, D_FF)
    g_w_up_t = _mm(da, h2, "tn", bf16, "up_dw_a", 1408, 1024, 1024, into=(lax.empty((2 * D_FF, D_MODEL), bf16), 0))
    g_w_up_t = _mm(db, h2, "tn", bf16, "up_dw_b", 1408, 1024, 1024, into=(g_w_up_t, D_FF))
    send("c", dict(w_up=g_w_up_t, conv_w=jnp.sum(g_conv_w, axis=0), w_down=g_w_down))
    dx1, g_norm2 = _mm_rms_bwd([(da, w["w_up_t"], 0), (db, w["w_up_t"], 1)], x1, p["norm2_g"], dx2, "up_dx_norm2_bwd", 256)

    dgl, dbr, g_w_branch, g_w_out = _merge_bwd(branches, wb, proj, merged, dx1, w["w_out"])
    dxq, dkv = _attn_bwd(proj, kv, dbr[2], Bl, S)
    dkv = dkv.reshape(Bl * MEM_LEN, 2 * 512)
    g_w_kv = _mm(memn, dkv, "tn", bf16, "kv_dw", 1024, 1024, 512)
    send("b", dict(w_mem_kv=g_w_kv, w_branch=g_w_branch, w_out=g_w_out))
    dzuv, g_ln_g, g_ln_b, g_w_sp, g_b_sp = _gmlp_bwd(proj, p["ln_v_g"], p["ln_v_b"], p["w_spatial"], b_st, dbr[0])
    dproj, g_lbl, g_ng = _hgrn_bwd(proj3, lbl, p["hgrn_norm_g"], states, dbr[1].reshape(Bl, S, 512), dzuv.reshape(Bl, S, -1),
                                   dxq.reshape(Bl, S, -1), dgl.reshape(3, Bl, S, Dd), Bl, S)
    dproj = dproj.reshape(T, IN_WIDTH)
    half = Dd // 2
    send("a0", dict(w_in_half=_mm(dproj, h, "tn", bf16, "proj_dw_0", 512, half, n_tiles=(0, 1))))
    g_half = _mm(dproj, h, "tn", bf16, "proj_dw_1", 512, half, n_tiles=(1, 1))
    dkv, g_half = lax.optimization_barrier((dkv, g_half))
    send("a1", dict(w_in_half=g_half))
    dmemn = _mm(dkv, w["w_mem_kv"], "nt", f32, "kv_dx", 512, 1024)
    _, g_mem_norm = _rms_bwd(mem2d, p["mem_norm_g"], dmemn, "memnorm_bwd")
    dx, g_norm1 = _mm_rms_bwd([(dproj, w_in_t, 0)], x2d, p["norm1_g"], dx1, "proj_dx_norm1_bwd", 256)

    gs = dict(w_spatial=g_w_sp, norm1_g=g_norm1, mem_norm_g=g_mem_norm, norm2_g=g_norm2, final_g=g_final, lb_logits=g_lbl,
              ln_v_g=g_ln_g, ln_v_b=g_ln_b, b_spatial=g_b_sp, hgrn_norm_g=g_ng, conv_b=g_conv_b)
    return loss_part, dx.reshape(Bl, S, Dd), gs


def _coords():
    return lax.axis_index("x"), lax.axis_index("y"), lax.axis_index("c")


def _slot(dev):
    return 4 * dev[0] + 2 * dev[1] + dev[2]


def _comm_call(body, name, arrays, out_shapes, n_sem):
    n = len(arrays)
    hbm = pl.BlockSpec(memory_space=pl.ANY)
    return pl.pallas_call(
        body, name=name, out_shape=out_shapes, in_specs=[hbm] * n, out_specs=[hbm] * n,
        scratch_shapes=[pltpu.SemaphoreType.DMA((n_sem, n)), pltpu.SemaphoreType.DMA((n_sem, n)), pltpu.SemaphoreType.DMA((n,))])(*arrays)


def _all_gather(blocks, name):
    n = len(blocks)

    def body(*refs):
        x_refs, o_refs, (send_sems, recv_sems, local_sems) = refs[:n], refs[n:2 * n], refs[2 * n:]
        x, y, c = _coords()
        me, sibling = (x, y, c), (x, y, 1 - c)
        chips = [(1 - x, y), (x, 1 - y), (1 - x, 1 - y)]

        def copy(a, k, block_dev, to, from_input=False):
            dst = o_refs[a].at[_slot(block_dev)]
            return pltpu.make_async_remote_copy(src_ref=x_refs[a] if from_input else dst, dst_ref=dst, send_sem=send_sems.at[k, a],
                                                recv_sem=recv_sems.at[k, a], device_id=to, device_id_type=MESH)

        mine = [pltpu.make_async_copy(x_refs[a], o_refs[a].at[_slot(me)], local_sems.at[a]) for a in range(n)]
        first = [copy(a, 0, me, sibling, True) for a in range(n)]
        first += [copy(a, 1 + j, me, (*chip, c), True) for j, chip in enumerate(chips) for a in range(n)]
        for cp in mine + first:
            cp.start()
        passed = []
        for j, chip in enumerate(chips):
            for a in range(n):
                copy(a, 1 + j, (*chip, c), me).wait_recv()
                fwd = copy(a, 4 + j, (*chip, c), sibling)
                fwd.start()
                passed.append(fwd)
        for a in range(n):
            copy(a, 0, sibling, me).wait_recv()
        for j, chip in enumerate(chips):
            for a in range(n):
                copy(a, 4 + j, (*chip, 1 - c), me).wait_recv()
        for cp in first + passed:
            cp.wait_send()
        for cp in mine:
            cp.wait()

    return _comm_call(body, name, blocks, [jax.ShapeDtypeStruct((N_DEV,) + b.shape, b.dtype) for b in blocks], 7)


_REL = [(0, 0, 1), (0, 1, 0), (0, 1, 1), (1, 0, 0), (1, 0, 1), (1, 1, 0), (1, 1, 1)]


def _seq_exchange(arrays, gather, name, collective_id):
    n = len(arrays)
    hbm = pltpu.MemorySpace.HBM
    srcs = [jax.new_ref(a, memory_space=hbm) for a in arrays]
    lands = [jax.empty_ref(jax.ShapeDtypeStruct(((N_DEV,) + a.shape) if gather else a.shape, a.dtype), memory_space=hbm) for a in arrays]

    @pl.kernel(mesh=plsc.ScalarSubcoreMesh(axis_name="sequencer", num_cores=1), name=name,
               scratch_types=(pltpu.SemaphoreType.DMA((7, n)), pltpu.SemaphoreType.DMA((7, n)), pltpu.SemaphoreType.DMA((n,))),
               compiler_params=pltpu.CompilerParams(collective_id=collective_id))
    def launch(send, recv, local):
        x, y, c = _coords()
        me = (x, y, c)
        peers = [(x ^ dx, y ^ dy, c ^ dc) for dx, dy, dc in _REL]
        barrier = pltpu.get_barrier_semaphore()
        for peer in peers:
            pl.semaphore_signal(barrier, inc=1, device_id=peer, device_id_type=MESH)
        pl.semaphore_wait(barrier, len(peers))

        def copy(a, k, peer, arrival):
            return pltpu.make_async_remote_copy(
                src_ref=srcs[a] if gather else srcs[a].at[_slot(peer)], dst_ref=lands[a].at[_slot(peer if arrival else me)],
                send_sem=send.at[k, a], recv_sem=recv.at[k, a], device_id=peer, device_id_type=MESH)

        mine = [pltpu.make_async_copy(srcs[a] if gather else srcs[a].at[_slot(me)], lands[a].at[_slot(me)], local.at[a])
                for a in range(n)]
        out = [copy(a, k, peer, False) for a in range(n) for k, peer in enumerate(peers)]
        for cp in mine + out:
            cp.start()
        for a in range(n):
            for k, peer in enumerate(peers):
                copy(a, k, peer, True).wait_recv()
        for cp in out:
            cp.wait_send()
        for cp in mine:
            cp.wait()

    launch()
    return [land[...] for land in lands]


def _adam_math(w, g, m, v):
    m_ = ADAM_B1 * m + (1.0 - ADAM_B1) * g
    v_ = ADAM_B2 * v + (1.0 - ADAM_B2) * jnp.square(g)
    m_hat = m_ / (1.0 - ADAM_B1 ** ADAM_STEP)
    v_hat = v_ / (1.0 - ADAM_B2 ** ADAM_STEP)
    return -ADAM_LR * (m_hat / (jnp.sqrt(v_hat) + ADAM_EPS) + ADAM_WD * w), m_, v_


def _reduce_adamw(parts, w, m, v, name):
    R, L = w.shape
    tr = _pick(R, (256, 208, 176, 128, 64, 32, 16, 8))
    n = len(parts)

    def body(*refs):
        w_ref, m_ref, v_ref, g_ref, d_ref, nm_ref, nv_ref = refs[n:]
        pieces = []
        for p_ref in refs[:n]:
            g = p_ref[0].astype(f32)
            for i in range(1, N_DEV):
                g = g + p_ref[i].astype(f32)
            pieces.append(g)
        g = pieces[0] if n == 1 else jnp.concatenate(pieces, axis=-1)
        g_ref[...] = g
        d_ref[...], nm_ref[...], nv_ref[...] = _adam_math(w_ref[...], g, m_ref[...], v_ref[...])

    blk = pl.BlockSpec((tr, L), lambda i: (i, 0))
    sh = jax.ShapeDtypeStruct((R, L), f32)
    return _call(body, name, (R // tr,), [pl.BlockSpec((N_DEV, tr, q.shape[2]), lambda i: (0, i, 0)) for q in parts] + [blk, blk, blk],
                 (blk,) * 4, (sh,) * 4, sem=("parallel",))(*parts, w, m, v)


SMALL = (("w_spatial", (512, 128), 0), ("norm1_g", (1, 1024), 512), ("mem_norm_g", (1, 1024), 520), ("norm2_g", (1, 1024), 528),
         ("final_g", (1, 1024), 536), ("lb_logits", (2, 512), 544), ("ln_v_g", (1, 512), 552), ("ln_v_b", (1, 512), 556),
         ("b_spatial", (4, 128), 560), ("hgrn_norm_g", (1, 128), 564), ("conv_b", (1, 2816), 565))
LOSS_ROW, SMALL_USED, SMALL_ROWS = 587, 588, 640


def _segments(shape, base):
    r, n = shape
    per = n // 128
    return [(base + i * per + j, i, slice(j * 128, (j + 1) * 128)) for i in range(r) for j in range(per)]


def _pack_small(gs, loss_part):
    names = [n for n, _, _ in SMALL]

    def body(*refs):
        src, loss_ref, o_ref = dict(zip(names, refs[:-2])), refs[-2], refs[-1]
        o_ref[SMALL_USED:SMALL_ROWS, :] = jnp.zeros((SMALL_ROWS - SMALL_USED, 128), f32)
        o_ref[LOSS_ROW:LOSS_ROW + 1, :] = loss_ref[...]
        for name, shape, base in SMALL:
            ref = src[name]
            if name == "w_spatial":
                o_ref[base:base + 512, :] = ref[...].reshape(512, 128)
            elif name == "b_spatial":
                o_ref[base:base + 4, :] = ref[0:4, :]
            elif name == "conv_b":
                per_example = functools.reduce(lambda u, v_: u + v_, [ref[b] for b in range(ref.shape[0])])
                for row, i, sl in _segments(shape, base):
                    o_ref[row:row + 1, :] = per_example[i:i + 1, sl]
            elif name == "hgrn_norm_g":
                per_head = [ref[b, h] for b in range(ref.shape[0]) for h in range(N_HEAD)]
                o_ref[base:base + 1, :] = functools.reduce(lambda u, v_: u + v_, per_head)
            else:
                for row, i, sl in _segments(shape, base):
                    o_ref[row:row + 1, :] = ref[i:i + 1, sl]

    return pl.pallas_call(body, name="pack_small", out_shape=jax.ShapeDtypeStruct((SMALL_ROWS, 128), f32))(
        *[gs[n] for n in names], loss_part)


def _small_update(gathered, w, m, v):
    names = [n for n, _, _ in SMALL]
    k = len(names)

    def body(*refs):
        p_ref = refs[0]
        ins = [dict(zip(names, refs[1 + i * k:1 + (i + 1) * k])) for i in range(3)]
        outs = [dict(zip(names, refs[1 + (3 + i) * k:1 + (4 + i) * k])) for i in range(4)]
        loss_ref, gsum = refs[-2], refs[-1]
        g = p_ref[0]
        for i in range(1, N_DEV):
            g = g + p_ref[i]
        gsum[...] = g
        loss_ref[...] = gsum[LOSS_ROW:LOSS_ROW + 1, :]
        for name, shape, base in SMALL:
            if name == "w_spatial":
                where = [(slice(base, base + 512), (slice(None), slice(None)))]
            else:
                where = [(slice(row, row + 1), (slice(i, i + 1), sl)) for row, i, sl in _segments(shape, base)]
            for rows, at in where:
                g_ = gsum[rows, :]
                d_, m_, v_ = _adam_math(ins[0][name][at], g_, ins[1][name][at], ins[2][name][at])
                for o, val in zip(outs, (g_, d_, m_, v_)):
                    o[name][at] = val

    args = [gathered] + [d[n] for d in (w, m, v) for n in names]
    out_shapes = [jax.ShapeDtypeStruct(shape, f32) for _ in range(4) for _, shape, _ in SMALL] + [jax.ShapeDtypeStruct((1, 128), f32)]
    outs = pl.pallas_call(body, name="small_update", out_shape=out_shapes, scratch_shapes=[pltpu.VMEM((SMALL_ROWS, 128), f32)])(*args)
    return [dict(zip(names, outs[i * k:(i + 1) * k])) for i in range(4)], outs[-1]


def _cols_full(g):
    return jnp.moveaxis(g, 0, -2).reshape(g.shape[1:-1] + (N_DEV * g.shape[-1],))


def _cols_parts(full):
    n = full.shape[-1] // N_DEV
    return jnp.moveaxis(full.reshape(full.shape[:-1] + (N_DEV, n)), -2, 0)


def kernel(x, mem, norm1_g, w_in, ln_v_g, ln_v_b, w_spatial, b_spatial, lb_logits, hgrn_norm_g, mem_norm_g, w_mem_kv, w_branch, w_out, norm2_g, w_up, conv_w, conv_b, w_down, final_g, loss_target, m_norm1_g, m_w_in, m_ln_v_g, m_ln_v_b, m_w_spatial, m_b_spatial, m_lb_logits, m_hgrn_norm_g, m_mem_norm_g, m_w_mem_kv, m_w_branch, m_w_out, m_norm2_g, m_w_up, m_conv_w, m_conv_b, m_w_down, m_final_g, v_norm1_g, v_w_in, v_ln_v_g, v_ln_v_b, v_w_spatial, v_b_spatial, v_lb_logits, v_hgrn_norm_g, v_mem_norm_g, v_w_mem_kv, v_w_branch, v_w_out, v_norm2_g, v_w_up, v_conv_w, v_conv_b, v_w_down, v_final_g):
    given = dict(locals())
    order = ("norm1_g", "w_in", "ln_v_g", "ln_v_b", "w_spatial", "b_spatial", "lb_logits", "hgrn_norm_g", "mem_norm_g",
             "w_mem_kv", "w_branch", "w_out", "norm2_g", "w_up", "conv_w", "conv_b", "w_down", "final_g")
    groups = dict(a=("w_in",), b=("w_mem_kv", "w_branch", "w_out"), c=("w_up", "conv_w", "w_down"))

    by_rows = ("w_in", "w_up")
    shard_of = lambda n, prefix="": jnp.swapaxes(given[prefix + n][0], 0, 1) if n in by_rows else given[prefix + n][0]
    wire = {n: shard_of(n).astype(f32 if n == "conv_w" else bf16) for ns in groups.values() for n in ns}
    w_in_full = _all_gather([wire["w_in"]], "gather_w_in")[0].reshape(IN_WIDTH, D_MODEL)
    w_in_full, wire_b, wire_c = lax.optimization_barrier((w_in_full, [wire[n] for n in groups["b"]], [wire[n] for n in groups["c"]]))
    rest_b = _seq_exchange(wire_b, True, "gather_b", 1)
    rest_c = _seq_exchange(wire_c, True, "gather_c", 6)

    def late_b(after):
        _, (kv_, br_, out_) = lax.optimization_barrier((after, tuple(rest_b)))
        br_ = _cols_full(br_)
        return dict(w_mem_kv=kv_.reshape(D_MODEL, 2 * 512), w_branch=[br_[n] for n in range(3)], w_out=out_.reshape(D_MODEL, D_MODEL))

    def late_c(after):
        _, (up_, cw_, down_) = lax.optimization_barrier((after, tuple(rest_c)))
        return dict(w_up_t=up_.reshape(2 * D_FF, D_MODEL), conv_w=_cols_full(cw_), w_down=down_.reshape(D_FF, D_MODEL))

    to_parts = dict(w_in_half=lambda g_: g_.reshape(N_DEV, -1, D_MODEL // 2), w_up=lambda g_: g_.reshape(N_DEV, -1, D_MODEL), conv_w=_cols_parts,
                    w_branch=lambda g_: _cols_parts(g_.astype(bf16)).reshape(N_DEV, -1, 128),
                    w_mem_kv=lambda g_: g_.reshape(N_DEV, -1, 2 * 512), w_out=lambda g_: g_.astype(bf16).reshape(N_DEV, -1, D_MODEL),
                    w_down=lambda g_: g_.reshape(N_DEV, -1, D_MODEL))
    scatters = {}


    def send(tag, grads_):
        parts = [to_parts[n](g_) for n, g_ in grads_.items()]
        scatters[tag] = _seq_exchange(parts, False, f"scatter_{tag}", dict(a0=2, a1=7, b=4, c=5)[tag])

    small_2d = lambda prefix: {n: given[prefix + n].reshape(shape) for n, shape, _ in SMALL}
    p = small_2d("")
    p["w_spatial"] = w_spatial[0]
    updates = {}

    def update(tag):
        arrived = [scatters["a0"] + scatters["a1"]] if tag == "a" else [[parts] for parts in scatters[tag]]
        for n, parts in zip(groups[tag], arrived):
            state = [shard_of(n, pre) for pre in ("", "m_", "v_")]
            res = _reduce_adamw(parts, *[a.reshape(-1, a.shape[-1]) for a in state], "adamw_" + n)
            updates[n] = [jnp.swapaxes(r, 0, 1) for r in res] if n in by_rows else res

    loss_part, grad_x, gs = _local_step(x, mem, loss_target, p, w_in_full, late_b, late_c, send)

    packed, (scatters["c"], scatters["b"]) = lax.optimization_barrier((_pack_small(gs, loss_part), (scatters["c"], scatters["b"])))
    gathered = _seq_exchange([packed], True, "gather_small", 3)[0]

    update("c")
    update("b")
    update("a")
    grads, delta, new_m, new_v = {}, {}, {}, {}
    for n, res in updates.items():
        grads[n], delta[n], new_m[n], new_v[n] = [r.reshape(given[n].shape) for r in res]

    small_results, loss_row = _small_update(gathered, small_2d(""), small_2d("m_"), small_2d("v_"))
    for dst, res in zip((grads, delta, new_m, new_v), small_results):
        for n, _, _ in SMALL:
            dst[n] = res[n].reshape(given[n].shape)
    loss = loss_row[0, 0]

    return (loss, grad_x, *[grads[n] for n in order], *[delta[n] for n in order], *[new_m[n] for n in order],
            *[new_v[n] for n in order])
```

```python
import functools

import jax
import jax.numpy as jnp
from jax import lax
from jax.experimental import pallas as pl
from jax.experimental.pallas import tpu as pltpu
from jax.experimental.pallas import tpu_sc as plsc

f32 = jnp.float32
bf16 = jnp.bfloat16

N_DEV = 8
D_MODEL = 1024
EPS = 1e-6
GM_CHUNK = 128
HG_CHUNK = 64
HEAD = 128
N_HEAD = 4
MEM_LEN = 256
D_FF = 2816
IN_WIDTH = 6656
C_ZU, C_HQ, C_HF, C_HI, C_HG, C_XQ, C_GL = 0, 1024, 1536, 2048, 2560, 3072, 3584
ADAM_LR, ADAM_B1, ADAM_B2, ADAM_EPS, ADAM_WD, ADAM_STEP = 0.001, 0.9, 0.999, 1e-08, 0.01, 10
VMEM_LIMIT = 56 * 1024 * 1024
MESH = pl.DeviceIdType.MESH


def _pick(n, cands):
    for c in cands:
        if n % c == 0:
            return c
    return n


def _call(body, name, grid, in_specs, out_specs, out_shape, scratch=(), sem=None, **cp):
    params = dict(vmem_limit_bytes=VMEM_LIMIT, **cp)
    if sem is not None:
        params["dimension_semantics"] = sem
    return pl.pallas_call(
        body, name=name, grid=grid, in_specs=in_specs, out_specs=out_specs, out_shape=out_shape,
        scratch_shapes=list(scratch), compiler_params=pltpu.CompilerParams(**params))


_DN = {"nn": (((1,), (0,)), ((), ())), "nt": (((1,), (1,)), ((), ())), "tn": (((0,), (0,)), ((), ()))}


def _raw_dot(a, b, mode):
    return lax.dot_general(a.astype(bf16), b.astype(bf16), _DN[mode], preferred_element_type=f32)


@jax.custom_vjp
def _dot_nn(a, b):
    return _raw_dot(a, b, "nn")


_dot_nn.defvjp(lambda a, b: (_raw_dot(a, b, "nn"), (a, b)),
               lambda r, g: (_raw_dot(g, r[1], "nt"), _raw_dot(r[0], g, "tn")))


@jax.custom_vjp
def _dot_nt(a, b):
    return _raw_dot(a, b, "nt")


_dot_nt.defvjp(lambda a, b: (_raw_dot(a, b, "nt"), (a, b)),
               lambda r, g: (_raw_dot(g, r[1], "nn"), _raw_dot(g, r[0], "tn")))


@jax.custom_vjp
def _dot_tn(a, b):
    return _raw_dot(a, b, "tn")


_dot_tn.defvjp(lambda a, b: (_raw_dot(a, b, "tn"), (a, b)),
               lambda r, g: (_raw_dot(r[1], g, "nt"), _raw_dot(r[0], g, "nn")))


def _tri(n, lower):
    r = lax.broadcasted_iota(jnp.int32, (n, n), 0)
    c = lax.broadcasted_iota(jnp.int32, (n, n), 1)
    return ((c <= r) if lower else (c >= r)).astype(f32)


def _sel_dot(sel, x, mode, x_first=False, pieces=3):
    sel = sel.astype(bf16)
    out, rest = None, x
    for p in range(pieces):
        piece = rest.astype(bf16)
        part = lax.dot_general(*((piece, sel) if x_first else (sel, piece)), _DN[mode], preferred_element_type=f32)
        out = part if out is None else out + part
        if p + 1 < pieces:
            rest = rest - piece.astype(f32)
    return out


def _egrad(fn, x, ct):
    return jax.vjp(fn, x)[1](ct)[0]


def _mm(a, b, mode, out_dtype, name, tm, tn, tk=None, residual=None, into=None, n_tiles=None):
    if mode == "nn":
        (M, K), (_, N) = a.shape, b.shape
    elif mode == "nt":
        (M, K), (N, _) = a.shape, b.shape
    else:
        (K, M), (_, N) = a.shape, b.shape
    j0 = 0
    if n_tiles is not None:
        assert mode != "nt" and N % tn == 0 and residual is None
        j0, N = n_tiles[0], n_tiles[1] * tn
    tm, tn = min(tm, M), min(tn, N)
    tk = K if tk is None else min(tk, K)
    assert M % tm == 0 and N % tn == 0 and K % tk == 0, (name, M, N, K, tm, tn, tk)
    nk = K // tk

    def body(*refs):
        acc_ref = refs[-1] if nk > 1 else None
        refs = refs[:-1] if nk > 1 else refs
        if residual is None:
            a_ref, b_ref, *_, o_ref = refs
        else:
            a_ref, b_ref, r_ref, o_ref = refs

        def finish(r):
            if residual is not None:
                r = r + r_ref[...]
            o_ref[...] = r.astype(out_dtype)

        part = _raw_dot(a_ref[...], b_ref[...], mode)
        if nk == 1:
            finish(part)
            return
        k = pl.program_id(2)

        @pl.when(k == 0)
        def _():
            acc_ref[...] = part

        @pl.when((k > 0) & (k < nk - 1))
        def _():
            acc_ref[...] += part

        @pl.when(k == nk - 1)
        def _():
            finish(acc_ref[...] + part)

    a_spec = {"nn": pl.BlockSpec((tm, tk), lambda i, j, k: (i, k)),
              "nt": pl.BlockSpec((tm, tk), lambda i, j, k: (i, k)),
              "tn": pl.BlockSpec((tk, tm), lambda i, j, k: (k, i))}[mode]
    b_spec = {"nn": pl.BlockSpec((tk, tn), lambda i, j, k: (k, j + j0)),
              "nt": pl.BlockSpec((tn, tk), lambda i, j, k: (j, k)),
              "tn": pl.BlockSpec((tk, tn), lambda i, j, k: (k, j + j0))}[mode]
    o_spec = pl.BlockSpec((tm, tn), lambda i, j, k: (i, j))
    in_specs = [a_spec, b_spec] + ([o_spec] if residual is not None else [])
    args = (a, b) + ((residual,) if residual is not None else ())
    out_shape = jax.ShapeDtypeStruct((M, N), out_dtype)
    extra = {}
    if into is not None:
        assert residual is None and into[1] % tm == 0
        o_spec = pl.BlockSpec((tm, tn), lambda i, j, k: (i + into[1] // tm, j))
        out_shape = jax.ShapeDtypeStruct(into[0].shape, out_dtype)
        in_specs, args = in_specs + [pl.BlockSpec(memory_space=pl.ANY)], args + (into[0],)
        extra = dict(input_output_aliases={2: 0})
    return pl.pallas_call(
        body, name=name, grid=(M // tm, N // tn, nk), in_specs=in_specs, out_specs=o_spec, out_shape=out_shape,
        scratch_shapes=[pltpu.VMEM((tm, tn), f32)] if nk > 1 else [],
        compiler_params=pltpu.CompilerParams(vmem_limit_bytes=VMEM_LIMIT, dimension_semantics=("parallel", "parallel", "arbitrary")),
        **extra)(*args)


def _rms_fwd(x, g, name):
    R, Dd = x.shape
    tr = _pick(R, (512, 256, 128))

    def body(x_ref, g_ref, o_ref):
        xf = x_ref[...]
        o_ref[...] = (xf * lax.rsqrt(jnp.mean(xf * xf, axis=-1, keepdims=True) + EPS) * g_ref[...]).astype(bf16)

    row = pl.BlockSpec((tr, Dd), lambda i: (i, 0))
    return _call(body, name, (R // tr,), [row, pl.BlockSpec((1, Dd), lambda i: (0, 0))], row, jax.ShapeDtypeStruct((R, Dd), bf16),
                 sem=("parallel",))(x, g)


def _norm_proj_fwd(x, g, w_t, tm, tn):
    R, Dd = x.shape
    N = w_t.shape[0]
    tm = min(tm, R)
    assert R % tm == 0 and N % tn == 0

    def body(x_ref, g_ref, w_ref, p_ref, h_ref):
        @pl.when(pl.program_id(1) == 0)
        def _():
            xf = x_ref[...]
            h_ref[...] = (xf * lax.rsqrt(jnp.mean(xf * xf, axis=-1, keepdims=True) + EPS) * g_ref[...]).astype(bf16)

        p_ref[...] = _raw_dot(h_ref[...], w_ref[...], "nt").astype(bf16)

    row = pl.BlockSpec((tm, Dd), lambda i, j: (i, 0))
    return _call(body, "norm1_proj_fwd", (R // tm, N // tn),
                 [row, pl.BlockSpec((1, Dd), lambda i, j: (0, 0)), pl.BlockSpec((tn, Dd), lambda i, j: (j, 0))],
                 (pl.BlockSpec((tm, tn), lambda i, j: (i, j)), row),
                 (jax.ShapeDtypeStruct((R, N), bf16), jax.ShapeDtypeStruct((R, Dd), bf16)), sem=("parallel", "arbitrary"))(x, g, w_t)


def _rms_bwd(x, g, dh, name, residual=None):
    R, Dd = x.shape
    tr = _pick(R, (512, 256, 128))

    def body(*refs):
        if residual is None:
            x_ref, g_ref, dh_ref, dx_ref, dg_ref = refs
        else:
            x_ref, g_ref, dh_ref, r_ref, dx_ref, dg_ref = refs
        xf = x_ref[...]
        rs = lax.rsqrt(jnp.mean(xf * xf, axis=-1, keepdims=True) + EPS)
        y = xf * rs
        dh_ = dh_ref[...].astype(f32)
        dy = dh_ * g_ref[...]
        dx = rs * (dy - y * jnp.mean(dy * y, axis=-1, keepdims=True))
        if residual is not None:
            dx = dx + r_ref[...]
        dx_ref[...] = dx

        @pl.when(pl.program_id(0) == 0)
        def _():
            dg_ref[...] = jnp.zeros_like(dg_ref)

        dg_ref[...] += jnp.sum(dh_ * y, axis=0, keepdims=True)

    row = pl.BlockSpec((tr, Dd), lambda i: (i, 0))
    vec = pl.BlockSpec((1, Dd), lambda i: (0, 0))
    in_specs = [row, vec, row] + ([row] if residual is not None else [])
    args = (x, g, dh) + ((residual,) if residual is not None else ())
    return _call(body, name, (R // tr,), in_specs, (row, vec),
                 (jax.ShapeDtypeStruct((R, Dd), f32), jax.ShapeDtypeStruct((1, Dd), f32)), sem=("arbitrary",))(*args)


def _mm_rms_bwd(pairs, x, g, residual, name, tm):
    M = x.shape[0]
    Dd = x.shape[1]
    tm = min(tm, M)
    n = len(pairs)

    def body(*refs):
        ab_refs, (x_ref, g_ref, r_ref, dx_ref, dg_ref) = refs[:2 * n], refs[2 * n:]
        dh_ = _raw_dot(ab_refs[0][...], ab_refs[1][...], "nn")
        for k in range(1, n):
            dh_ = dh_ + _raw_dot(ab_refs[2 * k][...], ab_refs[2 * k + 1][...], "nn")
        xf = x_ref[...]
        rs = lax.rsqrt(jnp.mean(xf * xf, axis=-1, keepdims=True) + EPS)
        y = xf * rs
        dy = dh_ * g_ref[...]
        dx_ref[...] = rs * (dy - y * jnp.mean(dy * y, axis=-1, keepdims=True)) + r_ref[...]

        @pl.when(pl.program_id(0) == 0)
        def _():
            dg_ref[...] = jnp.zeros_like(dg_ref)

        dg_ref[...] += jnp.sum(dh_ * y, axis=0, keepdims=True)

    row = pl.BlockSpec((tm, Dd), lambda i: (i, 0))
    vec = pl.BlockSpec((1, Dd), lambda i: (0, 0))
    in_specs, args = [], []
    for a, b, k in pairs:
        in_specs += [pl.BlockSpec((tm, a.shape[1]), lambda i: (i, 0)),
                     pl.BlockSpec((a.shape[1], b.shape[1]), functools.partial(lambda i, k_: (k_, 0), k_=k))]
        args += [a, b]
    in_specs += [row, vec, row]
    args += [x, g, residual]
    return _call(body, name, (M // tm,), in_specs, (row, vec),
                 (jax.ShapeDtypeStruct((M, Dd), f32), jax.ShapeDtypeStruct((1, Dd), f32)), sem=("arbitrary",))(*args)


def _down_final_loss(act, w_down, x1, g, target):
    R, Dd = x1.shape
    tr = _pick(R, (512, 256, 128))

    def body(a_ref, w_ref, x1_ref, g_ref, t_ref, loss_ref, dx_ref, dxb_ref, dg_ref):
        xf = _raw_dot(a_ref[...], w_ref[...], "nn") + x1_ref[...]
        rs = lax.rsqrt(jnp.mean(xf * xf, axis=-1, keepdims=True) + EPS)
        y = xf * rs
        err = y * g_ref[...] - t_ref[...]
        dh_ = err * (1.0 / Dd)
        dy = dh_ * g_ref[...]
        dx = rs * (dy - y * jnp.mean(dy * y, axis=-1, keepdims=True))
        dx_ref[...] = dx
        dxb_ref[...] = dx.astype(bf16)

        @pl.when(pl.program_id(0) == 0)
        def _():
            dg_ref[...] = jnp.zeros_like(dg_ref)
            loss_ref[...] = jnp.zeros_like(loss_ref)

        dg_ref[...] += jnp.sum(dh_ * y, axis=0, keepdims=True)
        part = jnp.sum(jnp.mean(err * err, axis=-1, keepdims=True), axis=0, keepdims=True)
        loss_ref[...] += 0.5 * part

    row = pl.BlockSpec((tr, Dd), lambda i: (i, 0))
    vec = pl.BlockSpec((1, Dd), lambda i: (0, 0))
    in_specs = [pl.BlockSpec((tr, act.shape[1]), lambda i: (i, 0)), pl.BlockSpec(w_down.shape, lambda i: (0, 0)), row, vec, row]
    return _call(body, "down_final_loss", (R // tr,), in_specs, (pl.BlockSpec((1, 128), lambda i: (0, 0)), row, row, vec),
                 (jax.ShapeDtypeStruct((1, 128), f32), jax.ShapeDtypeStruct((R, Dd), f32), jax.ShapeDtypeStruct((R, Dd), bf16),
                  jax.ShapeDtypeStruct((1, Dd), f32)), sem=("arbitrary",))(act, w_down, x1, g, target)


def _gmlp_parts(zuv, ln_g, ln_b):
    zu, zv = zuv[:, :512], zuv[:, 512:]
    u = jax.nn.gelu(zu)
    v = jax.nn.gelu(zv)
    mu = jnp.mean(v, axis=-1, keepdims=True)
    rs = lax.rsqrt(jnp.mean(jnp.square(v - mu), axis=-1, keepdims=True) + EPS)
    xh = (v - mu) * rs
    return zu, zv, u, xh, rs, xh * ln_g + ln_b


GM_TILE_CHUNKS = 4


def _gmlp_tile(T):
    n = _pick(T // GM_CHUNK, (GM_TILE_CHUNKS, 2, 1))
    return n, n * GM_CHUNK


def _gmlp_fwd(proj, ln_g, ln_b, w_s, b_st):
    T = proj.shape[0]
    nch, rows = _gmlp_tile(T)

    def body(p_ref, g_ref, b_ref, w_ref, bs_ref, o_ref):
        _, _, u, _, _, vn = _gmlp_parts(p_ref[...].astype(f32), g_ref[...], b_ref[...])
        causal = _tri(GM_CHUNK, True) > 0
        for gi in range(N_HEAD):
            sl = slice(gi * HEAD, (gi + 1) * HEAD)
            w = jnp.where(causal, w_ref[gi], 0.0)
            for ch in range(nch):
                rs_ = slice(ch * GM_CHUNK, (ch + 1) * GM_CHUNK)
                mixed = _raw_dot(w, vn[rs_, sl], "nn") + bs_ref[:, gi:gi + 1]
                o_ref[rs_, sl] = (u[rs_, sl] * mixed).astype(bf16)

    vec = pl.BlockSpec((1, 512), lambda i: (0, 0))
    return _call(body, "gmlp_fwd", (T // rows,),
                 [pl.BlockSpec((rows, 1024), lambda i: (i, 0)), vec, vec,
                  pl.BlockSpec((N_HEAD, GM_CHUNK, GM_CHUNK), lambda i: (0, 0, 0)), pl.BlockSpec((GM_CHUNK, 128), lambda i: (0, 0))],
                 pl.BlockSpec((rows, 512), lambda i: (i, 0)), jax.ShapeDtypeStruct((T, 512), bf16), sem=("parallel",))(
        proj, ln_g, ln_b, w_s, b_st)


def _gmlp_bwd(proj, ln_g, ln_b, w_s, b_st, da):
    T = proj.shape[0]
    nch, rows = _gmlp_tile(T)

    def body(p_ref, g_ref, b_ref, w_ref, bs_ref, da_ref, dp_ref, dg_ref, db_ref, dw_ref, dbs_ref):
        zu, zv, u, xh, rs, vn = _gmlp_parts(p_ref[...].astype(f32), g_ref[...], b_ref[...])
        causal = _tri(GM_CHUNK, True) > 0
        sub = lax.broadcasted_iota(jnp.int32, (8, GM_CHUNK), 0)
        ones = jnp.ones((8, HEAD), f32)
        dout = da_ref[...].astype(f32)

        @pl.when(pl.program_id(0) == 0)
        def _():
            for r in (dg_ref, db_ref, dw_ref, dbs_ref):
                r[...] = jnp.zeros_like(r)

        du, dvn, dbs = [], [], jnp.zeros((8, GM_CHUNK), f32)
        for gi in range(N_HEAD):
            sl = slice(gi * HEAD, (gi + 1) * HEAD)
            w = jnp.where(causal, w_ref[gi], 0.0)
            du_g, dvn_g, dw_g = [], [], jnp.zeros((GM_CHUNK, GM_CHUNK), f32)
            for ch in range(nch):
                rs_ = slice(ch * GM_CHUNK, (ch + 1) * GM_CHUNK)
                mixed = _raw_dot(w, vn[rs_, sl], "nn") + bs_ref[:, gi:gi + 1]
                du_g.append(dout[rs_, sl] * mixed)
                dm = dout[rs_, sl] * u[rs_, sl]
                dbs = dbs + jnp.where(sub == gi, _sel_dot(ones, dm, "nt"), 0.0)
                dw_g = dw_g + _raw_dot(dm, vn[rs_, sl], "nt")
                dvn_g.append(_raw_dot(w, dm, "tn"))
            dw_ref[gi] += jnp.where(causal, dw_g, 0.0)
            du.append(jnp.concatenate(du_g, axis=0))
            dvn.append(jnp.concatenate(dvn_g, axis=0))
        dbs_ref[...] += dbs
        du = jnp.concatenate(du, axis=-1)
        dvn = jnp.concatenate(dvn, axis=-1)
        dg_ref[...] += jnp.sum(dvn * xh, axis=0, keepdims=True)
        db_ref[...] += jnp.sum(dvn, axis=0, keepdims=True)
        dxh = dvn * g_ref[...]
        dv = rs * (dxh - jnp.mean(dxh, axis=-1, keepdims=True) - xh * jnp.mean(dxh * xh, axis=-1, keepdims=True))
        dp_ref[:, :512] = _egrad(jax.nn.gelu, zu, du).astype(bf16)
        dp_ref[:, 512:] = _egrad(jax.nn.gelu, zv, dv).astype(bf16)

    vec = pl.BlockSpec((1, 512), lambda i: (0, 0))
    wsp = pl.BlockSpec((N_HEAD, GM_CHUNK, GM_CHUNK), lambda i: (0, 0, 0))
    return _call(body, "gmlp_bwd", (T // rows,),
                 [pl.BlockSpec((rows, 1024), lambda i: (i, 0)), vec, vec, wsp, pl.BlockSpec((GM_CHUNK, 128), lambda i: (0, 0)),
                  pl.BlockSpec((rows, 512), lambda i: (i, 0))],
                 (pl.BlockSpec((rows, 1024), lambda i: (i, 0)), vec, vec, wsp, pl.BlockSpec((8, GM_CHUNK), lambda i: (0, 0))),
                 (jax.ShapeDtypeStruct((T, 1024), bf16), jax.ShapeDtypeStruct((1, 512), f32), jax.ShapeDtypeStruct((1, 512), f32),
                  jax.ShapeDtypeStruct((N_HEAD, GM_CHUNK, GM_CHUNK), f32), jax.ShapeDtypeStruct((8, GM_CHUNK), f32)),
                 sem=("arbitrary",))(proj, ln_g, ln_b, w_s, b_st, da)


HG_SUB = 8
HG_NSUB = HG_CHUNK // HG_SUB


def _two_level_matrix(transposed=False):
    shape = (HG_CHUNK, 2 * HG_CHUNK) if transposed else (2 * HG_CHUNK, HG_CHUNK)
    r = lax.broadcasted_iota(jnp.int32, shape, 1 if transposed else 0)
    c = lax.broadcasted_iota(jnp.int32, shape, 0 if transposed else 1)
    t = jnp.where(r < HG_CHUNK, r, r - HG_CHUNK)
    local = (r < HG_CHUNK) & (t // HG_SUB == c // HG_SUB) & (c <= t)
    before = (r >= HG_CHUNK) & (c < (t // HG_SUB) * HG_SUB)
    return (local | before).astype(f32)


def _two_level_sums(x):
    two = _sel_dot(_two_level_matrix(), x, "nn")
    return two[:HG_CHUNK], two[HG_CHUNK:]


@jax.custom_vjp
def _two_level_cumsum(x):
    return _two_level_sums(x)


_two_level_cumsum.defvjp(
    lambda x: (_two_level_sums(x), None),
    lambda _, g: (_sel_dot(_two_level_matrix(), jnp.concatenate(g, axis=0), "tn"),))


def _tile_matrix():
    s = lax.broadcasted_iota(jnp.int32, (HG_SUB, HG_CHUNK), 0)
    j = lax.broadcasted_iota(jnp.int32, (HG_SUB, HG_CHUNK), 1)
    return (j % HG_SUB == s).astype(f32)


@jax.custom_vjp
def _tile_lanes(x):
    return _sel_dot(_tile_matrix(), x, "nn", x_first=True, pieces=1)


_tile_lanes.defvjp(
    lambda x: (_sel_dot(_tile_matrix(), x, "nn", x_first=True, pieces=1), None),
    lambda _, g: (_sel_dot(_tile_matrix(), g, "nt", x_first=True, pieces=2),))


def _block_rows(x):
    k = x.shape[-1]
    return jnp.broadcast_to(x.reshape(HG_NSUB, 1, HG_SUB, k), (HG_NSUB, HG_SUB, HG_SUB, k)).reshape(HG_CHUNK, HG_SUB, k)


def _hgrn_chunk(st0, q_raw, f_raw, i_raw, g_raw, l0, l1, ng):
    C, SUB = HG_CHUNK, HG_SUB
    lb = jax.nn.sigmoid(l0 - l1)
    fg = lb + (1.0 - lb) * jax.nn.sigmoid(f_raw)
    kk = 1.0 - fg
    qf = jax.nn.silu(q_raw)
    al, base = _two_level_cumsum(jnp.log(fg))
    a = al + base
    row = lax.broadcasted_iota(jnp.int32, (C, HEAD), 0)
    a_last = jnp.sum(jnp.where(row == C - 1, a, 0.0), axis=0, keepdims=True)
    inter = _dot_nt(qf * jnp.exp(a), st0)
    qt = qf * jnp.exp(al)
    rb = lax.broadcasted_iota(jnp.int32, (C, C), 0) // SUB
    cb = lax.broadcasted_iota(jnp.int32, (C, C), 1) // SUB
    scores = jnp.zeros((C, C), f32)
    for i in range(1, HG_NSUB):
        base_i = jnp.sum(jnp.where(row == i * SUB, base, 0.0), axis=0, keepdims=True)
        kt = kk * jnp.exp(jnp.minimum(base_i - a, 0.0))
        scores = scores + jnp.where((rb == i) & (cb < i), _dot_nt(qt, kt), 0.0)
    t_i = lax.broadcasted_iota(jnp.int32, (C, SUB, HEAD), 0) % SUB
    s_i = lax.broadcasted_iota(jnp.int32, (C, SUB, HEAD), 1)
    decay = jnp.exp(jnp.where(s_i <= t_i, al[:, None, :] - _block_rows(al), -jnp.inf))
    diag = jnp.sum(qf[:, None, :] * decay * _block_rows(kk), axis=-1)
    scores = scores + jnp.where(rb == cb, _tile_lanes(diag), 0.0)
    o = inter + _dot_nn(scores, i_raw)
    st1 = jnp.exp(a_last) * st0 + _dot_tn(i_raw, kk * jnp.exp(a_last - a))
    on = o * lax.rsqrt(jnp.mean(o * o, axis=-1, keepdims=True) + EPS) * ng
    return st1, on * jax.nn.silu(g_raw)


def _hgrn_specs(S, Bl, rev):
    N = S // HG_CHUNK
    chunk = (lambda n: N - 1 - n) if rev else (lambda n: n)
    col = lambda c0: pl.BlockSpec((Bl, HG_CHUNK, 512), lambda n: (0, chunk(n), c0 // 512))
    st = pl.BlockSpec((Bl, N_HEAD, 1, HEAD, HEAD), lambda n: (0, 0, chunk(n), 0, 0))
    full = lambda *s: pl.BlockSpec(s, functools.partial(lambda n, nd: (0,) * nd, nd=len(s)))
    return N, col, st, full


def _hgrn_fwd(proj, lb_logits, ng, Bl, S):
    N, col, st, full = _hgrn_specs(S, Bl, False)

    def body(q_ref, f_ref, i_ref, g_ref, l_ref, ng_ref, o_ref, st_ref, state):
        @pl.when(pl.program_id(0) == 0)
        def _():
            state[...] = jnp.zeros_like(state)

        for b in range(Bl):
            for h in range(N_HEAD):
                sl = slice(h * HEAD, (h + 1) * HEAD)
                st0 = state[b, h]
                st_ref[b, h, 0] = st0
                st1, out = _hgrn_chunk(st0, *[r[b, :, sl].astype(f32) for r in (q_ref, f_ref, i_ref, g_ref)],
                                       l_ref[0:1, sl], l_ref[1:2, sl], ng_ref[...])
                state[b, h] = st1
                o_ref[b, :, sl] = out.astype(bf16)

    return _call(body, "hgrn_fwd", (N,), [col(C_HQ), col(C_HF), col(C_HI), col(C_HG), full(2, 512), full(1, HEAD)],
                 (col(0), st),
                 (jax.ShapeDtypeStruct((Bl, S, 512), bf16), jax.ShapeDtypeStruct((Bl, N_HEAD, N, HEAD, HEAD), f32)),
                 scratch=[pltpu.VMEM((Bl, N_HEAD, HEAD, HEAD), f32)], sem=("arbitrary",))(
        proj, proj, proj, proj, lb_logits, ng)


def _hgrn_bwd(proj, lb_logits, ng, states, db, dzuv, dxq, dgl, Bl, S):
    N, col, st, full = _hgrn_specs(S, Bl, True)
    rows = lambda width: pl.BlockSpec((Bl, HG_CHUNK, width), lambda n: (0, N - 1 - n, 0))

    def body(q_ref, f_ref, i_ref, g_ref, l_ref, ng_ref, st_ref, db_ref, dzuv_ref, dxq_ref, dgl_ref,
             dp_ref, dl_ref, dng_ref, dstate):
        @pl.when(pl.program_id(0) == 0)
        def _():
            dstate[...] = jnp.zeros_like(dstate)
            dl_ref[...] = jnp.zeros_like(dl_ref)
            dng_ref[...] = jnp.zeros_like(dng_ref)

        dp_ref[:, :, C_ZU:C_HQ] = dzuv_ref[...]
        dp_ref[:, :, C_XQ:C_GL] = dxq_ref[...]
        for n in range(3):
            dp_ref[:, :, C_GL + n * D_MODEL:C_GL + (n + 1) * D_MODEL] = dgl_ref[n]
        dq_ref, df_ref, di_ref, dg_ref = [dp_ref.at[:, :, c0:c0 + 512] for c0 in (C_HQ, C_HF, C_HI, C_HG)]
        for b in range(Bl):
            for h in range(N_HEAD):
                sl = slice(h * HEAD, (h + 1) * HEAD)
                _, vjp = jax.vjp(_hgrn_chunk, st_ref[b, h, 0], *[r[b, :, sl].astype(f32) for r in (q_ref, f_ref, i_ref, g_ref)],
                                 l_ref[0:1, sl], l_ref[1:2, sl], ng_ref[...])
                dst0, dq, df, di, dg, dl0, dl1, dng = vjp((dstate[b, h], db_ref[b, :, sl].astype(f32)))
                dstate[b, h] = dst0
                dq_ref[b, :, sl] = dq.astype(bf16)
                df_ref[b, :, sl] = df.astype(bf16)
                di_ref[b, :, sl] = di.astype(bf16)
                dg_ref[b, :, sl] = dg.astype(bf16)
                dl_ref[0:1, sl] += dl0
                dl_ref[1:2, sl] += dl1
                dng_ref[b, h] += dng

    return _call(body, "hgrn_bwd", (N,),
                 [col(C_HQ), col(C_HF), col(C_HI), col(C_HG), full(2, 512), full(1, HEAD), st, col(0), rows(C_HQ - C_ZU),
                  rows(C_GL - C_XQ), pl.BlockSpec((3, Bl, HG_CHUNK, D_MODEL), lambda n: (0, 0, N - 1 - n, 0))],
                 (rows(IN_WIDTH), full(2, 512), full(Bl, N_HEAD, 1, HEAD)),
                 (jax.ShapeDtypeStruct((Bl, S, IN_WIDTH), bf16), jax.ShapeDtypeStruct((2, 512), f32),
                  jax.ShapeDtypeStruct((Bl, N_HEAD, 1, HEAD), f32)),
                 scratch=[pltpu.VMEM((Bl, N_HEAD, HEAD, HEAD), f32)], sem=("arbitrary",))(
        proj, proj, proj, proj, lb_logits, ng, states, db, dzuv, dxq, dgl)


def _attn_probs(q, k):
    s = _raw_dot(q, k, "nt") * (HEAD ** -0.5)
    e = jnp.exp(s - jnp.max(s, axis=-1, keepdims=True))
    return e / jnp.sum(e, axis=-1, keepdims=True)


def _attn_specs(S, tq):
    nq = S // tq
    q = pl.BlockSpec((tq, 512), lambda b, i: (b * nq + i, C_XQ // 512))
    kv = pl.BlockSpec((1, MEM_LEN, 1024), lambda b, i: (b, 0, 0))
    o = pl.BlockSpec((tq, 512), lambda b, i: (b * nq + i, 0))
    return nq, q, kv, o


def _attn_fwd(proj, kv, Bl, S):
    tq = _pick(S, (512, 256, 128))
    nq, qs, kvs, os_ = _attn_specs(S, tq)

    def body(q_ref, kv_ref, o_ref):
        for h in range(N_HEAD):
            sl = slice(h * HEAD, (h + 1) * HEAD)
            p = _attn_probs(q_ref[:, sl], kv_ref[0, :, sl])
            o_ref[:, sl] = _raw_dot(p, kv_ref[0, :, 512 + h * HEAD:512 + (h + 1) * HEAD], "nn").astype(bf16)

    return _call(body, "attn_fwd", (Bl, nq), [qs, kvs], os_, jax.ShapeDtypeStruct((Bl * S, 512), bf16),
                 sem=("parallel", "parallel"))(proj, kv)


def _attn_bwd(proj, kv, dc, Bl, S):
    tq = _pick(S, (512, 256, 128))
    nq, qs, kvs, os_ = _attn_specs(S, tq)

    def body(q_ref, kv_ref, do_ref, dq_ref, dkv_ref):
        @pl.when(pl.program_id(1) == 0)
        def _():
            dkv_ref[...] = jnp.zeros_like(dkv_ref)

        for h in range(N_HEAD):
            sl = slice(h * HEAD, (h + 1) * HEAD)
            vsl = slice(512 + h * HEAD, 512 + (h + 1) * HEAD)
            q, k, v, do = q_ref[:, sl], kv_ref[0, :, sl], kv_ref[0, :, vsl], do_ref[:, sl]
            p = _attn_probs(q, k)
            dkv_ref[0, :, vsl] += _raw_dot(p, do, "tn")
            dp = _raw_dot(do, v, "nt")
            ds = p * (dp - jnp.sum(dp * p, axis=-1, keepdims=True)) * (HEAD ** -0.5)
            dq_ref[:, sl] = _raw_dot(ds, k, "nn").astype(bf16)
            dkv_ref[0, :, sl] += _raw_dot(ds, q, "tn")

    return _call(body, "attn_bwd", (Bl, nq), [qs, kvs, os_], (os_, kvs),
                 (jax.ShapeDtypeStruct((Bl * S, 512), bf16), jax.ShapeDtypeStruct((Bl, MEM_LEN, 1024), f32)),
                 sem=("arbitrary", "arbitrary"))(proj, kv, dc)


def _gate_specs(tm):
    half = D_MODEL // 2
    return [pl.BlockSpec((tm, half), functools.partial(lambda i, c: (i, c), c=(C_GL + n * D_MODEL) // half + k))
            for n in range(3) for k in range(2)]


def _merge_out_norm_fwd(branches, wb, proj, w_out, x, g):
    T = proj.shape[0]
    tm = _pick(T, (512, 256, 128))

    def body(a_ref, b_ref, c_ref, w0, w1, w2, g0a, g0b, g1a, g1b, g2a, g2b, wo_ref, x_ref, g_ref, m_ref, x1_ref, h_ref):
        acc = jnp.zeros((tm, D_MODEL), f32)
        for x_n, w_ref, ga, gb in ((a_ref, w0, g0a, g0b), (b_ref, w1, g1a, g1b), (c_ref, w2, g2a, g2b)):
            gate = jax.nn.sigmoid(jnp.concatenate([ga[...], gb[...]], axis=-1).astype(f32))
            acc = acc + gate * _raw_dot(x_n[...], w_ref[...], "nn")
        merged = acc.astype(bf16)
        m_ref[...] = merged
        x1 = x_ref[...] + _raw_dot(merged, wo_ref[...], "nn")
        x1_ref[...] = x1
        y = x1 * lax.rsqrt(jnp.mean(x1 * x1, axis=-1, keepdims=True) + EPS) * g_ref[...]
        h_ref[...] = y.astype(bf16)

    br = pl.BlockSpec((tm, 512), lambda i: (i, 0))
    w = pl.BlockSpec((512, D_MODEL), lambda i: (0, 0))
    row = pl.BlockSpec((tm, D_MODEL), lambda i: (i, 0))
    return _call(body, "merge_out_norm_fwd", (T // tm,),
                 [br, br, br, w, w, w, *_gate_specs(tm), pl.BlockSpec((D_MODEL, D_MODEL), lambda i: (0, 0)), row,
                  pl.BlockSpec((1, D_MODEL), lambda i: (0, 0))],
                 (row, row, row),
                 (jax.ShapeDtypeStruct((T, D_MODEL), bf16), jax.ShapeDtypeStruct((T, D_MODEL), f32),
                  jax.ShapeDtypeStruct((T, D_MODEL), bf16)),
                 sem=("parallel",))(*branches, *wb, *[proj] * 6, w_out, x, g)


def _merge_bwd(branches, wb, proj, merged, dx1, w_out):
    T = proj.shape[0]
    tm = _pick(T, (256, 128))

    def body(a_ref, b_ref, c_ref, w0, w1, w2, g0a, g0b, g1a, g1b, g2a, g2b, m_ref, dx_ref, wo_ref, dgl_ref, d0, d1, d2, gw_ref, gwo_ref):
        @pl.when(pl.program_id(0) == 0)
        def _():
            gw_ref[...] = jnp.zeros_like(gw_ref)
            gwo_ref[...] = jnp.zeros_like(gwo_ref)

        dx = dx_ref[...].astype(bf16)
        gwo_ref[...] += _raw_dot(m_ref[...], dx, "tn")
        dm = _raw_dot(dx, wo_ref[...], "nt")
        for n, (x_ref, w_ref, ga, gb, d_ref) in enumerate(((a_ref, w0, g0a, g0b, d0), (b_ref, w1, g1a, g1b, d1), (c_ref, w2, g2a, g2b, d2))):
            x, w = x_ref[...], w_ref[...]
            up = _raw_dot(x, w, "nn")
            sg = jax.nn.sigmoid(jnp.concatenate([ga[...], gb[...]], axis=-1).astype(f32))
            dgl_ref[n] = (dm * up * sg * (1.0 - sg)).astype(bf16)
            dup = (dm * sg).astype(bf16)
            d_ref[...] = _raw_dot(dup, w, "nt").astype(bf16)
            gw_ref[n] += _raw_dot(x, dup, "tn")

    br = pl.BlockSpec((tm, 512), lambda i: (i, 0))
    w = pl.BlockSpec((512, D_MODEL), lambda i: (0, 0))
    sh = jax.ShapeDtypeStruct((T, 512), bf16)
    row = pl.BlockSpec((tm, D_MODEL), lambda i: (i, 0))
    square = pl.BlockSpec((D_MODEL, D_MODEL), lambda i: (0, 0))
    outs = _call(body, "merge_bwd", (T // tm,), [br, br, br, w, w, w, *_gate_specs(tm), row, row, square],
                 (pl.BlockSpec((3, tm, D_MODEL), lambda i: (0, i, 0)), br, br, br, pl.BlockSpec((3, 512, D_MODEL), lambda i: (0, 0, 0)), square),
                 (jax.ShapeDtypeStruct((3, T, D_MODEL), bf16), sh, sh, sh, jax.ShapeDtypeStruct((3, 512, D_MODEL), f32),
                  jax.ShapeDtypeStruct((D_MODEL, D_MODEL), f32)),
                 sem=("arbitrary",))(*branches, *wb, *[proj] * 6, merged, dx1, w_out)
    return outs[0], outs[1:4], outs[4], outs[5]


CONV_TC = 256


def _shift_down(a, k):
    r = pltpu.roll(a, k, 0)
    row = lax.broadcasted_iota(jnp.int32, (8, a.shape[1]), 0)
    return jnp.concatenate([jnp.where(row >= k, r[:8], 0.0), r[8:]], axis=0)


def _shift_up(a, k):
    n = a.shape[0]
    r = pltpu.roll(a, n - k, 0)
    row = lax.broadcasted_iota(jnp.int32, (8, a.shape[1]), 0)
    return jnp.concatenate([r[:n - 8], jnp.where(row < 8 - k, r[n - 8:], 0.0)], axis=0)


def _conv_pre(a, a1, a2, cw, cb):
    return cb + cw[0:1] * a2 + cw[1:2] * a1 + cw[2:3] * a


def _up_conv_fwd(h2, w_up_t, cw, cb):
    Bl, S, Dd = h2.shape
    nc = D_FF // CONV_TC

    def body(h_ref, wa_ref, wb_ref, cw_ref, cb_ref, a_ref, b_ref, o_ref):
        a16 = _raw_dot(h_ref[0], wa_ref[...], "nt").astype(bf16)
        b16 = _raw_dot(h_ref[0], wb_ref[...], "nt").astype(bf16)
        a_ref[0], b_ref[0] = a16, b16
        a = a16.astype(f32)
        ac = _conv_pre(a, _shift_down(a, 1), _shift_down(a, 2), cw_ref[...], cb_ref[...])
        o_ref[0] = (jax.nn.silu(ac) * b16.astype(f32)).astype(bf16)

    seq = pl.BlockSpec((1, S, CONV_TC), lambda b, c: (b, 0, c))
    sh = jax.ShapeDtypeStruct((Bl, S, D_FF), bf16)
    return _call(body, "up_conv_fwd", (Bl, nc),
                 [pl.BlockSpec((1, S, Dd), lambda b, c: (b, 0, 0)), pl.BlockSpec((CONV_TC, Dd), lambda b, c: (c, 0)),
                  pl.BlockSpec((CONV_TC, Dd), lambda b, c: (nc + c, 0)), pl.BlockSpec((3, CONV_TC), lambda b, c: (0, c)),
                  pl.BlockSpec((1, CONV_TC), lambda b, c: (0, c))],
                 (seq, seq, seq), (sh, sh, sh), sem=("parallel", "parallel"))(h2, w_up_t, w_up_t, cw, cb)


def _down_conv_bwd(dx2, w_down, a, b, cw, cb):
    Bl, S, Dd = dx2.shape
    nc = D_FF // CONV_TC

    def body(dx_ref, wd_ref, a_ref, b_ref, cw_ref, cb_ref, da_ref, db_ref, dcw_ref, dcb_ref):
        dact = _raw_dot(dx_ref[0], wd_ref[...], "nt").astype(bf16).astype(f32)
        a, cw = a_ref[0].astype(f32), cw_ref[...]
        a1, a2 = _shift_down(a, 1), _shift_down(a, 2)
        ac = _conv_pre(a, a1, a2, cw, cb_ref[...])
        sg = jax.nn.sigmoid(ac)
        gated = dact * sg
        db_ref[0] = (gated * ac).astype(bf16)
        dac = gated * b_ref[0].astype(f32) * (1.0 + ac * (1.0 - sg))
        da_ref[0] = (cw[2:3] * dac + cw[1:2] * _shift_up(dac, 1) + cw[0:1] * _shift_up(dac, 2)).astype(bf16)
        dcw_ref[0, 0:1, :] = jnp.sum(dac * a2, axis=0, keepdims=True)
        dcw_ref[0, 1:2, :] = jnp.sum(dac * a1, axis=0, keepdims=True)
        dcw_ref[0, 2:3, :] = jnp.sum(dac * a, axis=0, keepdims=True)
        dcb_ref[0] = jnp.sum(dac, axis=0, keepdims=True)

    seq = pl.BlockSpec((1, S, CONV_TC), lambda b_, c: (b_, 0, c))
    sh = jax.ShapeDtypeStruct((Bl, S, D_FF), bf16)
    return _call(body, "down_conv_bwd", (Bl, nc),
                 [pl.BlockSpec((1, S, Dd), lambda b_, c: (b_, 0, 0)), pl.BlockSpec((CONV_TC, Dd), lambda b_, c: (c, 0)), seq, seq,
                  pl.BlockSpec((3, CONV_TC), lambda b_, c: (0, c)), pl.BlockSpec((1, CONV_TC), lambda b_, c: (0, c))],
                 (seq, seq, pl.BlockSpec((1, 3, CONV_TC), lambda b_, c: (b_, 0, c)), pl.BlockSpec((1, 1, CONV_TC), lambda b_, c: (b_, 0, c))),
                 (sh, sh, jax.ShapeDtypeStruct((Bl, 3, D_FF), f32), jax.ShapeDtypeStruct((Bl, 1, D_FF), f32)),
                 sem=("parallel", "parallel"))(dx2, w_down, a, b, cw, cb)


def _local_step(x, mem, target, p, w_in_t, late_b, late_c, send):
    Bl, S, Dd = x.shape
    T = Bl * S
    x2d, t2d, mem2d = x.reshape(T, Dd), target.reshape(T, Dd), mem.reshape(Bl * MEM_LEN, Dd)
    b_st = jnp.pad(p["b_spatial"].T, ((0, 0), (0, 128 - N_HEAD)))
    lbl = p["lb_logits"]

    proj, h = _norm_proj_fwd(x2d, p["norm1_g"], w_in_t, 1024, 1664)
    a_out = _gmlp_fwd(proj, p["ln_v_g"], p["ln_v_b"], p["w_spatial"], b_st)
    proj3 = proj.reshape(Bl, S, IN_WIDTH)
    b_out, states = _hgrn_fwd(proj3, lbl, p["hgrn_norm_g"], Bl, S)
    b_out = b_out.reshape(T, 512)
    memn = _rms_fwd(mem2d, p["mem_norm_g"], "memnorm_fwd")
    w = late_b(b_out)
    wb = w["w_branch"]
    kv = _mm(memn, w["w_mem_kv"], "nn", f32, "kv_fwd", 512, 1024).reshape(Bl, MEM_LEN, 2 * 512)
    c_out = _attn_fwd(proj, kv, Bl, S)
    branches = (a_out, b_out, c_out)
    merged, x1, h2 = _merge_out_norm_fwd(branches, wb, proj, w["w_out"], x2d, p["norm2_g"])
    w.update(late_c(h2))
    ffn_a, ffn_b, act = _up_conv_fwd(h2.reshape(Bl, S, Dd), w["w_up_t"], w["conv_w"], p["conv_b"])
    act = act.reshape(T, D_FF)
    loss_part, dx2, dx2_16, g_final = _down_final_loss(act, w["w_down"], x1, p["final_g"], t2d)

    g_w_down = _mm(act, dx2_16, "tn", bf16, "down_dw", 1408, 1024, 1024)
    da, db, g_conv_w, g_conv_b = _down_conv_bwd(dx2_16.reshape(Bl, S, Dd), w["w_down"], ffn_a, ffn_b, w["conv_w"], p["conv_b"])
    da, db = da.reshape(T, D_FF), db.reshape(T, D_FF)
    g_w_up_t = _mm(da, h2, "tn", bf16, "up_dw_a", 1408, 1024, 1024, into=(lax.empty((2 * D_FF, D_MODEL), bf16), 0))
    g_w_up_t = _mm(db, h2, "tn", bf16, "up_dw_b", 1408, 1024, 1024, into=(g_w_up_t, D_FF))
    send("c", dict(w_up=g_w_up_t, conv_w=jnp.sum(g_conv_w, axis=0), w_down=g_w_down))
    dx1, g_norm2 = _mm_rms_bwd([(da, w["w_up_t"], 0), (db, w["w_up_t"], 1)], x1, p["norm2_g"], dx2, "up_dx_norm2_bwd", 256)

    dgl, dbr, g_w_branch, g_w_out = _merge_bwd(branches, wb, proj, merged, dx1, w["w_out"])
    dxq, dkv = _attn_bwd(proj, kv, dbr[2], Bl, S)
    dkv = dkv.reshape(Bl * MEM_LEN, 2 * 512)
    g_w_kv = _mm(memn, dkv, "tn", bf16, "kv_dw", 1024, 1024, 512)
    send("b", dict(w_mem_kv=g_w_kv, w_branch=g_w_branch, w_out=g_w_out))
    dzuv, g_ln_g, g_ln_b, g_w_sp, g_b_sp = _gmlp_bwd(proj, p["ln_v_g"], p["ln_v_b"], p["w_spatial"], b_st, dbr[0])
    dproj, g_lbl, g_ng = _hgrn_bwd(proj3, lbl, p["hgrn_norm_g"], states, dbr[1].reshape(Bl, S, 512), dzuv.reshape(Bl, S, -1),
                                   dxq.reshape(Bl, S, -1), dgl.reshape(3, Bl, S, Dd), Bl, S)
    dproj = dproj.reshape(T, IN_WIDTH)
    quarter = Dd // 4
    send("a0", dict(w_in_part=_mm(dproj, h, "tn", bf16, "proj_dw_0", 512, quarter, n_tiles=(0, 1))))
    g_rest = _mm(dproj, h, "tn", bf16, "proj_dw_1", 512, quarter, n_tiles=(1, 3))
    dkv, g_rest = lax.optimization_barrier((dkv, g_rest))
    send("a1", dict(w_in_part=g_rest))
    dmemn = _mm(dkv, w["w_mem_kv"], "nt", f32, "kv_dx", 512, 1024)
    _, g_mem_norm = _rms_bwd(mem2d, p["mem_norm_g"], dmemn, "memnorm_bwd")
    dx, g_norm1 = _mm_rms_bwd([(dproj, w_in_t, 0)], x2d, p["norm1_g"], dx1, "proj_dx_norm1_bwd", 256)

    gs = dict(w_spatial=g_w_sp, norm1_g=g_norm1, mem_norm_g=g_mem_norm, norm2_g=g_norm2, final_g=g_final, lb_logits=g_lbl,
              ln_v_g=g_ln_g, ln_v_b=g_ln_b, b_spatial=g_b_sp, hgrn_norm_g=g_ng, conv_b=g_conv_b)
    return loss_part, dx.reshape(Bl, S, Dd), gs


def _coords():
    return lax.axis_index("x"), lax.axis_index("y"), lax.axis_index("c")


def _slot(dev):
    return 4 * dev[0] + 2 * dev[1] + dev[2]


def _comm_call(body, name, arrays, out_shapes, n_sem):
    n = len(arrays)
    hbm = pl.BlockSpec(memory_space=pl.ANY)
    return pl.pallas_call(
        body, name=name, out_shape=out_shapes, in_specs=[hbm] * n, out_specs=[hbm] * n,
        scratch_shapes=[pltpu.SemaphoreType.DMA((n_sem, n)), pltpu.SemaphoreType.DMA((n_sem, n)), pltpu.SemaphoreType.DMA((n,))])(*arrays)


def _all_gather(blocks, name):
    n = len(blocks)

    def body(*refs):
        x_refs, o_refs, (send_sems, recv_sems, local_sems) = refs[:n], refs[n:2 * n], refs[2 * n:]
        x, y, c = _coords()
        me, sibling = (x, y, c), (x, y, 1 - c)
        chips = [(1 - x, y), (x, 1 - y), (1 - x, 1 - y)]

        def copy(a, k, block_dev, to, from_input=False):
            dst = o_refs[a].at[_slot(block_dev)]
            return pltpu.make_async_remote_copy(src_ref=x_refs[a] if from_input else dst, dst_ref=dst, send_sem=send_sems.at[k, a],
                                                recv_sem=recv_sems.at[k, a], device_id=to, device_id_type=MESH)

        mine = [pltpu.make_async_copy(x_refs[a], o_refs[a].at[_slot(me)], local_sems.at[a]) for a in range(n)]
        first = [copy(a, 0, me, sibling, True) for a in range(n)]
        first += [copy(a, 1 + j, me, (*chip, c), True) for j, chip in enumerate(chips) for a in range(n)]
        for cp in mine + first:
            cp.start()
        passed = []
        for j, chip in enumerate(chips):
            for a in range(n):
                copy(a, 1 + j, (*chip, c), me).wait_recv()
                fwd = copy(a, 4 + j, (*chip, c), sibling)
                fwd.start()
                passed.append(fwd)
        for a in range(n):
            copy(a, 0, sibling, me).wait_recv()
        for j, chip in enumerate(chips):
            for a in range(n):
                copy(a, 4 + j, (*chip, 1 - c), me).wait_recv()
        for cp in first + passed:
            cp.wait_send()
        for cp in mine:
            cp.wait()

    return _comm_call(body, name, blocks, [jax.ShapeDtypeStruct((N_DEV,) + b.shape, b.dtype) for b in blocks], 7)


_REL = [(0, 0, 1), (0, 1, 0), (0, 1, 1), (1, 0, 0), (1, 0, 1), (1, 1, 0), (1, 1, 1)]


def _seq_exchange(arrays, gather, name, collective_id):
    n = len(arrays)
    hbm = pltpu.MemorySpace.HBM
    srcs = [jax.new_ref(a, memory_space=hbm) for a in arrays]
    lands = [jax.empty_ref(jax.ShapeDtypeStruct(((N_DEV,) + a.shape) if gather else a.shape, a.dtype), memory_space=hbm) for a in arrays]

    @pl.kernel(mesh=plsc.ScalarSubcoreMesh(axis_name="sequencer", num_cores=1), name=name,
               scratch_types=(pltpu.SemaphoreType.DMA((7, n)), pltpu.SemaphoreType.DMA((7, n)), pltpu.SemaphoreType.DMA((n,))),
               compiler_params=pltpu.CompilerParams(collective_id=collective_id))
    def launch(send, recv, local):
        x, y, c = _coords()
        me = (x, y, c)
        peers = [(x ^ dx, y ^ dy, c ^ dc) for dx, dy, dc in _REL]
        barrier = pltpu.get_barrier_semaphore()
        for peer in peers:
            pl.semaphore_signal(barrier, inc=1, device_id=peer, device_id_type=MESH)
        pl.semaphore_wait(barrier, len(peers))

        def copy(a, k, peer, arrival):
            return pltpu.make_async_remote_copy(
                src_ref=srcs[a] if gather else srcs[a].at[_slot(peer)], dst_ref=lands[a].at[_slot(peer if arrival else me)],
                send_sem=send.at[k, a], recv_sem=recv.at[k, a], device_id=peer, device_id_type=MESH)

        mine = [pltpu.make_async_copy(srcs[a] if gather else srcs[a].at[_slot(me)], lands[a].at[_slot(me)], local.at[a])
                for a in range(n)]
        out = [copy(a, k, peer, False) for a in range(n) for k, peer in enumerate(peers)]
        for cp in mine + out:
            cp.start()
        for a in range(n):
            for k, peer in enumerate(peers):
                copy(a, k, peer, True).wait_recv()
        for cp in out:
            cp.wait_send()
        for cp in mine:
            cp.wait()

    launch()
    return [land[...] for land in lands]


def _adam_math(w, g, m, v):
    m_ = ADAM_B1 * m + (1.0 - ADAM_B1) * g
    v_ = ADAM_B2 * v + (1.0 - ADAM_B2) * jnp.square(g)
    m_hat = m_ / (1.0 - ADAM_B1 ** ADAM_STEP)
    v_hat = v_ / (1.0 - ADAM_B2 ** ADAM_STEP)
    return -ADAM_LR * (m_hat / (jnp.sqrt(v_hat) + ADAM_EPS) + ADAM_WD * w), m_, v_


def _reduce_adamw(parts, w, m, v, name):
    R, L = w.shape
    tr = _pick(R, (256, 208, 176, 128, 64, 32, 16, 8))
    n = len(parts)

    def body(*refs):
        w_ref, m_ref, v_ref, g_ref, d_ref, nm_ref, nv_ref = refs[n:]
        pieces = []
        for p_ref in refs[:n]:
            g = p_ref[0].astype(f32)
            for i in range(1, N_DEV):
                g = g + p_ref[i].astype(f32)
            pieces.append(g)
        g = pieces[0] if n == 1 else jnp.concatenate(pieces, axis=-1)
        g_ref[...] = g
        d_ref[...], nm_ref[...], nv_ref[...] = _adam_math(w_ref[...], g, m_ref[...], v_ref[...])

    blk = pl.BlockSpec((tr, L), lambda i: (i, 0))
    sh = jax.ShapeDtypeStruct((R, L), f32)
    return _call(body, name, (R // tr,), [pl.BlockSpec((N_DEV, tr, q.shape[2]), lambda i: (0, i, 0)) for q in parts] + [blk, blk, blk],
                 (blk,) * 4, (sh,) * 4, sem=("parallel",))(*parts, w, m, v)


SMALL = (("w_spatial", (512, 128), 0), ("norm1_g", (1, 1024), 512), ("mem_norm_g", (1, 1024), 520), ("norm2_g", (1, 1024), 528),
         ("final_g", (1, 1024), 536), ("lb_logits", (2, 512), 544), ("ln_v_g", (1, 512), 552), ("ln_v_b", (1, 512), 556),
         ("b_spatial", (4, 128), 560), ("hgrn_norm_g", (1, 128), 564), ("conv_b", (1, 2816), 565))
LOSS_ROW, SMALL_USED, SMALL_ROWS = 587, 588, 640


def _segments(shape, base):
    r, n = shape
    per = n // 128
    return [(base + i * per + j, i, slice(j * 128, (j + 1) * 128)) for i in range(r) for j in range(per)]


def _pack_small(gs, loss_part):
    names = [n for n, _, _ in SMALL]

    def body(*refs):
        src, loss_ref, o_ref = dict(zip(names, refs[:-2])), refs[-2], refs[-1]
        o_ref[SMALL_USED:SMALL_ROWS, :] = jnp.zeros((SMALL_ROWS - SMALL_USED, 128), f32)
        o_ref[LOSS_ROW:LOSS_ROW + 1, :] = loss_ref[...]
        for name, shape, base in SMALL:
            ref = src[name]
            if name == "w_spatial":
                o_ref[base:base + 512, :] = ref[...].reshape(512, 128)
            elif name == "b_spatial":
                o_ref[base:base + 4, :] = ref[0:4, :]
            elif name == "conv_b":
                per_example = functools.reduce(lambda u, v_: u + v_, [ref[b] for b in range(ref.shape[0])])
                for row, i, sl in _segments(shape, base):
                    o_ref[row:row + 1, :] = per_example[i:i + 1, sl]
            elif name == "hgrn_norm_g":
                per_head = [ref[b, h] for b in range(ref.shape[0]) for h in range(N_HEAD)]
                o_ref[base:base + 1, :] = functools.reduce(lambda u, v_: u + v_, per_head)
            else:
                for row, i, sl in _segments(shape, base):
                    o_ref[row:row + 1, :] = ref[i:i + 1, sl]

    return pl.pallas_call(body, name="pack_small", out_shape=jax.ShapeDtypeStruct((SMALL_ROWS, 128), f32))(
        *[gs[n] for n in names], loss_part)


def _small_update(gathered, w, m, v):
    names = [n for n, _, _ in SMALL]
    k = len(names)

    def body(*refs):
        p_ref = refs[0]
        ins = [dict(zip(names, refs[1 + i * k:1 + (i + 1) * k])) for i in range(3)]
        outs = [dict(zip(names, refs[1 + (3 + i) * k:1 + (4 + i) * k])) for i in range(4)]
        loss_ref, gsum = refs[-2], refs[-1]
        g = p_ref[0]
        for i in range(1, N_DEV):
            g = g + p_ref[i]
        gsum[...] = g
        loss_ref[...] = gsum[LOSS_ROW:LOSS_ROW + 1, :]
        for name, shape, base in SMALL:
            if name == "w_spatial":
                where = [(slice(base, base + 512), (slice(None), slice(None)))]
            else:
                where = [(slice(row, row + 1), (slice(i, i + 1), sl)) for row, i, sl in _segments(shape, base)]
            for rows, at in where:
                g_ = gsum[rows, :]
                d_, m_, v_ = _adam_math(ins[0][name][at], g_, ins[1][name][at], ins[2][name][at])
                for o, val in zip(outs, (g_, d_, m_, v_)):
                    o[name][at] = val

    args = [gathered] + [d[n] for d in (w, m, v) for n in names]
    out_shapes = [jax.ShapeDtypeStruct(shape, f32) for _ in range(4) for _, shape, _ in SMALL] + [jax.ShapeDtypeStruct((1, 128), f32)]
    outs = pl.pallas_call(body, name="small_update", out_shape=out_shapes, scratch_shapes=[pltpu.VMEM((SMALL_ROWS, 128), f32)])(*args)
    return [dict(zip(names, outs[i * k:(i + 1) * k])) for i in range(4)], outs[-1]


def _cols_full(g):
    return jnp.moveaxis(g, 0, -2).reshape(g.shape[1:-1] + (N_DEV * g.shape[-1],))


def _cols_parts(full):
    n = full.shape[-1] // N_DEV
    return jnp.moveaxis(full.reshape(full.shape[:-1] + (N_DEV, n)), -2, 0)


def kernel(x, mem, norm1_g, w_in, ln_v_g, ln_v_b, w_spatial, b_spatial, lb_logits, hgrn_norm_g, mem_norm_g, w_mem_kv, w_branch, w_out, norm2_g, w_up, conv_w, conv_b, w_down, final_g, loss_target, m_norm1_g, m_w_in, m_ln_v_g, m_ln_v_b, m_w_spatial, m_b_spatial, m_lb_logits, m_hgrn_norm_g, m_mem_norm_g, m_w_mem_kv, m_w_branch, m_w_out, m_norm2_g, m_w_up, m_conv_w, m_conv_b, m_w_down, m_final_g, v_norm1_g, v_w_in, v_ln_v_g, v_ln_v_b, v_w_spatial, v_b_spatial, v_lb_logits, v_hgrn_norm_g, v_mem_norm_g, v_w_mem_kv, v_w_branch, v_w_out, v_norm2_g, v_w_up, v_conv_w, v_conv_b, v_w_down, v_final_g):
    given = dict(locals())
    order = ("norm1_g", "w_in", "ln_v_g", "ln_v_b", "w_spatial", "b_spatial", "lb_logits", "hgrn_norm_g", "mem_norm_g",
             "w_mem_kv", "w_branch", "w_out", "norm2_g", "w_up", "conv_w", "conv_b", "w_down", "final_g")
    groups = dict(a=("w_in",), b=("w_mem_kv", "w_branch", "w_out"), c=("w_up", "conv_w", "w_down"))

    by_rows = ("w_in", "w_up")
    shard_of = lambda n, prefix="": jnp.swapaxes(given[prefix + n][0], 0, 1) if n in by_rows else given[prefix + n][0]
    wire = {n: shard_of(n).astype(f32 if n == "conv_w" else bf16) for ns in groups.values() for n in ns}
    w_in_full = _all_gather([wire["w_in"]], "gather_w_in")[0].reshape(IN_WIDTH, D_MODEL)
    w_in_full, wire_b, wire_c = lax.optimization_barrier((w_in_full, [wire[n] for n in groups["b"]], [wire[n] for n in groups["c"]]))
    rest_b = _seq_exchange(wire_b, True, "gather_b", 1)
    rest_c = _seq_exchange(wire_c, True, "gather_c", 6)

    def late_b(after):
        _, (kv_, br_, out_) = lax.optimization_barrier((after, tuple(rest_b)))
        br_ = _cols_full(br_)
        return dict(w_mem_kv=kv_.reshape(D_MODEL, 2 * 512), w_branch=[br_[n] for n in range(3)], w_out=out_.reshape(D_MODEL, D_MODEL))

    def late_c(after):
        _, (up_, cw_, down_) = lax.optimization_barrier((after, tuple(rest_c)))
        return dict(w_up_t=up_.reshape(2 * D_FF, D_MODEL), conv_w=_cols_full(cw_), w_down=down_.reshape(D_FF, D_MODEL))

    to_parts = dict(w_in_part=lambda g_: g_.reshape(N_DEV, -1, g_.shape[-1]), w_up=lambda g_: g_.reshape(N_DEV, -1, D_MODEL), conv_w=_cols_parts,
                    w_branch=lambda g_: _cols_parts(g_.astype(bf16)).reshape(N_DEV, -1, 128),
                    w_mem_kv=lambda g_: g_.reshape(N_DEV, -1, 2 * 512), w_out=lambda g_: g_.astype(bf16).reshape(N_DEV, -1, D_MODEL),
                    w_down=lambda g_: g_.reshape(N_DEV, -1, D_MODEL))
    scatters = {}


    def send(tag, grads_):
        parts = [to_parts[n](g_) for n, g_ in grads_.items()]
        scatters[tag] = _seq_exchange(parts, False, f"scatter_{tag}", dict(a0=2, a1=7, b=4, c=5)[tag])

    small_2d = lambda prefix: {n: given[prefix + n].reshape(shape) for n, shape, _ in SMALL}
    p = small_2d("")
    p["w_spatial"] = w_spatial[0]
    updates = {}

    def update(tag):
        arrived = [scatters["a0"] + scatters["a1"]] if tag == "a" else [[parts] for parts in scatters[tag]]
        for n, parts in zip(groups[tag], arrived):
            state = [shard_of(n, pre) for pre in ("", "m_", "v_")]
            res = _reduce_adamw(parts, *[a.reshape(-1, a.shape[-1]) for a in state], "adamw_" + n)
            updates[n] = [jnp.swapaxes(r, 0, 1) for r in res] if n in by_rows else res

    loss_part, grad_x, gs = _local_step(x, mem, loss_target, p, w_in_full, late_b, late_c, send)

    packed, (scatters["c"], scatters["b"]) = lax.optimization_barrier((_pack_small(gs, loss_part), (scatters["c"], scatters["b"])))
    gathered = _seq_exchange([packed], True, "gather_small", 3)[0]

    update("c")
    update("b")
    update("a")
    grads, delta, new_m, new_v = {}, {}, {}, {}
    for n, res in updates.items():
        grads[n], delta[n], new_m[n], new_v[n] = [r.reshape(given[n].shape) for r in res]

    small_results, loss_row = _small_update(gathered, small_2d(""), small_2d("m_"), small_2d("v_"))
    for dst, res in zip((grads, delta, new_m, new_v), small_results):
        for n, _, _ in SMALL:
            dst[n] = res[n].reshape(given[n].shape)
    loss = loss_row[0, 0]

    return (loss, grad_x, *[grads[n] for n in order], *[delta[n] for n in order], *[new_m[n] for n in order],
            *[new_v[n] for n in order])
```

```python
import functools

import jax
import jax.numpy as jnp
from jax import lax
from jax.experimental import pallas as pl
from jax.experimental.pallas import tpu as pltpu
from jax.experimental.pallas import tpu_sc as plsc

f32 = jnp.float32
bf16 = jnp.bfloat16

N_DEV = 8
D_MODEL = 1024
EPS = 1e-6
GM_CHUNK = 128
HG_CHUNK = 64
HEAD = 128
N_HEAD = 4
MEM_LEN = 256
D_FF = 2816
IN_WIDTH = 6656
C_ZU, C_HQ, C_HF, C_HI, C_HG, C_XQ, C_GL = 0, 1024, 1536, 2048, 2560, 3072, 3584
ADAM_LR, ADAM_B1, ADAM_B2, ADAM_EPS, ADAM_WD, ADAM_STEP = 0.001, 0.9, 0.999, 1e-08, 0.01, 10
VMEM_LIMIT = 56 * 1024 * 1024
MESH = pl.DeviceIdType.MESH


def _pick(n, cands):
    for c in cands:
        if n % c == 0:
            return c
    return n


def _call(body, name, grid, in_specs, out_specs, out_shape, scratch=(), sem=None, **cp):
    params = dict(vmem_limit_bytes=VMEM_LIMIT, **cp)
    if sem is not None:
        params["dimension_semantics"] = sem
    return pl.pallas_call(
        body, name=name, grid=grid, in_specs=in_specs, out_specs=out_specs, out_shape=out_shape,
        scratch_shapes=list(scratch), compiler_params=pltpu.CompilerParams(**params))


_DN = {"nn": (((1,), (0,)), ((), ())), "nt": (((1,), (1,)), ((), ())), "tn": (((0,), (0,)), ((), ()))}


def _raw_dot(a, b, mode):
    return lax.dot_general(a.astype(bf16), b.astype(bf16), _DN[mode], preferred_element_type=f32)


@jax.custom_vjp
def _dot_nn(a, b):
    return _raw_dot(a, b, "nn")


_dot_nn.defvjp(lambda a, b: (_raw_dot(a, b, "nn"), (a, b)),
               lambda r, g: (_raw_dot(g, r[1], "nt"), _raw_dot(r[0], g, "tn")))


@jax.custom_vjp
def _dot_nt(a, b):
    return _raw_dot(a, b, "nt")


_dot_nt.defvjp(lambda a, b: (_raw_dot(a, b, "nt"), (a, b)),
               lambda r, g: (_raw_dot(g, r[1], "nn"), _raw_dot(g, r[0], "tn")))


@jax.custom_vjp
def _dot_tn(a, b):
    return _raw_dot(a, b, "tn")


_dot_tn.defvjp(lambda a, b: (_raw_dot(a, b, "tn"), (a, b)),
               lambda r, g: (_raw_dot(r[1], g, "nt"), _raw_dot(r[0], g, "nn")))


def _tri(n, lower):
    r = lax.broadcasted_iota(jnp.int32, (n, n), 0)
    c = lax.broadcasted_iota(jnp.int32, (n, n), 1)
    return ((c <= r) if lower else (c >= r)).astype(f32)


def _sel_dot(sel, x, mode, x_first=False, pieces=3):
    sel = sel.astype(bf16)
    out, rest = None, x
    for p in range(pieces):
        piece = rest.astype(bf16)
        part = lax.dot_general(*((piece, sel) if x_first else (sel, piece)), _DN[mode], preferred_element_type=f32)
        out = part if out is None else out + part
        if p + 1 < pieces:
            rest = rest - piece.astype(f32)
    return out


def _egrad(fn, x, ct):
    return jax.vjp(fn, x)[1](ct)[0]


def _mm(a, b, mode, out_dtype, name, tm, tn, tk=None, residual=None, into=None, n_tiles=None):
    if mode == "nn":
        (M, K), (_, N) = a.shape, b.shape
    elif mode == "nt":
        (M, K), (N, _) = a.shape, b.shape
    else:
        (K, M), (_, N) = a.shape, b.shape
    j0 = 0
    if n_tiles is not None:
        assert mode != "nt" and N % tn == 0 and residual is None
        j0, N = n_tiles[0], n_tiles[1] * tn
    tm, tn = min(tm, M), min(tn, N)
    tk = K if tk is None else min(tk, K)
    assert M % tm == 0 and N % tn == 0 and K % tk == 0, (name, M, N, K, tm, tn, tk)
    nk = K // tk

    def body(*refs):
        acc_ref = refs[-1] if nk > 1 else None
        refs = refs[:-1] if nk > 1 else refs
        if residual is None:
            a_ref, b_ref, *_, o_ref = refs
        else:
            a_ref, b_ref, r_ref, o_ref = refs

        def finish(r):
            if residual is not None:
                r = r + r_ref[...]
            o_ref[...] = r.astype(out_dtype)

        part = _raw_dot(a_ref[...], b_ref[...], mode)
        if nk == 1:
            finish(part)
            return
        k = pl.program_id(2)

        @pl.when(k == 0)
        def _():
            acc_ref[...] = part

        @pl.when((k > 0) & (k < nk - 1))
        def _():
            acc_ref[...] += part

        @pl.when(k == nk - 1)
        def _():
            finish(acc_ref[...] + part)

    a_spec = {"nn": pl.BlockSpec((tm, tk), lambda i, j, k: (i, k)),
              "nt": pl.BlockSpec((tm, tk), lambda i, j, k: (i, k)),
              "tn": pl.BlockSpec((tk, tm), lambda i, j, k: (k, i))}[mode]
    b_spec = {"nn": pl.BlockSpec((tk, tn), lambda i, j, k: (k, j + j0)),
              "nt": pl.BlockSpec((tn, tk), lambda i, j, k: (j, k)),
              "tn": pl.BlockSpec((tk, tn), lambda i, j, k: (k, j + j0))}[mode]
    o_spec = pl.BlockSpec((tm, tn), lambda i, j, k: (i, j))
    in_specs = [a_spec, b_spec] + ([o_spec] if residual is not None else [])
    args = (a, b) + ((residual,) if residual is not None else ())
    out_shape = jax.ShapeDtypeStruct((M, N), out_dtype)
    extra = {}
    if into is not None:
        assert residual is None and into[1] % tm == 0
        o_spec = pl.BlockSpec((tm, tn), lambda i, j, k: (i + into[1] // tm, j))
        out_shape = jax.ShapeDtypeStruct(into[0].shape, out_dtype)
        in_specs, args = in_specs + [pl.BlockSpec(memory_space=pl.ANY)], args + (into[0],)
        extra = dict(input_output_aliases={2: 0})
    return pl.pallas_call(
        body, name=name, grid=(M // tm, N // tn, nk), in_specs=in_specs, out_specs=o_spec, out_shape=out_shape,
        scratch_shapes=[pltpu.VMEM((tm, tn), f32)] if nk > 1 else [],
        compiler_params=pltpu.CompilerParams(vmem_limit_bytes=VMEM_LIMIT, dimension_semantics=("parallel", "parallel", "arbitrary")),
        **extra)(*args)


def _rms_fwd(x, g, name):
    R, Dd = x.shape
    tr = _pick(R, (512, 256, 128))

    def body(x_ref, g_ref, o_ref):
        xf = x_ref[...]
        o_ref[...] = (xf * lax.rsqrt(jnp.mean(xf * xf, axis=-1, keepdims=True) + EPS) * g_ref[...]).astype(bf16)

    row = pl.BlockSpec((tr, Dd), lambda i: (i, 0))
    return _call(body, name, (R // tr,), [row, pl.BlockSpec((1, Dd), lambda i: (0, 0))], row, jax.ShapeDtypeStruct((R, Dd), bf16),
                 sem=("parallel",))(x, g)


def _norm_proj_fwd(x, g, w_t, tm, tn):
    R, Dd = x.shape
    N = w_t.shape[0]
    tm = min(tm, R)
    assert R % tm == 0 and N % tn == 0

    def body(x_ref, g_ref, w_ref, p_ref, h_ref):
        @pl.when(pl.program_id(1) == 0)
        def _():
            xf = x_ref[...]
            h_ref[...] = (xf * lax.rsqrt(jnp.mean(xf * xf, axis=-1, keepdims=True) + EPS) * g_ref[...]).astype(bf16)

        p_ref[...] = _raw_dot(h_ref[...], w_ref[...], "nt").astype(bf16)

    row = pl.BlockSpec((tm, Dd), lambda i, j: (i, 0))
    return _call(body, "norm1_proj_fwd", (R // tm, N // tn),
                 [row, pl.BlockSpec((1, Dd), lambda i, j: (0, 0)), pl.BlockSpec((tn, Dd), lambda i, j: (j, 0))],
                 (pl.BlockSpec((tm, tn), lambda i, j: (i, j)), row),
                 (jax.ShapeDtypeStruct((R, N), bf16), jax.ShapeDtypeStruct((R, Dd), bf16)), sem=("parallel", "arbitrary"))(x, g, w_t)


def _rms_bwd(x, g, dh, name, residual=None):
    R, Dd = x.shape
    tr = _pick(R, (512, 256, 128))

    def body(*refs):
        if residual is None:
            x_ref, g_ref, dh_ref, dx_ref, dg_ref = refs
        else:
            x_ref, g_ref, dh_ref, r_ref, dx_ref, dg_ref = refs
        xf = x_ref[...]
        rs = lax.rsqrt(jnp.mean(xf * xf, axis=-1, keepdims=True) + EPS)
        y = xf * rs
        dh_ = dh_ref[...].astype(f32)
        dy = dh_ * g_ref[...]
        dx = rs * (dy - y * jnp.mean(dy * y, axis=-1, keepdims=True))
        if residual is not None:
            dx = dx + r_ref[...]
        dx_ref[...] = dx

        @pl.when(pl.program_id(0) == 0)
        def _():
            dg_ref[...] = jnp.zeros_like(dg_ref)

        dg_ref[...] += jnp.sum(dh_ * y, axis=0, keepdims=True)

    row = pl.BlockSpec((tr, Dd), lambda i: (i, 0))
    vec = pl.BlockSpec((1, Dd), lambda i: (0, 0))
    in_specs = [row, vec, row] + ([row] if residual is not None else [])
    args = (x, g, dh) + ((residual,) if residual is not None else ())
    return _call(body, name, (R // tr,), in_specs, (row, vec),
                 (jax.ShapeDtypeStruct((R, Dd), f32), jax.ShapeDtypeStruct((1, Dd), f32)), sem=("arbitrary",))(*args)


def _mm_rms_bwd(pairs, x, g, residual, name, tm):
    M = x.shape[0]
    Dd = x.shape[1]
    tm = min(tm, M)
    n = len(pairs)

    def body(*refs):
        ab_refs, (x_ref, g_ref, r_ref, dx_ref, dg_ref) = refs[:2 * n], refs[2 * n:]
        dh_ = _raw_dot(ab_refs[0][...], ab_refs[1][...], "nn")
        for k in range(1, n):
            dh_ = dh_ + _raw_dot(ab_refs[2 * k][...], ab_refs[2 * k + 1][...], "nn")
        xf = x_ref[...]
        rs = lax.rsqrt(jnp.mean(xf * xf, axis=-1, keepdims=True) + EPS)
        y = xf * rs
        dy = dh_ * g_ref[...]
        dx_ref[...] = rs * (dy - y * jnp.mean(dy * y, axis=-1, keepdims=True)) + r_ref[...]

        @pl.when(pl.program_id(0) == 0)
        def _():
            dg_ref[...] = jnp.zeros_like(dg_ref)

        dg_ref[...] += jnp.sum(dh_ * y, axis=0, keepdims=True)

    row = pl.BlockSpec((tm, Dd), lambda i: (i, 0))
    vec = pl.BlockSpec((1, Dd), lambda i: (0, 0))
    in_specs, args = [], []
    for a, b, k in pairs:
        in_specs += [pl.BlockSpec((tm, a.shape[1]), lambda i: (i, 0)),
                     pl.BlockSpec((a.shape[1], b.shape[1]), functools.partial(lambda i, k_: (k_, 0), k_=k),
                                  pipeline_mode=pl.Buffered(1))]
        args += [a, b]
    in_specs += [row, vec, row]
    args += [x, g, residual]
    return _call(body, name, (M // tm,), in_specs, (row, vec),
                 (jax.ShapeDtypeStruct((M, Dd), f32), jax.ShapeDtypeStruct((1, Dd), f32)), sem=("arbitrary",))(*args)


def _down_final_loss(act, w_down, x1, g, target):
    R, Dd = x1.shape
    tr = _pick(R, (512, 256, 128))

    def body(a_ref, w_ref, x1_ref, g_ref, t_ref, loss_ref, dx_ref, dxb_ref, dg_ref):
        xf = _raw_dot(a_ref[...], w_ref[...], "nn") + x1_ref[...]
        rs = lax.rsqrt(jnp.mean(xf * xf, axis=-1, keepdims=True) + EPS)
        y = xf * rs
        err = y * g_ref[...] - t_ref[...]
        dh_ = err * (1.0 / Dd)
        dy = dh_ * g_ref[...]
        dx = rs * (dy - y * jnp.mean(dy * y, axis=-1, keepdims=True))
        dx_ref[...] = dx
        dxb_ref[...] = dx.astype(bf16)

        @pl.when(pl.program_id(0) == 0)
        def _():
            dg_ref[...] = jnp.zeros_like(dg_ref)
            loss_ref[...] = jnp.zeros_like(loss_ref)

        dg_ref[...] += jnp.sum(dh_ * y, axis=0, keepdims=True)
        part = jnp.sum(jnp.mean(err * err, axis=-1, keepdims=True), axis=0, keepdims=True)
        loss_ref[...] += 0.5 * part

    row = pl.BlockSpec((tr, Dd), lambda i: (i, 0))
    vec = pl.BlockSpec((1, Dd), lambda i: (0, 0))
    in_specs = [pl.BlockSpec((tr, act.shape[1]), lambda i: (i, 0)), pl.BlockSpec(w_down.shape, lambda i: (0, 0)), row, vec, row]
    return _call(body, "down_final_loss", (R // tr,), in_specs, (pl.BlockSpec((1, 128), lambda i: (0, 0)), row, row, vec),
                 (jax.ShapeDtypeStruct((1, 128), f32), jax.ShapeDtypeStruct((R, Dd), f32), jax.ShapeDtypeStruct((R, Dd), bf16),
                  jax.ShapeDtypeStruct((1, Dd), f32)), sem=("arbitrary",))(act, w_down, x1, g, target)


def _gmlp_parts(zuv, ln_g, ln_b):
    zu, zv = zuv[:, :512], zuv[:, 512:]
    u = jax.nn.gelu(zu)
    v = jax.nn.gelu(zv)
    mu = jnp.mean(v, axis=-1, keepdims=True)
    rs = lax.rsqrt(jnp.mean(jnp.square(v - mu), axis=-1, keepdims=True) + EPS)
    xh = (v - mu) * rs
    return zu, zv, u, xh, rs, xh * ln_g + ln_b


GM_TILE_CHUNKS = 4


def _gmlp_tile(T):
    n = _pick(T // GM_CHUNK, (GM_TILE_CHUNKS, 2, 1))
    return n, n * GM_CHUNK


def _gmlp_fwd(proj, ln_g, ln_b, w_s, b_st):
    T = proj.shape[0]
    nch, rows = _gmlp_tile(T)

    def body(p_ref, g_ref, b_ref, w_ref, bs_ref, o_ref):
        _, _, u, _, _, vn = _gmlp_parts(p_ref[...].astype(f32), g_ref[...], b_ref[...])
        causal = _tri(GM_CHUNK, True) > 0
        for gi in range(N_HEAD):
            sl = slice(gi * HEAD, (gi + 1) * HEAD)
            w = jnp.where(causal, w_ref[gi], 0.0)
            for ch in range(nch):
                rs_ = slice(ch * GM_CHUNK, (ch + 1) * GM_CHUNK)
                mixed = _raw_dot(w, vn[rs_, sl], "nn") + bs_ref[:, gi:gi + 1]
                o_ref[rs_, sl] = (u[rs_, sl] * mixed).astype(bf16)

    vec = pl.BlockSpec((1, 512), lambda i: (0, 0))
    return _call(body, "gmlp_fwd", (T // rows,),
                 [pl.BlockSpec((rows, 1024), lambda i: (i, 0)), vec, vec,
                  pl.BlockSpec((N_HEAD, GM_CHUNK, GM_CHUNK), lambda i: (0, 0, 0)), pl.BlockSpec((GM_CHUNK, 128), lambda i: (0, 0))],
                 pl.BlockSpec((rows, 512), lambda i: (i, 0)), jax.ShapeDtypeStruct((T, 512), bf16), sem=("parallel",))(
        proj, ln_g, ln_b, w_s, b_st)


def _gmlp_bwd(proj, ln_g, ln_b, w_s, b_st, da):
    T = proj.shape[0]
    nch, rows = _gmlp_tile(T)

    def body(p_ref, g_ref, b_ref, w_ref, bs_ref, da_ref, dp_ref, dg_ref, db_ref, dw_ref, dbs_ref):
        zu, zv, u, xh, rs, vn = _gmlp_parts(p_ref[...].astype(f32), g_ref[...], b_ref[...])
        causal = _tri(GM_CHUNK, True) > 0
        sub = lax.broadcasted_iota(jnp.int32, (8, GM_CHUNK), 0)
        ones = jnp.ones((8, HEAD), f32)
        dout = da_ref[...].astype(f32)

        @pl.when(pl.program_id(0) == 0)
        def _():
            for r in (dg_ref, db_ref, dw_ref, dbs_ref):
                r[...] = jnp.zeros_like(r)

        du, dvn, dbs = [], [], jnp.zeros((8, GM_CHUNK), f32)
        for gi in range(N_HEAD):
            sl = slice(gi * HEAD, (gi + 1) * HEAD)
            w = jnp.where(causal, w_ref[gi], 0.0)
            du_g, dvn_g, dw_g = [], [], jnp.zeros((GM_CHUNK, GM_CHUNK), f32)
            for ch in range(nch):
                rs_ = slice(ch * GM_CHUNK, (ch + 1) * GM_CHUNK)
                mixed = _raw_dot(w, vn[rs_, sl], "nn") + bs_ref[:, gi:gi + 1]
                du_g.append(dout[rs_, sl] * mixed)
                dm = dout[rs_, sl] * u[rs_, sl]
                dbs = dbs + jnp.where(sub == gi, _sel_dot(ones, dm, "nt"), 0.0)
                dw_g = dw_g + _raw_dot(dm, vn[rs_, sl], "nt")
                dvn_g.append(_raw_dot(w, dm, "tn"))
            dw_ref[gi] += jnp.where(causal, dw_g, 0.0)
            du.append(jnp.concatenate(du_g, axis=0))
            dvn.append(jnp.concatenate(dvn_g, axis=0))
        dbs_ref[...] += dbs
        du = jnp.concatenate(du, axis=-1)
        dvn = jnp.concatenate(dvn, axis=-1)
        dg_ref[...] += jnp.sum(dvn * xh, axis=0, keepdims=True)
        db_ref[...] += jnp.sum(dvn, axis=0, keepdims=True)
        dxh = dvn * g_ref[...]
        dv = rs * (dxh - jnp.mean(dxh, axis=-1, keepdims=True) - xh * jnp.mean(dxh * xh, axis=-1, keepdims=True))
        dp_ref[:, :512] = _egrad(jax.nn.gelu, zu, du).astype(bf16)
        dp_ref[:, 512:] = _egrad(jax.nn.gelu, zv, dv).astype(bf16)

    vec = pl.BlockSpec((1, 512), lambda i: (0, 0))
    wsp = pl.BlockSpec((N_HEAD, GM_CHUNK, GM_CHUNK), lambda i: (0, 0, 0))
    return _call(body, "gmlp_bwd", (T // rows,),
                 [pl.BlockSpec((rows, 1024), lambda i: (i, 0)), vec, vec, wsp, pl.BlockSpec((GM_CHUNK, 128), lambda i: (0, 0)),
                  pl.BlockSpec((rows, 512), lambda i: (i, 0))],
                 (pl.BlockSpec((rows, 1024), lambda i: (i, 0)), vec, vec, wsp, pl.BlockSpec((8, GM_CHUNK), lambda i: (0, 0))),
                 (jax.ShapeDtypeStruct((T, 1024), bf16), jax.ShapeDtypeStruct((1, 512), f32), jax.ShapeDtypeStruct((1, 512), f32),
                  jax.ShapeDtypeStruct((N_HEAD, GM_CHUNK, GM_CHUNK), f32), jax.ShapeDtypeStruct((8, GM_CHUNK), f32)),
                 sem=("arbitrary",))(proj, ln_g, ln_b, w_s, b_st, da)


HG_SUB = 8
HG_NSUB = HG_CHUNK // HG_SUB


def _two_level_matrix(transposed=False):
    shape = (HG_CHUNK, 2 * HG_CHUNK) if transposed else (2 * HG_CHUNK, HG_CHUNK)
    r = lax.broadcasted_iota(jnp.int32, shape, 1 if transposed else 0)
    c = lax.broadcasted_iota(jnp.int32, shape, 0 if transposed else 1)
    t = jnp.where(r < HG_CHUNK, r, r - HG_CHUNK)
    local = (r < HG_CHUNK) & (t // HG_SUB == c // HG_SUB) & (c <= t)
    before = (r >= HG_CHUNK) & (c < (t // HG_SUB) * HG_SUB)
    return (local | before).astype(f32)


def _two_level_sums(x):
    two = _sel_dot(_two_level_matrix(), x, "nn")
    return two[:HG_CHUNK], two[HG_CHUNK:]


@jax.custom_vjp
def _two_level_cumsum(x):
    return _two_level_sums(x)


_two_level_cumsum.defvjp(
    lambda x: (_two_level_sums(x), None),
    lambda _, g: (_sel_dot(_two_level_matrix(), jnp.concatenate(g, axis=0), "tn"),))


def _tile_matrix():
    s = lax.broadcasted_iota(jnp.int32, (HG_SUB, HG_CHUNK), 0)
    j = lax.broadcasted_iota(jnp.int32, (HG_SUB, HG_CHUNK), 1)
    return (j % HG_SUB == s).astype(f32)


@jax.custom_vjp
def _tile_lanes(x):
    return _sel_dot(_tile_matrix(), x, "nn", x_first=True, pieces=1)


_tile_lanes.defvjp(
    lambda x: (_sel_dot(_tile_matrix(), x, "nn", x_first=True, pieces=1), None),
    lambda _, g: (_sel_dot(_tile_matrix(), g, "nt", x_first=True, pieces=2),))


def _block_rows(x):
    k = x.shape[-1]
    return jnp.broadcast_to(x.reshape(HG_NSUB, 1, HG_SUB, k), (HG_NSUB, HG_SUB, HG_SUB, k)).reshape(HG_CHUNK, HG_SUB, k)


def _hgrn_chunk(st0, q_raw, f_raw, i_raw, g_raw, l0, l1, ng):
    C, SUB = HG_CHUNK, HG_SUB
    lb = jax.nn.sigmoid(l0 - l1)
    fg = lb + (1.0 - lb) * jax.nn.sigmoid(f_raw)
    kk = 1.0 - fg
    qf = jax.nn.silu(q_raw)
    al, base = _two_level_cumsum(jnp.log(fg))
    a = al + base
    row = lax.broadcasted_iota(jnp.int32, (C, HEAD), 0)
    a_last = jnp.sum(jnp.where(row == C - 1, a, 0.0), axis=0, keepdims=True)
    inter = _dot_nt(qf * jnp.exp(a), st0)
    qt = qf * jnp.exp(al)
    rb = lax.broadcasted_iota(jnp.int32, (C, C), 0) // SUB
    cb = lax.broadcasted_iota(jnp.int32, (C, C), 1) // SUB
    scores = jnp.zeros((C, C), f32)
    for i in range(1, HG_NSUB):
        base_i = jnp.sum(jnp.where(row == i * SUB, base, 0.0), axis=0, keepdims=True)
        kt = kk * jnp.exp(jnp.minimum(base_i - a, 0.0))
        scores = scores + jnp.where((rb == i) & (cb < i), _dot_nt(qt, kt), 0.0)
    t_i = lax.broadcasted_iota(jnp.int32, (C, SUB, HEAD), 0) % SUB
    s_i = lax.broadcasted_iota(jnp.int32, (C, SUB, HEAD), 1)
    decay = jnp.exp(jnp.where(s_i <= t_i, al[:, None, :] - _block_rows(al), -jnp.inf))
    diag = jnp.sum(qf[:, None, :] * decay * _block_rows(kk), axis=-1)
    scores = scores + jnp.where(rb == cb, _tile_lanes(diag), 0.0)
    o = inter + _dot_nn(scores, i_raw)
    st1 = jnp.exp(a_last) * st0 + _dot_tn(i_raw, kk * jnp.exp(a_last - a))
    on = o * lax.rsqrt(jnp.mean(o * o, axis=-1, keepdims=True) + EPS) * ng
    return st1, on * jax.nn.silu(g_raw)


def _hgrn_specs(S, Bl, rev):
    N = S // HG_CHUNK
    chunk = (lambda n: N - 1 - n) if rev else (lambda n: n)
    col = lambda c0: pl.BlockSpec((Bl, HG_CHUNK, 512), lambda n: (0, chunk(n), c0 // 512))
    st = pl.BlockSpec((Bl, N_HEAD, 1, HEAD, HEAD), lambda n: (0, 0, chunk(n), 0, 0))
    full = lambda *s: pl.BlockSpec(s, functools.partial(lambda n, nd: (0,) * nd, nd=len(s)))
    return N, col, st, full


def _hgrn_fwd(proj, lb_logits, ng, Bl, S):
    N, col, st, full = _hgrn_specs(S, Bl, False)

    def body(q_ref, f_ref, i_ref, g_ref, l_ref, ng_ref, o_ref, st_ref, state):
        @pl.when(pl.program_id(0) == 0)
        def _():
            state[...] = jnp.zeros_like(state)

        for b in range(Bl):
            for h in range(N_HEAD):
                sl = slice(h * HEAD, (h + 1) * HEAD)
                st0 = state[b, h]
                st_ref[b, h, 0] = st0
                st1, out = _hgrn_chunk(st0, *[r[b, :, sl].astype(f32) for r in (q_ref, f_ref, i_ref, g_ref)],
                                       l_ref[0:1, sl], l_ref[1:2, sl], ng_ref[...])
                state[b, h] = st1
                o_ref[b, :, sl] = out.astype(bf16)

    return _call(body, "hgrn_fwd", (N,), [col(C_HQ), col(C_HF), col(C_HI), col(C_HG), full(2, 512), full(1, HEAD)],
                 (col(0), st),
                 (jax.ShapeDtypeStruct((Bl, S, 512), bf16), jax.ShapeDtypeStruct((Bl, N_HEAD, N, HEAD, HEAD), f32)),
                 scratch=[pltpu.VMEM((Bl, N_HEAD, HEAD, HEAD), f32)], sem=("arbitrary",))(
        proj, proj, proj, proj, lb_logits, ng)


def _hgrn_bwd(proj, lb_logits, ng, states, db, dzuv, dxq, dgl, Bl, S):
    N, col, st, full = _hgrn_specs(S, Bl, True)
    rows = lambda width: pl.BlockSpec((Bl, HG_CHUNK, width), lambda n: (0, N - 1 - n, 0))

    def body(q_ref, f_ref, i_ref, g_ref, l_ref, ng_ref, st_ref, db_ref, dzuv_ref, dxq_ref, dgl_ref,
             dp_ref, dl_ref, dng_ref, dstate):
        @pl.when(pl.program_id(0) == 0)
        def _():
            dstate[...] = jnp.zeros_like(dstate)
            dl_ref[...] = jnp.zeros_like(dl_ref)
            dng_ref[...] = jnp.zeros_like(dng_ref)

        dp_ref[:, :, C_ZU:C_HQ] = dzuv_ref[...]
        dp_ref[:, :, C_XQ:C_GL] = dxq_ref[...]
        for n in range(3):
            dp_ref[:, :, C_GL + n * D_MODEL:C_GL + (n + 1) * D_MODEL] = dgl_ref[n]
        dq_ref, df_ref, di_ref, dg_ref = [dp_ref.at[:, :, c0:c0 + 512] for c0 in (C_HQ, C_HF, C_HI, C_HG)]
        for b in range(Bl):
            for h in range(N_HEAD):
                sl = slice(h * HEAD, (h + 1) * HEAD)
                _, vjp = jax.vjp(_hgrn_chunk, st_ref[b, h, 0], *[r[b, :, sl].astype(f32) for r in (q_ref, f_ref, i_ref, g_ref)],
                                 l_ref[0:1, sl], l_ref[1:2, sl], ng_ref[...])
                dst0, dq, df, di, dg, dl0, dl1, dng = vjp((dstate[b, h], db_ref[b, :, sl].astype(f32)))
                dstate[b, h] = dst0
                dq_ref[b, :, sl] = dq.astype(bf16)
                df_ref[b, :, sl] = df.astype(bf16)
                di_ref[b, :, sl] = di.astype(bf16)
                dg_ref[b, :, sl] = dg.astype(bf16)
                dl_ref[0:1, sl] += dl0
                dl_ref[1:2, sl] += dl1
                dng_ref[b, h] += dng

    return _call(body, "hgrn_bwd", (N,),
                 [col(C_HQ), col(C_HF), col(C_HI), col(C_HG), full(2, 512), full(1, HEAD), st, col(0), rows(C_HQ - C_ZU),
                  rows(C_GL - C_XQ), pl.BlockSpec((3, Bl, HG_CHUNK, D_MODEL), lambda n: (0, 0, N - 1 - n, 0))],
                 (rows(IN_WIDTH), full(2, 512), full(Bl, N_HEAD, 1, HEAD)),
                 (jax.ShapeDtypeStruct((Bl, S, IN_WIDTH), bf16), jax.ShapeDtypeStruct((2, 512), f32),
                  jax.ShapeDtypeStruct((Bl, N_HEAD, 1, HEAD), f32)),
                 scratch=[pltpu.VMEM((Bl, N_HEAD, HEAD, HEAD), f32)], sem=("arbitrary",))(
        proj, proj, proj, proj, lb_logits, ng, states, db, dzuv, dxq, dgl)


def _attn_probs(q, k):
    s = _raw_dot(q, k, "nt") * (HEAD ** -0.5)
    e = jnp.exp(s - jnp.max(s, axis=-1, keepdims=True))
    return e / jnp.sum(e, axis=-1, keepdims=True)


def _attn_specs(S, tq):
    nq = S // tq
    q = pl.BlockSpec((tq, 512), lambda b, i: (b * nq + i, C_XQ // 512))
    kv = pl.BlockSpec((1, MEM_LEN, 1024), lambda b, i: (b, 0, 0))
    o = pl.BlockSpec((tq, 512), lambda b, i: (b * nq + i, 0))
    return nq, q, kv, o


def _attn_fwd(proj, kv, Bl, S):
    tq = _pick(S, (512, 256, 128))
    nq, qs, kvs, os_ = _attn_specs(S, tq)

    def body(q_ref, kv_ref, o_ref):
        for h in range(N_HEAD):
            sl = slice(h * HEAD, (h + 1) * HEAD)
            p = _attn_probs(q_ref[:, sl], kv_ref[0, :, sl])
            o_ref[:, sl] = _raw_dot(p, kv_ref[0, :, 512 + h * HEAD:512 + (h + 1) * HEAD], "nn").astype(bf16)

    return _call(body, "attn_fwd", (Bl, nq), [qs, kvs], os_, jax.ShapeDtypeStruct((Bl * S, 512), bf16),
                 sem=("parallel", "parallel"))(proj, kv)


def _attn_bwd(proj, kv, dc, Bl, S):
    tq = _pick(S, (512, 256, 128))
    nq, qs, kvs, os_ = _attn_specs(S, tq)

    def body(q_ref, kv_ref, do_ref, dq_ref, dkv_ref):
        @pl.when(pl.program_id(1) == 0)
        def _():
            dkv_ref[...] = jnp.zeros_like(dkv_ref)

        for h in range(N_HEAD):
            sl = slice(h * HEAD, (h + 1) * HEAD)
            vsl = slice(512 + h * HEAD, 512 + (h + 1) * HEAD)
            q, k, v, do = q_ref[:, sl], kv_ref[0, :, sl], kv_ref[0, :, vsl], do_ref[:, sl]
            p = _attn_probs(q, k)
            dkv_ref[0, :, vsl] += _raw_dot(p, do, "tn")
            dp = _raw_dot(do, v, "nt")
            ds = p * (dp - jnp.sum(dp * p, axis=-1, keepdims=True)) * (HEAD ** -0.5)
            dq_ref[:, sl] = _raw_dot(ds, k, "nn").astype(bf16)
            dkv_ref[0, :, sl] += _raw_dot(ds, q, "tn")

    return _call(body, "attn_bwd", (Bl, nq), [qs, kvs, os_], (os_, kvs),
                 (jax.ShapeDtypeStruct((Bl * S, 512), bf16), jax.ShapeDtypeStruct((Bl, MEM_LEN, 1024), f32)),
                 sem=("arbitrary", "arbitrary"))(proj, kv, dc)


def _gate_specs(tm):
    half = D_MODEL // 2
    return [pl.BlockSpec((tm, half), functools.partial(lambda i, c: (i, c), c=(C_GL + n * D_MODEL) // half + k))
            for n in range(3) for k in range(2)]


def _merge_out_norm_fwd(branches, wb, proj, w_out, x, g):
    T = proj.shape[0]
    tm = _pick(T, (512, 256, 128))

    def body(a_ref, b_ref, c_ref, w0, w1, w2, g0a, g0b, g1a, g1b, g2a, g2b, wo_ref, x_ref, g_ref, m_ref, x1_ref, h_ref):
        acc = jnp.zeros((tm, D_MODEL), f32)
        for x_n, w_ref, ga, gb in ((a_ref, w0, g0a, g0b), (b_ref, w1, g1a, g1b), (c_ref, w2, g2a, g2b)):
            gate = jax.nn.sigmoid(jnp.concatenate([ga[...], gb[...]], axis=-1).astype(f32))
            acc = acc + gate * _raw_dot(x_n[...], w_ref[...], "nn")
        merged = acc.astype(bf16)
        m_ref[...] = merged
        x1 = x_ref[...] + _raw_dot(merged, wo_ref[...], "nn")
        x1_ref[...] = x1
        y = x1 * lax.rsqrt(jnp.mean(x1 * x1, axis=-1, keepdims=True) + EPS) * g_ref[...]
        h_ref[...] = y.astype(bf16)

    br = pl.BlockSpec((tm, 512), lambda i: (i, 0))
    w = pl.BlockSpec((512, D_MODEL), lambda i: (0, 0))
    row = pl.BlockSpec((tm, D_MODEL), lambda i: (i, 0))
    return _call(body, "merge_out_norm_fwd", (T // tm,),
                 [br, br, br, w, w, w, *_gate_specs(tm), pl.BlockSpec((D_MODEL, D_MODEL), lambda i: (0, 0)), row,
                  pl.BlockSpec((1, D_MODEL), lambda i: (0, 0))],
                 (row, row, row),
                 (jax.ShapeDtypeStruct((T, D_MODEL), bf16), jax.ShapeDtypeStruct((T, D_MODEL), f32),
                  jax.ShapeDtypeStruct((T, D_MODEL), bf16)),
                 sem=("parallel",))(*branches, *wb, *[proj] * 6, w_out, x, g)


def _merge_bwd(branches, wb, proj, merged, dx1, w_out):
    T = proj.shape[0]
    tm = _pick(T, (256, 128))

    def body(a_ref, b_ref, c_ref, w0, w1, w2, g0a, g0b, g1a, g1b, g2a, g2b, m_ref, dx_ref, wo_ref, dgl_ref, d0, d1, d2, gw_ref, gwo_ref):
        @pl.when(pl.program_id(0) == 0)
        def _():
            gw_ref[...] = jnp.zeros_like(gw_ref)
            gwo_ref[...] = jnp.zeros_like(gwo_ref)

        dx = dx_ref[...].astype(bf16)
        gwo_ref[...] += _raw_dot(m_ref[...], dx, "tn")
        dm = _raw_dot(dx, wo_ref[...], "nt")
        for n, (x_ref, w_ref, ga, gb, d_ref) in enumerate(((a_ref, w0, g0a, g0b, d0), (b_ref, w1, g1a, g1b, d1), (c_ref, w2, g2a, g2b, d2))):
            x, w = x_ref[...], w_ref[...]
            up = _raw_dot(x, w, "nn")
            sg = jax.nn.sigmoid(jnp.concatenate([ga[...], gb[...]], axis=-1).astype(f32))
            dgl_ref[n] = (dm * up * sg * (1.0 - sg)).astype(bf16)
            dup = (dm * sg).astype(bf16)
            d_ref[...] = _raw_dot(dup, w, "nt").astype(bf16)
            gw_ref[n] += _raw_dot(x, dup, "tn")

    br = pl.BlockSpec((tm, 512), lambda i: (i, 0))
    w = pl.BlockSpec((512, D_MODEL), lambda i: (0, 0))
    sh = jax.ShapeDtypeStruct((T, 512), bf16)
    row = pl.BlockSpec((tm, D_MODEL), lambda i: (i, 0))
    square = pl.BlockSpec((D_MODEL, D_MODEL), lambda i: (0, 0))
    outs = _call(body, "merge_bwd", (T // tm,), [br, br, br, w, w, w, *_gate_specs(tm), row, row, square],
                 (pl.BlockSpec((3, tm, D_MODEL), lambda i: (0, i, 0)), br, br, br, pl.BlockSpec((3, 512, D_MODEL), lambda i: (0, 0, 0)), square),
                 (jax.ShapeDtypeStruct((3, T, D_MODEL), bf16), sh, sh, sh, jax.ShapeDtypeStruct((3, 512, D_MODEL), f32),
                  jax.ShapeDtypeStruct((D_MODEL, D_MODEL), f32)),
                 sem=("arbitrary",))(*branches, *wb, *[proj] * 6, merged, dx1, w_out)
    return outs[0], outs[1:4], outs[4], outs[5]


CONV_TC = 256


def _shift_down(a, k):
    r = pltpu.roll(a, k, 0)
    row = lax.broadcasted_iota(jnp.int32, (8, a.shape[1]), 0)
    return jnp.concatenate([jnp.where(row >= k, r[:8], 0.0), r[8:]], axis=0)


def _shift_up(a, k):
    n = a.shape[0]
    r = pltpu.roll(a, n - k, 0)
    row = lax.broadcasted_iota(jnp.int32, (8, a.shape[1]), 0)
    return jnp.concatenate([r[:n - 8], jnp.where(row < 8 - k, r[n - 8:], 0.0)], axis=0)


def _conv_pre(a, a1, a2, cw, cb):
    return cb + cw[0:1] * a2 + cw[1:2] * a1 + cw[2:3] * a


def _up_conv_fwd(h2, w_up_t, cw, cb):
    Bl, S, Dd = h2.shape
    nc = D_FF // CONV_TC

    def body(h_ref, wa_ref, wb_ref, cw_ref, cb_ref, a_ref, b_ref, o_ref):
        a16 = _raw_dot(h_ref[0], wa_ref[...], "nt").astype(bf16)
        b16 = _raw_dot(h_ref[0], wb_ref[...], "nt").astype(bf16)
        a_ref[0], b_ref[0] = a16, b16
        a = a16.astype(f32)
        ac = _conv_pre(a, _shift_down(a, 1), _shift_down(a, 2), cw_ref[...], cb_ref[...])
        o_ref[0] = (jax.nn.silu(ac) * b16.astype(f32)).astype(bf16)

    seq = pl.BlockSpec((1, S, CONV_TC), lambda b, c: (b, 0, c))
    sh = jax.ShapeDtypeStruct((Bl, S, D_FF), bf16)
    return _call(body, "up_conv_fwd", (Bl, nc),
                 [pl.BlockSpec((1, S, Dd), lambda b, c: (b, 0, 0)), pl.BlockSpec((CONV_TC, Dd), lambda b, c: (c, 0)),
                  pl.BlockSpec((CONV_TC, Dd), lambda b, c: (nc + c, 0)), pl.BlockSpec((3, CONV_TC), lambda b, c: (0, c)),
                  pl.BlockSpec((1, CONV_TC), lambda b, c: (0, c))],
                 (seq, seq, seq), (sh, sh, sh), sem=("parallel", "parallel"))(h2, w_up_t, w_up_t, cw, cb)


def _down_conv_bwd(dx2, w_down, a, b, cw, cb):
    Bl, S, Dd = dx2.shape
    nc = D_FF // CONV_TC

    def body(dx_ref, wd_ref, a_ref, b_ref, cw_ref, cb_ref, da_ref, db_ref, dcw_ref, dcb_ref):
        dact = _raw_dot(dx_ref[0], wd_ref[...], "nt").astype(bf16).astype(f32)
        a, cw = a_ref[0].astype(f32), cw_ref[...]
        a1, a2 = _shift_down(a, 1), _shift_down(a, 2)
        ac = _conv_pre(a, a1, a2, cw, cb_ref[...])
        sg = jax.nn.sigmoid(ac)
        gated = dact * sg
        db_ref[0] = (gated * ac).astype(bf16)
        dac = gated * b_ref[0].astype(f32) * (1.0 + ac * (1.0 - sg))
        da_ref[0] = (cw[2:3] * dac + cw[1:2] * _shift_up(dac, 1) + cw[0:1] * _shift_up(dac, 2)).astype(bf16)
        dcw_ref[0, 0:1, :] = jnp.sum(dac * a2, axis=0, keepdims=True)
        dcw_ref[0, 1:2, :] = jnp.sum(dac * a1, axis=0, keepdims=True)
        dcw_ref[0, 2:3, :] = jnp.sum(dac * a, axis=0, keepdims=True)
        dcb_ref[0] = jnp.sum(dac, axis=0, keepdims=True)

    seq = pl.BlockSpec((1, S, CONV_TC), lambda b_, c: (b_, 0, c))
    sh = jax.ShapeDtypeStruct((Bl, S, D_FF), bf16)
    return _call(body, "down_conv_bwd", (Bl, nc),
                 [pl.BlockSpec((1, S, Dd), lambda b_, c: (b_, 0, 0)), pl.BlockSpec((CONV_TC, Dd), lambda b_, c: (c, 0)), seq, seq,
                  pl.BlockSpec((3, CONV_TC), lambda b_, c: (0, c)), pl.BlockSpec((1, CONV_TC), lambda b_, c: (0, c))],
                 (seq, seq, pl.BlockSpec((1, 3, CONV_TC), lambda b_, c: (b_, 0, c)), pl.BlockSpec((1, 1, CONV_TC), lambda b_, c: (b_, 0, c))),
                 (sh, sh, jax.ShapeDtypeStruct((Bl, 3, D_FF), f32), jax.ShapeDtypeStruct((Bl, 1, D_FF), f32)),
                 sem=("parallel", "parallel"))(dx2, w_down, a, b, cw, cb)


def _local_step(x, mem, target, p, w_in_t, late_b, late_c, send):
    Bl, S, Dd = x.shape
    T = Bl * S
    x2d, t2d, mem2d = x.reshape(T, Dd), target.reshape(T, Dd), mem.reshape(Bl * MEM_LEN, Dd)
    b_st = jnp.pad(p["b_spatial"].T, ((0, 0), (0, 128 - N_HEAD)))
    lbl = p["lb_logits"]

    proj, h = _norm_proj_fwd(x2d, p["norm1_g"], w_in_t, 1024, 1664)
    a_out = _gmlp_fwd(proj, p["ln_v_g"], p["ln_v_b"], p["w_spatial"], b_st)
    proj3 = proj.reshape(Bl, S, IN_WIDTH)
    b_out, states = _hgrn_fwd(proj3, lbl, p["hgrn_norm_g"], Bl, S)
    b_out = b_out.reshape(T, 512)
    memn = _rms_fwd(mem2d, p["mem_norm_g"], "memnorm_fwd")
    w = late_b(b_out)
    wb = w["w_branch"]
    kv = _mm(memn, w["w_mem_kv"], "nn", f32, "kv_fwd", 512, 1024).reshape(Bl, MEM_LEN, 2 * 512)
    c_out = _attn_fwd(proj, kv, Bl, S)
    branches = (a_out, b_out, c_out)
    merged, x1, h2 = _merge_out_norm_fwd(branches, wb, proj, w["w_out"], x2d, p["norm2_g"])
    w.update(late_c(h2))
    ffn_a, ffn_b, act = _up_conv_fwd(h2.reshape(Bl, S, Dd), w["w_up_t"], w["conv_w"], p["conv_b"])
    act = act.reshape(T, D_FF)
    loss_part, dx2, dx2_16, g_final = _down_final_loss(act, w["w_down"], x1, p["final_g"], t2d)

    g_w_down = _mm(act, dx2_16, "tn", bf16, "down_dw", 1408, 1024, 1024)
    da, db, g_conv_w, g_conv_b = _down_conv_bwd(dx2_16.reshape(Bl, S, Dd), w["w_down"], ffn_a, ffn_b, w["conv_w"], p["conv_b"])
    da, db = da.reshape(T, D_FF), db.reshape(T, D_FF)
    g_w_up_t = _mm(da, h2, "tn", bf16, "up_dw_a", 1408, 1024, 1024, into=(lax.empty((2 * D_FF, D_MODEL), bf16), 0))
    g_w_up_t = _mm(db, h2, "tn", bf16, "up_dw_b", 1408, 1024, 1024, into=(g_w_up_t, D_FF))
    send("c", dict(w_up=g_w_up_t, conv_w=jnp.sum(g_conv_w, axis=0), w_down=g_w_down))
    dx1, g_norm2 = _mm_rms_bwd([(da, w["w_up_t"], 0), (db, w["w_up_t"], 1)], x1, p["norm2_g"], dx2, "up_dx_norm2_bwd", 512)

    dgl, dbr, g_w_branch, g_w_out = _merge_bwd(branches, wb, proj, merged, dx1, w["w_out"])
    dxq, dkv = _attn_bwd(proj, kv, dbr[2], Bl, S)
    dkv = dkv.reshape(Bl * MEM_LEN, 2 * 512)
    g_w_kv = _mm(memn, dkv, "tn", bf16, "kv_dw", 1024, 1024, 512)
    send("b", dict(w_mem_kv=g_w_kv, w_branch=g_w_branch, w_out=g_w_out))
    dzuv, g_ln_g, g_ln_b, g_w_sp, g_b_sp = _gmlp_bwd(proj, p["ln_v_g"], p["ln_v_b"], p["w_spatial"], b_st, dbr[0])
    dproj, g_lbl, g_ng = _hgrn_bwd(proj3, lbl, p["hgrn_norm_g"], states, dbr[1].reshape(Bl, S, 512), dzuv.reshape(Bl, S, -1),
                                   dxq.reshape(Bl, S, -1), dgl.reshape(3, Bl, S, Dd), Bl, S)
    dproj = dproj.reshape(T, IN_WIDTH)
    half = Dd // 2
    send("a0", dict(w_in_half=_mm(dproj, h, "tn", bf16, "proj_dw_0", 512, half, n_tiles=(0, 1))))
    g_half = _mm(dproj, h, "tn", bf16, "proj_dw_1", 512, half, n_tiles=(1, 1))
    dkv, g_half = lax.optimization_barrier((dkv, g_half))
    send("a1", dict(w_in_half=g_half))
    dmemn = _mm(dkv, w["w_mem_kv"], "nt", f32, "kv_dx", 512, 1024)
    _, g_mem_norm = _rms_bwd(mem2d, p["mem_norm_g"], dmemn, "memnorm_bwd")
    dx, g_norm1 = _mm_rms_bwd([(dproj, w_in_t, 0)], x2d, p["norm1_g"], dx1, "proj_dx_norm1_bwd", 512)

    gs = dict(w_spatial=g_w_sp, norm1_g=g_norm1, mem_norm_g=g_mem_norm, norm2_g=g_norm2, final_g=g_final, lb_logits=g_lbl,
              ln_v_g=g_ln_g, ln_v_b=g_ln_b, b_spatial=g_b_sp, hgrn_norm_g=g_ng, conv_b=g_conv_b)
    return loss_part, dx.reshape(Bl, S, Dd), gs


def _coords():
    return lax.axis_index("x"), lax.axis_index("y"), lax.axis_index("c")


def _slot(dev):
    return 4 * dev[0] + 2 * dev[1] + dev[2]


def _comm_call(body, name, arrays, out_shapes, n_sem):
    n = len(arrays)
    hbm = pl.BlockSpec(memory_space=pl.ANY)
    return pl.pallas_call(
        body, name=name, out_shape=out_shapes, in_specs=[hbm] * n, out_specs=[hbm] * n,
        scratch_shapes=[pltpu.SemaphoreType.DMA((n_sem, n)), pltpu.SemaphoreType.DMA((n_sem, n)), pltpu.SemaphoreType.DMA((n,))])(*arrays)


def _all_gather(blocks, name):
    n = len(blocks)

    def body(*refs):
        x_refs, o_refs, (send_sems, recv_sems, local_sems) = refs[:n], refs[n:2 * n], refs[2 * n:]
        x, y, c = _coords()
        me, sibling = (x, y, c), (x, y, 1 - c)
        chips = [(1 - x, y), (x, 1 - y), (1 - x, 1 - y)]

        def copy(a, k, block_dev, to, from_input=False):
            dst = o_refs[a].at[_slot(block_dev)]
            return pltpu.make_async_remote_copy(src_ref=x_refs[a] if from_input else dst, dst_ref=dst, send_sem=send_sems.at[k, a],
                                                recv_sem=recv_sems.at[k, a], device_id=to, device_id_type=MESH)

        mine = [pltpu.make_async_copy(x_refs[a], o_refs[a].at[_slot(me)], local_sems.at[a]) for a in range(n)]
        first = [copy(a, 0, me, sibling, True) for a in range(n)]
        first += [copy(a, 1 + j, me, (*chip, c), True) for j, chip in enumerate(chips) for a in range(n)]
        for cp in mine + first:
            cp.start()
        passed = []
        for j, chip in enumerate(chips):
            for a in range(n):
                copy(a, 1 + j, (*chip, c), me).wait_recv()
                fwd = copy(a, 4 + j, (*chip, c), sibling)
                fwd.start()
                passed.append(fwd)
        for a in range(n):
            copy(a, 0, sibling, me).wait_recv()
        for j, chip in enumerate(chips):
            for a in range(n):
                copy(a, 4 + j, (*chip, 1 - c), me).wait_recv()
        for cp in first + passed:
            cp.wait_send()
        for cp in mine:
            cp.wait()

    return _comm_call(body, name, blocks, [jax.ShapeDtypeStruct((N_DEV,) + b.shape, b.dtype) for b in blocks], 7)


_REL = [(0, 0, 1), (0, 1, 0), (0, 1, 1), (1, 0, 0), (1, 0, 1), (1, 1, 0), (1, 1, 1)]


def _seq_exchange(arrays, gather, name, collective_id):
    n = len(arrays)
    hbm = pltpu.MemorySpace.HBM
    srcs = [jax.new_ref(a, memory_space=hbm) for a in arrays]
    lands = [jax.empty_ref(jax.ShapeDtypeStruct(((N_DEV,) + a.shape) if gather else a.shape, a.dtype), memory_space=hbm) for a in arrays]

    @pl.kernel(mesh=plsc.ScalarSubcoreMesh(axis_name="sequencer", num_cores=1), name=name,
               scratch_types=(pltpu.SemaphoreType.DMA((7, n)), pltpu.SemaphoreType.DMA((7, n)), pltpu.SemaphoreType.DMA((n,))),
               compiler_params=pltpu.CompilerParams(collective_id=collective_id))
    def launch(send, recv, local):
        x, y, c = _coords()
        me = (x, y, c)
        peers = [(x ^ dx, y ^ dy, c ^ dc) for dx, dy, dc in _REL]
        barrier = pltpu.get_barrier_semaphore()
        for peer in peers:
            pl.semaphore_signal(barrier, inc=1, device_id=peer, device_id_type=MESH)
        pl.semaphore_wait(barrier, len(peers))

        def copy(a, k, peer, arrival):
            return pltpu.make_async_remote_copy(
                src_ref=srcs[a] if gather else srcs[a].at[_slot(peer)], dst_ref=lands[a].at[_slot(peer if arrival else me)],
                send_sem=send.at[k, a], recv_sem=recv.at[k, a], device_id=peer, device_id_type=MESH)

        mine = [pltpu.make_async_copy(srcs[a] if gather else srcs[a].at[_slot(me)], lands[a].at[_slot(me)], local.at[a])
                for a in range(n)]
        out = [copy(a, k, peer, False) for a in range(n) for k, peer in enumerate(peers)]
        for cp in mine + out:
            cp.start()
        for a in range(n):
            for k, peer in enumerate(peers):
                copy(a, k, peer, True).wait_recv()
        for cp in out:
            cp.wait_send()
        for cp in mine:
            cp.wait()

    launch()
    return [land[...] for land in lands]


def _adam_math(w, g, m, v):
    m_ = ADAM_B1 * m + (1.0 - ADAM_B1) * g
    v_ = ADAM_B2 * v + (1.0 - ADAM_B2) * jnp.square(g)
    m_hat = m_ / (1.0 - ADAM_B1 ** ADAM_STEP)
    v_hat = v_ / (1.0 - ADAM_B2 ** ADAM_STEP)
    return -ADAM_LR * (m_hat / (jnp.sqrt(v_hat) + ADAM_EPS) + ADAM_WD * w), m_, v_


def _reduce_adamw(parts, w, m, v, name):
    R, L = w.shape
    tr = _pick(R, (256, 208, 176, 128, 64, 32, 16, 8))
    n = len(parts)

    def body(*refs):
        w_ref, m_ref, v_ref, g_ref, d_ref, nm_ref, nv_ref = refs[n:]
        pieces = []
        for p_ref in refs[:n]:
            g = p_ref[0].astype(f32)
            for i in range(1, N_DEV):
                g = g + p_ref[i].astype(f32)
            pieces.append(g)
        g = pieces[0] if n == 1 else jnp.concatenate(pieces, axis=-1)
        g_ref[...] = g
        d_ref[...], nm_ref[...], nv_ref[...] = _adam_math(w_ref[...], g, m_ref[...], v_ref[...])

    blk = pl.BlockSpec((tr, L), lambda i: (i, 0))
    sh = jax.ShapeDtypeStruct((R, L), f32)
    return _call(body, name, (R // tr,), [pl.BlockSpec((N_DEV, tr, q.shape[2]), lambda i: (0, i, 0)) for q in parts] + [blk, blk, blk],
                 (blk,) * 4, (sh,) * 4, sem=("parallel",))(*parts, w, m, v)


SMALL = (("w_spatial", (512, 128), 0), ("norm1_g", (1, 1024), 512), ("mem_norm_g", (1, 1024), 520), ("norm2_g", (1, 1024), 528),
         ("final_g", (1, 1024), 536), ("lb_logits", (2, 512), 544), ("ln_v_g", (1, 512), 552), ("ln_v_b", (1, 512), 556),
         ("b_spatial", (4, 128), 560), ("hgrn_norm_g", (1, 128), 564), ("conv_b", (1, 2816), 565))
LOSS_ROW, SMALL_USED, SMALL_ROWS = 587, 588, 640


def _segments(shape, base):
    r, n = shape
    per = n // 128
    return [(base + i * per + j, i, slice(j * 128, (j + 1) * 128)) for i in range(r) for j in range(per)]


def _pack_small(gs, loss_part):
    names = [n for n, _, _ in SMALL]

    def body(*refs):
        src, loss_ref, o_ref = dict(zip(names, refs[:-2])), refs[-2], refs[-1]
        o_ref[SMALL_USED:SMALL_ROWS, :] = jnp.zeros((SMALL_ROWS - SMALL_USED, 128), f32)
        o_ref[LOSS_ROW:LOSS_ROW + 1, :] = loss_ref[...]
        for name, shape, base in SMALL:
            ref = src[name]
            if name == "w_spatial":
                o_ref[base:base + 512, :] = ref[...].reshape(512, 128)
            elif name == "b_spatial":
                o_ref[base:base + 4, :] = ref[0:4, :]
            elif name == "conv_b":
                per_example = functools.reduce(lambda u, v_: u + v_, [ref[b] for b in range(ref.shape[0])])
                for row, i, sl in _segments(shape, base):
                    o_ref[row:row + 1, :] = per_example[i:i + 1, sl]
            elif name == "hgrn_norm_g":
                per_head = [ref[b, h] for b in range(ref.shape[0]) for h in range(N_HEAD)]
                o_ref[base:base + 1, :] = functools.reduce(lambda u, v_: u + v_, per_head)
            else:
                for row, i, sl in _segments(shape, base):
                    o_ref[row:row + 1, :] = ref[i:i + 1, sl]

    return pl.pallas_call(body, name="pack_small", out_shape=jax.ShapeDtypeStruct((SMALL_ROWS, 128), f32))(
        *[gs[n] for n in names], loss_part)


def _small_update(gathered, w, m, v):
    names = [n for n, _, _ in SMALL]
    k = len(names)

    def body(*refs):
        p_ref = refs[0]
        ins = [dict(zip(names, refs[1 + i * k:1 + (i + 1) * k])) for i in range(3)]
        outs = [dict(zip(names, refs[1 + (3 + i) * k:1 + (4 + i) * k])) for i in range(4)]
        loss_ref, gsum = refs[-2], refs[-1]
        g = p_ref[0]
        for i in range(1, N_DEV):
            g = g + p_ref[i]
        gsum[...] = g
        loss_ref[...] = gsum[LOSS_ROW:LOSS_ROW + 1, :]
        for name, shape, base in SMALL:
            if name == "w_spatial":
                where = [(slice(base, base + 512), (slice(None), slice(None)))]
            else:
                where = [(slice(row, row + 1), (slice(i, i + 1), sl)) for row, i, sl in _segments(shape, base)]
            for rows, at in where:
                g_ = gsum[rows, :]
                d_, m_, v_ = _adam_math(ins[0][name][at], g_, ins[1][name][at], ins[2][name][at])
                for o, val in zip(outs, (g_, d_, m_, v_)):
                    o[name][at] = val

    args = [gathered] + [d[n] for d in (w, m, v) for n in names]
    out_shapes = [jax.ShapeDtypeStruct(shape, f32) for _ in range(4) for _, shape, _ in SMALL] + [jax.ShapeDtypeStruct((1, 128), f32)]
    outs = pl.pallas_call(body, name="small_update", out_shape=out_shapes, scratch_shapes=[pltpu.VMEM((SMALL_ROWS, 128), f32)])(*args)
    return [dict(zip(names, outs[i * k:(i + 1) * k])) for i in range(4)], outs[-1]


def _cols_full(g):
    return jnp.moveaxis(g, 0, -2).reshape(g.shape[1:-1] + (N_DEV * g.shape[-1],))


def _cols_parts(full):
    n = full.shape[-1] // N_DEV
    return jnp.moveaxis(full.reshape(full.shape[:-1] + (N_DEV, n)), -2, 0)


def kernel(x, mem, norm1_g, w_in, ln_v_g, ln_v_b, w_spatial, b_spatial, lb_logits, hgrn_norm_g, mem_norm_g, w_mem_kv, w_branch, w_out, norm2_g, w_up, conv_w, conv_b, w_down, final_g, loss_target, m_norm1_g, m_w_in, m_ln_v_g, m_ln_v_b, m_w_spatial, m_b_spatial, m_lb_logits, m_hgrn_norm_g, m_mem_norm_g, m_w_mem_kv, m_w_branch, m_w_out, m_norm2_g, m_w_up, m_conv_w, m_conv_b, m_w_down, m_final_g, v_norm1_g, v_w_in, v_ln_v_g, v_ln_v_b, v_w_spatial, v_b_spatial, v_lb_logits, v_hgrn_norm_g, v_mem_norm_g, v_w_mem_kv, v_w_branch, v_w_out, v_norm2_g, v_w_up, v_conv_w, v_conv_b, v_w_down, v_final_g):
    given = dict(locals())
    order = ("norm1_g", "w_in", "ln_v_g", "ln_v_b", "w_spatial", "b_spatial", "lb_logits", "hgrn_norm_g", "mem_norm_g",
             "w_mem_kv", "w_branch", "w_out", "norm2_g", "w_up", "conv_w", "conv_b", "w_down", "final_g")
    groups = dict(a=("w_in",), b=("w_mem_kv", "w_branch", "w_out"), c=("w_up", "conv_w", "w_down"))

    by_rows = ("w_in", "w_up")
    shard_of = lambda n, prefix="": jnp.swapaxes(given[prefix + n][0], 0, 1) if n in by_rows else given[prefix + n][0]
    wire = {n: shard_of(n).astype(f32 if n == "conv_w" else bf16) for ns in groups.values() for n in ns}
    w_in_full = _all_gather([wire["w_in"]], "gather_w_in")[0].reshape(IN_WIDTH, D_MODEL)
    w_in_full, wire_b, wire_c = lax.optimization_barrier((w_in_full, [wire[n] for n in groups["b"]], [wire[n] for n in groups["c"]]))
    rest_b = _seq_exchange(wire_b, True, "gather_b", 1)
    rest_c = _seq_exchange(wire_c, True, "gather_c", 6)

    def late_b(after):
        _, (kv_, br_, out_) = lax.optimization_barrier((after, tuple(rest_b)))
        br_ = _cols_full(br_)
        return dict(w_mem_kv=kv_.reshape(D_MODEL, 2 * 512), w_branch=[br_[n] for n in range(3)], w_out=out_.reshape(D_MODEL, D_MODEL))

    def late_c(after):
        _, (up_, cw_, down_) = lax.optimization_barrier((after, tuple(rest_c)))
        return dict(w_up_t=up_.reshape(2 * D_FF, D_MODEL), conv_w=_cols_full(cw_), w_down=down_.reshape(D_FF, D_MODEL))

    to_parts = dict(w_in_half=lambda g_: g_.reshape(N_DEV, -1, D_MODEL // 2), w_up=lambda g_: g_.reshape(N_DEV, -1, D_MODEL), conv_w=_cols_parts,
                    w_branch=lambda g_: _cols_parts(g_.astype(bf16)).reshape(N_DEV, -1, 128),
                    w_mem_kv=lambda g_: g_.reshape(N_DEV, -1, 2 * 512), w_out=lambda g_: g_.astype(bf16).reshape(N_DEV, -1, D_MODEL),
                    w_down=lambda g_: g_.reshape(N_DEV, -1, D_MODEL))
    scatters = {}


    def send(tag, grads_):
        parts = [to_parts[n](g_) for n, g_ in grads_.items()]
        scatters[tag] = _seq_exchange(parts, False, f"scatter_{tag}", dict(a0=2, a1=7, b=4, c=5)[tag])

    small_2d = lambda prefix: {n: given[prefix + n].reshape(shape) for n, shape, _ in SMALL}
    p = small_2d("")
    p["w_spatial"] = w_spatial[0]
    updates = {}

    def update(tag):
        arrived = [scatters["a0"] + scatters["a1"]] if tag == "a" else [[parts] for parts in scatters[tag]]
        for n, parts in zip(groups[tag], arrived):
            state = [shard_of(n, pre) for pre in ("", "m_", "v_")]
            res = _reduce_adamw(parts, *[a.reshape(-1, a.shape[-1]) for a in state], "adamw_" + n)
            updates[n] = [jnp.swapaxes(r, 0, 1) for r in res] if n in by_rows else res

    loss_part, grad_x, gs = _local_step(x, mem, loss_target, p, w_in_full, late_b, late_c, send)

    packed, (scatters["c"], scatters["b"]) = lax.optimization_barrier((_pack_small(gs, loss_part), (scatters["c"], scatters["b"])))
    gathered = _seq_exchange([packed], True, "gather_small", 3)[0]

    update("c")
    update("b")
    update("a")
    grads, delta, new_m, new_v = {}, {}, {}, {}
    for n, res in updates.items():
        grads[n], delta[n], new_m[n], new_v[n] = [r.reshape(given[n].shape) for r in res]

    small_results, loss_row = _small_update(gathered, small_2d(""), small_2d("m_"), small_2d("v_"))
    for dst, res in zip((grads, delta, new_m, new_v), small_results):
        for n, _, _ in SMALL:
            dst[n] = res[n].reshape(given[n].shape)
    loss = loss_row[0, 0]

    return (loss, grad_x, *[grads[n] for n in order], *[delta[n] for n in order], *[new_m[n] for n in order],
            *[new_v[n] for n in order])
```

```python
import functools

import jax
import jax.numpy as jnp
from jax import lax
from jax.experimental import pallas as pl
from jax.experimental.pallas import tpu as pltpu
from jax.experimental.pallas import tpu_sc as plsc

f32 = jnp.float32
bf16 = jnp.bfloat16

N_DEV = 8
D_MODEL = 1024
EPS = 1e-6
GM_CHUNK = 128
HG_CHUNK = 64
HEAD = 128
N_HEAD = 4
MEM_LEN = 256
D_FF = 2816
IN_WIDTH = 6656
C_ZU, C_HQ, C_HF, C_HI, C_HG, C_XQ, C_GL = 0, 1024, 1536, 2048, 2560, 3072, 3584
ADAM_LR, ADAM_B1, ADAM_B2, ADAM_EPS, ADAM_WD, ADAM_STEP = 0.001, 0.9, 0.999, 1e-08, 0.01, 10
VMEM_LIMIT = 56 * 1024 * 1024
MESH = pl.DeviceIdType.MESH


def _pick(n, cands):
    for c in cands:
        if n % c == 0:
            return c
    return n


def _call(body, name, grid, in_specs, out_specs, out_shape, scratch=(), sem=None, **cp):
    params = dict(vmem_limit_bytes=VMEM_LIMIT, **cp)
    if sem is not None:
        params["dimension_semantics"] = sem
    return pl.pallas_call(
        body, name=name, grid=grid, in_specs=in_specs, out_specs=out_specs, out_shape=out_shape,
        scratch_shapes=list(scratch), compiler_params=pltpu.CompilerParams(**params))


_DN = {"nn": (((1,), (0,)), ((), ())), "nt": (((1,), (1,)), ((), ())), "tn": (((0,), (0,)), ((), ()))}


def _raw_dot(a, b, mode):
    return lax.dot_general(a.astype(bf16), b.astype(bf16), _DN[mode], preferred_element_type=f32)


@jax.custom_vjp
def _dot_nn(a, b):
    return _raw_dot(a, b, "nn")


_dot_nn.defvjp(lambda a, b: (_raw_dot(a, b, "nn"), (a, b)),
               lambda r, g: (_raw_dot(g, r[1], "nt"), _raw_dot(r[0], g, "tn")))


@jax.custom_vjp
def _dot_nt(a, b):
    return _raw_dot(a, b, "nt")


_dot_nt.defvjp(lambda a, b: (_raw_dot(a, b, "nt"), (a, b)),
               lambda r, g: (_raw_dot(g, r[1], "nn"), _raw_dot(g, r[0], "tn")))


@jax.custom_vjp
def _dot_tn(a, b):
    return _raw_dot(a, b, "tn")


_dot_tn.defvjp(lambda a, b: (_raw_dot(a, b, "tn"), (a, b)),
               lambda r, g: (_raw_dot(r[1], g, "nt"), _raw_dot(r[0], g, "nn")))


def _tri(n, lower):
    r = lax.broadcasted_iota(jnp.int32, (n, n), 0)
    c = lax.broadcasted_iota(jnp.int32, (n, n), 1)
    return ((c <= r) if lower else (c >= r)).astype(f32)


def _sel_dot(sel, x, mode, x_first=False, pieces=3):
    sel = sel.astype(bf16)
    out, rest = None, x
    for p in range(pieces):
        piece = rest.astype(bf16)
        part = lax.dot_general(*((piece, sel) if x_first else (sel, piece)), _DN[mode], preferred_element_type=f32)
        out = part if out is None else out + part
        if p + 1 < pieces:
            rest = rest - piece.astype(f32)
    return out


def _egrad(fn, x, ct):
    return jax.vjp(fn, x)[1](ct)[0]


def _mm(a, b, mode, out_dtype, name, tm, tn, tk=None, residual=None, into=None, n_tiles=None):
    if mode == "nn":
        (M, K), (_, N) = a.shape, b.shape
    elif mode == "nt":
        (M, K), (N, _) = a.shape, b.shape
    else:
        (K, M), (_, N) = a.shape, b.shape
    j0 = 0
    if n_tiles is not None:
        assert mode != "nt" and N % tn == 0 and residual is None
        j0, N = n_tiles[0], n_tiles[1] * tn
    tm, tn = min(tm, M), min(tn, N)
    tk = K if tk is None else min(tk, K)
    assert M % tm == 0 and N % tn == 0 and K % tk == 0, (name, M, N, K, tm, tn, tk)
    nk = K // tk

    def body(*refs):
        acc_ref = refs[-1] if nk > 1 else None
        refs = refs[:-1] if nk > 1 else refs
        if residual is None:
            a_ref, b_ref, *_, o_ref = refs
        else:
            a_ref, b_ref, r_ref, o_ref = refs

        def finish(r):
            if residual is not None:
                r = r + r_ref[...]
            o_ref[...] = r.astype(out_dtype)

        part = _raw_dot(a_ref[...], b_ref[...], mode)
        if nk == 1:
            finish(part)
            return
        k = pl.program_id(2)

        @pl.when(k == 0)
        def _():
            acc_ref[...] = part

        @pl.when((k > 0) & (k < nk - 1))
        def _():
            acc_ref[...] += part

        @pl.when(k == nk - 1)
        def _():
            finish(acc_ref[...] + part)

    a_spec = {"nn": pl.BlockSpec((tm, tk), lambda i, j, k: (i, k)),
              "nt": pl.BlockSpec((tm, tk), lambda i, j, k: (i, k)),
              "tn": pl.BlockSpec((tk, tm), lambda i, j, k: (k, i))}[mode]
    b_spec = {"nn": pl.BlockSpec((tk, tn), lambda i, j, k: (k, j + j0)),
              "nt": pl.BlockSpec((tn, tk), lambda i, j, k: (j, k)),
              "tn": pl.BlockSpec((tk, tn), lambda i, j, k: (k, j + j0))}[mode]
    o_spec = pl.BlockSpec((tm, tn), lambda i, j, k: (i, j))
    in_specs = [a_spec, b_spec] + ([o_spec] if residual is not None else [])
    args = (a, b) + ((residual,) if residual is not None else ())
    out_shape = jax.ShapeDtypeStruct((M, N), out_dtype)
    extra = {}
    if into is not None:
        assert residual is None and into[1] % tm == 0
        o_spec = pl.BlockSpec((tm, tn), lambda i, j, k: (i + into[1] // tm, j))
        out_shape = jax.ShapeDtypeStruct(into[0].shape, out_dtype)
        in_specs, args = in_specs + [pl.BlockSpec(memory_space=pl.ANY)], args + (into[0],)
        extra = dict(input_output_aliases={2: 0})
    return pl.pallas_call(
        body, name=name, grid=(M // tm, N // tn, nk), in_specs=in_specs, out_specs=o_spec, out_shape=out_shape,
        scratch_shapes=[pltpu.VMEM((tm, tn), f32)] if nk > 1 else [],
        compiler_params=pltpu.CompilerParams(vmem_limit_bytes=VMEM_LIMIT, dimension_semantics=("parallel", "parallel", "arbitrary")),
        **extra)(*args)


def _rms_fwd(x, g, name):
    R, Dd = x.shape
    tr = _pick(R, (512, 256, 128))

    def body(x_ref, g_ref, o_ref):
        xf = x_ref[...]
        o_ref[...] = (xf * lax.rsqrt(jnp.mean(xf * xf, axis=-1, keepdims=True) + EPS) * g_ref[...]).astype(bf16)

    row = pl.BlockSpec((tr, Dd), lambda i: (i, 0))
    return _call(body, name, (R // tr,), [row, pl.BlockSpec((1, Dd), lambda i: (0, 0))], row, jax.ShapeDtypeStruct((R, Dd), bf16),
                 sem=("parallel",))(x, g)


def _norm_proj_fwd(x, g, w_t, tm, tn):
    R, Dd = x.shape
    N = w_t.shape[0]
    tm = min(tm, R)
    assert R % tm == 0 and N % tn == 0

    def body(x_ref, g_ref, w_ref, p_ref, h_ref):
        @pl.when(pl.program_id(1) == 0)
        def _():
            xf = x_ref[...]
            h_ref[...] = (xf * lax.rsqrt(jnp.mean(xf * xf, axis=-1, keepdims=True) + EPS) * g_ref[...]).astype(bf16)

        p_ref[...] = _raw_dot(h_ref[...], w_ref[...], "nt").astype(bf16)

    row = pl.BlockSpec((tm, Dd), lambda i, j: (i, 0))
    return _call(body, "norm1_proj_fwd", (R // tm, N // tn),
                 [row, pl.BlockSpec((1, Dd), lambda i, j: (0, 0)), pl.BlockSpec((tn, Dd), lambda i, j: (j, 0))],
                 (pl.BlockSpec((tm, tn), lambda i, j: (i, j)), row),
                 (jax.ShapeDtypeStruct((R, N), bf16), jax.ShapeDtypeStruct((R, Dd), bf16)), sem=("parallel", "arbitrary"))(x, g, w_t)


def _rms_bwd(x, g, dh, name, residual=None):
    R, Dd = x.shape
    tr = _pick(R, (512, 256, 128))

    def body(*refs):
        if residual is None:
            x_ref, g_ref, dh_ref, dx_ref, dg_ref = refs
        else:
            x_ref, g_ref, dh_ref, r_ref, dx_ref, dg_ref = refs
        xf = x_ref[...]
        rs = lax.rsqrt(jnp.mean(xf * xf, axis=-1, keepdims=True) + EPS)
        y = xf * rs
        dh_ = dh_ref[...].astype(f32)
        dy = dh_ * g_ref[...]
        dx = rs * (dy - y * jnp.mean(dy * y, axis=-1, keepdims=True))
        if residual is not None:
            dx = dx + r_ref[...]
        dx_ref[...] = dx

        @pl.when(pl.program_id(0) == 0)
        def _():
            dg_ref[...] = jnp.zeros_like(dg_ref)

        dg_ref[...] += jnp.sum(dh_ * y, axis=0, keepdims=True)

    row = pl.BlockSpec((tr, Dd), lambda i: (i, 0))
    vec = pl.BlockSpec((1, Dd), lambda i: (0, 0))
    in_specs = [row, vec, row] + ([row] if residual is not None else [])
    args = (x, g, dh) + ((residual,) if residual is not None else ())
    return _call(body, name, (R // tr,), in_specs, (row, vec),
                 (jax.ShapeDtypeStruct((R, Dd), f32), jax.ShapeDtypeStruct((1, Dd), f32)), sem=("arbitrary",))(*args)


def _mm_rms_bwd(pairs, x, g, residual, name, tm):
    M = x.shape[0]
    Dd = x.shape[1]
    tm = min(tm, M)
    n = len(pairs)

    def body(*refs):
        ab_refs, (x_ref, g_ref, r_ref, dx_ref, dg_ref) = refs[:2 * n], refs[2 * n:]
        dh_ = _raw_dot(ab_refs[0][...], ab_refs[1][...], "nn")
        for k in range(1, n):
            dh_ = dh_ + _raw_dot(ab_refs[2 * k][...], ab_refs[2 * k + 1][...], "nn")
        xf = x_ref[...]
        rs = lax.rsqrt(jnp.mean(xf * xf, axis=-1, keepdims=True) + EPS)
        y = xf * rs
        dy = dh_ * g_ref[...]
        dx_ref[...] = rs * (dy - y * jnp.mean(dy * y, axis=-1, keepdims=True)) + r_ref[...]

        @pl.when(pl.program_id(0) == 0)
        def _():
            dg_ref[...] = jnp.zeros_like(dg_ref)

        dg_ref[...] += jnp.sum(dh_ * y, axis=0, keepdims=True)

    row = pl.BlockSpec((tm, Dd), lambda i: (i, 0))
    vec = pl.BlockSpec((1, Dd), lambda i: (0, 0))
    in_specs, args = [], []
    for a, b, k in pairs:
        in_specs += [pl.BlockSpec((tm, a.shape[1]), lambda i: (i, 0)),
                     pl.BlockSpec((a.shape[1], b.shape[1]), functools.partial(lambda i, k_: (k_, 0), k_=k),
                                  pipeline_mode=pl.Buffered(1))]
        args += [a, b]
    in_specs += [row, vec, row]
    args += [x, g, residual]
    return _call(body, name, (M // tm,), in_specs, (row, vec),
                 (jax.ShapeDtypeStruct((M, Dd), f32), jax.ShapeDtypeStruct((1, Dd), f32)), sem=("arbitrary",))(*args)


def _down_final_loss(act, w_down, x1, g, target):
    R, Dd = x1.shape
    tr = _pick(R, (512, 256, 128))

    def body(a_ref, w_ref, x1_ref, g_ref, t_ref, loss_ref, dx_ref, dxb_ref, dg_ref):
        xf = _raw_dot(a_ref[...], w_ref[...], "nn") + x1_ref[...]
        rs = lax.rsqrt(jnp.mean(xf * xf, axis=-1, keepdims=True) + EPS)
        y = xf * rs
        err = y * g_ref[...] - t_ref[...]
        dh_ = err * (1.0 / Dd)
        dy = dh_ * g_ref[...]
        dx = rs * (dy - y * jnp.mean(dy * y, axis=-1, keepdims=True))
        dx_ref[...] = dx
        dxb_ref[...] = dx.astype(bf16)

        @pl.when(pl.program_id(0) == 0)
        def _():
            dg_ref[...] = jnp.zeros_like(dg_ref)
            loss_ref[...] = jnp.zeros_like(loss_ref)

        dg_ref[...] += jnp.sum(dh_ * y, axis=0, keepdims=True)
        part = jnp.sum(jnp.mean(err * err, axis=-1, keepdims=True), axis=0, keepdims=True)
        loss_ref[...] += 0.5 * part

    row = pl.BlockSpec((tr, Dd), lambda i: (i, 0))
    vec = pl.BlockSpec((1, Dd), lambda i: (0, 0))
    in_specs = [pl.BlockSpec((tr, act.shape[1]), lambda i: (i, 0)), pl.BlockSpec(w_down.shape, lambda i: (0, 0)), row, vec, row]
    return _call(body, "down_final_loss", (R // tr,), in_specs, (pl.BlockSpec((1, 128), lambda i: (0, 0)), row, row, vec),
                 (jax.ShapeDtypeStruct((1, 128), f32), jax.ShapeDtypeStruct((R, Dd), f32), jax.ShapeDtypeStruct((R, Dd), bf16),
                  jax.ShapeDtypeStruct((1, Dd), f32)), sem=("arbitrary",))(act, w_down, x1, g, target)


def _gmlp_parts(zuv, ln_g, ln_b):
    zu, zv = zuv[:, :512], zuv[:, 512:]
    u = jax.nn.gelu(zu)
    v = jax.nn.gelu(zv)
    mu = jnp.mean(v, axis=-1, keepdims=True)
    rs = lax.rsqrt(jnp.mean(jnp.square(v - mu), axis=-1, keepdims=True) + EPS)
    xh = (v - mu) * rs
    return zu, zv, u, xh, rs, xh * ln_g + ln_b


GM_TILE_CHUNKS = 4


def _gmlp_tile(T):
    n = _pick(T // GM_CHUNK, (GM_TILE_CHUNKS, 2, 1))
    return n, n * GM_CHUNK


def _gmlp_fwd(proj, ln_g, ln_b, w_s, b_st):
    T = proj.shape[0]
    nch, rows = _gmlp_tile(T)

    def body(p_ref, g_ref, b_ref, w_ref, bs_ref, o_ref):
        _, _, u, _, _, vn = _gmlp_parts(p_ref[...].astype(f32), g_ref[...], b_ref[...])
        causal = _tri(GM_CHUNK, True) > 0
        for gi in range(N_HEAD):
            sl = slice(gi * HEAD, (gi + 1) * HEAD)
            w = jnp.where(causal, w_ref[gi], 0.0)
            for ch in range(nch):
                rs_ = slice(ch * GM_CHUNK, (ch + 1) * GM_CHUNK)
                mixed = _raw_dot(w, vn[rs_, sl], "nn") + bs_ref[:, gi:gi + 1]
                o_ref[rs_, sl] = (u[rs_, sl] * mixed).astype(bf16)

    vec = pl.BlockSpec((1, 512), lambda i: (0, 0))
    return _call(body, "gmlp_fwd", (T // rows,),
                 [pl.BlockSpec((rows, 1024), lambda i: (i, 0)), vec, vec,
                  pl.BlockSpec((N_HEAD, GM_CHUNK, GM_CHUNK), lambda i: (0, 0, 0)), pl.BlockSpec((GM_CHUNK, 128), lambda i: (0, 0))],
                 pl.BlockSpec((rows, 512), lambda i: (i, 0)), jax.ShapeDtypeStruct((T, 512), bf16), sem=("parallel",))(
        proj, ln_g, ln_b, w_s, b_st)


def _gmlp_bwd(proj, ln_g, ln_b, w_s, b_st, da):
    T = proj.shape[0]
    nch, rows = _gmlp_tile(T)

    def body(p_ref, g_ref, b_ref, w_ref, bs_ref, da_ref, dp_ref, dg_ref, db_ref, dw_ref, dbs_ref):
        zu, zv, u, xh, rs, vn = _gmlp_parts(p_ref[...].astype(f32), g_ref[...], b_ref[...])
        causal = _tri(GM_CHUNK, True) > 0
        sub = lax.broadcasted_iota(jnp.int32, (8, GM_CHUNK), 0)
        ones = jnp.ones((8, HEAD), f32)
        dout = da_ref[...].astype(f32)

        @pl.when(pl.program_id(0) == 0)
        def _():
            for r in (dg_ref, db_ref, dw_ref, dbs_ref):
                r[...] = jnp.zeros_like(r)

        du, dvn, dbs = [], [], jnp.zeros((8, GM_CHUNK), f32)
        for gi in range(N_HEAD):
            sl = slice(gi * HEAD, (gi + 1) * HEAD)
            w = jnp.where(causal, w_ref[gi], 0.0)
            du_g, dvn_g, dw_g = [], [], jnp.zeros((GM_CHUNK, GM_CHUNK), f32)
            for ch in range(nch):
                rs_ = slice(ch * GM_CHUNK, (ch + 1) * GM_CHUNK)
                mixed = _raw_dot(w, vn[rs_, sl], "nn") + bs_ref[:, gi:gi + 1]
                du_g.append(dout[rs_, sl] * mixed)
                dm = dout[rs_, sl] * u[rs_, sl]
                dbs = dbs + jnp.where(sub == gi, _sel_dot(ones, dm, "nt"), 0.0)
                dw_g = dw_g + _raw_dot(dm, vn[rs_, sl], "nt")
                dvn_g.append(_raw_dot(w, dm, "tn"))
            dw_ref[gi] += jnp.where(causal, dw_g, 0.0)
            du.append(jnp.concatenate(du_g, axis=0))
            dvn.append(jnp.concatenate(dvn_g, axis=0))
        dbs_ref[...] += dbs
        du = jnp.concatenate(du, axis=-1)
        dvn = jnp.concatenate(dvn, axis=-1)
        dg_ref[...] += jnp.sum(dvn * xh, axis=0, keepdims=True)
        db_ref[...] += jnp.sum(dvn, axis=0, keepdims=True)
        dxh = dvn * g_ref[...]
        dv = rs * (dxh - jnp.mean(dxh, axis=-1, keepdims=True) - xh * jnp.mean(dxh * xh, axis=-1, keepdims=True))
        dp_ref[:, :512] = _egrad(jax.nn.gelu, zu, du).astype(bf16)
        dp_ref[:, 512:] = _egrad(jax.nn.gelu, zv, dv).astype(bf16)

    vec = pl.BlockSpec((1, 512), lambda i: (0, 0))
    wsp = pl.BlockSpec((N_HEAD, GM_CHUNK, GM_CHUNK), lambda i: (0, 0, 0))
    return _call(body, "gmlp_bwd", (T // rows,),
                 [pl.BlockSpec((rows, 1024), lambda i: (i, 0)), vec, vec, wsp, pl.BlockSpec((GM_CHUNK, 128), lambda i: (0, 0)),
                  pl.BlockSpec((rows, 512), lambda i: (i, 0))],
                 (pl.BlockSpec((rows, 1024), lambda i: (i, 0)), vec, vec, wsp, pl.BlockSpec((8, GM_CHUNK), lambda i: (0, 0))),
                 (jax.ShapeDtypeStruct((T, 1024), bf16), jax.ShapeDtypeStruct((1, 512), f32), jax.ShapeDtypeStruct((1, 512), f32),
                  jax.ShapeDtypeStruct((N_HEAD, GM_CHUNK, GM_CHUNK), f32), jax.ShapeDtypeStruct((8, GM_CHUNK), f32)),
                 sem=("arbitrary",))(proj, ln_g, ln_b, w_s, b_st, da)


HG_SUB = 8
HG_NSUB = HG_CHUNK // HG_SUB


def _two_level_matrix(transposed=False):
    shape = (HG_CHUNK, 2 * HG_CHUNK) if transposed else (2 * HG_CHUNK, HG_CHUNK)
    r = lax.broadcasted_iota(jnp.int32, shape, 1 if transposed else 0)
    c = lax.broadcasted_iota(jnp.int32, shape, 0 if transposed else 1)
    t = jnp.where(r < HG_CHUNK, r, r - HG_CHUNK)
    local = (r < HG_CHUNK) & (t // HG_SUB == c // HG_SUB) & (c <= t)
    before = (r >= HG_CHUNK) & (c < (t // HG_SUB) * HG_SUB)
    return (local | before).astype(f32)


def _two_level_sums(x):
    two = _sel_dot(_two_level_matrix(), x, "nn")
    return two[:HG_CHUNK], two[HG_CHUNK:]


@jax.custom_vjp
def _two_level_cumsum(x):
    return _two_level_sums(x)


_two_level_cumsum.defvjp(
    lambda x: (_two_level_sums(x), None),
    lambda _, g: (_sel_dot(_two_level_matrix(), jnp.concatenate(g, axis=0), "tn"),))


def _tile_matrix():
    s = lax.broadcasted_iota(jnp.int32, (HG_SUB, HG_CHUNK), 0)
    j = lax.broadcasted_iota(jnp.int32, (HG_SUB, HG_CHUNK), 1)
    return (j % HG_SUB == s).astype(f32)


@jax.custom_vjp
def _tile_lanes(x):
    return _sel_dot(_tile_matrix(), x, "nn", x_first=True, pieces=1)


_tile_lanes.defvjp(
    lambda x: (_sel_dot(_tile_matrix(), x, "nn", x_first=True, pieces=1), None),
    lambda _, g: (_sel_dot(_tile_matrix(), g, "nt", x_first=True, pieces=2),))


def _block_rows(x):
    k = x.shape[-1]
    return jnp.broadcast_to(x.reshape(HG_NSUB, 1, HG_SUB, k), (HG_NSUB, HG_SUB, HG_SUB, k)).reshape(HG_CHUNK, HG_SUB, k)


def _hgrn_chunk(st0, q_raw, f_raw, i_raw, g_raw, l0, l1, ng):
    C, SUB = HG_CHUNK, HG_SUB
    lb = jax.nn.sigmoid(l0 - l1)
    fg = lb + (1.0 - lb) * jax.nn.sigmoid(f_raw)
    kk = 1.0 - fg
    qf = jax.nn.silu(q_raw)
    al, base = _two_level_cumsum(jnp.log(fg))
    a = al + base
    row = lax.broadcasted_iota(jnp.int32, (C, HEAD), 0)
    a_last = jnp.sum(jnp.where(row == C - 1, a, 0.0), axis=0, keepdims=True)
    inter = _dot_nt(qf * jnp.exp(a), st0)
    qt = qf * jnp.exp(al)
    rb = lax.broadcasted_iota(jnp.int32, (C, C), 0) // SUB
    cb = lax.broadcasted_iota(jnp.int32, (C, C), 1) // SUB
    scores = jnp.zeros((C, C), f32)
    for i in range(1, HG_NSUB):
        base_i = jnp.sum(jnp.where(row == i * SUB, base, 0.0), axis=0, keepdims=True)
        kt = kk * jnp.exp(jnp.minimum(base_i - a, 0.0))
        scores = scores + jnp.where((rb == i) & (cb < i), _dot_nt(qt, kt), 0.0)
    t_i = lax.broadcasted_iota(jnp.int32, (C, SUB, HEAD), 0) % SUB
    s_i = lax.broadcasted_iota(jnp.int32, (C, SUB, HEAD), 1)
    decay = jnp.exp(jnp.where(s_i <= t_i, al[:, None, :] - _block_rows(al), -jnp.inf))
    diag = jnp.sum(qf[:, None, :] * decay * _block_rows(kk), axis=-1)
    scores = scores + jnp.where(rb == cb, _tile_lanes(diag), 0.0)
    o = inter + _dot_nn(scores, i_raw)
    st1 = jnp.exp(a_last) * st0 + _dot_tn(i_raw, kk * jnp.exp(a_last - a))
    on = o * lax.rsqrt(jnp.mean(o * o, axis=-1, keepdims=True) + EPS) * ng
    return st1, on * jax.nn.silu(g_raw)


def _hgrn_specs(S, Bl, rev):
    N = S // HG_CHUNK
    chunk = (lambda n: N - 1 - n) if rev else (lambda n: n)
    col = lambda c0: pl.BlockSpec((Bl, HG_CHUNK, 512), lambda n: (0, chunk(n), c0 // 512))
    st = pl.BlockSpec((Bl, N_HEAD, 1, HEAD, HEAD), lambda n: (0, 0, chunk(n), 0, 0))
    full = lambda *s: pl.BlockSpec(s, functools.partial(lambda n, nd: (0,) * nd, nd=len(s)))
    return N, col, st, full


def _hgrn_fwd(proj, lb_logits, ng, Bl, S):
    N, col, st, full = _hgrn_specs(S, Bl, False)

    def body(q_ref, f_ref, i_ref, g_ref, l_ref, ng_ref, o_ref, st_ref, state):
        @pl.when(pl.program_id(0) == 0)
        def _():
            state[...] = jnp.zeros_like(state)

        for b in range(Bl):
            for h in range(N_HEAD):
                sl = slice(h * HEAD, (h + 1) * HEAD)
                st0 = state[b, h]
                st_ref[b, h, 0] = st0
                st1, out = _hgrn_chunk(st0, *[r[b, :, sl].astype(f32) for r in (q_ref, f_ref, i_ref, g_ref)],
                                       l_ref[0:1, sl], l_ref[1:2, sl], ng_ref[...])
                state[b, h] = st1
                o_ref[b, :, sl] = out.astype(bf16)

    return _call(body, "hgrn_fwd", (N,), [col(C_HQ), col(C_HF), col(C_HI), col(C_HG), full(2, 512), full(1, HEAD)],
                 (col(0), st),
                 (jax.ShapeDtypeStruct((Bl, S, 512), bf16), jax.ShapeDtypeStruct((Bl, N_HEAD, N, HEAD, HEAD), f32)),
                 scratch=[pltpu.VMEM((Bl, N_HEAD, HEAD, HEAD), f32)], sem=("arbitrary",))(
        proj, proj, proj, proj, lb_logits, ng)


def _hgrn_bwd(proj, lb_logits, ng, states, db, dzuv, dxq, dgl, Bl, S):
    N, col, st, full = _hgrn_specs(S, Bl, True)
    rows = lambda width: pl.BlockSpec((Bl, HG_CHUNK, width), lambda n: (0, N - 1 - n, 0))

    def body(q_ref, f_ref, i_ref, g_ref, l_ref, ng_ref, st_ref, db_ref, dzuv_ref, dxq_ref, dgl_ref,
             dp_ref, dl_ref, dng_ref, dstate):
        @pl.when(pl.program_id(0) == 0)
        def _():
            dstate[...] = jnp.zeros_like(dstate)
            dl_ref[...] = jnp.zeros_like(dl_ref)
            dng_ref[...] = jnp.zeros_like(dng_ref)

        dp_ref[:, :, C_ZU:C_HQ] = dzuv_ref[...]
        dp_ref[:, :, C_XQ:C_GL] = dxq_ref[...]
        for n in range(3):
            dp_ref[:, :, C_GL + n * D_MODEL:C_GL + (n + 1) * D_MODEL] = dgl_ref[n]
        dq_ref, df_ref, di_ref, dg_ref = [dp_ref.at[:, :, c0:c0 + 512] for c0 in (C_HQ, C_HF, C_HI, C_HG)]
        for b in range(Bl):
            for h in range(N_HEAD):
                sl = slice(h * HEAD, (h + 1) * HEAD)
                _, vjp = jax.vjp(_hgrn_chunk, st_ref[b, h, 0], *[r[b, :, sl].astype(f32) for r in (q_ref, f_ref, i_ref, g_ref)],
                                 l_ref[0:1, sl], l_ref[1:2, sl], ng_ref[...])
                dst0, dq, df, di, dg, dl0, dl1, dng = vjp((dstate[b, h], db_ref[b, :, sl].astype(f32)))
                dstate[b, h] = dst0
                dq_ref[b, :, sl] = dq.astype(bf16)
                df_ref[b, :, sl] = df.astype(bf16)
                di_ref[b, :, sl] = di.astype(bf16)
                dg_ref[b, :, sl] = dg.astype(bf16)
                dl_ref[0:1, sl] += dl0
                dl_ref[1:2, sl] += dl1
                dng_ref[b, h] += dng

    return _call(body, "hgrn_bwd", (N,),
                 [col(C_HQ), col(C_HF), col(C_HI), col(C_HG), full(2, 512), full(1, HEAD), st, col(0), rows(C_HQ - C_ZU),
                  rows(C_GL - C_XQ), pl.BlockSpec((3, Bl, HG_CHUNK, D_MODEL), lambda n: (0, 0, N - 1 - n, 0))],
                 (rows(IN_WIDTH), full(2, 512), full(Bl, N_HEAD, 1, HEAD)),
                 (jax.ShapeDtypeStruct((Bl, S, IN_WIDTH), bf16), jax.ShapeDtypeStruct((2, 512), f32),
                  jax.ShapeDtypeStruct((Bl, N_HEAD, 1, HEAD), f32)),
                 scratch=[pltpu.VMEM((Bl, N_HEAD, HEAD, HEAD), f32)], sem=("arbitrary",))(
        proj, proj, proj, proj, lb_logits, ng, states, db, dzuv, dxq, dgl)


def _attn_probs(q, k):
    s = _raw_dot(q, k, "nt") * (HEAD ** -0.5)
    e = jnp.exp(s - jnp.max(s, axis=-1, keepdims=True))
    return e / jnp.sum(e, axis=-1, keepdims=True)


def _attn_specs(S, tq):
    nq = S // tq
    q = pl.BlockSpec((tq, 512), lambda b, i: (b * nq + i, C_XQ // 512))
    kv = pl.BlockSpec((1, MEM_LEN, 1024), lambda b, i: (b, 0, 0))
    o = pl.BlockSpec((tq, 512), lambda b, i: (b * nq + i, 0))
    return nq, q, kv, o


def _attn_fwd(proj, kv, Bl, S):
    tq = _pick(S, (512, 256, 128))
    nq, qs, kvs, os_ = _attn_specs(S, tq)

    def body(q_ref, kv_ref, o_ref):
        for h in range(N_HEAD):
            sl = slice(h * HEAD, (h + 1) * HEAD)
            p = _attn_probs(q_ref[:, sl], kv_ref[0, :, sl])
            o_ref[:, sl] = _raw_dot(p, kv_ref[0, :, 512 + h * HEAD:512 + (h + 1) * HEAD], "nn").astype(bf16)

    return _call(body, "attn_fwd", (Bl, nq), [qs, kvs], os_, jax.ShapeDtypeStruct((Bl * S, 512), bf16),
                 sem=("parallel", "parallel"))(proj, kv)


def _attn_bwd(proj, kv, dc, Bl, S):
    tq = _pick(S, (512, 256, 128))
    nq, qs, kvs, os_ = _attn_specs(S, tq)

    def body(q_ref, kv_ref, do_ref, dq_ref, dkv_ref):
        @pl.when(pl.program_id(1) == 0)
        def _():
            dkv_ref[...] = jnp.zeros_like(dkv_ref)

        for h in range(N_HEAD):
            sl = slice(h * HEAD, (h + 1) * HEAD)
            vsl = slice(512 + h * HEAD, 512 + (h + 1) * HEAD)
            q, k, v, do = q_ref[:, sl], kv_ref[0, :, sl], kv_ref[0, :, vsl], do_ref[:, sl]
            p = _attn_probs(q, k)
            dkv_ref[0, :, vsl] += _raw_dot(p, do, "tn")
            dp = _raw_dot(do, v, "nt")
            ds = p * (dp - jnp.sum(dp * p, axis=-1, keepdims=True)) * (HEAD ** -0.5)
            dq_ref[:, sl] = _raw_dot(ds, k, "nn").astype(bf16)
            dkv_ref[0, :, sl] += _raw_dot(ds, q, "tn")

    return _call(body, "attn_bwd", (Bl, nq), [qs, kvs, os_], (os_, kvs),
                 (jax.ShapeDtypeStruct((Bl * S, 512), bf16), jax.ShapeDtypeStruct((Bl, MEM_LEN, 1024), f32)),
                 sem=("arbitrary", "arbitrary"))(proj, kv, dc)


def _gate_specs(tm):
    half = D_MODEL // 2
    return [pl.BlockSpec((tm, half), functools.partial(lambda i, c: (i, c), c=(C_GL + n * D_MODEL) // half + k))
            for n in range(3) for k in range(2)]


def _merge_out_norm_fwd(branches, wb, proj, w_out, x, g):
    T = proj.shape[0]
    tm = _pick(T, (512, 256, 128))

    def body(a_ref, b_ref, c_ref, w0, w1, w2, g0a, g0b, g1a, g1b, g2a, g2b, wo_ref, x_ref, g_ref, m_ref, x1_ref, h_ref):
        acc = jnp.zeros((tm, D_MODEL), f32)
        for x_n, w_ref, ga, gb in ((a_ref, w0, g0a, g0b), (b_ref, w1, g1a, g1b), (c_ref, w2, g2a, g2b)):
            gate = jax.nn.sigmoid(jnp.concatenate([ga[...], gb[...]], axis=-1).astype(f32))
            acc = acc + gate * _raw_dot(x_n[...], w_ref[...], "nn")
        merged = acc.astype(bf16)
        m_ref[...] = merged
        x1 = x_ref[...] + _raw_dot(merged, wo_ref[...], "nn")
        x1_ref[...] = x1
        y = x1 * lax.rsqrt(jnp.mean(x1 * x1, axis=-1, keepdims=True) + EPS) * g_ref[...]
        h_ref[...] = y.astype(bf16)

    br = pl.BlockSpec((tm, 512), lambda i: (i, 0))
    w = pl.BlockSpec((512, D_MODEL), lambda i: (0, 0))
    row = pl.BlockSpec((tm, D_MODEL), lambda i: (i, 0))
    return _call(body, "merge_out_norm_fwd", (T // tm,),
                 [br, br, br, w, w, w, *_gate_specs(tm), pl.BlockSpec((D_MODEL, D_MODEL), lambda i: (0, 0)), row,
                  pl.BlockSpec((1, D_MODEL), lambda i: (0, 0))],
                 (row, row, row),
                 (jax.ShapeDtypeStruct((T, D_MODEL), bf16), jax.ShapeDtypeStruct((T, D_MODEL), f32),
                  jax.ShapeDtypeStruct((T, D_MODEL), bf16)),
                 sem=("parallel",))(*branches, *wb, *[proj] * 6, w_out, x, g)


def _merge_bwd(branches, wb, proj, merged, dx1, w_out):
    T = proj.shape[0]
    tm = _pick(T, (256, 128))

    def body(a_ref, b_ref, c_ref, w0, w1, w2, g0a, g0b, g1a, g1b, g2a, g2b, m_ref, dx_ref, wo_ref, dgl_ref, d0, d1, d2, gw_ref, gwo_ref):
        @pl.when(pl.program_id(0) == 0)
        def _():
            gw_ref[...] = jnp.zeros_like(gw_ref)
            gwo_ref[...] = jnp.zeros_like(gwo_ref)

        dx = dx_ref[...].astype(bf16)
        gwo_ref[...] += _raw_dot(m_ref[...], dx, "tn")
        dm = _raw_dot(dx, wo_ref[...], "nt")
        for n, (x_ref, w_ref, ga, gb, d_ref) in enumerate(((a_ref, w0, g0a, g0b, d0), (b_ref, w1, g1a, g1b, d1), (c_ref, w2, g2a, g2b, d2))):
            x, w = x_ref[...], w_ref[...]
            up = _raw_dot(x, w, "nn")
            sg = jax.nn.sigmoid(jnp.concatenate([ga[...], gb[...]], axis=-1).astype(f32))
            dgl_ref[n] = (dm * up * sg * (1.0 - sg)).astype(bf16)
            dup = (dm * sg).astype(bf16)
            d_ref[...] = _raw_dot(dup, w, "nt").astype(bf16)
            gw_ref[n] += _raw_dot(x, dup, "tn")

    br = pl.BlockSpec((tm, 512), lambda i: (i, 0))
    w = pl.BlockSpec((512, D_MODEL), lambda i: (0, 0))
    sh = jax.ShapeDtypeStruct((T, 512), bf16)
    row = pl.BlockSpec((tm, D_MODEL), lambda i: (i, 0))
    square = pl.BlockSpec((D_MODEL, D_MODEL), lambda i: (0, 0))
    outs = _call(body, "merge_bwd", (T // tm,), [br, br, br, w, w, w, *_gate_specs(tm), row, row, square],
                 (pl.BlockSpec((3, tm, D_MODEL), lambda i: (0, i, 0)), br, br, br, pl.BlockSpec((3, 512, D_MODEL), lambda i: (0, 0, 0)), square),
                 (jax.ShapeDtypeStruct((3, T, D_MODEL), bf16), sh, sh, sh, jax.ShapeDtypeStruct((3, 512, D_MODEL), f32),
                  jax.ShapeDtypeStruct((D_MODEL, D_MODEL), f32)),
                 sem=("arbitrary",))(*branches, *wb, *[proj] * 6, merged, dx1, w_out)
    return outs[0], outs[1:4], outs[4], outs[5]


CONV_TC = 256


def _shift_down(a, k):
    r = pltpu.roll(a, k, 0)
    row = lax.broadcasted_iota(jnp.int32, (8, a.shape[1]), 0)
    return jnp.concatenate([jnp.where(row >= k, r[:8], 0.0), r[8:]], axis=0)


CONV_ROWS = 512


def _shift_up(a, k):
    n = a.shape[0]
    r = pltpu.roll(a, n - k, 0)
    row = lax.broadcasted_iota(jnp.int32, (8, a.shape[1]), 0)
    return jnp.concatenate([r[:n - 8], jnp.where(row < 8 - k, r[n - 8:], 0.0)], axis=0)


def _conv_pre(a, a1, a2, cw, cb):
    return cb + cw[0:1] * a2 + cw[1:2] * a1 + cw[2:3] * a


def _up_conv_fwd(h2, w_up_t, cw, cb):
    Bl, S, Dd = h2.shape
    nc = D_FF // CONV_TC

    rows = min(CONV_ROWS, S)

    def body(h_ref, wa_ref, wb_ref, cw_ref, cb_ref, a_ref, b_ref, o_ref):
        for lo in range(0, S, rows):
            halo = 0 if lo == 0 else 16
            a16 = _raw_dot(h_ref[0, lo - halo:lo + rows], wa_ref[...], "nt").astype(bf16)
            b16 = _raw_dot(h_ref[0, lo:lo + rows], wb_ref[...], "nt").astype(bf16)
            a = a16.astype(f32)
            ac = _conv_pre(a, _shift_down(a, 1), _shift_down(a, 2), cw_ref[...], cb_ref[...])[halo:]
            a_ref[0, lo:lo + rows], b_ref[0, lo:lo + rows] = a16[halo:], b16
            o_ref[0, lo:lo + rows] = (jax.nn.silu(ac) * b16.astype(f32)).astype(bf16)

    seq = pl.BlockSpec((1, S, CONV_TC), lambda b, c: (b, 0, c))
    sh = jax.ShapeDtypeStruct((Bl, S, D_FF), bf16)
    return _call(body, "up_conv_fwd", (Bl, nc),
                 [pl.BlockSpec((1, S, Dd), lambda b, c: (b, 0, 0)), pl.BlockSpec((CONV_TC, Dd), lambda b, c: (c, 0)),
                  pl.BlockSpec((CONV_TC, Dd), lambda b, c: (nc + c, 0)), pl.BlockSpec((3, CONV_TC), lambda b, c: (0, c)),
                  pl.BlockSpec((1, CONV_TC), lambda b, c: (0, c))],
                 (seq, seq, seq), (sh, sh, sh), sem=("parallel", "parallel"))(h2, w_up_t, w_up_t, cw, cb)


def _down_conv_bwd(dx2, w_down, a, b, cw, cb):
    Bl, S, Dd = dx2.shape
    nc = D_FF // CONV_TC

    def body(dx_ref, wd_ref, a_ref, b_ref, cw_ref, cb_ref, da_ref, db_ref, dcw_ref, dcb_ref):
        dact = _raw_dot(dx_ref[0], wd_ref[...], "nt").astype(bf16).astype(f32)
        a, cw = a_ref[0].astype(f32), cw_ref[...]
        a1, a2 = _shift_down(a, 1), _shift_down(a, 2)
        ac = _conv_pre(a, a1, a2, cw, cb_ref[...])
        sg = jax.nn.sigmoid(ac)
        gated = dact * sg
        db_ref[0] = (gated * ac).astype(bf16)
        dac = gated * b_ref[0].astype(f32) * (1.0 + ac * (1.0 - sg))
        da_ref[0] = (cw[2:3] * dac + cw[1:2] * _shift_up(dac, 1) + cw[0:1] * _shift_up(dac, 2)).astype(bf16)
        dcw_ref[0, 0:1, :] = jnp.sum(dac * a2, axis=0, keepdims=True)
        dcw_ref[0, 1:2, :] = jnp.sum(dac * a1, axis=0, keepdims=True)
        dcw_ref[0, 2:3, :] = jnp.sum(dac * a, axis=0, keepdims=True)
        dcb_ref[0] = jnp.sum(dac, axis=0, keepdims=True)

    seq = pl.BlockSpec((1, S, CONV_TC), lambda b_, c: (b_, 0, c))
    sh = jax.ShapeDtypeStruct((Bl, S, D_FF), bf16)
    return _call(body, "down_conv_bwd", (Bl, nc),
                 [pl.BlockSpec((1, S, Dd), lambda b_, c: (b_, 0, 0)), pl.BlockSpec((CONV_TC, Dd), lambda b_, c: (c, 0)), seq, seq,
                  pl.BlockSpec((3, CONV_TC), lambda b_, c: (0, c)), pl.BlockSpec((1, CONV_TC), lambda b_, c: (0, c))],
                 (seq, seq, pl.BlockSpec((1, 3, CONV_TC), lambda b_, c: (b_, 0, c)), pl.BlockSpec((1, 1, CONV_TC), lambda b_, c: (b_, 0, c))),
                 (sh, sh, jax.ShapeDtypeStruct((Bl, 3, D_FF), f32), jax.ShapeDtypeStruct((Bl, 1, D_FF), f32)),
                 sem=("parallel", "parallel"))(dx2, w_down, a, b, cw, cb)


def _local_step(x, mem, target, p, w_in_t, late_b, late_c, send):
    Bl, S, Dd = x.shape
    T = Bl * S
    x2d, t2d, mem2d = x.reshape(T, Dd), target.reshape(T, Dd), mem.reshape(Bl * MEM_LEN, Dd)
    b_st = jnp.pad(p["b_spatial"].T, ((0, 0), (0, 128 - N_HEAD)))
    lbl = p["lb_logits"]

    proj, h = _norm_proj_fwd(x2d, p["norm1_g"], w_in_t, 1024, 1664)
    a_out = _gmlp_fwd(proj, p["ln_v_g"], p["ln_v_b"], p["w_spatial"], b_st)
    proj3 = proj.reshape(Bl, S, IN_WIDTH)
    b_out, states = _hgrn_fwd(proj3, lbl, p["hgrn_norm_g"], Bl, S)
    b_out = b_out.reshape(T, 512)
    memn = _rms_fwd(mem2d, p["mem_norm_g"], "memnorm_fwd")
    w = late_b(b_out)
    wb = w["w_branch"]
    kv = _mm(memn, w["w_mem_kv"], "nn", f32, "kv_fwd", 512, 1024).reshape(Bl, MEM_LEN, 2 * 512)
    c_out = _attn_fwd(proj, kv, Bl, S)
    branches = (a_out, b_out, c_out)
    merged, x1, h2 = _merge_out_norm_fwd(branches, wb, proj, w["w_out"], x2d, p["norm2_g"])
    w.update(late_c(h2))
    ffn_a, ffn_b, act = _up_conv_fwd(h2.reshape(Bl, S, Dd), w["w_up_t"], w["conv_w"], p["conv_b"])
    act = act.reshape(T, D_FF)
    loss_part, dx2, dx2_16, g_final = _down_final_loss(act, w["w_down"], x1, p["final_g"], t2d)

    g_w_down = _mm(act, dx2_16, "tn", bf16, "down_dw", 1408, 1024, 1024)
    da, db, g_conv_w, g_conv_b = _down_conv_bwd(dx2_16.reshape(Bl, S, Dd), w["w_down"], ffn_a, ffn_b, w["conv_w"], p["conv_b"])
    da, db = da.reshape(T, D_FF), db.reshape(T, D_FF)
    g_w_up_t = _mm(da, h2, "tn", bf16, "up_dw_a", 1408, 1024, 1024, into=(lax.empty((2 * D_FF, D_MODEL), bf16), 0))
    g_w_up_t = _mm(db, h2, "tn", bf16, "up_dw_b", 1408, 1024, 1024, into=(g_w_up_t, D_FF))
    send("c", dict(w_up=g_w_up_t, conv_w=jnp.sum(g_conv_w, axis=0), w_down=g_w_down))
    dx1, g_norm2 = _mm_rms_bwd([(da, w["w_up_t"], 0), (db, w["w_up_t"], 1)], x1, p["norm2_g"], dx2, "up_dx_norm2_bwd", 512)

    dgl, dbr, g_w_branch, g_w_out = _merge_bwd(branches, wb, proj, merged, dx1, w["w_out"])
    dxq, dkv = _attn_bwd(proj, kv, dbr[2], Bl, S)
    dkv = dkv.reshape(Bl * MEM_LEN, 2 * 512)
    g_w_kv = _mm(memn, dkv, "tn", bf16, "kv_dw", 1024, 1024, 512)
    send("b", dict(w_mem_kv=g_w_kv, w_branch=g_w_branch, w_out=g_w_out))
    dzuv, g_ln_g, g_ln_b, g_w_sp, g_b_sp = _gmlp_bwd(proj, p["ln_v_g"], p["ln_v_b"], p["w_spatial"], b_st, dbr[0])
    dproj, g_lbl, g_ng = _hgrn_bwd(proj3, lbl, p["hgrn_norm_g"], states, dbr[1].reshape(Bl, S, 512), dzuv.reshape(Bl, S, -1),
                                   dxq.reshape(Bl, S, -1), dgl.reshape(3, Bl, S, Dd), Bl, S)
    dproj = dproj.reshape(T, IN_WIDTH)
    half = Dd // 2
    send("a0", dict(w_in_half=_mm(dproj, h, "tn", bf16, "proj_dw_0", 512, half, n_tiles=(0, 1))))
    g_half = _mm(dproj, h, "tn", bf16, "proj_dw_1", 512, half, n_tiles=(1, 1))
    dkv, g_half = lax.optimization_barrier((dkv, g_half))
    send("a1", dict(w_in_half=g_half))
    dmemn = _mm(dkv, w["w_mem_kv"], "nt", f32, "kv_dx", 512, 1024)
    _, g_mem_norm = _rms_bwd(mem2d, p["mem_norm_g"], dmemn, "memnorm_bwd")
    dx, g_norm1 = _mm_rms_bwd([(dproj, w_in_t, 0)], x2d, p["norm1_g"], dx1, "proj_dx_norm1_bwd", 512)

    gs = dict(w_spatial=g_w_sp, norm1_g=g_norm1, mem_norm_g=g_mem_norm, norm2_g=g_norm2, final_g=g_final, lb_logits=g_lbl,
              ln_v_g=g_ln_g, ln_v_b=g_ln_b, b_spatial=g_b_sp, hgrn_norm_g=g_ng, conv_b=g_conv_b)
    return loss_part, dx.reshape(Bl, S, Dd), gs


def _coords():
    return lax.axis_index("x"), lax.axis_index("y"), lax.axis_index("c")


def _slot(dev):
    return 4 * dev[0] + 2 * dev[1] + dev[2]


def _comm_call(body, name, arrays, out_shapes, n_sem):
    n = len(arrays)
    hbm = pl.BlockSpec(memory_space=pl.ANY)
    return pl.pallas_call(
        body, name=name, out_shape=out_shapes, in_specs=[hbm] * n, out_specs=[hbm] * n,
        scratch_shapes=[pltpu.SemaphoreType.DMA((n_sem, n)), pltpu.SemaphoreType.DMA((n_sem, n)), pltpu.SemaphoreType.DMA((n,))])(*arrays)


def _all_gather(blocks, name):
    n = len(blocks)

    def body(*refs):
        x_refs, o_refs, (send_sems, recv_sems, local_sems) = refs[:n], refs[n:2 * n], refs[2 * n:]
        x, y, c = _coords()
        me, sibling = (x, y, c), (x, y, 1 - c)
        chips = [(1 - x, y), (x, 1 - y), (1 - x, 1 - y)]

        def copy(a, k, block_dev, to, from_input=False):
            dst = o_refs[a].at[_slot(block_dev)]
            return pltpu.make_async_remote_copy(src_ref=x_refs[a] if from_input else dst, dst_ref=dst, send_sem=send_sems.at[k, a],
                                                recv_sem=recv_sems.at[k, a], device_id=to, device_id_type=MESH)

        mine = [pltpu.make_async_copy(x_refs[a], o_refs[a].at[_slot(me)], local_sems.at[a]) for a in range(n)]
        first = [copy(a, 0, me, sibling, True) for a in range(n)]
        first += [copy(a, 1 + j, me, (*chip, c), True) for j, chip in enumerate(chips) for a in range(n)]
        for cp in mine + first:
            cp.start()
        passed = []
        for j, chip in enumerate(chips):
            for a in range(n):
                copy(a, 1 + j, (*chip, c), me).wait_recv()
                fwd = copy(a, 4 + j, (*chip, c), sibling)
                fwd.start()
                passed.append(fwd)
        for a in range(n):
            copy(a, 0, sibling, me).wait_recv()
        for j, chip in enumerate(chips):
            for a in range(n):
                copy(a, 4 + j, (*chip, 1 - c), me).wait_recv()
        for cp in first + passed:
            cp.wait_send()
        for cp in mine:
            cp.wait()

    return _comm_call(body, name, blocks, [jax.ShapeDtypeStruct((N_DEV,) + b.shape, b.dtype) for b in blocks], 7)


_REL = [(0, 0, 1), (0, 1, 0), (0, 1, 1), (1, 0, 0), (1, 0, 1), (1, 1, 0), (1, 1, 1)]


def _seq_exchange(arrays, gather, name, collective_id):
    n = len(arrays)
    hbm = pltpu.MemorySpace.HBM
    srcs = [jax.new_ref(a, memory_space=hbm) for a in arrays]
    lands = [jax.empty_ref(jax.ShapeDtypeStruct(((N_DEV,) + a.shape) if gather else a.shape, a.dtype), memory_space=hbm) for a in arrays]

    @pl.kernel(mesh=plsc.ScalarSubcoreMesh(axis_name="sequencer", num_cores=1), name=name,
               scratch_types=(pltpu.SemaphoreType.DMA((7, n)), pltpu.SemaphoreType.DMA((7, n)), pltpu.SemaphoreType.DMA((n,))),
               compiler_params=pltpu.CompilerParams(collective_id=collective_id))
    def launch(send, recv, local):
        x, y, c = _coords()
        me = (x, y, c)
        peers = [(x ^ dx, y ^ dy, c ^ dc) for dx, dy, dc in _REL]
        barrier = pltpu.get_barrier_semaphore()
        for peer in peers:
            pl.semaphore_signal(barrier, inc=1, device_id=peer, device_id_type=MESH)
        pl.semaphore_wait(barrier, len(peers))

        def copy(a, k, peer, arrival):
            return pltpu.make_async_remote_copy(
                src_ref=srcs[a] if gather else srcs[a].at[_slot(peer)], dst_ref=lands[a].at[_slot(peer if arrival else me)],
                send_sem=send.at[k, a], recv_sem=recv.at[k, a], device_id=peer, device_id_type=MESH)

        mine = [pltpu.make_async_copy(srcs[a] if gather else srcs[a].at[_slot(me)], lands[a].at[_slot(me)], local.at[a])
                for a in range(n)]
        out = [copy(a, k, peer, False) for a in range(n) for k, peer in enumerate(peers)]
        for cp in mine + out:
            cp.start()
        for a in range(n):
            for k, peer in enumerate(peers):
                copy(a, k, peer, True).wait_recv()
        for cp in out:
            cp.wait_send()
        for cp in mine:
            cp.wait()

    launch()
    return [land[...] for land in lands]


def _adam_math(w, g, m, v):
    m_ = ADAM_B1 * m + (1.0 - ADAM_B1) * g
    v_ = ADAM_B2 * v + (1.0 - ADAM_B2) * jnp.square(g)
    m_hat = m_ / (1.0 - ADAM_B1 ** ADAM_STEP)
    v_hat = v_ / (1.0 - ADAM_B2 ** ADAM_STEP)
    return -ADAM_LR * (m_hat / (jnp.sqrt(v_hat) + ADAM_EPS) + ADAM_WD * w), m_, v_


def _reduce_adamw(parts, w, m, v, name):
    R, L = w.shape
    tr = _pick(R, (256, 208, 176, 128, 64, 32, 16, 8))
    n = len(parts)

    def body(*refs):
        w_ref, m_ref, v_ref, g_ref, d_ref, nm_ref, nv_ref = refs[n:]
        pieces = []
        for p_ref in refs[:n]:
            g = p_ref[0].astype(f32)
            for i in range(1, N_DEV):
                g = g + p_ref[i].astype(f32)
            pieces.append(g)
        g = pieces[0] if n == 1 else jnp.concatenate(pieces, axis=-1)
        g_ref[...] = g
        d_ref[...], nm_ref[...], nv_ref[...] = _adam_math(w_ref[...], g, m_ref[...], v_ref[...])

    blk = pl.BlockSpec((tr, L), lambda i: (i, 0))
    sh = jax.ShapeDtypeStruct((R, L), f32)
    return _call(body, name, (R // tr,), [pl.BlockSpec((N_DEV, tr, q.shape[2]), lambda i: (0, i, 0)) for q in parts] + [blk, blk, blk],
                 (blk,) * 4, (sh,) * 4, sem=("parallel",))(*parts, w, m, v)


SMALL = (("w_spatial", (512, 128), 0), ("norm1_g", (1, 1024), 512), ("mem_norm_g", (1, 1024), 520), ("norm2_g", (1, 1024), 528),
         ("final_g", (1, 1024), 536), ("lb_logits", (2, 512), 544), ("ln_v_g", (1, 512), 552), ("ln_v_b", (1, 512), 556),
         ("b_spatial", (4, 128), 560), ("hgrn_norm_g", (1, 128), 564), ("conv_b", (1, 2816), 565))
LOSS_ROW, SMALL_USED, SMALL_ROWS = 587, 588, 640


def _segments(shape, base):
    r, n = shape
    per = n // 128
    return [(base + i * per + j, i, slice(j * 128, (j + 1) * 128)) for i in range(r) for j in range(per)]


def _pack_small(gs, loss_part):
    names = [n for n, _, _ in SMALL]

    def body(*refs):
        src, loss_ref, o_ref = dict(zip(names, refs[:-2])), refs[-2], refs[-1]
        o_ref[SMALL_USED:SMALL_ROWS, :] = jnp.zeros((SMALL_ROWS - SMALL_USED, 128), f32)
        o_ref[LOSS_ROW:LOSS_ROW + 1, :] = loss_ref[...]
        for name, shape, base in SMALL:
            ref = src[name]
            if name == "w_spatial":
                o_ref[base:base + 512, :] = ref[...].reshape(512, 128)
            elif name == "b_spatial":
                o_ref[base:base + 4, :] = ref[0:4, :]
            elif name == "conv_b":
                per_example = functools.reduce(lambda u, v_: u + v_, [ref[b] for b in range(ref.shape[0])])
                for row, i, sl in _segments(shape, base):
                    o_ref[row:row + 1, :] = per_example[i:i + 1, sl]
            elif name == "hgrn_norm_g":
                per_head = [ref[b, h] for b in range(ref.shape[0]) for h in range(N_HEAD)]
                o_ref[base:base + 1, :] = functools.reduce(lambda u, v_: u + v_, per_head)
            else:
                for row, i, sl in _segments(shape, base):
                    o_ref[row:row + 1, :] = ref[i:i + 1, sl]

    return pl.pallas_call(body, name="pack_small", out_shape=jax.ShapeDtypeStruct((SMALL_ROWS, 128), f32))(
        *[gs[n] for n in names], loss_part)


def _small_update(gathered, w, m, v):
    names = [n for n, _, _ in SMALL]
    k = len(names)

    def body(*refs):
        p_ref = refs[0]
        ins = [dict(zip(names, refs[1 + i * k:1 + (i + 1) * k])) for i in range(3)]
        outs = [dict(zip(names, refs[1 + (3 + i) * k:1 + (4 + i) * k])) for i in range(4)]
        loss_ref, gsum = refs[-2], refs[-1]
        g = p_ref[0]
        for i in range(1, N_DEV):
            g = g + p_ref[i]
        gsum[...] = g
        loss_ref[...] = gsum[LOSS_ROW:LOSS_ROW + 1, :]
        for name, shape, base in SMALL:
            if name == "w_spatial":
                where = [(slice(base, base + 512), (slice(None), slice(None)))]
            else:
                where = [(slice(row, row + 1), (slice(i, i + 1), sl)) for row, i, sl in _segments(shape, base)]
            for rows, at in where:
                g_ = gsum[rows, :]
                d_, m_, v_ = _adam_math(ins[0][name][at], g_, ins[1][name][at], ins[2][name][at])
                for o, val in zip(outs, (g_, d_, m_, v_)):
                    o[name][at] = val

    args = [gathered] + [d[n] for d in (w, m, v) for n in names]
    out_shapes = [jax.ShapeDtypeStruct(shape, f32) for _ in range(4) for _, shape, _ in SMALL] + [jax.ShapeDtypeStruct((1, 128), f32)]
    outs = pl.pallas_call(body, name="small_update", out_shape=out_shapes, scratch_shapes=[pltpu.VMEM((SMALL_ROWS, 128), f32)])(*args)
    return [dict(zip(names, outs[i * k:(i + 1) * k])) for i in range(4)], outs[-1]


def _cols_full(g):
    return jnp.moveaxis(g, 0, -2).reshape(g.shape[1:-1] + (N_DEV * g.shape[-1],))


def _cols_parts(full):
    n = full.shape[-1] // N_DEV
    return jnp.moveaxis(full.reshape(full.shape[:-1] + (N_DEV, n)), -2, 0)


def kernel(x, mem, norm1_g, w_in, ln_v_g, ln_v_b, w_spatial, b_spatial, lb_logits, hgrn_norm_g, mem_norm_g, w_mem_kv, w_branch, w_out, norm2_g, w_up, conv_w, conv_b, w_down, final_g, loss_target, m_norm1_g, m_w_in, m_ln_v_g, m_ln_v_b, m_w_spatial, m_b_spatial, m_lb_logits, m_hgrn_norm_g, m_mem_norm_g, m_w_mem_kv, m_w_branch, m_w_out, m_norm2_g, m_w_up, m_conv_w, m_conv_b, m_w_down, m_final_g, v_norm1_g, v_w_in, v_ln_v_g, v_ln_v_b, v_w_spatial, v_b_spatial, v_lb_logits, v_hgrn_norm_g, v_mem_norm_g, v_w_mem_kv, v_w_branch, v_w_out, v_norm2_g, v_w_up, v_conv_w, v_conv_b, v_w_down, v_final_g):
    given = dict(locals())
    order = ("norm1_g", "w_in", "ln_v_g", "ln_v_b", "w_spatial", "b_spatial", "lb_logits", "hgrn_norm_g", "mem_norm_g",
             "w_mem_kv", "w_branch", "w_out", "norm2_g", "w_up", "conv_w", "conv_b", "w_down", "final_g")
    groups = dict(a=("w_in",), b=("w_mem_kv", "w_branch", "w_out"), c=("w_up", "conv_w", "w_down"))

    by_rows = ("w_in", "w_up")
    shard_of = lambda n, prefix="": jnp.swapaxes(given[prefix + n][0], 0, 1) if n in by_rows else given[prefix + n][0]
    wire = {n: shard_of(n).astype(f32 if n == "conv_w" else bf16) for ns in groups.values() for n in ns}
    w_in_full = _all_gather([wire["w_in"]], "gather_w_in")[0].reshape(IN_WIDTH, D_MODEL)
    w_in_full, wire_b, wire_c = lax.optimization_barrier((w_in_full, [wire[n] for n in groups["b"]], [wire[n] for n in groups["c"]]))
    rest_b = _seq_exchange(wire_b, True, "gather_b", 1)
    rest_c = _seq_exchange(wire_c, True, "gather_c", 6)

    def late_b(after):
        _, (kv_, br_, out_) = lax.optimization_barrier((after, tuple(rest_b)))
        br_ = _cols_full(br_)
        return dict(w_mem_kv=kv_.reshape(D_MODEL, 2 * 512), w_branch=[br_[n] for n in range(3)], w_out=out_.reshape(D_MODEL, D_MODEL))

    def late_c(after):
        _, (up_, cw_, down_) = lax.optimization_barrier((after, tuple(rest_c)))
        return dict(w_up_t=up_.reshape(2 * D_FF, D_MODEL), conv_w=_cols_full(cw_), w_down=down_.reshape(D_FF, D_MODEL))

    to_parts = dict(w_in_half=lambda g_: g_.reshape(N_DEV, -1, D_MODEL // 2), w_up=lambda g_: g_.reshape(N_DEV, -1, D_MODEL), conv_w=_cols_parts,
                    w_branch=lambda g_: _cols_parts(g_.astype(bf16)).reshape(N_DEV, -1, 128),
                    w_mem_kv=lambda g_: g_.reshape(N_DEV, -1, 2 * 512), w_out=lambda g_: g_.astype(bf16).reshape(N_DEV, -1, D_MODEL),
                    w_down=lambda g_: g_.reshape(N_DEV, -1, D_MODEL))
    scatters = {}


    def send(tag, grads_):
        parts = [to_parts[n](g_) for n, g_ in grads_.items()]
        scatters[tag] = _seq_exchange(parts, False, f"scatter_{tag}", dict(a0=2, a1=7, b=4, c=5)[tag])

    small_2d = lambda prefix: {n: given[prefix + n].reshape(shape) for n, shape, _ in SMALL}
    p = small_2d("")
    p["w_spatial"] = w_spatial[0]
    updates = {}

    def update(tag):
        arrived = [scatters["a0"] + scatters["a1"]] if tag == "a" else [[parts] for parts in scatters[tag]]
        for n, parts in zip(groups[tag], arrived):
            state = [shard_of(n, pre) for pre in ("", "m_", "v_")]
            res = _reduce_adamw(parts, *[a.reshape(-1, a.shape[-1]) for a in state], "adamw_" + n)
            updates[n] = [jnp.swapaxes(r, 0, 1) for r in res] if n in by_rows else res

    loss_part, grad_x, gs = _local_step(x, mem, loss_target, p, w_in_full, late_b, late_c, send)

    packed, (scatters["c"], scatters["b"]) = lax.optimization_barrier((_pack_small(gs, loss_part), (scatters["c"], scatters["b"])))
    gathered = _seq_exchange([packed], True, "gather_small", 3)[0]

    update("c")
    update("b")
    update("a")
    grads, delta, new_m, new_v = {}, {}, {}, {}
    for n, res in updates.items():
        grads[n], delta[n], new_m[n], new_v[n] = [r.reshape(given[n].shape) for r in res]

    small_results, loss_row = _small_update(gathered, small_2d(""), small_2d("m_"), small_2d("v_"))
    for dst, res in zip((grads, delta, new_m, new_v), small_results):
        for n, _, _ in SMALL:
            dst[n] = res[n].reshape(given[n].shape)
    loss = loss_row[0, 0]

    return (loss, grad_x, *[grads[n] for n in order], *[delta[n] for n in order], *[new_m[n] for n in order],
            *[new_v[n] for n in order])
```

```python
import functools

import jax
import jax.numpy as jnp
from jax import lax
from jax.experimental import pallas as pl
from jax.experimental.pallas import tpu as pltpu
from jax.experimental.pallas import tpu_sc as plsc

f32 = jnp.float32
bf16 = jnp.bfloat16

N_DEV = 8
D_MODEL = 1024
EPS = 1e-6
GM_CHUNK = 128
HG_CHUNK = 64
HEAD = 128
N_HEAD = 4
MEM_LEN = 256
D_FF = 2816
IN_WIDTH = 6656
C_ZU, C_HQ, C_HF, C_HI, C_HG, C_XQ, C_GL = 0, 1024, 1536, 2048, 2560, 3072, 3584
ADAM_LR, ADAM_B1, ADAM_B2, ADAM_EPS, ADAM_WD, ADAM_STEP = 0.001, 0.9, 0.999, 1e-08, 0.01, 10
VMEM_LIMIT = 56 * 1024 * 1024
MESH = pl.DeviceIdType.MESH


def _pick(n, cands):
    for c in cands:
        if n % c == 0:
            return c
    return n


def _call(body, name, grid, in_specs, out_specs, out_shape, scratch=(), sem=None, **cp):
    params = dict(vmem_limit_bytes=VMEM_LIMIT, **cp)
    if sem is not None:
        params["dimension_semantics"] = sem
    return pl.pallas_call(
        body, name=name, grid=grid, in_specs=in_specs, out_specs=out_specs, out_shape=out_shape,
        scratch_shapes=list(scratch), compiler_params=pltpu.CompilerParams(**params))


_DN = {"nn": (((1,), (0,)), ((), ())), "nt": (((1,), (1,)), ((), ())), "tn": (((0,), (0,)), ((), ()))}


def _raw_dot(a, b, mode):
    return lax.dot_general(a.astype(bf16), b.astype(bf16), _DN[mode], preferred_element_type=f32)


@jax.custom_vjp
def _dot_nn(a, b):
    return _raw_dot(a, b, "nn")


_dot_nn.defvjp(lambda a, b: (_raw_dot(a, b, "nn"), (a, b)),
               lambda r, g: (_raw_dot(g, r[1], "nt"), _raw_dot(r[0], g, "tn")))


@jax.custom_vjp
def _dot_nt(a, b):
    return _raw_dot(a, b, "nt")


_dot_nt.defvjp(lambda a, b: (_raw_dot(a, b, "nt"), (a, b)),
               lambda r, g: (_raw_dot(g, r[1], "nn"), _raw_dot(g, r[0], "tn")))


@jax.custom_vjp
def _dot_tn(a, b):
    return _raw_dot(a, b, "tn")


_dot_tn.defvjp(lambda a, b: (_raw_dot(a, b, "tn"), (a, b)),
               lambda r, g: (_raw_dot(r[1], g, "nt"), _raw_dot(r[0], g, "nn")))


def _tri(n, lower):
    r = lax.broadcasted_iota(jnp.int32, (n, n), 0)
    c = lax.broadcasted_iota(jnp.int32, (n, n), 1)
    return ((c <= r) if lower else (c >= r)).astype(f32)


def _sel_dot(sel, x, mode, x_first=False, pieces=3):
    sel = sel.astype(bf16)
    out, rest = None, x
    for p in range(pieces):
        piece = rest.astype(bf16)
        part = lax.dot_general(*((piece, sel) if x_first else (sel, piece)), _DN[mode], preferred_element_type=f32)
        out = part if out is None else out + part
        if p + 1 < pieces:
            rest = rest - piece.astype(f32)
    return out


def _egrad(fn, x, ct):
    return jax.vjp(fn, x)[1](ct)[0]


def _mm(a, b, mode, out_dtype, name, tm, tn, tk=None, residual=None, into=None, n_tiles=None):
    if mode == "nn":
        (M, K), (_, N) = a.shape, b.shape
    elif mode == "nt":
        (M, K), (N, _) = a.shape, b.shape
    else:
        (K, M), (_, N) = a.shape, b.shape
    j0 = 0
    if n_tiles is not None:
        assert mode != "nt" and N % tn == 0 and residual is None
        j0, N = n_tiles[0], n_tiles[1] * tn
    tm, tn = min(tm, M), min(tn, N)
    tk = K if tk is None else min(tk, K)
    assert M % tm == 0 and N % tn == 0 and K % tk == 0, (name, M, N, K, tm, tn, tk)
    nk = K // tk

    def body(*refs):
        acc_ref = refs[-1] if nk > 1 else None
        refs = refs[:-1] if nk > 1 else refs
        if residual is None:
            a_ref, b_ref, *_, o_ref = refs
        else:
            a_ref, b_ref, r_ref, o_ref = refs

        def finish(r):
            if residual is not None:
                r = r + r_ref[...]
            o_ref[...] = r.astype(out_dtype)

        part = _raw_dot(a_ref[...], b_ref[...], mode)
        if nk == 1:
            finish(part)
            return
        k = pl.program_id(2)

        @pl.when(k == 0)
        def _():
            acc_ref[...] = part

        @pl.when((k > 0) & (k < nk - 1))
        def _():
            acc_ref[...] += part

        @pl.when(k == nk - 1)
        def _():
            finish(acc_ref[...] + part)

    a_spec = {"nn": pl.BlockSpec((tm, tk), lambda i, j, k: (i, k)),
              "nt": pl.BlockSpec((tm, tk), lambda i, j, k: (i, k)),
              "tn": pl.BlockSpec((tk, tm), lambda i, j, k: (k, i))}[mode]
    b_spec = {"nn": pl.BlockSpec((tk, tn), lambda i, j, k: (k, j + j0)),
              "nt": pl.BlockSpec((tn, tk), lambda i, j, k: (j, k)),
              "tn": pl.BlockSpec((tk, tn), lambda i, j, k: (k, j + j0))}[mode]
    o_spec = pl.BlockSpec((tm, tn), lambda i, j, k: (i, j))
    in_specs = [a_spec, b_spec] + ([o_spec] if residual is not None else [])
    args = (a, b) + ((residual,) if residual is not None else ())
    out_shape = jax.ShapeDtypeStruct((M, N), out_dtype)
    extra = {}
    if into is not None:
        assert residual is None and into[1] % tm == 0
        o_spec = pl.BlockSpec((tm, tn), lambda i, j, k: (i + into[1] // tm, j))
        out_shape = jax.ShapeDtypeStruct(into[0].shape, out_dtype)
        in_specs, args = in_specs + [pl.BlockSpec(memory_space=pl.ANY)], args + (into[0],)
        extra = dict(input_output_aliases={2: 0})
    return pl.pallas_call(
        body, name=name, grid=(M // tm, N // tn, nk), in_specs=in_specs, out_specs=o_spec, out_shape=out_shape,
        scratch_shapes=[pltpu.VMEM((tm, tn), f32)] if nk > 1 else [],
        compiler_params=pltpu.CompilerParams(vmem_limit_bytes=VMEM_LIMIT, dimension_semantics=("parallel", "parallel", "arbitrary")),
        **extra)(*args)


def _rms_fwd(x, g, name):
    R, Dd = x.shape
    tr = _pick(R, (512, 256, 128))

    def body(x_ref, g_ref, o_ref):
        xf = x_ref[...]
        o_ref[...] = (xf * lax.rsqrt(jnp.mean(xf * xf, axis=-1, keepdims=True) + EPS) * g_ref[...]).astype(bf16)

    row = pl.BlockSpec((tr, Dd), lambda i: (i, 0))
    return _call(body, name, (R // tr,), [row, pl.BlockSpec((1, Dd), lambda i: (0, 0))], row, jax.ShapeDtypeStruct((R, Dd), bf16),
                 sem=("parallel",))(x, g)


def _norm_proj_fwd(x, g, w_t, tm, tn):
    R, Dd = x.shape
    N = w_t.shape[0]
    tm = min(tm, R)
    assert R % tm == 0 and N % tn == 0

    def body(x_ref, g_ref, w_ref, p_ref, h_ref):
        @pl.when(pl.program_id(1) == 0)
        def _():
            xf = x_ref[...]
            h_ref[...] = (xf * lax.rsqrt(jnp.mean(xf * xf, axis=-1, keepdims=True) + EPS) * g_ref[...]).astype(bf16)

        p_ref[...] = _raw_dot(h_ref[...], w_ref[...], "nt").astype(bf16)

    row = pl.BlockSpec((tm, Dd), lambda i, j: (i, 0))
    return _call(body, "norm1_proj_fwd", (R // tm, N // tn),
                 [row, pl.BlockSpec((1, Dd), lambda i, j: (0, 0)), pl.BlockSpec((tn, Dd), lambda i, j: (j, 0))],
                 (pl.BlockSpec((tm, tn), lambda i, j: (i, j)), row),
                 (jax.ShapeDtypeStruct((R, N), bf16), jax.ShapeDtypeStruct((R, Dd), bf16)), sem=("parallel", "arbitrary"))(x, g, w_t)


def _rms_bwd(x, g, dh, name, residual=None):
    R, Dd = x.shape
    tr = _pick(R, (512, 256, 128))

    def body(*refs):
        if residual is None:
            x_ref, g_ref, dh_ref, dx_ref, dg_ref = refs
        else:
            x_ref, g_ref, dh_ref, r_ref, dx_ref, dg_ref = refs
        xf = x_ref[...]
        rs = lax.rsqrt(jnp.mean(xf * xf, axis=-1, keepdims=True) + EPS)
        y = xf * rs
        dh_ = dh_ref[...].astype(f32)
        dy = dh_ * g_ref[...]
        dx = rs * (dy - y * jnp.mean(dy * y, axis=-1, keepdims=True))
        if residual is not None:
            dx = dx + r_ref[...]
        dx_ref[...] = dx

        @pl.when(pl.program_id(0) == 0)
        def _():
            dg_ref[...] = jnp.zeros_like(dg_ref)

        dg_ref[...] += jnp.sum(dh_ * y, axis=0, keepdims=True)

    row = pl.BlockSpec((tr, Dd), lambda i: (i, 0))
    vec = pl.BlockSpec((1, Dd), lambda i: (0, 0))
    in_specs = [row, vec, row] + ([row] if residual is not None else [])
    args = (x, g, dh) + ((residual,) if residual is not None else ())
    return _call(body, name, (R // tr,), in_specs, (row, vec),
                 (jax.ShapeDtypeStruct((R, Dd), f32), jax.ShapeDtypeStruct((1, Dd), f32)), sem=("arbitrary",))(*args)


def _mm_rms_bwd(pairs, x, g, residual, name, tm):
    M = x.shape[0]
    Dd = x.shape[1]
    tm = min(tm, M)
    n = len(pairs)

    def body(*refs):
        ab_refs, (x_ref, g_ref, r_ref, dx_ref, dg_ref) = refs[:2 * n], refs[2 * n:]
        dh_ = _raw_dot(ab_refs[0][...], ab_refs[1][...], "nn")
        for k in range(1, n):
            dh_ = dh_ + _raw_dot(ab_refs[2 * k][...], ab_refs[2 * k + 1][...], "nn")
        xf = x_ref[...]
        rs = lax.rsqrt(jnp.mean(xf * xf, axis=-1, keepdims=True) + EPS)
        y = xf * rs
        dy = dh_ * g_ref[...]
        dx_ref[...] = rs * (dy - y * jnp.mean(dy * y, axis=-1, keepdims=True)) + r_ref[...]

        @pl.when(pl.program_id(0) == 0)
        def _():
            dg_ref[...] = jnp.zeros_like(dg_ref)

        dg_ref[...] += jnp.sum(dh_ * y, axis=0, keepdims=True)

    row = pl.BlockSpec((tm, Dd), lambda i: (i, 0))
    vec = pl.BlockSpec((1, Dd), lambda i: (0, 0))
    in_specs, args = [], []
    for a, b, k in pairs:
        in_specs += [pl.BlockSpec((tm, a.shape[1]), lambda i: (i, 0)),
                     pl.BlockSpec((a.shape[1], b.shape[1]), functools.partial(lambda i, k_: (k_, 0), k_=k),
                                  pipeline_mode=pl.Buffered(1))]
        args += [a, b]
    in_specs += [row, vec, row]
    args += [x, g, residual]
    return _call(body, name, (M // tm,), in_specs, (row, vec),
                 (jax.ShapeDtypeStruct((M, Dd), f32), jax.ShapeDtypeStruct((1, Dd), f32)), sem=("arbitrary",))(*args)


def _down_final_loss(act, w_down, x1, g, target):
    R, Dd = x1.shape
    tr = _pick(R, (512, 256, 128))

    def body(a_ref, w_ref, x1_ref, g_ref, t_ref, loss_ref, dx_ref, dxb_ref, dg_ref):
        xf = _raw_dot(a_ref[...], w_ref[...], "nn") + x1_ref[...]
        rs = lax.rsqrt(jnp.mean(xf * xf, axis=-1, keepdims=True) + EPS)
        y = xf * rs
        err = y * g_ref[...] - t_ref[...]
        dh_ = err * (1.0 / Dd)
        dy = dh_ * g_ref[...]
        dx = rs * (dy - y * jnp.mean(dy * y, axis=-1, keepdims=True))
        dx_ref[...] = dx
        dxb_ref[...] = dx.astype(bf16)

        @pl.when(pl.program_id(0) == 0)
        def _():
            dg_ref[...] = jnp.zeros_like(dg_ref)
            loss_ref[...] = jnp.zeros_like(loss_ref)

        dg_ref[...] += jnp.sum(dh_ * y, axis=0, keepdims=True)
        part = jnp.sum(jnp.mean(err * err, axis=-1, keepdims=True), axis=0, keepdims=True)
        loss_ref[...] += 0.5 * part

    row = pl.BlockSpec((tr, Dd), lambda i: (i, 0))
    vec = pl.BlockSpec((1, Dd), lambda i: (0, 0))
    in_specs = [pl.BlockSpec((tr, act.shape[1]), lambda i: (i, 0)), pl.BlockSpec(w_down.shape, lambda i: (0, 0)), row, vec, row]
    return _call(body, "down_final_loss", (R // tr,), in_specs, (pl.BlockSpec((1, 128), lambda i: (0, 0)), row, row, vec),
                 (jax.ShapeDtypeStruct((1, 128), f32), jax.ShapeDtypeStruct((R, Dd), f32), jax.ShapeDtypeStruct((R, Dd), bf16),
                  jax.ShapeDtypeStruct((1, Dd), f32)), sem=("arbitrary",))(act, w_down, x1, g, target)


def _gmlp_parts(zuv, ln_g, ln_b):
    zu, zv = zuv[:, :512], zuv[:, 512:]
    u = jax.nn.gelu(zu)
    v = jax.nn.gelu(zv)
    mu = jnp.mean(v, axis=-1, keepdims=True)
    rs = lax.rsqrt(jnp.mean(jnp.square(v - mu), axis=-1, keepdims=True) + EPS)
    xh = (v - mu) * rs
    return zu, zv, u, xh, rs, xh * ln_g + ln_b


GM_TILE_CHUNKS = 4


def _gmlp_tile(T):
    n = _pick(T // GM_CHUNK, (GM_TILE_CHUNKS, 2, 1))
    return n, n * GM_CHUNK


def _gmlp_fwd(proj, ln_g, ln_b, w_s, b_st):
    T = proj.shape[0]
    nch, rows = _gmlp_tile(T)

    def body(p_ref, g_ref, b_ref, w_ref, bs_ref, o_ref):
        _, _, u, _, _, vn = _gmlp_parts(p_ref[...].astype(f32), g_ref[...], b_ref[...])
        causal = _tri(GM_CHUNK, True) > 0
        for gi in range(N_HEAD):
            sl = slice(gi * HEAD, (gi + 1) * HEAD)
            w = jnp.where(causal, w_ref[gi], 0.0)
            for ch in range(nch):
                rs_ = slice(ch * GM_CHUNK, (ch + 1) * GM_CHUNK)
                mixed = _raw_dot(w, vn[rs_, sl], "nn") + bs_ref[:, gi:gi + 1]
                o_ref[rs_, sl] = (u[rs_, sl] * mixed).astype(bf16)

    vec = pl.BlockSpec((1, 512), lambda i: (0, 0))
    return _call(body, "gmlp_fwd", (T // rows,),
                 [pl.BlockSpec((rows, 1024), lambda i: (i, 0)), vec, vec,
                  pl.BlockSpec((N_HEAD, GM_CHUNK, GM_CHUNK), lambda i: (0, 0, 0)), pl.BlockSpec((GM_CHUNK, 128), lambda i: (0, 0))],
                 pl.BlockSpec((rows, 512), lambda i: (i, 0)), jax.ShapeDtypeStruct((T, 512), bf16), sem=("parallel",))(
        proj, ln_g, ln_b, w_s, b_st)


def _gmlp_bwd(proj, ln_g, ln_b, w_s, b_st, da):
    T = proj.shape[0]
    nch, rows = _gmlp_tile(T)

    def body(p_ref, g_ref, b_ref, w_ref, bs_ref, da_ref, dp_ref, dg_ref, db_ref, dw_ref, dbs_ref):
        zu, zv, u, xh, rs, vn = _gmlp_parts(p_ref[...].astype(f32), g_ref[...], b_ref[...])
        causal = _tri(GM_CHUNK, True) > 0
        sub = lax.broadcasted_iota(jnp.int32, (8, GM_CHUNK), 0)
        ones = jnp.ones((8, HEAD), f32)
        dout = da_ref[...].astype(f32)

        @pl.when(pl.program_id(0) == 0)
        def _():
            for r in (dg_ref, db_ref, dw_ref, dbs_ref):
                r[...] = jnp.zeros_like(r)

        du, dvn, dbs = [], [], jnp.zeros((8, GM_CHUNK), f32)
        for gi in range(N_HEAD):
            sl = slice(gi * HEAD, (gi + 1) * HEAD)
            w = jnp.where(causal, w_ref[gi], 0.0)
            du_g, dvn_g, dw_g = [], [], jnp.zeros((GM_CHUNK, GM_CHUNK), f32)
            for ch in range(nch):
                rs_ = slice(ch * GM_CHUNK, (ch + 1) * GM_CHUNK)
                mixed = _raw_dot(w, vn[rs_, sl], "nn") + bs_ref[:, gi:gi + 1]
                du_g.append(dout[rs_, sl] * mixed)
                dm = dout[rs_, sl] * u[rs_, sl]
                dbs = dbs + jnp.where(sub == gi, _sel_dot(ones, dm, "nt"), 0.0)
                dw_g = dw_g + _raw_dot(dm, vn[rs_, sl], "nt")
                dvn_g.append(_raw_dot(w, dm, "tn"))
            dw_ref[gi] += jnp.where(causal, dw_g, 0.0)
            du.append(jnp.concatenate(du_g, axis=0))
            dvn.append(jnp.concatenate(dvn_g, axis=0))
        dbs_ref[...] += dbs
        du = jnp.concatenate(du, axis=-1)
        dvn = jnp.concatenate(dvn, axis=-1)
        dg_ref[...] += jnp.sum(dvn * xh, axis=0, keepdims=True)
        db_ref[...] += jnp.sum(dvn, axis=0, keepdims=True)
        dxh = dvn * g_ref[...]
        dv = rs * (dxh - jnp.mean(dxh, axis=-1, keepdims=True) - xh * jnp.mean(dxh * xh, axis=-1, keepdims=True))
        dp_ref[:, :512] = _egrad(jax.nn.gelu, zu, du).astype(bf16)
        dp_ref[:, 512:] = _egrad(jax.nn.gelu, zv, dv).astype(bf16)

    vec = pl.BlockSpec((1, 512), lambda i: (0, 0))
    wsp = pl.BlockSpec((N_HEAD, GM_CHUNK, GM_CHUNK), lambda i: (0, 0, 0))
    return _call(body, "gmlp_bwd", (T // rows,),
                 [pl.BlockSpec((rows, 1024), lambda i: (i, 0)), vec, vec, wsp, pl.BlockSpec((GM_CHUNK, 128), lambda i: (0, 0)),
                  pl.BlockSpec((rows, 512), lambda i: (i, 0))],
                 (pl.BlockSpec((rows, 1024), lambda i: (i, 0)), vec, vec, wsp, pl.BlockSpec((8, GM_CHUNK), lambda i: (0, 0))),
                 (jax.ShapeDtypeStruct((T, 1024), bf16), jax.ShapeDtypeStruct((1, 512), f32), jax.ShapeDtypeStruct((1, 512), f32),
                  jax.ShapeDtypeStruct((N_HEAD, GM_CHUNK, GM_CHUNK), f32), jax.ShapeDtypeStruct((8, GM_CHUNK), f32)),
                 sem=("arbitrary",))(proj, ln_g, ln_b, w_s, b_st, da)


HG_SUB = 8
HG_NSUB = HG_CHUNK // HG_SUB


def _two_level_matrix(transposed=False):
    shape = (HG_CHUNK, 2 * HG_CHUNK) if transposed else (2 * HG_CHUNK, HG_CHUNK)
    r = lax.broadcasted_iota(jnp.int32, shape, 1 if transposed else 0)
    c = lax.broadcasted_iota(jnp.int32, shape, 0 if transposed else 1)
    t = jnp.where(r < HG_CHUNK, r, r - HG_CHUNK)
    local = (r < HG_CHUNK) & (t // HG_SUB == c // HG_SUB) & (c <= t)
    before = (r >= HG_CHUNK) & (c < (t // HG_SUB) * HG_SUB)
    return (local | before).astype(f32)


def _two_level_sums(x):
    two = _sel_dot(_two_level_matrix(), x, "nn")
    return two[:HG_CHUNK], two[HG_CHUNK:]


@jax.custom_vjp
def _two_level_cumsum(x):
    return _two_level_sums(x)


_two_level_cumsum.defvjp(
    lambda x: (_two_level_sums(x), None),
    lambda _, g: (_sel_dot(_two_level_matrix(), jnp.concatenate(g, axis=0), "tn"),))


def _tile_matrix():
    s = lax.broadcasted_iota(jnp.int32, (HG_SUB, HG_CHUNK), 0)
    j = lax.broadcasted_iota(jnp.int32, (HG_SUB, HG_CHUNK), 1)
    return (j % HG_SUB == s).astype(f32)


@jax.custom_vjp
def _tile_lanes(x):
    return _sel_dot(_tile_matrix(), x, "nn", x_first=True, pieces=1)


_tile_lanes.defvjp(
    lambda x: (_sel_dot(_tile_matrix(), x, "nn", x_first=True, pieces=1), None),
    lambda _, g: (_sel_dot(_tile_matrix(), g, "nt", x_first=True, pieces=2),))


def _block_rows(x):
    k = x.shape[-1]
    return jnp.broadcast_to(x.reshape(HG_NSUB, 1, HG_SUB, k), (HG_NSUB, HG_SUB, HG_SUB, k)).reshape(HG_CHUNK, HG_SUB, k)


def _hgrn_chunk(st0, q_raw, f_raw, i_raw, g_raw, l0, l1, ng):
    C, SUB = HG_CHUNK, HG_SUB
    lb = jax.nn.sigmoid(l0 - l1)
    fg = lb + (1.0 - lb) * jax.nn.sigmoid(f_raw)
    kk = 1.0 - fg
    qf = jax.nn.silu(q_raw)
    al, base = _two_level_cumsum(jnp.log(fg))
    a = al + base
    row = lax.broadcasted_iota(jnp.int32, (C, HEAD), 0)
    a_last = jnp.sum(jnp.where(row == C - 1, a, 0.0), axis=0, keepdims=True)
    inter = _dot_nt(qf * jnp.exp(a), st0)
    qt = qf * jnp.exp(al)
    rb = lax.broadcasted_iota(jnp.int32, (C, C), 0) // SUB
    cb = lax.broadcasted_iota(jnp.int32, (C, C), 1) // SUB
    scores = jnp.zeros((C, C), f32)
    for i in range(1, HG_NSUB):
        base_i = jnp.sum(jnp.where(row == i * SUB, base, 0.0), axis=0, keepdims=True)
        kt = kk * jnp.exp(jnp.minimum(base_i - a, 0.0))
        scores = scores + jnp.where((rb == i) & (cb < i), _dot_nt(qt, kt), 0.0)
    t_i = lax.broadcasted_iota(jnp.int32, (C, SUB, HEAD), 0) % SUB
    s_i = lax.broadcasted_iota(jnp.int32, (C, SUB, HEAD), 1)
    decay = jnp.exp(jnp.where(s_i <= t_i, al[:, None, :] - _block_rows(al), -jnp.inf))
    diag = jnp.sum(qf[:, None, :] * decay * _block_rows(kk), axis=-1)
    scores = scores + jnp.where(rb == cb, _tile_lanes(diag), 0.0)
    o = inter + _dot_nn(scores, i_raw)
    st1 = jnp.exp(a_last) * st0 + _dot_tn(i_raw, kk * jnp.exp(a_last - a))
    on = o * lax.rsqrt(jnp.mean(o * o, axis=-1, keepdims=True) + EPS) * ng
    return st1, on * jax.nn.silu(g_raw)


def _hgrn_specs(S, Bl, rev):
    N = S // HG_CHUNK
    chunk = (lambda n: N - 1 - n) if rev else (lambda n: n)
    col = lambda c0: pl.BlockSpec((Bl, HG_CHUNK, 512), lambda n: (0, chunk(n), c0 // 512))
    st = pl.BlockSpec((Bl, N_HEAD, 1, HEAD, HEAD), lambda n: (0, 0, chunk(n), 0, 0))
    full = lambda *s: pl.BlockSpec(s, functools.partial(lambda n, nd: (0,) * nd, nd=len(s)))
    return N, col, st, full


def _hgrn_fwd(proj, lb_logits, ng, Bl, S):
    N, col, st, full = _hgrn_specs(S, Bl, False)

    def body(q_ref, f_ref, i_ref, g_ref, l_ref, ng_ref, o_ref, st_ref, state):
        @pl.when(pl.program_id(0) == 0)
        def _():
            state[...] = jnp.zeros_like(state)

        for b in range(Bl):
            for h in range(N_HEAD):
                sl = slice(h * HEAD, (h + 1) * HEAD)
                st0 = state[b, h]
                st_ref[b, h, 0] = st0
                st1, out = _hgrn_chunk(st0, *[r[b, :, sl].astype(f32) for r in (q_ref, f_ref, i_ref, g_ref)],
                                       l_ref[0:1, sl], l_ref[1:2, sl], ng_ref[...])
                state[b, h] = st1
                o_ref[b, :, sl] = out.astype(bf16)

    return _call(body, "hgrn_fwd", (N,), [col(C_HQ), col(C_HF), col(C_HI), col(C_HG), full(2, 512), full(1, HEAD)],
                 (col(0), st),
                 (jax.ShapeDtypeStruct((Bl, S, 512), bf16), jax.ShapeDtypeStruct((Bl, N_HEAD, N, HEAD, HEAD), f32)),
                 scratch=[pltpu.VMEM((Bl, N_HEAD, HEAD, HEAD), f32)], sem=("arbitrary",))(
        proj, proj, proj, proj, lb_logits, ng)


def _hgrn_bwd(proj, lb_logits, ng, states, db, dzuv, dxq, dgl, Bl, S):
    N, col, st, full = _hgrn_specs(S, Bl, True)
    rows = lambda width: pl.BlockSpec((Bl, HG_CHUNK, width), lambda n: (0, N - 1 - n, 0))

    def body(q_ref, f_ref, i_ref, g_ref, l_ref, ng_ref, st_ref, db_ref, dzuv_ref, dxq_ref, dgl_ref,
             dp_ref, dl_ref, dng_ref, dstate):
        @pl.when(pl.program_id(0) == 0)
        def _():
            dstate[...] = jnp.zeros_like(dstate)
            dl_ref[...] = jnp.zeros_like(dl_ref)
            dng_ref[...] = jnp.zeros_like(dng_ref)

        dp_ref[:, :, C_ZU:C_HQ] = dzuv_ref[...]
        dp_ref[:, :, C_XQ:C_GL] = dxq_ref[...]
        for n in range(3):
            dp_ref[:, :, C_GL + n * D_MODEL:C_GL + (n + 1) * D_MODEL] = dgl_ref[n]
        dq_ref, df_ref, di_ref, dg_ref = [dp_ref.at[:, :, c0:c0 + 512] for c0 in (C_HQ, C_HF, C_HI, C_HG)]
        for b in range(Bl):
            for h in range(N_HEAD):
                sl = slice(h * HEAD, (h + 1) * HEAD)
                _, vjp = jax.vjp(_hgrn_chunk, st_ref[b, h, 0], *[r[b, :, sl].astype(f32) for r in (q_ref, f_ref, i_ref, g_ref)],
                                 l_ref[0:1, sl], l_ref[1:2, sl], ng_ref[...])
                dst0, dq, df, di, dg, dl0, dl1, dng = vjp((dstate[b, h], db_ref[b, :, sl].astype(f32)))
                dstate[b, h] = dst0
                dq_ref[b, :, sl] = dq.astype(bf16)
                df_ref[b, :, sl] = df.astype(bf16)
                di_ref[b, :, sl] = di.astype(bf16)
                dg_ref[b, :, sl] = dg.astype(bf16)
                dl_ref[0:1, sl] += dl0
                dl_ref[1:2, sl] += dl1
                dng_ref[b, h] += dng

    return _call(body, "hgrn_bwd", (N,),
                 [col(C_HQ), col(C_HF), col(C_HI), col(C_HG), full(2, 512), full(1, HEAD), st, col(0), rows(C_HQ - C_ZU),
                  rows(C_GL - C_XQ), pl.BlockSpec((3, Bl, HG_CHUNK, D_MODEL), lambda n: (0, 0, N - 1 - n, 0))],
                 (rows(IN_WIDTH), full(2, 512), full(Bl, N_HEAD, 1, HEAD)),
                 (jax.ShapeDtypeStruct((Bl, S, IN_WIDTH), bf16), jax.ShapeDtypeStruct((2, 512), f32),
                  jax.ShapeDtypeStruct((Bl, N_HEAD, 1, HEAD), f32)),
                 scratch=[pltpu.VMEM((Bl, N_HEAD, HEAD, HEAD), f32)], sem=("arbitrary",))(
        proj, proj, proj, proj, lb_logits, ng, states, db, dzuv, dxq, dgl)


def _attn_probs(q, k):
    s = _raw_dot(q, k, "nt") * (HEAD ** -0.5)
    e = jnp.exp(s - jnp.max(s, axis=-1, keepdims=True))
    return e / jnp.sum(e, axis=-1, keepdims=True)


def _attn_specs(S, tq):
    nq = S // tq
    q = pl.BlockSpec((tq, 512), lambda b, i: (b * nq + i, C_XQ // 512))
    kv = pl.BlockSpec((1, MEM_LEN, 1024), lambda b, i: (b, 0, 0))
    o = pl.BlockSpec((tq, 512), lambda b, i: (b * nq + i, 0))
    return nq, q, kv, o


def _attn_fwd(proj, kv, Bl, S):
    tq = _pick(S, (512, 256, 128))
    nq, qs, kvs, os_ = _attn_specs(S, tq)

    def body(q_ref, kv_ref, o_ref):
        for h in range(N_HEAD):
            sl = slice(h * HEAD, (h + 1) * HEAD)
            p = _attn_probs(q_ref[:, sl], kv_ref[0, :, sl])
            o_ref[:, sl] = _raw_dot(p, kv_ref[0, :, 512 + h * HEAD:512 + (h + 1) * HEAD], "nn").astype(bf16)

    return _call(body, "attn_fwd", (Bl, nq), [qs, kvs], os_, jax.ShapeDtypeStruct((Bl * S, 512), bf16),
                 sem=("parallel", "parallel"))(proj, kv)


def _attn_bwd(proj, kv, dc, Bl, S):
    tq = _pick(S, (512, 256, 128))
    nq, qs, kvs, os_ = _attn_specs(S, tq)

    def body(q_ref, kv_ref, do_ref, dq_ref, dkv_ref):
        @pl.when(pl.program_id(1) == 0)
        def _():
            dkv_ref[...] = jnp.zeros_like(dkv_ref)

        for h in range(N_HEAD):
            sl = slice(h * HEAD, (h + 1) * HEAD)
            vsl = slice(512 + h * HEAD, 512 + (h + 1) * HEAD)
            q, k, v, do = q_ref[:, sl], kv_ref[0, :, sl], kv_ref[0, :, vsl], do_ref[:, sl]
            p = _attn_probs(q, k)
            dkv_ref[0, :, vsl] += _raw_dot(p, do, "tn")
            dp = _raw_dot(do, v, "nt")
            ds = p * (dp - jnp.sum(dp * p, axis=-1, keepdims=True)) * (HEAD ** -0.5)
            dq_ref[:, sl] = _raw_dot(ds, k, "nn").astype(bf16)
            dkv_ref[0, :, sl] += _raw_dot(ds, q, "tn")

    return _call(body, "attn_bwd", (Bl, nq), [qs, kvs, os_], (os_, kvs),
                 (jax.ShapeDtypeStruct((Bl * S, 512), bf16), jax.ShapeDtypeStruct((Bl, MEM_LEN, 1024), f32)),
                 sem=("arbitrary", "arbitrary"))(proj, kv, dc)


def _gate_specs(tm):
    half = D_MODEL // 2
    return [pl.BlockSpec((tm, half), functools.partial(lambda i, c: (i, c), c=(C_GL + n * D_MODEL) // half + k))
            for n in range(3) for k in range(2)]


def _merge_out_norm_fwd(branches, wb, proj, w_out, x, g):
    T = proj.shape[0]
    tm = _pick(T, (512, 256, 128))

    def body(a_ref, b_ref, c_ref, w0, w1, w2, g0a, g0b, g1a, g1b, g2a, g2b, wo_ref, x_ref, g_ref, m_ref, x1_ref, h_ref):
        acc = jnp.zeros((tm, D_MODEL), f32)
        for x_n, w_ref, ga, gb in ((a_ref, w0, g0a, g0b), (b_ref, w1, g1a, g1b), (c_ref, w2, g2a, g2b)):
            gate = jax.nn.sigmoid(jnp.concatenate([ga[...], gb[...]], axis=-1).astype(f32))
            acc = acc + gate * _raw_dot(x_n[...], w_ref[...], "nn")
        merged = acc.astype(bf16)
        m_ref[...] = merged
        x1 = x_ref[...] + _raw_dot(merged, wo_ref[...], "nn")
        x1_ref[...] = x1
        y = x1 * lax.rsqrt(jnp.mean(x1 * x1, axis=-1, keepdims=True) + EPS) * g_ref[...]
        h_ref[...] = y.astype(bf16)

    br = pl.BlockSpec((tm, 512), lambda i: (i, 0))
    w = pl.BlockSpec((512, D_MODEL), lambda i: (0, 0))
    row = pl.BlockSpec((tm, D_MODEL), lambda i: (i, 0))
    return _call(body, "merge_out_norm_fwd", (T // tm,),
                 [br, br, br, w, w, w, *_gate_specs(tm), pl.BlockSpec((D_MODEL, D_MODEL), lambda i: (0, 0)), row,
                  pl.BlockSpec((1, D_MODEL), lambda i: (0, 0))],
                 (row, row, row),
                 (jax.ShapeDtypeStruct((T, D_MODEL), bf16), jax.ShapeDtypeStruct((T, D_MODEL), f32),
                  jax.ShapeDtypeStruct((T, D_MODEL), bf16)),
                 sem=("parallel",))(*branches, *wb, *[proj] * 6, w_out, x, g)


def _merge_bwd(branches, wb, proj, merged, dx1, w_out):
    T = proj.shape[0]
    tm = _pick(T, (256, 128))

    def body(a_ref, b_ref, c_ref, w0, w1, w2, g0a, g0b, g1a, g1b, g2a, g2b, m_ref, dx_ref, wo_ref, dgl_ref, d0, d1, d2, gw_ref, gwo_ref):
        @pl.when(pl.program_id(0) == 0)
        def _():
            gw_ref[...] = jnp.zeros_like(gw_ref)
            gwo_ref[...] = jnp.zeros_like(gwo_ref)

        dx = dx_ref[...].astype(bf16)
        gwo_ref[...] += _raw_dot(m_ref[...], dx, "tn")
        dm = _raw_dot(dx, wo_ref[...], "nt")
        for n, (x_ref, w_ref, ga, gb, d_ref) in enumerate(((a_ref, w0, g0a, g0b, d0), (b_ref, w1, g1a, g1b, d1), (c_ref, w2, g2a, g2b, d2))):
            x, w = x_ref[...], w_ref[...]
            up = _raw_dot(x, w, "nn")
            sg = jax.nn.sigmoid(jnp.concatenate([ga[...], gb[...]], axis=-1).astype(f32))
            dgl_ref[n] = (dm * up * sg * (1.0 - sg)).astype(bf16)
            dup = (dm * sg).astype(bf16)
            d_ref[...] = _raw_dot(dup, w, "nt").astype(bf16)
            gw_ref[n] += _raw_dot(x, dup, "tn")

    br = pl.BlockSpec((tm, 512), lambda i: (i, 0))
    w = pl.BlockSpec((512, D_MODEL), lambda i: (0, 0))
    sh = jax.ShapeDtypeStruct((T, 512), bf16)
    row = pl.BlockSpec((tm, D_MODEL), lambda i: (i, 0))
    square = pl.BlockSpec((D_MODEL, D_MODEL), lambda i: (0, 0))
    outs = _call(body, "merge_bwd", (T // tm,), [br, br, br, w, w, w, *_gate_specs(tm), row, row, square],
                 (pl.BlockSpec((3, tm, D_MODEL), lambda i: (0, i, 0)), br, br, br, pl.BlockSpec((3, 512, D_MODEL), lambda i: (0, 0, 0)), square),
                 (jax.ShapeDtypeStruct((3, T, D_MODEL), bf16), sh, sh, sh, jax.ShapeDtypeStruct((3, 512, D_MODEL), f32),
                  jax.ShapeDtypeStruct((D_MODEL, D_MODEL), f32)),
                 sem=("arbitrary",))(*branches, *wb, *[proj] * 6, merged, dx1, w_out)
    return outs[0], outs[1:4], outs[4], outs[5]


CONV_TC = 256


def _shift_down(a, k):
    r = pltpu.roll(a, k, 0)
    row = lax.broadcasted_iota(jnp.int32, (8, a.shape[1]), 0)
    return jnp.concatenate([jnp.where(row >= k, r[:8], 0.0), r[8:]], axis=0)


CONV_ROWS = 512


def _shift_up(a, k):
    n = a.shape[0]
    r = pltpu.roll(a, n - k, 0)
    row = lax.broadcasted_iota(jnp.int32, (8, a.shape[1]), 0)
    return jnp.concatenate([r[:n - 8], jnp.where(row < 8 - k, r[n - 8:], 0.0)], axis=0)


def _conv_pre(a, a1, a2, cw, cb):
    return cb + cw[0:1] * a2 + cw[1:2] * a1 + cw[2:3] * a


def _up_conv_fwd(h2, w_up_t, cw, cb):
    Bl, S, Dd = h2.shape
    nc = D_FF // CONV_TC

    rows = min(CONV_ROWS, S)

    def body(h_ref, wa_ref, wb_ref, cw_ref, cb_ref, a_ref, b_ref, o_ref):
        for lo in range(0, S, rows):
            halo = 0 if lo == 0 else 16
            a16 = _raw_dot(h_ref[0, lo - halo:lo + rows], wa_ref[...], "nt").astype(bf16)
            b16 = _raw_dot(h_ref[0, lo:lo + rows], wb_ref[...], "nt").astype(bf16)
            a = a16.astype(f32)
            ac = _conv_pre(a, _shift_down(a, 1), _shift_down(a, 2), cw_ref[...], cb_ref[...])[halo:]
            a_ref[0, lo:lo + rows], b_ref[0, lo:lo + rows] = a16[halo:], b16
            o_ref[0, lo:lo + rows] = (jax.nn.silu(ac) * b16.astype(f32)).astype(bf16)

    seq = pl.BlockSpec((1, S, CONV_TC), lambda b, c: (b, 0, c))
    sh = jax.ShapeDtypeStruct((Bl, S, D_FF), bf16)
    return _call(body, "up_conv_fwd", (Bl, nc),
                 [pl.BlockSpec((1, S, Dd), lambda b, c: (b, 0, 0)), pl.BlockSpec((CONV_TC, Dd), lambda b, c: (c, 0)),
                  pl.BlockSpec((CONV_TC, Dd), lambda b, c: (nc + c, 0)), pl.BlockSpec((3, CONV_TC), lambda b, c: (0, c)),
                  pl.BlockSpec((1, CONV_TC), lambda b, c: (0, c))],
                 (seq, seq, seq), (sh, sh, sh), sem=("parallel", "parallel"))(h2, w_up_t, w_up_t, cw, cb)


def _down_conv_bwd(dx2, w_down, a, b, cw, cb):
    Bl, S, Dd = dx2.shape
    nc = D_FF // CONV_TC

    def body(dx_ref, wd_ref, a_ref, b_ref, cw_ref, cb_ref, da_ref, db_ref, dcw_ref, dcb_ref):
        dact = _raw_dot(dx_ref[0], wd_ref[...], "nt")
        a, cw = a_ref[0].astype(f32), cw_ref[...]
        a1, a2 = _shift_down(a, 1), _shift_down(a, 2)
        ac = _conv_pre(a, a1, a2, cw, cb_ref[...])
        sg = jax.nn.sigmoid(ac)
        gated = dact * sg
        db_ref[0] = (gated * ac).astype(bf16)
        dac = gated * b_ref[0].astype(f32) * (1.0 + ac * (1.0 - sg))
        da_ref[0] = (cw[2:3] * dac + cw[1:2] * _shift_up(dac, 1) + cw[0:1] * _shift_up(dac, 2)).astype(bf16)
        dcw_ref[0, 0:1, :] = jnp.sum(dac * a2, axis=0, keepdims=True)
        dcw_ref[0, 1:2, :] = jnp.sum(dac * a1, axis=0, keepdims=True)
        dcw_ref[0, 2:3, :] = jnp.sum(dac * a, axis=0, keepdims=True)
        dcb_ref[0] = jnp.sum(dac, axis=0, keepdims=True)

    seq = pl.BlockSpec((1, S, CONV_TC), lambda b_, c: (b_, 0, c))
    sh = jax.ShapeDtypeStruct((Bl, S, D_FF), bf16)
    return _call(body, "down_conv_bwd", (Bl, nc),
                 [pl.BlockSpec((1, S, Dd), lambda b_, c: (b_, 0, 0)), pl.BlockSpec((CONV_TC, Dd), lambda b_, c: (c, 0)), seq, seq,
                  pl.BlockSpec((3, CONV_TC), lambda b_, c: (0, c)), pl.BlockSpec((1, CONV_TC), lambda b_, c: (0, c))],
                 (seq, seq, pl.BlockSpec((1, 3, CONV_TC), lambda b_, c: (b_, 0, c)), pl.BlockSpec((1, 1, CONV_TC), lambda b_, c: (b_, 0, c))),
                 (sh, sh, jax.ShapeDtypeStruct((Bl, 3, D_FF), f32), jax.ShapeDtypeStruct((Bl, 1, D_FF), f32)),
                 sem=("parallel", "parallel"))(dx2, w_down, a, b, cw, cb)


def _local_step(x, mem, target, p, w_in_t, late_b, late_c, send):
    Bl, S, Dd = x.shape
    T = Bl * S
    x2d, t2d, mem2d = x.reshape(T, Dd), target.reshape(T, Dd), mem.reshape(Bl * MEM_LEN, Dd)
    b_st = jnp.pad(p["b_spatial"].T, ((0, 0), (0, 128 - N_HEAD)))
    lbl = p["lb_logits"]

    proj, h = _norm_proj_fwd(x2d, p["norm1_g"], w_in_t, 1024, 1664)
    a_out = _gmlp_fwd(proj, p["ln_v_g"], p["ln_v_b"], p["w_spatial"], b_st)
    proj3 = proj.reshape(Bl, S, IN_WIDTH)
    b_out, states = _hgrn_fwd(proj3, lbl, p["hgrn_norm_g"], Bl, S)
    b_out = b_out.reshape(T, 512)
    memn = _rms_fwd(mem2d, p["mem_norm_g"], "memnorm_fwd")
    w = late_b(b_out)
    wb = w["w_branch"]
    kv = _mm(memn, w["w_mem_kv"], "nn", f32, "kv_fwd", 512, 1024).reshape(Bl, MEM_LEN, 2 * 512)
    c_out = _attn_fwd(proj, kv, Bl, S)
    branches = (a_out, b_out, c_out)
    merged, x1, h2 = _merge_out_norm_fwd(branches, wb, proj, w["w_out"], x2d, p["norm2_g"])
    w.update(late_c(h2))
    ffn_a, ffn_b, act = _up_conv_fwd(h2.reshape(Bl, S, Dd), w["w_up_t"], w["conv_w"], p["conv_b"])
    act = act.reshape(T, D_FF)
    loss_part, dx2, dx2_16, g_final = _down_final_loss(act, w["w_down"], x1, p["final_g"], t2d)

    g_w_down = _mm(act, dx2_16, "tn", bf16, "down_dw", 1408, 1024, 1024)
    da, db, g_conv_w, g_conv_b = _down_conv_bwd(dx2_16.reshape(Bl, S, Dd), w["w_down"], ffn_a, ffn_b, w["conv_w"], p["conv_b"])
    da, db = da.reshape(T, D_FF), db.reshape(T, D_FF)
    g_w_up_t = _mm(da, h2, "tn", bf16, "up_dw_a", 1408, 1024, 1024, into=(lax.empty((2 * D_FF, D_MODEL), bf16), 0))
    g_w_up_t = _mm(db, h2, "tn", bf16, "up_dw_b", 1408, 1024, 1024, into=(g_w_up_t, D_FF))
    send("c", dict(w_up=g_w_up_t, conv_w=jnp.sum(g_conv_w, axis=0), w_down=g_w_down))
    dx1, g_norm2 = _mm_rms_bwd([(da, w["w_up_t"], 0), (db, w["w_up_t"], 1)], x1, p["norm2_g"], dx2, "up_dx_norm2_bwd", 512)

    dgl, dbr, g_w_branch, g_w_out = _merge_bwd(branches, wb, proj, merged, dx1, w["w_out"])
    dxq, dkv = _attn_bwd(proj, kv, dbr[2], Bl, S)
    dkv = dkv.reshape(Bl * MEM_LEN, 2 * 512)
    g_w_kv = _mm(memn, dkv, "tn", bf16, "kv_dw", 1024, 1024, 512)
    send("b", dict(w_mem_kv=g_w_kv, w_branch=g_w_branch, w_out=g_w_out))
    dzuv, g_ln_g, g_ln_b, g_w_sp, g_b_sp = _gmlp_bwd(proj, p["ln_v_g"], p["ln_v_b"], p["w_spatial"], b_st, dbr[0])
    dproj, g_lbl, g_ng = _hgrn_bwd(proj3, lbl, p["hgrn_norm_g"], states, dbr[1].reshape(Bl, S, 512), dzuv.reshape(Bl, S, -1),
                                   dxq.reshape(Bl, S, -1), dgl.reshape(3, Bl, S, Dd), Bl, S)
    dproj = dproj.reshape(T, IN_WIDTH)
    half = Dd // 2
    send("a0", dict(w_in_half=_mm(dproj, h, "tn", bf16, "proj_dw_0", 512, half, n_tiles=(0, 1))))
    g_half = _mm(dproj, h, "tn", bf16, "proj_dw_1", 512, half, n_tiles=(1, 1))
    dkv, g_half = lax.optimization_barrier((dkv, g_half))
    send("a1", dict(w_in_half=g_half))
    dmemn = _mm(dkv, w["w_mem_kv"], "nt", f32, "kv_dx", 512, 1024)
    _, g_mem_norm = _rms_bwd(mem2d, p["mem_norm_g"], dmemn, "memnorm_bwd")
    dx, g_norm1 = _mm_rms_bwd([(dproj, w_in_t, 0)], x2d, p["norm1_g"], dx1, "proj_dx_norm1_bwd", 512)

    gs = dict(w_spatial=g_w_sp, norm1_g=g_norm1, mem_norm_g=g_mem_norm, norm2_g=g_norm2, final_g=g_final, lb_logits=g_lbl,
              ln_v_g=g_ln_g, ln_v_b=g_ln_b, b_spatial=g_b_sp, hgrn_norm_g=g_ng, conv_b=g_conv_b)
    return loss_part, dx.reshape(Bl, S, Dd), gs


def _coords():
    return lax.axis_index("x"), lax.axis_index("y"), lax.axis_index("c")


def _slot(dev):
    return 4 * dev[0] + 2 * dev[1] + dev[2]


def _comm_call(body, name, arrays, out_shapes, n_sem):
    n = len(arrays)
    hbm = pl.BlockSpec(memory_space=pl.ANY)
    return pl.pallas_call(
        body, name=name, out_shape=out_shapes, in_specs=[hbm] * n, out_specs=[hbm] * n,
        scratch_shapes=[pltpu.SemaphoreType.DMA((n_sem, n)), pltpu.SemaphoreType.DMA((n_sem, n)), pltpu.SemaphoreType.DMA((n,))])(*arrays)


def _all_gather(blocks, name):
    n = len(blocks)

    def body(*refs):
        x_refs, o_refs, (send_sems, recv_sems, local_sems) = refs[:n], refs[n:2 * n], refs[2 * n:]
        x, y, c = _coords()
        me, sibling = (x, y, c), (x, y, 1 - c)
        chips = [(1 - x, y), (x, 1 - y), (1 - x, 1 - y)]

        def copy(a, k, block_dev, to, from_input=False):
            dst = o_refs[a].at[_slot(block_dev)]
            return pltpu.make_async_remote_copy(src_ref=x_refs[a] if from_input else dst, dst_ref=dst, send_sem=send_sems.at[k, a],
                                                recv_sem=recv_sems.at[k, a], device_id=to, device_id_type=MESH)

        mine = [pltpu.make_async_copy(x_refs[a], o_refs[a].at[_slot(me)], local_sems.at[a]) for a in range(n)]
        first = [copy(a, 0, me, sibling, True) for a in range(n)]
        first += [copy(a, 1 + j, me, (*chip, c), True) for j, chip in enumerate(chips) for a in range(n)]
        for cp in mine + first:
            cp.start()
        passed = []
        for j, chip in enumerate(chips):
            for a in range(n):
                copy(a, 1 + j, (*chip, c), me).wait_recv()
                fwd = copy(a, 4 + j, (*chip, c), sibling)
                fwd.start()
                passed.append(fwd)
        for a in range(n):
            copy(a, 0, sibling, me).wait_recv()
        for j, chip in enumerate(chips):
            for a in range(n):
                copy(a, 4 + j, (*chip, 1 - c), me).wait_recv()
        for cp in first + passed:
            cp.wait_send()
        for cp in mine:
            cp.wait()

    return _comm_call(body, name, blocks, [jax.ShapeDtypeStruct((N_DEV,) + b.shape, b.dtype) for b in blocks], 7)


_REL = [(0, 0, 1), (0, 1, 0), (0, 1, 1), (1, 0, 0), (1, 0, 1), (1, 1, 0), (1, 1, 1)]


def _seq_exchange(arrays, gather, name, collective_id):
    n = len(arrays)
    hbm = pltpu.MemorySpace.HBM
    srcs = [jax.new_ref(a, memory_space=hbm) for a in arrays]
    lands = [jax.empty_ref(jax.ShapeDtypeStruct(((N_DEV,) + a.shape) if gather else a.shape, a.dtype), memory_space=hbm) for a in arrays]

    @pl.kernel(mesh=plsc.ScalarSubcoreMesh(axis_name="sequencer", num_cores=1), name=name,
               scratch_types=(pltpu.SemaphoreType.DMA((7, n)), pltpu.SemaphoreType.DMA((7, n)), pltpu.SemaphoreType.DMA((n,))),
               compiler_params=pltpu.CompilerParams(collective_id=collective_id))
    def launch(send, recv, local):
        x, y, c = _coords()
        me = (x, y, c)
        peers = [(x ^ dx, y ^ dy, c ^ dc) for dx, dy, dc in _REL]
        barrier = pltpu.get_barrier_semaphore()
        for peer in peers:
            pl.semaphore_signal(barrier, inc=1, device_id=peer, device_id_type=MESH)
        pl.semaphore_wait(barrier, len(peers))

        def copy(a, k, peer, arrival):
            return pltpu.make_async_remote_copy(
                src_ref=srcs[a] if gather else srcs[a].at[_slot(peer)], dst_ref=lands[a].at[_slot(peer if arrival else me)],
                send_sem=send.at[k, a], recv_sem=recv.at[k, a], device_id=peer, device_id_type=MESH)

        mine = [pltpu.make_async_copy(srcs[a] if gather else srcs[a].at[_slot(me)], lands[a].at[_slot(me)], local.at[a])
                for a in range(n)]
        out = [copy(a, k, peer, False) for a in range(n) for k, peer in enumerate(peers)]
        for cp in mine + out:
            cp.start()
        for a in range(n):
            for k, peer in enumerate(peers):
                copy(a, k, peer, True).wait_recv()
        for cp in out:
            cp.wait_send()
        for cp in mine:
            cp.wait()

    launch()
    return [land[...] for land in lands]


def _adam_math(w, g, m, v):
    m_ = ADAM_B1 * m + (1.0 - ADAM_B1) * g
    v_ = ADAM_B2 * v + (1.0 - ADAM_B2) * jnp.square(g)
    m_hat = m_ / (1.0 - ADAM_B1 ** ADAM_STEP)
    v_hat = v_ / (1.0 - ADAM_B2 ** ADAM_STEP)
    return -ADAM_LR * (m_hat / (jnp.sqrt(v_hat) + ADAM_EPS) + ADAM_WD * w), m_, v_


def _reduce_adamw(parts, w, m, v, name):
    R, L = w.shape
    tr = _pick(R, (256, 208, 176, 128, 64, 32, 16, 8))
    n = len(parts)

    def body(*refs):
        w_ref, m_ref, v_ref, g_ref, d_ref, nm_ref, nv_ref = refs[n:]
        pieces = []
        for p_ref in refs[:n]:
            g = p_ref[0].astype(f32)
            for i in range(1, N_DEV):
                g = g + p_ref[i].astype(f32)
            pieces.append(g)
        g = pieces[0] if n == 1 else jnp.concatenate(pieces, axis=-1)
        g_ref[...] = g
        d_ref[...], nm_ref[...], nv_ref[...] = _adam_math(w_ref[...], g, m_ref[...], v_ref[...])

    blk = pl.BlockSpec((tr, L), lambda i: (i, 0))
    sh = jax.ShapeDtypeStruct((R, L), f32)
    return _call(body, name, (R // tr,), [pl.BlockSpec((N_DEV, tr, q.shape[2]), lambda i: (0, i, 0)) for q in parts] + [blk, blk, blk],
                 (blk,) * 4, (sh,) * 4, sem=("parallel",))(*parts, w, m, v)


SMALL = (("w_spatial", (512, 128), 0), ("norm1_g", (1, 1024), 512), ("mem_norm_g", (1, 1024), 520), ("norm2_g", (1, 1024), 528),
         ("final_g", (1, 1024), 536), ("lb_logits", (2, 512), 544), ("ln_v_g", (1, 512), 552), ("ln_v_b", (1, 512), 556),
         ("b_spatial", (4, 128), 560), ("hgrn_norm_g", (1, 128), 564), ("conv_b", (1, 2816), 565))
LOSS_ROW, SMALL_USED, SMALL_ROWS = 587, 588, 640


def _segments(shape, base):
    r, n = shape
    per = n // 128
    return [(base + i * per + j, i, slice(j * 128, (j + 1) * 128)) for i in range(r) for j in range(per)]


def _pack_small(gs, loss_part):
    names = [n for n, _, _ in SMALL]

    def body(*refs):
        src, loss_ref, o_ref = dict(zip(names, refs[:-2])), refs[-2], refs[-1]
        o_ref[SMALL_USED:SMALL_ROWS, :] = jnp.zeros((SMALL_ROWS - SMALL_USED, 128), f32)
        o_ref[LOSS_ROW:LOSS_ROW + 1, :] = loss_ref[...]
        for name, shape, base in SMALL:
            ref = src[name]
            if name == "w_spatial":
                o_ref[base:base + 512, :] = ref[...].reshape(512, 128)
            elif name == "b_spatial":
                o_ref[base:base + 4, :] = ref[0:4, :]
            elif name == "conv_b":
                per_example = functools.reduce(lambda u, v_: u + v_, [ref[b] for b in range(ref.shape[0])])
                for row, i, sl in _segments(shape, base):
                    o_ref[row:row + 1, :] = per_example[i:i + 1, sl]
            elif name == "hgrn_norm_g":
                per_head = [ref[b, h] for b in range(ref.shape[0]) for h in range(N_HEAD)]
                o_ref[base:base + 1, :] = functools.reduce(lambda u, v_: u + v_, per_head)
            else:
                for row, i, sl in _segments(shape, base):
                    o_ref[row:row + 1, :] = ref[i:i + 1, sl]

    return pl.pallas_call(body, name="pack_small", out_shape=jax.ShapeDtypeStruct((SMALL_ROWS, 128), f32))(
        *[gs[n] for n in names], loss_part)


def _small_update(gathered, w, m, v):
    names = [n for n, _, _ in SMALL]
    k = len(names)

    def body(*refs):
        p_ref = refs[0]
        ins = [dict(zip(names, refs[1 + i * k:1 + (i + 1) * k])) for i in range(3)]
        outs = [dict(zip(names, refs[1 + (3 + i) * k:1 + (4 + i) * k])) for i in range(4)]
        loss_ref, gsum = refs[-2], refs[-1]
        g = p_ref[0]
        for i in range(1, N_DEV):
            g = g + p_ref[i]
        gsum[...] = g
        loss_ref[...] = gsum[LOSS_ROW:LOSS_ROW + 1, :]
        for name, shape, base in SMALL:
            if name == "w_spatial":
                where = [(slice(base, base + 512), (slice(None), slice(None)))]
            else:
                where = [(slice(row, row + 1), (slice(i, i + 1), sl)) for row, i, sl in _segments(shape, base)]
            for rows, at in where:
                g_ = gsum[rows, :]
                d_, m_, v_ = _adam_math(ins[0][name][at], g_, ins[1][name][at], ins[2][name][at])
                for o, val in zip(outs, (g_, d_, m_, v_)):
                    o[name][at] = val

    args = [gathered] + [d[n] for d in (w, m, v) for n in names]
    out_shapes = [jax.ShapeDtypeStruct(shape, f32) for _ in range(4) for _, shape, _ in SMALL] + [jax.ShapeDtypeStruct((1, 128), f32)]
    outs = pl.pallas_call(body, name="small_update", out_shape=out_shapes, scratch_shapes=[pltpu.VMEM((SMALL_ROWS, 128), f32)])(*args)
    return [dict(zip(names, outs[i * k:(i + 1) * k])) for i in range(4)], outs[-1]


def _cols_full(g):
    return jnp.moveaxis(g, 0, -2).reshape(g.shape[1:-1] + (N_DEV * g.shape[-1],))


def _cols_parts(full):
    n = full.shape[-1] // N_DEV
    return jnp.moveaxis(full.reshape(full.shape[:-1] + (N_DEV, n)), -2, 0)


def kernel(x, mem, norm1_g, w_in, ln_v_g, ln_v_b, w_spatial, b_spatial, lb_logits, hgrn_norm_g, mem_norm_g, w_mem_kv, w_branch, w_out, norm2_g, w_up, conv_w, conv_b, w_down, final_g, loss_target, m_norm1_g, m_w_in, m_ln_v_g, m_ln_v_b, m_w_spatial, m_b_spatial, m_lb_logits, m_hgrn_norm_g, m_mem_norm_g, m_w_mem_kv, m_w_branch, m_w_out, m_norm2_g, m_w_up, m_conv_w, m_conv_b, m_w_down, m_final_g, v_norm1_g, v_w_in, v_ln_v_g, v_ln_v_b, v_w_spatial, v_b_spatial, v_lb_logits, v_hgrn_norm_g, v_mem_norm_g, v_w_mem_kv, v_w_branch, v_w_out, v_norm2_g, v_w_up, v_conv_w, v_conv_b, v_w_down, v_final_g):
    given = dict(locals())
    order = ("norm1_g", "w_in", "ln_v_g", "ln_v_b", "w_spatial", "b_spatial", "lb_logits", "hgrn_norm_g", "mem_norm_g",
             "w_mem_kv", "w_branch", "w_out", "norm2_g", "w_up", "conv_w", "conv_b", "w_down", "final_g")
    groups = dict(a=("w_in",), b=("w_mem_kv", "w_branch", "w_out"), c=("w_up", "conv_w", "w_down"))

    by_rows = ("w_in", "w_up")
    shard_of = lambda n, prefix="": jnp.swapaxes(given[prefix + n][0], 0, 1) if n in by_rows else given[prefix + n][0]
    wire = {n: shard_of(n).astype(f32 if n == "conv_w" else bf16) for ns in groups.values() for n in ns}
    w_in_full = _all_gather([wire["w_in"]], "gather_w_in")[0].reshape(IN_WIDTH, D_MODEL)
    w_in_full, wire_b, wire_c = lax.optimization_barrier((w_in_full, [wire[n] for n in groups["b"]], [wire[n] for n in groups["c"]]))
    rest_b = _seq_exchange(wire_b, True, "gather_b", 1)
    rest_c = _seq_exchange(wire_c, True, "gather_c", 6)

    def late_b(after):
        _, (kv_, br_, out_) = lax.optimization_barrier((after, tuple(rest_b)))
        br_ = _cols_full(br_)
        return dict(w_mem_kv=kv_.reshape(D_MODEL, 2 * 512), w_branch=[br_[n] for n in range(3)], w_out=out_.reshape(D_MODEL, D_MODEL))

    def late_c(after):
        _, (up_, cw_, down_) = lax.optimization_barrier((after, tuple(rest_c)))
        return dict(w_up_t=up_.reshape(2 * D_FF, D_MODEL), conv_w=_cols_full(cw_), w_down=down_.reshape(D_FF, D_MODEL))

    to_parts = dict(w_in_half=lambda g_: g_.reshape(N_DEV, -1, D_MODEL // 2), w_up=lambda g_: g_.reshape(N_DEV, -1, D_MODEL), conv_w=_cols_parts,
                    w_branch=lambda g_: _cols_parts(g_.astype(bf16)).reshape(N_DEV, -1, 128),
                    w_mem_kv=lambda g_: g_.reshape(N_DEV, -1, 2 * 512), w_out=lambda g_: g_.astype(bf16).reshape(N_DEV, -1, D_MODEL),
                    w_down=lambda g_: g_.reshape(N_DEV, -1, D_MODEL))
    scatters = {}


    def send(tag, grads_):
        parts = [to_parts[n](g_) for n, g_ in grads_.items()]
        scatters[tag] = _seq_exchange(parts, False, f"scatter_{tag}", dict(a0=2, a1=7, b=4, c=5)[tag])

    small_2d = lambda prefix: {n: given[prefix + n].reshape(shape) for n, shape, _ in SMALL}
    p = small_2d("")
    p["w_spatial"] = w_spatial[0]
    updates = {}

    def update(tag):
        arrived = [scatters["a0"] + scatters["a1"]] if tag == "a" else [[parts] for parts in scatters[tag]]
        for n, parts in zip(groups[tag], arrived):
            state = [shard_of(n, pre) for pre in ("", "m_", "v_")]
            res = _reduce_adamw(parts, *[a.reshape(-1, a.shape[-1]) for a in state], "adamw_" + n)
            updates[n] = [jnp.swapaxes(r, 0, 1) for r in res] if n in by_rows else res

    loss_part, grad_x, gs = _local_step(x, mem, loss_target, p, w_in_full, late_b, late_c, send)

    packed, (scatters["c"], scatters["b"]) = lax.optimization_barrier((_pack_small(gs, loss_part), (scatters["c"], scatters["b"])))
    gathered = _seq_exchange([packed], True, "gather_small", 3)[0]

    update("c")
    update("b")
    update("a")
    grads, delta, new_m, new_v = {}, {}, {}, {}
    for n, res in updates.items():
        grads[n], delta[n], new_m[n], new_v[n] = [r.reshape(given[n].shape) for r in res]

    small_results, loss_row = _small_update(gathered, small_2d(""), small_2d("m_"), small_2d("v_"))
    for dst, res in zip((grads, delta, new_m, new_v), small_results):
        for n, _, _ in SMALL:
            dst[n] = res[n].reshape(given[n].shape)
    loss = loss_row[0, 0]

    return (loss, grad_x, *[grads[n] for n in order], *[delta[n] for n in order], *[new_m[n] for n in order],
            *[new_v[n] for n in order])
```
